```python
import math
import jax, jax.numpy as jnp
from jax import lax
import numpy as np

D_MODEL = 1024
BATCH = 8
SEQ = 4096
DEPTH = 2

SSD_EXPAND = 2
SSD_D_INNER = SSD_EXPAND * D_MODEL
SSD_HEAD_DIM = 64
SSD_N_HEADS = SSD_D_INNER // SSD_HEAD_DIM
SSD_N_GROUPS = 8
SSD_HEADS_PER_GROUP = SSD_N_HEADS // SSD_N_GROUPS
SSD_D_STATE = 128
SSD_CONV_WIDTH = 4
SSD_CHUNK = 128
SSD_CONV_DIM = SSD_D_INNER + 2 * SSD_N_GROUPS * SSD_D_STATE
SSD_IN_DIM = SSD_D_INNER + SSD_CONV_DIM + SSD_N_HEADS

ATTN_HEAD_DIM = 64
ATTN_N_Q_HEADS = D_MODEL // ATTN_HEAD_DIM
ATTN_N_KV_HEADS = 4
ATTN_REP = ATTN_N_Q_HEADS // ATTN_N_KV_HEADS
ATTN_WINDOW = 128
ATTN_QKV_DIM = (ATTN_N_Q_HEADS + 2 * ATTN_N_KV_HEADS) * ATTN_HEAD_DIM

D_FF = 4 * D_MODEL

N_MIXERS = 2
N_SSD_LAYERS = (DEPTH + 1) // 2
N_ATTN_LAYERS = DEPTH // 2
NORM_EPS = 1e-6

kernel_name = "hybrid_ssd_swa_sink_sqrelu_trunk"


def rms_norm(x, w):
    xf = x.astype(jnp.float32)
    y = xf * lax.rsqrt(jnp.mean(xf * xf, axis=-1, keepdims=True) + NORM_EPS)
    return (y * w.astype(jnp.float32)).astype(x.dtype)


def causal_depthwise_conv(x, w, b):
    c = x.shape[-1]
    y = lax.conv_general_dilated(
        x, w.astype(x.dtype)[:, None, :], window_strides=(1,),
        padding=[(SSD_CONV_WIDTH - 1, 0)],
        dimension_numbers=("NWC", "WIO", "NWC"), feature_group_count=c)
    return y + b.astype(x.dtype)


def ssd_chunked_scan(xs, dt, a, bmat, cmat):
    b, l, g, j, p = xs.shape
    n = bmat.shape[-1]
    nc = l // SSD_CHUNK

    def to_chunks(t):
        t = t.astype(jnp.float32).reshape((b, nc, SSD_CHUNK) + t.shape[2:])
        return jnp.moveaxis(t, 1, 0)

    xc, dtc, bc, cc = to_chunks(xs), to_chunks(dt), to_chunks(bmat), to_chunks(cmat)
    ac = dtc * a.astype(jnp.float32)
    causal = jnp.tril(jnp.ones((SSD_CHUNK, SSD_CHUNK), dtype=bool))[None, :, :, None, None]

    def step(state, inp):
        x_q, dt_q, a_q, b_q, c_q = inp
        cum = jnp.cumsum(a_q, axis=1)
        diff = cum[:, :, None] - cum[:, None, :]
        decay = jnp.exp(jnp.where(causal, diff, -jnp.inf))
        cb = jnp.einsum("btgn,bsgn->btsg", c_q, b_q)
        y_intra = jnp.einsum("btsg,btsgj,bsgj,bsgjp->btgjp", cb, decay, dt_q, x_q)
        y_inter = jnp.einsum("btgn,bgjpn->btgjp", c_q, state) * jnp.exp(cum)[..., None]
        decay_to_end = jnp.exp(cum[:, -1:] - cum)
        new_state = state * jnp.exp(cum[:, -1])[..., None, None] + jnp.einsum(
            "bsgn,bsgj,bsgjp->bgjpn", b_q, dt_q * decay_to_end, x_q)
        return new_state, y_intra + y_inter

    state0 = jnp.zeros((b, g, j, p, n), jnp.float32)
    _, ys = lax.scan(step, state0, (xc, dtc, ac, bc, cc))
    return jnp.moveaxis(ys, 0, 1).reshape(b, l, g, j, p)


def ssd_mixer(u, w_in, conv_w, conv_b, dt_bias, a_log, d_skip, norm_w, w_out):
    b, l, _ = u.shape
    zxbcdt = u @ w_in
    z = zxbcdt[..., :SSD_D_INNER]
    xbc = zxbcdt[..., SSD_D_INNER:SSD_D_INNER + SSD_CONV_DIM]
    dt_raw = zxbcdt[..., SSD_D_INNER + SSD_CONV_DIM:]
    xbc = jax.nn.silu(causal_depthwise_conv(xbc, conv_w, conv_b))
    gn = SSD_N_GROUPS * SSD_D_STATE
    xs = xbc[..., :SSD_D_INNER].reshape(b, l, SSD_N_GROUPS, SSD_HEADS_PER_GROUP, SSD_HEAD_DIM)
    bmat = xbc[..., SSD_D_INNER:SSD_D_INNER + gn].reshape(b, l, SSD_N_GROUPS, SSD_D_STATE)
    cmat = xbc[..., SSD_D_INNER + gn:].reshape(b, l, SSD_N_GROUPS, SSD_D_STATE)
    dt = jax.nn.softplus(dt_raw.astype(jnp.float32) + dt_bias.astype(jnp.float32))
    dt = dt.reshape(b, l, SSD_N_GROUPS, SSD_HEADS_PER_GROUP)
    a = -jnp.exp(a_log.astype(jnp.float32)).reshape(SSD_N_GROUPS, SSD_HEADS_PER_GROUP)
    y = ssd_chunked_scan(xs, dt, a, bmat, cmat)
    y = y + d_skip.astype(jnp.float32).reshape(SSD_N_GROUPS, SSD_HEADS_PER_GROUP, 1) * xs.astype(jnp.float32)
    y = y.reshape(b, l, SSD_D_INNER) * jax.nn.silu(z.astype(jnp.float32))
    y = y.reshape(b, l, SSD_N_GROUPS, SSD_D_INNER // SSD_N_GROUPS)
    y = y * lax.rsqrt(jnp.mean(y * y, axis=-1, keepdims=True) + NORM_EPS)
    y = (y.reshape(b, l, SSD_D_INNER) * norm_w.astype(jnp.float32)).astype(u.dtype)
    return y @ w_out


def swa_sink_attention(u, w_qkv, b_qkv, sinks, w_o, b_o):
    b, l, _ = u.shape
    nb = l // ATTN_WINDOW
    qkv = u @ w_qkv + b_qkv
    qd = ATTN_N_Q_HEADS * ATTN_HEAD_DIM
    kd = ATTN_N_KV_HEADS * ATTN_HEAD_DIM
    q = qkv[..., :qd].reshape(b, nb, ATTN_WINDOW, ATTN_N_KV_HEADS, ATTN_REP, ATTN_HEAD_DIM)
    k = qkv[..., qd:qd + kd].reshape(b, nb, ATTN_WINDOW, ATTN_N_KV_HEADS, ATTN_HEAD_DIM)
    v = qkv[..., qd + kd:].reshape(b, nb, ATTN_WINDOW, ATTN_N_KV_HEADS, ATTN_HEAD_DIM)

    def with_prev_block(t):
        prev = jnp.concatenate([jnp.zeros_like(t[:, :1]), t[:, :-1]], axis=1)
        return jnp.concatenate([prev, t], axis=2)

    kb, vb = with_prev_block(k), with_prev_block(v)
    scores = jnp.einsum("bnqkrd,bnskd->bnkrqs", q, kb).astype(jnp.float32) * (ATTN_HEAD_DIM ** -0.5)
    qpos = jnp.arange(ATTN_WINDOW) + ATTN_WINDOW
    kpos = jnp.arange(2 * ATTN_WINDOW)
    rel = qpos[:, None] - kpos[None, :]
    band = (rel >= 0) & (rel < ATTN_WINDOW)
    blk = jnp.arange(nb)[:, None, None]
    valid = band[None] & ~((blk == 0) & (kpos[None, None, :] < ATTN_WINDOW))
    scores = jnp.where(valid[None, :, None, None], scores, -jnp.inf)
    sink = sinks.astype(jnp.float32).reshape(1, 1, ATTN_N_KV_HEADS, ATTN_REP, 1, 1)
    m = jnp.maximum(jnp.max(scores, axis=-1, keepdims=True), sink)
    e = jnp.exp(scores - m)
    probs = e / (jnp.sum(e, axis=-1, keepdims=True) + jnp.exp(sink - m))
    out = jnp.einsum("bnkrqs,bnskd->bnqkrd", probs.astype(vb.dtype), vb).reshape(b, l, qd)
    return out @ w_o + b_o


def sqrelu_mlp(u, w_up, w_down):
    return jnp.square(jax.nn.relu(u @ w_up)) @ w_down


def _fwd_setup_inputs(seed: int = 0) -> dict:
    key = jax.random.key(seed)
    ks = jax.random.split(key, 24)
    f32 = jnp.float32

    def normal(k, shape, scale):
        return jax.random.normal(k, shape, f32) * scale

    def gain(k, shape):
        return 1.0 + 0.05 * jax.random.normal(k, shape, f32)

    x = jax.random.normal(ks[0], (BATCH, SEQ, D_MODEL), f32)
    ssd_w_in = normal(ks[1], (N_SSD_LAYERS, D_MODEL, SSD_IN_DIM), D_MODEL ** -0.5)
    ssd_conv_w = normal(ks[2], (N_SSD_LAYERS, SSD_CONV_WIDTH, SSD_CONV_DIM), SSD_CONV_WIDTH ** -0.5)
    ssd_conv_b = normal(ks[3], (N_SSD_LAYERS, SSD_CONV_DIM), 0.02)
    dt0 = jnp.exp(jax.random.uniform(ks[4], (N_SSD_LAYERS, SSD_N_HEADS), f32,
                                     math.log(1e-3), math.log(1e-1)))
    ssd_dt_bias = dt0 + jnp.log(-jnp.expm1(-dt0))
    ssd_a_log = jnp.log(jax.random.uniform(ks[5], (N_SSD_LAYERS, SSD_N_HEADS), f32, 1.0, 16.0))
    ssd_d = gain(ks[6], (N_SSD_LAYERS, SSD_N_HEADS))
    ssd_norm_w = gain(ks[7], (N_SSD_LAYERS, SSD_D_INNER))
    ssd_w_out = normal(ks[8], (N_SSD_LAYERS, SSD_D_INNER, D_MODEL), SSD_D_INNER ** -0.5)
    attn_w_qkv = normal(ks[9], (N_ATTN_LAYERS, D_MODEL, ATTN_QKV_DIM), D_MODEL ** -0.5)
    attn_b_qkv = normal(ks[10], (N_ATTN_LAYERS, ATTN_QKV_DIM), 0.02)
    attn_sinks = normal(ks[11], (N_ATTN_LAYERS, ATTN_N_Q_HEADS), 1.0)
    attn_w_o = normal(ks[12], (N_ATTN_LAYERS, ATTN_N_Q_HEADS * ATTN_HEAD_DIM, D_MODEL),
                      (ATTN_N_Q_HEADS * ATTN_HEAD_DIM) ** -0.5)
    attn_b_o = normal(ks[13], (N_ATTN_LAYERS, D_MODEL), 0.02)
    mlp_w_up = normal(ks[14], (DEPTH, D_MODEL, D_FF), D_MODEL ** -0.5)
    mlp_w_down = normal(ks[15], (DEPTH, D_FF, D_MODEL), D_FF ** -0.5)
    mix_pre_norm = gain(ks[16], (DEPTH, D_MODEL))
    mix_post_norm = gain(ks[17], (DEPTH, D_MODEL))
    ffn_pre_norm = gain(ks[18], (DEPTH, D_MODEL))
    ffn_post_norm = gain(ks[19], (DEPTH, D_MODEL))
    return {
        "x": x,
        "ssd_w_in": ssd_w_in, "ssd_conv_w": ssd_conv_w, "ssd_conv_b": ssd_conv_b,
        "ssd_dt_bias": ssd_dt_bias, "ssd_a_log": ssd_a_log, "ssd_d": ssd_d,
        "ssd_norm_w": ssd_norm_w, "ssd_w_out": ssd_w_out,
        "attn_w_qkv": attn_w_qkv, "attn_b_qkv": attn_b_qkv, "attn_sinks": attn_sinks,
        "attn_w_o": attn_w_o, "attn_b_o": attn_b_o,
        "mlp_w_up": mlp_w_up, "mlp_w_down": mlp_w_down,
        "mix_pre_norm": mix_pre_norm, "mix_post_norm": mix_post_norm,
        "ffn_pre_norm": ffn_pre_norm, "ffn_post_norm": ffn_post_norm,
    }


def _fwd_reference(x, ssd_w_in, ssd_conv_w, ssd_conv_b, ssd_dt_bias, ssd_a_log, ssd_d,
              ssd_norm_w, ssd_w_out, attn_w_qkv, attn_b_qkv, attn_sinks, attn_w_o,
              attn_b_o, mlp_w_up, mlp_w_down, mix_pre_norm, mix_post_norm,
              ffn_pre_norm, ffn_post_norm):
    h = x
    for i in range(DEPTH):
        u = rms_norm(h, mix_pre_norm[i])
        j = i // N_MIXERS
        if i % N_MIXERS == 0:
            mix = ssd_mixer(u, ssd_w_in[j], ssd_conv_w[j], ssd_conv_b[j], ssd_dt_bias[j],
                            ssd_a_log[j], ssd_d[j], ssd_norm_w[j], ssd_w_out[j])
        else:
            mix = swa_sink_attention(u, attn_w_qkv[j], attn_b_qkv[j], attn_sinks[j],
                                     attn_w_o[j], attn_b_o[j])
        h = h + rms_norm(mix, mix_post_norm[i])
        f = sqrelu_mlp(rms_norm(h, ffn_pre_norm[i]), mlp_w_up[i], mlp_w_down[i])
        h = h + rms_norm(f, ffn_post_norm[i])
    return h


import jax as _jax
import jax.numpy as _jnp

TWIN_FORMAT = 'train_step'
FWD_PARAMS = ['x', 'ssd_w_in', 'ssd_conv_w', 'ssd_conv_b', 'ssd_dt_bias', 'ssd_a_log', 'ssd_d', 'ssd_norm_w', 'ssd_w_out', 'attn_w_qkv', 'attn_b_qkv', 'attn_sinks', 'attn_w_o', 'attn_b_o', 'mlp_w_up', 'mlp_w_down', 'mix_pre_norm', 'mix_post_norm', 'ffn_pre_norm', 'ffn_post_norm']
TWIN_WEIGHTS = ['ssd_w_in', 'ssd_conv_w', 'ssd_conv_b', 'ssd_dt_bias', 'ssd_a_log', 'ssd_d', 'ssd_norm_w', 'ssd_w_out', 'attn_w_qkv', 'attn_b_qkv', 'attn_sinks', 'attn_w_o', 'attn_b_o', 'mlp_w_up', 'mlp_w_down', 'mix_pre_norm', 'mix_post_norm', 'ffn_pre_norm', 'ffn_post_norm']
TWIN_DIFF_INPUT = 'x'
TWIN_INPUTS = ['x', 'ssd_w_in', 'ssd_conv_w', 'ssd_conv_b', 'ssd_dt_bias', 'ssd_a_log', 'ssd_d', 'ssd_norm_w', 'ssd_w_out', 'attn_w_qkv', 'attn_b_qkv', 'attn_sinks', 'attn_w_o', 'attn_b_o', 'mlp_w_up', 'mlp_w_down', 'mix_pre_norm', 'mix_post_norm', 'ffn_pre_norm', 'ffn_post_norm', 'loss_target', 'm_ssd_w_in', 'm_ssd_conv_w', 'm_ssd_conv_b', 'm_ssd_dt_bias', 'm_ssd_a_log', 'm_ssd_d', 'm_ssd_norm_w', 'm_ssd_w_out', 'm_attn_w_qkv', 'm_attn_b_qkv', 'm_attn_sinks', 'm_attn_w_o', 'm_attn_b_o', 'm_mlp_w_up', 'm_mlp_w_down', 'm_mix_pre_norm', 'm_mix_post_norm', 'm_ffn_pre_norm', 'm_ffn_post_norm', 'v_ssd_w_in', 'v_ssd_conv_w', 'v_ssd_conv_b', 'v_ssd_dt_bias', 'v_ssd_a_log', 'v_ssd_d', 'v_ssd_norm_w', 'v_ssd_w_out', 'v_attn_w_qkv', 'v_attn_b_qkv', 'v_attn_sinks', 'v_attn_w_o', 'v_attn_b_o', 'v_mlp_w_up', 'v_mlp_w_down', 'v_mix_pre_norm', 'v_mix_post_norm', 'v_ffn_pre_norm', 'v_ffn_post_norm']
TWIN_OUTPUTS = ['loss', 'grad_x', 'grad_ssd_w_in', 'grad_ssd_conv_w', 'grad_ssd_conv_b', 'grad_ssd_dt_bias', 'grad_ssd_a_log', 'grad_ssd_d', 'grad_ssd_norm_w', 'grad_ssd_w_out', 'grad_attn_w_qkv', 'grad_attn_b_qkv', 'grad_attn_sinks', 'grad_attn_w_o', 'grad_attn_b_o', 'grad_mlp_w_up', 'grad_mlp_w_down', 'grad_mix_pre_norm', 'grad_mix_post_norm', 'grad_ffn_pre_norm', 'grad_ffn_post_norm', 'delta_ssd_w_in', 'delta_ssd_conv_w', 'delta_ssd_conv_b', 'delta_ssd_dt_bias', 'delta_ssd_a_log', 'delta_ssd_d', 'delta_ssd_norm_w', 'delta_ssd_w_out', 'delta_attn_w_qkv', 'delta_attn_b_qkv', 'delta_attn_sinks', 'delta_attn_w_o', 'delta_attn_b_o', 'delta_mlp_w_up', 'delta_mlp_w_down', 'delta_mix_pre_norm', 'delta_mix_post_norm', 'delta_ffn_pre_norm', 'delta_ffn_post_norm', 'new_m_ssd_w_in', 'new_m_ssd_conv_w', 'new_m_ssd_conv_b', 'new_m_ssd_dt_bias', 'new_m_ssd_a_log', 'new_m_ssd_d', 'new_m_ssd_norm_w', 'new_m_ssd_w_out', 'new_m_attn_w_qkv', 'new_m_attn_b_qkv', 'new_m_attn_sinks', 'new_m_attn_w_o', 'new_m_attn_b_o', 'new_m_mlp_w_up', 'new_m_mlp_w_down', 'new_m_mix_pre_norm', 'new_m_mix_post_norm', 'new_m_ffn_pre_norm', 'new_m_ffn_post_norm', 'new_v_ssd_w_in', 'new_v_ssd_conv_w', 'new_v_ssd_conv_b', 'new_v_ssd_dt_bias', 'new_v_ssd_a_log', 'new_v_ssd_d', 'new_v_ssd_norm_w', 'new_v_ssd_w_out', 'new_v_attn_w_qkv', 'new_v_attn_b_qkv', 'new_v_attn_sinks', 'new_v_attn_w_o', 'new_v_attn_b_o', 'new_v_mlp_w_up', 'new_v_mlp_w_down', 'new_v_mix_pre_norm', 'new_v_mix_post_norm', 'new_v_ffn_pre_norm', 'new_v_ffn_post_norm']
TWIN_LEAF_KINDS = {'loss': 'loss', 'grad_x': 'grad_x', 'grad_ssd_w_in': 'grad_w', 'grad_ssd_conv_w': 'grad_w', 'grad_ssd_conv_b': 'grad_w', 'grad_ssd_dt_bias': 'grad_w', 'grad_ssd_a_log': 'grad_w', 'grad_ssd_d': 'grad_w', 'grad_ssd_norm_w': 'grad_w', 'grad_ssd_w_out': 'grad_w', 'grad_attn_w_qkv': 'grad_w', 'grad_attn_b_qkv': 'grad_w', 'grad_attn_sinks': 'grad_w', 'grad_attn_w_o': 'grad_w', 'grad_attn_b_o': 'grad_w', 'grad_mlp_w_up': 'grad_w', 'grad_mlp_w_down': 'grad_w', 'grad_mix_pre_norm': 'grad_w', 'grad_mix_post_norm': 'grad_w', 'grad_ffn_pre_norm': 'grad_w', 'grad_ffn_post_norm': 'grad_w', 'delta_ssd_w_in': 'delta_w', 'delta_ssd_conv_w': 'delta_w', 'delta_ssd_conv_b': 'delta_w', 'delta_ssd_dt_bias': 'delta_w', 'delta_ssd_a_log': 'delta_w', 'delta_ssd_d': 'delta_w', 'delta_ssd_norm_w': 'delta_w', 'delta_ssd_w_out': 'delta_w', 'delta_attn_w_qkv': 'delta_w', 'delta_attn_b_qkv': 'delta_w', 'delta_attn_sinks': 'delta_w', 'delta_attn_w_o': 'delta_w', 'delta_attn_b_o': 'delta_w', 'delta_mlp_w_up': 'delta_w', 'delta_mlp_w_down': 'delta_w', 'delta_mix_pre_norm': 'delta_w', 'delta_mix_post_norm': 'delta_w', 'delta_ffn_pre_norm': 'delta_w', 'delta_ffn_post_norm': 'delta_w', 'new_m_ssd_w_in': 'new_m', 'new_m_ssd_conv_w': 'new_m', 'new_m_ssd_conv_b': 'new_m', 'new_m_ssd_dt_bias': 'new_m', 'new_m_ssd_a_log': 'new_m', 'new_m_ssd_d': 'new_m', 'new_m_ssd_norm_w': 'new_m', 'new_m_ssd_w_out': 'new_m', 'new_m_attn_w_qkv': 'new_m', 'new_m_attn_b_qkv': 'new_m', 'new_m_attn_sinks': 'new_m', 'new_m_attn_w_o': 'new_m', 'new_m_attn_b_o': 'new_m', 'new_m_mlp_w_up': 'new_m', 'new_m_mlp_w_down': 'new_m', 'new_m_mix_pre_norm': 'new_m', 'new_m_mix_post_norm': 'new_m', 'new_m_ffn_pre_norm': 'new_m', 'new_m_ffn_post_norm': 'new_m', 'new_v_ssd_w_in': 'new_v', 'new_v_ssd_conv_w': 'new_v', 'new_v_ssd_conv_b': 'new_v', 'new_v_ssd_dt_bias': 'new_v', 'new_v_ssd_a_log': 'new_v', 'new_v_ssd_d': 'new_v', 'new_v_ssd_norm_w': 'new_v', 'new_v_ssd_w_out': 'new_v', 'new_v_attn_w_qkv': 'new_v', 'new_v_attn_b_qkv': 'new_v', 'new_v_attn_sinks': 'new_v', 'new_v_attn_w_o': 'new_v', 'new_v_attn_b_o': 'new_v', 'new_v_mlp_w_up': 'new_v', 'new_v_mlp_w_down': 'new_v', 'new_v_mix_pre_norm': 'new_v', 'new_v_mix_post_norm': 'new_v', 'new_v_ffn_pre_norm': 'new_v', 'new_v_ffn_post_norm': 'new_v'}


def _forward(args):
    return _fwd_reference(*[args[k] for k in FWD_PARAMS])


def _output_shape():
    out = _jax.eval_shape(lambda: _forward(_fwd_setup_inputs(0)))
    return out.shape, out.dtype

N_MICROBATCH = 1
ADAM_LR = 0.001
ADAM_B1 = 0.9
ADAM_B2 = 0.999
ADAM_EPS = 1e-08
ADAM_WD = 0.01
ADAM_STEP = 10
PER_EXAMPLE_BATCH_AXIS = {'x': 0, 'loss_target': 0}
SHARED_INPUTS = []
_WEIGHT_DTYPES = {'ssd_w_in': _jnp.float32, 'ssd_conv_w': _jnp.float32, 'ssd_conv_b': _jnp.float32, 'ssd_dt_bias': _jnp.float32, 'ssd_a_log': _jnp.float32, 'ssd_d': _jnp.float32, 'ssd_norm_w': _jnp.float32, 'ssd_w_out': _jnp.float32, 'attn_w_qkv': _jnp.float32, 'attn_b_qkv': _jnp.float32, 'attn_sinks': _jnp.float32, 'attn_w_o': _jnp.float32, 'attn_b_o': _jnp.float32, 'mlp_w_up': _jnp.float32, 'mlp_w_down': _jnp.float32, 'mix_pre_norm': _jnp.float32, 'mix_post_norm': _jnp.float32, 'ffn_pre_norm': _jnp.float32, 'ffn_post_norm': _jnp.float32}
MOMENT_SCALE = {'ssd_w_in': 5.589422e-01, 'ssd_conv_w': 4.344514e+00, 'ssd_conv_b': 1.303929e+01, 'ssd_dt_bias': 4.173499e+00, 'ssd_a_log': 1.757547e+01, 'ssd_d': 2.552687e+01, 'ssd_norm_w': 8.524325e+00, 'ssd_w_out': 1.250919e+01, 'attn_w_qkv': 1.836303e+01, 'attn_b_qkv': 6.945690e+01, 'attn_sinks': 1.047178e+00, 'attn_w_o': 2.057222e+01, 'attn_b_o': 8.205258e+01, 'mlp_w_up': 4.532228e+00, 'mlp_w_down': 2.077624e+01, 'mix_pre_norm': 1.639361e+01, 'mix_post_norm': 4.178952e+01, 'ffn_pre_norm': 8.843305e+00, 'ffn_post_norm': 3.843289e+01}


def _to_microbatches(a, axis):
    t = _jnp.moveaxis(a, axis, 0)
    t = t.reshape((N_MICROBATCH, t.shape[0] // N_MICROBATCH) + t.shape[1:])
    return _jnp.moveaxis(t, 1, axis + 1)


def setup_inputs(seed: int = 0) -> dict:
    inp = _fwd_setup_inputs(seed)
    key = _jax.random.fold_in(_jax.random.key(seed), 7919)
    shape, _ = _output_shape()
    out = dict(inp)
    out["loss_target"] = _jax.random.normal(_jax.random.fold_in(key, 0), shape, _jnp.float32)
    for i, name in enumerate(TWIN_WEIGHTS):
        w = inp[name].astype(_jnp.float32)
        if MOMENT_SCALE is None:
            s = _jnp.sqrt(_jnp.mean(_jnp.square(w)) + 1e-30)
        else:
            s = MOMENT_SCALE[name]
        km, kv = _jax.random.split(_jax.random.fold_in(key, i + 1))
        out[name] = w
        out["m_" + name] = s * _jax.random.normal(km, w.shape, _jnp.float32)
        out["v_" + name] = (s * s) * _jax.random.uniform(kv, w.shape, _jnp.float32, 0.5, 1.5)
    if N_MICROBATCH > 1:
        for name, axis in PER_EXAMPLE_BATCH_AXIS.items():
            out[name] = _to_microbatches(out[name], axis)
    return {'x': out['x'], 'ssd_w_in': out['ssd_w_in'], 'ssd_conv_w': out['ssd_conv_w'], 'ssd_conv_b': out['ssd_conv_b'], 'ssd_dt_bias': out['ssd_dt_bias'], 'ssd_a_log': out['ssd_a_log'], 'ssd_d': out['ssd_d'], 'ssd_norm_w': out['ssd_norm_w'], 'ssd_w_out': out['ssd_w_out'], 'attn_w_qkv': out['attn_w_qkv'], 'attn_b_qkv': out['attn_b_qkv'], 'attn_sinks': out['attn_sinks'], 'attn_w_o': out['attn_w_o'], 'attn_b_o': out['attn_b_o'], 'mlp_w_up': out['mlp_w_up'], 'mlp_w_down': out['mlp_w_down'], 'mix_pre_norm': out['mix_pre_norm'], 'mix_post_norm': out['mix_post_norm'], 'ffn_pre_norm': out['ffn_pre_norm'], 'ffn_post_norm': out['ffn_post_norm'], 'loss_target': out['loss_target'], 'm_ssd_w_in': out['m_ssd_w_in'], 'm_ssd_conv_w': out['m_ssd_conv_w'], 'm_ssd_conv_b': out['m_ssd_conv_b'], 'm_ssd_dt_bias': out['m_ssd_dt_bias'], 'm_ssd_a_log': out['m_ssd_a_log'], 'm_ssd_d': out['m_ssd_d'], 'm_ssd_norm_w': out['m_ssd_norm_w'], 'm_ssd_w_out': out['m_ssd_w_out'], 'm_attn_w_qkv': out['m_attn_w_qkv'], 'm_attn_b_qkv': out['m_attn_b_qkv'], 'm_attn_sinks': out['m_attn_sinks'], 'm_attn_w_o': out['m_attn_w_o'], 'm_attn_b_o': out['m_attn_b_o'], 'm_mlp_w_up': out['m_mlp_w_up'], 'm_mlp_w_down': out['m_mlp_w_down'], 'm_mix_pre_norm': out['m_mix_pre_norm'], 'm_mix_post_norm': out['m_mix_post_norm'], 'm_ffn_pre_norm': out['m_ffn_pre_norm'], 'm_ffn_post_norm': out['m_ffn_post_norm'], 'v_ssd_w_in': out['v_ssd_w_in'], 'v_ssd_conv_w': out['v_ssd_conv_w'], 'v_ssd_conv_b': out['v_ssd_conv_b'], 'v_ssd_dt_bias': out['v_ssd_dt_bias'], 'v_ssd_a_log': out['v_ssd_a_log'], 'v_ssd_d': out['v_ssd_d'], 'v_ssd_norm_w': out['v_ssd_norm_w'], 'v_ssd_w_out': out['v_ssd_w_out'], 'v_attn_w_qkv': out['v_attn_w_qkv'], 'v_attn_b_qkv': out['v_attn_b_qkv'], 'v_attn_sinks': out['v_attn_sinks'], 'v_attn_w_o': out['v_attn_w_o'], 'v_attn_b_o': out['v_attn_b_o'], 'v_mlp_w_up': out['v_mlp_w_up'], 'v_mlp_w_down': out['v_mlp_w_down'], 'v_mix_pre_norm': out['v_mix_pre_norm'], 'v_mix_post_norm': out['v_mix_post_norm'], 'v_ffn_pre_norm': out['v_ffn_pre_norm'], 'v_ffn_post_norm': out['v_ffn_post_norm']}


def _loss(weights, diff, rest, loss_target):
    with _jax.named_scope("forward"):
        args = {**rest, TWIN_DIFF_INPUT: diff, **{k: w.astype(_WEIGHT_DTYPES[k]) for k, w in weights.items()}}
        y = _forward(args)
    with _jax.named_scope("loss_head"):
        err = _jnp.square(y.astype(_jnp.float32) - loss_target)
        return 0.5 * _jnp.sum(_jnp.mean(err, axis=-1)) if err.ndim else 0.5 * err


def _adamw(w, g, m, v):
    m = ADAM_B1 * m + (1.0 - ADAM_B1) * g
    v = ADAM_B2 * v + (1.0 - ADAM_B2) * _jnp.square(g)
    m_hat = m / (1.0 - ADAM_B1 ** ADAM_STEP)
    v_hat = v / (1.0 - ADAM_B2 ** ADAM_STEP)
    delta = -ADAM_LR * (m_hat / (_jnp.sqrt(v_hat) + ADAM_EPS) + ADAM_WD * w)
    return delta, m, v


def reference(x, ssd_w_in, ssd_conv_w, ssd_conv_b, ssd_dt_bias, ssd_a_log, ssd_d, ssd_norm_w, ssd_w_out, attn_w_qkv, attn_b_qkv, attn_sinks, attn_w_o, attn_b_o, mlp_w_up, mlp_w_down, mix_pre_norm, mix_post_norm, ffn_pre_norm, ffn_post_norm, loss_target, m_ssd_w_in, m_ssd_conv_w, m_ssd_conv_b, m_ssd_dt_bias, m_ssd_a_log, m_ssd_d, m_ssd_norm_w, m_ssd_w_out, m_attn_w_qkv, m_attn_b_qkv, m_attn_sinks, m_attn_w_o, m_attn_b_o, m_mlp_w_up, m_mlp_w_down, m_mix_pre_norm, m_mix_post_norm, m_ffn_pre_norm, m_ffn_post_norm, v_ssd_w_in, v_ssd_conv_w, v_ssd_conv_b, v_ssd_dt_bias, v_ssd_a_log, v_ssd_d, v_ssd_norm_w, v_ssd_w_out, v_attn_w_qkv, v_attn_b_qkv, v_attn_sinks, v_attn_w_o, v_attn_b_o, v_mlp_w_up, v_mlp_w_down, v_mix_pre_norm, v_mix_post_norm, v_ffn_pre_norm, v_ffn_post_norm):
    given = dict(x=x, ssd_w_in=ssd_w_in, ssd_conv_w=ssd_conv_w, ssd_conv_b=ssd_conv_b, ssd_dt_bias=ssd_dt_bias, ssd_a_log=ssd_a_log, ssd_d=ssd_d, ssd_norm_w=ssd_norm_w, ssd_w_out=ssd_w_out, attn_w_qkv=attn_w_qkv, attn_b_qkv=attn_b_qkv, attn_sinks=attn_sinks, attn_w_o=attn_w_o, attn_b_o=attn_b_o, mlp_w_up=mlp_w_up, mlp_w_down=mlp_w_down, mix_pre_norm=mix_pre_norm, mix_post_norm=mix_post_norm, ffn_pre_norm=ffn_pre_norm, ffn_post_norm=ffn_post_norm, loss_target=loss_target, m_ssd_w_in=m_ssd_w_in, m_ssd_conv_w=m_ssd_conv_w, m_ssd_conv_b=m_ssd_conv_b, m_ssd_dt_bias=m_ssd_dt_bias, m_ssd_a_log=m_ssd_a_log, m_ssd_d=m_ssd_d, m_ssd_norm_w=m_ssd_norm_w, m_ssd_w_out=m_ssd_w_out, m_attn_w_qkv=m_attn_w_qkv, m_attn_b_qkv=m_attn_b_qkv, m_attn_sinks=m_attn_sinks, m_attn_w_o=m_attn_w_o, m_attn_b_o=m_attn_b_o, m_mlp_w_up=m_mlp_w_up, m_mlp_w_down=m_mlp_w_down, m_mix_pre_norm=m_mix_pre_norm, m_mix_post_norm=m_mix_post_norm, m_ffn_pre_norm=m_ffn_pre_norm, m_ffn_post_norm=m_ffn_post_norm, v_ssd_w_in=v_ssd_w_in, v_ssd_conv_w=v_ssd_conv_w, v_ssd_conv_b=v_ssd_conv_b, v_ssd_dt_bias=v_ssd_dt_bias, v_ssd_a_log=v_ssd_a_log, v_ssd_d=v_ssd_d, v_ssd_norm_w=v_ssd_norm_w, v_ssd_w_out=v_ssd_w_out, v_attn_w_qkv=v_attn_w_qkv, v_attn_b_qkv=v_attn_b_qkv, v_attn_sinks=v_attn_sinks, v_attn_w_o=v_attn_w_o, v_attn_b_o=v_attn_b_o, v_mlp_w_up=v_mlp_w_up, v_mlp_w_down=v_mlp_w_down, v_mix_pre_norm=v_mix_pre_norm, v_mix_post_norm=v_mix_post_norm, v_ffn_pre_norm=v_ffn_pre_norm, v_ffn_post_norm=v_ffn_post_norm)
    weights = {n: given[n] for n in TWIN_WEIGHTS}
    shared = {n: given[n] for n in SHARED_INPUTS}
    per_example = {n: given[n] for n in ['x']}
    grad_fn = _jax.value_and_grad(_loss, argnums=(0, 1))

    def one_microbatch(ex, loss_target):
        ex = dict(ex)
        diff = ex.pop(TWIN_DIFF_INPUT)
        return grad_fn(weights, diff, {**shared, **ex}, loss_target)

    if N_MICROBATCH == 1:
        loss, (grad_w, grad_x) = one_microbatch(per_example, given["loss_target"])
    else:
        def body(carry, xs):
            loss_sum, grad_sum = carry
            l_k, (gw_k, gx_k) = one_microbatch(xs[0], xs[1])
            with _jax.named_scope("update"):
                return (loss_sum + l_k, _jax.tree.map(_jnp.add, grad_sum, gw_k)), gx_k

        init = (_jnp.zeros((), _jnp.float32), _jax.tree.map(_jnp.zeros_like, weights))
        (loss, grad_w), grad_x = _jax.lax.scan(body, init, (per_example, given["loss_target"]))
    with _jax.named_scope("update"):
        delta_w, new_m, new_v = {}, {}, {}
        for n in TWIN_WEIGHTS:
            delta_w[n], new_m[n], new_v[n] = _adamw(weights[n], grad_w[n], given["m_" + n], given["v_" + n])
    return (loss, grad_x, *[grad_w[n] for n in TWIN_WEIGHTS], *[delta_w[n] for n in TWIN_WEIGHTS],
            *[new_m[n] for n in TWIN_WEIGHTS], *[new_v[n] for n in TWIN_WEIGHTS])
```

```python
import functools
import math

import jax
import jax.numpy as jnp
from jax import lax
from jax.experimental import pallas as pl
from jax.experimental.pallas import tpu as pltpu

F32 = jnp.float32
BF16 = jnp.bfloat16

D_MODEL = 1024
SSD_D_INNER = 2048
SSD_HEAD_DIM = 64
SSD_N_HEADS = 32
SSD_N_GROUPS = 8
SSD_HPG = 4
SSD_D_STATE = 128
SSD_CONV_WIDTH = 4
SSD_CHUNK = 128
SSD_CONV_DIM = 4096
SSD_IN_DIM = 6176
SSD_IN_PAD = 6272
SSD_GW = SSD_HPG * SSD_HEAD_DIM
ATTN_HEAD_DIM = 64
ATTN_N_Q = 16
ATTN_N_KV = 4
ATTN_REP = 4
ATTN_WINDOW = 128
ATTN_QKV = 1536
D_FF = 4096
NORM_EPS = 1e-6

ADAM_LR = 0.001
ADAM_B1 = 0.9
ADAM_B2 = 0.999
ADAM_EPS = 1e-08
ADAM_WD = 0.01
ADAM_STEP = 10

N_CHIPS = 4
N_DEV = 8
LANES = 128
VMEM_LIMIT = 48 * 1024 * 1024

MESH = pl.DeviceIdType.MESH


def _params(*sem):
    return pltpu.CompilerParams(dimension_semantics=sem, vmem_limit_bytes=VMEM_LIMIT)


def _dot(a, b, dims):
    return lax.dot_general(a, b, (dims, ((), ())), preferred_element_type=F32)


def _dot_nn(a, b):
    return _dot(a, b, ((1,), (0,)))


def _dot_nt(a, b):
    return _dot(a, b, ((1,), (1,)))


def _dot_tn(a, b):
    return _dot(a, b, ((0,), (0,)))


def _sigmoid(x):
    return 1.0 / (1.0 + jnp.exp(-x))


def _matmul(a, b, *, mode, out_dtypes, name, epilogue=None, extras=(), tm=1024, tn=1024, tk=1024):
    if mode == "nn":
        (m, k), (k2, n) = a.shape, b.shape
    elif mode == "nt":
        (m, k), (n, k2) = a.shape, b.shape
    else:
        (k, m), (k2, n) = a.shape, b.shape
    assert k == k2, (a.shape, b.shape, mode)
    tm, tn, tk = min(tm, m), min(tn, n), min(tk, k)
    assert m % tm == 0 and n % tn == 0 and k % tk == 0, (m, n, k, tm, tn, tk)
    nk = k // tk
    if mode == "tn":
        a_spec = pl.BlockSpec((tk, tm), lambda i, j, kk: (kk, i))
    else:
        a_spec = pl.BlockSpec((tm, tk), lambda i, j, kk: (i, kk))
    if mode == "nt":
        b_spec = pl.BlockSpec((tn, tk), lambda i, j, kk: (j, kk))
    else:
        b_spec = pl.BlockSpec((tk, tn), lambda i, j, kk: (kk, j))
    dims = {"nn": ((1,), (0,)), "nt": ((1,), (1,)), "tn": ((0,), (0,))}[mode]
    ex_specs = []
    for arr, kind in extras:
        if kind == "tile":
            ex_specs.append(pl.BlockSpec((tm, tn), lambda i, j, kk: (i, j)))
        else:
            ex_specs.append(pl.BlockSpec((1, tn), lambda i, j, kk: (0, j)))
    n_ex, n_out = len(extras), len(out_dtypes)
    if epilogue is None:
        epilogue = lambda acc: (acc,)

    def body(a_ref, b_ref, *rest):
        ex = rest[:n_ex]
        outs = rest[n_ex:n_ex + n_out]

        def finish(acc):
            res = epilogue(acc, *[e[...] for e in ex])
            for o, r in zip(outs, res):
                o[...] = r.astype(o.dtype)

        if nk == 1:
            finish(_dot(a_ref[...], b_ref[...], dims))
        else:
            acc_ref = rest[-1]
            kk = pl.program_id(2)

            @pl.when(kk == 0)
            def _():
                acc_ref[...] = jnp.zeros_like(acc_ref)

            acc_ref[...] += _dot(a_ref[...], b_ref[...], dims)

            @pl.when(kk == nk - 1)
            def _():
                finish(acc_ref[...])

    outs = pl.pallas_call(
        body,
        grid=(m // tm, n // tn, nk),
        in_specs=[a_spec, b_spec] + ex_specs,
        out_specs=[pl.BlockSpec((tm, tn), lambda i, j, kk: (i, j)) for _ in out_dtypes],
        out_shape=[jax.ShapeDtypeStruct((m, n), dt) for dt in out_dtypes],
        scratch_shapes=[] if nk == 1 else [pltpu.VMEM((tm, tn), F32)],
        compiler_params=_params("parallel", "parallel", "arbitrary"),
        name=name,
    )(a, b, *[arr for arr, _ in extras])
    return outs[0] if n_out == 1 else outs


def _row_tile(t, want):
    return min(t, want)


def _rms_fwd(x, w, *, name, resid=None, want_u=None):
    t, d = x.shape
    tr = _row_tile(t, 512)

    def norm(v, wv):
        return v * lax.rsqrt(jnp.mean(v * v, axis=-1, keepdims=True) + NORM_EPS) * wv

    row = pl.BlockSpec((tr, d), lambda i: (i, 0))
    vec = pl.BlockSpec((1, d), lambda i: (0, 0))
    if resid is None:
        def body(x_ref, w_ref, o_ref):
            o_ref[...] = norm(x_ref[...], w_ref[...]).astype(BF16)
        ins, in_specs = (x, w), [row, vec]
        out_shape, out_specs = jax.ShapeDtypeStruct((t, d), BF16), row
    elif want_u is None:
        def body(x_ref, w_ref, r_ref, o_ref):
            o_ref[...] = r_ref[...] + norm(x_ref[...], w_ref[...])
        ins, in_specs = (x, w, resid), [row, vec, row]
        out_shape, out_specs = jax.ShapeDtypeStruct((t, d), F32), row
    else:
        def body(x_ref, w_ref, r_ref, w2_ref, o_ref, u_ref):
            h = r_ref[...] + norm(x_ref[...], w_ref[...])
            o_ref[...] = h
            u_ref[...] = norm(h, w2_ref[...]).astype(BF16)
        ins, in_specs = (x, w, resid, want_u), [row, vec, row, vec]
        out_shape = [jax.ShapeDtypeStruct((t, d), F32), jax.ShapeDtypeStruct((t, d), BF16)]
        out_specs = [row, row]
    return pl.pallas_call(body, grid=(t // tr,), in_specs=in_specs, out_specs=out_specs, out_shape=out_shape,
                          compiler_params=_params("parallel"), name=name)(*ins)


def _rms_bwd(x, w, dy, *, name, resid=None, out_dtype=F32):
    t, d = x.shape
    tr = _row_tile(t, 512)
    row = pl.BlockSpec((tr, d), lambda i: (i, 0))
    vec = pl.BlockSpec((1, d), lambda i: (0, 0))
    has_res = resid is not None

    def body(x_ref, w_ref, dy_ref, *rest):
        if has_res:
            r_ref, dx_ref, dw_ref = rest
        else:
            dx_ref, dw_ref = rest
        xv = x_ref[...]
        dyv = dy_ref[...].astype(F32)
        r = lax.rsqrt(jnp.mean(xv * xv, axis=-1, keepdims=True) + NORM_EPS)
        xhat = xv * r
        dyw = dyv * w_ref[...]
        dx = r * (dyw - xhat * jnp.mean(dyw * xhat, axis=-1, keepdims=True))
        if has_res:
            dx = dx + r_ref[...]
        dx_ref[...] = dx.astype(dx_ref.dtype)

        @pl.when(pl.program_id(0) == 0)
        def _():
            dw_ref[...] = jnp.zeros_like(dw_ref)

        dw_ref[...] += jnp.sum(dyv * xhat, axis=0, keepdims=True)

    ins = (x, w, dy) + ((resid,) if has_res else ())
    in_specs = [row, vec, row] + ([row] if has_res else [])
    return pl.pallas_call(
        body, grid=(t // tr,), in_specs=in_specs, out_specs=[row, vec],
        out_shape=[jax.ShapeDtypeStruct((t, d), out_dtype), jax.ShapeDtypeStruct((1, d), F32)],
        compiler_params=_params("arbitrary"), name=name)(*ins)


def _loss_head(h, target, *, name):
    t, d = h.shape
    tr = _row_tile(t, 512)
    row = pl.BlockSpec((tr, d), lambda i: (i, 0))

    def body(h_ref, t_ref, dh_ref, loss_ref):
        err = h_ref[...] - t_ref[...]
        dh_ref[...] = err * (1.0 / d)

        @pl.when(pl.program_id(0) == 0)
        def _():
            loss_ref[...] = jnp.zeros_like(loss_ref)

        part = jnp.sum(jnp.sum(err * err, axis=1, keepdims=True), axis=0, keepdims=True) * (0.5 / d)
        loss_ref[...] += jnp.broadcast_to(part, loss_ref.shape)

    return pl.pallas_call(
        body, grid=(t // tr,), in_specs=[row, row],
        out_specs=[row, pl.BlockSpec((8, LANES), lambda i: (0, 0))],
        out_shape=[jax.ShapeDtypeStruct((t, d), F32), jax.ShapeDtypeStruct((8, LANES), F32)],
        compiler_params=_params("arbitrary"), name=name)(h, target)


def _col_sum(x, *, name):
    t, n = x.shape
    tr = _row_tile(t, 512)

    def body(x_ref, o_ref):
        @pl.when(pl.program_id(0) == 0)
        def _():
            o_ref[...] = jnp.zeros_like(o_ref)

        o_ref[...] += jnp.sum(x_ref[...].astype(F32), axis=0, keepdims=True)

    return pl.pallas_call(
        body, grid=(t // tr,), in_specs=[pl.BlockSpec((tr, n), lambda i: (i, 0))],
        out_specs=pl.BlockSpec((1, n), lambda i: (0, 0)), out_shape=jax.ShapeDtypeStruct((1, n), F32),
        compiler_params=_params("arbitrary"), name=name)(x)


XBC_COL0 = SSD_D_INNER // LANES
DT_COL0 = (SSD_D_INNER + SSD_CONV_DIM) // LANES


def _shift_down(v, k, row_ids):
    return jnp.where(row_ids >= k, pltpu.roll(v, k, axis=0), 0.0)


def _shift_up(v, k, row_ids):
    n = v.shape[0]
    return jnp.where(row_ids < n - k, pltpu.roll(v, n - k, axis=0), 0.0)


def _conv_pre(x, w, b, row_ids):
    pre = b + w[3:4, :] * x
    for k in (1, 2, 3):
        pre = pre + w[3 - k:4 - k, :] * _shift_down(x, k, row_ids)
    return pre


def _conv_fwd(zx, conv_w, conv_b, *, name):
    t = zx.shape[0]
    nct = SSD_CONV_DIM // LANES

    def body(x_ref, w_ref, b_ref, o_ref):
        x = x_ref[...]
        row_ids = lax.broadcasted_iota(jnp.int32, x.shape, 0)
        pre = _conv_pre(x, w_ref[...], b_ref[...], row_ids)
        o_ref[...] = pre * _sigmoid(pre)

    return pl.pallas_call(
        body, grid=(nct,),
        in_specs=[pl.BlockSpec((t, LANES), lambda j: (0, XBC_COL0 + j)),
                  pl.BlockSpec((SSD_CONV_WIDTH, LANES), lambda j: (0, j)),
                  pl.BlockSpec((1, LANES), lambda j: (0, j))],
        out_specs=pl.BlockSpec((t, LANES), lambda j: (0, j)),
        out_shape=jax.ShapeDtypeStruct((t, SSD_CONV_DIM), F32),
        compiler_params=_params("parallel"), name=name)(zx, conv_w, conv_b)


def _conv_bwd(zx, conv_w, conv_b, dxc, *, name):
    t = zx.shape[0]
    nct = SSD_CONV_DIM // LANES

    def body(x_ref, w_ref, b_ref, dy_ref, dx_ref, dw_ref, db_ref):
        x = x_ref[...]
        w = w_ref[...]
        row_ids = lax.broadcasted_iota(jnp.int32, x.shape, 0)
        pre = _conv_pre(x, w, b_ref[...], row_ids)
        sg = _sigmoid(pre)
        dpre = dy_ref[...] * (sg * (1.0 + pre * (1.0 - sg)))
        dx = w[3:4, :] * dpre
        for k in (1, 2, 3):
            dx = dx + w[3 - k:4 - k, :] * _shift_up(dpre, k, row_ids)
        dx_ref[...] = dx.astype(dx_ref.dtype)
        db_ref[...] = jnp.sum(dpre, axis=0, keepdims=True)
        dw_ref[3:4, :] = jnp.sum(dpre * x, axis=0, keepdims=True)
        for k in (1, 2, 3):
            dw_ref[3 - k:4 - k, :] = jnp.sum(dpre * _shift_down(x, k, row_ids), axis=0, keepdims=True)

    col = pl.BlockSpec((t, LANES), lambda j: (0, j))
    return pl.pallas_call(
        body, grid=(nct,),
        in_specs=[pl.BlockSpec((t, LANES), lambda j: (0, XBC_COL0 + j)),
                  pl.BlockSpec((SSD_CONV_WIDTH, LANES), lambda j: (0, j)),
                  pl.BlockSpec((1, LANES), lambda j: (0, j)), col],
        out_specs=[col, pl.BlockSpec((SSD_CONV_WIDTH, LANES), lambda j: (0, j)), pl.BlockSpec((1, LANES), lambda j: (0, j))],
        out_shape=[jax.ShapeDtypeStruct((t, SSD_CONV_DIM), BF16),
                   jax.ShapeDtypeStruct((SSD_CONV_WIDTH, SSD_CONV_DIM), F32),
                   jax.ShapeDtypeStruct((1, SSD_CONV_DIM), F32)],
        compiler_params=_params("parallel"), name=name)(zx, conv_w, conv_b, dxc)


def _softplus_fwd(zx, bias_row, *, name):
    t = zx.shape[0]
    tr = _row_tile(t, 1024)

    def body(x_ref, b_ref, o_ref):
        v = x_ref[...] + b_ref[...]
        e = jnp.exp(-jnp.abs(v))
        u = 1.0 + e
        log1p = jnp.where(u == 1.0, e, jnp.log(u) * (e / (u - 1.0)))
        o_ref[...] = jnp.maximum(v, 0.0) + log1p

    return pl.pallas_call(
        body, grid=(t // tr,),
        in_specs=[pl.BlockSpec((tr, LANES), lambda i: (i, DT_COL0)), pl.BlockSpec((1, LANES), lambda i: (0, 0))],
        out_specs=pl.BlockSpec((tr, LANES), lambda i: (i, 0)),
        out_shape=jax.ShapeDtypeStruct((t, LANES), F32),
        compiler_params=_params("parallel"), name=name)(zx, bias_row)


def _softplus_bwd(zx, bias_row, ddt, *, name):
    t = zx.shape[0]
    tr = _row_tile(t, 1024)

    def body(x_ref, b_ref, g_ref, o_ref, db_ref):
        v = x_ref[...] + b_ref[...]
        lane = lax.broadcasted_iota(jnp.int32, v.shape, 1)
        d = jnp.where(lane < SSD_N_HEADS, g_ref[...] * _sigmoid(v), 0.0)
        o_ref[...] = d.astype(o_ref.dtype)

        @pl.when(pl.program_id(0) == 0)
        def _():
            db_ref[...] = jnp.zeros_like(db_ref)

        db_ref[...] += jnp.sum(d, axis=0, keepdims=True)

    return pl.pallas_call(
        body, grid=(t // tr,),
        in_specs=[pl.BlockSpec((tr, LANES), lambda i: (i, DT_COL0)), pl.BlockSpec((1, LANES), lambda i: (0, 0)),
                  pl.BlockSpec((tr, LANES), lambda i: (i, 0))],
        out_specs=[pl.BlockSpec((tr, LANES), lambda i: (i, 0)), pl.BlockSpec((1, LANES), lambda i: (0, 0))],
        out_shape=[jax.ShapeDtypeStruct((t, LANES), BF16), jax.ShapeDtypeStruct((1, LANES), F32)],
        compiler_params=_params("arbitrary"), name=name)(zx, bias_row, ddt)


def _ssd_masks():
    q = SSD_CHUNK
    tt = lax.broadcasted_iota(jnp.int32, (q, q), 0)
    ss = lax.broadcasted_iota(jnp.int32, (q, q), 1)
    lane = lax.broadcasted_iota(jnp.int32, (1, SSD_GW), 1)
    srow = lax.broadcasted_iota(jnp.int32, (SSD_GW, 1), 0)
    hm = [(lane >= SSD_HEAD_DIM * j) & (lane < SSD_HEAD_DIM * (j + 1)) for j in range(SSD_HPG)]
    rm = [(srow >= SSD_HEAD_DIM * j) & (srow < SSD_HEAD_DIM * (j + 1)) for j in range(SSD_HPG)]
    return tt, ss, hm, rm


def _ssd_head_terms(dtc, dtr, a_rows, j, tt, ss):
    q = SSD_CHUNK
    dt_col = dtc[:, j:j + 1]
    dt_row = dtr[j:j + 1, :]
    a_row1 = a_rows[j:j + 1, :]
    a_11 = a_rows[j:j + 1, 0:1]
    cum_col = jnp.sum(jnp.where(ss <= tt, dt_row * a_row1, 0.0), axis=1, keepdims=True)
    cum_row = jnp.sum(jnp.where(tt <= ss, dt_col * a_11, 0.0), axis=0, keepdims=True)
    decay = jnp.exp(jnp.where(ss <= tt, cum_col - cum_row, -jnp.inf))
    cum_last = cum_col[q - 1:q, :]
    e_col = jnp.exp(cum_col)
    dte_col = jnp.exp(cum_last - cum_col)
    e_last = jnp.exp(cum_last)
    return dt_col, dt_row, a_row1, a_11, decay, e_col, dte_col, e_last


def _ssd_group_specs(nc):
    xs = pl.BlockSpec((SSD_CHUNK, SSD_GW), lambda g, c: (c, g))
    bm = pl.BlockSpec((SSD_CHUNK, SSD_D_STATE), lambda g, c: (c, SSD_D_INNER // SSD_D_STATE + g))
    cm = pl.BlockSpec((SSD_CHUNK, SSD_D_STATE), lambda g, c: (c, SSD_D_INNER // SSD_D_STATE + SSD_N_GROUPS + g))
    dtc = pl.BlockSpec((None, SSD_CHUNK, SSD_HPG), lambda g, c: (g, c, 0))
    dtr = pl.BlockSpec((None, SSD_HPG, SSD_CHUNK), lambda g, c: (g, 0, c))
    par = pl.BlockSpec((None, SSD_HPG, LANES), lambda g, c: (g, 0, 0))
    st = pl.BlockSpec((None, None, SSD_GW, SSD_D_STATE), lambda g, c: (g, c, 0, 0))
    return xs, bm, cm, dtc, dtr, par, st


def _ssd_fwd(xc, dtc, dtr, alog_b, d_b, *, name):
    t = xc.shape[0]
    nc = t // SSD_CHUNK
    xs_s, bm_s, cm_s, dtc_s, dtr_s, par_s, st_s = _ssd_group_specs(nc)

    def body(x_ref, b_ref, c_ref, dtc_ref, dtr_ref, alog_ref, d_ref, y_ref, st_ref, s_scr):
        @pl.when(pl.program_id(1) == 0)
        def _():
            s_scr[...] = jnp.zeros_like(s_scr)

        tt, ss, hm, rm = _ssd_masks()
        x = x_ref[...]
        bm = b_ref[...].astype(BF16)
        cm = c_ref[...].astype(BF16)
        s_in = s_scr[...]
        st_ref[...] = s_in
        a_rows = -jnp.exp(alog_ref[...])
        d_rows = d_ref[...]
        dtc_v, dtr_v = dtc_ref[...], dtr_ref[...]
        xb = x.astype(BF16)
        g = _dot_nt(cm, bm)
        y = jnp.zeros(x.shape, F32)
        e_all = jnp.zeros(x.shape, F32)
        w_all = jnp.zeros(x.shape, F32)
        d_all = jnp.zeros((1, SSD_GW), F32)
        e_s = jnp.zeros((SSD_GW, 1), F32)
        for j in range(SSD_HPG):
            dt_col, dt_row, _, _, decay, e_col, dte_col, e_last = _ssd_head_terms(dtc_v, dtr_v, a_rows, j, tt, ss)
            m = g * decay * dt_row
            y = jnp.where(hm[j], _dot_nn(m.astype(BF16), xb), y)
            e_all = jnp.where(hm[j], e_col, e_all)
            w_all = jnp.where(hm[j], dt_col * dte_col, w_all)
            d_all = jnp.where(hm[j], d_rows[j:j + 1, 0:1], d_all)
            e_s = jnp.where(rm[j], e_last, e_s)
        y = y + _dot_nt(cm, s_in.astype(BF16)) * e_all + x * d_all
        y_ref[...] = y
        s_scr[...] = s_in * e_s + _dot_tn((x * w_all).astype(BF16), bm)

    return pl.pallas_call(
        body, grid=(SSD_N_GROUPS, nc),
        in_specs=[xs_s, bm_s, cm_s, dtc_s, dtr_s, par_s, par_s],
        out_specs=[pl.BlockSpec((SSD_CHUNK, SSD_GW), lambda g, c: (c, g)), st_s],
        out_shape=[jax.ShapeDtypeStruct((t, SSD_D_INNER), F32),
                   jax.ShapeDtypeStruct((SSD_N_GROUPS, nc, SSD_GW, SSD_D_STATE), F32)],
        scratch_shapes=[pltpu.VMEM((SSD_GW, SSD_D_STATE), F32)],
        compiler_params=_params("parallel", "arbitrary"), name=name)(xc, xc, xc, dtc, dtr, alog_b, d_b)


def _ssd_bwd(xc, dtc, dtr, alog_b, d_b, states, dy, *, name):
    t = xc.shape[0]
    nc = t // SSD_CHUNK
    q = SSD_CHUNK
    rev = lambda c: nc - 1 - c
    xs_s = pl.BlockSpec((q, SSD_GW), lambda g, c: (rev(c), g))
    bm_s = pl.BlockSpec((q, SSD_D_STATE), lambda g, c: (rev(c), SSD_D_INNER // SSD_D_STATE + g))
    cm_s = pl.BlockSpec((q, SSD_D_STATE), lambda g, c: (rev(c), SSD_D_INNER // SSD_D_STATE + SSD_N_GROUPS + g))
    dtc_s = pl.BlockSpec((None, q, SSD_HPG), lambda g, c: (g, rev(c), 0))
    dtr_s = pl.BlockSpec((None, SSD_HPG, q), lambda g, c: (g, 0, rev(c)))
    par_s = pl.BlockSpec((None, SSD_HPG, LANES), lambda g, c: (g, 0, 0))
    st_s = pl.BlockSpec((None, None, SSD_GW, SSD_D_STATE), lambda g, c: (g, rev(c), 0, 0))

    def body(x_ref, b_ref, c_ref, dtc_ref, dtr_ref, alog_ref, d_ref, st_ref, dy_ref,
             dx_ref, db_ref, dc_ref, ddt_ref, dpar_ref, ds_scr):
        @pl.when(pl.program_id(1) == 0)
        def _():
            ds_scr[...] = jnp.zeros_like(ds_scr)
            dpar_ref[...] = jnp.zeros_like(dpar_ref)

        tt, ss, hm, rm = _ssd_masks()
        tcol = lax.broadcasted_iota(jnp.int32, (q, 1), 0)
        lane = lax.broadcasted_iota(jnp.int32, (1, LANES), 1)
        x = x_ref[...]
        bm = b_ref[...].astype(BF16)
        cm = c_ref[...].astype(BF16)
        s_in = st_ref[...]
        ds = ds_scr[...]
        dyv = dy_ref[...]
        a_rows = -jnp.exp(alog_ref[...])
        d_rows = d_ref[...]
        dtc_v, dtr_v = dtc_ref[...], dtr_ref[...]
        xb = x.astype(BF16)
        dyb = dyv.astype(BF16)
        s_b = s_in.astype(BF16)
        ds_b = ds.astype(BF16)
        g = _dot_nt(cm, bm)
        cs = _dot_nt(cm, s_b)
        bds = _dot_nt(bm, ds_b)
        dy_cs = dyv * cs
        x_bds = x * bds
        dy_x = dyv * x
        ds_s = ds * s_in
        dg = jnp.zeros((q, q), F32)
        dx = jnp.zeros(x.shape, F32)
        e_all = jnp.zeros(x.shape, F32)
        w_all = jnp.zeros(x.shape, F32)
        d_all = jnp.zeros((1, SSD_GW), F32)
        e_s = jnp.zeros((SSD_GW, 1), F32)
        for j in range(SSD_HPG):
            dt_col, dt_row, a_row1, a_11, decay, e_col, dte_col, e_last = _ssd_head_terms(dtc_v, dtr_v, a_rows, j, tt, ss)
            dm = _dot_nt(jnp.where(hm[j], dyv, 0.0).astype(BF16), xb)
            gl = g * decay
            wp = dm * gl
            dg = dg + dm * decay * dt_row
            dx = jnp.where(hm[j], _dot_tn((gl * dt_row).astype(BF16), dyb), dx)
            w = wp * dt_row
            rw_col = jnp.sum(w, axis=1, keepdims=True)
            cw_row = jnp.sum(w, axis=0, keepdims=True)
            cwp_row = jnp.sum(wp, axis=0, keepdims=True)
            r1_col = jnp.sum(jnp.where(hm[j], dy_cs, 0.0), axis=1, keepdims=True) * e_col
            dw_col = jnp.sum(jnp.where(hm[j], x_bds, 0.0), axis=1, keepdims=True)
            w_col = dt_col * dte_col
            s_sum = jnp.sum(jnp.sum(jnp.where(rm[j], ds_s, 0.0), axis=1, keepdims=True), axis=0, keepdims=True)
            last_add = jnp.sum(dw_col * w_col, axis=0, keepdims=True) + e_last * s_sum
            dcum_col = rw_col + r1_col - dw_col * w_col + jnp.where(tcol == q - 1, last_add, 0.0)
            da_row = jnp.sum(jnp.where(tt >= ss, dcum_col, 0.0), axis=0, keepdims=True)
            da_col = jnp.sum(jnp.where(ss >= tt, -cw_row, 0.0), axis=1, keepdims=True)
            ddt_col = a_11 * da_col + dw_col * dte_col
            ddt_row = a_row1 * da_row + cwp_row + jnp.sum(jnp.where(tt == ss, ddt_col, 0.0), axis=0, keepdims=True)
            ddt_ref[j:j + 1, :] = ddt_row
            d_a = jnp.sum(dt_row * da_row, axis=1, keepdims=True) + jnp.sum(dt_col * da_col, axis=0, keepdims=True)
            d_d = jnp.sum(jnp.sum(jnp.where(hm[j], dy_x, 0.0), axis=1, keepdims=True), axis=0, keepdims=True)
            dpar_ref[j:j + 1, :] += jnp.where(lane == 0, d_a * a_11, 0.0) + jnp.where(lane == 1, d_d, 0.0)
            e_all = jnp.where(hm[j], e_col, e_all)
            w_all = jnp.where(hm[j], w_col, w_all)
            d_all = jnp.where(hm[j], d_rows[j:j + 1, 0:1], d_all)
            e_s = jnp.where(rm[j], e_last, e_s)
        dx_ref[...] = dx + w_all * bds + d_all * dyv
        dye = (dyv * e_all).astype(BF16)
        dgb = dg.astype(BF16)
        xw = (x * w_all).astype(BF16)
        dc_ref[...] = _dot_nn(dgb, bm) + _dot_nn(dye, s_b)
        db_ref[...] = _dot_tn(dgb, cm) + _dot_nn(xw, ds_b)
        ds_scr[...] = ds * e_s + _dot_tn(dye, cm)

    return pl.pallas_call(
        body, grid=(SSD_N_GROUPS, nc),
        in_specs=[xs_s, bm_s, cm_s, dtc_s, dtr_s, par_s, par_s, st_s, pl.BlockSpec((q, SSD_GW), lambda g, c: (rev(c), g))],
        out_specs=[pl.BlockSpec((q, SSD_GW), lambda g, c: (rev(c), g)),
                   pl.BlockSpec((q, SSD_D_STATE), lambda g, c: (rev(c), g)),
                   pl.BlockSpec((q, SSD_D_STATE), lambda g, c: (rev(c), g)),
                   pl.BlockSpec((None, SSD_HPG, q), lambda g, c: (g, 0, rev(c))),
                   pl.BlockSpec((None, SSD_HPG, LANES), lambda g, c: (g, 0, 0))],
        out_shape=[jax.ShapeDtypeStruct((t, SSD_D_INNER), F32),
                   jax.ShapeDtypeStruct((t, SSD_N_GROUPS * SSD_D_STATE), F32),
                   jax.ShapeDtypeStruct((t, SSD_N_GROUPS * SSD_D_STATE), F32),
                   jax.ShapeDtypeStruct((SSD_N_GROUPS, SSD_HPG, t), F32),
                   jax.ShapeDtypeStruct((SSD_N_GROUPS, SSD_HPG, LANES), F32)],
        scratch_shapes=[pltpu.VMEM((SSD_GW, SSD_D_STATE), F32)],
        compiler_params=_params("parallel", "arbitrary"), name=name)(xc, xc, xc, dtc, dtr, alog_b, d_b, states, dy)


def _gate_norm_fwd(y, zx, norm_w, *, name):
    t = y.shape[0]
    tr = _row_tile(t, 256)
    row = pl.BlockSpec((tr, SSD_D_INNER), lambda i: (i, 0))

    def body(y_ref, z_ref, w_ref, o_ref):
        for gi in range(SSD_N_GROUPS):
            sl = pl.ds(gi * SSD_GW, SSD_GW)
            z = z_ref[:, sl]
            gv = y_ref[:, sl] * (z * _sigmoid(z))
            r = lax.rsqrt(jnp.mean(gv * gv, axis=-1, keepdims=True) + NORM_EPS)
            o_ref[:, sl] = (gv * r * w_ref[:, sl]).astype(BF16)

    return pl.pallas_call(
        body, grid=(t // tr,), in_specs=[row, row, pl.BlockSpec((1, SSD_D_INNER), lambda i: (0, 0))],
        out_specs=row, out_shape=jax.ShapeDtypeStruct((t, SSD_D_INNER), BF16),
        compiler_params=_params("parallel"), name=name)(y, zx, norm_w)


def _gate_norm_bwd(y, zx, norm_w, dyn, *, name):
    t = y.shape[0]
    tr = _row_tile(t, 256)
    row = pl.BlockSpec((tr, SSD_D_INNER), lambda i: (i, 0))
    vec = pl.BlockSpec((1, SSD_D_INNER), lambda i: (0, 0))

    def body(y_ref, z_ref, w_ref, dyn_ref, dy_ref, dz_ref, dw_ref):
        @pl.when(pl.program_id(0) == 0)
        def _():
            dw_ref[...] = jnp.zeros_like(dw_ref)

        for gi in range(SSD_N_GROUPS):
            sl = pl.ds(gi * SSD_GW, SSD_GW)
            z = z_ref[:, sl]
            yv = y_ref[:, sl]
            sg = _sigmoid(z)
            sz = z * sg
            gv = yv * sz
            r = lax.rsqrt(jnp.mean(gv * gv, axis=-1, keepdims=True) + NORM_EPS)
            ghat = gv * r
            dout = dyn_ref[:, sl].astype(F32)
            dgh = dout * w_ref[:, sl]
            dgv = r * (dgh - ghat * jnp.mean(dgh * ghat, axis=-1, keepdims=True))
            dy_ref[:, sl] = dgv * sz
            dz_ref[:, sl] = (dgv * yv * (sg * (1.0 + z * (1.0 - sg)))).astype(dz_ref.dtype)
            dw_ref[:, sl] += jnp.sum(dout * ghat, axis=0, keepdims=True)

    return pl.pallas_call(
        body, grid=(t // tr,), in_specs=[row, row, vec, row], out_specs=[row, row, vec],
        out_shape=[jax.ShapeDtypeStruct((t, SSD_D_INNER), F32), jax.ShapeDtypeStruct((t, SSD_D_INNER), BF16),
                   jax.ShapeDtypeStruct((1, SSD_D_INNER), F32)],
        compiler_params=_params("arbitrary"), name=name)(y, zx, norm_w, dyn)


ATTN_KV_W = ATTN_N_KV * ATTN_HEAD_DIM
ATTN_Q_HALF = 512
ATTN_K_BLK = ATTN_N_Q * ATTN_HEAD_DIM // ATTN_KV_W
ATTN_V_BLK = ATTN_K_BLK + 1


def _attn_probs(qs, kb, sink_col, first_block):
    w = ATTN_WINDOW
    s = _dot_nt(qs, kb) * (ATTN_HEAD_DIM ** -0.5)
    qpos = lax.broadcasted_iota(jnp.int32, (w, 2 * w), 0) + w
    kpos = lax.broadcasted_iota(jnp.int32, (w, 2 * w), 1)
    rel = qpos - kpos
    valid = (rel >= 0) & (rel < w) & jnp.logical_not(first_block & (kpos < w))
    valid = jnp.concatenate([valid] * ATTN_REP, axis=0)
    s = jnp.where(valid, s, -jnp.inf)
    m = jnp.maximum(jnp.max(s, axis=1, keepdims=True), sink_col)
    e = jnp.exp(s - m)
    es = jnp.exp(sink_col - m)
    inv = 1.0 / (jnp.sum(e, axis=1, keepdims=True) + es)
    return e * inv, es * inv


def _attn_head_views(q_lo, q_hi, k_cur, k_prev, v_cur, v_prev, kh):
    hd = ATTN_HEAD_DIM
    q_half = q_lo if kh < 2 else q_hi
    base = (kh % 2) * ATTN_REP * hd
    qs = jnp.concatenate([q_half[:, base + r * hd: base + (r + 1) * hd] for r in range(ATTN_REP)], axis=0)
    ksl = slice(kh * hd, (kh + 1) * hd)
    kb = jnp.concatenate([k_prev[:, ksl], k_cur[:, ksl]], axis=0)
    vb = jnp.concatenate([v_prev[:, ksl], v_cur[:, ksl]], axis=0)
    return qs, kb, vb


def _sink_col(sink_ref, kh):
    rows = lax.broadcasted_iota(jnp.int32, (ATTN_REP * ATTN_WINDOW, 1), 0)
    col = jnp.zeros((ATTN_REP * ATTN_WINDOW, 1), F32)
    for r in range(ATTN_REP):
        h = kh * ATTN_REP + r
        col = jnp.where((rows >= r * ATTN_WINDOW) & (rows < (r + 1) * ATTN_WINDOW), sink_ref[h:h + 1, 0:1], col)
    return col


def _attn_fwd(qkv, sinks_b, *, name):
    t = qkv.shape[0]
    w = ATTN_WINDOW
    nb = t // w
    prev = lambda n: jnp.maximum(n - 1, 0)

    def body(qlo_ref, qhi_ref, kc_ref, kp_ref, vc_ref, vp_ref, sink_ref, o_ref):
        first = pl.program_id(0) == 0
        q_lo, q_hi = qlo_ref[...], qhi_ref[...]
        k_cur, k_prev, v_cur, v_prev = kc_ref[...], kp_ref[...], vc_ref[...], vp_ref[...]
        for kh in range(ATTN_N_KV):
            qs, kb, vb = _attn_head_views(q_lo, q_hi, k_cur, k_prev, v_cur, v_prev, kh)
            p, _ = _attn_probs(qs, kb, _sink_col(sink_ref, kh), first)
            o = _dot_nn(p.astype(BF16), vb)
            for r in range(ATTN_REP):
                h = kh * ATTN_REP + r
                o_ref[:, pl.ds(h * ATTN_HEAD_DIM, ATTN_HEAD_DIM)] = o[r * w:(r + 1) * w, :].astype(o_ref.dtype)

    qh = lambda half: pl.BlockSpec((w, ATTN_Q_HALF), lambda n: (n, half))
    kv = lambda blk, idx: pl.BlockSpec((w, ATTN_KV_W), lambda n: (idx(n), blk))
    cur = lambda n: n
    return pl.pallas_call(
        body, grid=(nb,),
        in_specs=[qh(0), qh(1), kv(ATTN_K_BLK, cur), kv(ATTN_K_BLK, prev), kv(ATTN_V_BLK, cur), kv(ATTN_V_BLK, prev),
                  pl.BlockSpec((ATTN_N_Q, LANES), lambda n: (0, 0))],
        out_specs=pl.BlockSpec((w, D_MODEL), lambda n: (n, 0)),
        out_shape=jax.ShapeDtypeStruct((t, D_MODEL), BF16),
        compiler_params=_params("parallel"), name=name)(qkv, qkv, qkv, qkv, qkv, qkv, sinks_b)


def _attn_bwd(qkv, sinks_b, dout, *, name):
    t = qkv.shape[0]
    w = ATTN_WINDOW
    nb = t // w
    hd = ATTN_HEAD_DIM
    clamp = lambda n: jnp.minimum(n, nb - 1)
    prev = lambda n: jnp.maximum(clamp(n) - 1, 0)

    def body(qlo_ref, qhi_ref, kc_ref, kp_ref, vc_ref, vp_ref, sink_ref, dolo_ref, dohi_ref,
             dq_ref, dkv_ref, dsink_ref, carry):
        n = pl.program_id(0)

        @pl.when(n == 0)
        def _():
            carry[...] = jnp.zeros_like(carry)
            dsink_ref[...] = jnp.zeros_like(dsink_ref)

        @pl.when(n < nb)
        def _():
            first = n == 0
            q_lo, q_hi = qlo_ref[...], qhi_ref[...]
            do_lo, do_hi = dolo_ref[...], dohi_ref[...]
            k_cur, k_prev, v_cur, v_prev = kc_ref[...], kp_ref[...], vc_ref[...], vp_ref[...]
            rows = lax.broadcasted_iota(jnp.int32, (ATTN_REP * w, 1), 0)
            for kh in range(ATTN_N_KV):
                qs, kb, vb = _attn_head_views(q_lo, q_hi, k_cur, k_prev, v_cur, v_prev, kh)
                do_half = do_lo if kh < 2 else do_hi
                base = (kh % 2) * ATTN_REP * hd
                dos = jnp.concatenate([do_half[:, base + r * hd: base + (r + 1) * hd] for r in range(ATTN_REP)], axis=0)
                p, p_sink = _attn_probs(qs, kb, _sink_col(sink_ref, kh), first)
                dp = _dot_nt(dos, vb)
                delta = jnp.sum(p * dp, axis=1, keepdims=True)
                dsc = (p * (dp - delta) * (hd ** -0.5)).astype(BF16)
                dqs = _dot_nn(dsc, kb)
                dkb = _dot_tn(dsc, qs)
                dvb = _dot_tn(p.astype(BF16), dos)
                sink_g = -p_sink * delta
                for r in range(ATTN_REP):
                    h = kh * ATTN_REP + r
                    dq_ref[:, pl.ds(h * hd, hd)] = dqs[r * w:(r + 1) * w, :].astype(dq_ref.dtype)
                    in_r = (rows >= r * w) & (rows < (r + 1) * w)
                    dsink_ref[h:h + 1, :] += jnp.broadcast_to(
                        jnp.sum(jnp.where(in_r, sink_g, 0.0), axis=0, keepdims=True), (1, LANES))
                kcol = pl.ds(kh * hd, hd)
                vcol = pl.ds(ATTN_KV_W + kh * hd, hd)
                dkv_ref[:, kcol] = (carry[:, kcol] + dkb[0:w, :]).astype(dkv_ref.dtype)
                dkv_ref[:, vcol] = (carry[:, vcol] + dvb[0:w, :]).astype(dkv_ref.dtype)
                carry[:, kcol] = dkb[w:2 * w, :]
                carry[:, vcol] = dvb[w:2 * w, :]

        @pl.when(n == nb)
        def _():
            dkv_ref[...] = carry[...].astype(dkv_ref.dtype)

    qh = lambda half: pl.BlockSpec((w, ATTN_Q_HALF), lambda n: (clamp(n), half))
    kv = lambda blk, idx: pl.BlockSpec((w, ATTN_KV_W), lambda n: (idx(n), blk))
    return pl.pallas_call(
        body, grid=(nb + 1,),
        in_specs=[qh(0), qh(1), kv(ATTN_K_BLK, clamp), kv(ATTN_K_BLK, prev), kv(ATTN_V_BLK, clamp), kv(ATTN_V_BLK, prev),
                  pl.BlockSpec((ATTN_N_Q, LANES), lambda n: (0, 0)), qh(0), qh(1)],
        out_specs=[pl.BlockSpec((w, D_MODEL), lambda n: (clamp(n), 0)),
                   pl.BlockSpec((w, 2 * ATTN_KV_W), lambda n: (jnp.maximum(n - 1, 0), 0)),
                   pl.BlockSpec((ATTN_N_Q, LANES), lambda n: (0, 0))],
        out_shape=[jax.ShapeDtypeStruct((t, D_MODEL), BF16), jax.ShapeDtypeStruct((t, 2 * ATTN_KV_W), BF16),
                   jax.ShapeDtypeStruct((ATTN_N_Q, LANES), F32)],
        scratch_shapes=[pltpu.VMEM((w, 2 * ATTN_KV_W), F32)],
        compiler_params=_params("arbitrary"), name=name)(qkv, qkv, qkv, qkv, qkv, qkv, sinks_b, dout, dout)


def _sq_relu_epilogue(acc):
    r = jnp.maximum(acc, 0.0)
    return acc, r * r


def _sq_relu_bwd_epilogue(acc, pre):
    return (acc * (2.0 * jnp.maximum(pre, 0.0)),)


def _bias_epilogue(acc, bias):
    return (acc + bias,)


def _mlp_fwd(u, w_up, w_down, tag):
    pre, act = _matmul(u, w_up, mode="nn", out_dtypes=(F32, BF16), epilogue=_sq_relu_epilogue, name=f"mlp_up_{tag}")
    f = _matmul(act, w_down, mode="nn", out_dtypes=(F32,), name=f"mlp_down_{tag}")
    return pre, act, f


def _mlp_bwd(u, pre, act, w_up, w_down, df, tag):
    dpre = _matmul(df, w_down, mode="nt", out_dtypes=(BF16,), epilogue=_sq_relu_bwd_epilogue,
                   extras=((pre, "tile"),), name=f"mlp_dact_{tag}")
    dw_down = _matmul(act, df, mode="tn", out_dtypes=(F32,), name=f"mlp_dwdown_{tag}")
    du = _matmul(dpre, w_up, mode="nt", out_dtypes=(F32,), name=f"mlp_du_{tag}")
    dw_up = _matmul(u, dpre, mode="tn", out_dtypes=(F32,), name=f"mlp_dwup_{tag}")
    return du, dw_up, dw_down


def _group_dt_layouts(dt):
    t = dt.shape[0]
    d = dt[:, :SSD_N_HEADS].reshape(t, SSD_N_GROUPS, SSD_HPG)
    return jnp.transpose(d, (1, 0, 2)), jnp.transpose(d, (1, 2, 0))


def _head_param_rows(p):
    return jnp.broadcast_to(p.reshape(SSD_N_GROUPS, SSD_HPG, 1), (SSD_N_GROUPS, SSD_HPG, LANES))


def _local_step(x, target, wts):
    t = x.shape[0]
    row = lambda v: v.reshape(1, -1)
    mix_pre, mix_post, ffn_pre, ffn_post = wts["mix_pre_norm"], wts["mix_post_norm"], wts["ffn_pre_norm"], wts["ffn_post_norm"]

    u0 = _rms_fwd(x, row(mix_pre[0]), name="rms_pre_mix0")
    zx = _matmul(u0, wts["ssd_w_in"], mode="nn", out_dtypes=(F32,), tn=896, name="ssd_in_proj")
    xc = _conv_fwd(zx, wts["ssd_conv_w"], row(wts["ssd_conv_b"]), name="ssd_conv_fwd")
    bias_row = jnp.pad(wts["ssd_dt_bias"], (0, LANES - SSD_N_HEADS)).reshape(1, LANES)
    dt = _softplus_fwd(zx, bias_row, name="ssd_dt_fwd")
    dtc, dtr = _group_dt_layouts(dt)
    alog_b, d_b = _head_param_rows(wts["ssd_a_log"]), _head_param_rows(wts["ssd_d"])
    y_ssd, states = _ssd_fwd(xc, dtc, dtr, alog_b, d_b, name="ssd_scan_fwd")
    norm_w = row(wts["ssd_norm_w"])
    yn = _gate_norm_fwd(y_ssd, zx, norm_w, name="ssd_gate_norm_fwd")
    mix0 = _matmul(yn, wts["ssd_w_out"], mode="nn", out_dtypes=(F32,), name="ssd_out_proj")
    h1, v0 = _rms_fwd(mix0, row(mix_post[0]), resid=x, want_u=row(ffn_pre[0]), name="rms_post_mix0")
    pre0, act0, f0 = _mlp_fwd(v0, wts["mlp_w_up"][0], wts["mlp_w_down"][0], "l0")
    h2, u1 = _rms_fwd(f0, row(ffn_post[0]), resid=h1, want_u=row(mix_pre[1]), name="rms_post_ffn0")

    qkv = _matmul(u1, wts["attn_w_qkv"], mode="nn", out_dtypes=(BF16,), epilogue=_bias_epilogue,
                  extras=((row(wts["attn_b_qkv"]), "row"),), tn=768, name="attn_qkv_proj")
    sinks_b = jnp.broadcast_to(wts["attn_sinks"].reshape(ATTN_N_Q, 1), (ATTN_N_Q, LANES))
    ao = _attn_fwd(qkv, sinks_b, name="attn_fwd")
    mix1 = _matmul(ao, wts["attn_w_o"], mode="nn", out_dtypes=(F32,), epilogue=_bias_epilogue,
                   extras=((row(wts["attn_b_o"]), "row"),), name="attn_out_proj")
    h3, v1 = _rms_fwd(mix1, row(mix_post[1]), resid=h2, want_u=row(ffn_pre[1]), name="rms_post_mix1")
    pre1, act1, f1 = _mlp_fwd(v1, wts["mlp_w_up"][1], wts["mlp_w_down"][1], "l1")
    h4 = _rms_fwd(f1, row(ffn_post[1]), resid=h3, name="rms_post_ffn1")

    dh4, loss_tile = _loss_head(h4, target, name="loss_head")

    df1, g_ffn_post1 = _rms_bwd(f1, row(ffn_post[1]), dh4, out_dtype=BF16, name="rms_post_ffn1_bwd")
    dv1, g_up1, g_down1 = _mlp_bwd(v1, pre1, act1, wts["mlp_w_up"][1], wts["mlp_w_down"][1], df1, "l1")
    dh3, g_ffn_pre1 = _rms_bwd(h3, row(ffn_pre[1]), dv1, resid=dh4, name="rms_pre_ffn1_bwd")
    dmix1, g_mix_post1 = _rms_bwd(mix1, row(mix_post[1]), dh3, out_dtype=BF16, name="rms_post_mix1_bwd")
    g_b_o = _col_sum(dmix1, name="attn_bo_grad")
    g_w_o = _matmul(ao, dmix1, mode="tn", out_dtypes=(F32,), name="attn_dwo")
    dao = _matmul(dmix1, wts["attn_w_o"], mode="nt", out_dtypes=(BF16,), name="attn_dao")
    dq, dkv, g_sinks = _attn_bwd(qkv, sinks_b, dao, name="attn_bwd")
    dqkv = jnp.concatenate([dq, dkv], axis=1)
    g_b_qkv = _col_sum(dqkv, name="attn_bqkv_grad")
    g_w_qkv = _matmul(u1, dqkv, mode="tn", out_dtypes=(F32,), tn=768, name="attn_dwqkv")
    du1 = _matmul(dqkv, wts["attn_w_qkv"], mode="nt", out_dtypes=(F32,), tk=768, name="attn_du")
    dh2, g_mix_pre1 = _rms_bwd(h2, row(mix_pre[1]), du1, resid=dh3, name="rms_pre_mix1_bwd")

    df0, g_ffn_post0 = _rms_bwd(f0, row(ffn_post[0]), dh2, out_dtype=BF16, name="rms_post_ffn0_bwd")
    dv0, g_up0, g_down0 = _mlp_bwd(v0, pre0, act0, wts["mlp_w_up"][0], wts["mlp_w_down"][0], df0, "l0")
    dh1, g_ffn_pre0 = _rms_bwd(h1, row(ffn_pre[0]), dv0, resid=dh2, name="rms_pre_ffn0_bwd")
    dmix0, g_mix_post0 = _rms_bwd(mix0, row(mix_post[0]), dh1, out_dtype=BF16, name="rms_post_mix0_bwd")
    g_w_out = _matmul(yn, dmix0, mode="tn", out_dtypes=(F32,), name="ssd_dwout")
    dyn = _matmul(dmix0, wts["ssd_w_out"], mode="nt", out_dtypes=(BF16,), name="ssd_dyn")
    dy_ssd, dz, g_norm_w = _gate_norm_bwd(y_ssd, zx, norm_w, dyn, name="ssd_gate_norm_bwd")
    dxc, dbm, dcm, ddt_r, dpar = _ssd_bwd(xc, dtc, dtr, alog_b, d_b, states, dy_ssd, name="ssd_scan_bwd")
    dxc_full = jnp.concatenate([dxc, dbm, dcm], axis=1)
    dxbc, g_conv_w, g_conv_b = _conv_bwd(zx, wts["ssd_conv_w"], row(wts["ssd_conv_b"]), dxc_full, name="ssd_conv_bwd")
    ddt = jnp.pad(jnp.transpose(ddt_r, (2, 0, 1)).reshape(t, SSD_N_HEADS), ((0, 0), (0, LANES - SSD_N_HEADS)))
    ddt_raw, g_dt_bias = _softplus_bwd(zx, bias_row, ddt, name="ssd_dt_bwd")
    dzx = jnp.concatenate([dz, dxbc, ddt_raw], axis=1)
    g_w_in = _matmul(u0, dzx, mode="tn", out_dtypes=(F32,), tn=896, name="ssd_dwin")
    du0 = _matmul(dzx, wts["ssd_w_in"], mode="nt", out_dtypes=(F32,), tk=896, name="ssd_du")
    grad_x, g_mix_pre0 = _rms_bwd(x, row(mix_pre[0]), du0, resid=dh1, name="rms_pre_mix0_bwd")

    dpar = dpar.reshape(SSD_N_HEADS, LANES)
    grads = {
        "ssd_w_in": g_w_in[:, :SSD_IN_DIM], "ssd_conv_w": g_conv_w, "ssd_conv_b": g_conv_b.reshape(-1),
        "ssd_dt_bias": g_dt_bias[0, :SSD_N_HEADS], "ssd_a_log": dpar[:, 0], "ssd_d": dpar[:, 1],
        "ssd_norm_w": g_norm_w.reshape(-1), "ssd_w_out": g_w_out,
        "attn_w_qkv": g_w_qkv, "attn_b_qkv": g_b_qkv.reshape(-1), "attn_sinks": g_sinks[:, 0],
        "attn_w_o": g_w_o, "attn_b_o": g_b_o.reshape(-1),
        "mlp_w_up": jnp.stack([g_up0, g_up1]), "mlp_w_down": jnp.stack([g_down0, g_down1]),
        "mix_pre_norm": jnp.concatenate([g_mix_pre0, g_mix_pre1]), "mix_post_norm": jnp.concatenate([g_mix_post0, g_mix_post1]),
        "ffn_pre_norm": jnp.concatenate([g_ffn_pre0, g_ffn_pre1]), "ffn_post_norm": jnp.concatenate([g_ffn_post0, g_ffn_post1]),
    }
    return loss_tile, grad_x, grads


ANY = pl.BlockSpec(memory_space=pl.ANY)


def _mesh_position():
    return lax.axis_index("x"), lax.axis_index("y"), lax.axis_index("c")


def _flip(v, bit):
    return 1 - v if bit else v


def _all_gather_chips(arrs, *, name):
    n = len(arrs)
    n_rem = (N_CHIPS - 1) * n

    def body(*refs):
        ins, outs = refs[:n], refs[n:2 * n]
        send_sems, recv_sems, loc_sems = refs[2 * n:]
        xi, yi, ci = _mesh_position()
        me = 2 * xi + yi
        local = [pltpu.make_async_copy(ins[i], outs[i].at[me], loc_sems.at[i]) for i in range(n)]
        sends, recvs = [], []
        for j, (bx, by) in enumerate(((1, 0), (0, 1), (1, 1))):
            px, py = _flip(xi, bx), _flip(yi, by)
            for i in range(n):
                k = j * n + i
                mk = functools.partial(pltpu.make_async_remote_copy, src_ref=ins[i], send_sem=send_sems.at[k],
                                       recv_sem=recv_sems.at[k], device_id=(px, py, ci), device_id_type=MESH)
                sends.append(mk(dst_ref=outs[i].at[me]))
                recvs.append(mk(dst_ref=outs[i].at[2 * px + py]))
        for cp in local + sends:
            cp.start()
        for cp in recvs:
            cp.wait_recv()
        for cp in sends:
            cp.wait_send()
        for cp in local:
            cp.wait()

    return pl.pallas_call(
        body, in_specs=[ANY] * n, out_specs=[ANY] * n,
        out_shape=[jax.ShapeDtypeStruct((N_CHIPS,) + a.shape, a.dtype) for a in arrs],
        scratch_shapes=[pltpu.SemaphoreType.DMA((n_rem,)), pltpu.SemaphoreType.DMA((n_rem,)), pltpu.SemaphoreType.DMA((n,))],
        name=name)(*arrs)


def _grad_exchange(pack, small, *, name):
    r = pack.shape[2]
    srows = small.shape[0]
    n_peer = N_DEV - 1

    def body(pack_ref, small_ref, recv_ref, small_all_ref, send_sems, recv_sems, loc_sems):
        xi, yi, ci = _mesh_position()
        me = 4 * xi + 2 * yi + ci
        local = [pltpu.make_async_copy(pack_ref.at[2 * xi + yi, ci], recv_ref.at[me], loc_sems.at[0]),
                 pltpu.make_async_copy(small_ref, small_all_ref.at[me], loc_sems.at[1])]
        sends, recvs = [], []
        for k in range(1, N_DEV):
            px, py, pc = _flip(xi, (k >> 2) & 1), _flip(yi, (k >> 1) & 1), _flip(ci, k & 1)
            peer = 4 * px + 2 * py + pc
            big = functools.partial(pltpu.make_async_remote_copy, src_ref=pack_ref.at[2 * px + py, pc],
                                    send_sem=send_sems.at[k - 1], recv_sem=recv_sems.at[k - 1],
                                    device_id=(px, py, pc), device_id_type=MESH)
            sml = functools.partial(pltpu.make_async_remote_copy, src_ref=small_ref,
                                    send_sem=send_sems.at[n_peer + k - 1], recv_sem=recv_sems.at[n_peer + k - 1],
                                    device_id=(px, py, pc), device_id_type=MESH)
            sends += [big(dst_ref=recv_ref.at[me]), sml(dst_ref=small_all_ref.at[me])]
            recvs += [big(dst_ref=recv_ref.at[peer]), sml(dst_ref=small_all_ref.at[peer])]
        for cp in local + sends:
            cp.start()
        for cp in recvs:
            cp.wait_recv()
        for cp in sends:
            cp.wait_send()
        for cp in local:
            cp.wait()

    return pl.pallas_call(
        body, in_specs=[ANY, ANY], out_specs=[ANY, ANY],
        out_shape=[jax.ShapeDtypeStruct((N_DEV, r, pack.shape[3]), pack.dtype),
                   jax.ShapeDtypeStruct((N_DEV, srows, small.shape[1]), small.dtype)],
        scratch_shapes=[pltpu.SemaphoreType.DMA((2 * n_peer,)), pltpu.SemaphoreType.DMA((2 * n_peer,)),
                        pltpu.SemaphoreType.DMA((2,))],
        name=name)(pack, small)


def _sibling_swap(half, *, name):
    def body(in_ref, out_ref, send_sem, recv_sem, loc_sem):
        xi, yi, ci = _mesh_position()
        local = pltpu.make_async_copy(in_ref, out_ref.at[ci], loc_sem)
        mk = functools.partial(pltpu.make_async_remote_copy, src_ref=in_ref, send_sem=send_sem, recv_sem=recv_sem,
                               device_id=(xi, yi, 1 - ci), device_id_type=MESH)
        send, recv = mk(dst_ref=out_ref.at[ci]), mk(dst_ref=out_ref.at[1 - ci])
        local.start()
        send.start()
        recv.wait_recv()
        send.wait_send()
        local.wait()

    return pl.pallas_call(
        body, in_specs=[ANY], out_specs=ANY, out_shape=jax.ShapeDtypeStruct((2,) + half.shape, half.dtype),
        scratch_shapes=[pltpu.SemaphoreType.DMA, pltpu.SemaphoreType.DMA, pltpu.SemaphoreType.DMA],
        name=name)(half)


def _sum_parts(parts, *, name, rows_per_tile):
    p, r, c = parts.shape
    tr = min(r, rows_per_tile)
    assert r % tr == 0

    def body(x_ref, o_ref):
        s = pl.program_id(1)

        @pl.when(s == 0)
        def _():
            o_ref[...] = x_ref[...].astype(F32)

        @pl.when(s > 0)
        def _():
            o_ref[...] += x_ref[...].astype(F32)

    return pl.pallas_call(
        body, grid=(r // tr, p), in_specs=[pl.BlockSpec((None, tr, c), lambda i, s: (s, i, 0))],
        out_specs=pl.BlockSpec((tr, c), lambda i, s: (i, 0)), out_shape=jax.ShapeDtypeStruct((r, c), F32),
        compiler_params=_params("parallel", "arbitrary"), name=name)(parts)


def _adamw(w, g, m, v, *, name):
    r, c = w.shape
    tr = 256 if r % 256 == 0 else r
    blk = pl.BlockSpec((tr, c), lambda i: (i, 0))

    def body(w_ref, g_ref, m_ref, v_ref, d_ref, nm_ref, nv_ref):
        gv = g_ref[...]
        nm = ADAM_B1 * m_ref[...] + (1.0 - ADAM_B1) * gv
        nv = ADAM_B2 * v_ref[...] + (1.0 - ADAM_B2) * (gv * gv)
        m_hat = nm / (1.0 - ADAM_B1 ** ADAM_STEP)
        v_hat = nv / (1.0 - ADAM_B2 ** ADAM_STEP)
        d_ref[...] = -ADAM_LR * (m_hat / (jnp.sqrt(v_hat) + ADAM_EPS) + ADAM_WD * w_ref[...])
        nm_ref[...] = nm
        nv_ref[...] = nv

    sh = jax.ShapeDtypeStruct((r, c), F32)
    return pl.pallas_call(body, grid=(r // tr,), in_specs=[blk] * 4, out_specs=[blk] * 3, out_shape=[sh] * 3,
                          compiler_params=_params("parallel"), name=name)(w, g, m, v)


PACK_PARTS = (("ssd_w_in", 772), ("ssd_w_out", 256), ("attn_w_qkv", 192), ("attn_w_o", 128), ("mlp_w_up", 1024), ("mlp_w_down", 1024))
PACK_ROWS = 3456
PACK_TILE = 1152

SM_CONV_B, SM_NORM_W, SM_MIX_PRE, SM_MIX_POST, SM_FFN_PRE, SM_FFN_POST, SM_MISC, SM_CONV_W, SM_B_QKV, SM_B_O = 0, 4, 6, 8, 10, 12, 14, 16, 32, 34
SM_ROWS = 40
MISC_DT_BIAS, MISC_A_LOG, MISC_D, MISC_SINKS, MISC_LOSS = 0, 32, 64, 96, 112


def _pack_matrix_grads(g):
    w_in = g["ssd_w_in"].reshape(2, 512, 4, 1544).transpose(2, 0, 1, 3).reshape(4, 2, 772, D_MODEL)
    w_out = g["ssd_w_out"].reshape(4, 2, 256, D_MODEL)
    qkv = g["attn_w_qkv"].reshape(2, 512, 4, 384).transpose(2, 0, 1, 3).reshape(4, 2, 192, D_MODEL)
    w_o = g["attn_w_o"].reshape(4, 2, 128, D_MODEL)
    up = g["mlp_w_up"].reshape(2, 2, 512, 4, D_MODEL).transpose(3, 1, 0, 2, 4).reshape(4, 2, 1024, D_MODEL)
    down = g["mlp_w_down"].reshape(2, 4, 2, 512, D_MODEL).transpose(1, 2, 0, 3, 4).reshape(4, 2, 1024, D_MODEL)
    used = sum(rows for _, rows in PACK_PARTS)
    pad = jnp.zeros((4, 2, PACK_ROWS - used, D_MODEL), F32)
    return jnp.concatenate([w_in, w_out, qkv, w_o, up, down, pad], axis=2).astype(BF16)


def _unpack_matrix_grads(red):
    out, lo = {}, 0
    for name, rows in PACK_PARTS:
        out[name] = red[:, lo:lo + rows]
        lo += rows
    return {
        "ssd_w_in": out["ssd_w_in"].reshape(1, 1024, 1544),
        "ssd_w_out": out["ssd_w_out"].reshape(1, 512, D_MODEL),
        "attn_w_qkv": out["attn_w_qkv"].reshape(1, 1024, 384),
        "attn_w_o": out["attn_w_o"].reshape(1, 256, D_MODEL),
        "mlp_w_up": out["mlp_w_up"].reshape(2, 2, 512, D_MODEL).transpose(1, 0, 2, 3).reshape(2, 1024, D_MODEL),
        "mlp_w_down": out["mlp_w_down"].reshape(2, 2, 512, D_MODEL).transpose(1, 0, 2, 3).reshape(2, 1024, D_MODEL),
    }


def _rows(v):
    return v.reshape(-1, D_MODEL)


def _misc_row(dt_bias, a_log, d, sinks, loss):
    pad = jnp.zeros((D_MODEL - MISC_LOSS - 1,), F32)
    return jnp.concatenate([dt_bias.reshape(-1), a_log.reshape(-1), d.reshape(-1), sinks.reshape(-1), loss.reshape(1), pad]).reshape(1, D_MODEL)


def _replicated_rows(p, loss):
    return jnp.concatenate([
        _rows(p["ssd_conv_b"]), _rows(p["ssd_norm_w"]), _rows(p["mix_pre_norm"]), _rows(p["mix_post_norm"]),
        _rows(p["ffn_pre_norm"]), _rows(p["ffn_post_norm"]),
        _misc_row(p["ssd_dt_bias"], p["ssd_a_log"], p["ssd_d"], p["attn_sinks"], loss), jnp.zeros((1, D_MODEL), F32)], axis=0)


def _sharded_rows(conv_w, b_qkv, b_o):
    last = jnp.concatenate([b_qkv.reshape(-1), b_o.reshape(-1), jnp.zeros((D_MODEL - 640,), F32)]).reshape(1, D_MODEL)
    return jnp.concatenate([conv_w.reshape(SSD_CONV_WIDTH, D_MODEL), last, jnp.zeros((3, D_MODEL), F32)], axis=0)


REPLICATED = ("ssd_conv_b", "ssd_dt_bias", "ssd_a_log", "ssd_d", "ssd_norm_w", "attn_sinks",
              "mix_pre_norm", "mix_post_norm", "ffn_pre_norm", "ffn_post_norm")
MATRICES = ("ssd_w_in", "ssd_w_out", "attn_w_qkv", "attn_w_o", "mlp_w_up", "mlp_w_down")
WEIGHT_NAMES = ("ssd_w_in", "ssd_conv_w", "ssd_conv_b", "ssd_dt_bias", "ssd_a_log", "ssd_d", "ssd_norm_w", "ssd_w_out",
                "attn_w_qkv", "attn_b_qkv", "attn_sinks", "attn_w_o", "attn_b_o", "mlp_w_up", "mlp_w_down",
                "mix_pre_norm", "mix_post_norm", "ffn_pre_norm", "ffn_post_norm")


def _unpack_small(rows16, rows8, like):
    misc = rows16[SM_MISC]
    out = {
        "ssd_conv_b": rows16[SM_CONV_B:SM_CONV_B + 4], "ssd_norm_w": rows16[SM_NORM_W:SM_NORM_W + 2],
        "mix_pre_norm": rows16[SM_MIX_PRE:SM_MIX_PRE + 2], "mix_post_norm": rows16[SM_MIX_POST:SM_MIX_POST + 2],
        "ffn_pre_norm": rows16[SM_FFN_PRE:SM_FFN_PRE + 2], "ffn_post_norm": rows16[SM_FFN_POST:SM_FFN_POST + 2],
        "ssd_dt_bias": misc[MISC_DT_BIAS:MISC_DT_BIAS + 32], "ssd_a_log": misc[MISC_A_LOG:MISC_A_LOG + 32],
        "ssd_d": misc[MISC_D:MISC_D + 32], "attn_sinks": misc[MISC_SINKS:MISC_SINKS + 16],
        "ssd_conv_w": rows8[0:SSD_CONV_WIDTH], "attn_b_qkv": rows8[SSD_CONV_WIDTH, 0:384], "attn_b_o": rows8[SSD_CONV_WIDTH, 384:640],
    }
    return {k: v.reshape(like[k].shape) for k, v in out.items()}


def kernel(x, ssd_w_in, ssd_conv_w, ssd_conv_b, ssd_dt_bias, ssd_a_log, ssd_d, ssd_norm_w, ssd_w_out, attn_w_qkv, attn_b_qkv, attn_sinks, attn_w_o, attn_b_o, mlp_w_up, mlp_w_down, mix_pre_norm, mix_post_norm, ffn_pre_norm, ffn_post_norm, loss_target, m_ssd_w_in, m_ssd_conv_w, m_ssd_conv_b, m_ssd_dt_bias, m_ssd_a_log, m_ssd_d, m_ssd_norm_w, m_ssd_w_out, m_attn_w_qkv, m_attn_b_qkv, m_attn_sinks, m_attn_w_o, m_attn_b_o, m_mlp_w_up, m_mlp_w_down, m_mix_pre_norm, m_mix_post_norm, m_ffn_pre_norm, m_ffn_post_norm, v_ssd_w_in, v_ssd_conv_w, v_ssd_conv_b, v_ssd_dt_bias, v_ssd_a_log, v_ssd_d, v_ssd_norm_w, v_ssd_w_out, v_attn_w_qkv, v_attn_b_qkv, v_attn_sinks, v_attn_w_o, v_attn_b_o, v_mlp_w_up, v_mlp_w_down, v_mix_pre_norm, v_mix_post_norm, v_ffn_pre_norm, v_ffn_post_norm):
    w = dict(zip(WEIGHT_NAMES, (ssd_w_in, ssd_conv_w, ssd_conv_b, ssd_dt_bias, ssd_a_log, ssd_d, ssd_norm_w, ssd_w_out, attn_w_qkv, attn_b_qkv, attn_sinks, attn_w_o, attn_b_o, mlp_w_up, mlp_w_down, mix_pre_norm, mix_post_norm, ffn_pre_norm, ffn_post_norm)))
    m = dict(zip(WEIGHT_NAMES, (m_ssd_w_in, m_ssd_conv_w, m_ssd_conv_b, m_ssd_dt_bias, m_ssd_a_log, m_ssd_d, m_ssd_norm_w, m_ssd_w_out, m_attn_w_qkv, m_attn_b_qkv, m_attn_sinks, m_attn_w_o, m_attn_b_o, m_mlp_w_up, m_mlp_w_down, m_mix_pre_norm, m_mix_post_norm, m_ffn_pre_norm, m_ffn_post_norm)))
    v = dict(zip(WEIGHT_NAMES, (v_ssd_w_in, v_ssd_conv_w, v_ssd_conv_b, v_ssd_dt_bias, v_ssd_a_log, v_ssd_d, v_ssd_norm_w, v_ssd_w_out, v_attn_w_qkv, v_attn_b_qkv, v_attn_sinks, v_attn_w_o, v_attn_b_o, v_mlp_w_up, v_mlp_w_down, v_mix_pre_norm, v_mix_post_norm, v_ffn_pre_norm, v_ffn_post_norm)))
    chip = 2 * lax.axis_index("x") + lax.axis_index("y")

    shards = [w["ssd_w_in"][0].astype(BF16), w["ssd_w_out"][0].astype(BF16), w["attn_w_qkv"][0].astype(BF16),
              w["attn_w_o"][0].astype(BF16), w["mlp_w_up"].astype(BF16), w["mlp_w_down"].astype(BF16),
              w["ssd_conv_w"][0], w["attn_b_qkv"], w["attn_b_o"]]
    g_in, g_out, g_qkv, g_o, g_up, g_down, g_conv, g_bqkv, g_bo = _all_gather_chips(shards, name="weight_all_gather")
    full = {
        "ssd_w_in": jnp.pad(g_in.transpose(1, 0, 2).reshape(D_MODEL, SSD_IN_DIM), ((0, 0), (0, SSD_IN_PAD - SSD_IN_DIM))),
        "ssd_w_out": g_out.reshape(SSD_D_INNER, D_MODEL),
        "attn_w_qkv": g_qkv.transpose(1, 0, 2).reshape(D_MODEL, ATTN_QKV),
        "attn_w_o": g_o.reshape(D_MODEL, D_MODEL),
        "mlp_w_up": g_up.transpose(1, 2, 0, 3).reshape(2, D_MODEL, D_FF),
        "mlp_w_down": g_down.transpose(1, 0, 2, 3).reshape(2, D_FF, D_MODEL),
        "ssd_conv_w": g_conv.transpose(1, 0, 2).reshape(SSD_CONV_WIDTH, SSD_CONV_DIM),
        "attn_b_qkv": g_bqkv.reshape(ATTN_QKV), "attn_b_o": g_bo.reshape(D_MODEL),
    }
    for name in REPLICATED:
        full[name] = w[name][0] if name.startswith(("ssd_", "attn_")) else w[name]

    loss_tile, grad_x, g = _local_step(x[0], loss_target[0], full)

    conv_w_rows = g["ssd_conv_w"].reshape(SSD_CONV_WIDTH * N_CHIPS, D_MODEL)
    b_qkv_rows = jnp.pad(g["attn_b_qkv"], (0, 2 * D_MODEL - ATTN_QKV)).reshape(2, D_MODEL)
    small = jnp.concatenate([_replicated_rows(g, loss_tile[0, 0]), conv_w_rows, b_qkv_rows, _rows(g["attn_b_o"]),
                             jnp.zeros((SM_ROWS - SM_B_O - 1, D_MODEL), F32)], axis=0)
    recv, small_all = _grad_exchange(_pack_matrix_grads(g), small, name="grad_exchange")
    half = _sum_parts(recv, name="grad_sum", rows_per_tile=PACK_TILE)
    red = _sibling_swap(half, name="grad_halves_swap")
    small_sum = _sum_parts(small_all, name="small_grad_sum", rows_per_tile=SM_ROWS)

    grads = _unpack_matrix_grads(red)
    conv_w_g = lax.dynamic_index_in_dim(small_sum[SM_CONV_W:SM_CONV_W + 16].reshape(SSD_CONV_WIDTH, N_CHIPS, D_MODEL), chip, axis=1, keepdims=False)
    b_qkv_g = lax.dynamic_slice_in_dim(small_sum[SM_B_QKV:SM_B_QKV + 2].reshape(-1), chip * 384, 384)
    b_o_g = lax.dynamic_slice_in_dim(small_sum[SM_B_O], chip * 256, 256)
    small_g = jnp.concatenate([small_sum[0:16], _sharded_rows(conv_w_g, b_qkv_g, b_o_g)], axis=0)
    grads.update(_unpack_small(small_g[0:16], small_g[16:24], w))
    loss = small_sum[SM_MISC, MISC_LOSS]

    delta, new_m, new_v = {}, {}, {}
    for name in MATRICES:
        shape = w[name].shape
        as2d = lambda a: a.reshape(-1, shape[-1])
        d2, m2, v2 = _adamw(as2d(w[name]), as2d(grads[name]), as2d(m[name]), as2d(v[name]), name=f"adamw_{name}")
        delta[name], new_m[name], new_v[name] = d2.reshape(shape), m2.reshape(shape), v2.reshape(shape)
    zero = jnp.zeros((), F32)
    small_pack = lambda p: jnp.concatenate([_replicated_rows({k: p[k] for k in REPLICATED}, zero),
                                            _sharded_rows(p["ssd_conv_w"], p["attn_b_qkv"], p["attn_b_o"])], axis=0)
    d_s, m_s, v_s = _adamw(small_pack(w), small_g, small_pack(m), small_pack(v), name="adamw_vectors")
    delta.update(_unpack_small(d_s[0:16], d_s[16:24], w))
    new_m.update(_unpack_small(m_s[0:16], m_s[16:24], w))
    new_v.update(_unpack_small(v_s[0:16], v_s[16:24], w))

    return (loss, grad_x[None], *[grads[n] for n in WEIGHT_NAMES], *[delta[n] for n in WEIGHT_NAMES],
            *[new_m[n] for n in WEIGHT_NAMES], *[new_v[n] for n in WEIGHT_NAMES])
```

```python
import functools
import math

import jax
import jax.numpy as jnp
from jax import lax
from jax.experimental import pallas as pl
from jax.experimental.pallas import tpu as pltpu

F32 = jnp.float32
BF16 = jnp.bfloat16

D_MODEL = 1024
SSD_D_INNER = 2048
SSD_HEAD_DIM = 64
SSD_N_HEADS = 32
SSD_N_GROUPS = 8
SSD_HPG = 4
SSD_D_STATE = 128
SSD_CONV_WIDTH = 4
SSD_CHUNK = 128
SSD_CONV_DIM = 4096
SSD_IN_DIM = 6176
SSD_IN_PAD = 6272
SSD_GW = SSD_HPG * SSD_HEAD_DIM
ATTN_HEAD_DIM = 64
ATTN_N_Q = 16
ATTN_N_KV = 4
ATTN_REP = 4
ATTN_WINDOW = 128
ATTN_QKV = 1536
D_FF = 4096
NORM_EPS = 1e-6

ADAM_LR = 0.001
ADAM_B1 = 0.9
ADAM_B2 = 0.999
ADAM_EPS = 1e-08
ADAM_WD = 0.01
ADAM_STEP = 10

N_CHIPS = 4
N_DEV = 8
LANES = 128
VMEM_LIMIT = 48 * 1024 * 1024

MESH = pl.DeviceIdType.MESH


def _params(*sem):
    return pltpu.CompilerParams(dimension_semantics=sem, vmem_limit_bytes=VMEM_LIMIT)


def _dot(a, b, dims):
    return lax.dot_general(a, b, (dims, ((), ())), preferred_element_type=F32)


def _dot_nn(a, b):
    return _dot(a, b, ((1,), (0,)))


def _dot_nt(a, b):
    return _dot(a, b, ((1,), (1,)))


def _dot_tn(a, b):
    return _dot(a, b, ((0,), (0,)))


def _sigmoid(x):
    return 1.0 / (1.0 + jnp.exp(-x))


def _matmul(a, b, *, mode, out_dtypes, name, epilogue=None, extras=(), tm=1024, tn=1024, tk=1024,
            b_shards=False, out_shards=False):
    if b_shards:
        s, b_rows, b_cols = b.shape
        b2 = (b_rows, s * b_cols)
        if mode == "nn":
            tn = b_cols
        else:
            assert mode == "nt"
            tk = b_cols
    else:
        b2 = b.shape
    if mode == "nn":
        (m, k), (k2, n) = a.shape, b2
    elif mode == "nt":
        (m, k), (n, k2) = a.shape, b2
    else:
        (k, m), (k2, n) = a.shape, b2
    assert k == k2, (a.shape, b.shape, mode)
    tm, tn, tk = min(tm, m), min(tn, n), min(tk, k)
    assert m % tm == 0 and n % tn == 0 and k % tk == 0, (m, n, k, tm, tn, tk)
    nk = k // tk
    if mode == "tn":
        a_spec = pl.BlockSpec((tk, tm), lambda i, j, kk: (kk, i))
    else:
        a_spec = pl.BlockSpec((tm, tk), lambda i, j, kk: (i, kk))
    if b_shards and mode == "nn":
        b_spec = pl.BlockSpec((None, tk, tn), lambda i, j, kk: (j, kk, 0))
    elif b_shards:
        b_spec = pl.BlockSpec((None, tn, tk), lambda i, j, kk: (kk, j, 0))
    elif mode == "nt":
        b_spec = pl.BlockSpec((tn, tk), lambda i, j, kk: (j, kk))
    else:
        b_spec = pl.BlockSpec((tk, tn), lambda i, j, kk: (kk, j))
    dims = {"nn": ((1,), (0,)), "nt": ((1,), (1,)), "tn": ((0,), (0,))}[mode]
    ex_specs = []
    for arr, kind in extras:
        if kind == "tile":
            ex_specs.append(pl.BlockSpec((tm, tn), lambda i, j, kk: (i, j)))
        else:
            ex_specs.append(pl.BlockSpec((1, tn), lambda i, j, kk: (0, j)))
    n_ex, n_out = len(extras), len(out_dtypes)
    if epilogue is None:
        epilogue = lambda acc: (acc,)

    def body(a_ref, b_ref, *rest):
        ex = rest[:n_ex]
        outs = rest[n_ex:n_ex + n_out]

        def finish(acc):
            res = epilogue(acc, *[e[...] for e in ex])
            for o, r in zip(outs, res):
                o[...] = r.astype(o.dtype)

        if nk == 1:
            finish(_dot(a_ref[...], b_ref[...], dims))
        else:
            acc_ref = rest[-1]
            kk = pl.program_id(2)

            @pl.when(kk == 0)
            def _():
                acc_ref[...] = jnp.zeros_like(acc_ref)

            acc_ref[...] += _dot(a_ref[...], b_ref[...], dims)

            @pl.when(kk == nk - 1)
            def _():
                finish(acc_ref[...])

    if out_shards:
        out_spec = pl.BlockSpec((None, tm, tn), lambda i, j, kk: (j, i, 0))
        out_dims = (n // tn, m, tn)
    else:
        out_spec = pl.BlockSpec((tm, tn), lambda i, j, kk: (i, j))
        out_dims = (m, n)
    outs = pl.pallas_call(
        body,
        grid=(m // tm, n // tn, nk),
        in_specs=[a_spec, b_spec] + ex_specs,
        out_specs=[out_spec for _ in out_dtypes],
        out_shape=[jax.ShapeDtypeStruct(out_dims, dt) for dt in out_dtypes],
        scratch_shapes=[] if nk == 1 else [pltpu.VMEM((tm, tn), F32)],
        compiler_params=_params("parallel", "parallel", "arbitrary"),
        name=name,
    )(a, b, *[arr for arr, _ in extras])
    return outs[0] if n_out == 1 else outs


def _row_tile(t, want):
    return min(t, want)


def _rms_fwd(x, w, *, name, resid=None, want_u=None):
    t, d = x.shape
    tr = _row_tile(t, 512)

    def norm(v, wv):
        return v * lax.rsqrt(jnp.mean(v * v, axis=-1, keepdims=True) + NORM_EPS) * wv

    row = pl.BlockSpec((tr, d), lambda i: (i, 0))
    vec = pl.BlockSpec((1, d), lambda i: (0, 0))
    if resid is None:
        def body(x_ref, w_ref, o_ref):
            o_ref[...] = norm(x_ref[...], w_ref[...]).astype(BF16)
        ins, in_specs = (x, w), [row, vec]
        out_shape, out_specs = jax.ShapeDtypeStruct((t, d), BF16), row
    elif want_u is None:
        def body(x_ref, w_ref, r_ref, o_ref):
            o_ref[...] = r_ref[...] + norm(x_ref[...], w_ref[...])
        ins, in_specs = (x, w, resid), [row, vec, row]
        out_shape, out_specs = jax.ShapeDtypeStruct((t, d), F32), row
    else:
        def body(x_ref, w_ref, r_ref, w2_ref, o_ref, u_ref):
            h = r_ref[...] + norm(x_ref[...], w_ref[...])
            o_ref[...] = h
            u_ref[...] = norm(h, w2_ref[...]).astype(BF16)
        ins, in_specs = (x, w, resid, want_u), [row, vec, row, vec]
        out_shape = [jax.ShapeDtypeStruct((t, d), F32), jax.ShapeDtypeStruct((t, d), BF16)]
        out_specs = [row, row]
    return pl.pallas_call(body, grid=(t // tr,), in_specs=in_specs, out_specs=out_specs, out_shape=out_shape,
                          compiler_params=_params("parallel"), name=name)(*ins)


def _rms_bwd(x, w, dy, *, name, resid=None, out_dtype=F32):
    t, d = x.shape
    tr = _row_tile(t, 512)
    row = pl.BlockSpec((tr, d), lambda i: (i, 0))
    vec = pl.BlockSpec((1, d), lambda i: (0, 0))
    has_res = resid is not None

    def body(x_ref, w_ref, dy_ref, *rest):
        if has_res:
            r_ref, dx_ref, dw_ref = rest
        else:
            dx_ref, dw_ref = rest
        xv = x_ref[...]
        dyv = dy_ref[...].astype(F32)
        r = lax.rsqrt(jnp.mean(xv * xv, axis=-1, keepdims=True) + NORM_EPS)
        xhat = xv * r
        dyw = dyv * w_ref[...]
        dx = r * (dyw - xhat * jnp.mean(dyw * xhat, axis=-1, keepdims=True))
        if has_res:
            dx = dx + r_ref[...]
        dx_ref[...] = dx.astype(dx_ref.dtype)

        @pl.when(pl.program_id(0) == 0)
        def _():
            dw_ref[...] = jnp.zeros_like(dw_ref)

        dw_ref[...] += jnp.sum(dyv * xhat, axis=0, keepdims=True)

    ins = (x, w, dy) + ((resid,) if has_res else ())
    in_specs = [row, vec, row] + ([row] if has_res else [])
    return pl.pallas_call(
        body, grid=(t // tr,), in_specs=in_specs, out_specs=[row, vec],
        out_shape=[jax.ShapeDtypeStruct((t, d), out_dtype), jax.ShapeDtypeStruct((1, d), F32)],
        compiler_params=_params("arbitrary"), name=name)(*ins)


def _loss_head(h, target, *, name):
    t, d = h.shape
    tr = _row_tile(t, 512)
    row = pl.BlockSpec((tr, d), lambda i: (i, 0))

    def body(h_ref, t_ref, dh_ref, loss_ref):
        err = h_ref[...] - t_ref[...]
        dh_ref[...] = err * (1.0 / d)

        @pl.when(pl.program_id(0) == 0)
        def _():
            loss_ref[...] = jnp.zeros_like(loss_ref)

        part = jnp.sum(jnp.sum(err * err, axis=1, keepdims=True), axis=0, keepdims=True) * (0.5 / d)
        loss_ref[...] += jnp.broadcast_to(part, loss_ref.shape)

    return pl.pallas_call(
        body, grid=(t // tr,), in_specs=[row, row],
        out_specs=[row, pl.BlockSpec((8, LANES), lambda i: (0, 0))],
        out_shape=[jax.ShapeDtypeStruct((t, d), F32), jax.ShapeDtypeStruct((8, LANES), F32)],
        compiler_params=_params("arbitrary"), name=name)(h, target)


def _col_sum(x, *, name):
    t, n = x.shape
    tr = _row_tile(t, 512)

    def body(x_ref, o_ref):
        @pl.when(pl.program_id(0) == 0)
        def _():
            o_ref[...] = jnp.zeros_like(o_ref)

        o_ref[...] += jnp.sum(x_ref[...].astype(F32), axis=0, keepdims=True)

    return pl.pallas_call(
        body, grid=(t // tr,), in_specs=[pl.BlockSpec((tr, n), lambda i: (i, 0))],
        out_specs=pl.BlockSpec((1, n), lambda i: (0, 0)), out_shape=jax.ShapeDtypeStruct((1, n), F32),
        compiler_params=_params("arbitrary"), name=name)(x)


SSD_IN_SHARD = SSD_IN_DIM // N_CHIPS


def _w_in_from_shards(shards, *, name):
    d = shards.shape[1]
    tr = 256

    def body(s_ref, o_ref):
        o_ref[:, pl.ds(SSD_IN_PAD - LANES, LANES)] = jnp.zeros((tr, LANES), o_ref.dtype)
        for s in range(N_CHIPS):
            o_ref[:, pl.ds(SSD_IN_SHARD * s, SSD_IN_SHARD)] = s_ref[s]

    return pl.pallas_call(
        body, grid=(d // tr,), in_specs=[pl.BlockSpec((N_CHIPS, tr, SSD_IN_SHARD), lambda i: (0, i, 0))],
        out_specs=pl.BlockSpec((tr, SSD_IN_PAD), lambda i: (i, 0)),
        out_shape=jax.ShapeDtypeStruct((d, SSD_IN_PAD), shards.dtype),
        compiler_params=_params("parallel"), name=name)(shards)


def _w_in_to_shards(g, *, name):
    d = g.shape[0]
    tr = 256

    def body(g_ref, o_ref):
        for s in range(N_CHIPS):
            o_ref[s] = g_ref[:, pl.ds(SSD_IN_SHARD * s, SSD_IN_SHARD)].astype(o_ref.dtype)

    return pl.pallas_call(
        body, grid=(d // tr,), in_specs=[pl.BlockSpec((tr, SSD_IN_PAD), lambda i: (i, 0))],
        out_specs=pl.BlockSpec((N_CHIPS, tr, SSD_IN_SHARD), lambda i: (0, i, 0)),
        out_shape=jax.ShapeDtypeStruct((N_CHIPS, d, SSD_IN_SHARD), BF16),
        compiler_params=_params("parallel"), name=name)(g)


XBC_COL0 = SSD_D_INNER // LANES
DT_COL0 = (SSD_D_INNER + SSD_CONV_DIM) // LANES


def _shift_down(v, k, row_ids):
    return jnp.where(row_ids >= k, pltpu.roll(v, k, axis=0), 0.0)


def _shift_up(v, k, row_ids):
    n = v.shape[0]
    return jnp.where(row_ids < n - k, pltpu.roll(v, n - k, axis=0), 0.0)


def _conv_pre(x, w, b, row_ids):
    pre = b + w[3:4, :] * x
    for k in (1, 2, 3):
        pre = pre + w[3 - k:4 - k, :] * _shift_down(x, k, row_ids)
    return pre


def _conv_fwd(zx, conv_w, conv_b, *, name):
    t = zx.shape[0]
    nct = SSD_CONV_DIM // LANES

    def body(x_ref, w_ref, b_ref, o_ref):
        x = x_ref[...]
        row_ids = lax.broadcasted_iota(jnp.int32, x.shape, 0)
        pre = _conv_pre(x, w_ref[...], b_ref[...], row_ids)
        o_ref[...] = pre * _sigmoid(pre)

    return pl.pallas_call(
        body, grid=(nct,),
        in_specs=[pl.BlockSpec((t, LANES), lambda j: (0, XBC_COL0 + j)),
                  pl.BlockSpec((SSD_CONV_WIDTH, LANES), lambda j: (0, j)),
                  pl.BlockSpec((1, LANES), lambda j: (0, j))],
        out_specs=pl.BlockSpec((t, LANES), lambda j: (0, j)),
        out_shape=jax.ShapeDtypeStruct((t, SSD_CONV_DIM), F32),
        compiler_params=_params("parallel"), name=name)(zx, conv_w, conv_b)


def _conv_bwd(zx, conv_w, conv_b, d_xs, d_bm, d_cm, *, name):
    t = zx.shape[0]
    nct = SSD_CONV_DIM // LANES
    n_xs = SSD_D_INNER // LANES
    n_bm = SSD_N_GROUPS * SSD_D_STATE // LANES

    def body(x_ref, w_ref, b_ref, dxs_ref, dbm_ref, dcm_ref, dx_ref, dw_ref, db_ref):
        x = x_ref[...]
        w = w_ref[...]
        j = pl.program_id(0)
        dy = jnp.where(j < n_xs, dxs_ref[...], jnp.where(j < n_xs + n_bm, dbm_ref[...], dcm_ref[...]))
        row_ids = lax.broadcasted_iota(jnp.int32, x.shape, 0)
        pre = _conv_pre(x, w, b_ref[...], row_ids)
        sg = _sigmoid(pre)
        dpre = dy * (sg * (1.0 + pre * (1.0 - sg)))
        dx = w[3:4, :] * dpre
        for k in (1, 2, 3):
            dx = dx + w[3 - k:4 - k, :] * _shift_up(dpre, k, row_ids)
        dx_ref[...] = dx.astype(dx_ref.dtype)
        db_ref[...] = jnp.sum(dpre, axis=0, keepdims=True)
        dw_ref[3:4, :] = jnp.sum(dpre * x, axis=0, keepdims=True)
        for k in (1, 2, 3):
            dw_ref[3 - k:4 - k, :] = jnp.sum(dpre * _shift_down(x, k, row_ids), axis=0, keepdims=True)

    col = pl.BlockSpec((t, LANES), lambda j: (0, j))
    clip = lambda j, lo, n: jnp.clip(j - lo, 0, n - 1)
    return pl.pallas_call(
        body, grid=(nct,),
        in_specs=[pl.BlockSpec((t, LANES), lambda j: (0, XBC_COL0 + j)),
                  pl.BlockSpec((SSD_CONV_WIDTH, LANES), lambda j: (0, j)),
                  pl.BlockSpec((1, LANES), lambda j: (0, j)),
                  pl.BlockSpec((t, LANES), lambda j: (0, clip(j, 0, n_xs))),
                  pl.BlockSpec((t, LANES), lambda j: (0, clip(j, n_xs, n_bm))),
                  pl.BlockSpec((t, LANES), lambda j: (0, clip(j, n_xs + n_bm, n_bm)))],
        out_specs=[col, pl.BlockSpec((SSD_CONV_WIDTH, LANES), lambda j: (0, j)), pl.BlockSpec((1, LANES), lambda j: (0, j))],
        out_shape=[jax.ShapeDtypeStruct((t, SSD_CONV_DIM), BF16),
                   jax.ShapeDtypeStruct((SSD_CONV_WIDTH, SSD_CONV_DIM), F32),
                   jax.ShapeDtypeStruct((1, SSD_CONV_DIM), F32)],
        compiler_params=_params("parallel"), name=name)(zx, conv_w, conv_b, d_xs, d_bm, d_cm)


def _softplus_fwd(zx, bias_row, *, name):
    t = zx.shape[0]
    tr = _row_tile(t, 1024)

    def body(x_ref, b_ref, o_ref):
        v = x_ref[...] + b_ref[...]
        e = jnp.exp(-jnp.abs(v))
        u = 1.0 + e
        log1p = jnp.where(u == 1.0, e, jnp.log(u) * (e / (u - 1.0)))
        o_ref[...] = jnp.maximum(v, 0.0) + log1p

    return pl.pallas_call(
        body, grid=(t // tr,),
        in_specs=[pl.BlockSpec((tr, LANES), lambda i: (i, DT_COL0)), pl.BlockSpec((1, LANES), lambda i: (0, 0))],
        out_specs=pl.BlockSpec((tr, LANES), lambda i: (i, 0)),
        out_shape=jax.ShapeDtypeStruct((t, LANES), F32),
        compiler_params=_params("parallel"), name=name)(zx, bias_row)


def _softplus_bwd(zx, bias_row, ddt, *, name):
    t = zx.shape[0]
    tr = _row_tile(t, 1024)

    def body(x_ref, b_ref, g_ref, o_ref, db_ref):
        v = x_ref[...] + b_ref[...]
        lane = lax.broadcasted_iota(jnp.int32, v.shape, 1)
        d = jnp.where(lane < SSD_N_HEADS, g_ref[...] * _sigmoid(v), 0.0)
        o_ref[...] = d.astype(o_ref.dtype)

        @pl.when(pl.program_id(0) == 0)
        def _():
            db_ref[...] = jnp.zeros_like(db_ref)

        db_ref[...] += jnp.sum(d, axis=0, keepdims=True)

    return pl.pallas_call(
        body, grid=(t // tr,),
        in_specs=[pl.BlockSpec((tr, LANES), lambda i: (i, DT_COL0)), pl.BlockSpec((1, LANES), lambda i: (0, 0)),
                  pl.BlockSpec((tr, LANES), lambda i: (i, 0))],
        out_specs=[pl.BlockSpec((tr, LANES), lambda i: (i, 0)), pl.BlockSpec((1, LANES), lambda i: (0, 0))],
        out_shape=[jax.ShapeDtypeStruct((t, LANES), BF16), jax.ShapeDtypeStruct((1, LANES), F32)],
        compiler_params=_params("arbitrary"), name=name)(zx, bias_row, ddt)


def _ssd_masks():
    q = SSD_CHUNK
    tt = lax.broadcasted_iota(jnp.int32, (q, q), 0)
    ss = lax.broadcasted_iota(jnp.int32, (q, q), 1)
    lane = lax.broadcasted_iota(jnp.int32, (1, SSD_GW), 1)
    srow = lax.broadcasted_iota(jnp.int32, (SSD_GW, 1), 0)
    hm = [(lane >= SSD_HEAD_DIM * j) & (lane < SSD_HEAD_DIM * (j + 1)) for j in range(SSD_HPG)]
    rm = [(srow >= SSD_HEAD_DIM * j) & (srow < SSD_HEAD_DIM * (j + 1)) for j in range(SSD_HPG)]
    return tt, ss, hm, rm


def _ssd_head_terms(dtc, dtr, a_rows, j, tt, ss):
    q = SSD_CHUNK
    dt_col = dtc[:, j:j + 1]
    dt_row = dtr[j:j + 1, :]
    a_row1 = a_rows[j:j + 1, :]
    a_11 = a_rows[j:j + 1, 0:1]
    cum_col = jnp.sum(jnp.where(ss <= tt, dt_row * a_row1, 0.0), axis=1, keepdims=True)
    cum_row = jnp.sum(jnp.where(tt <= ss, dt_col * a_11, 0.0), axis=0, keepdims=True)
    decay = jnp.exp(jnp.where(ss <= tt, cum_col - cum_row, -jnp.inf))
    cum_last = cum_col[q - 1:q, :]
    e_col = jnp.exp(cum_col)
    dte_col = jnp.exp(cum_last - cum_col)
    e_last = jnp.exp(cum_last)
    return dt_col, dt_row, a_row1, a_11, decay, e_col, dte_col, e_last


def _ssd_group_specs(nc):
    xs = pl.BlockSpec((SSD_CHUNK, SSD_GW), lambda g, c: (c, g))
    bm = pl.BlockSpec((SSD_CHUNK, SSD_D_STATE), lambda g, c: (c, SSD_D_INNER // SSD_D_STATE + g))
    cm = pl.BlockSpec((SSD_CHUNK, SSD_D_STATE), lambda g, c: (c, SSD_D_INNER // SSD_D_STATE + SSD_N_GROUPS + g))
    dtc = pl.BlockSpec((None, SSD_CHUNK, SSD_HPG), lambda g, c: (g, c, 0))
    dtr = pl.BlockSpec((None, SSD_HPG, SSD_CHUNK), lambda g, c: (g, 0, c))
    par = pl.BlockSpec((None, SSD_HPG, LANES), lambda g, c: (g, 0, 0))
    st = pl.BlockSpec((None, None, SSD_GW, SSD_D_STATE), lambda g, c: (g, c, 0, 0))
    return xs, bm, cm, dtc, dtr, par, st


def _ssd_fwd(xc, dtc, dtr, alog_b, d_b, *, name):
    t = xc.shape[0]
    nc = t // SSD_CHUNK
    xs_s, bm_s, cm_s, dtc_s, dtr_s, par_s, st_s = _ssd_group_specs(nc)

    def body(x_ref, b_ref, c_ref, dtc_ref, dtr_ref, alog_ref, d_ref, y_ref, st_ref, s_scr):
        @pl.when(pl.program_id(1) == 0)
        def _():
            s_scr[...] = jnp.zeros_like(s_scr)

        tt, ss, hm, rm = _ssd_masks()
        x = x_ref[...]
        bm = b_ref[...].astype(BF16)
        cm = c_ref[...].astype(BF16)
        s_in = s_scr[...]
        st_ref[...] = s_in
        a_rows = -jnp.exp(alog_ref[...])
        d_rows = d_ref[...]
        dtc_v, dtr_v = dtc_ref[...], dtr_ref[...]
        xb = x.astype(BF16)
        g = _dot_nt(cm, bm)
        y = jnp.zeros(x.shape, F32)
        e_all = jnp.zeros(x.shape, F32)
        w_all = jnp.zeros(x.shape, F32)
        d_all = jnp.zeros((1, SSD_GW), F32)
        e_s = jnp.zeros((SSD_GW, 1), F32)
        for j in range(SSD_HPG):
            dt_col, dt_row, _, _, decay, e_col, dte_col, e_last = _ssd_head_terms(dtc_v, dtr_v, a_rows, j, tt, ss)
            m = g * decay * dt_row
            y = jnp.where(hm[j], _dot_nn(m.astype(BF16), xb), y)
            e_all = jnp.where(hm[j], e_col, e_all)
            w_all = jnp.where(hm[j], dt_col * dte_col, w_all)
            d_all = jnp.where(hm[j], d_rows[j:j + 1, 0:1], d_all)
            e_s = jnp.where(rm[j], e_last, e_s)
        y = y + _dot_nt(cm, s_in.astype(BF16)) * e_all + x * d_all
        y_ref[...] = y
        s_scr[...] = s_in * e_s + _dot_tn((x * w_all).astype(BF16), bm)

    return pl.pallas_call(
        body, grid=(SSD_N_GROUPS, nc),
        in_specs=[xs_s, bm_s, cm_s, dtc_s, dtr_s, par_s, par_s],
        out_specs=[pl.BlockSpec((SSD_CHUNK, SSD_GW), lambda g, c: (c, g)), st_s],
        out_shape=[jax.ShapeDtypeStruct((t, SSD_D_INNER), F32),
                   jax.ShapeDtypeStruct((SSD_N_GROUPS, nc, SSD_GW, SSD_D_STATE), F32)],
        scratch_shapes=[pltpu.VMEM((SSD_GW, SSD_D_STATE), F32)],
        compiler_params=_params("parallel", "arbitrary"), name=name)(xc, xc, xc, dtc, dtr, alog_b, d_b)


def _ssd_bwd(xc, dtc, dtr, alog_b, d_b, states, dy, *, name):
    t = xc.shape[0]
    nc = t // SSD_CHUNK
    q = SSD_CHUNK
    rev = lambda c: nc - 1 - c
    xs_s = pl.BlockSpec((q, SSD_GW), lambda g, c: (rev(c), g))
    bm_s = pl.BlockSpec((q, SSD_D_STATE), lambda g, c: (rev(c), SSD_D_INNER // SSD_D_STATE + g))
    cm_s = pl.BlockSpec((q, SSD_D_STATE), lambda g, c: (rev(c), SSD_D_INNER // SSD_D_STATE + SSD_N_GROUPS + g))
    dtc_s = pl.BlockSpec((None, q, SSD_HPG), lambda g, c: (g, rev(c), 0))
    dtr_s = pl.BlockSpec((None, SSD_HPG, q), lambda g, c: (g, 0, rev(c)))
    par_s = pl.BlockSpec((None, SSD_HPG, LANES), lambda g, c: (g, 0, 0))
    st_s = pl.BlockSpec((None, None, SSD_GW, SSD_D_STATE), lambda g, c: (g, rev(c), 0, 0))

    def body(x_ref, b_ref, c_ref, dtc_ref, dtr_ref, alog_ref, d_ref, st_ref, dy_ref,
             dx_ref, db_ref, dc_ref, ddt_ref, dpar_ref, ds_scr):
        @pl.when(pl.program_id(1) == 0)
        def _():
            ds_scr[...] = jnp.zeros_like(ds_scr)
            dpar_ref[...] = jnp.zeros_like(dpar_ref)

        tt, ss, hm, rm = _ssd_masks()
        tcol = lax.broadcasted_iota(jnp.int32, (q, 1), 0)
        lane = lax.broadcasted_iota(jnp.int32, (1, LANES), 1)
        x = x_ref[...]
        bm = b_ref[...].astype(BF16)
        cm = c_ref[...].astype(BF16)
        s_in = st_ref[...]
        ds = ds_scr[...]
        dyv = dy_ref[...]
        a_rows = -jnp.exp(alog_ref[...])
        d_rows = d_ref[...]
        dtc_v, dtr_v = dtc_ref[...], dtr_ref[...]
        xb = x.astype(BF16)
        dyb = dyv.astype(BF16)
        s_b = s_in.astype(BF16)
        ds_b = ds.astype(BF16)
        g = _dot_nt(cm, bm)
        cs = _dot_nt(cm, s_b)
        bds = _dot_nt(bm, ds_b)
        dy_cs = dyv * cs
        x_bds = x * bds
        dy_x = dyv * x
        ds_s = ds * s_in
        dg = jnp.zeros((q, q), F32)
        dx = jnp.zeros(x.shape, F32)
        e_all = jnp.zeros(x.shape, F32)
        w_all = jnp.zeros(x.shape, F32)
        d_all = jnp.zeros((1, SSD_GW), F32)
        e_s = jnp.zeros((SSD_GW, 1), F32)
        for j in range(SSD_HPG):
            dt_col, dt_row, a_row1, a_11, decay, e_col, dte_col, e_last = _ssd_head_terms(dtc_v, dtr_v, a_rows, j, tt, ss)
            dm = _dot_nt(jnp.where(hm[j], dyv, 0.0).astype(BF16), xb)
            gl = g * decay
            wp = dm * gl
            dg = dg + dm * decay * dt_row
            dx = jnp.where(hm[j], _dot_tn((gl * dt_row).astype(BF16), dyb), dx)
            w = wp * dt_row
            rw_col = jnp.sum(w, axis=1, keepdims=True)
            cw_row = jnp.sum(w, axis=0, keepdims=True)
            cwp_row = jnp.sum(wp, axis=0, keepdims=True)
            r1_col = jnp.sum(jnp.where(hm[j], dy_cs, 0.0), axis=1, keepdims=True) * e_col
            dw_col = jnp.sum(jnp.where(hm[j], x_bds, 0.0), axis=1, keepdims=True)
            w_col = dt_col * dte_col
            s_sum = jnp.sum(jnp.sum(jnp.where(rm[j], ds_s, 0.0), axis=1, keepdims=True), axis=0, keepdims=True)
            last_add = jnp.sum(dw_col * w_col, axis=0, keepdims=True) + e_last * s_sum
            dcum_col = rw_col + r1_col - dw_col * w_col + jnp.where(tcol == q - 1, last_add, 0.0)
            da_row = jnp.sum(jnp.where(tt >= ss, dcum_col, 0.0), axis=0, keepdims=True)
            da_col = jnp.sum(jnp.where(ss >= tt, -cw_row, 0.0), axis=1, keepdims=True)
            ddt_col = a_11 * da_col + dw_col * dte_col
            ddt_row = a_row1 * da_row + cwp_row + jnp.sum(jnp.where(tt == ss, ddt_col, 0.0), axis=0, keepdims=True)
            ddt_ref[j:j + 1, :] = ddt_row
            d_a = jnp.sum(dt_row * da_row, axis=1, keepdims=True) + jnp.sum(dt_col * da_col, axis=0, keepdims=True)
            d_d = jnp.sum(jnp.sum(jnp.where(hm[j], dy_x, 0.0), axis=1, keepdims=True), axis=0, keepdims=True)
            dpar_ref[j:j + 1, :] += jnp.where(lane == 0, d_a * a_11, 0.0) + jnp.where(lane == 1, d_d, 0.0)
            e_all = jnp.where(hm[j], e_col, e_all)
            w_all = jnp.where(hm[j], w_col, w_all)
            d_all = jnp.where(hm[j], d_rows[j:j + 1, 0:1], d_all)
            e_s = jnp.where(rm[j], e_last, e_s)
        dx_ref[...] = dx + w_all * bds + d_all * dyv
        dye = (dyv * e_all).astype(BF16)
        dgb = dg.astype(BF16)
        xw = (x * w_all).astype(BF16)
        dc_ref[...] = _dot_nn(dgb, bm) + _dot_nn(dye, s_b)
        db_ref[...] = _dot_tn(dgb, cm) + _dot_nn(xw, ds_b)
        ds_scr[...] = ds * e_s + _dot_tn(dye, cm)

    return pl.pallas_call(
        body, grid=(SSD_N_GROUPS, nc),
        in_specs=[xs_s, bm_s, cm_s, dtc_s, dtr_s, par_s, par_s, st_s, pl.BlockSpec((q, SSD_GW), lambda g, c: (rev(c), g))],
        out_specs=[pl.BlockSpec((q, SSD_GW), lambda g, c: (rev(c), g)),
                   pl.BlockSpec((q, SSD_D_STATE), lambda g, c: (rev(c), g)),
                   pl.BlockSpec((q, SSD_D_STATE), lambda g, c: (rev(c), g)),
                   pl.BlockSpec((None, SSD_HPG, q), lambda g, c: (g, 0, rev(c))),
                   pl.BlockSpec((None, SSD_HPG, LANES), lambda g, c: (g, 0, 0))],
        out_shape=[jax.ShapeDtypeStruct((t, SSD_D_INNER), F32),
                   jax.ShapeDtypeStruct((t, SSD_N_GROUPS * SSD_D_STATE), F32),
                   jax.ShapeDtypeStruct((t, SSD_N_GROUPS * SSD_D_STATE), F32),
                   jax.ShapeDtypeStruct((SSD_N_GROUPS, SSD_HPG, t), F32),
                   jax.ShapeDtypeStruct((SSD_N_GROUPS, SSD_HPG, LANES), F32)],
        scratch_shapes=[pltpu.VMEM((SSD_GW, SSD_D_STATE), F32)],
        compiler_params=_params("parallel", "arbitrary"), name=name)(xc, xc, xc, dtc, dtr, alog_b, d_b, states, dy)


def _gate_norm_fwd(y, zx, norm_w, *, name):
    t = y.shape[0]
    tr = _row_tile(t, 256)
    row = pl.BlockSpec((tr, SSD_D_INNER), lambda i: (i, 0))

    def body(y_ref, z_ref, w_ref, o_ref):
        for gi in range(SSD_N_GROUPS):
            sl = pl.ds(gi * SSD_GW, SSD_GW)
            z = z_ref[:, sl]
            gv = y_ref[:, sl] * (z * _sigmoid(z))
            r = lax.rsqrt(jnp.mean(gv * gv, axis=-1, keepdims=True) + NORM_EPS)
            o_ref[:, sl] = (gv * r * w_ref[:, sl]).astype(BF16)

    return pl.pallas_call(
        body, grid=(t // tr,), in_specs=[row, row, pl.BlockSpec((1, SSD_D_INNER), lambda i: (0, 0))],
        out_specs=row, out_shape=jax.ShapeDtypeStruct((t, SSD_D_INNER), BF16),
        compiler_params=_params("parallel"), name=name)(y, zx, norm_w)


def _gate_norm_bwd(y, zx, norm_w, dyn, *, name):
    t = y.shape[0]
    tr = _row_tile(t, 256)
    row = pl.BlockSpec((tr, SSD_D_INNER), lambda i: (i, 0))
    vec = pl.BlockSpec((1, SSD_D_INNER), lambda i: (0, 0))

    def body(y_ref, z_ref, w_ref, dyn_ref, dy_ref, dz_ref, dw_ref):
        @pl.when(pl.program_id(0) == 0)
        def _():
            dw_ref[...] = jnp.zeros_like(dw_ref)

        for gi in range(SSD_N_GROUPS):
            sl = pl.ds(gi * SSD_GW, SSD_GW)
            z = z_ref[:, sl]
            yv = y_ref[:, sl]
            sg = _sigmoid(z)
            sz = z * sg
            gv = yv * sz
            r = lax.rsqrt(jnp.mean(gv * gv, axis=-1, keepdims=True) + NORM_EPS)
            ghat = gv * r
            dout = dyn_ref[:, sl].astype(F32)
            dgh = dout * w_ref[:, sl]
            dgv = r * (dgh - ghat * jnp.mean(dgh * ghat, axis=-1, keepdims=True))
            dy_ref[:, sl] = dgv * sz
            dz_ref[:, sl] = (dgv * yv * (sg * (1.0 + z * (1.0 - sg)))).astype(dz_ref.dtype)
            dw_ref[:, sl] += jnp.sum(dout * ghat, axis=0, keepdims=True)

    return pl.pallas_call(
        body, grid=(t // tr,), in_specs=[row, row, vec, row], out_specs=[row, row, vec],
        out_shape=[jax.ShapeDtypeStruct((t, SSD_D_INNER), F32), jax.ShapeDtypeStruct((t, SSD_D_INNER), BF16),
                   jax.ShapeDtypeStruct((1, SSD_D_INNER), F32)],
        compiler_params=_params("arbitrary"), name=name)(y, zx, norm_w, dyn)


ATTN_KV_W = ATTN_N_KV * ATTN_HEAD_DIM
ATTN_Q_HALF = 512
ATTN_K_BLK = ATTN_N_Q * ATTN_HEAD_DIM // ATTN_KV_W
ATTN_V_BLK = ATTN_K_BLK + 1


def _attn_probs(qs, kb, sink_col, first_block):
    w = ATTN_WINDOW
    s = _dot_nt(qs, kb) * (ATTN_HEAD_DIM ** -0.5)
    qpos = lax.broadcasted_iota(jnp.int32, (w, 2 * w), 0) + w
    kpos = lax.broadcasted_iota(jnp.int32, (w, 2 * w), 1)
    rel = qpos - kpos
    valid = (rel >= 0) & (rel < w) & jnp.logical_not(first_block & (kpos < w))
    valid = jnp.concatenate([valid] * ATTN_REP, axis=0)
    s = jnp.where(valid, s, -jnp.inf)
    m = jnp.maximum(jnp.max(s, axis=1, keepdims=True), sink_col)
    e = jnp.exp(s - m)
    es = jnp.exp(sink_col - m)
    inv = 1.0 / (jnp.sum(e, axis=1, keepdims=True) + es)
    return e * inv, es * inv


def _attn_head_views(q_lo, q_hi, k_cur, k_prev, v_cur, v_prev, kh):
    hd = ATTN_HEAD_DIM
    q_half = q_lo if kh < 2 else q_hi
    base = (kh % 2) * ATTN_REP * hd
    qs = jnp.concatenate([q_half[:, base + r * hd: base + (r + 1) * hd] for r in range(ATTN_REP)], axis=0)
    ksl = slice(kh * hd, (kh + 1) * hd)
    kb = jnp.concatenate([k_prev[:, ksl], k_cur[:, ksl]], axis=0)
    vb = jnp.concatenate([v_prev[:, ksl], v_cur[:, ksl]], axis=0)
    return qs, kb, vb


def _sink_col(sink_ref, kh):
    rows = lax.broadcasted_iota(jnp.int32, (ATTN_REP * ATTN_WINDOW, 1), 0)
    col = jnp.zeros((ATTN_REP * ATTN_WINDOW, 1), F32)
    for r in range(ATTN_REP):
        h = kh * ATTN_REP + r
        col = jnp.where((rows >= r * ATTN_WINDOW) & (rows < (r + 1) * ATTN_WINDOW), sink_ref[h:h + 1, 0:1], col)
    return col


def _attn_fwd(qkv, sinks_b, *, name):
    t = qkv.shape[0]
    w = ATTN_WINDOW
    nb = t // w
    prev = lambda n: jnp.maximum(n - 1, 0)

    def body(qlo_ref, qhi_ref, kc_ref, kp_ref, vc_ref, vp_ref, sink_ref, o_ref):
        first = pl.program_id(0) == 0
        q_lo, q_hi = qlo_ref[...], qhi_ref[...]
        k_cur, k_prev, v_cur, v_prev = kc_ref[...], kp_ref[...], vc_ref[...], vp_ref[...]
        for kh in range(ATTN_N_KV):
            qs, kb, vb = _attn_head_views(q_lo, q_hi, k_cur, k_prev, v_cur, v_prev, kh)
            p, _ = _attn_probs(qs, kb, _sink_col(sink_ref, kh), first)
            o = _dot_nn(p.astype(BF16), vb)
            for r in range(ATTN_REP):
                h = kh * ATTN_REP + r
                o_ref[:, pl.ds(h * ATTN_HEAD_DIM, ATTN_HEAD_DIM)] = o[r * w:(r + 1) * w, :].astype(o_ref.dtype)

    qh = lambda half: pl.BlockSpec((w, ATTN_Q_HALF), lambda n: (n, half))
    kv = lambda blk, idx: pl.BlockSpec((w, ATTN_KV_W), lambda n: (idx(n), blk))
    cur = lambda n: n
    return pl.pallas_call(
        body, grid=(nb,),
        in_specs=[qh(0), qh(1), kv(ATTN_K_BLK, cur), kv(ATTN_K_BLK, prev), kv(ATTN_V_BLK, cur), kv(ATTN_V_BLK, prev),
                  pl.BlockSpec((ATTN_N_Q, LANES), lambda n: (0, 0))],
        out_specs=pl.BlockSpec((w, D_MODEL), lambda n: (n, 0)),
        out_shape=jax.ShapeDtypeStruct((t, D_MODEL), BF16),
        compiler_params=_params("parallel"), name=name)(qkv, qkv, qkv, qkv, qkv, qkv, sinks_b)


def _attn_bwd(qkv, sinks_b, dout, *, name):
    t = qkv.shape[0]
    w = ATTN_WINDOW
    nb = t // w
    hd = ATTN_HEAD_DIM
    clamp = lambda n: jnp.minimum(n, nb - 1)
    prev = lambda n: jnp.maximum(clamp(n) - 1, 0)

    def body(qlo_ref, qhi_ref, kc_ref, kp_ref, vc_ref, vp_ref, sink_ref, dolo_ref, dohi_ref,
             dq_ref, dkv_ref, dsink_ref, carry):
        n = pl.program_id(0)

        @pl.when(n == 0)
        def _():
            carry[...] = jnp.zeros_like(carry)
            dsink_ref[...] = jnp.zeros_like(dsink_ref)

        @pl.when(n < nb)
        def _():
            first = n == 0
            q_lo, q_hi = qlo_ref[...], qhi_ref[...]
            do_lo, do_hi = dolo_ref[...], dohi_ref[...]
            k_cur, k_prev, v_cur, v_prev = kc_ref[...], kp_ref[...], vc_ref[...], vp_ref[...]
            rows = lax.broadcasted_iota(jnp.int32, (ATTN_REP * w, 1), 0)
            for kh in range(ATTN_N_KV):
                qs, kb, vb = _attn_head_views(q_lo, q_hi, k_cur, k_prev, v_cur, v_prev, kh)
                do_half = do_lo if kh < 2 else do_hi
                base = (kh % 2) * ATTN_REP * hd
                dos = jnp.concatenate([do_half[:, base + r * hd: base + (r + 1) * hd] for r in range(ATTN_REP)], axis=0)
                p, p_sink = _attn_probs(qs, kb, _sink_col(sink_ref, kh), first)
                dp = _dot_nt(dos, vb)
                delta = jnp.sum(p * dp, axis=1, keepdims=True)
                dsc = (p * (dp - delta) * (hd ** -0.5)).astype(BF16)
                dqs = _dot_nn(dsc, kb)
                dkb = _dot_tn(dsc, qs)
                dvb = _dot_tn(p.astype(BF16), dos)
                sink_g = -p_sink * delta
                for r in range(ATTN_REP):
                    h = kh * ATTN_REP + r
                    dq_ref[:, pl.ds(h * hd, hd)] = dqs[r * w:(r + 1) * w, :].astype(dq_ref.dtype)
                    in_r = (rows >= r * w) & (rows < (r + 1) * w)
                    dsink_ref[h:h + 1, :] += jnp.broadcast_to(
                        jnp.sum(jnp.where(in_r, sink_g, 0.0), axis=0, keepdims=True), (1, LANES))
                kcol = pl.ds(kh * hd, hd)
                vcol = pl.ds(ATTN_KV_W + kh * hd, hd)
                dkv_ref[:, kcol] = (carry[:, kcol] + dkb[0:w, :]).astype(dkv_ref.dtype)
                dkv_ref[:, vcol] = (carry[:, vcol] + dvb[0:w, :]).astype(dkv_ref.dtype)
                carry[:, kcol] = dkb[w:2 * w, :]
                carry[:, vcol] = dvb[w:2 * w, :]

        @pl.when(n == nb)
        def _():
            dkv_ref[...] = carry[...].astype(dkv_ref.dtype)

    qh = lambda half: pl.BlockSpec((w, ATTN_Q_HALF), lambda n: (clamp(n), half))
    kv = lambda blk, idx: pl.BlockSpec((w, ATTN_KV_W), lambda n: (idx(n), blk))
    return pl.pallas_call(
        body, grid=(nb + 1,),
        in_specs=[qh(0), qh(1), kv(ATTN_K_BLK, clamp), kv(ATTN_K_BLK, prev), kv(ATTN_V_BLK, clamp), kv(ATTN_V_BLK, prev),
                  pl.BlockSpec((ATTN_N_Q, LANES), lambda n: (0, 0)), qh(0), qh(1)],
        out_specs=[pl.BlockSpec((w, D_MODEL), lambda n: (clamp(n), 0)),
                   pl.BlockSpec((w, 2 * ATTN_KV_W), lambda n: (jnp.maximum(n - 1, 0), 0)),
                   pl.BlockSpec((ATTN_N_Q, LANES), lambda n: (0, 0))],
        out_shape=[jax.ShapeDtypeStruct((t, D_MODEL), BF16), jax.ShapeDtypeStruct((t, 2 * ATTN_KV_W), BF16),
                   jax.ShapeDtypeStruct((ATTN_N_Q, LANES), F32)],
        scratch_shapes=[pltpu.VMEM((w, 2 * ATTN_KV_W), F32)],
        compiler_params=_params("arbitrary"), name=name)(qkv, qkv, qkv, qkv, qkv, qkv, sinks_b, dout, dout)


def _sq_relu_epilogue(acc):
    r = jnp.maximum(acc, 0.0)
    return acc, r * r


def _sq_relu_bwd_epilogue(acc, pre):
    return (acc * (2.0 * jnp.maximum(pre, 0.0)),)


def _bias_epilogue(acc, bias):
    return (acc + bias,)


def _mlp_fwd(u, w_up, w_down, tag):
    pre, act = _matmul(u, w_up, mode="nn", out_dtypes=(F32, BF16), epilogue=_sq_relu_epilogue, b_shards=True,
                       name=f"mlp_up_{tag}")
    f = _matmul(act, w_down, mode="nn", out_dtypes=(F32,), name=f"mlp_down_{tag}")
    return pre, act, f


def _mlp_bwd(u, pre, act, w_up, w_down, df, tag):
    dpre = _matmul(df, w_down, mode="nt", out_dtypes=(BF16,), epilogue=_sq_relu_bwd_epilogue,
                   extras=((pre, "tile"),), name=f"mlp_dact_{tag}")
    dw_down = _matmul(act, df, mode="tn", out_dtypes=(BF16,), name=f"mlp_dwdown_{tag}")
    du = _matmul(dpre, w_up, mode="nt", out_dtypes=(F32,), b_shards=True, name=f"mlp_du_{tag}")
    dw_up = _matmul(u, dpre, mode="tn", out_dtypes=(BF16,), out_shards=True, name=f"mlp_dwup_{tag}")
    return du, dw_up, dw_down


def _group_dt_layouts(dt):
    t = dt.shape[0]
    d = dt[:, :SSD_N_HEADS].reshape(t, SSD_N_GROUPS, SSD_HPG)
    return jnp.transpose(d, (1, 0, 2)), jnp.transpose(d, (1, 2, 0))


def _head_param_rows(p):
    return jnp.broadcast_to(p.reshape(SSD_N_GROUPS, SSD_HPG, 1), (SSD_N_GROUPS, SSD_HPG, LANES))


def _local_step(x, target, wts):
    t = x.shape[0]
    row = lambda v: v.reshape(1, -1)
    mix_pre, mix_post, ffn_pre, ffn_post = wts["mix_pre_norm"], wts["mix_post_norm"], wts["ffn_pre_norm"], wts["ffn_post_norm"]

    u0 = _rms_fwd(x, row(mix_pre[0]), name="rms_pre_mix0")
    zx = _matmul(u0, wts["ssd_w_in"], mode="nn", out_dtypes=(F32,), tn=896, name="ssd_in_proj")
    xc = _conv_fwd(zx, wts["ssd_conv_w"], row(wts["ssd_conv_b"]), name="ssd_conv_fwd")
    bias_row = jnp.pad(wts["ssd_dt_bias"], (0, LANES - SSD_N_HEADS)).reshape(1, LANES)
    dt = _softplus_fwd(zx, bias_row, name="ssd_dt_fwd")
    dtc, dtr = _group_dt_layouts(dt)
    alog_b, d_b = _head_param_rows(wts["ssd_a_log"]), _head_param_rows(wts["ssd_d"])
    y_ssd, states = _ssd_fwd(xc, dtc, dtr, alog_b, d_b, name="ssd_scan_fwd")
    norm_w = row(wts["ssd_norm_w"])
    yn = _gate_norm_fwd(y_ssd, zx, norm_w, name="ssd_gate_norm_fwd")
    mix0 = _matmul(yn, wts["ssd_w_out"], mode="nn", out_dtypes=(F32,), name="ssd_out_proj")
    h1, v0 = _rms_fwd(mix0, row(mix_post[0]), resid=x, want_u=row(ffn_pre[0]), name="rms_post_mix0")
    pre0, act0, f0 = _mlp_fwd(v0, wts["mlp_w_up"][0], wts["mlp_w_down"][0], "l0")
    h2, u1 = _rms_fwd(f0, row(ffn_post[0]), resid=h1, want_u=row(mix_pre[1]), name="rms_post_ffn0")

    qkv = _matmul(u1, wts["attn_w_qkv"], mode="nn", out_dtypes=(BF16,), epilogue=_bias_epilogue,
                  extras=((row(wts["attn_b_qkv"]), "row"),), b_shards=True, name="attn_qkv_proj")
    sinks_b = jnp.broadcast_to(wts["attn_sinks"].reshape(ATTN_N_Q, 1), (ATTN_N_Q, LANES))
    ao = _attn_fwd(qkv, sinks_b, name="attn_fwd")
    mix1 = _matmul(ao, wts["attn_w_o"], mode="nn", out_dtypes=(F32,), epilogue=_bias_epilogue,
                   extras=((row(wts["attn_b_o"]), "row"),), name="attn_out_proj")
    h3, v1 = _rms_fwd(mix1, row(mix_post[1]), resid=h2, want_u=row(ffn_pre[1]), name="rms_post_mix1")
    pre1, act1, f1 = _mlp_fwd(v1, wts["mlp_w_up"][1], wts["mlp_w_down"][1], "l1")
    h4 = _rms_fwd(f1, row(ffn_post[1]), resid=h3, name="rms_post_ffn1")

    dh4, loss_tile = _loss_head(h4, target, name="loss_head")

    df1, g_ffn_post1 = _rms_bwd(f1, row(ffn_post[1]), dh4, out_dtype=BF16, name="rms_post_ffn1_bwd")
    dv1, g_up1, g_down1 = _mlp_bwd(v1, pre1, act1, wts["mlp_w_up"][1], wts["mlp_w_down"][1], df1, "l1")
    dh3, g_ffn_pre1 = _rms_bwd(h3, row(ffn_pre[1]), dv1, resid=dh4, name="rms_pre_ffn1_bwd")
    dmix1, g_mix_post1 = _rms_bwd(mix1, row(mix_post[1]), dh3, out_dtype=BF16, name="rms_post_mix1_bwd")
    g_b_o = _col_sum(dmix1, name="attn_bo_grad")
    g_w_o = _matmul(ao, dmix1, mode="tn", out_dtypes=(BF16,), name="attn_dwo")
    dao = _matmul(dmix1, wts["attn_w_o"], mode="nt", out_dtypes=(BF16,), name="attn_dao")
    dq, dkv, g_sinks = _attn_bwd(qkv, sinks_b, dao, name="attn_bwd")
    dqkv = jnp.concatenate([dq, dkv], axis=1)
    g_b_qkv = _col_sum(dqkv, name="attn_bqkv_grad")
    g_w_qkv = _matmul(u1, dqkv, mode="tn", out_dtypes=(BF16,), tn=ATTN_QKV // N_CHIPS, out_shards=True, name="attn_dwqkv")
    du1 = _matmul(dqkv, wts["attn_w_qkv"], mode="nt", out_dtypes=(F32,), b_shards=True, name="attn_du")
    dh2, g_mix_pre1 = _rms_bwd(h2, row(mix_pre[1]), du1, resid=dh3, name="rms_pre_mix1_bwd")

    df0, g_ffn_post0 = _rms_bwd(f0, row(ffn_post[0]), dh2, out_dtype=BF16, name="rms_post_ffn0_bwd")
    dv0, g_up0, g_down0 = _mlp_bwd(v0, pre0, act0, wts["mlp_w_up"][0], wts["mlp_w_down"][0], df0, "l0")
    dh1, g_ffn_pre0 = _rms_bwd(h1, row(ffn_pre[0]), dv0, resid=dh2, name="rms_pre_ffn0_bwd")
    dmix0, g_mix_post0 = _rms_bwd(mix0, row(mix_post[0]), dh1, out_dtype=BF16, name="rms_post_mix0_bwd")
    g_w_out = _matmul(yn, dmix0, mode="tn", out_dtypes=(BF16,), name="ssd_dwout")
    dyn = _matmul(dmix0, wts["ssd_w_out"], mode="nt", out_dtypes=(BF16,), name="ssd_dyn")
    dy_ssd, dz, g_norm_w = _gate_norm_bwd(y_ssd, zx, norm_w, dyn, name="ssd_gate_norm_bwd")
    dxc, dbm, dcm, ddt_r, dpar = _ssd_bwd(xc, dtc, dtr, alog_b, d_b, states, dy_ssd, name="ssd_scan_bwd")
    dxbc, g_conv_w, g_conv_b = _conv_bwd(zx, wts["ssd_conv_w"], row(wts["ssd_conv_b"]), dxc, dbm, dcm, name="ssd_conv_bwd")
    ddt = jnp.pad(jnp.transpose(ddt_r, (2, 0, 1)).reshape(t, SSD_N_HEADS), ((0, 0), (0, LANES - SSD_N_HEADS)))
    ddt_raw, g_dt_bias = _softplus_bwd(zx, bias_row, ddt, name="ssd_dt_bwd")
    dzx = jnp.concatenate([dz, dxbc, ddt_raw], axis=1)
    g_w_in = _w_in_to_shards(_matmul(u0, dzx, mode="tn", out_dtypes=(F32,), tn=896, name="ssd_dwin"), name="ssd_dwin_shards")
    du0 = _matmul(dzx, wts["ssd_w_in"], mode="nt", out_dtypes=(F32,), tk=896, name="ssd_du")
    grad_x, g_mix_pre0 = _rms_bwd(x, row(mix_pre[0]), du0, resid=dh1, name="rms_pre_mix0_bwd")

    dpar = dpar.reshape(SSD_N_HEADS, LANES)
    mats = {"ssd_w_in": g_w_in, "ssd_w_out": g_w_out, "attn_w_qkv": g_w_qkv, "attn_w_o": g_w_o,
            "mlp_w_up": (g_up0, g_up1), "mlp_w_down": (g_down0, g_down1)}
    vecs = {
        "ssd_conv_w": g_conv_w, "ssd_conv_b": g_conv_b.reshape(-1),
        "ssd_dt_bias": g_dt_bias[0, :SSD_N_HEADS], "ssd_a_log": dpar[:, 0], "ssd_d": dpar[:, 1],
        "ssd_norm_w": g_norm_w.reshape(-1), "attn_b_qkv": g_b_qkv.reshape(-1), "attn_sinks": g_sinks[:, 0],
        "attn_b_o": g_b_o.reshape(-1),
        "mix_pre_norm": jnp.concatenate([g_mix_pre0, g_mix_pre1]), "mix_post_norm": jnp.concatenate([g_mix_post0, g_mix_post1]),
        "ffn_pre_norm": jnp.concatenate([g_ffn_pre0, g_ffn_pre1]), "ffn_post_norm": jnp.concatenate([g_ffn_post0, g_ffn_post1]),
    }
    return loss_tile, grad_x, mats, vecs


ANY = pl.BlockSpec(memory_space=pl.ANY)


def _mesh_position():
    return lax.axis_index("x"), lax.axis_index("y"), lax.axis_index("c")


def _flip(v, bit):
    return 1 - v if bit else v


OTHER_CHIPS = ((1, 0), (0, 1), (1, 1))


def _all_gather_chips(mats, vecs, *, name):
    nm, nv = len(mats), len(vecs)
    n = nm + nv
    n_ici = (N_CHIPS - 1) * n
    n_fwd = (N_CHIPS - 1) * nm

    def body(*refs):
        ins, outs = refs[:n], refs[n:2 * n]
        ici_send, ici_recv, fwd_send, fwd_recv, loc_sems = refs[2 * n:]
        xi, yi, ci = _mesh_position()
        me = 2 * xi + yi
        local = [pltpu.make_async_copy(ins[i], outs[i].at[me], loc_sems.at[i]) for i in range(n)]
        sends, landed, forwards, from_sibling = [], [], [], []
        for j, (bx, by) in enumerate(OTHER_CHIPS):
            px, py = _flip(xi, bx), _flip(yi, by)
            peer = 2 * px + py
            for i in range(n):
                k = j * n + i
                is_mat = i < nm
                mk = functools.partial(pltpu.make_async_remote_copy, send_sem=ici_send.at[k], recv_sem=ici_recv.at[k],
                                       device_id=(px, py, ci), device_id_type=MESH)
                if is_mat:
                    sends.append(mk(src_ref=ins[i].at[ci], dst_ref=outs[i].at[me, ci]))
                    landed.append(mk(src_ref=ins[i].at[ci], dst_ref=outs[i].at[peer, ci]))
                    kf = j * nm + i
                    fw = functools.partial(pltpu.make_async_remote_copy, send_sem=fwd_send.at[kf], recv_sem=fwd_recv.at[kf],
                                           device_id=(xi, yi, 1 - ci), device_id_type=MESH)
                    forwards.append(fw(src_ref=outs[i].at[peer, ci], dst_ref=outs[i].at[peer, ci]))
                    from_sibling.append(fw(src_ref=outs[i].at[peer, ci], dst_ref=outs[i].at[peer, 1 - ci]))
                else:
                    sends.append(mk(src_ref=ins[i], dst_ref=outs[i].at[me]))
                    landed.append(mk(src_ref=ins[i], dst_ref=outs[i].at[peer]))
                    forwards.append(None)
        for cp in local + sends:
            cp.start()
        for cp, fw in zip(landed, forwards):
            cp.wait_recv()
            if fw is not None:
                fw.start()
        for cp in from_sibling:
            cp.wait_recv()
        for cp in sends + [fw for fw in forwards if fw is not None]:
            cp.wait_send()
        for cp in local:
            cp.wait()

    arrs = list(mats) + list(vecs)
    return pl.pallas_call(
        body, in_specs=[ANY] * n, out_specs=[ANY] * n,
        out_shape=[jax.ShapeDtypeStruct((N_CHIPS,) + a.shape, a.dtype) for a in arrs],
        scratch_shapes=[pltpu.SemaphoreType.DMA((n_ici,)), pltpu.SemaphoreType.DMA((n_ici,)),
                        pltpu.SemaphoreType.DMA((n_fwd,)), pltpu.SemaphoreType.DMA((n_fwd,)),
                        pltpu.SemaphoreType.DMA((n,))],
        name=name)(*arrs)


def _send_other_half(parts, *, name):
    n = len(parts)

    def body(*refs):
        ins, outs = refs[:n], refs[n:2 * n]
        send_sems, recv_sems = refs[2 * n:]
        xi, yi, ci = _mesh_position()
        sibling = (xi, yi, 1 - ci)
        for i in range(n):
            for s in range(N_CHIPS):
                pltpu.make_async_remote_copy(src_ref=ins[i].at[s, 1 - ci], dst_ref=outs[i].at[s], send_sem=send_sems.at[i],
                                             recv_sem=recv_sems.at[i], device_id=sibling, device_id_type=MESH).start()
        for i in range(n):
            pltpu.make_async_remote_copy(src_ref=outs[i], dst_ref=outs[i], send_sem=send_sems.at[i], recv_sem=recv_sems.at[i],
                                         device_id=sibling, device_id_type=MESH).wait()

    return pl.pallas_call(
        body, in_specs=[ANY] * n, out_specs=[ANY] * n,
        out_shape=[jax.ShapeDtypeStruct((p.shape[0],) + p.shape[2:], p.dtype) for p in parts],
        scratch_shapes=[pltpu.SemaphoreType.DMA((n,)), pltpu.SemaphoreType.DMA((n,))],
        name=name)(*parts)


ROW_BLOCKS = 8


def _add_sibling_half(parts, theirs, core, *, name):
    n = len(parts)

    def body(core_ref, *refs):
        for a_ref, b_ref, o_ref in zip(refs[:n], refs[n:2 * n], refs[2 * n:]):
            o_ref[...] = (a_ref[...].astype(F32) + b_ref[...].astype(F32)).astype(o_ref.dtype)

    mine = lambda p: pl.BlockSpec((None, None, p.shape[2] // ROW_BLOCKS, p.shape[3]), lambda s, rb, core_ref: (s, core_ref[0], rb, 0))
    other = lambda p: pl.BlockSpec((None, p.shape[1] // ROW_BLOCKS, p.shape[2]), lambda s, rb, core_ref: (s, rb, 0))
    return pl.pallas_call(
        body,
        grid_spec=pltpu.PrefetchScalarGridSpec(
            num_scalar_prefetch=1, grid=(N_CHIPS, ROW_BLOCKS),
            in_specs=[mine(p) for p in parts] + [other(q) for q in theirs], out_specs=[other(q) for q in theirs]),
        out_shape=[jax.ShapeDtypeStruct(q.shape, BF16) for q in theirs],
        compiler_params=_params("parallel", "parallel"), name=name)(core, *parts, *theirs)


def _grad_exchange(parts, small, *, name):
    n = len(parts)
    n_ici = (N_CHIPS - 1) * n
    n_peer = N_DEV - 1

    def body(*refs):
        ins, small_ref = refs[:n], refs[n]
        outs, small_all_ref = refs[n + 1:2 * n + 1], refs[2 * n + 1]
        send_sems, recv_sems, small_send, small_recv, loc_sems = refs[2 * n + 2:]
        xi, yi, ci = _mesh_position()
        me_chip = 2 * xi + yi
        me = 4 * xi + 2 * yi + ci
        local = [pltpu.make_async_copy(ins[i].at[me_chip], outs[i].at[me_chip], loc_sems.at[i]) for i in range(n)]
        local.append(pltpu.make_async_copy(small_ref, small_all_ref.at[me], loc_sems.at[n]))
        sends, recvs = [], []
        for j, (bx, by) in enumerate(OTHER_CHIPS):
            px, py = _flip(xi, bx), _flip(yi, by)
            peer = 2 * px + py
            for i in range(n):
                k = j * n + i
                mk = functools.partial(pltpu.make_async_remote_copy, src_ref=ins[i].at[peer], send_sem=send_sems.at[k],
                                       recv_sem=recv_sems.at[k], device_id=(px, py, ci), device_id_type=MESH)
                sends.append(mk(dst_ref=outs[i].at[me_chip]))
                recvs.append(mk(dst_ref=outs[i].at[peer]))
        for k in range(1, N_DEV):
            px, py, pc = _flip(xi, (k >> 2) & 1), _flip(yi, (k >> 1) & 1), _flip(ci, k & 1)
            mk = functools.partial(pltpu.make_async_remote_copy, src_ref=small_ref, send_sem=small_send.at[k - 1],
                                   recv_sem=small_recv.at[k - 1], device_id=(px, py, pc), device_id_type=MESH)
            sends.append(mk(dst_ref=small_all_ref.at[me]))
            recvs.append(mk(dst_ref=small_all_ref.at[4 * px + 2 * py + pc]))
        for cp in local + sends:
            cp.start()
        for cp in recvs:
            cp.wait_recv()
        for cp in sends:
            cp.wait_send()
        for cp in local:
            cp.wait()

    return pl.pallas_call(
        body, in_specs=[ANY] * (n + 1), out_specs=[ANY] * (n + 1),
        out_shape=[jax.ShapeDtypeStruct(p.shape, p.dtype) for p in parts]
        + [jax.ShapeDtypeStruct((N_DEV,) + small.shape, small.dtype)],
        scratch_shapes=[pltpu.SemaphoreType.DMA((n_ici,)), pltpu.SemaphoreType.DMA((n_ici,)),
                        pltpu.SemaphoreType.DMA((n_peer,)), pltpu.SemaphoreType.DMA((n_peer,)),
                        pltpu.SemaphoreType.DMA((n + 1,))],
        name=name)(*parts, small)


def _sum_chips(parts, *, name):
    n = len(parts)
    p = parts[0].shape[0]

    def body(*refs):
        s = pl.program_id(1)
        for x_ref, o_ref in zip(refs[:n], refs[n:]):
            @pl.when(s == 0)
            def _():
                o_ref[...] = x_ref[...].astype(F32)

            @pl.when(s > 0)
            def _():
                o_ref[...] += x_ref[...].astype(F32)

    blocks = lambda q: ROW_BLOCKS if q.shape[1] % (8 * ROW_BLOCKS) == 0 else 1
    assert len({blocks(q) for q in parts}) == 1
    nb = blocks(parts[0])
    return pl.pallas_call(
        body, grid=(nb, p),
        in_specs=[pl.BlockSpec((None, q.shape[1] // nb, q.shape[2]), lambda rb, s: (s, rb, 0)) for q in parts],
        out_specs=[pl.BlockSpec((q.shape[1] // nb, q.shape[2]), lambda rb, s: (rb, 0)) for q in parts],
        out_shape=[jax.ShapeDtypeStruct(q.shape[1:], F32) for q in parts],
        compiler_params=_params("parallel", "arbitrary"), name=name)(*parts)


def _swap_halves(halves, layers, *, name):
    n = len(halves)
    out_shapes, slots = [], []
    for i, h in enumerate(halves):
        pair = [p for p in layers if i in p]
        if pair and pair[0][1] == i:
            slots.append((slots[pair[0][0]][0], 1))
        elif pair:
            out_shapes.append(jax.ShapeDtypeStruct((2, 2) + h.shape, h.dtype))
            slots.append((len(out_shapes) - 1, 0))
        else:
            out_shapes.append(jax.ShapeDtypeStruct((2,) + h.shape, h.dtype))
            slots.append((len(out_shapes) - 1, None))
    n_out = len(out_shapes)

    def body(*refs):
        ins, outs = refs[:n], refs[n:n + n_out]
        send_sems, recv_sems, loc_sems = refs[n + n_out:]
        xi, yi, ci = _mesh_position()
        local, sends, recvs = [], [], []
        for i in range(n):
            o, layer = slots[i]
            dst = (lambda core: outs[o].at[core]) if layer is None else (lambda core: outs[o].at[layer, core])
            local.append(pltpu.make_async_copy(ins[i], dst(ci), loc_sems.at[i]))
            mk = functools.partial(pltpu.make_async_remote_copy, src_ref=ins[i], send_sem=send_sems.at[i],
                                   recv_sem=recv_sems.at[i], device_id=(xi, yi, 1 - ci), device_id_type=MESH)
            sends.append(mk(dst_ref=dst(ci)))
            recvs.append(mk(dst_ref=dst(1 - ci)))
        for cp in local + sends:
            cp.start()
        for cp in recvs:
            cp.wait_recv()
        for cp in sends:
            cp.wait_send()
        for cp in local:
            cp.wait()

    return pl.pallas_call(
        body, in_specs=[ANY] * n, out_specs=[ANY] * n_out, out_shape=out_shapes,
        scratch_shapes=[pltpu.SemaphoreType.DMA((n,)), pltpu.SemaphoreType.DMA((n,)), pltpu.SemaphoreType.DMA((n,))],
        name=name)(*halves)


def _adamw(w, g, m, v, *, name):
    r, c = w.shape
    tr = 256 if r % 256 == 0 else r
    blk = pl.BlockSpec((tr, c), lambda i: (i, 0))

    def body(w_ref, g_ref, m_ref, v_ref, d_ref, nm_ref, nv_ref):
        gv = g_ref[...]
        nm = ADAM_B1 * m_ref[...] + (1.0 - ADAM_B1) * gv
        nv = ADAM_B2 * v_ref[...] + (1.0 - ADAM_B2) * (gv * gv)
        m_hat = nm / (1.0 - ADAM_B1 ** ADAM_STEP)
        v_hat = nv / (1.0 - ADAM_B2 ** ADAM_STEP)
        d_ref[...] = -ADAM_LR * (m_hat / (jnp.sqrt(v_hat) + ADAM_EPS) + ADAM_WD * w_ref[...])
        nm_ref[...] = nm
        nv_ref[...] = nv

    sh = jax.ShapeDtypeStruct((r, c), F32)
    return pl.pallas_call(body, grid=(r // tr,), in_specs=[blk] * 4, out_specs=[blk] * 3, out_shape=[sh] * 3,
                          compiler_params=_params("parallel"), name=name)(w, g, m, v)


SM_CONV_B, SM_NORM_W, SM_MIX_PRE, SM_MIX_POST, SM_FFN_PRE, SM_FFN_POST, SM_MISC, SM_CONV_W, SM_B_QKV, SM_B_O = 0, 4, 6, 8, 10, 12, 14, 16, 32, 34
SM_ROWS = 40
MISC_DT_BIAS, MISC_A_LOG, MISC_D, MISC_SINKS, MISC_LOSS = 0, 32, 64, 96, 112


def _shard_halves(a):
    c = a.shape[-1]
    return a.reshape(N_CHIPS, 2, -1, c)


def _rows(v):
    return v.reshape(-1, D_MODEL)


def _misc_row(dt_bias, a_log, d, sinks, loss):
    pad = jnp.zeros((D_MODEL - MISC_LOSS - 1,), F32)
    return jnp.concatenate([dt_bias.reshape(-1), a_log.reshape(-1), d.reshape(-1), sinks.reshape(-1), loss.reshape(1), pad]).reshape(1, D_MODEL)


def _replicated_rows(p, loss):
    return jnp.concatenate([
        _rows(p["ssd_conv_b"]), _rows(p["ssd_norm_w"]), _rows(p["mix_pre_norm"]), _rows(p["mix_post_norm"]),
        _rows(p["ffn_pre_norm"]), _rows(p["ffn_post_norm"]),
        _misc_row(p["ssd_dt_bias"], p["ssd_a_log"], p["ssd_d"], p["attn_sinks"], loss), jnp.zeros((1, D_MODEL), F32)], axis=0)


def _sharded_rows(conv_w, b_qkv, b_o):
    last = jnp.concatenate([b_qkv.reshape(-1), b_o.reshape(-1), jnp.zeros((D_MODEL - 640,), F32)]).reshape(1, D_MODEL)
    return jnp.concatenate([conv_w.reshape(SSD_CONV_WIDTH, D_MODEL), last, jnp.zeros((3, D_MODEL), F32)], axis=0)


REPLICATED = ("ssd_conv_b", "ssd_dt_bias", "ssd_a_log", "ssd_d", "ssd_norm_w", "attn_sinks",
              "mix_pre_norm", "mix_post_norm", "ffn_pre_norm", "ffn_post_norm")
MATRICES = ("ssd_w_in", "ssd_w_out", "attn_w_qkv", "attn_w_o", "mlp_w_up", "mlp_w_down")
WEIGHT_NAMES = ("ssd_w_in", "ssd_conv_w", "ssd_conv_b", "ssd_dt_bias", "ssd_a_log", "ssd_d", "ssd_norm_w", "ssd_w_out",
                "attn_w_qkv", "attn_b_qkv", "attn_sinks", "attn_w_o", "attn_b_o", "mlp_w_up", "mlp_w_down",
                "mix_pre_norm", "mix_post_norm", "ffn_pre_norm", "ffn_post_norm")


def _unpack_small(rows16, rows8, like):
    misc = rows16[SM_MISC]
    out = {
        "ssd_conv_b": rows16[SM_CONV_B:SM_CONV_B + 4], "ssd_norm_w": rows16[SM_NORM_W:SM_NORM_W + 2],
        "mix_pre_norm": rows16[SM_MIX_PRE:SM_MIX_PRE + 2], "mix_post_norm": rows16[SM_MIX_POST:SM_MIX_POST + 2],
        "ffn_pre_norm": rows16[SM_FFN_PRE:SM_FFN_PRE + 2], "ffn_post_norm": rows16[SM_FFN_POST:SM_FFN_POST + 2],
        "ssd_dt_bias": misc[MISC_DT_BIAS:MISC_DT_BIAS + 32], "ssd_a_log": misc[MISC_A_LOG:MISC_A_LOG + 32],
        "ssd_d": misc[MISC_D:MISC_D + 32], "attn_sinks": misc[MISC_SINKS:MISC_SINKS + 16],
        "ssd_conv_w": rows8[0:SSD_CONV_WIDTH], "attn_b_qkv": rows8[SSD_CONV_WIDTH, 0:384], "attn_b_o": rows8[SSD_CONV_WIDTH, 384:640],
    }
    return {k: v.reshape(like[k].shape) for k, v in out.items()}


def kernel(x, ssd_w_in, ssd_conv_w, ssd_conv_b, ssd_dt_bias, ssd_a_log, ssd_d, ssd_norm_w, ssd_w_out, attn_w_qkv, attn_b_qkv, attn_sinks, attn_w_o, attn_b_o, mlp_w_up, mlp_w_down, mix_pre_norm, mix_post_norm, ffn_pre_norm, ffn_post_norm, loss_target, m_ssd_w_in, m_ssd_conv_w, m_ssd_conv_b, m_ssd_dt_bias, m_ssd_a_log, m_ssd_d, m_ssd_norm_w, m_ssd_w_out, m_attn_w_qkv, m_attn_b_qkv, m_attn_sinks, m_attn_w_o, m_attn_b_o, m_mlp_w_up, m_mlp_w_down, m_mix_pre_norm, m_mix_post_norm, m_ffn_pre_norm, m_ffn_post_norm, v_ssd_w_in, v_ssd_conv_w, v_ssd_conv_b, v_ssd_dt_bias, v_ssd_a_log, v_ssd_d, v_ssd_norm_w, v_ssd_w_out, v_attn_w_qkv, v_attn_b_qkv, v_attn_sinks, v_attn_w_o, v_attn_b_o, v_mlp_w_up, v_mlp_w_down, v_mix_pre_norm, v_mix_post_norm, v_ffn_pre_norm, v_ffn_post_norm):
    w = dict(zip(WEIGHT_NAMES, (ssd_w_in, ssd_conv_w, ssd_conv_b, ssd_dt_bias, ssd_a_log, ssd_d, ssd_norm_w, ssd_w_out, attn_w_qkv, attn_b_qkv, attn_sinks, attn_w_o, attn_b_o, mlp_w_up, mlp_w_down, mix_pre_norm, mix_post_norm, ffn_pre_norm, ffn_post_norm)))
    m = dict(zip(WEIGHT_NAMES, (m_ssd_w_in, m_ssd_conv_w, m_ssd_conv_b, m_ssd_dt_bias, m_ssd_a_log, m_ssd_d, m_ssd_norm_w, m_ssd_w_out, m_attn_w_qkv, m_attn_b_qkv, m_attn_sinks, m_attn_w_o, m_attn_b_o, m_mlp_w_up, m_mlp_w_down, m_mix_pre_norm, m_mix_post_norm, m_ffn_pre_norm, m_ffn_post_norm)))
    v = dict(zip(WEIGHT_NAMES, (v_ssd_w_in, v_ssd_conv_w, v_ssd_conv_b, v_ssd_dt_bias, v_ssd_a_log, v_ssd_d, v_ssd_norm_w, v_ssd_w_out, v_attn_w_qkv, v_attn_b_qkv, v_attn_sinks, v_attn_w_o, v_attn_b_o, v_mlp_w_up, v_mlp_w_down, v_mix_pre_norm, v_mix_post_norm, v_ffn_pre_norm, v_ffn_post_norm)))
    chip = 2 * lax.axis_index("x") + lax.axis_index("y")

    two_halves = lambda a: a.astype(BF16).reshape(2, a.shape[0] // 2, a.shape[1])
    mat_shards = [two_halves(a) for a in (w["ssd_w_in"][0], w["ssd_w_out"][0], w["attn_w_qkv"][0], w["attn_w_o"][0],
                                          w["mlp_w_up"][0], w["mlp_w_up"][1], w["mlp_w_down"][0], w["mlp_w_down"][1])]
    g_in, g_out, g_qkv, g_o, g_up0, g_up1, g_down0, g_down1, g_conv, g_bqkv, g_bo = _all_gather_chips(
        mat_shards, [w["ssd_conv_w"][0], w["attn_b_qkv"], w["attn_b_o"]], name="weight_all_gather")
    whole = lambda a: a.reshape(N_CHIPS, 2 * a.shape[2], a.shape[3])
    full = {
        "ssd_w_in": _w_in_from_shards(whole(g_in), name="ssd_w_in_unshard"),
        "ssd_w_out": g_out.reshape(SSD_D_INNER, D_MODEL),
        "attn_w_qkv": whole(g_qkv),
        "attn_w_o": g_o.reshape(D_MODEL, D_MODEL),
        "mlp_w_up": (whole(g_up0), whole(g_up1)),
        "mlp_w_down": (g_down0.reshape(D_FF, D_MODEL), g_down1.reshape(D_FF, D_MODEL)),
        "ssd_conv_w": g_conv.transpose(1, 0, 2).reshape(SSD_CONV_WIDTH, SSD_CONV_DIM),
        "attn_b_qkv": g_bqkv.reshape(ATTN_QKV), "attn_b_o": g_bo.reshape(D_MODEL),
    }
    for name in REPLICATED:
        full[name] = w[name][0] if name.startswith(("ssd_", "attn_")) else w[name]

    loss_tile, grad_x, gm, g = _local_step(x[0], loss_target[0], full)

    parts = [_shard_halves(a) for a in (gm["ssd_w_in"], gm["ssd_w_out"], gm["attn_w_qkv"], gm["attn_w_o"],
                                        gm["mlp_w_up"][0], gm["mlp_w_up"][1], gm["mlp_w_down"][0], gm["mlp_w_down"][1])]
    core = lax.axis_index("c").astype(jnp.int32).reshape(1)
    theirs = _send_other_half(parts, name="grad_sibling_send")
    chip_sums = _add_sibling_half(parts, theirs, core, name="grad_chip_sum")
    conv_w_rows = g["ssd_conv_w"].reshape(SSD_CONV_WIDTH * N_CHIPS, D_MODEL)
    b_qkv_rows = jnp.pad(g["attn_b_qkv"], (0, 2 * D_MODEL - ATTN_QKV)).reshape(2, D_MODEL)
    small = jnp.concatenate([_replicated_rows(g, loss_tile[0, 0]), conv_w_rows, b_qkv_rows, _rows(g["attn_b_o"]),
                             jnp.zeros((SM_ROWS - SM_B_O - 1, D_MODEL), F32)], axis=0)
    *recv, small_all = _grad_exchange(chip_sums, small, name="grad_exchange")
    halves = _sum_chips(recv, name="grad_sum")
    r_in, r_out, r_qkv, r_o, r_up, r_down = _swap_halves(halves, layers=((4, 5), (6, 7)), name="grad_halves_swap")
    small_sum, = _sum_chips([small_all], name="small_grad_sum")

    grads = {"ssd_w_in": r_in, "ssd_w_out": r_out, "attn_w_qkv": r_qkv, "attn_w_o": r_o, "mlp_w_up": r_up, "mlp_w_down": r_down}
    grads = {k: a.reshape(w[k].shape) for k, a in grads.items()}
    conv_w_g = lax.dynamic_index_in_dim(small_sum[SM_CONV_W:SM_CONV_W + 16].reshape(SSD_CONV_WIDTH, N_CHIPS, D_MODEL), chip, axis=1, keepdims=False)
    b_qkv_g = lax.dynamic_slice_in_dim(small_sum[SM_B_QKV:SM_B_QKV + 2].reshape(-1), chip * 384, 384)
    b_o_g = lax.dynamic_slice_in_dim(small_sum[SM_B_O], chip * 256, 256)
    small_g = jnp.concatenate([small_sum[0:16], _sharded_rows(conv_w_g, b_qkv_g, b_o_g)], axis=0)
    grads.update(_unpack_small(small_g[0:16], small_g[16:24], w))
    loss = small_sum[SM_MISC, MISC_LOSS]

    delta, new_m, new_v = {}, {}, {}
    for name in MATRICES:
        shape = w[name].shape
        as2d = lambda a: a.reshape(-1, shape[-1])
        d2, m2, v2 = _adamw(as2d(w[name]), as2d(grads[name]), as2d(m[name]), as2d(v[name]), name=f"adamw_{name}")
        delta[name], new_m[name], new_v[name] = d2.reshape(shape), m2.reshape(shape), v2.reshape(shape)
    zero = jnp.zeros((), F32)
    small_pack = lambda p: jnp.concatenate([_replicated_rows({k: p[k] for k in REPLICATED}, zero),
                                            _sharded_rows(p["ssd_conv_w"], p["attn_b_qkv"], p["attn_b_o"])], axis=0)
    d_s, m_s, v_s = _adamw(small_pack(w), small_g, small_pack(m), small_pack(v), name="adamw_vectors")
    delta.update(_unpack_small(d_s[0:16], d_s[16:24], w))
    new_m.update(_unpack_small(m_s[0:16], m_s[16:24], w))
    new_v.update(_unpack_small(v_s[0:16], v_s[16:24], w))

    return (loss, grad_x[None], *[grads[n] for n in WEIGHT_NAMES], *[delta[n] for n in WEIGHT_NAMES],
            *[new_m[n] for n in WEIGHT_NAMES], *[new_v[n] for n in WEIGHT_NAMES])
```

```python
import functools
import math

import jax
import jax.numpy as jnp
from jax import lax
from jax.experimental import pallas as pl
from jax.experimental.pallas import tpu as pltpu

F32 = jnp.float32
BF16 = jnp.bfloat16

D_MODEL = 1024
SSD_D_INNER = 2048
SSD_HEAD_DIM = 64
SSD_N_HEADS = 32
SSD_N_GROUPS = 8
SSD_HPG = 4
SSD_D_STATE = 128
SSD_CONV_WIDTH = 4
SSD_CHUNK = 128
SSD_CONV_DIM = 4096
SSD_IN_DIM = 6176
SSD_IN_PAD = 6272
SSD_GW = SSD_HPG * SSD_HEAD_DIM
ATTN_HEAD_DIM = 64
ATTN_N_Q = 16
ATTN_N_KV = 4
ATTN_REP = 4
ATTN_WINDOW = 128
ATTN_QKV = 1536
D_FF = 4096
NORM_EPS = 1e-6

ADAM_LR = 0.001
ADAM_B1 = 0.9
ADAM_B2 = 0.999
ADAM_EPS = 1e-08
ADAM_WD = 0.01
ADAM_STEP = 10

N_CHIPS = 4
N_DEV = 8
LANES = 128
VMEM_LIMIT = 48 * 1024 * 1024

MESH = pl.DeviceIdType.MESH


def _params(*sem):
    return pltpu.CompilerParams(dimension_semantics=sem, vmem_limit_bytes=VMEM_LIMIT)


def _dot(a, b, dims):
    return lax.dot_general(a, b, (dims, ((), ())), preferred_element_type=F32)


def _dot_nn(a, b):
    return _dot(a, b, ((1,), (0,)))


def _dot_nt(a, b):
    return _dot(a, b, ((1,), (1,)))


def _dot_tn(a, b):
    return _dot(a, b, ((0,), (0,)))


def _sigmoid(x):
    return 1.0 / (1.0 + jnp.exp(-x))


def _matmul(a, b, *, mode, out_dtypes, name, epilogue=None, extras=(), tm=1024, tn=1024, tk=1024,
            b_shards=False, out_shards=False):
    if b_shards:
        s, b_rows, b_cols = b.shape
        b2 = (b_rows, s * b_cols)
        if mode == "nn":
            tn = b_cols
        else:
            assert mode == "nt"
            tk = b_cols
    else:
        b2 = b.shape
    if mode == "nn":
        (m, k), (k2, n) = a.shape, b2
    elif mode == "nt":
        (m, k), (n, k2) = a.shape, b2
    else:
        (k, m), (k2, n) = a.shape, b2
    assert k == k2, (a.shape, b.shape, mode)
    tm, tn, tk = min(tm, m), min(tn, n), min(tk, k)
    assert m % tm == 0 and n % tn == 0 and k % tk == 0, (m, n, k, tm, tn, tk)
    nk = k // tk
    if mode == "tn":
        a_spec = pl.BlockSpec((tk, tm), lambda i, j, kk: (kk, i))
    else:
        a_spec = pl.BlockSpec((tm, tk), lambda i, j, kk: (i, kk))
    if b_shards and mode == "nn":
        b_spec = pl.BlockSpec((None, tk, tn), lambda i, j, kk: (j, kk, 0))
    elif b_shards:
        b_spec = pl.BlockSpec((None, tn, tk), lambda i, j, kk: (kk, j, 0))
    elif mode == "nt":
        b_spec = pl.BlockSpec((tn, tk), lambda i, j, kk: (j, kk))
    else:
        b_spec = pl.BlockSpec((tk, tn), lambda i, j, kk: (kk, j))
    dims = {"nn": ((1,), (0,)), "nt": ((1,), (1,)), "tn": ((0,), (0,))}[mode]
    ex_specs = []
    for arr, kind in extras:
        if kind == "tile":
            ex_specs.append(pl.BlockSpec((tm, tn), lambda i, j, kk: (i, j)))
        else:
            ex_specs.append(pl.BlockSpec((1, tn), lambda i, j, kk: (0, j)))
    n_ex, n_out = len(extras), len(out_dtypes)
    if epilogue is None:
        epilogue = lambda acc: (acc,)

    def body(a_ref, b_ref, *rest):
        ex = rest[:n_ex]
        outs = rest[n_ex:n_ex + n_out]

        def finish(acc):
            res = epilogue(acc, *[e[...] for e in ex])
            for o, r in zip(outs, res):
                o[...] = r.astype(o.dtype)

        if nk == 1:
            finish(_dot(a_ref[...], b_ref[...], dims))
        else:
            acc_ref = rest[-1]
            kk = pl.program_id(2)

            @pl.when(kk == 0)
            def _():
                acc_ref[...] = jnp.zeros_like(acc_ref)

            acc_ref[...] += _dot(a_ref[...], b_ref[...], dims)

            @pl.when(kk == nk - 1)
            def _():
                finish(acc_ref[...])

    if out_shards:
        out_spec = pl.BlockSpec((None, tm, tn), lambda i, j, kk: (j, i, 0))
        out_dims = (n // tn, m, tn)
    else:
        out_spec = pl.BlockSpec((tm, tn), lambda i, j, kk: (i, j))
        out_dims = (m, n)
    outs = pl.pallas_call(
        body,
        grid=(m // tm, n // tn, nk),
        in_specs=[a_spec, b_spec] + ex_specs,
        out_specs=[out_spec for _ in out_dtypes],
        out_shape=[jax.ShapeDtypeStruct(out_dims, dt) for dt in out_dtypes],
        scratch_shapes=[] if nk == 1 else [pltpu.VMEM((tm, tn), F32)],
        compiler_params=_params("parallel", "parallel", "arbitrary"),
        name=name,
    )(a, b, *[arr for arr, _ in extras])
    return outs[0] if n_out == 1 else outs


def _row_tile(t, want):
    return min(t, want)


def _rms_fwd(x, w, *, name, resid=None, want_u=None):
    t, d = x.shape
    tr = _row_tile(t, 512)

    def norm(v, wv):
        return v * lax.rsqrt(jnp.mean(v * v, axis=-1, keepdims=True) + NORM_EPS) * wv

    row = pl.BlockSpec((tr, d), lambda i: (i, 0))
    vec = pl.BlockSpec((1, d), lambda i: (0, 0))
    if resid is None:
        def body(x_ref, w_ref, o_ref):
            o_ref[...] = norm(x_ref[...], w_ref[...]).astype(BF16)
        ins, in_specs = (x, w), [row, vec]
        out_shape, out_specs = jax.ShapeDtypeStruct((t, d), BF16), row
    elif want_u is None:
        def body(x_ref, w_ref, r_ref, o_ref):
            o_ref[...] = r_ref[...] + norm(x_ref[...], w_ref[...])
        ins, in_specs = (x, w, resid), [row, vec, row]
        out_shape, out_specs = jax.ShapeDtypeStruct((t, d), F32), row
    else:
        def body(x_ref, w_ref, r_ref, w2_ref, o_ref, u_ref):
            h = r_ref[...] + norm(x_ref[...], w_ref[...])
            o_ref[...] = h
            u_ref[...] = norm(h, w2_ref[...]).astype(BF16)
        ins, in_specs = (x, w, resid, want_u), [row, vec, row, vec]
        out_shape = [jax.ShapeDtypeStruct((t, d), F32), jax.ShapeDtypeStruct((t, d), BF16)]
        out_specs = [row, row]
    return pl.pallas_call(body, grid=(t // tr,), in_specs=in_specs, out_specs=out_specs, out_shape=out_shape,
                          compiler_params=_params("parallel"), name=name)(*ins)


def _rms_bwd(x, w, dy, *, name, resid=None, out_dtype=F32):
    t, d = x.shape
    tr = _row_tile(t, 512)
    row = pl.BlockSpec((tr, d), lambda i: (i, 0))
    vec = pl.BlockSpec((1, d), lambda i: (0, 0))
    has_res = resid is not None

    def body(x_ref, w_ref, dy_ref, *rest):
        if has_res:
            r_ref, dx_ref, dw_ref = rest
        else:
            dx_ref, dw_ref = rest
        xv = x_ref[...]
        dyv = dy_ref[...].astype(F32)
        r = lax.rsqrt(jnp.mean(xv * xv, axis=-1, keepdims=True) + NORM_EPS)
        xhat = xv * r
        dyw = dyv * w_ref[...]
        dx = r * (dyw - xhat * jnp.mean(dyw * xhat, axis=-1, keepdims=True))
        if has_res:
            dx = dx + r_ref[...]
        dx_ref[...] = dx.astype(dx_ref.dtype)

        @pl.when(pl.program_id(0) == 0)
        def _():
            dw_ref[...] = jnp.zeros_like(dw_ref)

        dw_ref[...] += jnp.sum(dyv * xhat, axis=0, keepdims=True)

    ins = (x, w, dy) + ((resid,) if has_res else ())
    in_specs = [row, vec, row] + ([row] if has_res else [])
    return pl.pallas_call(
        body, grid=(t // tr,), in_specs=in_specs, out_specs=[row, vec],
        out_shape=[jax.ShapeDtypeStruct((t, d), out_dtype), jax.ShapeDtypeStruct((1, d), F32)],
        compiler_params=_params("arbitrary"), name=name)(*ins)


def _loss_head(h, target, *, name):
    t, d = h.shape
    tr = _row_tile(t, 512)
    row = pl.BlockSpec((tr, d), lambda i: (i, 0))

    def body(h_ref, t_ref, dh_ref, loss_ref):
        err = h_ref[...] - t_ref[...]
        dh_ref[...] = err * (1.0 / d)

        @pl.when(pl.program_id(0) == 0)
        def _():
            loss_ref[...] = jnp.zeros_like(loss_ref)

        part = jnp.sum(jnp.sum(err * err, axis=1, keepdims=True), axis=0, keepdims=True) * (0.5 / d)
        loss_ref[...] += jnp.broadcast_to(part, loss_ref.shape)

    return pl.pallas_call(
        body, grid=(t // tr,), in_specs=[row, row],
        out_specs=[row, pl.BlockSpec((8, LANES), lambda i: (0, 0))],
        out_shape=[jax.ShapeDtypeStruct((t, d), F32), jax.ShapeDtypeStruct((8, LANES), F32)],
        compiler_params=_params("arbitrary"), name=name)(h, target)


def _col_sum(x, *, name):
    t, n = x.shape
    tr = _row_tile(t, 512)

    def body(x_ref, o_ref):
        @pl.when(pl.program_id(0) == 0)
        def _():
            o_ref[...] = jnp.zeros_like(o_ref)

        o_ref[...] += jnp.sum(x_ref[...].astype(F32), axis=0, keepdims=True)

    return pl.pallas_call(
        body, grid=(t // tr,), in_specs=[pl.BlockSpec((tr, n), lambda i: (i, 0))],
        out_specs=pl.BlockSpec((1, n), lambda i: (0, 0)), out_shape=jax.ShapeDtypeStruct((1, n), F32),
        compiler_params=_params("arbitrary"), name=name)(x)


SSD_IN_SHARD = SSD_IN_DIM // N_CHIPS


def _w_in_from_shards(shards, *, name):
    d = shards.shape[1]
    tr = 256

    def body(s_ref, o_ref):
        o_ref[:, pl.ds(SSD_IN_PAD - LANES, LANES)] = jnp.zeros((tr, LANES), o_ref.dtype)
        for s in range(N_CHIPS):
            o_ref[:, pl.ds(SSD_IN_SHARD * s, SSD_IN_SHARD)] = s_ref[s]

    return pl.pallas_call(
        body, grid=(d // tr,), in_specs=[pl.BlockSpec((N_CHIPS, tr, SSD_IN_SHARD), lambda i: (0, i, 0))],
        out_specs=pl.BlockSpec((tr, SSD_IN_PAD), lambda i: (i, 0)),
        out_shape=jax.ShapeDtypeStruct((d, SSD_IN_PAD), shards.dtype),
        compiler_params=_params("parallel"), name=name)(shards)


def _w_in_to_shards(g, *, name):
    d = g.shape[0]
    tr = 256

    def body(g_ref, o_ref):
        for s in range(N_CHIPS):
            o_ref[s] = g_ref[:, pl.ds(SSD_IN_SHARD * s, SSD_IN_SHARD)].astype(o_ref.dtype)

    return pl.pallas_call(
        body, grid=(d // tr,), in_specs=[pl.BlockSpec((tr, SSD_IN_PAD), lambda i: (i, 0))],
        out_specs=pl.BlockSpec((N_CHIPS, tr, SSD_IN_SHARD), lambda i: (0, i, 0)),
        out_shape=jax.ShapeDtypeStruct((N_CHIPS, d, SSD_IN_SHARD), BF16),
        compiler_params=_params("parallel"), name=name)(g)


XBC_COL0 = SSD_D_INNER // LANES
DT_COL0 = (SSD_D_INNER + SSD_CONV_DIM) // LANES


def _shift_down(v, k, row_ids):
    return jnp.where(row_ids >= k, pltpu.roll(v, k, axis=0), 0.0)


def _shift_up(v, k, row_ids):
    n = v.shape[0]
    return jnp.where(row_ids < n - k, pltpu.roll(v, n - k, axis=0), 0.0)


def _conv_pre(x, w, b, row_ids):
    pre = b + w[3:4, :] * x
    for k in (1, 2, 3):
        pre = pre + w[3 - k:4 - k, :] * _shift_down(x, k, row_ids)
    return pre


def _conv_fwd(zx, conv_w, conv_b, *, name):
    t = zx.shape[0]
    nct = SSD_CONV_DIM // LANES

    def body(x_ref, w_ref, b_ref, o_ref):
        x = x_ref[...]
        row_ids = lax.broadcasted_iota(jnp.int32, x.shape, 0)
        pre = _conv_pre(x, w_ref[...], b_ref[...], row_ids)
        o_ref[...] = pre * _sigmoid(pre)

    return pl.pallas_call(
        body, grid=(nct,),
        in_specs=[pl.BlockSpec((t, LANES), lambda j: (0, XBC_COL0 + j)),
                  pl.BlockSpec((SSD_CONV_WIDTH, LANES), lambda j: (0, j)),
                  pl.BlockSpec((1, LANES), lambda j: (0, j))],
        out_specs=pl.BlockSpec((t, LANES), lambda j: (0, j)),
        out_shape=jax.ShapeDtypeStruct((t, SSD_CONV_DIM), F32),
        compiler_params=_params("parallel"), name=name)(zx, conv_w, conv_b)


def _conv_bwd(zx, conv_w, conv_b, d_xs, d_bm, d_cm, *, name):
    t = zx.shape[0]
    nct = SSD_CONV_DIM // LANES
    n_xs = SSD_D_INNER // LANES
    n_bm = SSD_N_GROUPS * SSD_D_STATE // LANES

    def body(x_ref, w_ref, b_ref, dxs_ref, dbm_ref, dcm_ref, dx_ref, dw_ref, db_ref):
        x = x_ref[...]
        w = w_ref[...]
        j = pl.program_id(0)
        dy = jnp.where(j < n_xs, dxs_ref[...], jnp.where(j < n_xs + n_bm, dbm_ref[...], dcm_ref[...]))
        row_ids = lax.broadcasted_iota(jnp.int32, x.shape, 0)
        pre = _conv_pre(x, w, b_ref[...], row_ids)
        sg = _sigmoid(pre)
        dpre = dy * (sg * (1.0 + pre * (1.0 - sg)))
        dx = w[3:4, :] * dpre
        for k in (1, 2, 3):
            dx = dx + w[3 - k:4 - k, :] * _shift_up(dpre, k, row_ids)
        dx_ref[...] = dx.astype(dx_ref.dtype)
        db_ref[...] = jnp.sum(dpre, axis=0, keepdims=True)
        dw_ref[3:4, :] = jnp.sum(dpre * x, axis=0, keepdims=True)
        for k in (1, 2, 3):
            dw_ref[3 - k:4 - k, :] = jnp.sum(dpre * _shift_down(x, k, row_ids), axis=0, keepdims=True)

    col = pl.BlockSpec((t, LANES), lambda j: (0, j))
    clip = lambda j, lo, n: jnp.clip(j - lo, 0, n - 1)
    return pl.pallas_call(
        body, grid=(nct,),
        in_specs=[pl.BlockSpec((t, LANES), lambda j: (0, XBC_COL0 + j)),
                  pl.BlockSpec((SSD_CONV_WIDTH, LANES), lambda j: (0, j)),
                  pl.BlockSpec((1, LANES), lambda j: (0, j)),
                  pl.BlockSpec((t, LANES), lambda j: (0, clip(j, 0, n_xs))),
                  pl.BlockSpec((t, LANES), lambda j: (0, clip(j, n_xs, n_bm))),
                  pl.BlockSpec((t, LANES), lambda j: (0, clip(j, n_xs + n_bm, n_bm)))],
        out_specs=[col, pl.BlockSpec((SSD_CONV_WIDTH, LANES), lambda j: (0, j)), pl.BlockSpec((1, LANES), lambda j: (0, j))],
        out_shape=[jax.ShapeDtypeStruct((t, SSD_CONV_DIM), BF16),
                   jax.ShapeDtypeStruct((SSD_CONV_WIDTH, SSD_CONV_DIM), F32),
                   jax.ShapeDtypeStruct((1, SSD_CONV_DIM), F32)],
        compiler_params=_params("parallel"), name=name)(zx, conv_w, conv_b, d_xs, d_bm, d_cm)


def _softplus_fwd(zx, bias_row, *, name):
    t = zx.shape[0]
    tr = _row_tile(t, 1024)

    def body(x_ref, b_ref, o_ref):
        v = x_ref[...] + b_ref[...]
        e = jnp.exp(-jnp.abs(v))
        u = 1.0 + e
        log1p = jnp.where(u == 1.0, e, jnp.log(u) * (e / (u - 1.0)))
        o_ref[...] = jnp.maximum(v, 0.0) + log1p

    return pl.pallas_call(
        body, grid=(t // tr,),
        in_specs=[pl.BlockSpec((tr, LANES), lambda i: (i, DT_COL0)), pl.BlockSpec((1, LANES), lambda i: (0, 0))],
        out_specs=pl.BlockSpec((tr, LANES), lambda i: (i, 0)),
        out_shape=jax.ShapeDtypeStruct((t, LANES), F32),
        compiler_params=_params("parallel"), name=name)(zx, bias_row)


def _softplus_bwd(zx, bias_row, ddt, *, name):
    t = zx.shape[0]
    tr = _row_tile(t, 1024)

    def body(x_ref, b_ref, g_ref, o_ref, db_ref):
        v = x_ref[...] + b_ref[...]
        lane = lax.broadcasted_iota(jnp.int32, v.shape, 1)
        d = jnp.where(lane < SSD_N_HEADS, g_ref[...] * _sigmoid(v), 0.0)
        o_ref[...] = d.astype(o_ref.dtype)

        @pl.when(pl.program_id(0) == 0)
        def _():
            db_ref[...] = jnp.zeros_like(db_ref)

        db_ref[...] += jnp.sum(d, axis=0, keepdims=True)

    return pl.pallas_call(
        body, grid=(t // tr,),
        in_specs=[pl.BlockSpec((tr, LANES), lambda i: (i, DT_COL0)), pl.BlockSpec((1, LANES), lambda i: (0, 0)),
                  pl.BlockSpec((tr, LANES), lambda i: (i, 0))],
        out_specs=[pl.BlockSpec((tr, LANES), lambda i: (i, 0)), pl.BlockSpec((1, LANES), lambda i: (0, 0))],
        out_shape=[jax.ShapeDtypeStruct((t, LANES), BF16), jax.ShapeDtypeStruct((1, LANES), F32)],
        compiler_params=_params("arbitrary"), name=name)(zx, bias_row, ddt)


def _ssd_masks():
    q = SSD_CHUNK
    tt = lax.broadcasted_iota(jnp.int32, (q, q), 0)
    ss = lax.broadcasted_iota(jnp.int32, (q, q), 1)
    lane = lax.broadcasted_iota(jnp.int32, (1, SSD_GW), 1)
    srow = lax.broadcasted_iota(jnp.int32, (SSD_GW, 1), 0)
    hm = [(lane >= SSD_HEAD_DIM * j) & (lane < SSD_HEAD_DIM * (j + 1)) for j in range(SSD_HPG)]
    rm = [(srow >= SSD_HEAD_DIM * j) & (srow < SSD_HEAD_DIM * (j + 1)) for j in range(SSD_HPG)]
    return tt, ss, hm, rm


def _ssd_head_terms(dtc, dtr, a_rows, j, tt, ss):
    q = SSD_CHUNK
    dt_col = dtc[:, j:j + 1]
    dt_row = dtr[j:j + 1, :]
    a_row1 = a_rows[j:j + 1, :]
    a_11 = a_rows[j:j + 1, 0:1]
    cum_col = jnp.sum(jnp.where(ss <= tt, dt_row * a_row1, 0.0), axis=1, keepdims=True)
    cum_row = jnp.sum(jnp.where(tt <= ss, dt_col * a_11, 0.0), axis=0, keepdims=True)
    decay = jnp.exp(jnp.where(ss <= tt, cum_col - cum_row, -jnp.inf))
    cum_last = cum_col[q - 1:q, :]
    e_col = jnp.exp(cum_col)
    dte_col = jnp.exp(cum_last - cum_col)
    e_last = jnp.exp(cum_last)
    return dt_col, dt_row, a_row1, a_11, decay, e_col, dte_col, e_last


def _ssd_group_specs(nc):
    xs = pl.BlockSpec((SSD_CHUNK, SSD_GW), lambda g, c: (c, g))
    bm = pl.BlockSpec((SSD_CHUNK, SSD_D_STATE), lambda g, c: (c, SSD_D_INNER // SSD_D_STATE + g))
    cm = pl.BlockSpec((SSD_CHUNK, SSD_D_STATE), lambda g, c: (c, SSD_D_INNER // SSD_D_STATE + SSD_N_GROUPS + g))
    dtc = pl.BlockSpec((None, SSD_CHUNK, SSD_HPG), lambda g, c: (g, c, 0))
    dtr = pl.BlockSpec((None, SSD_HPG, SSD_CHUNK), lambda g, c: (g, 0, c))
    par = pl.BlockSpec((None, SSD_HPG, LANES), lambda g, c: (g, 0, 0))
    st = pl.BlockSpec((None, None, SSD_GW, SSD_D_STATE), lambda g, c: (g, c, 0, 0))
    return xs, bm, cm, dtc, dtr, par, st


def _ssd_fwd(xc, dtc, dtr, alog_b, d_b, *, name):
    t = xc.shape[0]
    nc = t // SSD_CHUNK
    xs_s, bm_s, cm_s, dtc_s, dtr_s, par_s, st_s = _ssd_group_specs(nc)

    def body(x_ref, b_ref, c_ref, dtc_ref, dtr_ref, alog_ref, d_ref, y_ref, st_ref, s_scr):
        @pl.when(pl.program_id(1) == 0)
        def _():
            s_scr[...] = jnp.zeros_like(s_scr)

        tt, ss, hm, rm = _ssd_masks()
        x = x_ref[...]
        bm = b_ref[...].astype(BF16)
        cm = c_ref[...].astype(BF16)
        s_in = s_scr[...]
        st_ref[...] = s_in
        a_rows = -jnp.exp(alog_ref[...])
        d_rows = d_ref[...]
        dtc_v, dtr_v = dtc_ref[...], dtr_ref[...]
        xb = x.astype(BF16)
        g = _dot_nt(cm, bm)
        y = jnp.zeros(x.shape, F32)
        e_all = jnp.zeros(x.shape, F32)
        w_all = jnp.zeros(x.shape, F32)
        d_all = jnp.zeros((1, SSD_GW), F32)
        e_s = jnp.zeros((SSD_GW, 1), F32)
        for j in range(SSD_HPG):
            dt_col, dt_row, _, _, decay, e_col, dte_col, e_last = _ssd_head_terms(dtc_v, dtr_v, a_rows, j, tt, ss)
            m = g * decay * dt_row
            y = jnp.where(hm[j], _dot_nn(m.astype(BF16), xb), y)
            e_all = jnp.where(hm[j], e_col, e_all)
            w_all = jnp.where(hm[j], dt_col * dte_col, w_all)
            d_all = jnp.where(hm[j], d_rows[j:j + 1, 0:1], d_all)
            e_s = jnp.where(rm[j], e_last, e_s)
        y = y + _dot_nt(cm, s_in.astype(BF16)) * e_all + x * d_all
        y_ref[...] = y
        s_scr[...] = s_in * e_s + _dot_tn((x * w_all).astype(BF16), bm)

    return pl.pallas_call(
        body, grid=(SSD_N_GROUPS, nc),
        in_specs=[xs_s, bm_s, cm_s, dtc_s, dtr_s, par_s, par_s],
        out_specs=[pl.BlockSpec((SSD_CHUNK, SSD_GW), lambda g, c: (c, g)), st_s],
        out_shape=[jax.ShapeDtypeStruct((t, SSD_D_INNER), F32),
                   jax.ShapeDtypeStruct((SSD_N_GROUPS, nc, SSD_GW, SSD_D_STATE), F32)],
        scratch_shapes=[pltpu.VMEM((SSD_GW, SSD_D_STATE), F32)],
        compiler_params=_params("parallel", "arbitrary"), name=name)(xc, xc, xc, dtc, dtr, alog_b, d_b)


def _ssd_bwd(xc, dtc, dtr, alog_b, d_b, states, dy, *, name):
    t = xc.shape[0]
    nc = t // SSD_CHUNK
    q = SSD_CHUNK
    rev = lambda c: nc - 1 - c
    xs_s = pl.BlockSpec((q, SSD_GW), lambda g, c: (rev(c), g))
    bm_s = pl.BlockSpec((q, SSD_D_STATE), lambda g, c: (rev(c), SSD_D_INNER // SSD_D_STATE + g))
    cm_s = pl.BlockSpec((q, SSD_D_STATE), lambda g, c: (rev(c), SSD_D_INNER // SSD_D_STATE + SSD_N_GROUPS + g))
    dtc_s = pl.BlockSpec((None, q, SSD_HPG), lambda g, c: (g, rev(c), 0))
    dtr_s = pl.BlockSpec((None, SSD_HPG, q), lambda g, c: (g, 0, rev(c)))
    par_s = pl.BlockSpec((None, SSD_HPG, LANES), lambda g, c: (g, 0, 0))
    st_s = pl.BlockSpec((None, None, SSD_GW, SSD_D_STATE), lambda g, c: (g, rev(c), 0, 0))

    def body(x_ref, b_ref, c_ref, dtc_ref, dtr_ref, alog_ref, d_ref, st_ref, dy_ref,
             dx_ref, db_ref, dc_ref, ddt_ref, dpar_ref, ds_scr):
        @pl.when(pl.program_id(1) == 0)
        def _():
            ds_scr[...] = jnp.zeros_like(ds_scr)
            dpar_ref[...] = jnp.zeros_like(dpar_ref)

        tt, ss, hm, rm = _ssd_masks()
        tcol = lax.broadcasted_iota(jnp.int32, (q, 1), 0)
        lane = lax.broadcasted_iota(jnp.int32, (1, LANES), 1)
        x = x_ref[...]
        bm = b_ref[...].astype(BF16)
        cm = c_ref[...].astype(BF16)
        s_in = st_ref[...]
        ds = ds_scr[...]
        dyv = dy_ref[...]
        a_rows = -jnp.exp(alog_ref[...])
        d_rows = d_ref[...]
        dtc_v, dtr_v = dtc_ref[...], dtr_ref[...]
        xb = x.astype(BF16)
        dyb = dyv.astype(BF16)
        s_b = s_in.astype(BF16)
        ds_b = ds.astype(BF16)
        g = _dot_nt(cm, bm)
        cs = _dot_nt(cm, s_b)
        bds = _dot_nt(bm, ds_b)
        dy_cs = dyv * cs
        x_bds = x * bds
        dy_x = dyv * x
        ds_s = ds * s_in
        dg = jnp.zeros((q, q), F32)
        dx = jnp.zeros(x.shape, F32)
        e_all = jnp.zeros(x.shape, F32)
        w_all = jnp.zeros(x.shape, F32)
        d_all = jnp.zeros((1, SSD_GW), F32)
        e_s = jnp.zeros((SSD_GW, 1), F32)
        for j in range(SSD_HPG):
            dt_col, dt_row, a_row1, a_11, decay, e_col, dte_col, e_last = _ssd_head_terms(dtc_v, dtr_v, a_rows, j, tt, ss)
            dm = _dot_nt(jnp.where(hm[j], dyv, 0.0).astype(BF16), xb)
            gl = g * decay
            wp = dm * gl
            dg = dg + dm * decay * dt_row
            dx = jnp.where(hm[j], _dot_tn((gl * dt_row).astype(BF16), dyb), dx)
            w = wp * dt_row
            rw_col = jnp.sum(w, axis=1, keepdims=True)
            cw_row = jnp.sum(w, axis=0, keepdims=True)
            cwp_row = jnp.sum(wp, axis=0, keepdims=True)
            r1_col = jnp.sum(jnp.where(hm[j], dy_cs, 0.0), axis=1, keepdims=True) * e_col
            dw_col = jnp.sum(jnp.where(hm[j], x_bds, 0.0), axis=1, keepdims=True)
            w_col = dt_col * dte_col
            s_sum = jnp.sum(jnp.sum(jnp.where(rm[j], ds_s, 0.0), axis=1, keepdims=True), axis=0, keepdims=True)
            last_add = jnp.sum(dw_col * w_col, axis=0, keepdims=True) + e_last * s_sum
            dcum_col = rw_col + r1_col - dw_col * w_col + jnp.where(tcol == q - 1, last_add, 0.0)
            da_row = jnp.sum(jnp.where(tt >= ss, dcum_col, 0.0), axis=0, keepdims=True)
            da_col = jnp.sum(jnp.where(ss >= tt, -cw_row, 0.0), axis=1, keepdims=True)
            ddt_col = a_11 * da_col + dw_col * dte_col
            ddt_row = a_row1 * da_row + cwp_row + jnp.sum(jnp.where(tt == ss, ddt_col, 0.0), axis=0, keepdims=True)
            ddt_ref[j:j + 1, :] = ddt_row
            d_a = jnp.sum(dt_row * da_row, axis=1, keepdims=True) + jnp.sum(dt_col * da_col, axis=0, keepdims=True)
            d_d = jnp.sum(jnp.sum(jnp.where(hm[j], dy_x, 0.0), axis=1, keepdims=True), axis=0, keepdims=True)
            dpar_ref[j:j + 1, :] += jnp.where(lane == 0, d_a * a_11, 0.0) + jnp.where(lane == 1, d_d, 0.0)
            e_all = jnp.where(hm[j], e_col, e_all)
            w_all = jnp.where(hm[j], w_col, w_all)
            d_all = jnp.where(hm[j], d_rows[j:j + 1, 0:1], d_all)
            e_s = jnp.where(rm[j], e_last, e_s)
        dx_ref[...] = dx + w_all * bds + d_all * dyv
        dye = (dyv * e_all).astype(BF16)
        dgb = dg.astype(BF16)
        xw = (x * w_all).astype(BF16)
        dc_ref[...] = _dot_nn(dgb, bm) + _dot_nn(dye, s_b)
        db_ref[...] = _dot_tn(dgb, cm) + _dot_nn(xw, ds_b)
        ds_scr[...] = ds * e_s + _dot_tn(dye, cm)

    return pl.pallas_call(
        body, grid=(SSD_N_GROUPS, nc),
        in_specs=[xs_s, bm_s, cm_s, dtc_s, dtr_s, par_s, par_s, st_s, pl.BlockSpec((q, SSD_GW), lambda g, c: (rev(c), g))],
        out_specs=[pl.BlockSpec((q, SSD_GW), lambda g, c: (rev(c), g)),
                   pl.BlockSpec((q, SSD_D_STATE), lambda g, c: (rev(c), g)),
                   pl.BlockSpec((q, SSD_D_STATE), lambda g, c: (rev(c), g)),
                   pl.BlockSpec((None, SSD_HPG, q), lambda g, c: (g, 0, rev(c))),
                   pl.BlockSpec((None, SSD_HPG, LANES), lambda g, c: (g, 0, 0))],
        out_shape=[jax.ShapeDtypeStruct((t, SSD_D_INNER), F32),
                   jax.ShapeDtypeStruct((t, SSD_N_GROUPS * SSD_D_STATE), F32),
                   jax.ShapeDtypeStruct((t, SSD_N_GROUPS * SSD_D_STATE), F32),
                   jax.ShapeDtypeStruct((SSD_N_GROUPS, SSD_HPG, t), F32),
                   jax.ShapeDtypeStruct((SSD_N_GROUPS, SSD_HPG, LANES), F32)],
        scratch_shapes=[pltpu.VMEM((SSD_GW, SSD_D_STATE), F32)],
        compiler_params=_params("parallel", "arbitrary"), name=name)(xc, xc, xc, dtc, dtr, alog_b, d_b, states, dy)


def _gate_norm_fwd(y, zx, norm_w, *, name):
    t = y.shape[0]
    tr = _row_tile(t, 256)
    row = pl.BlockSpec((tr, SSD_D_INNER), lambda i: (i, 0))

    def body(y_ref, z_ref, w_ref, o_ref):
        for gi in range(SSD_N_GROUPS):
            sl = pl.ds(gi * SSD_GW, SSD_GW)
            z = z_ref[:, sl]
            gv = y_ref[:, sl] * (z * _sigmoid(z))
            r = lax.rsqrt(jnp.mean(gv * gv, axis=-1, keepdims=True) + NORM_EPS)
            o_ref[:, sl] = (gv * r * w_ref[:, sl]).astype(BF16)

    return pl.pallas_call(
        body, grid=(t // tr,), in_specs=[row, row, pl.BlockSpec((1, SSD_D_INNER), lambda i: (0, 0))],
        out_specs=row, out_shape=jax.ShapeDtypeStruct((t, SSD_D_INNER), BF16),
        compiler_params=_params("parallel"), name=name)(y, zx, norm_w)


def _gate_norm_bwd(y, zx, norm_w, dyn, *, name):
    t = y.shape[0]
    tr = _row_tile(t, 256)
    row = pl.BlockSpec((tr, SSD_D_INNER), lambda i: (i, 0))
    vec = pl.BlockSpec((1, SSD_D_INNER), lambda i: (0, 0))

    def body(y_ref, z_ref, w_ref, dyn_ref, dy_ref, dz_ref, dw_ref):
        @pl.when(pl.program_id(0) == 0)
        def _():
            dw_ref[...] = jnp.zeros_like(dw_ref)

        for gi in range(SSD_N_GROUPS):
            sl = pl.ds(gi * SSD_GW, SSD_GW)
            z = z_ref[:, sl]
            yv = y_ref[:, sl]
            sg = _sigmoid(z)
            sz = z * sg
            gv = yv * sz
            r = lax.rsqrt(jnp.mean(gv * gv, axis=-1, keepdims=True) + NORM_EPS)
            ghat = gv * r
            dout = dyn_ref[:, sl].astype(F32)
            dgh = dout * w_ref[:, sl]
            dgv = r * (dgh - ghat * jnp.mean(dgh * ghat, axis=-1, keepdims=True))
            dy_ref[:, sl] = dgv * sz
            dz_ref[:, sl] = (dgv * yv * (sg * (1.0 + z * (1.0 - sg)))).astype(dz_ref.dtype)
            dw_ref[:, sl] += jnp.sum(dout * ghat, axis=0, keepdims=True)

    return pl.pallas_call(
        body, grid=(t // tr,), in_specs=[row, row, vec, row], out_specs=[row, row, vec],
        out_shape=[jax.ShapeDtypeStruct((t, SSD_D_INNER), F32), jax.ShapeDtypeStruct((t, SSD_D_INNER), BF16),
                   jax.ShapeDtypeStruct((1, SSD_D_INNER), F32)],
        compiler_params=_params("arbitrary"), name=name)(y, zx, norm_w, dyn)


ATTN_KV_W = ATTN_N_KV * ATTN_HEAD_DIM
ATTN_Q_HALF = 512
ATTN_K_BLK = ATTN_N_Q * ATTN_HEAD_DIM // ATTN_KV_W
ATTN_V_BLK = ATTN_K_BLK + 1


def _attn_probs(qs, kb, sink_col, first_block):
    w = ATTN_WINDOW
    s = _dot_nt(qs, kb) * (ATTN_HEAD_DIM ** -0.5)
    qpos = lax.broadcasted_iota(jnp.int32, (w, 2 * w), 0) + w
    kpos = lax.broadcasted_iota(jnp.int32, (w, 2 * w), 1)
    rel = qpos - kpos
    valid = (rel >= 0) & (rel < w) & jnp.logical_not(first_block & (kpos < w))
    valid = jnp.concatenate([valid] * ATTN_REP, axis=0)
    s = jnp.where(valid, s, -jnp.inf)
    m = jnp.maximum(jnp.max(s, axis=1, keepdims=True), sink_col)
    e = jnp.exp(s - m)
    es = jnp.exp(sink_col - m)
    inv = 1.0 / (jnp.sum(e, axis=1, keepdims=True) + es)
    return e * inv, es * inv


def _attn_head_views(q_lo, q_hi, k_cur, k_prev, v_cur, v_prev, kh):
    hd = ATTN_HEAD_DIM
    q_half = q_lo if kh < 2 else q_hi
    base = (kh % 2) * ATTN_REP * hd
    qs = jnp.concatenate([q_half[:, base + r * hd: base + (r + 1) * hd] for r in range(ATTN_REP)], axis=0)
    ksl = slice(kh * hd, (kh + 1) * hd)
    kb = jnp.concatenate([k_prev[:, ksl], k_cur[:, ksl]], axis=0)
    vb = jnp.concatenate([v_prev[:, ksl], v_cur[:, ksl]], axis=0)
    return qs, kb, vb


def _sink_col(sink_ref, kh):
    rows = lax.broadcasted_iota(jnp.int32, (ATTN_REP * ATTN_WINDOW, 1), 0)
    col = jnp.zeros((ATTN_REP * ATTN_WINDOW, 1), F32)
    for r in range(ATTN_REP):
        h = kh * ATTN_REP + r
        col = jnp.where((rows >= r * ATTN_WINDOW) & (rows < (r + 1) * ATTN_WINDOW), sink_ref[h:h + 1, 0:1], col)
    return col


def _attn_fwd(qkv, sinks_b, *, name):
    t = qkv.shape[0]
    w = ATTN_WINDOW
    nb = t // w
    prev = lambda n: jnp.maximum(n - 1, 0)

    def body(qlo_ref, qhi_ref, kc_ref, kp_ref, vc_ref, vp_ref, sink_ref, o_ref):
        first = pl.program_id(0) == 0
        q_lo, q_hi = qlo_ref[...], qhi_ref[...]
        k_cur, k_prev, v_cur, v_prev = kc_ref[...], kp_ref[...], vc_ref[...], vp_ref[...]
        for kh in range(ATTN_N_KV):
            qs, kb, vb = _attn_head_views(q_lo, q_hi, k_cur, k_prev, v_cur, v_prev, kh)
            p, _ = _attn_probs(qs, kb, _sink_col(sink_ref, kh), first)
            o = _dot_nn(p.astype(BF16), vb)
            for r in range(ATTN_REP):
                h = kh * ATTN_REP + r
                o_ref[:, pl.ds(h * ATTN_HEAD_DIM, ATTN_HEAD_DIM)] = o[r * w:(r + 1) * w, :].astype(o_ref.dtype)

    qh = lambda half: pl.BlockSpec((w, ATTN_Q_HALF), lambda n: (n, half))
    kv = lambda blk, idx: pl.BlockSpec((w, ATTN_KV_W), lambda n: (idx(n), blk))
    cur = lambda n: n
    return pl.pallas_call(
        body, grid=(nb,),
        in_specs=[qh(0), qh(1), kv(ATTN_K_BLK, cur), kv(ATTN_K_BLK, prev), kv(ATTN_V_BLK, cur), kv(ATTN_V_BLK, prev),
                  pl.BlockSpec((ATTN_N_Q, LANES), lambda n: (0, 0))],
        out_specs=pl.BlockSpec((w, D_MODEL), lambda n: (n, 0)),
        out_shape=jax.ShapeDtypeStruct((t, D_MODEL), BF16),
        compiler_params=_params("parallel"), name=name)(qkv, qkv, qkv, qkv, qkv, qkv, sinks_b)


def _attn_bwd(qkv, sinks_b, dout, *, name):
    t = qkv.shape[0]
    w = ATTN_WINDOW
    nb = t // w
    hd = ATTN_HEAD_DIM
    clamp = lambda n: jnp.minimum(n, nb - 1)
    prev = lambda n: jnp.maximum(clamp(n) - 1, 0)

    def body(qlo_ref, qhi_ref, kc_ref, kp_ref, vc_ref, vp_ref, sink_ref, dolo_ref, dohi_ref,
             dq_ref, dkv_ref, dsink_ref, carry):
        n = pl.program_id(0)

        @pl.when(n == 0)
        def _():
            carry[...] = jnp.zeros_like(carry)
            dsink_ref[...] = jnp.zeros_like(dsink_ref)

        @pl.when(n < nb)
        def _():
            first = n == 0
            q_lo, q_hi = qlo_ref[...], qhi_ref[...]
            do_lo, do_hi = dolo_ref[...], dohi_ref[...]
            k_cur, k_prev, v_cur, v_prev = kc_ref[...], kp_ref[...], vc_ref[...], vp_ref[...]
            rows = lax.broadcasted_iota(jnp.int32, (ATTN_REP * w, 1), 0)
            for kh in range(ATTN_N_KV):
                qs, kb, vb = _attn_head_views(q_lo, q_hi, k_cur, k_prev, v_cur, v_prev, kh)
                do_half = do_lo if kh < 2 else do_hi
                base = (kh % 2) * ATTN_REP * hd
                dos = jnp.concatenate([do_half[:, base + r * hd: base + (r + 1) * hd] for r in range(ATTN_REP)], axis=0)
                p, p_sink = _attn_probs(qs, kb, _sink_col(sink_ref, kh), first)
                dp = _dot_nt(dos, vb)
                delta = jnp.sum(p * dp, axis=1, keepdims=True)
                dsc = (p * (dp - delta) * (hd ** -0.5)).astype(BF16)
                dqs = _dot_nn(dsc, kb)
                dkb = _dot_tn(dsc, qs)
                dvb = _dot_tn(p.astype(BF16), dos)
                sink_g = -p_sink * delta
                for r in range(ATTN_REP):
                    h = kh * ATTN_REP + r
                    dq_ref[:, pl.ds(h * hd, hd)] = dqs[r * w:(r + 1) * w, :].astype(dq_ref.dtype)
                    in_r = (rows >= r * w) & (rows < (r + 1) * w)
                    dsink_ref[h:h + 1, :] += jnp.broadcast_to(
                        jnp.sum(jnp.where(in_r, sink_g, 0.0), axis=0, keepdims=True), (1, LANES))
                kcol = pl.ds(kh * hd, hd)
                vcol = pl.ds(ATTN_KV_W + kh * hd, hd)
                dkv_ref[:, kcol] = (carry[:, kcol] + dkb[0:w, :]).astype(dkv_ref.dtype)
                dkv_ref[:, vcol] = (carry[:, vcol] + dvb[0:w, :]).astype(dkv_ref.dtype)
                carry[:, kcol] = dkb[w:2 * w, :]
                carry[:, vcol] = dvb[w:2 * w, :]

        @pl.when(n == nb)
        def _():
            dkv_ref[...] = carry[...].astype(dkv_ref.dtype)

    qh = lambda half: pl.BlockSpec((w, ATTN_Q_HALF), lambda n: (clamp(n), half))
    kv = lambda blk, idx: pl.BlockSpec((w, ATTN_KV_W), lambda n: (idx(n), blk))
    return pl.pallas_call(
        body, grid=(nb + 1,),
        in_specs=[qh(0), qh(1), kv(ATTN_K_BLK, clamp), kv(ATTN_K_BLK, prev), kv(ATTN_V_BLK, clamp), kv(ATTN_V_BLK, prev),
                  pl.BlockSpec((ATTN_N_Q, LANES), lambda n: (0, 0)), qh(0), qh(1)],
        out_specs=[pl.BlockSpec((w, D_MODEL), lambda n: (clamp(n), 0)),
                   pl.BlockSpec((w, 2 * ATTN_KV_W), lambda n: (jnp.maximum(n - 1, 0), 0)),
                   pl.BlockSpec((ATTN_N_Q, LANES), lambda n: (0, 0))],
        out_shape=[jax.ShapeDtypeStruct((t, D_MODEL), BF16), jax.ShapeDtypeStruct((t, 2 * ATTN_KV_W), BF16),
                   jax.ShapeDtypeStruct((ATTN_N_Q, LANES), F32)],
        scratch_shapes=[pltpu.VMEM((w, 2 * ATTN_KV_W), F32)],
        compiler_params=_params("arbitrary"), name=name)(qkv, qkv, qkv, qkv, qkv, qkv, sinks_b, dout, dout)


def _sq_relu_epilogue(acc):
    r = jnp.maximum(acc, 0.0)
    return acc, r * r


def _sq_relu_bwd_epilogue(acc, pre):
    return (acc * (2.0 * jnp.maximum(pre, 0.0)),)


def _bias_epilogue(acc, bias):
    return (acc + bias,)


def _mlp_fwd(u, w_up, w_down, tag):
    pre, act = _matmul(u, w_up, mode="nn", out_dtypes=(F32, BF16), epilogue=_sq_relu_epilogue, b_shards=True,
                       name=f"mlp_up_{tag}")
    f = _matmul(act, w_down, mode="nn", out_dtypes=(F32,), name=f"mlp_down_{tag}")
    return pre, act, f


def _mlp_bwd(u, pre, act, w_up, w_down, df, tag):
    dpre = _matmul(df, w_down, mode="nt", out_dtypes=(BF16,), epilogue=_sq_relu_bwd_epilogue,
                   extras=((pre, "tile"),), name=f"mlp_dact_{tag}")
    dw_down = _matmul(act, df, mode="tn", out_dtypes=(BF16,), name=f"mlp_dwdown_{tag}")
    du = _matmul(dpre, w_up, mode="nt", out_dtypes=(F32,), b_shards=True, name=f"mlp_du_{tag}")
    dw_up = _matmul(u, dpre, mode="tn", out_dtypes=(BF16,), out_shards=True, name=f"mlp_dwup_{tag}")
    return du, dw_up, dw_down


def _group_dt_layouts(dt):
    t = dt.shape[0]
    d = dt[:, :SSD_N_HEADS].reshape(t, SSD_N_GROUPS, SSD_HPG)
    return jnp.transpose(d, (1, 0, 2)), jnp.transpose(d, (1, 2, 0))


def _head_param_rows(p):
    return jnp.broadcast_to(p.reshape(SSD_N_GROUPS, SSD_HPG, 1), (SSD_N_GROUPS, SSD_HPG, LANES))


def _local_step(x, target, wts):
    t = x.shape[0]
    row = lambda v: v.reshape(1, -1)
    mix_pre, mix_post, ffn_pre, ffn_post = wts["mix_pre_norm"], wts["mix_post_norm"], wts["ffn_pre_norm"], wts["ffn_post_norm"]

    u0 = _rms_fwd(x, row(mix_pre[0]), name="rms_pre_mix0")
    zx = _matmul(u0, wts["ssd_w_in"], mode="nn", out_dtypes=(F32,), tn=896, name="ssd_in_proj")
    xc = _conv_fwd(zx, wts["ssd_conv_w"], row(wts["ssd_conv_b"]), name="ssd_conv_fwd")
    bias_row = jnp.pad(wts["ssd_dt_bias"], (0, LANES - SSD_N_HEADS)).reshape(1, LANES)
    dt = _softplus_fwd(zx, bias_row, name="ssd_dt_fwd")
    dtc, dtr = _group_dt_layouts(dt)
    alog_b, d_b = _head_param_rows(wts["ssd_a_log"]), _head_param_rows(wts["ssd_d"])
    y_ssd, states = _ssd_fwd(xc, dtc, dtr, alog_b, d_b, name="ssd_scan_fwd")
    norm_w = row(wts["ssd_norm_w"])
    yn = _gate_norm_fwd(y_ssd, zx, norm_w, name="ssd_gate_norm_fwd")
    mix0 = _matmul(yn, wts["ssd_w_out"], mode="nn", out_dtypes=(F32,), name="ssd_out_proj")
    h1, v0 = _rms_fwd(mix0, row(mix_post[0]), resid=x, want_u=row(ffn_pre[0]), name="rms_post_mix0")
    pre0, act0, f0 = _mlp_fwd(v0, wts["mlp_w_up"][0], wts["mlp_w_down"][0], "l0")
    h2, u1 = _rms_fwd(f0, row(ffn_post[0]), resid=h1, want_u=row(mix_pre[1]), name="rms_post_ffn0")

    qkv = _matmul(u1, wts["attn_w_qkv"], mode="nn", out_dtypes=(BF16,), epilogue=_bias_epilogue,
                  extras=((row(wts["attn_b_qkv"]), "row"),), b_shards=True, name="attn_qkv_proj")
    sinks_b = jnp.broadcast_to(wts["attn_sinks"].reshape(ATTN_N_Q, 1), (ATTN_N_Q, LANES))
    ao = _attn_fwd(qkv, sinks_b, name="attn_fwd")
    mix1 = _matmul(ao, wts["attn_w_o"], mode="nn", out_dtypes=(F32,), epilogue=_bias_epilogue,
                   extras=((row(wts["attn_b_o"]), "row"),), name="attn_out_proj")
    h3, v1 = _rms_fwd(mix1, row(mix_post[1]), resid=h2, want_u=row(ffn_pre[1]), name="rms_post_mix1")
    pre1, act1, f1 = _mlp_fwd(v1, wts["mlp_w_up"][1], wts["mlp_w_down"][1], "l1")
    h4 = _rms_fwd(f1, row(ffn_post[1]), resid=h3, name="rms_post_ffn1")

    dh4, loss_tile = _loss_head(h4, target, name="loss_head")

    df1, g_ffn_post1 = _rms_bwd(f1, row(ffn_post[1]), dh4, out_dtype=BF16, name="rms_post_ffn1_bwd")
    dv1, g_up1, g_down1 = _mlp_bwd(v1, pre1, act1, wts["mlp_w_up"][1], wts["mlp_w_down"][1], df1, "l1")
    dh3, g_ffn_pre1 = _rms_bwd(h3, row(ffn_pre[1]), dv1, resid=dh4, name="rms_pre_ffn1_bwd")
    dmix1, g_mix_post1 = _rms_bwd(mix1, row(mix_post[1]), dh3, out_dtype=BF16, name="rms_post_mix1_bwd")
    g_b_o = _col_sum(dmix1, name="attn_bo_grad")
    g_w_o = _matmul(ao, dmix1, mode="tn", out_dtypes=(BF16,), name="attn_dwo")
    dao = _matmul(dmix1, wts["attn_w_o"], mode="nt", out_dtypes=(BF16,), name="attn_dao")
    dq, dkv, g_sinks = _attn_bwd(qkv, sinks_b, dao, name="attn_bwd")
    dqkv = jnp.concatenate([dq, dkv], axis=1)
    g_b_qkv = _col_sum(dqkv, name="attn_bqkv_grad")
    g_w_qkv = _matmul(u1, dqkv, mode="tn", out_dtypes=(BF16,), tn=ATTN_QKV // N_CHIPS, out_shards=True, name="attn_dwqkv")
    du1 = _matmul(dqkv, wts["attn_w_qkv"], mode="nt", out_dtypes=(F32,), b_shards=True, name="attn_du")
    dh2, g_mix_pre1 = _rms_bwd(h2, row(mix_pre[1]), du1, resid=dh3, name="rms_pre_mix1_bwd")

    df0, g_ffn_post0 = _rms_bwd(f0, row(ffn_post[0]), dh2, out_dtype=BF16, name="rms_post_ffn0_bwd")
    dv0, g_up0, g_down0 = _mlp_bwd(v0, pre0, act0, wts["mlp_w_up"][0], wts["mlp_w_down"][0], df0, "l0")
    dh1, g_ffn_pre0 = _rms_bwd(h1, row(ffn_pre[0]), dv0, resid=dh2, name="rms_pre_ffn0_bwd")
    dmix0, g_mix_post0 = _rms_bwd(mix0, row(mix_post[0]), dh1, out_dtype=BF16, name="rms_post_mix0_bwd")
    g_w_out = _matmul(yn, dmix0, mode="tn", out_dtypes=(BF16,), name="ssd_dwout")
    dyn = _matmul(dmix0, wts["ssd_w_out"], mode="nt", out_dtypes=(BF16,), name="ssd_dyn")
    dy_ssd, dz, g_norm_w = _gate_norm_bwd(y_ssd, zx, norm_w, dyn, name="ssd_gate_norm_bwd")
    dxc, dbm, dcm, ddt_r, dpar = _ssd_bwd(xc, dtc, dtr, alog_b, d_b, states, dy_ssd, name="ssd_scan_bwd")
    dxbc, g_conv_w, g_conv_b = _conv_bwd(zx, wts["ssd_conv_w"], row(wts["ssd_conv_b"]), dxc, dbm, dcm, name="ssd_conv_bwd")
    ddt = jnp.pad(jnp.transpose(ddt_r, (2, 0, 1)).reshape(t, SSD_N_HEADS), ((0, 0), (0, LANES - SSD_N_HEADS)))
    ddt_raw, g_dt_bias = _softplus_bwd(zx, bias_row, ddt, name="ssd_dt_bwd")
    dzx = jnp.concatenate([dz, dxbc, ddt_raw], axis=1)
    g_w_in = _w_in_to_shards(_matmul(u0, dzx, mode="tn", out_dtypes=(F32,), tn=896, name="ssd_dwin"), name="ssd_dwin_shards")
    du0 = _matmul(dzx, wts["ssd_w_in"], mode="nt", out_dtypes=(F32,), tk=896, name="ssd_du")
    grad_x, g_mix_pre0 = _rms_bwd(x, row(mix_pre[0]), du0, resid=dh1, name="rms_pre_mix0_bwd")

    dpar = dpar.reshape(SSD_N_HEADS, LANES)
    mats = {"ssd_w_in": g_w_in, "ssd_w_out": g_w_out, "attn_w_qkv": g_w_qkv, "attn_w_o": g_w_o,
            "mlp_w_up": (g_up0, g_up1), "mlp_w_down": (g_down0, g_down1)}
    vecs = {
        "ssd_conv_w": g_conv_w, "ssd_conv_b": g_conv_b.reshape(-1),
        "ssd_dt_bias": g_dt_bias[0, :SSD_N_HEADS], "ssd_a_log": dpar[:, 0], "ssd_d": dpar[:, 1],
        "ssd_norm_w": g_norm_w.reshape(-1), "attn_b_qkv": g_b_qkv.reshape(-1), "attn_sinks": g_sinks[:, 0],
        "attn_b_o": g_b_o.reshape(-1),
        "mix_pre_norm": jnp.concatenate([g_mix_pre0, g_mix_pre1]), "mix_post_norm": jnp.concatenate([g_mix_post0, g_mix_post1]),
        "ffn_pre_norm": jnp.concatenate([g_ffn_pre0, g_ffn_pre1]), "ffn_post_norm": jnp.concatenate([g_ffn_post0, g_ffn_post1]),
    }
    return loss_tile, grad_x, mats, vecs


ANY = pl.BlockSpec(memory_space=pl.ANY)


def _mesh_position():
    return lax.axis_index("x"), lax.axis_index("y"), lax.axis_index("c")


def _flip(v, bit):
    return 1 - v if bit else v


OTHER_CHIPS = ((1, 0), (0, 1), (1, 1))


def _comm_params():
    return pltpu.CompilerParams(vmem_limit_bytes=VMEM_LIMIT)


def _staged_copies(srcs, dsts, bufs, sems_in, sems_out):
    loads = [pltpu.make_async_copy(s, b, sems_in.at[i]) for i, (s, b) in enumerate(zip(srcs, bufs))]
    stores = [pltpu.make_async_copy(b, d, sems_out.at[i]) for i, (b, d) in enumerate(zip(bufs, dsts))]
    return loads, stores


def _all_gather_chips(mats, vecs, *, name):
    nm, nv = len(mats), len(vecs)
    n = nm + nv
    n_ici = (N_CHIPS - 1) * n
    n_fwd = (N_CHIPS - 1) * nm

    def body(*refs):
        ins, outs, bufs = refs[:n], refs[n:2 * n], refs[2 * n:3 * n]
        ici_send, ici_recv, fwd_send, fwd_recv, load_sems, store_sems = refs[3 * n:]
        xi, yi, ci = _mesh_position()
        me = 2 * xi + yi
        loads, stores = _staged_copies(ins, [outs[i].at[me] for i in range(n)], bufs, load_sems, store_sems)
        sends, landed, forwards, from_sibling = [], [], [], []
        for j, (bx, by) in enumerate(OTHER_CHIPS):
            px, py = _flip(xi, bx), _flip(yi, by)
            peer = 2 * px + py
            for i in range(n):
                k = j * n + i
                is_mat = i < nm
                mk = functools.partial(pltpu.make_async_remote_copy, send_sem=ici_send.at[k], recv_sem=ici_recv.at[k],
                                       device_id=(px, py, ci), device_id_type=MESH)
                if is_mat:
                    sends.append(mk(src_ref=ins[i].at[ci], dst_ref=outs[i].at[me, ci]))
                    landed.append(mk(src_ref=ins[i].at[ci], dst_ref=outs[i].at[peer, ci]))
                    kf = j * nm + i
                    fw = functools.partial(pltpu.make_async_remote_copy, send_sem=fwd_send.at[kf], recv_sem=fwd_recv.at[kf],
                                           device_id=(xi, yi, 1 - ci), device_id_type=MESH)
                    forwards.append(fw(src_ref=outs[i].at[peer, ci], dst_ref=outs[i].at[peer, ci]))
                    from_sibling.append(fw(src_ref=outs[i].at[peer, ci], dst_ref=outs[i].at[peer, 1 - ci]))
                else:
                    sends.append(mk(src_ref=ins[i], dst_ref=outs[i].at[me]))
                    landed.append(mk(src_ref=ins[i], dst_ref=outs[i].at[peer]))
                    forwards.append(None)
        for cp in loads + sends:
            cp.start()
        for ld, st in zip(loads, stores):
            ld.wait()
            st.start()
        for cp, fw in zip(landed, forwards):
            cp.wait_recv()
            if fw is not None:
                fw.start()
        for cp in from_sibling:
            cp.wait_recv()
        for cp in sends + [fw for fw in forwards if fw is not None]:
            cp.wait_send()
        for st in stores:
            st.wait()

    arrs = list(mats) + list(vecs)
    return pl.pallas_call(
        body, in_specs=[ANY] * n, out_specs=[ANY] * n,
        out_shape=[jax.ShapeDtypeStruct((N_CHIPS,) + a.shape, a.dtype) for a in arrs],
        scratch_shapes=[pltpu.VMEM(a.shape, a.dtype) for a in arrs]
        + [pltpu.SemaphoreType.DMA((n_ici,)), pltpu.SemaphoreType.DMA((n_ici,)),
           pltpu.SemaphoreType.DMA((n_fwd,)), pltpu.SemaphoreType.DMA((n_fwd,)),
           pltpu.SemaphoreType.DMA((n,)), pltpu.SemaphoreType.DMA((n,))],
        compiler_params=_comm_params(), name=name)(*arrs)


def _send_other_half(parts, *, name):
    n = len(parts)

    def body(*refs):
        ins, outs = refs[:n], refs[n:2 * n]
        send_sems, recv_sems = refs[2 * n:]
        xi, yi, ci = _mesh_position()
        sibling = (xi, yi, 1 - ci)
        for i in range(n):
            for s in range(N_CHIPS):
                pltpu.make_async_remote_copy(src_ref=ins[i].at[s, 1 - ci], dst_ref=outs[i].at[s], send_sem=send_sems.at[i],
                                             recv_sem=recv_sems.at[i], device_id=sibling, device_id_type=MESH).start()
        for i in range(n):
            pltpu.make_async_remote_copy(src_ref=outs[i], dst_ref=outs[i], send_sem=send_sems.at[i], recv_sem=recv_sems.at[i],
                                         device_id=sibling, device_id_type=MESH).wait()

    return pl.pallas_call(
        body, in_specs=[ANY] * n, out_specs=[ANY] * n,
        out_shape=[jax.ShapeDtypeStruct((p.shape[0],) + p.shape[2:], p.dtype) for p in parts],
        scratch_shapes=[pltpu.SemaphoreType.DMA((n,)), pltpu.SemaphoreType.DMA((n,))],
        name=name)(*parts)


ROW_BLOCKS = 8


def _add_sibling_half(parts, theirs, core, *, name):
    n = len(parts)

    def body(core_ref, *refs):
        for a_ref, b_ref, o_ref in zip(refs[:n], refs[n:2 * n], refs[2 * n:]):
            o_ref[...] = (a_ref[...].astype(F32) + b_ref[...].astype(F32)).astype(o_ref.dtype)

    mine = lambda p: pl.BlockSpec((None, None, p.shape[2] // ROW_BLOCKS, p.shape[3]), lambda s, rb, core_ref: (s, core_ref[0], rb, 0))
    other = lambda p: pl.BlockSpec((None, p.shape[1] // ROW_BLOCKS, p.shape[2]), lambda s, rb, core_ref: (s, rb, 0))
    return pl.pallas_call(
        body,
        grid_spec=pltpu.PrefetchScalarGridSpec(
            num_scalar_prefetch=1, grid=(N_CHIPS, ROW_BLOCKS),
            in_specs=[mine(p) for p in parts] + [other(q) for q in theirs], out_specs=[other(q) for q in theirs]),
        out_shape=[jax.ShapeDtypeStruct(q.shape, BF16) for q in theirs],
        compiler_params=_params("parallel", "parallel"), name=name)(core, *parts, *theirs)


def _grad_exchange(parts, small, *, name):
    n = len(parts)
    n_ici = (N_CHIPS - 1) * n
    n_peer = N_DEV - 1

    def body(*refs):
        ins, small_ref = refs[:n], refs[n]
        outs, small_all_ref = refs[n + 1:2 * n + 1], refs[2 * n + 1]
        bufs = refs[2 * n + 2:3 * n + 3]
        send_sems, recv_sems, small_send, small_recv, load_sems, store_sems = refs[3 * n + 3:]
        xi, yi, ci = _mesh_position()
        me_chip = 2 * xi + yi
        me = 4 * xi + 2 * yi + ci
        loads, stores = _staged_copies([ins[i].at[me_chip] for i in range(n)] + [small_ref],
                                       [outs[i].at[me_chip] for i in range(n)] + [small_all_ref.at[me]],
                                       bufs, load_sems, store_sems)
        sends, recvs = [], []
        for j, (bx, by) in enumerate(OTHER_CHIPS):
            px, py = _flip(xi, bx), _flip(yi, by)
            peer = 2 * px + py
            for i in range(n):
                k = j * n + i
                mk = functools.partial(pltpu.make_async_remote_copy, src_ref=ins[i].at[peer], send_sem=send_sems.at[k],
                                       recv_sem=recv_sems.at[k], device_id=(px, py, ci), device_id_type=MESH)
                sends.append(mk(dst_ref=outs[i].at[me_chip]))
                recvs.append(mk(dst_ref=outs[i].at[peer]))
        for k in range(1, N_DEV):
            px, py, pc = _flip(xi, (k >> 2) & 1), _flip(yi, (k >> 1) & 1), _flip(ci, k & 1)
            mk = functools.partial(pltpu.make_async_remote_copy, src_ref=small_ref, send_sem=small_send.at[k - 1],
                                   recv_sem=small_recv.at[k - 1], device_id=(px, py, pc), device_id_type=MESH)
            sends.append(mk(dst_ref=small_all_ref.at[me]))
            recvs.append(mk(dst_ref=small_all_ref.at[4 * px + 2 * py + pc]))
        for cp in loads + sends:
            cp.start()
        for ld, st in zip(loads, stores):
            ld.wait()
            st.start()
        for cp in recvs:
            cp.wait_recv()
        for cp in sends:
            cp.wait_send()
        for st in stores:
            st.wait()

    return pl.pallas_call(
        body, in_specs=[ANY] * (n + 1), out_specs=[ANY] * (n + 1),
        out_shape=[jax.ShapeDtypeStruct(p.shape, p.dtype) for p in parts]
        + [jax.ShapeDtypeStruct((N_DEV,) + small.shape, small.dtype)],
        scratch_shapes=[pltpu.VMEM(p.shape[1:], p.dtype) for p in parts] + [pltpu.VMEM(small.shape, small.dtype)]
        + [pltpu.SemaphoreType.DMA((n_ici,)), pltpu.SemaphoreType.DMA((n_ici,)),
           pltpu.SemaphoreType.DMA((n_peer,)), pltpu.SemaphoreType.DMA((n_peer,)),
           pltpu.SemaphoreType.DMA((n + 1,)), pltpu.SemaphoreType.DMA((n + 1,))],
        compiler_params=_comm_params(), name=name)(*parts, small)


def _sum_chips(parts, *, name):
    n = len(parts)
    p = parts[0].shape[0]

    def body(*refs):
        s = pl.program_id(1)
        for x_ref, o_ref in zip(refs[:n], refs[n:]):
            @pl.when(s == 0)
            def _():
                o_ref[...] = x_ref[...].astype(F32)

            @pl.when(s > 0)
            def _():
                o_ref[...] += x_ref[...].astype(F32)

    blocks = lambda q: ROW_BLOCKS if q.shape[1] % (8 * ROW_BLOCKS) == 0 else 1
    assert len({blocks(q) for q in parts}) == 1
    nb = blocks(parts[0])
    return pl.pallas_call(
        body, grid=(nb, p),
        in_specs=[pl.BlockSpec((None, q.shape[1] // nb, q.shape[2]), lambda rb, s: (s, rb, 0)) for q in parts],
        out_specs=[pl.BlockSpec((q.shape[1] // nb, q.shape[2]), lambda rb, s: (rb, 0)) for q in parts],
        out_shape=[jax.ShapeDtypeStruct(q.shape[1:], F32) for q in parts],
        compiler_params=_params("parallel", "arbitrary"), name=name)(*parts)


def _swap_halves(halves, layers, *, name):
    n = len(halves)
    out_shapes, slots = [], []
    for i, h in enumerate(halves):
        pair = [p for p in layers if i in p]
        if pair and pair[0][1] == i:
            slots.append((slots[pair[0][0]][0], 1))
        elif pair:
            out_shapes.append(jax.ShapeDtypeStruct((2, 2) + h.shape, h.dtype))
            slots.append((len(out_shapes) - 1, 0))
        else:
            out_shapes.append(jax.ShapeDtypeStruct((2,) + h.shape, h.dtype))
            slots.append((len(out_shapes) - 1, None))
    n_out = len(out_shapes)

    def body(*refs):
        ins, outs, bufs = refs[:n], refs[n:n + n_out], refs[n + n_out:2 * n + n_out]
        send_sems, recv_sems, load_sems, store_sems = refs[2 * n + n_out:]
        xi, yi, ci = _mesh_position()
        own, sends, recvs = [], [], []
        for i in range(n):
            o, layer = slots[i]
            dst = (lambda core: outs[o].at[core]) if layer is None else (lambda core: outs[o].at[layer, core])
            own.append(dst(ci))
            mk = functools.partial(pltpu.make_async_remote_copy, src_ref=ins[i], send_sem=send_sems.at[i],
                                   recv_sem=recv_sems.at[i], device_id=(xi, yi, 1 - ci), device_id_type=MESH)
            sends.append(mk(dst_ref=dst(ci)))
            recvs.append(mk(dst_ref=dst(1 - ci)))
        loads, stores = _staged_copies(ins, own, bufs, load_sems, store_sems)
        for cp in loads + sends:
            cp.start()
        for ld, st in zip(loads, stores):
            ld.wait()
            st.start()
        for cp in recvs:
            cp.wait_recv()
        for cp in sends:
            cp.wait_send()
        for st in stores:
            st.wait()

    return pl.pallas_call(
        body, in_specs=[ANY] * n, out_specs=[ANY] * n_out, out_shape=out_shapes,
        scratch_shapes=[pltpu.VMEM(h.shape, h.dtype) for h in halves]
        + [pltpu.SemaphoreType.DMA((n,)), pltpu.SemaphoreType.DMA((n,)), pltpu.SemaphoreType.DMA((n,)), pltpu.SemaphoreType.DMA((n,))],
        compiler_params=_comm_params(), name=name)(*halves)


def _adamw(w, g, m, v, *, name):
    r, c = w.shape
    tr = 256 if r % 256 == 0 else r
    blk = pl.BlockSpec((tr, c), lambda i: (i, 0))

    def body(w_ref, g_ref, m_ref, v_ref, d_ref, nm_ref, nv_ref):
        gv = g_ref[...]
        nm = ADAM_B1 * m_ref[...] + (1.0 - ADAM_B1) * gv
        nv = ADAM_B2 * v_ref[...] + (1.0 - ADAM_B2) * (gv * gv)
        m_hat = nm / (1.0 - ADAM_B1 ** ADAM_STEP)
        v_hat = nv / (1.0 - ADAM_B2 ** ADAM_STEP)
        d_ref[...] = -ADAM_LR * (m_hat / (jnp.sqrt(v_hat) + ADAM_EPS) + ADAM_WD * w_ref[...])
        nm_ref[...] = nm
        nv_ref[...] = nv

    sh = jax.ShapeDtypeStruct((r, c), F32)
    return pl.pallas_call(body, grid=(r // tr,), in_specs=[blk] * 4, out_specs=[blk] * 3, out_shape=[sh] * 3,
                          compiler_params=_params("parallel"), name=name)(w, g, m, v)


SM_CONV_B, SM_NORM_W, SM_MIX_PRE, SM_MIX_POST, SM_FFN_PRE, SM_FFN_POST, SM_MISC, SM_CONV_W, SM_B_QKV, SM_B_O = 0, 4, 6, 8, 10, 12, 14, 16, 32, 34
SM_ROWS = 40
MISC_DT_BIAS, MISC_A_LOG, MISC_D, MISC_SINKS, MISC_LOSS = 0, 32, 64, 96, 112


def _shard_halves(a):
    c = a.shape[-1]
    return a.reshape(N_CHIPS, 2, -1, c)


def _rows(v):
    return v.reshape(-1, D_MODEL)


def _misc_row(dt_bias, a_log, d, sinks, loss):
    pad = jnp.zeros((D_MODEL - MISC_LOSS - 1,), F32)
    return jnp.concatenate([dt_bias.reshape(-1), a_log.reshape(-1), d.reshape(-1), sinks.reshape(-1), loss.reshape(1), pad]).reshape(1, D_MODEL)


def _replicated_rows(p, loss):
    return jnp.concatenate([
        _rows(p["ssd_conv_b"]), _rows(p["ssd_norm_w"]), _rows(p["mix_pre_norm"]), _rows(p["mix_post_norm"]),
        _rows(p["ffn_pre_norm"]), _rows(p["ffn_post_norm"]),
        _misc_row(p["ssd_dt_bias"], p["ssd_a_log"], p["ssd_d"], p["attn_sinks"], loss), jnp.zeros((1, D_MODEL), F32)], axis=0)


def _sharded_rows(conv_w, b_qkv, b_o):
    last = jnp.concatenate([b_qkv.reshape(-1), b_o.reshape(-1), jnp.zeros((D_MODEL - 640,), F32)]).reshape(1, D_MODEL)
    return jnp.concatenate([conv_w.reshape(SSD_CONV_WIDTH, D_MODEL), last, jnp.zeros((3, D_MODEL), F32)], axis=0)


REPLICATED = ("ssd_conv_b", "ssd_dt_bias", "ssd_a_log", "ssd_d", "ssd_norm_w", "attn_sinks",
              "mix_pre_norm", "mix_post_norm", "ffn_pre_norm", "ffn_post_norm")
MATRICES = ("ssd_w_in", "ssd_w_out", "attn_w_qkv", "attn_w_o", "mlp_w_up", "mlp_w_down")
WEIGHT_NAMES = ("ssd_w_in", "ssd_conv_w", "ssd_conv_b", "ssd_dt_bias", "ssd_a_log", "ssd_d", "ssd_norm_w", "ssd_w_out",
                "attn_w_qkv", "attn_b_qkv", "attn_sinks", "attn_w_o", "attn_b_o", "mlp_w_up", "mlp_w_down",
                "mix_pre_norm", "mix_post_norm", "ffn_pre_norm", "ffn_post_norm")


def _unpack_small(rows16, rows8, like):
    misc = rows16[SM_MISC]
    out = {
        "ssd_conv_b": rows16[SM_CONV_B:SM_CONV_B + 4], "ssd_norm_w": rows16[SM_NORM_W:SM_NORM_W + 2],
        "mix_pre_norm": rows16[SM_MIX_PRE:SM_MIX_PRE + 2], "mix_post_norm": rows16[SM_MIX_POST:SM_MIX_POST + 2],
        "ffn_pre_norm": rows16[SM_FFN_PRE:SM_FFN_PRE + 2], "ffn_post_norm": rows16[SM_FFN_POST:SM_FFN_POST + 2],
        "ssd_dt_bias": misc[MISC_DT_BIAS:MISC_DT_BIAS + 32], "ssd_a_log": misc[MISC_A_LOG:MISC_A_LOG + 32],
        "ssd_d": misc[MISC_D:MISC_D + 32], "attn_sinks": misc[MISC_SINKS:MISC_SINKS + 16],
        "ssd_conv_w": rows8[0:SSD_CONV_WIDTH], "attn_b_qkv": rows8[SSD_CONV_WIDTH, 0:384], "attn_b_o": rows8[SSD_CONV_WIDTH, 384:640],
    }
    return {k: v.reshape(like[k].shape) for k, v in out.items()}


def kernel(x, ssd_w_in, ssd_conv_w, ssd_conv_b, ssd_dt_bias, ssd_a_log, ssd_d, ssd_norm_w, ssd_w_out, attn_w_qkv, attn_b_qkv, attn_sinks, attn_w_o, attn_b_o, mlp_w_up, mlp_w_down, mix_pre_norm, mix_post_norm, ffn_pre_norm, ffn_post_norm, loss_target, m_ssd_w_in, m_ssd_conv_w, m_ssd_conv_b, m_ssd_dt_bias, m_ssd_a_log, m_ssd_d, m_ssd_norm_w, m_ssd_w_out, m_attn_w_qkv, m_attn_b_qkv, m_attn_sinks, m_attn_w_o, m_attn_b_o, m_mlp_w_up, m_mlp_w_down, m_mix_pre_norm, m_mix_post_norm, m_ffn_pre_norm, m_ffn_post_norm, v_ssd_w_in, v_ssd_conv_w, v_ssd_conv_b, v_ssd_dt_bias, v_ssd_a_log, v_ssd_d, v_ssd_norm_w, v_ssd_w_out, v_attn_w_qkv, v_attn_b_qkv, v_attn_sinks, v_attn_w_o, v_attn_b_o, v_mlp_w_up, v_mlp_w_down, v_mix_pre_norm, v_mix_post_norm, v_ffn_pre_norm, v_ffn_post_norm):
    w = dict(zip(WEIGHT_NAMES, (ssd_w_in, ssd_conv_w, ssd_conv_b, ssd_dt_bias, ssd_a_log, ssd_d, ssd_norm_w, ssd_w_out, attn_w_qkv, attn_b_qkv, attn_sinks, attn_w_o, attn_b_o, mlp_w_up, mlp_w_down, mix_pre_norm, mix_post_norm, ffn_pre_norm, ffn_post_norm)))
    m = dict(zip(WEIGHT_NAMES, (m_ssd_w_in, m_ssd_conv_w, m_ssd_conv_b, m_ssd_dt_bias, m_ssd_a_log, m_ssd_d, m_ssd_norm_w, m_ssd_w_out, m_attn_w_qkv, m_attn_b_qkv, m_attn_sinks, m_attn_w_o, m_attn_b_o, m_mlp_w_up, m_mlp_w_down, m_mix_pre_norm, m_mix_post_norm, m_ffn_pre_norm, m_ffn_post_norm)))
    v = dict(zip(WEIGHT_NAMES, (v_ssd_w_in, v_ssd_conv_w, v_ssd_conv_b, v_ssd_dt_bias, v_ssd_a_log, v_ssd_d, v_ssd_norm_w, v_ssd_w_out, v_attn_w_qkv, v_attn_b_qkv, v_attn_sinks, v_attn_w_o, v_attn_b_o, v_mlp_w_up, v_mlp_w_down, v_mix_pre_norm, v_mix_post_norm, v_ffn_pre_norm, v_ffn_post_norm)))
    chip = 2 * lax.axis_index("x") + lax.axis_index("y")

    two_halves = lambda a: a.astype(BF16).reshape(2, a.shape[0] // 2, a.shape[1])
    mat_shards = [two_halves(a) for a in (w["ssd_w_in"][0], w["ssd_w_out"][0], w["attn_w_qkv"][0], w["attn_w_o"][0],
                                          w["mlp_w_up"][0], w["mlp_w_up"][1], w["mlp_w_down"][0], w["mlp_w_down"][1])]
    g_in, g_out, g_qkv, g_o, g_up0, g_up1, g_down0, g_down1, g_conv, g_bqkv, g_bo = _all_gather_chips(
        mat_shards, [w["ssd_conv_w"][0], w["attn_b_qkv"], w["attn_b_o"]], name="weight_all_gather")
    whole = lambda a: a.reshape(N_CHIPS, 2 * a.shape[2], a.shape[3])
    full = {
        "ssd_w_in": _w_in_from_shards(whole(g_in), name="ssd_w_in_unshard"),
        "ssd_w_out": g_out.reshape(SSD_D_INNER, D_MODEL),
        "attn_w_qkv": whole(g_qkv),
        "attn_w_o": g_o.reshape(D_MODEL, D_MODEL),
        "mlp_w_up": (whole(g_up0), whole(g_up1)),
        "mlp_w_down": (g_down0.reshape(D_FF, D_MODEL), g_down1.reshape(D_FF, D_MODEL)),
        "ssd_conv_w": g_conv.transpose(1, 0, 2).reshape(SSD_CONV_WIDTH, SSD_CONV_DIM),
        "attn_b_qkv": g_bqkv.reshape(ATTN_QKV), "attn_b_o": g_bo.reshape(D_MODEL),
    }
    for name in REPLICATED:
        full[name] = w[name][0] if name.startswith(("ssd_", "attn_")) else w[name]

    loss_tile, grad_x, gm, g = _local_step(x[0], loss_target[0], full)

    parts = [_shard_halves(a) for a in (gm["ssd_w_in"], gm["ssd_w_out"], gm["attn_w_qkv"], gm["attn_w_o"],
                                        gm["mlp_w_up"][0], gm["mlp_w_up"][1], gm["mlp_w_down"][0], gm["mlp_w_down"][1])]
    core = lax.axis_index("c").astype(jnp.int32).reshape(1)
    theirs = _send_other_half(parts, name="grad_sibling_send")
    chip_sums = _add_sibling_half(parts, theirs, core, name="grad_chip_sum")
    conv_w_rows = g["ssd_conv_w"].reshape(SSD_CONV_WIDTH * N_CHIPS, D_MODEL)
    b_qkv_rows = jnp.pad(g["attn_b_qkv"], (0, 2 * D_MODEL - ATTN_QKV)).reshape(2, D_MODEL)
    small = jnp.concatenate([_replicated_rows(g, loss_tile[0, 0]), conv_w_rows, b_qkv_rows, _rows(g["attn_b_o"]),
                             jnp.zeros((SM_ROWS - SM_B_O - 1, D_MODEL), F32)], axis=0)
    *recv, small_all = _grad_exchange(chip_sums, small, name="grad_exchange")
    halves = _sum_chips(recv, name="grad_sum")
    r_in, r_out, r_qkv, r_o, r_up, r_down = _swap_halves(halves, layers=((4, 5), (6, 7)), name="grad_halves_swap")
    small_sum, = _sum_chips([small_all], name="small_grad_sum")

    grads = {"ssd_w_in": r_in, "ssd_w_out": r_out, "attn_w_qkv": r_qkv, "attn_w_o": r_o, "mlp_w_up": r_up, "mlp_w_down": r_down}
    grads = {k: a.reshape(w[k].shape) for k, a in grads.items()}
    conv_w_g = lax.dynamic_index_in_dim(small_sum[SM_CONV_W:SM_CONV_W + 16].reshape(SSD_CONV_WIDTH, N_CHIPS, D_MODEL), chip, axis=1, keepdims=False)
    b_qkv_g = lax.dynamic_slice_in_dim(small_sum[SM_B_QKV:SM_B_QKV + 2].reshape(-1), chip * 384, 384)
    b_o_g = lax.dynamic_slice_in_dim(small_sum[SM_B_O], chip * 256, 256)
    small_g = jnp.concatenate([small_sum[0:16], _sharded_rows(conv_w_g, b_qkv_g, b_o_g)], axis=0)
    grads.update(_unpack_small(small_g[0:16], small_g[16:24], w))
    loss = small_sum[SM_MISC, MISC_LOSS]

    delta, new_m, new_v = {}, {}, {}
    for name in MATRICES:
        shape = w[name].shape
        as2d = lambda a: a.reshape(-1, shape[-1])
        d2, m2, v2 = _adamw(as2d(w[name]), as2d(grads[name]), as2d(m[name]), as2d(v[name]), name=f"adamw_{name}")
        delta[name], new_m[name], new_v[name] = d2.reshape(shape), m2.reshape(shape), v2.reshape(shape)
    zero = jnp.zeros((), F32)
    small_pack = lambda p: jnp.concatenate([_replicated_rows({k: p[k] for k in REPLICATED}, zero),
                                            _sharded_rows(p["ssd_conv_w"], p["attn_b_qkv"], p["attn_b_o"])], axis=0)
    d_s, m_s, v_s = _adamw(small_pack(w), small_g, small_pack(m), small_pack(v), name="adamw_vectors")
    delta.update(_unpack_small(d_s[0:16], d_s[16:24], w))
    new_m.update(_unpack_small(m_s[0:16], m_s[16:24], w))
    new_v.update(_unpack_small(v_s[0:16], v_s[16:24], w))

    return (loss, grad_x[None], *[grads[n] for n in WEIGHT_NAMES], *[delta[n] for n in WEIGHT_NAMES],
            *[new_m[n] for n in WEIGHT_NAMES], *[new_v[n] for n in WEIGHT_NAMES])
```

```python
import functools
import math

import jax
import jax.numpy as jnp
from jax import lax
from jax.experimental import pallas as pl
from jax.experimental.pallas import tpu as pltpu

F32 = jnp.float32
BF16 = jnp.bfloat16

D_MODEL = 1024
SSD_D_INNER = 2048
SSD_HEAD_DIM = 64
SSD_N_HEADS = 32
SSD_N_GROUPS = 8
SSD_HPG = 4
SSD_D_STATE = 128
SSD_CONV_WIDTH = 4
SSD_CHUNK = 128
SSD_CONV_DIM = 4096
SSD_IN_DIM = 6176
SSD_IN_PAD = 6272
SSD_GW = SSD_HPG * SSD_HEAD_DIM
ATTN_HEAD_DIM = 64
ATTN_N_Q = 16
ATTN_N_KV = 4
ATTN_REP = 4
ATTN_WINDOW = 128
ATTN_QKV = 1536
D_FF = 4096
NORM_EPS = 1e-6

ADAM_LR = 0.001
ADAM_B1 = 0.9
ADAM_B2 = 0.999
ADAM_EPS = 1e-08
ADAM_WD = 0.01
ADAM_STEP = 10

N_CHIPS = 4
N_DEV = 8
LANES = 128
VMEM_LIMIT = 48 * 1024 * 1024

MESH = pl.DeviceIdType.MESH


def _params(*sem):
    return pltpu.CompilerParams(dimension_semantics=sem, vmem_limit_bytes=VMEM_LIMIT)


def _dot(a, b, dims):
    return lax.dot_general(a, b, (dims, ((), ())), preferred_element_type=F32)


def _dot_nn(a, b):
    return _dot(a, b, ((1,), (0,)))


def _dot_nt(a, b):
    return _dot(a, b, ((1,), (1,)))


def _dot_tn(a, b):
    return _dot(a, b, ((0,), (0,)))


def _sigmoid(x):
    return 1.0 / (1.0 + jnp.exp(-x))


def _matmul(a, b, *, mode, out_dtypes, name, epilogue=None, extras=(), tm=1024, tn=1024, tk=1024,
            b_shards=False, out_shards=False):
    if b_shards:
        s, b_rows, b_cols = b.shape
        b2 = (b_rows, s * b_cols)
        if mode == "nn":
            tn = b_cols
        else:
            assert mode == "nt"
            tk = b_cols
    else:
        b2 = b.shape
    if mode == "nn":
        (m, k), (k2, n) = a.shape, b2
    elif mode == "nt":
        (m, k), (n, k2) = a.shape, b2
    else:
        (k, m), (k2, n) = a.shape, b2
    assert k == k2, (a.shape, b.shape, mode)
    tm, tn, tk = min(tm, m), min(tn, n), min(tk, k)
    assert m % tm == 0 and n % tn == 0 and k % tk == 0, (m, n, k, tm, tn, tk)
    nk = k // tk
    if mode == "tn":
        a_spec = pl.BlockSpec((tk, tm), lambda i, j, kk: (kk, i))
    else:
        a_spec = pl.BlockSpec((tm, tk), lambda i, j, kk: (i, kk))
    if b_shards and mode == "nn":
        b_spec = pl.BlockSpec((None, tk, tn), lambda i, j, kk: (j, kk, 0))
    elif b_shards:
        b_spec = pl.BlockSpec((None, tn, tk), lambda i, j, kk: (kk, j, 0))
    elif mode == "nt":
        b_spec = pl.BlockSpec((tn, tk), lambda i, j, kk: (j, kk))
    else:
        b_spec = pl.BlockSpec((tk, tn), lambda i, j, kk: (kk, j))
    dims = {"nn": ((1,), (0,)), "nt": ((1,), (1,)), "tn": ((0,), (0,))}[mode]
    ex_specs = []
    for arr, kind in extras:
        if kind == "tile":
            ex_specs.append(pl.BlockSpec((tm, tn), lambda i, j, kk: (i, j)))
        else:
            ex_specs.append(pl.BlockSpec((1, tn), lambda i, j, kk: (0, j)))
    n_ex, n_out = len(extras), len(out_dtypes)
    if epilogue is None:
        epilogue = lambda acc: (acc,)

    def body(a_ref, b_ref, *rest):
        ex = rest[:n_ex]
        outs = rest[n_ex:n_ex + n_out]

        def finish(acc):
            res = epilogue(acc, *[e[...] for e in ex])
            for o, r in zip(outs, res):
                o[...] = r.astype(o.dtype)

        if nk == 1:
            finish(_dot(a_ref[...], b_ref[...], dims))
        else:
            acc_ref = rest[-1]
            kk = pl.program_id(2)

            @pl.when(kk == 0)
            def _():
                acc_ref[...] = jnp.zeros_like(acc_ref)

            acc_ref[...] += _dot(a_ref[...], b_ref[...], dims)

            @pl.when(kk == nk - 1)
            def _():
                finish(acc_ref[...])

    if out_shards:
        out_spec = pl.BlockSpec((None, tm, tn), lambda i, j, kk: (j, i, 0))
        out_dims = (n // tn, m, tn)
    else:
        out_spec = pl.BlockSpec((tm, tn), lambda i, j, kk: (i, j))
        out_dims = (m, n)
    outs = pl.pallas_call(
        body,
        grid=(m // tm, n // tn, nk),
        in_specs=[a_spec, b_spec] + ex_specs,
        out_specs=[out_spec for _ in out_dtypes],
        out_shape=[jax.ShapeDtypeStruct(out_dims, dt) for dt in out_dtypes],
        scratch_shapes=[] if nk == 1 else [pltpu.VMEM((tm, tn), F32)],
        compiler_params=_params("parallel", "parallel", "arbitrary"),
        name=name,
    )(a, b, *[arr for arr, _ in extras])
    return outs[0] if n_out == 1 else outs


def _row_tile(t, want):
    return min(t, want)


def _rms_fwd(x, w, *, name, resid=None, want_u=None):
    t, d = x.shape
    tr = _row_tile(t, 512)

    def norm(v, wv):
        return v * lax.rsqrt(jnp.mean(v * v, axis=-1, keepdims=True) + NORM_EPS) * wv

    row = pl.BlockSpec((tr, d), lambda i: (i, 0))
    vec = pl.BlockSpec((1, d), lambda i: (0, 0))
    if resid is None:
        def body(x_ref, w_ref, o_ref):
            o_ref[...] = norm(x_ref[...], w_ref[...]).astype(BF16)
        ins, in_specs = (x, w), [row, vec]
        out_shape, out_specs = jax.ShapeDtypeStruct((t, d), BF16), row
    elif want_u is None:
        def body(x_ref, w_ref, r_ref, o_ref):
            o_ref[...] = r_ref[...] + norm(x_ref[...], w_ref[...])
        ins, in_specs = (x, w, resid), [row, vec, row]
        out_shape, out_specs = jax.ShapeDtypeStruct((t, d), F32), row
    else:
        def body(x_ref, w_ref, r_ref, w2_ref, o_ref, u_ref):
            h = r_ref[...] + norm(x_ref[...], w_ref[...])
            o_ref[...] = h
            u_ref[...] = norm(h, w2_ref[...]).astype(BF16)
        ins, in_specs = (x, w, resid, want_u), [row, vec, row, vec]
        out_shape = [jax.ShapeDtypeStruct((t, d), F32), jax.ShapeDtypeStruct((t, d), BF16)]
        out_specs = [row, row]
    return pl.pallas_call(body, grid=(t // tr,), in_specs=in_specs, out_specs=out_specs, out_shape=out_shape,
                          compiler_params=_params("parallel"), name=name)(*ins)


def _rms_bwd(x, w, dy, *, name, resid=None, out_dtype=F32):
    t, d = x.shape
    tr = _row_tile(t, 512)
    row = pl.BlockSpec((tr, d), lambda i: (i, 0))
    vec = pl.BlockSpec((1, d), lambda i: (0, 0))
    has_res = resid is not None

    def body(x_ref, w_ref, dy_ref, *rest):
        if has_res:
            r_ref, dx_ref, dw_ref = rest
        else:
            dx_ref, dw_ref = rest
        xv = x_ref[...]
        dyv = dy_ref[...].astype(F32)
        r = lax.rsqrt(jnp.mean(xv * xv, axis=-1, keepdims=True) + NORM_EPS)
        xhat = xv * r
        dyw = dyv * w_ref[...]
        dx = r * (dyw - xhat * jnp.mean(dyw * xhat, axis=-1, keepdims=True))
        if has_res:
            dx = dx + r_ref[...]
        dx_ref[...] = dx.astype(dx_ref.dtype)

        @pl.when(pl.program_id(0) == 0)
        def _():
            dw_ref[...] = jnp.zeros_like(dw_ref)

        dw_ref[...] += jnp.sum(dyv * xhat, axis=0, keepdims=True)

    ins = (x, w, dy) + ((resid,) if has_res else ())
    in_specs = [row, vec, row] + ([row] if has_res else [])
    return pl.pallas_call(
        body, grid=(t // tr,), in_specs=in_specs, out_specs=[row, vec],
        out_shape=[jax.ShapeDtypeStruct((t, d), out_dtype), jax.ShapeDtypeStruct((1, d), F32)],
        compiler_params=_params("arbitrary"), name=name)(*ins)


def _loss_head(h, target, *, name):
    t, d = h.shape
    tr = _row_tile(t, 512)
    row = pl.BlockSpec((tr, d), lambda i: (i, 0))

    def body(h_ref, t_ref, dh_ref, loss_ref):
        err = h_ref[...] - t_ref[...]
        dh_ref[...] = err * (1.0 / d)

        @pl.when(pl.program_id(0) == 0)
        def _():
            loss_ref[...] = jnp.zeros_like(loss_ref)

        part = jnp.sum(jnp.sum(err * err, axis=1, keepdims=True), axis=0, keepdims=True) * (0.5 / d)
        loss_ref[...] += jnp.broadcast_to(part, loss_ref.shape)

    return pl.pallas_call(
        body, grid=(t // tr,), in_specs=[row, row],
        out_specs=[row, pl.BlockSpec((8, LANES), lambda i: (0, 0))],
        out_shape=[jax.ShapeDtypeStruct((t, d), F32), jax.ShapeDtypeStruct((8, LANES), F32)],
        compiler_params=_params("arbitrary"), name=name)(h, target)


def _col_sum(x, *, name):
    t, n = x.shape
    tr = _row_tile(t, 512)

    def body(x_ref, o_ref):
        @pl.when(pl.program_id(0) == 0)
        def _():
            o_ref[...] = jnp.zeros_like(o_ref)

        o_ref[...] += jnp.sum(x_ref[...].astype(F32), axis=0, keepdims=True)

    return pl.pallas_call(
        body, grid=(t // tr,), in_specs=[pl.BlockSpec((tr, n), lambda i: (i, 0))],
        out_specs=pl.BlockSpec((1, n), lambda i: (0, 0)), out_shape=jax.ShapeDtypeStruct((1, n), F32),
        compiler_params=_params("arbitrary"), name=name)(x)


SSD_IN_SHARD = SSD_IN_DIM // N_CHIPS


def _w_in_from_shards(shards, *, name):
    d = shards.shape[1]
    tr = 256

    def body(s_ref, o_ref):
        o_ref[:, pl.ds(SSD_IN_PAD - LANES, LANES)] = jnp.zeros((tr, LANES), o_ref.dtype)
        for s in range(N_CHIPS):
            o_ref[:, pl.ds(SSD_IN_SHARD * s, SSD_IN_SHARD)] = s_ref[s]

    return pl.pallas_call(
        body, grid=(d // tr,), in_specs=[pl.BlockSpec((N_CHIPS, tr, SSD_IN_SHARD), lambda i: (0, i, 0))],
        out_specs=pl.BlockSpec((tr, SSD_IN_PAD), lambda i: (i, 0)),
        out_shape=jax.ShapeDtypeStruct((d, SSD_IN_PAD), shards.dtype),
        compiler_params=_params("parallel"), name=name)(shards)


def _w_in_to_shards(g, *, name):
    d = g.shape[0]
    tr = 256

    def body(g_ref, o_ref):
        for s in range(N_CHIPS):
            o_ref[s] = g_ref[:, pl.ds(SSD_IN_SHARD * s, SSD_IN_SHARD)].astype(o_ref.dtype)

    return pl.pallas_call(
        body, grid=(d // tr,), in_specs=[pl.BlockSpec((tr, SSD_IN_PAD), lambda i: (i, 0))],
        out_specs=pl.BlockSpec((N_CHIPS, tr, SSD_IN_SHARD), lambda i: (0, i, 0)),
        out_shape=jax.ShapeDtypeStruct((N_CHIPS, d, SSD_IN_SHARD), BF16),
        compiler_params=_params("parallel"), name=name)(g)


XBC_COL0 = SSD_D_INNER // LANES
DT_COL0 = (SSD_D_INNER + SSD_CONV_DIM) // LANES


def _shift_down(v, k, row_ids):
    return jnp.where(row_ids >= k, pltpu.roll(v, k, axis=0), 0.0)


def _shift_up(v, k, row_ids):
    n = v.shape[0]
    return jnp.where(row_ids < n - k, pltpu.roll(v, n - k, axis=0), 0.0)


def _conv_pre(x, w, b, row_ids):
    pre = b + w[3:4, :] * x
    for k in (1, 2, 3):
        pre = pre + w[3 - k:4 - k, :] * _shift_down(x, k, row_ids)
    return pre


def _conv_fwd(zx, conv_w, conv_b, *, name):
    t = zx.shape[0]
    nct = SSD_CONV_DIM // LANES

    def body(x_ref, w_ref, b_ref, o_ref):
        x = x_ref[...]
        row_ids = lax.broadcasted_iota(jnp.int32, x.shape, 0)
        pre = _conv_pre(x, w_ref[...], b_ref[...], row_ids)
        o_ref[...] = pre * _sigmoid(pre)

    return pl.pallas_call(
        body, grid=(nct,),
        in_specs=[pl.BlockSpec((t, LANES), lambda j: (0, XBC_COL0 + j)),
                  pl.BlockSpec((SSD_CONV_WIDTH, LANES), lambda j: (0, j)),
                  pl.BlockSpec((1, LANES), lambda j: (0, j))],
        out_specs=pl.BlockSpec((t, LANES), lambda j: (0, j)),
        out_shape=jax.ShapeDtypeStruct((t, SSD_CONV_DIM), F32),
        compiler_params=_params("parallel"), name=name)(zx, conv_w, conv_b)


def _conv_bwd(zx, conv_w, conv_b, d_xs, d_bm, d_cm, *, name):
    t = zx.shape[0]
    nct = SSD_CONV_DIM // LANES
    n_xs = SSD_D_INNER // LANES
    n_bm = SSD_N_GROUPS * SSD_D_STATE // LANES

    def body(x_ref, w_ref, b_ref, dxs_ref, dbm_ref, dcm_ref, dx_ref, dw_ref, db_ref):
        x = x_ref[...]
        w = w_ref[...]
        j = pl.program_id(0)
        dy = jnp.where(j < n_xs, dxs_ref[...], jnp.where(j < n_xs + n_bm, dbm_ref[...], dcm_ref[...]))
        row_ids = lax.broadcasted_iota(jnp.int32, x.shape, 0)
        pre = _conv_pre(x, w, b_ref[...], row_ids)
        sg = _sigmoid(pre)
        dpre = dy * (sg * (1.0 + pre * (1.0 - sg)))
        dx = w[3:4, :] * dpre
        for k in (1, 2, 3):
            dx = dx + w[3 - k:4 - k, :] * _shift_up(dpre, k, row_ids)
        dx_ref[...] = dx.astype(dx_ref.dtype)
        db_ref[...] = jnp.sum(dpre, axis=0, keepdims=True)
        dw_ref[3:4, :] = jnp.sum(dpre * x, axis=0, keepdims=True)
        for k in (1, 2, 3):
            dw_ref[3 - k:4 - k, :] = jnp.sum(dpre * _shift_down(x, k, row_ids), axis=0, keepdims=True)

    col = pl.BlockSpec((t, LANES), lambda j: (0, j))
    clip = lambda j, lo, n: jnp.clip(j - lo, 0, n - 1)
    return pl.pallas_call(
        body, grid=(nct,),
        in_specs=[pl.BlockSpec((t, LANES), lambda j: (0, XBC_COL0 + j)),
                  pl.BlockSpec((SSD_CONV_WIDTH, LANES), lambda j: (0, j)),
                  pl.BlockSpec((1, LANES), lambda j: (0, j)),
                  pl.BlockSpec((t, LANES), lambda j: (0, clip(j, 0, n_xs))),
                  pl.BlockSpec((t, LANES), lambda j: (0, clip(j, n_xs, n_bm))),
                  pl.BlockSpec((t, LANES), lambda j: (0, clip(j, n_xs + n_bm, n_bm)))],
        out_specs=[col, pl.BlockSpec((SSD_CONV_WIDTH, LANES), lambda j: (0, j)), pl.BlockSpec((1, LANES), lambda j: (0, j))],
        out_shape=[jax.ShapeDtypeStruct((t, SSD_CONV_DIM), BF16),
                   jax.ShapeDtypeStruct((SSD_CONV_WIDTH, SSD_CONV_DIM), F32),
                   jax.ShapeDtypeStruct((1, SSD_CONV_DIM), F32)],
        compiler_params=_params("parallel"), name=name)(zx, conv_w, conv_b, d_xs, d_bm, d_cm)


def _softplus_fwd(zx, bias_row, *, name):
    t = zx.shape[0]
    tr = _row_tile(t, 1024)

    def body(x_ref, b_ref, o_ref):
        v = x_ref[...] + b_ref[...]
        e = jnp.exp(-jnp.abs(v))
        u = 1.0 + e
        log1p = jnp.where(u == 1.0, e, jnp.log(u) * (e / (u - 1.0)))
        o_ref[...] = jnp.maximum(v, 0.0) + log1p

    return pl.pallas_call(
        body, grid=(t // tr,),
        in_specs=[pl.BlockSpec((tr, LANES), lambda i: (i, DT_COL0)), pl.BlockSpec((1, LANES), lambda i: (0, 0))],
        out_specs=pl.BlockSpec((tr, LANES), lambda i: (i, 0)),
        out_shape=jax.ShapeDtypeStruct((t, LANES), F32),
        compiler_params=_params("parallel"), name=name)(zx, bias_row)


def _softplus_bwd(zx, bias_row, ddt, *, name):
    t = zx.shape[0]
    tr = _row_tile(t, 1024)

    def body(x_ref, b_ref, g_ref, o_ref, db_ref):
        v = x_ref[...] + b_ref[...]
        lane = lax.broadcasted_iota(jnp.int32, v.shape, 1)
        d = jnp.where(lane < SSD_N_HEADS, g_ref[...] * _sigmoid(v), 0.0)
        o_ref[...] = d.astype(o_ref.dtype)

        @pl.when(pl.program_id(0) == 0)
        def _():
            db_ref[...] = jnp.zeros_like(db_ref)

        db_ref[...] += jnp.sum(d, axis=0, keepdims=True)

    return pl.pallas_call(
        body, grid=(t // tr,),
        in_specs=[pl.BlockSpec((tr, LANES), lambda i: (i, DT_COL0)), pl.BlockSpec((1, LANES), lambda i: (0, 0)),
                  pl.BlockSpec((tr, LANES), lambda i: (i, 0))],
        out_specs=[pl.BlockSpec((tr, LANES), lambda i: (i, 0)), pl.BlockSpec((1, LANES), lambda i: (0, 0))],
        out_shape=[jax.ShapeDtypeStruct((t, LANES), BF16), jax.ShapeDtypeStruct((1, LANES), F32)],
        compiler_params=_params("arbitrary"), name=name)(zx, bias_row, ddt)


def _ssd_masks():
    q = SSD_CHUNK
    tt = lax.broadcasted_iota(jnp.int32, (q, q), 0)
    ss = lax.broadcasted_iota(jnp.int32, (q, q), 1)
    lane = lax.broadcasted_iota(jnp.int32, (1, SSD_GW), 1)
    srow = lax.broadcasted_iota(jnp.int32, (SSD_GW, 1), 0)
    hm = [(lane >= SSD_HEAD_DIM * j) & (lane < SSD_HEAD_DIM * (j + 1)) for j in range(SSD_HPG)]
    rm = [(srow >= SSD_HEAD_DIM * j) & (srow < SSD_HEAD_DIM * (j + 1)) for j in range(SSD_HPG)]
    return tt, ss, hm, rm


def _ssd_head_terms(dtc, dtr, a_rows, j, tt, ss):
    q = SSD_CHUNK
    dt_col = dtc[:, j:j + 1]
    dt_row = dtr[j:j + 1, :]
    a_row1 = a_rows[j:j + 1, :]
    a_11 = a_rows[j:j + 1, 0:1]
    cum_col = jnp.sum(jnp.where(ss <= tt, dt_row * a_row1, 0.0), axis=1, keepdims=True)
    cum_row = jnp.sum(jnp.where(tt <= ss, dt_col * a_11, 0.0), axis=0, keepdims=True)
    decay = jnp.exp(jnp.where(ss <= tt, cum_col - cum_row, -jnp.inf))
    cum_last = cum_col[q - 1:q, :]
    e_col = jnp.exp(cum_col)
    dte_col = jnp.exp(cum_last - cum_col)
    e_last = jnp.exp(cum_last)
    return dt_col, dt_row, a_row1, a_11, decay, e_col, dte_col, e_last


SSD_CHUNKS_PER_STEP = 4
SSD_BC_COL0 = SSD_D_INNER // SSD_D_STATE


def _ssd_head_selects(terms, hm, rm):
    e_all = jnp.zeros((SSD_CHUNK, SSD_GW), F32)
    w_all = jnp.zeros((SSD_CHUNK, SSD_GW), F32)
    e_s = jnp.zeros((SSD_GW, 1), F32)
    for j in range(SSD_HPG):
        dt_col, _, _, _, _, e_col, dte_col, e_last = terms[j]
        e_all = jnp.where(hm[j], e_col, e_all)
        w_all = jnp.where(hm[j], dt_col * dte_col, w_all)
        e_s = jnp.where(rm[j], e_last, e_s)
    return e_all, w_all, e_s


def _ssd_fwd(xc, dtc, dtr, alog_b, d_b, *, name):
    t = xc.shape[0]
    q = SSD_CHUNK
    nc = t // q
    kc = min(SSD_CHUNKS_PER_STEP, nc)
    rows = kc * q

    def body(x_ref, b_ref, c_ref, dtc_ref, dtr_ref, alog_ref, d_ref, y_ref, st_ref, s_scr):
        @pl.when(pl.program_id(1) == 0)
        def _():
            s_scr[...] = jnp.zeros_like(s_scr)

        tt, ss, hm, rm = _ssd_masks()
        a_rows = -jnp.exp(alog_ref[...])
        d_rows = d_ref[...]
        d_all = jnp.zeros((1, SSD_GW), F32)
        for j in range(SSD_HPG):
            d_all = jnp.where(hm[j], d_rows[j:j + 1, 0:1], d_all)
        ks, hs = range(kc), range(SSD_HPG)
        sl = [pl.ds(k * q, q) for k in ks]
        x = [x_ref[sl[k], :] for k in ks]
        bm = [b_ref[sl[k], :].astype(BF16) for k in ks]
        cm = [c_ref[sl[k], :].astype(BF16) for k in ks]
        xb = [x[k].astype(BF16) for k in ks]
        terms = [[_ssd_head_terms(dtc_ref[sl[k], :], dtr_ref[:, sl[k]], a_rows, j, tt, ss) for j in hs] for k in ks]
        g = [_dot_nt(cm[k], bm[k]) for k in ks]
        m = [[(g[k] * terms[k][j][4] * terms[k][j][1]).astype(BF16) for j in hs] for k in ks]
        yj = [[_dot_nn(m[k][j], xb[k]) for j in hs] for k in ks]
        sel = [_ssd_head_selects(terms[k], hm, rm) for k in ks]
        upd = [_dot_tn((x[k] * sel[k][1]).astype(BF16), bm[k]) for k in ks]
        states = [s_scr[...]]
        for k in ks:
            states.append(states[k] * sel[k][2] + upd[k])
        inter = [_dot_nt(cm[k], states[k].astype(BF16)) for k in ks]
        ys = []
        for k in ks:
            y = jnp.zeros((q, SSD_GW), F32)
            for j in hs:
                y = jnp.where(hm[j], yj[k][j], y)
            ys.append(y + inter[k] * sel[k][0] + x[k] * d_all)
        for k in ks:
            st_ref[k] = states[k]
        y_ref[...] = jnp.concatenate(ys, axis=0)
        s_scr[...] = states[kc]

    blk = lambda width, off: pl.BlockSpec((rows, width), lambda g, c: (c, off + g))
    par_s = pl.BlockSpec((None, SSD_HPG, LANES), lambda g, c: (g, 0, 0))
    return pl.pallas_call(
        body, grid=(SSD_N_GROUPS, nc // kc),
        in_specs=[blk(SSD_GW, 0), blk(SSD_D_STATE, SSD_BC_COL0), blk(SSD_D_STATE, SSD_BC_COL0 + SSD_N_GROUPS),
                  pl.BlockSpec((None, rows, SSD_HPG), lambda g, c: (g, c, 0)),
                  pl.BlockSpec((None, SSD_HPG, rows), lambda g, c: (g, 0, c)), par_s, par_s],
        out_specs=[blk(SSD_GW, 0), pl.BlockSpec((None, kc, SSD_GW, SSD_D_STATE), lambda g, c: (g, c, 0, 0))],
        out_shape=[jax.ShapeDtypeStruct((t, SSD_D_INNER), F32),
                   jax.ShapeDtypeStruct((SSD_N_GROUPS, nc, SSD_GW, SSD_D_STATE), F32)],
        scratch_shapes=[pltpu.VMEM((SSD_GW, SSD_D_STATE), F32)],
        compiler_params=_params("parallel", "arbitrary"), name=name)(xc, xc, xc, dtc, dtr, alog_b, d_b)


def _ssd_bwd(xc, dtc, dtr, alog_b, d_b, states, dy, *, name):
    t = xc.shape[0]
    q = SSD_CHUNK
    nc = t // q
    kc = min(SSD_CHUNKS_PER_STEP, nc)
    nst = nc // kc
    rows = kc * q
    rev = lambda c: nst - 1 - c

    def body(x_ref, b_ref, c_ref, dtc_ref, dtr_ref, alog_ref, d_ref, st_ref, dy_ref,
             dx_ref, db_ref, dc_ref, ddt_ref, dpar_ref, ds_scr):
        @pl.when(pl.program_id(1) == 0)
        def _():
            ds_scr[...] = jnp.zeros_like(ds_scr)
            dpar_ref[...] = jnp.zeros_like(dpar_ref)

        tt, ss, hm, rm = _ssd_masks()
        tcol = lax.broadcasted_iota(jnp.int32, (q, 1), 0)
        lane = lax.broadcasted_iota(jnp.int32, (1, LANES), 1)
        a_rows = -jnp.exp(alog_ref[...])
        d_rows = d_ref[...]
        d_all = jnp.zeros((1, SSD_GW), F32)
        for j in range(SSD_HPG):
            d_all = jnp.where(hm[j], d_rows[j:j + 1, 0:1], d_all)
        ks, hs = range(kc), range(SSD_HPG)
        sl = [pl.ds(k * q, q) for k in ks]
        x = [x_ref[sl[k], :] for k in ks]
        dyv = [dy_ref[sl[k], :] for k in ks]
        bm = [b_ref[sl[k], :].astype(BF16) for k in ks]
        cm = [c_ref[sl[k], :].astype(BF16) for k in ks]
        s_in = [st_ref[k] for k in ks]
        xb = [x[k].astype(BF16) for k in ks]
        dyb = [dyv[k].astype(BF16) for k in ks]
        s_b = [s_in[k].astype(BF16) for k in ks]
        terms = [[_ssd_head_terms(dtc_ref[sl[k], :], dtr_ref[:, sl[k]], a_rows, j, tt, ss) for j in hs] for k in ks]
        sel = [_ssd_head_selects(terms[k], hm, rm) for k in ks]
        e_all, w_all, e_s = [s_[0] for s_ in sel], [s_[1] for s_ in sel], [s_[2] for s_ in sel]
        dye = [(dyv[k] * e_all[k]).astype(BF16) for k in ks]
        ds_loc = [_dot_tn(dye[k], cm[k]) for k in ks]
        ds = [None] * kc
        running = ds_scr[...]
        for k in reversed(ks):
            ds[k] = running
            running = running * e_s[k] + ds_loc[k]
        ds_scr[...] = running
        ds_b = [ds[k].astype(BF16) for k in ks]
        g = [_dot_nt(cm[k], bm[k]) for k in ks]
        cs = [_dot_nt(cm[k], s_b[k]) for k in ks]
        bds = [_dot_nt(bm[k], ds_b[k]) for k in ks]
        dm = [[_dot_nt(jnp.where(hm[j], dyv[k], 0.0).astype(BF16), xb[k]) for j in hs] for k in ks]
        gl = [[g[k] * terms[k][j][4] for j in hs] for k in ks]
        wp = [[dm[k][j] * gl[k][j] for j in hs] for k in ks]
        mt = [[(gl[k][j] * terms[k][j][1]).astype(BF16) for j in hs] for k in ks]
        dxj = [[_dot_tn(mt[k][j], dyb[k]) for j in hs] for k in ks]
        dg = []
        for k in ks:
            acc = jnp.zeros((q, q), F32)
            for j in hs:
                acc = acc + dm[k][j] * terms[k][j][4] * terms[k][j][1]
            dg.append(acc.astype(BF16))
        dy_cs = [dyv[k] * cs[k] for k in ks]
        x_bds = [x[k] * bds[k] for k in ks]
        dy_x = [dyv[k] * x[k] for k in ks]
        ds_s = [ds[k] * s_in[k] for k in ks]
        w = [[wp[k][j] * terms[k][j][1] for j in hs] for k in ks]
        rw_col = [[jnp.sum(w[k][j], axis=1, keepdims=True) for j in hs] for k in ks]
        cw_row = [[jnp.sum(w[k][j], axis=0, keepdims=True) for j in hs] for k in ks]
        cwp_row = [[jnp.sum(wp[k][j], axis=0, keepdims=True) for j in hs] for k in ks]
        r1_col = [[jnp.sum(jnp.where(hm[j], dy_cs[k], 0.0), axis=1, keepdims=True) * terms[k][j][5] for j in hs] for k in ks]
        dw_col = [[jnp.sum(jnp.where(hm[j], x_bds[k], 0.0), axis=1, keepdims=True) for j in hs] for k in ks]
        total = lambda v: jnp.sum(jnp.sum(v, axis=1, keepdims=True), axis=0, keepdims=True)
        s_sum = [[total(jnp.where(rm[j], ds_s[k], 0.0)) for j in hs] for k in ks]
        d_d = [[total(jnp.where(hm[j], dy_x[k], 0.0)) for j in hs] for k in ks]
        ddt_rows = [[None] * SSD_HPG for _ in ks]
        dpar = [jnp.zeros((1, LANES), F32) for _ in hs]
        for k in ks:
            for j in hs:
                dt_col, dt_row, a_row1, a_11, _, _, dte_col, e_last = terms[k][j]
                dww = dw_col[k][j] * (dt_col * dte_col)
                last_add = jnp.sum(dww, axis=0, keepdims=True) + e_last * s_sum[k][j]
                dcum_col = rw_col[k][j] + r1_col[k][j] - dww + jnp.where(tcol == q - 1, last_add, 0.0)
                da_row = jnp.sum(jnp.where(tt >= ss, dcum_col, 0.0), axis=0, keepdims=True)
                da_col = jnp.sum(jnp.where(ss >= tt, -cw_row[k][j], 0.0), axis=1, keepdims=True)
                ddt_col = a_11 * da_col + dw_col[k][j] * dte_col
                ddt_rows[k][j] = (a_row1 * da_row + cwp_row[k][j]
                                  + jnp.sum(jnp.where(tt == ss, ddt_col, 0.0), axis=0, keepdims=True))
                d_a = jnp.sum(dt_row * da_row, axis=1, keepdims=True) + jnp.sum(dt_col * da_col, axis=0, keepdims=True)
                dpar[j] = dpar[j] + jnp.where(lane == 0, d_a * a_11, 0.0) + jnp.where(lane == 1, d_d[k][j], 0.0)
        dxs = []
        for k in ks:
            acc = jnp.zeros((q, SSD_GW), F32)
            for j in hs:
                acc = jnp.where(hm[j], dxj[k][j], acc)
            dxs.append(acc + w_all[k] * bds[k] + d_all * dyv[k])
        xw = [(x[k] * w_all[k]).astype(BF16) for k in ks]
        dc = [_dot_nn(dg[k], bm[k]) + _dot_nn(dye[k], s_b[k]) for k in ks]
        db = [_dot_tn(dg[k], cm[k]) + _dot_nn(xw[k], ds_b[k]) for k in ks]
        dx_ref[...] = jnp.concatenate(dxs, axis=0)
        dc_ref[...] = jnp.concatenate(dc, axis=0)
        db_ref[...] = jnp.concatenate(db, axis=0)
        ddt_ref[...] = jnp.concatenate([jnp.concatenate([ddt_rows[k][j] for k in ks], axis=1) for j in hs], axis=0)
        dpar_ref[...] += jnp.concatenate(dpar, axis=0)

    blk = lambda width, off: pl.BlockSpec((rows, width), lambda g, c: (rev(c), off + g))
    par_s = pl.BlockSpec((None, SSD_HPG, LANES), lambda g, c: (g, 0, 0))
    return pl.pallas_call(
        body, grid=(SSD_N_GROUPS, nst),
        in_specs=[blk(SSD_GW, 0), blk(SSD_D_STATE, SSD_BC_COL0), blk(SSD_D_STATE, SSD_BC_COL0 + SSD_N_GROUPS),
                  pl.BlockSpec((None, rows, SSD_HPG), lambda g, c: (g, rev(c), 0)),
                  pl.BlockSpec((None, SSD_HPG, rows), lambda g, c: (g, 0, rev(c))), par_s, par_s,
                  pl.BlockSpec((None, kc, SSD_GW, SSD_D_STATE), lambda g, c: (g, rev(c), 0, 0)), blk(SSD_GW, 0)],
        out_specs=[blk(SSD_GW, 0), blk(SSD_D_STATE, 0), blk(SSD_D_STATE, 0),
                   pl.BlockSpec((None, SSD_HPG, rows), lambda g, c: (g, 0, rev(c))), par_s],
        out_shape=[jax.ShapeDtypeStruct((t, SSD_D_INNER), F32),
                   jax.ShapeDtypeStruct((t, SSD_N_GROUPS * SSD_D_STATE), F32),
                   jax.ShapeDtypeStruct((t, SSD_N_GROUPS * SSD_D_STATE), F32),
                   jax.ShapeDtypeStruct((SSD_N_GROUPS, SSD_HPG, t), F32),
                   jax.ShapeDtypeStruct((SSD_N_GROUPS, SSD_HPG, LANES), F32)],
        scratch_shapes=[pltpu.VMEM((SSD_GW, SSD_D_STATE), F32)],
        compiler_params=_params("parallel", "arbitrary"), name=name)(xc, xc, xc, dtc, dtr, alog_b, d_b, states, dy)


def _gate_norm_fwd(y, zx, norm_w, *, name):
    t = y.shape[0]
    tr = _row_tile(t, 256)
    row = pl.BlockSpec((tr, SSD_D_INNER), lambda i: (i, 0))

    def body(y_ref, z_ref, w_ref, o_ref):
        for gi in range(SSD_N_GROUPS):
            sl = pl.ds(gi * SSD_GW, SSD_GW)
            z = z_ref[:, sl]
            gv = y_ref[:, sl] * (z * _sigmoid(z))
            r = lax.rsqrt(jnp.mean(gv * gv, axis=-1, keepdims=True) + NORM_EPS)
            o_ref[:, sl] = (gv * r * w_ref[:, sl]).astype(BF16)

    return pl.pallas_call(
        body, grid=(t // tr,), in_specs=[row, row, pl.BlockSpec((1, SSD_D_INNER), lambda i: (0, 0))],
        out_specs=row, out_shape=jax.ShapeDtypeStruct((t, SSD_D_INNER), BF16),
        compiler_params=_params("parallel"), name=name)(y, zx, norm_w)


def _gate_norm_bwd(y, zx, norm_w, dyn, *, name):
    t = y.shape[0]
    tr = _row_tile(t, 256)
    row = pl.BlockSpec((tr, SSD_D_INNER), lambda i: (i, 0))
    vec = pl.BlockSpec((1, SSD_D_INNER), lambda i: (0, 0))

    def body(y_ref, z_ref, w_ref, dyn_ref, dy_ref, dz_ref, dw_ref):
        @pl.when(pl.program_id(0) == 0)
        def _():
            dw_ref[...] = jnp.zeros_like(dw_ref)

        for gi in range(SSD_N_GROUPS):
            sl = pl.ds(gi * SSD_GW, SSD_GW)
            z = z_ref[:, sl]
            yv = y_ref[:, sl]
            sg = _sigmoid(z)
            sz = z * sg
            gv = yv * sz
            r = lax.rsqrt(jnp.mean(gv * gv, axis=-1, keepdims=True) + NORM_EPS)
            ghat = gv * r
            dout = dyn_ref[:, sl].astype(F32)
            dgh = dout * w_ref[:, sl]
            dgv = r * (dgh - ghat * jnp.mean(dgh * ghat, axis=-1, keepdims=True))
            dy_ref[:, sl] = dgv * sz
            dz_ref[:, sl] = (dgv * yv * (sg * (1.0 + z * (1.0 - sg)))).astype(dz_ref.dtype)
            dw_ref[:, sl] += jnp.sum(dout * ghat, axis=0, keepdims=True)

    return pl.pallas_call(
        body, grid=(t // tr,), in_specs=[row, row, vec, row], out_specs=[row, row, vec],
        out_shape=[jax.ShapeDtypeStruct((t, SSD_D_INNER), F32), jax.ShapeDtypeStruct((t, SSD_D_INNER), BF16),
                   jax.ShapeDtypeStruct((1, SSD_D_INNER), F32)],
        compiler_params=_params("arbitrary"), name=name)(y, zx, norm_w, dyn)


ATTN_KV_W = ATTN_N_KV * ATTN_HEAD_DIM
ATTN_Q_HALF = 512
ATTN_K_BLK = ATTN_N_Q * ATTN_HEAD_DIM // ATTN_KV_W
ATTN_V_BLK = ATTN_K_BLK + 1


def _attn_valid(first_block):
    w = ATTN_WINDOW
    qpos = lax.broadcasted_iota(jnp.int32, (w, 2 * w), 0) + w
    kpos = lax.broadcasted_iota(jnp.int32, (w, 2 * w), 1)
    rel = qpos - kpos
    return (rel >= 0) & (rel < w) & jnp.logical_not(first_block & (kpos < w))


def _attn_block_views(lo_ref, hi_ref, kc_ref, kp_ref, vc_ref, vp_ref):
    hd = ATTN_HEAD_DIM
    per_half = ATTN_Q_HALF // hd
    heads = [(lo_ref if h < per_half else hi_ref)[:, pl.ds((h % per_half) * hd, hd)] for h in range(ATTN_N_Q)]
    kv_cols = [pl.ds(kh * hd, hd) for kh in range(ATTN_N_KV)]
    kb = [jnp.concatenate([kp_ref[:, c], kc_ref[:, c]], axis=0) for c in kv_cols]
    vb = [jnp.concatenate([vp_ref[:, c], vc_ref[:, c]], axis=0) for c in kv_cols]
    return heads, kb, vb


def _attn_softmax(q, kb, sink, valid):
    heads = range(ATTN_N_Q)
    scale = ATTN_HEAD_DIM ** -0.5
    s = [jnp.where(valid, _dot_nt(q[h], kb[h // ATTN_REP]) * scale, -jnp.inf) for h in heads]
    m = [jnp.maximum(jnp.max(s[h], axis=1, keepdims=True), sink[h]) for h in heads]
    e = [jnp.exp(s[h] - m[h]) for h in heads]
    es = [jnp.exp(sink[h] - m[h]) for h in heads]
    inv = [1.0 / (jnp.sum(e[h], axis=1, keepdims=True) + es[h]) for h in heads]
    return e, es, inv


def _attn_fwd(qkv, sinks_b, *, name):
    t = qkv.shape[0]
    w = ATTN_WINDOW
    nb = t // w
    prev = lambda n: jnp.maximum(n - 1, 0)

    def body(qlo_ref, qhi_ref, kc_ref, kp_ref, vc_ref, vp_ref, sink_ref, o_ref):
        heads = range(ATTN_N_Q)
        q, kb, vb = _attn_block_views(qlo_ref, qhi_ref, kc_ref, kp_ref, vc_ref, vp_ref)
        sink = [sink_ref[h:h + 1, 0:1] for h in heads]
        e, _, inv = _attn_softmax(q, kb, sink, _attn_valid(pl.program_id(0) == 0))
        out = [_dot_nn((e[h] * inv[h]).astype(BF16), vb[h // ATTN_REP]).astype(o_ref.dtype) for h in heads]
        o_ref[...] = jnp.concatenate(out, axis=1)

    qh = lambda half: pl.BlockSpec((w, ATTN_Q_HALF), lambda n: (n, half))
    kv = lambda blk, idx: pl.BlockSpec((w, ATTN_KV_W), lambda n: (idx(n), blk))
    cur = lambda n: n
    return pl.pallas_call(
        body, grid=(nb,),
        in_specs=[qh(0), qh(1), kv(ATTN_K_BLK, cur), kv(ATTN_K_BLK, prev), kv(ATTN_V_BLK, cur), kv(ATTN_V_BLK, prev),
                  pl.BlockSpec((ATTN_N_Q, LANES), lambda n: (0, 0))],
        out_specs=pl.BlockSpec((w, D_MODEL), lambda n: (n, 0)),
        out_shape=jax.ShapeDtypeStruct((t, D_MODEL), BF16),
        compiler_params=_params("parallel"), name=name)(qkv, qkv, qkv, qkv, qkv, qkv, sinks_b)


def _attn_bwd(qkv, sinks_b, dout, *, name):
    t = qkv.shape[0]
    w = ATTN_WINDOW
    nb = t // w
    hd = ATTN_HEAD_DIM
    clamp = lambda n: jnp.minimum(n, nb - 1)
    prev = lambda n: jnp.maximum(clamp(n) - 1, 0)

    def body(qlo_ref, qhi_ref, kc_ref, kp_ref, vc_ref, vp_ref, sink_ref, dolo_ref, dohi_ref,
             dq_ref, dkv_ref, dsink_ref, carry):
        n = pl.program_id(0)

        @pl.when(n == 0)
        def _():
            carry[...] = jnp.zeros_like(carry)
            dsink_ref[...] = jnp.zeros_like(dsink_ref)

        @pl.when(n < nb)
        def _():
            heads, kvs = range(ATTN_N_Q), range(ATTN_N_KV)
            q, kb, vb = _attn_block_views(qlo_ref, qhi_ref, kc_ref, kp_ref, vc_ref, vp_ref)
            do, _, _ = _attn_block_views(dolo_ref, dohi_ref, kc_ref, kp_ref, vc_ref, vp_ref)
            sink = [sink_ref[h:h + 1, 0:1] for h in heads]
            e, es, inv = _attn_softmax(q, kb, sink, _attn_valid(n == 0))
            dp = [_dot_nt(do[h], vb[h // ATTN_REP]) for h in heads]
            p = [e[h] * inv[h] for h in heads]
            delta = [jnp.sum(p[h] * dp[h], axis=1, keepdims=True) for h in heads]
            dsc = [(p[h] * (dp[h] - delta[h]) * (hd ** -0.5)).astype(BF16) for h in heads]
            pb = [p[h].astype(BF16) for h in heads]
            dq = [_dot_nn(dsc[h], kb[h // ATTN_REP]).astype(dq_ref.dtype) for h in heads]
            stack = lambda per_head, kh: jnp.concatenate(per_head[kh * ATTN_REP:(kh + 1) * ATTN_REP], axis=0)
            dkb = [_dot_tn(stack(dsc, kh), stack(q, kh)) for kh in kvs]
            dvb = [_dot_tn(stack(pb, kh), stack(do, kh)) for kh in kvs]
            dsink = [jnp.broadcast_to(jnp.sum(-es[h] * inv[h] * delta[h], axis=0, keepdims=True), (1, LANES)) for h in heads]
            dq_ref[...] = jnp.concatenate(dq, axis=1)
            dsink_ref[...] += jnp.concatenate(dsink, axis=0)
            dkv_ref[...] = (carry[...] + jnp.concatenate([d[0:w, :] for d in dkb + dvb], axis=1)).astype(dkv_ref.dtype)
            carry[...] = jnp.concatenate([d[w:2 * w, :] for d in dkb + dvb], axis=1)

        @pl.when(n == nb)
        def _():
            dkv_ref[...] = carry[...].astype(dkv_ref.dtype)

    qh = lambda half: pl.BlockSpec((w, ATTN_Q_HALF), lambda n: (clamp(n), half))
    kv = lambda blk, idx: pl.BlockSpec((w, ATTN_KV_W), lambda n: (idx(n), blk))
    return pl.pallas_call(
        body, grid=(nb + 1,),
        in_specs=[qh(0), qh(1), kv(ATTN_K_BLK, clamp), kv(ATTN_K_BLK, prev), kv(ATTN_V_BLK, clamp), kv(ATTN_V_BLK, prev),
                  pl.BlockSpec((ATTN_N_Q, LANES), lambda n: (0, 0)), qh(0), qh(1)],
        out_specs=[pl.BlockSpec((w, D_MODEL), lambda n: (clamp(n), 0)),
                   pl.BlockSpec((w, 2 * ATTN_KV_W), lambda n: (jnp.maximum(n - 1, 0), 0)),
                   pl.BlockSpec((ATTN_N_Q, LANES), lambda n: (0, 0))],
        out_shape=[jax.ShapeDtypeStruct((t, D_MODEL), BF16), jax.ShapeDtypeStruct((t, 2 * ATTN_KV_W), BF16),
                   jax.ShapeDtypeStruct((ATTN_N_Q, LANES), F32)],
        scratch_shapes=[pltpu.VMEM((w, 2 * ATTN_KV_W), F32)],
        compiler_params=_params("arbitrary"), name=name)(qkv, qkv, qkv, qkv, qkv, qkv, sinks_b, dout, dout)


def _sq_relu_epilogue(acc):
    r = jnp.maximum(acc, 0.0)
    return acc, r * r


def _sq_relu_bwd_epilogue(acc, pre):
    return (acc * (2.0 * jnp.maximum(pre, 0.0)),)


def _bias_epilogue(acc, bias):
    return (acc + bias,)


def _mlp_fwd(u, w_up, w_down, tag):
    pre, act = _matmul(u, w_up, mode="nn", out_dtypes=(F32, BF16), epilogue=_sq_relu_epilogue, b_shards=True,
                       name=f"mlp_up_{tag}")
    f = _matmul(act, w_down, mode="nn", out_dtypes=(F32,), name=f"mlp_down_{tag}")
    return pre, act, f


def _mlp_bwd(u, pre, act, w_up, w_down, df, tag):
    dpre = _matmul(df, w_down, mode="nt", out_dtypes=(BF16,), epilogue=_sq_relu_bwd_epilogue,
                   extras=((pre, "tile"),), name=f"mlp_dact_{tag}")
    dw_down = _matmul(act, df, mode="tn", out_dtypes=(BF16,), name=f"mlp_dwdown_{tag}")
    du = _matmul(dpre, w_up, mode="nt", out_dtypes=(F32,), b_shards=True, name=f"mlp_du_{tag}")
    dw_up = _matmul(u, dpre, mode="tn", out_dtypes=(BF16,), out_shards=True, name=f"mlp_dwup_{tag}")
    return du, dw_up, dw_down


def _group_dt_layouts(dt):
    t = dt.shape[0]
    d = dt[:, :SSD_N_HEADS].reshape(t, SSD_N_GROUPS, SSD_HPG)
    return jnp.transpose(d, (1, 0, 2)), jnp.transpose(d, (1, 2, 0))


def _head_param_rows(p):
    return jnp.broadcast_to(p.reshape(SSD_N_GROUPS, SSD_HPG, 1), (SSD_N_GROUPS, SSD_HPG, LANES))


def _local_step(x, target, wts):
    t = x.shape[0]
    row = lambda v: v.reshape(1, -1)
    mix_pre, mix_post, ffn_pre, ffn_post = wts["mix_pre_norm"], wts["mix_post_norm"], wts["ffn_pre_norm"], wts["ffn_post_norm"]

    u0 = _rms_fwd(x, row(mix_pre[0]), name="rms_pre_mix0")
    zx = _matmul(u0, wts["ssd_w_in"], mode="nn", out_dtypes=(F32,), tn=896, name="ssd_in_proj")
    xc = _conv_fwd(zx, wts["ssd_conv_w"], row(wts["ssd_conv_b"]), name="ssd_conv_fwd")
    bias_row = jnp.pad(wts["ssd_dt_bias"], (0, LANES - SSD_N_HEADS)).reshape(1, LANES)
    dt = _softplus_fwd(zx, bias_row, name="ssd_dt_fwd")
    dtc, dtr = _group_dt_layouts(dt)
    alog_b, d_b = _head_param_rows(wts["ssd_a_log"]), _head_param_rows(wts["ssd_d"])
    y_ssd, states = _ssd_fwd(xc, dtc, dtr, alog_b, d_b, name="ssd_scan_fwd")
    norm_w = row(wts["ssd_norm_w"])
    yn = _gate_norm_fwd(y_ssd, zx, norm_w, name="ssd_gate_norm_fwd")
    mix0 = _matmul(yn, wts["ssd_w_out"], mode="nn", out_dtypes=(F32,), name="ssd_out_proj")
    h1, v0 = _rms_fwd(mix0, row(mix_post[0]), resid=x, want_u=row(ffn_pre[0]), name="rms_post_mix0")
    pre0, act0, f0 = _mlp_fwd(v0, wts["mlp_w_up"][0], wts["mlp_w_down"][0], "l0")
    h2, u1 = _rms_fwd(f0, row(ffn_post[0]), resid=h1, want_u=row(mix_pre[1]), name="rms_post_ffn0")

    qkv = _matmul(u1, wts["attn_w_qkv"], mode="nn", out_dtypes=(BF16,), epilogue=_bias_epilogue,
                  extras=((row(wts["attn_b_qkv"]), "row"),), b_shards=True, name="attn_qkv_proj")
    sinks_b = jnp.broadcast_to(wts["attn_sinks"].reshape(ATTN_N_Q, 1), (ATTN_N_Q, LANES))
    ao = _attn_fwd(qkv, sinks_b, name="attn_fwd")
    mix1 = _matmul(ao, wts["attn_w_o"], mode="nn", out_dtypes=(F32,), epilogue=_bias_epilogue,
                   extras=((row(wts["attn_b_o"]), "row"),), name="attn_out_proj")
    h3, v1 = _rms_fwd(mix1, row(mix_post[1]), resid=h2, want_u=row(ffn_pre[1]), name="rms_post_mix1")
    pre1, act1, f1 = _mlp_fwd(v1, wts["mlp_w_up"][1], wts["mlp_w_down"][1], "l1")
    h4 = _rms_fwd(f1, row(ffn_post[1]), resid=h3, name="rms_post_ffn1")

    dh4, loss_tile = _loss_head(h4, target, name="loss_head")

    df1, g_ffn_post1 = _rms_bwd(f1, row(ffn_post[1]), dh4, out_dtype=BF16, name="rms_post_ffn1_bwd")
    dv1, g_up1, g_down1 = _mlp_bwd(v1, pre1, act1, wts["mlp_w_up"][1], wts["mlp_w_down"][1], df1, "l1")
    dh3, g_ffn_pre1 = _rms_bwd(h3, row(ffn_pre[1]), dv1, resid=dh4, name="rms_pre_ffn1_bwd")
    dmix1, g_mix_post1 = _rms_bwd(mix1, row(mix_post[1]), dh3, out_dtype=BF16, name="rms_post_mix1_bwd")
    g_b_o = _col_sum(dmix1, name="attn_bo_grad")
    g_w_o = _matmul(ao, dmix1, mode="tn", out_dtypes=(BF16,), name="attn_dwo")
    dao = _matmul(dmix1, wts["attn_w_o"], mode="nt", out_dtypes=(BF16,), name="attn_dao")
    dq, dkv, g_sinks = _attn_bwd(qkv, sinks_b, dao, name="attn_bwd")
    dqkv = jnp.concatenate([dq, dkv], axis=1)
    g_b_qkv = _col_sum(dqkv, name="attn_bqkv_grad")
    g_w_qkv = _matmul(u1, dqkv, mode="tn", out_dtypes=(BF16,), tn=ATTN_QKV // N_CHIPS, out_shards=True, name="attn_dwqkv")
    du1 = _matmul(dqkv, wts["attn_w_qkv"], mode="nt", out_dtypes=(F32,), b_shards=True, name="attn_du")
    dh2, g_mix_pre1 = _rms_bwd(h2, row(mix_pre[1]), du1, resid=dh3, name="rms_pre_mix1_bwd")

    df0, g_ffn_post0 = _rms_bwd(f0, row(ffn_post[0]), dh2, out_dtype=BF16, name="rms_post_ffn0_bwd")
    dv0, g_up0, g_down0 = _mlp_bwd(v0, pre0, act0, wts["mlp_w_up"][0], wts["mlp_w_down"][0], df0, "l0")
    dh1, g_ffn_pre0 = _rms_bwd(h1, row(ffn_pre[0]), dv0, resid=dh2, name="rms_pre_ffn0_bwd")
    dmix0, g_mix_post0 = _rms_bwd(mix0, row(mix_post[0]), dh1, out_dtype=BF16, name="rms_post_mix0_bwd")
    g_w_out = _matmul(yn, dmix0, mode="tn", out_dtypes=(BF16,), name="ssd_dwout")
    dyn = _matmul(dmix0, wts["ssd_w_out"], mode="nt", out_dtypes=(BF16,), name="ssd_dyn")
    dy_ssd, dz, g_norm_w = _gate_norm_bwd(y_ssd, zx, norm_w, dyn, name="ssd_gate_norm_bwd")
    dxc, dbm, dcm, ddt_r, dpar = _ssd_bwd(xc, dtc, dtr, alog_b, d_b, states, dy_ssd, name="ssd_scan_bwd")
    dxbc, g_conv_w, g_conv_b = _conv_bwd(zx, wts["ssd_conv_w"], row(wts["ssd_conv_b"]), dxc, dbm, dcm, name="ssd_conv_bwd")
    ddt = jnp.pad(jnp.transpose(ddt_r, (2, 0, 1)).reshape(t, SSD_N_HEADS), ((0, 0), (0, LANES - SSD_N_HEADS)))
    ddt_raw, g_dt_bias = _softplus_bwd(zx, bias_row, ddt, name="ssd_dt_bwd")
    dzx = jnp.concatenate([dz, dxbc, ddt_raw], axis=1)
    g_w_in = _w_in_to_shards(_matmul(u0, dzx, mode="tn", out_dtypes=(F32,), tn=896, name="ssd_dwin"), name="ssd_dwin_shards")
    du0 = _matmul(dzx, wts["ssd_w_in"], mode="nt", out_dtypes=(F32,), tk=896, name="ssd_du")
    grad_x, g_mix_pre0 = _rms_bwd(x, row(mix_pre[0]), du0, resid=dh1, name="rms_pre_mix0_bwd")

    dpar = dpar.reshape(SSD_N_HEADS, LANES)
    mats = {"ssd_w_in": g_w_in, "ssd_w_out": g_w_out, "attn_w_qkv": g_w_qkv, "attn_w_o": g_w_o,
            "mlp_w_up": (g_up0, g_up1), "mlp_w_down": (g_down0, g_down1)}
    vecs = {
        "ssd_conv_w": g_conv_w, "ssd_conv_b": g_conv_b.reshape(-1),
        "ssd_dt_bias": g_dt_bias[0, :SSD_N_HEADS], "ssd_a_log": dpar[:, 0], "ssd_d": dpar[:, 1],
        "ssd_norm_w": g_norm_w.reshape(-1), "attn_b_qkv": g_b_qkv.reshape(-1), "attn_sinks": g_sinks[:, 0],
        "attn_b_o": g_b_o.reshape(-1),
        "mix_pre_norm": jnp.concatenate([g_mix_pre0, g_mix_pre1]), "mix_post_norm": jnp.concatenate([g_mix_post0, g_mix_post1]),
        "ffn_pre_norm": jnp.concatenate([g_ffn_pre0, g_ffn_pre1]), "ffn_post_norm": jnp.concatenate([g_ffn_post0, g_ffn_post1]),
    }
    return loss_tile, grad_x, mats, vecs


ANY = pl.BlockSpec(memory_space=pl.ANY)


def _mesh_position():
    return lax.axis_index("x"), lax.axis_index("y"), lax.axis_index("c")


def _flip(v, bit):
    return 1 - v if bit else v


OTHER_CHIPS = ((1, 0), (0, 1), (1, 1))


def _comm_params():
    return pltpu.CompilerParams(vmem_limit_bytes=VMEM_LIMIT)


def _staged_copies(srcs, dsts, bufs, sems_in, sems_out):
    loads = [pltpu.make_async_copy(s, b, sems_in.at[i]) for i, (s, b) in enumerate(zip(srcs, bufs))]
    stores = [pltpu.make_async_copy(b, d, sems_out.at[i]) for i, (b, d) in enumerate(zip(bufs, dsts))]
    return loads, stores


def _all_gather_chips(mats, vecs, *, name):
    nm, nv = len(mats), len(vecs)
    n = nm + nv
    n_ici = (N_CHIPS - 1) * n
    n_fwd = (N_CHIPS - 1) * nm

    def body(*refs):
        ins, outs, bufs = refs[:n], refs[n:2 * n], refs[2 * n:3 * n]
        ici_send, ici_recv, fwd_send, fwd_recv, load_sems, store_sems = refs[3 * n:]
        xi, yi, ci = _mesh_position()
        me = 2 * xi + yi
        loads, stores = _staged_copies(ins, [outs[i].at[me] for i in range(n)], bufs, load_sems, store_sems)
        sends, landed, forwards, from_sibling = [], [], [], []
        for j, (bx, by) in enumerate(OTHER_CHIPS):
            px, py = _flip(xi, bx), _flip(yi, by)
            peer = 2 * px + py
            for i in range(n):
                k = j * n + i
                is_mat = i < nm
                mk = functools.partial(pltpu.make_async_remote_copy, send_sem=ici_send.at[k], recv_sem=ici_recv.at[k],
                                       device_id=(px, py, ci), device_id_type=MESH)
                if is_mat:
                    sends.append(mk(src_ref=ins[i].at[ci], dst_ref=outs[i].at[me, ci]))
                    landed.append(mk(src_ref=ins[i].at[ci], dst_ref=outs[i].at[peer, ci]))
                    kf = j * nm + i
                    fw = functools.partial(pltpu.make_async_remote_copy, send_sem=fwd_send.at[kf], recv_sem=fwd_recv.at[kf],
                                           device_id=(xi, yi, 1 - ci), device_id_type=MESH)
                    forwards.append(fw(src_ref=outs[i].at[peer, ci], dst_ref=outs[i].at[peer, ci]))
                    from_sibling.append(fw(src_ref=outs[i].at[peer, ci], dst_ref=outs[i].at[peer, 1 - ci]))
                else:
                    sends.append(mk(src_ref=ins[i], dst_ref=outs[i].at[me]))
                    landed.append(mk(src_ref=ins[i], dst_ref=outs[i].at[peer]))
                    forwards.append(None)
        for cp in loads + sends:
            cp.start()
        for ld, st in zip(loads, stores):
            ld.wait()
            st.start()
        for cp, fw in zip(landed, forwards):
            cp.wait_recv()
            if fw is not None:
                fw.start()
        for cp in from_sibling:
            cp.wait_recv()
        for cp in sends + [fw for fw in forwards if fw is not None]:
            cp.wait_send()
        for st in stores:
            st.wait()

    arrs = list(mats) + list(vecs)
    return pl.pallas_call(
        body, in_specs=[ANY] * n, out_specs=[ANY] * n,
        out_shape=[jax.ShapeDtypeStruct((N_CHIPS,) + a.shape, a.dtype) for a in arrs],
        scratch_shapes=[pltpu.VMEM(a.shape, a.dtype) for a in arrs]
        + [pltpu.SemaphoreType.DMA((n_ici,)), pltpu.SemaphoreType.DMA((n_ici,)),
           pltpu.SemaphoreType.DMA((n_fwd,)), pltpu.SemaphoreType.DMA((n_fwd,)),
           pltpu.SemaphoreType.DMA((n,)), pltpu.SemaphoreType.DMA((n,))],
        compiler_params=_comm_params(), name=name)(*arrs)


def _send_other_half(parts, *, name):
    n = len(parts)

    def body(*refs):
        ins, outs = refs[:n], refs[n:2 * n]
        send_sems, recv_sems = refs[2 * n:]
        xi, yi, ci = _mesh_position()
        sibling = (xi, yi, 1 - ci)
        for i in range(n):
            for s in range(N_CHIPS):
                pltpu.make_async_remote_copy(src_ref=ins[i].at[s, 1 - ci], dst_ref=outs[i].at[s], send_sem=send_sems.at[i],
                                             recv_sem=recv_sems.at[i], device_id=sibling, device_id_type=MESH).start()
        for i in range(n):
            pltpu.make_async_remote_copy(src_ref=outs[i], dst_ref=outs[i], send_sem=send_sems.at[i], recv_sem=recv_sems.at[i],
                                         device_id=sibling, device_id_type=MESH).wait()

    return pl.pallas_call(
        body, in_specs=[ANY] * n, out_specs=[ANY] * n,
        out_shape=[jax.ShapeDtypeStruct((p.shape[0],) + p.shape[2:], p.dtype) for p in parts],
        scratch_shapes=[pltpu.SemaphoreType.DMA((n,)), pltpu.SemaphoreType.DMA((n,))],
        name=name)(*parts)


ROW_BLOCKS = 8


def _add_sibling_half(parts, theirs, core, *, name):
    n = len(parts)

    def body(core_ref, *refs):
        for a_ref, b_ref, o_ref in zip(refs[:n], refs[n:2 * n], refs[2 * n:]):
            o_ref[...] = (a_ref[...].astype(F32) + b_ref[...].astype(F32)).astype(o_ref.dtype)

    mine = lambda p: pl.BlockSpec((None, None, p.shape[2] // ROW_BLOCKS, p.shape[3]), lambda s, rb, core_ref: (s, core_ref[0], rb, 0))
    other = lambda p: pl.BlockSpec((None, p.shape[1] // ROW_BLOCKS, p.shape[2]), lambda s, rb, core_ref: (s, rb, 0))
    return pl.pallas_call(
        body,
        grid_spec=pltpu.PrefetchScalarGridSpec(
            num_scalar_prefetch=1, grid=(N_CHIPS, ROW_BLOCKS),
            in_specs=[mine(p) for p in parts] + [other(q) for q in theirs], out_specs=[other(q) for q in theirs]),
        out_shape=[jax.ShapeDtypeStruct(q.shape, BF16) for q in theirs],
        compiler_params=_params("parallel", "parallel"), name=name)(core, *parts, *theirs)


def _grad_exchange(parts, small, *, name):
    n = len(parts)
    n_ici = (N_CHIPS - 1) * n
    n_peer = N_DEV - 1

    def body(*refs):
        ins, small_ref = refs[:n], refs[n]
        outs, small_all_ref = refs[n + 1:2 * n + 1], refs[2 * n + 1]
        bufs = refs[2 * n + 2:3 * n + 3]
        send_sems, recv_sems, small_send, small_recv, load_sems, store_sems = refs[3 * n + 3:]
        xi, yi, ci = _mesh_position()
        me_chip = 2 * xi + yi
        me = 4 * xi + 2 * yi + ci
        loads, stores = _staged_copies([ins[i].at[me_chip] for i in range(n)] + [small_ref],
                                       [outs[i].at[me_chip] for i in range(n)] + [small_all_ref.at[me]],
                                       bufs, load_sems, store_sems)
        sends, recvs = [], []
        for j, (bx, by) in enumerate(OTHER_CHIPS):
            px, py = _flip(xi, bx), _flip(yi, by)
            peer = 2 * px + py
            for i in range(n):
                k = j * n + i
                mk = functools.partial(pltpu.make_async_remote_copy, src_ref=ins[i].at[peer], send_sem=send_sems.at[k],
                                       recv_sem=recv_sems.at[k], device_id=(px, py, ci), device_id_type=MESH)
                sends.append(mk(dst_ref=outs[i].at[me_chip]))
                recvs.append(mk(dst_ref=outs[i].at[peer]))
        for k in range(1, N_DEV):
            px, py, pc = _flip(xi, (k >> 2) & 1), _flip(yi, (k >> 1) & 1), _flip(ci, k & 1)
            mk = functools.partial(pltpu.make_async_remote_copy, src_ref=small_ref, send_sem=small_send.at[k - 1],
                                   recv_sem=small_recv.at[k - 1], device_id=(px, py, pc), device_id_type=MESH)
            sends.append(mk(dst_ref=small_all_ref.at[me]))
            recvs.append(mk(dst_ref=small_all_ref.at[4 * px + 2 * py + pc]))
        for cp in loads + sends:
            cp.start()
        for ld, st in zip(loads, stores):
            ld.wait()
            st.start()
        for cp in recvs:
            cp.wait_recv()
        for cp in sends:
            cp.wait_send()
        for st in stores:
            st.wait()

    return pl.pallas_call(
        body, in_specs=[ANY] * (n + 1), out_specs=[ANY] * (n + 1),
        out_shape=[jax.ShapeDtypeStruct(p.shape, p.dtype) for p in parts]
        + [jax.ShapeDtypeStruct((N_DEV,) + small.shape, small.dtype)],
        scratch_shapes=[pltpu.VMEM(p.shape[1:], p.dtype) for p in parts] + [pltpu.VMEM(small.shape, small.dtype)]
        + [pltpu.SemaphoreType.DMA((n_ici,)), pltpu.SemaphoreType.DMA((n_ici,)),
           pltpu.SemaphoreType.DMA((n_peer,)), pltpu.SemaphoreType.DMA((n_peer,)),
           pltpu.SemaphoreType.DMA((n + 1,)), pltpu.SemaphoreType.DMA((n + 1,))],
        compiler_params=_comm_params(), name=name)(*parts, small)


def _sum_chips(parts, *, name):
    n = len(parts)
    p = parts[0].shape[0]

    def body(*refs):
        s = pl.program_id(1)
        for x_ref, o_ref in zip(refs[:n], refs[n:]):
            @pl.when(s == 0)
            def _():
                o_ref[...] = x_ref[...].astype(F32)

            @pl.when(s > 0)
            def _():
                o_ref[...] += x_ref[...].astype(F32)

    blocks = lambda q: ROW_BLOCKS if q.shape[1] % (8 * ROW_BLOCKS) == 0 else 1
    assert len({blocks(q) for q in parts}) == 1
    nb = blocks(parts[0])
    return pl.pallas_call(
        body, grid=(nb, p),
        in_specs=[pl.BlockSpec((None, q.shape[1] // nb, q.shape[2]), lambda rb, s: (s, rb, 0)) for q in parts],
        out_specs=[pl.BlockSpec((q.shape[1] // nb, q.shape[2]), lambda rb, s: (rb, 0)) for q in parts],
        out_shape=[jax.ShapeDtypeStruct(q.shape[1:], F32) for q in parts],
        compiler_params=_params("parallel", "arbitrary"), name=name)(*parts)


def _swap_halves(halves, layers, *, name):
    n = len(halves)
    out_shapes, slots = [], []
    for i, h in enumerate(halves):
        pair = [p for p in layers if i in p]
        if pair and pair[0][1] == i:
            slots.append((slots[pair[0][0]][0], 1))
        elif pair:
            out_shapes.append(jax.ShapeDtypeStruct((2, 2) + h.shape, h.dtype))
            slots.append((len(out_shapes) - 1, 0))
        else:
            out_shapes.append(jax.ShapeDtypeStruct((2,) + h.shape, h.dtype))
            slots.append((len(out_shapes) - 1, None))
    n_out = len(out_shapes)

    def body(*refs):
        ins, outs, bufs = refs[:n], refs[n:n + n_out], refs[n + n_out:2 * n + n_out]
        send_sems, recv_sems, load_sems, store_sems = refs[2 * n + n_out:]
        xi, yi, ci = _mesh_position()
        own, sends, recvs = [], [], []
        for i in range(n):
            o, layer = slots[i]
            dst = (lambda core: outs[o].at[core]) if layer is None else (lambda core: outs[o].at[layer, core])
            own.append(dst(ci))
            mk = functools.partial(pltpu.make_async_remote_copy, src_ref=ins[i], send_sem=send_sems.at[i],
                                   recv_sem=recv_sems.at[i], device_id=(xi, yi, 1 - ci), device_id_type=MESH)
            sends.append(mk(dst_ref=dst(ci)))
            recvs.append(mk(dst_ref=dst(1 - ci)))
        loads, stores = _staged_copies(ins, own, bufs, load_sems, store_sems)
        for cp in loads + sends:
            cp.start()
        for ld, st in zip(loads, stores):
            ld.wait()
            st.start()
        for cp in recvs:
            cp.wait_recv()
        for cp in sends:
            cp.wait_send()
        for st in stores:
            st.wait()

    return pl.pallas_call(
        body, in_specs=[ANY] * n, out_specs=[ANY] * n_out, out_shape=out_shapes,
        scratch_shapes=[pltpu.VMEM(h.shape, h.dtype) for h in halves]
        + [pltpu.SemaphoreType.DMA((n,)), pltpu.SemaphoreType.DMA((n,)), pltpu.SemaphoreType.DMA((n,)), pltpu.SemaphoreType.DMA((n,))],
        compiler_params=_comm_params(), name=name)(*halves)


def _adamw(w, g, m, v, *, name):
    r, c = w.shape
    tr = 256 if r % 256 == 0 else r
    blk = pl.BlockSpec((tr, c), lambda i: (i, 0))

    def body(w_ref, g_ref, m_ref, v_ref, d_ref, nm_ref, nv_ref):
        gv = g_ref[...]
        nm = ADAM_B1 * m_ref[...] + (1.0 - ADAM_B1) * gv
        nv = ADAM_B2 * v_ref[...] + (1.0 - ADAM_B2) * (gv * gv)
        m_hat = nm / (1.0 - ADAM_B1 ** ADAM_STEP)
        v_hat = nv / (1.0 - ADAM_B2 ** ADAM_STEP)
        d_ref[...] = -ADAM_LR * (m_hat / (jnp.sqrt(v_hat) + ADAM_EPS) + ADAM_WD * w_ref[...])
        nm_ref[...] = nm
        nv_ref[...] = nv

    sh = jax.ShapeDtypeStruct((r, c), F32)
    return pl.pallas_call(body, grid=(r // tr,), in_specs=[blk] * 4, out_specs=[blk] * 3, out_shape=[sh] * 3,
                          compiler_params=_params("parallel"), name=name)(w, g, m, v)


SM_CONV_B, SM_NORM_W, SM_MIX_PRE, SM_MIX_POST, SM_FFN_PRE, SM_FFN_POST, SM_MISC, SM_CONV_W, SM_B_QKV, SM_B_O = 0, 4, 6, 8, 10, 12, 14, 16, 32, 34
SM_ROWS = 40
MISC_DT_BIAS, MISC_A_LOG, MISC_D, MISC_SINKS, MISC_LOSS = 0, 32, 64, 96, 112


def _shard_halves(a):
    c = a.shape[-1]
    return a.reshape(N_CHIPS, 2, -1, c)


def _rows(v):
    return v.reshape(-1, D_MODEL)


def _misc_row(dt_bias, a_log, d, sinks, loss):
    pad = jnp.zeros((D_MODEL - MISC_LOSS - 1,), F32)
    return jnp.concatenate([dt_bias.reshape(-1), a_log.reshape(-1), d.reshape(-1), sinks.reshape(-1), loss.reshape(1), pad]).reshape(1, D_MODEL)


def _replicated_rows(p, loss):
    return jnp.concatenate([
        _rows(p["ssd_conv_b"]), _rows(p["ssd_norm_w"]), _rows(p["mix_pre_norm"]), _rows(p["mix_post_norm"]),
        _rows(p["ffn_pre_norm"]), _rows(p["ffn_post_norm"]),
        _misc_row(p["ssd_dt_bias"], p["ssd_a_log"], p["ssd_d"], p["attn_sinks"], loss), jnp.zeros((1, D_MODEL), F32)], axis=0)


def _sharded_rows(conv_w, b_qkv, b_o):
    last = jnp.concatenate([b_qkv.reshape(-1), b_o.reshape(-1), jnp.zeros((D_MODEL - 640,), F32)]).reshape(1, D_MODEL)
    return jnp.concatenate([conv_w.reshape(SSD_CONV_WIDTH, D_MODEL), last, jnp.zeros((3, D_MODEL), F32)], axis=0)


REPLICATED = ("ssd_conv_b", "ssd_dt_bias", "ssd_a_log", "ssd_d", "ssd_norm_w", "attn_sinks",
              "mix_pre_norm", "mix_post_norm", "ffn_pre_norm", "ffn_post_norm")
MATRICES = ("ssd_w_in", "ssd_w_out", "attn_w_qkv", "attn_w_o", "mlp_w_up", "mlp_w_down")
WEIGHT_NAMES = ("ssd_w_in", "ssd_conv_w", "ssd_conv_b", "ssd_dt_bias", "ssd_a_log", "ssd_d", "ssd_norm_w", "ssd_w_out",
                "attn_w_qkv", "attn_b_qkv", "attn_sinks", "attn_w_o", "attn_b_o", "mlp_w_up", "mlp_w_down",
                "mix_pre_norm", "mix_post_norm", "ffn_pre_norm", "ffn_post_norm")


def _unpack_small(rows16, rows8, like):
    misc = rows16[SM_MISC]
    out = {
        "ssd_conv_b": rows16[SM_CONV_B:SM_CONV_B + 4], "ssd_norm_w": rows16[SM_NORM_W:SM_NORM_W + 2],
        "mix_pre_norm": rows16[SM_MIX_PRE:SM_MIX_PRE + 2], "mix_post_norm": rows16[SM_MIX_POST:SM_MIX_POST + 2],
        "ffn_pre_norm": rows16[SM_FFN_PRE:SM_FFN_PRE + 2], "ffn_post_norm": rows16[SM_FFN_POST:SM_FFN_POST + 2],
        "ssd_dt_bias": misc[MISC_DT_BIAS:MISC_DT_BIAS + 32], "ssd_a_log": misc[MISC_A_LOG:MISC_A_LOG + 32],
        "ssd_d": misc[MISC_D:MISC_D + 32], "attn_sinks": misc[MISC_SINKS:MISC_SINKS + 16],
        "ssd_conv_w": rows8[0:SSD_CONV_WIDTH], "attn_b_qkv": rows8[SSD_CONV_WIDTH, 0:384], "attn_b_o": rows8[SSD_CONV_WIDTH, 384:640],
    }
    return {k: v.reshape(like[k].shape) for k, v in out.items()}


def kernel(x, ssd_w_in, ssd_conv_w, ssd_conv_b, ssd_dt_bias, ssd_a_log, ssd_d, ssd_norm_w, ssd_w_out, attn_w_qkv, attn_b_qkv, attn_sinks, attn_w_o, attn_b_o, mlp_w_up, mlp_w_down, mix_pre_norm, mix_post_norm, ffn_pre_norm, ffn_post_norm, loss_target, m_ssd_w_in, m_ssd_conv_w, m_ssd_conv_b, m_ssd_dt_bias, m_ssd_a_log, m_ssd_d, m_ssd_norm_w, m_ssd_w_out, m_attn_w_qkv, m_attn_b_qkv, m_attn_sinks, m_attn_w_o, m_attn_b_o, m_mlp_w_up, m_mlp_w_down, m_mix_pre_norm, m_mix_post_norm, m_ffn_pre_norm, m_ffn_post_norm, v_ssd_w_in, v_ssd_conv_w, v_ssd_conv_b, v_ssd_dt_bias, v_ssd_a_log, v_ssd_d, v_ssd_norm_w, v_ssd_w_out, v_attn_w_qkv, v_attn_b_qkv, v_attn_sinks, v_attn_w_o, v_attn_b_o, v_mlp_w_up, v_mlp_w_down, v_mix_pre_norm, v_mix_post_norm, v_ffn_pre_norm, v_ffn_post_norm):
    w = dict(zip(WEIGHT_NAMES, (ssd_w_in, ssd_conv_w, ssd_conv_b, ssd_dt_bias, ssd_a_log, ssd_d, ssd_norm_w, ssd_w_out, attn_w_qkv, attn_b_qkv, attn_sinks, attn_w_o, attn_b_o, mlp_w_up, mlp_w_down, mix_pre_norm, mix_post_norm, ffn_pre_norm, ffn_post_norm)))
    m = dict(zip(WEIGHT_NAMES, (m_ssd_w_in, m_ssd_conv_w, m_ssd_conv_b, m_ssd_dt_bias, m_ssd_a_log, m_ssd_d, m_ssd_norm_w, m_ssd_w_out, m_attn_w_qkv, m_attn_b_qkv, m_attn_sinks, m_attn_w_o, m_attn_b_o, m_mlp_w_up, m_mlp_w_down, m_mix_pre_norm, m_mix_post_norm, m_ffn_pre_norm, m_ffn_post_norm)))
    v = dict(zip(WEIGHT_NAMES, (v_ssd_w_in, v_ssd_conv_w, v_ssd_conv_b, v_ssd_dt_bias, v_ssd_a_log, v_ssd_d, v_ssd_norm_w, v_ssd_w_out, v_attn_w_qkv, v_attn_b_qkv, v_attn_sinks, v_attn_w_o, v_attn_b_o, v_mlp_w_up, v_mlp_w_down, v_mix_pre_norm, v_mix_post_norm, v_ffn_pre_norm, v_ffn_post_norm)))
    chip = 2 * lax.axis_index("x") + lax.axis_index("y")

    two_halves = lambda a: a.astype(BF16).reshape(2, a.shape[0] // 2, a.shape[1])
    mat_shards = [two_halves(a) for a in (w["ssd_w_in"][0], w["ssd_w_out"][0], w["attn_w_qkv"][0], w["attn_w_o"][0],
                                          w["mlp_w_up"][0], w["mlp_w_up"][1], w["mlp_w_down"][0], w["mlp_w_down"][1])]
    g_in, g_out, g_qkv, g_o, g_up0, g_up1, g_down0, g_down1, g_conv, g_bqkv, g_bo = _all_gather_chips(
        mat_shards, [w["ssd_conv_w"][0], w["attn_b_qkv"], w["attn_b_o"]], name="weight_all_gather")
    whole = lambda a: a.reshape(N_CHIPS, 2 * a.shape[2], a.shape[3])
    full = {
        "ssd_w_in": _w_in_from_shards(whole(g_in), name="ssd_w_in_unshard"),
        "ssd_w_out": g_out.reshape(SSD_D_INNER, D_MODEL),
        "attn_w_qkv": whole(g_qkv),
        "attn_w_o": g_o.reshape(D_MODEL, D_MODEL),
        "mlp_w_up": (whole(g_up0), whole(g_up1)),
        "mlp_w_down": (g_down0.reshape(D_FF, D_MODEL), g_down1.reshape(D_FF, D_MODEL)),
        "ssd_conv_w": g_conv.transpose(1, 0, 2).reshape(SSD_CONV_WIDTH, SSD_CONV_DIM),
        "attn_b_qkv": g_bqkv.reshape(ATTN_QKV), "attn_b_o": g_bo.reshape(D_MODEL),
    }
    for name in REPLICATED:
        full[name] = w[name][0] if name.startswith(("ssd_", "attn_")) else w[name]

    loss_tile, grad_x, gm, g = _local_step(x[0], loss_target[0], full)

    parts = [_shard_halves(a) for a in (gm["ssd_w_in"], gm["ssd_w_out"], gm["attn_w_qkv"], gm["attn_w_o"],
                                        gm["mlp_w_up"][0], gm["mlp_w_up"][1], gm["mlp_w_down"][0], gm["mlp_w_down"][1])]
    core = lax.axis_index("c").astype(jnp.int32).reshape(1)
    theirs = _send_other_half(parts, name="grad_sibling_send")
    chip_sums = _add_sibling_half(parts, theirs, core, name="grad_chip_sum")
    conv_w_rows = g["ssd_conv_w"].reshape(SSD_CONV_WIDTH * N_CHIPS, D_MODEL)
    b_qkv_rows = jnp.pad(g["attn_b_qkv"], (0, 2 * D_MODEL - ATTN_QKV)).reshape(2, D_MODEL)
    small = jnp.concatenate([_replicated_rows(g, loss_tile[0, 0]), conv_w_rows, b_qkv_rows, _rows(g["attn_b_o"]),
                             jnp.zeros((SM_ROWS - SM_B_O - 1, D_MODEL), F32)], axis=0)
    *recv, small_all = _grad_exchange(chip_sums, small, name="grad_exchange")
    halves = _sum_chips(recv, name="grad_sum")
    r_in, r_out, r_qkv, r_o, r_up, r_down = _swap_halves(halves, layers=((4, 5), (6, 7)), name="grad_halves_swap")
    small_sum, = _sum_chips([small_all], name="small_grad_sum")

    grads = {"ssd_w_in": r_in, "ssd_w_out": r_out, "attn_w_qkv": r_qkv, "attn_w_o": r_o, "mlp_w_up": r_up, "mlp_w_down": r_down}
    grads = {k: a.reshape(w[k].shape) for k, a in grads.items()}
    conv_w_g = lax.dynamic_index_in_dim(small_sum[SM_CONV_W:SM_CONV_W + 16].reshape(SSD_CONV_WIDTH, N_CHIPS, D_MODEL), chip, axis=1, keepdims=False)
    b_qkv_g = lax.dynamic_slice_in_dim(small_sum[SM_B_QKV:SM_B_QKV + 2].reshape(-1), chip * 384, 384)
    b_o_g = lax.dynamic_slice_in_dim(small_sum[SM_B_O], chip * 256, 256)
    small_g = jnp.concatenate([small_sum[0:16], _sharded_rows(conv_w_g, b_qkv_g, b_o_g)], axis=0)
    grads.update(_unpack_small(small_g[0:16], small_g[16:24], w))
    loss = small_sum[SM_MISC, MISC_LOSS]

    delta, new_m, new_v = {}, {}, {}
    for name in MATRICES:
        shape = w[name].shape
        as2d = lambda a: a.reshape(-1, shape[-1])
        d2, m2, v2 = _adamw(as2d(w[name]), as2d(grads[name]), as2d(m[name]), as2d(v[name]), name=f"adamw_{name}")
        delta[name], new_m[name], new_v[name] = d2.reshape(shape), m2.reshape(shape), v2.reshape(shape)
    zero = jnp.zeros((), F32)
    small_pack = lambda p: jnp.concatenate([_replicated_rows({k: p[k] for k in REPLICATED}, zero),
                                            _sharded_rows(p["ssd_conv_w"], p["attn_b_qkv"], p["attn_b_o"])], axis=0)
    d_s, m_s, v_s = _adamw(small_pack(w), small_g, small_pack(m), small_pack(v), name="adamw_vectors")
    delta.update(_unpack_small(d_s[0:16], d_s[16:24], w))
    new_m.update(_unpack_small(m_s[0:16], m_s[16:24], w))
    new_v.update(_unpack_small(v_s[0:16], v_s[16:24], w))

    return (loss, grad_x[None], *[grads[n] for n in WEIGHT_NAMES], *[delta[n] for n in WEIGHT_NAMES],
            *[new_m[n] for n in WEIGHT_NAMES], *[new_v[n] for n in WEIGHT_NAMES])
```

```python
import functools
import math

import jax
import jax.numpy as jnp
from jax import lax
from jax.experimental import pallas as pl
from jax.experimental.pallas import tpu as pltpu

F32 = jnp.float32
BF16 = jnp.bfloat16

D_MODEL = 1024
SSD_D_INNER = 2048
SSD_HEAD_DIM = 64
SSD_N_HEADS = 32
SSD_N_GROUPS = 8
SSD_HPG = 4
SSD_D_STATE = 128
SSD_CONV_WIDTH = 4
SSD_CHUNK = 128
SSD_CONV_DIM = 4096
SSD_IN_DIM = 6176
SSD_IN_PAD = 6272
SSD_GW = SSD_HPG * SSD_HEAD_DIM
ATTN_HEAD_DIM = 64
ATTN_N_Q = 16
ATTN_N_KV = 4
ATTN_REP = 4
ATTN_WINDOW = 128
ATTN_QKV = 1536
D_FF = 4096
NORM_EPS = 1e-6

ADAM_LR = 0.001
ADAM_B1 = 0.9
ADAM_B2 = 0.999
ADAM_EPS = 1e-08
ADAM_WD = 0.01
ADAM_STEP = 10

N_CHIPS = 4
N_DEV = 8
LANES = 128
VMEM_LIMIT = 48 * 1024 * 1024

MESH = pl.DeviceIdType.MESH


def _params(*sem):
    return pltpu.CompilerParams(dimension_semantics=sem, vmem_limit_bytes=VMEM_LIMIT)


def _dot(a, b, dims):
    return lax.dot_general(a, b, (dims, ((), ())), preferred_element_type=F32)


def _dot_nn(a, b):
    return _dot(a, b, ((1,), (0,)))


def _dot_nt(a, b):
    return _dot(a, b, ((1,), (1,)))


def _dot_tn(a, b):
    return _dot(a, b, ((0,), (0,)))


def _sigmoid(x):
    return 1.0 / (1.0 + jnp.exp(-x))


ANY = pl.BlockSpec(memory_space=pl.ANY)


class _HookSlots:
    def __init__(self, hook, n_in, n_out, n_scratch):
        self.hook = hook
        self.n_in, self.n_out, self.n_scratch = n_in, n_out, n_scratch
        self.inputs = list(hook.arrs) if hook else []
        self.out_shape = list(hook.out_shape) if hook else []
        self.scratch = list(hook.scratch) if hook else []
        self.in_specs = [ANY] * len(self.inputs)
        self.out_specs = [ANY] * len(self.out_shape)

    def _split(self, refs):
        a = self.n_in
        b = a + len(self.inputs)
        c = b + self.n_out
        d = c + len(self.out_shape)
        e = d + self.n_scratch
        return refs[:a], refs[a:b], refs[b:c], refs[c:d], refs[d:e], refs[e:]

    def own(self, refs):
        ins, _, outs, _, scratch, _ = self._split(refs)
        return ins, outs, scratch

    def run(self, refs, step, n_steps):
        _, h_in, _, h_out, _, h_scratch = self._split(refs)
        _run_hook(self.hook, h_in, h_out, h_scratch, step, n_steps)

    def semantics(self, *sem):
        return sem if self.hook is None else ("arbitrary",) * len(sem)


def _matmul(a, b, *, mode, out_dtypes, name, epilogue=None, extras=(), tm=1024, tn=1024, tk=1024,
            b_shards=False, out_shards=False, hook=None):
    if b_shards:
        s, b_rows, b_cols = b.shape
        b2 = (b_rows, s * b_cols)
        if mode == "nn":
            tn = b_cols
        else:
            assert mode == "nt"
            tk = b_cols
    else:
        b2 = b.shape
    if mode == "nn":
        (m, k), (k2, n) = a.shape, b2
    elif mode == "nt":
        (m, k), (n, k2) = a.shape, b2
    else:
        (k, m), (k2, n) = a.shape, b2
    assert k == k2, (a.shape, b.shape, mode)
    tm, tn, tk = min(tm, m), min(tn, n), min(tk, k)
    assert m % tm == 0 and n % tn == 0 and k % tk == 0, (m, n, k, tm, tn, tk)
    nk = k // tk
    if mode == "tn":
        a_spec = pl.BlockSpec((tk, tm), lambda i, j, kk: (kk, i))
    else:
        a_spec = pl.BlockSpec((tm, tk), lambda i, j, kk: (i, kk))
    if b_shards and mode == "nn":
        b_spec = pl.BlockSpec((None, tk, tn), lambda i, j, kk: (j, kk, 0))
    elif b_shards:
        b_spec = pl.BlockSpec((None, tn, tk), lambda i, j, kk: (kk, j, 0))
    elif mode == "nt":
        b_spec = pl.BlockSpec((tn, tk), lambda i, j, kk: (j, kk))
    else:
        b_spec = pl.BlockSpec((tk, tn), lambda i, j, kk: (kk, j))
    dims = {"nn": ((1,), (0,)), "nt": ((1,), (1,)), "tn": ((0,), (0,))}[mode]
    ex_specs = []
    for arr, kind in extras:
        if kind == "tile":
            ex_specs.append(pl.BlockSpec((tm, tn), lambda i, j, kk: (i, j)))
        else:
            ex_specs.append(pl.BlockSpec((1, tn), lambda i, j, kk: (0, j)))
    n_ex, n_out = len(extras), len(out_dtypes)
    if epilogue is None:
        epilogue = lambda acc: (acc,)
    hk = _HookSlots(hook, n_in=2 + n_ex, n_out=n_out, n_scratch=0 if nk == 1 else 1)
    grid = (m // tm, n // tn, nk)

    def body(*refs):
        (a_ref, b_ref, *ex), outs, scratch = hk.own(refs)
        if hook is not None:
            step = (pl.program_id(0) * grid[1] + pl.program_id(1)) * grid[2] + pl.program_id(2)
            hk.run(refs, step, grid[0] * grid[1] * grid[2])

        def finish(acc):
            res = epilogue(acc, *[e[...] for e in ex])
            for o, r in zip(outs, res):
                o[...] = r.astype(o.dtype)

        if nk == 1:
            finish(_dot(a_ref[...], b_ref[...], dims))
        else:
            acc_ref = scratch[0]
            kk = pl.program_id(2)

            @pl.when(kk == 0)
            def _():
                acc_ref[...] = jnp.zeros_like(acc_ref)

            acc_ref[...] += _dot(a_ref[...], b_ref[...], dims)

            @pl.when(kk == nk - 1)
            def _():
                finish(acc_ref[...])

    if out_shards:
        out_spec = pl.BlockSpec((None, tm, tn), lambda i, j, kk: (j, i, 0))
        out_dims = (n // tn, m, tn)
    else:
        out_spec = pl.BlockSpec((tm, tn), lambda i, j, kk: (i, j))
        out_dims = (m, n)
    outs = pl.pallas_call(
        body,
        grid=grid,
        in_specs=[a_spec, b_spec] + ex_specs + hk.in_specs,
        out_specs=[out_spec for _ in out_dtypes] + hk.out_specs,
        out_shape=[jax.ShapeDtypeStruct(out_dims, dt) for dt in out_dtypes] + hk.out_shape,
        scratch_shapes=([] if nk == 1 else [pltpu.VMEM((tm, tn), F32)]) + hk.scratch,
        compiler_params=_params(*hk.semantics("parallel", "parallel", "arbitrary")),
        name=name,
    )(a, b, *[arr for arr, _ in extras], *hk.inputs)
    own = outs[0] if n_out == 1 else outs[:n_out]
    return own if hook is None else (own, outs[n_out:])


def _row_tile(t, want):
    return min(t, want)


def _rms_fwd(x, w, *, name, resid=None, want_u=None):
    t, d = x.shape
    tr = _row_tile(t, 512)

    def norm(v, wv):
        return v * lax.rsqrt(jnp.mean(v * v, axis=-1, keepdims=True) + NORM_EPS) * wv

    row = pl.BlockSpec((tr, d), lambda i: (i, 0))
    vec = pl.BlockSpec((1, d), lambda i: (0, 0))
    if resid is None:
        def body(x_ref, w_ref, o_ref):
            o_ref[...] = norm(x_ref[...], w_ref[...]).astype(BF16)
        ins, in_specs = (x, w), [row, vec]
        out_shape, out_specs = jax.ShapeDtypeStruct((t, d), BF16), row
    elif want_u is None:
        def body(x_ref, w_ref, r_ref, o_ref):
            o_ref[...] = r_ref[...] + norm(x_ref[...], w_ref[...])
        ins, in_specs = (x, w, resid), [row, vec, row]
        out_shape, out_specs = jax.ShapeDtypeStruct((t, d), F32), row
    else:
        def body(x_ref, w_ref, r_ref, w2_ref, o_ref, u_ref):
            h = r_ref[...] + norm(x_ref[...], w_ref[...])
            o_ref[...] = h
            u_ref[...] = norm(h, w2_ref[...]).astype(BF16)
        ins, in_specs = (x, w, resid, want_u), [row, vec, row, vec]
        out_shape = [jax.ShapeDtypeStruct((t, d), F32), jax.ShapeDtypeStruct((t, d), BF16)]
        out_specs = [row, row]
    return pl.pallas_call(body, grid=(t // tr,), in_specs=in_specs, out_specs=out_specs, out_shape=out_shape,
                          compiler_params=_params("parallel"), name=name)(*ins)


def _rms_bwd(x, w, dy, *, name, resid=None, out_dtype=F32):
    t, d = x.shape
    tr = _row_tile(t, 512)
    row = pl.BlockSpec((tr, d), lambda i: (i, 0))
    vec = pl.BlockSpec((1, d), lambda i: (0, 0))
    has_res = resid is not None

    def body(x_ref, w_ref, dy_ref, *rest):
        if has_res:
            r_ref, dx_ref, dw_ref = rest
        else:
            dx_ref, dw_ref = rest
        xv = x_ref[...]
        dyv = dy_ref[...].astype(F32)
        r = lax.rsqrt(jnp.mean(xv * xv, axis=-1, keepdims=True) + NORM_EPS)
        xhat = xv * r
        dyw = dyv * w_ref[...]
        dx = r * (dyw - xhat * jnp.mean(dyw * xhat, axis=-1, keepdims=True))
        if has_res:
            dx = dx + r_ref[...]
        dx_ref[...] = dx.astype(dx_ref.dtype)

        @pl.when(pl.program_id(0) == 0)
        def _():
            dw_ref[...] = jnp.zeros_like(dw_ref)

        dw_ref[...] += jnp.sum(dyv * xhat, axis=0, keepdims=True)

    ins = (x, w, dy) + ((resid,) if has_res else ())
    in_specs = [row, vec, row] + ([row] if has_res else [])
    return pl.pallas_call(
        body, grid=(t // tr,), in_specs=in_specs, out_specs=[row, vec],
        out_shape=[jax.ShapeDtypeStruct((t, d), out_dtype), jax.ShapeDtypeStruct((1, d), F32)],
        compiler_params=_params("arbitrary"), name=name)(*ins)


def _loss_head(h, target, *, name):
    t, d = h.shape
    tr = _row_tile(t, 512)
    row = pl.BlockSpec((tr, d), lambda i: (i, 0))

    def body(h_ref, t_ref, dh_ref, loss_ref):
        err = h_ref[...] - t_ref[...]
        dh_ref[...] = err * (1.0 / d)

        @pl.when(pl.program_id(0) == 0)
        def _():
            loss_ref[...] = jnp.zeros_like(loss_ref)

        part = jnp.sum(jnp.sum(err * err, axis=1, keepdims=True), axis=0, keepdims=True) * (0.5 / d)
        loss_ref[...] += jnp.broadcast_to(part, loss_ref.shape)

    return pl.pallas_call(
        body, grid=(t // tr,), in_specs=[row, row],
        out_specs=[row, pl.BlockSpec((8, LANES), lambda i: (0, 0))],
        out_shape=[jax.ShapeDtypeStruct((t, d), F32), jax.ShapeDtypeStruct((8, LANES), F32)],
        compiler_params=_params("arbitrary"), name=name)(h, target)


def _col_sum(x, *, name):
    t, n = x.shape
    tr = _row_tile(t, 512)

    def body(x_ref, o_ref):
        @pl.when(pl.program_id(0) == 0)
        def _():
            o_ref[...] = jnp.zeros_like(o_ref)

        o_ref[...] += jnp.sum(x_ref[...].astype(F32), axis=0, keepdims=True)

    return pl.pallas_call(
        body, grid=(t // tr,), in_specs=[pl.BlockSpec((tr, n), lambda i: (i, 0))],
        out_specs=pl.BlockSpec((1, n), lambda i: (0, 0)), out_shape=jax.ShapeDtypeStruct((1, n), F32),
        compiler_params=_params("arbitrary"), name=name)(x)


SSD_IN_SHARD = SSD_IN_DIM // N_CHIPS


def _w_in_from_shards(shards, *, name):
    d = shards.shape[1]
    tr = 256

    def body(s_ref, o_ref):
        o_ref[:, pl.ds(SSD_IN_PAD - LANES, LANES)] = jnp.zeros((tr, LANES), o_ref.dtype)
        for s in range(N_CHIPS):
            o_ref[:, pl.ds(SSD_IN_SHARD * s, SSD_IN_SHARD)] = s_ref[s]

    return pl.pallas_call(
        body, grid=(d // tr,), in_specs=[pl.BlockSpec((N_CHIPS, tr, SSD_IN_SHARD), lambda i: (0, i, 0))],
        out_specs=pl.BlockSpec((tr, SSD_IN_PAD), lambda i: (i, 0)),
        out_shape=jax.ShapeDtypeStruct((d, SSD_IN_PAD), shards.dtype),
        compiler_params=_params("parallel"), name=name)(shards)


def _w_in_to_shards(g, *, name):
    d = g.shape[0]
    tr = 256

    def body(g_ref, o_ref):
        for s in range(N_CHIPS):
            o_ref[s] = g_ref[:, pl.ds(SSD_IN_SHARD * s, SSD_IN_SHARD)].astype(o_ref.dtype)

    return pl.pallas_call(
        body, grid=(d // tr,), in_specs=[pl.BlockSpec((tr, SSD_IN_PAD), lambda i: (i, 0))],
        out_specs=pl.BlockSpec((N_CHIPS, tr, SSD_IN_SHARD), lambda i: (0, i, 0)),
        out_shape=jax.ShapeDtypeStruct((N_CHIPS, d, SSD_IN_SHARD), BF16),
        compiler_params=_params("parallel"), name=name)(g)


XBC_COL0 = SSD_D_INNER // LANES
DT_COL0 = (SSD_D_INNER + SSD_CONV_DIM) // LANES


def _shift_down(v, k, row_ids):
    return jnp.where(row_ids >= k, pltpu.roll(v, k, axis=0), 0.0)


def _shift_up(v, k, row_ids):
    n = v.shape[0]
    return jnp.where(row_ids < n - k, pltpu.roll(v, n - k, axis=0), 0.0)


def _conv_pre(x, w, b, row_ids):
    pre = b + w[3:4, :] * x
    for k in (1, 2, 3):
        pre = pre + w[3 - k:4 - k, :] * _shift_down(x, k, row_ids)
    return pre


def _conv_fwd(zx, conv_w, conv_b, *, name, hook=None):
    t = zx.shape[0]
    nct = SSD_CONV_DIM // LANES
    hk = _HookSlots(hook, n_in=3, n_out=1, n_scratch=0)

    def body(*refs):
        (x_ref, w_ref, b_ref), (o_ref,), _ = hk.own(refs)
        if hook is not None:
            hk.run(refs, pl.program_id(0), nct)
        x = x_ref[...]
        row_ids = lax.broadcasted_iota(jnp.int32, x.shape, 0)
        pre = _conv_pre(x, w_ref[...], b_ref[...], row_ids)
        o_ref[...] = pre * _sigmoid(pre)

    outs = pl.pallas_call(
        body, grid=(nct,),
        in_specs=[pl.BlockSpec((t, LANES), lambda j: (0, XBC_COL0 + j)),
                  pl.BlockSpec((SSD_CONV_WIDTH, LANES), lambda j: (0, j)),
                  pl.BlockSpec((1, LANES), lambda j: (0, j))] + hk.in_specs,
        out_specs=[pl.BlockSpec((t, LANES), lambda j: (0, j))] + hk.out_specs,
        out_shape=[jax.ShapeDtypeStruct((t, SSD_CONV_DIM), F32)] + hk.out_shape,
        scratch_shapes=hk.scratch,
        compiler_params=_params(*hk.semantics("parallel")), name=name)(zx, conv_w, conv_b, *hk.inputs)
    return outs[0] if hook is None else (outs[0], outs[1:])


def _conv_bwd(zx, conv_w, conv_b, d_xs, d_bm, d_cm, *, name):
    t = zx.shape[0]
    nct = SSD_CONV_DIM // LANES
    n_xs = SSD_D_INNER // LANES
    n_bm = SSD_N_GROUPS * SSD_D_STATE // LANES

    def body(x_ref, w_ref, b_ref, dxs_ref, dbm_ref, dcm_ref, dx_ref, dw_ref, db_ref):
        x = x_ref[...]
        w = w_ref[...]
        j = pl.program_id(0)
        dy = jnp.where(j < n_xs, dxs_ref[...], jnp.where(j < n_xs + n_bm, dbm_ref[...], dcm_ref[...]))
        row_ids = lax.broadcasted_iota(jnp.int32, x.shape, 0)
        pre = _conv_pre(x, w, b_ref[...], row_ids)
        sg = _sigmoid(pre)
        dpre = dy * (sg * (1.0 + pre * (1.0 - sg)))
        dx = w[3:4, :] * dpre
        for k in (1, 2, 3):
            dx = dx + w[3 - k:4 - k, :] * _shift_up(dpre, k, row_ids)
        dx_ref[...] = dx.astype(dx_ref.dtype)
        db_ref[...] = jnp.sum(dpre, axis=0, keepdims=True)
        dw_ref[3:4, :] = jnp.sum(dpre * x, axis=0, keepdims=True)
        for k in (1, 2, 3):
            dw_ref[3 - k:4 - k, :] = jnp.sum(dpre * _shift_down(x, k, row_ids), axis=0, keepdims=True)

    col = pl.BlockSpec((t, LANES), lambda j: (0, j))
    clip = lambda j, lo, n: jnp.clip(j - lo, 0, n - 1)
    return pl.pallas_call(
        body, grid=(nct,),
        in_specs=[pl.BlockSpec((t, LANES), lambda j: (0, XBC_COL0 + j)),
                  pl.BlockSpec((SSD_CONV_WIDTH, LANES), lambda j: (0, j)),
                  pl.BlockSpec((1, LANES), lambda j: (0, j)),
                  pl.BlockSpec((t, LANES), lambda j: (0, clip(j, 0, n_xs))),
                  pl.BlockSpec((t, LANES), lambda j: (0, clip(j, n_xs, n_bm))),
                  pl.BlockSpec((t, LANES), lambda j: (0, clip(j, n_xs + n_bm, n_bm)))],
        out_specs=[col, pl.BlockSpec((SSD_CONV_WIDTH, LANES), lambda j: (0, j)), pl.BlockSpec((1, LANES), lambda j: (0, j))],
        out_shape=[jax.ShapeDtypeStruct((t, SSD_CONV_DIM), BF16),
                   jax.ShapeDtypeStruct((SSD_CONV_WIDTH, SSD_CONV_DIM), F32),
                   jax.ShapeDtypeStruct((1, SSD_CONV_DIM), F32)],
        compiler_params=_params("parallel"), name=name)(zx, conv_w, conv_b, d_xs, d_bm, d_cm)


def _softplus_fwd(zx, bias_row, *, name):
    t = zx.shape[0]
    tr = _row_tile(t, 1024)

    def body(x_ref, b_ref, o_ref):
        v = x_ref[...] + b_ref[...]
        e = jnp.exp(-jnp.abs(v))
        u = 1.0 + e
        log1p = jnp.where(u == 1.0, e, jnp.log(u) * (e / (u - 1.0)))
        o_ref[...] = jnp.maximum(v, 0.0) + log1p

    return pl.pallas_call(
        body, grid=(t // tr,),
        in_specs=[pl.BlockSpec((tr, LANES), lambda i: (i, DT_COL0)), pl.BlockSpec((1, LANES), lambda i: (0, 0))],
        out_specs=pl.BlockSpec((tr, LANES), lambda i: (i, 0)),
        out_shape=jax.ShapeDtypeStruct((t, LANES), F32),
        compiler_params=_params("parallel"), name=name)(zx, bias_row)


def _softplus_bwd(zx, bias_row, ddt, *, name):
    t = zx.shape[0]
    tr = _row_tile(t, 1024)

    def body(x_ref, b_ref, g_ref, o_ref, db_ref):
        v = x_ref[...] + b_ref[...]
        lane = lax.broadcasted_iota(jnp.int32, v.shape, 1)
        d = jnp.where(lane < SSD_N_HEADS, g_ref[...] * _sigmoid(v), 0.0)
        o_ref[...] = d.astype(o_ref.dtype)

        @pl.when(pl.program_id(0) == 0)
        def _():
            db_ref[...] = jnp.zeros_like(db_ref)

        db_ref[...] += jnp.sum(d, axis=0, keepdims=True)

    return pl.pallas_call(
        body, grid=(t // tr,),
        in_specs=[pl.BlockSpec((tr, LANES), lambda i: (i, DT_COL0)), pl.BlockSpec((1, LANES), lambda i: (0, 0)),
                  pl.BlockSpec((tr, LANES), lambda i: (i, 0))],
        out_specs=[pl.BlockSpec((tr, LANES), lambda i: (i, 0)), pl.BlockSpec((1, LANES), lambda i: (0, 0))],
        out_shape=[jax.ShapeDtypeStruct((t, LANES), BF16), jax.ShapeDtypeStruct((1, LANES), F32)],
        compiler_params=_params("arbitrary"), name=name)(zx, bias_row, ddt)


def _ssd_masks():
    q = SSD_CHUNK
    tt = lax.broadcasted_iota(jnp.int32, (q, q), 0)
    ss = lax.broadcasted_iota(jnp.int32, (q, q), 1)
    lane = lax.broadcasted_iota(jnp.int32, (1, SSD_GW), 1)
    srow = lax.broadcasted_iota(jnp.int32, (SSD_GW, 1), 0)
    hm = [(lane >= SSD_HEAD_DIM * j) & (lane < SSD_HEAD_DIM * (j + 1)) for j in range(SSD_HPG)]
    rm = [(srow >= SSD_HEAD_DIM * j) & (srow < SSD_HEAD_DIM * (j + 1)) for j in range(SSD_HPG)]
    return tt, ss, hm, rm


def _ssd_head_terms(dtc, dtr, a_rows, j, tt, ss):
    q = SSD_CHUNK
    dt_col = dtc[:, j:j + 1]
    dt_row = dtr[j:j + 1, :]
    a_row1 = a_rows[j:j + 1, :]
    a_11 = a_rows[j:j + 1, 0:1]
    cum_col = jnp.sum(jnp.where(ss <= tt, dt_row * a_row1, 0.0), axis=1, keepdims=True)
    cum_row = jnp.sum(jnp.where(tt <= ss, dt_col * a_11, 0.0), axis=0, keepdims=True)
    decay = jnp.exp(jnp.where(ss <= tt, cum_col - cum_row, -jnp.inf))
    cum_last = cum_col[q - 1:q, :]
    e_col = jnp.exp(cum_col)
    dte_col = jnp.exp(cum_last - cum_col)
    e_last = jnp.exp(cum_last)
    return dt_col, dt_row, a_row1, a_11, decay, e_col, dte_col, e_last


SSD_CHUNKS_PER_STEP = 4
SSD_BC_COL0 = SSD_D_INNER // SSD_D_STATE


def _ssd_head_selects(terms, hm, rm):
    e_all = jnp.zeros((SSD_CHUNK, SSD_GW), F32)
    w_all = jnp.zeros((SSD_CHUNK, SSD_GW), F32)
    e_s = jnp.zeros((SSD_GW, 1), F32)
    for j in range(SSD_HPG):
        dt_col, _, _, _, _, e_col, dte_col, e_last = terms[j]
        e_all = jnp.where(hm[j], e_col, e_all)
        w_all = jnp.where(hm[j], dt_col * dte_col, w_all)
        e_s = jnp.where(rm[j], e_last, e_s)
    return e_all, w_all, e_s


def _ssd_fwd(xc, dtc, dtr, alog_b, d_b, *, name, hook=None):
    t = xc.shape[0]
    q = SSD_CHUNK
    nc = t // q
    kc = min(SSD_CHUNKS_PER_STEP, nc)
    rows = kc * q
    hk = _HookSlots(hook, n_in=7, n_out=2, n_scratch=1)

    def body(*refs):
        (x_ref, b_ref, c_ref, dtc_ref, dtr_ref, alog_ref, d_ref), (y_ref, st_ref), (s_scr,) = hk.own(refs)
        if hook is not None:
            hk.run(refs, pl.program_id(0) * (nc // kc) + pl.program_id(1), SSD_N_GROUPS * (nc // kc))

        @pl.when(pl.program_id(1) == 0)
        def _():
            s_scr[...] = jnp.zeros_like(s_scr)

        tt, ss, hm, rm = _ssd_masks()
        a_rows = -jnp.exp(alog_ref[...])
        d_rows = d_ref[...]
        d_all = jnp.zeros((1, SSD_GW), F32)
        for j in range(SSD_HPG):
            d_all = jnp.where(hm[j], d_rows[j:j + 1, 0:1], d_all)
        ks, hs = range(kc), range(SSD_HPG)
        sl = [pl.ds(k * q, q) for k in ks]
        x = [x_ref[sl[k], :] for k in ks]
        bm = [b_ref[sl[k], :].astype(BF16) for k in ks]
        cm = [c_ref[sl[k], :].astype(BF16) for k in ks]
        xb = [x[k].astype(BF16) for k in ks]
        terms = [[_ssd_head_terms(dtc_ref[sl[k], :], dtr_ref[:, sl[k]], a_rows, j, tt, ss) for j in hs] for k in ks]
        g = [_dot_nt(cm[k], bm[k]) for k in ks]
        m = [[(g[k] * terms[k][j][4] * terms[k][j][1]).astype(BF16) for j in hs] for k in ks]
        yj = [[_dot_nn(m[k][j], xb[k]) for j in hs] for k in ks]
        sel = [_ssd_head_selects(terms[k], hm, rm) for k in ks]
        upd = [_dot_tn((x[k] * sel[k][1]).astype(BF16), bm[k]) for k in ks]
        states = [s_scr[...]]
        for k in ks:
            states.append(states[k] * sel[k][2] + upd[k])
        inter = [_dot_nt(cm[k], states[k].astype(BF16)) for k in ks]
        ys = []
        for k in ks:
            y = jnp.zeros((q, SSD_GW), F32)
            for j in hs:
                y = jnp.where(hm[j], yj[k][j], y)
            ys.append(y + inter[k] * sel[k][0] + x[k] * d_all)
        for k in ks:
            st_ref[k] = states[k]
        y_ref[...] = jnp.concatenate(ys, axis=0)
        s_scr[...] = states[kc]

    blk = lambda width, off: pl.BlockSpec((rows, width), lambda g, c: (c, off + g))
    par_s = pl.BlockSpec((None, SSD_HPG, LANES), lambda g, c: (g, 0, 0))
    outs = pl.pallas_call(
        body, grid=(SSD_N_GROUPS, nc // kc),
        in_specs=[blk(SSD_GW, 0), blk(SSD_D_STATE, SSD_BC_COL0), blk(SSD_D_STATE, SSD_BC_COL0 + SSD_N_GROUPS),
                  pl.BlockSpec((None, rows, SSD_HPG), lambda g, c: (g, c, 0)),
                  pl.BlockSpec((None, SSD_HPG, rows), lambda g, c: (g, 0, c)), par_s, par_s] + hk.in_specs,
        out_specs=[blk(SSD_GW, 0), pl.BlockSpec((None, kc, SSD_GW, SSD_D_STATE), lambda g, c: (g, c, 0, 0))] + hk.out_specs,
        out_shape=[jax.ShapeDtypeStruct((t, SSD_D_INNER), F32),
                   jax.ShapeDtypeStruct((SSD_N_GROUPS, nc, SSD_GW, SSD_D_STATE), F32)] + hk.out_shape,
        scratch_shapes=[pltpu.VMEM((SSD_GW, SSD_D_STATE), F32)] + hk.scratch,
        compiler_params=_params(*hk.semantics("parallel", "arbitrary")), name=name)(xc, xc, xc, dtc, dtr, alog_b, d_b, *hk.inputs)
    return outs if hook is None else (outs[:2], outs[2:])


def _ssd_bwd(xc, dtc, dtr, alog_b, d_b, states, dy, *, name, hook=None):
    t = xc.shape[0]
    q = SSD_CHUNK
    nc = t // q
    kc = min(SSD_CHUNKS_PER_STEP, nc)
    nst = nc // kc
    rows = kc * q
    rev = lambda c: nst - 1 - c
    hk = _HookSlots(hook, n_in=9, n_out=5, n_scratch=1)

    def body(*refs):
        ((x_ref, b_ref, c_ref, dtc_ref, dtr_ref, alog_ref, d_ref, st_ref, dy_ref),
         (dx_ref, db_ref, dc_ref, ddt_ref, dpar_ref), (ds_scr,)) = hk.own(refs)
        if hook is not None:
            hk.run(refs, pl.program_id(0) * nst + pl.program_id(1), SSD_N_GROUPS * nst)

        @pl.when(pl.program_id(1) == 0)
        def _():
            ds_scr[...] = jnp.zeros_like(ds_scr)
            dpar_ref[...] = jnp.zeros_like(dpar_ref)

        tt, ss, hm, rm = _ssd_masks()
        tcol = lax.broadcasted_iota(jnp.int32, (q, 1), 0)
        lane = lax.broadcasted_iota(jnp.int32, (1, LANES), 1)
        a_rows = -jnp.exp(alog_ref[...])
        d_rows = d_ref[...]
        d_all = jnp.zeros((1, SSD_GW), F32)
        for j in range(SSD_HPG):
            d_all = jnp.where(hm[j], d_rows[j:j + 1, 0:1], d_all)
        ks, hs = range(kc), range(SSD_HPG)
        sl = [pl.ds(k * q, q) for k in ks]
        x = [x_ref[sl[k], :] for k in ks]
        dyv = [dy_ref[sl[k], :] for k in ks]
        bm = [b_ref[sl[k], :].astype(BF16) for k in ks]
        cm = [c_ref[sl[k], :].astype(BF16) for k in ks]
        s_in = [st_ref[k] for k in ks]
        xb = [x[k].astype(BF16) for k in ks]
        dyb = [dyv[k].astype(BF16) for k in ks]
        s_b = [s_in[k].astype(BF16) for k in ks]
        terms = [[_ssd_head_terms(dtc_ref[sl[k], :], dtr_ref[:, sl[k]], a_rows, j, tt, ss) for j in hs] for k in ks]
        sel = [_ssd_head_selects(terms[k], hm, rm) for k in ks]
        e_all, w_all, e_s = [s_[0] for s_ in sel], [s_[1] for s_ in sel], [s_[2] for s_ in sel]
        dye = [(dyv[k] * e_all[k]).astype(BF16) for k in ks]
        ds_loc = [_dot_tn(dye[k], cm[k]) for k in ks]
        ds = [None] * kc
        running = ds_scr[...]
        for k in reversed(ks):
            ds[k] = running
            running = running * e_s[k] + ds_loc[k]
        ds_scr[...] = running
        ds_b = [ds[k].astype(BF16) for k in ks]
        g = [_dot_nt(cm[k], bm[k]) for k in ks]
        cs = [_dot_nt(cm[k], s_b[k]) for k in ks]
        bds = [_dot_nt(bm[k], ds_b[k]) for k in ks]
        dm = [[_dot_nt(jnp.where(hm[j], dyv[k], 0.0).astype(BF16), xb[k]) for j in hs] for k in ks]
        gl = [[g[k] * terms[k][j][4] for j in hs] for k in ks]
        wp = [[dm[k][j] * gl[k][j] for j in hs] for k in ks]
        mt = [[(gl[k][j] * terms[k][j][1]).astype(BF16) for j in hs] for k in ks]
        dxj = [[_dot_tn(mt[k][j], dyb[k]) for j in hs] for k in ks]
        dg = []
        for k in ks:
            acc = jnp.zeros((q, q), F32)
            for j in hs:
                acc = acc + dm[k][j] * terms[k][j][4] * terms[k][j][1]
            dg.append(acc.astype(BF16))
        dy_cs = [dyv[k] * cs[k] for k in ks]
        x_bds = [x[k] * bds[k] for k in ks]
        dy_x = [dyv[k] * x[k] for k in ks]
        ds_s = [ds[k] * s_in[k] for k in ks]
        w = [[wp[k][j] * terms[k][j][1] for j in hs] for k in ks]
        rw_col = [[jnp.sum(w[k][j], axis=1, keepdims=True) for j in hs] for k in ks]
        cw_row = [[jnp.sum(w[k][j], axis=0, keepdims=True) for j in hs] for k in ks]
        cwp_row = [[jnp.sum(wp[k][j], axis=0, keepdims=True) for j in hs] for k in ks]
        r1_col = [[jnp.sum(jnp.where(hm[j], dy_cs[k], 0.0), axis=1, keepdims=True) * terms[k][j][5] for j in hs] for k in ks]
        dw_col = [[jnp.sum(jnp.where(hm[j], x_bds[k], 0.0), axis=1, keepdims=True) for j in hs] for k in ks]
        total = lambda v: jnp.sum(jnp.sum(v, axis=1, keepdims=True), axis=0, keepdims=True)
        s_sum = [[total(jnp.where(rm[j], ds_s[k], 0.0)) for j in hs] for k in ks]
        d_d = [[total(jnp.where(hm[j], dy_x[k], 0.0)) for j in hs] for k in ks]
        ddt_rows = [[None] * SSD_HPG for _ in ks]
        dpar = [jnp.zeros((1, LANES), F32) for _ in hs]
        for k in ks:
            for j in hs:
                dt_col, dt_row, a_row1, a_11, _, _, dte_col, e_last = terms[k][j]
                dww = dw_col[k][j] * (dt_col * dte_col)
                last_add = jnp.sum(dww, axis=0, keepdims=True) + e_last * s_sum[k][j]
                dcum_col = rw_col[k][j] + r1_col[k][j] - dww + jnp.where(tcol == q - 1, last_add, 0.0)
                da_row = jnp.sum(jnp.where(tt >= ss, dcum_col, 0.0), axis=0, keepdims=True)
                da_col = jnp.sum(jnp.where(ss >= tt, -cw_row[k][j], 0.0), axis=1, keepdims=True)
                ddt_col = a_11 * da_col + dw_col[k][j] * dte_col
                ddt_rows[k][j] = (a_row1 * da_row + cwp_row[k][j]
                                  + jnp.sum(jnp.where(tt == ss, ddt_col, 0.0), axis=0, keepdims=True))
                d_a = jnp.sum(dt_row * da_row, axis=1, keepdims=True) + jnp.sum(dt_col * da_col, axis=0, keepdims=True)
                dpar[j] = dpar[j] + jnp.where(lane == 0, d_a * a_11, 0.0) + jnp.where(lane == 1, d_d[k][j], 0.0)
        dxs = []
        for k in ks:
            acc = jnp.zeros((q, SSD_GW), F32)
            for j in hs:
                acc = jnp.where(hm[j], dxj[k][j], acc)
            dxs.append(acc + w_all[k] * bds[k] + d_all * dyv[k])
        xw = [(x[k] * w_all[k]).astype(BF16) for k in ks]
        dc = [_dot_nn(dg[k], bm[k]) + _dot_nn(dye[k], s_b[k]) for k in ks]
        db = [_dot_tn(dg[k], cm[k]) + _dot_nn(xw[k], ds_b[k]) for k in ks]
        dx_ref[...] = jnp.concatenate(dxs, axis=0)
        dc_ref[...] = jnp.concatenate(dc, axis=0)
        db_ref[...] = jnp.concatenate(db, axis=0)
        ddt_ref[...] = jnp.concatenate([jnp.concatenate([ddt_rows[k][j] for k in ks], axis=1) for j in hs], axis=0)
        dpar_ref[...] += jnp.concatenate(dpar, axis=0)

    blk = lambda width, off: pl.BlockSpec((rows, width), lambda g, c: (rev(c), off + g))
    par_s = pl.BlockSpec((None, SSD_HPG, LANES), lambda g, c: (g, 0, 0))
    outs = pl.pallas_call(
        body, grid=(SSD_N_GROUPS, nst),
        in_specs=[blk(SSD_GW, 0), blk(SSD_D_STATE, SSD_BC_COL0), blk(SSD_D_STATE, SSD_BC_COL0 + SSD_N_GROUPS),
                  pl.BlockSpec((None, rows, SSD_HPG), lambda g, c: (g, rev(c), 0)),
                  pl.BlockSpec((None, SSD_HPG, rows), lambda g, c: (g, 0, rev(c))), par_s, par_s,
                  pl.BlockSpec((None, kc, SSD_GW, SSD_D_STATE), lambda g, c: (g, rev(c), 0, 0)), blk(SSD_GW, 0)] + hk.in_specs,
        out_specs=[blk(SSD_GW, 0), blk(SSD_D_STATE, 0), blk(SSD_D_STATE, 0),
                   pl.BlockSpec((None, SSD_HPG, rows), lambda g, c: (g, 0, rev(c))), par_s] + hk.out_specs,
        out_shape=[jax.ShapeDtypeStruct((t, SSD_D_INNER), F32),
                   jax.ShapeDtypeStruct((t, SSD_N_GROUPS * SSD_D_STATE), F32),
                   jax.ShapeDtypeStruct((t, SSD_N_GROUPS * SSD_D_STATE), F32),
                   jax.ShapeDtypeStruct((SSD_N_GROUPS, SSD_HPG, t), F32),
                   jax.ShapeDtypeStruct((SSD_N_GROUPS, SSD_HPG, LANES), F32)] + hk.out_shape,
        scratch_shapes=[pltpu.VMEM((SSD_GW, SSD_D_STATE), F32)] + hk.scratch,
        compiler_params=_params(*hk.semantics("parallel", "arbitrary")), name=name)(
            xc, xc, xc, dtc, dtr, alog_b, d_b, states, dy, *hk.inputs)
    return outs if hook is None else (outs[:5], outs[5:])


def _gate_norm_fwd(y, zx, norm_w, *, name):
    t = y.shape[0]
    tr = _row_tile(t, 256)
    row = pl.BlockSpec((tr, SSD_D_INNER), lambda i: (i, 0))

    def body(y_ref, z_ref, w_ref, o_ref):
        for gi in range(SSD_N_GROUPS):
            sl = pl.ds(gi * SSD_GW, SSD_GW)
            z = z_ref[:, sl]
            gv = y_ref[:, sl] * (z * _sigmoid(z))
            r = lax.rsqrt(jnp.mean(gv * gv, axis=-1, keepdims=True) + NORM_EPS)
            o_ref[:, sl] = (gv * r * w_ref[:, sl]).astype(BF16)

    return pl.pallas_call(
        body, grid=(t // tr,), in_specs=[row, row, pl.BlockSpec((1, SSD_D_INNER), lambda i: (0, 0))],
        out_specs=row, out_shape=jax.ShapeDtypeStruct((t, SSD_D_INNER), BF16),
        compiler_params=_params("parallel"), name=name)(y, zx, norm_w)


def _gate_norm_bwd(y, zx, norm_w, dyn, *, name):
    t = y.shape[0]
    tr = _row_tile(t, 256)
    row = pl.BlockSpec((tr, SSD_D_INNER), lambda i: (i, 0))
    vec = pl.BlockSpec((1, SSD_D_INNER), lambda i: (0, 0))

    def body(y_ref, z_ref, w_ref, dyn_ref, dy_ref, dz_ref, dw_ref):
        @pl.when(pl.program_id(0) == 0)
        def _():
            dw_ref[...] = jnp.zeros_like(dw_ref)

        for gi in range(SSD_N_GROUPS):
            sl = pl.ds(gi * SSD_GW, SSD_GW)
            z = z_ref[:, sl]
            yv = y_ref[:, sl]
            sg = _sigmoid(z)
            sz = z * sg
            gv = yv * sz
            r = lax.rsqrt(jnp.mean(gv * gv, axis=-1, keepdims=True) + NORM_EPS)
            ghat = gv * r
            dout = dyn_ref[:, sl].astype(F32)
            dgh = dout * w_ref[:, sl]
            dgv = r * (dgh - ghat * jnp.mean(dgh * ghat, axis=-1, keepdims=True))
            dy_ref[:, sl] = dgv * sz
            dz_ref[:, sl] = (dgv * yv * (sg * (1.0 + z * (1.0 - sg)))).astype(dz_ref.dtype)
            dw_ref[:, sl] += jnp.sum(dout * ghat, axis=0, keepdims=True)

    return pl.pallas_call(
        body, grid=(t // tr,), in_specs=[row, row, vec, row], out_specs=[row, row, vec],
        out_shape=[jax.ShapeDtypeStruct((t, SSD_D_INNER), F32), jax.ShapeDtypeStruct((t, SSD_D_INNER), BF16),
                   jax.ShapeDtypeStruct((1, SSD_D_INNER), F32)],
        compiler_params=_params("arbitrary"), name=name)(y, zx, norm_w, dyn)


ATTN_KV_W = ATTN_N_KV * ATTN_HEAD_DIM
ATTN_Q_HALF = 512
ATTN_K_BLK = ATTN_N_Q * ATTN_HEAD_DIM // ATTN_KV_W
ATTN_V_BLK = ATTN_K_BLK + 1


def _attn_valid(first_block):
    w = ATTN_WINDOW
    qpos = lax.broadcasted_iota(jnp.int32, (w, 2 * w), 0) + w
    kpos = lax.broadcasted_iota(jnp.int32, (w, 2 * w), 1)
    rel = qpos - kpos
    return (rel >= 0) & (rel < w) & jnp.logical_not(first_block & (kpos < w))


def _attn_block_views(lo_ref, hi_ref, kc_ref, kp_ref, vc_ref, vp_ref):
    hd = ATTN_HEAD_DIM
    per_half = ATTN_Q_HALF // hd
    heads = [(lo_ref if h < per_half else hi_ref)[:, pl.ds((h % per_half) * hd, hd)] for h in range(ATTN_N_Q)]
    kv_cols = [pl.ds(kh * hd, hd) for kh in range(ATTN_N_KV)]
    kb = [jnp.concatenate([kp_ref[:, c], kc_ref[:, c]], axis=0) for c in kv_cols]
    vb = [jnp.concatenate([vp_ref[:, c], vc_ref[:, c]], axis=0) for c in kv_cols]
    return heads, kb, vb


def _attn_softmax(q, kb, sink, valid):
    heads = range(ATTN_N_Q)
    scale = ATTN_HEAD_DIM ** -0.5
    s = [jnp.where(valid, _dot_nt(q[h], kb[h // ATTN_REP]) * scale, -jnp.inf) for h in heads]
    m = [jnp.maximum(jnp.max(s[h], axis=1, keepdims=True), sink[h]) for h in heads]
    e = [jnp.exp(s[h] - m[h]) for h in heads]
    es = [jnp.exp(sink[h] - m[h]) for h in heads]
    inv = [1.0 / (jnp.sum(e[h], axis=1, keepdims=True) + es[h]) for h in heads]
    return e, es, inv


def _attn_fwd(qkv, sinks_b, *, name):
    t = qkv.shape[0]
    w = ATTN_WINDOW
    nb = t // w
    prev = lambda n: jnp.maximum(n - 1, 0)

    def body(qlo_ref, qhi_ref, kc_ref, kp_ref, vc_ref, vp_ref, sink_ref, o_ref):
        heads = range(ATTN_N_Q)
        q, kb, vb = _attn_block_views(qlo_ref, qhi_ref, kc_ref, kp_ref, vc_ref, vp_ref)
        sink = [sink_ref[h:h + 1, 0:1] for h in heads]
        e, _, inv = _attn_softmax(q, kb, sink, _attn_valid(pl.program_id(0) == 0))
        out = [_dot_nn((e[h] * inv[h]).astype(BF16), vb[h // ATTN_REP]).astype(o_ref.dtype) for h in heads]
        o_ref[...] = jnp.concatenate(out, axis=1)

    qh = lambda half: pl.BlockSpec((w, ATTN_Q_HALF), lambda n: (n, half))
    kv = lambda blk, idx: pl.BlockSpec((w, ATTN_KV_W), lambda n: (idx(n), blk))
    cur = lambda n: n
    return pl.pallas_call(
        body, grid=(nb,),
        in_specs=[qh(0), qh(1), kv(ATTN_K_BLK, cur), kv(ATTN_K_BLK, prev), kv(ATTN_V_BLK, cur), kv(ATTN_V_BLK, prev),
                  pl.BlockSpec((ATTN_N_Q, LANES), lambda n: (0, 0))],
        out_specs=pl.BlockSpec((w, D_MODEL), lambda n: (n, 0)),
        out_shape=jax.ShapeDtypeStruct((t, D_MODEL), BF16),
        compiler_params=_params("parallel"), name=name)(qkv, qkv, qkv, qkv, qkv, qkv, sinks_b)


def _attn_bwd(qkv, sinks_b, dout, *, name):
    t = qkv.shape[0]
    w = ATTN_WINDOW
    nb = t // w
    hd = ATTN_HEAD_DIM
    clamp = lambda n: jnp.minimum(n, nb - 1)
    prev = lambda n: jnp.maximum(clamp(n) - 1, 0)

    def body(qlo_ref, qhi_ref, kc_ref, kp_ref, vc_ref, vp_ref, sink_ref, dolo_ref, dohi_ref,
             dq_ref, dkv_ref, dsink_ref, carry):
        n = pl.program_id(0)

        @pl.when(n == 0)
        def _():
            carry[...] = jnp.zeros_like(carry)
            dsink_ref[...] = jnp.zeros_like(dsink_ref)

        @pl.when(n < nb)
        def _():
            heads, kvs = range(ATTN_N_Q), range(ATTN_N_KV)
            q, kb, vb = _attn_block_views(qlo_ref, qhi_ref, kc_ref, kp_ref, vc_ref, vp_ref)
            do, _, _ = _attn_block_views(dolo_ref, dohi_ref, kc_ref, kp_ref, vc_ref, vp_ref)
            sink = [sink_ref[h:h + 1, 0:1] for h in heads]
            e, es, inv = _attn_softmax(q, kb, sink, _attn_valid(n == 0))
            dp = [_dot_nt(do[h], vb[h // ATTN_REP]) for h in heads]
            p = [e[h] * inv[h] for h in heads]
            delta = [jnp.sum(p[h] * dp[h], axis=1, keepdims=True) for h in heads]
            dsc = [(p[h] * (dp[h] - delta[h]) * (hd ** -0.5)).astype(BF16) for h in heads]
            pb = [p[h].astype(BF16) for h in heads]
            dq = [_dot_nn(dsc[h], kb[h // ATTN_REP]).astype(dq_ref.dtype) for h in heads]
            stack = lambda per_head, kh: jnp.concatenate(per_head[kh * ATTN_REP:(kh + 1) * ATTN_REP], axis=0)
            dkb = [_dot_tn(stack(dsc, kh), stack(q, kh)) for kh in kvs]
            dvb = [_dot_tn(stack(pb, kh), stack(do, kh)) for kh in kvs]
            dsink = [jnp.broadcast_to(jnp.sum(-es[h] * inv[h] * delta[h], axis=0, keepdims=True), (1, LANES)) for h in heads]
            dq_ref[...] = jnp.concatenate(dq, axis=1)
            dsink_ref[...] += jnp.concatenate(dsink, axis=0)
            dkv_ref[...] = (carry[...] + jnp.concatenate([d[0:w, :] for d in dkb + dvb], axis=1)).astype(dkv_ref.dtype)
            carry[...] = jnp.concatenate([d[w:2 * w, :] for d in dkb + dvb], axis=1)

        @pl.when(n == nb)
        def _():
            dkv_ref[...] = carry[...].astype(dkv_ref.dtype)

    qh = lambda half: pl.BlockSpec((w, ATTN_Q_HALF), lambda n: (clamp(n), half))
    kv = lambda blk, idx: pl.BlockSpec((w, ATTN_KV_W), lambda n: (idx(n), blk))
    return pl.pallas_call(
        body, grid=(nb + 1,),
        in_specs=[qh(0), qh(1), kv(ATTN_K_BLK, clamp), kv(ATTN_K_BLK, prev), kv(ATTN_V_BLK, clamp), kv(ATTN_V_BLK, prev),
                  pl.BlockSpec((ATTN_N_Q, LANES), lambda n: (0, 0)), qh(0), qh(1)],
        out_specs=[pl.BlockSpec((w, D_MODEL), lambda n: (clamp(n), 0)),
                   pl.BlockSpec((w, 2 * ATTN_KV_W), lambda n: (jnp.maximum(n - 1, 0), 0)),
                   pl.BlockSpec((ATTN_N_Q, LANES), lambda n: (0, 0))],
        out_shape=[jax.ShapeDtypeStruct((t, D_MODEL), BF16), jax.ShapeDtypeStruct((t, 2 * ATTN_KV_W), BF16),
                   jax.ShapeDtypeStruct((ATTN_N_Q, LANES), F32)],
        scratch_shapes=[pltpu.VMEM((w, 2 * ATTN_KV_W), F32)],
        compiler_params=_params("arbitrary"), name=name)(qkv, qkv, qkv, qkv, qkv, qkv, sinks_b, dout, dout)


def _sq_relu_epilogue(acc):
    r = jnp.maximum(acc, 0.0)
    return acc, r * r


def _sq_relu_bwd_epilogue(acc, pre):
    return (acc * (2.0 * jnp.maximum(pre, 0.0)),)


def _bias_epilogue(acc, bias):
    return (acc + bias,)


def _mlp_fwd(u, w_up, w_down, tag):
    pre, act = _matmul(u, w_up, mode="nn", out_dtypes=(F32, BF16), epilogue=_sq_relu_epilogue, b_shards=True,
                       name=f"mlp_up_{tag}")
    f = _matmul(act, w_down, mode="nn", out_dtypes=(F32,), name=f"mlp_down_{tag}")
    return pre, act, f


def _mlp_bwd(u, pre, act, w_up, w_down, df, tag):
    dpre = _matmul(df, w_down, mode="nt", out_dtypes=(BF16,), epilogue=_sq_relu_bwd_epilogue,
                   extras=((pre, "tile"),), name=f"mlp_dact_{tag}")
    dw_down = _matmul(act, df, mode="tn", out_dtypes=(BF16,), name=f"mlp_dwdown_{tag}")
    du = _matmul(dpre, w_up, mode="nt", out_dtypes=(F32,), b_shards=True, name=f"mlp_du_{tag}")
    dw_up = _matmul(u, dpre, mode="tn", out_dtypes=(BF16,), out_shards=True, name=f"mlp_dwup_{tag}")
    return du, dw_up, dw_down


def _group_dt_layouts(dt):
    t = dt.shape[0]
    d = dt[:, :SSD_N_HEADS].reshape(t, SSD_N_GROUPS, SSD_HPG)
    return jnp.transpose(d, (1, 0, 2)), jnp.transpose(d, (1, 2, 0))


def _head_param_rows(p):
    return jnp.broadcast_to(p.reshape(SSD_N_GROUPS, SSD_HPG, 1), (SSD_N_GROUPS, SSD_HPG, LANES))


def _local_step(x, target, wts, comm=None):
    t = x.shape[0]
    wts = dict(wts)
    row = lambda v: v.reshape(1, -1)
    mix_pre, mix_post, ffn_pre, ffn_post = wts["mix_pre_norm"], wts["mix_post_norm"], wts["ffn_pre_norm"], wts["ffn_post_norm"]

    def gathering(stage, fn, *args, **kwargs):
        if comm is None:
            return fn(*args, **kwargs)
        out, got = fn(*args, hook=comm.gather_hook(stage), **kwargs)
        wts.update(comm.weights_from(stage, got))
        return out

    u0 = _rms_fwd(x, row(mix_pre[0]), name="rms_pre_mix0")
    zx = gathering("in_proj", _matmul, u0, wts["ssd_w_in"], mode="nn", out_dtypes=(F32,), tn=896, name="ssd_in_proj")
    xc = gathering("conv", _conv_fwd, zx, wts["ssd_conv_w"], row(wts["ssd_conv_b"]), name="ssd_conv_fwd")
    bias_row = jnp.pad(wts["ssd_dt_bias"], (0, LANES - SSD_N_HEADS)).reshape(1, LANES)
    dt = _softplus_fwd(zx, bias_row, name="ssd_dt_fwd")
    dtc, dtr = _group_dt_layouts(dt)
    alog_b, d_b = _head_param_rows(wts["ssd_a_log"]), _head_param_rows(wts["ssd_d"])
    y_ssd, states = gathering("scan", _ssd_fwd, xc, dtc, dtr, alog_b, d_b, name="ssd_scan_fwd")
    norm_w = row(wts["ssd_norm_w"])
    yn = _gate_norm_fwd(y_ssd, zx, norm_w, name="ssd_gate_norm_fwd")
    mix0 = _matmul(yn, wts["ssd_w_out"], mode="nn", out_dtypes=(F32,), name="ssd_out_proj")
    h1, v0 = _rms_fwd(mix0, row(mix_post[0]), resid=x, want_u=row(ffn_pre[0]), name="rms_post_mix0")
    pre0, act0, f0 = _mlp_fwd(v0, wts["mlp_w_up0"], wts["mlp_w_down0"], "l0")
    h2, u1 = _rms_fwd(f0, row(ffn_post[0]), resid=h1, want_u=row(mix_pre[1]), name="rms_post_ffn0")

    qkv = _matmul(u1, wts["attn_w_qkv"], mode="nn", out_dtypes=(BF16,), epilogue=_bias_epilogue,
                  extras=((row(wts["attn_b_qkv"]), "row"),), b_shards=True, name="attn_qkv_proj")
    sinks_b = jnp.broadcast_to(wts["attn_sinks"].reshape(ATTN_N_Q, 1), (ATTN_N_Q, LANES))
    ao = _attn_fwd(qkv, sinks_b, name="attn_fwd")
    mix1 = _matmul(ao, wts["attn_w_o"], mode="nn", out_dtypes=(F32,), epilogue=_bias_epilogue,
                   extras=((row(wts["attn_b_o"]), "row"),), name="attn_out_proj")
    h3, v1 = _rms_fwd(mix1, row(mix_post[1]), resid=h2, want_u=row(ffn_pre[1]), name="rms_post_mix1")
    pre1, act1, f1 = _mlp_fwd(v1, wts["mlp_w_up1"], wts["mlp_w_down1"], "l1")
    h4 = _rms_fwd(f1, row(ffn_post[1]), resid=h3, name="rms_post_ffn1")

    dh4, loss_tile = _loss_head(h4, target, name="loss_head")

    df1, g_ffn_post1 = _rms_bwd(f1, row(ffn_post[1]), dh4, out_dtype=BF16, name="rms_post_ffn1_bwd")
    dv1, g_up1, g_down1 = _mlp_bwd(v1, pre1, act1, wts["mlp_w_up1"], wts["mlp_w_down1"], df1, "l1")
    dh3, g_ffn_pre1 = _rms_bwd(h3, row(ffn_pre[1]), dv1, resid=dh4, name="rms_pre_ffn1_bwd")
    dmix1, g_mix_post1 = _rms_bwd(mix1, row(mix_post[1]), dh3, out_dtype=BF16, name="rms_post_mix1_bwd")
    g_b_o = _col_sum(dmix1, name="attn_bo_grad")
    g_w_o = _matmul(ao, dmix1, mode="tn", out_dtypes=(BF16,), name="attn_dwo")
    dao = _matmul(dmix1, wts["attn_w_o"], mode="nt", out_dtypes=(BF16,), name="attn_dao")
    dq, dkv, g_sinks = _attn_bwd(qkv, sinks_b, dao, name="attn_bwd")
    dqkv = jnp.concatenate([dq, dkv], axis=1)
    g_b_qkv = _col_sum(dqkv, name="attn_bqkv_grad")
    g_w_qkv = _matmul(u1, dqkv, mode="tn", out_dtypes=(BF16,), tn=ATTN_QKV // N_CHIPS, out_shards=True, name="attn_dwqkv")
    du1 = _matmul(dqkv, wts["attn_w_qkv"], mode="nt", out_dtypes=(F32,), b_shards=True, name="attn_du")
    dh2, g_mix_pre1 = _rms_bwd(h2, row(mix_pre[1]), du1, resid=dh3, name="rms_pre_mix1_bwd")

    df0, g_ffn_post0 = _rms_bwd(f0, row(ffn_post[0]), dh2, out_dtype=BF16, name="rms_post_ffn0_bwd")
    dv0, g_up0, g_down0 = _mlp_bwd(v0, pre0, act0, wts["mlp_w_up0"], wts["mlp_w_down0"], df0, "l0")
    dh1, g_ffn_pre0 = _rms_bwd(h1, row(ffn_pre[0]), dv0, resid=dh2, name="rms_pre_ffn0_bwd")
    dmix0, g_mix_post0 = _rms_bwd(mix0, row(mix_post[0]), dh1, out_dtype=BF16, name="rms_post_mix0_bwd")
    g_w_out = _matmul(yn, dmix0, mode="tn", out_dtypes=(BF16,), name="ssd_dwout")
    dyn = _matmul(dmix0, wts["ssd_w_out"], mode="nt", out_dtypes=(BF16,), name="ssd_dyn")
    dy_ssd, dz, g_norm_w = _gate_norm_bwd(y_ssd, zx, norm_w, dyn, name="ssd_gate_norm_bwd")
    mats = {"ssd_w_out": g_w_out, "attn_w_qkv": g_w_qkv, "attn_w_o": g_w_o,
            "mlp_w_up0": g_up0, "mlp_w_up1": g_up1, "mlp_w_down0": g_down0, "mlp_w_down1": g_down1}
    if comm is None:
        dxc, dbm, dcm, ddt_r, dpar = _ssd_bwd(xc, dtc, dtr, alog_b, d_b, states, dy_ssd, name="ssd_scan_bwd")
    else:
        (dxc, dbm, dcm, ddt_r, dpar), received = _ssd_bwd(xc, dtc, dtr, alog_b, d_b, states, dy_ssd, name="ssd_scan_bwd",
                                                          hook=comm.exchange_hook(mats))
        comm.received(received)
    dxbc, g_conv_w, g_conv_b = _conv_bwd(zx, wts["ssd_conv_w"], row(wts["ssd_conv_b"]), dxc, dbm, dcm, name="ssd_conv_bwd")
    ddt = jnp.pad(jnp.transpose(ddt_r, (2, 0, 1)).reshape(t, SSD_N_HEADS), ((0, 0), (0, LANES - SSD_N_HEADS)))
    ddt_raw, g_dt_bias = _softplus_bwd(zx, bias_row, ddt, name="ssd_dt_bwd")
    dzx = jnp.concatenate([dz, dxbc, ddt_raw], axis=1)
    g_w_in = _w_in_to_shards(_matmul(u0, dzx, mode="tn", out_dtypes=(F32,), tn=896, name="ssd_dwin"), name="ssd_dwin_shards")
    du0 = _matmul(dzx, wts["ssd_w_in"], mode="nt", out_dtypes=(F32,), tk=896, name="ssd_du")
    grad_x, g_mix_pre0 = _rms_bwd(x, row(mix_pre[0]), du0, resid=dh1, name="rms_pre_mix0_bwd")

    dpar = dpar.reshape(SSD_N_HEADS, LANES)
    mats["ssd_w_in"] = g_w_in
    vecs = {
        "ssd_conv_w": g_conv_w, "ssd_conv_b": g_conv_b.reshape(-1),
        "ssd_dt_bias": g_dt_bias[0, :SSD_N_HEADS], "ssd_a_log": dpar[:, 0], "ssd_d": dpar[:, 1],
        "ssd_norm_w": g_norm_w.reshape(-1), "attn_b_qkv": g_b_qkv.reshape(-1), "attn_sinks": g_sinks[:, 0],
        "attn_b_o": g_b_o.reshape(-1),
        "mix_pre_norm": jnp.concatenate([g_mix_pre0, g_mix_pre1]), "mix_post_norm": jnp.concatenate([g_mix_post0, g_mix_post1]),
        "ffn_pre_norm": jnp.concatenate([g_ffn_pre0, g_ffn_pre1]), "ffn_post_norm": jnp.concatenate([g_ffn_post0, g_ffn_post1]),
    }
    return loss_tile, grad_x, mats, vecs


def _mesh_position():
    return lax.axis_index("x"), lax.axis_index("y"), lax.axis_index("c")


def _flip(v, bit):
    return 1 - v if bit else v


OTHER_CHIPS = ((1, 0), (0, 1), (1, 1))


def _comm_params():
    return pltpu.CompilerParams(vmem_limit_bytes=VMEM_LIMIT)


def _staged_copies(srcs, dsts, bufs, sems_in, sems_out):
    loads = [pltpu.make_async_copy(s, b, sems_in.at[i]) for i, (s, b) in enumerate(zip(srcs, bufs))]
    stores = [pltpu.make_async_copy(b, d, sems_out.at[i]) for i, (b, d) in enumerate(zip(bufs, dsts))]
    return loads, stores


class _GatherHook:
    def __init__(self, mats, vecs=()):
        self.arrs = list(mats) + list(vecs)
        self.nm, self.n = len(mats), len(self.arrs)
        n_ici, n_fwd = (N_CHIPS - 1) * self.n, max((N_CHIPS - 1) * self.nm, 1)
        dma = pltpu.SemaphoreType.DMA
        self.out_shape = [jax.ShapeDtypeStruct((N_CHIPS,) + a.shape, a.dtype) for a in self.arrs]
        self.scratch = [pltpu.VMEM(a.shape, a.dtype) for a in self.arrs] + [
            dma((n_ici,)), dma((n_ici,)), dma((n_fwd,)), dma((n_fwd,)), dma((self.n,)), dma((self.n,))]

    def plan(self, ins, outs, scratch):
        n, nm = self.n, self.nm
        bufs = scratch[:n]
        ici_send, ici_recv, fwd_send, fwd_recv, load_sems, store_sems = scratch[n:]
        xi, yi, ci = _mesh_position()
        me = 2 * xi + yi
        loads, stores = _staged_copies(ins, [outs[i].at[me] for i in range(n)], bufs, load_sems, store_sems)
        sends, landed, forwards, from_sibling = [], [], [], []
        for j, (bx, by) in enumerate(OTHER_CHIPS):
            px, py = _flip(xi, bx), _flip(yi, by)
            peer = 2 * px + py
            for i in range(n):
                k = j * n + i
                mk = functools.partial(pltpu.make_async_remote_copy, send_sem=ici_send.at[k], recv_sem=ici_recv.at[k],
                                       device_id=(px, py, ci), device_id_type=MESH)
                if i < nm:
                    sends.append(mk(src_ref=ins[i].at[ci], dst_ref=outs[i].at[me, ci]))
                    landed.append(mk(src_ref=ins[i].at[ci], dst_ref=outs[i].at[peer, ci]))
                    kf = j * nm + i
                    fw = functools.partial(pltpu.make_async_remote_copy, send_sem=fwd_send.at[kf], recv_sem=fwd_recv.at[kf],
                                           device_id=(xi, yi, 1 - ci), device_id_type=MESH)
                    forwards.append(fw(src_ref=outs[i].at[peer, ci], dst_ref=outs[i].at[peer, ci]))
                    from_sibling.append(fw(src_ref=outs[i].at[peer, ci], dst_ref=outs[i].at[peer, 1 - ci]))
                else:
                    sends.append(mk(src_ref=ins[i], dst_ref=outs[i].at[me]))
                    landed.append(mk(src_ref=ins[i], dst_ref=outs[i].at[peer]))
                    forwards.append(None)
        return loads, stores, sends, landed, forwards, from_sibling

    @staticmethod
    def start(p):
        loads, _, sends, _, _, _ = p
        for cp in loads + sends:
            cp.start()

    @staticmethod
    def relay(p):
        loads, stores, _, landed, forwards, _ = p
        for ld, st in zip(loads, stores):
            ld.wait()
            st.start()
        for cp, fw in zip(landed, forwards):
            cp.wait_recv()
            if fw is not None:
                fw.start()

    @staticmethod
    def finish(p):
        _, stores, sends, _, forwards, from_sibling = p
        for cp in from_sibling:
            cp.wait_recv()
        for cp in sends + [fw for fw in forwards if fw is not None]:
            cp.wait_send()
        for st in stores:
            st.wait()


def _run_hook(hook, ins, outs, scratch, step, n_steps):
    p = hook.plan(ins, outs, scratch)
    relay_step = min(max(1, (3 * n_steps) // 4), n_steps - 1)

    @pl.when(step == 0)
    def _():
        hook.start(p)

    if relay_step < n_steps - 1:
        @pl.when(step == relay_step)
        def _():
            hook.relay(p)

    @pl.when(step == n_steps - 1)
    def _():
        if relay_step == n_steps - 1:
            hook.relay(p)
        hook.finish(p)


def _hook_call(hook, *, name):
    n = len(hook.arrs)

    def body(*refs):
        p = hook.plan(refs[:n], refs[n:n + len(hook.out_shape)], refs[n + len(hook.out_shape):])
        hook.start(p)
        hook.relay(p)
        hook.finish(p)

    return pl.pallas_call(
        body, in_specs=[ANY] * n, out_specs=[ANY] * len(hook.out_shape), out_shape=hook.out_shape,
        scratch_shapes=hook.scratch, compiler_params=_comm_params(), name=name)(*hook.arrs)


def _send_other_half(parts, *, name):
    n = len(parts)

    def body(*refs):
        ins, outs = refs[:n], refs[n:2 * n]
        send_sems, recv_sems = refs[2 * n:]
        xi, yi, ci = _mesh_position()
        sibling = (xi, yi, 1 - ci)
        for i in range(n):
            for s in range(N_CHIPS):
                pltpu.make_async_remote_copy(src_ref=ins[i].at[s, 1 - ci], dst_ref=outs[i].at[s], send_sem=send_sems.at[i],
                                             recv_sem=recv_sems.at[i], device_id=sibling, device_id_type=MESH).start()
        for i in range(n):
            pltpu.make_async_remote_copy(src_ref=outs[i], dst_ref=outs[i], send_sem=send_sems.at[i], recv_sem=recv_sems.at[i],
                                         device_id=sibling, device_id_type=MESH).wait()

    return pl.pallas_call(
        body, in_specs=[ANY] * n, out_specs=[ANY] * n,
        out_shape=[jax.ShapeDtypeStruct((p.shape[0],) + p.shape[2:], p.dtype) for p in parts],
        scratch_shapes=[pltpu.SemaphoreType.DMA((n,)), pltpu.SemaphoreType.DMA((n,))],
        name=name)(*parts)


ROW_BLOCKS = 8


def _add_sibling_half(parts, theirs, core, *, name):
    n = len(parts)

    def body(core_ref, *refs):
        for a_ref, b_ref, o_ref in zip(refs[:n], refs[n:2 * n], refs[2 * n:]):
            o_ref[...] = (a_ref[...].astype(F32) + b_ref[...].astype(F32)).astype(o_ref.dtype)

    mine = lambda p: pl.BlockSpec((None, None, p.shape[2] // ROW_BLOCKS, p.shape[3]), lambda s, rb, core_ref: (s, core_ref[0], rb, 0))
    other = lambda p: pl.BlockSpec((None, p.shape[1] // ROW_BLOCKS, p.shape[2]), lambda s, rb, core_ref: (s, rb, 0))
    return pl.pallas_call(
        body,
        grid_spec=pltpu.PrefetchScalarGridSpec(
            num_scalar_prefetch=1, grid=(N_CHIPS, ROW_BLOCKS),
            in_specs=[mine(p) for p in parts] + [other(q) for q in theirs], out_specs=[other(q) for q in theirs]),
        out_shape=[jax.ShapeDtypeStruct(q.shape, BF16) for q in theirs],
        compiler_params=_params("parallel", "parallel"), name=name)(core, *parts, *theirs)


def _grad_exchange(parts, small, *, name):
    outs = _hook_call(_ExchangeHook(parts, [small]), name=name)
    return outs


class _ExchangeHook:
    def __init__(self, parts, to_all=()):
        self.arrs = list(parts) + list(to_all)
        self.n_parts, self.n = len(parts), len(self.arrs)
        n_ici, n_peer = (N_CHIPS - 1) * self.n_parts, (N_DEV - 1) * max(len(to_all), 1)
        dma = pltpu.SemaphoreType.DMA
        self.out_shape = [jax.ShapeDtypeStruct(p.shape, p.dtype) for p in parts] + [
            jax.ShapeDtypeStruct((N_DEV,) + a.shape, a.dtype) for a in to_all]
        self.scratch = [pltpu.VMEM(p.shape[1:], p.dtype) for p in parts] + [pltpu.VMEM(a.shape, a.dtype) for a in to_all] + [
            dma((n_ici,)), dma((n_ici,)), dma((n_peer,)), dma((n_peer,)), dma((self.n,)), dma((self.n,))]

    def plan(self, ins, outs, scratch):
        n, npt = self.n, self.n_parts
        bufs = scratch[:n]
        send_sems, recv_sems, all_send, all_recv, load_sems, store_sems = scratch[n:]
        xi, yi, ci = _mesh_position()
        me_chip = 2 * xi + yi
        me = 4 * xi + 2 * yi + ci
        loads, stores = _staged_copies([ins[i].at[me_chip] for i in range(npt)] + list(ins[npt:]),
                                       [outs[i].at[me_chip] for i in range(npt)] + [outs[i].at[me] for i in range(npt, n)],
                                       bufs, load_sems, store_sems)
        sends, recvs = [], []
        for j, (bx, by) in enumerate(OTHER_CHIPS):
            px, py = _flip(xi, bx), _flip(yi, by)
            peer = 2 * px + py
            for i in range(npt):
                k = j * npt + i
                mk = functools.partial(pltpu.make_async_remote_copy, src_ref=ins[i].at[peer], send_sem=send_sems.at[k],
                                       recv_sem=recv_sems.at[k], device_id=(px, py, ci), device_id_type=MESH)
                sends.append(mk(dst_ref=outs[i].at[me_chip]))
                recvs.append(mk(dst_ref=outs[i].at[peer]))
        for i in range(npt, n):
            for k in range(1, N_DEV):
                px, py, pc = _flip(xi, (k >> 2) & 1), _flip(yi, (k >> 1) & 1), _flip(ci, k & 1)
                slot = (i - npt) * (N_DEV - 1) + k - 1
                mk = functools.partial(pltpu.make_async_remote_copy, src_ref=ins[i], send_sem=all_send.at[slot],
                                       recv_sem=all_recv.at[slot], device_id=(px, py, pc), device_id_type=MESH)
                sends.append(mk(dst_ref=outs[i].at[me]))
                recvs.append(mk(dst_ref=outs[i].at[4 * px + 2 * py + pc]))
        return loads, stores, sends, recvs

    @staticmethod
    def start(p):
        loads, _, sends, _ = p
        for cp in loads + sends:
            cp.start()

    @staticmethod
    def relay(p):
        loads, stores, _, _ = p
        for ld, st in zip(loads, stores):
            ld.wait()
            st.start()

    @staticmethod
    def finish(p):
        _, stores, sends, recvs = p
        for cp in recvs:
            cp.wait_recv()
        for cp in sends:
            cp.wait_send()
        for st in stores:
            st.wait()


def _sum_chips(parts, *, name):
    n = len(parts)
    p = parts[0].shape[0]

    def body(*refs):
        s = pl.program_id(1)
        for x_ref, o_ref in zip(refs[:n], refs[n:]):
            @pl.when(s == 0)
            def _():
                o_ref[...] = x_ref[...].astype(F32)

            @pl.when(s > 0)
            def _():
                o_ref[...] += x_ref[...].astype(F32)

    blocks = lambda q: ROW_BLOCKS if q.shape[1] % (8 * ROW_BLOCKS) == 0 else 1
    assert len({blocks(q) for q in parts}) == 1
    nb = blocks(parts[0])
    return pl.pallas_call(
        body, grid=(nb, p),
        in_specs=[pl.BlockSpec((None, q.shape[1] // nb, q.shape[2]), lambda rb, s: (s, rb, 0)) for q in parts],
        out_specs=[pl.BlockSpec((q.shape[1] // nb, q.shape[2]), lambda rb, s: (rb, 0)) for q in parts],
        out_shape=[jax.ShapeDtypeStruct(q.shape[1:], F32) for q in parts],
        compiler_params=_params("parallel", "arbitrary"), name=name)(*parts)


def _swap_halves(halves, layers, *, name):
    n = len(halves)
    out_shapes, slots = [], []
    for i, h in enumerate(halves):
        pair = [p for p in layers if i in p]
        if pair and pair[0][1] == i:
            slots.append((slots[pair[0][0]][0], 1))
        elif pair:
            out_shapes.append(jax.ShapeDtypeStruct((2, 2) + h.shape, h.dtype))
            slots.append((len(out_shapes) - 1, 0))
        else:
            out_shapes.append(jax.ShapeDtypeStruct((2,) + h.shape, h.dtype))
            slots.append((len(out_shapes) - 1, None))
    n_out = len(out_shapes)

    def body(*refs):
        ins, outs, bufs = refs[:n], refs[n:n + n_out], refs[n + n_out:2 * n + n_out]
        send_sems, recv_sems, load_sems, store_sems = refs[2 * n + n_out:]
        xi, yi, ci = _mesh_position()
        own, sends, recvs = [], [], []
        for i in range(n):
            o, layer = slots[i]
            dst = (lambda core: outs[o].at[core]) if layer is None else (lambda core: outs[o].at[layer, core])
            own.append(dst(ci))
            mk = functools.partial(pltpu.make_async_remote_copy, src_ref=ins[i], send_sem=send_sems.at[i],
                                   recv_sem=recv_sems.at[i], device_id=(xi, yi, 1 - ci), device_id_type=MESH)
            sends.append(mk(dst_ref=dst(ci)))
            recvs.append(mk(dst_ref=dst(1 - ci)))
        loads, stores = _staged_copies(ins, own, bufs, load_sems, store_sems)
        for cp in loads + sends:
            cp.start()
        for ld, st in zip(loads, stores):
            ld.wait()
            st.start()
        for cp in recvs:
            cp.wait_recv()
        for cp in sends:
            cp.wait_send()
        for st in stores:
            st.wait()

    return pl.pallas_call(
        body, in_specs=[ANY] * n, out_specs=[ANY] * n_out, out_shape=out_shapes,
        scratch_shapes=[pltpu.VMEM(h.shape, h.dtype) for h in halves]
        + [pltpu.SemaphoreType.DMA((n,)), pltpu.SemaphoreType.DMA((n,)), pltpu.SemaphoreType.DMA((n,)), pltpu.SemaphoreType.DMA((n,))],
        compiler_params=_comm_params(), name=name)(*halves)


def _full_weight(name, gathered):
    s, _, r, c = gathered.shape
    if name == "ssd_w_in":
        return _w_in_from_shards(gathered.reshape(s, 2 * r, c), name="ssd_w_in_unshard")
    if name in ("attn_w_qkv", "mlp_w_up0", "mlp_w_up1"):
        return gathered.reshape(s, 2 * r, c)
    return gathered.reshape(s * 2 * r, c)


class _StepComm:
    GATHER = {"in_proj": ("mlp_w_up0", "attn_w_qkv"), "conv": ("mlp_w_down0", "attn_w_o"),
              "scan": ("ssd_w_out", "mlp_w_up1", "mlp_w_down1")}
    EARLY = ("ssd_w_out", "attn_w_qkv", "attn_w_o", "mlp_w_up0", "mlp_w_up1", "mlp_w_down0", "mlp_w_down1")

    def __init__(self, shards, core):
        self.shards, self.core = shards, core
        self.early_received = None

    def gather_hook(self, stage):
        return _GatherHook([self.shards[n] for n in self.GATHER[stage]])

    def weights_from(self, stage, gathered):
        return {n: _full_weight(n, g) for n, g in zip(self.GATHER[stage], gathered)}

    def chip_sums(self, mats, tag):
        parts = [_shard_halves(a) for a in mats.values()]
        theirs = _send_other_half(parts, name=f"grad_sibling_send_{tag}")
        return _add_sibling_half(parts, theirs, self.core, name=f"grad_chip_sum_{tag}")

    def exchange_hook(self, mats):
        return _ExchangeHook(self.chip_sums({n: mats[n] for n in self.EARLY}, "early"))

    def received(self, arrays):
        self.early_received = list(arrays)


def _adamw(w, g, m, v, *, name):
    r, c = w.shape
    tr = 256 if r % 256 == 0 else r
    blk = pl.BlockSpec((tr, c), lambda i: (i, 0))

    def body(w_ref, g_ref, m_ref, v_ref, d_ref, nm_ref, nv_ref):
        gv = g_ref[...]
        nm = ADAM_B1 * m_ref[...] + (1.0 - ADAM_B1) * gv
        nv = ADAM_B2 * v_ref[...] + (1.0 - ADAM_B2) * (gv * gv)
        m_hat = nm / (1.0 - ADAM_B1 ** ADAM_STEP)
        v_hat = nv / (1.0 - ADAM_B2 ** ADAM_STEP)
        d_ref[...] = -ADAM_LR * (m_hat / (jnp.sqrt(v_hat) + ADAM_EPS) + ADAM_WD * w_ref[...])
        nm_ref[...] = nm
        nv_ref[...] = nv

    sh = jax.ShapeDtypeStruct((r, c), F32)
    return pl.pallas_call(body, grid=(r // tr,), in_specs=[blk] * 4, out_specs=[blk] * 3, out_shape=[sh] * 3,
                          compiler_params=_params("parallel"), name=name)(w, g, m, v)


SM_CONV_B, SM_NORM_W, SM_MIX_PRE, SM_MIX_POST, SM_FFN_PRE, SM_FFN_POST, SM_MISC, SM_CONV_W, SM_B_QKV, SM_B_O = 0, 4, 6, 8, 10, 12, 14, 16, 32, 34
SM_ROWS = 40
MISC_DT_BIAS, MISC_A_LOG, MISC_D, MISC_SINKS, MISC_LOSS = 0, 32, 64, 96, 112


def _shard_halves(a):
    c = a.shape[-1]
    return a.reshape(N_CHIPS, 2, -1, c)


def _rows(v):
    return v.reshape(-1, D_MODEL)


def _misc_row(dt_bias, a_log, d, sinks, loss):
    pad = jnp.zeros((D_MODEL - MISC_LOSS - 1,), F32)
    return jnp.concatenate([dt_bias.reshape(-1), a_log.reshape(-1), d.reshape(-1), sinks.reshape(-1), loss.reshape(1), pad]).reshape(1, D_MODEL)


def _replicated_rows(p, loss):
    return jnp.concatenate([
        _rows(p["ssd_conv_b"]), _rows(p["ssd_norm_w"]), _rows(p["mix_pre_norm"]), _rows(p["mix_post_norm"]),
        _rows(p["ffn_pre_norm"]), _rows(p["ffn_post_norm"]),
        _misc_row(p["ssd_dt_bias"], p["ssd_a_log"], p["ssd_d"], p["attn_sinks"], loss), jnp.zeros((1, D_MODEL), F32)], axis=0)


def _sharded_rows(conv_w, b_qkv, b_o):
    last = jnp.concatenate([b_qkv.reshape(-1), b_o.reshape(-1), jnp.zeros((D_MODEL - 640,), F32)]).reshape(1, D_MODEL)
    return jnp.concatenate([conv_w.reshape(SSD_CONV_WIDTH, D_MODEL), last, jnp.zeros((3, D_MODEL), F32)], axis=0)


REPLICATED = ("ssd_conv_b", "ssd_dt_bias", "ssd_a_log", "ssd_d", "ssd_norm_w", "attn_sinks",
              "mix_pre_norm", "mix_post_norm", "ffn_pre_norm", "ffn_post_norm")
MATRICES = ("ssd_w_in", "ssd_w_out", "attn_w_qkv", "attn_w_o", "mlp_w_up", "mlp_w_down")
WEIGHT_NAMES = ("ssd_w_in", "ssd_conv_w", "ssd_conv_b", "ssd_dt_bias", "ssd_a_log", "ssd_d", "ssd_norm_w", "ssd_w_out",
                "attn_w_qkv", "attn_b_qkv", "attn_sinks", "attn_w_o", "attn_b_o", "mlp_w_up", "mlp_w_down",
                "mix_pre_norm", "mix_post_norm", "ffn_pre_norm", "ffn_post_norm")


def _unpack_small(rows16, rows8, like):
    misc = rows16[SM_MISC]
    out = {
        "ssd_conv_b": rows16[SM_CONV_B:SM_CONV_B + 4], "ssd_norm_w": rows16[SM_NORM_W:SM_NORM_W + 2],
        "mix_pre_norm": rows16[SM_MIX_PRE:SM_MIX_PRE + 2], "mix_post_norm": rows16[SM_MIX_POST:SM_MIX_POST + 2],
        "ffn_pre_norm": rows16[SM_FFN_PRE:SM_FFN_PRE + 2], "ffn_post_norm": rows16[SM_FFN_POST:SM_FFN_POST + 2],
        "ssd_dt_bias": misc[MISC_DT_BIAS:MISC_DT_BIAS + 32], "ssd_a_log": misc[MISC_A_LOG:MISC_A_LOG + 32],
        "ssd_d": misc[MISC_D:MISC_D + 32], "attn_sinks": misc[MISC_SINKS:MISC_SINKS + 16],
        "ssd_conv_w": rows8[0:SSD_CONV_WIDTH], "attn_b_qkv": rows8[SSD_CONV_WIDTH, 0:384], "attn_b_o": rows8[SSD_CONV_WIDTH, 384:640],
    }
    return {k: v.reshape(like[k].shape) for k, v in out.items()}


def kernel(x, ssd_w_in, ssd_conv_w, ssd_conv_b, ssd_dt_bias, ssd_a_log, ssd_d, ssd_norm_w, ssd_w_out, attn_w_qkv, attn_b_qkv, attn_sinks, attn_w_o, attn_b_o, mlp_w_up, mlp_w_down, mix_pre_norm, mix_post_norm, ffn_pre_norm, ffn_post_norm, loss_target, m_ssd_w_in, m_ssd_conv_w, m_ssd_conv_b, m_ssd_dt_bias, m_ssd_a_log, m_ssd_d, m_ssd_norm_w, m_ssd_w_out, m_attn_w_qkv, m_attn_b_qkv, m_attn_sinks, m_attn_w_o, m_attn_b_o, m_mlp_w_up, m_mlp_w_down, m_mix_pre_norm, m_mix_post_norm, m_ffn_pre_norm, m_ffn_post_norm, v_ssd_w_in, v_ssd_conv_w, v_ssd_conv_b, v_ssd_dt_bias, v_ssd_a_log, v_ssd_d, v_ssd_norm_w, v_ssd_w_out, v_attn_w_qkv, v_attn_b_qkv, v_attn_sinks, v_attn_w_o, v_attn_b_o, v_mlp_w_up, v_mlp_w_down, v_mix_pre_norm, v_mix_post_norm, v_ffn_pre_norm, v_ffn_post_norm):
    w = dict(zip(WEIGHT_NAMES, (ssd_w_in, ssd_conv_w, ssd_conv_b, ssd_dt_bias, ssd_a_log, ssd_d, ssd_norm_w, ssd_w_out, attn_w_qkv, attn_b_qkv, attn_sinks, attn_w_o, attn_b_o, mlp_w_up, mlp_w_down, mix_pre_norm, mix_post_norm, ffn_pre_norm, ffn_post_norm)))
    m = dict(zip(WEIGHT_NAMES, (m_ssd_w_in, m_ssd_conv_w, m_ssd_conv_b, m_ssd_dt_bias, m_ssd_a_log, m_ssd_d, m_ssd_norm_w, m_ssd_w_out, m_attn_w_qkv, m_attn_b_qkv, m_attn_sinks, m_attn_w_o, m_attn_b_o, m_mlp_w_up, m_mlp_w_down, m_mix_pre_norm, m_mix_post_norm, m_ffn_pre_norm, m_ffn_post_norm)))
    v = dict(zip(WEIGHT_NAMES, (v_ssd_w_in, v_ssd_conv_w, v_ssd_conv_b, v_ssd_dt_bias, v_ssd_a_log, v_ssd_d, v_ssd_norm_w, v_ssd_w_out, v_attn_w_qkv, v_attn_b_qkv, v_attn_sinks, v_attn_w_o, v_attn_b_o, v_mlp_w_up, v_mlp_w_down, v_mix_pre_norm, v_mix_post_norm, v_ffn_pre_norm, v_ffn_post_norm)))
    chip = 2 * lax.axis_index("x") + lax.axis_index("y")

    two_halves = lambda a: a.astype(BF16).reshape(2, a.shape[0] // 2, a.shape[1])
    shards = {"ssd_w_out": w["ssd_w_out"][0], "attn_w_qkv": w["attn_w_qkv"][0], "attn_w_o": w["attn_w_o"][0],
              "mlp_w_up0": w["mlp_w_up"][0], "mlp_w_up1": w["mlp_w_up"][1],
              "mlp_w_down0": w["mlp_w_down"][0], "mlp_w_down1": w["mlp_w_down"][1]}
    core = lax.axis_index("c").astype(jnp.int32).reshape(1)
    comm = _StepComm({k: two_halves(a) for k, a in shards.items()}, core)
    g_in, g_conv, g_bqkv, g_bo = _hook_call(
        _GatherHook([two_halves(w["ssd_w_in"][0])], [w["ssd_conv_w"][0], w["attn_b_qkv"], w["attn_b_o"]]), name="weight_all_gather")
    full = {
        "ssd_w_in": _full_weight("ssd_w_in", g_in),
        "ssd_conv_w": g_conv.transpose(1, 0, 2).reshape(SSD_CONV_WIDTH, SSD_CONV_DIM),
        "attn_b_qkv": g_bqkv.reshape(ATTN_QKV), "attn_b_o": g_bo.reshape(D_MODEL),
    }
    for name in REPLICATED:
        full[name] = w[name][0] if name.startswith(("ssd_", "attn_")) else w[name]

    loss_tile, grad_x, gm, g = _local_step(x[0], loss_target[0], full, comm)

    chip_sums = comm.chip_sums({"ssd_w_in": gm["ssd_w_in"]}, "late")
    conv_w_rows = g["ssd_conv_w"].reshape(SSD_CONV_WIDTH * N_CHIPS, D_MODEL)
    b_qkv_rows = jnp.pad(g["attn_b_qkv"], (0, 2 * D_MODEL - ATTN_QKV)).reshape(2, D_MODEL)
    small = jnp.concatenate([_replicated_rows(g, loss_tile[0, 0]), conv_w_rows, b_qkv_rows, _rows(g["attn_b_o"]),
                             jnp.zeros((SM_ROWS - SM_B_O - 1, D_MODEL), F32)], axis=0)
    recv_in, small_all = _grad_exchange(chip_sums, small, name="grad_exchange")
    early = dict(zip(_StepComm.EARLY, comm.early_received))
    order = ("ssd_w_out", "attn_w_qkv", "attn_w_o", "mlp_w_up0", "mlp_w_up1", "mlp_w_down0", "mlp_w_down1")
    halves = _sum_chips([recv_in] + [early[k] for k in order], name="grad_sum")
    r_in, r_out, r_qkv, r_o, r_up, r_down = _swap_halves(halves, layers=((4, 5), (6, 7)), name="grad_halves_swap")
    small_sum, = _sum_chips([small_all], name="small_grad_sum")

    grads = {"ssd_w_in": r_in, "ssd_w_out": r_out, "attn_w_qkv": r_qkv, "attn_w_o": r_o, "mlp_w_up": r_up, "mlp_w_down": r_down}
    grads = {k: a.reshape(w[k].shape) for k, a in grads.items()}
    conv_w_g = lax.dynamic_index_in_dim(small_sum[SM_CONV_W:SM_CONV_W + 16].reshape(SSD_CONV_WIDTH, N_CHIPS, D_MODEL), chip, axis=1, keepdims=False)
    b_qkv_g = lax.dynamic_slice_in_dim(small_sum[SM_B_QKV:SM_B_QKV + 2].reshape(-1), chip * 384, 384)
    b_o_g = lax.dynamic_slice_in_dim(small_sum[SM_B_O], chip * 256, 256)
    small_g = jnp.concatenate([small_sum[0:16], _sharded_rows(conv_w_g, b_qkv_g, b_o_g)], axis=0)
    grads.update(_unpack_small(small_g[0:16], small_g[16:24], w))
    loss = small_sum[SM_MISC, MISC_LOSS]

    delta, new_m, new_v = {}, {}, {}
    for name in MATRICES:
        shape = w[name].shape
        as2d = lambda a: a.reshape(-1, shape[-1])
        d2, m2, v2 = _adamw(as2d(w[name]), as2d(grads[name]), as2d(m[name]), as2d(v[name]), name=f"adamw_{name}")
        delta[name], new_m[name], new_v[name] = d2.reshape(shape), m2.reshape(shape), v2.reshape(shape)
    zero = jnp.zeros((), F32)
    small_pack = lambda p: jnp.concatenate([_replicated_rows({k: p[k] for k in REPLICATED}, zero),
                                            _sharded_rows(p["ssd_conv_w"], p["attn_b_qkv"], p["attn_b_o"])], axis=0)
    d_s, m_s, v_s = _adamw(small_pack(w), small_g, small_pack(m), small_pack(v), name="adamw_vectors")
    delta.update(_unpack_small(d_s[0:16], d_s[16:24], w))
    new_m.update(_unpack_small(m_s[0:16], m_s[16:24], w))
    new_v.update(_unpack_small(v_s[0:16], v_s[16:24], w))

    return (loss, grad_x[None], *[grads[n] for n in WEIGHT_NAMES], *[delta[n] for n in WEIGHT_NAMES],
            *[new_m[n] for n in WEIGHT_NAMES], *[new_v[n] for n in WEIGHT_NAMES])
```

```python
import functools
import math

import jax
import jax.numpy as jnp
from jax import lax
from jax.experimental import pallas as pl
from jax.experimental.pallas import tpu as pltpu

F32 = jnp.float32
BF16 = jnp.bfloat16

D_MODEL = 1024
SSD_D_INNER = 2048
SSD_HEAD_DIM = 64
SSD_N_HEADS = 32
SSD_N_GROUPS = 8
SSD_HPG = 4
SSD_D_STATE = 128
SSD_CONV_WIDTH = 4
SSD_CHUNK = 128
SSD_CONV_DIM = 4096
SSD_IN_DIM = 6176
SSD_IN_PAD = 6272
SSD_GW = SSD_HPG * SSD_HEAD_DIM
ATTN_HEAD_DIM = 64
ATTN_N_Q = 16
ATTN_N_KV = 4
ATTN_REP = 4
ATTN_WINDOW = 128
ATTN_QKV = 1536
D_FF = 4096
NORM_EPS = 1e-6

ADAM_LR = 0.001
ADAM_B1 = 0.9
ADAM_B2 = 0.999
ADAM_EPS = 1e-08
ADAM_WD = 0.01
ADAM_STEP = 10

N_CHIPS = 4
N_DEV = 8
LANES = 128
VMEM_LIMIT = 48 * 1024 * 1024

MESH = pl.DeviceIdType.MESH


def _params(*sem):
    return pltpu.CompilerParams(dimension_semantics=sem, vmem_limit_bytes=VMEM_LIMIT)


def _dot(a, b, dims):
    return lax.dot_general(a, b, (dims, ((), ())), preferred_element_type=F32)


def _dot_nn(a, b):
    return _dot(a, b, ((1,), (0,)))


def _dot_nt(a, b):
    return _dot(a, b, ((1,), (1,)))


def _dot_tn(a, b):
    return _dot(a, b, ((0,), (0,)))


def _sigmoid(x):
    return 1.0 / (1.0 + jnp.exp(-x))


ANY = pl.BlockSpec(memory_space=pl.ANY)


class _HookSlots:
    def __init__(self, hook, n_in, n_out, n_scratch):
        self.hook = hook
        self.n_in, self.n_out, self.n_scratch = n_in, n_out, n_scratch
        self.inputs = list(hook.arrs) if hook else []
        self.out_shape = list(hook.out_shape) if hook else []
        self.scratch = list(hook.scratch) if hook else []
        self.in_specs = [ANY] * len(self.inputs)
        self.out_specs = [ANY] * len(self.out_shape)

    def _split(self, refs):
        a = self.n_in
        b = a + len(self.inputs)
        c = b + self.n_out
        d = c + len(self.out_shape)
        e = d + self.n_scratch
        return refs[:a], refs[a:b], refs[b:c], refs[c:d], refs[d:e], refs[e:]

    def own(self, refs):
        ins, _, outs, _, scratch, _ = self._split(refs)
        return ins, outs, scratch

    def run(self, refs, step, n_steps):
        _, h_in, _, h_out, _, h_scratch = self._split(refs)
        _run_hook(self.hook, h_in, h_out, h_scratch, step, n_steps)

    def semantics(self, *sem):
        return sem if self.hook is None else ("arbitrary",) * len(sem)


def _matmul(a, b, *, mode, out_dtypes, name, epilogue=None, extras=(), tm=1024, tn=1024, tk=1024,
            b_shards=False, out_shards=False, hook=None):
    if b_shards:
        s, b_rows, b_cols = b.shape
        b2 = (b_rows, s * b_cols)
        if mode == "nn":
            tn = b_cols
        else:
            assert mode == "nt"
            tk = b_cols
    else:
        b2 = b.shape
    if mode == "nn":
        (m, k), (k2, n) = a.shape, b2
    elif mode == "nt":
        (m, k), (n, k2) = a.shape, b2
    else:
        (k, m), (k2, n) = a.shape, b2
    assert k == k2, (a.shape, b.shape, mode)
    tm, tn, tk = min(tm, m), min(tn, n), min(tk, k)
    assert m % tm == 0 and n % tn == 0 and k % tk == 0, (m, n, k, tm, tn, tk)
    nk = k // tk
    if mode == "tn":
        a_spec = pl.BlockSpec((tk, tm), lambda i, j, kk: (kk, i))
    else:
        a_spec = pl.BlockSpec((tm, tk), lambda i, j, kk: (i, kk))
    if b_shards and mode == "nn":
        b_spec = pl.BlockSpec((None, tk, tn), lambda i, j, kk: (j, kk, 0))
    elif b_shards:
        b_spec = pl.BlockSpec((None, tn, tk), lambda i, j, kk: (kk, j, 0))
    elif mode == "nt":
        b_spec = pl.BlockSpec((tn, tk), lambda i, j, kk: (j, kk))
    else:
        b_spec = pl.BlockSpec((tk, tn), lambda i, j, kk: (kk, j))
    dims = {"nn": ((1,), (0,)), "nt": ((1,), (1,)), "tn": ((0,), (0,))}[mode]
    ex_specs = []
    for arr, kind in extras:
        if kind == "tile":
            ex_specs.append(pl.BlockSpec((tm, tn), lambda i, j, kk: (i, j)))
        else:
            ex_specs.append(pl.BlockSpec((1, tn), lambda i, j, kk: (0, j)))
    n_ex, n_out = len(extras), len(out_dtypes)
    if epilogue is None:
        epilogue = lambda acc: (acc,)
    hk = _HookSlots(hook, n_in=2 + n_ex, n_out=n_out, n_scratch=0 if nk == 1 else 1)
    grid = (m // tm, n // tn, nk)

    def body(*refs):
        (a_ref, b_ref, *ex), outs, scratch = hk.own(refs)
        if hook is not None:
            step = (pl.program_id(0) * grid[1] + pl.program_id(1)) * grid[2] + pl.program_id(2)
            hk.run(refs, step, grid[0] * grid[1] * grid[2])

        def finish(acc):
            res = epilogue(acc, *[e[...] for e in ex])
            for o, r in zip(outs, res):
                o[...] = r.astype(o.dtype)

        if nk == 1:
            finish(_dot(a_ref[...], b_ref[...], dims))
        else:
            acc_ref = scratch[0]
            kk = pl.program_id(2)

            @pl.when(kk == 0)
            def _():
                acc_ref[...] = jnp.zeros_like(acc_ref)

            acc_ref[...] += _dot(a_ref[...], b_ref[...], dims)

            @pl.when(kk == nk - 1)
            def _():
                finish(acc_ref[...])

    if out_shards:
        out_spec = pl.BlockSpec((None, tm, tn), lambda i, j, kk: (j, i, 0))
        out_dims = (n // tn, m, tn)
    else:
        out_spec = pl.BlockSpec((tm, tn), lambda i, j, kk: (i, j))
        out_dims = (m, n)
    outs = pl.pallas_call(
        body,
        grid=grid,
        in_specs=[a_spec, b_spec] + ex_specs + hk.in_specs,
        out_specs=[out_spec for _ in out_dtypes] + hk.out_specs,
        out_shape=[jax.ShapeDtypeStruct(out_dims, dt) for dt in out_dtypes] + hk.out_shape,
        scratch_shapes=([] if nk == 1 else [pltpu.VMEM((tm, tn), F32)]) + hk.scratch,
        compiler_params=_params(*hk.semantics("parallel", "parallel", "arbitrary")),
        name=name,
    )(a, b, *[arr for arr, _ in extras], *hk.inputs)
    own = outs[0] if n_out == 1 else outs[:n_out]
    return own if hook is None else (own, outs[n_out:])


def _row_tile(t, want):
    return min(t, want)


def _rms_fwd(x, w, *, name, resid=None, want_u=None):
    t, d = x.shape
    tr = _row_tile(t, 512)

    def norm(v, wv):
        return v * lax.rsqrt(jnp.mean(v * v, axis=-1, keepdims=True) + NORM_EPS) * wv

    row = pl.BlockSpec((tr, d), lambda i: (i, 0))
    vec = pl.BlockSpec((1, d), lambda i: (0, 0))
    if resid is None:
        def body(x_ref, w_ref, o_ref):
            o_ref[...] = norm(x_ref[...], w_ref[...]).astype(BF16)
        ins, in_specs = (x, w), [row, vec]
        out_shape, out_specs = jax.ShapeDtypeStruct((t, d), BF16), row
    elif want_u is None:
        def body(x_ref, w_ref, r_ref, o_ref):
            o_ref[...] = r_ref[...] + norm(x_ref[...], w_ref[...])
        ins, in_specs = (x, w, resid), [row, vec, row]
        out_shape, out_specs = jax.ShapeDtypeStruct((t, d), F32), row
    else:
        def body(x_ref, w_ref, r_ref, w2_ref, o_ref, u_ref):
            h = r_ref[...] + norm(x_ref[...], w_ref[...])
            o_ref[...] = h
            u_ref[...] = norm(h, w2_ref[...]).astype(BF16)
        ins, in_specs = (x, w, resid, want_u), [row, vec, row, vec]
        out_shape = [jax.ShapeDtypeStruct((t, d), F32), jax.ShapeDtypeStruct((t, d), BF16)]
        out_specs = [row, row]
    return pl.pallas_call(body, grid=(t // tr,), in_specs=in_specs, out_specs=out_specs, out_shape=out_shape,
                          compiler_params=_params("parallel"), name=name)(*ins)


def _rms_bwd(x, w, dy, *, name, resid=None, out_dtype=F32):
    t, d = x.shape
    tr = _row_tile(t, 512)
    row = pl.BlockSpec((tr, d), lambda i: (i, 0))
    vec = pl.BlockSpec((1, d), lambda i: (0, 0))
    has_res = resid is not None

    def body(x_ref, w_ref, dy_ref, *rest):
        if has_res:
            r_ref, dx_ref, dw_ref = rest
        else:
            dx_ref, dw_ref = rest
        xv = x_ref[...]
        dyv = dy_ref[...].astype(F32)
        r = lax.rsqrt(jnp.mean(xv * xv, axis=-1, keepdims=True) + NORM_EPS)
        xhat = xv * r
        dyw = dyv * w_ref[...]
        dx = r * (dyw - xhat * jnp.mean(dyw * xhat, axis=-1, keepdims=True))
        if has_res:
            dx = dx + r_ref[...]
        dx_ref[...] = dx.astype(dx_ref.dtype)

        @pl.when(pl.program_id(0) == 0)
        def _():
            dw_ref[...] = jnp.zeros_like(dw_ref)

        dw_ref[...] += jnp.sum(dyv * xhat, axis=0, keepdims=True)

    ins = (x, w, dy) + ((resid,) if has_res else ())
    in_specs = [row, vec, row] + ([row] if has_res else [])
    return pl.pallas_call(
        body, grid=(t // tr,), in_specs=in_specs, out_specs=[row, vec],
        out_shape=[jax.ShapeDtypeStruct((t, d), out_dtype), jax.ShapeDtypeStruct((1, d), F32)],
        compiler_params=_params("arbitrary"), name=name)(*ins)


def _loss_head(h, target, *, name):
    t, d = h.shape
    tr = _row_tile(t, 512)
    row = pl.BlockSpec((tr, d), lambda i: (i, 0))

    def body(h_ref, t_ref, dh_ref, loss_ref):
        err = h_ref[...] - t_ref[...]
        dh_ref[...] = err * (1.0 / d)

        @pl.when(pl.program_id(0) == 0)
        def _():
            loss_ref[...] = jnp.zeros_like(loss_ref)

        part = jnp.sum(jnp.sum(err * err, axis=1, keepdims=True), axis=0, keepdims=True) * (0.5 / d)
        loss_ref[...] += jnp.broadcast_to(part, loss_ref.shape)

    return pl.pallas_call(
        body, grid=(t // tr,), in_specs=[row, row],
        out_specs=[row, pl.BlockSpec((8, LANES), lambda i: (0, 0))],
        out_shape=[jax.ShapeDtypeStruct((t, d), F32), jax.ShapeDtypeStruct((8, LANES), F32)],
        compiler_params=_params("arbitrary"), name=name)(h, target)


def _col_sum(x, *, name):
    t, n = x.shape
    tr = _row_tile(t, 512)

    def body(x_ref, o_ref):
        @pl.when(pl.program_id(0) == 0)
        def _():
            o_ref[...] = jnp.zeros_like(o_ref)

        o_ref[...] += jnp.sum(x_ref[...].astype(F32), axis=0, keepdims=True)

    return pl.pallas_call(
        body, grid=(t // tr,), in_specs=[pl.BlockSpec((tr, n), lambda i: (i, 0))],
        out_specs=pl.BlockSpec((1, n), lambda i: (0, 0)), out_shape=jax.ShapeDtypeStruct((1, n), F32),
        compiler_params=_params("arbitrary"), name=name)(x)


SSD_IN_SHARD = SSD_IN_DIM // N_CHIPS


def _w_in_from_shards(shards, *, name):
    d = shards.shape[1]
    tr = 256

    def body(s_ref, o_ref):
        o_ref[:, pl.ds(SSD_IN_PAD - LANES, LANES)] = jnp.zeros((tr, LANES), o_ref.dtype)
        for s in range(N_CHIPS):
            o_ref[:, pl.ds(SSD_IN_SHARD * s, SSD_IN_SHARD)] = s_ref[s]

    return pl.pallas_call(
        body, grid=(d // tr,), in_specs=[pl.BlockSpec((N_CHIPS, tr, SSD_IN_SHARD), lambda i: (0, i, 0))],
        out_specs=pl.BlockSpec((tr, SSD_IN_PAD), lambda i: (i, 0)),
        out_shape=jax.ShapeDtypeStruct((d, SSD_IN_PAD), shards.dtype),
        compiler_params=_params("parallel"), name=name)(shards)


def _w_in_to_shards(g, *, name):
    d = g.shape[0]
    tr = 256

    def body(g_ref, o_ref):
        for s in range(N_CHIPS):
            o_ref[s] = g_ref[:, pl.ds(SSD_IN_SHARD * s, SSD_IN_SHARD)].astype(o_ref.dtype)

    return pl.pallas_call(
        body, grid=(d // tr,), in_specs=[pl.BlockSpec((tr, SSD_IN_PAD), lambda i: (i, 0))],
        out_specs=pl.BlockSpec((N_CHIPS, tr, SSD_IN_SHARD), lambda i: (0, i, 0)),
        out_shape=jax.ShapeDtypeStruct((N_CHIPS, d, SSD_IN_SHARD), BF16),
        compiler_params=_params("parallel"), name=name)(g)


XBC_COL0 = SSD_D_INNER // LANES
DT_COL0 = (SSD_D_INNER + SSD_CONV_DIM) // LANES


def _shift_down(v, k, row_ids):
    return jnp.where(row_ids >= k, pltpu.roll(v, k, axis=0), 0.0)


def _shift_up(v, k, row_ids):
    n = v.shape[0]
    return jnp.where(row_ids < n - k, pltpu.roll(v, n - k, axis=0), 0.0)


def _conv_pre(x, w, b, row_ids):
    pre = b + w[3:4, :] * x
    for k in (1, 2, 3):
        pre = pre + w[3 - k:4 - k, :] * _shift_down(x, k, row_ids)
    return pre


def _conv_fwd(zx, conv_w, conv_b, *, name, hook=None):
    t = zx.shape[0]
    nct = SSD_CONV_DIM // LANES
    hk = _HookSlots(hook, n_in=3, n_out=1, n_scratch=0)

    def body(*refs):
        (x_ref, w_ref, b_ref), (o_ref,), _ = hk.own(refs)
        if hook is not None:
            hk.run(refs, pl.program_id(0), nct)
        x = x_ref[...]
        row_ids = lax.broadcasted_iota(jnp.int32, x.shape, 0)
        pre = _conv_pre(x, w_ref[...], b_ref[...], row_ids)
        o_ref[...] = pre * _sigmoid(pre)

    outs = pl.pallas_call(
        body, grid=(nct,),
        in_specs=[pl.BlockSpec((t, LANES), lambda j: (0, XBC_COL0 + j)),
                  pl.BlockSpec((SSD_CONV_WIDTH, LANES), lambda j: (0, j)),
                  pl.BlockSpec((1, LANES), lambda j: (0, j))] + hk.in_specs,
        out_specs=[pl.BlockSpec((t, LANES), lambda j: (0, j))] + hk.out_specs,
        out_shape=[jax.ShapeDtypeStruct((t, SSD_CONV_DIM), F32)] + hk.out_shape,
        scratch_shapes=hk.scratch,
        compiler_params=_params(*hk.semantics("parallel")), name=name)(zx, conv_w, conv_b, *hk.inputs)
    return outs[0] if hook is None else (outs[0], outs[1:])


def _conv_bwd(zx, conv_w, conv_b, d_xs, d_bm, d_cm, *, name):
    t = zx.shape[0]
    nct = SSD_CONV_DIM // LANES
    n_xs = SSD_D_INNER // LANES
    n_bm = SSD_N_GROUPS * SSD_D_STATE // LANES

    def body(x_ref, w_ref, b_ref, dxs_ref, dbm_ref, dcm_ref, dx_ref, dw_ref, db_ref):
        x = x_ref[...]
        w = w_ref[...]
        j = pl.program_id(0)
        dy = jnp.where(j < n_xs, dxs_ref[...], jnp.where(j < n_xs + n_bm, dbm_ref[...], dcm_ref[...]))
        row_ids = lax.broadcasted_iota(jnp.int32, x.shape, 0)
        pre = _conv_pre(x, w, b_ref[...], row_ids)
        sg = _sigmoid(pre)
        dpre = dy * (sg * (1.0 + pre * (1.0 - sg)))
        dx = w[3:4, :] * dpre
        for k in (1, 2, 3):
            dx = dx + w[3 - k:4 - k, :] * _shift_up(dpre, k, row_ids)
        dx_ref[...] = dx.astype(dx_ref.dtype)
        db_ref[...] = jnp.sum(dpre, axis=0, keepdims=True)
        dw_ref[3:4, :] = jnp.sum(dpre * x, axis=0, keepdims=True)
        for k in (1, 2, 3):
            dw_ref[3 - k:4 - k, :] = jnp.sum(dpre * _shift_down(x, k, row_ids), axis=0, keepdims=True)

    col = pl.BlockSpec((t, LANES), lambda j: (0, j))
    clip = lambda j, lo, n: jnp.clip(j - lo, 0, n - 1)
    return pl.pallas_call(
        body, grid=(nct,),
        in_specs=[pl.BlockSpec((t, LANES), lambda j: (0, XBC_COL0 + j)),
                  pl.BlockSpec((SSD_CONV_WIDTH, LANES), lambda j: (0, j)),
                  pl.BlockSpec((1, LANES), lambda j: (0, j)),
                  pl.BlockSpec((t, LANES), lambda j: (0, clip(j, 0, n_xs))),
                  pl.BlockSpec((t, LANES), lambda j: (0, clip(j, n_xs, n_bm))),
                  pl.BlockSpec((t, LANES), lambda j: (0, clip(j, n_xs + n_bm, n_bm)))],
        out_specs=[col, pl.BlockSpec((SSD_CONV_WIDTH, LANES), lambda j: (0, j)), pl.BlockSpec((1, LANES), lambda j: (0, j))],
        out_shape=[jax.ShapeDtypeStruct((t, SSD_CONV_DIM), BF16),
                   jax.ShapeDtypeStruct((SSD_CONV_WIDTH, SSD_CONV_DIM), F32),
                   jax.ShapeDtypeStruct((1, SSD_CONV_DIM), F32)],
        compiler_params=_params("parallel"), name=name)(zx, conv_w, conv_b, d_xs, d_bm, d_cm)


def _softplus_fwd(zx, bias_row, alog_row, *, name):
    t = zx.shape[0]
    q = SSD_CHUNK
    tr = _row_tile(t, 1024)

    def body(x_ref, b_ref, al_ref, dt_ref, cum_ref):
        v = x_ref[...] + b_ref[...]
        e = jnp.exp(-jnp.abs(v))
        u = 1.0 + e
        log1p = jnp.where(u == 1.0, e, jnp.log(u) * (e / (u - 1.0)))
        dt = jnp.maximum(v, 0.0) + log1p
        dt_ref[...] = dt
        a = dt * -jnp.exp(al_ref[...])
        lower = (lax.broadcasted_iota(jnp.int32, (q, q), 1) <= lax.broadcasted_iota(jnp.int32, (q, q), 0)).astype(F32)
        cums = [lax.dot_general(lower, a[c * q:(c + 1) * q, :], ((((1,), (0,))), ((), ())), precision=lax.Precision.HIGHEST,
                                preferred_element_type=F32) for c in range(tr // q)]
        cum_ref[...] = jnp.concatenate(cums, axis=0)

    blk = pl.BlockSpec((tr, LANES), lambda i: (i, 0))
    vec = pl.BlockSpec((1, LANES), lambda i: (0, 0))
    return pl.pallas_call(
        body, grid=(t // tr,),
        in_specs=[pl.BlockSpec((tr, LANES), lambda i: (i, DT_COL0)), vec, vec],
        out_specs=[blk, blk],
        out_shape=[jax.ShapeDtypeStruct((t, LANES), F32), jax.ShapeDtypeStruct((t, LANES), F32)],
        compiler_params=_params("parallel"), name=name)(zx, bias_row, alog_row)


def _softplus_bwd(zx, bias_row, ddt, *, name):
    t = zx.shape[0]
    tr = _row_tile(t, 1024)

    def body(x_ref, b_ref, g_ref, o_ref, db_ref):
        v = x_ref[...] + b_ref[...]
        lane = lax.broadcasted_iota(jnp.int32, v.shape, 1)
        d = jnp.where(lane < SSD_N_HEADS, g_ref[...] * _sigmoid(v), 0.0)
        o_ref[...] = d.astype(o_ref.dtype)

        @pl.when(pl.program_id(0) == 0)
        def _():
            db_ref[...] = jnp.zeros_like(db_ref)

        db_ref[...] += jnp.sum(d, axis=0, keepdims=True)

    return pl.pallas_call(
        body, grid=(t // tr,),
        in_specs=[pl.BlockSpec((tr, LANES), lambda i: (i, DT_COL0)), pl.BlockSpec((1, LANES), lambda i: (0, 0)),
                  pl.BlockSpec((tr, LANES), lambda i: (i, 0))],
        out_specs=[pl.BlockSpec((tr, LANES), lambda i: (i, 0)), pl.BlockSpec((1, LANES), lambda i: (0, 0))],
        out_shape=[jax.ShapeDtypeStruct((t, LANES), BF16), jax.ShapeDtypeStruct((1, LANES), F32)],
        compiler_params=_params("arbitrary"), name=name)(zx, bias_row, ddt)


def _ssd_masks():
    q = SSD_CHUNK
    tt = lax.broadcasted_iota(jnp.int32, (q, q), 0)
    ss = lax.broadcasted_iota(jnp.int32, (q, q), 1)
    lane = lax.broadcasted_iota(jnp.int32, (1, SSD_GW), 1)
    srow = lax.broadcasted_iota(jnp.int32, (SSD_GW, 1), 0)
    hm = [(lane >= SSD_HEAD_DIM * j) & (lane < SSD_HEAD_DIM * (j + 1)) for j in range(SSD_HPG)]
    rm = [(srow >= SSD_HEAD_DIM * j) & (srow < SSD_HEAD_DIM * (j + 1)) for j in range(SSD_HPG)]
    return tt, ss, hm, rm


def _ssd_head_terms(dt_rows, cum_rows, a_rows, j, tt, ss):
    q = SSD_CHUNK
    dt_row = dt_rows[j:j + 1, :]
    dt_col = jnp.sum(jnp.where(tt == ss, dt_row, 0.0), axis=1, keepdims=True)
    a_row1 = a_rows[j:j + 1, :]
    a_11 = a_rows[j:j + 1, 0:1]
    cum_col = jnp.sum(jnp.where(ss <= tt, dt_row * a_row1, 0.0), axis=1, keepdims=True)
    cum_row = cum_rows[j:j + 1, :]
    decay = jnp.exp(jnp.where(ss <= tt, cum_col - cum_row, -jnp.inf))
    cum_last = cum_col[q - 1:q, :]
    e_col = jnp.exp(cum_col)
    dte_col = jnp.exp(cum_last - cum_col)
    e_last = jnp.exp(cum_last)
    return dt_col, dt_row, a_row1, a_11, decay, e_col, dte_col, e_last


SSD_CHUNKS_PER_STEP = 4
SSD_BC_COL0 = SSD_D_INNER // SSD_D_STATE


def _ssd_head_selects(terms, hm, rm):
    e_all = jnp.zeros((SSD_CHUNK, SSD_GW), F32)
    w_all = jnp.zeros((SSD_CHUNK, SSD_GW), F32)
    e_s = jnp.zeros((SSD_GW, 1), F32)
    for j in range(SSD_HPG):
        dt_col, _, _, _, _, e_col, dte_col, e_last = terms[j]
        e_all = jnp.where(hm[j], e_col, e_all)
        w_all = jnp.where(hm[j], dt_col * dte_col, w_all)
        e_s = jnp.where(rm[j], e_last, e_s)
    return e_all, w_all, e_s


def _ssd_fwd(xc, dtr, cumr, alog_b, d_b, *, name, hook=None):
    t = xc.shape[0]
    q = SSD_CHUNK
    nc = t // q
    kc = min(SSD_CHUNKS_PER_STEP, nc)
    rows = kc * q
    hk = _HookSlots(hook, n_in=7, n_out=2, n_scratch=1)

    def body(*refs):
        (x_ref, b_ref, c_ref, dtr_ref, cumr_ref, alog_ref, d_ref), (y_ref, st_ref), (s_scr,) = hk.own(refs)
        if hook is not None:
            hk.run(refs, pl.program_id(0) * (nc // kc) + pl.program_id(1), SSD_N_GROUPS * (nc // kc))

        @pl.when(pl.program_id(1) == 0)
        def _():
            s_scr[...] = jnp.zeros_like(s_scr)

        tt, ss, hm, rm = _ssd_masks()
        a_rows = -jnp.exp(alog_ref[...])
        d_rows = d_ref[...]
        d_all = jnp.zeros((1, SSD_GW), F32)
        for j in range(SSD_HPG):
            d_all = jnp.where(hm[j], d_rows[j:j + 1, 0:1], d_all)
        ks, hs = range(kc), range(SSD_HPG)
        sl = [pl.ds(k * q, q) for k in ks]
        x = [x_ref[sl[k], :] for k in ks]
        bm = [b_ref[sl[k], :].astype(BF16) for k in ks]
        cm = [c_ref[sl[k], :].astype(BF16) for k in ks]
        xb = [x[k].astype(BF16) for k in ks]
        terms = [[_ssd_head_terms(dtr_ref[:, sl[k]], cumr_ref[:, sl[k]], a_rows, j, tt, ss) for j in hs] for k in ks]
        g = [_dot_nt(cm[k], bm[k]) for k in ks]
        m = [[(g[k] * terms[k][j][4] * terms[k][j][1]).astype(BF16) for j in hs] for k in ks]
        yj = [[_dot_nn(m[k][j], xb[k]) for j in hs] for k in ks]
        sel = [_ssd_head_selects(terms[k], hm, rm) for k in ks]
        upd = [_dot_tn((x[k] * sel[k][1]).astype(BF16), bm[k]) for k in ks]
        states = [s_scr[...]]
        for k in ks:
            states.append(states[k] * sel[k][2] + upd[k])
        inter = [_dot_nt(cm[k], states[k].astype(BF16)) for k in ks]
        ys = []
        for k in ks:
            y = jnp.zeros((q, SSD_GW), F32)
            for j in hs:
                y = jnp.where(hm[j], yj[k][j], y)
            ys.append(y + inter[k] * sel[k][0] + x[k] * d_all)
        for k in ks:
            st_ref[k] = states[k]
        y_ref[...] = jnp.concatenate(ys, axis=0)
        s_scr[...] = states[kc]

    blk = lambda width, off: pl.BlockSpec((rows, width), lambda g, c: (c, off + g))
    par_s = pl.BlockSpec((None, SSD_HPG, LANES), lambda g, c: (g, 0, 0))
    row_s = pl.BlockSpec((None, SSD_HPG, rows), lambda g, c: (g, 0, c))
    outs = pl.pallas_call(
        body, grid=(SSD_N_GROUPS, nc // kc),
        in_specs=[blk(SSD_GW, 0), blk(SSD_D_STATE, SSD_BC_COL0), blk(SSD_D_STATE, SSD_BC_COL0 + SSD_N_GROUPS),
                  row_s, row_s, par_s, par_s] + hk.in_specs,
        out_specs=[blk(SSD_GW, 0), pl.BlockSpec((None, kc, SSD_GW, SSD_D_STATE), lambda g, c: (g, c, 0, 0))] + hk.out_specs,
        out_shape=[jax.ShapeDtypeStruct((t, SSD_D_INNER), F32),
                   jax.ShapeDtypeStruct((SSD_N_GROUPS, nc, SSD_GW, SSD_D_STATE), F32)] + hk.out_shape,
        scratch_shapes=[pltpu.VMEM((SSD_GW, SSD_D_STATE), F32)] + hk.scratch,
        compiler_params=_params(*hk.semantics("parallel", "arbitrary")), name=name)(
            xc, xc, xc, dtr, cumr, alog_b, d_b, *hk.inputs)
    return outs if hook is None else (outs[:2], outs[2:])


def _ssd_bwd(xc, dtr, cumr, alog_b, d_b, states, dy, *, name, hook=None):
    t = xc.shape[0]
    q = SSD_CHUNK
    nc = t // q
    kc = min(SSD_CHUNKS_PER_STEP, nc)
    nst = nc // kc
    rows = kc * q
    rev = lambda c: nst - 1 - c
    hk = _HookSlots(hook, n_in=9, n_out=5, n_scratch=1)

    def body(*refs):
        ((x_ref, b_ref, c_ref, dtr_ref, cumr_ref, alog_ref, d_ref, st_ref, dy_ref),
         (dx_ref, db_ref, dc_ref, ddt_ref, dpar_ref), (ds_scr,)) = hk.own(refs)
        if hook is not None:
            hk.run(refs, pl.program_id(0) * nst + pl.program_id(1), SSD_N_GROUPS * nst)

        @pl.when(pl.program_id(1) == 0)
        def _():
            ds_scr[...] = jnp.zeros_like(ds_scr)
            dpar_ref[...] = jnp.zeros_like(dpar_ref)

        tt, ss, hm, rm = _ssd_masks()
        tcol = lax.broadcasted_iota(jnp.int32, (q, 1), 0)
        lane = lax.broadcasted_iota(jnp.int32, (1, LANES), 1)
        a_rows = -jnp.exp(alog_ref[...])
        d_rows = d_ref[...]
        d_all = jnp.zeros((1, SSD_GW), F32)
        for j in range(SSD_HPG):
            d_all = jnp.where(hm[j], d_rows[j:j + 1, 0:1], d_all)
        ks, hs = range(kc), range(SSD_HPG)
        sl = [pl.ds(k * q, q) for k in ks]
        x = [x_ref[sl[k], :] for k in ks]
        dyv = [dy_ref[sl[k], :] for k in ks]
        bm = [b_ref[sl[k], :].astype(BF16) for k in ks]
        cm = [c_ref[sl[k], :].astype(BF16) for k in ks]
        s_in = [st_ref[k] for k in ks]
        xb = [x[k].astype(BF16) for k in ks]
        dyb = [dyv[k].astype(BF16) for k in ks]
        s_b = [s_in[k].astype(BF16) for k in ks]
        terms = [[_ssd_head_terms(dtr_ref[:, sl[k]], cumr_ref[:, sl[k]], a_rows, j, tt, ss) for j in hs] for k in ks]
        sel = [_ssd_head_selects(terms[k], hm, rm) for k in ks]
        e_all, w_all, e_s = [s_[0] for s_ in sel], [s_[1] for s_ in sel], [s_[2] for s_ in sel]
        dye = [(dyv[k] * e_all[k]).astype(BF16) for k in ks]
        ds_loc = [_dot_tn(dye[k], cm[k]) for k in ks]
        ds = [None] * kc
        running = ds_scr[...]
        for k in reversed(ks):
            ds[k] = running
            running = running * e_s[k] + ds_loc[k]
        ds_scr[...] = running
        ds_b = [ds[k].astype(BF16) for k in ks]
        g = [_dot_nt(cm[k], bm[k]) for k in ks]
        cs = [_dot_nt(cm[k], s_b[k]) for k in ks]
        bds = [_dot_nt(bm[k], ds_b[k]) for k in ks]
        dm = [[_dot_nt(jnp.where(hm[j], dyv[k], 0.0).astype(BF16), xb[k]) for j in hs] for k in ks]
        gl = [[g[k] * terms[k][j][4] for j in hs] for k in ks]
        wp = [[dm[k][j] * gl[k][j] for j in hs] for k in ks]
        mt = [[(gl[k][j] * terms[k][j][1]).astype(BF16) for j in hs] for k in ks]
        dxj = [[_dot_tn(mt[k][j], dyb[k]) for j in hs] for k in ks]
        dg = []
        for k in ks:
            acc = jnp.zeros((q, q), F32)
            for j in hs:
                acc = acc + dm[k][j] * terms[k][j][4] * terms[k][j][1]
            dg.append(acc.astype(BF16))
        dy_cs = [dyv[k] * cs[k] for k in ks]
        x_bds = [x[k] * bds[k] for k in ks]
        dy_x = [dyv[k] * x[k] for k in ks]
        ds_s = [ds[k] * s_in[k] for k in ks]
        w = [[wp[k][j] * terms[k][j][1] for j in hs] for k in ks]
        rw_col = [[jnp.sum(w[k][j], axis=1, keepdims=True) for j in hs] for k in ks]
        cw_row = [[jnp.sum(w[k][j], axis=0, keepdims=True) for j in hs] for k in ks]
        cwp_row = [[jnp.sum(wp[k][j], axis=0, keepdims=True) for j in hs] for k in ks]
        r1_col = [[jnp.sum(jnp.where(hm[j], dy_cs[k], 0.0), axis=1, keepdims=True) * terms[k][j][5] for j in hs] for k in ks]
        dw_col = [[jnp.sum(jnp.where(hm[j], x_bds[k], 0.0), axis=1, keepdims=True) for j in hs] for k in ks]
        head_rows = [slice(j * SSD_HEAD_DIM, (j + 1) * SSD_HEAD_DIM) for j in hs]
        lane_sum = lambda v: jnp.sum(v, axis=1, keepdims=True)
        s_sum = [[lane_sum(jnp.sum(ds_s[k][head_rows[j], :], axis=0, keepdims=True)) for j in hs] for k in ks]
        dy_x_cols = [jnp.sum(dy_x[k], axis=0, keepdims=True) for k in ks]
        d_d = [[lane_sum(jnp.where(hm[j], dy_x_cols[k], 0.0)) for j in hs] for k in ks]
        ddt_rows = [[None] * SSD_HPG for _ in ks]
        dpar = [jnp.zeros((1, LANES), F32) for _ in hs]
        for k in ks:
            for j in hs:
                dt_col, dt_row, a_row1, a_11, _, _, dte_col, e_last = terms[k][j]
                dww = dw_col[k][j] * (dt_col * dte_col)
                last_add = jnp.sum(dww, axis=0, keepdims=True) + e_last * s_sum[k][j]
                dcum_col = rw_col[k][j] + r1_col[k][j] - dww + jnp.where(tcol == q - 1, last_add, 0.0)
                da_row = jnp.sum(jnp.where(tt >= ss, dcum_col, 0.0), axis=0, keepdims=True)
                da_col = jnp.sum(jnp.where(ss >= tt, -cw_row[k][j], 0.0), axis=1, keepdims=True)
                ddt_col = a_11 * da_col + dw_col[k][j] * dte_col
                ddt_rows[k][j] = (a_row1 * da_row + cwp_row[k][j]
                                  + jnp.sum(jnp.where(tt == ss, ddt_col, 0.0), axis=0, keepdims=True))
                d_a = jnp.sum(dt_row * da_row, axis=1, keepdims=True) + jnp.sum(dt_col * da_col, axis=0, keepdims=True)
                dpar[j] = dpar[j] + jnp.where(lane == 0, d_a * a_11, 0.0) + jnp.where(lane == 1, d_d[k][j], 0.0)
        dxs = []
        for k in ks:
            acc = jnp.zeros((q, SSD_GW), F32)
            for j in hs:
                acc = jnp.where(hm[j], dxj[k][j], acc)
            dxs.append(acc + w_all[k] * bds[k] + d_all * dyv[k])
        xw = [(x[k] * w_all[k]).astype(BF16) for k in ks]
        dc = [_dot_nn(dg[k], bm[k]) + _dot_nn(dye[k], s_b[k]) for k in ks]
        db = [_dot_tn(dg[k], cm[k]) + _dot_nn(xw[k], ds_b[k]) for k in ks]
        dx_ref[...] = jnp.concatenate(dxs, axis=0)
        dc_ref[...] = jnp.concatenate(dc, axis=0)
        db_ref[...] = jnp.concatenate(db, axis=0)
        ddt_ref[...] = jnp.concatenate([jnp.concatenate([ddt_rows[k][j] for k in ks], axis=1) for j in hs], axis=0)
        dpar_ref[...] += jnp.concatenate(dpar, axis=0)

    blk = lambda width, off: pl.BlockSpec((rows, width), lambda g, c: (rev(c), off + g))
    par_s = pl.BlockSpec((None, SSD_HPG, LANES), lambda g, c: (g, 0, 0))
    outs = pl.pallas_call(
        body, grid=(SSD_N_GROUPS, nst),
        in_specs=[blk(SSD_GW, 0), blk(SSD_D_STATE, SSD_BC_COL0), blk(SSD_D_STATE, SSD_BC_COL0 + SSD_N_GROUPS),
                  pl.BlockSpec((None, SSD_HPG, rows), lambda g, c: (g, 0, rev(c))),
                  pl.BlockSpec((None, SSD_HPG, rows), lambda g, c: (g, 0, rev(c))), par_s, par_s,
                  pl.BlockSpec((None, kc, SSD_GW, SSD_D_STATE), lambda g, c: (g, rev(c), 0, 0)), blk(SSD_GW, 0)] + hk.in_specs,
        out_specs=[blk(SSD_GW, 0), blk(SSD_D_STATE, 0), blk(SSD_D_STATE, 0),
                   pl.BlockSpec((None, SSD_HPG, rows), lambda g, c: (g, 0, rev(c))), par_s] + hk.out_specs,
        out_shape=[jax.ShapeDtypeStruct((t, SSD_D_INNER), F32),
                   jax.ShapeDtypeStruct((t, SSD_N_GROUPS * SSD_D_STATE), F32),
                   jax.ShapeDtypeStruct((t, SSD_N_GROUPS * SSD_D_STATE), F32),
                   jax.ShapeDtypeStruct((SSD_N_GROUPS, SSD_HPG, t), F32),
                   jax.ShapeDtypeStruct((SSD_N_GROUPS, SSD_HPG, LANES), F32)] + hk.out_shape,
        scratch_shapes=[pltpu.VMEM((SSD_GW, SSD_D_STATE), F32)] + hk.scratch,
        compiler_params=_params(*hk.semantics("parallel", "arbitrary")), name=name)(
            xc, xc, xc, dtr, cumr, alog_b, d_b, states, dy, *hk.inputs)
    return outs if hook is None else (outs[:5], outs[5:])


def _gate_norm_fwd(y, zx, norm_w, *, name):
    t = y.shape[0]
    tr = _row_tile(t, 256)
    row = pl.BlockSpec((tr, SSD_D_INNER), lambda i: (i, 0))

    def body(y_ref, z_ref, w_ref, o_ref):
        for gi in range(SSD_N_GROUPS):
            sl = pl.ds(gi * SSD_GW, SSD_GW)
            z = z_ref[:, sl]
            gv = y_ref[:, sl] * (z * _sigmoid(z))
            r = lax.rsqrt(jnp.mean(gv * gv, axis=-1, keepdims=True) + NORM_EPS)
            o_ref[:, sl] = (gv * r * w_ref[:, sl]).astype(BF16)

    return pl.pallas_call(
        body, grid=(t // tr,), in_specs=[row, row, pl.BlockSpec((1, SSD_D_INNER), lambda i: (0, 0))],
        out_specs=row, out_shape=jax.ShapeDtypeStruct((t, SSD_D_INNER), BF16),
        compiler_params=_params("parallel"), name=name)(y, zx, norm_w)


def _gate_norm_bwd(y, zx, norm_w, dyn, *, name):
    t = y.shape[0]
    tr = _row_tile(t, 256)
    row = pl.BlockSpec((tr, SSD_D_INNER), lambda i: (i, 0))
    vec = pl.BlockSpec((1, SSD_D_INNER), lambda i: (0, 0))

    def body(y_ref, z_ref, w_ref, dyn_ref, dy_ref, dz_ref, dw_ref):
        @pl.when(pl.program_id(0) == 0)
        def _():
            dw_ref[...] = jnp.zeros_like(dw_ref)

        for gi in range(SSD_N_GROUPS):
            sl = pl.ds(gi * SSD_GW, SSD_GW)
            z = z_ref[:, sl]
            yv = y_ref[:, sl]
            sg = _sigmoid(z)
            sz = z * sg
            gv = yv * sz
            r = lax.rsqrt(jnp.mean(gv * gv, axis=-1, keepdims=True) + NORM_EPS)
            ghat = gv * r
            dout = dyn_ref[:, sl].astype(F32)
            dgh = dout * w_ref[:, sl]
            dgv = r * (dgh - ghat * jnp.mean(dgh * ghat, axis=-1, keepdims=True))
            dy_ref[:, sl] = dgv * sz
            dz_ref[:, sl] = (dgv * yv * (sg * (1.0 + z * (1.0 - sg)))).astype(dz_ref.dtype)
            dw_ref[:, sl] += jnp.sum(dout * ghat, axis=0, keepdims=True)

    return pl.pallas_call(
        body, grid=(t // tr,), in_specs=[row, row, vec, row], out_specs=[row, row, vec],
        out_shape=[jax.ShapeDtypeStruct((t, SSD_D_INNER), F32), jax.ShapeDtypeStruct((t, SSD_D_INNER), BF16),
                   jax.ShapeDtypeStruct((1, SSD_D_INNER), F32)],
        compiler_params=_params("arbitrary"), name=name)(y, zx, norm_w, dyn)


ATTN_KV_W = ATTN_N_KV * ATTN_HEAD_DIM
ATTN_Q_HALF = 512
ATTN_K_BLK = ATTN_N_Q * ATTN_HEAD_DIM // ATTN_KV_W
ATTN_V_BLK = ATTN_K_BLK + 1


def _attn_valid(first_block):
    w = ATTN_WINDOW
    qpos = lax.broadcasted_iota(jnp.int32, (w, 2 * w), 0) + w
    kpos = lax.broadcasted_iota(jnp.int32, (w, 2 * w), 1)
    rel = qpos - kpos
    return (rel >= 0) & (rel < w) & jnp.logical_not(first_block & (kpos < w))


def _attn_block_views(lo_ref, hi_ref, kc_ref, kp_ref, vc_ref, vp_ref):
    hd = ATTN_HEAD_DIM
    per_half = ATTN_Q_HALF // hd
    heads = [(lo_ref if h < per_half else hi_ref)[:, pl.ds((h % per_half) * hd, hd)] for h in range(ATTN_N_Q)]
    kv_cols = [pl.ds(kh * hd, hd) for kh in range(ATTN_N_KV)]
    kb = [jnp.concatenate([kp_ref[:, c], kc_ref[:, c]], axis=0) for c in kv_cols]
    vb = [jnp.concatenate([vp_ref[:, c], vc_ref[:, c]], axis=0) for c in kv_cols]
    return heads, kb, vb


def _attn_softmax(q, kb, sink, valid):
    heads = range(ATTN_N_Q)
    scale = ATTN_HEAD_DIM ** -0.5
    s = [jnp.where(valid, _dot_nt(q[h], kb[h // ATTN_REP]) * scale, -jnp.inf) for h in heads]
    m = [jnp.maximum(jnp.max(s[h], axis=1, keepdims=True), sink[h]) for h in heads]
    e = [jnp.exp(s[h] - m[h]) for h in heads]
    es = [jnp.exp(sink[h] - m[h]) for h in heads]
    inv = [1.0 / (jnp.sum(e[h], axis=1, keepdims=True) + es[h]) for h in heads]
    return e, es, inv


def _attn_fwd(qkv, sinks_b, *, name):
    t = qkv.shape[0]
    w = ATTN_WINDOW
    nb = t // w
    prev = lambda n: jnp.maximum(n - 1, 0)

    def body(qlo_ref, qhi_ref, kc_ref, kp_ref, vc_ref, vp_ref, sink_ref, o_ref):
        heads = range(ATTN_N_Q)
        q, kb, vb = _attn_block_views(qlo_ref, qhi_ref, kc_ref, kp_ref, vc_ref, vp_ref)
        sink = [sink_ref[h:h + 1, 0:1] for h in heads]
        e, _, inv = _attn_softmax(q, kb, sink, _attn_valid(pl.program_id(0) == 0))
        out = [_dot_nn((e[h] * inv[h]).astype(BF16), vb[h // ATTN_REP]).astype(o_ref.dtype) for h in heads]
        o_ref[...] = jnp.concatenate(out, axis=1)

    qh = lambda half: pl.BlockSpec((w, ATTN_Q_HALF), lambda n: (n, half))
    kv = lambda blk, idx: pl.BlockSpec((w, ATTN_KV_W), lambda n: (idx(n), blk))
    cur = lambda n: n
    return pl.pallas_call(
        body, grid=(nb,),
        in_specs=[qh(0), qh(1), kv(ATTN_K_BLK, cur), kv(ATTN_K_BLK, prev), kv(ATTN_V_BLK, cur), kv(ATTN_V_BLK, prev),
                  pl.BlockSpec((ATTN_N_Q, LANES), lambda n: (0, 0))],
        out_specs=pl.BlockSpec((w, D_MODEL), lambda n: (n, 0)),
        out_shape=jax.ShapeDtypeStruct((t, D_MODEL), BF16),
        compiler_params=_params("parallel"), name=name)(qkv, qkv, qkv, qkv, qkv, qkv, sinks_b)


def _attn_bwd(qkv, sinks_b, dout, *, name):
    t = qkv.shape[0]
    w = ATTN_WINDOW
    nb = t // w
    hd = ATTN_HEAD_DIM
    clamp = lambda n: jnp.minimum(n, nb - 1)
    prev = lambda n: jnp.maximum(clamp(n) - 1, 0)

    def body(qlo_ref, qhi_ref, kc_ref, kp_ref, vc_ref, vp_ref, sink_ref, dolo_ref, dohi_ref,
             dq_ref, dkv_ref, dsink_ref, carry):
        n = pl.program_id(0)

        @pl.when(n == 0)
        def _():
            carry[...] = jnp.zeros_like(carry)
            dsink_ref[...] = jnp.zeros_like(dsink_ref)

        @pl.when(n < nb)
        def _():
            heads, kvs = range(ATTN_N_Q), range(ATTN_N_KV)
            q, kb, vb = _attn_block_views(qlo_ref, qhi_ref, kc_ref, kp_ref, vc_ref, vp_ref)
            do, _, _ = _attn_block_views(dolo_ref, dohi_ref, kc_ref, kp_ref, vc_ref, vp_ref)
            sink = [sink_ref[h:h + 1, 0:1] for h in heads]
            e, es, inv = _attn_softmax(q, kb, sink, _attn_valid(n == 0))
            dp = [_dot_nt(do[h], vb[h // ATTN_REP]) for h in heads]
            p = [e[h] * inv[h] for h in heads]
            delta = [jnp.sum(p[h] * dp[h], axis=1, keepdims=True) for h in heads]
            dsc = [(p[h] * (dp[h] - delta[h]) * (hd ** -0.5)).astype(BF16) for h in heads]
            pb = [p[h].astype(BF16) for h in heads]
            dq = [_dot_nn(dsc[h], kb[h // ATTN_REP]).astype(dq_ref.dtype) for h in heads]
            stack = lambda per_head, kh: jnp.concatenate(per_head[kh * ATTN_REP:(kh + 1) * ATTN_REP], axis=0)
            dkb = [_dot_tn(stack(dsc, kh), stack(q, kh)) for kh in kvs]
            dvb = [_dot_tn(stack(pb, kh), stack(do, kh)) for kh in kvs]
            dsink = [jnp.broadcast_to(jnp.sum(-es[h] * inv[h] * delta[h], axis=0, keepdims=True), (1, LANES)) for h in heads]
            dq_ref[...] = jnp.concatenate(dq, axis=1)
            dsink_ref[...] += jnp.concatenate(dsink, axis=0)
            dkv_ref[...] = (carry[...] + jnp.concatenate([d[0:w, :] for d in dkb + dvb], axis=1)).astype(dkv_ref.dtype)
            carry[...] = jnp.concatenate([d[w:2 * w, :] for d in dkb + dvb], axis=1)

        @pl.when(n == nb)
        def _():
            dkv_ref[...] = carry[...].astype(dkv_ref.dtype)

    qh = lambda half: pl.BlockSpec((w, ATTN_Q_HALF), lambda n: (clamp(n), half))
    kv = lambda blk, idx: pl.BlockSpec((w, ATTN_KV_W), lambda n: (idx(n), blk))
    return pl.pallas_call(
        body, grid=(nb + 1,),
        in_specs=[qh(0), qh(1), kv(ATTN_K_BLK, clamp), kv(ATTN_K_BLK, prev), kv(ATTN_V_BLK, clamp), kv(ATTN_V_BLK, prev),
                  pl.BlockSpec((ATTN_N_Q, LANES), lambda n: (0, 0)), qh(0), qh(1)],
        out_specs=[pl.BlockSpec((w, D_MODEL), lambda n: (clamp(n), 0)),
                   pl.BlockSpec((w, 2 * ATTN_KV_W), lambda n: (jnp.maximum(n - 1, 0), 0)),
                   pl.BlockSpec((ATTN_N_Q, LANES), lambda n: (0, 0))],
        out_shape=[jax.ShapeDtypeStruct((t, D_MODEL), BF16), jax.ShapeDtypeStruct((t, 2 * ATTN_KV_W), BF16),
                   jax.ShapeDtypeStruct((ATTN_N_Q, LANES), F32)],
        scratch_shapes=[pltpu.VMEM((w, 2 * ATTN_KV_W), F32)],
        compiler_params=_params("arbitrary"), name=name)(qkv, qkv, qkv, qkv, qkv, qkv, sinks_b, dout, dout)


def _sq_relu_epilogue(acc):
    r = jnp.maximum(acc, 0.0)
    return acc, r * r


def _sq_relu_bwd_epilogue(acc, pre):
    return (acc * (2.0 * jnp.maximum(pre, 0.0)),)


def _bias_epilogue(acc, bias):
    return (acc + bias,)


def _mlp_fwd(u, w_up, w_down, tag):
    pre, act = _matmul(u, w_up, mode="nn", out_dtypes=(F32, BF16), epilogue=_sq_relu_epilogue, b_shards=True,
                       name=f"mlp_up_{tag}")
    f = _matmul(act, w_down, mode="nn", out_dtypes=(F32,), name=f"mlp_down_{tag}")
    return pre, act, f


def _mlp_bwd(u, pre, act, w_up, w_down, df, tag):
    dpre = _matmul(df, w_down, mode="nt", out_dtypes=(BF16,), epilogue=_sq_relu_bwd_epilogue,
                   extras=((pre, "tile"),), name=f"mlp_dact_{tag}")
    dw_down = _matmul(act, df, mode="tn", out_dtypes=(BF16,), name=f"mlp_dwdown_{tag}")
    du = _matmul(dpre, w_up, mode="nt", out_dtypes=(F32,), b_shards=True, name=f"mlp_du_{tag}")
    dw_up = _matmul(u, dpre, mode="tn", out_dtypes=(BF16,), out_shards=True, name=f"mlp_dwup_{tag}")
    return du, dw_up, dw_down


def _group_rows(dt):
    t = dt.shape[0]
    return jnp.transpose(dt[:, :SSD_N_HEADS].reshape(t, SSD_N_GROUPS, SSD_HPG), (1, 2, 0))


def _head_param_rows(p):
    return jnp.broadcast_to(p.reshape(SSD_N_GROUPS, SSD_HPG, 1), (SSD_N_GROUPS, SSD_HPG, LANES))


def _local_step(x, target, wts, comm=None):
    t = x.shape[0]
    wts = dict(wts)
    row = lambda v: v.reshape(1, -1)
    mix_pre, mix_post, ffn_pre, ffn_post = wts["mix_pre_norm"], wts["mix_post_norm"], wts["ffn_pre_norm"], wts["ffn_post_norm"]

    def gathering(stage, fn, *args, **kwargs):
        if comm is None:
            return fn(*args, **kwargs)
        out, got = fn(*args, hook=comm.gather_hook(stage), **kwargs)
        wts.update(comm.weights_from(stage, got))
        return out

    u0 = _rms_fwd(x, row(mix_pre[0]), name="rms_pre_mix0")
    zx = gathering("in_proj", _matmul, u0, wts["ssd_w_in"], mode="nn", out_dtypes=(F32,), tn=896, name="ssd_in_proj")
    xc = gathering("conv", _conv_fwd, zx, wts["ssd_conv_w"], row(wts["ssd_conv_b"]), name="ssd_conv_fwd")
    bias_row = jnp.pad(wts["ssd_dt_bias"], (0, LANES - SSD_N_HEADS)).reshape(1, LANES)
    alog_row = jnp.pad(wts["ssd_a_log"], (0, LANES - SSD_N_HEADS)).reshape(1, LANES)
    dt, cum = _softplus_fwd(zx, bias_row, alog_row, name="ssd_dt_fwd")
    dtr, cumr = _group_rows(dt), _group_rows(cum)
    alog_b, d_b = _head_param_rows(wts["ssd_a_log"]), _head_param_rows(wts["ssd_d"])
    y_ssd, states = gathering("scan", _ssd_fwd, xc, dtr, cumr, alog_b, d_b, name="ssd_scan_fwd")
    norm_w = row(wts["ssd_norm_w"])
    yn = _gate_norm_fwd(y_ssd, zx, norm_w, name="ssd_gate_norm_fwd")
    mix0 = _matmul(yn, wts["ssd_w_out"], mode="nn", out_dtypes=(F32,), name="ssd_out_proj")
    h1, v0 = _rms_fwd(mix0, row(mix_post[0]), resid=x, want_u=row(ffn_pre[0]), name="rms_post_mix0")
    pre0, act0, f0 = _mlp_fwd(v0, wts["mlp_w_up0"], wts["mlp_w_down0"], "l0")
    h2, u1 = _rms_fwd(f0, row(ffn_post[0]), resid=h1, want_u=row(mix_pre[1]), name="rms_post_ffn0")

    qkv = _matmul(u1, wts["attn_w_qkv"], mode="nn", out_dtypes=(BF16,), epilogue=_bias_epilogue,
                  extras=((row(wts["attn_b_qkv"]), "row"),), b_shards=True, name="attn_qkv_proj")
    sinks_b = jnp.broadcast_to(wts["attn_sinks"].reshape(ATTN_N_Q, 1), (ATTN_N_Q, LANES))
    ao = _attn_fwd(qkv, sinks_b, name="attn_fwd")
    mix1 = _matmul(ao, wts["attn_w_o"], mode="nn", out_dtypes=(F32,), epilogue=_bias_epilogue,
                   extras=((row(wts["attn_b_o"]), "row"),), name="attn_out_proj")
    h3, v1 = _rms_fwd(mix1, row(mix_post[1]), resid=h2, want_u=row(ffn_pre[1]), name="rms_post_mix1")
    pre1, act1, f1 = _mlp_fwd(v1, wts["mlp_w_up1"], wts["mlp_w_down1"], "l1")
    h4 = _rms_fwd(f1, row(ffn_post[1]), resid=h3, name="rms_post_ffn1")

    dh4, loss_tile = _loss_head(h4, target, name="loss_head")

    df1, g_ffn_post1 = _rms_bwd(f1, row(ffn_post[1]), dh4, out_dtype=BF16, name="rms_post_ffn1_bwd")
    dv1, g_up1, g_down1 = _mlp_bwd(v1, pre1, act1, wts["mlp_w_up1"], wts["mlp_w_down1"], df1, "l1")
    dh3, g_ffn_pre1 = _rms_bwd(h3, row(ffn_pre[1]), dv1, resid=dh4, name="rms_pre_ffn1_bwd")
    dmix1, g_mix_post1 = _rms_bwd(mix1, row(mix_post[1]), dh3, out_dtype=BF16, name="rms_post_mix1_bwd")
    g_b_o = _col_sum(dmix1, name="attn_bo_grad")
    g_w_o = _matmul(ao, dmix1, mode="tn", out_dtypes=(BF16,), name="attn_dwo")
    dao = _matmul(dmix1, wts["attn_w_o"], mode="nt", out_dtypes=(BF16,), name="attn_dao")
    dq, dkv, g_sinks = _attn_bwd(qkv, sinks_b, dao, name="attn_bwd")
    dqkv = jnp.concatenate([dq, dkv], axis=1)
    g_b_qkv = _col_sum(dqkv, name="attn_bqkv_grad")
    g_w_qkv = _matmul(u1, dqkv, mode="tn", out_dtypes=(BF16,), tn=ATTN_QKV // N_CHIPS, out_shards=True, name="attn_dwqkv")
    du1 = _matmul(dqkv, wts["attn_w_qkv"], mode="nt", out_dtypes=(F32,), b_shards=True, name="attn_du")
    dh2, g_mix_pre1 = _rms_bwd(h2, row(mix_pre[1]), du1, resid=dh3, name="rms_pre_mix1_bwd")

    df0, g_ffn_post0 = _rms_bwd(f0, row(ffn_post[0]), dh2, out_dtype=BF16, name="rms_post_ffn0_bwd")
    dv0, g_up0, g_down0 = _mlp_bwd(v0, pre0, act0, wts["mlp_w_up0"], wts["mlp_w_down0"], df0, "l0")
    dh1, g_ffn_pre0 = _rms_bwd(h1, row(ffn_pre[0]), dv0, resid=dh2, name="rms_pre_ffn0_bwd")
    dmix0, g_mix_post0 = _rms_bwd(mix0, row(mix_post[0]), dh1, out_dtype=BF16, name="rms_post_mix0_bwd")
    g_w_out = _matmul(yn, dmix0, mode="tn", out_dtypes=(BF16,), name="ssd_dwout")
    dyn = _matmul(dmix0, wts["ssd_w_out"], mode="nt", out_dtypes=(BF16,), name="ssd_dyn")
    dy_ssd, dz, g_norm_w = _gate_norm_bwd(y_ssd, zx, norm_w, dyn, name="ssd_gate_norm_bwd")
    mats = {"ssd_w_out": g_w_out, "attn_w_qkv": g_w_qkv, "attn_w_o": g_w_o,
            "mlp_w_up0": g_up0, "mlp_w_up1": g_up1, "mlp_w_down0": g_down0, "mlp_w_down1": g_down1}
    if comm is None:
        dxc, dbm, dcm, ddt_r, dpar = _ssd_bwd(xc, dtr, cumr, alog_b, d_b, states, dy_ssd, name="ssd_scan_bwd")
    else:
        (dxc, dbm, dcm, ddt_r, dpar), received = _ssd_bwd(xc, dtr, cumr, alog_b, d_b, states, dy_ssd,
                                                          name="ssd_scan_bwd", hook=comm.exchange_hook(mats, "early"))
        comm.received(received)
    dxbc, g_conv_w, g_conv_b = _conv_bwd(zx, wts["ssd_conv_w"], row(wts["ssd_conv_b"]), dxc, dbm, dcm, name="ssd_conv_bwd")
    ddt = jnp.pad(jnp.transpose(ddt_r, (2, 0, 1)).reshape(t, SSD_N_HEADS), ((0, 0), (0, LANES - SSD_N_HEADS)))
    ddt_raw, g_dt_bias = _softplus_bwd(zx, bias_row, ddt, name="ssd_dt_bwd")
    dzx = jnp.concatenate([dz, dxbc, ddt_raw], axis=1)
    g_w_in = _w_in_to_shards(_matmul(u0, dzx, mode="tn", out_dtypes=(F32,), tn=896, name="ssd_dwin"), name="ssd_dwin_shards")
    mats["ssd_w_in"] = g_w_in
    if comm is None:
        du0 = _matmul(dzx, wts["ssd_w_in"], mode="nt", out_dtypes=(F32,), tk=896, name="ssd_du")
    else:
        du0, received = _matmul(dzx, wts["ssd_w_in"], mode="nt", out_dtypes=(F32,), tk=896, name="ssd_du",
                                hook=comm.exchange_hook(mats, "late"))
        comm.received(received)
    grad_x, g_mix_pre0 = _rms_bwd(x, row(mix_pre[0]), du0, resid=dh1, name="rms_pre_mix0_bwd")

    dpar = dpar.reshape(SSD_N_HEADS, LANES)
    vecs = {
        "ssd_conv_w": g_conv_w, "ssd_conv_b": g_conv_b.reshape(-1),
        "ssd_dt_bias": g_dt_bias[0, :SSD_N_HEADS], "ssd_a_log": dpar[:, 0], "ssd_d": dpar[:, 1],
        "ssd_norm_w": g_norm_w.reshape(-1), "attn_b_qkv": g_b_qkv.reshape(-1), "attn_sinks": g_sinks[:, 0],
        "attn_b_o": g_b_o.reshape(-1),
        "mix_pre_norm": jnp.concatenate([g_mix_pre0, g_mix_pre1]), "mix_post_norm": jnp.concatenate([g_mix_post0, g_mix_post1]),
        "ffn_pre_norm": jnp.concatenate([g_ffn_pre0, g_ffn_pre1]), "ffn_post_norm": jnp.concatenate([g_ffn_post0, g_ffn_post1]),
    }
    return loss_tile, grad_x, mats, vecs


def _mesh_position():
    return lax.axis_index("x"), lax.axis_index("y"), lax.axis_index("c")


def _flip(v, bit):
    return 1 - v if bit else v


OTHER_CHIPS = ((1, 0), (0, 1), (1, 1))


def _comm_params():
    return pltpu.CompilerParams(vmem_limit_bytes=VMEM_LIMIT)


def _staged_copies(srcs, dsts, bufs, sems_in, sems_out):
    loads = [pltpu.make_async_copy(s, b, sems_in.at[i]) for i, (s, b) in enumerate(zip(srcs, bufs))]
    stores = [pltpu.make_async_copy(b, d, sems_out.at[i]) for i, (b, d) in enumerate(zip(bufs, dsts))]
    return loads, stores


class _GatherHook:
    def __init__(self, mats, vecs=()):
        self.arrs = list(mats) + list(vecs)
        self.nm, self.n = len(mats), len(self.arrs)
        n_ici, n_fwd = (N_CHIPS - 1) * self.n, max((N_CHIPS - 1) * self.nm, 1)
        dma = pltpu.SemaphoreType.DMA
        self.out_shape = [jax.ShapeDtypeStruct((N_CHIPS,) + a.shape, a.dtype) for a in self.arrs]
        self.scratch = [pltpu.VMEM(a.shape, a.dtype) for a in self.arrs] + [
            dma((n_ici,)), dma((n_ici,)), dma((n_fwd,)), dma((n_fwd,)), dma((self.n,)), dma((self.n,))]

    def plan(self, ins, outs, scratch):
        n, nm = self.n, self.nm
        bufs = scratch[:n]
        ici_send, ici_recv, fwd_send, fwd_recv, load_sems, store_sems = scratch[n:]
        xi, yi, ci = _mesh_position()
        me = 2 * xi + yi
        loads, stores = _staged_copies(ins, [outs[i].at[me] for i in range(n)], bufs, load_sems, store_sems)
        sends, landed, forwards, from_sibling = [], [], [], []
        for j, (bx, by) in enumerate(OTHER_CHIPS):
            px, py = _flip(xi, bx), _flip(yi, by)
            peer = 2 * px + py
            for i in range(n):
                k = j * n + i
                mk = functools.partial(pltpu.make_async_remote_copy, send_sem=ici_send.at[k], recv_sem=ici_recv.at[k],
                                       device_id=(px, py, ci), device_id_type=MESH)
                if i < nm:
                    sends.append(mk(src_ref=ins[i].at[ci], dst_ref=outs[i].at[me, ci]))
                    landed.append(mk(src_ref=ins[i].at[ci], dst_ref=outs[i].at[peer, ci]))
                    kf = j * nm + i
                    fw = functools.partial(pltpu.make_async_remote_copy, send_sem=fwd_send.at[kf], recv_sem=fwd_recv.at[kf],
                                           device_id=(xi, yi, 1 - ci), device_id_type=MESH)
                    forwards.append(fw(src_ref=outs[i].at[peer, ci], dst_ref=outs[i].at[peer, ci]))
                    from_sibling.append(fw(src_ref=outs[i].at[peer, ci], dst_ref=outs[i].at[peer, 1 - ci]))
                else:
                    sends.append(mk(src_ref=ins[i], dst_ref=outs[i].at[me]))
                    landed.append(mk(src_ref=ins[i], dst_ref=outs[i].at[peer]))
                    forwards.append(None)
        return loads, stores, sends, landed, forwards, from_sibling

    @staticmethod
    def start(p):
        loads, _, sends, _, _, _ = p
        for cp in loads + sends:
            cp.start()

    @staticmethod
    def relay(p):
        loads, stores, _, landed, forwards, _ = p
        for ld, st in zip(loads, stores):
            ld.wait()
            st.start()
        for cp, fw in zip(landed, forwards):
            cp.wait_recv()
            if fw is not None:
                fw.start()

    @staticmethod
    def finish(p):
        _, stores, sends, _, forwards, from_sibling = p
        for cp in from_sibling:
            cp.wait_recv()
        for cp in sends + [fw for fw in forwards if fw is not None]:
            cp.wait_send()
        for st in stores:
            st.wait()


def _run_hook(hook, ins, outs, scratch, step, n_steps):
    p = hook.plan(ins, outs, scratch)
    relay_step = min(max(1, (3 * n_steps) // 4), n_steps - 1)

    @pl.when(step == 0)
    def _():
        hook.start(p)

    if relay_step < n_steps - 1:
        @pl.when(step == relay_step)
        def _():
            hook.relay(p)

    @pl.when(step == n_steps - 1)
    def _():
        if relay_step == n_steps - 1:
            hook.relay(p)
        hook.finish(p)


def _hook_call(hook, *, name):
    n = len(hook.arrs)

    def body(*refs):
        p = hook.plan(refs[:n], refs[n:n + len(hook.out_shape)], refs[n + len(hook.out_shape):])
        hook.start(p)
        hook.relay(p)
        hook.finish(p)

    return pl.pallas_call(
        body, in_specs=[ANY] * n, out_specs=[ANY] * len(hook.out_shape), out_shape=hook.out_shape,
        scratch_shapes=hook.scratch, compiler_params=_comm_params(), name=name)(*hook.arrs)


def _send_other_half(parts, *, name):
    n = len(parts)

    def body(*refs):
        ins, outs = refs[:n], refs[n:2 * n]
        send_sems, recv_sems = refs[2 * n:]
        xi, yi, ci = _mesh_position()
        sibling = (xi, yi, 1 - ci)
        for i in range(n):
            for s in range(N_CHIPS):
                pltpu.make_async_remote_copy(src_ref=ins[i].at[s, 1 - ci], dst_ref=outs[i].at[s], send_sem=send_sems.at[i],
                                             recv_sem=recv_sems.at[i], device_id=sibling, device_id_type=MESH).start()
        for i in range(n):
            pltpu.make_async_remote_copy(src_ref=outs[i], dst_ref=outs[i], send_sem=send_sems.at[i], recv_sem=recv_sems.at[i],
                                         device_id=sibling, device_id_type=MESH).wait()

    return pl.pallas_call(
        body, in_specs=[ANY] * n, out_specs=[ANY] * n,
        out_shape=[jax.ShapeDtypeStruct((p.shape[0],) + p.shape[2:], p.dtype) for p in parts],
        scratch_shapes=[pltpu.SemaphoreType.DMA((n,)), pltpu.SemaphoreType.DMA((n,))],
        name=name)(*parts)


ROW_BLOCKS = 8


def _add_sibling_half(parts, theirs, core, *, name):
    n = len(parts)

    def body(core_ref, *refs):
        for a_ref, b_ref, o_ref in zip(refs[:n], refs[n:2 * n], refs[2 * n:]):
            o_ref[...] = (a_ref[...].astype(F32) + b_ref[...].astype(F32)).astype(o_ref.dtype)

    mine = lambda p: pl.BlockSpec((None, None, p.shape[2] // ROW_BLOCKS, p.shape[3]), lambda s, rb, core_ref: (s, core_ref[0], rb, 0))
    other = lambda p: pl.BlockSpec((None, p.shape[1] // ROW_BLOCKS, p.shape[2]), lambda s, rb, core_ref: (s, rb, 0))
    return pl.pallas_call(
        body,
        grid_spec=pltpu.PrefetchScalarGridSpec(
            num_scalar_prefetch=1, grid=(N_CHIPS, ROW_BLOCKS),
            in_specs=[mine(p) for p in parts] + [other(q) for q in theirs], out_specs=[other(q) for q in theirs]),
        out_shape=[jax.ShapeDtypeStruct(q.shape, BF16) for q in theirs],
        compiler_params=_params("parallel", "parallel"), name=name)(core, *parts, *theirs)


class _ExchangeHook:
    def __init__(self, parts, to_all=()):
        self.arrs = list(parts) + list(to_all)
        self.n_parts, self.n = len(parts), len(self.arrs)
        n_ici, n_peer = max((N_CHIPS - 1) * self.n_parts, 1), (N_DEV - 1) * max(len(to_all), 1)
        dma = pltpu.SemaphoreType.DMA
        self.out_shape = [jax.ShapeDtypeStruct(p.shape, p.dtype) for p in parts] + [
            jax.ShapeDtypeStruct((N_DEV,) + a.shape, a.dtype) for a in to_all]
        self.scratch = [pltpu.VMEM(p.shape[1:], p.dtype) for p in parts] + [pltpu.VMEM(a.shape, a.dtype) for a in to_all] + [
            dma((n_ici,)), dma((n_ici,)), dma((n_peer,)), dma((n_peer,)), dma((self.n,)), dma((self.n,))]

    def plan(self, ins, outs, scratch):
        n, npt = self.n, self.n_parts
        bufs = scratch[:n]
        send_sems, recv_sems, all_send, all_recv, load_sems, store_sems = scratch[n:]
        xi, yi, ci = _mesh_position()
        me_chip = 2 * xi + yi
        me = 4 * xi + 2 * yi + ci
        loads, stores = _staged_copies([ins[i].at[me_chip] for i in range(npt)] + list(ins[npt:]),
                                       [outs[i].at[me_chip] for i in range(npt)] + [outs[i].at[me] for i in range(npt, n)],
                                       bufs, load_sems, store_sems)
        sends, recvs = [], []
        for j, (bx, by) in enumerate(OTHER_CHIPS):
            px, py = _flip(xi, bx), _flip(yi, by)
            peer = 2 * px + py
            for i in range(npt):
                k = j * npt + i
                mk = functools.partial(pltpu.make_async_remote_copy, src_ref=ins[i].at[peer], send_sem=send_sems.at[k],
                                       recv_sem=recv_sems.at[k], device_id=(px, py, ci), device_id_type=MESH)
                sends.append(mk(dst_ref=outs[i].at[me_chip]))
                recvs.append(mk(dst_ref=outs[i].at[peer]))
        for i in range(npt, n):
            for k in range(1, N_DEV):
                px, py, pc = _flip(xi, (k >> 2) & 1), _flip(yi, (k >> 1) & 1), _flip(ci, k & 1)
                slot = (i - npt) * (N_DEV - 1) + k - 1
                mk = functools.partial(pltpu.make_async_remote_copy, src_ref=ins[i], send_sem=all_send.at[slot],
                                       recv_sem=all_recv.at[slot], device_id=(px, py, pc), device_id_type=MESH)
                sends.append(mk(dst_ref=outs[i].at[me]))
                recvs.append(mk(dst_ref=outs[i].at[4 * px + 2 * py + pc]))
        return loads, stores, sends, recvs

    @staticmethod
    def start(p):
        loads, _, sends, _ = p
        for cp in loads + sends:
            cp.start()

    @staticmethod
    def relay(p):
        loads, stores, _, _ = p
        for ld, st in zip(loads, stores):
            ld.wait()
            st.start()

    @staticmethod
    def finish(p):
        _, stores, sends, recvs = p
        for cp in recvs:
            cp.wait_recv()
        for cp in sends:
            cp.wait_send()
        for st in stores:
            st.wait()


def _sum_chips(parts, *, name):
    n = len(parts)
    p = parts[0].shape[0]

    def body(*refs):
        s = pl.program_id(1)
        for x_ref, o_ref in zip(refs[:n], refs[n:]):
            @pl.when(s == 0)
            def _():
                o_ref[...] = x_ref[...].astype(F32)

            @pl.when(s > 0)
            def _():
                o_ref[...] += x_ref[...].astype(F32)

    blocks = lambda q: ROW_BLOCKS if q.shape[1] % (8 * ROW_BLOCKS) == 0 else 1
    assert len({blocks(q) for q in parts}) == 1
    nb = blocks(parts[0])
    return pl.pallas_call(
        body, grid=(nb, p),
        in_specs=[pl.BlockSpec((None, q.shape[1] // nb, q.shape[2]), lambda rb, s: (s, rb, 0)) for q in parts],
        out_specs=[pl.BlockSpec((q.shape[1] // nb, q.shape[2]), lambda rb, s: (rb, 0)) for q in parts],
        out_shape=[jax.ShapeDtypeStruct(q.shape[1:], F32) for q in parts],
        compiler_params=_params("parallel", "arbitrary"), name=name)(*parts)


def _swap_halves(halves, layers, *, name):
    n = len(halves)
    out_shapes, slots = [], []
    for i, h in enumerate(halves):
        pair = [p for p in layers if i in p]
        if pair and pair[0][1] == i:
            slots.append((slots[pair[0][0]][0], 1))
        elif pair:
            out_shapes.append(jax.ShapeDtypeStruct((2, 2) + h.shape, h.dtype))
            slots.append((len(out_shapes) - 1, 0))
        else:
            out_shapes.append(jax.ShapeDtypeStruct((2,) + h.shape, h.dtype))
            slots.append((len(out_shapes) - 1, None))
    n_out = len(out_shapes)

    def body(*refs):
        ins, outs, bufs = refs[:n], refs[n:n + n_out], refs[n + n_out:2 * n + n_out]
        send_sems, recv_sems, load_sems, store_sems = refs[2 * n + n_out:]
        xi, yi, ci = _mesh_position()
        own, sends, recvs = [], [], []
        for i in range(n):
            o, layer = slots[i]
            dst = (lambda core: outs[o].at[core]) if layer is None else (lambda core: outs[o].at[layer, core])
            own.append(dst(ci))
            mk = functools.partial(pltpu.make_async_remote_copy, src_ref=ins[i], send_sem=send_sems.at[i],
                                   recv_sem=recv_sems.at[i], device_id=(xi, yi, 1 - ci), device_id_type=MESH)
            sends.append(mk(dst_ref=dst(ci)))
            recvs.append(mk(dst_ref=dst(1 - ci)))
        loads, stores = _staged_copies(ins, own, bufs, load_sems, store_sems)
        for cp in loads + sends:
            cp.start()
        for ld, st in zip(loads, stores):
            ld.wait()
            st.start()
        for cp in recvs:
            cp.wait_recv()
        for cp in sends:
            cp.wait_send()
        for st in stores:
            st.wait()

    return pl.pallas_call(
        body, in_specs=[ANY] * n, out_specs=[ANY] * n_out, out_shape=out_shapes,
        scratch_shapes=[pltpu.VMEM(h.shape, h.dtype) for h in halves]
        + [pltpu.SemaphoreType.DMA((n,)), pltpu.SemaphoreType.DMA((n,)), pltpu.SemaphoreType.DMA((n,)), pltpu.SemaphoreType.DMA((n,))],
        compiler_params=_comm_params(), name=name)(*halves)


def _full_weight(name, gathered):
    s, _, r, c = gathered.shape
    if name == "ssd_w_in":
        return _w_in_from_shards(gathered.reshape(s, 2 * r, c), name="ssd_w_in_unshard")
    if name in ("attn_w_qkv", "mlp_w_up0", "mlp_w_up1"):
        return gathered.reshape(s, 2 * r, c)
    return gathered.reshape(s * 2 * r, c)


class _StepComm:
    GATHER = {"in_proj": ("mlp_w_up0", "attn_w_qkv"), "conv": ("mlp_w_down0", "attn_w_o"),
              "scan": ("ssd_w_out", "mlp_w_up1", "mlp_w_down1")}
    EXCHANGE = {"early": ("ssd_w_out", "attn_w_qkv", "attn_w_o", "mlp_w_up0", "mlp_w_up1", "mlp_w_down0", "mlp_w_down1"),
                "late": ("ssd_w_in",)}

    def __init__(self, shards, core):
        self.shards, self.core = shards, core
        self.chip_parts = {}
        self._pending = None

    def gather_hook(self, stage):
        return _GatherHook([self.shards[n] for n in self.GATHER[stage]])

    def weights_from(self, stage, gathered):
        return {n: _full_weight(n, g) for n, g in zip(self.GATHER[stage], gathered)}

    def chip_sums(self, mats, tag):
        parts = [_shard_halves(a) for a in mats.values()]
        theirs = _send_other_half(parts, name=f"grad_sibling_send_{tag}")
        return _add_sibling_half(parts, theirs, self.core, name=f"grad_chip_sum_{tag}")

    def exchange_hook(self, mats, which):
        self._pending = self.EXCHANGE[which]
        return _ExchangeHook(self.chip_sums({n: mats[n] for n in self._pending}, which))

    def received(self, arrays):
        self.chip_parts.update(zip(self._pending, arrays))


def _adamw(w, g, m, v, *, name):
    r, c = w.shape
    tr = 256 if r % 256 == 0 else r
    blk = pl.BlockSpec((tr, c), lambda i: (i, 0))

    def body(w_ref, g_ref, m_ref, v_ref, d_ref, nm_ref, nv_ref):
        gv = g_ref[...]
        nm = ADAM_B1 * m_ref[...] + (1.0 - ADAM_B1) * gv
        nv = ADAM_B2 * v_ref[...] + (1.0 - ADAM_B2) * (gv * gv)
        m_hat = nm / (1.0 - ADAM_B1 ** ADAM_STEP)
        v_hat = nv / (1.0 - ADAM_B2 ** ADAM_STEP)
        d_ref[...] = -ADAM_LR * (m_hat / (jnp.sqrt(v_hat) + ADAM_EPS) + ADAM_WD * w_ref[...])
        nm_ref[...] = nm
        nv_ref[...] = nv

    sh = jax.ShapeDtypeStruct((r, c), F32)
    return pl.pallas_call(body, grid=(r // tr,), in_specs=[blk] * 4, out_specs=[blk] * 3, out_shape=[sh] * 3,
                          compiler_params=_params("parallel"), name=name)(w, g, m, v)


SM_CONV_B, SM_NORM_W, SM_MIX_PRE, SM_MIX_POST, SM_FFN_PRE, SM_FFN_POST, SM_MISC, SM_CONV_W, SM_B_QKV, SM_B_O = 0, 4, 6, 8, 10, 12, 14, 16, 32, 34
SM_ROWS = 40
MISC_DT_BIAS, MISC_A_LOG, MISC_D, MISC_SINKS, MISC_LOSS = 0, 32, 64, 96, 112


def _shard_halves(a):
    c = a.shape[-1]
    return a.reshape(N_CHIPS, 2, -1, c)


def _rows(v):
    return v.reshape(-1, D_MODEL)


def _misc_row(dt_bias, a_log, d, sinks, loss):
    pad = jnp.zeros((D_MODEL - MISC_LOSS - 1,), F32)
    return jnp.concatenate([dt_bias.reshape(-1), a_log.reshape(-1), d.reshape(-1), sinks.reshape(-1), loss.reshape(1), pad]).reshape(1, D_MODEL)


def _replicated_rows(p, loss):
    return jnp.concatenate([
        _rows(p["ssd_conv_b"]), _rows(p["ssd_norm_w"]), _rows(p["mix_pre_norm"]), _rows(p["mix_post_norm"]),
        _rows(p["ffn_pre_norm"]), _rows(p["ffn_post_norm"]),
        _misc_row(p["ssd_dt_bias"], p["ssd_a_log"], p["ssd_d"], p["attn_sinks"], loss), jnp.zeros((1, D_MODEL), F32)], axis=0)


def _sharded_rows(conv_w, b_qkv, b_o):
    last = jnp.concatenate([b_qkv.reshape(-1), b_o.reshape(-1), jnp.zeros((D_MODEL - 640,), F32)]).reshape(1, D_MODEL)
    return jnp.concatenate([conv_w.reshape(SSD_CONV_WIDTH, D_MODEL), last, jnp.zeros((3, D_MODEL), F32)], axis=0)


REPLICATED = ("ssd_conv_b", "ssd_dt_bias", "ssd_a_log", "ssd_d", "ssd_norm_w", "attn_sinks",
              "mix_pre_norm", "mix_post_norm", "ffn_pre_norm", "ffn_post_norm")
MATRICES = ("ssd_w_in", "ssd_w_out", "attn_w_qkv", "attn_w_o", "mlp_w_up", "mlp_w_down")
WEIGHT_NAMES = ("ssd_w_in", "ssd_conv_w", "ssd_conv_b", "ssd_dt_bias", "ssd_a_log", "ssd_d", "ssd_norm_w", "ssd_w_out",
                "attn_w_qkv", "attn_b_qkv", "attn_sinks", "attn_w_o", "attn_b_o", "mlp_w_up", "mlp_w_down",
                "mix_pre_norm", "mix_post_norm", "ffn_pre_norm", "ffn_post_norm")


def _unpack_small(rows16, rows8, like):
    misc = rows16[SM_MISC]
    out = {
        "ssd_conv_b": rows16[SM_CONV_B:SM_CONV_B + 4], "ssd_norm_w": rows16[SM_NORM_W:SM_NORM_W + 2],
        "mix_pre_norm": rows16[SM_MIX_PRE:SM_MIX_PRE + 2], "mix_post_norm": rows16[SM_MIX_POST:SM_MIX_POST + 2],
        "ffn_pre_norm": rows16[SM_FFN_PRE:SM_FFN_PRE + 2], "ffn_post_norm": rows16[SM_FFN_POST:SM_FFN_POST + 2],
        "ssd_dt_bias": misc[MISC_DT_BIAS:MISC_DT_BIAS + 32], "ssd_a_log": misc[MISC_A_LOG:MISC_A_LOG + 32],
        "ssd_d": misc[MISC_D:MISC_D + 32], "attn_sinks": misc[MISC_SINKS:MISC_SINKS + 16],
        "ssd_conv_w": rows8[0:SSD_CONV_WIDTH], "attn_b_qkv": rows8[SSD_CONV_WIDTH, 0:384], "attn_b_o": rows8[SSD_CONV_WIDTH, 384:640],
    }
    return {k: v.reshape(like[k].shape) for k, v in out.items()}


def kernel(x, ssd_w_in, ssd_conv_w, ssd_conv_b, ssd_dt_bias, ssd_a_log, ssd_d, ssd_norm_w, ssd_w_out, attn_w_qkv, attn_b_qkv, attn_sinks, attn_w_o, attn_b_o, mlp_w_up, mlp_w_down, mix_pre_norm, mix_post_norm, ffn_pre_norm, ffn_post_norm, loss_target, m_ssd_w_in, m_ssd_conv_w, m_ssd_conv_b, m_ssd_dt_bias, m_ssd_a_log, m_ssd_d, m_ssd_norm_w, m_ssd_w_out, m_attn_w_qkv, m_attn_b_qkv, m_attn_sinks, m_attn_w_o, m_attn_b_o, m_mlp_w_up, m_mlp_w_down, m_mix_pre_norm, m_mix_post_norm, m_ffn_pre_norm, m_ffn_post_norm, v_ssd_w_in, v_ssd_conv_w, v_ssd_conv_b, v_ssd_dt_bias, v_ssd_a_log, v_ssd_d, v_ssd_norm_w, v_ssd_w_out, v_attn_w_qkv, v_attn_b_qkv, v_attn_sinks, v_attn_w_o, v_attn_b_o, v_mlp_w_up, v_mlp_w_down, v_mix_pre_norm, v_mix_post_norm, v_ffn_pre_norm, v_ffn_post_norm):
    w = dict(zip(WEIGHT_NAMES, (ssd_w_in, ssd_conv_w, ssd_conv_b, ssd_dt_bias, ssd_a_log, ssd_d, ssd_norm_w, ssd_w_out, attn_w_qkv, attn_b_qkv, attn_sinks, attn_w_o, attn_b_o, mlp_w_up, mlp_w_down, mix_pre_norm, mix_post_norm, ffn_pre_norm, ffn_post_norm)))
    m = dict(zip(WEIGHT_NAMES, (m_ssd_w_in, m_ssd_conv_w, m_ssd_conv_b, m_ssd_dt_bias, m_ssd_a_log, m_ssd_d, m_ssd_norm_w, m_ssd_w_out, m_attn_w_qkv, m_attn_b_qkv, m_attn_sinks, m_attn_w_o, m_attn_b_o, m_mlp_w_up, m_mlp_w_down, m_mix_pre_norm, m_mix_post_norm, m_ffn_pre_norm, m_ffn_post_norm)))
    v = dict(zip(WEIGHT_NAMES, (v_ssd_w_in, v_ssd_conv_w, v_ssd_conv_b, v_ssd_dt_bias, v_ssd_a_log, v_ssd_d, v_ssd_norm_w, v_ssd_w_out, v_attn_w_qkv, v_attn_b_qkv, v_attn_sinks, v_attn_w_o, v_attn_b_o, v_mlp_w_up, v_mlp_w_down, v_mix_pre_norm, v_mix_post_norm, v_ffn_pre_norm, v_ffn_post_norm)))
    chip = 2 * lax.axis_index("x") + lax.axis_index("y")

    two_halves = lambda a: a.astype(BF16).reshape(2, a.shape[0] // 2, a.shape[1])
    shards = {"ssd_w_out": w["ssd_w_out"][0], "attn_w_qkv": w["attn_w_qkv"][0], "attn_w_o": w["attn_w_o"][0],
              "mlp_w_up0": w["mlp_w_up"][0], "mlp_w_up1": w["mlp_w_up"][1],
              "mlp_w_down0": w["mlp_w_down"][0], "mlp_w_down1": w["mlp_w_down"][1]}
    core = lax.axis_index("c").astype(jnp.int32).reshape(1)
    comm = _StepComm({k: two_halves(a) for k, a in shards.items()}, core)
    g_in, g_conv, g_bqkv, g_bo = _hook_call(
        _GatherHook([two_halves(w["ssd_w_in"][0])], [w["ssd_conv_w"][0], w["attn_b_qkv"], w["attn_b_o"]]), name="weight_all_gather")
    full = {
        "ssd_w_in": _full_weight("ssd_w_in", g_in),
        "ssd_conv_w": g_conv.transpose(1, 0, 2).reshape(SSD_CONV_WIDTH, SSD_CONV_DIM),
        "attn_b_qkv": g_bqkv.reshape(ATTN_QKV), "attn_b_o": g_bo.reshape(D_MODEL),
    }
    for name in REPLICATED:
        full[name] = w[name][0] if name.startswith(("ssd_", "attn_")) else w[name]

    loss_tile, grad_x, gm, g = _local_step(x[0], loss_target[0], full, comm)

    conv_w_rows = g["ssd_conv_w"].reshape(SSD_CONV_WIDTH * N_CHIPS, D_MODEL)
    b_qkv_rows = jnp.pad(g["attn_b_qkv"], (0, 2 * D_MODEL - ATTN_QKV)).reshape(2, D_MODEL)
    small = jnp.concatenate([_replicated_rows(g, loss_tile[0, 0]), conv_w_rows, b_qkv_rows, _rows(g["attn_b_o"]),
                             jnp.zeros((SM_ROWS - SM_B_O - 1, D_MODEL), F32)], axis=0)
    small_all, = _hook_call(_ExchangeHook([], [small]), name="vector_grad_all_gather")
    order = ("ssd_w_in", "ssd_w_out", "attn_w_qkv", "attn_w_o", "mlp_w_up0", "mlp_w_up1", "mlp_w_down0", "mlp_w_down1")
    halves = _sum_chips([comm.chip_parts[k] for k in order], name="grad_sum")
    r_in, r_out, r_qkv, r_o, r_up, r_down = _swap_halves(halves, layers=((4, 5), (6, 7)), name="grad_halves_swap")
    small_sum, = _sum_chips([small_all], name="small_grad_sum")

    grads = {"ssd_w_in": r_in, "ssd_w_out": r_out, "attn_w_qkv": r_qkv, "attn_w_o": r_o, "mlp_w_up": r_up, "mlp_w_down": r_down}
    grads = {k: a.reshape(w[k].shape) for k, a in grads.items()}
    conv_w_g = lax.dynamic_index_in_dim(small_sum[SM_CONV_W:SM_CONV_W + 16].reshape(SSD_CONV_WIDTH, N_CHIPS, D_MODEL), chip, axis=1, keepdims=False)
    b_qkv_g = lax.dynamic_slice_in_dim(small_sum[SM_B_QKV:SM_B_QKV + 2].reshape(-1), chip * 384, 384)
    b_o_g = lax.dynamic_slice_in_dim(small_sum[SM_B_O], chip * 256, 256)
    small_g = jnp.concatenate([small_sum[0:16], _sharded_rows(conv_w_g, b_qkv_g, b_o_g)], axis=0)
    grads.update(_unpack_small(small_g[0:16], small_g[16:24], w))
    loss = small_sum[SM_MISC, MISC_LOSS]

    delta, new_m, new_v = {}, {}, {}
    for name in MATRICES:
        shape = w[name].shape
        as2d = lambda a: a.reshape(-1, shape[-1])
        d2, m2, v2 = _adamw(as2d(w[name]), as2d(grads[name]), as2d(m[name]), as2d(v[name]), name=f"adamw_{name}")
        delta[name], new_m[name], new_v[name] = d2.reshape(shape), m2.reshape(shape), v2.reshape(shape)
    zero = jnp.zeros((), F32)
    small_pack = lambda p: jnp.concatenate([_replicated_rows({k: p[k] for k in REPLICATED}, zero),
                                            _sharded_rows(p["ssd_conv_w"], p["attn_b_qkv"], p["attn_b_o"])], axis=0)
    d_s, m_s, v_s = _adamw(small_pack(w), small_g, small_pack(m), small_pack(v), name="adamw_vectors")
    delta.update(_unpack_small(d_s[0:16], d_s[16:24], w))
    new_m.update(_unpack_small(m_s[0:16], m_s[16:24], w))
    new_v.update(_unpack_small(v_s[0:16], v_s[16:24], w))

    return (loss, grad_x[None], *[grads[n] for n in WEIGHT_NAMES], *[delta[n] for n in WEIGHT_NAMES],
            *[new_m[n] for n in WEIGHT_NAMES], *[new_v[n] for n in WEIGHT_NAMES])
```

```python
import functools
import math

import jax
import jax.numpy as jnp
from jax import lax
from jax.experimental import pallas as pl
from jax.experimental.pallas import tpu as pltpu

F32 = jnp.float32
BF16 = jnp.bfloat16

D_MODEL = 1024
SSD_D_INNER = 2048
SSD_HEAD_DIM = 64
SSD_N_HEADS = 32
SSD_N_GROUPS = 8
SSD_HPG = 4
SSD_D_STATE = 128
SSD_CONV_WIDTH = 4
SSD_CHUNK = 128
SSD_CONV_DIM = 4096
SSD_IN_DIM = 6176
SSD_IN_PAD = 6272
SSD_GW = SSD_HPG * SSD_HEAD_DIM
ATTN_HEAD_DIM = 64
ATTN_N_Q = 16
ATTN_N_KV = 4
ATTN_REP = 4
ATTN_WINDOW = 128
ATTN_QKV = 1536
D_FF = 4096
NORM_EPS = 1e-6

ADAM_LR = 0.001
ADAM_B1 = 0.9
ADAM_B2 = 0.999
ADAM_EPS = 1e-08
ADAM_WD = 0.01
ADAM_STEP = 10

N_CHIPS = 4
N_DEV = 8
LANES = 128
VMEM_LIMIT = 48 * 1024 * 1024

MESH = pl.DeviceIdType.MESH


def _params(*sem):
    return pltpu.CompilerParams(dimension_semantics=sem, vmem_limit_bytes=VMEM_LIMIT)


def _dot(a, b, dims):
    return lax.dot_general(a, b, (dims, ((), ())), preferred_element_type=F32)


def _dot_nn(a, b):
    return _dot(a, b, ((1,), (0,)))


def _dot_nt(a, b):
    return _dot(a, b, ((1,), (1,)))


def _dot_tn(a, b):
    return _dot(a, b, ((0,), (0,)))


def _sigmoid(x):
    return 0.5 * jnp.tanh(0.5 * x) + 0.5


ANY = pl.BlockSpec(memory_space=pl.ANY)


class _HookSlots:
    def __init__(self, hook, n_in, n_out, n_scratch):
        self.hook = hook
        self.n_in, self.n_out, self.n_scratch = n_in, n_out, n_scratch
        self.inputs = list(hook.arrs) if hook else []
        self.out_shape = list(hook.out_shape) if hook else []
        self.scratch = list(hook.scratch) if hook else []
        self.in_specs = [ANY] * len(self.inputs)
        self.out_specs = [ANY] * len(self.out_shape)

    def _split(self, refs):
        a = self.n_in
        b = a + len(self.inputs)
        c = b + self.n_out
        d = c + len(self.out_shape)
        e = d + self.n_scratch
        return refs[:a], refs[a:b], refs[b:c], refs[c:d], refs[d:e], refs[e:]

    def own(self, refs):
        ins, _, outs, _, scratch, _ = self._split(refs)
        return ins, outs, scratch

    def run(self, refs, step, n_steps):
        _, h_in, _, h_out, _, h_scratch = self._split(refs)
        _run_hook(self.hook, h_in, h_out, h_scratch, step, n_steps)

    def semantics(self, *sem):
        return sem if self.hook is None else ("arbitrary",) * len(sem)


def _matmul(a, b, *, mode, out_dtypes, name, epilogue=None, extras=(), tm=1024, tn=1024, tk=1024,
            b_shards=False, out_shards=False, hook=None):
    if b_shards:
        s, b_rows, b_cols = b.shape
        b2 = (b_rows, s * b_cols)
        if mode == "nn":
            tn = b_cols
        else:
            assert mode == "nt"
            tk = b_cols
    else:
        b2 = b.shape
    if mode == "nn":
        (m, k), (k2, n) = a.shape, b2
    elif mode == "nt":
        (m, k), (n, k2) = a.shape, b2
    else:
        (k, m), (k2, n) = a.shape, b2
    assert k == k2, (a.shape, b.shape, mode)
    tm, tn, tk = min(tm, m), min(tn, n), min(tk, k)
    assert m % tm == 0 and n % tn == 0 and k % tk == 0, (m, n, k, tm, tn, tk)
    nk = k // tk
    if mode == "tn":
        a_spec = pl.BlockSpec((tk, tm), lambda i, j, kk: (kk, i))
    else:
        a_spec = pl.BlockSpec((tm, tk), lambda i, j, kk: (i, kk))
    if b_shards and mode == "nn":
        b_spec = pl.BlockSpec((None, tk, tn), lambda i, j, kk: (j, kk, 0))
    elif b_shards:
        b_spec = pl.BlockSpec((None, tn, tk), lambda i, j, kk: (kk, j, 0))
    elif mode == "nt":
        b_spec = pl.BlockSpec((tn, tk), lambda i, j, kk: (j, kk))
    else:
        b_spec = pl.BlockSpec((tk, tn), lambda i, j, kk: (kk, j))
    dims = {"nn": ((1,), (0,)), "nt": ((1,), (1,)), "tn": ((0,), (0,))}[mode]
    ex_specs = []
    for arr, kind in extras:
        if kind == "tile":
            ex_specs.append(pl.BlockSpec((tm, tn), lambda i, j, kk: (i, j)))
        else:
            ex_specs.append(pl.BlockSpec((1, tn), lambda i, j, kk: (0, j)))
    n_ex, n_out = len(extras), len(out_dtypes)
    if epilogue is None:
        epilogue = lambda acc: (acc,)
    hk = _HookSlots(hook, n_in=2 + n_ex, n_out=n_out, n_scratch=0 if nk == 1 else 1)
    grid = (m // tm, n // tn, nk)

    def body(*refs):
        (a_ref, b_ref, *ex), outs, scratch = hk.own(refs)
        if hook is not None:
            step = (pl.program_id(0) * grid[1] + pl.program_id(1)) * grid[2] + pl.program_id(2)
            hk.run(refs, step, grid[0] * grid[1] * grid[2])

        def finish(acc):
            res = epilogue(acc, *[e[...] for e in ex])
            for o, r in zip(outs, res):
                o[...] = r.astype(o.dtype)

        if nk == 1:
            finish(_dot(a_ref[...], b_ref[...], dims))
        else:
            acc_ref = scratch[0]
            kk = pl.program_id(2)

            @pl.when(kk == 0)
            def _():
                acc_ref[...] = jnp.zeros_like(acc_ref)

            acc_ref[...] += _dot(a_ref[...], b_ref[...], dims)

            @pl.when(kk == nk - 1)
            def _():
                finish(acc_ref[...])

    if out_shards:
        out_spec = pl.BlockSpec((None, tm, tn), lambda i, j, kk: (j, i, 0))
        out_dims = (n // tn, m, tn)
    else:
        out_spec = pl.BlockSpec((tm, tn), lambda i, j, kk: (i, j))
        out_dims = (m, n)
    outs = pl.pallas_call(
        body,
        grid=grid,
        in_specs=[a_spec, b_spec] + ex_specs + hk.in_specs,
        out_specs=[out_spec for _ in out_dtypes] + hk.out_specs,
        out_shape=[jax.ShapeDtypeStruct(out_dims, dt) for dt in out_dtypes] + hk.out_shape,
        scratch_shapes=([] if nk == 1 else [pltpu.VMEM((tm, tn), F32)]) + hk.scratch,
        compiler_params=_params(*hk.semantics("parallel", "parallel", "arbitrary")),
        name=name,
    )(a, b, *[arr for arr, _ in extras], *hk.inputs)
    own = outs[0] if n_out == 1 else outs[:n_out]
    return own if hook is None else (own, outs[n_out:])


def _row_tile(t, want):
    return min(t, want)


def _rms_fwd(x, w, *, name, resid=None, want_u=None):
    t, d = x.shape
    tr = _row_tile(t, 512)

    def norm(v, wv):
        return v * lax.rsqrt(jnp.mean(v * v, axis=-1, keepdims=True) + NORM_EPS) * wv

    row = pl.BlockSpec((tr, d), lambda i: (i, 0))
    vec = pl.BlockSpec((1, d), lambda i: (0, 0))
    if resid is None:
        def body(x_ref, w_ref, o_ref):
            o_ref[...] = norm(x_ref[...], w_ref[...]).astype(BF16)
        ins, in_specs = (x, w), [row, vec]
        out_shape, out_specs = jax.ShapeDtypeStruct((t, d), BF16), row
    elif want_u is None:
        def body(x_ref, w_ref, r_ref, o_ref):
            o_ref[...] = r_ref[...] + norm(x_ref[...], w_ref[...])
        ins, in_specs = (x, w, resid), [row, vec, row]
        out_shape, out_specs = jax.ShapeDtypeStruct((t, d), F32), row
    else:
        def body(x_ref, w_ref, r_ref, w2_ref, o_ref, u_ref):
            h = r_ref[...] + norm(x_ref[...], w_ref[...])
            o_ref[...] = h
            u_ref[...] = norm(h, w2_ref[...]).astype(BF16)
        ins, in_specs = (x, w, resid, want_u), [row, vec, row, vec]
        out_shape = [jax.ShapeDtypeStruct((t, d), F32), jax.ShapeDtypeStruct((t, d), BF16)]
        out_specs = [row, row]
    return pl.pallas_call(body, grid=(t // tr,), in_specs=in_specs, out_specs=out_specs, out_shape=out_shape,
                          compiler_params=_params("parallel"), name=name)(*ins)


def _rms_bwd(x, w, dy, *, name, resid=None, out_dtype=F32):
    t, d = x.shape
    tr = _row_tile(t, 512)
    row = pl.BlockSpec((tr, d), lambda i: (i, 0))
    vec = pl.BlockSpec((1, d), lambda i: (0, 0))
    has_res = resid is not None

    def body(x_ref, w_ref, dy_ref, *rest):
        if has_res:
            r_ref, dx_ref, dw_ref = rest
        else:
            dx_ref, dw_ref = rest
        xv = x_ref[...]
        dyv = dy_ref[...].astype(F32)
        r = lax.rsqrt(jnp.mean(xv * xv, axis=-1, keepdims=True) + NORM_EPS)
        xhat = xv * r
        dyw = dyv * w_ref[...]
        dx = r * (dyw - xhat * jnp.mean(dyw * xhat, axis=-1, keepdims=True))
        if has_res:
            dx = dx + r_ref[...]
        dx_ref[...] = dx.astype(dx_ref.dtype)

        @pl.when(pl.program_id(0) == 0)
        def _():
            dw_ref[...] = jnp.zeros_like(dw_ref)

        dw_ref[...] += jnp.sum(dyv * xhat, axis=0, keepdims=True)

    ins = (x, w, dy) + ((resid,) if has_res else ())
    in_specs = [row, vec, row] + ([row] if has_res else [])
    return pl.pallas_call(
        body, grid=(t // tr,), in_specs=in_specs, out_specs=[row, vec],
        out_shape=[jax.ShapeDtypeStruct((t, d), out_dtype), jax.ShapeDtypeStruct((1, d), F32)],
        compiler_params=_params("arbitrary"), name=name)(*ins)


def _loss_head(h, target, *, name):
    t, d = h.shape
    tr = _row_tile(t, 512)
    row = pl.BlockSpec((tr, d), lambda i: (i, 0))

    def body(h_ref, t_ref, dh_ref, loss_ref):
        err = h_ref[...] - t_ref[...]
        dh_ref[...] = err * (1.0 / d)

        @pl.when(pl.program_id(0) == 0)
        def _():
            loss_ref[...] = jnp.zeros_like(loss_ref)

        part = jnp.sum(jnp.sum(err * err, axis=1, keepdims=True), axis=0, keepdims=True) * (0.5 / d)
        loss_ref[...] += jnp.broadcast_to(part, loss_ref.shape)

    return pl.pallas_call(
        body, grid=(t // tr,), in_specs=[row, row],
        out_specs=[row, pl.BlockSpec((8, LANES), lambda i: (0, 0))],
        out_shape=[jax.ShapeDtypeStruct((t, d), F32), jax.ShapeDtypeStruct((8, LANES), F32)],
        compiler_params=_params("arbitrary"), name=name)(h, target)


def _col_sum(x, *, name):
    t, n = x.shape
    tr = _row_tile(t, 512)

    def body(x_ref, o_ref):
        @pl.when(pl.program_id(0) == 0)
        def _():
            o_ref[...] = jnp.zeros_like(o_ref)

        o_ref[...] += jnp.sum(x_ref[...].astype(F32), axis=0, keepdims=True)

    return pl.pallas_call(
        body, grid=(t // tr,), in_specs=[pl.BlockSpec((tr, n), lambda i: (i, 0))],
        out_specs=pl.BlockSpec((1, n), lambda i: (0, 0)), out_shape=jax.ShapeDtypeStruct((1, n), F32),
        compiler_params=_params("arbitrary"), name=name)(x)


SSD_IN_SHARD = SSD_IN_DIM // N_CHIPS


def _w_in_from_shards(shards, *, name):
    d = shards.shape[1]
    tr = 256

    def body(s_ref, o_ref):
        o_ref[:, pl.ds(SSD_IN_PAD - LANES, LANES)] = jnp.zeros((tr, LANES), o_ref.dtype)
        for s in range(N_CHIPS):
            o_ref[:, pl.ds(SSD_IN_SHARD * s, SSD_IN_SHARD)] = s_ref[s]

    return pl.pallas_call(
        body, grid=(d // tr,), in_specs=[pl.BlockSpec((N_CHIPS, tr, SSD_IN_SHARD), lambda i: (0, i, 0))],
        out_specs=pl.BlockSpec((tr, SSD_IN_PAD), lambda i: (i, 0)),
        out_shape=jax.ShapeDtypeStruct((d, SSD_IN_PAD), shards.dtype),
        compiler_params=_params("parallel"), name=name)(shards)


def _w_in_to_shards(g, *, name):
    d = g.shape[0]
    tr = 256

    def body(g_ref, o_ref):
        for s in range(N_CHIPS):
            o_ref[s] = g_ref[:, pl.ds(SSD_IN_SHARD * s, SSD_IN_SHARD)].astype(o_ref.dtype)

    return pl.pallas_call(
        body, grid=(d // tr,), in_specs=[pl.BlockSpec((tr, SSD_IN_PAD), lambda i: (i, 0))],
        out_specs=pl.BlockSpec((N_CHIPS, tr, SSD_IN_SHARD), lambda i: (0, i, 0)),
        out_shape=jax.ShapeDtypeStruct((N_CHIPS, d, SSD_IN_SHARD), BF16),
        compiler_params=_params("parallel"), name=name)(g)


XBC_COL0 = SSD_D_INNER // LANES
DT_COL0 = (SSD_D_INNER + SSD_CONV_DIM) // LANES


def _shift_down(v, k, row_ids):
    return jnp.where(row_ids >= k, pltpu.roll(v, k, axis=0), 0.0)


def _shift_up(v, k, row_ids):
    n = v.shape[0]
    return jnp.where(row_ids < n - k, pltpu.roll(v, n - k, axis=0), 0.0)


def _conv_pre(x, w, b, row_ids):
    pre = b + w[3:4, :] * x
    for k in (1, 2, 3):
        pre = pre + w[3 - k:4 - k, :] * _shift_down(x, k, row_ids)
    return pre


def _conv_fwd(zx, conv_w, conv_b, *, name, hook=None):
    t = zx.shape[0]
    nct = SSD_CONV_DIM // LANES
    hk = _HookSlots(hook, n_in=3, n_out=1, n_scratch=0)

    def body(*refs):
        (x_ref, w_ref, b_ref), (o_ref,), _ = hk.own(refs)
        if hook is not None:
            hk.run(refs, pl.program_id(0), nct)
        x = x_ref[...]
        row_ids = lax.broadcasted_iota(jnp.int32, x.shape, 0)
        pre = _conv_pre(x, w_ref[...], b_ref[...], row_ids)
        o_ref[...] = pre * _sigmoid(pre)

    outs = pl.pallas_call(
        body, grid=(nct,),
        in_specs=[pl.BlockSpec((t, LANES), lambda j: (0, XBC_COL0 + j)),
                  pl.BlockSpec((SSD_CONV_WIDTH, LANES), lambda j: (0, j)),
                  pl.BlockSpec((1, LANES), lambda j: (0, j))] + hk.in_specs,
        out_specs=[pl.BlockSpec((t, LANES), lambda j: (0, j))] + hk.out_specs,
        out_shape=[jax.ShapeDtypeStruct((t, SSD_CONV_DIM), F32)] + hk.out_shape,
        scratch_shapes=hk.scratch,
        compiler_params=_params(*hk.semantics("parallel")), name=name)(zx, conv_w, conv_b, *hk.inputs)
    return outs[0] if hook is None else (outs[0], outs[1:])


def _conv_bwd(zx, conv_w, conv_b, d_xs, d_bm, d_cm, *, name):
    t = zx.shape[0]
    nct = SSD_CONV_DIM // LANES
    n_xs = SSD_D_INNER // LANES
    n_bm = SSD_N_GROUPS * SSD_D_STATE // LANES

    def body(x_ref, w_ref, b_ref, dxs_ref, dbm_ref, dcm_ref, dx_ref, dw_ref, db_ref):
        x = x_ref[...]
        w = w_ref[...]
        j = pl.program_id(0)
        dy = jnp.where(j < n_xs, dxs_ref[...], jnp.where(j < n_xs + n_bm, dbm_ref[...], dcm_ref[...]))
        row_ids = lax.broadcasted_iota(jnp.int32, x.shape, 0)
        pre = _conv_pre(x, w, b_ref[...], row_ids)
        sg = _sigmoid(pre)
        dpre = dy * (sg * (1.0 + pre * (1.0 - sg)))
        dx = w[3:4, :] * dpre
        for k in (1, 2, 3):
            dx = dx + w[3 - k:4 - k, :] * _shift_up(dpre, k, row_ids)
        dx_ref[...] = dx.astype(dx_ref.dtype)
        db_ref[...] = jnp.sum(dpre, axis=0, keepdims=True)
        dw_ref[3:4, :] = jnp.sum(dpre * x, axis=0, keepdims=True)
        for k in (1, 2, 3):
            dw_ref[3 - k:4 - k, :] = jnp.sum(dpre * _shift_down(x, k, row_ids), axis=0, keepdims=True)

    col = pl.BlockSpec((t, LANES), lambda j: (0, j))
    clip = lambda j, lo, n: jnp.clip(j - lo, 0, n - 1)
    return pl.pallas_call(
        body, grid=(nct,),
        in_specs=[pl.BlockSpec((t, LANES), lambda j: (0, XBC_COL0 + j)),
                  pl.BlockSpec((SSD_CONV_WIDTH, LANES), lambda j: (0, j)),
                  pl.BlockSpec((1, LANES), lambda j: (0, j)),
                  pl.BlockSpec((t, LANES), lambda j: (0, clip(j, 0, n_xs))),
                  pl.BlockSpec((t, LANES), lambda j: (0, clip(j, n_xs, n_bm))),
                  pl.BlockSpec((t, LANES), lambda j: (0, clip(j, n_xs + n_bm, n_bm)))],
        out_specs=[col, pl.BlockSpec((SSD_CONV_WIDTH, LANES), lambda j: (0, j)), pl.BlockSpec((1, LANES), lambda j: (0, j))],
        out_shape=[jax.ShapeDtypeStruct((t, SSD_CONV_DIM), BF16),
                   jax.ShapeDtypeStruct((SSD_CONV_WIDTH, SSD_CONV_DIM), F32),
                   jax.ShapeDtypeStruct((1, SSD_CONV_DIM), F32)],
        compiler_params=_params("parallel"), name=name)(zx, conv_w, conv_b, d_xs, d_bm, d_cm)


def _softplus_fwd(zx, bias_row, alog_row, *, name):
    t = zx.shape[0]
    q = SSD_CHUNK
    tr = _row_tile(t, 1024)

    def body(x_ref, b_ref, al_ref, dt_ref, cum_ref):
        v = x_ref[...] + b_ref[...]
        e = jnp.exp(-jnp.abs(v))
        u = 1.0 + e
        log1p = jnp.where(u == 1.0, e, jnp.log(u) * (e / (u - 1.0)))
        dt = jnp.maximum(v, 0.0) + log1p
        dt_ref[...] = dt
        a = dt * -jnp.exp(al_ref[...])
        lower = (lax.broadcasted_iota(jnp.int32, (q, q), 1) <= lax.broadcasted_iota(jnp.int32, (q, q), 0)).astype(F32)
        cums = [lax.dot_general(lower, a[c * q:(c + 1) * q, :], ((((1,), (0,))), ((), ())), precision=lax.Precision.HIGHEST,
                                preferred_element_type=F32) for c in range(tr // q)]
        cum_ref[...] = jnp.concatenate(cums, axis=0)

    blk = pl.BlockSpec((tr, LANES), lambda i: (i, 0))
    vec = pl.BlockSpec((1, LANES), lambda i: (0, 0))
    return pl.pallas_call(
        body, grid=(t // tr,),
        in_specs=[pl.BlockSpec((tr, LANES), lambda i: (i, DT_COL0)), vec, vec],
        out_specs=[blk, blk],
        out_shape=[jax.ShapeDtypeStruct((t, LANES), F32), jax.ShapeDtypeStruct((t, LANES), F32)],
        compiler_params=_params("parallel"), name=name)(zx, bias_row, alog_row)


def _softplus_bwd(zx, bias_row, ddt, *, name):
    t = zx.shape[0]
    tr = _row_tile(t, 1024)

    def body(x_ref, b_ref, g_ref, o_ref, db_ref):
        v = x_ref[...] + b_ref[...]
        lane = lax.broadcasted_iota(jnp.int32, v.shape, 1)
        d = jnp.where(lane < SSD_N_HEADS, g_ref[...] * _sigmoid(v), 0.0)
        o_ref[...] = d.astype(o_ref.dtype)

        @pl.when(pl.program_id(0) == 0)
        def _():
            db_ref[...] = jnp.zeros_like(db_ref)

        db_ref[...] += jnp.sum(d, axis=0, keepdims=True)

    return pl.pallas_call(
        body, grid=(t // tr,),
        in_specs=[pl.BlockSpec((tr, LANES), lambda i: (i, DT_COL0)), pl.BlockSpec((1, LANES), lambda i: (0, 0)),
                  pl.BlockSpec((tr, LANES), lambda i: (i, 0))],
        out_specs=[pl.BlockSpec((tr, LANES), lambda i: (i, 0)), pl.BlockSpec((1, LANES), lambda i: (0, 0))],
        out_shape=[jax.ShapeDtypeStruct((t, LANES), BF16), jax.ShapeDtypeStruct((1, LANES), F32)],
        compiler_params=_params("arbitrary"), name=name)(zx, bias_row, ddt)


def _ssd_masks():
    q = SSD_CHUNK
    tt = lax.broadcasted_iota(jnp.int32, (q, q), 0)
    ss = lax.broadcasted_iota(jnp.int32, (q, q), 1)
    lane = lax.broadcasted_iota(jnp.int32, (1, SSD_GW), 1)
    srow = lax.broadcasted_iota(jnp.int32, (SSD_GW, 1), 0)
    hm = [(lane >= SSD_HEAD_DIM * j) & (lane < SSD_HEAD_DIM * (j + 1)) for j in range(SSD_HPG)]
    rm = [(srow >= SSD_HEAD_DIM * j) & (srow < SSD_HEAD_DIM * (j + 1)) for j in range(SSD_HPG)]
    return tt, ss, hm, rm


def _ssd_head_terms(dt_rows, cum_rows, a_rows, j, tt, ss):
    q = SSD_CHUNK
    dt_row = dt_rows[j:j + 1, :]
    dt_col = jnp.sum(jnp.where(tt == ss, dt_row, 0.0), axis=1, keepdims=True)
    a_row1 = a_rows[j:j + 1, :]
    a_11 = a_rows[j:j + 1, 0:1]
    cum_col = jnp.sum(jnp.where(ss <= tt, dt_row * a_row1, 0.0), axis=1, keepdims=True)
    cum_row = cum_rows[j:j + 1, :]
    decay = jnp.exp(jnp.where(ss <= tt, cum_col - cum_row, -jnp.inf))
    cum_last = cum_col[q - 1:q, :]
    e_col = jnp.exp(cum_col)
    dte_col = jnp.exp(cum_last - cum_col)
    e_last = jnp.exp(cum_last)
    return dt_col, dt_row, a_row1, a_11, decay, e_col, dte_col, e_last


SSD_CHUNKS_PER_STEP = 4
SSD_BC_COL0 = SSD_D_INNER // SSD_D_STATE


def _ssd_head_selects(terms, hm, rm):
    e_all = jnp.zeros((SSD_CHUNK, SSD_GW), F32)
    w_all = jnp.zeros((SSD_CHUNK, SSD_GW), F32)
    e_s = jnp.zeros((SSD_GW, 1), F32)
    for j in range(SSD_HPG):
        dt_col, _, _, _, _, e_col, dte_col, e_last = terms[j]
        e_all = jnp.where(hm[j], e_col, e_all)
        w_all = jnp.where(hm[j], dt_col * dte_col, w_all)
        e_s = jnp.where(rm[j], e_last, e_s)
    return e_all, w_all, e_s


def _ssd_fwd(xc, dtr, cumr, alog_b, d_b, *, name, hook=None):
    t = xc.shape[0]
    q = SSD_CHUNK
    nc = t // q
    kc = min(SSD_CHUNKS_PER_STEP, nc)
    rows = kc * q
    hk = _HookSlots(hook, n_in=7, n_out=2, n_scratch=1)

    def body(*refs):
        (x_ref, b_ref, c_ref, dtr_ref, cumr_ref, alog_ref, d_ref), (y_ref, st_ref), (s_scr,) = hk.own(refs)
        if hook is not None:
            hk.run(refs, pl.program_id(0) * (nc // kc) + pl.program_id(1), SSD_N_GROUPS * (nc // kc))

        @pl.when(pl.program_id(1) == 0)
        def _():
            s_scr[...] = jnp.zeros_like(s_scr)

        tt, ss, hm, rm = _ssd_masks()
        a_rows = -jnp.exp(alog_ref[...])
        d_rows = d_ref[...]
        d_all = jnp.zeros((1, SSD_GW), F32)
        for j in range(SSD_HPG):
            d_all = jnp.where(hm[j], d_rows[j:j + 1, 0:1], d_all)
        ks, hs = range(kc), range(SSD_HPG)
        sl = [pl.ds(k * q, q) for k in ks]
        x = [x_ref[sl[k], :] for k in ks]
        bm = [b_ref[sl[k], :].astype(BF16) for k in ks]
        cm = [c_ref[sl[k], :].astype(BF16) for k in ks]
        xb = [x[k].astype(BF16) for k in ks]
        terms = [[_ssd_head_terms(dtr_ref[:, sl[k]], cumr_ref[:, sl[k]], a_rows, j, tt, ss) for j in hs] for k in ks]
        g = [_dot_nt(cm[k], bm[k]) for k in ks]
        m = [[(g[k] * terms[k][j][4] * terms[k][j][1]).astype(BF16) for j in hs] for k in ks]
        yj = [[_dot_nn(m[k][j], xb[k]) for j in hs] for k in ks]
        sel = [_ssd_head_selects(terms[k], hm, rm) for k in ks]
        upd = [_dot_tn((x[k] * sel[k][1]).astype(BF16), bm[k]) for k in ks]
        states = [s_scr[...]]
        for k in ks:
            states.append(states[k] * sel[k][2] + upd[k])
        inter = [_dot_nt(cm[k], states[k].astype(BF16)) for k in ks]
        ys = []
        for k in ks:
            y = jnp.zeros((q, SSD_GW), F32)
            for j in hs:
                y = jnp.where(hm[j], yj[k][j], y)
            ys.append(y + inter[k] * sel[k][0] + x[k] * d_all)
        for k in ks:
            st_ref[k] = states[k]
        y_ref[...] = jnp.concatenate(ys, axis=0)
        s_scr[...] = states[kc]

    blk = lambda width, off: pl.BlockSpec((rows, width), lambda g, c: (c, off + g))
    par_s = pl.BlockSpec((None, SSD_HPG, LANES), lambda g, c: (g, 0, 0))
    row_s = pl.BlockSpec((None, SSD_HPG, rows), lambda g, c: (g, 0, c))
    outs = pl.pallas_call(
        body, grid=(SSD_N_GROUPS, nc // kc),
        in_specs=[blk(SSD_GW, 0), blk(SSD_D_STATE, SSD_BC_COL0), blk(SSD_D_STATE, SSD_BC_COL0 + SSD_N_GROUPS),
                  row_s, row_s, par_s, par_s] + hk.in_specs,
        out_specs=[blk(SSD_GW, 0), pl.BlockSpec((None, kc, SSD_GW, SSD_D_STATE), lambda g, c: (g, c, 0, 0))] + hk.out_specs,
        out_shape=[jax.ShapeDtypeStruct((t, SSD_D_INNER), F32),
                   jax.ShapeDtypeStruct((SSD_N_GROUPS, nc, SSD_GW, SSD_D_STATE), F32)] + hk.out_shape,
        scratch_shapes=[pltpu.VMEM((SSD_GW, SSD_D_STATE), F32)] + hk.scratch,
        compiler_params=_params(*hk.semantics("parallel", "arbitrary")), name=name)(
            xc, xc, xc, dtr, cumr, alog_b, d_b, *hk.inputs)
    return outs if hook is None else (outs[:2], outs[2:])


def _ssd_bwd(xc, dtr, cumr, alog_b, d_b, states, dy, *, name, hook=None):
    t = xc.shape[0]
    q = SSD_CHUNK
    nc = t // q
    kc = min(SSD_CHUNKS_PER_STEP, nc)
    nst = nc // kc
    rows = kc * q
    rev = lambda c: nst - 1 - c
    hk = _HookSlots(hook, n_in=9, n_out=5, n_scratch=1)

    def body(*refs):
        ((x_ref, b_ref, c_ref, dtr_ref, cumr_ref, alog_ref, d_ref, st_ref, dy_ref),
         (dx_ref, db_ref, dc_ref, ddt_ref, dpar_ref), (ds_scr,)) = hk.own(refs)
        if hook is not None:
            hk.run(refs, pl.program_id(0) * nst + pl.program_id(1), SSD_N_GROUPS * nst)

        @pl.when(pl.program_id(1) == 0)
        def _():
            ds_scr[...] = jnp.zeros_like(ds_scr)
            dpar_ref[...] = jnp.zeros_like(dpar_ref)

        tt, ss, hm, rm = _ssd_masks()
        tcol = lax.broadcasted_iota(jnp.int32, (q, 1), 0)
        lane = lax.broadcasted_iota(jnp.int32, (1, LANES), 1)
        a_rows = -jnp.exp(alog_ref[...])
        d_rows = d_ref[...]
        d_all = jnp.zeros((1, SSD_GW), F32)
        for j in range(SSD_HPG):
            d_all = jnp.where(hm[j], d_rows[j:j + 1, 0:1], d_all)
        ks, hs = range(kc), range(SSD_HPG)
        sl = [pl.ds(k * q, q) for k in ks]
        x = [x_ref[sl[k], :] for k in ks]
        dyv = [dy_ref[sl[k], :] for k in ks]
        bm = [b_ref[sl[k], :].astype(BF16) for k in ks]
        cm = [c_ref[sl[k], :].astype(BF16) for k in ks]
        s_in = [st_ref[k] for k in ks]
        xb = [x[k].astype(BF16) for k in ks]
        dyb = [dyv[k].astype(BF16) for k in ks]
        s_b = [s_in[k].astype(BF16) for k in ks]
        terms = [[_ssd_head_terms(dtr_ref[:, sl[k]], cumr_ref[:, sl[k]], a_rows, j, tt, ss) for j in hs] for k in ks]
        sel = [_ssd_head_selects(terms[k], hm, rm) for k in ks]
        e_all, w_all, e_s = [s_[0] for s_ in sel], [s_[1] for s_ in sel], [s_[2] for s_ in sel]
        dye = [(dyv[k] * e_all[k]).astype(BF16) for k in ks]
        ds_loc = [_dot_tn(dye[k], cm[k]) for k in ks]
        ds = [None] * kc
        running = ds_scr[...]
        for k in reversed(ks):
            ds[k] = running
            running = running * e_s[k] + ds_loc[k]
        ds_scr[...] = running
        ds_b = [ds[k].astype(BF16) for k in ks]
        g = [_dot_nt(cm[k], bm[k]) for k in ks]
        cs = [_dot_nt(cm[k], s_b[k]) for k in ks]
        bds = [_dot_nt(bm[k], ds_b[k]) for k in ks]
        dm = [[_dot_nt(jnp.where(hm[j], dyv[k], 0.0).astype(BF16), xb[k]) for j in hs] for k in ks]
        gl = [[g[k] * terms[k][j][4] for j in hs] for k in ks]
        wp = [[dm[k][j] * gl[k][j] for j in hs] for k in ks]
        mt = [[(gl[k][j] * terms[k][j][1]).astype(BF16) for j in hs] for k in ks]
        dxj = [[_dot_tn(mt[k][j], dyb[k]) for j in hs] for k in ks]
        dg = []
        for k in ks:
            acc = jnp.zeros((q, q), F32)
            for j in hs:
                acc = acc + dm[k][j] * terms[k][j][4] * terms[k][j][1]
            dg.append(acc.astype(BF16))
        dy_cs = [dyv[k] * cs[k] for k in ks]
        x_bds = [x[k] * bds[k] for k in ks]
        dy_x = [dyv[k] * x[k] for k in ks]
        ds_s = [ds[k] * s_in[k] for k in ks]
        w = [[wp[k][j] * terms[k][j][1] for j in hs] for k in ks]
        rw_col = [[jnp.sum(w[k][j], axis=1, keepdims=True) for j in hs] for k in ks]
        cw_row = [[jnp.sum(w[k][j], axis=0, keepdims=True) for j in hs] for k in ks]
        cwp_row = [[jnp.sum(wp[k][j], axis=0, keepdims=True) for j in hs] for k in ks]
        r1_col = [[jnp.sum(jnp.where(hm[j], dy_cs[k], 0.0), axis=1, keepdims=True) * terms[k][j][5] for j in hs] for k in ks]
        dw_col = [[jnp.sum(jnp.where(hm[j], x_bds[k], 0.0), axis=1, keepdims=True) for j in hs] for k in ks]
        head_rows = [slice(j * SSD_HEAD_DIM, (j + 1) * SSD_HEAD_DIM) for j in hs]
        lane_sum = lambda v: jnp.sum(v, axis=1, keepdims=True)
        s_sum = [[lane_sum(jnp.sum(ds_s[k][head_rows[j], :], axis=0, keepdims=True)) for j in hs] for k in ks]
        dy_x_cols = [jnp.sum(dy_x[k], axis=0, keepdims=True) for k in ks]
        d_d = [[lane_sum(jnp.where(hm[j], dy_x_cols[k], 0.0)) for j in hs] for k in ks]
        ddt_rows = [[None] * SSD_HPG for _ in ks]
        dpar = [jnp.zeros((1, LANES), F32) for _ in hs]
        for k in ks:
            for j in hs:
                dt_col, dt_row, a_row1, a_11, _, _, dte_col, e_last = terms[k][j]
                dww = dw_col[k][j] * (dt_col * dte_col)
                last_add = jnp.sum(dww, axis=0, keepdims=True) + e_last * s_sum[k][j]
                dcum_col = rw_col[k][j] + r1_col[k][j] - dww + jnp.where(tcol == q - 1, last_add, 0.0)
                da_row = jnp.sum(jnp.where(tt >= ss, dcum_col, 0.0), axis=0, keepdims=True)
                da_col = jnp.sum(jnp.where(ss >= tt, -cw_row[k][j], 0.0), axis=1, keepdims=True)
                ddt_col = a_11 * da_col + dw_col[k][j] * dte_col
                ddt_rows[k][j] = (a_row1 * da_row + cwp_row[k][j]
                                  + jnp.sum(jnp.where(tt == ss, ddt_col, 0.0), axis=0, keepdims=True))
                d_a = jnp.sum(dt_row * da_row, axis=1, keepdims=True) + jnp.sum(dt_col * da_col, axis=0, keepdims=True)
                dpar[j] = dpar[j] + jnp.where(lane == 0, d_a * a_11, 0.0) + jnp.where(lane == 1, d_d[k][j], 0.0)
        dxs = []
        for k in ks:
            acc = jnp.zeros((q, SSD_GW), F32)
            for j in hs:
                acc = jnp.where(hm[j], dxj[k][j], acc)
            dxs.append(acc + w_all[k] * bds[k] + d_all * dyv[k])
        xw = [(x[k] * w_all[k]).astype(BF16) for k in ks]
        dc = [_dot_nn(dg[k], bm[k]) + _dot_nn(dye[k], s_b[k]) for k in ks]
        db = [_dot_tn(dg[k], cm[k]) + _dot_nn(xw[k], ds_b[k]) for k in ks]
        dx_ref[...] = jnp.concatenate(dxs, axis=0)
        dc_ref[...] = jnp.concatenate(dc, axis=0)
        db_ref[...] = jnp.concatenate(db, axis=0)
        ddt_ref[...] = jnp.concatenate([jnp.concatenate([ddt_rows[k][j] for k in ks], axis=1) for j in hs], axis=0)
        dpar_ref[...] += jnp.concatenate(dpar, axis=0)

    blk = lambda width, off: pl.BlockSpec((rows, width), lambda g, c: (rev(c), off + g))
    par_s = pl.BlockSpec((None, SSD_HPG, LANES), lambda g, c: (g, 0, 0))
    outs = pl.pallas_call(
        body, grid=(SSD_N_GROUPS, nst),
        in_specs=[blk(SSD_GW, 0), blk(SSD_D_STATE, SSD_BC_COL0), blk(SSD_D_STATE, SSD_BC_COL0 + SSD_N_GROUPS),
                  pl.BlockSpec((None, SSD_HPG, rows), lambda g, c: (g, 0, rev(c))),
                  pl.BlockSpec((None, SSD_HPG, rows), lambda g, c: (g, 0, rev(c))), par_s, par_s,
                  pl.BlockSpec((None, kc, SSD_GW, SSD_D_STATE), lambda g, c: (g, rev(c), 0, 0)), blk(SSD_GW, 0)] + hk.in_specs,
        out_specs=[blk(SSD_GW, 0), blk(SSD_D_STATE, 0), blk(SSD_D_STATE, 0),
                   pl.BlockSpec((None, SSD_HPG, rows), lambda g, c: (g, 0, rev(c))), par_s] + hk.out_specs,
        out_shape=[jax.ShapeDtypeStruct((t, SSD_D_INNER), F32),
                   jax.ShapeDtypeStruct((t, SSD_N_GROUPS * SSD_D_STATE), F32),
                   jax.ShapeDtypeStruct((t, SSD_N_GROUPS * SSD_D_STATE), F32),
                   jax.ShapeDtypeStruct((SSD_N_GROUPS, SSD_HPG, t), F32),
                   jax.ShapeDtypeStruct((SSD_N_GROUPS, SSD_HPG, LANES), F32)] + hk.out_shape,
        scratch_shapes=[pltpu.VMEM((SSD_GW, SSD_D_STATE), F32)] + hk.scratch,
        compiler_params=_params(*hk.semantics("parallel", "arbitrary")), name=name)(
            xc, xc, xc, dtr, cumr, alog_b, d_b, states, dy, *hk.inputs)
    return outs if hook is None else (outs[:5], outs[5:])


def _gate_norm_fwd(y, zx, norm_w, *, name):
    t = y.shape[0]
    tr = _row_tile(t, 256)
    row = pl.BlockSpec((tr, SSD_D_INNER), lambda i: (i, 0))

    def body(y_ref, z_ref, w_ref, o_ref):
        for gi in range(SSD_N_GROUPS):
            sl = pl.ds(gi * SSD_GW, SSD_GW)
            z = z_ref[:, sl]
            gv = y_ref[:, sl] * (z * _sigmoid(z))
            r = lax.rsqrt(jnp.mean(gv * gv, axis=-1, keepdims=True) + NORM_EPS)
            o_ref[:, sl] = (gv * r * w_ref[:, sl]).astype(BF16)

    return pl.pallas_call(
        body, grid=(t // tr,), in_specs=[row, row, pl.BlockSpec((1, SSD_D_INNER), lambda i: (0, 0))],
        out_specs=row, out_shape=jax.ShapeDtypeStruct((t, SSD_D_INNER), BF16),
        compiler_params=_params("parallel"), name=name)(y, zx, norm_w)


def _gate_norm_bwd(y, zx, norm_w, dyn, *, name):
    t = y.shape[0]
    tr = _row_tile(t, 256)
    row = pl.BlockSpec((tr, SSD_D_INNER), lambda i: (i, 0))
    vec = pl.BlockSpec((1, SSD_D_INNER), lambda i: (0, 0))

    def body(y_ref, z_ref, w_ref, dyn_ref, dy_ref, dz_ref, dw_ref):
        @pl.when(pl.program_id(0) == 0)
        def _():
            dw_ref[...] = jnp.zeros_like(dw_ref)

        for gi in range(SSD_N_GROUPS):
            sl = pl.ds(gi * SSD_GW, SSD_GW)
            z = z_ref[:, sl]
            yv = y_ref[:, sl]
            sg = _sigmoid(z)
            sz = z * sg
            gv = yv * sz
            r = lax.rsqrt(jnp.mean(gv * gv, axis=-1, keepdims=True) + NORM_EPS)
            ghat = gv * r
            dout = dyn_ref[:, sl].astype(F32)
            dgh = dout * w_ref[:, sl]
            dgv = r * (dgh - ghat * jnp.mean(dgh * ghat, axis=-1, keepdims=True))
            dy_ref[:, sl] = dgv * sz
            dz_ref[:, sl] = (dgv * yv * (sg * (1.0 + z * (1.0 - sg)))).astype(dz_ref.dtype)
            dw_ref[:, sl] += jnp.sum(dout * ghat, axis=0, keepdims=True)

    return pl.pallas_call(
        body, grid=(t // tr,), in_specs=[row, row, vec, row], out_specs=[row, row, vec],
        out_shape=[jax.ShapeDtypeStruct((t, SSD_D_INNER), F32), jax.ShapeDtypeStruct((t, SSD_D_INNER), BF16),
                   jax.ShapeDtypeStruct((1, SSD_D_INNER), F32)],
        compiler_params=_params("arbitrary"), name=name)(y, zx, norm_w, dyn)


ATTN_KV_W = ATTN_N_KV * ATTN_HEAD_DIM
ATTN_Q_HALF = 512
ATTN_K_BLK = ATTN_N_Q * ATTN_HEAD_DIM // ATTN_KV_W
ATTN_V_BLK = ATTN_K_BLK + 1


def _attn_valid(first_block):
    w = ATTN_WINDOW
    qpos = lax.broadcasted_iota(jnp.int32, (w, 2 * w), 0) + w
    kpos = lax.broadcasted_iota(jnp.int32, (w, 2 * w), 1)
    rel = qpos - kpos
    return (rel >= 0) & (rel < w) & jnp.logical_not(first_block & (kpos < w))


def _attn_block_views(lo_ref, hi_ref, kc_ref, kp_ref, vc_ref, vp_ref):
    hd = ATTN_HEAD_DIM
    per_half = ATTN_Q_HALF // hd
    heads = [(lo_ref if h < per_half else hi_ref)[:, pl.ds((h % per_half) * hd, hd)] for h in range(ATTN_N_Q)]
    kv_cols = [pl.ds(kh * hd, hd) for kh in range(ATTN_N_KV)]
    kb = [jnp.concatenate([kp_ref[:, c], kc_ref[:, c]], axis=0) for c in kv_cols]
    vb = [jnp.concatenate([vp_ref[:, c], vc_ref[:, c]], axis=0) for c in kv_cols]
    return heads, kb, vb


def _attn_softmax(q, kb, sink, valid):
    heads = range(ATTN_N_Q)
    scale = ATTN_HEAD_DIM ** -0.5
    s = [jnp.where(valid, _dot_nt(q[h], kb[h // ATTN_REP]) * scale, -jnp.inf) for h in heads]
    m = [jnp.maximum(jnp.max(s[h], axis=1, keepdims=True), sink[h]) for h in heads]
    e = [jnp.exp(s[h] - m[h]) for h in heads]
    es = [jnp.exp(sink[h] - m[h]) for h in heads]
    inv = [1.0 / (jnp.sum(e[h], axis=1, keepdims=True) + es[h]) for h in heads]
    return e, es, inv


def _attn_fwd(qkv, sinks_b, *, name):
    t = qkv.shape[0]
    w = ATTN_WINDOW
    nb = t // w
    prev = lambda n: jnp.maximum(n - 1, 0)

    def body(qlo_ref, qhi_ref, kc_ref, kp_ref, vc_ref, vp_ref, sink_ref, o_ref):
        heads = range(ATTN_N_Q)
        q, kb, vb = _attn_block_views(qlo_ref, qhi_ref, kc_ref, kp_ref, vc_ref, vp_ref)
        sink = [sink_ref[h:h + 1, 0:1] for h in heads]
        e, _, inv = _attn_softmax(q, kb, sink, _attn_valid(pl.program_id(0) == 0))
        out = [_dot_nn((e[h] * inv[h]).astype(BF16), vb[h // ATTN_REP]).astype(o_ref.dtype) for h in heads]
        o_ref[...] = jnp.concatenate(out, axis=1)

    qh = lambda half: pl.BlockSpec((w, ATTN_Q_HALF), lambda n: (n, half))
    kv = lambda blk, idx: pl.BlockSpec((w, ATTN_KV_W), lambda n: (idx(n), blk))
    cur = lambda n: n
    return pl.pallas_call(
        body, grid=(nb,),
        in_specs=[qh(0), qh(1), kv(ATTN_K_BLK, cur), kv(ATTN_K_BLK, prev), kv(ATTN_V_BLK, cur), kv(ATTN_V_BLK, prev),
                  pl.BlockSpec((ATTN_N_Q, LANES), lambda n: (0, 0))],
        out_specs=pl.BlockSpec((w, D_MODEL), lambda n: (n, 0)),
        out_shape=jax.ShapeDtypeStruct((t, D_MODEL), BF16),
        compiler_params=_params("parallel"), name=name)(qkv, qkv, qkv, qkv, qkv, qkv, sinks_b)


def _attn_bwd(qkv, sinks_b, dout, *, name):
    t = qkv.shape[0]
    w = ATTN_WINDOW
    nb = t // w
    hd = ATTN_HEAD_DIM
    clamp = lambda n: jnp.minimum(n, nb - 1)
    prev = lambda n: jnp.maximum(clamp(n) - 1, 0)

    def body(qlo_ref, qhi_ref, kc_ref, kp_ref, vc_ref, vp_ref, sink_ref, dolo_ref, dohi_ref,
             dq_ref, dkv_ref, dsink_ref, carry):
        n = pl.program_id(0)

        @pl.when(n == 0)
        def _():
            carry[...] = jnp.zeros_like(carry)
            dsink_ref[...] = jnp.zeros_like(dsink_ref)

        @pl.when(n < nb)
        def _():
            heads, kvs = range(ATTN_N_Q), range(ATTN_N_KV)
            q, kb, vb = _attn_block_views(qlo_ref, qhi_ref, kc_ref, kp_ref, vc_ref, vp_ref)
            do, _, _ = _attn_block_views(dolo_ref, dohi_ref, kc_ref, kp_ref, vc_ref, vp_ref)
            sink = [sink_ref[h:h + 1, 0:1] for h in heads]
            e, es, inv = _attn_softmax(q, kb, sink, _attn_valid(n == 0))
            dp = [_dot_nt(do[h], vb[h // ATTN_REP]) for h in heads]
            p = [e[h] * inv[h] for h in heads]
            delta = [jnp.sum(p[h] * dp[h], axis=1, keepdims=True) for h in heads]
            dsc = [(p[h] * (dp[h] - delta[h]) * (hd ** -0.5)).astype(BF16) for h in heads]
            pb = [p[h].astype(BF16) for h in heads]
            dq = [_dot_nn(dsc[h], kb[h // ATTN_REP]).astype(dq_ref.dtype) for h in heads]
            stack = lambda per_head, kh: jnp.concatenate(per_head[kh * ATTN_REP:(kh + 1) * ATTN_REP], axis=0)
            dkb = [_dot_tn(stack(dsc, kh), stack(q, kh)) for kh in kvs]
            dvb = [_dot_tn(stack(pb, kh), stack(do, kh)) for kh in kvs]
            dsink = [jnp.broadcast_to(jnp.sum(-es[h] * inv[h] * delta[h], axis=0, keepdims=True), (1, LANES)) for h in heads]
            dq_ref[...] = jnp.concatenate(dq, axis=1)
            dsink_ref[...] += jnp.concatenate(dsink, axis=0)
            dkv_ref[...] = (carry[...] + jnp.concatenate([d[0:w, :] for d in dkb + dvb], axis=1)).astype(dkv_ref.dtype)
            carry[...] = jnp.concatenate([d[w:2 * w, :] for d in dkb + dvb], axis=1)

        @pl.when(n == nb)
        def _():
            dkv_ref[...] = carry[...].astype(dkv_ref.dtype)

    qh = lambda half: pl.BlockSpec((w, ATTN_Q_HALF), lambda n: (clamp(n), half))
    kv = lambda blk, idx: pl.BlockSpec((w, ATTN_KV_W), lambda n: (idx(n), blk))
    return pl.pallas_call(
        body, grid=(nb + 1,),
        in_specs=[qh(0), qh(1), kv(ATTN_K_BLK, clamp), kv(ATTN_K_BLK, prev), kv(ATTN_V_BLK, clamp), kv(ATTN_V_BLK, prev),
                  pl.BlockSpec((ATTN_N_Q, LANES), lambda n: (0, 0)), qh(0), qh(1)],
        out_specs=[pl.BlockSpec((w, D_MODEL), lambda n: (clamp(n), 0)),
                   pl.BlockSpec((w, 2 * ATTN_KV_W), lambda n: (jnp.maximum(n - 1, 0), 0)),
                   pl.BlockSpec((ATTN_N_Q, LANES), lambda n: (0, 0))],
        out_shape=[jax.ShapeDtypeStruct((t, D_MODEL), BF16), jax.ShapeDtypeStruct((t, 2 * ATTN_KV_W), BF16),
                   jax.ShapeDtypeStruct((ATTN_N_Q, LANES), F32)],
        scratch_shapes=[pltpu.VMEM((w, 2 * ATTN_KV_W), F32)],
        compiler_params=_params("arbitrary"), name=name)(qkv, qkv, qkv, qkv, qkv, qkv, sinks_b, dout, dout)


def _sq_relu_epilogue(acc):
    r = jnp.maximum(acc, 0.0)
    return (r * r,)


def _sq_relu_bwd_epilogue(acc, act):
    return (acc * (2.0 * jnp.sqrt(act.astype(F32))),)


def _bias_epilogue(acc, bias):
    return (acc + bias,)


def _plain_run(stage, fn, *args, **kwargs):
    return fn(*args, **kwargs)


def _mlp_fwd(u, w_up, w_down, tag, run=_plain_run):
    act = run(f"mlp_up_{tag}", _matmul, u, w_up, mode="nn", out_dtypes=(BF16,), epilogue=_sq_relu_epilogue, b_shards=True,
              name=f"mlp_up_{tag}")
    f = run(f"mlp_down_{tag}", _matmul, act, w_down, mode="nn", out_dtypes=(F32,), name=f"mlp_down_{tag}")
    return act, f


def _mlp_bwd(u, act, w_up, w_down, df, tag):
    dpre = _matmul(df, w_down, mode="nt", out_dtypes=(BF16,), epilogue=_sq_relu_bwd_epilogue,
                   extras=((act, "tile"),), name=f"mlp_dact_{tag}")
    dw_down = _matmul(act, df, mode="tn", out_dtypes=(BF16,), name=f"mlp_dwdown_{tag}")
    du = _matmul(dpre, w_up, mode="nt", out_dtypes=(F32,), b_shards=True, name=f"mlp_du_{tag}")
    dw_up = _matmul(u, dpre, mode="tn", out_dtypes=(BF16,), out_shards=True, name=f"mlp_dwup_{tag}")
    return du, dw_up, dw_down


def _group_rows(dt):
    t = dt.shape[0]
    return jnp.transpose(dt[:, :SSD_N_HEADS].reshape(t, SSD_N_GROUPS, SSD_HPG), (1, 2, 0))


def _head_param_rows(p):
    return jnp.broadcast_to(p.reshape(SSD_N_GROUPS, SSD_HPG, 1), (SSD_N_GROUPS, SSD_HPG, LANES))


def _local_step(x, target, wts, comm=None):
    t = x.shape[0]
    wts = dict(wts)
    row = lambda v: v.reshape(1, -1)
    mix_pre, mix_post, ffn_pre, ffn_post = wts["mix_pre_norm"], wts["mix_post_norm"], wts["ffn_pre_norm"], wts["ffn_post_norm"]

    def gathering(stage, fn, *args, **kwargs):
        hook = comm.gather_hook(stage) if comm is not None else None
        if hook is None:
            return fn(*args, **kwargs)
        out, got = fn(*args, hook=hook, **kwargs)
        wts.update(comm.weights_from(stage, got))
        return out

    u0 = _rms_fwd(x, row(mix_pre[0]), name="rms_pre_mix0")
    zx = gathering("in_proj", _matmul, u0, wts["ssd_w_in"], mode="nn", out_dtypes=(F32,), tn=896, name="ssd_in_proj")
    xc = gathering("conv", _conv_fwd, zx, wts["ssd_conv_w"], row(wts["ssd_conv_b"]), name="ssd_conv_fwd")
    bias_row = jnp.pad(wts["ssd_dt_bias"], (0, LANES - SSD_N_HEADS)).reshape(1, LANES)
    alog_row = jnp.pad(wts["ssd_a_log"], (0, LANES - SSD_N_HEADS)).reshape(1, LANES)
    dt, cum = _softplus_fwd(zx, bias_row, alog_row, name="ssd_dt_fwd")
    dtr, cumr = _group_rows(dt), _group_rows(cum)
    alog_b, d_b = _head_param_rows(wts["ssd_a_log"]), _head_param_rows(wts["ssd_d"])
    y_ssd, states = gathering("scan", _ssd_fwd, xc, dtr, cumr, alog_b, d_b, name="ssd_scan_fwd")
    norm_w = row(wts["ssd_norm_w"])
    yn = _gate_norm_fwd(y_ssd, zx, norm_w, name="ssd_gate_norm_fwd")
    mix0 = _matmul(yn, wts["ssd_w_out"], mode="nn", out_dtypes=(F32,), name="ssd_out_proj")
    h1, v0 = _rms_fwd(mix0, row(mix_post[0]), resid=x, want_u=row(ffn_pre[0]), name="rms_post_mix0")
    act0, f0 = _mlp_fwd(v0, wts["mlp_w_up0"], wts["mlp_w_down0"], "l0", run=gathering)
    h2, u1 = _rms_fwd(f0, row(ffn_post[0]), resid=h1, want_u=row(mix_pre[1]), name="rms_post_ffn0")

    qkv = _matmul(u1, wts["attn_w_qkv"], mode="nn", out_dtypes=(BF16,), epilogue=_bias_epilogue,
                  extras=((row(wts["attn_b_qkv"]), "row"),), b_shards=True, name="attn_qkv_proj")
    sinks_b = jnp.broadcast_to(wts["attn_sinks"].reshape(ATTN_N_Q, 1), (ATTN_N_Q, LANES))
    ao = _attn_fwd(qkv, sinks_b, name="attn_fwd")
    mix1 = _matmul(ao, wts["attn_w_o"], mode="nn", out_dtypes=(F32,), epilogue=_bias_epilogue,
                   extras=((row(wts["attn_b_o"]), "row"),), name="attn_out_proj")
    h3, v1 = _rms_fwd(mix1, row(mix_post[1]), resid=h2, want_u=row(ffn_pre[1]), name="rms_post_mix1")
    act1, f1 = _mlp_fwd(v1, wts["mlp_w_up1"], wts["mlp_w_down1"], "l1")
    h4 = _rms_fwd(f1, row(ffn_post[1]), resid=h3, name="rms_post_ffn1")

    dh4, loss_tile = _loss_head(h4, target, name="loss_head")

    df1, g_ffn_post1 = _rms_bwd(f1, row(ffn_post[1]), dh4, out_dtype=BF16, name="rms_post_ffn1_bwd")
    dv1, g_up1, g_down1 = _mlp_bwd(v1, act1, wts["mlp_w_up1"], wts["mlp_w_down1"], df1, "l1")
    dh3, g_ffn_pre1 = _rms_bwd(h3, row(ffn_pre[1]), dv1, resid=dh4, name="rms_pre_ffn1_bwd")
    dmix1, g_mix_post1 = _rms_bwd(mix1, row(mix_post[1]), dh3, out_dtype=BF16, name="rms_post_mix1_bwd")
    g_b_o = _col_sum(dmix1, name="attn_bo_grad")
    g_w_o = _matmul(ao, dmix1, mode="tn", out_dtypes=(BF16,), name="attn_dwo")
    dao = _matmul(dmix1, wts["attn_w_o"], mode="nt", out_dtypes=(BF16,), name="attn_dao")
    dq, dkv, g_sinks = _attn_bwd(qkv, sinks_b, dao, name="attn_bwd")
    dqkv = jnp.concatenate([dq, dkv], axis=1)
    g_b_qkv = _col_sum(dqkv, name="attn_bqkv_grad")
    g_w_qkv = _matmul(u1, dqkv, mode="tn", out_dtypes=(BF16,), tn=ATTN_QKV // N_CHIPS, out_shards=True, name="attn_dwqkv")
    du1 = _matmul(dqkv, wts["attn_w_qkv"], mode="nt", out_dtypes=(F32,), b_shards=True, name="attn_du")
    dh2, g_mix_pre1 = _rms_bwd(h2, row(mix_pre[1]), du1, resid=dh3, name="rms_pre_mix1_bwd")

    df0, g_ffn_post0 = _rms_bwd(f0, row(ffn_post[0]), dh2, out_dtype=BF16, name="rms_post_ffn0_bwd")
    dv0, g_up0, g_down0 = _mlp_bwd(v0, act0, wts["mlp_w_up0"], wts["mlp_w_down0"], df0, "l0")
    dh1, g_ffn_pre0 = _rms_bwd(h1, row(ffn_pre[0]), dv0, resid=dh2, name="rms_pre_ffn0_bwd")
    dmix0, g_mix_post0 = _rms_bwd(mix0, row(mix_post[0]), dh1, out_dtype=BF16, name="rms_post_mix0_bwd")
    g_w_out = _matmul(yn, dmix0, mode="tn", out_dtypes=(BF16,), name="ssd_dwout")
    dyn = _matmul(dmix0, wts["ssd_w_out"], mode="nt", out_dtypes=(BF16,), name="ssd_dyn")
    dy_ssd, dz, g_norm_w = _gate_norm_bwd(y_ssd, zx, norm_w, dyn, name="ssd_gate_norm_bwd")
    mats = {"ssd_w_out": g_w_out, "attn_w_qkv": g_w_qkv, "attn_w_o": g_w_o,
            "mlp_w_up0": g_up0, "mlp_w_up1": g_up1, "mlp_w_down0": g_down0, "mlp_w_down1": g_down1}
    if comm is None:
        dxc, dbm, dcm, ddt_r, dpar = _ssd_bwd(xc, dtr, cumr, alog_b, d_b, states, dy_ssd, name="ssd_scan_bwd")
    else:
        (dxc, dbm, dcm, ddt_r, dpar), received = _ssd_bwd(xc, dtr, cumr, alog_b, d_b, states, dy_ssd,
                                                          name="ssd_scan_bwd", hook=comm.exchange_hook(mats, "early"))
        comm.received(received)
    dxbc, g_conv_w, g_conv_b = _conv_bwd(zx, wts["ssd_conv_w"], row(wts["ssd_conv_b"]), dxc, dbm, dcm, name="ssd_conv_bwd")
    ddt = jnp.pad(jnp.transpose(ddt_r, (2, 0, 1)).reshape(t, SSD_N_HEADS), ((0, 0), (0, LANES - SSD_N_HEADS)))
    ddt_raw, g_dt_bias = _softplus_bwd(zx, bias_row, ddt, name="ssd_dt_bwd")
    dzx = jnp.concatenate([dz, dxbc, ddt_raw], axis=1)
    g_w_in = _w_in_to_shards(_matmul(u0, dzx, mode="tn", out_dtypes=(F32,), tn=896, name="ssd_dwin"), name="ssd_dwin_shards")
    mats["ssd_w_in"] = g_w_in
    if comm is None:
        du0 = _matmul(dzx, wts["ssd_w_in"], mode="nt", out_dtypes=(F32,), tk=896, name="ssd_du")
    else:
        du0, received = _matmul(dzx, wts["ssd_w_in"], mode="nt", out_dtypes=(F32,), tk=896, name="ssd_du",
                                hook=comm.exchange_hook(mats, "late"))
        comm.received(received)
    grad_x, g_mix_pre0 = _rms_bwd(x, row(mix_pre[0]), du0, resid=dh1, name="rms_pre_mix0_bwd")

    dpar = dpar.reshape(SSD_N_HEADS, LANES)
    vecs = {
        "ssd_conv_w": g_conv_w, "ssd_conv_b": g_conv_b.reshape(-1),
        "ssd_dt_bias": g_dt_bias[0, :SSD_N_HEADS], "ssd_a_log": dpar[:, 0], "ssd_d": dpar[:, 1],
        "ssd_norm_w": g_norm_w.reshape(-1), "attn_b_qkv": g_b_qkv.reshape(-1), "attn_sinks": g_sinks[:, 0],
        "attn_b_o": g_b_o.reshape(-1),
        "mix_pre_norm": jnp.concatenate([g_mix_pre0, g_mix_pre1]), "mix_post_norm": jnp.concatenate([g_mix_post0, g_mix_post1]),
        "ffn_pre_norm": jnp.concatenate([g_ffn_pre0, g_ffn_pre1]), "ffn_post_norm": jnp.concatenate([g_ffn_post0, g_ffn_post1]),
    }
    return loss_tile, grad_x, mats, vecs


def _mesh_position():
    return lax.axis_index("x"), lax.axis_index("y"), lax.axis_index("c")


def _flip(v, bit):
    return 1 - v if bit else v


OTHER_CHIPS = ((1, 0), (0, 1), (1, 1))


def _comm_params():
    return pltpu.CompilerParams(vmem_limit_bytes=VMEM_LIMIT)


def _staged_copies(srcs, dsts, bufs, sems_in, sems_out):
    loads = [pltpu.make_async_copy(s, b, sems_in.at[i]) for i, (s, b) in enumerate(zip(srcs, bufs))]
    stores = [pltpu.make_async_copy(b, d, sems_out.at[i]) for i, (b, d) in enumerate(zip(bufs, dsts))]
    return loads, stores


class _GatherHook:
    def __init__(self, mats, vecs=()):
        self.arrs = list(mats) + list(vecs)
        self.nm, self.n = len(mats), len(self.arrs)
        n_ici, n_fwd = (N_CHIPS - 1) * self.n, max((N_CHIPS - 1) * self.nm, 1)
        dma = pltpu.SemaphoreType.DMA
        self.out_shape = [jax.ShapeDtypeStruct((N_CHIPS,) + a.shape, a.dtype) for a in self.arrs]
        self.scratch = [pltpu.VMEM(a.shape, a.dtype) for a in self.arrs] + [
            dma((n_ici,)), dma((n_ici,)), dma((n_fwd,)), dma((n_fwd,)), dma((self.n,)), dma((self.n,))]

    def plan(self, ins, outs, scratch):
        n, nm = self.n, self.nm
        bufs = scratch[:n]
        ici_send, ici_recv, fwd_send, fwd_recv, load_sems, store_sems = scratch[n:]
        xi, yi, ci = _mesh_position()
        me = 2 * xi + yi
        loads, stores = _staged_copies(ins, [outs[i].at[me] for i in range(n)], bufs, load_sems, store_sems)
        sends, landed, forwards, from_sibling = [], [], [], []
        for j, (bx, by) in enumerate(OTHER_CHIPS):
            px, py = _flip(xi, bx), _flip(yi, by)
            peer = 2 * px + py
            for i in range(n):
                k = j * n + i
                mk = functools.partial(pltpu.make_async_remote_copy, send_sem=ici_send.at[k], recv_sem=ici_recv.at[k],
                                       device_id=(px, py, ci), device_id_type=MESH)
                if i < nm:
                    sends.append(mk(src_ref=ins[i].at[ci], dst_ref=outs[i].at[me, ci]))
                    landed.append(mk(src_ref=ins[i].at[ci], dst_ref=outs[i].at[peer, ci]))
                    kf = j * nm + i
                    fw = functools.partial(pltpu.make_async_remote_copy, send_sem=fwd_send.at[kf], recv_sem=fwd_recv.at[kf],
                                           device_id=(xi, yi, 1 - ci), device_id_type=MESH)
                    forwards.append(fw(src_ref=outs[i].at[peer, ci], dst_ref=outs[i].at[peer, ci]))
                    from_sibling.append(fw(src_ref=outs[i].at[peer, ci], dst_ref=outs[i].at[peer, 1 - ci]))
                else:
                    sends.append(mk(src_ref=ins[i], dst_ref=outs[i].at[me]))
                    landed.append(mk(src_ref=ins[i], dst_ref=outs[i].at[peer]))
                    forwards.append(None)
        return loads, stores, sends, landed, forwards, from_sibling

    @staticmethod
    def start(p):
        loads, _, sends, _, _, _ = p
        for cp in loads + sends:
            cp.start()

    @staticmethod
    def relay(p):
        loads, stores, _, landed, forwards, _ = p
        for ld, st in zip(loads, stores):
            ld.wait()
            st.start()
        for cp, fw in zip(landed, forwards):
            cp.wait_recv()
            if fw is not None:
                fw.start()

    @staticmethod
    def finish(p):
        _, stores, sends, _, forwards, from_sibling = p
        for cp in from_sibling:
            cp.wait_recv()
        for cp in sends + [fw for fw in forwards if fw is not None]:
            cp.wait_send()
        for st in stores:
            st.wait()


def _run_hook(hook, ins, outs, scratch, step, n_steps):
    p = hook.plan(ins, outs, scratch)
    relay_step = min(max(1, (3 * n_steps) // 4), n_steps - 1)

    @pl.when(step == 0)
    def _():
        hook.start(p)

    if relay_step < n_steps - 1:
        @pl.when(step == relay_step)
        def _():
            hook.relay(p)

    @pl.when(step == n_steps - 1)
    def _():
        if relay_step == n_steps - 1:
            hook.relay(p)
        hook.finish(p)


def _hook_call(hook, *, name):
    n = len(hook.arrs)

    def body(*refs):
        p = hook.plan(refs[:n], refs[n:n + len(hook.out_shape)], refs[n + len(hook.out_shape):])
        hook.start(p)
        hook.relay(p)
        hook.finish(p)

    return pl.pallas_call(
        body, in_specs=[ANY] * n, out_specs=[ANY] * len(hook.out_shape), out_shape=hook.out_shape,
        scratch_shapes=hook.scratch, compiler_params=_comm_params(), name=name)(*hook.arrs)


def _send_other_half(parts, *, name):
    n = len(parts)

    def body(*refs):
        ins, outs = refs[:n], refs[n:2 * n]
        send_sems, recv_sems = refs[2 * n:]
        xi, yi, ci = _mesh_position()
        sibling = (xi, yi, 1 - ci)
        for i in range(n):
            for s in range(N_CHIPS):
                pltpu.make_async_remote_copy(src_ref=ins[i].at[s, 1 - ci], dst_ref=outs[i].at[s], send_sem=send_sems.at[i],
                                             recv_sem=recv_sems.at[i], device_id=sibling, device_id_type=MESH).start()
        for i in range(n):
            pltpu.make_async_remote_copy(src_ref=outs[i], dst_ref=outs[i], send_sem=send_sems.at[i], recv_sem=recv_sems.at[i],
                                         device_id=sibling, device_id_type=MESH).wait()

    return pl.pallas_call(
        body, in_specs=[ANY] * n, out_specs=[ANY] * n,
        out_shape=[jax.ShapeDtypeStruct((p.shape[0],) + p.shape[2:], p.dtype) for p in parts],
        scratch_shapes=[pltpu.SemaphoreType.DMA((n,)), pltpu.SemaphoreType.DMA((n,))],
        name=name)(*parts)


ROW_BLOCKS = 8


def _add_sibling_half(parts, theirs, core, *, name):
    n = len(parts)

    def body(core_ref, *refs):
        for a_ref, b_ref, o_ref in zip(refs[:n], refs[n:2 * n], refs[2 * n:]):
            o_ref[...] = (a_ref[...].astype(F32) + b_ref[...].astype(F32)).astype(o_ref.dtype)

    mine = lambda p: pl.BlockSpec((None, None, p.shape[2] // ROW_BLOCKS, p.shape[3]), lambda s, rb, core_ref: (s, core_ref[0], rb, 0))
    other = lambda p: pl.BlockSpec((None, p.shape[1] // ROW_BLOCKS, p.shape[2]), lambda s, rb, core_ref: (s, rb, 0))
    return pl.pallas_call(
        body,
        grid_spec=pltpu.PrefetchScalarGridSpec(
            num_scalar_prefetch=1, grid=(N_CHIPS, ROW_BLOCKS),
            in_specs=[mine(p) for p in parts] + [other(q) for q in theirs], out_specs=[other(q) for q in theirs]),
        out_shape=[jax.ShapeDtypeStruct(q.shape, BF16) for q in theirs],
        compiler_params=_params("parallel", "parallel"), name=name)(core, *parts, *theirs)


class _ExchangeHook:
    def __init__(self, parts, to_all=()):
        self.arrs = list(parts) + list(to_all)
        self.n_parts, self.n = len(parts), len(self.arrs)
        n_ici, n_peer = max((N_CHIPS - 1) * self.n_parts, 1), (N_DEV - 1) * max(len(to_all), 1)
        dma = pltpu.SemaphoreType.DMA
        self.out_shape = [jax.ShapeDtypeStruct(p.shape, p.dtype) for p in parts] + [
            jax.ShapeDtypeStruct((N_DEV,) + a.shape, a.dtype) for a in to_all]
        self.scratch = [pltpu.VMEM(p.shape[1:], p.dtype) for p in parts] + [pltpu.VMEM(a.shape, a.dtype) for a in to_all] + [
            dma((n_ici,)), dma((n_ici,)), dma((n_peer,)), dma((n_peer,)), dma((self.n,)), dma((self.n,))]

    def plan(self, ins, outs, scratch):
        n, npt = self.n, self.n_parts
        bufs = scratch[:n]
        send_sems, recv_sems, all_send, all_recv, load_sems, store_sems = scratch[n:]
        xi, yi, ci = _mesh_position()
        me_chip = 2 * xi + yi
        me = 4 * xi + 2 * yi + ci
        loads, stores = _staged_copies([ins[i].at[me_chip] for i in range(npt)] + list(ins[npt:]),
                                       [outs[i].at[me_chip] for i in range(npt)] + [outs[i].at[me] for i in range(npt, n)],
                                       bufs, load_sems, store_sems)
        sends, recvs = [], []
        for j, (bx, by) in enumerate(OTHER_CHIPS):
            px, py = _flip(xi, bx), _flip(yi, by)
            peer = 2 * px + py
            for i in range(npt):
                k = j * npt + i
                mk = functools.partial(pltpu.make_async_remote_copy, src_ref=ins[i].at[peer], send_sem=send_sems.at[k],
                                       recv_sem=recv_sems.at[k], device_id=(px, py, ci), device_id_type=MESH)
                sends.append(mk(dst_ref=outs[i].at[me_chip]))
                recvs.append(mk(dst_ref=outs[i].at[peer]))
        for i in range(npt, n):
            for k in range(1, N_DEV):
                px, py, pc = _flip(xi, (k >> 2) & 1), _flip(yi, (k >> 1) & 1), _flip(ci, k & 1)
                slot = (i - npt) * (N_DEV - 1) + k - 1
                mk = functools.partial(pltpu.make_async_remote_copy, src_ref=ins[i], send_sem=all_send.at[slot],
                                       recv_sem=all_recv.at[slot], device_id=(px, py, pc), device_id_type=MESH)
                sends.append(mk(dst_ref=outs[i].at[me]))
                recvs.append(mk(dst_ref=outs[i].at[4 * px + 2 * py + pc]))
        return loads, stores, sends, recvs

    @staticmethod
    def start(p):
        loads, _, sends, _ = p
        for cp in loads + sends:
            cp.start()

    @staticmethod
    def relay(p):
        loads, stores, _, _ = p
        for ld, st in zip(loads, stores):
            ld.wait()
            st.start()

    @staticmethod
    def finish(p):
        _, stores, sends, recvs = p
        for cp in recvs:
            cp.wait_recv()
        for cp in sends:
            cp.wait_send()
        for st in stores:
            st.wait()


def _sum_chips(parts, *, name):
    n = len(parts)
    p = parts[0].shape[0]

    def body(*refs):
        s = pl.program_id(1)
        for x_ref, o_ref in zip(refs[:n], refs[n:]):
            @pl.when(s == 0)
            def _():
                o_ref[...] = x_ref[...].astype(F32)

            @pl.when(s > 0)
            def _():
                o_ref[...] += x_ref[...].astype(F32)

    blocks = lambda q: ROW_BLOCKS if q.shape[1] % (8 * ROW_BLOCKS) == 0 else 1
    assert len({blocks(q) for q in parts}) == 1
    nb = blocks(parts[0])
    return pl.pallas_call(
        body, grid=(nb, p),
        in_specs=[pl.BlockSpec((None, q.shape[1] // nb, q.shape[2]), lambda rb, s: (s, rb, 0)) for q in parts],
        out_specs=[pl.BlockSpec((q.shape[1] // nb, q.shape[2]), lambda rb, s: (rb, 0)) for q in parts],
        out_shape=[jax.ShapeDtypeStruct(q.shape[1:], F32) for q in parts],
        compiler_params=_params("parallel", "arbitrary"), name=name)(*parts)


def _swap_halves(halves, layers, *, name):
    n = len(halves)
    out_shapes, slots = [], []
    for i, h in enumerate(halves):
        pair = [p for p in layers if i in p]
        if pair and pair[0][1] == i:
            slots.append((slots[pair[0][0]][0], 1))
        elif pair:
            out_shapes.append(jax.ShapeDtypeStruct((2, 2) + h.shape, h.dtype))
            slots.append((len(out_shapes) - 1, 0))
        else:
            out_shapes.append(jax.ShapeDtypeStruct((2,) + h.shape, h.dtype))
            slots.append((len(out_shapes) - 1, None))
    n_out = len(out_shapes)

    def body(*refs):
        ins, outs, bufs = refs[:n], refs[n:n + n_out], refs[n + n_out:2 * n + n_out]
        send_sems, recv_sems, load_sems, store_sems = refs[2 * n + n_out:]
        xi, yi, ci = _mesh_position()
        own, sends, recvs = [], [], []
        for i in range(n):
            o, layer = slots[i]
            dst = (lambda core: outs[o].at[core]) if layer is None else (lambda core: outs[o].at[layer, core])
            own.append(dst(ci))
            mk = functools.partial(pltpu.make_async_remote_copy, src_ref=ins[i], send_sem=send_sems.at[i],
                                   recv_sem=recv_sems.at[i], device_id=(xi, yi, 1 - ci), device_id_type=MESH)
            sends.append(mk(dst_ref=dst(ci)))
            recvs.append(mk(dst_ref=dst(1 - ci)))
        loads, stores = _staged_copies(ins, own, bufs, load_sems, store_sems)
        for cp in loads + sends:
            cp.start()
        for ld, st in zip(loads, stores):
            ld.wait()
            st.start()
        for cp in recvs:
            cp.wait_recv()
        for cp in sends:
            cp.wait_send()
        for st in stores:
            st.wait()

    return pl.pallas_call(
        body, in_specs=[ANY] * n, out_specs=[ANY] * n_out, out_shape=out_shapes,
        scratch_shapes=[pltpu.VMEM(h.shape, h.dtype) for h in halves]
        + [pltpu.SemaphoreType.DMA((n,)), pltpu.SemaphoreType.DMA((n,)), pltpu.SemaphoreType.DMA((n,)), pltpu.SemaphoreType.DMA((n,))],
        compiler_params=_comm_params(), name=name)(*halves)


def _full_weight(name, gathered):
    s, _, r, c = gathered.shape
    if name == "ssd_w_in":
        return _w_in_from_shards(gathered.reshape(s, 2 * r, c), name="ssd_w_in_unshard")
    if name in ("attn_w_qkv", "mlp_w_up0", "mlp_w_up1"):
        return gathered.reshape(s, 2 * r, c)
    return gathered.reshape(s * 2 * r, c)


class _StepComm:
    GATHER = {"in_proj": ("mlp_w_up0", "attn_w_qkv"), "conv": ("mlp_w_down0", "attn_w_o"), "scan": ("ssd_w_out",),
              "mlp_up_l0": ("mlp_w_up1",), "mlp_down_l0": ("mlp_w_down1",)}
    EXCHANGE = {"early": ("ssd_w_out", "attn_w_qkv", "attn_w_o", "mlp_w_up0", "mlp_w_up1", "mlp_w_down0", "mlp_w_down1"),
                "late": ("ssd_w_in",)}

    def __init__(self, shards, core):
        self.shards, self.core = shards, core
        self.chip_parts = {}
        self._pending = None

    def gather_hook(self, stage):
        names = self.GATHER.get(stage)
        return _GatherHook([self.shards[n] for n in names]) if names else None

    def weights_from(self, stage, gathered):
        return {n: _full_weight(n, g) for n, g in zip(self.GATHER[stage], gathered)}

    def chip_sums(self, mats, tag):
        parts = [_shard_halves(a) for a in mats.values()]
        theirs = _send_other_half(parts, name=f"grad_sibling_send_{tag}")
        return _add_sibling_half(parts, theirs, self.core, name=f"grad_chip_sum_{tag}")

    def exchange_hook(self, mats, which):
        self._pending = self.EXCHANGE[which]
        return _ExchangeHook(self.chip_sums({n: mats[n] for n in self._pending}, which))

    def received(self, arrays):
        self.chip_parts.update(zip(self._pending, arrays))


def _adamw(w, g, m, v, *, name):
    r, c = w.shape
    tr = 256 if r % 256 == 0 else r
    blk = pl.BlockSpec((tr, c), lambda i: (i, 0))

    def body(w_ref, g_ref, m_ref, v_ref, d_ref, nm_ref, nv_ref):
        gv = g_ref[...]
        nm = ADAM_B1 * m_ref[...] + (1.0 - ADAM_B1) * gv
        nv = ADAM_B2 * v_ref[...] + (1.0 - ADAM_B2) * (gv * gv)
        m_hat = nm / (1.0 - ADAM_B1 ** ADAM_STEP)
        v_hat = nv / (1.0 - ADAM_B2 ** ADAM_STEP)
        d_ref[...] = -ADAM_LR * (m_hat / (jnp.sqrt(v_hat) + ADAM_EPS) + ADAM_WD * w_ref[...])
        nm_ref[...] = nm
        nv_ref[...] = nv

    sh = jax.ShapeDtypeStruct((r, c), F32)
    return pl.pallas_call(body, grid=(r // tr,), in_specs=[blk] * 4, out_specs=[blk] * 3, out_shape=[sh] * 3,
                          compiler_params=_params("parallel"), name=name)(w, g, m, v)


SM_CONV_B, SM_NORM_W, SM_MIX_PRE, SM_MIX_POST, SM_FFN_PRE, SM_FFN_POST, SM_MISC, SM_CONV_W, SM_B_QKV, SM_B_O = 0, 4, 6, 8, 10, 12, 14, 16, 32, 34
SM_ROWS = 40
MISC_DT_BIAS, MISC_A_LOG, MISC_D, MISC_SINKS, MISC_LOSS = 0, 32, 64, 96, 112


def _shard_halves(a):
    c = a.shape[-1]
    return a.reshape(N_CHIPS, 2, -1, c)


def _rows(v):
    return v.reshape(-1, D_MODEL)


def _misc_row(dt_bias, a_log, d, sinks, loss):
    pad = jnp.zeros((D_MODEL - MISC_LOSS - 1,), F32)
    return jnp.concatenate([dt_bias.reshape(-1), a_log.reshape(-1), d.reshape(-1), sinks.reshape(-1), loss.reshape(1), pad]).reshape(1, D_MODEL)


def _replicated_rows(p, loss):
    return jnp.concatenate([
        _rows(p["ssd_conv_b"]), _rows(p["ssd_norm_w"]), _rows(p["mix_pre_norm"]), _rows(p["mix_post_norm"]),
        _rows(p["ffn_pre_norm"]), _rows(p["ffn_post_norm"]),
        _misc_row(p["ssd_dt_bias"], p["ssd_a_log"], p["ssd_d"], p["attn_sinks"], loss), jnp.zeros((1, D_MODEL), F32)], axis=0)


def _sharded_rows(conv_w, b_qkv, b_o):
    last = jnp.concatenate([b_qkv.reshape(-1), b_o.reshape(-1), jnp.zeros((D_MODEL - 640,), F32)]).reshape(1, D_MODEL)
    return jnp.concatenate([conv_w.reshape(SSD_CONV_WIDTH, D_MODEL), last, jnp.zeros((3, D_MODEL), F32)], axis=0)


REPLICATED = ("ssd_conv_b", "ssd_dt_bias", "ssd_a_log", "ssd_d", "ssd_norm_w", "attn_sinks",
              "mix_pre_norm", "mix_post_norm", "ffn_pre_norm", "ffn_post_norm")
MATRICES = ("ssd_w_in", "ssd_w_out", "attn_w_qkv", "attn_w_o", "mlp_w_up", "mlp_w_down")
WEIGHT_NAMES = ("ssd_w_in", "ssd_conv_w", "ssd_conv_b", "ssd_dt_bias", "ssd_a_log", "ssd_d", "ssd_norm_w", "ssd_w_out",
                "attn_w_qkv", "attn_b_qkv", "attn_sinks", "attn_w_o", "attn_b_o", "mlp_w_up", "mlp_w_down",
                "mix_pre_norm", "mix_post_norm", "ffn_pre_norm", "ffn_post_norm")


def _unpack_small(rows16, rows8, like):
    misc = rows16[SM_MISC]
    out = {
        "ssd_conv_b": rows16[SM_CONV_B:SM_CONV_B + 4], "ssd_norm_w": rows16[SM_NORM_W:SM_NORM_W + 2],
        "mix_pre_norm": rows16[SM_MIX_PRE:SM_MIX_PRE + 2], "mix_post_norm": rows16[SM_MIX_POST:SM_MIX_POST + 2],
        "ffn_pre_norm": rows16[SM_FFN_PRE:SM_FFN_PRE + 2], "ffn_post_norm": rows16[SM_FFN_POST:SM_FFN_POST + 2],
        "ssd_dt_bias": misc[MISC_DT_BIAS:MISC_DT_BIAS + 32], "ssd_a_log": misc[MISC_A_LOG:MISC_A_LOG + 32],
        "ssd_d": misc[MISC_D:MISC_D + 32], "attn_sinks": misc[MISC_SINKS:MISC_SINKS + 16],
        "ssd_conv_w": rows8[0:SSD_CONV_WIDTH], "attn_b_qkv": rows8[SSD_CONV_WIDTH, 0:384], "attn_b_o": rows8[SSD_CONV_WIDTH, 384:640],
    }
    return {k: v.reshape(like[k].shape) for k, v in out.items()}


def kernel(x, ssd_w_in, ssd_conv_w, ssd_conv_b, ssd_dt_bias, ssd_a_log, ssd_d, ssd_norm_w, ssd_w_out, attn_w_qkv, attn_b_qkv, attn_sinks, attn_w_o, attn_b_o, mlp_w_up, mlp_w_down, mix_pre_norm, mix_post_norm, ffn_pre_norm, ffn_post_norm, loss_target, m_ssd_w_in, m_ssd_conv_w, m_ssd_conv_b, m_ssd_dt_bias, m_ssd_a_log, m_ssd_d, m_ssd_norm_w, m_ssd_w_out, m_attn_w_qkv, m_attn_b_qkv, m_attn_sinks, m_attn_w_o, m_attn_b_o, m_mlp_w_up, m_mlp_w_down, m_mix_pre_norm, m_mix_post_norm, m_ffn_pre_norm, m_ffn_post_norm, v_ssd_w_in, v_ssd_conv_w, v_ssd_conv_b, v_ssd_dt_bias, v_ssd_a_log, v_ssd_d, v_ssd_norm_w, v_ssd_w_out, v_attn_w_qkv, v_attn_b_qkv, v_attn_sinks, v_attn_w_o, v_attn_b_o, v_mlp_w_up, v_mlp_w_down, v_mix_pre_norm, v_mix_post_norm, v_ffn_pre_norm, v_ffn_post_norm):
    w = dict(zip(WEIGHT_NAMES, (ssd_w_in, ssd_conv_w, ssd_conv_b, ssd_dt_bias, ssd_a_log, ssd_d, ssd_norm_w, ssd_w_out, attn_w_qkv, attn_b_qkv, attn_sinks, attn_w_o, attn_b_o, mlp_w_up, mlp_w_down, mix_pre_norm, mix_post_norm, ffn_pre_norm, ffn_post_norm)))
    m = dict(zip(WEIGHT_NAMES, (m_ssd_w_in, m_ssd_conv_w, m_ssd_conv_b, m_ssd_dt_bias, m_ssd_a_log, m_ssd_d, m_ssd_norm_w, m_ssd_w_out, m_attn_w_qkv, m_attn_b_qkv, m_attn_sinks, m_attn_w_o, m_attn_b_o, m_mlp_w_up, m_mlp_w_down, m_mix_pre_norm, m_mix_post_norm, m_ffn_pre_norm, m_ffn_post_norm)))
    v = dict(zip(WEIGHT_NAMES, (v_ssd_w_in, v_ssd_conv_w, v_ssd_conv_b, v_ssd_dt_bias, v_ssd_a_log, v_ssd_d, v_ssd_norm_w, v_ssd_w_out, v_attn_w_qkv, v_attn_b_qkv, v_attn_sinks, v_attn_w_o, v_attn_b_o, v_mlp_w_up, v_mlp_w_down, v_mix_pre_norm, v_mix_post_norm, v_ffn_pre_norm, v_ffn_post_norm)))
    chip = 2 * lax.axis_index("x") + lax.axis_index("y")

    two_halves = lambda a: a.astype(BF16).reshape(2, a.shape[0] // 2, a.shape[1])
    shards = {"ssd_w_out": w["ssd_w_out"][0], "attn_w_qkv": w["attn_w_qkv"][0], "attn_w_o": w["attn_w_o"][0],
              "mlp_w_up0": w["mlp_w_up"][0], "mlp_w_up1": w["mlp_w_up"][1],
              "mlp_w_down0": w["mlp_w_down"][0], "mlp_w_down1": w["mlp_w_down"][1]}
    core = lax.axis_index("c").astype(jnp.int32).reshape(1)
    comm = _StepComm({k: two_halves(a) for k, a in shards.items()}, core)
    g_in, g_conv, g_bqkv, g_bo = _hook_call(
        _GatherHook([two_halves(w["ssd_w_in"][0])], [w["ssd_conv_w"][0], w["attn_b_qkv"], w["attn_b_o"]]), name="weight_all_gather")
    full = {
        "ssd_w_in": _full_weight("ssd_w_in", g_in),
        "ssd_conv_w": g_conv.transpose(1, 0, 2).reshape(SSD_CONV_WIDTH, SSD_CONV_DIM),
        "attn_b_qkv": g_bqkv.reshape(ATTN_QKV), "attn_b_o": g_bo.reshape(D_MODEL),
    }
    for name in REPLICATED:
        full[name] = w[name][0] if name.startswith(("ssd_", "attn_")) else w[name]

    loss_tile, grad_x, gm, g = _local_step(x[0], loss_target[0], full, comm)

    conv_w_rows = g["ssd_conv_w"].reshape(SSD_CONV_WIDTH * N_CHIPS, D_MODEL)
    b_qkv_rows = jnp.pad(g["attn_b_qkv"], (0, 2 * D_MODEL - ATTN_QKV)).reshape(2, D_MODEL)
    small = jnp.concatenate([_replicated_rows(g, loss_tile[0, 0]), conv_w_rows, b_qkv_rows, _rows(g["attn_b_o"]),
                             jnp.zeros((SM_ROWS - SM_B_O - 1, D_MODEL), F32)], axis=0)
    small_all, = _hook_call(_ExchangeHook([], [small]), name="vector_grad_all_gather")
    order = ("ssd_w_in", "ssd_w_out", "attn_w_qkv", "attn_w_o", "mlp_w_up0", "mlp_w_up1", "mlp_w_down0", "mlp_w_down1")
    halves = _sum_chips([comm.chip_parts[k] for k in order], name="grad_sum")
    r_in, r_out, r_qkv, r_o, r_up, r_down = _swap_halves(halves, layers=((4, 5), (6, 7)), name="grad_halves_swap")
    small_sum, = _sum_chips([small_all], name="small_grad_sum")

    grads = {"ssd_w_in": r_in, "ssd_w_out": r_out, "attn_w_qkv": r_qkv, "attn_w_o": r_o, "mlp_w_up": r_up, "mlp_w_down": r_down}
    grads = {k: a.reshape(w[k].shape) for k, a in grads.items()}
    conv_w_g = lax.dynamic_index_in_dim(small_sum[SM_CONV_W:SM_CONV_W + 16].reshape(SSD_CONV_WIDTH, N_CHIPS, D_MODEL), chip, axis=1, keepdims=False)
    b_qkv_g = lax.dynamic_slice_in_dim(small_sum[SM_B_QKV:SM_B_QKV + 2].reshape(-1), chip * 384, 384)
    b_o_g = lax.dynamic_slice_in_dim(small_sum[SM_B_O], chip * 256, 256)
    small_g = jnp.concatenate([small_sum[0:16], _sharded_rows(conv_w_g, b_qkv_g, b_o_g)], axis=0)
    grads.update(_unpack_small(small_g[0:16], small_g[16:24], w))
    loss = small_sum[SM_MISC, MISC_LOSS]

    delta, new_m, new_v = {}, {}, {}
    for name in MATRICES:
        shape = w[name].shape
        as2d = lambda a: a.reshape(-1, shape[-1])
        d2, m2, v2 = _adamw(as2d(w[name]), as2d(grads[name]), as2d(m[name]), as2d(v[name]), name=f"adamw_{name}")
        delta[name], new_m[name], new_v[name] = d2.reshape(shape), m2.reshape(shape), v2.reshape(shape)
    zero = jnp.zeros((), F32)
    small_pack = lambda p: jnp.concatenate([_replicated_rows({k: p[k] for k in REPLICATED}, zero),
                                            _sharded_rows(p["ssd_conv_w"], p["attn_b_qkv"], p["attn_b_o"])], axis=0)
    d_s, m_s, v_s = _adamw(small_pack(w), small_g, small_pack(m), small_pack(v), name="adamw_vectors")
    delta.update(_unpack_small(d_s[0:16], d_s[16:24], w))
    new_m.update(_unpack_small(m_s[0:16], m_s[16:24], w))
    new_v.update(_unpack_small(v_s[0:16], v_s[16:24], w))

    return (loss, grad_x[None], *[grads[n] for n in WEIGHT_NAMES], *[delta[n] for n in WEIGHT_NAMES],
            *[new_m[n] for n in WEIGHT_NAMES], *[new_v[n] for n in WEIGHT_NAMES])
```

```python
import functools
import math

import jax
import jax.numpy as jnp
from jax import lax
from jax.experimental import pallas as pl
from jax.experimental.pallas import tpu as pltpu

F32 = jnp.float32
BF16 = jnp.bfloat16

D_MODEL = 1024
SSD_D_INNER = 2048
SSD_HEAD_DIM = 64
SSD_N_HEADS = 32
SSD_N_GROUPS = 8
SSD_HPG = 4
SSD_D_STATE = 128
SSD_CONV_WIDTH = 4
SSD_CHUNK = 128
SSD_CONV_DIM = 4096
SSD_IN_DIM = 6176
SSD_IN_PAD = 6272
SSD_GW = SSD_HPG * SSD_HEAD_DIM
ATTN_HEAD_DIM = 64
ATTN_N_Q = 16
ATTN_N_KV = 4
ATTN_REP = 4
ATTN_WINDOW = 128
ATTN_QKV = 1536
D_FF = 4096
NORM_EPS = 1e-6

ADAM_LR = 0.001
ADAM_B1 = 0.9
ADAM_B2 = 0.999
ADAM_EPS = 1e-08
ADAM_WD = 0.01
ADAM_STEP = 10

N_CHIPS = 4
N_DEV = 8
LANES = 128
VMEM_LIMIT = 48 * 1024 * 1024

MESH = pl.DeviceIdType.MESH


def _params(*sem):
    return pltpu.CompilerParams(dimension_semantics=sem, vmem_limit_bytes=VMEM_LIMIT)


def _dot(a, b, dims):
    return lax.dot_general(a, b, (dims, ((), ())), preferred_element_type=F32)


def _dot_nn(a, b):
    return _dot(a, b, ((1,), (0,)))


def _dot_nt(a, b):
    return _dot(a, b, ((1,), (1,)))


def _dot_tn(a, b):
    return _dot(a, b, ((0,), (0,)))


def _sigmoid(x):
    return 0.5 * jnp.tanh(0.5 * x) + 0.5


ANY = pl.BlockSpec(memory_space=pl.ANY)


class _HookSlots:
    def __init__(self, hook, n_in, n_out, n_scratch):
        self.hook = hook
        self.n_in, self.n_out, self.n_scratch = n_in, n_out, n_scratch
        self.inputs = list(hook.arrs) if hook else []
        self.out_shape = list(hook.out_shape) if hook else []
        self.scratch = list(hook.scratch) if hook else []
        self.in_specs = [ANY] * len(self.inputs)
        self.out_specs = [ANY] * len(self.out_shape)

    def _split(self, refs):
        a = self.n_in
        b = a + len(self.inputs)
        c = b + self.n_out
        d = c + len(self.out_shape)
        e = d + self.n_scratch
        return refs[:a], refs[a:b], refs[b:c], refs[c:d], refs[d:e], refs[e:]

    def own(self, refs):
        ins, _, outs, _, scratch, _ = self._split(refs)
        return ins, outs, scratch

    def run(self, refs, step, n_steps):
        _, h_in, _, h_out, _, h_scratch = self._split(refs)
        _run_hook(self.hook, h_in, h_out, h_scratch, step, n_steps)

    def semantics(self, *sem):
        return sem if self.hook is None else ("arbitrary",) * len(sem)


def _matmul(a, b, *, mode, out_dtypes, name, epilogue=None, extras=(), tm=1024, tn=1024, tk=1024,
            b_shards=False, out_shards=False, hook=None):
    if b_shards:
        s, b_rows, b_cols = b.shape
        b2 = (b_rows, s * b_cols)
        if mode == "nn":
            tn = b_cols
        else:
            assert mode == "nt"
            tk = b_cols
    else:
        b2 = b.shape
    if mode == "nn":
        (m, k), (k2, n) = a.shape, b2
    elif mode == "nt":
        (m, k), (n, k2) = a.shape, b2
    else:
        (k, m), (k2, n) = a.shape, b2
    assert k == k2, (a.shape, b.shape, mode)
    tm, tn, tk = min(tm, m), min(tn, n), min(tk, k)
    assert m % tm == 0 and n % tn == 0 and k % tk == 0, (m, n, k, tm, tn, tk)
    nk = k // tk
    if mode == "tn":
        a_spec = pl.BlockSpec((tk, tm), lambda i, j, kk: (kk, i))
    else:
        a_spec = pl.BlockSpec((tm, tk), lambda i, j, kk: (i, kk))
    if b_shards and mode == "nn":
        b_spec = pl.BlockSpec((None, tk, tn), lambda i, j, kk: (j, kk, 0))
    elif b_shards:
        b_spec = pl.BlockSpec((None, tn, tk), lambda i, j, kk: (kk, j, 0))
    elif mode == "nt":
        b_spec = pl.BlockSpec((tn, tk), lambda i, j, kk: (j, kk))
    else:
        b_spec = pl.BlockSpec((tk, tn), lambda i, j, kk: (kk, j))
    dims = {"nn": ((1,), (0,)), "nt": ((1,), (1,)), "tn": ((0,), (0,))}[mode]
    ex_specs = []
    for arr, kind in extras:
        if kind == "tile":
            ex_specs.append(pl.BlockSpec((tm, tn), lambda i, j, kk: (i, j)))
        else:
            ex_specs.append(pl.BlockSpec((1, tn), lambda i, j, kk: (0, j)))
    n_ex, n_out = len(extras), len(out_dtypes)
    if epilogue is None:
        epilogue = lambda acc: (acc,)
    hk = _HookSlots(hook, n_in=2 + n_ex, n_out=n_out, n_scratch=0 if nk == 1 else 1)
    grid = (m // tm, n // tn, nk)

    def body(*refs):
        (a_ref, b_ref, *ex), outs, scratch = hk.own(refs)
        if hook is not None:
            step = (pl.program_id(0) * grid[1] + pl.program_id(1)) * grid[2] + pl.program_id(2)
            hk.run(refs, step, grid[0] * grid[1] * grid[2])

        def finish(acc):
            res = epilogue(acc, *[e[...] for e in ex])
            for o, r in zip(outs, res):
                o[...] = r.astype(o.dtype)

        if nk == 1:
            finish(_dot(a_ref[...], b_ref[...], dims))
        else:
            acc_ref = scratch[0]
            kk = pl.program_id(2)

            @pl.when(kk == 0)
            def _():
                acc_ref[...] = jnp.zeros_like(acc_ref)

            acc_ref[...] += _dot(a_ref[...], b_ref[...], dims)

            @pl.when(kk == nk - 1)
            def _():
                finish(acc_ref[...])

    if out_shards:
        out_spec = pl.BlockSpec((None, tm, tn), lambda i, j, kk: (j, i, 0))
        out_dims = (n // tn, m, tn)
    else:
        out_spec = pl.BlockSpec((tm, tn), lambda i, j, kk: (i, j))
        out_dims = (m, n)
    outs = pl.pallas_call(
        body,
        grid=grid,
        in_specs=[a_spec, b_spec] + ex_specs + hk.in_specs,
        out_specs=[out_spec for _ in out_dtypes] + hk.out_specs,
        out_shape=[jax.ShapeDtypeStruct(out_dims, dt) for dt in out_dtypes] + hk.out_shape,
        scratch_shapes=([] if nk == 1 else [pltpu.VMEM((tm, tn), F32)]) + hk.scratch,
        compiler_params=_params(*hk.semantics("parallel", "parallel", "arbitrary")),
        name=name,
    )(a, b, *[arr for arr, _ in extras], *hk.inputs)
    own = outs[0] if n_out == 1 else outs[:n_out]
    return own if hook is None else (own, outs[n_out:])


def _row_tile(t, want):
    return min(t, want)


def _rms_fwd(x, w, *, name, resid=None, want_u=None):
    t, d = x.shape
    tr = _row_tile(t, 512)

    def norm(v, wv):
        return v * lax.rsqrt(jnp.mean(v * v, axis=-1, keepdims=True) + NORM_EPS) * wv

    row = pl.BlockSpec((tr, d), lambda i: (i, 0))
    vec = pl.BlockSpec((1, d), lambda i: (0, 0))
    if resid is None:
        def body(x_ref, w_ref, o_ref):
            o_ref[...] = norm(x_ref[...], w_ref[...]).astype(BF16)
        ins, in_specs = (x, w), [row, vec]
        out_shape, out_specs = jax.ShapeDtypeStruct((t, d), BF16), row
    elif want_u is None:
        def body(x_ref, w_ref, r_ref, o_ref):
            o_ref[...] = r_ref[...] + norm(x_ref[...], w_ref[...])
        ins, in_specs = (x, w, resid), [row, vec, row]
        out_shape, out_specs = jax.ShapeDtypeStruct((t, d), F32), row
    else:
        def body(x_ref, w_ref, r_ref, w2_ref, o_ref, u_ref):
            h = r_ref[...] + norm(x_ref[...], w_ref[...])
            o_ref[...] = h
            u_ref[...] = norm(h, w2_ref[...]).astype(BF16)
        ins, in_specs = (x, w, resid, want_u), [row, vec, row, vec]
        out_shape = [jax.ShapeDtypeStruct((t, d), F32), jax.ShapeDtypeStruct((t, d), BF16)]
        out_specs = [row, row]
    return pl.pallas_call(body, grid=(t // tr,), in_specs=in_specs, out_specs=out_specs, out_shape=out_shape,
                          compiler_params=_params("parallel"), name=name)(*ins)


def _rms_bwd(x, w, dy, *, name, resid=None, out_dtype=F32):
    t, d = x.shape
    tr = _row_tile(t, 512)
    row = pl.BlockSpec((tr, d), lambda i: (i, 0))
    vec = pl.BlockSpec((1, d), lambda i: (0, 0))
    has_res = resid is not None

    def body(x_ref, w_ref, dy_ref, *rest):
        if has_res:
            r_ref, dx_ref, dw_ref = rest
        else:
            dx_ref, dw_ref = rest
        xv = x_ref[...]
        dyv = dy_ref[...].astype(F32)
        r = lax.rsqrt(jnp.mean(xv * xv, axis=-1, keepdims=True) + NORM_EPS)
        xhat = xv * r
        dyw = dyv * w_ref[...]
        dx = r * (dyw - xhat * jnp.mean(dyw * xhat, axis=-1, keepdims=True))
        if has_res:
            dx = dx + r_ref[...]
        dx_ref[...] = dx.astype(dx_ref.dtype)

        @pl.when(pl.program_id(0) == 0)
        def _():
            dw_ref[...] = jnp.zeros_like(dw_ref)

        dw_ref[...] += jnp.sum(dyv * xhat, axis=0, keepdims=True)

    ins = (x, w, dy) + ((resid,) if has_res else ())
    in_specs = [row, vec, row] + ([row] if has_res else [])
    return pl.pallas_call(
        body, grid=(t // tr,), in_specs=in_specs, out_specs=[row, vec],
        out_shape=[jax.ShapeDtypeStruct((t, d), out_dtype), jax.ShapeDtypeStruct((1, d), F32)],
        compiler_params=_params("arbitrary"), name=name)(*ins)


def _loss_head(h, target, *, name):
    t, d = h.shape
    tr = _row_tile(t, 512)
    row = pl.BlockSpec((tr, d), lambda i: (i, 0))

    def body(h_ref, t_ref, dh_ref, loss_ref):
        err = h_ref[...] - t_ref[...]
        dh_ref[...] = err * (1.0 / d)

        @pl.when(pl.program_id(0) == 0)
        def _():
            loss_ref[...] = jnp.zeros_like(loss_ref)

        part = jnp.sum(jnp.sum(err * err, axis=1, keepdims=True), axis=0, keepdims=True) * (0.5 / d)
        loss_ref[...] += jnp.broadcast_to(part, loss_ref.shape)

    return pl.pallas_call(
        body, grid=(t // tr,), in_specs=[row, row],
        out_specs=[row, pl.BlockSpec((8, LANES), lambda i: (0, 0))],
        out_shape=[jax.ShapeDtypeStruct((t, d), F32), jax.ShapeDtypeStruct((8, LANES), F32)],
        compiler_params=_params("arbitrary"), name=name)(h, target)


def _col_sum(x, *, name):
    t, n = x.shape
    tr = _row_tile(t, 512)

    def body(x_ref, o_ref):
        @pl.when(pl.program_id(0) == 0)
        def _():
            o_ref[...] = jnp.zeros_like(o_ref)

        o_ref[...] += jnp.sum(x_ref[...].astype(F32), axis=0, keepdims=True)

    return pl.pallas_call(
        body, grid=(t // tr,), in_specs=[pl.BlockSpec((tr, n), lambda i: (i, 0))],
        out_specs=pl.BlockSpec((1, n), lambda i: (0, 0)), out_shape=jax.ShapeDtypeStruct((1, n), F32),
        compiler_params=_params("arbitrary"), name=name)(x)


SSD_IN_SHARD = SSD_IN_DIM // N_CHIPS


def _w_in_from_shards(shards, *, name):
    d = shards.shape[1]
    tr = 256

    def body(s_ref, o_ref):
        o_ref[:, pl.ds(SSD_IN_PAD - LANES, LANES)] = jnp.zeros((tr, LANES), o_ref.dtype)
        for s in range(N_CHIPS):
            o_ref[:, pl.ds(SSD_IN_SHARD * s, SSD_IN_SHARD)] = s_ref[s]

    return pl.pallas_call(
        body, grid=(d // tr,), in_specs=[pl.BlockSpec((N_CHIPS, tr, SSD_IN_SHARD), lambda i: (0, i, 0))],
        out_specs=pl.BlockSpec((tr, SSD_IN_PAD), lambda i: (i, 0)),
        out_shape=jax.ShapeDtypeStruct((d, SSD_IN_PAD), shards.dtype),
        compiler_params=_params("parallel"), name=name)(shards)


def _w_in_to_shards(g, *, name):
    d = g.shape[0]
    tr = 256

    def body(g_ref, o_ref):
        for s in range(N_CHIPS):
            o_ref[s] = g_ref[:, pl.ds(SSD_IN_SHARD * s, SSD_IN_SHARD)].astype(o_ref.dtype)

    return pl.pallas_call(
        body, grid=(d // tr,), in_specs=[pl.BlockSpec((tr, SSD_IN_PAD), lambda i: (i, 0))],
        out_specs=pl.BlockSpec((N_CHIPS, tr, SSD_IN_SHARD), lambda i: (0, i, 0)),
        out_shape=jax.ShapeDtypeStruct((N_CHIPS, d, SSD_IN_SHARD), BF16),
        compiler_params=_params("parallel"), name=name)(g)


XBC_COL0 = SSD_D_INNER // LANES
DT_COL0 = (SSD_D_INNER + SSD_CONV_DIM) // LANES


def _shift_down(v, k, row_ids):
    return jnp.where(row_ids >= k, pltpu.roll(v, k, axis=0), 0.0)


def _shift_up(v, k, row_ids):
    n = v.shape[0]
    return jnp.where(row_ids < n - k, pltpu.roll(v, n - k, axis=0), 0.0)


def _conv_pre(x, w, b, row_ids):
    pre = b + w[3:4, :] * x
    for k in (1, 2, 3):
        pre = pre + w[3 - k:4 - k, :] * _shift_down(x, k, row_ids)
    return pre


def _conv_fwd(zx, conv_w, conv_b, *, name, hook=None):
    t = zx.shape[0]
    nct = SSD_CONV_DIM // LANES
    hk = _HookSlots(hook, n_in=3, n_out=1, n_scratch=0)

    def body(*refs):
        (x_ref, w_ref, b_ref), (o_ref,), _ = hk.own(refs)
        if hook is not None:
            hk.run(refs, pl.program_id(0), nct)
        x = x_ref[...]
        row_ids = lax.broadcasted_iota(jnp.int32, x.shape, 0)
        pre = _conv_pre(x, w_ref[...], b_ref[...], row_ids)
        o_ref[...] = pre * _sigmoid(pre)

    outs = pl.pallas_call(
        body, grid=(nct,),
        in_specs=[pl.BlockSpec((t, LANES), lambda j: (0, XBC_COL0 + j)),
                  pl.BlockSpec((SSD_CONV_WIDTH, LANES), lambda j: (0, j)),
                  pl.BlockSpec((1, LANES), lambda j: (0, j))] + hk.in_specs,
        out_specs=[pl.BlockSpec((t, LANES), lambda j: (0, j))] + hk.out_specs,
        out_shape=[jax.ShapeDtypeStruct((t, SSD_CONV_DIM), F32)] + hk.out_shape,
        scratch_shapes=hk.scratch,
        compiler_params=_params(*hk.semantics("parallel")), name=name)(zx, conv_w, conv_b, *hk.inputs)
    return outs[0] if hook is None else (outs[0], outs[1:])


def _conv_bwd(zx, conv_w, conv_b, d_xs, d_bm, d_cm, *, name):
    t = zx.shape[0]
    nct = SSD_CONV_DIM // LANES
    n_xs = SSD_D_INNER // LANES
    n_bm = SSD_N_GROUPS * SSD_D_STATE // LANES

    def body(x_ref, w_ref, b_ref, dxs_ref, dbm_ref, dcm_ref, dx_ref, dw_ref, db_ref):
        x = x_ref[...]
        w = w_ref[...]
        j = pl.program_id(0)
        dy = jnp.where(j < n_xs, dxs_ref[...], jnp.where(j < n_xs + n_bm, dbm_ref[...], dcm_ref[...]))
        row_ids = lax.broadcasted_iota(jnp.int32, x.shape, 0)
        pre = _conv_pre(x, w, b_ref[...], row_ids)
        sg = _sigmoid(pre)
        dpre = dy * (sg * (1.0 + pre * (1.0 - sg)))
        dx = w[3:4, :] * dpre
        for k in (1, 2, 3):
            dx = dx + w[3 - k:4 - k, :] * _shift_up(dpre, k, row_ids)
        dx_ref[...] = dx.astype(dx_ref.dtype)
        db_ref[...] = jnp.sum(dpre, axis=0, keepdims=True)
        dw_ref[3:4, :] = jnp.sum(dpre * x, axis=0, keepdims=True)
        for k in (1, 2, 3):
            dw_ref[3 - k:4 - k, :] = jnp.sum(dpre * _shift_down(x, k, row_ids), axis=0, keepdims=True)

    col = pl.BlockSpec((t, LANES), lambda j: (0, j))
    clip = lambda j, lo, n: jnp.clip(j - lo, 0, n - 1)
    return pl.pallas_call(
        body, grid=(nct,),
        in_specs=[pl.BlockSpec((t, LANES), lambda j: (0, XBC_COL0 + j)),
                  pl.BlockSpec((SSD_CONV_WIDTH, LANES), lambda j: (0, j)),
                  pl.BlockSpec((1, LANES), lambda j: (0, j)),
                  pl.BlockSpec((t, LANES), lambda j: (0, clip(j, 0, n_xs))),
                  pl.BlockSpec((t, LANES), lambda j: (0, clip(j, n_xs, n_bm))),
                  pl.BlockSpec((t, LANES), lambda j: (0, clip(j, n_xs + n_bm, n_bm)))],
        out_specs=[col, pl.BlockSpec((SSD_CONV_WIDTH, LANES), lambda j: (0, j)), pl.BlockSpec((1, LANES), lambda j: (0, j))],
        out_shape=[jax.ShapeDtypeStruct((t, SSD_CONV_DIM), BF16),
                   jax.ShapeDtypeStruct((SSD_CONV_WIDTH, SSD_CONV_DIM), F32),
                   jax.ShapeDtypeStruct((1, SSD_CONV_DIM), F32)],
        compiler_params=_params("parallel"), name=name)(zx, conv_w, conv_b, d_xs, d_bm, d_cm)


def _softplus_fwd(zx, bias_row, alog_row, *, name):
    t = zx.shape[0]
    q = SSD_CHUNK
    tr = _row_tile(t, 1024)

    def body(x_ref, b_ref, al_ref, dt_ref, cum_ref):
        v = x_ref[...] + b_ref[...]
        e = jnp.exp(-jnp.abs(v))
        u = 1.0 + e
        log1p = jnp.where(u == 1.0, e, jnp.log(u) * (e / (u - 1.0)))
        dt = jnp.maximum(v, 0.0) + log1p
        dt_ref[...] = dt
        a = dt * -jnp.exp(al_ref[...])
        lower = (lax.broadcasted_iota(jnp.int32, (q, q), 1) <= lax.broadcasted_iota(jnp.int32, (q, q), 0)).astype(F32)
        cums = [lax.dot_general(lower, a[c * q:(c + 1) * q, :], ((((1,), (0,))), ((), ())), precision=lax.Precision.HIGHEST,
                                preferred_element_type=F32) for c in range(tr // q)]
        cum_ref[...] = jnp.concatenate(cums, axis=0)

    blk = pl.BlockSpec((tr, LANES), lambda i: (i, 0))
    vec = pl.BlockSpec((1, LANES), lambda i: (0, 0))
    return pl.pallas_call(
        body, grid=(t // tr,),
        in_specs=[pl.BlockSpec((tr, LANES), lambda i: (i, DT_COL0)), vec, vec],
        out_specs=[blk, blk],
        out_shape=[jax.ShapeDtypeStruct((t, LANES), F32), jax.ShapeDtypeStruct((t, LANES), F32)],
        compiler_params=_params("parallel"), name=name)(zx, bias_row, alog_row)


def _softplus_bwd(zx, bias_row, ddt, *, name):
    t = zx.shape[0]
    tr = _row_tile(t, 1024)

    def body(x_ref, b_ref, g_ref, o_ref, db_ref):
        v = x_ref[...] + b_ref[...]
        lane = lax.broadcasted_iota(jnp.int32, v.shape, 1)
        d = jnp.where(lane < SSD_N_HEADS, g_ref[...] * _sigmoid(v), 0.0)
        o_ref[...] = d.astype(o_ref.dtype)

        @pl.when(pl.program_id(0) == 0)
        def _():
            db_ref[...] = jnp.zeros_like(db_ref)

        db_ref[...] += jnp.sum(d, axis=0, keepdims=True)

    return pl.pallas_call(
        body, grid=(t // tr,),
        in_specs=[pl.BlockSpec((tr, LANES), lambda i: (i, DT_COL0)), pl.BlockSpec((1, LANES), lambda i: (0, 0)),
                  pl.BlockSpec((tr, LANES), lambda i: (i, 0))],
        out_specs=[pl.BlockSpec((tr, LANES), lambda i: (i, 0)), pl.BlockSpec((1, LANES), lambda i: (0, 0))],
        out_shape=[jax.ShapeDtypeStruct((t, LANES), BF16), jax.ShapeDtypeStruct((1, LANES), F32)],
        compiler_params=_params("arbitrary"), name=name)(zx, bias_row, ddt)


def _ssd_masks():
    q = SSD_CHUNK
    tt = lax.broadcasted_iota(jnp.int32, (q, q), 0)
    ss = lax.broadcasted_iota(jnp.int32, (q, q), 1)
    lane = lax.broadcasted_iota(jnp.int32, (1, SSD_GW), 1)
    srow = lax.broadcasted_iota(jnp.int32, (SSD_GW, 1), 0)
    hm = [(lane >= SSD_HEAD_DIM * j) & (lane < SSD_HEAD_DIM * (j + 1)) for j in range(SSD_HPG)]
    rm = [(srow >= SSD_HEAD_DIM * j) & (srow < SSD_HEAD_DIM * (j + 1)) for j in range(SSD_HPG)]
    return tt, ss, hm, rm


def _ssd_head_terms(dt_rows, cum_rows, a_rows, j, tt, ss):
    q = SSD_CHUNK
    dt_row = dt_rows[j:j + 1, :]
    dt_col = jnp.sum(jnp.where(tt == ss, dt_row, 0.0), axis=1, keepdims=True)
    a_row1 = a_rows[j:j + 1, :]
    a_11 = a_rows[j:j + 1, 0:1]
    cum_col = jnp.sum(jnp.where(ss <= tt, dt_row * a_row1, 0.0), axis=1, keepdims=True)
    cum_row = cum_rows[j:j + 1, :]
    decay = jnp.exp(jnp.where(ss <= tt, cum_col - cum_row, -jnp.inf))
    cum_last = cum_col[q - 1:q, :]
    e_col = jnp.exp(cum_col)
    dte_col = jnp.exp(cum_last - cum_col)
    e_last = jnp.exp(cum_last)
    return dt_col, dt_row, a_row1, a_11, decay, e_col, dte_col, e_last


SSD_CHUNKS_PER_STEP = 4
SSD_BC_COL0 = SSD_D_INNER // SSD_D_STATE


def _ssd_head_selects(terms, hm, rm):
    e_all = jnp.zeros((SSD_CHUNK, SSD_GW), F32)
    w_all = jnp.zeros((SSD_CHUNK, SSD_GW), F32)
    e_s = jnp.zeros((SSD_GW, 1), F32)
    for j in range(SSD_HPG):
        dt_col, _, _, _, _, e_col, dte_col, e_last = terms[j]
        e_all = jnp.where(hm[j], e_col, e_all)
        w_all = jnp.where(hm[j], dt_col * dte_col, w_all)
        e_s = jnp.where(rm[j], e_last, e_s)
    return e_all, w_all, e_s


def _ssd_fwd(xc, dtr, cumr, alog_b, d_b, *, name, hook=None):
    t = xc.shape[0]
    q = SSD_CHUNK
    nc = t // q
    kc = min(SSD_CHUNKS_PER_STEP, nc)
    rows = kc * q
    hk = _HookSlots(hook, n_in=7, n_out=2, n_scratch=1)

    def body(*refs):
        (x_ref, b_ref, c_ref, dtr_ref, cumr_ref, alog_ref, d_ref), (y_ref, st_ref), (s_scr,) = hk.own(refs)
        if hook is not None:
            hk.run(refs, pl.program_id(0) * (nc // kc) + pl.program_id(1), SSD_N_GROUPS * (nc // kc))

        @pl.when(pl.program_id(1) == 0)
        def _():
            s_scr[...] = jnp.zeros_like(s_scr)

        tt, ss, hm, rm = _ssd_masks()
        a_rows = -jnp.exp(alog_ref[...])
        d_rows = d_ref[...]
        d_all = jnp.zeros((1, SSD_GW), F32)
        for j in range(SSD_HPG):
            d_all = jnp.where(hm[j], d_rows[j:j + 1, 0:1], d_all)
        ks, hs = range(kc), range(SSD_HPG)
        sl = [pl.ds(k * q, q) for k in ks]
        x = [x_ref[sl[k], :] for k in ks]
        bm = [b_ref[sl[k], :].astype(BF16) for k in ks]
        cm = [c_ref[sl[k], :].astype(BF16) for k in ks]
        xb = [x[k].astype(BF16) for k in ks]
        terms = [[_ssd_head_terms(dtr_ref[:, sl[k]], cumr_ref[:, sl[k]], a_rows, j, tt, ss) for j in hs] for k in ks]
        g = [_dot_nt(cm[k], bm[k]) for k in ks]
        m = [[(g[k] * terms[k][j][4] * terms[k][j][1]).astype(BF16) for j in hs] for k in ks]
        yj = [[_dot_nn(m[k][j], xb[k]) for j in hs] for k in ks]
        sel = [_ssd_head_selects(terms[k], hm, rm) for k in ks]
        upd = [_dot_tn((x[k] * sel[k][1]).astype(BF16), bm[k]) for k in ks]
        states = [s_scr[...]]
        for k in ks:
            states.append(states[k] * sel[k][2] + upd[k])
        inter = [_dot_nt(cm[k], states[k].astype(BF16)) for k in ks]
        ys = []
        for k in ks:
            y = jnp.zeros((q, SSD_GW), F32)
            for j in hs:
                y = jnp.where(hm[j], yj[k][j], y)
            ys.append(y + inter[k] * sel[k][0] + x[k] * d_all)
        for k in ks:
            st_ref[k] = states[k]
        y_ref[...] = jnp.concatenate(ys, axis=0)
        s_scr[...] = states[kc]

    blk = lambda width, off: pl.BlockSpec((rows, width), lambda g, c: (c, off + g))
    par_s = pl.BlockSpec((None, SSD_HPG, LANES), lambda g, c: (g, 0, 0))
    row_s = pl.BlockSpec((None, SSD_HPG, rows), lambda g, c: (g, 0, c))
    outs = pl.pallas_call(
        body, grid=(SSD_N_GROUPS, nc // kc),
        in_specs=[blk(SSD_GW, 0), blk(SSD_D_STATE, SSD_BC_COL0), blk(SSD_D_STATE, SSD_BC_COL0 + SSD_N_GROUPS),
                  row_s, row_s, par_s, par_s] + hk.in_specs,
        out_specs=[blk(SSD_GW, 0), pl.BlockSpec((None, kc, SSD_GW, SSD_D_STATE), lambda g, c: (g, c, 0, 0))] + hk.out_specs,
        out_shape=[jax.ShapeDtypeStruct((t, SSD_D_INNER), F32),
                   jax.ShapeDtypeStruct((SSD_N_GROUPS, nc, SSD_GW, SSD_D_STATE), F32)] + hk.out_shape,
        scratch_shapes=[pltpu.VMEM((SSD_GW, SSD_D_STATE), F32)] + hk.scratch,
        compiler_params=_params(*hk.semantics("parallel", "arbitrary")), name=name)(
            xc, xc, xc, dtr, cumr, alog_b, d_b, *hk.inputs)
    return outs if hook is None else (outs[:2], outs[2:])


def _ssd_bwd(xc, dtr, cumr, alog_b, d_b, states, dy, *, name, hook=None):
    t = xc.shape[0]
    q = SSD_CHUNK
    nc = t // q
    kc = min(SSD_CHUNKS_PER_STEP, nc)
    nst = nc // kc
    rows = kc * q
    rev = lambda c: nst - 1 - c
    hk = _HookSlots(hook, n_in=9, n_out=5, n_scratch=1)

    def body(*refs):
        ((x_ref, b_ref, c_ref, dtr_ref, cumr_ref, alog_ref, d_ref, st_ref, dy_ref),
         (dx_ref, db_ref, dc_ref, ddt_ref, dpar_ref), (ds_scr,)) = hk.own(refs)
        if hook is not None:
            hk.run(refs, pl.program_id(0) * nst + pl.program_id(1), SSD_N_GROUPS * nst)

        @pl.when(pl.program_id(1) == 0)
        def _():
            ds_scr[...] = jnp.zeros_like(ds_scr)
            dpar_ref[...] = jnp.zeros_like(dpar_ref)

        tt, ss, hm, rm = _ssd_masks()
        tcol = lax.broadcasted_iota(jnp.int32, (q, 1), 0)
        lane = lax.broadcasted_iota(jnp.int32, (1, LANES), 1)
        a_rows = -jnp.exp(alog_ref[...])
        d_rows = d_ref[...]
        d_all = jnp.zeros((1, SSD_GW), F32)
        for j in range(SSD_HPG):
            d_all = jnp.where(hm[j], d_rows[j:j + 1, 0:1], d_all)
        ks, hs = range(kc), range(SSD_HPG)
        sl = [pl.ds(k * q, q) for k in ks]
        x = [x_ref[sl[k], :] for k in ks]
        dyv = [dy_ref[sl[k], :] for k in ks]
        bm = [b_ref[sl[k], :].astype(BF16) for k in ks]
        cm = [c_ref[sl[k], :].astype(BF16) for k in ks]
        s_in = [st_ref[k] for k in ks]
        xb = [x[k].astype(BF16) for k in ks]
        dyb = [dyv[k].astype(BF16) for k in ks]
        s_b = [s_in[k].astype(BF16) for k in ks]
        terms = [[_ssd_head_terms(dtr_ref[:, sl[k]], cumr_ref[:, sl[k]], a_rows, j, tt, ss) for j in hs] for k in ks]
        sel = [_ssd_head_selects(terms[k], hm, rm) for k in ks]
        e_all, w_all, e_s = [s_[0] for s_ in sel], [s_[1] for s_ in sel], [s_[2] for s_ in sel]
        dye = [(dyv[k] * e_all[k]).astype(BF16) for k in ks]
        ds_loc = [_dot_tn(dye[k], cm[k]) for k in ks]
        ds = [None] * kc
        running = ds_scr[...]
        for k in reversed(ks):
            ds[k] = running
            running = running * e_s[k] + ds_loc[k]
        ds_scr[...] = running
        ds_b = [ds[k].astype(BF16) for k in ks]
        g = [_dot_nt(cm[k], bm[k]) for k in ks]
        cs = [_dot_nt(cm[k], s_b[k]) for k in ks]
        bds = [_dot_nt(bm[k], ds_b[k]) for k in ks]
        dm = [[_dot_nt(jnp.where(hm[j], dyv[k], 0.0).astype(BF16), xb[k]) for j in hs] for k in ks]
        gl = [[g[k] * terms[k][j][4] for j in hs] for k in ks]
        wp = [[dm[k][j] * gl[k][j] for j in hs] for k in ks]
        mt = [[(gl[k][j] * terms[k][j][1]).astype(BF16) for j in hs] for k in ks]
        dxj = [[_dot_tn(mt[k][j], dyb[k]) for j in hs] for k in ks]
        dg = []
        for k in ks:
            acc = jnp.zeros((q, q), F32)
            for j in hs:
                acc = acc + dm[k][j] * terms[k][j][4] * terms[k][j][1]
            dg.append(acc.astype(BF16))
        dy_cs = [dyv[k] * cs[k] for k in ks]
        x_bds = [x[k] * bds[k] for k in ks]
        dy_x = [dyv[k] * x[k] for k in ks]
        ds_s = [ds[k] * s_in[k] for k in ks]
        w = [[wp[k][j] * terms[k][j][1] for j in hs] for k in ks]
        rw_col = [[jnp.sum(w[k][j], axis=1, keepdims=True) for j in hs] for k in ks]
        cw_row = [[jnp.sum(w[k][j], axis=0, keepdims=True) for j in hs] for k in ks]
        cwp_row = [[jnp.sum(wp[k][j], axis=0, keepdims=True) for j in hs] for k in ks]
        r1_col = [[jnp.sum(jnp.where(hm[j], dy_cs[k], 0.0), axis=1, keepdims=True) * terms[k][j][5] for j in hs] for k in ks]
        dw_col = [[jnp.sum(jnp.where(hm[j], x_bds[k], 0.0), axis=1, keepdims=True) for j in hs] for k in ks]
        head_rows = [slice(j * SSD_HEAD_DIM, (j + 1) * SSD_HEAD_DIM) for j in hs]
        lane_sum = lambda v: jnp.sum(v, axis=1, keepdims=True)
        s_sum = [[lane_sum(jnp.sum(ds_s[k][head_rows[j], :], axis=0, keepdims=True)) for j in hs] for k in ks]
        dy_x_cols = [jnp.sum(dy_x[k], axis=0, keepdims=True) for k in ks]
        d_d = [[lane_sum(jnp.where(hm[j], dy_x_cols[k], 0.0)) for j in hs] for k in ks]
        ddt_rows = [[None] * SSD_HPG for _ in ks]
        dpar = [jnp.zeros((1, LANES), F32) for _ in hs]
        for k in ks:
            for j in hs:
                dt_col, dt_row, a_row1, a_11, _, _, dte_col, e_last = terms[k][j]
                dww = dw_col[k][j] * (dt_col * dte_col)
                last_add = jnp.sum(dww, axis=0, keepdims=True) + e_last * s_sum[k][j]
                dcum_col = rw_col[k][j] + r1_col[k][j] - dww + jnp.where(tcol == q - 1, last_add, 0.0)
                da_row = jnp.sum(jnp.where(tt >= ss, dcum_col, 0.0), axis=0, keepdims=True)
                da_col = jnp.sum(jnp.where(ss >= tt, -cw_row[k][j], 0.0), axis=1, keepdims=True)
                ddt_col = a_11 * da_col + dw_col[k][j] * dte_col
                ddt_rows[k][j] = (a_row1 * da_row + cwp_row[k][j]
                                  + jnp.sum(jnp.where(tt == ss, ddt_col, 0.0), axis=0, keepdims=True))
                d_a = jnp.sum(dt_row * da_row, axis=1, keepdims=True) + jnp.sum(dt_col * da_col, axis=0, keepdims=True)
                dpar[j] = dpar[j] + jnp.where(lane == 0, d_a * a_11, 0.0) + jnp.where(lane == 1, d_d[k][j], 0.0)
        dxs = []
        for k in ks:
            acc = jnp.zeros((q, SSD_GW), F32)
            for j in hs:
                acc = jnp.where(hm[j], dxj[k][j], acc)
            dxs.append(acc + w_all[k] * bds[k] + d_all * dyv[k])
        xw = [(x[k] * w_all[k]).astype(BF16) for k in ks]
        dc = [_dot_nn(dg[k], bm[k]) + _dot_nn(dye[k], s_b[k]) for k in ks]
        db = [_dot_tn(dg[k], cm[k]) + _dot_nn(xw[k], ds_b[k]) for k in ks]
        dx_ref[...] = jnp.concatenate(dxs, axis=0)
        dc_ref[...] = jnp.concatenate(dc, axis=0)
        db_ref[...] = jnp.concatenate(db, axis=0)
        ddt_ref[...] = jnp.concatenate([jnp.concatenate([ddt_rows[k][j] for k in ks], axis=1) for j in hs], axis=0)
        dpar_ref[...] += jnp.concatenate(dpar, axis=0)

    blk = lambda width, off: pl.BlockSpec((rows, width), lambda g, c: (rev(c), off + g))
    par_s = pl.BlockSpec((None, SSD_HPG, LANES), lambda g, c: (g, 0, 0))
    outs = pl.pallas_call(
        body, grid=(SSD_N_GROUPS, nst),
        in_specs=[blk(SSD_GW, 0), blk(SSD_D_STATE, SSD_BC_COL0), blk(SSD_D_STATE, SSD_BC_COL0 + SSD_N_GROUPS),
                  pl.BlockSpec((None, SSD_HPG, rows), lambda g, c: (g, 0, rev(c))),
                  pl.BlockSpec((None, SSD_HPG, rows), lambda g, c: (g, 0, rev(c))), par_s, par_s,
                  pl.BlockSpec((None, kc, SSD_GW, SSD_D_STATE), lambda g, c: (g, rev(c), 0, 0)), blk(SSD_GW, 0)] + hk.in_specs,
        out_specs=[blk(SSD_GW, 0), blk(SSD_D_STATE, 0), blk(SSD_D_STATE, 0),
                   pl.BlockSpec((None, SSD_HPG, rows), lambda g, c: (g, 0, rev(c))), par_s] + hk.out_specs,
        out_shape=[jax.ShapeDtypeStruct((t, SSD_D_INNER), F32),
                   jax.ShapeDtypeStruct((t, SSD_N_GROUPS * SSD_D_STATE), F32),
                   jax.ShapeDtypeStruct((t, SSD_N_GROUPS * SSD_D_STATE), F32),
                   jax.ShapeDtypeStruct((SSD_N_GROUPS, SSD_HPG, t), F32),
                   jax.ShapeDtypeStruct((SSD_N_GROUPS, SSD_HPG, LANES), F32)] + hk.out_shape,
        scratch_shapes=[pltpu.VMEM((SSD_GW, SSD_D_STATE), F32)] + hk.scratch,
        compiler_params=_params(*hk.semantics("parallel", "arbitrary")), name=name)(
            xc, xc, xc, dtr, cumr, alog_b, d_b, states, dy, *hk.inputs)
    return outs if hook is None else (outs[:5], outs[5:])


def _gate_norm_fwd(y, zx, norm_w, *, name):
    t = y.shape[0]
    tr = _row_tile(t, 256)
    row = pl.BlockSpec((tr, SSD_D_INNER), lambda i: (i, 0))

    def body(y_ref, z_ref, w_ref, o_ref):
        for gi in range(SSD_N_GROUPS):
            sl = pl.ds(gi * SSD_GW, SSD_GW)
            z = z_ref[:, sl]
            gv = y_ref[:, sl] * (z * _sigmoid(z))
            r = lax.rsqrt(jnp.mean(gv * gv, axis=-1, keepdims=True) + NORM_EPS)
            o_ref[:, sl] = (gv * r * w_ref[:, sl]).astype(BF16)

    return pl.pallas_call(
        body, grid=(t // tr,), in_specs=[row, row, pl.BlockSpec((1, SSD_D_INNER), lambda i: (0, 0))],
        out_specs=row, out_shape=jax.ShapeDtypeStruct((t, SSD_D_INNER), BF16),
        compiler_params=_params("parallel"), name=name)(y, zx, norm_w)


def _gate_norm_bwd(y, zx, norm_w, dyn, *, name):
    t = y.shape[0]
    tr = _row_tile(t, 256)
    row = pl.BlockSpec((tr, SSD_D_INNER), lambda i: (i, 0))
    vec = pl.BlockSpec((1, SSD_D_INNER), lambda i: (0, 0))

    def body(y_ref, z_ref, w_ref, dyn_ref, dy_ref, dz_ref, dw_ref):
        @pl.when(pl.program_id(0) == 0)
        def _():
            dw_ref[...] = jnp.zeros_like(dw_ref)

        for gi in range(SSD_N_GROUPS):
            sl = pl.ds(gi * SSD_GW, SSD_GW)
            z = z_ref[:, sl]
            yv = y_ref[:, sl]
            sg = _sigmoid(z)
            sz = z * sg
            gv = yv * sz
            r = lax.rsqrt(jnp.mean(gv * gv, axis=-1, keepdims=True) + NORM_EPS)
            ghat = gv * r
            dout = dyn_ref[:, sl].astype(F32)
            dgh = dout * w_ref[:, sl]
            dgv = r * (dgh - ghat * jnp.mean(dgh * ghat, axis=-1, keepdims=True))
            dy_ref[:, sl] = dgv * sz
            dz_ref[:, sl] = (dgv * yv * (sg * (1.0 + z * (1.0 - sg)))).astype(dz_ref.dtype)
            dw_ref[:, sl] += jnp.sum(dout * ghat, axis=0, keepdims=True)

    return pl.pallas_call(
        body, grid=(t // tr,), in_specs=[row, row, vec, row], out_specs=[row, row, vec],
        out_shape=[jax.ShapeDtypeStruct((t, SSD_D_INNER), F32), jax.ShapeDtypeStruct((t, SSD_D_INNER), BF16),
                   jax.ShapeDtypeStruct((1, SSD_D_INNER), F32)],
        compiler_params=_params("arbitrary"), name=name)(y, zx, norm_w, dyn)


ATTN_KV_W = ATTN_N_KV * ATTN_HEAD_DIM
ATTN_Q_HALF = 512
ATTN_K_BLK = ATTN_N_Q * ATTN_HEAD_DIM // ATTN_KV_W
ATTN_V_BLK = ATTN_K_BLK + 1


def _attn_valid(first_block):
    w = ATTN_WINDOW
    qpos = lax.broadcasted_iota(jnp.int32, (w, 2 * w), 0) + w
    kpos = lax.broadcasted_iota(jnp.int32, (w, 2 * w), 1)
    rel = qpos - kpos
    return (rel >= 0) & (rel < w) & jnp.logical_not(first_block & (kpos < w))


def _attn_head_views(lo_ref, hi_ref):
    hd = ATTN_HEAD_DIM
    per_half = ATTN_Q_HALF // hd
    return [(lo_ref if h < per_half else hi_ref)[:, pl.ds((h % per_half) * hd, hd)] for h in range(ATTN_N_Q)]


def _attn_block_views(lo_ref, hi_ref, kc_ref, kp_ref, vc_ref, vp_ref):
    hd = ATTN_HEAD_DIM
    kv_cols = [pl.ds(kh * hd, hd) for kh in range(ATTN_N_KV)]
    kb = [jnp.concatenate([kp_ref[:, c], kc_ref[:, c]], axis=0) for c in kv_cols]
    vb = [jnp.concatenate([vp_ref[:, c], vc_ref[:, c]], axis=0) for c in kv_cols]
    return _attn_head_views(lo_ref, hi_ref), kb, vb


def _attn_scores(q, kb, valid):
    scale = ATTN_HEAD_DIM ** -0.5
    return [jnp.where(valid, _dot_nt(q[h], kb[h // ATTN_REP]) * scale, -jnp.inf) for h in range(ATTN_N_Q)]


def _attn_softmax(s, sink):
    heads = range(ATTN_N_Q)
    m = [jnp.maximum(jnp.max(s[h], axis=1, keepdims=True), sink[h]) for h in heads]
    e = [jnp.exp(s[h] - m[h]) for h in heads]
    es = [jnp.exp(sink[h] - m[h]) for h in heads]
    inv = [1.0 / (jnp.sum(e[h], axis=1, keepdims=True) + es[h]) for h in heads]
    return e, es, inv


def _attn_fwd(qkv, sinks_b, *, name):
    t = qkv.shape[0]
    w = ATTN_WINDOW
    nb = t // w
    prev = lambda n: jnp.maximum(n - 1, 0)

    def body(qlo_ref, qhi_ref, kc_ref, kp_ref, vc_ref, vp_ref, sink_ref, o_ref):
        heads = range(ATTN_N_Q)
        q, kb, vb = _attn_block_views(qlo_ref, qhi_ref, kc_ref, kp_ref, vc_ref, vp_ref)
        sink = [sink_ref[h:h + 1, 0:1] for h in heads]
        e, _, inv = _attn_softmax(_attn_scores(q, kb, _attn_valid(pl.program_id(0) == 0)), sink)
        out = [_dot_nn((e[h] * inv[h]).astype(BF16), vb[h // ATTN_REP]).astype(o_ref.dtype) for h in heads]
        o_ref[...] = jnp.concatenate(out, axis=1)

    qh = lambda half: pl.BlockSpec((w, ATTN_Q_HALF), lambda n: (n, half))
    kv = lambda blk, idx: pl.BlockSpec((w, ATTN_KV_W), lambda n: (idx(n), blk))
    cur = lambda n: n
    return pl.pallas_call(
        body, grid=(nb,),
        in_specs=[qh(0), qh(1), kv(ATTN_K_BLK, cur), kv(ATTN_K_BLK, prev), kv(ATTN_V_BLK, cur), kv(ATTN_V_BLK, prev),
                  pl.BlockSpec((ATTN_N_Q, LANES), lambda n: (0, 0))],
        out_specs=pl.BlockSpec((w, D_MODEL), lambda n: (n, 0)),
        out_shape=jax.ShapeDtypeStruct((t, D_MODEL), BF16),
        compiler_params=_params("parallel"), name=name)(qkv, qkv, qkv, qkv, qkv, qkv, sinks_b)


def _attn_bwd(qkv, sinks_b, dout, *, name):
    t = qkv.shape[0]
    w = ATTN_WINDOW
    nb = t // w
    hd = ATTN_HEAD_DIM
    clamp = lambda n: jnp.minimum(n, nb - 1)
    prev = lambda n: jnp.maximum(clamp(n) - 1, 0)

    def body(qlo_ref, qhi_ref, kc_ref, kp_ref, vc_ref, vp_ref, sink_ref, dolo_ref, dohi_ref,
             dq_ref, dkv_ref, dsink_ref, carry):
        n = pl.program_id(0)

        @pl.when(n == 0)
        def _():
            carry[...] = jnp.zeros_like(carry)
            dsink_ref[...] = jnp.zeros_like(dsink_ref)

        @pl.when(n < nb)
        def _():
            heads, kvs = range(ATTN_N_Q), range(ATTN_N_KV)
            q, kb, vb = _attn_block_views(qlo_ref, qhi_ref, kc_ref, kp_ref, vc_ref, vp_ref)
            do = _attn_head_views(dolo_ref, dohi_ref)
            sink = [sink_ref[h:h + 1, 0:1] for h in heads]
            s = _attn_scores(q, kb, _attn_valid(n == 0))
            dp = [_dot_nt(do[h], vb[h // ATTN_REP]) for h in heads]
            e, es, inv = _attn_softmax(s, sink)
            p = [e[h] * inv[h] for h in heads]
            delta = [jnp.sum(p[h] * dp[h], axis=1, keepdims=True) for h in heads]
            dsc = [(p[h] * (dp[h] - delta[h]) * (hd ** -0.5)).astype(BF16) for h in heads]
            pb = [p[h].astype(BF16) for h in heads]
            dq = [_dot_nn(dsc[h], kb[h // ATTN_REP]).astype(dq_ref.dtype) for h in heads]
            stack = lambda per_head, kh: jnp.concatenate(per_head[kh * ATTN_REP:(kh + 1) * ATTN_REP], axis=0)
            dkb = [_dot_tn(stack(dsc, kh), stack(q, kh)) for kh in kvs]
            dvb = [_dot_tn(stack(pb, kh), stack(do, kh)) for kh in kvs]
            dsink = [jnp.broadcast_to(jnp.sum(-es[h] * inv[h] * delta[h], axis=0, keepdims=True), (1, LANES)) for h in heads]
            dq_ref[...] = jnp.concatenate(dq, axis=1)
            dsink_ref[...] += jnp.concatenate(dsink, axis=0)
            dkv_ref[...] = (carry[...] + jnp.concatenate([d[0:w, :] for d in dkb + dvb], axis=1)).astype(dkv_ref.dtype)
            carry[...] = jnp.concatenate([d[w:2 * w, :] for d in dkb + dvb], axis=1)

        @pl.when(n == nb)
        def _():
            dkv_ref[...] = carry[...].astype(dkv_ref.dtype)

    qh = lambda half: pl.BlockSpec((w, ATTN_Q_HALF), lambda n: (clamp(n), half))
    kv = lambda blk, idx: pl.BlockSpec((w, ATTN_KV_W), lambda n: (idx(n), blk))
    return pl.pallas_call(
        body, grid=(nb + 1,),
        in_specs=[qh(0), qh(1), kv(ATTN_K_BLK, clamp), kv(ATTN_K_BLK, prev), kv(ATTN_V_BLK, clamp), kv(ATTN_V_BLK, prev),
                  pl.BlockSpec((ATTN_N_Q, LANES), lambda n: (0, 0)), qh(0), qh(1)],
        out_specs=[pl.BlockSpec((w, D_MODEL), lambda n: (clamp(n), 0)),
                   pl.BlockSpec((w, 2 * ATTN_KV_W), lambda n: (jnp.maximum(n - 1, 0), 0)),
                   pl.BlockSpec((ATTN_N_Q, LANES), lambda n: (0, 0))],
        out_shape=[jax.ShapeDtypeStruct((t, D_MODEL), BF16), jax.ShapeDtypeStruct((t, 2 * ATTN_KV_W), BF16),
                   jax.ShapeDtypeStruct((ATTN_N_Q, LANES), F32)],
        scratch_shapes=[pltpu.VMEM((w, 2 * ATTN_KV_W), F32)],
        compiler_params=_params("arbitrary"), name=name)(qkv, qkv, qkv, qkv, qkv, qkv, sinks_b, dout, dout)


def _sq_relu_epilogue(acc):
    r = jnp.maximum(acc, 0.0)
    return (r * r,)


def _sq_relu_bwd_epilogue(acc, act):
    return (acc * (2.0 * jnp.sqrt(act.astype(F32))),)


def _bias_epilogue(acc, bias):
    return (acc + bias,)


def _plain_run(stage, fn, *args, **kwargs):
    return fn(*args, **kwargs)


def _mlp_fwd(u, w_up, w_down, tag, run=_plain_run):
    act = run(f"mlp_up_{tag}", _matmul, u, w_up, mode="nn", out_dtypes=(BF16,), epilogue=_sq_relu_epilogue, b_shards=True,
              name=f"mlp_up_{tag}")
    f = run(f"mlp_down_{tag}", _matmul, act, w_down, mode="nn", out_dtypes=(F32,), name=f"mlp_down_{tag}")
    return act, f


def _mlp_bwd(u, act, w_up, w_down, df, tag):
    dpre = _matmul(df, w_down, mode="nt", out_dtypes=(BF16,), epilogue=_sq_relu_bwd_epilogue,
                   extras=((act, "tile"),), name=f"mlp_dact_{tag}")
    dw_down = _matmul(act, df, mode="tn", out_dtypes=(BF16,), name=f"mlp_dwdown_{tag}")
    du = _matmul(dpre, w_up, mode="nt", out_dtypes=(F32,), b_shards=True, name=f"mlp_du_{tag}")
    dw_up = _matmul(u, dpre, mode="tn", out_dtypes=(BF16,), out_shards=True, name=f"mlp_dwup_{tag}")
    return du, dw_up, dw_down


def _group_rows(dt):
    t = dt.shape[0]
    return jnp.transpose(dt[:, :SSD_N_HEADS].reshape(t, SSD_N_GROUPS, SSD_HPG), (1, 2, 0))


def _head_param_rows(p):
    return jnp.broadcast_to(p.reshape(SSD_N_GROUPS, SSD_HPG, 1), (SSD_N_GROUPS, SSD_HPG, LANES))


def _local_step(x, target, wts, comm=None):
    t = x.shape[0]
    wts = dict(wts)
    row = lambda v: v.reshape(1, -1)
    mix_pre, mix_post, ffn_pre, ffn_post = wts["mix_pre_norm"], wts["mix_post_norm"], wts["ffn_pre_norm"], wts["ffn_post_norm"]

    def gathering(stage, fn, *args, **kwargs):
        hook = comm.gather_hook(stage) if comm is not None else None
        if hook is None:
            return fn(*args, **kwargs)
        out, got = fn(*args, hook=hook, **kwargs)
        wts.update(comm.weights_from(stage, got))
        return out

    u0 = _rms_fwd(x, row(mix_pre[0]), name="rms_pre_mix0")
    zx = gathering("in_proj", _matmul, u0, wts["ssd_w_in"], mode="nn", out_dtypes=(F32,), tn=896, name="ssd_in_proj")
    xc = gathering("conv", _conv_fwd, zx, wts["ssd_conv_w"], row(wts["ssd_conv_b"]), name="ssd_conv_fwd")
    bias_row = jnp.pad(wts["ssd_dt_bias"], (0, LANES - SSD_N_HEADS)).reshape(1, LANES)
    alog_row = jnp.pad(wts["ssd_a_log"], (0, LANES - SSD_N_HEADS)).reshape(1, LANES)
    dt, cum = _softplus_fwd(zx, bias_row, alog_row, name="ssd_dt_fwd")
    dtr, cumr = _group_rows(dt), _group_rows(cum)
    alog_b, d_b = _head_param_rows(wts["ssd_a_log"]), _head_param_rows(wts["ssd_d"])
    y_ssd, states = gathering("scan", _ssd_fwd, xc, dtr, cumr, alog_b, d_b, name="ssd_scan_fwd")
    norm_w = row(wts["ssd_norm_w"])
    yn = _gate_norm_fwd(y_ssd, zx, norm_w, name="ssd_gate_norm_fwd")
    mix0 = _matmul(yn, wts["ssd_w_out"], mode="nn", out_dtypes=(F32,), name="ssd_out_proj")
    h1, v0 = _rms_fwd(mix0, row(mix_post[0]), resid=x, want_u=row(ffn_pre[0]), name="rms_post_mix0")
    act0, f0 = _mlp_fwd(v0, wts["mlp_w_up0"], wts["mlp_w_down0"], "l0", run=gathering)
    h2, u1 = _rms_fwd(f0, row(ffn_post[0]), resid=h1, want_u=row(mix_pre[1]), name="rms_post_ffn0")

    qkv = _matmul(u1, wts["attn_w_qkv"], mode="nn", out_dtypes=(BF16,), epilogue=_bias_epilogue,
                  extras=((row(wts["attn_b_qkv"]), "row"),), b_shards=True, name="attn_qkv_proj")
    sinks_b = jnp.broadcast_to(wts["attn_sinks"].reshape(ATTN_N_Q, 1), (ATTN_N_Q, LANES))
    ao = _attn_fwd(qkv, sinks_b, name="attn_fwd")
    mix1 = _matmul(ao, wts["attn_w_o"], mode="nn", out_dtypes=(F32,), epilogue=_bias_epilogue,
                   extras=((row(wts["attn_b_o"]), "row"),), name="attn_out_proj")
    h3, v1 = _rms_fwd(mix1, row(mix_post[1]), resid=h2, want_u=row(ffn_pre[1]), name="rms_post_mix1")
    act1, f1 = _mlp_fwd(v1, wts["mlp_w_up1"], wts["mlp_w_down1"], "l1")
    h4 = _rms_fwd(f1, row(ffn_post[1]), resid=h3, name="rms_post_ffn1")

    dh4, loss_tile = _loss_head(h4, target, name="loss_head")

    df1, g_ffn_post1 = _rms_bwd(f1, row(ffn_post[1]), dh4, out_dtype=BF16, name="rms_post_ffn1_bwd")
    dv1, g_up1, g_down1 = _mlp_bwd(v1, act1, wts["mlp_w_up1"], wts["mlp_w_down1"], df1, "l1")
    dh3, g_ffn_pre1 = _rms_bwd(h3, row(ffn_pre[1]), dv1, resid=dh4, name="rms_pre_ffn1_bwd")
    dmix1, g_mix_post1 = _rms_bwd(mix1, row(mix_post[1]), dh3, out_dtype=BF16, name="rms_post_mix1_bwd")
    g_b_o = _col_sum(dmix1, name="attn_bo_grad")
    g_w_o = _matmul(ao, dmix1, mode="tn", out_dtypes=(BF16,), name="attn_dwo")
    dao = _matmul(dmix1, wts["attn_w_o"], mode="nt", out_dtypes=(BF16,), name="attn_dao")
    dq, dkv, g_sinks = _attn_bwd(qkv, sinks_b, dao, name="attn_bwd")
    dqkv = jnp.concatenate([dq, dkv], axis=1)
    g_b_qkv = _col_sum(dqkv, name="attn_bqkv_grad")
    g_w_qkv = _matmul(u1, dqkv, mode="tn", out_dtypes=(BF16,), tn=ATTN_QKV // N_CHIPS, out_shards=True, name="attn_dwqkv")
    du1 = _matmul(dqkv, wts["attn_w_qkv"], mode="nt", out_dtypes=(F32,), b_shards=True, name="attn_du")
    dh2, g_mix_pre1 = _rms_bwd(h2, row(mix_pre[1]), du1, resid=dh3, name="rms_pre_mix1_bwd")

    df0, g_ffn_post0 = _rms_bwd(f0, row(ffn_post[0]), dh2, out_dtype=BF16, name="rms_post_ffn0_bwd")
    dv0, g_up0, g_down0 = _mlp_bwd(v0, act0, wts["mlp_w_up0"], wts["mlp_w_down0"], df0, "l0")
    dh1, g_ffn_pre0 = _rms_bwd(h1, row(ffn_pre[0]), dv0, resid=dh2, name="rms_pre_ffn0_bwd")
    dmix0, g_mix_post0 = _rms_bwd(mix0, row(mix_post[0]), dh1, out_dtype=BF16, name="rms_post_mix0_bwd")
    g_w_out = _matmul(yn, dmix0, mode="tn", out_dtypes=(BF16,), name="ssd_dwout")
    dyn = _matmul(dmix0, wts["ssd_w_out"], mode="nt", out_dtypes=(BF16,), name="ssd_dyn")
    dy_ssd, dz, g_norm_w = _gate_norm_bwd(y_ssd, zx, norm_w, dyn, name="ssd_gate_norm_bwd")
    mats = {"ssd_w_out": g_w_out, "attn_w_qkv": g_w_qkv, "attn_w_o": g_w_o,
            "mlp_w_up0": g_up0, "mlp_w_up1": g_up1, "mlp_w_down0": g_down0, "mlp_w_down1": g_down1}
    if comm is None:
        dxc, dbm, dcm, ddt_r, dpar = _ssd_bwd(xc, dtr, cumr, alog_b, d_b, states, dy_ssd, name="ssd_scan_bwd")
    else:
        (dxc, dbm, dcm, ddt_r, dpar), received = _ssd_bwd(xc, dtr, cumr, alog_b, d_b, states, dy_ssd,
                                                          name="ssd_scan_bwd", hook=comm.exchange_hook(mats, "early"))
        comm.received(received)
    dxbc, g_conv_w, g_conv_b = _conv_bwd(zx, wts["ssd_conv_w"], row(wts["ssd_conv_b"]), dxc, dbm, dcm, name="ssd_conv_bwd")
    ddt = jnp.pad(jnp.transpose(ddt_r, (2, 0, 1)).reshape(t, SSD_N_HEADS), ((0, 0), (0, LANES - SSD_N_HEADS)))
    ddt_raw, g_dt_bias = _softplus_bwd(zx, bias_row, ddt, name="ssd_dt_bwd")
    dzx = jnp.concatenate([dz, dxbc, ddt_raw], axis=1)
    g_w_in = _w_in_to_shards(_matmul(u0, dzx, mode="tn", out_dtypes=(F32,), tn=896, name="ssd_dwin"), name="ssd_dwin_shards")
    mats["ssd_w_in"] = g_w_in
    if comm is None:
        du0 = _matmul(dzx, wts["ssd_w_in"], mode="nt", out_dtypes=(F32,), tk=896, name="ssd_du")
    else:
        du0, received = _matmul(dzx, wts["ssd_w_in"], mode="nt", out_dtypes=(F32,), tk=896, name="ssd_du",
                                hook=comm.exchange_hook(mats, "late"))
        comm.received(received)
    grad_x, g_mix_pre0 = _rms_bwd(x, row(mix_pre[0]), du0, resid=dh1, name="rms_pre_mix0_bwd")

    dpar = dpar.reshape(SSD_N_HEADS, LANES)
    vecs = {
        "ssd_conv_w": g_conv_w, "ssd_conv_b": g_conv_b.reshape(-1),
        "ssd_dt_bias": g_dt_bias[0, :SSD_N_HEADS], "ssd_a_log": dpar[:, 0], "ssd_d": dpar[:, 1],
        "ssd_norm_w": g_norm_w.reshape(-1), "attn_b_qkv": g_b_qkv.reshape(-1), "attn_sinks": g_sinks[:, 0],
        "attn_b_o": g_b_o.reshape(-1),
        "mix_pre_norm": jnp.concatenate([g_mix_pre0, g_mix_pre1]), "mix_post_norm": jnp.concatenate([g_mix_post0, g_mix_post1]),
        "ffn_pre_norm": jnp.concatenate([g_ffn_pre0, g_ffn_pre1]), "ffn_post_norm": jnp.concatenate([g_ffn_post0, g_ffn_post1]),
    }
    return loss_tile, grad_x, mats, vecs


def _mesh_position():
    return lax.axis_index("x"), lax.axis_index("y"), lax.axis_index("c")


def _flip(v, bit):
    return 1 - v if bit else v


OTHER_CHIPS = ((1, 0), (0, 1), (1, 1))


def _comm_params():
    return pltpu.CompilerParams(vmem_limit_bytes=VMEM_LIMIT)


def _staged_copies(srcs, dsts, bufs, sems_in, sems_out):
    loads = [pltpu.make_async_copy(s, b, sems_in.at[i]) for i, (s, b) in enumerate(zip(srcs, bufs))]
    stores = [pltpu.make_async_copy(b, d, sems_out.at[i]) for i, (b, d) in enumerate(zip(bufs, dsts))]
    return loads, stores


class _GatherHook:
    def __init__(self, mats, vecs=()):
        self.arrs = list(mats) + list(vecs)
        self.nm, self.n = len(mats), len(self.arrs)
        n_ici, n_fwd = (N_CHIPS - 1) * self.n, max((N_CHIPS - 1) * self.nm, 1)
        dma = pltpu.SemaphoreType.DMA
        self.out_shape = [jax.ShapeDtypeStruct((N_CHIPS,) + a.shape, a.dtype) for a in self.arrs]
        self.scratch = [pltpu.VMEM(a.shape, a.dtype) for a in self.arrs] + [
            dma((n_ici,)), dma((n_ici,)), dma((n_fwd,)), dma((n_fwd,)), dma((self.n,)), dma((self.n,))]

    def plan(self, ins, outs, scratch):
        n, nm = self.n, self.nm
        bufs = scratch[:n]
        ici_send, ici_recv, fwd_send, fwd_recv, load_sems, store_sems = scratch[n:]
        xi, yi, ci = _mesh_position()
        me = 2 * xi + yi
        loads, stores = _staged_copies(ins, [outs[i].at[me] for i in range(n)], bufs, load_sems, store_sems)
        sends, landed, forwards, from_sibling = [], [], [], []
        for j, (bx, by) in enumerate(OTHER_CHIPS):
            px, py = _flip(xi, bx), _flip(yi, by)
            peer = 2 * px + py
            for i in range(n):
                k = j * n + i
                mk = functools.partial(pltpu.make_async_remote_copy, send_sem=ici_send.at[k], recv_sem=ici_recv.at[k],
                                       device_id=(px, py, ci), device_id_type=MESH)
                if i < nm:
                    sends.append(mk(src_ref=ins[i].at[ci], dst_ref=outs[i].at[me, ci]))
                    landed.append(mk(src_ref=ins[i].at[ci], dst_ref=outs[i].at[peer, ci]))
                    kf = j * nm + i
                    fw = functools.partial(pltpu.make_async_remote_copy, send_sem=fwd_send.at[kf], recv_sem=fwd_recv.at[kf],
                                           device_id=(xi, yi, 1 - ci), device_id_type=MESH)
                    forwards.append(fw(src_ref=outs[i].at[peer, ci], dst_ref=outs[i].at[peer, ci]))
                    from_sibling.append(fw(src_ref=outs[i].at[peer, ci], dst_ref=outs[i].at[peer, 1 - ci]))
                else:
                    sends.append(mk(src_ref=ins[i], dst_ref=outs[i].at[me]))
                    landed.append(mk(src_ref=ins[i], dst_ref=outs[i].at[peer]))
                    forwards.append(None)
        return loads, stores, sends, landed, forwards, from_sibling

    @staticmethod
    def start(p):
        loads, _, sends, _, _, _ = p
        for cp in loads + sends:
            cp.start()

    @staticmethod
    def relay(p):
        loads, stores, _, landed, forwards, _ = p
        for ld, st in zip(loads, stores):
            ld.wait()
            st.start()
        for cp, fw in zip(landed, forwards):
            cp.wait_recv()
            if fw is not None:
                fw.start()

    @staticmethod
    def finish(p):
        _, stores, sends, _, forwards, from_sibling = p
        for cp in from_sibling:
            cp.wait_recv()
        for cp in sends + [fw for fw in forwards if fw is not None]:
            cp.wait_send()
        for st in stores:
            st.wait()


def _run_hook(hook, ins, outs, scratch, step, n_steps):
    p = hook.plan(ins, outs, scratch)
    relay_step = min(max(1, (3 * n_steps) // 4), n_steps - 1)

    @pl.when(step == 0)
    def _():
        hook.start(p)

    if relay_step < n_steps - 1:
        @pl.when(step == relay_step)
        def _():
            hook.relay(p)

    @pl.when(step == n_steps - 1)
    def _():
        if relay_step == n_steps - 1:
            hook.relay(p)
        hook.finish(p)


def _hook_call(hook, *, name):
    n = len(hook.arrs)

    def body(*refs):
        p = hook.plan(refs[:n], refs[n:n + len(hook.out_shape)], refs[n + len(hook.out_shape):])
        hook.start(p)
        hook.relay(p)
        hook.finish(p)

    return pl.pallas_call(
        body, in_specs=[ANY] * n, out_specs=[ANY] * len(hook.out_shape), out_shape=hook.out_shape,
        scratch_shapes=hook.scratch, compiler_params=_comm_params(), name=name)(*hook.arrs)


def _send_other_half(parts, *, name):
    n = len(parts)

    def body(*refs):
        ins, outs = refs[:n], refs[n:2 * n]
        send_sems, recv_sems = refs[2 * n:]
        xi, yi, ci = _mesh_position()
        sibling = (xi, yi, 1 - ci)
        for i in range(n):
            for s in range(N_CHIPS):
                pltpu.make_async_remote_copy(src_ref=ins[i].at[s, 1 - ci], dst_ref=outs[i].at[s], send_sem=send_sems.at[i],
                                             recv_sem=recv_sems.at[i], device_id=sibling, device_id_type=MESH).start()
        for i in range(n):
            pltpu.make_async_remote_copy(src_ref=outs[i], dst_ref=outs[i], send_sem=send_sems.at[i], recv_sem=recv_sems.at[i],
                                         device_id=sibling, device_id_type=MESH).wait()

    return pl.pallas_call(
        body, in_specs=[ANY] * n, out_specs=[ANY] * n,
        out_shape=[jax.ShapeDtypeStruct((p.shape[0],) + p.shape[2:], p.dtype) for p in parts],
        scratch_shapes=[pltpu.SemaphoreType.DMA((n,)), pltpu.SemaphoreType.DMA((n,))],
        name=name)(*parts)


ROW_BLOCKS = 8


def _add_sibling_half(parts, theirs, core, *, name):
    n = len(parts)

    def body(core_ref, *refs):
        for a_ref, b_ref, o_ref in zip(refs[:n], refs[n:2 * n], refs[2 * n:]):
            o_ref[...] = (a_ref[...].astype(F32) + b_ref[...].astype(F32)).astype(o_ref.dtype)

    mine = lambda p: pl.BlockSpec((None, None, p.shape[2] // ROW_BLOCKS, p.shape[3]), lambda s, rb, core_ref: (s, core_ref[0], rb, 0))
    other = lambda p: pl.BlockSpec((None, p.shape[1] // ROW_BLOCKS, p.shape[2]), lambda s, rb, core_ref: (s, rb, 0))
    return pl.pallas_call(
        body,
        grid_spec=pltpu.PrefetchScalarGridSpec(
            num_scalar_prefetch=1, grid=(N_CHIPS, ROW_BLOCKS),
            in_specs=[mine(p) for p in parts] + [other(q) for q in theirs], out_specs=[other(q) for q in theirs]),
        out_shape=[jax.ShapeDtypeStruct(q.shape, BF16) for q in theirs],
        compiler_params=_params("parallel", "parallel"), name=name)(core, *parts, *theirs)


class _ExchangeHook:
    def __init__(self, parts, to_all=()):
        self.arrs = list(parts) + list(to_all)
        self.n_parts, self.n = len(parts), len(self.arrs)
        n_ici, n_peer = max((N_CHIPS - 1) * self.n_parts, 1), (N_DEV - 1) * max(len(to_all), 1)
        dma = pltpu.SemaphoreType.DMA
        self.out_shape = [jax.ShapeDtypeStruct(p.shape, p.dtype) for p in parts] + [
            jax.ShapeDtypeStruct((N_DEV,) + a.shape, a.dtype) for a in to_all]
        self.scratch = [pltpu.VMEM(p.shape[1:], p.dtype) for p in parts] + [pltpu.VMEM(a.shape, a.dtype) for a in to_all] + [
            dma((n_ici,)), dma((n_ici,)), dma((n_peer,)), dma((n_peer,)), dma((self.n,)), dma((self.n,))]

    def plan(self, ins, outs, scratch):
        n, npt = self.n, self.n_parts
        bufs = scratch[:n]
        send_sems, recv_sems, all_send, all_recv, load_sems, store_sems = scratch[n:]
        xi, yi, ci = _mesh_position()
        me_chip = 2 * xi + yi
        me = 4 * xi + 2 * yi + ci
        loads, stores = _staged_copies([ins[i].at[me_chip] for i in range(npt)] + list(ins[npt:]),
                                       [outs[i].at[me_chip] for i in range(npt)] + [outs[i].at[me] for i in range(npt, n)],
                                       bufs, load_sems, store_sems)
        sends, recvs = [], []
        for j, (bx, by) in enumerate(OTHER_CHIPS):
            px, py = _flip(xi, bx), _flip(yi, by)
            peer = 2 * px + py
            for i in range(npt):
                k = j * npt + i
                mk = functools.partial(pltpu.make_async_remote_copy, src_ref=ins[i].at[peer], send_sem=send_sems.at[k],
                                       recv_sem=recv_sems.at[k], device_id=(px, py, ci), device_id_type=MESH)
                sends.append(mk(dst_ref=outs[i].at[me_chip]))
                recvs.append(mk(dst_ref=outs[i].at[peer]))
        for i in range(npt, n):
            for k in range(1, N_DEV):
                px, py, pc = _flip(xi, (k >> 2) & 1), _flip(yi, (k >> 1) & 1), _flip(ci, k & 1)
                slot = (i - npt) * (N_DEV - 1) + k - 1
                mk = functools.partial(pltpu.make_async_remote_copy, src_ref=ins[i], send_sem=all_send.at[slot],
                                       recv_sem=all_recv.at[slot], device_id=(px, py, pc), device_id_type=MESH)
                sends.append(mk(dst_ref=outs[i].at[me]))
                recvs.append(mk(dst_ref=outs[i].at[4 * px + 2 * py + pc]))
        return loads, stores, sends, recvs

    @staticmethod
    def start(p):
        loads, _, sends, _ = p
        for cp in loads + sends:
            cp.start()

    @staticmethod
    def relay(p):
        loads, stores, _, _ = p
        for ld, st in zip(loads, stores):
            ld.wait()
            st.start()

    @staticmethod
    def finish(p):
        _, stores, sends, recvs = p
        for cp in recvs:
            cp.wait_recv()
        for cp in sends:
            cp.wait_send()
        for st in stores:
            st.wait()


def _sum_chips(parts, *, name):
    n = len(parts)
    p = parts[0].shape[0]

    def body(*refs):
        s = pl.program_id(1)
        for x_ref, o_ref in zip(refs[:n], refs[n:]):
            @pl.when(s == 0)
            def _():
                o_ref[...] = x_ref[...].astype(F32)

            @pl.when(s > 0)
            def _():
                o_ref[...] += x_ref[...].astype(F32)

    blocks = lambda q: ROW_BLOCKS if q.shape[1] % (8 * ROW_BLOCKS) == 0 else 1
    assert len({blocks(q) for q in parts}) == 1
    nb = blocks(parts[0])
    return pl.pallas_call(
        body, grid=(nb, p),
        in_specs=[pl.BlockSpec((None, q.shape[1] // nb, q.shape[2]), lambda rb, s: (s, rb, 0)) for q in parts],
        out_specs=[pl.BlockSpec((q.shape[1] // nb, q.shape[2]), lambda rb, s: (rb, 0)) for q in parts],
        out_shape=[jax.ShapeDtypeStruct(q.shape[1:], F32) for q in parts],
        compiler_params=_params("parallel", "arbitrary"), name=name)(*parts)


def _swap_halves(halves, layers, *, name):
    n = len(halves)
    out_shapes, slots = [], []
    for i, h in enumerate(halves):
        pair = [p for p in layers if i in p]
        if pair and pair[0][1] == i:
            slots.append((slots[pair[0][0]][0], 1))
        elif pair:
            out_shapes.append(jax.ShapeDtypeStruct((2, 2) + h.shape, h.dtype))
            slots.append((len(out_shapes) - 1, 0))
        else:
            out_shapes.append(jax.ShapeDtypeStruct((2,) + h.shape, h.dtype))
            slots.append((len(out_shapes) - 1, None))
    n_out = len(out_shapes)

    def body(*refs):
        ins, outs, bufs = refs[:n], refs[n:n + n_out], refs[n + n_out:2 * n + n_out]
        send_sems, recv_sems, load_sems, store_sems = refs[2 * n + n_out:]
        xi, yi, ci = _mesh_position()
        own, sends, recvs = [], [], []
        for i in range(n):
            o, layer = slots[i]
            dst = (lambda core: outs[o].at[core]) if layer is None else (lambda core: outs[o].at[layer, core])
            own.append(dst(ci))
            mk = functools.partial(pltpu.make_async_remote_copy, src_ref=ins[i], send_sem=send_sems.at[i],
                                   recv_sem=recv_sems.at[i], device_id=(xi, yi, 1 - ci), device_id_type=MESH)
            sends.append(mk(dst_ref=dst(ci)))
            recvs.append(mk(dst_ref=dst(1 - ci)))
        loads, stores = _staged_copies(ins, own, bufs, load_sems, store_sems)
        for cp in loads + sends:
            cp.start()
        for ld, st in zip(loads, stores):
            ld.wait()
            st.start()
        for cp in recvs:
            cp.wait_recv()
        for cp in sends:
            cp.wait_send()
        for st in stores:
            st.wait()

    return pl.pallas_call(
        body, in_specs=[ANY] * n, out_specs=[ANY] * n_out, out_shape=out_shapes,
        scratch_shapes=[pltpu.VMEM(h.shape, h.dtype) for h in halves]
        + [pltpu.SemaphoreType.DMA((n,)), pltpu.SemaphoreType.DMA((n,)), pltpu.SemaphoreType.DMA((n,)), pltpu.SemaphoreType.DMA((n,))],
        compiler_params=_comm_params(), name=name)(*halves)


def _cast_bf16(layers, *, name, hook=None):
    n = len(layers)
    hk = _HookSlots(hook, n_in=n, n_out=n, n_scratch=0)

    def body(*refs):
        ins, outs, _ = hk.own(refs)
        if hook is not None:
            hk.run(refs, pl.program_id(0), ROW_BLOCKS)
        for i_ref, o_ref in zip(ins, outs):
            o_ref[...] = i_ref[...].astype(o_ref.dtype)

    in_blk = lambda a, l: pl.BlockSpec((None, a.shape[1] // ROW_BLOCKS, a.shape[2]), lambda i: (l, i, 0))
    out_blk = lambda a: pl.BlockSpec((a.shape[1] // ROW_BLOCKS, a.shape[2]), lambda i: (i, 0))
    outs = pl.pallas_call(
        body, grid=(ROW_BLOCKS,),
        in_specs=[in_blk(a, l) for a, l in layers] + hk.in_specs,
        out_specs=[out_blk(a) for a, _ in layers] + hk.out_specs,
        out_shape=[jax.ShapeDtypeStruct(a.shape[1:], BF16) for a, _ in layers] + hk.out_shape,
        scratch_shapes=hk.scratch,
        compiler_params=_params(*hk.semantics("parallel")), name=name)(*[a for a, _ in layers], *hk.inputs)
    return outs[:n] if hook is None else (outs[:n], outs[n:])


def _full_weight(name, gathered):
    s, _, r, c = gathered.shape
    if name == "ssd_w_in":
        return _w_in_from_shards(gathered.reshape(s, 2 * r, c), name="ssd_w_in_unshard")
    if name in ("attn_w_qkv", "mlp_w_up0", "mlp_w_up1"):
        return gathered.reshape(s, 2 * r, c)
    return gathered.reshape(s * 2 * r, c)


class _StepComm:
    GATHER = {"in_proj": ("mlp_w_up0", "attn_w_qkv"), "conv": ("mlp_w_down0", "attn_w_o"), "scan": ("ssd_w_out",),
              "mlp_up_l0": ("mlp_w_up1",), "mlp_down_l0": ("mlp_w_down1",)}
    EXCHANGE = {"early": ("ssd_w_out", "attn_w_qkv", "attn_w_o", "mlp_w_up0", "mlp_w_up1", "mlp_w_down0", "mlp_w_down1"),
                "late": ("ssd_w_in",)}

    def __init__(self, shards, core):
        self.shards, self.core = shards, core
        self.chip_parts = {}
        self._pending = None

    def gather_hook(self, stage):
        names = self.GATHER.get(stage)
        return _GatherHook([self.shards[n] for n in names]) if names else None

    def weights_from(self, stage, gathered):
        return {n: _full_weight(n, g) for n, g in zip(self.GATHER[stage], gathered)}

    def chip_sums(self, mats, tag):
        parts = [_shard_halves(a) for a in mats.values()]
        theirs = _send_other_half(parts, name=f"grad_sibling_send_{tag}")
        return _add_sibling_half(parts, theirs, self.core, name=f"grad_chip_sum_{tag}")

    def exchange_hook(self, mats, which):
        self._pending = self.EXCHANGE[which]
        return _ExchangeHook(self.chip_sums({n: mats[n] for n in self._pending}, which))

    def received(self, arrays):
        self.chip_parts.update(zip(self._pending, arrays))


def _adamw(w, g, m, v, *, name):
    r, c = w.shape
    tr = 256 if r % 256 == 0 else r
    blk = pl.BlockSpec((tr, c), lambda i: (i, 0))

    def body(w_ref, g_ref, m_ref, v_ref, d_ref, nm_ref, nv_ref):
        gv = g_ref[...]
        nm = ADAM_B1 * m_ref[...] + (1.0 - ADAM_B1) * gv
        nv = ADAM_B2 * v_ref[...] + (1.0 - ADAM_B2) * (gv * gv)
        m_hat = nm / (1.0 - ADAM_B1 ** ADAM_STEP)
        v_hat = nv / (1.0 - ADAM_B2 ** ADAM_STEP)
        d_ref[...] = -ADAM_LR * (m_hat / (jnp.sqrt(v_hat) + ADAM_EPS) + ADAM_WD * w_ref[...])
        nm_ref[...] = nm
        nv_ref[...] = nv

    sh = jax.ShapeDtypeStruct((r, c), F32)
    return pl.pallas_call(body, grid=(r // tr,), in_specs=[blk] * 4, out_specs=[blk] * 3, out_shape=[sh] * 3,
                          compiler_params=_params("parallel"), name=name)(w, g, m, v)


SM_CONV_B, SM_NORM_W, SM_MIX_PRE, SM_MIX_POST, SM_FFN_PRE, SM_FFN_POST, SM_MISC, SM_CONV_W, SM_B_QKV, SM_B_O = 0, 4, 6, 8, 10, 12, 14, 16, 32, 34
SM_ROWS = 40
MISC_DT_BIAS, MISC_A_LOG, MISC_D, MISC_SINKS, MISC_LOSS = 0, 32, 64, 96, 112


def _shard_halves(a):
    c = a.shape[-1]
    return a.reshape(N_CHIPS, 2, -1, c)


def _rows(v):
    return v.reshape(-1, D_MODEL)


def _misc_row(dt_bias, a_log, d, sinks, loss):
    pad = jnp.zeros((D_MODEL - MISC_LOSS - 1,), F32)
    return jnp.concatenate([dt_bias.reshape(-1), a_log.reshape(-1), d.reshape(-1), sinks.reshape(-1), loss.reshape(1), pad]).reshape(1, D_MODEL)


def _replicated_rows(p, loss):
    return jnp.concatenate([
        _rows(p["ssd_conv_b"]), _rows(p["ssd_norm_w"]), _rows(p["mix_pre_norm"]), _rows(p["mix_post_norm"]),
        _rows(p["ffn_pre_norm"]), _rows(p["ffn_post_norm"]),
        _misc_row(p["ssd_dt_bias"], p["ssd_a_log"], p["ssd_d"], p["attn_sinks"], loss), jnp.zeros((1, D_MODEL), F32)], axis=0)


def _sharded_rows(conv_w, b_qkv, b_o):
    last = jnp.concatenate([b_qkv.reshape(-1), b_o.reshape(-1), jnp.zeros((D_MODEL - 640,), F32)]).reshape(1, D_MODEL)
    return jnp.concatenate([conv_w.reshape(SSD_CONV_WIDTH, D_MODEL), last, jnp.zeros((3, D_MODEL), F32)], axis=0)


REPLICATED = ("ssd_conv_b", "ssd_dt_bias", "ssd_a_log", "ssd_d", "ssd_norm_w", "attn_sinks",
              "mix_pre_norm", "mix_post_norm", "ffn_pre_norm", "ffn_post_norm")
MATRICES = ("ssd_w_in", "ssd_w_out", "attn_w_qkv", "attn_w_o", "mlp_w_up", "mlp_w_down")
WEIGHT_NAMES = ("ssd_w_in", "ssd_conv_w", "ssd_conv_b", "ssd_dt_bias", "ssd_a_log", "ssd_d", "ssd_norm_w", "ssd_w_out",
                "attn_w_qkv", "attn_b_qkv", "attn_sinks", "attn_w_o", "attn_b_o", "mlp_w_up", "mlp_w_down",
                "mix_pre_norm", "mix_post_norm", "ffn_pre_norm", "ffn_post_norm")


def _unpack_small(rows16, rows8, like):
    misc = rows16[SM_MISC]
    out = {
        "ssd_conv_b": rows16[SM_CONV_B:SM_CONV_B + 4], "ssd_norm_w": rows16[SM_NORM_W:SM_NORM_W + 2],
        "mix_pre_norm": rows16[SM_MIX_PRE:SM_MIX_PRE + 2], "mix_post_norm": rows16[SM_MIX_POST:SM_MIX_POST + 2],
        "ffn_pre_norm": rows16[SM_FFN_PRE:SM_FFN_PRE + 2], "ffn_post_norm": rows16[SM_FFN_POST:SM_FFN_POST + 2],
        "ssd_dt_bias": misc[MISC_DT_BIAS:MISC_DT_BIAS + 32], "ssd_a_log": misc[MISC_A_LOG:MISC_A_LOG + 32],
        "ssd_d": misc[MISC_D:MISC_D + 32], "attn_sinks": misc[MISC_SINKS:MISC_SINKS + 16],
        "ssd_conv_w": rows8[0:SSD_CONV_WIDTH], "attn_b_qkv": rows8[SSD_CONV_WIDTH, 0:384], "attn_b_o": rows8[SSD_CONV_WIDTH, 384:640],
    }
    return {k: v.reshape(like[k].shape) for k, v in out.items()}


def kernel(x, ssd_w_in, ssd_conv_w, ssd_conv_b, ssd_dt_bias, ssd_a_log, ssd_d, ssd_norm_w, ssd_w_out, attn_w_qkv, attn_b_qkv, attn_sinks, attn_w_o, attn_b_o, mlp_w_up, mlp_w_down, mix_pre_norm, mix_post_norm, ffn_pre_norm, ffn_post_norm, loss_target, m_ssd_w_in, m_ssd_conv_w, m_ssd_conv_b, m_ssd_dt_bias, m_ssd_a_log, m_ssd_d, m_ssd_norm_w, m_ssd_w_out, m_attn_w_qkv, m_attn_b_qkv, m_attn_sinks, m_attn_w_o, m_attn_b_o, m_mlp_w_up, m_mlp_w_down, m_mix_pre_norm, m_mix_post_norm, m_ffn_pre_norm, m_ffn_post_norm, v_ssd_w_in, v_ssd_conv_w, v_ssd_conv_b, v_ssd_dt_bias, v_ssd_a_log, v_ssd_d, v_ssd_norm_w, v_ssd_w_out, v_attn_w_qkv, v_attn_b_qkv, v_attn_sinks, v_attn_w_o, v_attn_b_o, v_mlp_w_up, v_mlp_w_down, v_mix_pre_norm, v_mix_post_norm, v_ffn_pre_norm, v_ffn_post_norm):
    w = dict(zip(WEIGHT_NAMES, (ssd_w_in, ssd_conv_w, ssd_conv_b, ssd_dt_bias, ssd_a_log, ssd_d, ssd_norm_w, ssd_w_out, attn_w_qkv, attn_b_qkv, attn_sinks, attn_w_o, attn_b_o, mlp_w_up, mlp_w_down, mix_pre_norm, mix_post_norm, ffn_pre_norm, ffn_post_norm)))
    m = dict(zip(WEIGHT_NAMES, (m_ssd_w_in, m_ssd_conv_w, m_ssd_conv_b, m_ssd_dt_bias, m_ssd_a_log, m_ssd_d, m_ssd_norm_w, m_ssd_w_out, m_attn_w_qkv, m_attn_b_qkv, m_attn_sinks, m_attn_w_o, m_attn_b_o, m_mlp_w_up, m_mlp_w_down, m_mix_pre_norm, m_mix_post_norm, m_ffn_pre_norm, m_ffn_post_norm)))
    v = dict(zip(WEIGHT_NAMES, (v_ssd_w_in, v_ssd_conv_w, v_ssd_conv_b, v_ssd_dt_bias, v_ssd_a_log, v_ssd_d, v_ssd_norm_w, v_ssd_w_out, v_attn_w_qkv, v_attn_b_qkv, v_attn_sinks, v_attn_w_o, v_attn_b_o, v_mlp_w_up, v_mlp_w_down, v_mix_pre_norm, v_mix_post_norm, v_ffn_pre_norm, v_ffn_post_norm)))
    chip = 2 * lax.axis_index("x") + lax.axis_index("y")

    two_halves = lambda a: a.reshape(2, a.shape[0] // 2, a.shape[1])
    later = {"ssd_w_out": (w["ssd_w_out"], 0), "attn_w_qkv": (w["attn_w_qkv"], 0), "attn_w_o": (w["attn_w_o"], 0),
             "mlp_w_up0": (w["mlp_w_up"], 0), "mlp_w_up1": (w["mlp_w_up"], 1),
             "mlp_w_down0": (w["mlp_w_down"], 0), "mlp_w_down1": (w["mlp_w_down"], 1)}
    first = _GatherHook([two_halves(w["ssd_w_in"][0].astype(BF16))], [w["ssd_conv_w"][0], w["attn_b_qkv"], w["attn_b_o"]])
    cast, (g_in, g_conv, g_bqkv, g_bo) = _cast_bf16(list(later.values()), name="weights_to_bf16", hook=first)
    core = lax.axis_index("c").astype(jnp.int32).reshape(1)
    comm = _StepComm({k: two_halves(a) for k, a in zip(later, cast)}, core)
    full = {
        "ssd_w_in": _full_weight("ssd_w_in", g_in),
        "ssd_conv_w": g_conv.transpose(1, 0, 2).reshape(SSD_CONV_WIDTH, SSD_CONV_DIM),
        "attn_b_qkv": g_bqkv.reshape(ATTN_QKV), "attn_b_o": g_bo.reshape(D_MODEL),
    }
    for name in REPLICATED:
        full[name] = w[name][0] if name.startswith(("ssd_", "attn_")) else w[name]

    loss_tile, grad_x, gm, g = _local_step(x[0], loss_target[0], full, comm)

    conv_w_rows = g["ssd_conv_w"].reshape(SSD_CONV_WIDTH * N_CHIPS, D_MODEL)
    b_qkv_rows = jnp.pad(g["attn_b_qkv"], (0, 2 * D_MODEL - ATTN_QKV)).reshape(2, D_MODEL)
    small = jnp.concatenate([_replicated_rows(g, loss_tile[0, 0]), conv_w_rows, b_qkv_rows, _rows(g["attn_b_o"]),
                             jnp.zeros((SM_ROWS - SM_B_O - 1, D_MODEL), F32)], axis=0)
    small_all, = _hook_call(_ExchangeHook([], [small]), name="vector_grad_all_gather")
    order = ("ssd_w_in", "ssd_w_out", "attn_w_qkv", "attn_w_o", "mlp_w_up0", "mlp_w_up1", "mlp_w_down0", "mlp_w_down1")
    halves = _sum_chips([comm.chip_parts[k] for k in order], name="grad_sum")
    r_in, r_out, r_qkv, r_o, r_up, r_down = _swap_halves(halves, layers=((4, 5), (6, 7)), name="grad_halves_swap")
    small_sum, = _sum_chips([small_all], name="small_grad_sum")

    grads = {"ssd_w_in": r_in, "ssd_w_out": r_out, "attn_w_qkv": r_qkv, "attn_w_o": r_o, "mlp_w_up": r_up, "mlp_w_down": r_down}
    grads = {k: a.reshape(w[k].shape) for k, a in grads.items()}
    conv_w_g = lax.dynamic_index_in_dim(small_sum[SM_CONV_W:SM_CONV_W + 16].reshape(SSD_CONV_WIDTH, N_CHIPS, D_MODEL), chip, axis=1, keepdims=False)
    b_qkv_g = lax.dynamic_slice_in_dim(small_sum[SM_B_QKV:SM_B_QKV + 2].reshape(-1), chip * 384, 384)
    b_o_g = lax.dynamic_slice_in_dim(small_sum[SM_B_O], chip * 256, 256)
    small_g = jnp.concatenate([small_sum[0:16], _sharded_rows(conv_w_g, b_qkv_g, b_o_g)], axis=0)
    grads.update(_unpack_small(small_g[0:16], small_g[16:24], w))
    loss = small_sum[SM_MISC, MISC_LOSS]

    delta, new_m, new_v = {}, {}, {}
    for name in MATRICES:
        shape = w[name].shape
        as2d = lambda a: a.reshape(-1, shape[-1])
        d2, m2, v2 = _adamw(as2d(w[name]), as2d(grads[name]), as2d(m[name]), as2d(v[name]), name=f"adamw_{name}")
        delta[name], new_m[name], new_v[name] = d2.reshape(shape), m2.reshape(shape), v2.reshape(shape)
    zero = jnp.zeros((), F32)
    small_pack = lambda p: jnp.concatenate([_replicated_rows({k: p[k] for k in REPLICATED}, zero),
                                            _sharded_rows(p["ssd_conv_w"], p["attn_b_qkv"], p["attn_b_o"])], axis=0)
    d_s, m_s, v_s = _adamw(small_pack(w), small_g, small_pack(m), small_pack(v), name="adamw_vectors")
    delta.update(_unpack_small(d_s[0:16], d_s[16:24], w))
    new_m.update(_unpack_small(m_s[0:16], m_s[16:24], w))
    new_v.update(_unpack_small(v_s[0:16], v_s[16:24], w))

    return (loss, grad_x[None], *[grads[n] for n in WEIGHT_NAMES], *[delta[n] for n in WEIGHT_NAMES],
            *[new_m[n] for n in WEIGHT_NAMES], *[new_v[n] for n in WEIGHT_NAMES])
```

```python
import functools
import math

import jax
import jax.numpy as jnp
from jax import lax
from jax.experimental import pallas as pl
from jax.experimental.pallas import tpu as pltpu

F32 = jnp.float32
BF16 = jnp.bfloat16

D_MODEL = 1024
SSD_D_INNER = 2048
SSD_HEAD_DIM = 64
SSD_N_HEADS = 32
SSD_N_GROUPS = 8
SSD_HPG = 4
SSD_D_STATE = 128
SSD_CONV_WIDTH = 4
SSD_CHUNK = 128
SSD_CONV_DIM = 4096
SSD_IN_DIM = 6176
SSD_IN_PAD = 6272
SSD_GW = SSD_HPG * SSD_HEAD_DIM
ATTN_HEAD_DIM = 64
ATTN_N_Q = 16
ATTN_N_KV = 4
ATTN_REP = 4
ATTN_WINDOW = 128
ATTN_QKV = 1536
D_FF = 4096
NORM_EPS = 1e-6

ADAM_LR = 0.001
ADAM_B1 = 0.9
ADAM_B2 = 0.999
ADAM_EPS = 1e-08
ADAM_WD = 0.01
ADAM_STEP = 10

N_CHIPS = 4
N_DEV = 8
LANES = 128
VMEM_LIMIT = 48 * 1024 * 1024

MESH = pl.DeviceIdType.MESH


def _params(*sem):
    return pltpu.CompilerParams(dimension_semantics=sem, vmem_limit_bytes=VMEM_LIMIT)


def _dot(a, b, dims):
    return lax.dot_general(a, b, (dims, ((), ())), preferred_element_type=F32)


def _dot_nn(a, b):
    return _dot(a, b, ((1,), (0,)))


def _dot_nt(a, b):
    return _dot(a, b, ((1,), (1,)))


def _dot_tn(a, b):
    return _dot(a, b, ((0,), (0,)))


def _sigmoid(x):
    return 0.5 * jnp.tanh(0.5 * x) + 0.5


ANY = pl.BlockSpec(memory_space=pl.ANY)


class _HookSlots:
    def __init__(self, hook, n_in, n_out, n_scratch):
        self.hook = hook
        self.n_in, self.n_out, self.n_scratch = n_in, n_out, n_scratch
        self.inputs = list(hook.arrs) if hook else []
        self.out_shape = list(hook.out_shape) if hook else []
        self.scratch = list(hook.scratch) if hook else []
        self.in_specs = [ANY] * len(self.inputs)
        self.out_specs = [ANY] * len(self.out_shape)

    def _split(self, refs):
        a = self.n_in
        b = a + len(self.inputs)
        c = b + self.n_out
        d = c + len(self.out_shape)
        e = d + self.n_scratch
        return refs[:a], refs[a:b], refs[b:c], refs[c:d], refs[d:e], refs[e:]

    def own(self, refs):
        ins, _, outs, _, scratch, _ = self._split(refs)
        return ins, outs, scratch

    def run(self, refs, step, n_steps):
        _, h_in, _, h_out, _, h_scratch = self._split(refs)
        _run_hook(self.hook, h_in, h_out, h_scratch, step, n_steps)

    def semantics(self, *sem):
        return sem if self.hook is None else ("arbitrary",) * len(sem)


def _matmul(a, b, *, mode, out_dtypes, name, epilogue=None, extras=(), tm=1024, tn=1024, tk=1024,
            b_shards=False, out_shards=False, hook=None):
    if b_shards:
        s, b_rows, b_cols = b.shape
        b2 = (b_rows, s * b_cols)
        if mode == "nn":
            tn = b_cols
        else:
            assert mode == "nt"
            tk = b_cols
    else:
        b2 = b.shape
    if mode == "nn":
        (m, k), (k2, n) = a.shape, b2
    elif mode == "nt":
        (m, k), (n, k2) = a.shape, b2
    else:
        (k, m), (k2, n) = a.shape, b2
    assert k == k2, (a.shape, b.shape, mode)
    tm, tn, tk = min(tm, m), min(tn, n), min(tk, k)
    assert m % tm == 0 and n % tn == 0 and k % tk == 0, (m, n, k, tm, tn, tk)
    nk = k // tk
    if mode == "tn":
        a_spec = pl.BlockSpec((tk, tm), lambda i, j, kk: (kk, i))
    else:
        a_spec = pl.BlockSpec((tm, tk), lambda i, j, kk: (i, kk))
    if b_shards and mode == "nn":
        b_spec = pl.BlockSpec((None, tk, tn), lambda i, j, kk: (j, kk, 0))
    elif b_shards:
        b_spec = pl.BlockSpec((None, tn, tk), lambda i, j, kk: (kk, j, 0))
    elif mode == "nt":
        b_spec = pl.BlockSpec((tn, tk), lambda i, j, kk: (j, kk))
    else:
        b_spec = pl.BlockSpec((tk, tn), lambda i, j, kk: (kk, j))
    dims = {"nn": ((1,), (0,)), "nt": ((1,), (1,)), "tn": ((0,), (0,))}[mode]
    ex_specs = []
    for arr, kind in extras:
        if kind == "tile":
            ex_specs.append(pl.BlockSpec((tm, tn), lambda i, j, kk: (i, j)))
        else:
            ex_specs.append(pl.BlockSpec((1, tn), lambda i, j, kk: (0, j)))
    n_ex, n_out = len(extras), len(out_dtypes)
    if epilogue is None:
        epilogue = lambda acc: (acc,)
    hk = _HookSlots(hook, n_in=2 + n_ex, n_out=n_out, n_scratch=0 if nk == 1 else 1)
    grid = (m // tm, n // tn, nk)

    def body(*refs):
        (a_ref, b_ref, *ex), outs, scratch = hk.own(refs)
        if hook is not None:
            step = (pl.program_id(0) * grid[1] + pl.program_id(1)) * grid[2] + pl.program_id(2)
            hk.run(refs, step, grid[0] * grid[1] * grid[2])

        def finish(acc):
            res = epilogue(acc, *[e[...] for e in ex])
            for o, r in zip(outs, res):
                o[...] = r.astype(o.dtype)

        if nk == 1:
            finish(_dot(a_ref[...], b_ref[...], dims))
        else:
            acc_ref = scratch[0]
            kk = pl.program_id(2)

            @pl.when(kk == 0)
            def _():
                acc_ref[...] = jnp.zeros_like(acc_ref)

            acc_ref[...] += _dot(a_ref[...], b_ref[...], dims)

            @pl.when(kk == nk - 1)
            def _():
                finish(acc_ref[...])

    if out_shards:
        out_spec = pl.BlockSpec((None, tm, tn), lambda i, j, kk: (j, i, 0))
        out_dims = (n // tn, m, tn)
    else:
        out_spec = pl.BlockSpec((tm, tn), lambda i, j, kk: (i, j))
        out_dims = (m, n)
    outs = pl.pallas_call(
        body,
        grid=grid,
        in_specs=[a_spec, b_spec] + ex_specs + hk.in_specs,
        out_specs=[out_spec for _ in out_dtypes] + hk.out_specs,
        out_shape=[jax.ShapeDtypeStruct(out_dims, dt) for dt in out_dtypes] + hk.out_shape,
        scratch_shapes=([] if nk == 1 else [pltpu.VMEM((tm, tn), F32)]) + hk.scratch,
        compiler_params=_params(*hk.semantics("parallel", "parallel", "arbitrary")),
        name=name,
    )(a, b, *[arr for arr, _ in extras], *hk.inputs)
    own = outs[0] if n_out == 1 else outs[:n_out]
    return own if hook is None else (own, outs[n_out:])


def _row_tile(t, want):
    return min(t, want)


def _rms_fwd(x, w, *, name, resid=None, want_u=None, target=None):
    t, d = x.shape
    tr = _row_tile(t, 512)

    def norm(v, wv):
        return v * lax.rsqrt(jnp.mean(v * v, axis=-1, keepdims=True) + NORM_EPS) * wv

    row = pl.BlockSpec((tr, d), lambda i: (i, 0))
    vec = pl.BlockSpec((1, d), lambda i: (0, 0))
    if target is not None:
        def body(x_ref, w_ref, r_ref, t_ref, dh_ref, loss_ref):
            err = r_ref[...] + norm(x_ref[...], w_ref[...]) - t_ref[...]
            dh_ref[...] = err * (1.0 / d)

            @pl.when(pl.program_id(0) == 0)
            def _():
                loss_ref[...] = jnp.zeros_like(loss_ref)

            part = jnp.sum(jnp.sum(err * err, axis=1, keepdims=True), axis=0, keepdims=True) * (0.5 / d)
            loss_ref[...] += jnp.broadcast_to(part, loss_ref.shape)

        return pl.pallas_call(
            body, grid=(t // tr,), in_specs=[row, vec, row, row],
            out_specs=[row, pl.BlockSpec((8, LANES), lambda i: (0, 0))],
            out_shape=[jax.ShapeDtypeStruct((t, d), F32), jax.ShapeDtypeStruct((8, LANES), F32)],
            compiler_params=_params("arbitrary"), name=name)(x, w, resid, target)
    if resid is None:
        def body(x_ref, w_ref, o_ref):
            o_ref[...] = norm(x_ref[...], w_ref[...]).astype(BF16)
        ins, in_specs = (x, w), [row, vec]
        out_shape, out_specs = jax.ShapeDtypeStruct((t, d), BF16), row
    elif want_u is None:
        def body(x_ref, w_ref, r_ref, o_ref):
            o_ref[...] = r_ref[...] + norm(x_ref[...], w_ref[...])
        ins, in_specs = (x, w, resid), [row, vec, row]
        out_shape, out_specs = jax.ShapeDtypeStruct((t, d), F32), row
    else:
        def body(x_ref, w_ref, r_ref, w2_ref, o_ref, u_ref):
            h = r_ref[...] + norm(x_ref[...], w_ref[...])
            o_ref[...] = h
            u_ref[...] = norm(h, w2_ref[...]).astype(BF16)
        ins, in_specs = (x, w, resid, want_u), [row, vec, row, vec]
        out_shape = [jax.ShapeDtypeStruct((t, d), F32), jax.ShapeDtypeStruct((t, d), BF16)]
        out_specs = [row, row]
    return pl.pallas_call(body, grid=(t // tr,), in_specs=in_specs, out_specs=out_specs, out_shape=out_shape,
                          compiler_params=_params("parallel"), name=name)(*ins)


def _rms_bwd(x, w, dy, *, name, resid=None, out_dtype=F32, dx_col_sum=False):
    t, d = x.shape
    tr = _row_tile(t, 512)
    row = pl.BlockSpec((tr, d), lambda i: (i, 0))
    vec = pl.BlockSpec((1, d), lambda i: (0, 0))
    has_res = resid is not None

    def body(x_ref, w_ref, dy_ref, *rest):
        r_ref = rest[0] if has_res else None
        dx_ref, dw_ref = rest[has_res:has_res + 2]
        xv = x_ref[...]
        dyv = dy_ref[...].astype(F32)
        r = lax.rsqrt(jnp.mean(xv * xv, axis=-1, keepdims=True) + NORM_EPS)
        xhat = xv * r
        dyw = dyv * w_ref[...]
        dx = r * (dyw - xhat * jnp.mean(dyw * xhat, axis=-1, keepdims=True))
        if has_res:
            dx = dx + r_ref[...]
        dx_ref[...] = dx.astype(dx_ref.dtype)

        sums = [(dw_ref, dyv * xhat)] + ([(rest[-1], dx)] if dx_col_sum else [])

        @pl.when(pl.program_id(0) == 0)
        def _():
            for acc_ref, _ in sums:
                acc_ref[...] = jnp.zeros_like(acc_ref)

        for acc_ref, rows in sums:
            acc_ref[...] += jnp.sum(rows, axis=0, keepdims=True)

    ins = (x, w, dy) + ((resid,) if has_res else ())
    in_specs = [row, vec, row] + ([row] if has_res else [])
    n_vec = 2 if dx_col_sum else 1
    return pl.pallas_call(
        body, grid=(t // tr,), in_specs=in_specs, out_specs=[row] + [vec] * n_vec,
        out_shape=[jax.ShapeDtypeStruct((t, d), out_dtype)] + [jax.ShapeDtypeStruct((1, d), F32)] * n_vec,
        compiler_params=_params("arbitrary"), name=name)(*ins)


def _col_sum(x, *, name):
    t, n = x.shape
    tr = _row_tile(t, 512)

    def body(x_ref, o_ref):
        @pl.when(pl.program_id(0) == 0)
        def _():
            o_ref[...] = jnp.zeros_like(o_ref)

        o_ref[...] += jnp.sum(x_ref[...].astype(F32), axis=0, keepdims=True)

    return pl.pallas_call(
        body, grid=(t // tr,), in_specs=[pl.BlockSpec((tr, n), lambda i: (i, 0))],
        out_specs=pl.BlockSpec((1, n), lambda i: (0, 0)), out_shape=jax.ShapeDtypeStruct((1, n), F32),
        compiler_params=_params("arbitrary"), name=name)(x)


SSD_IN_SHARD = SSD_IN_DIM // N_CHIPS


def _w_in_from_shards(shards, *, name):
    d = shards.shape[1]
    tr = 256

    def body(s_ref, o_ref):
        o_ref[:, pl.ds(SSD_IN_PAD - LANES, LANES)] = jnp.zeros((tr, LANES), o_ref.dtype)
        for s in range(N_CHIPS):
            o_ref[:, pl.ds(SSD_IN_SHARD * s, SSD_IN_SHARD)] = s_ref[s]

    return pl.pallas_call(
        body, grid=(d // tr,), in_specs=[pl.BlockSpec((N_CHIPS, tr, SSD_IN_SHARD), lambda i: (0, i, 0))],
        out_specs=pl.BlockSpec((tr, SSD_IN_PAD), lambda i: (i, 0)),
        out_shape=jax.ShapeDtypeStruct((d, SSD_IN_PAD), shards.dtype),
        compiler_params=_params("parallel"), name=name)(shards)


def _w_in_to_shards(g, *, name):
    d = g.shape[0]
    tr = 256

    def body(g_ref, o_ref):
        for s in range(N_CHIPS):
            o_ref[s] = g_ref[:, pl.ds(SSD_IN_SHARD * s, SSD_IN_SHARD)].astype(o_ref.dtype)

    return pl.pallas_call(
        body, grid=(d // tr,), in_specs=[pl.BlockSpec((tr, SSD_IN_PAD), lambda i: (i, 0))],
        out_specs=pl.BlockSpec((N_CHIPS, tr, SSD_IN_SHARD), lambda i: (0, i, 0)),
        out_shape=jax.ShapeDtypeStruct((N_CHIPS, d, SSD_IN_SHARD), BF16),
        compiler_params=_params("parallel"), name=name)(g)


XBC_COL0 = SSD_D_INNER // LANES
DT_COL0 = (SSD_D_INNER + SSD_CONV_DIM) // LANES


def _shift_down(v, k, row_ids):
    return jnp.where(row_ids >= k, pltpu.roll(v, k, axis=0), 0.0)


def _shift_up(v, k, row_ids):
    n = v.shape[0]
    return jnp.where(row_ids < n - k, pltpu.roll(v, n - k, axis=0), 0.0)


def _conv_pre(x, w, b, row_ids):
    pre = b + w[3:4, :] * x
    for k in (1, 2, 3):
        pre = pre + w[3 - k:4 - k, :] * _shift_down(x, k, row_ids)
    return pre


def _conv_fwd(zx, conv_w, conv_b, *, name, hook=None):
    t = zx.shape[0]
    nct = SSD_CONV_DIM // LANES
    hk = _HookSlots(hook, n_in=3, n_out=1, n_scratch=0)

    def body(*refs):
        (x_ref, w_ref, b_ref), (o_ref,), _ = hk.own(refs)
        if hook is not None:
            hk.run(refs, pl.program_id(0), nct)
        x = x_ref[...]
        row_ids = lax.broadcasted_iota(jnp.int32, x.shape, 0)
        pre = _conv_pre(x, w_ref[...], b_ref[...], row_ids)
        o_ref[...] = pre * _sigmoid(pre)

    outs = pl.pallas_call(
        body, grid=(nct,),
        in_specs=[pl.BlockSpec((t, LANES), lambda j: (0, XBC_COL0 + j)),
                  pl.BlockSpec((SSD_CONV_WIDTH, LANES), lambda j: (0, j)),
                  pl.BlockSpec((1, LANES), lambda j: (0, j))] + hk.in_specs,
        out_specs=[pl.BlockSpec((t, LANES), lambda j: (0, j))] + hk.out_specs,
        out_shape=[jax.ShapeDtypeStruct((t, SSD_CONV_DIM), F32)] + hk.out_shape,
        scratch_shapes=hk.scratch,
        compiler_params=_params(*hk.semantics("parallel")), name=name)(zx, conv_w, conv_b, *hk.inputs)
    return outs[0] if hook is None else (outs[0], outs[1:])


def _conv_bwd(zx, conv_w, conv_b, d_xs, d_bm, d_cm, dzx, *, name):
    t = zx.shape[0]
    nct = SSD_CONV_DIM // LANES
    n_xs = SSD_D_INNER // LANES
    n_bm = SSD_N_GROUPS * SSD_D_STATE // LANES

    def body(x_ref, w_ref, b_ref, dxs_ref, dbm_ref, dcm_ref, _, dx_ref, dw_ref, db_ref):
        x = x_ref[...]
        w = w_ref[...]
        j = pl.program_id(0)
        dy = jnp.where(j < n_xs, dxs_ref[...], jnp.where(j < n_xs + n_bm, dbm_ref[...], dcm_ref[...]))
        row_ids = lax.broadcasted_iota(jnp.int32, x.shape, 0)
        pre = _conv_pre(x, w, b_ref[...], row_ids)
        sg = _sigmoid(pre)
        dpre = dy * (sg * (1.0 + pre * (1.0 - sg)))
        dx = w[3:4, :] * dpre
        for k in (1, 2, 3):
            dx = dx + w[3 - k:4 - k, :] * _shift_up(dpre, k, row_ids)
        dx_ref[...] = dx.astype(dx_ref.dtype)
        db_ref[...] = jnp.sum(dpre, axis=0, keepdims=True)
        dw_ref[3:4, :] = jnp.sum(dpre * x, axis=0, keepdims=True)
        for k in (1, 2, 3):
            dw_ref[3 - k:4 - k, :] = jnp.sum(dpre * _shift_down(x, k, row_ids), axis=0, keepdims=True)

    clip = lambda j, lo, n: jnp.clip(j - lo, 0, n - 1)
    return pl.pallas_call(
        body, grid=(nct,),
        in_specs=[pl.BlockSpec((t, LANES), lambda j: (0, XBC_COL0 + j)),
                  pl.BlockSpec((SSD_CONV_WIDTH, LANES), lambda j: (0, j)),
                  pl.BlockSpec((1, LANES), lambda j: (0, j)),
                  pl.BlockSpec((t, LANES), lambda j: (0, clip(j, 0, n_xs))),
                  pl.BlockSpec((t, LANES), lambda j: (0, clip(j, n_xs, n_bm))),
                  pl.BlockSpec((t, LANES), lambda j: (0, clip(j, n_xs + n_bm, n_bm))), ANY],
        out_specs=[pl.BlockSpec((t, LANES), lambda j: (0, XBC_COL0 + j)),
                   pl.BlockSpec((SSD_CONV_WIDTH, LANES), lambda j: (0, j)), pl.BlockSpec((1, LANES), lambda j: (0, j))],
        out_shape=[jax.ShapeDtypeStruct(dzx.shape, dzx.dtype),
                   jax.ShapeDtypeStruct((SSD_CONV_WIDTH, SSD_CONV_DIM), F32),
                   jax.ShapeDtypeStruct((1, SSD_CONV_DIM), F32)],
        input_output_aliases={6: 0},
        compiler_params=_params("parallel"), name=name)(zx, conv_w, conv_b, d_xs, d_bm, d_cm, dzx)


def _softplus_fwd(zx, bias_row, alog_row, *, name):
    t = zx.shape[0]
    q = SSD_CHUNK
    tr = _row_tile(t, 1024)

    def body(x_ref, b_ref, al_ref, dt_ref, cum_ref):
        v = x_ref[...] + b_ref[...]
        e = jnp.exp(-jnp.abs(v))
        u = 1.0 + e
        log1p = jnp.where(u == 1.0, e, jnp.log(u) * (e / (u - 1.0)))
        dt = jnp.maximum(v, 0.0) + log1p
        dt_ref[...] = dt
        a = dt * -jnp.exp(al_ref[...])
        lower = (lax.broadcasted_iota(jnp.int32, (q, q), 1) <= lax.broadcasted_iota(jnp.int32, (q, q), 0)).astype(F32)
        cums = [lax.dot_general(lower, a[c * q:(c + 1) * q, :], ((((1,), (0,))), ((), ())), precision=lax.Precision.HIGHEST,
                                preferred_element_type=F32) for c in range(tr // q)]
        cum_ref[...] = jnp.concatenate(cums, axis=0)

    blk = pl.BlockSpec((tr, LANES), lambda i: (i, 0))
    vec = pl.BlockSpec((1, LANES), lambda i: (0, 0))
    return pl.pallas_call(
        body, grid=(t // tr,),
        in_specs=[pl.BlockSpec((tr, LANES), lambda i: (i, DT_COL0)), vec, vec],
        out_specs=[blk, blk],
        out_shape=[jax.ShapeDtypeStruct((t, LANES), F32), jax.ShapeDtypeStruct((t, LANES), F32)],
        compiler_params=_params("parallel"), name=name)(zx, bias_row, alog_row)


def _softplus_bwd(zx, bias_row, ddt, dzx, *, name):
    t = zx.shape[0]
    tr = _row_tile(t, 1024)

    def body(x_ref, b_ref, g_ref, _, o_ref, db_ref):
        v = x_ref[...] + b_ref[...]
        lane = lax.broadcasted_iota(jnp.int32, v.shape, 1)
        d = jnp.where(lane < SSD_N_HEADS, g_ref[...] * _sigmoid(v), 0.0)
        o_ref[...] = d.astype(o_ref.dtype)

        @pl.when(pl.program_id(0) == 0)
        def _():
            db_ref[...] = jnp.zeros_like(db_ref)

        db_ref[...] += jnp.sum(d, axis=0, keepdims=True)

    return pl.pallas_call(
        body, grid=(t // tr,),
        in_specs=[pl.BlockSpec((tr, LANES), lambda i: (i, DT_COL0)), pl.BlockSpec((1, LANES), lambda i: (0, 0)),
                  pl.BlockSpec((tr, LANES), lambda i: (i, 0)), ANY],
        out_specs=[pl.BlockSpec((tr, LANES), lambda i: (i, DT_COL0)), pl.BlockSpec((1, LANES), lambda i: (0, 0))],
        out_shape=[jax.ShapeDtypeStruct(dzx.shape, dzx.dtype), jax.ShapeDtypeStruct((1, LANES), F32)],
        input_output_aliases={3: 0},
        compiler_params=_params("arbitrary"), name=name)(zx, bias_row, ddt, dzx)


def _ssd_masks():
    q = SSD_CHUNK
    tt = lax.broadcasted_iota(jnp.int32, (q, q), 0)
    ss = lax.broadcasted_iota(jnp.int32, (q, q), 1)
    lane = lax.broadcasted_iota(jnp.int32, (1, SSD_GW), 1)
    srow = lax.broadcasted_iota(jnp.int32, (SSD_GW, 1), 0)
    hm = [(lane >= SSD_HEAD_DIM * j) & (lane < SSD_HEAD_DIM * (j + 1)) for j in range(SSD_HPG)]
    rm = [(srow >= SSD_HEAD_DIM * j) & (srow < SSD_HEAD_DIM * (j + 1)) for j in range(SSD_HPG)]
    return tt, ss, hm, rm


def _ssd_head_terms(dt_rows, cum_rows, a_rows, j, tt, ss):
    q = SSD_CHUNK
    dt_row = dt_rows[j:j + 1, :]
    dt_col = jnp.sum(jnp.where(tt == ss, dt_row, 0.0), axis=1, keepdims=True)
    a_row1 = a_rows[j:j + 1, :]
    a_11 = a_rows[j:j + 1, 0:1]
    cum_col = jnp.sum(jnp.where(ss <= tt, dt_row * a_row1, 0.0), axis=1, keepdims=True)
    cum_row = cum_rows[j:j + 1, :]
    decay = jnp.exp(jnp.where(ss <= tt, cum_col - cum_row, -jnp.inf))
    cum_last = cum_col[q - 1:q, :]
    e_col = jnp.exp(cum_col)
    dte_col = jnp.exp(cum_last - cum_col)
    e_last = jnp.exp(cum_last)
    return dt_col, dt_row, a_row1, a_11, decay, e_col, dte_col, e_last


SSD_CHUNKS_PER_STEP = 4
SSD_BC_COL0 = SSD_D_INNER // SSD_D_STATE


def _ssd_head_selects(terms, hm, rm):
    e_all = jnp.zeros((SSD_CHUNK, SSD_GW), F32)
    w_all = jnp.zeros((SSD_CHUNK, SSD_GW), F32)
    e_s = jnp.zeros((SSD_GW, 1), F32)
    for j in range(SSD_HPG):
        dt_col, _, _, _, _, e_col, dte_col, e_last = terms[j]
        e_all = jnp.where(hm[j], e_col, e_all)
        w_all = jnp.where(hm[j], dt_col * dte_col, w_all)
        e_s = jnp.where(rm[j], e_last, e_s)
    return e_all, w_all, e_s


def _ssd_fwd(xc, dtr, cumr, alog_b, d_b, *, name, hook=None):
    t = xc.shape[0]
    q = SSD_CHUNK
    nc = t // q
    kc = min(SSD_CHUNKS_PER_STEP, nc)
    rows = kc * q
    hk = _HookSlots(hook, n_in=7, n_out=2, n_scratch=1)

    def body(*refs):
        (x_ref, b_ref, c_ref, dtr_ref, cumr_ref, alog_ref, d_ref), (y_ref, st_ref), (s_scr,) = hk.own(refs)
        if hook is not None:
            hk.run(refs, pl.program_id(0) * (nc // kc) + pl.program_id(1), SSD_N_GROUPS * (nc // kc))

        @pl.when(pl.program_id(1) == 0)
        def _():
            s_scr[...] = jnp.zeros_like(s_scr)

        tt, ss, hm, rm = _ssd_masks()
        a_rows = -jnp.exp(alog_ref[...])
        d_rows = d_ref[...]
        d_all = jnp.zeros((1, SSD_GW), F32)
        for j in range(SSD_HPG):
            d_all = jnp.where(hm[j], d_rows[j:j + 1, 0:1], d_all)
        ks, hs = range(kc), range(SSD_HPG)
        sl = [pl.ds(k * q, q) for k in ks]
        x = [x_ref[sl[k], :] for k in ks]
        bm = [b_ref[sl[k], :].astype(BF16) for k in ks]
        cm = [c_ref[sl[k], :].astype(BF16) for k in ks]
        xb = [x[k].astype(BF16) for k in ks]
        terms = [[_ssd_head_terms(dtr_ref[:, sl[k]], cumr_ref[:, sl[k]], a_rows, j, tt, ss) for j in hs] for k in ks]
        g = [_dot_nt(cm[k], bm[k]) for k in ks]
        m = [[(g[k] * terms[k][j][4] * terms[k][j][1]).astype(BF16) for j in hs] for k in ks]
        yj = [[_dot_nn(m[k][j], xb[k]) for j in hs] for k in ks]
        sel = [_ssd_head_selects(terms[k], hm, rm) for k in ks]
        upd = [_dot_tn((x[k] * sel[k][1]).astype(BF16), bm[k]) for k in ks]
        states = [s_scr[...]]
        for k in ks:
            states.append(states[k] * sel[k][2] + upd[k])
        inter = [_dot_nt(cm[k], states[k].astype(BF16)) for k in ks]
        ys = []
        for k in ks:
            y = jnp.zeros((q, SSD_GW), F32)
            for j in hs:
                y = jnp.where(hm[j], yj[k][j], y)
            ys.append(y + inter[k] * sel[k][0] + x[k] * d_all)
        for k in ks:
            st_ref[k] = states[k]
        y_ref[...] = jnp.concatenate(ys, axis=0)
        s_scr[...] = states[kc]

    blk = lambda width, off: pl.BlockSpec((rows, width), lambda g, c: (c, off + g))
    par_s = pl.BlockSpec((None, SSD_HPG, LANES), lambda g, c: (g, 0, 0))
    row_s = pl.BlockSpec((None, SSD_HPG, rows), lambda g, c: (g, 0, c))
    outs = pl.pallas_call(
        body, grid=(SSD_N_GROUPS, nc // kc),
        in_specs=[blk(SSD_GW, 0), blk(SSD_D_STATE, SSD_BC_COL0), blk(SSD_D_STATE, SSD_BC_COL0 + SSD_N_GROUPS),
                  row_s, row_s, par_s, par_s] + hk.in_specs,
        out_specs=[blk(SSD_GW, 0), pl.BlockSpec((None, kc, SSD_GW, SSD_D_STATE), lambda g, c: (g, c, 0, 0))] + hk.out_specs,
        out_shape=[jax.ShapeDtypeStruct((t, SSD_D_INNER), F32),
                   jax.ShapeDtypeStruct((SSD_N_GROUPS, nc, SSD_GW, SSD_D_STATE), F32)] + hk.out_shape,
        scratch_shapes=[pltpu.VMEM((SSD_GW, SSD_D_STATE), F32)] + hk.scratch,
        compiler_params=_params(*hk.semantics("parallel", "arbitrary")), name=name)(
            xc, xc, xc, dtr, cumr, alog_b, d_b, *hk.inputs)
    return outs if hook is None else (outs[:2], outs[2:])


def _ssd_bwd(xc, dtr, cumr, alog_b, d_b, states, dy, *, name, hook=None):
    t = xc.shape[0]
    q = SSD_CHUNK
    nc = t // q
    kc = min(SSD_CHUNKS_PER_STEP, nc)
    nst = nc // kc
    rows = kc * q
    rev = lambda c: nst - 1 - c
    hk = _HookSlots(hook, n_in=9, n_out=5, n_scratch=1)

    def body(*refs):
        ((x_ref, b_ref, c_ref, dtr_ref, cumr_ref, alog_ref, d_ref, st_ref, dy_ref),
         (dx_ref, db_ref, dc_ref, ddt_ref, dpar_ref), (ds_scr,)) = hk.own(refs)
        if hook is not None:
            hk.run(refs, pl.program_id(0) * nst + pl.program_id(1), SSD_N_GROUPS * nst)

        @pl.when(pl.program_id(1) == 0)
        def _():
            ds_scr[...] = jnp.zeros_like(ds_scr)
            dpar_ref[...] = jnp.zeros_like(dpar_ref)

        tt, ss, hm, rm = _ssd_masks()
        tcol = lax.broadcasted_iota(jnp.int32, (q, 1), 0)
        lane = lax.broadcasted_iota(jnp.int32, (1, LANES), 1)
        a_rows = -jnp.exp(alog_ref[...])
        d_rows = d_ref[...]
        d_all = jnp.zeros((1, SSD_GW), F32)
        for j in range(SSD_HPG):
            d_all = jnp.where(hm[j], d_rows[j:j + 1, 0:1], d_all)
        ks, hs = range(kc), range(SSD_HPG)
        sl = [pl.ds(k * q, q) for k in ks]
        x = [x_ref[sl[k], :] for k in ks]
        dyv = [dy_ref[sl[k], :] for k in ks]
        bm = [b_ref[sl[k], :].astype(BF16) for k in ks]
        cm = [c_ref[sl[k], :].astype(BF16) for k in ks]
        s_in = [st_ref[k] for k in ks]
        xb = [x[k].astype(BF16) for k in ks]
        dyb = [dyv[k].astype(BF16) for k in ks]
        s_b = [s_in[k].astype(BF16) for k in ks]
        terms = [[_ssd_head_terms(dtr_ref[:, sl[k]], cumr_ref[:, sl[k]], a_rows, j, tt, ss) for j in hs] for k in ks]
        sel = [_ssd_head_selects(terms[k], hm, rm) for k in ks]
        e_all, w_all, e_s = [s_[0] for s_ in sel], [s_[1] for s_ in sel], [s_[2] for s_ in sel]
        dye = [(dyv[k] * e_all[k]).astype(BF16) for k in ks]
        ds_loc = [_dot_tn(dye[k], cm[k]) for k in ks]
        ds = [None] * kc
        running = ds_scr[...]
        for k in reversed(ks):
            ds[k] = running
            running = running * e_s[k] + ds_loc[k]
        ds_scr[...] = running
        ds_b = [ds[k].astype(BF16) for k in ks]
        g = [_dot_nt(cm[k], bm[k]) for k in ks]
        cs = [_dot_nt(cm[k], s_b[k]) for k in ks]
        bds = [_dot_nt(bm[k], ds_b[k]) for k in ks]
        dm = [[_dot_nt(jnp.where(hm[j], dyv[k], 0.0).astype(BF16), xb[k]) for j in hs] for k in ks]
        gl = [[g[k] * terms[k][j][4] for j in hs] for k in ks]
        wp = [[dm[k][j] * gl[k][j] for j in hs] for k in ks]
        mt = [[(gl[k][j] * terms[k][j][1]).astype(BF16) for j in hs] for k in ks]
        dxj = [[_dot_tn(mt[k][j], dyb[k]) for j in hs] for k in ks]
        dg = []
        for k in ks:
            acc = jnp.zeros((q, q), F32)
            for j in hs:
                acc = acc + dm[k][j] * terms[k][j][4] * terms[k][j][1]
            dg.append(acc.astype(BF16))
        dy_cs = [dyv[k] * cs[k] for k in ks]
        x_bds = [x[k] * bds[k] for k in ks]
        dy_x = [dyv[k] * x[k] for k in ks]
        ds_s = [ds[k] * s_in[k] for k in ks]
        w = [[wp[k][j] * terms[k][j][1] for j in hs] for k in ks]
        rw_col = [[jnp.sum(w[k][j], axis=1, keepdims=True) for j in hs] for k in ks]
        cw_row = [[jnp.sum(w[k][j], axis=0, keepdims=True) for j in hs] for k in ks]
        cwp_row = [[jnp.sum(wp[k][j], axis=0, keepdims=True) for j in hs] for k in ks]
        r1_col = [[jnp.sum(jnp.where(hm[j], dy_cs[k], 0.0), axis=1, keepdims=True) * terms[k][j][5] for j in hs] for k in ks]
        dw_col = [[jnp.sum(jnp.where(hm[j], x_bds[k], 0.0), axis=1, keepdims=True) for j in hs] for k in ks]
        head_rows = [slice(j * SSD_HEAD_DIM, (j + 1) * SSD_HEAD_DIM) for j in hs]
        lane_sum = lambda v: jnp.sum(v, axis=1, keepdims=True)
        s_sum = [[lane_sum(jnp.sum(ds_s[k][head_rows[j], :], axis=0, keepdims=True)) for j in hs] for k in ks]
        dy_x_cols = [jnp.sum(dy_x[k], axis=0, keepdims=True) for k in ks]
        d_d = [[lane_sum(jnp.where(hm[j], dy_x_cols[k], 0.0)) for j in hs] for k in ks]
        ddt_rows = [[None] * SSD_HPG for _ in ks]
        dpar = [jnp.zeros((1, LANES), F32) for _ in hs]
        for k in ks:
            for j in hs:
                dt_col, dt_row, a_row1, a_11, _, _, dte_col, e_last = terms[k][j]
                dww = dw_col[k][j] * (dt_col * dte_col)
                last_add = jnp.sum(dww, axis=0, keepdims=True) + e_last * s_sum[k][j]
                dcum_col = rw_col[k][j] + r1_col[k][j] - dww + jnp.where(tcol == q - 1, last_add, 0.0)
                da_row = jnp.sum(jnp.where(tt >= ss, dcum_col, 0.0), axis=0, keepdims=True)
                da_col = jnp.sum(jnp.where(ss >= tt, -cw_row[k][j], 0.0), axis=1, keepdims=True)
                ddt_col = a_11 * da_col + dw_col[k][j] * dte_col
                ddt_rows[k][j] = (a_row1 * da_row + cwp_row[k][j]
                                  + jnp.sum(jnp.where(tt == ss, ddt_col, 0.0), axis=0, keepdims=True))
                d_a = jnp.sum(dt_row * da_row, axis=1, keepdims=True) + jnp.sum(dt_col * da_col, axis=0, keepdims=True)
                dpar[j] = dpar[j] + jnp.where(lane == 0, d_a * a_11, 0.0) + jnp.where(lane == 1, d_d[k][j], 0.0)
        dxs = []
        for k in ks:
            acc = jnp.zeros((q, SSD_GW), F32)
            for j in hs:
                acc = jnp.where(hm[j], dxj[k][j], acc)
            dxs.append(acc + w_all[k] * bds[k] + d_all * dyv[k])
        xw = [(x[k] * w_all[k]).astype(BF16) for k in ks]
        dc = [_dot_nn(dg[k], bm[k]) + _dot_nn(dye[k], s_b[k]) for k in ks]
        db = [_dot_tn(dg[k], cm[k]) + _dot_nn(xw[k], ds_b[k]) for k in ks]
        dx_ref[...] = jnp.concatenate(dxs, axis=0)
        dc_ref[...] = jnp.concatenate(dc, axis=0)
        db_ref[...] = jnp.concatenate(db, axis=0)
        ddt_ref[...] = jnp.concatenate([jnp.concatenate([ddt_rows[k][j] for k in ks], axis=1) for j in hs], axis=0)
        dpar_ref[...] += jnp.concatenate(dpar, axis=0)

    blk = lambda width, off: pl.BlockSpec((rows, width), lambda g, c: (rev(c), off + g))
    par_s = pl.BlockSpec((None, SSD_HPG, LANES), lambda g, c: (g, 0, 0))
    outs = pl.pallas_call(
        body, grid=(SSD_N_GROUPS, nst),
        in_specs=[blk(SSD_GW, 0), blk(SSD_D_STATE, SSD_BC_COL0), blk(SSD_D_STATE, SSD_BC_COL0 + SSD_N_GROUPS),
                  pl.BlockSpec((None, SSD_HPG, rows), lambda g, c: (g, 0, rev(c))),
                  pl.BlockSpec((None, SSD_HPG, rows), lambda g, c: (g, 0, rev(c))), par_s, par_s,
                  pl.BlockSpec((None, kc, SSD_GW, SSD_D_STATE), lambda g, c: (g, rev(c), 0, 0)), blk(SSD_GW, 0)] + hk.in_specs,
        out_specs=[blk(SSD_GW, 0), blk(SSD_D_STATE, 0), blk(SSD_D_STATE, 0),
                   pl.BlockSpec((None, SSD_HPG, rows), lambda g, c: (g, 0, rev(c))), par_s] + hk.out_specs,
        out_shape=[jax.ShapeDtypeStruct((t, SSD_D_INNER), F32),
                   jax.ShapeDtypeStruct((t, SSD_N_GROUPS * SSD_D_STATE), F32),
                   jax.ShapeDtypeStruct((t, SSD_N_GROUPS * SSD_D_STATE), F32),
                   jax.ShapeDtypeStruct((SSD_N_GROUPS, SSD_HPG, t), F32),
                   jax.ShapeDtypeStruct((SSD_N_GROUPS, SSD_HPG, LANES), F32)] + hk.out_shape,
        scratch_shapes=[pltpu.VMEM((SSD_GW, SSD_D_STATE), F32)] + hk.scratch,
        compiler_params=_params(*hk.semantics("parallel", "arbitrary")), name=name)(
            xc, xc, xc, dtr, cumr, alog_b, d_b, states, dy, *hk.inputs)
    return outs if hook is None else (outs[:5], outs[5:])


def _gate_norm_fwd(y, zx, norm_w, *, name):
    t = y.shape[0]
    tr = _row_tile(t, 256)
    row = pl.BlockSpec((tr, SSD_D_INNER), lambda i: (i, 0))

    def body(y_ref, z_ref, w_ref, o_ref):
        for gi in range(SSD_N_GROUPS):
            sl = pl.ds(gi * SSD_GW, SSD_GW)
            z = z_ref[:, sl]
            gv = y_ref[:, sl] * (z * _sigmoid(z))
            r = lax.rsqrt(jnp.mean(gv * gv, axis=-1, keepdims=True) + NORM_EPS)
            o_ref[:, sl] = (gv * r * w_ref[:, sl]).astype(BF16)

    return pl.pallas_call(
        body, grid=(t // tr,), in_specs=[row, row, pl.BlockSpec((1, SSD_D_INNER), lambda i: (0, 0))],
        out_specs=row, out_shape=jax.ShapeDtypeStruct((t, SSD_D_INNER), BF16),
        compiler_params=_params("parallel"), name=name)(y, zx, norm_w)


def _gate_norm_bwd(y, zx, norm_w, dyn, *, name):
    t = y.shape[0]
    tr = _row_tile(t, 256)
    row = pl.BlockSpec((tr, SSD_D_INNER), lambda i: (i, 0))
    vec = pl.BlockSpec((1, SSD_D_INNER), lambda i: (0, 0))

    def body(y_ref, z_ref, w_ref, dyn_ref, dy_ref, dz_ref, dw_ref):
        @pl.when(pl.program_id(0) == 0)
        def _():
            dw_ref[...] = jnp.zeros_like(dw_ref)

        for gi in range(SSD_N_GROUPS):
            sl = pl.ds(gi * SSD_GW, SSD_GW)
            z = z_ref[:, sl]
            yv = y_ref[:, sl]
            sg = _sigmoid(z)
            sz = z * sg
            gv = yv * sz
            r = lax.rsqrt(jnp.mean(gv * gv, axis=-1, keepdims=True) + NORM_EPS)
            ghat = gv * r
            dout = dyn_ref[:, sl].astype(F32)
            dgh = dout * w_ref[:, sl]
            dgv = r * (dgh - ghat * jnp.mean(dgh * ghat, axis=-1, keepdims=True))
            dy_ref[:, sl] = dgv * sz
            dz_ref[:, sl] = (dgv * yv * (sg * (1.0 + z * (1.0 - sg)))).astype(dz_ref.dtype)
            dw_ref[:, sl] += jnp.sum(dout * ghat, axis=0, keepdims=True)

    return pl.pallas_call(
        body, grid=(t // tr,), in_specs=[row, row, vec, row], out_specs=[row, row, vec],
        out_shape=[jax.ShapeDtypeStruct((t, SSD_D_INNER), F32), jax.ShapeDtypeStruct((t, SSD_IN_PAD), BF16),
                   jax.ShapeDtypeStruct((1, SSD_D_INNER), F32)],
        compiler_params=_params("arbitrary"), name=name)(y, zx, norm_w, dyn)


ATTN_KV_W = ATTN_N_KV * ATTN_HEAD_DIM
ATTN_Q_HALF = 512
ATTN_K_BLK = ATTN_N_Q * ATTN_HEAD_DIM // ATTN_KV_W
ATTN_V_BLK = ATTN_K_BLK + 1


def _attn_valid(first_block):
    w = ATTN_WINDOW
    qpos = lax.broadcasted_iota(jnp.int32, (w, 2 * w), 0) + w
    kpos = lax.broadcasted_iota(jnp.int32, (w, 2 * w), 1)
    rel = qpos - kpos
    return (rel >= 0) & (rel < w) & jnp.logical_not(first_block & (kpos < w))


def _attn_head_views(lo_ref, hi_ref):
    hd = ATTN_HEAD_DIM
    per_half = ATTN_Q_HALF // hd
    return [(lo_ref if h < per_half else hi_ref)[:, pl.ds((h % per_half) * hd, hd)] for h in range(ATTN_N_Q)]


def _attn_block_views(lo_ref, hi_ref, kc_ref, kp_ref, vc_ref, vp_ref):
    hd = ATTN_HEAD_DIM
    kv_cols = [pl.ds(kh * hd, hd) for kh in range(ATTN_N_KV)]
    kb = [jnp.concatenate([kp_ref[:, c], kc_ref[:, c]], axis=0) for c in kv_cols]
    vb = [jnp.concatenate([vp_ref[:, c], vc_ref[:, c]], axis=0) for c in kv_cols]
    return _attn_head_views(lo_ref, hi_ref), kb, vb


def _attn_scores(q, kb, valid):
    scale = ATTN_HEAD_DIM ** -0.5
    return [jnp.where(valid, _dot_nt(q[h], kb[h // ATTN_REP]) * scale, -jnp.inf) for h in range(ATTN_N_Q)]


def _attn_softmax(s, sink):
    heads = range(ATTN_N_Q)
    m = [jnp.maximum(jnp.max(s[h], axis=1, keepdims=True), sink[h]) for h in heads]
    e = [jnp.exp(s[h] - m[h]) for h in heads]
    es = [jnp.exp(sink[h] - m[h]) for h in heads]
    inv = [1.0 / (jnp.sum(e[h], axis=1, keepdims=True) + es[h]) for h in heads]
    return e, es, inv


def _attn_fwd(qkv, sinks_b, *, name):
    t = qkv.shape[0]
    w = ATTN_WINDOW
    nb = t // w
    prev = lambda n: jnp.maximum(n - 1, 0)

    def body(qlo_ref, qhi_ref, kc_ref, kp_ref, vc_ref, vp_ref, sink_ref, o_ref):
        heads = range(ATTN_N_Q)
        q, kb, vb = _attn_block_views(qlo_ref, qhi_ref, kc_ref, kp_ref, vc_ref, vp_ref)
        sink = [sink_ref[h:h + 1, 0:1] for h in heads]
        e, _, inv = _attn_softmax(_attn_scores(q, kb, _attn_valid(pl.program_id(0) == 0)), sink)
        out = [_dot_nn((e[h] * inv[h]).astype(BF16), vb[h // ATTN_REP]).astype(o_ref.dtype) for h in heads]
        o_ref[...] = jnp.concatenate(out, axis=1)

    qh = lambda half: pl.BlockSpec((w, ATTN_Q_HALF), lambda n: (n, half))
    kv = lambda blk, idx: pl.BlockSpec((w, ATTN_KV_W), lambda n: (idx(n), blk))
    cur = lambda n: n
    return pl.pallas_call(
        body, grid=(nb,),
        in_specs=[qh(0), qh(1), kv(ATTN_K_BLK, cur), kv(ATTN_K_BLK, prev), kv(ATTN_V_BLK, cur), kv(ATTN_V_BLK, prev),
                  pl.BlockSpec((ATTN_N_Q, LANES), lambda n: (0, 0))],
        out_specs=pl.BlockSpec((w, D_MODEL), lambda n: (n, 0)),
        out_shape=jax.ShapeDtypeStruct((t, D_MODEL), BF16),
        compiler_params=_params("parallel"), name=name)(qkv, qkv, qkv, qkv, qkv, qkv, sinks_b)


def _attn_bwd(qkv, sinks_b, dout, *, name):
    t = qkv.shape[0]
    w = ATTN_WINDOW
    nb = t // w
    hd = ATTN_HEAD_DIM
    clamp = lambda n: jnp.minimum(n, nb - 1)
    prev = lambda n: jnp.maximum(clamp(n) - 1, 0)

    def body(qlo_ref, qhi_ref, kc_ref, kp_ref, vc_ref, vp_ref, sink_ref, dolo_ref, dohi_ref,
             dq_ref, dkv_ref, dsink_ref, carry):
        n = pl.program_id(0)

        @pl.when(n == 0)
        def _():
            carry[...] = jnp.zeros_like(carry)
            dsink_ref[...] = jnp.zeros_like(dsink_ref)

        @pl.when(n < nb)
        def _():
            heads, kvs = range(ATTN_N_Q), range(ATTN_N_KV)
            q, kb, vb = _attn_block_views(qlo_ref, qhi_ref, kc_ref, kp_ref, vc_ref, vp_ref)
            do = _attn_head_views(dolo_ref, dohi_ref)
            sink = [sink_ref[h:h + 1, 0:1] for h in heads]
            s = _attn_scores(q, kb, _attn_valid(n == 0))
            dp = [_dot_nt(do[h], vb[h // ATTN_REP]) for h in heads]
            e, es, inv = _attn_softmax(s, sink)
            p = [e[h] * inv[h] for h in heads]
            delta = [jnp.sum(p[h] * dp[h], axis=1, keepdims=True) for h in heads]
            dsc = [(p[h] * (dp[h] - delta[h]) * (hd ** -0.5)).astype(BF16) for h in heads]
            pb = [p[h].astype(BF16) for h in heads]
            dq = [_dot_nn(dsc[h], kb[h // ATTN_REP]).astype(dq_ref.dtype) for h in heads]
            stack = lambda per_head, kh: jnp.concatenate(per_head[kh * ATTN_REP:(kh + 1) * ATTN_REP], axis=0)
            dkb = [_dot_tn(stack(dsc, kh), stack(q, kh)) for kh in kvs]
            dvb = [_dot_tn(stack(pb, kh), stack(do, kh)) for kh in kvs]
            dsink = [jnp.broadcast_to(jnp.sum(-es[h] * inv[h] * delta[h], axis=0, keepdims=True), (1, LANES)) for h in heads]
            dq_ref[...] = jnp.concatenate(dq, axis=1)
            dsink_ref[...] += jnp.concatenate(dsink, axis=0)
            dkv_ref[...] = (carry[...] + jnp.concatenate([d[0:w, :] for d in dkb + dvb], axis=1)).astype(dkv_ref.dtype)
            carry[...] = jnp.concatenate([d[w:2 * w, :] for d in dkb + dvb], axis=1)

        @pl.when(n == nb)
        def _():
            dkv_ref[...] = carry[...].astype(dkv_ref.dtype)

    qh = lambda half: pl.BlockSpec((w, ATTN_Q_HALF), lambda n: (clamp(n), half))
    kv = lambda blk, idx: pl.BlockSpec((w, ATTN_KV_W), lambda n: (idx(n), blk))
    return pl.pallas_call(
        body, grid=(nb + 1,),
        in_specs=[qh(0), qh(1), kv(ATTN_K_BLK, clamp), kv(ATTN_K_BLK, prev), kv(ATTN_V_BLK, clamp), kv(ATTN_V_BLK, prev),
                  pl.BlockSpec((ATTN_N_Q, LANES), lambda n: (0, 0)), qh(0), qh(1)],
        out_specs=[pl.BlockSpec((w, D_MODEL), lambda n: (clamp(n), 0)),
                   pl.BlockSpec((w, 2 * ATTN_KV_W), lambda n: (jnp.maximum(n - 1, 0), 0)),
                   pl.BlockSpec((ATTN_N_Q, LANES), lambda n: (0, 0))],
        out_shape=[jax.ShapeDtypeStruct((t, D_MODEL), BF16), jax.ShapeDtypeStruct((t, 2 * ATTN_KV_W), BF16),
                   jax.ShapeDtypeStruct((ATTN_N_Q, LANES), F32)],
        scratch_shapes=[pltpu.VMEM((w, 2 * ATTN_KV_W), F32)],
        compiler_params=_params("arbitrary"), name=name)(qkv, qkv, qkv, qkv, qkv, qkv, sinks_b, dout, dout)


def _sq_relu_epilogue(acc):
    r = jnp.maximum(acc, 0.0)
    return (r * r,)


def _sq_relu_bwd_epilogue(acc, act):
    return (acc * (2.0 * jnp.sqrt(act.astype(F32))),)


def _bias_epilogue(acc, bias):
    return (acc + bias,)


def _plain_run(stage, fn, *args, **kwargs):
    return fn(*args, **kwargs)


def _mlp_fwd(u, w_up, w_down, tag, run=_plain_run):
    act = run(f"mlp_up_{tag}", _matmul, u, w_up, mode="nn", out_dtypes=(BF16,), epilogue=_sq_relu_epilogue, b_shards=True,
              name=f"mlp_up_{tag}")
    f = run(f"mlp_down_{tag}", _matmul, act, w_down, mode="nn", out_dtypes=(F32,), name=f"mlp_down_{tag}")
    return act, f


def _mlp_bwd(u, act, w_up, w_down, df, tag):
    dpre = _matmul(df, w_down, mode="nt", out_dtypes=(BF16,), epilogue=_sq_relu_bwd_epilogue,
                   extras=((act, "tile"),), name=f"mlp_dact_{tag}")
    dw_down = _matmul(act, df, mode="tn", out_dtypes=(BF16,), name=f"mlp_dwdown_{tag}")
    du = _matmul(dpre, w_up, mode="nt", out_dtypes=(F32,), b_shards=True, name=f"mlp_du_{tag}")
    dw_up = _matmul(u, dpre, mode="tn", out_dtypes=(BF16,), out_shards=True, name=f"mlp_dwup_{tag}")
    return du, dw_up, dw_down


def _group_rows(dt):
    t = dt.shape[0]
    return jnp.transpose(dt[:, :SSD_N_HEADS].reshape(t, SSD_N_GROUPS, SSD_HPG), (1, 2, 0))


def _head_param_rows(p):
    return jnp.broadcast_to(p.reshape(SSD_N_GROUPS, SSD_HPG, 1), (SSD_N_GROUPS, SSD_HPG, LANES))


def _local_step(x, target, wts, comm=None):
    t = x.shape[0]
    wts = dict(wts)
    row = lambda v: v.reshape(1, -1)
    mix_pre, mix_post, ffn_pre, ffn_post = wts["mix_pre_norm"], wts["mix_post_norm"], wts["ffn_pre_norm"], wts["ffn_post_norm"]

    def gathering(stage, fn, *args, **kwargs):
        hook = comm.gather_hook(stage) if comm is not None else None
        if hook is None:
            return fn(*args, **kwargs)
        out, got = fn(*args, hook=hook, **kwargs)
        wts.update(comm.weights_from(stage, got))
        return out

    u0 = _rms_fwd(x, row(mix_pre[0]), name="rms_pre_mix0")
    zx = gathering("in_proj", _matmul, u0, wts["ssd_w_in"], mode="nn", out_dtypes=(F32,), tn=896, name="ssd_in_proj")
    xc = gathering("conv", _conv_fwd, zx, wts["ssd_conv_w"], row(wts["ssd_conv_b"]), name="ssd_conv_fwd")
    bias_row = jnp.pad(wts["ssd_dt_bias"], (0, LANES - SSD_N_HEADS)).reshape(1, LANES)
    alog_row = jnp.pad(wts["ssd_a_log"], (0, LANES - SSD_N_HEADS)).reshape(1, LANES)
    dt, cum = _softplus_fwd(zx, bias_row, alog_row, name="ssd_dt_fwd")
    dtr, cumr = _group_rows(dt), _group_rows(cum)
    alog_b, d_b = _head_param_rows(wts["ssd_a_log"]), _head_param_rows(wts["ssd_d"])
    y_ssd, states = gathering("scan", _ssd_fwd, xc, dtr, cumr, alog_b, d_b, name="ssd_scan_fwd")
    norm_w = row(wts["ssd_norm_w"])
    yn = _gate_norm_fwd(y_ssd, zx, norm_w, name="ssd_gate_norm_fwd")
    mix0 = _matmul(yn, wts["ssd_w_out"], mode="nn", out_dtypes=(F32,), name="ssd_out_proj")
    h1, v0 = _rms_fwd(mix0, row(mix_post[0]), resid=x, want_u=row(ffn_pre[0]), name="rms_post_mix0")
    act0, f0 = _mlp_fwd(v0, wts["mlp_w_up0"], wts["mlp_w_down0"], "l0", run=gathering)
    h2, u1 = _rms_fwd(f0, row(ffn_post[0]), resid=h1, want_u=row(mix_pre[1]), name="rms_post_ffn0")

    qkv = _matmul(u1, wts["attn_w_qkv"], mode="nn", out_dtypes=(BF16,), epilogue=_bias_epilogue,
                  extras=((row(wts["attn_b_qkv"]), "row"),), b_shards=True, name="attn_qkv_proj")
    sinks_b = jnp.broadcast_to(wts["attn_sinks"].reshape(ATTN_N_Q, 1), (ATTN_N_Q, LANES))
    ao = _attn_fwd(qkv, sinks_b, name="attn_fwd")
    mix1 = _matmul(ao, wts["attn_w_o"], mode="nn", out_dtypes=(F32,), epilogue=_bias_epilogue,
                   extras=((row(wts["attn_b_o"]), "row"),), name="attn_out_proj")
    h3, v1 = _rms_fwd(mix1, row(mix_post[1]), resid=h2, want_u=row(ffn_pre[1]), name="rms_post_mix1")
    act1, f1 = _mlp_fwd(v1, wts["mlp_w_up1"], wts["mlp_w_down1"], "l1")
    dh4, loss_tile = _rms_fwd(f1, row(ffn_post[1]), resid=h3, target=target, name="rms_post_ffn1_loss")

    df1, g_ffn_post1 = _rms_bwd(f1, row(ffn_post[1]), dh4, out_dtype=BF16, name="rms_post_ffn1_bwd")
    dv1, g_up1, g_down1 = _mlp_bwd(v1, act1, wts["mlp_w_up1"], wts["mlp_w_down1"], df1, "l1")
    dh3, g_ffn_pre1 = _rms_bwd(h3, row(ffn_pre[1]), dv1, resid=dh4, name="rms_pre_ffn1_bwd")
    dmix1, g_mix_post1, g_b_o = _rms_bwd(mix1, row(mix_post[1]), dh3, out_dtype=BF16, dx_col_sum=True, name="rms_post_mix1_bwd")
    g_w_o = _matmul(ao, dmix1, mode="tn", out_dtypes=(BF16,), name="attn_dwo")
    dao = _matmul(dmix1, wts["attn_w_o"], mode="nt", out_dtypes=(BF16,), name="attn_dao")
    dq, dkv, g_sinks = _attn_bwd(qkv, sinks_b, dao, name="attn_bwd")
    dqkv = jnp.concatenate([dq, dkv], axis=1)
    g_b_qkv = _col_sum(dqkv, name="attn_bqkv_grad")
    g_w_qkv = _matmul(u1, dqkv, mode="tn", out_dtypes=(BF16,), tn=ATTN_QKV // N_CHIPS, out_shards=True, name="attn_dwqkv")
    du1 = _matmul(dqkv, wts["attn_w_qkv"], mode="nt", out_dtypes=(F32,), b_shards=True, name="attn_du")
    dh2, g_mix_pre1 = _rms_bwd(h2, row(mix_pre[1]), du1, resid=dh3, name="rms_pre_mix1_bwd")

    df0, g_ffn_post0 = _rms_bwd(f0, row(ffn_post[0]), dh2, out_dtype=BF16, name="rms_post_ffn0_bwd")
    dv0, g_up0, g_down0 = _mlp_bwd(v0, act0, wts["mlp_w_up0"], wts["mlp_w_down0"], df0, "l0")
    dh1, g_ffn_pre0 = _rms_bwd(h1, row(ffn_pre[0]), dv0, resid=dh2, name="rms_pre_ffn0_bwd")
    dmix0, g_mix_post0 = _rms_bwd(mix0, row(mix_post[0]), dh1, out_dtype=BF16, name="rms_post_mix0_bwd")
    g_w_out = _matmul(yn, dmix0, mode="tn", out_dtypes=(BF16,), name="ssd_dwout")
    dyn = _matmul(dmix0, wts["ssd_w_out"], mode="nt", out_dtypes=(BF16,), name="ssd_dyn")
    dy_ssd, dzx, g_norm_w = _gate_norm_bwd(y_ssd, zx, norm_w, dyn, name="ssd_gate_norm_bwd")
    mats = {"ssd_w_out": g_w_out, "attn_w_qkv": g_w_qkv, "attn_w_o": g_w_o,
            "mlp_w_up0": g_up0, "mlp_w_up1": g_up1, "mlp_w_down0": g_down0, "mlp_w_down1": g_down1}
    if comm is None:
        dxc, dbm, dcm, ddt_r, dpar = _ssd_bwd(xc, dtr, cumr, alog_b, d_b, states, dy_ssd, name="ssd_scan_bwd")
    else:
        (dxc, dbm, dcm, ddt_r, dpar), received = _ssd_bwd(xc, dtr, cumr, alog_b, d_b, states, dy_ssd,
                                                          name="ssd_scan_bwd", hook=comm.exchange_hook(mats, "early"))
        comm.received(received)
    dzx, g_conv_w, g_conv_b = _conv_bwd(zx, wts["ssd_conv_w"], row(wts["ssd_conv_b"]), dxc, dbm, dcm, dzx, name="ssd_conv_bwd")
    ddt = jnp.pad(jnp.transpose(ddt_r, (2, 0, 1)).reshape(t, SSD_N_HEADS), ((0, 0), (0, LANES - SSD_N_HEADS)))
    dzx, g_dt_bias = _softplus_bwd(zx, bias_row, ddt, dzx, name="ssd_dt_bwd")
    g_w_in = _w_in_to_shards(_matmul(u0, dzx, mode="tn", out_dtypes=(F32,), tn=896, name="ssd_dwin"), name="ssd_dwin_shards")
    mats["ssd_w_in"] = g_w_in
    if comm is None:
        du0 = _matmul(dzx, wts["ssd_w_in"], mode="nt", out_dtypes=(F32,), tk=896, name="ssd_du")
    else:
        du0, received = _matmul(dzx, wts["ssd_w_in"], mode="nt", out_dtypes=(F32,), tk=896, name="ssd_du",
                                hook=comm.exchange_hook(mats, "late"))
        comm.received(received)
    grad_x, g_mix_pre0 = _rms_bwd(x, row(mix_pre[0]), du0, resid=dh1, name="rms_pre_mix0_bwd")

    dpar = dpar.reshape(SSD_N_HEADS, LANES)
    vecs = {
        "ssd_conv_w": g_conv_w, "ssd_conv_b": g_conv_b.reshape(-1),
        "ssd_dt_bias": g_dt_bias[0, :SSD_N_HEADS], "ssd_a_log": dpar[:, 0], "ssd_d": dpar[:, 1],
        "ssd_norm_w": g_norm_w.reshape(-1), "attn_b_qkv": g_b_qkv.reshape(-1), "attn_sinks": g_sinks[:, 0],
        "attn_b_o": g_b_o.reshape(-1),
        "mix_pre_norm": jnp.concatenate([g_mix_pre0, g_mix_pre1]), "mix_post_norm": jnp.concatenate([g_mix_post0, g_mix_post1]),
        "ffn_pre_norm": jnp.concatenate([g_ffn_pre0, g_ffn_pre1]), "ffn_post_norm": jnp.concatenate([g_ffn_post0, g_ffn_post1]),
    }
    return loss_tile, grad_x, mats, vecs


def _mesh_position():
    return lax.axis_index("x"), lax.axis_index("y"), lax.axis_index("c")


def _flip(v, bit):
    return 1 - v if bit else v


OTHER_CHIPS = ((1, 0), (0, 1), (1, 1))


def _comm_params():
    return pltpu.CompilerParams(vmem_limit_bytes=VMEM_LIMIT)


def _staged_copies(srcs, dsts, bufs, sems_in, sems_out):
    loads = [pltpu.make_async_copy(s, b, sems_in.at[i]) for i, (s, b) in enumerate(zip(srcs, bufs))]
    stores = [pltpu.make_async_copy(b, d, sems_out.at[i]) for i, (b, d) in enumerate(zip(bufs, dsts))]
    return loads, stores


class _GatherHook:
    def __init__(self, mats, vecs=()):
        self.arrs = list(mats) + list(vecs)
        self.nm, self.n = len(mats), len(self.arrs)
        n_ici, n_fwd = (N_CHIPS - 1) * self.n, max((N_CHIPS - 1) * self.nm, 1)
        dma = pltpu.SemaphoreType.DMA
        self.out_shape = [jax.ShapeDtypeStruct((N_CHIPS,) + a.shape, a.dtype) for a in self.arrs]
        self.scratch = [pltpu.VMEM(a.shape, a.dtype) for a in self.arrs] + [
            dma((n_ici,)), dma((n_ici,)), dma((n_fwd,)), dma((n_fwd,)), dma((self.n,)), dma((self.n,))]

    def plan(self, ins, outs, scratch):
        n, nm = self.n, self.nm
        bufs = scratch[:n]
        ici_send, ici_recv, fwd_send, fwd_recv, load_sems, store_sems = scratch[n:]
        xi, yi, ci = _mesh_position()
        me = 2 * xi + yi
        loads, stores = _staged_copies(ins, [outs[i].at[me] for i in range(n)], bufs, load_sems, store_sems)
        sends, landed, forwards, from_sibling = [], [], [], []
        for j, (bx, by) in enumerate(OTHER_CHIPS):
            px, py = _flip(xi, bx), _flip(yi, by)
            peer = 2 * px + py
            for i in range(n):
                k = j * n + i
                mk = functools.partial(pltpu.make_async_remote_copy, send_sem=ici_send.at[k], recv_sem=ici_recv.at[k],
                                       device_id=(px, py, ci), device_id_type=MESH)
                if i < nm:
                    sends.append(mk(src_ref=ins[i].at[ci], dst_ref=outs[i].at[me, ci]))
                    landed.append(mk(src_ref=ins[i].at[ci], dst_ref=outs[i].at[peer, ci]))
                    kf = j * nm + i
                    fw = functools.partial(pltpu.make_async_remote_copy, send_sem=fwd_send.at[kf], recv_sem=fwd_recv.at[kf],
                                           device_id=(xi, yi, 1 - ci), device_id_type=MESH)
                    forwards.append(fw(src_ref=outs[i].at[peer, ci], dst_ref=outs[i].at[peer, ci]))
                    from_sibling.append(fw(src_ref=outs[i].at[peer, ci], dst_ref=outs[i].at[peer, 1 - ci]))
                else:
                    sends.append(mk(src_ref=ins[i], dst_ref=outs[i].at[me]))
                    landed.append(mk(src_ref=ins[i], dst_ref=outs[i].at[peer]))
                    forwards.append(None)
        return loads, stores, sends, landed, forwards, from_sibling

    @staticmethod
    def start(p):
        loads, _, sends, _, _, _ = p
        for cp in loads + sends:
            cp.start()

    @staticmethod
    def relay(p):
        loads, stores, _, landed, forwards, _ = p
        for ld, st in zip(loads, stores):
            ld.wait()
            st.start()
        for cp, fw in zip(landed, forwards):
            cp.wait_recv()
            if fw is not None:
                fw.start()

    @staticmethod
    def finish(p):
        _, stores, sends, _, forwards, from_sibling = p
        for cp in from_sibling:
            cp.wait_recv()
        for cp in sends + [fw for fw in forwards if fw is not None]:
            cp.wait_send()
        for st in stores:
            st.wait()


def _run_hook(hook, ins, outs, scratch, step, n_steps):
    p = hook.plan(ins, outs, scratch)
    relay_step = min(max(1, (3 * n_steps) // 4), n_steps - 1)

    @pl.when(step == 0)
    def _():
        hook.start(p)

    if relay_step < n_steps - 1:
        @pl.when(step == relay_step)
        def _():
            hook.relay(p)

    @pl.when(step == n_steps - 1)
    def _():
        if relay_step == n_steps - 1:
            hook.relay(p)
        hook.finish(p)


def _hook_call(hook, *, name):
    n = len(hook.arrs)

    def body(*refs):
        p = hook.plan(refs[:n], refs[n:n + len(hook.out_shape)], refs[n + len(hook.out_shape):])
        hook.start(p)
        hook.relay(p)
        hook.finish(p)

    return pl.pallas_call(
        body, in_specs=[ANY] * n, out_specs=[ANY] * len(hook.out_shape), out_shape=hook.out_shape,
        scratch_shapes=hook.scratch, compiler_params=_comm_params(), name=name)(*hook.arrs)


def _send_other_half(parts, *, name):
    n = len(parts)

    def body(*refs):
        ins, outs = refs[:n], refs[n:2 * n]
        send_sems, recv_sems = refs[2 * n:]
        xi, yi, ci = _mesh_position()
        sibling = (xi, yi, 1 - ci)
        for i in range(n):
            for s in range(N_CHIPS):
                pltpu.make_async_remote_copy(src_ref=ins[i].at[s, 1 - ci], dst_ref=outs[i].at[s], send_sem=send_sems.at[i],
                                             recv_sem=recv_sems.at[i], device_id=sibling, device_id_type=MESH).start()
        for i in range(n):
            pltpu.make_async_remote_copy(src_ref=outs[i], dst_ref=outs[i], send_sem=send_sems.at[i], recv_sem=recv_sems.at[i],
                                         device_id=sibling, device_id_type=MESH).wait()

    return pl.pallas_call(
        body, in_specs=[ANY] * n, out_specs=[ANY] * n,
        out_shape=[jax.ShapeDtypeStruct((p.shape[0],) + p.shape[2:], p.dtype) for p in parts],
        scratch_shapes=[pltpu.SemaphoreType.DMA((n,)), pltpu.SemaphoreType.DMA((n,))],
        name=name)(*parts)


ROW_BLOCKS = 8


def _add_sibling_half(parts, theirs, core, *, name):
    n = len(parts)

    def body(core_ref, *refs):
        for a_ref, b_ref, o_ref in zip(refs[:n], refs[n:2 * n], refs[2 * n:]):
            o_ref[...] = (a_ref[...].astype(F32) + b_ref[...].astype(F32)).astype(o_ref.dtype)

    mine = lambda p: pl.BlockSpec((None, None, p.shape[2] // ROW_BLOCKS, p.shape[3]), lambda s, rb, core_ref: (s, core_ref[0], rb, 0))
    other = lambda p: pl.BlockSpec((None, p.shape[1] // ROW_BLOCKS, p.shape[2]), lambda s, rb, core_ref: (s, rb, 0))
    return pl.pallas_call(
        body,
        grid_spec=pltpu.PrefetchScalarGridSpec(
            num_scalar_prefetch=1, grid=(N_CHIPS, ROW_BLOCKS),
            in_specs=[mine(p) for p in parts] + [other(q) for q in theirs], out_specs=[other(q) for q in theirs]),
        out_shape=[jax.ShapeDtypeStruct(q.shape, BF16) for q in theirs],
        compiler_params=_params("parallel", "parallel"), name=name)(core, *parts, *theirs)


class _ExchangeHook:
    def __init__(self, parts, to_all=()):
        self.arrs = list(parts) + list(to_all)
        self.n_parts, self.n = len(parts), len(self.arrs)
        n_ici, n_peer = max((N_CHIPS - 1) * self.n_parts, 1), (N_DEV - 1) * max(len(to_all), 1)
        dma = pltpu.SemaphoreType.DMA
        self.out_shape = [jax.ShapeDtypeStruct(p.shape, p.dtype) for p in parts] + [
            jax.ShapeDtypeStruct((N_DEV,) + a.shape, a.dtype) for a in to_all]
        self.scratch = [pltpu.VMEM(p.shape[1:], p.dtype) for p in parts] + [pltpu.VMEM(a.shape, a.dtype) for a in to_all] + [
            dma((n_ici,)), dma((n_ici,)), dma((n_peer,)), dma((n_peer,)), dma((self.n,)), dma((self.n,))]

    def plan(self, ins, outs, scratch):
        n, npt = self.n, self.n_parts
        bufs = scratch[:n]
        send_sems, recv_sems, all_send, all_recv, load_sems, store_sems = scratch[n:]
        xi, yi, ci = _mesh_position()
        me_chip = 2 * xi + yi
        me = 4 * xi + 2 * yi + ci
        loads, stores = _staged_copies([ins[i].at[me_chip] for i in range(npt)] + list(ins[npt:]),
                                       [outs[i].at[me_chip] for i in range(npt)] + [outs[i].at[me] for i in range(npt, n)],
                                       bufs, load_sems, store_sems)
        sends, recvs = [], []
        for j, (bx, by) in enumerate(OTHER_CHIPS):
            px, py = _flip(xi, bx), _flip(yi, by)
            peer = 2 * px + py
            for i in range(npt):
                k = j * npt + i
                mk = functools.partial(pltpu.make_async_remote_copy, src_ref=ins[i].at[peer], send_sem=send_sems.at[k],
                                       recv_sem=recv_sems.at[k], device_id=(px, py, ci), device_id_type=MESH)
                sends.append(mk(dst_ref=outs[i].at[me_chip]))
                recvs.append(mk(dst_ref=outs[i].at[peer]))
        for i in range(npt, n):
            for k in range(1, N_DEV):
                px, py, pc = _flip(xi, (k >> 2) & 1), _flip(yi, (k >> 1) & 1), _flip(ci, k & 1)
                slot = (i - npt) * (N_DEV - 1) + k - 1
                mk = functools.partial(pltpu.make_async_remote_copy, src_ref=ins[i], send_sem=all_send.at[slot],
                                       recv_sem=all_recv.at[slot], device_id=(px, py, pc), device_id_type=MESH)
                sends.append(mk(dst_ref=outs[i].at[me]))
                recvs.append(mk(dst_ref=outs[i].at[4 * px + 2 * py + pc]))
        return loads, stores, sends, recvs

    @staticmethod
    def start(p):
        loads, _, sends, _ = p
        for cp in loads + sends:
            cp.start()

    @staticmethod
    def relay(p):
        loads, stores, _, _ = p
        for ld, st in zip(loads, stores):
            ld.wait()
            st.start()

    @staticmethod
    def finish(p):
        _, stores, sends, recvs = p
        for cp in recvs:
            cp.wait_recv()
        for cp in sends:
            cp.wait_send()
        for st in stores:
            st.wait()


def _sum_chips(parts, *, name):
    n = len(parts)
    p = parts[0].shape[0]

    def body(*refs):
        s = pl.program_id(1)
        for x_ref, o_ref in zip(refs[:n], refs[n:]):
            @pl.when(s == 0)
            def _():
                o_ref[...] = x_ref[...].astype(F32)

            @pl.when(s > 0)
            def _():
                o_ref[...] += x_ref[...].astype(F32)

    blocks = lambda q: ROW_BLOCKS if q.shape[1] % (8 * ROW_BLOCKS) == 0 else 1
    assert len({blocks(q) for q in parts}) == 1
    nb = blocks(parts[0])
    return pl.pallas_call(
        body, grid=(nb, p),
        in_specs=[pl.BlockSpec((None, q.shape[1] // nb, q.shape[2]), lambda rb, s: (s, rb, 0)) for q in parts],
        out_specs=[pl.BlockSpec((q.shape[1] // nb, q.shape[2]), lambda rb, s: (rb, 0)) for q in parts],
        out_shape=[jax.ShapeDtypeStruct(q.shape[1:], F32) for q in parts],
        compiler_params=_params("parallel", "arbitrary"), name=name)(*parts)


def _swap_halves(halves, layers, *, name):
    n = len(halves)
    out_shapes, slots = [], []
    for i, h in enumerate(halves):
        pair = [p for p in layers if i in p]
        if pair and pair[0][1] == i:
            slots.append((slots[pair[0][0]][0], 1))
        elif pair:
            out_shapes.append(jax.ShapeDtypeStruct((2, 2) + h.shape, h.dtype))
            slots.append((len(out_shapes) - 1, 0))
        else:
            out_shapes.append(jax.ShapeDtypeStruct((2,) + h.shape, h.dtype))
            slots.append((len(out_shapes) - 1, None))
    n_out = len(out_shapes)

    def body(*refs):
        ins, outs, bufs = refs[:n], refs[n:n + n_out], refs[n + n_out:2 * n + n_out]
        send_sems, recv_sems, load_sems, store_sems = refs[2 * n + n_out:]
        xi, yi, ci = _mesh_position()
        own, sends, recvs = [], [], []
        for i in range(n):
            o, layer = slots[i]
            dst = (lambda core: outs[o].at[core]) if layer is None else (lambda core: outs[o].at[layer, core])
            own.append(dst(ci))
            mk = functools.partial(pltpu.make_async_remote_copy, src_ref=ins[i], send_sem=send_sems.at[i],
                                   recv_sem=recv_sems.at[i], device_id=(xi, yi, 1 - ci), device_id_type=MESH)
            sends.append(mk(dst_ref=dst(ci)))
            recvs.append(mk(dst_ref=dst(1 - ci)))
        loads, stores = _staged_copies(ins, own, bufs, load_sems, store_sems)
        for cp in loads + sends:
            cp.start()
        for ld, st in zip(loads, stores):
            ld.wait()
            st.start()
        for cp in recvs:
            cp.wait_recv()
        for cp in sends:
            cp.wait_send()
        for st in stores:
            st.wait()

    return pl.pallas_call(
        body, in_specs=[ANY] * n, out_specs=[ANY] * n_out, out_shape=out_shapes,
        scratch_shapes=[pltpu.VMEM(h.shape, h.dtype) for h in halves]
        + [pltpu.SemaphoreType.DMA((n,)), pltpu.SemaphoreType.DMA((n,)), pltpu.SemaphoreType.DMA((n,)), pltpu.SemaphoreType.DMA((n,))],
        compiler_params=_comm_params(), name=name)(*halves)


def _cast_bf16(layers, *, name, hook=None):
    n = len(layers)
    hk = _HookSlots(hook, n_in=n, n_out=n, n_scratch=0)

    def body(*refs):
        ins, outs, _ = hk.own(refs)
        if hook is not None:
            hk.run(refs, pl.program_id(0), ROW_BLOCKS)
        for i_ref, o_ref in zip(ins, outs):
            o_ref[...] = i_ref[...].astype(o_ref.dtype)

    in_blk = lambda a, l: pl.BlockSpec((None, a.shape[1] // ROW_BLOCKS, a.shape[2]), lambda i: (l, i, 0))
    out_blk = lambda a: pl.BlockSpec((a.shape[1] // ROW_BLOCKS, a.shape[2]), lambda i: (i, 0))
    outs = pl.pallas_call(
        body, grid=(ROW_BLOCKS,),
        in_specs=[in_blk(a, l) for a, l in layers] + hk.in_specs,
        out_specs=[out_blk(a) for a, _ in layers] + hk.out_specs,
        out_shape=[jax.ShapeDtypeStruct(a.shape[1:], BF16) for a, _ in layers] + hk.out_shape,
        scratch_shapes=hk.scratch,
        compiler_params=_params(*hk.semantics("parallel")), name=name)(*[a for a, _ in layers], *hk.inputs)
    return outs[:n] if hook is None else (outs[:n], outs[n:])


def _full_weight(name, gathered):
    s, _, r, c = gathered.shape
    if name == "ssd_w_in":
        return _w_in_from_shards(gathered.reshape(s, 2 * r, c), name="ssd_w_in_unshard")
    if name in ("attn_w_qkv", "mlp_w_up0", "mlp_w_up1"):
        return gathered.reshape(s, 2 * r, c)
    return gathered.reshape(s * 2 * r, c)


class _StepComm:
    GATHER = {"in_proj": ("mlp_w_up0", "attn_w_qkv"), "conv": ("mlp_w_down0", "attn_w_o"), "scan": ("ssd_w_out",),
              "mlp_up_l0": ("mlp_w_up1",), "mlp_down_l0": ("mlp_w_down1",)}
    EXCHANGE = {"early": ("ssd_w_out", "attn_w_qkv", "attn_w_o", "mlp_w_up0", "mlp_w_up1", "mlp_w_down0", "mlp_w_down1"),
                "late": ("ssd_w_in",)}

    def __init__(self, shards, core):
        self.shards, self.core = shards, core
        self.chip_parts = {}
        self._pending = None

    def gather_hook(self, stage):
        names = self.GATHER.get(stage)
        return _GatherHook([self.shards[n] for n in names]) if names else None

    def weights_from(self, stage, gathered):
        return {n: _full_weight(n, g) for n, g in zip(self.GATHER[stage], gathered)}

    def chip_sums(self, mats, tag):
        parts = [_shard_halves(a) for a in mats.values()]
        theirs = _send_other_half(parts, name=f"grad_sibling_send_{tag}")
        return _add_sibling_half(parts, theirs, self.core, name=f"grad_chip_sum_{tag}")

    def exchange_hook(self, mats, which):
        self._pending = self.EXCHANGE[which]
        return _ExchangeHook(self.chip_sums({n: mats[n] for n in self._pending}, which))

    def received(self, arrays):
        self.chip_parts.update(zip(self._pending, arrays))


ADAMW_ROW_BLOCKS = 16


def _adamw(ws, gs, ms, vs, *, name):
    n = len(ws)
    nb = ADAMW_ROW_BLOCKS if all(a.shape[0] % (8 * ADAMW_ROW_BLOCKS) == 0 for a in ws) else 1

    def body(*refs):
        ins, outs = refs[:4 * n], refs[4 * n:]
        for i in range(n):
            w_ref, g_ref, m_ref, v_ref = ins[i], ins[n + i], ins[2 * n + i], ins[3 * n + i]
            go_ref, d_ref, nm_ref, nv_ref = outs[i], outs[n + i], outs[2 * n + i], outs[3 * n + i]
            gv = g_ref[...]
            nm = ADAM_B1 * m_ref[...] + (1.0 - ADAM_B1) * gv
            nv = ADAM_B2 * v_ref[...] + (1.0 - ADAM_B2) * (gv * gv)
            m_hat = nm / (1.0 - ADAM_B1 ** ADAM_STEP)
            v_hat = nv / (1.0 - ADAM_B2 ** ADAM_STEP)
            go_ref[...] = gv
            d_ref[...] = -ADAM_LR * (m_hat / (jnp.sqrt(v_hat) + ADAM_EPS) + ADAM_WD * w_ref[...])
            nm_ref[...] = nm
            nv_ref[...] = nv

    blks = [pl.BlockSpec((a.shape[0] // nb, a.shape[1]), lambda i: (i, 0)) for a in ws]
    shapes = [jax.ShapeDtypeStruct(a.shape, F32) for a in ws]
    outs = pl.pallas_call(body, grid=(nb,), in_specs=blks * 4, out_specs=blks * 4, out_shape=shapes * 4,
                          compiler_params=_params("parallel"), name=name)(*ws, *gs, *ms, *vs)
    return [tuple(outs[k * n + i] for k in range(4)) for i in range(n)]


SM_CONV_B, SM_NORM_W, SM_MIX_PRE, SM_MIX_POST, SM_FFN_PRE, SM_FFN_POST, SM_MISC, SM_CONV_W, SM_B_QKV, SM_B_O = 0, 4, 6, 8, 10, 12, 14, 16, 32, 34
SM_ROWS = 40
MISC_DT_BIAS, MISC_A_LOG, MISC_D, MISC_SINKS, MISC_LOSS = 0, 32, 64, 96, 112


def _shard_halves(a):
    c = a.shape[-1]
    return a.reshape(N_CHIPS, 2, -1, c)


def _rows(v):
    return v.reshape(-1, D_MODEL)


def _misc_row(dt_bias, a_log, d, sinks, loss):
    pad = jnp.zeros((D_MODEL - MISC_LOSS - 1,), F32)
    return jnp.concatenate([dt_bias.reshape(-1), a_log.reshape(-1), d.reshape(-1), sinks.reshape(-1), loss.reshape(1), pad]).reshape(1, D_MODEL)


def _replicated_rows(p, loss):
    return jnp.concatenate([
        _rows(p["ssd_conv_b"]), _rows(p["ssd_norm_w"]), _rows(p["mix_pre_norm"]), _rows(p["mix_post_norm"]),
        _rows(p["ffn_pre_norm"]), _rows(p["ffn_post_norm"]),
        _misc_row(p["ssd_dt_bias"], p["ssd_a_log"], p["ssd_d"], p["attn_sinks"], loss), jnp.zeros((1, D_MODEL), F32)], axis=0)


def _sharded_rows(conv_w, b_qkv, b_o):
    last = jnp.concatenate([b_qkv.reshape(-1), b_o.reshape(-1), jnp.zeros((D_MODEL - 640,), F32)]).reshape(1, D_MODEL)
    return jnp.concatenate([conv_w.reshape(SSD_CONV_WIDTH, D_MODEL), last, jnp.zeros((3, D_MODEL), F32)], axis=0)


REPLICATED = ("ssd_conv_b", "ssd_dt_bias", "ssd_a_log", "ssd_d", "ssd_norm_w", "attn_sinks",
              "mix_pre_norm", "mix_post_norm", "ffn_pre_norm", "ffn_post_norm")
MATRICES = ("ssd_w_in", "ssd_w_out", "attn_w_qkv", "attn_w_o", "mlp_w_up", "mlp_w_down")
WEIGHT_NAMES = ("ssd_w_in", "ssd_conv_w", "ssd_conv_b", "ssd_dt_bias", "ssd_a_log", "ssd_d", "ssd_norm_w", "ssd_w_out",
                "attn_w_qkv", "attn_b_qkv", "attn_sinks", "attn_w_o", "attn_b_o", "mlp_w_up", "mlp_w_down",
                "mix_pre_norm", "mix_post_norm", "ffn_pre_norm", "ffn_post_norm")


def _unpack_small(rows16, rows8, like):
    misc = rows16[SM_MISC]
    out = {
        "ssd_conv_b": rows16[SM_CONV_B:SM_CONV_B + 4], "ssd_norm_w": rows16[SM_NORM_W:SM_NORM_W + 2],
        "mix_pre_norm": rows16[SM_MIX_PRE:SM_MIX_PRE + 2], "mix_post_norm": rows16[SM_MIX_POST:SM_MIX_POST + 2],
        "ffn_pre_norm": rows16[SM_FFN_PRE:SM_FFN_PRE + 2], "ffn_post_norm": rows16[SM_FFN_POST:SM_FFN_POST + 2],
        "ssd_dt_bias": misc[MISC_DT_BIAS:MISC_DT_BIAS + 32], "ssd_a_log": misc[MISC_A_LOG:MISC_A_LOG + 32],
        "ssd_d": misc[MISC_D:MISC_D + 32], "attn_sinks": misc[MISC_SINKS:MISC_SINKS + 16],
        "ssd_conv_w": rows8[0:SSD_CONV_WIDTH], "attn_b_qkv": rows8[SSD_CONV_WIDTH, 0:384], "attn_b_o": rows8[SSD_CONV_WIDTH, 384:640],
    }
    return {k: v.reshape(like[k].shape) for k, v in out.items()}


def kernel(x, ssd_w_in, ssd_conv_w, ssd_conv_b, ssd_dt_bias, ssd_a_log, ssd_d, ssd_norm_w, ssd_w_out, attn_w_qkv, attn_b_qkv, attn_sinks, attn_w_o, attn_b_o, mlp_w_up, mlp_w_down, mix_pre_norm, mix_post_norm, ffn_pre_norm, ffn_post_norm, loss_target, m_ssd_w_in, m_ssd_conv_w, m_ssd_conv_b, m_ssd_dt_bias, m_ssd_a_log, m_ssd_d, m_ssd_norm_w, m_ssd_w_out, m_attn_w_qkv, m_attn_b_qkv, m_attn_sinks, m_attn_w_o, m_attn_b_o, m_mlp_w_up, m_mlp_w_down, m_mix_pre_norm, m_mix_post_norm, m_ffn_pre_norm, m_ffn_post_norm, v_ssd_w_in, v_ssd_conv_w, v_ssd_conv_b, v_ssd_dt_bias, v_ssd_a_log, v_ssd_d, v_ssd_norm_w, v_ssd_w_out, v_attn_w_qkv, v_attn_b_qkv, v_attn_sinks, v_attn_w_o, v_attn_b_o, v_mlp_w_up, v_mlp_w_down, v_mix_pre_norm, v_mix_post_norm, v_ffn_pre_norm, v_ffn_post_norm):
    w = dict(zip(WEIGHT_NAMES, (ssd_w_in, ssd_conv_w, ssd_conv_b, ssd_dt_bias, ssd_a_log, ssd_d, ssd_norm_w, ssd_w_out, attn_w_qkv, attn_b_qkv, attn_sinks, attn_w_o, attn_b_o, mlp_w_up, mlp_w_down, mix_pre_norm, mix_post_norm, ffn_pre_norm, ffn_post_norm)))
    m = dict(zip(WEIGHT_NAMES, (m_ssd_w_in, m_ssd_conv_w, m_ssd_conv_b, m_ssd_dt_bias, m_ssd_a_log, m_ssd_d, m_ssd_norm_w, m_ssd_w_out, m_attn_w_qkv, m_attn_b_qkv, m_attn_sinks, m_attn_w_o, m_attn_b_o, m_mlp_w_up, m_mlp_w_down, m_mix_pre_norm, m_mix_post_norm, m_ffn_pre_norm, m_ffn_post_norm)))
    v = dict(zip(WEIGHT_NAMES, (v_ssd_w_in, v_ssd_conv_w, v_ssd_conv_b, v_ssd_dt_bias, v_ssd_a_log, v_ssd_d, v_ssd_norm_w, v_ssd_w_out, v_attn_w_qkv, v_attn_b_qkv, v_attn_sinks, v_attn_w_o, v_attn_b_o, v_mlp_w_up, v_mlp_w_down, v_mix_pre_norm, v_mix_post_norm, v_ffn_pre_norm, v_ffn_post_norm)))
    chip = 2 * lax.axis_index("x") + lax.axis_index("y")

    two_halves = lambda a: a.reshape(2, a.shape[0] // 2, a.shape[1])
    later = {"ssd_w_out": (w["ssd_w_out"], 0), "attn_w_qkv": (w["attn_w_qkv"], 0), "attn_w_o": (w["attn_w_o"], 0),
             "mlp_w_up0": (w["mlp_w_up"], 0), "mlp_w_up1": (w["mlp_w_up"], 1),
             "mlp_w_down0": (w["mlp_w_down"], 0), "mlp_w_down1": (w["mlp_w_down"], 1)}
    first = _GatherHook([two_halves(w["ssd_w_in"][0].astype(BF16))], [w["ssd_conv_w"][0], w["attn_b_qkv"], w["attn_b_o"]])
    cast, (g_in, g_conv, g_bqkv, g_bo) = _cast_bf16(list(later.values()), name="weights_to_bf16", hook=first)
    core = lax.axis_index("c").astype(jnp.int32).reshape(1)
    comm = _StepComm({k: two_halves(a) for k, a in zip(later, cast)}, core)
    full = {
        "ssd_w_in": _full_weight("ssd_w_in", g_in),
        "ssd_conv_w": g_conv.transpose(1, 0, 2).reshape(SSD_CONV_WIDTH, SSD_CONV_DIM),
        "attn_b_qkv": g_bqkv.reshape(ATTN_QKV), "attn_b_o": g_bo.reshape(D_MODEL),
    }
    for name in REPLICATED:
        full[name] = w[name][0] if name.startswith(("ssd_", "attn_")) else w[name]

    loss_tile, grad_x, gm, g = _local_step(x[0], loss_target[0], full, comm)

    conv_w_rows = g["ssd_conv_w"].reshape(SSD_CONV_WIDTH * N_CHIPS, D_MODEL)
    b_qkv_rows = jnp.pad(g["attn_b_qkv"], (0, 2 * D_MODEL - ATTN_QKV)).reshape(2, D_MODEL)
    small = jnp.concatenate([_replicated_rows(g, loss_tile[0, 0]), conv_w_rows, b_qkv_rows, _rows(g["attn_b_o"]),
                             jnp.zeros((SM_ROWS - SM_B_O - 1, D_MODEL), F32)], axis=0)
    small_all, = _hook_call(_ExchangeHook([], [small]), name="vector_grad_all_gather")
    order = ("ssd_w_in", "ssd_w_out", "attn_w_qkv", "attn_w_o", "mlp_w_up0", "mlp_w_up1", "mlp_w_down0", "mlp_w_down1")
    halves = _sum_chips([comm.chip_parts[k] for k in order], name="grad_sum")
    r_in, r_out, r_qkv, r_o, r_up, r_down = _swap_halves(halves, layers=((4, 5), (6, 7)), name="grad_halves_swap")
    small_sum, = _sum_chips([small_all], name="small_grad_sum")

    grads = {"ssd_w_in": r_in, "ssd_w_out": r_out, "attn_w_qkv": r_qkv, "attn_w_o": r_o, "mlp_w_up": r_up, "mlp_w_down": r_down}
    grads = {k: a.reshape(w[k].shape) for k, a in grads.items()}
    conv_w_g = lax.dynamic_index_in_dim(small_sum[SM_CONV_W:SM_CONV_W + 16].reshape(SSD_CONV_WIDTH, N_CHIPS, D_MODEL), chip, axis=1, keepdims=False)
    b_qkv_g = lax.dynamic_slice_in_dim(small_sum[SM_B_QKV:SM_B_QKV + 2].reshape(-1), chip * 384, 384)
    b_o_g = lax.dynamic_slice_in_dim(small_sum[SM_B_O], chip * 256, 256)
    small_g = jnp.concatenate([small_sum[0:16], _sharded_rows(conv_w_g, b_qkv_g, b_o_g)], axis=0)
    grads.update(_unpack_small(small_g[0:16], small_g[16:24], w))
    loss = small_sum[SM_MISC, MISC_LOSS]

    delta, new_m, new_v = {}, {}, {}
    as2d = lambda p: [p[name].reshape(-1, p[name].shape[-1]) for name in MATRICES]
    for name, (g2, d2, m2, v2) in zip(MATRICES, _adamw(as2d(w), as2d(grads), as2d(m), as2d(v), name="adamw_matrices")):
        shape = w[name].shape
        grads[name], delta[name], new_m[name], new_v[name] = g2.reshape(shape), d2.reshape(shape), m2.reshape(shape), v2.reshape(shape)
    zero = jnp.zeros((), F32)
    small_pack = lambda p: jnp.concatenate([_replicated_rows({k: p[k] for k in REPLICATED}, zero),
                                            _sharded_rows(p["ssd_conv_w"], p["attn_b_qkv"], p["attn_b_o"])], axis=0)
    (_, d_s, m_s, v_s), = _adamw([small_pack(w)], [small_g], [small_pack(m)], [small_pack(v)], name="adamw_vectors")
    delta.update(_unpack_small(d_s[0:16], d_s[16:24], w))
    new_m.update(_unpack_small(m_s[0:16], m_s[16:24], w))
    new_v.update(_unpack_small(v_s[0:16], v_s[16:24], w))

    return (loss, grad_x[None], *[grads[n] for n in WEIGHT_NAMES], *[delta[n] for n in WEIGHT_NAMES],
            *[new_m[n] for n in WEIGHT_NAMES], *[new_v[n] for n in WEIGHT_NAMES])
```

```python
import functools
import math

import jax
import jax.numpy as jnp
from jax import lax
from jax.experimental import pallas as pl
from jax.experimental.pallas import tpu as pltpu

F32 = jnp.float32
BF16 = jnp.bfloat16

D_MODEL = 1024
SSD_D_INNER = 2048
SSD_HEAD_DIM = 64
SSD_N_HEADS = 32
SSD_N_GROUPS = 8
SSD_HPG = 4
SSD_D_STATE = 128
SSD_CONV_WIDTH = 4
SSD_CHUNK = 128
SSD_CONV_DIM = 4096
SSD_IN_DIM = 6176
SSD_IN_PAD = 6272
SSD_GW = SSD_HPG * SSD_HEAD_DIM
ATTN_HEAD_DIM = 64
ATTN_N_Q = 16
ATTN_N_KV = 4
ATTN_REP = 4
ATTN_WINDOW = 128
ATTN_QKV = 1536
D_FF = 4096
NORM_EPS = 1e-6

ADAM_LR = 0.001
ADAM_B1 = 0.9
ADAM_B2 = 0.999
ADAM_EPS = 1e-08
ADAM_WD = 0.01
ADAM_STEP = 10

N_CHIPS = 4
N_DEV = 8
LANES = 128
VMEM_LIMIT = 48 * 1024 * 1024

MESH = pl.DeviceIdType.MESH


def _params(*sem):
    return pltpu.CompilerParams(dimension_semantics=sem, vmem_limit_bytes=VMEM_LIMIT)


def _dot(a, b, dims):
    return lax.dot_general(a, b, (dims, ((), ())), preferred_element_type=F32)


def _dot_nn(a, b):
    return _dot(a, b, ((1,), (0,)))


def _dot_nt(a, b):
    return _dot(a, b, ((1,), (1,)))


def _dot_tn(a, b):
    return _dot(a, b, ((0,), (0,)))


def _sigmoid(x):
    return 0.5 * jnp.tanh(0.5 * x) + 0.5


ANY = pl.BlockSpec(memory_space=pl.ANY)


class _HookSlots:
    def __init__(self, hook, n_in, n_out, n_scratch):
        self.hook = hook
        self.n_in, self.n_out, self.n_scratch = n_in, n_out, n_scratch
        self.inputs = list(hook.arrs) if hook else []
        self.out_shape = list(hook.out_shape) if hook else []
        self.scratch = list(hook.scratch) if hook else []
        self.in_specs = [ANY] * len(self.inputs)
        self.out_specs = [ANY] * len(self.out_shape)

    def _split(self, refs):
        a = self.n_in
        b = a + len(self.inputs)
        c = b + self.n_out
        d = c + len(self.out_shape)
        e = d + self.n_scratch
        return refs[:a], refs[a:b], refs[b:c], refs[c:d], refs[d:e], refs[e:]

    def own(self, refs):
        ins, _, outs, _, scratch, _ = self._split(refs)
        return ins, outs, scratch

    def run(self, refs, step, n_steps):
        _, h_in, _, h_out, _, h_scratch = self._split(refs)
        _run_hook(self.hook, h_in, h_out, h_scratch, step, n_steps)

    def semantics(self, *sem):
        return sem if self.hook is None else ("arbitrary",) * len(sem)


def _matmul(a, b, *, mode, out_dtypes, name, epilogue=None, extras=(), tm=1024, tn=1024, tk=1024,
            b_shards=False, out_shards=False, hook=None, f32_tail=False):
    if b_shards:
        s, b_rows, b_cols = b.shape
        b2 = (b_rows, s * b_cols)
        if mode == "nn":
            tn = b_cols
        else:
            assert mode == "nt"
            tk = b_cols
    else:
        b2 = b.shape
    if mode == "nn":
        (m, k), (k2, n) = a.shape, b2
    elif mode == "nt":
        (m, k), (n, k2) = a.shape, b2
    else:
        (k, m), (k2, n) = a.shape, b2
    assert k == k2, (a.shape, b.shape, mode)
    tm, tn, tk = min(tm, m), min(tn, n), min(tk, k)
    assert m % tm == 0 and n % tn == 0 and k % tk == 0, (m, n, k, tm, tn, tk)
    nk = k // tk
    if mode == "tn":
        a_spec = pl.BlockSpec((tk, tm), lambda i, j, kk: (kk, i))
    else:
        a_spec = pl.BlockSpec((tm, tk), lambda i, j, kk: (i, kk))
    if b_shards and mode == "nn":
        b_spec = pl.BlockSpec((None, tk, tn), lambda i, j, kk: (j, kk, 0))
    elif b_shards:
        b_spec = pl.BlockSpec((None, tn, tk), lambda i, j, kk: (kk, j, 0))
    elif mode == "nt":
        b_spec = pl.BlockSpec((tn, tk), lambda i, j, kk: (j, kk))
    else:
        b_spec = pl.BlockSpec((tk, tn), lambda i, j, kk: (kk, j))
    dims = {"nn": ((1,), (0,)), "nt": ((1,), (1,)), "tn": ((0,), (0,))}[mode]
    ex_specs = []
    for arr, kind in extras:
        if kind == "tile":
            ex_specs.append(pl.BlockSpec((tm, tn), lambda i, j, kk: (i, j)))
        else:
            ex_specs.append(pl.BlockSpec((1, tn), lambda i, j, kk: (0, j)))
    n_ex, n_out = len(extras), len(out_dtypes)
    if epilogue is None:
        epilogue = lambda acc: (acc,)
    hk = _HookSlots(hook, n_in=2 + n_ex, n_out=n_out + f32_tail, n_scratch=0 if nk == 1 else 1)
    grid = (m // tm, n // tn, nk)

    def body(*refs):
        (a_ref, b_ref, *ex), outs, scratch = hk.own(refs)
        if hook is not None:
            step = (pl.program_id(0) * grid[1] + pl.program_id(1)) * grid[2] + pl.program_id(2)
            hk.run(refs, step, grid[0] * grid[1] * grid[2])

        def finish(acc):
            res = epilogue(acc, *[e[...] for e in ex])
            for o, r in zip(outs, res):
                o[...] = r.astype(o.dtype)
            if f32_tail:
                outs[n_out][...] = acc[:, tn - LANES:]

        if nk == 1:
            finish(_dot(a_ref[...], b_ref[...], dims))
        else:
            acc_ref = scratch[0]
            kk = pl.program_id(2)

            @pl.when(kk == 0)
            def _():
                acc_ref[...] = jnp.zeros_like(acc_ref)

            acc_ref[...] += _dot(a_ref[...], b_ref[...], dims)

            @pl.when(kk == nk - 1)
            def _():
                finish(acc_ref[...])

    if out_shards:
        out_spec = pl.BlockSpec((None, tm, tn), lambda i, j, kk: (j, i, 0))
        out_dims = (n // tn, m, tn)
    else:
        out_spec = pl.BlockSpec((tm, tn), lambda i, j, kk: (i, j))
        out_dims = (m, n)
    tail_specs = [pl.BlockSpec((tm, LANES), lambda i, j, kk: (i, 0))] if f32_tail else []
    tail_shapes = [jax.ShapeDtypeStruct((m, LANES), F32)] if f32_tail else []
    outs = pl.pallas_call(
        body,
        grid=grid,
        in_specs=[a_spec, b_spec] + ex_specs + hk.in_specs,
        out_specs=[out_spec for _ in out_dtypes] + tail_specs + hk.out_specs,
        out_shape=[jax.ShapeDtypeStruct(out_dims, dt) for dt in out_dtypes] + tail_shapes + hk.out_shape,
        scratch_shapes=([] if nk == 1 else [pltpu.VMEM((tm, tn), F32)]) + hk.scratch,
        compiler_params=_params(*hk.semantics("parallel", "arbitrary" if f32_tail else "parallel", "arbitrary")),
        name=name,
    )(a, b, *[arr for arr, _ in extras], *hk.inputs)
    n_own = n_out + f32_tail
    own = outs[0] if n_own == 1 else outs[:n_own]
    return own if hook is None else (own, outs[n_own:])


def _row_tile(t, want):
    return min(t, want)


def _rms_fwd(x, w, *, name, resid=None, want_u=None, target=None):
    t, d = x.shape
    tr = _row_tile(t, 512)

    def norm(v, wv):
        return v * lax.rsqrt(jnp.mean(v * v, axis=-1, keepdims=True) + NORM_EPS) * wv

    row = pl.BlockSpec((tr, d), lambda i: (i, 0))
    vec = pl.BlockSpec((1, d), lambda i: (0, 0))
    if target is not None:
        def body(x_ref, w_ref, r_ref, t_ref, dh_ref, loss_ref):
            err = r_ref[...] + norm(x_ref[...], w_ref[...]) - t_ref[...]
            dh_ref[...] = err * (1.0 / d)

            @pl.when(pl.program_id(0) == 0)
            def _():
                loss_ref[...] = jnp.zeros_like(loss_ref)

            part = jnp.sum(jnp.sum(err * err, axis=1, keepdims=True), axis=0, keepdims=True) * (0.5 / d)
            loss_ref[...] += jnp.broadcast_to(part, loss_ref.shape)

        return pl.pallas_call(
            body, grid=(t // tr,), in_specs=[row, vec, row, row],
            out_specs=[row, pl.BlockSpec((8, LANES), lambda i: (0, 0))],
            out_shape=[jax.ShapeDtypeStruct((t, d), F32), jax.ShapeDtypeStruct((8, LANES), F32)],
            compiler_params=_params("arbitrary"), name=name)(x, w, resid, target)
    if resid is None:
        def body(x_ref, w_ref, o_ref):
            o_ref[...] = norm(x_ref[...], w_ref[...]).astype(BF16)
        ins, in_specs = (x, w), [row, vec]
        out_shape, out_specs = jax.ShapeDtypeStruct((t, d), BF16), row
    elif want_u is None:
        def body(x_ref, w_ref, r_ref, o_ref):
            o_ref[...] = r_ref[...] + norm(x_ref[...], w_ref[...])
        ins, in_specs = (x, w, resid), [row, vec, row]
        out_shape, out_specs = jax.ShapeDtypeStruct((t, d), F32), row
    else:
        def body(x_ref, w_ref, r_ref, w2_ref, o_ref, u_ref):
            h = r_ref[...] + norm(x_ref[...], w_ref[...])
            o_ref[...] = h
            u_ref[...] = norm(h, w2_ref[...]).astype(BF16)
        ins, in_specs = (x, w, resid, want_u), [row, vec, row, vec]
        out_shape = [jax.ShapeDtypeStruct((t, d), F32), jax.ShapeDtypeStruct((t, d), BF16)]
        out_specs = [row, row]
    return pl.pallas_call(body, grid=(t // tr,), in_specs=in_specs, out_specs=out_specs, out_shape=out_shape,
                          compiler_params=_params("parallel"), name=name)(*ins)


def _rms_bwd(x, w, dy, *, name, resid=None, out_dtype=F32, dx_col_sum=False):
    t, d = x.shape
    tr = _row_tile(t, 512)
    row = pl.BlockSpec((tr, d), lambda i: (i, 0))
    vec = pl.BlockSpec((1, d), lambda i: (0, 0))
    has_res = resid is not None

    def body(x_ref, w_ref, dy_ref, *rest):
        r_ref = rest[0] if has_res else None
        dx_ref, dw_ref = rest[has_res:has_res + 2]
        xv = x_ref[...]
        dyv = dy_ref[...].astype(F32)
        r = lax.rsqrt(jnp.mean(xv * xv, axis=-1, keepdims=True) + NORM_EPS)
        xhat = xv * r
        dyw = dyv * w_ref[...]
        dx = r * (dyw - xhat * jnp.mean(dyw * xhat, axis=-1, keepdims=True))
        if has_res:
            dx = dx + r_ref[...]
        dx_ref[...] = dx.astype(dx_ref.dtype)

        sums = [(dw_ref, dyv * xhat)] + ([(rest[-1], dx)] if dx_col_sum else [])

        @pl.when(pl.program_id(0) == 0)
        def _():
            for acc_ref, _ in sums:
                acc_ref[...] = jnp.zeros_like(acc_ref)

        for acc_ref, rows in sums:
            acc_ref[...] += jnp.sum(rows, axis=0, keepdims=True)

    ins = (x, w, dy) + ((resid,) if has_res else ())
    in_specs = [row, vec, row] + ([row] if has_res else [])
    n_vec = 2 if dx_col_sum else 1
    return pl.pallas_call(
        body, grid=(t // tr,), in_specs=in_specs, out_specs=[row] + [vec] * n_vec,
        out_shape=[jax.ShapeDtypeStruct((t, d), out_dtype)] + [jax.ShapeDtypeStruct((1, d), F32)] * n_vec,
        compiler_params=_params("arbitrary"), name=name)(*ins)


def _col_sum(x, *, name):
    t, n = x.shape
    tr = _row_tile(t, 512)

    def body(x_ref, o_ref):
        @pl.when(pl.program_id(0) == 0)
        def _():
            o_ref[...] = jnp.zeros_like(o_ref)

        o_ref[...] += jnp.sum(x_ref[...].astype(F32), axis=0, keepdims=True)

    return pl.pallas_call(
        body, grid=(t // tr,), in_specs=[pl.BlockSpec((tr, n), lambda i: (i, 0))],
        out_specs=pl.BlockSpec((1, n), lambda i: (0, 0)), out_shape=jax.ShapeDtypeStruct((1, n), F32),
        compiler_params=_params("arbitrary"), name=name)(x)


SSD_IN_SHARD = SSD_IN_DIM // N_CHIPS


def _w_in_from_shards(shards, *, name):
    d = shards.shape[1]
    tr = 256

    def body(s_ref, o_ref):
        o_ref[:, pl.ds(SSD_IN_PAD - LANES, LANES)] = jnp.zeros((tr, LANES), o_ref.dtype)
        for s in range(N_CHIPS):
            o_ref[:, pl.ds(SSD_IN_SHARD * s, SSD_IN_SHARD)] = s_ref[s]

    return pl.pallas_call(
        body, grid=(d // tr,), in_specs=[pl.BlockSpec((N_CHIPS, tr, SSD_IN_SHARD), lambda i: (0, i, 0))],
        out_specs=pl.BlockSpec((tr, SSD_IN_PAD), lambda i: (i, 0)),
        out_shape=jax.ShapeDtypeStruct((d, SSD_IN_PAD), shards.dtype),
        compiler_params=_params("parallel"), name=name)(shards)


def _w_in_to_shards(g, *, name):
    d = g.shape[0]
    tr = 256

    def body(g_ref, o_ref):
        for s in range(N_CHIPS):
            o_ref[s] = g_ref[:, pl.ds(SSD_IN_SHARD * s, SSD_IN_SHARD)].astype(o_ref.dtype)

    return pl.pallas_call(
        body, grid=(d // tr,), in_specs=[pl.BlockSpec((tr, SSD_IN_PAD), lambda i: (i, 0))],
        out_specs=pl.BlockSpec((N_CHIPS, tr, SSD_IN_SHARD), lambda i: (0, i, 0)),
        out_shape=jax.ShapeDtypeStruct((N_CHIPS, d, SSD_IN_SHARD), BF16),
        compiler_params=_params("parallel"), name=name)(g)


XBC_COL0 = SSD_D_INNER // LANES
DT_COL0 = (SSD_D_INNER + SSD_CONV_DIM) // LANES


def _shift_down(v, k, row_ids):
    return jnp.where(row_ids >= k, pltpu.roll(v, k, axis=0), 0.0)


def _shift_up(v, k, row_ids):
    n = v.shape[0]
    return jnp.where(row_ids < n - k, pltpu.roll(v, n - k, axis=0), 0.0)


def _conv_pre(x, w, b, row_ids):
    pre = b + w[3:4, :] * x
    for k in (1, 2, 3):
        pre = pre + w[3 - k:4 - k, :] * _shift_down(x, k, row_ids)
    return pre


def _conv_fwd(zx, conv_w, conv_b, *, name, hook=None):
    t = zx.shape[0]
    nct = SSD_CONV_DIM // LANES
    hk = _HookSlots(hook, n_in=3, n_out=1, n_scratch=0)

    def body(*refs):
        (x_ref, w_ref, b_ref), (o_ref,), _ = hk.own(refs)
        if hook is not None:
            hk.run(refs, pl.program_id(0), nct)
        x = x_ref[...].astype(F32)
        row_ids = lax.broadcasted_iota(jnp.int32, x.shape, 0)
        pre = _conv_pre(x, w_ref[...], b_ref[...], row_ids)
        o_ref[...] = pre * _sigmoid(pre)

    outs = pl.pallas_call(
        body, grid=(nct,),
        in_specs=[pl.BlockSpec((t, LANES), lambda j: (0, XBC_COL0 + j)),
                  pl.BlockSpec((SSD_CONV_WIDTH, LANES), lambda j: (0, j)),
                  pl.BlockSpec((1, LANES), lambda j: (0, j))] + hk.in_specs,
        out_specs=[pl.BlockSpec((t, LANES), lambda j: (0, j))] + hk.out_specs,
        out_shape=[jax.ShapeDtypeStruct((t, SSD_CONV_DIM), F32)] + hk.out_shape,
        scratch_shapes=hk.scratch,
        compiler_params=_params(*hk.semantics("parallel")), name=name)(zx, conv_w, conv_b, *hk.inputs)
    return outs[0] if hook is None else (outs[0], outs[1:])


def _conv_bwd(zx, conv_w, conv_b, d_xs, d_bm, d_cm, dzx, *, name):
    t = zx.shape[0]
    nct = SSD_CONV_DIM // LANES
    n_xs = SSD_D_INNER // LANES
    n_bm = SSD_N_GROUPS * SSD_D_STATE // LANES

    def body(x_ref, w_ref, b_ref, dxs_ref, dbm_ref, dcm_ref, _, dx_ref, dw_ref, db_ref):
        x = x_ref[...].astype(F32)
        w = w_ref[...]
        j = pl.program_id(0)
        dy = jnp.where(j < n_xs, dxs_ref[...], jnp.where(j < n_xs + n_bm, dbm_ref[...], dcm_ref[...]))
        row_ids = lax.broadcasted_iota(jnp.int32, x.shape, 0)
        pre = _conv_pre(x, w, b_ref[...], row_ids)
        sg = _sigmoid(pre)
        dpre = dy * (sg * (1.0 + pre * (1.0 - sg)))
        dx = w[3:4, :] * dpre
        for k in (1, 2, 3):
            dx = dx + w[3 - k:4 - k, :] * _shift_up(dpre, k, row_ids)
        dx_ref[...] = dx.astype(dx_ref.dtype)
        db_ref[...] = jnp.sum(dpre, axis=0, keepdims=True)
        dw_ref[3:4, :] = jnp.sum(dpre * x, axis=0, keepdims=True)
        for k in (1, 2, 3):
            dw_ref[3 - k:4 - k, :] = jnp.sum(dpre * _shift_down(x, k, row_ids), axis=0, keepdims=True)

    clip = lambda j, lo, n: jnp.clip(j - lo, 0, n - 1)
    return pl.pallas_call(
        body, grid=(nct,),
        in_specs=[pl.BlockSpec((t, LANES), lambda j: (0, XBC_COL0 + j)),
                  pl.BlockSpec((SSD_CONV_WIDTH, LANES), lambda j: (0, j)),
                  pl.BlockSpec((1, LANES), lambda j: (0, j)),
                  pl.BlockSpec((t, LANES), lambda j: (0, clip(j, 0, n_xs))),
                  pl.BlockSpec((t, LANES), lambda j: (0, clip(j, n_xs, n_bm))),
                  pl.BlockSpec((t, LANES), lambda j: (0, clip(j, n_xs + n_bm, n_bm))), ANY],
        out_specs=[pl.BlockSpec((t, LANES), lambda j: (0, XBC_COL0 + j)),
                   pl.BlockSpec((SSD_CONV_WIDTH, LANES), lambda j: (0, j)), pl.BlockSpec((1, LANES), lambda j: (0, j))],
        out_shape=[jax.ShapeDtypeStruct(dzx.shape, dzx.dtype),
                   jax.ShapeDtypeStruct((SSD_CONV_WIDTH, SSD_CONV_DIM), F32),
                   jax.ShapeDtypeStruct((1, SSD_CONV_DIM), F32)],
        input_output_aliases={6: 0},
        compiler_params=_params("parallel"), name=name)(zx, conv_w, conv_b, d_xs, d_bm, d_cm, dzx)


def _softplus_fwd(zx, bias_row, alog_row, *, name):
    t = zx.shape[0]
    q = SSD_CHUNK
    tr = _row_tile(t, 1024)

    def body(x_ref, b_ref, al_ref, dt_ref, cum_ref):
        v = x_ref[...] + b_ref[...]
        e = jnp.exp(-jnp.abs(v))
        u = 1.0 + e
        log1p = jnp.where(u == 1.0, e, jnp.log(u) * (e / (u - 1.0)))
        dt = jnp.maximum(v, 0.0) + log1p
        dt_ref[...] = dt
        a = dt * -jnp.exp(al_ref[...])
        lower = (lax.broadcasted_iota(jnp.int32, (q, q), 1) <= lax.broadcasted_iota(jnp.int32, (q, q), 0)).astype(F32)
        cums = [lax.dot_general(lower, a[c * q:(c + 1) * q, :], ((((1,), (0,))), ((), ())), precision=lax.Precision.HIGHEST,
                                preferred_element_type=F32) for c in range(tr // q)]
        cum_ref[...] = jnp.concatenate(cums, axis=0)

    blk = pl.BlockSpec((tr, LANES), lambda i: (i, 0))
    vec = pl.BlockSpec((1, LANES), lambda i: (0, 0))
    return pl.pallas_call(
        body, grid=(t // tr,),
        in_specs=[blk, vec, vec],
        out_specs=[blk, blk],
        out_shape=[jax.ShapeDtypeStruct((t, LANES), F32), jax.ShapeDtypeStruct((t, LANES), F32)],
        compiler_params=_params("parallel"), name=name)(zx, bias_row, alog_row)


def _softplus_bwd(zx, bias_row, ddt, dzx, *, name):
    t = zx.shape[0]
    tr = _row_tile(t, 1024)

    def body(x_ref, b_ref, g_ref, _, o_ref, db_ref):
        v = x_ref[...] + b_ref[...]
        lane = lax.broadcasted_iota(jnp.int32, v.shape, 1)
        d = jnp.where(lane < SSD_N_HEADS, g_ref[...] * _sigmoid(v), 0.0)
        o_ref[...] = d.astype(o_ref.dtype)

        @pl.when(pl.program_id(0) == 0)
        def _():
            db_ref[...] = jnp.zeros_like(db_ref)

        db_ref[...] += jnp.sum(d, axis=0, keepdims=True)

    return pl.pallas_call(
        body, grid=(t // tr,),
        in_specs=[pl.BlockSpec((tr, LANES), lambda i: (i, 0)), pl.BlockSpec((1, LANES), lambda i: (0, 0)),
                  pl.BlockSpec((tr, LANES), lambda i: (i, 0)), ANY],
        out_specs=[pl.BlockSpec((tr, LANES), lambda i: (i, DT_COL0)), pl.BlockSpec((1, LANES), lambda i: (0, 0))],
        out_shape=[jax.ShapeDtypeStruct(dzx.shape, dzx.dtype), jax.ShapeDtypeStruct((1, LANES), F32)],
        input_output_aliases={3: 0},
        compiler_params=_params("arbitrary"), name=name)(zx, bias_row, ddt, dzx)


def _ssd_masks():
    q = SSD_CHUNK
    tt = lax.broadcasted_iota(jnp.int32, (q, q), 0)
    ss = lax.broadcasted_iota(jnp.int32, (q, q), 1)
    lane = lax.broadcasted_iota(jnp.int32, (1, SSD_GW), 1)
    srow = lax.broadcasted_iota(jnp.int32, (SSD_GW, 1), 0)
    hm = [(lane >= SSD_HEAD_DIM * j) & (lane < SSD_HEAD_DIM * (j + 1)) for j in range(SSD_HPG)]
    rm = [(srow >= SSD_HEAD_DIM * j) & (srow < SSD_HEAD_DIM * (j + 1)) for j in range(SSD_HPG)]
    return tt, ss, hm, rm


def _ssd_head_terms(dt_rows, cum_rows, a_rows, j, tt, ss):
    q = SSD_CHUNK
    dt_row = dt_rows[j:j + 1, :]
    dt_col = jnp.sum(jnp.where(tt == ss, dt_row, 0.0), axis=1, keepdims=True)
    a_row1 = a_rows[j:j + 1, :]
    a_11 = a_rows[j:j + 1, 0:1]
    cum_col = jnp.sum(jnp.where(ss <= tt, dt_row * a_row1, 0.0), axis=1, keepdims=True)
    cum_row = cum_rows[j:j + 1, :]
    decay = jnp.exp(jnp.where(ss <= tt, cum_col - cum_row, -jnp.inf))
    cum_last = cum_col[q - 1:q, :]
    e_col = jnp.exp(cum_col)
    dte_col = jnp.exp(cum_last - cum_col)
    e_last = jnp.exp(cum_last)
    return dt_col, dt_row, a_row1, a_11, decay, e_col, dte_col, e_last


SSD_CHUNKS_PER_STEP = 4
SSD_BC_COL0 = SSD_D_INNER // SSD_D_STATE


def _ssd_head_selects(terms, hm, rm):
    e_all = jnp.zeros((SSD_CHUNK, SSD_GW), F32)
    w_all = jnp.zeros((SSD_CHUNK, SSD_GW), F32)
    e_s = jnp.zeros((SSD_GW, 1), F32)
    for j in range(SSD_HPG):
        dt_col, _, _, _, _, e_col, dte_col, e_last = terms[j]
        e_all = jnp.where(hm[j], e_col, e_all)
        w_all = jnp.where(hm[j], dt_col * dte_col, w_all)
        e_s = jnp.where(rm[j], e_last, e_s)
    return e_all, w_all, e_s


def _ssd_fwd(xc, dtr, cumr, alog_b, d_b, *, name, hook=None):
    t = xc.shape[0]
    q = SSD_CHUNK
    nc = t // q
    kc = min(SSD_CHUNKS_PER_STEP, nc)
    rows = kc * q
    hk = _HookSlots(hook, n_in=7, n_out=2, n_scratch=1)

    def body(*refs):
        (x_ref, b_ref, c_ref, dtr_ref, cumr_ref, alog_ref, d_ref), (y_ref, st_ref), (s_scr,) = hk.own(refs)
        if hook is not None:
            hk.run(refs, pl.program_id(0) * (nc // kc) + pl.program_id(1), SSD_N_GROUPS * (nc // kc))

        @pl.when(pl.program_id(1) == 0)
        def _():
            s_scr[...] = jnp.zeros_like(s_scr)

        tt, ss, hm, rm = _ssd_masks()
        a_rows = -jnp.exp(alog_ref[...])
        d_rows = d_ref[...]
        d_all = jnp.zeros((1, SSD_GW), F32)
        for j in range(SSD_HPG):
            d_all = jnp.where(hm[j], d_rows[j:j + 1, 0:1], d_all)
        ks, hs = range(kc), range(SSD_HPG)
        sl = [pl.ds(k * q, q) for k in ks]
        x = [x_ref[sl[k], :] for k in ks]
        bm = [b_ref[sl[k], :].astype(BF16) for k in ks]
        cm = [c_ref[sl[k], :].astype(BF16) for k in ks]
        xb = [x[k].astype(BF16) for k in ks]
        terms = [[_ssd_head_terms(dtr_ref[:, sl[k]], cumr_ref[:, sl[k]], a_rows, j, tt, ss) for j in hs] for k in ks]
        g = [_dot_nt(cm[k], bm[k]) for k in ks]
        m = [[(g[k] * terms[k][j][4] * terms[k][j][1]).astype(BF16) for j in hs] for k in ks]
        yj = [[_dot_nn(m[k][j], xb[k]) for j in hs] for k in ks]
        sel = [_ssd_head_selects(terms[k], hm, rm) for k in ks]
        upd = [_dot_tn((x[k] * sel[k][1]).astype(BF16), bm[k]) for k in ks]
        states = [s_scr[...]]
        for k in ks:
            states.append(states[k] * sel[k][2] + upd[k])
        inter = [_dot_nt(cm[k], states[k].astype(BF16)) for k in ks]
        ys = []
        for k in ks:
            y = jnp.zeros((q, SSD_GW), F32)
            for j in hs:
                y = jnp.where(hm[j], yj[k][j], y)
            ys.append(y + inter[k] * sel[k][0] + x[k] * d_all)
        for k in ks:
            st_ref[k] = states[k]
        y_ref[...] = jnp.concatenate(ys, axis=0)
        s_scr[...] = states[kc]

    blk = lambda width, off: pl.BlockSpec((rows, width), lambda g, c: (c, off + g))
    par_s = pl.BlockSpec((None, SSD_HPG, LANES), lambda g, c: (g, 0, 0))
    row_s = pl.BlockSpec((None, SSD_HPG, rows), lambda g, c: (g, 0, c))
    outs = pl.pallas_call(
        body, grid=(SSD_N_GROUPS, nc // kc),
        in_specs=[blk(SSD_GW, 0), blk(SSD_D_STATE, SSD_BC_COL0), blk(SSD_D_STATE, SSD_BC_COL0 + SSD_N_GROUPS),
                  row_s, row_s, par_s, par_s] + hk.in_specs,
        out_specs=[blk(SSD_GW, 0), pl.BlockSpec((None, kc, SSD_GW, SSD_D_STATE), lambda g, c: (g, c, 0, 0))] + hk.out_specs,
        out_shape=[jax.ShapeDtypeStruct((t, SSD_D_INNER), F32),
                   jax.ShapeDtypeStruct((SSD_N_GROUPS, nc, SSD_GW, SSD_D_STATE), F32)] + hk.out_shape,
        scratch_shapes=[pltpu.VMEM((SSD_GW, SSD_D_STATE), F32)] + hk.scratch,
        compiler_params=_params(*hk.semantics("parallel", "arbitrary")), name=name)(
            xc, xc, xc, dtr, cumr, alog_b, d_b, *hk.inputs)
    return outs if hook is None else (outs[:2], outs[2:])


def _ssd_bwd(xc, dtr, cumr, alog_b, d_b, states, dy, *, name, hook=None):
    t = xc.shape[0]
    q = SSD_CHUNK
    nc = t // q
    kc = min(SSD_CHUNKS_PER_STEP, nc)
    nst = nc // kc
    rows = kc * q
    rev = lambda c: nst - 1 - c
    hk = _HookSlots(hook, n_in=9, n_out=5, n_scratch=1)

    def body(*refs):
        ((x_ref, b_ref, c_ref, dtr_ref, cumr_ref, alog_ref, d_ref, st_ref, dy_ref),
         (dx_ref, db_ref, dc_ref, ddt_ref, dpar_ref), (ds_scr,)) = hk.own(refs)
        if hook is not None:
            hk.run(refs, pl.program_id(0) * nst + pl.program_id(1), SSD_N_GROUPS * nst)

        @pl.when(pl.program_id(1) == 0)
        def _():
            ds_scr[...] = jnp.zeros_like(ds_scr)
            dpar_ref[...] = jnp.zeros_like(dpar_ref)

        tt, ss, hm, rm = _ssd_masks()
        tcol = lax.broadcasted_iota(jnp.int32, (q, 1), 0)
        lane = lax.broadcasted_iota(jnp.int32, (1, LANES), 1)
        a_rows = -jnp.exp(alog_ref[...])
        d_rows = d_ref[...]
        d_all = jnp.zeros((1, SSD_GW), F32)
        for j in range(SSD_HPG):
            d_all = jnp.where(hm[j], d_rows[j:j + 1, 0:1], d_all)
        ks, hs = range(kc), range(SSD_HPG)
        sl = [pl.ds(k * q, q) for k in ks]
        x = [x_ref[sl[k], :] for k in ks]
        dyv = [dy_ref[sl[k], :] for k in ks]
        bm = [b_ref[sl[k], :].astype(BF16) for k in ks]
        cm = [c_ref[sl[k], :].astype(BF16) for k in ks]
        s_in = [st_ref[k] for k in ks]
        xb = [x[k].astype(BF16) for k in ks]
        dyb = [dyv[k].astype(BF16) for k in ks]
        s_b = [s_in[k].astype(BF16) for k in ks]
        terms = [[_ssd_head_terms(dtr_ref[:, sl[k]], cumr_ref[:, sl[k]], a_rows, j, tt, ss) for j in hs] for k in ks]
        sel = [_ssd_head_selects(terms[k], hm, rm) for k in ks]
        e_all, w_all, e_s = [s_[0] for s_ in sel], [s_[1] for s_ in sel], [s_[2] for s_ in sel]
        dye = [(dyv[k] * e_all[k]).astype(BF16) for k in ks]
        ds_loc = [_dot_tn(dye[k], cm[k]) for k in ks]
        ds = [None] * kc
        running = ds_scr[...]
        for k in reversed(ks):
            ds[k] = running
            running = running * e_s[k] + ds_loc[k]
        ds_scr[...] = running
        ds_b = [ds[k].astype(BF16) for k in ks]
        g = [_dot_nt(cm[k], bm[k]) for k in ks]
        cs = [_dot_nt(cm[k], s_b[k]) for k in ks]
        bds = [_dot_nt(bm[k], ds_b[k]) for k in ks]
        dm = [[_dot_nt(jnp.where(hm[j], dyv[k], 0.0).astype(BF16), xb[k]) for j in hs] for k in ks]
        gl = [[g[k] * terms[k][j][4] for j in hs] for k in ks]
        wp = [[dm[k][j] * gl[k][j] for j in hs] for k in ks]
        mt = [[(gl[k][j] * terms[k][j][1]).astype(BF16) for j in hs] for k in ks]
        dxj = [[_dot_tn(mt[k][j], dyb[k]) for j in hs] for k in ks]
        dg = []
        for k in ks:
            acc = jnp.zeros((q, q), F32)
            for j in hs:
                acc = acc + dm[k][j] * terms[k][j][4] * terms[k][j][1]
            dg.append(acc.astype(BF16))
        dy_cs = [dyv[k] * cs[k] for k in ks]
        x_bds = [x[k] * bds[k] for k in ks]
        dy_x = [dyv[k] * x[k] for k in ks]
        ds_s = [ds[k] * s_in[k] for k in ks]
        w = [[wp[k][j] * terms[k][j][1] for j in hs] for k in ks]
        rw_col = [[jnp.sum(w[k][j], axis=1, keepdims=True) for j in hs] for k in ks]
        cw_row = [[jnp.sum(w[k][j], axis=0, keepdims=True) for j in hs] for k in ks]
        cwp_row = [[jnp.sum(wp[k][j], axis=0, keepdims=True) for j in hs] for k in ks]
        r1_col = [[jnp.sum(jnp.where(hm[j], dy_cs[k], 0.0), axis=1, keepdims=True) * terms[k][j][5] for j in hs] for k in ks]
        dw_col = [[jnp.sum(jnp.where(hm[j], x_bds[k], 0.0), axis=1, keepdims=True) for j in hs] for k in ks]
        head_rows = [slice(j * SSD_HEAD_DIM, (j + 1) * SSD_HEAD_DIM) for j in hs]
        lane_sum = lambda v: jnp.sum(v, axis=1, keepdims=True)
        s_sum = [[lane_sum(jnp.sum(ds_s[k][head_rows[j], :], axis=0, keepdims=True)) for j in hs] for k in ks]
        dy_x_cols = [jnp.sum(dy_x[k], axis=0, keepdims=True) for k in ks]
        d_d = [[lane_sum(jnp.where(hm[j], dy_x_cols[k], 0.0)) for j in hs] for k in ks]
        ddt_rows = [[None] * SSD_HPG for _ in ks]
        dpar = [jnp.zeros((1, LANES), F32) for _ in hs]
        for k in ks:
            for j in hs:
                dt_col, dt_row, a_row1, a_11, _, _, dte_col, e_last = terms[k][j]
                dww = dw_col[k][j] * (dt_col * dte_col)
                last_add = jnp.sum(dww, axis=0, keepdims=True) + e_last * s_sum[k][j]
                dcum_col = rw_col[k][j] + r1_col[k][j] - dww + jnp.where(tcol == q - 1, last_add, 0.0)
                da_row = jnp.sum(jnp.where(tt >= ss, dcum_col, 0.0), axis=0, keepdims=True)
                da_col = jnp.sum(jnp.where(ss >= tt, -cw_row[k][j], 0.0), axis=1, keepdims=True)
                ddt_col = a_11 * da_col + dw_col[k][j] * dte_col
                ddt_rows[k][j] = (a_row1 * da_row + cwp_row[k][j]
                                  + jnp.sum(jnp.where(tt == ss, ddt_col, 0.0), axis=0, keepdims=True))
                d_a = jnp.sum(dt_row * da_row, axis=1, keepdims=True) + jnp.sum(dt_col * da_col, axis=0, keepdims=True)
                dpar[j] = dpar[j] + jnp.where(lane == 0, d_a * a_11, 0.0) + jnp.where(lane == 1, d_d[k][j], 0.0)
        dxs = []
        for k in ks:
            acc = jnp.zeros((q, SSD_GW), F32)
            for j in hs:
                acc = jnp.where(hm[j], dxj[k][j], acc)
            dxs.append(acc + w_all[k] * bds[k] + d_all * dyv[k])
        xw = [(x[k] * w_all[k]).astype(BF16) for k in ks]
        dc = [_dot_nn(dg[k], bm[k]) + _dot_nn(dye[k], s_b[k]) for k in ks]
        db = [_dot_tn(dg[k], cm[k]) + _dot_nn(xw[k], ds_b[k]) for k in ks]
        dx_ref[...] = jnp.concatenate(dxs, axis=0)
        dc_ref[...] = jnp.concatenate(dc, axis=0)
        db_ref[...] = jnp.concatenate(db, axis=0)
        ddt_ref[...] = jnp.concatenate([jnp.concatenate([ddt_rows[k][j] for k in ks], axis=1) for j in hs], axis=0)
        dpar_ref[...] += jnp.concatenate(dpar, axis=0)

    blk = lambda width, off: pl.BlockSpec((rows, width), lambda g, c: (rev(c), off + g))
    par_s = pl.BlockSpec((None, SSD_HPG, LANES), lambda g, c: (g, 0, 0))
    outs = pl.pallas_call(
        body, grid=(SSD_N_GROUPS, nst),
        in_specs=[blk(SSD_GW, 0), blk(SSD_D_STATE, SSD_BC_COL0), blk(SSD_D_STATE, SSD_BC_COL0 + SSD_N_GROUPS),
                  pl.BlockSpec((None, SSD_HPG, rows), lambda g, c: (g, 0, rev(c))),
                  pl.BlockSpec((None, SSD_HPG, rows), lambda g, c: (g, 0, rev(c))), par_s, par_s,
                  pl.BlockSpec((None, kc, SSD_GW, SSD_D_STATE), lambda g, c: (g, rev(c), 0, 0)), blk(SSD_GW, 0)] + hk.in_specs,
        out_specs=[blk(SSD_GW, 0), blk(SSD_D_STATE, 0), blk(SSD_D_STATE, 0),
                   pl.BlockSpec((None, SSD_HPG, rows), lambda g, c: (g, 0, rev(c))), par_s] + hk.out_specs,
        out_shape=[jax.ShapeDtypeStruct((t, SSD_D_INNER), F32),
                   jax.ShapeDtypeStruct((t, SSD_N_GROUPS * SSD_D_STATE), F32),
                   jax.ShapeDtypeStruct((t, SSD_N_GROUPS * SSD_D_STATE), F32),
                   jax.ShapeDtypeStruct((SSD_N_GROUPS, SSD_HPG, t), F32),
                   jax.ShapeDtypeStruct((SSD_N_GROUPS, SSD_HPG, LANES), F32)] + hk.out_shape,
        scratch_shapes=[pltpu.VMEM((SSD_GW, SSD_D_STATE), F32)] + hk.scratch,
        compiler_params=_params(*hk.semantics("parallel", "arbitrary")), name=name)(
            xc, xc, xc, dtr, cumr, alog_b, d_b, states, dy, *hk.inputs)
    return outs if hook is None else (outs[:5], outs[5:])


def _gate_norm_fwd(y, zx, norm_w, *, name):
    t = y.shape[0]
    tr = _row_tile(t, 256)
    row = pl.BlockSpec((tr, SSD_D_INNER), lambda i: (i, 0))

    def body(y_ref, z_ref, w_ref, o_ref):
        for gi in range(SSD_N_GROUPS):
            sl = pl.ds(gi * SSD_GW, SSD_GW)
            z = z_ref[:, sl].astype(F32)
            gv = y_ref[:, sl] * (z * _sigmoid(z))
            r = lax.rsqrt(jnp.mean(gv * gv, axis=-1, keepdims=True) + NORM_EPS)
            o_ref[:, sl] = (gv * r * w_ref[:, sl]).astype(BF16)

    return pl.pallas_call(
        body, grid=(t // tr,), in_specs=[row, row, pl.BlockSpec((1, SSD_D_INNER), lambda i: (0, 0))],
        out_specs=row, out_shape=jax.ShapeDtypeStruct((t, SSD_D_INNER), BF16),
        compiler_params=_params("parallel"), name=name)(y, zx, norm_w)


def _gate_norm_bwd(y, zx, norm_w, dyn, *, name):
    t = y.shape[0]
    tr = _row_tile(t, 256)
    row = pl.BlockSpec((tr, SSD_D_INNER), lambda i: (i, 0))
    vec = pl.BlockSpec((1, SSD_D_INNER), lambda i: (0, 0))

    def body(y_ref, z_ref, w_ref, dyn_ref, dy_ref, dz_ref, dw_ref):
        @pl.when(pl.program_id(0) == 0)
        def _():
            dw_ref[...] = jnp.zeros_like(dw_ref)

        for gi in range(SSD_N_GROUPS):
            sl = pl.ds(gi * SSD_GW, SSD_GW)
            z = z_ref[:, sl].astype(F32)
            yv = y_ref[:, sl]
            sg = _sigmoid(z)
            sz = z * sg
            gv = yv * sz
            r = lax.rsqrt(jnp.mean(gv * gv, axis=-1, keepdims=True) + NORM_EPS)
            ghat = gv * r
            dout = dyn_ref[:, sl].astype(F32)
            dgh = dout * w_ref[:, sl]
            dgv = r * (dgh - ghat * jnp.mean(dgh * ghat, axis=-1, keepdims=True))
            dy_ref[:, sl] = dgv * sz
            dz_ref[:, sl] = (dgv * yv * (sg * (1.0 + z * (1.0 - sg)))).astype(dz_ref.dtype)
            dw_ref[:, sl] += jnp.sum(dout * ghat, axis=0, keepdims=True)

    return pl.pallas_call(
        body, grid=(t // tr,), in_specs=[row, row, vec, row], out_specs=[row, row, vec],
        out_shape=[jax.ShapeDtypeStruct((t, SSD_D_INNER), F32), jax.ShapeDtypeStruct((t, SSD_IN_PAD), BF16),
                   jax.ShapeDtypeStruct((1, SSD_D_INNER), F32)],
        compiler_params=_params("arbitrary"), name=name)(y, zx, norm_w, dyn)


ATTN_KV_W = ATTN_N_KV * ATTN_HEAD_DIM
ATTN_Q_HALF = 512
ATTN_K_BLK = ATTN_N_Q * ATTN_HEAD_DIM // ATTN_KV_W
ATTN_V_BLK = ATTN_K_BLK + 1


def _attn_valid(first_block):
    w = ATTN_WINDOW
    qpos = lax.broadcasted_iota(jnp.int32, (w, 2 * w), 0) + w
    kpos = lax.broadcasted_iota(jnp.int32, (w, 2 * w), 1)
    rel = qpos - kpos
    return (rel >= 0) & (rel < w) & jnp.logical_not(first_block & (kpos < w))


def _attn_head_views(lo_ref, hi_ref):
    hd = ATTN_HEAD_DIM
    per_half = ATTN_Q_HALF // hd
    return [(lo_ref if h < per_half else hi_ref)[:, pl.ds((h % per_half) * hd, hd)] for h in range(ATTN_N_Q)]


def _attn_block_views(lo_ref, hi_ref, kc_ref, kp_ref, vc_ref, vp_ref):
    hd = ATTN_HEAD_DIM
    kv_cols = [pl.ds(kh * hd, hd) for kh in range(ATTN_N_KV)]
    kb = [jnp.concatenate([kp_ref[:, c], kc_ref[:, c]], axis=0) for c in kv_cols]
    vb = [jnp.concatenate([vp_ref[:, c], vc_ref[:, c]], axis=0) for c in kv_cols]
    return _attn_head_views(lo_ref, hi_ref), kb, vb


def _attn_scores(q, kb, valid):
    scale = ATTN_HEAD_DIM ** -0.5
    return [jnp.where(valid, _dot_nt(q[h], kb[h // ATTN_REP]) * scale, -jnp.inf) for h in range(ATTN_N_Q)]


def _attn_softmax(s, sink):
    heads = range(ATTN_N_Q)
    m = [jnp.maximum(jnp.max(s[h], axis=1, keepdims=True), sink[h]) for h in heads]
    e = [jnp.exp(s[h] - m[h]) for h in heads]
    es = [jnp.exp(sink[h] - m[h]) for h in heads]
    inv = [1.0 / (jnp.sum(e[h], axis=1, keepdims=True) + es[h]) for h in heads]
    return e, es, inv


def _attn_fwd(qkv, sinks_b, *, name):
    t = qkv.shape[0]
    w = ATTN_WINDOW
    nb = t // w
    prev = lambda n: jnp.maximum(n - 1, 0)

    def body(qlo_ref, qhi_ref, kc_ref, kp_ref, vc_ref, vp_ref, sink_ref, o_ref):
        heads = range(ATTN_N_Q)
        q, kb, vb = _attn_block_views(qlo_ref, qhi_ref, kc_ref, kp_ref, vc_ref, vp_ref)
        sink = [sink_ref[h:h + 1, 0:1] for h in heads]
        e, _, inv = _attn_softmax(_attn_scores(q, kb, _attn_valid(pl.program_id(0) == 0)), sink)
        out = [_dot_nn((e[h] * inv[h]).astype(BF16), vb[h // ATTN_REP]).astype(o_ref.dtype) for h in heads]
        o_ref[...] = jnp.concatenate(out, axis=1)

    qh = lambda half: pl.BlockSpec((w, ATTN_Q_HALF), lambda n: (n, half))
    kv = lambda blk, idx: pl.BlockSpec((w, ATTN_KV_W), lambda n: (idx(n), blk))
    cur = lambda n: n
    return pl.pallas_call(
        body, grid=(nb,),
        in_specs=[qh(0), qh(1), kv(ATTN_K_BLK, cur), kv(ATTN_K_BLK, prev), kv(ATTN_V_BLK, cur), kv(ATTN_V_BLK, prev),
                  pl.BlockSpec((ATTN_N_Q, LANES), lambda n: (0, 0))],
        out_specs=pl.BlockSpec((w, D_MODEL), lambda n: (n, 0)),
        out_shape=jax.ShapeDtypeStruct((t, D_MODEL), BF16),
        compiler_params=_params("parallel"), name=name)(qkv, qkv, qkv, qkv, qkv, qkv, sinks_b)


def _attn_bwd(qkv, sinks_b, dout, *, name):
    t = qkv.shape[0]
    w = ATTN_WINDOW
    nb = t // w
    hd = ATTN_HEAD_DIM
    clamp = lambda n: jnp.minimum(n, nb - 1)
    prev = lambda n: jnp.maximum(clamp(n) - 1, 0)

    def body(qlo_ref, qhi_ref, kc_ref, kp_ref, vc_ref, vp_ref, sink_ref, dolo_ref, dohi_ref,
             dq_ref, dkv_ref, dsink_ref, carry):
        n = pl.program_id(0)

        @pl.when(n == 0)
        def _():
            carry[...] = jnp.zeros_like(carry)
            dsink_ref[...] = jnp.zeros_like(dsink_ref)

        @pl.when(n < nb)
        def _():
            heads, kvs = range(ATTN_N_Q), range(ATTN_N_KV)
            q, kb, vb = _attn_block_views(qlo_ref, qhi_ref, kc_ref, kp_ref, vc_ref, vp_ref)
            do = _attn_head_views(dolo_ref, dohi_ref)
            sink = [sink_ref[h:h + 1, 0:1] for h in heads]
            s = _attn_scores(q, kb, _attn_valid(n == 0))
            dp = [_dot_nt(do[h], vb[h // ATTN_REP]) for h in heads]
            e, es, inv = _attn_softmax(s, sink)
            p = [e[h] * inv[h] for h in heads]
            delta = [jnp.sum(p[h] * dp[h], axis=1, keepdims=True) for h in heads]
            dsc = [(p[h] * (dp[h] - delta[h]) * (hd ** -0.5)).astype(BF16) for h in heads]
            pb = [p[h].astype(BF16) for h in heads]
            dq = [_dot_nn(dsc[h], kb[h // ATTN_REP]).astype(dq_ref.dtype) for h in heads]
            stack = lambda per_head, kh: jnp.concatenate(per_head[kh * ATTN_REP:(kh + 1) * ATTN_REP], axis=0)
            dkb = [_dot_tn(stack(dsc, kh), stack(q, kh)) for kh in kvs]
            dvb = [_dot_tn(stack(pb, kh), stack(do, kh)) for kh in kvs]
            dsink = [jnp.broadcast_to(jnp.sum(-es[h] * inv[h] * delta[h], axis=0, keepdims=True), (1, LANES)) for h in heads]
            dq_ref[...] = jnp.concatenate(dq, axis=1)
            dsink_ref[...] += jnp.concatenate(dsink, axis=0)
            dkv_ref[...] = (carry[...] + jnp.concatenate([d[0:w, :] for d in dkb + dvb], axis=1)).astype(dkv_ref.dtype)
            carry[...] = jnp.concatenate([d[w:2 * w, :] for d in dkb + dvb], axis=1)

        @pl.when(n == nb)
        def _():
            dkv_ref[...] = carry[...].astype(dkv_ref.dtype)

    qh = lambda half: pl.BlockSpec((w, ATTN_Q_HALF), lambda n: (clamp(n), half))
    kv = lambda blk, idx: pl.BlockSpec((w, ATTN_KV_W), lambda n: (idx(n), blk))
    return pl.pallas_call(
        body, grid=(nb + 1,),
        in_specs=[qh(0), qh(1), kv(ATTN_K_BLK, clamp), kv(ATTN_K_BLK, prev), kv(ATTN_V_BLK, clamp), kv(ATTN_V_BLK, prev),
                  pl.BlockSpec((ATTN_N_Q, LANES), lambda n: (0, 0)), qh(0), qh(1)],
        out_specs=[pl.BlockSpec((w, D_MODEL), lambda n: (clamp(n), 0)),
                   pl.BlockSpec((w, 2 * ATTN_KV_W), lambda n: (jnp.maximum(n - 1, 0), 0)),
                   pl.BlockSpec((ATTN_N_Q, LANES), lambda n: (0, 0))],
        out_shape=[jax.ShapeDtypeStruct((t, D_MODEL), BF16), jax.ShapeDtypeStruct((t, 2 * ATTN_KV_W), BF16),
                   jax.ShapeDtypeStruct((ATTN_N_Q, LANES), F32)],
        scratch_shapes=[pltpu.VMEM((w, 2 * ATTN_KV_W), F32)],
        compiler_params=_params("arbitrary"), name=name)(qkv, qkv, qkv, qkv, qkv, qkv, sinks_b, dout, dout)


def _sq_relu_epilogue(acc):
    r = jnp.maximum(acc, 0.0)
    return (r * r,)


def _sq_relu_bwd_epilogue(acc, act):
    return (acc * (2.0 * jnp.sqrt(act.astype(F32))),)


def _bias_epilogue(acc, bias):
    return (acc + bias,)


def _plain_run(stage, fn, *args, **kwargs):
    return fn(*args, **kwargs)


def _mlp_fwd(u, w_up, w_down, tag, run=_plain_run):
    act = run(f"mlp_up_{tag}", _matmul, u, w_up, mode="nn", out_dtypes=(BF16,), epilogue=_sq_relu_epilogue, b_shards=True,
              name=f"mlp_up_{tag}")
    f = run(f"mlp_down_{tag}", _matmul, act, w_down, mode="nn", out_dtypes=(F32,), name=f"mlp_down_{tag}")
    return act, f


def _mlp_bwd(u, act, w_up, w_down, df, tag):
    dpre = _matmul(df, w_down, mode="nt", out_dtypes=(BF16,), epilogue=_sq_relu_bwd_epilogue,
                   extras=((act, "tile"),), name=f"mlp_dact_{tag}")
    dw_down = _matmul(act, df, mode="tn", out_dtypes=(BF16,), name=f"mlp_dwdown_{tag}")
    du = _matmul(dpre, w_up, mode="nt", out_dtypes=(F32,), b_shards=True, name=f"mlp_du_{tag}")
    dw_up = _matmul(u, dpre, mode="tn", out_dtypes=(BF16,), out_shards=True, name=f"mlp_dwup_{tag}")
    return du, dw_up, dw_down


def _group_rows(dt):
    t = dt.shape[0]
    return jnp.transpose(dt[:, :SSD_N_HEADS].reshape(t, SSD_N_GROUPS, SSD_HPG), (1, 2, 0))


def _head_param_rows(p):
    return jnp.broadcast_to(p.reshape(SSD_N_GROUPS, SSD_HPG, 1), (SSD_N_GROUPS, SSD_HPG, LANES))


def _local_step(x, target, wts, comm=None):
    t = x.shape[0]
    wts = dict(wts)
    row = lambda v: v.reshape(1, -1)
    mix_pre, mix_post, ffn_pre, ffn_post = wts["mix_pre_norm"], wts["mix_post_norm"], wts["ffn_pre_norm"], wts["ffn_post_norm"]

    def gathering(stage, fn, *args, **kwargs):
        hook = comm.gather_hook(stage) if comm is not None else None
        if hook is None:
            return fn(*args, **kwargs)
        out, got = fn(*args, hook=hook, **kwargs)
        wts.update(comm.weights_from(stage, got))
        return out

    u0 = _rms_fwd(x, row(mix_pre[0]), name="rms_pre_mix0")
    zx, dt_raw = gathering("in_proj", _matmul, u0, wts["ssd_w_in"], mode="nn", out_dtypes=(BF16,), tn=896, f32_tail=True,
                           name="ssd_in_proj")
    xc = gathering("conv", _conv_fwd, zx, wts["ssd_conv_w"], row(wts["ssd_conv_b"]), name="ssd_conv_fwd")
    bias_row = jnp.pad(wts["ssd_dt_bias"], (0, LANES - SSD_N_HEADS)).reshape(1, LANES)
    alog_row = jnp.pad(wts["ssd_a_log"], (0, LANES - SSD_N_HEADS)).reshape(1, LANES)
    dt, cum = _softplus_fwd(dt_raw, bias_row, alog_row, name="ssd_dt_fwd")
    dtr, cumr = _group_rows(dt), _group_rows(cum)
    alog_b, d_b = _head_param_rows(wts["ssd_a_log"]), _head_param_rows(wts["ssd_d"])
    y_ssd, states = gathering("scan", _ssd_fwd, xc, dtr, cumr, alog_b, d_b, name="ssd_scan_fwd")
    norm_w = row(wts["ssd_norm_w"])
    yn = _gate_norm_fwd(y_ssd, zx, norm_w, name="ssd_gate_norm_fwd")
    mix0 = _matmul(yn, wts["ssd_w_out"], mode="nn", out_dtypes=(F32,), name="ssd_out_proj")
    h1, v0 = _rms_fwd(mix0, row(mix_post[0]), resid=x, want_u=row(ffn_pre[0]), name="rms_post_mix0")
    act0, f0 = _mlp_fwd(v0, wts["mlp_w_up0"], wts["mlp_w_down0"], "l0", run=gathering)
    h2, u1 = _rms_fwd(f0, row(ffn_post[0]), resid=h1, want_u=row(mix_pre[1]), name="rms_post_ffn0")

    qkv = _matmul(u1, wts["attn_w_qkv"], mode="nn", out_dtypes=(BF16,), epilogue=_bias_epilogue,
                  extras=((row(wts["attn_b_qkv"]), "row"),), b_shards=True, name="attn_qkv_proj")
    sinks_b = jnp.broadcast_to(wts["attn_sinks"].reshape(ATTN_N_Q, 1), (ATTN_N_Q, LANES))
    ao = _attn_fwd(qkv, sinks_b, name="attn_fwd")
    mix1 = _matmul(ao, wts["attn_w_o"], mode="nn", out_dtypes=(F32,), epilogue=_bias_epilogue,
                   extras=((row(wts["attn_b_o"]), "row"),), name="attn_out_proj")
    h3, v1 = _rms_fwd(mix1, row(mix_post[1]), resid=h2, want_u=row(ffn_pre[1]), name="rms_post_mix1")
    act1, f1 = _mlp_fwd(v1, wts["mlp_w_up1"], wts["mlp_w_down1"], "l1")
    dh4, loss_tile = _rms_fwd(f1, row(ffn_post[1]), resid=h3, target=target, name="rms_post_ffn1_loss")

    df1, g_ffn_post1 = _rms_bwd(f1, row(ffn_post[1]), dh4, out_dtype=BF16, name="rms_post_ffn1_bwd")
    dv1, g_up1, g_down1 = _mlp_bwd(v1, act1, wts["mlp_w_up1"], wts["mlp_w_down1"], df1, "l1")
    dh3, g_ffn_pre1 = _rms_bwd(h3, row(ffn_pre[1]), dv1, resid=dh4, name="rms_pre_ffn1_bwd")
    dmix1, g_mix_post1, g_b_o = _rms_bwd(mix1, row(mix_post[1]), dh3, out_dtype=BF16, dx_col_sum=True, name="rms_post_mix1_bwd")
    g_w_o = _matmul(ao, dmix1, mode="tn", out_dtypes=(BF16,), name="attn_dwo")
    dao = _matmul(dmix1, wts["attn_w_o"], mode="nt", out_dtypes=(BF16,), name="attn_dao")
    dq, dkv, g_sinks = _attn_bwd(qkv, sinks_b, dao, name="attn_bwd")
    dqkv = jnp.concatenate([dq, dkv], axis=1)
    g_b_qkv = _col_sum(dqkv, name="attn_bqkv_grad")
    g_w_qkv = _matmul(u1, dqkv, mode="tn", out_dtypes=(BF16,), tn=ATTN_QKV // N_CHIPS, out_shards=True, name="attn_dwqkv")
    du1 = _matmul(dqkv, wts["attn_w_qkv"], mode="nt", out_dtypes=(F32,), b_shards=True, name="attn_du")
    dh2, g_mix_pre1 = _rms_bwd(h2, row(mix_pre[1]), du1, resid=dh3, name="rms_pre_mix1_bwd")

    df0, g_ffn_post0 = _rms_bwd(f0, row(ffn_post[0]), dh2, out_dtype=BF16, name="rms_post_ffn0_bwd")
    dv0, g_up0, g_down0 = _mlp_bwd(v0, act0, wts["mlp_w_up0"], wts["mlp_w_down0"], df0, "l0")
    dh1, g_ffn_pre0 = _rms_bwd(h1, row(ffn_pre[0]), dv0, resid=dh2, name="rms_pre_ffn0_bwd")
    dmix0, g_mix_post0 = _rms_bwd(mix0, row(mix_post[0]), dh1, out_dtype=BF16, name="rms_post_mix0_bwd")
    g_w_out = _matmul(yn, dmix0, mode="tn", out_dtypes=(BF16,), name="ssd_dwout")
    dyn = _matmul(dmix0, wts["ssd_w_out"], mode="nt", out_dtypes=(BF16,), name="ssd_dyn")
    dy_ssd, dzx, g_norm_w = _gate_norm_bwd(y_ssd, zx, norm_w, dyn, name="ssd_gate_norm_bwd")
    mats = {"ssd_w_out": g_w_out, "attn_w_qkv": g_w_qkv, "attn_w_o": g_w_o,
            "mlp_w_up0": g_up0, "mlp_w_up1": g_up1, "mlp_w_down0": g_down0, "mlp_w_down1": g_down1}
    if comm is None:
        dxc, dbm, dcm, ddt_r, dpar = _ssd_bwd(xc, dtr, cumr, alog_b, d_b, states, dy_ssd, name="ssd_scan_bwd")
    else:
        (dxc, dbm, dcm, ddt_r, dpar), received = _ssd_bwd(xc, dtr, cumr, alog_b, d_b, states, dy_ssd,
                                                          name="ssd_scan_bwd", hook=comm.exchange_hook(mats, "early"))
        comm.received(received)
    dzx, g_conv_w, g_conv_b = _conv_bwd(zx, wts["ssd_conv_w"], row(wts["ssd_conv_b"]), dxc, dbm, dcm, dzx, name="ssd_conv_bwd")
    ddt = jnp.pad(jnp.transpose(ddt_r, (2, 0, 1)).reshape(t, SSD_N_HEADS), ((0, 0), (0, LANES - SSD_N_HEADS)))
    dzx, g_dt_bias = _softplus_bwd(dt_raw, bias_row, ddt, dzx, name="ssd_dt_bwd")
    g_w_in = _w_in_to_shards(_matmul(u0, dzx, mode="tn", out_dtypes=(F32,), tn=896, name="ssd_dwin"), name="ssd_dwin_shards")
    mats["ssd_w_in"] = g_w_in
    if comm is None:
        du0 = _matmul(dzx, wts["ssd_w_in"], mode="nt", out_dtypes=(F32,), tk=896, name="ssd_du")
    else:
        du0, received = _matmul(dzx, wts["ssd_w_in"], mode="nt", out_dtypes=(F32,), tk=896, name="ssd_du",
                                hook=comm.exchange_hook(mats, "late"))
        comm.received(received)
    grad_x, g_mix_pre0 = _rms_bwd(x, row(mix_pre[0]), du0, resid=dh1, name="rms_pre_mix0_bwd")

    dpar = dpar.reshape(SSD_N_HEADS, LANES)
    vecs = {
        "ssd_conv_w": g_conv_w, "ssd_conv_b": g_conv_b.reshape(-1),
        "ssd_dt_bias": g_dt_bias[0, :SSD_N_HEADS], "ssd_a_log": dpar[:, 0], "ssd_d": dpar[:, 1],
        "ssd_norm_w": g_norm_w.reshape(-1), "attn_b_qkv": g_b_qkv.reshape(-1), "attn_sinks": g_sinks[:, 0],
        "attn_b_o": g_b_o.reshape(-1),
        "mix_pre_norm": jnp.concatenate([g_mix_pre0, g_mix_pre1]), "mix_post_norm": jnp.concatenate([g_mix_post0, g_mix_post1]),
        "ffn_pre_norm": jnp.concatenate([g_ffn_pre0, g_ffn_pre1]), "ffn_post_norm": jnp.concatenate([g_ffn_post0, g_ffn_post1]),
    }
    return loss_tile, grad_x, mats, vecs


def _mesh_position():
    return lax.axis_index("x"), lax.axis_index("y"), lax.axis_index("c")


def _flip(v, bit):
    return 1 - v if bit else v


OTHER_CHIPS = ((1, 0), (0, 1), (1, 1))


def _comm_params():
    return pltpu.CompilerParams(vmem_limit_bytes=VMEM_LIMIT)


def _staged_copies(srcs, dsts, bufs, sems_in, sems_out):
    loads = [pltpu.make_async_copy(s, b, sems_in.at[i]) for i, (s, b) in enumerate(zip(srcs, bufs))]
    stores = [pltpu.make_async_copy(b, d, sems_out.at[i]) for i, (b, d) in enumerate(zip(bufs, dsts))]
    return loads, stores


class _GatherHook:
    def __init__(self, mats, vecs=()):
        self.arrs = list(mats) + list(vecs)
        self.nm, self.n = len(mats), len(self.arrs)
        n_ici, n_fwd = (N_CHIPS - 1) * self.n, max((N_CHIPS - 1) * self.nm, 1)
        dma = pltpu.SemaphoreType.DMA
        self.out_shape = [jax.ShapeDtypeStruct((N_CHIPS,) + a.shape, a.dtype) for a in self.arrs]
        self.scratch = [pltpu.VMEM(a.shape, a.dtype) for a in self.arrs] + [
            dma((n_ici,)), dma((n_ici,)), dma((n_fwd,)), dma((n_fwd,)), dma((self.n,)), dma((self.n,))]

    def plan(self, ins, outs, scratch):
        n, nm = self.n, self.nm
        bufs = scratch[:n]
        ici_send, ici_recv, fwd_send, fwd_recv, load_sems, store_sems = scratch[n:]
        xi, yi, ci = _mesh_position()
        me = 2 * xi + yi
        loads, stores = _staged_copies(ins, [outs[i].at[me] for i in range(n)], bufs, load_sems, store_sems)
        sends, landed, forwards, from_sibling = [], [], [], []
        for j, (bx, by) in enumerate(OTHER_CHIPS):
            px, py = _flip(xi, bx), _flip(yi, by)
            peer = 2 * px + py
            for i in range(n):
                k = j * n + i
                mk = functools.partial(pltpu.make_async_remote_copy, send_sem=ici_send.at[k], recv_sem=ici_recv.at[k],
                                       device_id=(px, py, ci), device_id_type=MESH)
                if i < nm:
                    sends.append(mk(src_ref=ins[i].at[ci], dst_ref=outs[i].at[me, ci]))
                    landed.append(mk(src_ref=ins[i].at[ci], dst_ref=outs[i].at[peer, ci]))
                    kf = j * nm + i
                    fw = functools.partial(pltpu.make_async_remote_copy, send_sem=fwd_send.at[kf], recv_sem=fwd_recv.at[kf],
                                           device_id=(xi, yi, 1 - ci), device_id_type=MESH)
                    forwards.append(fw(src_ref=outs[i].at[peer, ci], dst_ref=outs[i].at[peer, ci]))
                    from_sibling.append(fw(src_ref=outs[i].at[peer, ci], dst_ref=outs[i].at[peer, 1 - ci]))
                else:
                    sends.append(mk(src_ref=ins[i], dst_ref=outs[i].at[me]))
                    landed.append(mk(src_ref=ins[i], dst_ref=outs[i].at[peer]))
                    forwards.append(None)
        return loads, stores, sends, landed, forwards, from_sibling

    @staticmethod
    def start(p):
        loads, _, sends, _, _, _ = p
        for cp in loads + sends:
            cp.start()

    @staticmethod
    def relay(p):
        loads, stores, _, landed, forwards, _ = p
        for ld, st in zip(loads, stores):
            ld.wait()
            st.start()
        for cp, fw in zip(landed, forwards):
            cp.wait_recv()
            if fw is not None:
                fw.start()

    @staticmethod
    def finish(p):
        _, stores, sends, _, forwards, from_sibling = p
        for cp in from_sibling:
            cp.wait_recv()
        for cp in sends + [fw for fw in forwards if fw is not None]:
            cp.wait_send()
        for st in stores:
            st.wait()


def _run_hook(hook, ins, outs, scratch, step, n_steps):
    p = hook.plan(ins, outs, scratch)
    relay_step = min(max(1, (3 * n_steps) // 4), n_steps - 1)

    @pl.when(step == 0)
    def _():
        hook.start(p)

    if relay_step < n_steps - 1:
        @pl.when(step == relay_step)
        def _():
            hook.relay(p)

    @pl.when(step == n_steps - 1)
    def _():
        if relay_step == n_steps - 1:
            hook.relay(p)
        hook.finish(p)


def _hook_call(hook, *, name):
    n = len(hook.arrs)

    def body(*refs):
        p = hook.plan(refs[:n], refs[n:n + len(hook.out_shape)], refs[n + len(hook.out_shape):])
        hook.start(p)
        hook.relay(p)
        hook.finish(p)

    return pl.pallas_call(
        body, in_specs=[ANY] * n, out_specs=[ANY] * len(hook.out_shape), out_shape=hook.out_shape,
        scratch_shapes=hook.scratch, compiler_params=_comm_params(), name=name)(*hook.arrs)


def _send_other_half(parts, *, name):
    n = len(parts)

    def body(*refs):
        ins, outs = refs[:n], refs[n:2 * n]
        send_sems, recv_sems = refs[2 * n:]
        xi, yi, ci = _mesh_position()
        sibling = (xi, yi, 1 - ci)
        for i in range(n):
            for s in range(N_CHIPS):
                pltpu.make_async_remote_copy(src_ref=ins[i].at[s, 1 - ci], dst_ref=outs[i].at[s], send_sem=send_sems.at[i],
                                             recv_sem=recv_sems.at[i], device_id=sibling, device_id_type=MESH).start()
        for i in range(n):
            pltpu.make_async_remote_copy(src_ref=outs[i], dst_ref=outs[i], send_sem=send_sems.at[i], recv_sem=recv_sems.at[i],
                                         device_id=sibling, device_id_type=MESH).wait()

    return pl.pallas_call(
        body, in_specs=[ANY] * n, out_specs=[ANY] * n,
        out_shape=[jax.ShapeDtypeStruct((p.shape[0],) + p.shape[2:], p.dtype) for p in parts],
        scratch_shapes=[pltpu.SemaphoreType.DMA((n,)), pltpu.SemaphoreType.DMA((n,))],
        name=name)(*parts)


ROW_BLOCKS = 8


def _add_sibling_half(parts, theirs, core, *, name):
    n = len(parts)

    def body(core_ref, *refs):
        for a_ref, b_ref, o_ref in zip(refs[:n], refs[n:2 * n], refs[2 * n:]):
            o_ref[...] = (a_ref[...].astype(F32) + b_ref[...].astype(F32)).astype(o_ref.dtype)

    mine = lambda p: pl.BlockSpec((None, None, p.shape[2] // ROW_BLOCKS, p.shape[3]), lambda s, rb, core_ref: (s, core_ref[0], rb, 0))
    other = lambda p: pl.BlockSpec((None, p.shape[1] // ROW_BLOCKS, p.shape[2]), lambda s, rb, core_ref: (s, rb, 0))
    return pl.pallas_call(
        body,
        grid_spec=pltpu.PrefetchScalarGridSpec(
            num_scalar_prefetch=1, grid=(N_CHIPS, ROW_BLOCKS),
            in_specs=[mine(p) for p in parts] + [other(q) for q in theirs], out_specs=[other(q) for q in theirs]),
        out_shape=[jax.ShapeDtypeStruct(q.shape, BF16) for q in theirs],
        compiler_params=_params("parallel", "parallel"), name=name)(core, *parts, *theirs)


class _ExchangeHook:
    def __init__(self, parts, to_all=()):
        self.arrs = list(parts) + list(to_all)
        self.n_parts, self.n = len(parts), len(self.arrs)
        n_ici, n_peer = max((N_CHIPS - 1) * self.n_parts, 1), (N_DEV - 1) * max(len(to_all), 1)
        dma = pltpu.SemaphoreType.DMA
        self.out_shape = [jax.ShapeDtypeStruct(p.shape, p.dtype) for p in parts] + [
            jax.ShapeDtypeStruct((N_DEV,) + a.shape, a.dtype) for a in to_all]
        self.scratch = [pltpu.VMEM(p.shape[1:], p.dtype) for p in parts] + [pltpu.VMEM(a.shape, a.dtype) for a in to_all] + [
            dma((n_ici,)), dma((n_ici,)), dma((n_peer,)), dma((n_peer,)), dma((self.n,)), dma((self.n,))]

    def plan(self, ins, outs, scratch):
        n, npt = self.n, self.n_parts
        bufs = scratch[:n]
        send_sems, recv_sems, all_send, all_recv, load_sems, store_sems = scratch[n:]
        xi, yi, ci = _mesh_position()
        me_chip = 2 * xi + yi
        me = 4 * xi + 2 * yi + ci
        loads, stores = _staged_copies([ins[i].at[me_chip] for i in range(npt)] + list(ins[npt:]),
                                       [outs[i].at[me_chip] for i in range(npt)] + [outs[i].at[me] for i in range(npt, n)],
                                       bufs, load_sems, store_sems)
        sends, recvs = [], []
        for j, (bx, by) in enumerate(OTHER_CHIPS):
            px, py = _flip(xi, bx), _flip(yi, by)
            peer = 2 * px + py
            for i in range(npt):
                k = j * npt + i
                mk = functools.partial(pltpu.make_async_remote_copy, src_ref=ins[i].at[peer], send_sem=send_sems.at[k],
                                       recv_sem=recv_sems.at[k], device_id=(px, py, ci), device_id_type=MESH)
                sends.append(mk(dst_ref=outs[i].at[me_chip]))
                recvs.append(mk(dst_ref=outs[i].at[peer]))
        for i in range(npt, n):
            for k in range(1, N_DEV):
                px, py, pc = _flip(xi, (k >> 2) & 1), _flip(yi, (k >> 1) & 1), _flip(ci, k & 1)
                slot = (i - npt) * (N_DEV - 1) + k - 1
                mk = functools.partial(pltpu.make_async_remote_copy, src_ref=ins[i], send_sem=all_send.at[slot],
                                       recv_sem=all_recv.at[slot], device_id=(px, py, pc), device_id_type=MESH)
                sends.append(mk(dst_ref=outs[i].at[me]))
                recvs.append(mk(dst_ref=outs[i].at[4 * px + 2 * py + pc]))
        return loads, stores, sends, recvs

    @staticmethod
    def start(p):
        loads, _, sends, _ = p
        for cp in loads + sends:
            cp.start()

    @staticmethod
    def relay(p):
        loads, stores, _, _ = p
        for ld, st in zip(loads, stores):
            ld.wait()
            st.start()

    @staticmethod
    def finish(p):
        _, stores, sends, recvs = p
        for cp in recvs:
            cp.wait_recv()
        for cp in sends:
            cp.wait_send()
        for st in stores:
            st.wait()


def _sum_chips(parts, *, name):
    n = len(parts)
    p = parts[0].shape[0]

    def body(*refs):
        s = pl.program_id(1)
        for x_ref, o_ref in zip(refs[:n], refs[n:]):
            @pl.when(s == 0)
            def _():
                o_ref[...] = x_ref[...].astype(F32)

            @pl.when(s > 0)
            def _():
                o_ref[...] += x_ref[...].astype(F32)

    blocks = lambda q: ROW_BLOCKS if q.shape[1] % (8 * ROW_BLOCKS) == 0 else 1
    assert len({blocks(q) for q in parts}) == 1
    nb = blocks(parts[0])
    return pl.pallas_call(
        body, grid=(nb, p),
        in_specs=[pl.BlockSpec((None, q.shape[1] // nb, q.shape[2]), lambda rb, s: (s, rb, 0)) for q in parts],
        out_specs=[pl.BlockSpec((q.shape[1] // nb, q.shape[2]), lambda rb, s: (rb, 0)) for q in parts],
        out_shape=[jax.ShapeDtypeStruct(q.shape[1:], F32) for q in parts],
        compiler_params=_params("parallel", "arbitrary"), name=name)(*parts)


def _swap_halves(halves, layers, *, name):
    n = len(halves)
    out_shapes, slots = [], []
    for i, h in enumerate(halves):
        pair = [p for p in layers if i in p]
        if pair and pair[0][1] == i:
            slots.append((slots[pair[0][0]][0], 1))
        elif pair:
            out_shapes.append(jax.ShapeDtypeStruct((2, 2) + h.shape, h.dtype))
            slots.append((len(out_shapes) - 1, 0))
        else:
            out_shapes.append(jax.ShapeDtypeStruct((2,) + h.shape, h.dtype))
            slots.append((len(out_shapes) - 1, None))
    n_out = len(out_shapes)

    def body(*refs):
        ins, outs, bufs = refs[:n], refs[n:n + n_out], refs[n + n_out:2 * n + n_out]
        send_sems, recv_sems, load_sems, store_sems = refs[2 * n + n_out:]
        xi, yi, ci = _mesh_position()
        own, sends, recvs = [], [], []
        for i in range(n):
            o, layer = slots[i]
            dst = (lambda core: outs[o].at[core]) if layer is None else (lambda core: outs[o].at[layer, core])
            own.append(dst(ci))
            mk = functools.partial(pltpu.make_async_remote_copy, src_ref=ins[i], send_sem=send_sems.at[i],
                                   recv_sem=recv_sems.at[i], device_id=(xi, yi, 1 - ci), device_id_type=MESH)
            sends.append(mk(dst_ref=dst(ci)))
            recvs.append(mk(dst_ref=dst(1 - ci)))
        loads, stores = _staged_copies(ins, own, bufs, load_sems, store_sems)
        for cp in loads + sends:
            cp.start()
        for ld, st in zip(loads, stores):
            ld.wait()
            st.start()
        for cp in recvs:
            cp.wait_recv()
        for cp in sends:
            cp.wait_send()
        for st in stores:
            st.wait()

    return pl.pallas_call(
        body, in_specs=[ANY] * n, out_specs=[ANY] * n_out, out_shape=out_shapes,
        scratch_shapes=[pltpu.VMEM(h.shape, h.dtype) for h in halves]
        + [pltpu.SemaphoreType.DMA((n,)), pltpu.SemaphoreType.DMA((n,)), pltpu.SemaphoreType.DMA((n,)), pltpu.SemaphoreType.DMA((n,))],
        compiler_params=_comm_params(), name=name)(*halves)


def _cast_bf16(layers, *, name, hook=None):
    n = len(layers)
    hk = _HookSlots(hook, n_in=n, n_out=n, n_scratch=0)

    def body(*refs):
        ins, outs, _ = hk.own(refs)
        if hook is not None:
            hk.run(refs, pl.program_id(0), ROW_BLOCKS)
        for i_ref, o_ref in zip(ins, outs):
            o_ref[...] = i_ref[...].astype(o_ref.dtype)

    in_blk = lambda a, l: pl.BlockSpec((None, a.shape[1] // ROW_BLOCKS, a.shape[2]), lambda i: (l, i, 0))
    out_blk = lambda a: pl.BlockSpec((a.shape[1] // ROW_BLOCKS, a.shape[2]), lambda i: (i, 0))
    outs = pl.pallas_call(
        body, grid=(ROW_BLOCKS,),
        in_specs=[in_blk(a, l) for a, l in layers] + hk.in_specs,
        out_specs=[out_blk(a) for a, _ in layers] + hk.out_specs,
        out_shape=[jax.ShapeDtypeStruct(a.shape[1:], BF16) for a, _ in layers] + hk.out_shape,
        scratch_shapes=hk.scratch,
        compiler_params=_params(*hk.semantics("parallel")), name=name)(*[a for a, _ in layers], *hk.inputs)
    return outs[:n] if hook is None else (outs[:n], outs[n:])


def _full_weight(name, gathered):
    s, _, r, c = gathered.shape
    if name == "ssd_w_in":
        return _w_in_from_shards(gathered.reshape(s, 2 * r, c), name="ssd_w_in_unshard")
    if name in ("attn_w_qkv", "mlp_w_up0", "mlp_w_up1"):
        return gathered.reshape(s, 2 * r, c)
    return gathered.reshape(s * 2 * r, c)


class _StepComm:
    GATHER = {"in_proj": ("mlp_w_up0", "attn_w_qkv"), "conv": ("mlp_w_down0", "attn_w_o"), "scan": ("ssd_w_out",),
              "mlp_up_l0": ("mlp_w_up1",), "mlp_down_l0": ("mlp_w_down1",)}
    EXCHANGE = {"early": ("ssd_w_out", "attn_w_qkv", "attn_w_o", "mlp_w_up0", "mlp_w_up1", "mlp_w_down0", "mlp_w_down1"),
                "late": ("ssd_w_in",)}

    def __init__(self, shards, core):
        self.shards, self.core = shards, core
        self.chip_parts = {}
        self._pending = None

    def gather_hook(self, stage):
        names = self.GATHER.get(stage)
        return _GatherHook([self.shards[n] for n in names]) if names else None

    def weights_from(self, stage, gathered):
        return {n: _full_weight(n, g) for n, g in zip(self.GATHER[stage], gathered)}

    def chip_sums(self, mats, tag):
        parts = [_shard_halves(a) for a in mats.values()]
        theirs = _send_other_half(parts, name=f"grad_sibling_send_{tag}")
        return _add_sibling_half(parts, theirs, self.core, name=f"grad_chip_sum_{tag}")

    def exchange_hook(self, mats, which):
        self._pending = self.EXCHANGE[which]
        return _ExchangeHook(self.chip_sums({n: mats[n] for n in self._pending}, which))

    def received(self, arrays):
        self.chip_parts.update(zip(self._pending, arrays))


ADAMW_ROW_BLOCKS = 16


def _adamw(ws, gs, ms, vs, *, name):
    n = len(ws)
    nb = ADAMW_ROW_BLOCKS if all(a.shape[0] % (8 * ADAMW_ROW_BLOCKS) == 0 for a in ws) else 1

    def body(*refs):
        ins, outs = refs[:4 * n], refs[4 * n:]
        for i in range(n):
            w_ref, g_ref, m_ref, v_ref = ins[i], ins[n + i], ins[2 * n + i], ins[3 * n + i]
            go_ref, d_ref, nm_ref, nv_ref = outs[i], outs[n + i], outs[2 * n + i], outs[3 * n + i]
            gv = g_ref[...]
            nm = ADAM_B1 * m_ref[...] + (1.0 - ADAM_B1) * gv
            nv = ADAM_B2 * v_ref[...] + (1.0 - ADAM_B2) * (gv * gv)
            m_hat = nm / (1.0 - ADAM_B1 ** ADAM_STEP)
            v_hat = nv / (1.0 - ADAM_B2 ** ADAM_STEP)
            go_ref[...] = gv
            d_ref[...] = -ADAM_LR * (m_hat / (jnp.sqrt(v_hat) + ADAM_EPS) + ADAM_WD * w_ref[...])
            nm_ref[...] = nm
            nv_ref[...] = nv

    blks = [pl.BlockSpec((a.shape[0] // nb, a.shape[1]), lambda i: (i, 0)) for a in ws]
    shapes = [jax.ShapeDtypeStruct(a.shape, F32) for a in ws]
    outs = pl.pallas_call(body, grid=(nb,), in_specs=blks * 4, out_specs=blks * 4, out_shape=shapes * 4,
                          compiler_params=_params("parallel"), name=name)(*ws, *gs, *ms, *vs)
    return [tuple(outs[k * n + i] for k in range(4)) for i in range(n)]


SM_CONV_B, SM_NORM_W, SM_MIX_PRE, SM_MIX_POST, SM_FFN_PRE, SM_FFN_POST, SM_MISC, SM_CONV_W, SM_B_QKV, SM_B_O = 0, 4, 6, 8, 10, 12, 14, 16, 32, 34
SM_ROWS = 40
MISC_DT_BIAS, MISC_A_LOG, MISC_D, MISC_SINKS, MISC_LOSS = 0, 32, 64, 96, 112


def _shard_halves(a):
    c = a.shape[-1]
    return a.reshape(N_CHIPS, 2, -1, c)


def _rows(v):
    return v.reshape(-1, D_MODEL)


def _misc_row(dt_bias, a_log, d, sinks, loss):
    pad = jnp.zeros((D_MODEL - MISC_LOSS - 1,), F32)
    return jnp.concatenate([dt_bias.reshape(-1), a_log.reshape(-1), d.reshape(-1), sinks.reshape(-1), loss.reshape(1), pad]).reshape(1, D_MODEL)


def _replicated_rows(p, loss):
    return jnp.concatenate([
        _rows(p["ssd_conv_b"]), _rows(p["ssd_norm_w"]), _rows(p["mix_pre_norm"]), _rows(p["mix_post_norm"]),
        _rows(p["ffn_pre_norm"]), _rows(p["ffn_post_norm"]),
        _misc_row(p["ssd_dt_bias"], p["ssd_a_log"], p["ssd_d"], p["attn_sinks"], loss), jnp.zeros((1, D_MODEL), F32)], axis=0)


def _sharded_rows(conv_w, b_qkv, b_o):
    last = jnp.concatenate([b_qkv.reshape(-1), b_o.reshape(-1), jnp.zeros((D_MODEL - 640,), F32)]).reshape(1, D_MODEL)
    return jnp.concatenate([conv_w.reshape(SSD_CONV_WIDTH, D_MODEL), last, jnp.zeros((3, D_MODEL), F32)], axis=0)


REPLICATED = ("ssd_conv_b", "ssd_dt_bias", "ssd_a_log", "ssd_d", "ssd_norm_w", "attn_sinks",
              "mix_pre_norm", "mix_post_norm", "ffn_pre_norm", "ffn_post_norm")
MATRICES = ("ssd_w_in", "ssd_w_out", "attn_w_qkv", "attn_w_o", "mlp_w_up", "mlp_w_down")
WEIGHT_NAMES = ("ssd_w_in", "ssd_conv_w", "ssd_conv_b", "ssd_dt_bias", "ssd_a_log", "ssd_d", "ssd_norm_w", "ssd_w_out",
                "attn_w_qkv", "attn_b_qkv", "attn_sinks", "attn_w_o", "attn_b_o", "mlp_w_up", "mlp_w_down",
                "mix_pre_norm", "mix_post_norm", "ffn_pre_norm", "ffn_post_norm")


def _unpack_small(rows16, rows8, like):
    misc = rows16[SM_MISC]
    out = {
        "ssd_conv_b": rows16[SM_CONV_B:SM_CONV_B + 4], "ssd_norm_w": rows16[SM_NORM_W:SM_NORM_W + 2],
        "mix_pre_norm": rows16[SM_MIX_PRE:SM_MIX_PRE + 2], "mix_post_norm": rows16[SM_MIX_POST:SM_MIX_POST + 2],
        "ffn_pre_norm": rows16[SM_FFN_PRE:SM_FFN_PRE + 2], "ffn_post_norm": rows16[SM_FFN_POST:SM_FFN_POST + 2],
        "ssd_dt_bias": misc[MISC_DT_BIAS:MISC_DT_BIAS + 32], "ssd_a_log": misc[MISC_A_LOG:MISC_A_LOG + 32],
        "ssd_d": misc[MISC_D:MISC_D + 32], "attn_sinks": misc[MISC_SINKS:MISC_SINKS + 16],
        "ssd_conv_w": rows8[0:SSD_CONV_WIDTH], "attn_b_qkv": rows8[SSD_CONV_WIDTH, 0:384], "attn_b_o": rows8[SSD_CONV_WIDTH, 384:640],
    }
    return {k: v.reshape(like[k].shape) for k, v in out.items()}


def kernel(x, ssd_w_in, ssd_conv_w, ssd_conv_b, ssd_dt_bias, ssd_a_log, ssd_d, ssd_norm_w, ssd_w_out, attn_w_qkv, attn_b_qkv, attn_sinks, attn_w_o, attn_b_o, mlp_w_up, mlp_w_down, mix_pre_norm, mix_post_norm, ffn_pre_norm, ffn_post_norm, loss_target, m_ssd_w_in, m_ssd_conv_w, m_ssd_conv_b, m_ssd_dt_bias, m_ssd_a_log, m_ssd_d, m_ssd_norm_w, m_ssd_w_out, m_attn_w_qkv, m_attn_b_qkv, m_attn_sinks, m_attn_w_o, m_attn_b_o, m_mlp_w_up, m_mlp_w_down, m_mix_pre_norm, m_mix_post_norm, m_ffn_pre_norm, m_ffn_post_norm, v_ssd_w_in, v_ssd_conv_w, v_ssd_conv_b, v_ssd_dt_bias, v_ssd_a_log, v_ssd_d, v_ssd_norm_w, v_ssd_w_out, v_attn_w_qkv, v_attn_b_qkv, v_attn_sinks, v_attn_w_o, v_attn_b_o, v_mlp_w_up, v_mlp_w_down, v_mix_pre_norm, v_mix_post_norm, v_ffn_pre_norm, v_ffn_post_norm):
    w = dict(zip(WEIGHT_NAMES, (ssd_w_in, ssd_conv_w, ssd_conv_b, ssd_dt_bias, ssd_a_log, ssd_d, ssd_norm_w, ssd_w_out, attn_w_qkv, attn_b_qkv, attn_sinks, attn_w_o, attn_b_o, mlp_w_up, mlp_w_down, mix_pre_norm, mix_post_norm, ffn_pre_norm, ffn_post_norm)))
    m = dict(zip(WEIGHT_NAMES, (m_ssd_w_in, m_ssd_conv_w, m_ssd_conv_b, m_ssd_dt_bias, m_ssd_a_log, m_ssd_d, m_ssd_norm_w, m_ssd_w_out, m_attn_w_qkv, m_attn_b_qkv, m_attn_sinks, m_attn_w_o, m_attn_b_o, m_mlp_w_up, m_mlp_w_down, m_mix_pre_norm, m_mix_post_norm, m_ffn_pre_norm, m_ffn_post_norm)))
    v = dict(zip(WEIGHT_NAMES, (v_ssd_w_in, v_ssd_conv_w, v_ssd_conv_b, v_ssd_dt_bias, v_ssd_a_log, v_ssd_d, v_ssd_norm_w, v_ssd_w_out, v_attn_w_qkv, v_attn_b_qkv, v_attn_sinks, v_attn_w_o, v_attn_b_o, v_mlp_w_up, v_mlp_w_down, v_mix_pre_norm, v_mix_post_norm, v_ffn_pre_norm, v_ffn_post_norm)))
    chip = 2 * lax.axis_index("x") + lax.axis_index("y")

    two_halves = lambda a: a.reshape(2, a.shape[0] // 2, a.shape[1])
    later = {"ssd_w_out": (w["ssd_w_out"], 0), "attn_w_qkv": (w["attn_w_qkv"], 0), "attn_w_o": (w["attn_w_o"], 0),
             "mlp_w_up0": (w["mlp_w_up"], 0), "mlp_w_up1": (w["mlp_w_up"], 1),
             "mlp_w_down0": (w["mlp_w_down"], 0), "mlp_w_down1": (w["mlp_w_down"], 1)}
    first = _GatherHook([two_halves(w["ssd_w_in"][0].astype(BF16))], [w["ssd_conv_w"][0], w["attn_b_qkv"], w["attn_b_o"]])
    cast, (g_in, g_conv, g_bqkv, g_bo) = _cast_bf16(list(later.values()), name="weights_to_bf16", hook=first)
    core = lax.axis_index("c").astype(jnp.int32).reshape(1)
    comm = _StepComm({k: two_halves(a) for k, a in zip(later, cast)}, core)
    full = {
        "ssd_w_in": _full_weight("ssd_w_in", g_in),
        "ssd_conv_w": g_conv.transpose(1, 0, 2).reshape(SSD_CONV_WIDTH, SSD_CONV_DIM),
        "attn_b_qkv": g_bqkv.reshape(ATTN_QKV), "attn_b_o": g_bo.reshape(D_MODEL),
    }
    for name in REPLICATED:
        full[name] = w[name][0] if name.startswith(("ssd_", "attn_")) else w[name]

    loss_tile, grad_x, gm, g = _local_step(x[0], loss_target[0], full, comm)

    conv_w_rows = g["ssd_conv_w"].reshape(SSD_CONV_WIDTH * N_CHIPS, D_MODEL)
    b_qkv_rows = jnp.pad(g["attn_b_qkv"], (0, 2 * D_MODEL - ATTN_QKV)).reshape(2, D_MODEL)
    small = jnp.concatenate([_replicated_rows(g, loss_tile[0, 0]), conv_w_rows, b_qkv_rows, _rows(g["attn_b_o"]),
                             jnp.zeros((SM_ROWS - SM_B_O - 1, D_MODEL), F32)], axis=0)
    small_all, = _hook_call(_ExchangeHook([], [small]), name="vector_grad_all_gather")
    order = ("ssd_w_in", "ssd_w_out", "attn_w_qkv", "attn_w_o", "mlp_w_up0", "mlp_w_up1", "mlp_w_down0", "mlp_w_down1")
    halves = _sum_chips([comm.chip_parts[k] for k in order], name="grad_sum")
    r_in, r_out, r_qkv, r_o, r_up, r_down = _swap_halves(halves, layers=((4, 5), (6, 7)), name="grad_halves_swap")
    small_sum, = _sum_chips([small_all], name="small_grad_sum")

    grads = {"ssd_w_in": r_in, "ssd_w_out": r_out, "attn_w_qkv": r_qkv, "attn_w_o": r_o, "mlp_w_up": r_up, "mlp_w_down": r_down}
    grads = {k: a.reshape(w[k].shape) for k, a in grads.items()}
    conv_w_g = lax.dynamic_index_in_dim(small_sum[SM_CONV_W:SM_CONV_W + 16].reshape(SSD_CONV_WIDTH, N_CHIPS, D_MODEL), chip, axis=1, keepdims=False)
    b_qkv_g = lax.dynamic_slice_in_dim(small_sum[SM_B_QKV:SM_B_QKV + 2].reshape(-1), chip * 384, 384)
    b_o_g = lax.dynamic_slice_in_dim(small_sum[SM_B_O], chip * 256, 256)
    small_g = jnp.concatenate([small_sum[0:16], _sharded_rows(conv_w_g, b_qkv_g, b_o_g)], axis=0)
    grads.update(_unpack_small(small_g[0:16], small_g[16:24], w))
    loss = small_sum[SM_MISC, MISC_LOSS]

    delta, new_m, new_v = {}, {}, {}
    as2d = lambda p: [p[name].reshape(-1, p[name].shape[-1]) for name in MATRICES]
    for name, (g2, d2, m2, v2) in zip(MATRICES, _adamw(as2d(w), as2d(grads), as2d(m), as2d(v), name="adamw_matrices")):
        shape = w[name].shape
        grads[name], delta[name], new_m[name], new_v[name] = g2.reshape(shape), d2.reshape(shape), m2.reshape(shape), v2.reshape(shape)
    zero = jnp.zeros((), F32)
    small_pack = lambda p: jnp.concatenate([_replicated_rows({k: p[k] for k in REPLICATED}, zero),
                                            _sharded_rows(p["ssd_conv_w"], p["attn_b_qkv"], p["attn_b_o"])], axis=0)
    (_, d_s, m_s, v_s), = _adamw([small_pack(w)], [small_g], [small_pack(m)], [small_pack(v)], name="adamw_vectors")
    delta.update(_unpack_small(d_s[0:16], d_s[16:24], w))
    new_m.update(_unpack_small(m_s[0:16], m_s[16:24], w))
    new_v.update(_unpack_small(v_s[0:16], v_s[16:24], w))

    return (loss, grad_x[None], *[grads[n] for n in WEIGHT_NAMES], *[delta[n] for n in WEIGHT_NAMES],
            *[new_m[n] for n in WEIGHT_NAMES], *[new_v[n] for n in WEIGHT_NAMES])
```

```python
import functools
import math

import jax
import jax.numpy as jnp
from jax import lax
from jax.experimental import pallas as pl
from jax.experimental.pallas import tpu as pltpu

F32 = jnp.float32
BF16 = jnp.bfloat16

D_MODEL = 1024
SSD_D_INNER = 2048
SSD_HEAD_DIM = 64
SSD_N_HEADS = 32
SSD_N_GROUPS = 8
SSD_HPG = 4
SSD_D_STATE = 128
SSD_CONV_WIDTH = 4
SSD_CHUNK = 128
SSD_CONV_DIM = 4096
SSD_IN_DIM = 6176
SSD_IN_PAD = 6400
SSD_IN_TILE = 1280
SSD_DT_COL = 6144
SSD_GW = SSD_HPG * SSD_HEAD_DIM
ATTN_HEAD_DIM = 64
ATTN_N_Q = 16
ATTN_N_KV = 4
ATTN_REP = 4
ATTN_WINDOW = 128
ATTN_QKV = 1536
D_FF = 4096
NORM_EPS = 1e-6

ADAM_LR = 0.001
ADAM_B1 = 0.9
ADAM_B2 = 0.999
ADAM_EPS = 1e-08
ADAM_WD = 0.01
ADAM_STEP = 10

N_CHIPS = 4
N_DEV = 8
LANES = 128
VMEM_LIMIT = 48 * 1024 * 1024

MESH = pl.DeviceIdType.MESH


def _params(*sem):
    return pltpu.CompilerParams(dimension_semantics=sem, vmem_limit_bytes=VMEM_LIMIT)


def _dot(a, b, dims):
    return lax.dot_general(a, b, (dims, ((), ())), preferred_element_type=F32)


def _dot_nn(a, b):
    return _dot(a, b, ((1,), (0,)))


def _dot_nt(a, b):
    return _dot(a, b, ((1,), (1,)))


def _dot_tn(a, b):
    return _dot(a, b, ((0,), (0,)))


def _sigmoid(x):
    return 0.5 * jnp.tanh(0.5 * x) + 0.5


ANY = pl.BlockSpec(memory_space=pl.ANY)


class _HookSlots:
    def __init__(self, hook, n_in, n_out, n_scratch):
        self.hook = hook
        self.n_in, self.n_out, self.n_scratch = n_in, n_out, n_scratch
        self.inputs = list(hook.arrs) if hook else []
        self.out_shape = list(hook.out_shape) if hook else []
        self.scratch = list(hook.scratch) if hook else []
        self.in_specs = [ANY] * len(self.inputs)
        self.out_specs = [ANY] * len(self.out_shape)

    def _split(self, refs):
        a = self.n_in
        b = a + len(self.inputs)
        c = b + self.n_out
        d = c + len(self.out_shape)
        e = d + self.n_scratch
        return refs[:a], refs[a:b], refs[b:c], refs[c:d], refs[d:e], refs[e:]

    def own(self, refs):
        ins, _, outs, _, scratch, _ = self._split(refs)
        return ins, outs, scratch

    def run(self, refs, step, n_steps):
        _, h_in, _, h_out, _, h_scratch = self._split(refs)
        _run_hook(self.hook, h_in, h_out, h_scratch, step, n_steps)

    def semantics(self, *sem):
        return sem if self.hook is None else ("arbitrary",) * len(sem)


def _matmul(a, b, *, mode, out_dtypes, name, epilogue=None, extras=(), tm=1024, tn=1024, tk=1024,
            b_shards=False, out_shards=False, hook=None, f32_block=None):
    f32_tail = f32_block is not None
    if b_shards:
        s, b_rows, b_cols = b.shape
        b2 = (b_rows, s * b_cols)
        if mode == "nn":
            tn = b_cols
        else:
            assert mode == "nt"
            tk = b_cols
    else:
        b2 = b.shape
    if mode == "nn":
        (m, k), (k2, n) = a.shape, b2
    elif mode == "nt":
        (m, k), (n, k2) = a.shape, b2
    else:
        (k, m), (k2, n) = a.shape, b2
    assert k == k2, (a.shape, b.shape, mode)
    tm, tn, tk = min(tm, m), min(tn, n), min(tk, k)
    assert m % tm == 0 and n % tn == 0 and k % tk == 0, (m, n, k, tm, tn, tk)
    nk = k // tk
    if mode == "tn":
        a_spec = pl.BlockSpec((tk, tm), lambda i, j, kk: (kk, i))
    else:
        a_spec = pl.BlockSpec((tm, tk), lambda i, j, kk: (i, kk))
    if b_shards and mode == "nn":
        b_spec = pl.BlockSpec((None, tk, tn), lambda i, j, kk: (j, kk, 0))
    elif b_shards:
        b_spec = pl.BlockSpec((None, tn, tk), lambda i, j, kk: (kk, j, 0))
    elif mode == "nt":
        b_spec = pl.BlockSpec((tn, tk), lambda i, j, kk: (j, kk))
    else:
        b_spec = pl.BlockSpec((tk, tn), lambda i, j, kk: (kk, j))
    dims = {"nn": ((1,), (0,)), "nt": ((1,), (1,)), "tn": ((0,), (0,))}[mode]
    ex_specs = []
    for arr, kind in extras:
        if kind == "tile":
            ex_specs.append(pl.BlockSpec((tm, tn), lambda i, j, kk: (i, j)))
        else:
            ex_specs.append(pl.BlockSpec((1, tn), lambda i, j, kk: (0, j)))
    n_ex, n_out = len(extras), len(out_dtypes)
    if epilogue is None:
        epilogue = lambda acc: (acc,)
    hk = _HookSlots(hook, n_in=2 + n_ex, n_out=n_out + f32_tail, n_scratch=0 if nk == 1 else 1)
    grid = (m // tm, n // tn, nk)

    def body(*refs):
        (a_ref, b_ref, *ex), outs, scratch = hk.own(refs)
        if hook is not None:
            step = (pl.program_id(0) * grid[1] + pl.program_id(1)) * grid[2] + pl.program_id(2)
            hk.run(refs, step, grid[0] * grid[1] * grid[2])

        def finish(acc):
            res = epilogue(acc, *[e[...] for e in ex])
            for o, r in zip(outs, res):
                o[...] = r.astype(o.dtype)
            if f32_tail:
                outs[n_out][...] = acc[:, f32_block:f32_block + LANES]

        if nk == 1:
            finish(_dot(a_ref[...], b_ref[...], dims))
        else:
            acc_ref = scratch[0]
            kk = pl.program_id(2)

            @pl.when(kk == 0)
            def _():
                acc_ref[...] = jnp.zeros_like(acc_ref)

            acc_ref[...] += _dot(a_ref[...], b_ref[...], dims)

            @pl.when(kk == nk - 1)
            def _():
                finish(acc_ref[...])

    if out_shards:
        out_spec = pl.BlockSpec((None, tm, tn), lambda i, j, kk: (j, i, 0))
        out_dims = (n // tn, m, tn)
    else:
        out_spec = pl.BlockSpec((tm, tn), lambda i, j, kk: (i, j))
        out_dims = (m, n)
    tail_specs = [pl.BlockSpec((tm, LANES), lambda i, j, kk: (i, 0))] if f32_tail else []
    tail_shapes = [jax.ShapeDtypeStruct((m, LANES), F32)] if f32_tail else []
    outs = pl.pallas_call(
        body,
        grid=grid,
        in_specs=[a_spec, b_spec] + ex_specs + hk.in_specs,
        out_specs=[out_spec for _ in out_dtypes] + tail_specs + hk.out_specs,
        out_shape=[jax.ShapeDtypeStruct(out_dims, dt) for dt in out_dtypes] + tail_shapes + hk.out_shape,
        scratch_shapes=([] if nk == 1 else [pltpu.VMEM((tm, tn), F32)]) + hk.scratch,
        compiler_params=_params(*hk.semantics("parallel", "arbitrary" if f32_tail else "parallel", "arbitrary")),
        name=name,
    )(a, b, *[arr for arr, _ in extras], *hk.inputs)
    n_own = n_out + f32_tail
    own = outs[0] if n_own == 1 else outs[:n_own]
    return own if hook is None else (own, outs[n_own:])


def _row_tile(t, want):
    return min(t, want)


def _rms_fwd(x, w, *, name, resid=None, want_u=None, target=None):
    t, d = x.shape
    tr = _row_tile(t, 512)

    def norm(v, wv):
        return v * lax.rsqrt(jnp.mean(v * v, axis=-1, keepdims=True) + NORM_EPS) * wv

    row = pl.BlockSpec((tr, d), lambda i: (i, 0))
    vec = pl.BlockSpec((1, d), lambda i: (0, 0))
    if target is not None:
        def body(x_ref, w_ref, r_ref, t_ref, dh_ref, loss_ref):
            err = r_ref[...] + norm(x_ref[...], w_ref[...]) - t_ref[...]
            dh_ref[...] = err * (1.0 / d)

            @pl.when(pl.program_id(0) == 0)
            def _():
                loss_ref[...] = jnp.zeros_like(loss_ref)

            part = jnp.sum(jnp.sum(err * err, axis=1, keepdims=True), axis=0, keepdims=True) * (0.5 / d)
            loss_ref[...] += jnp.broadcast_to(part, loss_ref.shape)

        return pl.pallas_call(
            body, grid=(t // tr,), in_specs=[row, vec, row, row],
            out_specs=[row, pl.BlockSpec((8, LANES), lambda i: (0, 0))],
            out_shape=[jax.ShapeDtypeStruct((t, d), F32), jax.ShapeDtypeStruct((8, LANES), F32)],
            compiler_params=_params("arbitrary"), name=name)(x, w, resid, target)
    if resid is None:
        def body(x_ref, w_ref, o_ref):
            o_ref[...] = norm(x_ref[...], w_ref[...]).astype(BF16)
        ins, in_specs = (x, w), [row, vec]
        out_shape, out_specs = jax.ShapeDtypeStruct((t, d), BF16), row
    elif want_u is None:
        def body(x_ref, w_ref, r_ref, o_ref):
            o_ref[...] = r_ref[...] + norm(x_ref[...], w_ref[...])
        ins, in_specs = (x, w, resid), [row, vec, row]
        out_shape, out_specs = jax.ShapeDtypeStruct((t, d), F32), row
    else:
        def body(x_ref, w_ref, r_ref, w2_ref, o_ref, u_ref):
            h = r_ref[...] + norm(x_ref[...], w_ref[...])
            o_ref[...] = h
            u_ref[...] = norm(h, w2_ref[...]).astype(BF16)
        ins, in_specs = (x, w, resid, want_u), [row, vec, row, vec]
        out_shape = [jax.ShapeDtypeStruct((t, d), F32), jax.ShapeDtypeStruct((t, d), BF16)]
        out_specs = [row, row]
    return pl.pallas_call(body, grid=(t // tr,), in_specs=in_specs, out_specs=out_specs, out_shape=out_shape,
                          compiler_params=_params("parallel"), name=name)(*ins)


def _rms_bwd(x, w, dy, *, name, resid=None, out_dtype=F32, dx_col_sum=False):
    t, d = x.shape
    tr = _row_tile(t, 512)
    row = pl.BlockSpec((tr, d), lambda i: (i, 0))
    vec = pl.BlockSpec((1, d), lambda i: (0, 0))
    has_res = resid is not None

    def body(x_ref, w_ref, dy_ref, *rest):
        r_ref = rest[0] if has_res else None
        dx_ref, dw_ref = rest[has_res:has_res + 2]
        xv = x_ref[...]
        dyv = dy_ref[...].astype(F32)
        r = lax.rsqrt(jnp.mean(xv * xv, axis=-1, keepdims=True) + NORM_EPS)
        xhat = xv * r
        dyw = dyv * w_ref[...]
        dx = r * (dyw - xhat * jnp.mean(dyw * xhat, axis=-1, keepdims=True))
        if has_res:
            dx = dx + r_ref[...]
        dx_ref[...] = dx.astype(dx_ref.dtype)

        sums = [(dw_ref, dyv * xhat)] + ([(rest[-1], dx)] if dx_col_sum else [])

        @pl.when(pl.program_id(0) == 0)
        def _():
            for acc_ref, _ in sums:
                acc_ref[...] = jnp.zeros_like(acc_ref)

        for acc_ref, rows in sums:
            acc_ref[...] += jnp.sum(rows, axis=0, keepdims=True)

    ins = (x, w, dy) + ((resid,) if has_res else ())
    in_specs = [row, vec, row] + ([row] if has_res else [])
    n_vec = 2 if dx_col_sum else 1
    return pl.pallas_call(
        body, grid=(t // tr,), in_specs=in_specs, out_specs=[row] + [vec] * n_vec,
        out_shape=[jax.ShapeDtypeStruct((t, d), out_dtype)] + [jax.ShapeDtypeStruct((1, d), F32)] * n_vec,
        compiler_params=_params("arbitrary"), name=name)(*ins)


def _col_sum(x, *, name):
    t, n = x.shape
    tr = _row_tile(t, 512)

    def body(x_ref, o_ref):
        @pl.when(pl.program_id(0) == 0)
        def _():
            o_ref[...] = jnp.zeros_like(o_ref)

        o_ref[...] += jnp.sum(x_ref[...].astype(F32), axis=0, keepdims=True)

    return pl.pallas_call(
        body, grid=(t // tr,), in_specs=[pl.BlockSpec((tr, n), lambda i: (i, 0))],
        out_specs=pl.BlockSpec((1, n), lambda i: (0, 0)), out_shape=jax.ShapeDtypeStruct((1, n), F32),
        compiler_params=_params("arbitrary"), name=name)(x)


SSD_IN_SHARD = SSD_IN_DIM // N_CHIPS


def _w_in_from_shards(shards, *, name):
    d = shards.shape[1]
    tr = 256

    def body(s_ref, o_ref):
        o_ref[:, pl.ds(SSD_DT_COL, SSD_IN_PAD - SSD_DT_COL)] = jnp.zeros((tr, SSD_IN_PAD - SSD_DT_COL), o_ref.dtype)
        for s in range(N_CHIPS):
            o_ref[:, pl.ds(SSD_IN_SHARD * s, SSD_IN_SHARD)] = s_ref[s]

    return pl.pallas_call(
        body, grid=(d // tr,), in_specs=[pl.BlockSpec((N_CHIPS, tr, SSD_IN_SHARD), lambda i: (0, i, 0))],
        out_specs=pl.BlockSpec((tr, SSD_IN_PAD), lambda i: (i, 0)),
        out_shape=jax.ShapeDtypeStruct((d, SSD_IN_PAD), shards.dtype),
        compiler_params=_params("parallel"), name=name)(shards)


def _w_in_to_shards(g, *, name):
    d = g.shape[0]
    tr = 256

    def body(g_ref, o_ref):
        for s in range(N_CHIPS):
            o_ref[s] = g_ref[:, pl.ds(SSD_IN_SHARD * s, SSD_IN_SHARD)].astype(o_ref.dtype)

    return pl.pallas_call(
        body, grid=(d // tr,), in_specs=[pl.BlockSpec((tr, SSD_IN_PAD), lambda i: (i, 0))],
        out_specs=pl.BlockSpec((N_CHIPS, tr, SSD_IN_SHARD), lambda i: (0, i, 0)),
        out_shape=jax.ShapeDtypeStruct((N_CHIPS, d, SSD_IN_SHARD), BF16),
        compiler_params=_params("parallel"), name=name)(g)


XBC_COL0 = SSD_D_INNER // LANES


def _shift_down(v, k, row_ids):
    return jnp.where(row_ids >= k, pltpu.roll(v, k, axis=0), 0.0)


def _shift_up(v, k, row_ids):
    n = v.shape[0]
    return jnp.where(row_ids < n - k, pltpu.roll(v, n - k, axis=0), 0.0)


def _conv_pre(x, w, b, row_ids):
    pre = b + w[3:4, :] * x
    for k in (1, 2, 3):
        pre = pre + w[3 - k:4 - k, :] * _shift_down(x, k, row_ids)
    return pre


def _conv_fwd(zx, conv_w, conv_b, *, name, hook=None):
    t = zx.shape[0]
    nct = SSD_CONV_DIM // LANES
    hk = _HookSlots(hook, n_in=3, n_out=1, n_scratch=0)

    def body(*refs):
        (x_ref, w_ref, b_ref), (o_ref,), _ = hk.own(refs)
        if hook is not None:
            hk.run(refs, pl.program_id(0), nct)
        x = x_ref[...].astype(F32)
        row_ids = lax.broadcasted_iota(jnp.int32, x.shape, 0)
        pre = _conv_pre(x, w_ref[...], b_ref[...], row_ids)
        o_ref[...] = pre * _sigmoid(pre)

    outs = pl.pallas_call(
        body, grid=(nct,),
        in_specs=[pl.BlockSpec((t, LANES), lambda j: (0, XBC_COL0 + j)),
                  pl.BlockSpec((SSD_CONV_WIDTH, LANES), lambda j: (0, j)),
                  pl.BlockSpec((1, LANES), lambda j: (0, j))] + hk.in_specs,
        out_specs=[pl.BlockSpec((t, LANES), lambda j: (0, j))] + hk.out_specs,
        out_shape=[jax.ShapeDtypeStruct((t, SSD_CONV_DIM), F32)] + hk.out_shape,
        scratch_shapes=hk.scratch,
        compiler_params=_params(*hk.semantics("parallel")), name=name)(zx, conv_w, conv_b, *hk.inputs)
    return outs[0] if hook is None else (outs[0], outs[1:])


def _conv_bwd(zx, conv_w, conv_b, d_xs, d_bm, d_cm, dzx, *, name):
    t = zx.shape[0]
    nct = SSD_CONV_DIM // LANES
    n_xs = SSD_D_INNER // LANES
    n_bm = SSD_N_GROUPS * SSD_D_STATE // LANES

    def body(x_ref, w_ref, b_ref, dxs_ref, dbm_ref, dcm_ref, _, dx_ref, dw_ref, db_ref):
        x = x_ref[...].astype(F32)
        w = w_ref[...]
        j = pl.program_id(0)
        dy = jnp.where(j < n_xs, dxs_ref[...], jnp.where(j < n_xs + n_bm, dbm_ref[...], dcm_ref[...]))
        row_ids = lax.broadcasted_iota(jnp.int32, x.shape, 0)
        pre = _conv_pre(x, w, b_ref[...], row_ids)
        sg = _sigmoid(pre)
        dpre = dy * (sg * (1.0 + pre * (1.0 - sg)))
        dx = w[3:4, :] * dpre
        for k in (1, 2, 3):
            dx = dx + w[3 - k:4 - k, :] * _shift_up(dpre, k, row_ids)
        dx_ref[...] = dx.astype(dx_ref.dtype)
        db_ref[...] = jnp.sum(dpre, axis=0, keepdims=True)
        dw_ref[3:4, :] = jnp.sum(dpre * x, axis=0, keepdims=True)
        for k in (1, 2, 3):
            dw_ref[3 - k:4 - k, :] = jnp.sum(dpre * _shift_down(x, k, row_ids), axis=0, keepdims=True)

    clip = lambda j, lo, n: jnp.clip(j - lo, 0, n - 1)
    return pl.pallas_call(
        body, grid=(nct,),
        in_specs=[pl.BlockSpec((t, LANES), lambda j: (0, XBC_COL0 + j)),
                  pl.BlockSpec((SSD_CONV_WIDTH, LANES), lambda j: (0, j)),
                  pl.BlockSpec((1, LANES), lambda j: (0, j)),
                  pl.BlockSpec((t, LANES), lambda j: (0, clip(j, 0, n_xs))),
                  pl.BlockSpec((t, LANES), lambda j: (0, clip(j, n_xs, n_bm))),
                  pl.BlockSpec((t, LANES), lambda j: (0, clip(j, n_xs + n_bm, n_bm))), ANY],
        out_specs=[pl.BlockSpec((t, LANES), lambda j: (0, XBC_COL0 + j)),
                   pl.BlockSpec((SSD_CONV_WIDTH, LANES), lambda j: (0, j)), pl.BlockSpec((1, LANES), lambda j: (0, j))],
        out_shape=[jax.ShapeDtypeStruct(dzx.shape, dzx.dtype),
                   jax.ShapeDtypeStruct((SSD_CONV_WIDTH, SSD_CONV_DIM), F32),
                   jax.ShapeDtypeStruct((1, SSD_CONV_DIM), F32)],
        input_output_aliases={6: 0},
        compiler_params=_params("parallel"), name=name)(zx, conv_w, conv_b, d_xs, d_bm, d_cm, dzx)


def _softplus_fwd(zx, bias_row, alog_row, *, name):
    t = zx.shape[0]
    q = SSD_CHUNK
    tr = _row_tile(t, 1024)

    def body(x_ref, b_ref, al_ref, dt_ref, cum_ref):
        v = x_ref[...] + b_ref[...]
        e = jnp.exp(-jnp.abs(v))
        u = 1.0 + e
        log1p = jnp.where(u == 1.0, e, jnp.log(u) * (e / (u - 1.0)))
        dt = jnp.maximum(v, 0.0) + log1p
        dt_ref[...] = dt
        a = dt * -jnp.exp(al_ref[...])
        lower = (lax.broadcasted_iota(jnp.int32, (q, q), 1) <= lax.broadcasted_iota(jnp.int32, (q, q), 0)).astype(F32)
        cums = [lax.dot_general(lower, a[c * q:(c + 1) * q, :], ((((1,), (0,))), ((), ())), precision=lax.Precision.HIGHEST,
                                preferred_element_type=F32) for c in range(tr // q)]
        cum_ref[...] = jnp.concatenate(cums, axis=0)

    blk = pl.BlockSpec((tr, LANES), lambda i: (i, 0))
    vec = pl.BlockSpec((1, LANES), lambda i: (0, 0))
    return pl.pallas_call(
        body, grid=(t // tr,),
        in_specs=[blk, vec, vec],
        out_specs=[blk, blk],
        out_shape=[jax.ShapeDtypeStruct((t, LANES), F32), jax.ShapeDtypeStruct((t, LANES), F32)],
        compiler_params=_params("parallel"), name=name)(zx, bias_row, alog_row)


def _softplus_bwd(dt_raw, bias_row, ddt, dzx, *, name):
    t = dt_raw.shape[0]
    tr = _row_tile(t, 1024)
    tail = SSD_IN_PAD - SSD_DT_COL

    def body(x_ref, b_ref, g_ref, _, o_ref, db_ref):
        v = x_ref[...] + b_ref[...]
        lane = lax.broadcasted_iota(jnp.int32, v.shape, 1)
        d = jnp.where(lane < SSD_N_HEADS, g_ref[...] * _sigmoid(v), 0.0)
        o_ref[:, pl.ds(0, LANES)] = d.astype(o_ref.dtype)
        o_ref[:, pl.ds(LANES, tail - LANES)] = jnp.zeros((tr, tail - LANES), o_ref.dtype)

        @pl.when(pl.program_id(0) == 0)
        def _():
            db_ref[...] = jnp.zeros_like(db_ref)

        db_ref[...] += jnp.sum(d, axis=0, keepdims=True)

    return pl.pallas_call(
        body, grid=(t // tr,),
        in_specs=[pl.BlockSpec((tr, LANES), lambda i: (i, 0)), pl.BlockSpec((1, LANES), lambda i: (0, 0)),
                  pl.BlockSpec((tr, LANES), lambda i: (i, 0)), ANY],
        out_specs=[pl.BlockSpec((tr, tail), lambda i: (i, SSD_DT_COL // tail)), pl.BlockSpec((1, LANES), lambda i: (0, 0))],
        out_shape=[jax.ShapeDtypeStruct(dzx.shape, dzx.dtype), jax.ShapeDtypeStruct((1, LANES), F32)],
        input_output_aliases={3: 0},
        compiler_params=_params("arbitrary"), name=name)(dt_raw, bias_row, ddt, dzx)


def _ssd_masks():
    q = SSD_CHUNK
    tt = lax.broadcasted_iota(jnp.int32, (q, q), 0)
    ss = lax.broadcasted_iota(jnp.int32, (q, q), 1)
    lane = lax.broadcasted_iota(jnp.int32, (1, SSD_GW), 1)
    srow = lax.broadcasted_iota(jnp.int32, (SSD_GW, 1), 0)
    hm = [(lane >= SSD_HEAD_DIM * j) & (lane < SSD_HEAD_DIM * (j + 1)) for j in range(SSD_HPG)]
    rm = [(srow >= SSD_HEAD_DIM * j) & (srow < SSD_HEAD_DIM * (j + 1)) for j in range(SSD_HPG)]
    return tt, ss, hm, rm


def _ssd_head_terms(dt_rows, cum_rows, a_rows, j, tt, ss):
    q = SSD_CHUNK
    dt_row = dt_rows[j:j + 1, :]
    dt_col = jnp.sum(jnp.where(tt == ss, dt_row, 0.0), axis=1, keepdims=True)
    a_row1 = a_rows[j:j + 1, :]
    a_11 = a_rows[j:j + 1, 0:1]
    cum_col = jnp.sum(jnp.where(ss <= tt, dt_row * a_row1, 0.0), axis=1, keepdims=True)
    cum_row = cum_rows[j:j + 1, :]
    decay = jnp.exp(jnp.where(ss <= tt, cum_col - cum_row, -jnp.inf))
    cum_last = cum_col[q - 1:q, :]
    e_col = jnp.exp(cum_col)
    dte_col = jnp.exp(cum_last - cum_col)
    e_last = jnp.exp(cum_last)
    return dt_col, dt_row, a_row1, a_11, decay, e_col, dte_col, e_last


SSD_CHUNKS_PER_STEP = 4
SSD_BC_COL0 = SSD_D_INNER // SSD_D_STATE


def _ssd_head_selects(terms, hm, rm):
    e_all = jnp.zeros((SSD_CHUNK, SSD_GW), F32)
    w_all = jnp.zeros((SSD_CHUNK, SSD_GW), F32)
    e_s = jnp.zeros((SSD_GW, 1), F32)
    for j in range(SSD_HPG):
        dt_col, _, _, _, _, e_col, dte_col, e_last = terms[j]
        e_all = jnp.where(hm[j], e_col, e_all)
        w_all = jnp.where(hm[j], dt_col * dte_col, w_all)
        e_s = jnp.where(rm[j], e_last, e_s)
    return e_all, w_all, e_s


def _ssd_fwd(xc, dtr, cumr, alog_b, d_b, *, name, hook=None):
    t = xc.shape[0]
    q = SSD_CHUNK
    nc = t // q
    kc = min(SSD_CHUNKS_PER_STEP, nc)
    rows = kc * q
    hk = _HookSlots(hook, n_in=7, n_out=2, n_scratch=1)

    def body(*refs):
        (x_ref, b_ref, c_ref, dtr_ref, cumr_ref, alog_ref, d_ref), (y_ref, st_ref), (s_scr,) = hk.own(refs)
        if hook is not None:
            hk.run(refs, pl.program_id(0) * (nc // kc) + pl.program_id(1), SSD_N_GROUPS * (nc // kc))

        @pl.when(pl.program_id(1) == 0)
        def _():
            s_scr[...] = jnp.zeros_like(s_scr)

        tt, ss, hm, rm = _ssd_masks()
        a_rows = -jnp.exp(alog_ref[...])
        d_rows = d_ref[...]
        d_all = jnp.zeros((1, SSD_GW), F32)
        for j in range(SSD_HPG):
            d_all = jnp.where(hm[j], d_rows[j:j + 1, 0:1], d_all)
        ks, hs = range(kc), range(SSD_HPG)
        sl = [pl.ds(k * q, q) for k in ks]
        x = [x_ref[sl[k], :] for k in ks]
        bm = [b_ref[sl[k], :].astype(BF16) for k in ks]
        cm = [c_ref[sl[k], :].astype(BF16) for k in ks]
        xb = [x[k].astype(BF16) for k in ks]
        terms = [[_ssd_head_terms(dtr_ref[:, sl[k]], cumr_ref[:, sl[k]], a_rows, j, tt, ss) for j in hs] for k in ks]
        g = [_dot_nt(cm[k], bm[k]) for k in ks]
        m = [[(g[k] * terms[k][j][4] * terms[k][j][1]).astype(BF16) for j in hs] for k in ks]
        yj = [[_dot_nn(m[k][j], xb[k]) for j in hs] for k in ks]
        sel = [_ssd_head_selects(terms[k], hm, rm) for k in ks]
        upd = [_dot_tn((x[k] * sel[k][1]).astype(BF16), bm[k]) for k in ks]
        states = [s_scr[...]]
        for k in ks:
            states.append(states[k] * sel[k][2] + upd[k])
        inter = [_dot_nt(cm[k], states[k].astype(BF16)) for k in ks]
        ys = []
        for k in ks:
            y = jnp.zeros((q, SSD_GW), F32)
            for j in hs:
                y = jnp.where(hm[j], yj[k][j], y)
            ys.append(y + inter[k] * sel[k][0] + x[k] * d_all)
        for k in ks:
            st_ref[k] = states[k]
        y_ref[...] = jnp.concatenate(ys, axis=0)
        s_scr[...] = states[kc]

    blk = lambda width, off: pl.BlockSpec((rows, width), lambda g, c: (c, off + g))
    par_s = pl.BlockSpec((None, SSD_HPG, LANES), lambda g, c: (g, 0, 0))
    row_s = pl.BlockSpec((None, SSD_HPG, rows), lambda g, c: (g, 0, c))
    outs = pl.pallas_call(
        body, grid=(SSD_N_GROUPS, nc // kc),
        in_specs=[blk(SSD_GW, 0), blk(SSD_D_STATE, SSD_BC_COL0), blk(SSD_D_STATE, SSD_BC_COL0 + SSD_N_GROUPS),
                  row_s, row_s, par_s, par_s] + hk.in_specs,
        out_specs=[blk(SSD_GW, 0), pl.BlockSpec((None, kc, SSD_GW, SSD_D_STATE), lambda g, c: (g, c, 0, 0))] + hk.out_specs,
        out_shape=[jax.ShapeDtypeStruct((t, SSD_D_INNER), F32),
                   jax.ShapeDtypeStruct((SSD_N_GROUPS, nc, SSD_GW, SSD_D_STATE), F32)] + hk.out_shape,
        scratch_shapes=[pltpu.VMEM((SSD_GW, SSD_D_STATE), F32)] + hk.scratch,
        compiler_params=_params(*hk.semantics("parallel", "arbitrary")), name=name)(
            xc, xc, xc, dtr, cumr, alog_b, d_b, *hk.inputs)
    return outs if hook is None else (outs[:2], outs[2:])


def _ssd_bwd(xc, dtr, cumr, alog_b, d_b, states, dy, *, name, hook=None):
    t = xc.shape[0]
    q = SSD_CHUNK
    nc = t // q
    kc = min(SSD_CHUNKS_PER_STEP, nc)
    nst = nc // kc
    rows = kc * q
    rev = lambda c: nst - 1 - c
    hk = _HookSlots(hook, n_in=9, n_out=5, n_scratch=1)

    def body(*refs):
        ((x_ref, b_ref, c_ref, dtr_ref, cumr_ref, alog_ref, d_ref, st_ref, dy_ref),
         (dx_ref, db_ref, dc_ref, ddt_ref, dpar_ref), (ds_scr,)) = hk.own(refs)
        if hook is not None:
            hk.run(refs, pl.program_id(0) * nst + pl.program_id(1), SSD_N_GROUPS * nst)

        @pl.when(pl.program_id(1) == 0)
        def _():
            ds_scr[...] = jnp.zeros_like(ds_scr)
            dpar_ref[...] = jnp.zeros_like(dpar_ref)

        tt, ss, hm, rm = _ssd_masks()
        tcol = lax.broadcasted_iota(jnp.int32, (q, 1), 0)
        lane = lax.broadcasted_iota(jnp.int32, (1, LANES), 1)
        a_rows = -jnp.exp(alog_ref[...])
        d_rows = d_ref[...]
        d_all = jnp.zeros((1, SSD_GW), F32)
        for j in range(SSD_HPG):
            d_all = jnp.where(hm[j], d_rows[j:j + 1, 0:1], d_all)
        ks, hs = range(kc), range(SSD_HPG)
        sl = [pl.ds(k * q, q) for k in ks]
        x = [x_ref[sl[k], :] for k in ks]
        dyv = [dy_ref[sl[k], :] for k in ks]
        bm = [b_ref[sl[k], :].astype(BF16) for k in ks]
        cm = [c_ref[sl[k], :].astype(BF16) for k in ks]
        s_in = [st_ref[k] for k in ks]
        xb = [x[k].astype(BF16) for k in ks]
        dyb = [dyv[k].astype(BF16) for k in ks]
        s_b = [s_in[k].astype(BF16) for k in ks]
        terms = [[_ssd_head_terms(dtr_ref[:, sl[k]], cumr_ref[:, sl[k]], a_rows, j, tt, ss) for j in hs] for k in ks]
        sel = [_ssd_head_selects(terms[k], hm, rm) for k in ks]
        e_all, w_all, e_s = [s_[0] for s_ in sel], [s_[1] for s_ in sel], [s_[2] for s_ in sel]
        dye = [(dyv[k] * e_all[k]).astype(BF16) for k in ks]
        ds_loc = [_dot_tn(dye[k], cm[k]) for k in ks]
        ds = [None] * kc
        running = ds_scr[...]
        for k in reversed(ks):
            ds[k] = running
            running = running * e_s[k] + ds_loc[k]
        ds_scr[...] = running
        ds_b = [ds[k].astype(BF16) for k in ks]
        g = [_dot_nt(cm[k], bm[k]) for k in ks]
        cs = [_dot_nt(cm[k], s_b[k]) for k in ks]
        bds = [_dot_nt(bm[k], ds_b[k]) for k in ks]
        dm = [[_dot_nt(jnp.where(hm[j], dyv[k], 0.0).astype(BF16), xb[k]) for j in hs] for k in ks]
        gl = [[g[k] * terms[k][j][4] for j in hs] for k in ks]
        wp = [[dm[k][j] * gl[k][j] for j in hs] for k in ks]
        mt = [[(gl[k][j] * terms[k][j][1]).astype(BF16) for j in hs] for k in ks]
        dxj = [[_dot_tn(mt[k][j], dyb[k]) for j in hs] for k in ks]
        dg = []
        for k in ks:
            acc = jnp.zeros((q, q), F32)
            for j in hs:
                acc = acc + dm[k][j] * terms[k][j][4] * terms[k][j][1]
            dg.append(acc.astype(BF16))
        dy_cs = [dyv[k] * cs[k] for k in ks]
        x_bds = [x[k] * bds[k] for k in ks]
        dy_x = [dyv[k] * x[k] for k in ks]
        ds_s = [ds[k] * s_in[k] for k in ks]
        w = [[wp[k][j] * terms[k][j][1] for j in hs] for k in ks]
        rw_col = [[jnp.sum(w[k][j], axis=1, keepdims=True) for j in hs] for k in ks]
        cw_row = [[jnp.sum(w[k][j], axis=0, keepdims=True) for j in hs] for k in ks]
        cwp_row = [[jnp.sum(wp[k][j], axis=0, keepdims=True) for j in hs] for k in ks]
        r1_col = [[jnp.sum(jnp.where(hm[j], dy_cs[k], 0.0), axis=1, keepdims=True) * terms[k][j][5] for j in hs] for k in ks]
        dw_col = [[jnp.sum(jnp.where(hm[j], x_bds[k], 0.0), axis=1, keepdims=True) for j in hs] for k in ks]
        head_rows = [slice(j * SSD_HEAD_DIM, (j + 1) * SSD_HEAD_DIM) for j in hs]
        lane_sum = lambda v: jnp.sum(v, axis=1, keepdims=True)
        s_sum = [[lane_sum(jnp.sum(ds_s[k][head_rows[j], :], axis=0, keepdims=True)) for j in hs] for k in ks]
        dy_x_cols = [jnp.sum(dy_x[k], axis=0, keepdims=True) for k in ks]
        d_d = [[lane_sum(jnp.where(hm[j], dy_x_cols[k], 0.0)) for j in hs] for k in ks]
        ddt_rows = [[None] * SSD_HPG for _ in ks]
        dpar = [jnp.zeros((1, LANES), F32) for _ in hs]
        for k in ks:
            for j in hs:
                dt_col, dt_row, a_row1, a_11, _, _, dte_col, e_last = terms[k][j]
                dww = dw_col[k][j] * (dt_col * dte_col)
                last_add = jnp.sum(dww, axis=0, keepdims=True) + e_last * s_sum[k][j]
                dcum_col = rw_col[k][j] + r1_col[k][j] - dww + jnp.where(tcol == q - 1, last_add, 0.0)
                da_row = jnp.sum(jnp.where(tt >= ss, dcum_col, 0.0), axis=0, keepdims=True)
                da_col = jnp.sum(jnp.where(ss >= tt, -cw_row[k][j], 0.0), axis=1, keepdims=True)
                ddt_col = a_11 * da_col + dw_col[k][j] * dte_col
                ddt_rows[k][j] = (a_row1 * da_row + cwp_row[k][j]
                                  + jnp.sum(jnp.where(tt == ss, ddt_col, 0.0), axis=0, keepdims=True))
                d_a = jnp.sum(dt_row * da_row, axis=1, keepdims=True) + jnp.sum(dt_col * da_col, axis=0, keepdims=True)
                dpar[j] = dpar[j] + jnp.where(lane == 0, d_a * a_11, 0.0) + jnp.where(lane == 1, d_d[k][j], 0.0)
        dxs = []
        for k in ks:
            acc = jnp.zeros((q, SSD_GW), F32)
            for j in hs:
                acc = jnp.where(hm[j], dxj[k][j], acc)
            dxs.append(acc + w_all[k] * bds[k] + d_all * dyv[k])
        xw = [(x[k] * w_all[k]).astype(BF16) for k in ks]
        dc = [_dot_nn(dg[k], bm[k]) + _dot_nn(dye[k], s_b[k]) for k in ks]
        db = [_dot_tn(dg[k], cm[k]) + _dot_nn(xw[k], ds_b[k]) for k in ks]
        dx_ref[...] = jnp.concatenate(dxs, axis=0)
        dc_ref[...] = jnp.concatenate(dc, axis=0)
        db_ref[...] = jnp.concatenate(db, axis=0)
        ddt_ref[...] = jnp.concatenate([jnp.concatenate([ddt_rows[k][j] for k in ks], axis=1) for j in hs], axis=0)
        dpar_ref[...] += jnp.concatenate(dpar, axis=0)

    blk = lambda width, off: pl.BlockSpec((rows, width), lambda g, c: (rev(c), off + g))
    par_s = pl.BlockSpec((None, SSD_HPG, LANES), lambda g, c: (g, 0, 0))
    outs = pl.pallas_call(
        body, grid=(SSD_N_GROUPS, nst),
        in_specs=[blk(SSD_GW, 0), blk(SSD_D_STATE, SSD_BC_COL0), blk(SSD_D_STATE, SSD_BC_COL0 + SSD_N_GROUPS),
                  pl.BlockSpec((None, SSD_HPG, rows), lambda g, c: (g, 0, rev(c))),
                  pl.BlockSpec((None, SSD_HPG, rows), lambda g, c: (g, 0, rev(c))), par_s, par_s,
                  pl.BlockSpec((None, kc, SSD_GW, SSD_D_STATE), lambda g, c: (g, rev(c), 0, 0)), blk(SSD_GW, 0)] + hk.in_specs,
        out_specs=[blk(SSD_GW, 0), blk(SSD_D_STATE, 0), blk(SSD_D_STATE, 0),
                   pl.BlockSpec((None, SSD_HPG, rows), lambda g, c: (g, 0, rev(c))), par_s] + hk.out_specs,
        out_shape=[jax.ShapeDtypeStruct((t, SSD_D_INNER), F32),
                   jax.ShapeDtypeStruct((t, SSD_N_GROUPS * SSD_D_STATE), F32),
                   jax.ShapeDtypeStruct((t, SSD_N_GROUPS * SSD_D_STATE), F32),
                   jax.ShapeDtypeStruct((SSD_N_GROUPS, SSD_HPG, t), F32),
                   jax.ShapeDtypeStruct((SSD_N_GROUPS, SSD_HPG, LANES), F32)] + hk.out_shape,
        scratch_shapes=[pltpu.VMEM((SSD_GW, SSD_D_STATE), F32)] + hk.scratch,
        compiler_params=_params(*hk.semantics("parallel", "arbitrary")), name=name)(
            xc, xc, xc, dtr, cumr, alog_b, d_b, states, dy, *hk.inputs)
    return outs if hook is None else (outs[:5], outs[5:])


def _gate_norm_fwd(y, zx, norm_w, *, name):
    t = y.shape[0]
    tr = _row_tile(t, 256)
    row = pl.BlockSpec((tr, SSD_D_INNER), lambda i: (i, 0))

    def body(y_ref, z_ref, w_ref, o_ref):
        for gi in range(SSD_N_GROUPS):
            sl = pl.ds(gi * SSD_GW, SSD_GW)
            z = z_ref[:, sl].astype(F32)
            gv = y_ref[:, sl] * (z * _sigmoid(z))
            r = lax.rsqrt(jnp.mean(gv * gv, axis=-1, keepdims=True) + NORM_EPS)
            o_ref[:, sl] = (gv * r * w_ref[:, sl]).astype(BF16)

    return pl.pallas_call(
        body, grid=(t // tr,), in_specs=[row, row, pl.BlockSpec((1, SSD_D_INNER), lambda i: (0, 0))],
        out_specs=row, out_shape=jax.ShapeDtypeStruct((t, SSD_D_INNER), BF16),
        compiler_params=_params("parallel"), name=name)(y, zx, norm_w)


def _gate_norm_bwd(y, zx, norm_w, dyn, *, name):
    t = y.shape[0]
    tr = _row_tile(t, 256)
    row = pl.BlockSpec((tr, SSD_D_INNER), lambda i: (i, 0))
    vec = pl.BlockSpec((1, SSD_D_INNER), lambda i: (0, 0))

    def body(y_ref, z_ref, w_ref, dyn_ref, dy_ref, dz_ref, dw_ref):
        @pl.when(pl.program_id(0) == 0)
        def _():
            dw_ref[...] = jnp.zeros_like(dw_ref)

        for gi in range(SSD_N_GROUPS):
            sl = pl.ds(gi * SSD_GW, SSD_GW)
            z = z_ref[:, sl].astype(F32)
            yv = y_ref[:, sl]
            sg = _sigmoid(z)
            sz = z * sg
            gv = yv * sz
            r = lax.rsqrt(jnp.mean(gv * gv, axis=-1, keepdims=True) + NORM_EPS)
            ghat = gv * r
            dout = dyn_ref[:, sl].astype(F32)
            dgh = dout * w_ref[:, sl]
            dgv = r * (dgh - ghat * jnp.mean(dgh * ghat, axis=-1, keepdims=True))
            dy_ref[:, sl] = dgv * sz
            dz_ref[:, sl] = (dgv * yv * (sg * (1.0 + z * (1.0 - sg)))).astype(dz_ref.dtype)
            dw_ref[:, sl] += jnp.sum(dout * ghat, axis=0, keepdims=True)

    return pl.pallas_call(
        body, grid=(t // tr,), in_specs=[row, row, vec, row], out_specs=[row, row, vec],
        out_shape=[jax.ShapeDtypeStruct((t, SSD_D_INNER), F32), jax.ShapeDtypeStruct((t, SSD_IN_PAD), BF16),
                   jax.ShapeDtypeStruct((1, SSD_D_INNER), F32)],
        compiler_params=_params("arbitrary"), name=name)(y, zx, norm_w, dyn)


ATTN_KV_W = ATTN_N_KV * ATTN_HEAD_DIM
ATTN_Q_HALF = 512
ATTN_K_BLK = ATTN_N_Q * ATTN_HEAD_DIM // ATTN_KV_W
ATTN_V_BLK = ATTN_K_BLK + 1


def _attn_valid(first_block):
    w = ATTN_WINDOW
    qpos = lax.broadcasted_iota(jnp.int32, (w, 2 * w), 0) + w
    kpos = lax.broadcasted_iota(jnp.int32, (w, 2 * w), 1)
    rel = qpos - kpos
    return (rel >= 0) & (rel < w) & jnp.logical_not(first_block & (kpos < w))


def _attn_head_views(lo_ref, hi_ref):
    hd = ATTN_HEAD_DIM
    per_half = ATTN_Q_HALF // hd
    return [(lo_ref if h < per_half else hi_ref)[:, pl.ds((h % per_half) * hd, hd)] for h in range(ATTN_N_Q)]


def _attn_block_views(lo_ref, hi_ref, kc_ref, kp_ref, vc_ref, vp_ref):
    hd = ATTN_HEAD_DIM
    kv_cols = [pl.ds(kh * hd, hd) for kh in range(ATTN_N_KV)]
    kb = [jnp.concatenate([kp_ref[:, c], kc_ref[:, c]], axis=0) for c in kv_cols]
    vb = [jnp.concatenate([vp_ref[:, c], vc_ref[:, c]], axis=0) for c in kv_cols]
    return _attn_head_views(lo_ref, hi_ref), kb, vb


def _attn_scores(q, kb, valid):
    scale = ATTN_HEAD_DIM ** -0.5
    return [jnp.where(valid, _dot_nt(q[h], kb[h // ATTN_REP]) * scale, -jnp.inf) for h in range(ATTN_N_Q)]


def _attn_softmax(s, sink):
    heads = range(ATTN_N_Q)
    m = [jnp.maximum(jnp.max(s[h], axis=1, keepdims=True), sink[h]) for h in heads]
    e = [jnp.exp(s[h] - m[h]) for h in heads]
    es = [jnp.exp(sink[h] - m[h]) for h in heads]
    inv = [1.0 / (jnp.sum(e[h], axis=1, keepdims=True) + es[h]) for h in heads]
    return e, es, inv


def _attn_fwd(qkv, sinks_b, *, name, hook=None):
    t = qkv.shape[0]
    w = ATTN_WINDOW
    nb = t // w
    prev = lambda n: jnp.maximum(n - 1, 0)
    hk = _HookSlots(hook, n_in=7, n_out=1, n_scratch=0)

    def body(*refs):
        (qlo_ref, qhi_ref, kc_ref, kp_ref, vc_ref, vp_ref, sink_ref), (o_ref,), _ = hk.own(refs)
        if hook is not None:
            hk.run(refs, pl.program_id(0), nb)
        heads = range(ATTN_N_Q)
        q, kb, vb = _attn_block_views(qlo_ref, qhi_ref, kc_ref, kp_ref, vc_ref, vp_ref)
        sink = [sink_ref[h:h + 1, 0:1] for h in heads]
        e, _, inv = _attn_softmax(_attn_scores(q, kb, _attn_valid(pl.program_id(0) == 0)), sink)
        out = [_dot_nn((e[h] * inv[h]).astype(BF16), vb[h // ATTN_REP]).astype(o_ref.dtype) for h in heads]
        o_ref[...] = jnp.concatenate(out, axis=1)

    qh = lambda half: pl.BlockSpec((w, ATTN_Q_HALF), lambda n: (n, half))
    kv = lambda blk, idx: pl.BlockSpec((w, ATTN_KV_W), lambda n: (idx(n), blk))
    cur = lambda n: n
    outs = pl.pallas_call(
        body, grid=(nb,),
        in_specs=[qh(0), qh(1), kv(ATTN_K_BLK, cur), kv(ATTN_K_BLK, prev), kv(ATTN_V_BLK, cur), kv(ATTN_V_BLK, prev),
                  pl.BlockSpec((ATTN_N_Q, LANES), lambda n: (0, 0))] + hk.in_specs,
        out_specs=[pl.BlockSpec((w, D_MODEL), lambda n: (n, 0))] + hk.out_specs,
        out_shape=[jax.ShapeDtypeStruct((t, D_MODEL), BF16)] + hk.out_shape,
        scratch_shapes=hk.scratch,
        compiler_params=_params(*hk.semantics("parallel")), name=name)(qkv, qkv, qkv, qkv, qkv, qkv, sinks_b, *hk.inputs)
    return outs[0] if hook is None else (outs[0], outs[1:])


def _attn_bwd(qkv, sinks_b, dout, *, name):
    t = qkv.shape[0]
    w = ATTN_WINDOW
    nb = t // w
    hd = ATTN_HEAD_DIM
    clamp = lambda n: jnp.minimum(n, nb - 1)
    prev = lambda n: jnp.maximum(clamp(n) - 1, 0)

    def body(qlo_ref, qhi_ref, kc_ref, kp_ref, vc_ref, vp_ref, sink_ref, dolo_ref, dohi_ref,
             dq_ref, dkv_ref, dsink_ref, carry):
        n = pl.program_id(0)

        @pl.when(n == 0)
        def _():
            carry[...] = jnp.zeros_like(carry)
            dsink_ref[...] = jnp.zeros_like(dsink_ref)

        @pl.when(n < nb)
        def _():
            heads, kvs = range(ATTN_N_Q), range(ATTN_N_KV)
            q, kb, vb = _attn_block_views(qlo_ref, qhi_ref, kc_ref, kp_ref, vc_ref, vp_ref)
            do = _attn_head_views(dolo_ref, dohi_ref)
            sink = [sink_ref[h:h + 1, 0:1] for h in heads]
            s = _attn_scores(q, kb, _attn_valid(n == 0))
            dp = [_dot_nt(do[h], vb[h // ATTN_REP]) for h in heads]
            e, es, inv = _attn_softmax(s, sink)
            p = [e[h] * inv[h] for h in heads]
            delta = [jnp.sum(p[h] * dp[h], axis=1, keepdims=True) for h in heads]
            dsc = [(p[h] * (dp[h] - delta[h]) * (hd ** -0.5)).astype(BF16) for h in heads]
            pb = [p[h].astype(BF16) for h in heads]
            dq = [_dot_nn(dsc[h], kb[h // ATTN_REP]).astype(dq_ref.dtype) for h in heads]
            stack = lambda per_head, kh: jnp.concatenate(per_head[kh * ATTN_REP:(kh + 1) * ATTN_REP], axis=0)
            dkb = [_dot_tn(stack(dsc, kh), stack(q, kh)) for kh in kvs]
            dvb = [_dot_tn(stack(pb, kh), stack(do, kh)) for kh in kvs]
            dsink = [jnp.broadcast_to(jnp.sum(-es[h] * inv[h] * delta[h], axis=0, keepdims=True), (1, LANES)) for h in heads]
            dq_ref[...] = jnp.concatenate(dq, axis=1)
            dsink_ref[...] += jnp.concatenate(dsink, axis=0)
            dkv_ref[...] = (carry[...] + jnp.concatenate([d[0:w, :] for d in dkb + dvb], axis=1)).astype(dkv_ref.dtype)
            carry[...] = jnp.concatenate([d[w:2 * w, :] for d in dkb + dvb], axis=1)

        @pl.when(n == nb)
        def _():
            dkv_ref[...] = carry[...].astype(dkv_ref.dtype)

    qh = lambda half: pl.BlockSpec((w, ATTN_Q_HALF), lambda n: (clamp(n), half))
    kv = lambda blk, idx: pl.BlockSpec((w, ATTN_KV_W), lambda n: (idx(n), blk))
    return pl.pallas_call(
        body, grid=(nb + 1,),
        in_specs=[qh(0), qh(1), kv(ATTN_K_BLK, clamp), kv(ATTN_K_BLK, prev), kv(ATTN_V_BLK, clamp), kv(ATTN_V_BLK, prev),
                  pl.BlockSpec((ATTN_N_Q, LANES), lambda n: (0, 0)), qh(0), qh(1)],
        out_specs=[pl.BlockSpec((w, D_MODEL), lambda n: (clamp(n), 0)),
                   pl.BlockSpec((w, 2 * ATTN_KV_W), lambda n: (jnp.maximum(n - 1, 0), 0)),
                   pl.BlockSpec((ATTN_N_Q, LANES), lambda n: (0, 0))],
        out_shape=[jax.ShapeDtypeStruct((t, D_MODEL), BF16), jax.ShapeDtypeStruct((t, 2 * ATTN_KV_W), BF16),
                   jax.ShapeDtypeStruct((ATTN_N_Q, LANES), F32)],
        scratch_shapes=[pltpu.VMEM((w, 2 * ATTN_KV_W), F32)],
        compiler_params=_params("arbitrary"), name=name)(qkv, qkv, qkv, qkv, qkv, qkv, sinks_b, dout, dout)


def _sq_relu_epilogue(acc):
    r = jnp.maximum(acc, 0.0)
    return (r * r,)


def _sq_relu_bwd_epilogue(acc, act):
    return (acc * (2.0 * jnp.sqrt(act.astype(F32))),)


def _bias_epilogue(acc, bias):
    return (acc + bias,)


def _plain_run(stage, fn, *args, **kwargs):
    return fn(*args, **kwargs)


def _mlp_fwd(u, w_up, w_down, tag, run=_plain_run):
    act = run(f"mlp_up_{tag}", _matmul, u, w_up, mode="nn", out_dtypes=(BF16,), epilogue=_sq_relu_epilogue, b_shards=True,
              name=f"mlp_up_{tag}")
    f = run(f"mlp_down_{tag}", _matmul, act, w_down, mode="nn", out_dtypes=(F32,), name=f"mlp_down_{tag}")
    return act, f


def _mlp_bwd(u, act, w_up, w_down, df, tag):
    dpre = _matmul(df, w_down, mode="nt", out_dtypes=(BF16,), epilogue=_sq_relu_bwd_epilogue,
                   extras=((act, "tile"),), name=f"mlp_dact_{tag}")
    dw_down = _matmul(act, df, mode="tn", out_dtypes=(BF16,), name=f"mlp_dwdown_{tag}")
    du = _matmul(dpre, w_up, mode="nt", out_dtypes=(F32,), b_shards=True, name=f"mlp_du_{tag}")
    dw_up = _matmul(u, dpre, mode="tn", out_dtypes=(BF16,), out_shards=True, name=f"mlp_dwup_{tag}")
    return du, dw_up, dw_down


def _group_rows(dt):
    t = dt.shape[0]
    return jnp.transpose(dt[:, :SSD_N_HEADS].reshape(t, SSD_N_GROUPS, SSD_HPG), (1, 2, 0))


def _head_param_rows(p):
    return jnp.broadcast_to(p.reshape(SSD_N_GROUPS, SSD_HPG, 1), (SSD_N_GROUPS, SSD_HPG, LANES))


def _local_step(x, target, wts, comm=None):
    t = x.shape[0]
    wts = dict(wts)
    row = lambda v: v.reshape(1, -1)
    mix_pre, mix_post, ffn_pre, ffn_post = wts["mix_pre_norm"], wts["mix_post_norm"], wts["ffn_pre_norm"], wts["ffn_post_norm"]

    def gathering(stage, fn, *args, **kwargs):
        hook = comm.gather_hook(stage) if comm is not None else None
        if hook is None:
            return fn(*args, **kwargs)
        out, got = fn(*args, hook=hook, **kwargs)
        wts.update(comm.weights_from(stage, got))
        return out

    u0 = _rms_fwd(x, row(mix_pre[0]), name="rms_pre_mix0")
    zx, dt_raw = gathering("in_proj", _matmul, u0, wts["ssd_w_in"], mode="nn", out_dtypes=(BF16,), tn=SSD_IN_TILE,
                           f32_block=SSD_DT_COL - (SSD_IN_PAD - SSD_IN_TILE),
                           name="ssd_in_proj")
    xc = gathering("conv", _conv_fwd, zx, wts["ssd_conv_w"], row(wts["ssd_conv_b"]), name="ssd_conv_fwd")
    bias_row = jnp.pad(wts["ssd_dt_bias"], (0, LANES - SSD_N_HEADS)).reshape(1, LANES)
    alog_row = jnp.pad(wts["ssd_a_log"], (0, LANES - SSD_N_HEADS)).reshape(1, LANES)
    dt, cum = _softplus_fwd(dt_raw, bias_row, alog_row, name="ssd_dt_fwd")
    dtr, cumr = _group_rows(dt), _group_rows(cum)
    alog_b, d_b = _head_param_rows(wts["ssd_a_log"]), _head_param_rows(wts["ssd_d"])
    y_ssd, states = gathering("scan", _ssd_fwd, xc, dtr, cumr, alog_b, d_b, name="ssd_scan_fwd")
    norm_w = row(wts["ssd_norm_w"])
    yn = _gate_norm_fwd(y_ssd, zx, norm_w, name="ssd_gate_norm_fwd")
    mix0 = _matmul(yn, wts["ssd_w_out"], mode="nn", out_dtypes=(F32,), name="ssd_out_proj")
    h1, v0 = _rms_fwd(mix0, row(mix_post[0]), resid=x, want_u=row(ffn_pre[0]), name="rms_post_mix0")
    act0, f0 = _mlp_fwd(v0, wts["mlp_w_up0"], wts["mlp_w_down0"], "l0", run=gathering)
    h2, u1 = _rms_fwd(f0, row(ffn_post[0]), resid=h1, want_u=row(mix_pre[1]), name="rms_post_ffn0")

    qkv = _matmul(u1, wts["attn_w_qkv"], mode="nn", out_dtypes=(BF16,), epilogue=_bias_epilogue,
                  extras=((row(wts["attn_b_qkv"]), "row"),), b_shards=True, name="attn_qkv_proj")
    sinks_b = jnp.broadcast_to(wts["attn_sinks"].reshape(ATTN_N_Q, 1), (ATTN_N_Q, LANES))
    ao = gathering("attn_fwd", _attn_fwd, qkv, sinks_b, name="attn_fwd")
    mix1 = _matmul(ao, wts["attn_w_o"], mode="nn", out_dtypes=(F32,), epilogue=_bias_epilogue,
                   extras=((row(wts["attn_b_o"]), "row"),), name="attn_out_proj")
    h3, v1 = _rms_fwd(mix1, row(mix_post[1]), resid=h2, want_u=row(ffn_pre[1]), name="rms_post_mix1")
    act1, f1 = _mlp_fwd(v1, wts["mlp_w_up1"], wts["mlp_w_down1"], "l1")
    dh4, loss_tile = _rms_fwd(f1, row(ffn_post[1]), resid=h3, target=target, name="rms_post_ffn1_loss")

    df1, g_ffn_post1 = _rms_bwd(f1, row(ffn_post[1]), dh4, out_dtype=BF16, name="rms_post_ffn1_bwd")
    dv1, g_up1, g_down1 = _mlp_bwd(v1, act1, wts["mlp_w_up1"], wts["mlp_w_down1"], df1, "l1")
    dh3, g_ffn_pre1 = _rms_bwd(h3, row(ffn_pre[1]), dv1, resid=dh4, name="rms_pre_ffn1_bwd")
    dmix1, g_mix_post1, g_b_o = _rms_bwd(mix1, row(mix_post[1]), dh3, out_dtype=BF16, dx_col_sum=True, name="rms_post_mix1_bwd")
    g_w_o = _matmul(ao, dmix1, mode="tn", out_dtypes=(BF16,), name="attn_dwo")
    dao = _matmul(dmix1, wts["attn_w_o"], mode="nt", out_dtypes=(BF16,), name="attn_dao")
    dq, dkv, g_sinks = _attn_bwd(qkv, sinks_b, dao, name="attn_bwd")
    dqkv = jnp.concatenate([dq, dkv], axis=1)
    g_b_qkv = _col_sum(dqkv, name="attn_bqkv_grad")
    g_w_qkv = _matmul(u1, dqkv, mode="tn", out_dtypes=(BF16,), tn=ATTN_QKV // N_CHIPS, out_shards=True, name="attn_dwqkv")
    du1 = _matmul(dqkv, wts["attn_w_qkv"], mode="nt", out_dtypes=(F32,), b_shards=True, name="attn_du")
    dh2, g_mix_pre1 = _rms_bwd(h2, row(mix_pre[1]), du1, resid=dh3, name="rms_pre_mix1_bwd")

    df0, g_ffn_post0 = _rms_bwd(f0, row(ffn_post[0]), dh2, out_dtype=BF16, name="rms_post_ffn0_bwd")
    dv0, g_up0, g_down0 = _mlp_bwd(v0, act0, wts["mlp_w_up0"], wts["mlp_w_down0"], df0, "l0")
    dh1, g_ffn_pre0 = _rms_bwd(h1, row(ffn_pre[0]), dv0, resid=dh2, name="rms_pre_ffn0_bwd")
    dmix0, g_mix_post0 = _rms_bwd(mix0, row(mix_post[0]), dh1, out_dtype=BF16, name="rms_post_mix0_bwd")
    g_w_out = _matmul(yn, dmix0, mode="tn", out_dtypes=(BF16,), name="ssd_dwout")
    dyn = _matmul(dmix0, wts["ssd_w_out"], mode="nt", out_dtypes=(BF16,), name="ssd_dyn")
    dy_ssd, dzx, g_norm_w = _gate_norm_bwd(y_ssd, zx, norm_w, dyn, name="ssd_gate_norm_bwd")
    mats = {"ssd_w_out": g_w_out, "attn_w_qkv": g_w_qkv, "attn_w_o": g_w_o,
            "mlp_w_up0": g_up0, "mlp_w_up1": g_up1, "mlp_w_down0": g_down0, "mlp_w_down1": g_down1}
    if comm is None:
        dxc, dbm, dcm, ddt_r, dpar = _ssd_bwd(xc, dtr, cumr, alog_b, d_b, states, dy_ssd, name="ssd_scan_bwd")
    else:
        (dxc, dbm, dcm, ddt_r, dpar), received = _ssd_bwd(xc, dtr, cumr, alog_b, d_b, states, dy_ssd,
                                                          name="ssd_scan_bwd", hook=comm.exchange_hook(mats, "early"))
        comm.received(received)
    dzx, g_conv_w, g_conv_b = _conv_bwd(zx, wts["ssd_conv_w"], row(wts["ssd_conv_b"]), dxc, dbm, dcm, dzx, name="ssd_conv_bwd")
    ddt = jnp.pad(jnp.transpose(ddt_r, (2, 0, 1)).reshape(t, SSD_N_HEADS), ((0, 0), (0, LANES - SSD_N_HEADS)))
    dzx, g_dt_bias = _softplus_bwd(dt_raw, bias_row, ddt, dzx, name="ssd_dt_bwd")
    g_w_in = _w_in_to_shards(_matmul(u0, dzx, mode="tn", out_dtypes=(F32,), tn=SSD_IN_TILE, name="ssd_dwin"), name="ssd_dwin_shards")
    mats["ssd_w_in"] = g_w_in
    if comm is None:
        du0 = _matmul(dzx, wts["ssd_w_in"], mode="nt", out_dtypes=(F32,), tk=SSD_IN_TILE, name="ssd_du")
    else:
        du0, received = _matmul(dzx, wts["ssd_w_in"], mode="nt", out_dtypes=(F32,), tk=SSD_IN_TILE, name="ssd_du",
                                hook=comm.exchange_hook(mats, "late"))
        comm.received(received)
    grad_x, g_mix_pre0 = _rms_bwd(x, row(mix_pre[0]), du0, resid=dh1, name="rms_pre_mix0_bwd")

    dpar = dpar.reshape(SSD_N_HEADS, LANES)
    vecs = {
        "ssd_conv_w": g_conv_w, "ssd_conv_b": g_conv_b.reshape(-1),
        "ssd_dt_bias": g_dt_bias[0, :SSD_N_HEADS], "ssd_a_log": dpar[:, 0], "ssd_d": dpar[:, 1],
        "ssd_norm_w": g_norm_w.reshape(-1), "attn_b_qkv": g_b_qkv.reshape(-1), "attn_sinks": g_sinks[:, 0],
        "attn_b_o": g_b_o.reshape(-1),
        "mix_pre_norm": jnp.concatenate([g_mix_pre0, g_mix_pre1]), "mix_post_norm": jnp.concatenate([g_mix_post0, g_mix_post1]),
        "ffn_pre_norm": jnp.concatenate([g_ffn_pre0, g_ffn_pre1]), "ffn_post_norm": jnp.concatenate([g_ffn_post0, g_ffn_post1]),
    }
    return loss_tile, grad_x, mats, vecs


def _mesh_position():
    return lax.axis_index("x"), lax.axis_index("y"), lax.axis_index("c")


def _flip(v, bit):
    return 1 - v if bit else v


OTHER_CHIPS = ((1, 0), (0, 1), (1, 1))


def _comm_params():
    return pltpu.CompilerParams(vmem_limit_bytes=VMEM_LIMIT)


def _staged_copies(srcs, dsts, bufs, sems_in, sems_out):
    loads = [pltpu.make_async_copy(s, b, sems_in.at[i]) for i, (s, b) in enumerate(zip(srcs, bufs))]
    stores = [pltpu.make_async_copy(b, d, sems_out.at[i]) for i, (b, d) in enumerate(zip(bufs, dsts))]
    return loads, stores


class _GatherHook:
    def __init__(self, mats, vecs=()):
        self.arrs = list(mats) + list(vecs)
        self.nm, self.n = len(mats), len(self.arrs)
        n_ici, n_fwd = (N_CHIPS - 1) * self.n, max((N_CHIPS - 1) * self.nm, 1)
        dma = pltpu.SemaphoreType.DMA
        self.out_shape = [jax.ShapeDtypeStruct((N_CHIPS,) + a.shape, a.dtype) for a in self.arrs]
        self.scratch = [pltpu.VMEM(a.shape, a.dtype) for a in self.arrs] + [
            dma((n_ici,)), dma((n_ici,)), dma((n_fwd,)), dma((n_fwd,)), dma((self.n,)), dma((self.n,))]

    def plan(self, ins, outs, scratch):
        n, nm = self.n, self.nm
        bufs = scratch[:n]
        ici_send, ici_recv, fwd_send, fwd_recv, load_sems, store_sems = scratch[n:]
        xi, yi, ci = _mesh_position()
        me = 2 * xi + yi
        loads, stores = _staged_copies(ins, [outs[i].at[me] for i in range(n)], bufs, load_sems, store_sems)
        sends, landed, forwards, from_sibling = [], [], [], []
        for j, (bx, by) in enumerate(OTHER_CHIPS):
            px, py = _flip(xi, bx), _flip(yi, by)
            peer = 2 * px + py
            for i in range(n):
                k = j * n + i
                mk = functools.partial(pltpu.make_async_remote_copy, send_sem=ici_send.at[k], recv_sem=ici_recv.at[k],
                                       device_id=(px, py, ci), device_id_type=MESH)
                if i < nm:
                    sends.append(mk(src_ref=ins[i].at[ci], dst_ref=outs[i].at[me, ci]))
                    landed.append(mk(src_ref=ins[i].at[ci], dst_ref=outs[i].at[peer, ci]))
                    kf = j * nm + i
                    fw = functools.partial(pltpu.make_async_remote_copy, send_sem=fwd_send.at[kf], recv_sem=fwd_recv.at[kf],
                                           device_id=(xi, yi, 1 - ci), device_id_type=MESH)
                    forwards.append(fw(src_ref=outs[i].at[peer, ci], dst_ref=outs[i].at[peer, ci]))
                    from_sibling.append(fw(src_ref=outs[i].at[peer, ci], dst_ref=outs[i].at[peer, 1 - ci]))
                else:
                    sends.append(mk(src_ref=ins[i], dst_ref=outs[i].at[me]))
                    landed.append(mk(src_ref=ins[i], dst_ref=outs[i].at[peer]))
                    forwards.append(None)
        return loads, stores, sends, landed, forwards, from_sibling

    @staticmethod
    def start(p):
        loads, _, sends, _, _, _ = p
        for cp in loads + sends:
            cp.start()

    @staticmethod
    def relay(p):
        loads, stores, _, landed, forwards, _ = p
        for ld, st in zip(loads, stores):
            ld.wait()
            st.start()
        for cp, fw in zip(landed, forwards):
            cp.wait_recv()
            if fw is not None:
                fw.start()

    @staticmethod
    def finish(p):
        _, stores, sends, _, forwards, from_sibling = p
        for cp in from_sibling:
            cp.wait_recv()
        for cp in sends + [fw for fw in forwards if fw is not None]:
            cp.wait_send()
        for st in stores:
            st.wait()


def _run_hook(hook, ins, outs, scratch, step, n_steps):
    p = hook.plan(ins, outs, scratch)
    relay_step = min(max(1, (3 * n_steps) // 4), n_steps - 1)

    @pl.when(step == 0)
    def _():
        hook.start(p)

    if relay_step < n_steps - 1:
        @pl.when(step == relay_step)
        def _():
            hook.relay(p)

    @pl.when(step == n_steps - 1)
    def _():
        if relay_step == n_steps - 1:
            hook.relay(p)
        hook.finish(p)


def _hook_call(hook, *, name):
    n = len(hook.arrs)

    def body(*refs):
        p = hook.plan(refs[:n], refs[n:n + len(hook.out_shape)], refs[n + len(hook.out_shape):])
        hook.start(p)
        hook.relay(p)
        hook.finish(p)

    return pl.pallas_call(
        body, in_specs=[ANY] * n, out_specs=[ANY] * len(hook.out_shape), out_shape=hook.out_shape,
        scratch_shapes=hook.scratch, compiler_params=_comm_params(), name=name)(*hook.arrs)


def _send_other_half(parts, *, name):
    n = len(parts)

    def body(*refs):
        ins, outs = refs[:n], refs[n:2 * n]
        send_sems, recv_sems = refs[2 * n:]
        xi, yi, ci = _mesh_position()
        sibling = (xi, yi, 1 - ci)
        for i in range(n):
            for s in range(N_CHIPS):
                pltpu.make_async_remote_copy(src_ref=ins[i].at[s, 1 - ci], dst_ref=outs[i].at[s], send_sem=send_sems.at[i],
                                             recv_sem=recv_sems.at[i], device_id=sibling, device_id_type=MESH).start()
        for i in range(n):
            pltpu.make_async_remote_copy(src_ref=outs[i], dst_ref=outs[i], send_sem=send_sems.at[i], recv_sem=recv_sems.at[i],
                                         device_id=sibling, device_id_type=MESH).wait()

    return pl.pallas_call(
        body, in_specs=[ANY] * n, out_specs=[ANY] * n,
        out_shape=[jax.ShapeDtypeStruct((p.shape[0],) + p.shape[2:], p.dtype) for p in parts],
        scratch_shapes=[pltpu.SemaphoreType.DMA((n,)), pltpu.SemaphoreType.DMA((n,))],
        name=name)(*parts)


ROW_BLOCKS = 8


def _add_sibling_half(parts, theirs, core, *, name):
    n = len(parts)

    def body(core_ref, *refs):
        for a_ref, b_ref, o_ref in zip(refs[:n], refs[n:2 * n], refs[2 * n:]):
            o_ref[...] = (a_ref[...].astype(F32) + b_ref[...].astype(F32)).astype(o_ref.dtype)

    mine = lambda p: pl.BlockSpec((None, None, p.shape[2] // ROW_BLOCKS, p.shape[3]), lambda s, rb, core_ref: (s, core_ref[0], rb, 0))
    other = lambda p: pl.BlockSpec((None, p.shape[1] // ROW_BLOCKS, p.shape[2]), lambda s, rb, core_ref: (s, rb, 0))
    return pl.pallas_call(
        body,
        grid_spec=pltpu.PrefetchScalarGridSpec(
            num_scalar_prefetch=1, grid=(N_CHIPS, ROW_BLOCKS),
            in_specs=[mine(p) for p in parts] + [other(q) for q in theirs], out_specs=[other(q) for q in theirs]),
        out_shape=[jax.ShapeDtypeStruct(q.shape, BF16) for q in theirs],
        compiler_params=_params("parallel", "parallel"), name=name)(core, *parts, *theirs)


class _ExchangeHook:
    def __init__(self, parts, to_all=()):
        self.arrs = list(parts) + list(to_all)
        self.n_parts, self.n = len(parts), len(self.arrs)
        n_ici, n_peer = max((N_CHIPS - 1) * self.n_parts, 1), (N_DEV - 1) * max(len(to_all), 1)
        dma = pltpu.SemaphoreType.DMA
        self.out_shape = [jax.ShapeDtypeStruct(p.shape, p.dtype) for p in parts] + [
            jax.ShapeDtypeStruct((N_DEV,) + a.shape, a.dtype) for a in to_all]
        self.scratch = [pltpu.VMEM(p.shape[1:], p.dtype) for p in parts] + [pltpu.VMEM(a.shape, a.dtype) for a in to_all] + [
            dma((n_ici,)), dma((n_ici,)), dma((n_peer,)), dma((n_peer,)), dma((self.n,)), dma((self.n,))]

    def plan(self, ins, outs, scratch):
        n, npt = self.n, self.n_parts
        bufs = scratch[:n]
        send_sems, recv_sems, all_send, all_recv, load_sems, store_sems = scratch[n:]
        xi, yi, ci = _mesh_position()
        me_chip = 2 * xi + yi
        me = 4 * xi + 2 * yi + ci
        loads, stores = _staged_copies([ins[i].at[me_chip] for i in range(npt)] + list(ins[npt:]),
                                       [outs[i].at[me_chip] for i in range(npt)] + [outs[i].at[me] for i in range(npt, n)],
                                       bufs, load_sems, store_sems)
        sends, recvs = [], []
        for j, (bx, by) in enumerate(OTHER_CHIPS):
            px, py = _flip(xi, bx), _flip(yi, by)
            peer = 2 * px + py
            for i in range(npt):
                k = j * npt + i
                mk = functools.partial(pltpu.make_async_remote_copy, src_ref=ins[i].at[peer], send_sem=send_sems.at[k],
                                       recv_sem=recv_sems.at[k], device_id=(px, py, ci), device_id_type=MESH)
                sends.append(mk(dst_ref=outs[i].at[me_chip]))
                recvs.append(mk(dst_ref=outs[i].at[peer]))
        for i in range(npt, n):
            for k in range(1, N_DEV):
                px, py, pc = _flip(xi, (k >> 2) & 1), _flip(yi, (k >> 1) & 1), _flip(ci, k & 1)
                slot = (i - npt) * (N_DEV - 1) + k - 1
                mk = functools.partial(pltpu.make_async_remote_copy, src_ref=ins[i], send_sem=all_send.at[slot],
                                       recv_sem=all_recv.at[slot], device_id=(px, py, pc), device_id_type=MESH)
                sends.append(mk(dst_ref=outs[i].at[me]))
                recvs.append(mk(dst_ref=outs[i].at[4 * px + 2 * py + pc]))
        return loads, stores, sends, recvs

    @staticmethod
    def start(p):
        loads, _, sends, _ = p
        for cp in loads + sends:
            cp.start()

    @staticmethod
    def relay(p):
        loads, stores, _, _ = p
        for ld, st in zip(loads, stores):
            ld.wait()
            st.start()

    @staticmethod
    def finish(p):
        _, stores, sends, recvs = p
        for cp in recvs:
            cp.wait_recv()
        for cp in sends:
            cp.wait_send()
        for st in stores:
            st.wait()


def _sum_chips(parts, *, name):
    n = len(parts)
    p = parts[0].shape[0]

    def body(*refs):
        s = pl.program_id(1)
        for x_ref, o_ref in zip(refs[:n], refs[n:]):
            @pl.when(s == 0)
            def _():
                o_ref[...] = x_ref[...].astype(F32)

            @pl.when(s > 0)
            def _():
                o_ref[...] += x_ref[...].astype(F32)

    blocks = lambda q: ROW_BLOCKS if q.shape[1] % (8 * ROW_BLOCKS) == 0 else 1
    assert len({blocks(q) for q in parts}) == 1
    nb = blocks(parts[0])
    return pl.pallas_call(
        body, grid=(nb, p),
        in_specs=[pl.BlockSpec((None, q.shape[1] // nb, q.shape[2]), lambda rb, s: (s, rb, 0)) for q in parts],
        out_specs=[pl.BlockSpec((q.shape[1] // nb, q.shape[2]), lambda rb, s: (rb, 0)) for q in parts],
        out_shape=[jax.ShapeDtypeStruct(q.shape[1:], F32) for q in parts],
        compiler_params=_params("parallel", "arbitrary"), name=name)(*parts)


def _swap_halves(halves, layers, *, name):
    n = len(halves)
    out_shapes, slots = [], []
    for i, h in enumerate(halves):
        pair = [p for p in layers if i in p]
        if pair and pair[0][1] == i:
            slots.append((slots[pair[0][0]][0], 1))
        elif pair:
            out_shapes.append(jax.ShapeDtypeStruct((2, 2) + h.shape, h.dtype))
            slots.append((len(out_shapes) - 1, 0))
        else:
            out_shapes.append(jax.ShapeDtypeStruct((2,) + h.shape, h.dtype))
            slots.append((len(out_shapes) - 1, None))
    n_out = len(out_shapes)

    def body(*refs):
        ins, outs, bufs = refs[:n], refs[n:n + n_out], refs[n + n_out:2 * n + n_out]
        send_sems, recv_sems, load_sems, store_sems = refs[2 * n + n_out:]
        xi, yi, ci = _mesh_position()
        own, sends, recvs = [], [], []
        for i in range(n):
            o, layer = slots[i]
            dst = (lambda core: outs[o].at[core]) if layer is None else (lambda core: outs[o].at[layer, core])
            own.append(dst(ci))
            mk = functools.partial(pltpu.make_async_remote_copy, src_ref=ins[i], send_sem=send_sems.at[i],
                                   recv_sem=recv_sems.at[i], device_id=(xi, yi, 1 - ci), device_id_type=MESH)
            sends.append(mk(dst_ref=dst(ci)))
            recvs.append(mk(dst_ref=dst(1 - ci)))
        loads, stores = _staged_copies(ins, own, bufs, load_sems, store_sems)
        for cp in loads + sends:
            cp.start()
        for ld, st in zip(loads, stores):
            ld.wait()
            st.start()
        for cp in recvs:
            cp.wait_recv()
        for cp in sends:
            cp.wait_send()
        for st in stores:
            st.wait()

    return pl.pallas_call(
        body, in_specs=[ANY] * n, out_specs=[ANY] * n_out, out_shape=out_shapes,
        scratch_shapes=[pltpu.VMEM(h.shape, h.dtype) for h in halves]
        + [pltpu.SemaphoreType.DMA((n,)), pltpu.SemaphoreType.DMA((n,)), pltpu.SemaphoreType.DMA((n,)), pltpu.SemaphoreType.DMA((n,))],
        compiler_params=_comm_params(), name=name)(*halves)


def _cast_bf16(layers, *, name, hook=None):
    n = len(layers)
    hk = _HookSlots(hook, n_in=n, n_out=n, n_scratch=0)

    def body(*refs):
        ins, outs, _ = hk.own(refs)
        if hook is not None:
            hk.run(refs, pl.program_id(0), ROW_BLOCKS)
        for i_ref, o_ref in zip(ins, outs):
            o_ref[...] = i_ref[...].astype(o_ref.dtype)

    in_blk = lambda a, l: pl.BlockSpec((None, a.shape[1] // ROW_BLOCKS, a.shape[2]), lambda i: (l, i, 0))
    out_blk = lambda a: pl.BlockSpec((a.shape[1] // ROW_BLOCKS, a.shape[2]), lambda i: (i, 0))
    outs = pl.pallas_call(
        body, grid=(ROW_BLOCKS,),
        in_specs=[in_blk(a, l) for a, l in layers] + hk.in_specs,
        out_specs=[out_blk(a) for a, _ in layers] + hk.out_specs,
        out_shape=[jax.ShapeDtypeStruct(a.shape[1:], BF16) for a, _ in layers] + hk.out_shape,
        scratch_shapes=hk.scratch,
        compiler_params=_params(*hk.semantics("parallel")), name=name)(*[a for a, _ in layers], *hk.inputs)
    return outs[:n] if hook is None else (outs[:n], outs[n:])


def _full_weight(name, gathered):
    s, _, r, c = gathered.shape
    if name == "ssd_w_in":
        return _w_in_from_shards(gathered.reshape(s, 2 * r, c), name="ssd_w_in_unshard")
    if name in ("attn_w_qkv", "mlp_w_up0", "mlp_w_up1"):
        return gathered.reshape(s, 2 * r, c)
    return gathered.reshape(s * 2 * r, c)


class _StepComm:
    GATHER = {"in_proj": ("mlp_w_up0", "attn_w_o"), "conv": ("mlp_w_down0",), "scan": ("ssd_w_out", "mlp_w_up1"),
              "mlp_up_l0": ("attn_w_qkv",), "attn_fwd": ("mlp_w_down1",)}
    EXCHANGE = {"early": ("ssd_w_out", "attn_w_qkv", "attn_w_o", "mlp_w_up0", "mlp_w_up1", "mlp_w_down0", "mlp_w_down1"),
                "late": ("ssd_w_in",)}

    def __init__(self, shards, core):
        self.shards, self.core = shards, core
        self.chip_parts = {}
        self._pending = None

    def gather_hook(self, stage):
        names = self.GATHER.get(stage)
        return _GatherHook([self.shards[n] for n in names]) if names else None

    def weights_from(self, stage, gathered):
        return {n: _full_weight(n, g) for n, g in zip(self.GATHER[stage], gathered)}

    def chip_sums(self, mats, tag):
        parts = [_shard_halves(a) for a in mats.values()]
        theirs = _send_other_half(parts, name=f"grad_sibling_send_{tag}")
        return _add_sibling_half(parts, theirs, self.core, name=f"grad_chip_sum_{tag}")

    def exchange_hook(self, mats, which):
        self._pending = self.EXCHANGE[which]
        return _ExchangeHook(self.chip_sums({n: mats[n] for n in self._pending}, which))

    def received(self, arrays):
        self.chip_parts.update(zip(self._pending, arrays))


ADAMW_ROW_BLOCKS = 16


def _adamw(ws, gs, ms, vs, *, name):
    n = len(ws)
    nb = ADAMW_ROW_BLOCKS if all(a.shape[0] % (8 * ADAMW_ROW_BLOCKS) == 0 for a in ws) else 1

    def body(*refs):
        ins, outs = refs[:4 * n], refs[4 * n:]
        for i in range(n):
            w_ref, g_ref, m_ref, v_ref = ins[i], ins[n + i], ins[2 * n + i], ins[3 * n + i]
            go_ref, d_ref, nm_ref, nv_ref = outs[i], outs[n + i], outs[2 * n + i], outs[3 * n + i]
            gv = g_ref[...]
            nm = ADAM_B1 * m_ref[...] + (1.0 - ADAM_B1) * gv
            nv = ADAM_B2 * v_ref[...] + (1.0 - ADAM_B2) * (gv * gv)
            m_hat = nm / (1.0 - ADAM_B1 ** ADAM_STEP)
            v_hat = nv / (1.0 - ADAM_B2 ** ADAM_STEP)
            go_ref[...] = gv
            d_ref[...] = -ADAM_LR * (m_hat / (jnp.sqrt(v_hat) + ADAM_EPS) + ADAM_WD * w_ref[...])
            nm_ref[...] = nm
            nv_ref[...] = nv

    blks = [pl.BlockSpec((a.shape[0] // nb, a.shape[1]), lambda i: (i, 0)) for a in ws]
    shapes = [jax.ShapeDtypeStruct(a.shape, F32) for a in ws]
    outs = pl.pallas_call(body, grid=(nb,), in_specs=blks * 4, out_specs=blks * 4, out_shape=shapes * 4,
                          compiler_params=_params("parallel"), name=name)(*ws, *gs, *ms, *vs)
    return [tuple(outs[k * n + i] for k in range(4)) for i in range(n)]


SM_CONV_B, SM_NORM_W, SM_MIX_PRE, SM_MIX_POST, SM_FFN_PRE, SM_FFN_POST, SM_MISC, SM_CONV_W, SM_B_QKV, SM_B_O = 0, 4, 6, 8, 10, 12, 14, 16, 32, 34
SM_ROWS = 40
MISC_DT_BIAS, MISC_A_LOG, MISC_D, MISC_SINKS, MISC_LOSS = 0, 32, 64, 96, 112


def _shard_halves(a):
    c = a.shape[-1]
    return a.reshape(N_CHIPS, 2, -1, c)


def _rows(v):
    return v.reshape(-1, D_MODEL)


def _misc_row(dt_bias, a_log, d, sinks, loss):
    pad = jnp.zeros((D_MODEL - MISC_LOSS - 1,), F32)
    return jnp.concatenate([dt_bias.reshape(-1), a_log.reshape(-1), d.reshape(-1), sinks.reshape(-1), loss.reshape(1), pad]).reshape(1, D_MODEL)


def _replicated_rows(p, loss):
    return jnp.concatenate([
        _rows(p["ssd_conv_b"]), _rows(p["ssd_norm_w"]), _rows(p["mix_pre_norm"]), _rows(p["mix_post_norm"]),
        _rows(p["ffn_pre_norm"]), _rows(p["ffn_post_norm"]),
        _misc_row(p["ssd_dt_bias"], p["ssd_a_log"], p["ssd_d"], p["attn_sinks"], loss), jnp.zeros((1, D_MODEL), F32)], axis=0)


def _sharded_rows(conv_w, b_qkv, b_o):
    last = jnp.concatenate([b_qkv.reshape(-1), b_o.reshape(-1), jnp.zeros((D_MODEL - 640,), F32)]).reshape(1, D_MODEL)
    return jnp.concatenate([conv_w.reshape(SSD_CONV_WIDTH, D_MODEL), last, jnp.zeros((3, D_MODEL), F32)], axis=0)


REPLICATED = ("ssd_conv_b", "ssd_dt_bias", "ssd_a_log", "ssd_d", "ssd_norm_w", "attn_sinks",
              "mix_pre_norm", "mix_post_norm", "ffn_pre_norm", "ffn_post_norm")
MATRICES = ("ssd_w_in", "ssd_w_out", "attn_w_qkv", "attn_w_o", "mlp_w_up", "mlp_w_down")
WEIGHT_NAMES = ("ssd_w_in", "ssd_conv_w", "ssd_conv_b", "ssd_dt_bias", "ssd_a_log", "ssd_d", "ssd_norm_w", "ssd_w_out",
                "attn_w_qkv", "attn_b_qkv", "attn_sinks", "attn_w_o", "attn_b_o", "mlp_w_up", "mlp_w_down",
                "mix_pre_norm", "mix_post_norm", "ffn_pre_norm", "ffn_post_norm")


def _unpack_small(rows16, rows8, like):
    misc = rows16[SM_MISC]
    out = {
        "ssd_conv_b": rows16[SM_CONV_B:SM_CONV_B + 4], "ssd_norm_w": rows16[SM_NORM_W:SM_NORM_W + 2],
        "mix_pre_norm": rows16[SM_MIX_PRE:SM_MIX_PRE + 2], "mix_post_norm": rows16[SM_MIX_POST:SM_MIX_POST + 2],
        "ffn_pre_norm": rows16[SM_FFN_PRE:SM_FFN_PRE + 2], "ffn_post_norm": rows16[SM_FFN_POST:SM_FFN_POST + 2],
        "ssd_dt_bias": misc[MISC_DT_BIAS:MISC_DT_BIAS + 32], "ssd_a_log": misc[MISC_A_LOG:MISC_A_LOG + 32],
        "ssd_d": misc[MISC_D:MISC_D + 32], "attn_sinks": misc[MISC_SINKS:MISC_SINKS + 16],
        "ssd_conv_w": rows8[0:SSD_CONV_WIDTH], "attn_b_qkv": rows8[SSD_CONV_WIDTH, 0:384], "attn_b_o": rows8[SSD_CONV_WIDTH, 384:640],
    }
    return {k: v.reshape(like[k].shape) for k, v in out.items()}


def kernel(x, ssd_w_in, ssd_conv_w, ssd_conv_b, ssd_dt_bias, ssd_a_log, ssd_d, ssd_norm_w, ssd_w_out, attn_w_qkv, attn_b_qkv, attn_sinks, attn_w_o, attn_b_o, mlp_w_up, mlp_w_down, mix_pre_norm, mix_post_norm, ffn_pre_norm, ffn_post_norm, loss_target, m_ssd_w_in, m_ssd_conv_w, m_ssd_conv_b, m_ssd_dt_bias, m_ssd_a_log, m_ssd_d, m_ssd_norm_w, m_ssd_w_out, m_attn_w_qkv, m_attn_b_qkv, m_attn_sinks, m_attn_w_o, m_attn_b_o, m_mlp_w_up, m_mlp_w_down, m_mix_pre_norm, m_mix_post_norm, m_ffn_pre_norm, m_ffn_post_norm, v_ssd_w_in, v_ssd_conv_w, v_ssd_conv_b, v_ssd_dt_bias, v_ssd_a_log, v_ssd_d, v_ssd_norm_w, v_ssd_w_out, v_attn_w_qkv, v_attn_b_qkv, v_attn_sinks, v_attn_w_o, v_attn_b_o, v_mlp_w_up, v_mlp_w_down, v_mix_pre_norm, v_mix_post_norm, v_ffn_pre_norm, v_ffn_post_norm):
    w = dict(zip(WEIGHT_NAMES, (ssd_w_in, ssd_conv_w, ssd_conv_b, ssd_dt_bias, ssd_a_log, ssd_d, ssd_norm_w, ssd_w_out, attn_w_qkv, attn_b_qkv, attn_sinks, attn_w_o, attn_b_o, mlp_w_up, mlp_w_down, mix_pre_norm, mix_post_norm, ffn_pre_norm, ffn_post_norm)))
    m = dict(zip(WEIGHT_NAMES, (m_ssd_w_in, m_ssd_conv_w, m_ssd_conv_b, m_ssd_dt_bias, m_ssd_a_log, m_ssd_d, m_ssd_norm_w, m_ssd_w_out, m_attn_w_qkv, m_attn_b_qkv, m_attn_sinks, m_attn_w_o, m_attn_b_o, m_mlp_w_up, m_mlp_w_down, m_mix_pre_norm, m_mix_post_norm, m_ffn_pre_norm, m_ffn_post_norm)))
    v = dict(zip(WEIGHT_NAMES, (v_ssd_w_in, v_ssd_conv_w, v_ssd_conv_b, v_ssd_dt_bias, v_ssd_a_log, v_ssd_d, v_ssd_norm_w, v_ssd_w_out, v_attn_w_qkv, v_attn_b_qkv, v_attn_sinks, v_attn_w_o, v_attn_b_o, v_mlp_w_up, v_mlp_w_down, v_mix_pre_norm, v_mix_post_norm, v_ffn_pre_norm, v_ffn_post_norm)))
    chip = 2 * lax.axis_index("x") + lax.axis_index("y")

    two_halves = lambda a: a.reshape(2, a.shape[0] // 2, a.shape[1])
    later = {"ssd_w_out": (w["ssd_w_out"], 0), "attn_w_qkv": (w["attn_w_qkv"], 0), "attn_w_o": (w["attn_w_o"], 0),
             "mlp_w_up0": (w["mlp_w_up"], 0), "mlp_w_up1": (w["mlp_w_up"], 1),
             "mlp_w_down0": (w["mlp_w_down"], 0), "mlp_w_down1": (w["mlp_w_down"], 1)}
    first = _GatherHook([two_halves(w["ssd_w_in"][0].astype(BF16))], [w["ssd_conv_w"][0], w["attn_b_qkv"], w["attn_b_o"]])
    cast, (g_in, g_conv, g_bqkv, g_bo) = _cast_bf16(list(later.values()), name="weights_to_bf16", hook=first)
    core = lax.axis_index("c").astype(jnp.int32).reshape(1)
    comm = _StepComm({k: two_halves(a) for k, a in zip(later, cast)}, core)
    full = {
        "ssd_w_in": _full_weight("ssd_w_in", g_in),
        "ssd_conv_w": g_conv.transpose(1, 0, 2).reshape(SSD_CONV_WIDTH, SSD_CONV_DIM),
        "attn_b_qkv": g_bqkv.reshape(ATTN_QKV), "attn_b_o": g_bo.reshape(D_MODEL),
    }
    for name in REPLICATED:
        full[name] = w[name][0] if name.startswith(("ssd_", "attn_")) else w[name]

    loss_tile, grad_x, gm, g = _local_step(x[0], loss_target[0], full, comm)

    conv_w_rows = g["ssd_conv_w"].reshape(SSD_CONV_WIDTH * N_CHIPS, D_MODEL)
    b_qkv_rows = jnp.pad(g["attn_b_qkv"], (0, 2 * D_MODEL - ATTN_QKV)).reshape(2, D_MODEL)
    small = jnp.concatenate([_replicated_rows(g, loss_tile[0, 0]), conv_w_rows, b_qkv_rows, _rows(g["attn_b_o"]),
                             jnp.zeros((SM_ROWS - SM_B_O - 1, D_MODEL), F32)], axis=0)
    small_all, = _hook_call(_ExchangeHook([], [small]), name="vector_grad_all_gather")
    order = ("ssd_w_in", "ssd_w_out", "attn_w_qkv", "attn_w_o", "mlp_w_up0", "mlp_w_up1", "mlp_w_down0", "mlp_w_down1")
    halves = _sum_chips([comm.chip_parts[k] for k in order], name="grad_sum")
    r_in, r_out, r_qkv, r_o, r_up, r_down = _swap_halves(halves, layers=((4, 5), (6, 7)), name="grad_halves_swap")
    small_sum, = _sum_chips([small_all], name="small_grad_sum")

    grads = {"ssd_w_in": r_in, "ssd_w_out": r_out, "attn_w_qkv": r_qkv, "attn_w_o": r_o, "mlp_w_up": r_up, "mlp_w_down": r_down}
    grads = {k: a.reshape(w[k].shape) for k, a in grads.items()}
    conv_w_g = lax.dynamic_index_in_dim(small_sum[SM_CONV_W:SM_CONV_W + 16].reshape(SSD_CONV_WIDTH, N_CHIPS, D_MODEL), chip, axis=1, keepdims=False)
    b_qkv_g = lax.dynamic_slice_in_dim(small_sum[SM_B_QKV:SM_B_QKV + 2].reshape(-1), chip * 384, 384)
    b_o_g = lax.dynamic_slice_in_dim(small_sum[SM_B_O], chip * 256, 256)
    small_g = jnp.concatenate([small_sum[0:16], _sharded_rows(conv_w_g, b_qkv_g, b_o_g)], axis=0)
    grads.update(_unpack_small(small_g[0:16], small_g[16:24], w))
    loss = small_sum[SM_MISC, MISC_LOSS]

    delta, new_m, new_v = {}, {}, {}
    as2d = lambda p: [p[name].reshape(-1, p[name].shape[-1]) for name in MATRICES]
    for name, (g2, d2, m2, v2) in zip(MATRICES, _adamw(as2d(w), as2d(grads), as2d(m), as2d(v), name="adamw_matrices")):
        shape = w[name].shape
        grads[name], delta[name], new_m[name], new_v[name] = g2.reshape(shape), d2.reshape(shape), m2.reshape(shape), v2.reshape(shape)
    zero = jnp.zeros((), F32)
    small_pack = lambda p: jnp.concatenate([_replicated_rows({k: p[k] for k in REPLICATED}, zero),
                                            _sharded_rows(p["ssd_conv_w"], p["attn_b_qkv"], p["attn_b_o"])], axis=0)
    (_, d_s, m_s, v_s), = _adamw([small_pack(w)], [small_g], [small_pack(m)], [small_pack(v)], name="adamw_vectors")
    delta.update(_unpack_small(d_s[0:16], d_s[16:24], w))
    new_m.update(_unpack_small(m_s[0:16], m_s[16:24], w))
    new_v.update(_unpack_small(v_s[0:16], v_s[16:24], w))

    return (loss, grad_x[None], *[grads[n] for n in WEIGHT_NAMES], *[delta[n] for n in WEIGHT_NAMES],
            *[new_m[n] for n in WEIGHT_NAMES], *[new_v[n] for n in WEIGHT_NAMES])
```

```python
import functools
import math

import jax
import jax.numpy as jnp
from jax import lax
from jax.experimental import pallas as pl
from jax.experimental.pallas import tpu as pltpu

F32 = jnp.float32
BF16 = jnp.bfloat16

D_MODEL = 1024
SSD_D_INNER = 2048
SSD_HEAD_DIM = 64
SSD_N_HEADS = 32
SSD_N_GROUPS = 8
SSD_HPG = 4
SSD_D_STATE = 128
SSD_CONV_WIDTH = 4
SSD_CHUNK = 128
SSD_CONV_DIM = 4096
SSD_IN_DIM = 6176
SSD_IN_PAD = 6400
SSD_IN_TILE = 1280
SSD_DT_COL = 6144
SSD_GW = SSD_HPG * SSD_HEAD_DIM
ATTN_HEAD_DIM = 64
ATTN_N_Q = 16
ATTN_N_KV = 4
ATTN_REP = 4
ATTN_WINDOW = 128
ATTN_QKV = 1536
D_FF = 4096
NORM_EPS = 1e-6

ADAM_LR = 0.001
ADAM_B1 = 0.9
ADAM_B2 = 0.999
ADAM_EPS = 1e-08
ADAM_WD = 0.01
ADAM_STEP = 10

N_CHIPS = 4
N_DEV = 8
LANES = 128
VMEM_LIMIT = 48 * 1024 * 1024
BIG_TILE = 2048

MESH = pl.DeviceIdType.MESH


def _params(*sem):
    return pltpu.CompilerParams(dimension_semantics=sem, vmem_limit_bytes=VMEM_LIMIT)


def _dot(a, b, dims):
    return lax.dot_general(a, b, (dims, ((), ())), preferred_element_type=F32)


def _dot_nn(a, b):
    return _dot(a, b, ((1,), (0,)))


def _dot_nt(a, b):
    return _dot(a, b, ((1,), (1,)))


def _dot_tn(a, b):
    return _dot(a, b, ((0,), (0,)))


def _sigmoid(x):
    return 0.5 * jnp.tanh(0.5 * x) + 0.5


ANY = pl.BlockSpec(memory_space=pl.ANY)


class _HookSlots:
    def __init__(self, hook, n_in, n_out, n_scratch):
        self.hook = hook
        self.n_in, self.n_out, self.n_scratch = n_in, n_out, n_scratch
        self.inputs = list(hook.arrs) if hook else []
        self.out_shape = list(hook.out_shape) if hook else []
        self.scratch = list(hook.scratch) if hook else []
        self.in_specs = [ANY] * len(self.inputs)
        self.out_specs = [ANY] * len(self.out_shape)

    def _split(self, refs):
        a = self.n_in
        b = a + len(self.inputs)
        c = b + self.n_out
        d = c + len(self.out_shape)
        e = d + self.n_scratch
        return refs[:a], refs[a:b], refs[b:c], refs[c:d], refs[d:e], refs[e:]

    def own(self, refs):
        ins, _, outs, _, scratch, _ = self._split(refs)
        return ins, outs, scratch

    def run(self, refs, step, n_steps):
        _, h_in, _, h_out, _, h_scratch = self._split(refs)
        _run_hook(self.hook, h_in, h_out, h_scratch, step, n_steps)

    def semantics(self, *sem):
        return sem if self.hook is None else ("arbitrary",) * len(sem)


def _matmul(a, b, *, mode, out_dtypes, name, epilogue=None, extras=(), tm=1024, tn=1024, tk=1024,
            b_shards=False, out_shards=False, hook=None, f32_block=None):
    f32_tail = f32_block is not None
    if b_shards:
        s, b_rows, b_cols = b.shape
        b2 = (b_rows, s * b_cols)
        if mode == "nn":
            tn = b_cols
        else:
            assert mode == "nt"
            tk = b_cols
    else:
        b2 = b.shape
    if mode == "nn":
        (m, k), (k2, n) = a.shape, b2
    elif mode == "nt":
        (m, k), (n, k2) = a.shape, b2
    else:
        (k, m), (k2, n) = a.shape, b2
    assert k == k2, (a.shape, b.shape, mode)
    tm, tn, tk = min(tm, m), min(tn, n), min(tk, k)
    assert m % tm == 0 and n % tn == 0 and k % tk == 0, (m, n, k, tm, tn, tk)
    nk = k // tk
    if mode == "tn":
        a_spec = pl.BlockSpec((tk, tm), lambda i, j, kk: (kk, i))
    else:
        a_spec = pl.BlockSpec((tm, tk), lambda i, j, kk: (i, kk))
    if b_shards and mode == "nn":
        b_spec = pl.BlockSpec((None, tk, tn), lambda i, j, kk: (j, kk, 0))
    elif b_shards:
        b_spec = pl.BlockSpec((None, tn, tk), lambda i, j, kk: (kk, j, 0))
    elif mode == "nt":
        b_spec = pl.BlockSpec((tn, tk), lambda i, j, kk: (j, kk))
    else:
        b_spec = pl.BlockSpec((tk, tn), lambda i, j, kk: (kk, j))
    dims = {"nn": ((1,), (0,)), "nt": ((1,), (1,)), "tn": ((0,), (0,))}[mode]
    ex_specs = []
    for arr, kind in extras:
        if kind == "tile":
            ex_specs.append(pl.BlockSpec((tm, tn), lambda i, j, kk: (i, j)))
        else:
            ex_specs.append(pl.BlockSpec((1, tn), lambda i, j, kk: (0, j)))
    n_ex, n_out = len(extras), len(out_dtypes)
    if epilogue is None:
        epilogue = lambda acc: (acc,)
    hk = _HookSlots(hook, n_in=2 + n_ex, n_out=n_out + f32_tail, n_scratch=0 if nk == 1 else 1)
    grid = (m // tm, n // tn, nk)

    def body(*refs):
        (a_ref, b_ref, *ex), outs, scratch = hk.own(refs)
        if hook is not None:
            step = (pl.program_id(0) * grid[1] + pl.program_id(1)) * grid[2] + pl.program_id(2)
            hk.run(refs, step, grid[0] * grid[1] * grid[2])

        def finish(acc):
            res = epilogue(acc, *[e[...] for e in ex])
            for o, r in zip(outs, res):
                o[...] = r.astype(o.dtype)
            if f32_tail:
                outs[n_out][...] = acc[:, f32_block:f32_block + LANES]

        if nk == 1:
            finish(_dot(a_ref[...], b_ref[...], dims))
        else:
            acc_ref = scratch[0]
            kk = pl.program_id(2)

            @pl.when(kk == 0)
            def _():
                acc_ref[...] = jnp.zeros_like(acc_ref)

            acc_ref[...] += _dot(a_ref[...], b_ref[...], dims)

            @pl.when(kk == nk - 1)
            def _():
                finish(acc_ref[...])

    if out_shards:
        out_spec = pl.BlockSpec((None, tm, tn), lambda i, j, kk: (j, i, 0))
        out_dims = (n // tn, m, tn)
    else:
        out_spec = pl.BlockSpec((tm, tn), lambda i, j, kk: (i, j))
        out_dims = (m, n)
    tail_specs = [pl.BlockSpec((tm, LANES), lambda i, j, kk: (i, 0))] if f32_tail else []
    tail_shapes = [jax.ShapeDtypeStruct((m, LANES), F32)] if f32_tail else []
    outs = pl.pallas_call(
        body,
        grid=grid,
        in_specs=[a_spec, b_spec] + ex_specs + hk.in_specs,
        out_specs=[out_spec for _ in out_dtypes] + tail_specs + hk.out_specs,
        out_shape=[jax.ShapeDtypeStruct(out_dims, dt) for dt in out_dtypes] + tail_shapes + hk.out_shape,
        scratch_shapes=([] if nk == 1 else [pltpu.VMEM((tm, tn), F32)]) + hk.scratch,
        compiler_params=_params(*hk.semantics("parallel", "arbitrary" if f32_tail else "parallel", "arbitrary")),
        name=name,
    )(a, b, *[arr for arr, _ in extras], *hk.inputs)
    n_own = n_out + f32_tail
    own = outs[0] if n_own == 1 else outs[:n_own]
    return own if hook is None else (own, outs[n_own:])


def _row_tile(t, want):
    return min(t, want)


def _rms_fwd(x, w, *, name, resid=None, want_u=None, target=None):
    t, d = x.shape
    tr = _row_tile(t, 512)

    def norm(v, wv):
        return v * lax.rsqrt(jnp.mean(v * v, axis=-1, keepdims=True) + NORM_EPS) * wv

    row = pl.BlockSpec((tr, d), lambda i: (i, 0))
    vec = pl.BlockSpec((1, d), lambda i: (0, 0))
    if target is not None:
        def body(x_ref, w_ref, r_ref, t_ref, dh_ref, loss_ref):
            err = r_ref[...] + norm(x_ref[...], w_ref[...]) - t_ref[...]
            dh_ref[...] = err * (1.0 / d)

            @pl.when(pl.program_id(0) == 0)
            def _():
                loss_ref[...] = jnp.zeros_like(loss_ref)

            part = jnp.sum(jnp.sum(err * err, axis=1, keepdims=True), axis=0, keepdims=True) * (0.5 / d)
            loss_ref[...] += jnp.broadcast_to(part, loss_ref.shape)

        return pl.pallas_call(
            body, grid=(t // tr,), in_specs=[row, vec, row, row],
            out_specs=[row, pl.BlockSpec((8, LANES), lambda i: (0, 0))],
            out_shape=[jax.ShapeDtypeStruct((t, d), F32), jax.ShapeDtypeStruct((8, LANES), F32)],
            compiler_params=_params("arbitrary"), name=name)(x, w, resid, target)
    if resid is None:
        def body(x_ref, w_ref, o_ref):
            o_ref[...] = norm(x_ref[...], w_ref[...]).astype(BF16)
        ins, in_specs = (x, w), [row, vec]
        out_shape, out_specs = jax.ShapeDtypeStruct((t, d), BF16), row
    elif want_u is None:
        def body(x_ref, w_ref, r_ref, o_ref):
            o_ref[...] = r_ref[...] + norm(x_ref[...], w_ref[...])
        ins, in_specs = (x, w, resid), [row, vec, row]
        out_shape, out_specs = jax.ShapeDtypeStruct((t, d), F32), row
    else:
        def body(x_ref, w_ref, r_ref, w2_ref, o_ref, u_ref):
            h = r_ref[...] + norm(x_ref[...], w_ref[...])
            o_ref[...] = h
            u_ref[...] = norm(h, w2_ref[...]).astype(BF16)
        ins, in_specs = (x, w, resid, want_u), [row, vec, row, vec]
        out_shape = [jax.ShapeDtypeStruct((t, d), F32), jax.ShapeDtypeStruct((t, d), BF16)]
        out_specs = [row, row]
    return pl.pallas_call(body, grid=(t // tr,), in_specs=in_specs, out_specs=out_specs, out_shape=out_shape,
                          compiler_params=_params("parallel"), name=name)(*ins)


def _rms_bwd(x, w, dy, *, name, resid=None, out_dtype=F32, dx_col_sum=False):
    t, d = x.shape
    tr = _row_tile(t, 512)
    row = pl.BlockSpec((tr, d), lambda i: (i, 0))
    vec = pl.BlockSpec((1, d), lambda i: (0, 0))
    has_res = resid is not None

    def body(x_ref, w_ref, dy_ref, *rest):
        r_ref = rest[0] if has_res else None
        dx_ref, dw_ref = rest[has_res:has_res + 2]
        xv = x_ref[...]
        dyv = dy_ref[...].astype(F32)
        r = lax.rsqrt(jnp.mean(xv * xv, axis=-1, keepdims=True) + NORM_EPS)
        xhat = xv * r
        dyw = dyv * w_ref[...]
        dx = r * (dyw - xhat * jnp.mean(dyw * xhat, axis=-1, keepdims=True))
        if has_res:
            dx = dx + r_ref[...]
        dx_ref[...] = dx.astype(dx_ref.dtype)

        sums = [(dw_ref, dyv * xhat)] + ([(rest[-1], dx)] if dx_col_sum else [])

        @pl.when(pl.program_id(0) == 0)
        def _():
            for acc_ref, _ in sums:
                acc_ref[...] = jnp.zeros_like(acc_ref)

        for acc_ref, rows in sums:
            acc_ref[...] += jnp.sum(rows, axis=0, keepdims=True)

    ins = (x, w, dy) + ((resid,) if has_res else ())
    in_specs = [row, vec, row] + ([row] if has_res else [])
    n_vec = 2 if dx_col_sum else 1
    return pl.pallas_call(
        body, grid=(t // tr,), in_specs=in_specs, out_specs=[row] + [vec] * n_vec,
        out_shape=[jax.ShapeDtypeStruct((t, d), out_dtype)] + [jax.ShapeDtypeStruct((1, d), F32)] * n_vec,
        compiler_params=_params("arbitrary"), name=name)(*ins)


def _col_sum(x, *, name):
    t, n = x.shape
    tr = _row_tile(t, 512)

    def body(x_ref, o_ref):
        @pl.when(pl.program_id(0) == 0)
        def _():
            o_ref[...] = jnp.zeros_like(o_ref)

        o_ref[...] += jnp.sum(x_ref[...].astype(F32), axis=0, keepdims=True)

    return pl.pallas_call(
        body, grid=(t // tr,), in_specs=[pl.BlockSpec((tr, n), lambda i: (i, 0))],
        out_specs=pl.BlockSpec((1, n), lambda i: (0, 0)), out_shape=jax.ShapeDtypeStruct((1, n), F32),
        compiler_params=_params("arbitrary"), name=name)(x)


SSD_IN_SHARD = SSD_IN_DIM // N_CHIPS


def _w_in_from_shards(shards, *, name):
    d = shards.shape[1]
    tr = 256

    def body(s_ref, o_ref):
        o_ref[:, pl.ds(SSD_DT_COL, SSD_IN_PAD - SSD_DT_COL)] = jnp.zeros((tr, SSD_IN_PAD - SSD_DT_COL), o_ref.dtype)
        for s in range(N_CHIPS):
            o_ref[:, pl.ds(SSD_IN_SHARD * s, SSD_IN_SHARD)] = s_ref[s]

    return pl.pallas_call(
        body, grid=(d // tr,), in_specs=[pl.BlockSpec((N_CHIPS, tr, SSD_IN_SHARD), lambda i: (0, i, 0))],
        out_specs=pl.BlockSpec((tr, SSD_IN_PAD), lambda i: (i, 0)),
        out_shape=jax.ShapeDtypeStruct((d, SSD_IN_PAD), shards.dtype),
        compiler_params=_params("parallel"), name=name)(shards)


def _w_in_to_shards(g, *, name):
    d = g.shape[0]
    tr = 256

    def body(g_ref, o_ref):
        for s in range(N_CHIPS):
            o_ref[s] = g_ref[:, pl.ds(SSD_IN_SHARD * s, SSD_IN_SHARD)].astype(o_ref.dtype)

    return pl.pallas_call(
        body, grid=(d // tr,), in_specs=[pl.BlockSpec((tr, SSD_IN_PAD), lambda i: (i, 0))],
        out_specs=pl.BlockSpec((N_CHIPS, tr, SSD_IN_SHARD), lambda i: (0, i, 0)),
        out_shape=jax.ShapeDtypeStruct((N_CHIPS, d, SSD_IN_SHARD), BF16),
        compiler_params=_params("parallel"), name=name)(g)


XBC_COL0 = SSD_D_INNER // LANES


def _shift_down(v, k, row_ids):
    return jnp.where(row_ids >= k, pltpu.roll(v, k, axis=0), 0.0)


def _shift_up(v, k, row_ids):
    n = v.shape[0]
    return jnp.where(row_ids < n - k, pltpu.roll(v, n - k, axis=0), 0.0)


def _conv_pre(x, w, b, row_ids):
    pre = b + w[3:4, :] * x
    for k in (1, 2, 3):
        pre = pre + w[3 - k:4 - k, :] * _shift_down(x, k, row_ids)
    return pre


def _conv_fwd(zx, conv_w, conv_b, *, name, hook=None):
    t = zx.shape[0]
    nct = SSD_CONV_DIM // LANES
    hk = _HookSlots(hook, n_in=3, n_out=1, n_scratch=0)

    def body(*refs):
        (x_ref, w_ref, b_ref), (o_ref,), _ = hk.own(refs)
        if hook is not None:
            hk.run(refs, pl.program_id(0), nct)
        x = x_ref[...].astype(F32)
        row_ids = lax.broadcasted_iota(jnp.int32, x.shape, 0)
        pre = _conv_pre(x, w_ref[...], b_ref[...], row_ids)
        o_ref[...] = pre * _sigmoid(pre)

    outs = pl.pallas_call(
        body, grid=(nct,),
        in_specs=[pl.BlockSpec((t, LANES), lambda j: (0, XBC_COL0 + j)),
                  pl.BlockSpec((SSD_CONV_WIDTH, LANES), lambda j: (0, j)),
                  pl.BlockSpec((1, LANES), lambda j: (0, j))] + hk.in_specs,
        out_specs=[pl.BlockSpec((t, LANES), lambda j: (0, j))] + hk.out_specs,
        out_shape=[jax.ShapeDtypeStruct((t, SSD_CONV_DIM), F32)] + hk.out_shape,
        scratch_shapes=hk.scratch,
        compiler_params=_params(*hk.semantics("parallel")), name=name)(zx, conv_w, conv_b, *hk.inputs)
    return outs[0] if hook is None else (outs[0], outs[1:])


def _conv_bwd(zx, conv_w, conv_b, d_xs, d_bm, d_cm, dzx, *, name):
    t = zx.shape[0]
    nct = SSD_CONV_DIM // LANES
    n_xs = SSD_D_INNER // LANES
    n_bm = SSD_N_GROUPS * SSD_D_STATE // LANES

    def body(x_ref, w_ref, b_ref, dxs_ref, dbm_ref, dcm_ref, _, dx_ref, dw_ref, db_ref):
        x = x_ref[...].astype(F32)
        w = w_ref[...]
        j = pl.program_id(0)
        dy = jnp.where(j < n_xs, dxs_ref[...], jnp.where(j < n_xs + n_bm, dbm_ref[...], dcm_ref[...]))
        row_ids = lax.broadcasted_iota(jnp.int32, x.shape, 0)
        pre = _conv_pre(x, w, b_ref[...], row_ids)
        sg = _sigmoid(pre)
        dpre = dy * (sg * (1.0 + pre * (1.0 - sg)))
        dx = w[3:4, :] * dpre
        for k in (1, 2, 3):
            dx = dx + w[3 - k:4 - k, :] * _shift_up(dpre, k, row_ids)
        dx_ref[...] = dx.astype(dx_ref.dtype)
        db_ref[...] = jnp.sum(dpre, axis=0, keepdims=True)
        dw_ref[3:4, :] = jnp.sum(dpre * x, axis=0, keepdims=True)
        for k in (1, 2, 3):
            dw_ref[3 - k:4 - k, :] = jnp.sum(dpre * _shift_down(x, k, row_ids), axis=0, keepdims=True)

    clip = lambda j, lo, n: jnp.clip(j - lo, 0, n - 1)
    return pl.pallas_call(
        body, grid=(nct,),
        in_specs=[pl.BlockSpec((t, LANES), lambda j: (0, XBC_COL0 + j)),
                  pl.BlockSpec((SSD_CONV_WIDTH, LANES), lambda j: (0, j)),
                  pl.BlockSpec((1, LANES), lambda j: (0, j)),
                  pl.BlockSpec((t, LANES), lambda j: (0, clip(j, 0, n_xs))),
                  pl.BlockSpec((t, LANES), lambda j: (0, clip(j, n_xs, n_bm))),
                  pl.BlockSpec((t, LANES), lambda j: (0, clip(j, n_xs + n_bm, n_bm))), ANY],
        out_specs=[pl.BlockSpec((t, LANES), lambda j: (0, XBC_COL0 + j)),
                   pl.BlockSpec((SSD_CONV_WIDTH, LANES), lambda j: (0, j)), pl.BlockSpec((1, LANES), lambda j: (0, j))],
        out_shape=[jax.ShapeDtypeStruct(dzx.shape, dzx.dtype),
                   jax.ShapeDtypeStruct((SSD_CONV_WIDTH, SSD_CONV_DIM), F32),
                   jax.ShapeDtypeStruct((1, SSD_CONV_DIM), F32)],
        input_output_aliases={6: 0},
        compiler_params=_params("parallel"), name=name)(zx, conv_w, conv_b, d_xs, d_bm, d_cm, dzx)


def _softplus_fwd(zx, bias_row, alog_row, *, name):
    t = zx.shape[0]
    q = SSD_CHUNK
    tr = _row_tile(t, 1024)

    def body(x_ref, b_ref, al_ref, dt_ref, cum_ref):
        v = x_ref[...] + b_ref[...]
        e = jnp.exp(-jnp.abs(v))
        u = 1.0 + e
        log1p = jnp.where(u == 1.0, e, jnp.log(u) * (e / (u - 1.0)))
        dt = jnp.maximum(v, 0.0) + log1p
        dt_ref[...] = dt
        a = dt * -jnp.exp(al_ref[...])
        lower = (lax.broadcasted_iota(jnp.int32, (q, q), 1) <= lax.broadcasted_iota(jnp.int32, (q, q), 0)).astype(F32)
        cums = [lax.dot_general(lower, a[c * q:(c + 1) * q, :], ((((1,), (0,))), ((), ())), precision=lax.Precision.HIGHEST,
                                preferred_element_type=F32) for c in range(tr // q)]
        cum_ref[...] = jnp.concatenate(cums, axis=0)

    blk = pl.BlockSpec((tr, LANES), lambda i: (i, 0))
    vec = pl.BlockSpec((1, LANES), lambda i: (0, 0))
    return pl.pallas_call(
        body, grid=(t // tr,),
        in_specs=[blk, vec, vec],
        out_specs=[blk, blk],
        out_shape=[jax.ShapeDtypeStruct((t, LANES), F32), jax.ShapeDtypeStruct((t, LANES), F32)],
        compiler_params=_params("parallel"), name=name)(zx, bias_row, alog_row)


def _softplus_bwd(dt_raw, bias_row, ddt, dzx, *, name):
    t = dt_raw.shape[0]
    tr = _row_tile(t, 1024)
    tail = SSD_IN_PAD - SSD_DT_COL

    def body(x_ref, b_ref, g_ref, _, o_ref, db_ref):
        v = x_ref[...] + b_ref[...]
        lane = lax.broadcasted_iota(jnp.int32, v.shape, 1)
        d = jnp.where(lane < SSD_N_HEADS, g_ref[...] * _sigmoid(v), 0.0)
        o_ref[:, pl.ds(0, LANES)] = d.astype(o_ref.dtype)
        o_ref[:, pl.ds(LANES, tail - LANES)] = jnp.zeros((tr, tail - LANES), o_ref.dtype)

        @pl.when(pl.program_id(0) == 0)
        def _():
            db_ref[...] = jnp.zeros_like(db_ref)

        db_ref[...] += jnp.sum(d, axis=0, keepdims=True)

    return pl.pallas_call(
        body, grid=(t // tr,),
        in_specs=[pl.BlockSpec((tr, LANES), lambda i: (i, 0)), pl.BlockSpec((1, LANES), lambda i: (0, 0)),
                  pl.BlockSpec((tr, LANES), lambda i: (i, 0)), ANY],
        out_specs=[pl.BlockSpec((tr, tail), lambda i: (i, SSD_DT_COL // tail)), pl.BlockSpec((1, LANES), lambda i: (0, 0))],
        out_shape=[jax.ShapeDtypeStruct(dzx.shape, dzx.dtype), jax.ShapeDtypeStruct((1, LANES), F32)],
        input_output_aliases={3: 0},
        compiler_params=_params("arbitrary"), name=name)(dt_raw, bias_row, ddt, dzx)


def _ssd_masks():
    q = SSD_CHUNK
    tt = lax.broadcasted_iota(jnp.int32, (q, q), 0)
    ss = lax.broadcasted_iota(jnp.int32, (q, q), 1)
    lane = lax.broadcasted_iota(jnp.int32, (1, SSD_GW), 1)
    srow = lax.broadcasted_iota(jnp.int32, (SSD_GW, 1), 0)
    hm = [(lane >= SSD_HEAD_DIM * j) & (lane < SSD_HEAD_DIM * (j + 1)) for j in range(SSD_HPG)]
    rm = [(srow >= SSD_HEAD_DIM * j) & (srow < SSD_HEAD_DIM * (j + 1)) for j in range(SSD_HPG)]
    return tt, ss, hm, rm


def _ssd_head_terms(dt_rows, cum_rows, a_rows, j, tt, ss):
    q = SSD_CHUNK
    dt_row = dt_rows[j:j + 1, :]
    dt_col = jnp.sum(jnp.where(tt == ss, dt_row, 0.0), axis=1, keepdims=True)
    a_row1 = a_rows[j:j + 1, :]
    a_11 = a_rows[j:j + 1, 0:1]
    cum_col = jnp.sum(jnp.where(ss <= tt, dt_row * a_row1, 0.0), axis=1, keepdims=True)
    cum_row = cum_rows[j:j + 1, :]
    decay = jnp.exp(jnp.where(ss <= tt, cum_col - cum_row, -jnp.inf))
    cum_last = cum_col[q - 1:q, :]
    e_col = jnp.exp(cum_col)
    dte_col = jnp.exp(cum_last - cum_col)
    e_last = jnp.exp(cum_last)
    return dt_col, dt_row, a_row1, a_11, decay, e_col, dte_col, e_last


SSD_CHUNKS_PER_STEP = 4
SSD_BC_COL0 = SSD_D_INNER // SSD_D_STATE


def _ssd_head_selects(terms, hm, rm):
    e_all = jnp.zeros((SSD_CHUNK, SSD_GW), F32)
    w_all = jnp.zeros((SSD_CHUNK, SSD_GW), F32)
    e_s = jnp.zeros((SSD_GW, 1), F32)
    for j in range(SSD_HPG):
        dt_col, _, _, _, _, e_col, dte_col, e_last = terms[j]
        e_all = jnp.where(hm[j], e_col, e_all)
        w_all = jnp.where(hm[j], dt_col * dte_col, w_all)
        e_s = jnp.where(rm[j], e_last, e_s)
    return e_all, w_all, e_s


def _ssd_fwd(xc, dtr, cumr, alog_b, d_b, *, name, hook=None):
    t = xc.shape[0]
    q = SSD_CHUNK
    nc = t // q
    kc = min(SSD_CHUNKS_PER_STEP, nc)
    rows = kc * q
    hk = _HookSlots(hook, n_in=7, n_out=2, n_scratch=1)

    def body(*refs):
        (x_ref, b_ref, c_ref, dtr_ref, cumr_ref, alog_ref, d_ref), (y_ref, st_ref), (s_scr,) = hk.own(refs)
        if hook is not None:
            hk.run(refs, pl.program_id(0) * (nc // kc) + pl.program_id(1), SSD_N_GROUPS * (nc // kc))

        @pl.when(pl.program_id(1) == 0)
        def _():
            s_scr[...] = jnp.zeros_like(s_scr)

        tt, ss, hm, rm = _ssd_masks()
        a_rows = -jnp.exp(alog_ref[...])
        d_rows = d_ref[...]
        d_all = jnp.zeros((1, SSD_GW), F32)
        for j in range(SSD_HPG):
            d_all = jnp.where(hm[j], d_rows[j:j + 1, 0:1], d_all)
        ks, hs = range(kc), range(SSD_HPG)
        sl = [pl.ds(k * q, q) for k in ks]
        x = [x_ref[sl[k], :] for k in ks]
        bm = [b_ref[sl[k], :].astype(BF16) for k in ks]
        cm = [c_ref[sl[k], :].astype(BF16) for k in ks]
        xb = [x[k].astype(BF16) for k in ks]
        terms = [[_ssd_head_terms(dtr_ref[:, sl[k]], cumr_ref[:, sl[k]], a_rows, j, tt, ss) for j in hs] for k in ks]
        g = [_dot_nt(cm[k], bm[k]) for k in ks]
        m = [[(g[k] * terms[k][j][4] * terms[k][j][1]).astype(BF16) for j in hs] for k in ks]
        yj = [[_dot_nn(m[k][j], xb[k]) for j in hs] for k in ks]
        sel = [_ssd_head_selects(terms[k], hm, rm) for k in ks]
        upd = [_dot_tn((x[k] * sel[k][1]).astype(BF16), bm[k]) for k in ks]
        states = [s_scr[...]]
        for k in ks:
            states.append(states[k] * sel[k][2] + upd[k])
        inter = [_dot_nt(cm[k], states[k].astype(BF16)) for k in ks]
        ys = []
        for k in ks:
            y = jnp.zeros((q, SSD_GW), F32)
            for j in hs:
                y = jnp.where(hm[j], yj[k][j], y)
            ys.append(y + inter[k] * sel[k][0] + x[k] * d_all)
        for k in ks:
            st_ref[k] = states[k]
        y_ref[...] = jnp.concatenate(ys, axis=0)
        s_scr[...] = states[kc]

    blk = lambda width, off: pl.BlockSpec((rows, width), lambda g, c: (c, off + g))
    par_s = pl.BlockSpec((None, SSD_HPG, LANES), lambda g, c: (g, 0, 0))
    row_s = pl.BlockSpec((None, SSD_HPG, rows), lambda g, c: (g, 0, c))
    outs = pl.pallas_call(
        body, grid=(SSD_N_GROUPS, nc // kc),
        in_specs=[blk(SSD_GW, 0), blk(SSD_D_STATE, SSD_BC_COL0), blk(SSD_D_STATE, SSD_BC_COL0 + SSD_N_GROUPS),
                  row_s, row_s, par_s, par_s] + hk.in_specs,
        out_specs=[blk(SSD_GW, 0), pl.BlockSpec((None, kc, SSD_GW, SSD_D_STATE), lambda g, c: (g, c, 0, 0))] + hk.out_specs,
        out_shape=[jax.ShapeDtypeStruct((t, SSD_D_INNER), F32),
                   jax.ShapeDtypeStruct((SSD_N_GROUPS, nc, SSD_GW, SSD_D_STATE), F32)] + hk.out_shape,
        scratch_shapes=[pltpu.VMEM((SSD_GW, SSD_D_STATE), F32)] + hk.scratch,
        compiler_params=_params(*hk.semantics("parallel", "arbitrary")), name=name)(
            xc, xc, xc, dtr, cumr, alog_b, d_b, *hk.inputs)
    return outs if hook is None else (outs[:2], outs[2:])


def _ssd_bwd(xc, dtr, cumr, alog_b, d_b, states, dy, *, name, hook=None):
    t = xc.shape[0]
    q = SSD_CHUNK
    nc = t // q
    kc = min(SSD_CHUNKS_PER_STEP, nc)
    nst = nc // kc
    rows = kc * q
    rev = lambda c: nst - 1 - c
    hk = _HookSlots(hook, n_in=9, n_out=5, n_scratch=1)

    def body(*refs):
        ((x_ref, b_ref, c_ref, dtr_ref, cumr_ref, alog_ref, d_ref, st_ref, dy_ref),
         (dx_ref, db_ref, dc_ref, ddt_ref, dpar_ref), (ds_scr,)) = hk.own(refs)
        if hook is not None:
            hk.run(refs, pl.program_id(0) * nst + pl.program_id(1), SSD_N_GROUPS * nst)

        @pl.when(pl.program_id(1) == 0)
        def _():
            ds_scr[...] = jnp.zeros_like(ds_scr)
            dpar_ref[...] = jnp.zeros_like(dpar_ref)

        tt, ss, hm, rm = _ssd_masks()
        tcol = lax.broadcasted_iota(jnp.int32, (q, 1), 0)
        lane = lax.broadcasted_iota(jnp.int32, (1, LANES), 1)
        a_rows = -jnp.exp(alog_ref[...])
        d_rows = d_ref[...]
        d_all = jnp.zeros((1, SSD_GW), F32)
        for j in range(SSD_HPG):
            d_all = jnp.where(hm[j], d_rows[j:j + 1, 0:1], d_all)
        ks, hs = range(kc), range(SSD_HPG)
        sl = [pl.ds(k * q, q) for k in ks]
        x = [x_ref[sl[k], :] for k in ks]
        dyv = [dy_ref[sl[k], :] for k in ks]
        bm = [b_ref[sl[k], :].astype(BF16) for k in ks]
        cm = [c_ref[sl[k], :].astype(BF16) for k in ks]
        s_in = [st_ref[k] for k in ks]
        xb = [x[k].astype(BF16) for k in ks]
        dyb = [dyv[k].astype(BF16) for k in ks]
        s_b = [s_in[k].astype(BF16) for k in ks]
        terms = [[_ssd_head_terms(dtr_ref[:, sl[k]], cumr_ref[:, sl[k]], a_rows, j, tt, ss) for j in hs] for k in ks]
        sel = [_ssd_head_selects(terms[k], hm, rm) for k in ks]
        e_all, w_all, e_s = [s_[0] for s_ in sel], [s_[1] for s_ in sel], [s_[2] for s_ in sel]
        dye = [(dyv[k] * e_all[k]).astype(BF16) for k in ks]
        ds_loc = [_dot_tn(dye[k], cm[k]) for k in ks]
        ds = [None] * kc
        running = ds_scr[...]
        for k in reversed(ks):
            ds[k] = running
            running = running * e_s[k] + ds_loc[k]
        ds_scr[...] = running
        ds_b = [ds[k].astype(BF16) for k in ks]
        g = [_dot_nt(cm[k], bm[k]) for k in ks]
        cs = [_dot_nt(cm[k], s_b[k]) for k in ks]
        bds = [_dot_nt(bm[k], ds_b[k]) for k in ks]
        dm = [[_dot_nt(jnp.where(hm[j], dyv[k], 0.0).astype(BF16), xb[k]) for j in hs] for k in ks]
        gl = [[g[k] * terms[k][j][4] for j in hs] for k in ks]
        wp = [[dm[k][j] * gl[k][j] for j in hs] for k in ks]
        mt = [[(gl[k][j] * terms[k][j][1]).astype(BF16) for j in hs] for k in ks]
        dxj = [[_dot_tn(mt[k][j], dyb[k]) for j in hs] for k in ks]
        dg = []
        for k in ks:
            acc = jnp.zeros((q, q), F32)
            for j in hs:
                acc = acc + dm[k][j] * terms[k][j][4] * terms[k][j][1]
            dg.append(acc.astype(BF16))
        dy_cs = [dyv[k] * cs[k] for k in ks]
        x_bds = [x[k] * bds[k] for k in ks]
        dy_x = [dyv[k] * x[k] for k in ks]
        ds_s = [ds[k] * s_in[k] for k in ks]
        w = [[wp[k][j] * terms[k][j][1] for j in hs] for k in ks]
        rw_col = [[jnp.sum(w[k][j], axis=1, keepdims=True) for j in hs] for k in ks]
        cw_row = [[jnp.sum(w[k][j], axis=0, keepdims=True) for j in hs] for k in ks]
        cwp_row = [[jnp.sum(wp[k][j], axis=0, keepdims=True) for j in hs] for k in ks]
        r1_col = [[jnp.sum(jnp.where(hm[j], dy_cs[k], 0.0), axis=1, keepdims=True) * terms[k][j][5] for j in hs] for k in ks]
        dw_col = [[jnp.sum(jnp.where(hm[j], x_bds[k], 0.0), axis=1, keepdims=True) for j in hs] for k in ks]
        head_rows = [slice(j * SSD_HEAD_DIM, (j + 1) * SSD_HEAD_DIM) for j in hs]
        lane_sum = lambda v: jnp.sum(v, axis=1, keepdims=True)
        s_sum = [[lane_sum(jnp.sum(ds_s[k][head_rows[j], :], axis=0, keepdims=True)) for j in hs] for k in ks]
        dy_x_cols = [jnp.sum(dy_x[k], axis=0, keepdims=True) for k in ks]
        d_d = [[lane_sum(jnp.where(hm[j], dy_x_cols[k], 0.0)) for j in hs] for k in ks]
        ddt_rows = [[None] * SSD_HPG for _ in ks]
        dpar = [jnp.zeros((1, LANES), F32) for _ in hs]
        for k in ks:
            for j in hs:
                dt_col, dt_row, a_row1, a_11, _, _, dte_col, e_last = terms[k][j]
                dww = dw_col[k][j] * (dt_col * dte_col)
                last_add = jnp.sum(dww, axis=0, keepdims=True) + e_last * s_sum[k][j]
                dcum_col = rw_col[k][j] + r1_col[k][j] - dww + jnp.where(tcol == q - 1, last_add, 0.0)
                da_row = jnp.sum(jnp.where(tt >= ss, dcum_col, 0.0), axis=0, keepdims=True)
                da_col = jnp.sum(jnp.where(ss >= tt, -cw_row[k][j], 0.0), axis=1, keepdims=True)
                ddt_col = a_11 * da_col + dw_col[k][j] * dte_col
                ddt_rows[k][j] = (a_row1 * da_row + cwp_row[k][j]
                                  + jnp.sum(jnp.where(tt == ss, ddt_col, 0.0), axis=0, keepdims=True))
                d_a = jnp.sum(dt_row * da_row, axis=1, keepdims=True) + jnp.sum(dt_col * da_col, axis=0, keepdims=True)
                dpar[j] = dpar[j] + jnp.where(lane == 0, d_a * a_11, 0.0) + jnp.where(lane == 1, d_d[k][j], 0.0)
        dxs = []
        for k in ks:
            acc = jnp.zeros((q, SSD_GW), F32)
            for j in hs:
                acc = jnp.where(hm[j], dxj[k][j], acc)
            dxs.append(acc + w_all[k] * bds[k] + d_all * dyv[k])
        xw = [(x[k] * w_all[k]).astype(BF16) for k in ks]
        dc = [_dot_nn(dg[k], bm[k]) + _dot_nn(dye[k], s_b[k]) for k in ks]
        db = [_dot_tn(dg[k], cm[k]) + _dot_nn(xw[k], ds_b[k]) for k in ks]
        dx_ref[...] = jnp.concatenate(dxs, axis=0)
        dc_ref[...] = jnp.concatenate(dc, axis=0)
        db_ref[...] = jnp.concatenate(db, axis=0)
        ddt_ref[...] = jnp.concatenate([jnp.concatenate([ddt_rows[k][j] for k in ks], axis=1) for j in hs], axis=0)
        dpar_ref[...] += jnp.concatenate(dpar, axis=0)

    blk = lambda width, off: pl.BlockSpec((rows, width), lambda g, c: (rev(c), off + g))
    par_s = pl.BlockSpec((None, SSD_HPG, LANES), lambda g, c: (g, 0, 0))
    outs = pl.pallas_call(
        body, grid=(SSD_N_GROUPS, nst),
        in_specs=[blk(SSD_GW, 0), blk(SSD_D_STATE, SSD_BC_COL0), blk(SSD_D_STATE, SSD_BC_COL0 + SSD_N_GROUPS),
                  pl.BlockSpec((None, SSD_HPG, rows), lambda g, c: (g, 0, rev(c))),
                  pl.BlockSpec((None, SSD_HPG, rows), lambda g, c: (g, 0, rev(c))), par_s, par_s,
                  pl.BlockSpec((None, kc, SSD_GW, SSD_D_STATE), lambda g, c: (g, rev(c), 0, 0)), blk(SSD_GW, 0)] + hk.in_specs,
        out_specs=[blk(SSD_GW, 0), blk(SSD_D_STATE, 0), blk(SSD_D_STATE, 0),
                   pl.BlockSpec((None, SSD_HPG, rows), lambda g, c: (g, 0, rev(c))), par_s] + hk.out_specs,
        out_shape=[jax.ShapeDtypeStruct((t, SSD_D_INNER), F32),
                   jax.ShapeDtypeStruct((t, SSD_N_GROUPS * SSD_D_STATE), F32),
                   jax.ShapeDtypeStruct((t, SSD_N_GROUPS * SSD_D_STATE), F32),
                   jax.ShapeDtypeStruct((SSD_N_GROUPS, SSD_HPG, t), F32),
                   jax.ShapeDtypeStruct((SSD_N_GROUPS, SSD_HPG, LANES), F32)] + hk.out_shape,
        scratch_shapes=[pltpu.VMEM((SSD_GW, SSD_D_STATE), F32)] + hk.scratch,
        compiler_params=_params(*hk.semantics("parallel", "arbitrary")), name=name)(
            xc, xc, xc, dtr, cumr, alog_b, d_b, states, dy, *hk.inputs)
    return outs if hook is None else (outs[:5], outs[5:])


def _gate_norm_fwd(y, zx, norm_w, *, name):
    t = y.shape[0]
    tr = _row_tile(t, 256)
    row = pl.BlockSpec((tr, SSD_D_INNER), lambda i: (i, 0))

    def body(y_ref, z_ref, w_ref, o_ref):
        for gi in range(SSD_N_GROUPS):
            sl = pl.ds(gi * SSD_GW, SSD_GW)
            z = z_ref[:, sl].astype(F32)
            gv = y_ref[:, sl] * (z * _sigmoid(z))
            r = lax.rsqrt(jnp.mean(gv * gv, axis=-1, keepdims=True) + NORM_EPS)
            o_ref[:, sl] = (gv * r * w_ref[:, sl]).astype(BF16)

    return pl.pallas_call(
        body, grid=(t // tr,), in_specs=[row, row, pl.BlockSpec((1, SSD_D_INNER), lambda i: (0, 0))],
        out_specs=row, out_shape=jax.ShapeDtypeStruct((t, SSD_D_INNER), BF16),
        compiler_params=_params("parallel"), name=name)(y, zx, norm_w)


def _gate_norm_bwd(y, zx, norm_w, dyn, *, name):
    t = y.shape[0]
    tr = _row_tile(t, 256)
    row = pl.BlockSpec((tr, SSD_D_INNER), lambda i: (i, 0))
    vec = pl.BlockSpec((1, SSD_D_INNER), lambda i: (0, 0))

    def body(y_ref, z_ref, w_ref, dyn_ref, dy_ref, dz_ref, dw_ref):
        @pl.when(pl.program_id(0) == 0)
        def _():
            dw_ref[...] = jnp.zeros_like(dw_ref)

        for gi in range(SSD_N_GROUPS):
            sl = pl.ds(gi * SSD_GW, SSD_GW)
            z = z_ref[:, sl].astype(F32)
            yv = y_ref[:, sl]
            sg = _sigmoid(z)
            sz = z * sg
            gv = yv * sz
            r = lax.rsqrt(jnp.mean(gv * gv, axis=-1, keepdims=True) + NORM_EPS)
            ghat = gv * r
            dout = dyn_ref[:, sl].astype(F32)
            dgh = dout * w_ref[:, sl]
            dgv = r * (dgh - ghat * jnp.mean(dgh * ghat, axis=-1, keepdims=True))
            dy_ref[:, sl] = dgv * sz
            dz_ref[:, sl] = (dgv * yv * (sg * (1.0 + z * (1.0 - sg)))).astype(dz_ref.dtype)
            dw_ref[:, sl] += jnp.sum(dout * ghat, axis=0, keepdims=True)

    return pl.pallas_call(
        body, grid=(t // tr,), in_specs=[row, row, vec, row], out_specs=[row, row, vec],
        out_shape=[jax.ShapeDtypeStruct((t, SSD_D_INNER), F32), jax.ShapeDtypeStruct((t, SSD_IN_PAD), BF16),
                   jax.ShapeDtypeStruct((1, SSD_D_INNER), F32)],
        compiler_params=_params("arbitrary"), name=name)(y, zx, norm_w, dyn)


ATTN_KV_W = ATTN_N_KV * ATTN_HEAD_DIM
ATTN_Q_HALF = 512
ATTN_K_BLK = ATTN_N_Q * ATTN_HEAD_DIM // ATTN_KV_W
ATTN_V_BLK = ATTN_K_BLK + 1


def _attn_valid(first_block):
    w = ATTN_WINDOW
    qpos = lax.broadcasted_iota(jnp.int32, (w, 2 * w), 0) + w
    kpos = lax.broadcasted_iota(jnp.int32, (w, 2 * w), 1)
    rel = qpos - kpos
    return (rel >= 0) & (rel < w) & jnp.logical_not(first_block & (kpos < w))


def _attn_head_views(lo_ref, hi_ref):
    hd = ATTN_HEAD_DIM
    per_half = ATTN_Q_HALF // hd
    return [(lo_ref if h < per_half else hi_ref)[:, pl.ds((h % per_half) * hd, hd)] for h in range(ATTN_N_Q)]


def _attn_block_views(lo_ref, hi_ref, kc_ref, kp_ref, vc_ref, vp_ref):
    hd = ATTN_HEAD_DIM
    kv_cols = [pl.ds(kh * hd, hd) for kh in range(ATTN_N_KV)]
    kb = [jnp.concatenate([kp_ref[:, c], kc_ref[:, c]], axis=0) for c in kv_cols]
    vb = [jnp.concatenate([vp_ref[:, c], vc_ref[:, c]], axis=0) for c in kv_cols]
    return _attn_head_views(lo_ref, hi_ref), kb, vb


def _attn_scores(q, kb, valid):
    scale = ATTN_HEAD_DIM ** -0.5
    return [jnp.where(valid, _dot_nt(q[h], kb[h // ATTN_REP]) * scale, -jnp.inf) for h in range(ATTN_N_Q)]


def _attn_softmax(s, sink):
    heads = range(ATTN_N_Q)
    m = [jnp.maximum(jnp.max(s[h], axis=1, keepdims=True), sink[h]) for h in heads]
    e = [jnp.exp(s[h] - m[h]) for h in heads]
    es = [jnp.exp(sink[h] - m[h]) for h in heads]
    inv = [1.0 / (jnp.sum(e[h], axis=1, keepdims=True) + es[h]) for h in heads]
    return e, es, inv


def _attn_fwd(qkv, sinks_b, *, name, hook=None):
    t = qkv.shape[0]
    w = ATTN_WINDOW
    nb = t // w
    prev = lambda n: jnp.maximum(n - 1, 0)
    hk = _HookSlots(hook, n_in=7, n_out=1, n_scratch=0)

    def body(*refs):
        (qlo_ref, qhi_ref, kc_ref, kp_ref, vc_ref, vp_ref, sink_ref), (o_ref,), _ = hk.own(refs)
        if hook is not None:
            hk.run(refs, pl.program_id(0), nb)
        heads = range(ATTN_N_Q)
        q, kb, vb = _attn_block_views(qlo_ref, qhi_ref, kc_ref, kp_ref, vc_ref, vp_ref)
        sink = [sink_ref[h:h + 1, 0:1] for h in heads]
        e, _, inv = _attn_softmax(_attn_scores(q, kb, _attn_valid(pl.program_id(0) == 0)), sink)
        out = [_dot_nn((e[h] * inv[h]).astype(BF16), vb[h // ATTN_REP]).astype(o_ref.dtype) for h in heads]
        o_ref[...] = jnp.concatenate(out, axis=1)

    qh = lambda half: pl.BlockSpec((w, ATTN_Q_HALF), lambda n: (n, half))
    kv = lambda blk, idx: pl.BlockSpec((w, ATTN_KV_W), lambda n: (idx(n), blk))
    cur = lambda n: n
    outs = pl.pallas_call(
        body, grid=(nb,),
        in_specs=[qh(0), qh(1), kv(ATTN_K_BLK, cur), kv(ATTN_K_BLK, prev), kv(ATTN_V_BLK, cur), kv(ATTN_V_BLK, prev),
                  pl.BlockSpec((ATTN_N_Q, LANES), lambda n: (0, 0))] + hk.in_specs,
        out_specs=[pl.BlockSpec((w, D_MODEL), lambda n: (n, 0))] + hk.out_specs,
        out_shape=[jax.ShapeDtypeStruct((t, D_MODEL), BF16)] + hk.out_shape,
        scratch_shapes=hk.scratch,
        compiler_params=_params(*hk.semantics("parallel")), name=name)(qkv, qkv, qkv, qkv, qkv, qkv, sinks_b, *hk.inputs)
    return outs[0] if hook is None else (outs[0], outs[1:])


def _attn_bwd(qkv, sinks_b, dout, *, name):
    t = qkv.shape[0]
    w = ATTN_WINDOW
    nb = t // w
    hd = ATTN_HEAD_DIM
    clamp = lambda n: jnp.minimum(n, nb - 1)
    prev = lambda n: jnp.maximum(clamp(n) - 1, 0)

    def body(qlo_ref, qhi_ref, kc_ref, kp_ref, vc_ref, vp_ref, sink_ref, dolo_ref, dohi_ref,
             dq_ref, dkv_ref, dsink_ref, carry):
        n = pl.program_id(0)

        @pl.when(n == 0)
        def _():
            carry[...] = jnp.zeros_like(carry)
            dsink_ref[...] = jnp.zeros_like(dsink_ref)

        @pl.when(n < nb)
        def _():
            heads, kvs = range(ATTN_N_Q), range(ATTN_N_KV)
            q, kb, vb = _attn_block_views(qlo_ref, qhi_ref, kc_ref, kp_ref, vc_ref, vp_ref)
            do = _attn_head_views(dolo_ref, dohi_ref)
            sink = [sink_ref[h:h + 1, 0:1] for h in heads]
            s = _attn_scores(q, kb, _attn_valid(n == 0))
            dp = [_dot_nt(do[h], vb[h // ATTN_REP]) for h in heads]
            e, es, inv = _attn_softmax(s, sink)
            p = [e[h] * inv[h] for h in heads]
            delta = [jnp.sum(p[h] * dp[h], axis=1, keepdims=True) for h in heads]
            dsc = [(p[h] * (dp[h] - delta[h]) * (hd ** -0.5)).astype(BF16) for h in heads]
            pb = [p[h].astype(BF16) for h in heads]
            dq = [_dot_nn(dsc[h], kb[h // ATTN_REP]).astype(dq_ref.dtype) for h in heads]
            stack = lambda per_head, kh: jnp.concatenate(per_head[kh * ATTN_REP:(kh + 1) * ATTN_REP], axis=0)
            dkb = [_dot_tn(stack(dsc, kh), stack(q, kh)) for kh in kvs]
            dvb = [_dot_tn(stack(pb, kh), stack(do, kh)) for kh in kvs]
            dsink = [jnp.broadcast_to(jnp.sum(-es[h] * inv[h] * delta[h], axis=0, keepdims=True), (1, LANES)) for h in heads]
            dq_ref[...] = jnp.concatenate(dq, axis=1)
            dsink_ref[...] += jnp.concatenate(dsink, axis=0)
            dkv_ref[...] = (carry[...] + jnp.concatenate([d[0:w, :] for d in dkb + dvb], axis=1)).astype(dkv_ref.dtype)
            carry[...] = jnp.concatenate([d[w:2 * w, :] for d in dkb + dvb], axis=1)

        @pl.when(n == nb)
        def _():
            dkv_ref[...] = carry[...].astype(dkv_ref.dtype)

    qh = lambda half: pl.BlockSpec((w, ATTN_Q_HALF), lambda n: (clamp(n), half))
    kv = lambda blk, idx: pl.BlockSpec((w, ATTN_KV_W), lambda n: (idx(n), blk))
    return pl.pallas_call(
        body, grid=(nb + 1,),
        in_specs=[qh(0), qh(1), kv(ATTN_K_BLK, clamp), kv(ATTN_K_BLK, prev), kv(ATTN_V_BLK, clamp), kv(ATTN_V_BLK, prev),
                  pl.BlockSpec((ATTN_N_Q, LANES), lambda n: (0, 0)), qh(0), qh(1)],
        out_specs=[pl.BlockSpec((w, D_MODEL), lambda n: (clamp(n), 0)),
                   pl.BlockSpec((w, 2 * ATTN_KV_W), lambda n: (jnp.maximum(n - 1, 0), 0)),
                   pl.BlockSpec((ATTN_N_Q, LANES), lambda n: (0, 0))],
        out_shape=[jax.ShapeDtypeStruct((t, D_MODEL), BF16), jax.ShapeDtypeStruct((t, 2 * ATTN_KV_W), BF16),
                   jax.ShapeDtypeStruct((ATTN_N_Q, LANES), F32)],
        scratch_shapes=[pltpu.VMEM((w, 2 * ATTN_KV_W), F32)],
        compiler_params=_params("arbitrary"), name=name)(qkv, qkv, qkv, qkv, qkv, qkv, sinks_b, dout, dout)


def _sq_relu_epilogue(acc):
    r = jnp.maximum(acc, 0.0)
    return (r * r,)


def _sq_relu_bwd_epilogue(acc, act):
    return (acc * (2.0 * jnp.sqrt(act.astype(F32))),)


def _bias_epilogue(acc, bias):
    return (acc + bias,)


def _plain_run(stage, fn, *args, **kwargs):
    return fn(*args, **kwargs)


def _mlp_fwd(u, w_up, w_down, tag, run=_plain_run):
    act = run(f"mlp_up_{tag}", _matmul, u, w_up, mode="nn", out_dtypes=(BF16,), epilogue=_sq_relu_epilogue, b_shards=True,
              tm=BIG_TILE, name=f"mlp_up_{tag}")
    f = run(f"mlp_down_{tag}", _matmul, act, w_down, mode="nn", out_dtypes=(F32,), tk=BIG_TILE, name=f"mlp_down_{tag}")
    return act, f


def _mlp_bwd(u, act, w_up, w_down, df, tag):
    dpre = _matmul(df, w_down, mode="nt", out_dtypes=(BF16,), epilogue=_sq_relu_bwd_epilogue,
                   extras=((act, "tile"),), tm=BIG_TILE, name=f"mlp_dact_{tag}")
    dw_down = _matmul(act, df, mode="tn", out_dtypes=(BF16,), tk=BIG_TILE, name=f"mlp_dwdown_{tag}")
    du = _matmul(dpre, w_up, mode="nt", out_dtypes=(F32,), b_shards=True, tm=BIG_TILE, name=f"mlp_du_{tag}")
    dw_up = _matmul(u, dpre, mode="tn", out_dtypes=(BF16,), out_shards=True, tk=BIG_TILE, name=f"mlp_dwup_{tag}")
    return du, dw_up, dw_down


def _group_rows(dt):
    t = dt.shape[0]
    return jnp.transpose(dt[:, :SSD_N_HEADS].reshape(t, SSD_N_GROUPS, SSD_HPG), (1, 2, 0))


def _head_param_rows(p):
    return jnp.broadcast_to(p.reshape(SSD_N_GROUPS, SSD_HPG, 1), (SSD_N_GROUPS, SSD_HPG, LANES))


def _local_step(x, target, wts, comm=None):
    t = x.shape[0]
    wts = dict(wts)
    row = lambda v: v.reshape(1, -1)
    mix_pre, mix_post, ffn_pre, ffn_post = wts["mix_pre_norm"], wts["mix_post_norm"], wts["ffn_pre_norm"], wts["ffn_post_norm"]

    def gathering(stage, fn, *args, **kwargs):
        hook = comm.gather_hook(stage) if comm is not None else None
        if hook is None:
            return fn(*args, **kwargs)
        out, got = fn(*args, hook=hook, **kwargs)
        wts.update(comm.weights_from(stage, got))
        return out

    u0 = _rms_fwd(x, row(mix_pre[0]), name="rms_pre_mix0")
    zx, dt_raw = gathering("in_proj", _matmul, u0, wts["ssd_w_in"], mode="nn", out_dtypes=(BF16,), tm=BIG_TILE, tn=SSD_IN_TILE,
                           f32_block=SSD_DT_COL - (SSD_IN_PAD - SSD_IN_TILE),
                           name="ssd_in_proj")
    xc = gathering("conv", _conv_fwd, zx, wts["ssd_conv_w"], row(wts["ssd_conv_b"]), name="ssd_conv_fwd")
    bias_row = jnp.pad(wts["ssd_dt_bias"], (0, LANES - SSD_N_HEADS)).reshape(1, LANES)
    alog_row = jnp.pad(wts["ssd_a_log"], (0, LANES - SSD_N_HEADS)).reshape(1, LANES)
    dt, cum = _softplus_fwd(dt_raw, bias_row, alog_row, name="ssd_dt_fwd")
    dtr, cumr = _group_rows(dt), _group_rows(cum)
    alog_b, d_b = _head_param_rows(wts["ssd_a_log"]), _head_param_rows(wts["ssd_d"])
    y_ssd, states = gathering("scan", _ssd_fwd, xc, dtr, cumr, alog_b, d_b, name="ssd_scan_fwd")
    norm_w = row(wts["ssd_norm_w"])
    yn = _gate_norm_fwd(y_ssd, zx, norm_w, name="ssd_gate_norm_fwd")
    mix0 = _matmul(yn, wts["ssd_w_out"], mode="nn", out_dtypes=(F32,), name="ssd_out_proj")
    h1, v0 = _rms_fwd(mix0, row(mix_post[0]), resid=x, want_u=row(ffn_pre[0]), name="rms_post_mix0")
    act0, f0 = _mlp_fwd(v0, wts["mlp_w_up0"], wts["mlp_w_down0"], "l0", run=gathering)
    h2, u1 = _rms_fwd(f0, row(ffn_post[0]), resid=h1, want_u=row(mix_pre[1]), name="rms_post_ffn0")

    qkv = _matmul(u1, wts["attn_w_qkv"], mode="nn", out_dtypes=(BF16,), epilogue=_bias_epilogue,
                  extras=((row(wts["attn_b_qkv"]), "row"),), b_shards=True, name="attn_qkv_proj")
    sinks_b = jnp.broadcast_to(wts["attn_sinks"].reshape(ATTN_N_Q, 1), (ATTN_N_Q, LANES))
    ao = gathering("attn_fwd", _attn_fwd, qkv, sinks_b, name="attn_fwd")
    mix1 = _matmul(ao, wts["attn_w_o"], mode="nn", out_dtypes=(F32,), epilogue=_bias_epilogue,
                   extras=((row(wts["attn_b_o"]), "row"),), name="attn_out_proj")
    h3, v1 = _rms_fwd(mix1, row(mix_post[1]), resid=h2, want_u=row(ffn_pre[1]), name="rms_post_mix1")
    act1, f1 = _mlp_fwd(v1, wts["mlp_w_up1"], wts["mlp_w_down1"], "l1")
    dh4, loss_tile = _rms_fwd(f1, row(ffn_post[1]), resid=h3, target=target, name="rms_post_ffn1_loss")

    df1, g_ffn_post1 = _rms_bwd(f1, row(ffn_post[1]), dh4, out_dtype=BF16, name="rms_post_ffn1_bwd")
    dv1, g_up1, g_down1 = _mlp_bwd(v1, act1, wts["mlp_w_up1"], wts["mlp_w_down1"], df1, "l1")
    dh3, g_ffn_pre1 = _rms_bwd(h3, row(ffn_pre[1]), dv1, resid=dh4, name="rms_pre_ffn1_bwd")
    dmix1, g_mix_post1, g_b_o = _rms_bwd(mix1, row(mix_post[1]), dh3, out_dtype=BF16, dx_col_sum=True, name="rms_post_mix1_bwd")
    g_w_o = _matmul(ao, dmix1, mode="tn", out_dtypes=(BF16,), name="attn_dwo")
    dao = _matmul(dmix1, wts["attn_w_o"], mode="nt", out_dtypes=(BF16,), name="attn_dao")
    dq, dkv, g_sinks = _attn_bwd(qkv, sinks_b, dao, name="attn_bwd")
    dqkv = jnp.concatenate([dq, dkv], axis=1)
    g_b_qkv = _col_sum(dqkv, name="attn_bqkv_grad")
    g_w_qkv = _matmul(u1, dqkv, mode="tn", out_dtypes=(BF16,), tn=ATTN_QKV // N_CHIPS, out_shards=True, name="attn_dwqkv")
    du1 = _matmul(dqkv, wts["attn_w_qkv"], mode="nt", out_dtypes=(F32,), b_shards=True, name="attn_du")
    dh2, g_mix_pre1 = _rms_bwd(h2, row(mix_pre[1]), du1, resid=dh3, name="rms_pre_mix1_bwd")

    df0, g_ffn_post0 = _rms_bwd(f0, row(ffn_post[0]), dh2, out_dtype=BF16, name="rms_post_ffn0_bwd")
    dv0, g_up0, g_down0 = _mlp_bwd(v0, act0, wts["mlp_w_up0"], wts["mlp_w_down0"], df0, "l0")
    dh1, g_ffn_pre0 = _rms_bwd(h1, row(ffn_pre[0]), dv0, resid=dh2, name="rms_pre_ffn0_bwd")
    dmix0, g_mix_post0 = _rms_bwd(mix0, row(mix_post[0]), dh1, out_dtype=BF16, name="rms_post_mix0_bwd")
    g_w_out = _matmul(yn, dmix0, mode="tn", out_dtypes=(BF16,), name="ssd_dwout")
    dyn = _matmul(dmix0, wts["ssd_w_out"], mode="nt", out_dtypes=(BF16,), name="ssd_dyn")
    dy_ssd, dzx, g_norm_w = _gate_norm_bwd(y_ssd, zx, norm_w, dyn, name="ssd_gate_norm_bwd")
    mats = {"ssd_w_out": g_w_out, "attn_w_qkv": g_w_qkv, "attn_w_o": g_w_o,
            "mlp_w_up0": g_up0, "mlp_w_up1": g_up1, "mlp_w_down0": g_down0, "mlp_w_down1": g_down1}
    if comm is None:
        dxc, dbm, dcm, ddt_r, dpar = _ssd_bwd(xc, dtr, cumr, alog_b, d_b, states, dy_ssd, name="ssd_scan_bwd")
    else:
        (dxc, dbm, dcm, ddt_r, dpar), received = _ssd_bwd(xc, dtr, cumr, alog_b, d_b, states, dy_ssd,
                                                          name="ssd_scan_bwd", hook=comm.exchange_hook(mats, "early"))
        comm.received(received)
    dzx, g_conv_w, g_conv_b = _conv_bwd(zx, wts["ssd_conv_w"], row(wts["ssd_conv_b"]), dxc, dbm, dcm, dzx, name="ssd_conv_bwd")
    ddt = jnp.pad(jnp.transpose(ddt_r, (2, 0, 1)).reshape(t, SSD_N_HEADS), ((0, 0), (0, LANES - SSD_N_HEADS)))
    dzx, g_dt_bias = _softplus_bwd(dt_raw, bias_row, ddt, dzx, name="ssd_dt_bwd")
    g_w_in = _w_in_to_shards(_matmul(u0, dzx, mode="tn", out_dtypes=(F32,), tn=SSD_IN_TILE, tk=BIG_TILE, name="ssd_dwin"), name="ssd_dwin_shards")
    mats["ssd_w_in"] = g_w_in
    if comm is None:
        du0 = _matmul(dzx, wts["ssd_w_in"], mode="nt", out_dtypes=(F32,), tm=BIG_TILE, tk=SSD_IN_TILE, name="ssd_du")
    else:
        du0, received = _matmul(dzx, wts["ssd_w_in"], mode="nt", out_dtypes=(F32,), tm=BIG_TILE, tk=SSD_IN_TILE, name="ssd_du",
                                hook=comm.exchange_hook(mats, "late"))
        comm.received(received)
    grad_x, g_mix_pre0 = _rms_bwd(x, row(mix_pre[0]), du0, resid=dh1, name="rms_pre_mix0_bwd")

    dpar = dpar.reshape(SSD_N_HEADS, LANES)
    vecs = {
        "ssd_conv_w": g_conv_w, "ssd_conv_b": g_conv_b.reshape(-1),
        "ssd_dt_bias": g_dt_bias[0, :SSD_N_HEADS], "ssd_a_log": dpar[:, 0], "ssd_d": dpar[:, 1],
        "ssd_norm_w": g_norm_w.reshape(-1), "attn_b_qkv": g_b_qkv.reshape(-1), "attn_sinks": g_sinks[:, 0],
        "attn_b_o": g_b_o.reshape(-1),
        "mix_pre_norm": jnp.concatenate([g_mix_pre0, g_mix_pre1]), "mix_post_norm": jnp.concatenate([g_mix_post0, g_mix_post1]),
        "ffn_pre_norm": jnp.concatenate([g_ffn_pre0, g_ffn_pre1]), "ffn_post_norm": jnp.concatenate([g_ffn_post0, g_ffn_post1]),
    }
    return loss_tile, grad_x, mats, vecs


def _mesh_position():
    return lax.axis_index("x"), lax.axis_index("y"), lax.axis_index("c")


def _flip(v, bit):
    return 1 - v if bit else v


OTHER_CHIPS = ((1, 0), (0, 1), (1, 1))


def _comm_params():
    return pltpu.CompilerParams(vmem_limit_bytes=VMEM_LIMIT)


def _staged_copies(srcs, dsts, bufs, sems_in, sems_out):
    loads = [pltpu.make_async_copy(s, b, sems_in.at[i]) for i, (s, b) in enumerate(zip(srcs, bufs))]
    stores = [pltpu.make_async_copy(b, d, sems_out.at[i]) for i, (b, d) in enumerate(zip(bufs, dsts))]
    return loads, stores


class _GatherHook:
    def __init__(self, mats, vecs=()):
        self.arrs = list(mats) + list(vecs)
        self.nm, self.n = len(mats), len(self.arrs)
        n_ici, n_fwd = (N_CHIPS - 1) * self.n, max((N_CHIPS - 1) * self.nm, 1)
        dma = pltpu.SemaphoreType.DMA
        self.out_shape = [jax.ShapeDtypeStruct((N_CHIPS,) + a.shape, a.dtype) for a in self.arrs]
        self.scratch = [pltpu.VMEM(a.shape, a.dtype) for a in self.arrs] + [
            dma((n_ici,)), dma((n_ici,)), dma((n_fwd,)), dma((n_fwd,)), dma((self.n,)), dma((self.n,))]

    def plan(self, ins, outs, scratch):
        n, nm = self.n, self.nm
        bufs = scratch[:n]
        ici_send, ici_recv, fwd_send, fwd_recv, load_sems, store_sems = scratch[n:]
        xi, yi, ci = _mesh_position()
        me = 2 * xi + yi
        loads, stores = _staged_copies(ins, [outs[i].at[me] for i in range(n)], bufs, load_sems, store_sems)
        sends, landed, forwards, from_sibling = [], [], [], []
        for j, (bx, by) in enumerate(OTHER_CHIPS):
            px, py = _flip(xi, bx), _flip(yi, by)
            peer = 2 * px + py
            for i in range(n):
                k = j * n + i
                mk = functools.partial(pltpu.make_async_remote_copy, send_sem=ici_send.at[k], recv_sem=ici_recv.at[k],
                                       device_id=(px, py, ci), device_id_type=MESH)
                if i < nm:
                    sends.append(mk(src_ref=ins[i].at[ci], dst_ref=outs[i].at[me, ci]))
                    landed.append(mk(src_ref=ins[i].at[ci], dst_ref=outs[i].at[peer, ci]))
                    kf = j * nm + i
                    fw = functools.partial(pltpu.make_async_remote_copy, send_sem=fwd_send.at[kf], recv_sem=fwd_recv.at[kf],
                                           device_id=(xi, yi, 1 - ci), device_id_type=MESH)
                    forwards.append(fw(src_ref=outs[i].at[peer, ci], dst_ref=outs[i].at[peer, ci]))
                    from_sibling.append(fw(src_ref=outs[i].at[peer, ci], dst_ref=outs[i].at[peer, 1 - ci]))
                else:
                    sends.append(mk(src_ref=ins[i], dst_ref=outs[i].at[me]))
                    landed.append(mk(src_ref=ins[i], dst_ref=outs[i].at[peer]))
                    forwards.append(None)
        return loads, stores, sends, landed, forwards, from_sibling

    @staticmethod
    def start(p):
        loads, _, sends, _, _, _ = p
        for cp in loads + sends:
            cp.start()

    @staticmethod
    def relay(p):
        loads, stores, _, landed, forwards, _ = p
        for ld, st in zip(loads, stores):
            ld.wait()
            st.start()
        for cp, fw in zip(landed, forwards):
            cp.wait_recv()
            if fw is not None:
                fw.start()

    @staticmethod
    def finish(p):
        _, stores, sends, _, forwards, from_sibling = p
        for cp in from_sibling:
            cp.wait_recv()
        for cp in sends + [fw for fw in forwards if fw is not None]:
            cp.wait_send()
        for st in stores:
            st.wait()


def _run_hook(hook, ins, outs, scratch, step, n_steps):
    p = hook.plan(ins, outs, scratch)
    relay_step = min(max(1, (3 * n_steps) // 4), n_steps - 1)

    @pl.when(step == 0)
    def _():
        hook.start(p)

    if relay_step < n_steps - 1:
        @pl.when(step == relay_step)
        def _():
            hook.relay(p)

    @pl.when(step == n_steps - 1)
    def _():
        if relay_step == n_steps - 1:
            hook.relay(p)
        hook.finish(p)


def _hook_call(hook, *, name):
    n = len(hook.arrs)

    def body(*refs):
        p = hook.plan(refs[:n], refs[n:n + len(hook.out_shape)], refs[n + len(hook.out_shape):])
        hook.start(p)
        hook.relay(p)
        hook.finish(p)

    return pl.pallas_call(
        body, in_specs=[ANY] * n, out_specs=[ANY] * len(hook.out_shape), out_shape=hook.out_shape,
        scratch_shapes=hook.scratch, compiler_params=_comm_params(), name=name)(*hook.arrs)


def _send_other_half(parts, *, name):
    n = len(parts)

    def body(*refs):
        ins, outs = refs[:n], refs[n:2 * n]
        send_sems, recv_sems = refs[2 * n:]
        xi, yi, ci = _mesh_position()
        sibling = (xi, yi, 1 - ci)
        for i in range(n):
            for s in range(N_CHIPS):
                pltpu.make_async_remote_copy(src_ref=ins[i].at[s, 1 - ci], dst_ref=outs[i].at[s], send_sem=send_sems.at[i],
                                             recv_sem=recv_sems.at[i], device_id=sibling, device_id_type=MESH).start()
        for i in range(n):
            pltpu.make_async_remote_copy(src_ref=outs[i], dst_ref=outs[i], send_sem=send_sems.at[i], recv_sem=recv_sems.at[i],
                                         device_id=sibling, device_id_type=MESH).wait()

    return pl.pallas_call(
        body, in_specs=[ANY] * n, out_specs=[ANY] * n,
        out_shape=[jax.ShapeDtypeStruct((p.shape[0],) + p.shape[2:], p.dtype) for p in parts],
        scratch_shapes=[pltpu.SemaphoreType.DMA((n,)), pltpu.SemaphoreType.DMA((n,))],
        name=name)(*parts)


ROW_BLOCKS = 8


def _add_sibling_half(parts, theirs, core, *, name):
    n = len(parts)

    def body(core_ref, *refs):
        for a_ref, b_ref, o_ref in zip(refs[:n], refs[n:2 * n], refs[2 * n:]):
            o_ref[...] = (a_ref[...].astype(F32) + b_ref[...].astype(F32)).astype(o_ref.dtype)

    mine = lambda p: pl.BlockSpec((None, None, p.shape[2] // ROW_BLOCKS, p.shape[3]), lambda s, rb, core_ref: (s, core_ref[0], rb, 0))
    other = lambda p: pl.BlockSpec((None, p.shape[1] // ROW_BLOCKS, p.shape[2]), lambda s, rb, core_ref: (s, rb, 0))
    return pl.pallas_call(
        body,
        grid_spec=pltpu.PrefetchScalarGridSpec(
            num_scalar_prefetch=1, grid=(N_CHIPS, ROW_BLOCKS),
            in_specs=[mine(p) for p in parts] + [other(q) for q in theirs], out_specs=[other(q) for q in theirs]),
        out_shape=[jax.ShapeDtypeStruct(q.shape, BF16) for q in theirs],
        compiler_params=_params("parallel", "parallel"), name=name)(core, *parts, *theirs)


class _ExchangeHook:
    def __init__(self, parts, to_all=()):
        self.arrs = list(parts) + list(to_all)
        self.n_parts, self.n = len(parts), len(self.arrs)
        n_ici, n_peer = max((N_CHIPS - 1) * self.n_parts, 1), (N_DEV - 1) * max(len(to_all), 1)
        dma = pltpu.SemaphoreType.DMA
        self.out_shape = [jax.ShapeDtypeStruct(p.shape, p.dtype) for p in parts] + [
            jax.ShapeDtypeStruct((N_DEV,) + a.shape, a.dtype) for a in to_all]
        self.scratch = [pltpu.VMEM(p.shape[1:], p.dtype) for p in parts] + [pltpu.VMEM(a.shape, a.dtype) for a in to_all] + [
            dma((n_ici,)), dma((n_ici,)), dma((n_peer,)), dma((n_peer,)), dma((self.n,)), dma((self.n,))]

    def plan(self, ins, outs, scratch):
        n, npt = self.n, self.n_parts
        bufs = scratch[:n]
        send_sems, recv_sems, all_send, all_recv, load_sems, store_sems = scratch[n:]
        xi, yi, ci = _mesh_position()
        me_chip = 2 * xi + yi
        me = 4 * xi + 2 * yi + ci
        loads, stores = _staged_copies([ins[i].at[me_chip] for i in range(npt)] + list(ins[npt:]),
                                       [outs[i].at[me_chip] for i in range(npt)] + [outs[i].at[me] for i in range(npt, n)],
                                       bufs, load_sems, store_sems)
        sends, recvs = [], []
        for j, (bx, by) in enumerate(OTHER_CHIPS):
            px, py = _flip(xi, bx), _flip(yi, by)
            peer = 2 * px + py
            for i in range(npt):
                k = j * npt + i
                mk = functools.partial(pltpu.make_async_remote_copy, src_ref=ins[i].at[peer], send_sem=send_sems.at[k],
                                       recv_sem=recv_sems.at[k], device_id=(px, py, ci), device_id_type=MESH)
                sends.append(mk(dst_ref=outs[i].at[me_chip]))
                recvs.append(mk(dst_ref=outs[i].at[peer]))
        for i in range(npt, n):
            for k in range(1, N_DEV):
                px, py, pc = _flip(xi, (k >> 2) & 1), _flip(yi, (k >> 1) & 1), _flip(ci, k & 1)
                slot = (i - npt) * (N_DEV - 1) + k - 1
                mk = functools.partial(pltpu.make_async_remote_copy, src_ref=ins[i], send_sem=all_send.at[slot],
                                       recv_sem=all_recv.at[slot], device_id=(px, py, pc), device_id_type=MESH)
                sends.append(mk(dst_ref=outs[i].at[me]))
                recvs.append(mk(dst_ref=outs[i].at[4 * px + 2 * py + pc]))
        return loads, stores, sends, recvs

    @staticmethod
    def start(p):
        loads, _, sends, _ = p
        for cp in loads + sends:
            cp.start()

    @staticmethod
    def relay(p):
        loads, stores, _, _ = p
        for ld, st in zip(loads, stores):
            ld.wait()
            st.start()

    @staticmethod
    def finish(p):
        _, stores, sends, recvs = p
        for cp in recvs:
            cp.wait_recv()
        for cp in sends:
            cp.wait_send()
        for st in stores:
            st.wait()


def _sum_chips(parts, *, name):
    n = len(parts)
    p = parts[0].shape[0]

    def body(*refs):
        s = pl.program_id(1)
        for x_ref, o_ref in zip(refs[:n], refs[n:]):
            @pl.when(s == 0)
            def _():
                o_ref[...] = x_ref[...].astype(F32)

            @pl.when(s > 0)
            def _():
                o_ref[...] += x_ref[...].astype(F32)

    blocks = lambda q: ROW_BLOCKS if q.shape[1] % (8 * ROW_BLOCKS) == 0 else 1
    assert len({blocks(q) for q in parts}) == 1
    nb = blocks(parts[0])
    return pl.pallas_call(
        body, grid=(nb, p),
        in_specs=[pl.BlockSpec((None, q.shape[1] // nb, q.shape[2]), lambda rb, s: (s, rb, 0)) for q in parts],
        out_specs=[pl.BlockSpec((q.shape[1] // nb, q.shape[2]), lambda rb, s: (rb, 0)) for q in parts],
        out_shape=[jax.ShapeDtypeStruct(q.shape[1:], F32) for q in parts],
        compiler_params=_params("parallel", "arbitrary"), name=name)(*parts)


def _swap_halves(halves, layers, *, name):
    n = len(halves)
    out_shapes, slots = [], []
    for i, h in enumerate(halves):
        pair = [p for p in layers if i in p]
        if pair and pair[0][1] == i:
            slots.append((slots[pair[0][0]][0], 1))
        elif pair:
            out_shapes.append(jax.ShapeDtypeStruct((2, 2) + h.shape, h.dtype))
            slots.append((len(out_shapes) - 1, 0))
        else:
            out_shapes.append(jax.ShapeDtypeStruct((2,) + h.shape, h.dtype))
            slots.append((len(out_shapes) - 1, None))
    n_out = len(out_shapes)

    def body(*refs):
        ins, outs, bufs = refs[:n], refs[n:n + n_out], refs[n + n_out:2 * n + n_out]
        send_sems, recv_sems, load_sems, store_sems = refs[2 * n + n_out:]
        xi, yi, ci = _mesh_position()
        own, sends, recvs = [], [], []
        for i in range(n):
            o, layer = slots[i]
            dst = (lambda core: outs[o].at[core]) if layer is None else (lambda core: outs[o].at[layer, core])
            own.append(dst(ci))
            mk = functools.partial(pltpu.make_async_remote_copy, src_ref=ins[i], send_sem=send_sems.at[i],
                                   recv_sem=recv_sems.at[i], device_id=(xi, yi, 1 - ci), device_id_type=MESH)
            sends.append(mk(dst_ref=dst(ci)))
            recvs.append(mk(dst_ref=dst(1 - ci)))
        loads, stores = _staged_copies(ins, own, bufs, load_sems, store_sems)
        for cp in loads + sends:
            cp.start()
        for ld, st in zip(loads, stores):
            ld.wait()
            st.start()
        for cp in recvs:
            cp.wait_recv()
        for cp in sends:
            cp.wait_send()
        for st in stores:
            st.wait()

    return pl.pallas_call(
        body, in_specs=[ANY] * n, out_specs=[ANY] * n_out, out_shape=out_shapes,
        scratch_shapes=[pltpu.VMEM(h.shape, h.dtype) for h in halves]
        + [pltpu.SemaphoreType.DMA((n,)), pltpu.SemaphoreType.DMA((n,)), pltpu.SemaphoreType.DMA((n,)), pltpu.SemaphoreType.DMA((n,))],
        compiler_params=_comm_params(), name=name)(*halves)


def _cast_bf16(layers, *, name, hook=None):
    n = len(layers)
    hk = _HookSlots(hook, n_in=n, n_out=n, n_scratch=0)

    def body(*refs):
        ins, outs, _ = hk.own(refs)
        if hook is not None:
            hk.run(refs, pl.program_id(0), ROW_BLOCKS)
        for i_ref, o_ref in zip(ins, outs):
            o_ref[...] = i_ref[...].astype(o_ref.dtype)

    in_blk = lambda a, l: pl.BlockSpec((None, a.shape[1] // ROW_BLOCKS, a.shape[2]), lambda i: (l, i, 0))
    out_blk = lambda a: pl.BlockSpec((a.shape[1] // ROW_BLOCKS, a.shape[2]), lambda i: (i, 0))
    outs = pl.pallas_call(
        body, grid=(ROW_BLOCKS,),
        in_specs=[in_blk(a, l) for a, l in layers] + hk.in_specs,
        out_specs=[out_blk(a) for a, _ in layers] + hk.out_specs,
        out_shape=[jax.ShapeDtypeStruct(a.shape[1:], BF16) for a, _ in layers] + hk.out_shape,
        scratch_shapes=hk.scratch,
        compiler_params=_params(*hk.semantics("parallel")), name=name)(*[a for a, _ in layers], *hk.inputs)
    return outs[:n] if hook is None else (outs[:n], outs[n:])


def _full_weight(name, gathered):
    s, _, r, c = gathered.shape
    if name == "ssd_w_in":
        return _w_in_from_shards(gathered.reshape(s, 2 * r, c), name="ssd_w_in_unshard")
    if name in ("attn_w_qkv", "mlp_w_up0", "mlp_w_up1"):
        return gathered.reshape(s, 2 * r, c)
    return gathered.reshape(s * 2 * r, c)


class _StepComm:
    GATHER = {"in_proj": ("mlp_w_up0", "attn_w_o"), "conv": ("mlp_w_down0",), "scan": ("ssd_w_out", "mlp_w_up1"),
              "mlp_up_l0": ("attn_w_qkv",), "attn_fwd": ("mlp_w_down1",)}
    EXCHANGE = {"early": ("ssd_w_out", "attn_w_qkv", "attn_w_o", "mlp_w_up0", "mlp_w_up1", "mlp_w_down0", "mlp_w_down1"),
                "late": ("ssd_w_in",)}

    def __init__(self, shards, core):
        self.shards, self.core = shards, core
        self.chip_parts = {}
        self._pending = None

    def gather_hook(self, stage):
        names = self.GATHER.get(stage)
        return _GatherHook([self.shards[n] for n in names]) if names else None

    def weights_from(self, stage, gathered):
        return {n: _full_weight(n, g) for n, g in zip(self.GATHER[stage], gathered)}

    def chip_sums(self, mats, tag):
        parts = [_shard_halves(a) for a in mats.values()]
        theirs = _send_other_half(parts, name=f"grad_sibling_send_{tag}")
        return _add_sibling_half(parts, theirs, self.core, name=f"grad_chip_sum_{tag}")

    def exchange_hook(self, mats, which):
        self._pending = self.EXCHANGE[which]
        return _ExchangeHook(self.chip_sums({n: mats[n] for n in self._pending}, which))

    def received(self, arrays):
        self.chip_parts.update(zip(self._pending, arrays))


ADAMW_ROW_BLOCKS = 16


def _adamw(ws, gs, ms, vs, *, name):
    n = len(ws)
    nb = ADAMW_ROW_BLOCKS if all(a.shape[0] % (8 * ADAMW_ROW_BLOCKS) == 0 for a in ws) else 1

    def body(*refs):
        ins, outs = refs[:4 * n], refs[4 * n:]
        for i in range(n):
            w_ref, g_ref, m_ref, v_ref = ins[i], ins[n + i], ins[2 * n + i], ins[3 * n + i]
            go_ref, d_ref, nm_ref, nv_ref = outs[i], outs[n + i], outs[2 * n + i], outs[3 * n + i]
            gv = g_ref[...]
            nm = ADAM_B1 * m_ref[...] + (1.0 - ADAM_B1) * gv
            nv = ADAM_B2 * v_ref[...] + (1.0 - ADAM_B2) * (gv * gv)
            m_hat = nm / (1.0 - ADAM_B1 ** ADAM_STEP)
            v_hat = nv / (1.0 - ADAM_B2 ** ADAM_STEP)
            go_ref[...] = gv
            d_ref[...] = -ADAM_LR * (m_hat / (jnp.sqrt(v_hat) + ADAM_EPS) + ADAM_WD * w_ref[...])
            nm_ref[...] = nm
            nv_ref[...] = nv

    blks = [pl.BlockSpec((a.shape[0] // nb, a.shape[1]), lambda i: (i, 0)) for a in ws]
    shapes = [jax.ShapeDtypeStruct(a.shape, F32) for a in ws]
    outs = pl.pallas_call(body, grid=(nb,), in_specs=blks * 4, out_specs=blks * 4, out_shape=shapes * 4,
                          compiler_params=_params("parallel"), name=name)(*ws, *gs, *ms, *vs)
    return [tuple(outs[k * n + i] for k in range(4)) for i in range(n)]


SM_CONV_B, SM_NORM_W, SM_MIX_PRE, SM_MIX_POST, SM_FFN_PRE, SM_FFN_POST, SM_MISC, SM_CONV_W, SM_B_QKV, SM_B_O = 0, 4, 6, 8, 10, 12, 14, 16, 32, 34
SM_ROWS = 40
MISC_DT_BIAS, MISC_A_LOG, MISC_D, MISC_SINKS, MISC_LOSS = 0, 32, 64, 96, 112


def _shard_halves(a):
    c = a.shape[-1]
    return a.reshape(N_CHIPS, 2, -1, c)


def _rows(v):
    return v.reshape(-1, D_MODEL)


def _misc_row(dt_bias, a_log, d, sinks, loss):
    pad = jnp.zeros((D_MODEL - MISC_LOSS - 1,), F32)
    return jnp.concatenate([dt_bias.reshape(-1), a_log.reshape(-1), d.reshape(-1), sinks.reshape(-1), loss.reshape(1), pad]).reshape(1, D_MODEL)


def _replicated_rows(p, loss):
    return jnp.concatenate([
        _rows(p["ssd_conv_b"]), _rows(p["ssd_norm_w"]), _rows(p["mix_pre_norm"]), _rows(p["mix_post_norm"]),
        _rows(p["ffn_pre_norm"]), _rows(p["ffn_post_norm"]),
        _misc_row(p["ssd_dt_bias"], p["ssd_a_log"], p["ssd_d"], p["attn_sinks"], loss), jnp.zeros((1, D_MODEL), F32)], axis=0)


def _sharded_rows(conv_w, b_qkv, b_o):
    last = jnp.concatenate([b_qkv.reshape(-1), b_o.reshape(-1), jnp.zeros((D_MODEL - 640,), F32)]).reshape(1, D_MODEL)
    return jnp.concatenate([conv_w.reshape(SSD_CONV_WIDTH, D_MODEL), last, jnp.zeros((3, D_MODEL), F32)], axis=0)


REPLICATED = ("ssd_conv_b", "ssd_dt_bias", "ssd_a_log", "ssd_d", "ssd_norm_w", "attn_sinks",
              "mix_pre_norm", "mix_post_norm", "ffn_pre_norm", "ffn_post_norm")
MATRICES = ("ssd_w_in", "ssd_w_out", "attn_w_qkv", "attn_w_o", "mlp_w_up", "mlp_w_down")
WEIGHT_NAMES = ("ssd_w_in", "ssd_conv_w", "ssd_conv_b", "ssd_dt_bias", "ssd_a_log", "ssd_d", "ssd_norm_w", "ssd_w_out",
                "attn_w_qkv", "attn_b_qkv", "attn_sinks", "attn_w_o", "attn_b_o", "mlp_w_up", "mlp_w_down",
                "mix_pre_norm", "mix_post_norm", "ffn_pre_norm", "ffn_post_norm")


def _unpack_small(rows16, rows8, like):
    misc = rows16[SM_MISC]
    out = {
        "ssd_conv_b": rows16[SM_CONV_B:SM_CONV_B + 4], "ssd_norm_w": rows16[SM_NORM_W:SM_NORM_W + 2],
        "mix_pre_norm": rows16[SM_MIX_PRE:SM_MIX_PRE + 2], "mix_post_norm": rows16[SM_MIX_POST:SM_MIX_POST + 2],
        "ffn_pre_norm": rows16[SM_FFN_PRE:SM_FFN_PRE + 2], "ffn_post_norm": rows16[SM_FFN_POST:SM_FFN_POST + 2],
        "ssd_dt_bias": misc[MISC_DT_BIAS:MISC_DT_BIAS + 32], "ssd_a_log": misc[MISC_A_LOG:MISC_A_LOG + 32],
        "ssd_d": misc[MISC_D:MISC_D + 32], "attn_sinks": misc[MISC_SINKS:MISC_SINKS + 16],
        "ssd_conv_w": rows8[0:SSD_CONV_WIDTH], "attn_b_qkv": rows8[SSD_CONV_WIDTH, 0:384], "attn_b_o": rows8[SSD_CONV_WIDTH, 384:640],
    }
    return {k: v.reshape(like[k].shape) for k, v in out.items()}


def kernel(x, ssd_w_in, ssd_conv_w, ssd_conv_b, ssd_dt_bias, ssd_a_log, ssd_d, ssd_norm_w, ssd_w_out, attn_w_qkv, attn_b_qkv, attn_sinks, attn_w_o, attn_b_o, mlp_w_up, mlp_w_down, mix_pre_norm, mix_post_norm, ffn_pre_norm, ffn_post_norm, loss_target, m_ssd_w_in, m_ssd_conv_w, m_ssd_conv_b, m_ssd_dt_bias, m_ssd_a_log, m_ssd_d, m_ssd_norm_w, m_ssd_w_out, m_attn_w_qkv, m_attn_b_qkv, m_attn_sinks, m_attn_w_o, m_attn_b_o, m_mlp_w_up, m_mlp_w_down, m_mix_pre_norm, m_mix_post_norm, m_ffn_pre_norm, m_ffn_post_norm, v_ssd_w_in, v_ssd_conv_w, v_ssd_conv_b, v_ssd_dt_bias, v_ssd_a_log, v_ssd_d, v_ssd_norm_w, v_ssd_w_out, v_attn_w_qkv, v_attn_b_qkv, v_attn_sinks, v_attn_w_o, v_attn_b_o, v_mlp_w_up, v_mlp_w_down, v_mix_pre_norm, v_mix_post_norm, v_ffn_pre_norm, v_ffn_post_norm):
    w = dict(zip(WEIGHT_NAMES, (ssd_w_in, ssd_conv_w, ssd_conv_b, ssd_dt_bias, ssd_a_log, ssd_d, ssd_norm_w, ssd_w_out, attn_w_qkv, attn_b_qkv, attn_sinks, attn_w_o, attn_b_o, mlp_w_up, mlp_w_down, mix_pre_norm, mix_post_norm, ffn_pre_norm, ffn_post_norm)))
    m = dict(zip(WEIGHT_NAMES, (m_ssd_w_in, m_ssd_conv_w, m_ssd_conv_b, m_ssd_dt_bias, m_ssd_a_log, m_ssd_d, m_ssd_norm_w, m_ssd_w_out, m_attn_w_qkv, m_attn_b_qkv, m_attn_sinks, m_attn_w_o, m_attn_b_o, m_mlp_w_up, m_mlp_w_down, m_mix_pre_norm, m_mix_post_norm, m_ffn_pre_norm, m_ffn_post_norm)))
    v = dict(zip(WEIGHT_NAMES, (v_ssd_w_in, v_ssd_conv_w, v_ssd_conv_b, v_ssd_dt_bias, v_ssd_a_log, v_ssd_d, v_ssd_norm_w, v_ssd_w_out, v_attn_w_qkv, v_attn_b_qkv, v_attn_sinks, v_attn_w_o, v_attn_b_o, v_mlp_w_up, v_mlp_w_down, v_mix_pre_norm, v_mix_post_norm, v_ffn_pre_norm, v_ffn_post_norm)))
    chip = 2 * lax.axis_index("x") + lax.axis_index("y")

    two_halves = lambda a: a.reshape(2, a.shape[0] // 2, a.shape[1])
    later = {"ssd_w_out": (w["ssd_w_out"], 0), "attn_w_qkv": (w["attn_w_qkv"], 0), "attn_w_o": (w["attn_w_o"], 0),
             "mlp_w_up0": (w["mlp_w_up"], 0), "mlp_w_up1": (w["mlp_w_up"], 1),
             "mlp_w_down0": (w["mlp_w_down"], 0), "mlp_w_down1": (w["mlp_w_down"], 1)}
    first = _GatherHook([two_halves(w["ssd_w_in"][0].astype(BF16))], [w["ssd_conv_w"][0], w["attn_b_qkv"], w["attn_b_o"]])
    cast, (g_in, g_conv, g_bqkv, g_bo) = _cast_bf16(list(later.values()), name="weights_to_bf16", hook=first)
    core = lax.axis_index("c").astype(jnp.int32).reshape(1)
    comm = _StepComm({k: two_halves(a) for k, a in zip(later, cast)}, core)
    full = {
        "ssd_w_in": _full_weight("ssd_w_in", g_in),
        "ssd_conv_w": g_conv.transpose(1, 0, 2).reshape(SSD_CONV_WIDTH, SSD_CONV_DIM),
        "attn_b_qkv": g_bqkv.reshape(ATTN_QKV), "attn_b_o": g_bo.reshape(D_MODEL),
    }
    for name in REPLICATED:
        full[name] = w[name][0] if name.startswith(("ssd_", "attn_")) else w[name]

    loss_tile, grad_x, gm, g = _local_step(x[0], loss_target[0], full, comm)

    conv_w_rows = g["ssd_conv_w"].reshape(SSD_CONV_WIDTH * N_CHIPS, D_MODEL)
    b_qkv_rows = jnp.pad(g["attn_b_qkv"], (0, 2 * D_MODEL - ATTN_QKV)).reshape(2, D_MODEL)
    small = jnp.concatenate([_replicated_rows(g, loss_tile[0, 0]), conv_w_rows, b_qkv_rows, _rows(g["attn_b_o"]),
                             jnp.zeros((SM_ROWS - SM_B_O - 1, D_MODEL), F32)], axis=0)
    small_all, = _hook_call(_ExchangeHook([], [small]), name="vector_grad_all_gather")
    order = ("ssd_w_in", "ssd_w_out", "attn_w_qkv", "attn_w_o", "mlp_w_up0", "mlp_w_up1", "mlp_w_down0", "mlp_w_down1")
    halves = _sum_chips([comm.chip_parts[k] for k in order], name="grad_sum")
    r_in, r_out, r_qkv, r_o, r_up, r_down = _swap_halves(halves, layers=((4, 5), (6, 7)), name="grad_halves_swap")
    small_sum, = _sum_chips([small_all], name="small_grad_sum")

    grads = {"ssd_w_in": r_in, "ssd_w_out": r_out, "attn_w_qkv": r_qkv, "attn_w_o": r_o, "mlp_w_up": r_up, "mlp_w_down": r_down}
    grads = {k: a.reshape(w[k].shape) for k, a in grads.items()}
    conv_w_g = lax.dynamic_index_in_dim(small_sum[SM_CONV_W:SM_CONV_W + 16].reshape(SSD_CONV_WIDTH, N_CHIPS, D_MODEL), chip, axis=1, keepdims=False)
    b_qkv_g = lax.dynamic_slice_in_dim(small_sum[SM_B_QKV:SM_B_QKV + 2].reshape(-1), chip * 384, 384)
    b_o_g = lax.dynamic_slice_in_dim(small_sum[SM_B_O], chip * 256, 256)
    small_g = jnp.concatenate([small_sum[0:16], _sharded_rows(conv_w_g, b_qkv_g, b_o_g)], axis=0)
    grads.update(_unpack_small(small_g[0:16], small_g[16:24], w))
    loss = small_sum[SM_MISC, MISC_LOSS]

    delta, new_m, new_v = {}, {}, {}
    as2d = lambda p: [p[name].reshape(-1, p[name].shape[-1]) for name in MATRICES]
    for name, (g2, d2, m2, v2) in zip(MATRICES, _adamw(as2d(w), as2d(grads), as2d(m), as2d(v), name="adamw_matrices")):
        shape = w[name].shape
        grads[name], delta[name], new_m[name], new_v[name] = g2.reshape(shape), d2.reshape(shape), m2.reshape(shape), v2.reshape(shape)
    zero = jnp.zeros((), F32)
    small_pack = lambda p: jnp.concatenate([_replicated_rows({k: p[k] for k in REPLICATED}, zero),
                                            _sharded_rows(p["ssd_conv_w"], p["attn_b_qkv"], p["attn_b_o"])], axis=0)
    (_, d_s, m_s, v_s), = _adamw([small_pack(w)], [small_g], [small_pack(m)], [small_pack(v)], name="adamw_vectors")
    delta.update(_unpack_small(d_s[0:16], d_s[16:24], w))
    new_m.update(_unpack_small(m_s[0:16], m_s[16:24], w))
    new_v.update(_unpack_small(v_s[0:16], v_s[16:24], w))

    return (loss, grad_x[None], *[grads[n] for n in WEIGHT_NAMES], *[delta[n] for n in WEIGHT_NAMES],
            *[new_m[n] for n in WEIGHT_NAMES], *[new_v[n] for n in WEIGHT_NAMES])
```

```python
import functools
import math

import jax
import jax.numpy as jnp
from jax import lax
from jax.experimental import pallas as pl
from jax.experimental.pallas import tpu as pltpu

F32 = jnp.float32
BF16 = jnp.bfloat16

D_MODEL = 1024
SSD_D_INNER = 2048
SSD_HEAD_DIM = 64
SSD_N_HEADS = 32
SSD_N_GROUPS = 8
SSD_HPG = 4
SSD_D_STATE = 128
SSD_CONV_WIDTH = 4
SSD_CHUNK = 128
SSD_CONV_DIM = 4096
SSD_IN_DIM = 6176
SSD_IN_PAD = 6400
SSD_IN_TILE = 1280
SSD_DT_COL = 6144
SSD_GW = SSD_HPG * SSD_HEAD_DIM
ATTN_HEAD_DIM = 64
ATTN_N_Q = 16
ATTN_N_KV = 4
ATTN_REP = 4
ATTN_WINDOW = 128
ATTN_QKV = 1536
D_FF = 4096
NORM_EPS = 1e-6

ADAM_LR = 0.001
ADAM_B1 = 0.9
ADAM_B2 = 0.999
ADAM_EPS = 1e-08
ADAM_WD = 0.01
ADAM_STEP = 10

N_CHIPS = 4
N_DEV = 8
LANES = 128
VMEM_LIMIT = 48 * 1024 * 1024
BIG_TILE = 2048

MESH = pl.DeviceIdType.MESH


def _params(*sem):
    return pltpu.CompilerParams(dimension_semantics=sem, vmem_limit_bytes=VMEM_LIMIT)


def _dot(a, b, dims):
    return lax.dot_general(a, b, (dims, ((), ())), preferred_element_type=F32)


def _dot_nn(a, b):
    return _dot(a, b, ((1,), (0,)))


def _dot_nt(a, b):
    return _dot(a, b, ((1,), (1,)))


def _dot_tn(a, b):
    return _dot(a, b, ((0,), (0,)))


def _sigmoid(x):
    return 0.5 * jnp.tanh(0.5 * x) + 0.5


ANY = pl.BlockSpec(memory_space=pl.ANY)


class _HookSlots:
    def __init__(self, hook, n_in, n_out, n_scratch):
        self.hook = hook
        self.n_in, self.n_out, self.n_scratch = n_in, n_out, n_scratch
        self.inputs = list(hook.arrs) if hook else []
        self.out_shape = list(hook.out_shape) if hook else []
        self.scratch = list(hook.scratch) if hook else []
        self.in_specs = [ANY] * len(self.inputs)
        self.out_specs = [ANY] * len(self.out_shape)

    def _split(self, refs):
        a = self.n_in
        b = a + len(self.inputs)
        c = b + self.n_out
        d = c + len(self.out_shape)
        e = d + self.n_scratch
        return refs[:a], refs[a:b], refs[b:c], refs[c:d], refs[d:e], refs[e:]

    def own(self, refs):
        ins, _, outs, _, scratch, _ = self._split(refs)
        return ins, outs, scratch

    def run(self, refs, step, n_steps):
        _, h_in, _, h_out, _, h_scratch = self._split(refs)
        _run_hook(self.hook, h_in, h_out, h_scratch, step, n_steps)

    def semantics(self, *sem):
        return sem if self.hook is None else ("arbitrary",) * len(sem)


def _matmul(a, b, *, mode, out_dtypes, name, epilogue=None, extras=(), tm=1024, tn=1024, tk=1024,
            b_shards=False, out_shards=False, hook=None, f32_block=None):
    f32_tail = f32_block is not None
    if b_shards:
        s, b_rows, b_cols = b.shape
        b2 = (b_rows, s * b_cols)
        if mode == "nn":
            tn = b_cols
        else:
            assert mode == "nt"
            tk = b_cols
    else:
        b2 = b.shape
    if mode == "nn":
        (m, k), (k2, n) = a.shape, b2
    elif mode == "nt":
        (m, k), (n, k2) = a.shape, b2
    else:
        (k, m), (k2, n) = a.shape, b2
    assert k == k2, (a.shape, b.shape, mode)
    tm, tn, tk = min(tm, m), min(tn, n), min(tk, k)
    assert m % tm == 0 and n % tn == 0 and k % tk == 0, (m, n, k, tm, tn, tk)
    nk = k // tk
    if mode == "tn":
        a_spec = pl.BlockSpec((tk, tm), lambda i, j, kk: (kk, i))
    else:
        a_spec = pl.BlockSpec((tm, tk), lambda i, j, kk: (i, kk))
    if b_shards and mode == "nn":
        b_spec = pl.BlockSpec((None, tk, tn), lambda i, j, kk: (j, kk, 0))
    elif b_shards:
        b_spec = pl.BlockSpec((None, tn, tk), lambda i, j, kk: (kk, j, 0))
    elif mode == "nt":
        b_spec = pl.BlockSpec((tn, tk), lambda i, j, kk: (j, kk))
    else:
        b_spec = pl.BlockSpec((tk, tn), lambda i, j, kk: (kk, j))
    dims = {"nn": ((1,), (0,)), "nt": ((1,), (1,)), "tn": ((0,), (0,))}[mode]
    ex_specs = []
    for arr, kind in extras:
        if kind == "tile":
            ex_specs.append(pl.BlockSpec((tm, tn), lambda i, j, kk: (i, j)))
        else:
            ex_specs.append(pl.BlockSpec((1, tn), lambda i, j, kk: (0, j)))
    n_ex, n_out = len(extras), len(out_dtypes)
    if epilogue is None:
        epilogue = lambda acc: (acc,)
    hk = _HookSlots(hook, n_in=2 + n_ex, n_out=n_out + f32_tail, n_scratch=0 if nk == 1 else 1)
    grid = (m // tm, n // tn, nk)

    def body(*refs):
        (a_ref, b_ref, *ex), outs, scratch = hk.own(refs)
        if hook is not None:
            step = (pl.program_id(0) * grid[1] + pl.program_id(1)) * grid[2] + pl.program_id(2)
            hk.run(refs, step, grid[0] * grid[1] * grid[2])

        def finish(acc):
            res = epilogue(acc, *[e[...] for e in ex])
            for o, r in zip(outs, res):
                o[...] = r.astype(o.dtype)
            if f32_tail:
                outs[n_out][...] = acc[:, f32_block:f32_block + LANES]

        if nk == 1:
            finish(_dot(a_ref[...], b_ref[...], dims))
        else:
            acc_ref = scratch[0]
            kk = pl.program_id(2)

            @pl.when(kk == 0)
            def _():
                acc_ref[...] = jnp.zeros_like(acc_ref)

            acc_ref[...] += _dot(a_ref[...], b_ref[...], dims)

            @pl.when(kk == nk - 1)
            def _():
                finish(acc_ref[...])

    if out_shards:
        out_spec = pl.BlockSpec((None, tm, tn), lambda i, j, kk: (j, i, 0))
        out_dims = (n // tn, m, tn)
    else:
        out_spec = pl.BlockSpec((tm, tn), lambda i, j, kk: (i, j))
        out_dims = (m, n)
    tail_specs = [pl.BlockSpec((tm, LANES), lambda i, j, kk: (i, 0))] if f32_tail else []
    tail_shapes = [jax.ShapeDtypeStruct((m, LANES), F32)] if f32_tail else []
    outs = pl.pallas_call(
        body,
        grid=grid,
        in_specs=[a_spec, b_spec] + ex_specs + hk.in_specs,
        out_specs=[out_spec for _ in out_dtypes] + tail_specs + hk.out_specs,
        out_shape=[jax.ShapeDtypeStruct(out_dims, dt) for dt in out_dtypes] + tail_shapes + hk.out_shape,
        scratch_shapes=([] if nk == 1 else [pltpu.VMEM((tm, tn), F32)]) + hk.scratch,
        compiler_params=_params(*hk.semantics("parallel", "arbitrary" if f32_tail else "parallel", "arbitrary")),
        name=name,
    )(a, b, *[arr for arr, _ in extras], *hk.inputs)
    n_own = n_out + f32_tail
    own = outs[0] if n_own == 1 else outs[:n_own]
    return own if hook is None else (own, outs[n_own:])


def _row_tile(t, want):
    return min(t, want)


def _rms_fwd(x, w, *, name, resid=None, want_u=None, target=None):
    t, d = x.shape
    tr = _row_tile(t, 512)

    def norm(v, wv):
        return v * lax.rsqrt(jnp.mean(v * v, axis=-1, keepdims=True) + NORM_EPS) * wv

    row = pl.BlockSpec((tr, d), lambda i: (i, 0))
    vec = pl.BlockSpec((1, d), lambda i: (0, 0))
    if target is not None:
        def body(x_ref, w_ref, r_ref, t_ref, dh_ref, loss_ref):
            err = r_ref[...] + norm(x_ref[...], w_ref[...]) - t_ref[...]
            dh_ref[...] = err * (1.0 / d)

            @pl.when(pl.program_id(0) == 0)
            def _():
                loss_ref[...] = jnp.zeros_like(loss_ref)

            part = jnp.sum(jnp.sum(err * err, axis=1, keepdims=True), axis=0, keepdims=True) * (0.5 / d)
            loss_ref[...] += jnp.broadcast_to(part, loss_ref.shape)

        return pl.pallas_call(
            body, grid=(t // tr,), in_specs=[row, vec, row, row],
            out_specs=[row, pl.BlockSpec((8, LANES), lambda i: (0, 0))],
            out_shape=[jax.ShapeDtypeStruct((t, d), F32), jax.ShapeDtypeStruct((8, LANES), F32)],
            compiler_params=_params("arbitrary"), name=name)(x, w, resid, target)
    if resid is None:
        def body(x_ref, w_ref, o_ref):
            o_ref[...] = norm(x_ref[...], w_ref[...]).astype(BF16)
        ins, in_specs = (x, w), [row, vec]
        out_shape, out_specs = jax.ShapeDtypeStruct((t, d), BF16), row
    elif want_u is None:
        def body(x_ref, w_ref, r_ref, o_ref):
            o_ref[...] = r_ref[...] + norm(x_ref[...], w_ref[...])
        ins, in_specs = (x, w, resid), [row, vec, row]
        out_shape, out_specs = jax.ShapeDtypeStruct((t, d), F32), row
    else:
        def body(x_ref, w_ref, r_ref, w2_ref, o_ref, u_ref):
            h = r_ref[...] + norm(x_ref[...], w_ref[...])
            o_ref[...] = h
            u_ref[...] = norm(h, w2_ref[...]).astype(BF16)
        ins, in_specs = (x, w, resid, want_u), [row, vec, row, vec]
        out_shape = [jax.ShapeDtypeStruct((t, d), F32), jax.ShapeDtypeStruct((t, d), BF16)]
        out_specs = [row, row]
    return pl.pallas_call(body, grid=(t // tr,), in_specs=in_specs, out_specs=out_specs, out_shape=out_shape,
                          compiler_params=_params("parallel"), name=name)(*ins)


def _rms_bwd(x, w, dy, *, name, resid=None, out_dtype=F32, dx_col_sum=False):
    t, d = x.shape
    tr = _row_tile(t, 512)
    row = pl.BlockSpec((tr, d), lambda i: (i, 0))
    vec = pl.BlockSpec((1, d), lambda i: (0, 0))
    has_res = resid is not None

    def body(x_ref, w_ref, dy_ref, *rest):
        r_ref = rest[0] if has_res else None
        dx_ref, dw_ref = rest[has_res:has_res + 2]
        xv = x_ref[...]
        dyv = dy_ref[...].astype(F32)
        r = lax.rsqrt(jnp.mean(xv * xv, axis=-1, keepdims=True) + NORM_EPS)
        xhat = xv * r
        dyw = dyv * w_ref[...]
        dx = r * (dyw - xhat * jnp.mean(dyw * xhat, axis=-1, keepdims=True))
        if has_res:
            dx = dx + r_ref[...]
        dx_ref[...] = dx.astype(dx_ref.dtype)

        sums = [(dw_ref, dyv * xhat)] + ([(rest[-1], dx)] if dx_col_sum else [])

        @pl.when(pl.program_id(0) == 0)
        def _():
            for acc_ref, _ in sums:
                acc_ref[...] = jnp.zeros_like(acc_ref)

        for acc_ref, rows in sums:
            acc_ref[...] += jnp.sum(rows, axis=0, keepdims=True)

    ins = (x, w, dy) + ((resid,) if has_res else ())
    in_specs = [row, vec, row] + ([row] if has_res else [])
    n_vec = 2 if dx_col_sum else 1
    return pl.pallas_call(
        body, grid=(t // tr,), in_specs=in_specs, out_specs=[row] + [vec] * n_vec,
        out_shape=[jax.ShapeDtypeStruct((t, d), out_dtype)] + [jax.ShapeDtypeStruct((1, d), F32)] * n_vec,
        compiler_params=_params("arbitrary"), name=name)(*ins)


def _col_sum(x, *, name):
    t, n = x.shape
    tr = _row_tile(t, 512)

    def body(x_ref, o_ref):
        @pl.when(pl.program_id(0) == 0)
        def _():
            o_ref[...] = jnp.zeros_like(o_ref)

        o_ref[...] += jnp.sum(x_ref[...].astype(F32), axis=0, keepdims=True)

    return pl.pallas_call(
        body, grid=(t // tr,), in_specs=[pl.BlockSpec((tr, n), lambda i: (i, 0))],
        out_specs=pl.BlockSpec((1, n), lambda i: (0, 0)), out_shape=jax.ShapeDtypeStruct((1, n), F32),
        compiler_params=_params("arbitrary"), name=name)(x)


SSD_IN_SHARD = SSD_IN_DIM // N_CHIPS


def _w_in_from_shards(shards, *, name):
    d = shards.shape[1]
    tr = 256

    def body(s_ref, o_ref):
        o_ref[:, pl.ds(SSD_DT_COL, SSD_IN_PAD - SSD_DT_COL)] = jnp.zeros((tr, SSD_IN_PAD - SSD_DT_COL), o_ref.dtype)
        for s in range(N_CHIPS):
            o_ref[:, pl.ds(SSD_IN_SHARD * s, SSD_IN_SHARD)] = s_ref[s]

    return pl.pallas_call(
        body, grid=(d // tr,), in_specs=[pl.BlockSpec((N_CHIPS, tr, SSD_IN_SHARD), lambda i: (0, i, 0))],
        out_specs=pl.BlockSpec((tr, SSD_IN_PAD), lambda i: (i, 0)),
        out_shape=jax.ShapeDtypeStruct((d, SSD_IN_PAD), shards.dtype),
        compiler_params=_params("parallel"), name=name)(shards)


def _w_in_to_shards(g, *, name):
    d = g.shape[0]
    tr = 256

    def body(g_ref, o_ref):
        for s in range(N_CHIPS):
            o_ref[s] = g_ref[:, pl.ds(SSD_IN_SHARD * s, SSD_IN_SHARD)].astype(o_ref.dtype)

    return pl.pallas_call(
        body, grid=(d // tr,), in_specs=[pl.BlockSpec((tr, SSD_IN_PAD), lambda i: (i, 0))],
        out_specs=pl.BlockSpec((N_CHIPS, tr, SSD_IN_SHARD), lambda i: (0, i, 0)),
        out_shape=jax.ShapeDtypeStruct((N_CHIPS, d, SSD_IN_SHARD), BF16),
        compiler_params=_params("parallel"), name=name)(g)


XBC_COL0 = SSD_D_INNER // LANES


def _shift_down(v, k, row_ids):
    return jnp.where(row_ids >= k, pltpu.roll(v, k, axis=0), 0.0)


def _shift_up(v, k, row_ids):
    n = v.shape[0]
    return jnp.where(row_ids < n - k, pltpu.roll(v, n - k, axis=0), 0.0)


def _conv_pre(x, w, b, row_ids):
    pre = b + w[3:4, :] * x
    for k in (1, 2, 3):
        pre = pre + w[3 - k:4 - k, :] * _shift_down(x, k, row_ids)
    return pre


def _conv_fwd(zx, conv_w, conv_b, *, name, hook=None):
    t = zx.shape[0]
    nct = SSD_CONV_DIM // LANES
    hk = _HookSlots(hook, n_in=3, n_out=1, n_scratch=0)

    def body(*refs):
        (x_ref, w_ref, b_ref), (o_ref,), _ = hk.own(refs)
        if hook is not None:
            hk.run(refs, pl.program_id(0), nct)
        x = x_ref[...].astype(F32)
        row_ids = lax.broadcasted_iota(jnp.int32, x.shape, 0)
        pre = _conv_pre(x, w_ref[...], b_ref[...], row_ids)
        o_ref[...] = pre * _sigmoid(pre)

    outs = pl.pallas_call(
        body, grid=(nct,),
        in_specs=[pl.BlockSpec((t, LANES), lambda j: (0, XBC_COL0 + j)),
                  pl.BlockSpec((SSD_CONV_WIDTH, LANES), lambda j: (0, j)),
                  pl.BlockSpec((1, LANES), lambda j: (0, j))] + hk.in_specs,
        out_specs=[pl.BlockSpec((t, LANES), lambda j: (0, j))] + hk.out_specs,
        out_shape=[jax.ShapeDtypeStruct((t, SSD_CONV_DIM), F32)] + hk.out_shape,
        scratch_shapes=hk.scratch,
        compiler_params=_params(*hk.semantics("parallel")), name=name)(zx, conv_w, conv_b, *hk.inputs)
    return outs[0] if hook is None else (outs[0], outs[1:])


def _conv_bwd(zx, conv_w, conv_b, d_xs, d_bm, d_cm, dzx, *, name):
    t = zx.shape[0]
    nct = SSD_CONV_DIM // LANES
    n_xs = SSD_D_INNER // LANES
    n_bm = SSD_N_GROUPS * SSD_D_STATE // LANES

    def body(x_ref, w_ref, b_ref, dxs_ref, dbm_ref, dcm_ref, _, dx_ref, dw_ref, db_ref):
        x = x_ref[...].astype(F32)
        w = w_ref[...]
        j = pl.program_id(0)
        dy = jnp.where(j < n_xs, dxs_ref[...], jnp.where(j < n_xs + n_bm, dbm_ref[...], dcm_ref[...]))
        row_ids = lax.broadcasted_iota(jnp.int32, x.shape, 0)
        pre = _conv_pre(x, w, b_ref[...], row_ids)
        sg = _sigmoid(pre)
        dpre = dy * (sg * (1.0 + pre * (1.0 - sg)))
        dx = w[3:4, :] * dpre
        for k in (1, 2, 3):
            dx = dx + w[3 - k:4 - k, :] * _shift_up(dpre, k, row_ids)
        dx_ref[...] = dx.astype(dx_ref.dtype)
        db_ref[...] = jnp.sum(dpre, axis=0, keepdims=True)
        dw_ref[3:4, :] = jnp.sum(dpre * x, axis=0, keepdims=True)
        for k in (1, 2, 3):
            dw_ref[3 - k:4 - k, :] = jnp.sum(dpre * _shift_down(x, k, row_ids), axis=0, keepdims=True)

    clip = lambda j, lo, n: jnp.clip(j - lo, 0, n - 1)
    return pl.pallas_call(
        body, grid=(nct,),
        in_specs=[pl.BlockSpec((t, LANES), lambda j: (0, XBC_COL0 + j)),
                  pl.BlockSpec((SSD_CONV_WIDTH, LANES), lambda j: (0, j)),
                  pl.BlockSpec((1, LANES), lambda j: (0, j)),
                  pl.BlockSpec((t, LANES), lambda j: (0, clip(j, 0, n_xs))),
                  pl.BlockSpec((t, LANES), lambda j: (0, clip(j, n_xs, n_bm))),
                  pl.BlockSpec((t, LANES), lambda j: (0, clip(j, n_xs + n_bm, n_bm))), ANY],
        out_specs=[pl.BlockSpec((t, LANES), lambda j: (0, XBC_COL0 + j)),
                   pl.BlockSpec((SSD_CONV_WIDTH, LANES), lambda j: (0, j)), pl.BlockSpec((1, LANES), lambda j: (0, j))],
        out_shape=[jax.ShapeDtypeStruct(dzx.shape, dzx.dtype),
                   jax.ShapeDtypeStruct((SSD_CONV_WIDTH, SSD_CONV_DIM), F32),
                   jax.ShapeDtypeStruct((1, SSD_CONV_DIM), F32)],
        input_output_aliases={6: 0},
        compiler_params=_params("parallel"), name=name)(zx, conv_w, conv_b, d_xs, d_bm, d_cm, dzx)


def _softplus_fwd(zx, bias_row, alog_row, *, name):
    t = zx.shape[0]
    q = SSD_CHUNK
    tr = _row_tile(t, 1024)

    def body(x_ref, b_ref, al_ref, dt_ref, cum_ref):
        v = x_ref[...] + b_ref[...]
        e = jnp.exp(-jnp.abs(v))
        u = 1.0 + e
        log1p = jnp.where(u == 1.0, e, jnp.log(u) * (e / (u - 1.0)))
        dt = jnp.maximum(v, 0.0) + log1p
        dt_ref[...] = dt
        a = dt * -jnp.exp(al_ref[...])
        lower = (lax.broadcasted_iota(jnp.int32, (q, q), 1) <= lax.broadcasted_iota(jnp.int32, (q, q), 0)).astype(F32)
        cums = [lax.dot_general(lower, a[c * q:(c + 1) * q, :], ((((1,), (0,))), ((), ())), precision=lax.Precision.HIGHEST,
                                preferred_element_type=F32) for c in range(tr // q)]
        cum_ref[...] = jnp.concatenate(cums, axis=0)

    blk = pl.BlockSpec((tr, LANES), lambda i: (i, 0))
    vec = pl.BlockSpec((1, LANES), lambda i: (0, 0))
    return pl.pallas_call(
        body, grid=(t // tr,),
        in_specs=[blk, vec, vec],
        out_specs=[blk, blk],
        out_shape=[jax.ShapeDtypeStruct((t, LANES), F32), jax.ShapeDtypeStruct((t, LANES), F32)],
        compiler_params=_params("parallel"), name=name)(zx, bias_row, alog_row)


def _softplus_bwd(dt_raw, bias_row, ddt, dzx, *, name):
    t = dt_raw.shape[0]
    tr = _row_tile(t, 1024)
    tail = SSD_IN_PAD - SSD_DT_COL

    def body(x_ref, b_ref, g_ref, _, o_ref, db_ref):
        v = x_ref[...] + b_ref[...]
        lane = lax.broadcasted_iota(jnp.int32, v.shape, 1)
        d = jnp.where(lane < SSD_N_HEADS, g_ref[...] * _sigmoid(v), 0.0)
        o_ref[:, pl.ds(0, LANES)] = d.astype(o_ref.dtype)
        o_ref[:, pl.ds(LANES, tail - LANES)] = jnp.zeros((tr, tail - LANES), o_ref.dtype)

        @pl.when(pl.program_id(0) == 0)
        def _():
            db_ref[...] = jnp.zeros_like(db_ref)

        db_ref[...] += jnp.sum(d, axis=0, keepdims=True)

    return pl.pallas_call(
        body, grid=(t // tr,),
        in_specs=[pl.BlockSpec((tr, LANES), lambda i: (i, 0)), pl.BlockSpec((1, LANES), lambda i: (0, 0)),
                  pl.BlockSpec((tr, LANES), lambda i: (i, 0)), ANY],
        out_specs=[pl.BlockSpec((tr, tail), lambda i: (i, SSD_DT_COL // tail)), pl.BlockSpec((1, LANES), lambda i: (0, 0))],
        out_shape=[jax.ShapeDtypeStruct(dzx.shape, dzx.dtype), jax.ShapeDtypeStruct((1, LANES), F32)],
        input_output_aliases={3: 0},
        compiler_params=_params("arbitrary"), name=name)(dt_raw, bias_row, ddt, dzx)


def _ssd_masks():
    q = SSD_CHUNK
    tt = lax.broadcasted_iota(jnp.int32, (q, q), 0)
    ss = lax.broadcasted_iota(jnp.int32, (q, q), 1)
    lane = lax.broadcasted_iota(jnp.int32, (1, SSD_GW), 1)
    srow = lax.broadcasted_iota(jnp.int32, (SSD_GW, 1), 0)
    hm = [(lane >= SSD_HEAD_DIM * j) & (lane < SSD_HEAD_DIM * (j + 1)) for j in range(SSD_HPG)]
    rm = [(srow >= SSD_HEAD_DIM * j) & (srow < SSD_HEAD_DIM * (j + 1)) for j in range(SSD_HPG)]
    return tt, ss, hm, rm


def _ssd_head_terms(dt_rows, cum_rows, a_rows, j, tt, ss):
    q = SSD_CHUNK
    dt_row = dt_rows[j:j + 1, :]
    dt_col = jnp.sum(jnp.where(tt == ss, dt_row, 0.0), axis=1, keepdims=True)
    a_row1 = a_rows[j:j + 1, :]
    a_11 = a_rows[j:j + 1, 0:1]
    cum_col = jnp.sum(jnp.where(ss <= tt, dt_row * a_row1, 0.0), axis=1, keepdims=True)
    cum_row = cum_rows[j:j + 1, :]
    decay = jnp.exp(jnp.where(ss <= tt, cum_col - cum_row, -jnp.inf))
    cum_last = cum_col[q - 1:q, :]
    e_col = jnp.exp(cum_col)
    dte_col = jnp.exp(cum_last - cum_col)
    e_last = jnp.exp(cum_last)
    return dt_col, dt_row, a_row1, a_11, decay, e_col, dte_col, e_last


SSD_CHUNKS_PER_STEP = 8
SSD_BC_COL0 = SSD_D_INNER // SSD_D_STATE


def _ssd_head_selects(terms, hm, rm):
    e_all = jnp.zeros((SSD_CHUNK, SSD_GW), F32)
    w_all = jnp.zeros((SSD_CHUNK, SSD_GW), F32)
    e_s = jnp.zeros((SSD_GW, 1), F32)
    for j in range(SSD_HPG):
        dt_col, _, _, _, _, e_col, dte_col, e_last = terms[j]
        e_all = jnp.where(hm[j], e_col, e_all)
        w_all = jnp.where(hm[j], dt_col * dte_col, w_all)
        e_s = jnp.where(rm[j], e_last, e_s)
    return e_all, w_all, e_s


def _ssd_fwd(xc, dtr, cumr, alog_b, d_b, *, name, hook=None):
    t = xc.shape[0]
    q = SSD_CHUNK
    nc = t // q
    kc = min(SSD_CHUNKS_PER_STEP, nc)
    rows = kc * q
    hk = _HookSlots(hook, n_in=7, n_out=2, n_scratch=1)

    def body(*refs):
        (x_ref, b_ref, c_ref, dtr_ref, cumr_ref, alog_ref, d_ref), (y_ref, st_ref), (s_scr,) = hk.own(refs)
        if hook is not None:
            hk.run(refs, pl.program_id(0) * (nc // kc) + pl.program_id(1), SSD_N_GROUPS * (nc // kc))

        @pl.when(pl.program_id(1) == 0)
        def _():
            s_scr[...] = jnp.zeros_like(s_scr)

        tt, ss, hm, rm = _ssd_masks()
        a_rows = -jnp.exp(alog_ref[...])
        d_rows = d_ref[...]
        d_all = jnp.zeros((1, SSD_GW), F32)
        for j in range(SSD_HPG):
            d_all = jnp.where(hm[j], d_rows[j:j + 1, 0:1], d_all)
        ks, hs = range(kc), range(SSD_HPG)
        sl = [pl.ds(k * q, q) for k in ks]
        x = [x_ref[sl[k], :] for k in ks]
        bm = [b_ref[sl[k], :].astype(BF16) for k in ks]
        cm = [c_ref[sl[k], :].astype(BF16) for k in ks]
        xb = [x[k].astype(BF16) for k in ks]
        terms = [[_ssd_head_terms(dtr_ref[:, sl[k]], cumr_ref[:, sl[k]], a_rows, j, tt, ss) for j in hs] for k in ks]
        g = [_dot_nt(cm[k], bm[k]) for k in ks]
        m = [[(g[k] * terms[k][j][4] * terms[k][j][1]).astype(BF16) for j in hs] for k in ks]
        yj = [[_dot_nn(m[k][j], xb[k]) for j in hs] for k in ks]
        sel = [_ssd_head_selects(terms[k], hm, rm) for k in ks]
        upd = [_dot_tn((x[k] * sel[k][1]).astype(BF16), bm[k]) for k in ks]
        states = [s_scr[...]]
        for k in ks:
            states.append(states[k] * sel[k][2] + upd[k])
        inter = [_dot_nt(cm[k], states[k].astype(BF16)) for k in ks]
        ys = []
        for k in ks:
            y = jnp.zeros((q, SSD_GW), F32)
            for j in hs:
                y = jnp.where(hm[j], yj[k][j], y)
            ys.append(y + inter[k] * sel[k][0] + x[k] * d_all)
        for k in ks:
            st_ref[k] = states[k]
        y_ref[...] = jnp.concatenate(ys, axis=0)
        s_scr[...] = states[kc]

    blk = lambda width, off: pl.BlockSpec((rows, width), lambda g, c: (c, off + g))
    par_s = pl.BlockSpec((None, SSD_HPG, LANES), lambda g, c: (g, 0, 0))
    row_s = pl.BlockSpec((None, SSD_HPG, rows), lambda g, c: (g, 0, c))
    outs = pl.pallas_call(
        body, grid=(SSD_N_GROUPS, nc // kc),
        in_specs=[blk(SSD_GW, 0), blk(SSD_D_STATE, SSD_BC_COL0), blk(SSD_D_STATE, SSD_BC_COL0 + SSD_N_GROUPS),
                  row_s, row_s, par_s, par_s] + hk.in_specs,
        out_specs=[blk(SSD_GW, 0), pl.BlockSpec((None, kc, SSD_GW, SSD_D_STATE), lambda g, c: (g, c, 0, 0))] + hk.out_specs,
        out_shape=[jax.ShapeDtypeStruct((t, SSD_D_INNER), F32),
                   jax.ShapeDtypeStruct((SSD_N_GROUPS, nc, SSD_GW, SSD_D_STATE), F32)] + hk.out_shape,
        scratch_shapes=[pltpu.VMEM((SSD_GW, SSD_D_STATE), F32)] + hk.scratch,
        compiler_params=_params(*hk.semantics("parallel", "arbitrary")), name=name)(
            xc, xc, xc, dtr, cumr, alog_b, d_b, *hk.inputs)
    return outs if hook is None else (outs[:2], outs[2:])


def _ssd_bwd(xc, dtr, cumr, alog_b, d_b, states, dy, *, name, hook=None):
    t = xc.shape[0]
    q = SSD_CHUNK
    nc = t // q
    kc = min(SSD_CHUNKS_PER_STEP, nc)
    nst = nc // kc
    rows = kc * q
    rev = lambda c: nst - 1 - c
    hk = _HookSlots(hook, n_in=9, n_out=5, n_scratch=1)

    def body(*refs):
        ((x_ref, b_ref, c_ref, dtr_ref, cumr_ref, alog_ref, d_ref, st_ref, dy_ref),
         (dx_ref, db_ref, dc_ref, ddt_ref, dpar_ref), (ds_scr,)) = hk.own(refs)
        if hook is not None:
            hk.run(refs, pl.program_id(0) * nst + pl.program_id(1), SSD_N_GROUPS * nst)

        @pl.when(pl.program_id(1) == 0)
        def _():
            ds_scr[...] = jnp.zeros_like(ds_scr)
            dpar_ref[...] = jnp.zeros_like(dpar_ref)

        tt, ss, hm, rm = _ssd_masks()
        tcol = lax.broadcasted_iota(jnp.int32, (q, 1), 0)
        lane = lax.broadcasted_iota(jnp.int32, (1, LANES), 1)
        a_rows = -jnp.exp(alog_ref[...])
        d_rows = d_ref[...]
        d_all = jnp.zeros((1, SSD_GW), F32)
        for j in range(SSD_HPG):
            d_all = jnp.where(hm[j], d_rows[j:j + 1, 0:1], d_all)
        ks, hs = range(kc), range(SSD_HPG)
        sl = [pl.ds(k * q, q) for k in ks]
        x = [x_ref[sl[k], :] for k in ks]
        dyv = [dy_ref[sl[k], :] for k in ks]
        bm = [b_ref[sl[k], :].astype(BF16) for k in ks]
        cm = [c_ref[sl[k], :].astype(BF16) for k in ks]
        s_in = [st_ref[k] for k in ks]
        xb = [x[k].astype(BF16) for k in ks]
        dyb = [dyv[k].astype(BF16) for k in ks]
        s_b = [s_in[k].astype(BF16) for k in ks]
        terms = [[_ssd_head_terms(dtr_ref[:, sl[k]], cumr_ref[:, sl[k]], a_rows, j, tt, ss) for j in hs] for k in ks]
        sel = [_ssd_head_selects(terms[k], hm, rm) for k in ks]
        e_all, w_all, e_s = [s_[0] for s_ in sel], [s_[1] for s_ in sel], [s_[2] for s_ in sel]
        dye = [(dyv[k] * e_all[k]).astype(BF16) for k in ks]
        ds_loc = [_dot_tn(dye[k], cm[k]) for k in ks]
        ds = [None] * kc
        running = ds_scr[...]
        for k in reversed(ks):
            ds[k] = running
            running = running * e_s[k] + ds_loc[k]
        ds_scr[...] = running
        ds_b = [ds[k].astype(BF16) for k in ks]
        g = [_dot_nt(cm[k], bm[k]) for k in ks]
        cs = [_dot_nt(cm[k], s_b[k]) for k in ks]
        bds = [_dot_nt(bm[k], ds_b[k]) for k in ks]
        dm = [[_dot_nt(jnp.where(hm[j], dyv[k], 0.0).astype(BF16), xb[k]) for j in hs] for k in ks]
        gl = [[g[k] * terms[k][j][4] for j in hs] for k in ks]
        wp = [[dm[k][j] * gl[k][j] for j in hs] for k in ks]
        mt = [[(gl[k][j] * terms[k][j][1]).astype(BF16) for j in hs] for k in ks]
        dxj = [[_dot_tn(mt[k][j], dyb[k]) for j in hs] for k in ks]
        dg = []
        for k in ks:
            acc = jnp.zeros((q, q), F32)
            for j in hs:
                acc = acc + dm[k][j] * terms[k][j][4] * terms[k][j][1]
            dg.append(acc.astype(BF16))
        dy_cs = [dyv[k] * cs[k] for k in ks]
        x_bds = [x[k] * bds[k] for k in ks]
        dy_x = [dyv[k] * x[k] for k in ks]
        ds_s = [ds[k] * s_in[k] for k in ks]
        w = [[wp[k][j] * terms[k][j][1] for j in hs] for k in ks]
        rw_col = [[jnp.sum(w[k][j], axis=1, keepdims=True) for j in hs] for k in ks]
        cw_row = [[jnp.sum(w[k][j], axis=0, keepdims=True) for j in hs] for k in ks]
        cwp_row = [[jnp.sum(wp[k][j], axis=0, keepdims=True) for j in hs] for k in ks]
        r1_col = [[jnp.sum(jnp.where(hm[j], dy_cs[k], 0.0), axis=1, keepdims=True) * terms[k][j][5] for j in hs] for k in ks]
        dw_col = [[jnp.sum(jnp.where(hm[j], x_bds[k], 0.0), axis=1, keepdims=True) for j in hs] for k in ks]
        head_rows = [slice(j * SSD_HEAD_DIM, (j + 1) * SSD_HEAD_DIM) for j in hs]
        lane_sum = lambda v: jnp.sum(v, axis=1, keepdims=True)
        s_sum = [[lane_sum(jnp.sum(ds_s[k][head_rows[j], :], axis=0, keepdims=True)) for j in hs] for k in ks]
        dy_x_cols = [jnp.sum(dy_x[k], axis=0, keepdims=True) for k in ks]
        d_d = [[lane_sum(jnp.where(hm[j], dy_x_cols[k], 0.0)) for j in hs] for k in ks]
        ddt_rows = [[None] * SSD_HPG for _ in ks]
        dpar = [jnp.zeros((1, LANES), F32) for _ in hs]
        for k in ks:
            for j in hs:
                dt_col, dt_row, a_row1, a_11, _, _, dte_col, e_last = terms[k][j]
                dww = dw_col[k][j] * (dt_col * dte_col)
                last_add = jnp.sum(dww, axis=0, keepdims=True) + e_last * s_sum[k][j]
                dcum_col = rw_col[k][j] + r1_col[k][j] - dww + jnp.where(tcol == q - 1, last_add, 0.0)
                da_row = jnp.sum(jnp.where(tt >= ss, dcum_col, 0.0), axis=0, keepdims=True)
                da_col = jnp.sum(jnp.where(ss >= tt, -cw_row[k][j], 0.0), axis=1, keepdims=True)
                ddt_col = a_11 * da_col + dw_col[k][j] * dte_col
                ddt_rows[k][j] = (a_row1 * da_row + cwp_row[k][j]
                                  + jnp.sum(jnp.where(tt == ss, ddt_col, 0.0), axis=0, keepdims=True))
                d_a = jnp.sum(dt_row * da_row, axis=1, keepdims=True) + jnp.sum(dt_col * da_col, axis=0, keepdims=True)
                dpar[j] = dpar[j] + jnp.where(lane == 0, d_a * a_11, 0.0) + jnp.where(lane == 1, d_d[k][j], 0.0)
        dxs = []
        for k in ks:
            acc = jnp.zeros((q, SSD_GW), F32)
            for j in hs:
                acc = jnp.where(hm[j], dxj[k][j], acc)
            dxs.append(acc + w_all[k] * bds[k] + d_all * dyv[k])
        xw = [(x[k] * w_all[k]).astype(BF16) for k in ks]
        dc = [_dot_nn(dg[k], bm[k]) + _dot_nn(dye[k], s_b[k]) for k in ks]
        db = [_dot_tn(dg[k], cm[k]) + _dot_nn(xw[k], ds_b[k]) for k in ks]
        dx_ref[...] = jnp.concatenate(dxs, axis=0)
        dc_ref[...] = jnp.concatenate(dc, axis=0)
        db_ref[...] = jnp.concatenate(db, axis=0)
        ddt_ref[...] = jnp.concatenate([jnp.concatenate([ddt_rows[k][j] for k in ks], axis=1) for j in hs], axis=0)
        dpar_ref[...] += jnp.concatenate(dpar, axis=0)

    blk = lambda width, off: pl.BlockSpec((rows, width), lambda g, c: (rev(c), off + g))
    par_s = pl.BlockSpec((None, SSD_HPG, LANES), lambda g, c: (g, 0, 0))
    outs = pl.pallas_call(
        body, grid=(SSD_N_GROUPS, nst),
        in_specs=[blk(SSD_GW, 0), blk(SSD_D_STATE, SSD_BC_COL0), blk(SSD_D_STATE, SSD_BC_COL0 + SSD_N_GROUPS),
                  pl.BlockSpec((None, SSD_HPG, rows), lambda g, c: (g, 0, rev(c))),
                  pl.BlockSpec((None, SSD_HPG, rows), lambda g, c: (g, 0, rev(c))), par_s, par_s,
                  pl.BlockSpec((None, kc, SSD_GW, SSD_D_STATE), lambda g, c: (g, rev(c), 0, 0)), blk(SSD_GW, 0)] + hk.in_specs,
        out_specs=[blk(SSD_GW, 0), blk(SSD_D_STATE, 0), blk(SSD_D_STATE, 0),
                   pl.BlockSpec((None, SSD_HPG, rows), lambda g, c: (g, 0, rev(c))), par_s] + hk.out_specs,
        out_shape=[jax.ShapeDtypeStruct((t, SSD_D_INNER), F32),
                   jax.ShapeDtypeStruct((t, SSD_N_GROUPS * SSD_D_STATE), F32),
                   jax.ShapeDtypeStruct((t, SSD_N_GROUPS * SSD_D_STATE), F32),
                   jax.ShapeDtypeStruct((SSD_N_GROUPS, SSD_HPG, t), F32),
                   jax.ShapeDtypeStruct((SSD_N_GROUPS, SSD_HPG, LANES), F32)] + hk.out_shape,
        scratch_shapes=[pltpu.VMEM((SSD_GW, SSD_D_STATE), F32)] + hk.scratch,
        compiler_params=_params(*hk.semantics("parallel", "arbitrary")), name=name)(
            xc, xc, xc, dtr, cumr, alog_b, d_b, states, dy, *hk.inputs)
    return outs if hook is None else (outs[:5], outs[5:])


def _gate_norm_fwd(y, zx, norm_w, *, name):
    t = y.shape[0]
    tr = _row_tile(t, 256)
    row = pl.BlockSpec((tr, SSD_D_INNER), lambda i: (i, 0))

    def body(y_ref, z_ref, w_ref, o_ref):
        for gi in range(SSD_N_GROUPS):
            sl = pl.ds(gi * SSD_GW, SSD_GW)
            z = z_ref[:, sl].astype(F32)
            gv = y_ref[:, sl] * (z * _sigmoid(z))
            r = lax.rsqrt(jnp.mean(gv * gv, axis=-1, keepdims=True) + NORM_EPS)
            o_ref[:, sl] = (gv * r * w_ref[:, sl]).astype(BF16)

    return pl.pallas_call(
        body, grid=(t // tr,), in_specs=[row, row, pl.BlockSpec((1, SSD_D_INNER), lambda i: (0, 0))],
        out_specs=row, out_shape=jax.ShapeDtypeStruct((t, SSD_D_INNER), BF16),
        compiler_params=_params("parallel"), name=name)(y, zx, norm_w)


def _gate_norm_bwd(y, zx, norm_w, dyn, *, name):
    t = y.shape[0]
    tr = _row_tile(t, 256)
    row = pl.BlockSpec((tr, SSD_D_INNER), lambda i: (i, 0))
    vec = pl.BlockSpec((1, SSD_D_INNER), lambda i: (0, 0))

    def body(y_ref, z_ref, w_ref, dyn_ref, dy_ref, dz_ref, dw_ref):
        @pl.when(pl.program_id(0) == 0)
        def _():
            dw_ref[...] = jnp.zeros_like(dw_ref)

        for gi in range(SSD_N_GROUPS):
            sl = pl.ds(gi * SSD_GW, SSD_GW)
            z = z_ref[:, sl].astype(F32)
            yv = y_ref[:, sl]
            sg = _sigmoid(z)
            sz = z * sg
            gv = yv * sz
            r = lax.rsqrt(jnp.mean(gv * gv, axis=-1, keepdims=True) + NORM_EPS)
            ghat = gv * r
            dout = dyn_ref[:, sl].astype(F32)
            dgh = dout * w_ref[:, sl]
            dgv = r * (dgh - ghat * jnp.mean(dgh * ghat, axis=-1, keepdims=True))
            dy_ref[:, sl] = dgv * sz
            dz_ref[:, sl] = (dgv * yv * (sg * (1.0 + z * (1.0 - sg)))).astype(dz_ref.dtype)
            dw_ref[:, sl] += jnp.sum(dout * ghat, axis=0, keepdims=True)

    return pl.pallas_call(
        body, grid=(t // tr,), in_specs=[row, row, vec, row], out_specs=[row, row, vec],
        out_shape=[jax.ShapeDtypeStruct((t, SSD_D_INNER), F32), jax.ShapeDtypeStruct((t, SSD_IN_PAD), BF16),
                   jax.ShapeDtypeStruct((1, SSD_D_INNER), F32)],
        compiler_params=_params("arbitrary"), name=name)(y, zx, norm_w, dyn)


ATTN_KV_W = ATTN_N_KV * ATTN_HEAD_DIM
ATTN_Q_HALF = 512
ATTN_K_BLK = ATTN_N_Q * ATTN_HEAD_DIM // ATTN_KV_W
ATTN_V_BLK = ATTN_K_BLK + 1


def _attn_valid(first_block):
    w = ATTN_WINDOW
    qpos = lax.broadcasted_iota(jnp.int32, (w, 2 * w), 0) + w
    kpos = lax.broadcasted_iota(jnp.int32, (w, 2 * w), 1)
    rel = qpos - kpos
    return (rel >= 0) & (rel < w) & jnp.logical_not(first_block & (kpos < w))


def _attn_head_views(lo_ref, hi_ref):
    hd = ATTN_HEAD_DIM
    per_half = ATTN_Q_HALF // hd
    return [(lo_ref if h < per_half else hi_ref)[:, pl.ds((h % per_half) * hd, hd)] for h in range(ATTN_N_Q)]


def _attn_block_views(lo_ref, hi_ref, kc_ref, kp_ref, vc_ref, vp_ref):
    hd = ATTN_HEAD_DIM
    kv_cols = [pl.ds(kh * hd, hd) for kh in range(ATTN_N_KV)]
    kb = [jnp.concatenate([kp_ref[:, c], kc_ref[:, c]], axis=0) for c in kv_cols]
    vb = [jnp.concatenate([vp_ref[:, c], vc_ref[:, c]], axis=0) for c in kv_cols]
    return _attn_head_views(lo_ref, hi_ref), kb, vb


def _attn_scores(q, kb, valid):
    scale = ATTN_HEAD_DIM ** -0.5
    return [jnp.where(valid, _dot_nt(q[h], kb[h // ATTN_REP]) * scale, -jnp.inf) for h in range(ATTN_N_Q)]


def _attn_softmax(s, sink):
    heads = range(ATTN_N_Q)
    m = [jnp.maximum(jnp.max(s[h], axis=1, keepdims=True), sink[h]) for h in heads]
    e = [jnp.exp(s[h] - m[h]) for h in heads]
    es = [jnp.exp(sink[h] - m[h]) for h in heads]
    inv = [1.0 / (jnp.sum(e[h], axis=1, keepdims=True) + es[h]) for h in heads]
    return e, es, inv


def _attn_fwd(qkv, sinks_b, *, name, hook=None):
    t = qkv.shape[0]
    w = ATTN_WINDOW
    nb = t // w
    prev = lambda n: jnp.maximum(n - 1, 0)
    hk = _HookSlots(hook, n_in=7, n_out=1, n_scratch=0)

    def body(*refs):
        (qlo_ref, qhi_ref, kc_ref, kp_ref, vc_ref, vp_ref, sink_ref), (o_ref,), _ = hk.own(refs)
        if hook is not None:
            hk.run(refs, pl.program_id(0), nb)
        heads = range(ATTN_N_Q)
        q, kb, vb = _attn_block_views(qlo_ref, qhi_ref, kc_ref, kp_ref, vc_ref, vp_ref)
        sink = [sink_ref[h:h + 1, 0:1] for h in heads]
        e, _, inv = _attn_softmax(_attn_scores(q, kb, _attn_valid(pl.program_id(0) == 0)), sink)
        out = [_dot_nn((e[h] * inv[h]).astype(BF16), vb[h // ATTN_REP]).astype(o_ref.dtype) for h in heads]
        o_ref[...] = jnp.concatenate(out, axis=1)

    qh = lambda half: pl.BlockSpec((w, ATTN_Q_HALF), lambda n: (n, half))
    kv = lambda blk, idx: pl.BlockSpec((w, ATTN_KV_W), lambda n: (idx(n), blk))
    cur = lambda n: n
    outs = pl.pallas_call(
        body, grid=(nb,),
        in_specs=[qh(0), qh(1), kv(ATTN_K_BLK, cur), kv(ATTN_K_BLK, prev), kv(ATTN_V_BLK, cur), kv(ATTN_V_BLK, prev),
                  pl.BlockSpec((ATTN_N_Q, LANES), lambda n: (0, 0))] + hk.in_specs,
        out_specs=[pl.BlockSpec((w, D_MODEL), lambda n: (n, 0))] + hk.out_specs,
        out_shape=[jax.ShapeDtypeStruct((t, D_MODEL), BF16)] + hk.out_shape,
        scratch_shapes=hk.scratch,
        compiler_params=_params(*hk.semantics("parallel")), name=name)(qkv, qkv, qkv, qkv, qkv, qkv, sinks_b, *hk.inputs)
    return outs[0] if hook is None else (outs[0], outs[1:])


def _attn_bwd(qkv, sinks_b, dout, *, name):
    t = qkv.shape[0]
    w = ATTN_WINDOW
    nb = t // w
    hd = ATTN_HEAD_DIM
    clamp = lambda n: jnp.minimum(n, nb - 1)
    prev = lambda n: jnp.maximum(clamp(n) - 1, 0)

    def body(qlo_ref, qhi_ref, kc_ref, kp_ref, vc_ref, vp_ref, sink_ref, dolo_ref, dohi_ref,
             dq_ref, dkv_ref, dsink_ref, carry):
        n = pl.program_id(0)

        @pl.when(n == 0)
        def _():
            carry[...] = jnp.zeros_like(carry)
            dsink_ref[...] = jnp.zeros_like(dsink_ref)

        @pl.when(n < nb)
        def _():
            heads, kvs = range(ATTN_N_Q), range(ATTN_N_KV)
            q, kb, vb = _attn_block_views(qlo_ref, qhi_ref, kc_ref, kp_ref, vc_ref, vp_ref)
            do = _attn_head_views(dolo_ref, dohi_ref)
            sink = [sink_ref[h:h + 1, 0:1] for h in heads]
            s = _attn_scores(q, kb, _attn_valid(n == 0))
            dp = [_dot_nt(do[h], vb[h // ATTN_REP]) for h in heads]
            e, es, inv = _attn_softmax(s, sink)
            p = [e[h] * inv[h] for h in heads]
            delta = [jnp.sum(p[h] * dp[h], axis=1, keepdims=True) for h in heads]
            dsc = [(p[h] * (dp[h] - delta[h]) * (hd ** -0.5)).astype(BF16) for h in heads]
            pb = [p[h].astype(BF16) for h in heads]
            dq = [_dot_nn(dsc[h], kb[h // ATTN_REP]).astype(dq_ref.dtype) for h in heads]
            stack = lambda per_head, kh: jnp.concatenate(per_head[kh * ATTN_REP:(kh + 1) * ATTN_REP], axis=0)
            dkb = [_dot_tn(stack(dsc, kh), stack(q, kh)) for kh in kvs]
            dvb = [_dot_tn(stack(pb, kh), stack(do, kh)) for kh in kvs]
            dsink = [jnp.broadcast_to(jnp.sum(-es[h] * inv[h] * delta[h], axis=0, keepdims=True), (1, LANES)) for h in heads]
            dq_ref[...] = jnp.concatenate(dq, axis=1)
            dsink_ref[...] += jnp.concatenate(dsink, axis=0)
            dkv_ref[...] = (carry[...] + jnp.concatenate([d[0:w, :] for d in dkb + dvb], axis=1)).astype(dkv_ref.dtype)
            carry[...] = jnp.concatenate([d[w:2 * w, :] for d in dkb + dvb], axis=1)

        @pl.when(n == nb)
        def _():
            dkv_ref[...] = carry[...].astype(dkv_ref.dtype)

    qh = lambda half: pl.BlockSpec((w, ATTN_Q_HALF), lambda n: (clamp(n), half))
    kv = lambda blk, idx: pl.BlockSpec((w, ATTN_KV_W), lambda n: (idx(n), blk))
    return pl.pallas_call(
        body, grid=(nb + 1,),
        in_specs=[qh(0), qh(1), kv(ATTN_K_BLK, clamp), kv(ATTN_K_BLK, prev), kv(ATTN_V_BLK, clamp), kv(ATTN_V_BLK, prev),
                  pl.BlockSpec((ATTN_N_Q, LANES), lambda n: (0, 0)), qh(0), qh(1)],
        out_specs=[pl.BlockSpec((w, D_MODEL), lambda n: (clamp(n), 0)),
                   pl.BlockSpec((w, 2 * ATTN_KV_W), lambda n: (jnp.maximum(n - 1, 0), 0)),
                   pl.BlockSpec((ATTN_N_Q, LANES), lambda n: (0, 0))],
        out_shape=[jax.ShapeDtypeStruct((t, D_MODEL), BF16), jax.ShapeDtypeStruct((t, 2 * ATTN_KV_W), BF16),
                   jax.ShapeDtypeStruct((ATTN_N_Q, LANES), F32)],
        scratch_shapes=[pltpu.VMEM((w, 2 * ATTN_KV_W), F32)],
        compiler_params=_params("arbitrary"), name=name)(qkv, qkv, qkv, qkv, qkv, qkv, sinks_b, dout, dout)


def _sq_relu_epilogue(acc):
    r = jnp.maximum(acc, 0.0)
    return (r * r,)


def _sq_relu_bwd_epilogue(acc, act):
    return (acc * (2.0 * jnp.sqrt(act.astype(F32))),)


def _bias_epilogue(acc, bias):
    return (acc + bias,)


def _plain_run(stage, fn, *args, **kwargs):
    return fn(*args, **kwargs)


def _mlp_fwd(u, w_up, w_down, tag, run=_plain_run):
    act = run(f"mlp_up_{tag}", _matmul, u, w_up, mode="nn", out_dtypes=(BF16,), epilogue=_sq_relu_epilogue, b_shards=True,
              tm=BIG_TILE, name=f"mlp_up_{tag}")
    f = run(f"mlp_down_{tag}", _matmul, act, w_down, mode="nn", out_dtypes=(F32,), tk=BIG_TILE, name=f"mlp_down_{tag}")
    return act, f


def _mlp_bwd(u, act, w_up, w_down, df, tag):
    dpre = _matmul(df, w_down, mode="nt", out_dtypes=(BF16,), epilogue=_sq_relu_bwd_epilogue,
                   extras=((act, "tile"),), name=f"mlp_dact_{tag}")
    dw_down = _matmul(act, df, mode="tn", out_dtypes=(BF16,), tk=BIG_TILE, name=f"mlp_dwdown_{tag}")
    du = _matmul(dpre, w_up, mode="nt", out_dtypes=(F32,), b_shards=True, tm=BIG_TILE, name=f"mlp_du_{tag}")
    dw_up = _matmul(u, dpre, mode="tn", out_dtypes=(BF16,), out_shards=True, tk=BIG_TILE, name=f"mlp_dwup_{tag}")
    return du, dw_up, dw_down


def _group_rows(dt):
    t = dt.shape[0]
    return jnp.transpose(dt[:, :SSD_N_HEADS].reshape(t, SSD_N_GROUPS, SSD_HPG), (1, 2, 0))


def _head_param_rows(p):
    return jnp.broadcast_to(p.reshape(SSD_N_GROUPS, SSD_HPG, 1), (SSD_N_GROUPS, SSD_HPG, LANES))


def _local_step(x, target, wts, comm=None):
    t = x.shape[0]
    wts = dict(wts)
    row = lambda v: v.reshape(1, -1)
    mix_pre, mix_post, ffn_pre, ffn_post = wts["mix_pre_norm"], wts["mix_post_norm"], wts["ffn_pre_norm"], wts["ffn_post_norm"]

    def gathering(stage, fn, *args, **kwargs):
        hook = comm.gather_hook(stage) if comm is not None else None
        if hook is None:
            return fn(*args, **kwargs)
        out, got = fn(*args, hook=hook, **kwargs)
        wts.update(comm.weights_from(stage, got))
        return out

    u0 = _rms_fwd(x, row(mix_pre[0]), name="rms_pre_mix0")
    zx, dt_raw = gathering("in_proj", _matmul, u0, wts["ssd_w_in"], mode="nn", out_dtypes=(BF16,), tn=SSD_IN_TILE,
                           f32_block=SSD_DT_COL - (SSD_IN_PAD - SSD_IN_TILE),
                           name="ssd_in_proj")
    xc = gathering("conv", _conv_fwd, zx, wts["ssd_conv_w"], row(wts["ssd_conv_b"]), name="ssd_conv_fwd")
    bias_row = jnp.pad(wts["ssd_dt_bias"], (0, LANES - SSD_N_HEADS)).reshape(1, LANES)
    alog_row = jnp.pad(wts["ssd_a_log"], (0, LANES - SSD_N_HEADS)).reshape(1, LANES)
    dt, cum = _softplus_fwd(dt_raw, bias_row, alog_row, name="ssd_dt_fwd")
    dtr, cumr = _group_rows(dt), _group_rows(cum)
    alog_b, d_b = _head_param_rows(wts["ssd_a_log"]), _head_param_rows(wts["ssd_d"])
    y_ssd, states = gathering("scan", _ssd_fwd, xc, dtr, cumr, alog_b, d_b, name="ssd_scan_fwd")
    norm_w = row(wts["ssd_norm_w"])
    yn = _gate_norm_fwd(y_ssd, zx, norm_w, name="ssd_gate_norm_fwd")
    mix0 = _matmul(yn, wts["ssd_w_out"], mode="nn", out_dtypes=(F32,), tk=BIG_TILE, name="ssd_out_proj")
    h1, v0 = _rms_fwd(mix0, row(mix_post[0]), resid=x, want_u=row(ffn_pre[0]), name="rms_post_mix0")
    act0, f0 = _mlp_fwd(v0, wts["mlp_w_up0"], wts["mlp_w_down0"], "l0", run=gathering)
    h2, u1 = _rms_fwd(f0, row(ffn_post[0]), resid=h1, want_u=row(mix_pre[1]), name="rms_post_ffn0")

    qkv = _matmul(u1, wts["attn_w_qkv"], mode="nn", out_dtypes=(BF16,), epilogue=_bias_epilogue,
                  extras=((row(wts["attn_b_qkv"]), "row"),), b_shards=True, name="attn_qkv_proj")
    sinks_b = jnp.broadcast_to(wts["attn_sinks"].reshape(ATTN_N_Q, 1), (ATTN_N_Q, LANES))
    ao = gathering("attn_fwd", _attn_fwd, qkv, sinks_b, name="attn_fwd")
    mix1 = _matmul(ao, wts["attn_w_o"], mode="nn", out_dtypes=(F32,), epilogue=_bias_epilogue,
                   extras=((row(wts["attn_b_o"]), "row"),), name="attn_out_proj")
    h3, v1 = _rms_fwd(mix1, row(mix_post[1]), resid=h2, want_u=row(ffn_pre[1]), name="rms_post_mix1")
    act1, f1 = _mlp_fwd(v1, wts["mlp_w_up1"], wts["mlp_w_down1"], "l1")
    dh4, loss_tile = _rms_fwd(f1, row(ffn_post[1]), resid=h3, target=target, name="rms_post_ffn1_loss")

    df1, g_ffn_post1 = _rms_bwd(f1, row(ffn_post[1]), dh4, out_dtype=BF16, name="rms_post_ffn1_bwd")
    dv1, g_up1, g_down1 = _mlp_bwd(v1, act1, wts["mlp_w_up1"], wts["mlp_w_down1"], df1, "l1")
    dh3, g_ffn_pre1 = _rms_bwd(h3, row(ffn_pre[1]), dv1, resid=dh4, name="rms_pre_ffn1_bwd")
    dmix1, g_mix_post1, g_b_o = _rms_bwd(mix1, row(mix_post[1]), dh3, out_dtype=BF16, dx_col_sum=True, name="rms_post_mix1_bwd")
    g_w_o = _matmul(ao, dmix1, mode="tn", out_dtypes=(BF16,), tk=BIG_TILE, name="attn_dwo")
    dao = _matmul(dmix1, wts["attn_w_o"], mode="nt", out_dtypes=(BF16,), name="attn_dao")
    dq, dkv, g_sinks = _attn_bwd(qkv, sinks_b, dao, name="attn_bwd")
    dqkv = jnp.concatenate([dq, dkv], axis=1)
    g_b_qkv = _col_sum(dqkv, name="attn_bqkv_grad")
    g_w_qkv = _matmul(u1, dqkv, mode="tn", out_dtypes=(BF16,), tn=ATTN_QKV // N_CHIPS, out_shards=True, tk=BIG_TILE, name="attn_dwqkv")
    du1 = _matmul(dqkv, wts["attn_w_qkv"], mode="nt", out_dtypes=(F32,), b_shards=True, name="attn_du")
    dh2, g_mix_pre1 = _rms_bwd(h2, row(mix_pre[1]), du1, resid=dh3, name="rms_pre_mix1_bwd")

    df0, g_ffn_post0 = _rms_bwd(f0, row(ffn_post[0]), dh2, out_dtype=BF16, name="rms_post_ffn0_bwd")
    dv0, g_up0, g_down0 = _mlp_bwd(v0, act0, wts["mlp_w_up0"], wts["mlp_w_down0"], df0, "l0")
    dh1, g_ffn_pre0 = _rms_bwd(h1, row(ffn_pre[0]), dv0, resid=dh2, name="rms_pre_ffn0_bwd")
    dmix0, g_mix_post0 = _rms_bwd(mix0, row(mix_post[0]), dh1, out_dtype=BF16, name="rms_post_mix0_bwd")
    g_w_out = _matmul(yn, dmix0, mode="tn", out_dtypes=(BF16,), tk=BIG_TILE, name="ssd_dwout")
    dyn = _matmul(dmix0, wts["ssd_w_out"], mode="nt", out_dtypes=(BF16,), name="ssd_dyn")
    dy_ssd, dzx, g_norm_w = _gate_norm_bwd(y_ssd, zx, norm_w, dyn, name="ssd_gate_norm_bwd")
    mats = {"ssd_w_out": g_w_out, "attn_w_qkv": g_w_qkv, "attn_w_o": g_w_o,
            "mlp_w_up0": g_up0, "mlp_w_up1": g_up1, "mlp_w_down0": g_down0, "mlp_w_down1": g_down1}
    if comm is None:
        dxc, dbm, dcm, ddt_r, dpar = _ssd_bwd(xc, dtr, cumr, alog_b, d_b, states, dy_ssd, name="ssd_scan_bwd")
    else:
        (dxc, dbm, dcm, ddt_r, dpar), received = _ssd_bwd(xc, dtr, cumr, alog_b, d_b, states, dy_ssd,
                                                          name="ssd_scan_bwd", hook=comm.exchange_hook(mats, "early"))
        comm.received(received)
    dzx, g_conv_w, g_conv_b = _conv_bwd(zx, wts["ssd_conv_w"], row(wts["ssd_conv_b"]), dxc, dbm, dcm, dzx, name="ssd_conv_bwd")
    ddt = jnp.pad(jnp.transpose(ddt_r, (2, 0, 1)).reshape(t, SSD_N_HEADS), ((0, 0), (0, LANES - SSD_N_HEADS)))
    dzx, g_dt_bias = _softplus_bwd(dt_raw, bias_row, ddt, dzx, name="ssd_dt_bwd")
    g_w_in = _w_in_to_shards(_matmul(u0, dzx, mode="tn", out_dtypes=(F32,), tn=SSD_IN_TILE, tk=BIG_TILE, name="ssd_dwin"), name="ssd_dwin_shards")
    mats["ssd_w_in"] = g_w_in
    if comm is None:
        du0 = _matmul(dzx, wts["ssd_w_in"], mode="nt", out_dtypes=(F32,), tk=SSD_IN_TILE, name="ssd_du")
    else:
        du0, received = _matmul(dzx, wts["ssd_w_in"], mode="nt", out_dtypes=(F32,), tk=SSD_IN_TILE, name="ssd_du",
                                hook=comm.exchange_hook(mats, "late"))
        comm.received(received)
    grad_x, g_mix_pre0 = _rms_bwd(x, row(mix_pre[0]), du0, resid=dh1, name="rms_pre_mix0_bwd")

    dpar = dpar.reshape(SSD_N_HEADS, LANES)
    vecs = {
        "ssd_conv_w": g_conv_w, "ssd_conv_b": g_conv_b.reshape(-1),
        "ssd_dt_bias": g_dt_bias[0, :SSD_N_HEADS], "ssd_a_log": dpar[:, 0], "ssd_d": dpar[:, 1],
        "ssd_norm_w": g_norm_w.reshape(-1), "attn_b_qkv": g_b_qkv.reshape(-1), "attn_sinks": g_sinks[:, 0],
        "attn_b_o": g_b_o.reshape(-1),
        "mix_pre_norm": jnp.concatenate([g_mix_pre0, g_mix_pre1]), "mix_post_norm": jnp.concatenate([g_mix_post0, g_mix_post1]),
        "ffn_pre_norm": jnp.concatenate([g_ffn_pre0, g_ffn_pre1]), "ffn_post_norm": jnp.concatenate([g_ffn_post0, g_ffn_post1]),
    }
    return loss_tile, grad_x, mats, vecs


def _mesh_position():
    return lax.axis_index("x"), lax.axis_index("y"), lax.axis_index("c")


def _flip(v, bit):
    return 1 - v if bit else v


OTHER_CHIPS = ((1, 0), (0, 1), (1, 1))


def _comm_params():
    return pltpu.CompilerParams(vmem_limit_bytes=VMEM_LIMIT)


def _staged_copies(srcs, dsts, bufs, sems_in, sems_out):
    loads = [pltpu.make_async_copy(s, b, sems_in.at[i]) for i, (s, b) in enumerate(zip(srcs, bufs))]
    stores = [pltpu.make_async_copy(b, d, sems_out.at[i]) for i, (b, d) in enumerate(zip(bufs, dsts))]
    return loads, stores


class _GatherHook:
    def __init__(self, mats, vecs=()):
        self.arrs = list(mats) + list(vecs)
        self.nm, self.n = len(mats), len(self.arrs)
        n_ici, n_fwd = (N_CHIPS - 1) * self.n, max((N_CHIPS - 1) * self.nm, 1)
        dma = pltpu.SemaphoreType.DMA
        self.out_shape = [jax.ShapeDtypeStruct((N_CHIPS,) + a.shape, a.dtype) for a in self.arrs]
        self.scratch = [pltpu.VMEM(a.shape, a.dtype) for a in self.arrs] + [
            dma((n_ici,)), dma((n_ici,)), dma((n_fwd,)), dma((n_fwd,)), dma((self.n,)), dma((self.n,))]

    def plan(self, ins, outs, scratch):
        n, nm = self.n, self.nm
        bufs = scratch[:n]
        ici_send, ici_recv, fwd_send, fwd_recv, load_sems, store_sems = scratch[n:]
        xi, yi, ci = _mesh_position()
        me = 2 * xi + yi
        loads, stores = _staged_copies(ins, [outs[i].at[me] for i in range(n)], bufs, load_sems, store_sems)
        sends, landed, forwards, from_sibling = [], [], [], []
        for j, (bx, by) in enumerate(OTHER_CHIPS):
            px, py = _flip(xi, bx), _flip(yi, by)
            peer = 2 * px + py
            for i in range(n):
                k = j * n + i
                mk = functools.partial(pltpu.make_async_remote_copy, send_sem=ici_send.at[k], recv_sem=ici_recv.at[k],
                                       device_id=(px, py, ci), device_id_type=MESH)
                if i < nm:
                    sends.append(mk(src_ref=ins[i].at[ci], dst_ref=outs[i].at[me, ci]))
                    landed.append(mk(src_ref=ins[i].at[ci], dst_ref=outs[i].at[peer, ci]))
                    kf = j * nm + i
                    fw = functools.partial(pltpu.make_async_remote_copy, send_sem=fwd_send.at[kf], recv_sem=fwd_recv.at[kf],
                                           device_id=(xi, yi, 1 - ci), device_id_type=MESH)
                    forwards.append(fw(src_ref=outs[i].at[peer, ci], dst_ref=outs[i].at[peer, ci]))
                    from_sibling.append(fw(src_ref=outs[i].at[peer, ci], dst_ref=outs[i].at[peer, 1 - ci]))
                else:
                    sends.append(mk(src_ref=ins[i], dst_ref=outs[i].at[me]))
                    landed.append(mk(src_ref=ins[i], dst_ref=outs[i].at[peer]))
                    forwards.append(None)
        return loads, stores, sends, landed, forwards, from_sibling

    @staticmethod
    def start(p):
        loads, _, sends, _, _, _ = p
        for cp in loads + sends:
            cp.start()

    @staticmethod
    def relay(p):
        loads, stores, _, landed, forwards, _ = p
        for ld, st in zip(loads, stores):
            ld.wait()
            st.start()
        for cp, fw in zip(landed, forwards):
            cp.wait_recv()
            if fw is not None:
                fw.start()

    @staticmethod
    def finish(p):
        _, stores, sends, _, forwards, from_sibling = p
        for cp in from_sibling:
            cp.wait_recv()
        for cp in sends + [fw for fw in forwards if fw is not None]:
            cp.wait_send()
        for st in stores:
            st.wait()


def _run_hook(hook, ins, outs, scratch, step, n_steps):
    p = hook.plan(ins, outs, scratch)
    relay_step = min(max(1, (3 * n_steps) // 4), n_steps - 1)

    @pl.when(step == 0)
    def _():
        hook.start(p)

    if relay_step < n_steps - 1:
        @pl.when(step == relay_step)
        def _():
            hook.relay(p)

    @pl.when(step == n_steps - 1)
    def _():
        if relay_step == n_steps - 1:
            hook.relay(p)
        hook.finish(p)


def _hook_call(hook, *, name):
    n = len(hook.arrs)

    def body(*refs):
        p = hook.plan(refs[:n], refs[n:n + len(hook.out_shape)], refs[n + len(hook.out_shape):])
        hook.start(p)
        hook.relay(p)
        hook.finish(p)

    return pl.pallas_call(
        body, in_specs=[ANY] * n, out_specs=[ANY] * len(hook.out_shape), out_shape=hook.out_shape,
        scratch_shapes=hook.scratch, compiler_params=_comm_params(), name=name)(*hook.arrs)


def _send_other_half(parts, *, name):
    n = len(parts)

    def body(*refs):
        ins, outs = refs[:n], refs[n:2 * n]
        send_sems, recv_sems = refs[2 * n:]
        xi, yi, ci = _mesh_position()
        sibling = (xi, yi, 1 - ci)
        for i in range(n):
            for s in range(N_CHIPS):
                pltpu.make_async_remote_copy(src_ref=ins[i].at[s, 1 - ci], dst_ref=outs[i].at[s], send_sem=send_sems.at[i],
                                             recv_sem=recv_sems.at[i], device_id=sibling, device_id_type=MESH).start()
        for i in range(n):
            pltpu.make_async_remote_copy(src_ref=outs[i], dst_ref=outs[i], send_sem=send_sems.at[i], recv_sem=recv_sems.at[i],
                                         device_id=sibling, device_id_type=MESH).wait()

    return pl.pallas_call(
        body, in_specs=[ANY] * n, out_specs=[ANY] * n,
        out_shape=[jax.ShapeDtypeStruct((p.shape[0],) + p.shape[2:], p.dtype) for p in parts],
        scratch_shapes=[pltpu.SemaphoreType.DMA((n,)), pltpu.SemaphoreType.DMA((n,))],
        name=name)(*parts)


ROW_BLOCKS = 8


def _add_sibling_half(parts, theirs, core, *, name):
    n = len(parts)

    def body(core_ref, *refs):
        for a_ref, b_ref, o_ref in zip(refs[:n], refs[n:2 * n], refs[2 * n:]):
            o_ref[...] = (a_ref[...].astype(F32) + b_ref[...].astype(F32)).astype(o_ref.dtype)

    mine = lambda p: pl.BlockSpec((None, None, p.shape[2] // ROW_BLOCKS, p.shape[3]), lambda s, rb, core_ref: (s, core_ref[0], rb, 0))
    other = lambda p: pl.BlockSpec((None, p.shape[1] // ROW_BLOCKS, p.shape[2]), lambda s, rb, core_ref: (s, rb, 0))
    return pl.pallas_call(
        body,
        grid_spec=pltpu.PrefetchScalarGridSpec(
            num_scalar_prefetch=1, grid=(N_CHIPS, ROW_BLOCKS),
            in_specs=[mine(p) for p in parts] + [other(q) for q in theirs], out_specs=[other(q) for q in theirs]),
        out_shape=[jax.ShapeDtypeStruct(q.shape, BF16) for q in theirs],
        compiler_params=_params("parallel", "parallel"), name=name)(core, *parts, *theirs)


class _ExchangeHook:
    def __init__(self, parts, to_all=()):
        self.arrs = list(parts) + list(to_all)
        self.n_parts, self.n = len(parts), len(self.arrs)
        n_ici, n_peer = max((N_CHIPS - 1) * self.n_parts, 1), (N_DEV - 1) * max(len(to_all), 1)
        dma = pltpu.SemaphoreType.DMA
        self.out_shape = [jax.ShapeDtypeStruct(p.shape, p.dtype) for p in parts] + [
            jax.ShapeDtypeStruct((N_DEV,) + a.shape, a.dtype) for a in to_all]
        self.scratch = [pltpu.VMEM(p.shape[1:], p.dtype) for p in parts] + [pltpu.VMEM(a.shape, a.dtype) for a in to_all] + [
            dma((n_ici,)), dma((n_ici,)), dma((n_peer,)), dma((n_peer,)), dma((self.n,)), dma((self.n,))]

    def plan(self, ins, outs, scratch):
        n, npt = self.n, self.n_parts
        bufs = scratch[:n]
        send_sems, recv_sems, all_send, all_recv, load_sems, store_sems = scratch[n:]
        xi, yi, ci = _mesh_position()
        me_chip = 2 * xi + yi
        me = 4 * xi + 2 * yi + ci
        loads, stores = _staged_copies([ins[i].at[me_chip] for i in range(npt)] + list(ins[npt:]),
                                       [outs[i].at[me_chip] for i in range(npt)] + [outs[i].at[me] for i in range(npt, n)],
                                       bufs, load_sems, store_sems)
        sends, recvs = [], []
        for j, (bx, by) in enumerate(OTHER_CHIPS):
            px, py = _flip(xi, bx), _flip(yi, by)
            peer = 2 * px + py
            for i in range(npt):
                k = j * npt + i
                mk = functools.partial(pltpu.make_async_remote_copy, src_ref=ins[i].at[peer], send_sem=send_sems.at[k],
                                       recv_sem=recv_sems.at[k], device_id=(px, py, ci), device_id_type=MESH)
                sends.append(mk(dst_ref=outs[i].at[me_chip]))
                recvs.append(mk(dst_ref=outs[i].at[peer]))
        for i in range(npt, n):
            for k in range(1, N_DEV):
                px, py, pc = _flip(xi, (k >> 2) & 1), _flip(yi, (k >> 1) & 1), _flip(ci, k & 1)
                slot = (i - npt) * (N_DEV - 1) + k - 1
                mk = functools.partial(pltpu.make_async_remote_copy, src_ref=ins[i], send_sem=all_send.at[slot],
                                       recv_sem=all_recv.at[slot], device_id=(px, py, pc), device_id_type=MESH)
                sends.append(mk(dst_ref=outs[i].at[me]))
                recvs.append(mk(dst_ref=outs[i].at[4 * px + 2 * py + pc]))
        return loads, stores, sends, recvs

    @staticmethod
    def start(p):
        loads, _, sends, _ = p
        for cp in loads + sends:
            cp.start()

    @staticmethod
    def relay(p):
        loads, stores, _, _ = p
        for ld, st in zip(loads, stores):
            ld.wait()
            st.start()

    @staticmethod
    def finish(p):
        _, stores, sends, recvs = p
        for cp in recvs:
            cp.wait_recv()
        for cp in sends:
            cp.wait_send()
        for st in stores:
            st.wait()


def _sum_chips(parts, *, name):
    n = len(parts)
    p = parts[0].shape[0]

    def body(*refs):
        s = pl.program_id(1)
        for x_ref, o_ref in zip(refs[:n], refs[n:]):
            @pl.when(s == 0)
            def _():
                o_ref[...] = x_ref[...].astype(F32)

            @pl.when(s > 0)
            def _():
                o_ref[...] += x_ref[...].astype(F32)

    blocks = lambda q: ROW_BLOCKS if q.shape[1] % (8 * ROW_BLOCKS) == 0 else 1
    assert len({blocks(q) for q in parts}) == 1
    nb = blocks(parts[0])
    return pl.pallas_call(
        body, grid=(nb, p),
        in_specs=[pl.BlockSpec((None, q.shape[1] // nb, q.shape[2]), lambda rb, s: (s, rb, 0)) for q in parts],
        out_specs=[pl.BlockSpec((q.shape[1] // nb, q.shape[2]), lambda rb, s: (rb, 0)) for q in parts],
        out_shape=[jax.ShapeDtypeStruct(q.shape[1:], F32) for q in parts],
        compiler_params=_params("parallel", "arbitrary"), name=name)(*parts)


def _swap_halves(halves, layers, *, name):
    n = len(halves)
    out_shapes, slots = [], []
    for i, h in enumerate(halves):
        pair = [p for p in layers if i in p]
        if pair and pair[0][1] == i:
            slots.append((slots[pair[0][0]][0], 1))
        elif pair:
            out_shapes.append(jax.ShapeDtypeStruct((2, 2) + h.shape, h.dtype))
            slots.append((len(out_shapes) - 1, 0))
        else:
            out_shapes.append(jax.ShapeDtypeStruct((2,) + h.shape, h.dtype))
            slots.append((len(out_shapes) - 1, None))
    n_out = len(out_shapes)

    def body(*refs):
        ins, outs, bufs = refs[:n], refs[n:n + n_out], refs[n + n_out:2 * n + n_out]
        send_sems, recv_sems, load_sems, store_sems = refs[2 * n + n_out:]
        xi, yi, ci = _mesh_position()
        own, sends, recvs = [], [], []
        for i in range(n):
            o, layer = slots[i]
            dst = (lambda core: outs[o].at[core]) if layer is None else (lambda core: outs[o].at[layer, core])
            own.append(dst(ci))
            mk = functools.partial(pltpu.make_async_remote_copy, src_ref=ins[i], send_sem=send_sems.at[i],
                                   recv_sem=recv_sems.at[i], device_id=(xi, yi, 1 - ci), device_id_type=MESH)
            sends.append(mk(dst_ref=dst(ci)))
            recvs.append(mk(dst_ref=dst(1 - ci)))
        loads, stores = _staged_copies(ins, own, bufs, load_sems, store_sems)
        for cp in loads + sends:
            cp.start()
        for ld, st in zip(loads, stores):
            ld.wait()
            st.start()
        for cp in recvs:
            cp.wait_recv()
        for cp in sends:
            cp.wait_send()
        for st in stores:
            st.wait()

    return pl.pallas_call(
        body, in_specs=[ANY] * n, out_specs=[ANY] * n_out, out_shape=out_shapes,
        scratch_shapes=[pltpu.VMEM(h.shape, h.dtype) for h in halves]
        + [pltpu.SemaphoreType.DMA((n,)), pltpu.SemaphoreType.DMA((n,)), pltpu.SemaphoreType.DMA((n,)), pltpu.SemaphoreType.DMA((n,))],
        compiler_params=_comm_params(), name=name)(*halves)


def _cast_bf16(layers, *, name, hook=None):
    n = len(layers)
    hk = _HookSlots(hook, n_in=n, n_out=n, n_scratch=0)

    def body(*refs):
        ins, outs, _ = hk.own(refs)
        if hook is not None:
            hk.run(refs, pl.program_id(0), ROW_BLOCKS)
        for i_ref, o_ref in zip(ins, outs):
            o_ref[...] = i_ref[...].astype(o_ref.dtype)

    in_blk = lambda a, l: pl.BlockSpec((None, a.shape[1] // ROW_BLOCKS, a.shape[2]), lambda i: (l, i, 0))
    out_blk = lambda a: pl.BlockSpec((a.shape[1] // ROW_BLOCKS, a.shape[2]), lambda i: (i, 0))
    outs = pl.pallas_call(
        body, grid=(ROW_BLOCKS,),
        in_specs=[in_blk(a, l) for a, l in layers] + hk.in_specs,
        out_specs=[out_blk(a) for a, _ in layers] + hk.out_specs,
        out_shape=[jax.ShapeDtypeStruct(a.shape[1:], BF16) for a, _ in layers] + hk.out_shape,
        scratch_shapes=hk.scratch,
        compiler_params=_params(*hk.semantics("parallel")), name=name)(*[a for a, _ in layers], *hk.inputs)
    return outs[:n] if hook is None else (outs[:n], outs[n:])


def _full_weight(name, gathered):
    s, _, r, c = gathered.shape
    if name == "ssd_w_in":
        return _w_in_from_shards(gathered.reshape(s, 2 * r, c), name="ssd_w_in_unshard")
    if name in ("attn_w_qkv", "mlp_w_up0", "mlp_w_up1"):
        return gathered.reshape(s, 2 * r, c)
    return gathered.reshape(s * 2 * r, c)


class _StepComm:
    GATHER = {"in_proj": ("mlp_w_up0", "attn_w_o"), "conv": ("mlp_w_down0",), "scan": ("ssd_w_out", "mlp_w_up1"),
              "mlp_up_l0": ("attn_w_qkv",), "attn_fwd": ("mlp_w_down1",)}
    EXCHANGE = {"early": ("ssd_w_out", "attn_w_qkv", "attn_w_o", "mlp_w_up0", "mlp_w_up1", "mlp_w_down0", "mlp_w_down1"),
                "late": ("ssd_w_in",)}

    def __init__(self, shards, core):
        self.shards, self.core = shards, core
        self.chip_parts = {}
        self._pending = None

    def gather_hook(self, stage):
        names = self.GATHER.get(stage)
        return _GatherHook([self.shards[n] for n in names]) if names else None

    def weights_from(self, stage, gathered):
        return {n: _full_weight(n, g) for n, g in zip(self.GATHER[stage], gathered)}

    def chip_sums(self, mats, tag):
        parts = [_shard_halves(a) for a in mats.values()]
        theirs = _send_other_half(parts, name=f"grad_sibling_send_{tag}")
        return _add_sibling_half(parts, theirs, self.core, name=f"grad_chip_sum_{tag}")

    def exchange_hook(self, mats, which):
        self._pending = self.EXCHANGE[which]
        return _ExchangeHook(self.chip_sums({n: mats[n] for n in self._pending}, which))

    def received(self, arrays):
        self.chip_parts.update(zip(self._pending, arrays))


ADAMW_ROW_BLOCKS = 16


def _adamw(ws, gs, ms, vs, *, name):
    n = len(ws)
    nb = ADAMW_ROW_BLOCKS if all(a.shape[0] % (8 * ADAMW_ROW_BLOCKS) == 0 for a in ws) else 1

    def body(*refs):
        ins, outs = refs[:4 * n], refs[4 * n:]
        for i in range(n):
            w_ref, g_ref, m_ref, v_ref = ins[i], ins[n + i], ins[2 * n + i], ins[3 * n + i]
            go_ref, d_ref, nm_ref, nv_ref = outs[i], outs[n + i], outs[2 * n + i], outs[3 * n + i]
            gv = g_ref[...]
            nm = ADAM_B1 * m_ref[...] + (1.0 - ADAM_B1) * gv
            nv = ADAM_B2 * v_ref[...] + (1.0 - ADAM_B2) * (gv * gv)
            m_hat = nm / (1.0 - ADAM_B1 ** ADAM_STEP)
            v_hat = nv / (1.0 - ADAM_B2 ** ADAM_STEP)
            go_ref[...] = gv
            d_ref[...] = -ADAM_LR * (m_hat / (jnp.sqrt(v_hat) + ADAM_EPS) + ADAM_WD * w_ref[...])
            nm_ref[...] = nm
            nv_ref[...] = nv

    blks = [pl.BlockSpec((a.shape[0] // nb, a.shape[1]), lambda i: (i, 0)) for a in ws]
    shapes = [jax.ShapeDtypeStruct(a.shape, F32) for a in ws]
    outs = pl.pallas_call(body, grid=(nb,), in_specs=blks * 4, out_specs=blks * 4, out_shape=shapes * 4,
                          compiler_params=_params("parallel"), name=name)(*ws, *gs, *ms, *vs)
    return [tuple(outs[k * n + i] for k in range(4)) for i in range(n)]


SM_CONV_B, SM_NORM_W, SM_MIX_PRE, SM_MIX_POST, SM_FFN_PRE, SM_FFN_POST, SM_MISC, SM_CONV_W, SM_B_QKV, SM_B_O = 0, 4, 6, 8, 10, 12, 14, 16, 32, 34
SM_ROWS = 40
MISC_DT_BIAS, MISC_A_LOG, MISC_D, MISC_SINKS, MISC_LOSS = 0, 32, 64, 96, 112


def _shard_halves(a):
    c = a.shape[-1]
    return a.reshape(N_CHIPS, 2, -1, c)


def _rows(v):
    return v.reshape(-1, D_MODEL)


def _misc_row(dt_bias, a_log, d, sinks, loss):
    pad = jnp.zeros((D_MODEL - MISC_LOSS - 1,), F32)
    return jnp.concatenate([dt_bias.reshape(-1), a_log.reshape(-1), d.reshape(-1), sinks.reshape(-1), loss.reshape(1), pad]).reshape(1, D_MODEL)


def _replicated_rows(p, loss):
    return jnp.concatenate([
        _rows(p["ssd_conv_b"]), _rows(p["ssd_norm_w"]), _rows(p["mix_pre_norm"]), _rows(p["mix_post_norm"]),
        _rows(p["ffn_pre_norm"]), _rows(p["ffn_post_norm"]),
        _misc_row(p["ssd_dt_bias"], p["ssd_a_log"], p["ssd_d"], p["attn_sinks"], loss), jnp.zeros((1, D_MODEL), F32)], axis=0)


def _sharded_rows(conv_w, b_qkv, b_o):
    last = jnp.concatenate([b_qkv.reshape(-1), b_o.reshape(-1), jnp.zeros((D_MODEL - 640,), F32)]).reshape(1, D_MODEL)
    return jnp.concatenate([conv_w.reshape(SSD_CONV_WIDTH, D_MODEL), last, jnp.zeros((3, D_MODEL), F32)], axis=0)


REPLICATED = ("ssd_conv_b", "ssd_dt_bias", "ssd_a_log", "ssd_d", "ssd_norm_w", "attn_sinks",
              "mix_pre_norm", "mix_post_norm", "ffn_pre_norm", "ffn_post_norm")
MATRICES = ("ssd_w_in", "ssd_w_out", "attn_w_qkv", "attn_w_o", "mlp_w_up", "mlp_w_down")
WEIGHT_NAMES = ("ssd_w_in", "ssd_conv_w", "ssd_conv_b", "ssd_dt_bias", "ssd_a_log", "ssd_d", "ssd_norm_w", "ssd_w_out",
                "attn_w_qkv", "attn_b_qkv", "attn_sinks", "attn_w_o", "attn_b_o", "mlp_w_up", "mlp_w_down",
                "mix_pre_norm", "mix_post_norm", "ffn_pre_norm", "ffn_post_norm")


def _unpack_small(rows16, rows8, like):
    misc = rows16[SM_MISC]
    out = {
        "ssd_conv_b": rows16[SM_CONV_B:SM_CONV_B + 4], "ssd_norm_w": rows16[SM_NORM_W:SM_NORM_W + 2],
        "mix_pre_norm": rows16[SM_MIX_PRE:SM_MIX_PRE + 2], "mix_post_norm": rows16[SM_MIX_POST:SM_MIX_POST + 2],
        "ffn_pre_norm": rows16[SM_FFN_PRE:SM_FFN_PRE + 2], "ffn_post_norm": rows16[SM_FFN_POST:SM_FFN_POST + 2],
        "ssd_dt_bias": misc[MISC_DT_BIAS:MISC_DT_BIAS + 32], "ssd_a_log": misc[MISC_A_LOG:MISC_A_LOG + 32],
        "ssd_d": misc[MISC_D:MISC_D + 32], "attn_sinks": misc[MISC_SINKS:MISC_SINKS + 16],
        "ssd_conv_w": rows8[0:SSD_CONV_WIDTH], "attn_b_qkv": rows8[SSD_CONV_WIDTH, 0:384], "attn_b_o": rows8[SSD_CONV_WIDTH, 384:640],
    }
    return {k: v.reshape(like[k].shape) for k, v in out.items()}


def kernel(x, ssd_w_in, ssd_conv_w, ssd_conv_b, ssd_dt_bias, ssd_a_log, ssd_d, ssd_norm_w, ssd_w_out, attn_w_qkv, attn_b_qkv, attn_sinks, attn_w_o, attn_b_o, mlp_w_up, mlp_w_down, mix_pre_norm, mix_post_norm, ffn_pre_norm, ffn_post_norm, loss_target, m_ssd_w_in, m_ssd_conv_w, m_ssd_conv_b, m_ssd_dt_bias, m_ssd_a_log, m_ssd_d, m_ssd_norm_w, m_ssd_w_out, m_attn_w_qkv, m_attn_b_qkv, m_attn_sinks, m_attn_w_o, m_attn_b_o, m_mlp_w_up, m_mlp_w_down, m_mix_pre_norm, m_mix_post_norm, m_ffn_pre_norm, m_ffn_post_norm, v_ssd_w_in, v_ssd_conv_w, v_ssd_conv_b, v_ssd_dt_bias, v_ssd_a_log, v_ssd_d, v_ssd_norm_w, v_ssd_w_out, v_attn_w_qkv, v_attn_b_qkv, v_attn_sinks, v_attn_w_o, v_attn_b_o, v_mlp_w_up, v_mlp_w_down, v_mix_pre_norm, v_mix_post_norm, v_ffn_pre_norm, v_ffn_post_norm):
    w = dict(zip(WEIGHT_NAMES, (ssd_w_in, ssd_conv_w, ssd_conv_b, ssd_dt_bias, ssd_a_log, ssd_d, ssd_norm_w, ssd_w_out, attn_w_qkv, attn_b_qkv, attn_sinks, attn_w_o, attn_b_o, mlp_w_up, mlp_w_down, mix_pre_norm, mix_post_norm, ffn_pre_norm, ffn_post_norm)))
    m = dict(zip(WEIGHT_NAMES, (m_ssd_w_in, m_ssd_conv_w, m_ssd_conv_b, m_ssd_dt_bias, m_ssd_a_log, m_ssd_d, m_ssd_norm_w, m_ssd_w_out, m_attn_w_qkv, m_attn_b_qkv, m_attn_sinks, m_attn_w_o, m_attn_b_o, m_mlp_w_up, m_mlp_w_down, m_mix_pre_norm, m_mix_post_norm, m_ffn_pre_norm, m_ffn_post_norm)))
    v = dict(zip(WEIGHT_NAMES, (v_ssd_w_in, v_ssd_conv_w, v_ssd_conv_b, v_ssd_dt_bias, v_ssd_a_log, v_ssd_d, v_ssd_norm_w, v_ssd_w_out, v_attn_w_qkv, v_attn_b_qkv, v_attn_sinks, v_attn_w_o, v_attn_b_o, v_mlp_w_up, v_mlp_w_down, v_mix_pre_norm, v_mix_post_norm, v_ffn_pre_norm, v_ffn_post_norm)))
    chip = 2 * lax.axis_index("x") + lax.axis_index("y")

    two_halves = lambda a: a.reshape(2, a.shape[0] // 2, a.shape[1])
    later = {"ssd_w_out": (w["ssd_w_out"], 0), "attn_w_qkv": (w["attn_w_qkv"], 0), "attn_w_o": (w["attn_w_o"], 0),
             "mlp_w_up0": (w["mlp_w_up"], 0), "mlp_w_up1": (w["mlp_w_up"], 1),
             "mlp_w_down0": (w["mlp_w_down"], 0), "mlp_w_down1": (w["mlp_w_down"], 1)}
    first = _GatherHook([two_halves(w["ssd_w_in"][0].astype(BF16))], [w["ssd_conv_w"][0], w["attn_b_qkv"], w["attn_b_o"]])
    cast, (g_in, g_conv, g_bqkv, g_bo) = _cast_bf16(list(later.values()), name="weights_to_bf16", hook=first)
    core = lax.axis_index("c").astype(jnp.int32).reshape(1)
    comm = _StepComm({k: two_halves(a) for k, a in zip(later, cast)}, core)
    full = {
        "ssd_w_in": _full_weight("ssd_w_in", g_in),
        "ssd_conv_w": g_conv.transpose(1, 0, 2).reshape(SSD_CONV_WIDTH, SSD_CONV_DIM),
        "attn_b_qkv": g_bqkv.reshape(ATTN_QKV), "attn_b_o": g_bo.reshape(D_MODEL),
    }
    for name in REPLICATED:
        full[name] = w[name][0] if name.startswith(("ssd_", "attn_")) else w[name]

    loss_tile, grad_x, gm, g = _local_step(x[0], loss_target[0], full, comm)

    conv_w_rows = g["ssd_conv_w"].reshape(SSD_CONV_WIDTH * N_CHIPS, D_MODEL)
    b_qkv_rows = jnp.pad(g["attn_b_qkv"], (0, 2 * D_MODEL - ATTN_QKV)).reshape(2, D_MODEL)
    small = jnp.concatenate([_replicated_rows(g, loss_tile[0, 0]), conv_w_rows, b_qkv_rows, _rows(g["attn_b_o"]),
                             jnp.zeros((SM_ROWS - SM_B_O - 1, D_MODEL), F32)], axis=0)
    small_all, = _hook_call(_ExchangeHook([], [small]), name="vector_grad_all_gather")
    order = ("ssd_w_in", "ssd_w_out", "attn_w_qkv", "attn_w_o", "mlp_w_up0", "mlp_w_up1", "mlp_w_down0", "mlp_w_down1")
    halves = _sum_chips([comm.chip_parts[k] for k in order], name="grad_sum")
    r_in, r_out, r_qkv, r_o, r_up, r_down = _swap_halves(halves, layers=((4, 5), (6, 7)), name="grad_halves_swap")
    small_sum, = _sum_chips([small_all], name="small_grad_sum")

    grads = {"ssd_w_in": r_in, "ssd_w_out": r_out, "attn_w_qkv": r_qkv, "attn_w_o": r_o, "mlp_w_up": r_up, "mlp_w_down": r_down}
    grads = {k: a.reshape(w[k].shape) for k, a in grads.items()}
    conv_w_g = lax.dynamic_index_in_dim(small_sum[SM_CONV_W:SM_CONV_W + 16].reshape(SSD_CONV_WIDTH, N_CHIPS, D_MODEL), chip, axis=1, keepdims=False)
    b_qkv_g = lax.dynamic_slice_in_dim(small_sum[SM_B_QKV:SM_B_QKV + 2].reshape(-1), chip * 384, 384)
    b_o_g = lax.dynamic_slice_in_dim(small_sum[SM_B_O], chip * 256, 256)
    small_g = jnp.concatenate([small_sum[0:16], _sharded_rows(conv_w_g, b_qkv_g, b_o_g)], axis=0)
    grads.update(_unpack_small(small_g[0:16], small_g[16:24], w))
    loss = small_sum[SM_MISC, MISC_LOSS]

    delta, new_m, new_v = {}, {}, {}
    as2d = lambda p: [p[name].reshape(-1, p[name].shape[-1]) for name in MATRICES]
    for name, (g2, d2, m2, v2) in zip(MATRICES, _adamw(as2d(w), as2d(grads), as2d(m), as2d(v), name="adamw_matrices")):
        shape = w[name].shape
        grads[name], delta[name], new_m[name], new_v[name] = g2.reshape(shape), d2.reshape(shape), m2.reshape(shape), v2.reshape(shape)
    zero = jnp.zeros((), F32)
    small_pack = lambda p: jnp.concatenate([_replicated_rows({k: p[k] for k in REPLICATED}, zero),
                                            _sharded_rows(p["ssd_conv_w"], p["attn_b_qkv"], p["attn_b_o"])], axis=0)
    (_, d_s, m_s, v_s), = _adamw([small_pack(w)], [small_g], [small_pack(m)], [small_pack(v)], name="adamw_vectors")
    delta.update(_unpack_small(d_s[0:16], d_s[16:24], w))
    new_m.update(_unpack_small(m_s[0:16], m_s[16:24], w))
    new_v.update(_unpack_small(v_s[0:16], v_s[16:24], w))

    return (loss, grad_x[None], *[grads[n] for n in WEIGHT_NAMES], *[delta[n] for n in WEIGHT_NAMES],
            *[new_m[n] for n in WEIGHT_NAMES], *[new_v[n] for n in WEIGHT_NAMES])
```

```python
import functools

import jax
import jax.numpy as jnp
from jax import lax
from jax.experimental import pallas as pl
from jax.experimental.pallas import tpu as pltpu

F32 = jnp.float32
BF16 = jnp.bfloat16

D_MODEL = 1024
SSD_D_INNER = 2048
SSD_HEAD_DIM = 64
SSD_N_HEADS = 32
SSD_N_GROUPS = 8
SSD_HPG = 4
SSD_D_STATE = 128
SSD_CONV_WIDTH = 4
SSD_CHUNK = 128
SSD_CONV_DIM = 4096
SSD_IN_DIM = 6176
SSD_IN_PAD = 6400
SSD_IN_TILE = 1280
SSD_DT_COL = 6144
SSD_GW = SSD_HPG * SSD_HEAD_DIM
ATTN_HEAD_DIM = 64
ATTN_N_Q = 16
ATTN_N_KV = 4
ATTN_REP = 4
ATTN_WINDOW = 128
ATTN_QKV = 1536
D_FF = 4096
NORM_EPS = 1e-6

ADAM_LR = 0.001
ADAM_B1 = 0.9
ADAM_B2 = 0.999
ADAM_EPS = 1e-08
ADAM_WD = 0.01
ADAM_STEP = 10

N_CHIPS = 4
N_DEV = 8
LANES = 128
VMEM_LIMIT = 48 * 1024 * 1024
BIG_TILE = 2048

MESH = pl.DeviceIdType.MESH


def _params(*sem):
    return pltpu.CompilerParams(dimension_semantics=sem, vmem_limit_bytes=VMEM_LIMIT)


def _dot(a, b, dims):
    return lax.dot_general(a, b, (dims, ((), ())), preferred_element_type=F32)


def _dot_nn(a, b):
    return _dot(a, b, ((1,), (0,)))


def _dot_nt(a, b):
    return _dot(a, b, ((1,), (1,)))


def _dot_tn(a, b):
    return _dot(a, b, ((0,), (0,)))


def _sigmoid(x):
    return 0.5 * jnp.tanh(0.5 * x) + 0.5


ANY = pl.BlockSpec(memory_space=pl.ANY)


class _HookSlots:
    def __init__(self, hook, n_in, n_out, n_scratch):
        self.hook = hook
        self.n_in, self.n_out, self.n_scratch = n_in, n_out, n_scratch
        self.inputs = list(hook.arrs) if hook else []
        self.out_shape = list(hook.out_shape) if hook else []
        self.scratch = list(hook.scratch) if hook else []
        self.in_specs = [ANY] * len(self.inputs)
        self.out_specs = [ANY] * len(self.out_shape)

    def _split(self, refs):
        a = self.n_in
        b = a + len(self.inputs)
        c = b + self.n_out
        d = c + len(self.out_shape)
        e = d + self.n_scratch
        return refs[:a], refs[a:b], refs[b:c], refs[c:d], refs[d:e], refs[e:]

    def own(self, refs):
        ins, _, outs, _, scratch, _ = self._split(refs)
        return ins, outs, scratch

    def run(self, refs, step, n_steps):
        _, h_in, _, h_out, _, h_scratch = self._split(refs)
        _run_hook(self.hook, h_in, h_out, h_scratch, step, n_steps)

    def semantics(self, *sem):
        return sem if self.hook is None else ("arbitrary",) * len(sem)


def _matmul(a, b, *, mode, out_dtypes, name, epilogue=None, extras=(), tm=1024, tn=1024, tk=1024,
            b_shards=False, out_shards=False, hook=None, f32_block=None):
    f32_tail = f32_block is not None
    if b_shards:
        s, b_rows, b_cols = b.shape
        b2 = (b_rows, s * b_cols)
        if mode == "nn":
            tn = b_cols
        else:
            assert mode == "nt"
            tk = b_cols
    else:
        b2 = b.shape
    if mode == "nn":
        (m, k), (k2, n) = a.shape, b2
    elif mode == "nt":
        (m, k), (n, k2) = a.shape, b2
    else:
        (k, m), (k2, n) = a.shape, b2
    assert k == k2, (a.shape, b.shape, mode)
    tm, tn, tk = min(tm, m), min(tn, n), min(tk, k)
    assert m % tm == 0 and n % tn == 0 and k % tk == 0, (m, n, k, tm, tn, tk)
    nk = k // tk
    if mode == "tn":
        a_spec = pl.BlockSpec((tk, tm), lambda i, j, kk: (kk, i))
    else:
        a_spec = pl.BlockSpec((tm, tk), lambda i, j, kk: (i, kk))
    if b_shards and mode == "nn":
        b_spec = pl.BlockSpec((None, tk, tn), lambda i, j, kk: (j, kk, 0))
    elif b_shards:
        b_spec = pl.BlockSpec((None, tn, tk), lambda i, j, kk: (kk, j, 0))
    elif mode == "nt":
        b_spec = pl.BlockSpec((tn, tk), lambda i, j, kk: (j, kk))
    else:
        b_spec = pl.BlockSpec((tk, tn), lambda i, j, kk: (kk, j))
    dims = {"nn": ((1,), (0,)), "nt": ((1,), (1,)), "tn": ((0,), (0,))}[mode]
    ex_specs = []
    for arr, kind in extras:
        if kind == "tile":
            ex_specs.append(pl.BlockSpec((tm, tn), lambda i, j, kk: (i, j)))
        else:
            ex_specs.append(pl.BlockSpec((1, tn), lambda i, j, kk: (0, j)))
    n_ex, n_out = len(extras), len(out_dtypes)
    if epilogue is None:
        epilogue = lambda acc: (acc,)
    hk = _HookSlots(hook, n_in=2 + n_ex, n_out=n_out + f32_tail, n_scratch=0 if nk == 1 else 1)
    grid = (m // tm, n // tn, nk)

    def body(*refs):
        (a_ref, b_ref, *ex), outs, scratch = hk.own(refs)
        if hook is not None:
            step = (pl.program_id(0) * grid[1] + pl.program_id(1)) * grid[2] + pl.program_id(2)
            hk.run(refs, step, grid[0] * grid[1] * grid[2])

        def finish(acc):
            res = epilogue(acc, *[e[...] for e in ex])
            for o, r in zip(outs, res):
                o[...] = r.astype(o.dtype)
            if f32_tail:
                outs[n_out][...] = acc[:, f32_block:f32_block + LANES]

        if nk == 1:
            finish(_dot(a_ref[...], b_ref[...], dims))
        else:
            acc_ref = scratch[0]
            kk = pl.program_id(2)

            @pl.when(kk == 0)
            def _():
                acc_ref[...] = jnp.zeros_like(acc_ref)

            acc_ref[...] += _dot(a_ref[...], b_ref[...], dims)

            @pl.when(kk == nk - 1)
            def _():
                finish(acc_ref[...])

    if out_shards:
        out_spec = pl.BlockSpec((None, tm, tn), lambda i, j, kk: (j, i, 0))
        out_dims = (n // tn, m, tn)
    else:
        out_spec = pl.BlockSpec((tm, tn), lambda i, j, kk: (i, j))
        out_dims = (m, n)
    tail_specs = [pl.BlockSpec((tm, LANES), lambda i, j, kk: (i, 0))] if f32_tail else []
    tail_shapes = [jax.ShapeDtypeStruct((m, LANES), F32)] if f32_tail else []
    outs = pl.pallas_call(
        body,
        grid=grid,
        in_specs=[a_spec, b_spec] + ex_specs + hk.in_specs,
        out_specs=[out_spec for _ in out_dtypes] + tail_specs + hk.out_specs,
        out_shape=[jax.ShapeDtypeStruct(out_dims, dt) for dt in out_dtypes] + tail_shapes + hk.out_shape,
        scratch_shapes=([] if nk == 1 else [pltpu.VMEM((tm, tn), F32)]) + hk.scratch,
        compiler_params=_params(*hk.semantics("parallel", "arbitrary" if f32_tail else "parallel", "arbitrary")),
        name=name,
    )(a, b, *[arr for arr, _ in extras], *hk.inputs)
    n_own = n_out + f32_tail
    own = outs[0] if n_own == 1 else outs[:n_own]
    return own if hook is None else (own, outs[n_own:])


def _row_tile(t, want):
    return min(t, want)


def _rms_fwd(x, w, *, name, resid=None, want_u=None, target=None):
    t, d = x.shape
    tr = _row_tile(t, 512)

    def norm(v, wv):
        return v * lax.rsqrt(jnp.mean(v * v, axis=-1, keepdims=True) + NORM_EPS) * wv

    row = pl.BlockSpec((tr, d), lambda i: (i, 0))
    vec = pl.BlockSpec((1, d), lambda i: (0, 0))
    if target is not None:
        def body(x_ref, w_ref, r_ref, t_ref, dh_ref, loss_ref):
            err = r_ref[...] + norm(x_ref[...], w_ref[...]) - t_ref[...]
            dh_ref[...] = err * (1.0 / d)

            @pl.when(pl.program_id(0) == 0)
            def _():
                loss_ref[...] = jnp.zeros_like(loss_ref)

            part = jnp.sum(jnp.sum(err * err, axis=1, keepdims=True), axis=0, keepdims=True) * (0.5 / d)
            loss_ref[...] += jnp.broadcast_to(part, loss_ref.shape)

        return pl.pallas_call(
            body, grid=(t // tr,), in_specs=[row, vec, row, row],
            out_specs=[row, pl.BlockSpec((8, LANES), lambda i: (0, 0))],
            out_shape=[jax.ShapeDtypeStruct((t, d), F32), jax.ShapeDtypeStruct((8, LANES), F32)],
            compiler_params=_params("arbitrary"), name=name)(x, w, resid, target)
    if resid is None:
        def body(x_ref, w_ref, o_ref):
            o_ref[...] = norm(x_ref[...], w_ref[...]).astype(BF16)
        ins, in_specs = (x, w), [row, vec]
        out_shape, out_specs = jax.ShapeDtypeStruct((t, d), BF16), row
    elif want_u is None:
        def body(x_ref, w_ref, r_ref, o_ref):
            o_ref[...] = r_ref[...] + norm(x_ref[...], w_ref[...])
        ins, in_specs = (x, w, resid), [row, vec, row]
        out_shape, out_specs = jax.ShapeDtypeStruct((t, d), F32), row
    else:
        def body(x_ref, w_ref, r_ref, w2_ref, o_ref, u_ref):
            h = r_ref[...] + norm(x_ref[...], w_ref[...])
            o_ref[...] = h
            u_ref[...] = norm(h, w2_ref[...]).astype(BF16)
        ins, in_specs = (x, w, resid, want_u), [row, vec, row, vec]
        out_shape = [jax.ShapeDtypeStruct((t, d), F32), jax.ShapeDtypeStruct((t, d), BF16)]
        out_specs = [row, row]
    return pl.pallas_call(body, grid=(t // tr,), in_specs=in_specs, out_specs=out_specs, out_shape=out_shape,
                          compiler_params=_params("parallel"), name=name)(*ins)


def _rms_bwd(x, w, dy, *, name, resid=None, out_dtype=F32, dx_col_sum=False):
    t, d = x.shape
    tr = _row_tile(t, 512)
    row = pl.BlockSpec((tr, d), lambda i: (i, 0))
    vec = pl.BlockSpec((1, d), lambda i: (0, 0))
    has_res = resid is not None

    def body(x_ref, w_ref, dy_ref, *rest):
        r_ref = rest[0] if has_res else None
        dx_ref, dw_ref = rest[has_res:has_res + 2]
        xv = x_ref[...]
        dyv = dy_ref[...].astype(F32)
        r = lax.rsqrt(jnp.mean(xv * xv, axis=-1, keepdims=True) + NORM_EPS)
        xhat = xv * r
        dyw = dyv * w_ref[...]
        dx = r * (dyw - xhat * jnp.mean(dyw * xhat, axis=-1, keepdims=True))
        if has_res:
            dx = dx + r_ref[...]
        dx_ref[...] = dx.astype(dx_ref.dtype)

        sums = [(dw_ref, dyv * xhat)] + ([(rest[-1], dx)] if dx_col_sum else [])

        @pl.when(pl.program_id(0) == 0)
        def _():
            for acc_ref, _ in sums:
                acc_ref[...] = jnp.zeros_like(acc_ref)

        for acc_ref, rows in sums:
            acc_ref[...] += jnp.sum(rows, axis=0, keepdims=True)

    ins = (x, w, dy) + ((resid,) if has_res else ())
    in_specs = [row, vec, row] + ([row] if has_res else [])
    n_vec = 2 if dx_col_sum else 1
    return pl.pallas_call(
        body, grid=(t // tr,), in_specs=in_specs, out_specs=[row] + [vec] * n_vec,
        out_shape=[jax.ShapeDtypeStruct((t, d), out_dtype)] + [jax.ShapeDtypeStruct((1, d), F32)] * n_vec,
        compiler_params=_params("arbitrary"), name=name)(*ins)


def _col_sum(x, *, name):
    t, n = x.shape
    tr = _row_tile(t, 512)

    def body(x_ref, o_ref):
        @pl.when(pl.program_id(0) == 0)
        def _():
            o_ref[...] = jnp.zeros_like(o_ref)

        o_ref[...] += jnp.sum(x_ref[...].astype(F32), axis=0, keepdims=True)

    return pl.pallas_call(
        body, grid=(t // tr,), in_specs=[pl.BlockSpec((tr, n), lambda i: (i, 0))],
        out_specs=pl.BlockSpec((1, n), lambda i: (0, 0)), out_shape=jax.ShapeDtypeStruct((1, n), F32),
        compiler_params=_params("arbitrary"), name=name)(x)


SSD_IN_SHARD = SSD_IN_DIM // N_CHIPS


def _w_in_from_shards(shards, *, name):
    d = shards.shape[1]
    tr = 256

    def body(s_ref, o_ref):
        o_ref[:, pl.ds(SSD_DT_COL, SSD_IN_PAD - SSD_DT_COL)] = jnp.zeros((tr, SSD_IN_PAD - SSD_DT_COL), o_ref.dtype)
        for s in range(N_CHIPS):
            o_ref[:, pl.ds(SSD_IN_SHARD * s, SSD_IN_SHARD)] = s_ref[s]

    return pl.pallas_call(
        body, grid=(d // tr,), in_specs=[pl.BlockSpec((N_CHIPS, tr, SSD_IN_SHARD), lambda i: (0, i, 0))],
        out_specs=pl.BlockSpec((tr, SSD_IN_PAD), lambda i: (i, 0)),
        out_shape=jax.ShapeDtypeStruct((d, SSD_IN_PAD), shards.dtype),
        compiler_params=_params("parallel"), name=name)(shards)


def _w_in_to_shards(g, *, name):
    d = g.shape[0]
    tr = 256

    def body(g_ref, o_ref):
        for s in range(N_CHIPS):
            o_ref[s] = g_ref[:, pl.ds(SSD_IN_SHARD * s, SSD_IN_SHARD)].astype(o_ref.dtype)

    return pl.pallas_call(
        body, grid=(d // tr,), in_specs=[pl.BlockSpec((tr, SSD_IN_PAD), lambda i: (i, 0))],
        out_specs=pl.BlockSpec((N_CHIPS, tr, SSD_IN_SHARD), lambda i: (0, i, 0)),
        out_shape=jax.ShapeDtypeStruct((N_CHIPS, d, SSD_IN_SHARD), BF16),
        compiler_params=_params("parallel"), name=name)(g)


XBC_COL0 = SSD_D_INNER // LANES


def _shift_down(v, k, row_ids):
    return jnp.where(row_ids >= k, pltpu.roll(v, k, axis=0), 0.0)


def _shift_up(v, k, row_ids):
    n = v.shape[0]
    return jnp.where(row_ids < n - k, pltpu.roll(v, n - k, axis=0), 0.0)


def _conv_pre(x, w, b, row_ids):
    pre = b + w[3:4, :] * x
    for k in (1, 2, 3):
        pre = pre + w[3 - k:4 - k, :] * _shift_down(x, k, row_ids)
    return pre


def _conv_fwd(zx, conv_w, conv_b, *, name, hook=None):
    t = zx.shape[0]
    nct = SSD_CONV_DIM // LANES
    hk = _HookSlots(hook, n_in=3, n_out=1, n_scratch=0)

    def body(*refs):
        (x_ref, w_ref, b_ref), (o_ref,), _ = hk.own(refs)
        if hook is not None:
            hk.run(refs, pl.program_id(0), nct)
        x = x_ref[...].astype(F32)
        row_ids = lax.broadcasted_iota(jnp.int32, x.shape, 0)
        pre = _conv_pre(x, w_ref[...], b_ref[...], row_ids)
        o_ref[...] = pre * _sigmoid(pre)

    outs = pl.pallas_call(
        body, grid=(nct,),
        in_specs=[pl.BlockSpec((t, LANES), lambda j: (0, XBC_COL0 + j)),
                  pl.BlockSpec((SSD_CONV_WIDTH, LANES), lambda j: (0, j)),
                  pl.BlockSpec((1, LANES), lambda j: (0, j))] + hk.in_specs,
        out_specs=[pl.BlockSpec((t, LANES), lambda j: (0, j))] + hk.out_specs,
        out_shape=[jax.ShapeDtypeStruct((t, SSD_CONV_DIM), F32)] + hk.out_shape,
        scratch_shapes=hk.scratch,
        compiler_params=_params(*hk.semantics("parallel")), name=name)(zx, conv_w, conv_b, *hk.inputs)
    return outs[0] if hook is None else (outs[0], outs[1:])


def _conv_bwd(zx, conv_w, conv_b, d_xs, d_bm, d_cm, dzx, *, name):
    t = zx.shape[0]
    nct = SSD_CONV_DIM // LANES
    n_xs = SSD_D_INNER // LANES
    n_bm = SSD_N_GROUPS * SSD_D_STATE // LANES

    def body(x_ref, w_ref, b_ref, dxs_ref, dbm_ref, dcm_ref, _, dx_ref, dw_ref, db_ref):
        x = x_ref[...].astype(F32)
        w = w_ref[...]
        j = pl.program_id(0)
        dy = jnp.where(j < n_xs, dxs_ref[...], jnp.where(j < n_xs + n_bm, dbm_ref[...], dcm_ref[...]))
        row_ids = lax.broadcasted_iota(jnp.int32, x.shape, 0)
        pre = _conv_pre(x, w, b_ref[...], row_ids)
        sg = _sigmoid(pre)
        dpre = dy * (sg * (1.0 + pre * (1.0 - sg)))
        dx = w[3:4, :] * dpre
        for k in (1, 2, 3):
            dx = dx + w[3 - k:4 - k, :] * _shift_up(dpre, k, row_ids)
        dx_ref[...] = dx.astype(dx_ref.dtype)
        db_ref[...] = jnp.sum(dpre, axis=0, keepdims=True)
        dw_ref[3:4, :] = jnp.sum(dpre * x, axis=0, keepdims=True)
        for k in (1, 2, 3):
            dw_ref[3 - k:4 - k, :] = jnp.sum(dpre * _shift_down(x, k, row_ids), axis=0, keepdims=True)

    clip = lambda j, lo, n: jnp.clip(j - lo, 0, n - 1)
    return pl.pallas_call(
        body, grid=(nct,),
        in_specs=[pl.BlockSpec((t, LANES), lambda j: (0, XBC_COL0 + j)),
                  pl.BlockSpec((SSD_CONV_WIDTH, LANES), lambda j: (0, j)),
                  pl.BlockSpec((1, LANES), lambda j: (0, j)),
                  pl.BlockSpec((t, LANES), lambda j: (0, clip(j, 0, n_xs))),
                  pl.BlockSpec((t, LANES), lambda j: (0, clip(j, n_xs, n_bm))),
                  pl.BlockSpec((t, LANES), lambda j: (0, clip(j, n_xs + n_bm, n_bm))), ANY],
        out_specs=[pl.BlockSpec((t, LANES), lambda j: (0, XBC_COL0 + j)),
                   pl.BlockSpec((SSD_CONV_WIDTH, LANES), lambda j: (0, j)), pl.BlockSpec((1, LANES), lambda j: (0, j))],
        out_shape=[jax.ShapeDtypeStruct(dzx.shape, dzx.dtype),
                   jax.ShapeDtypeStruct((SSD_CONV_WIDTH, SSD_CONV_DIM), F32),
                   jax.ShapeDtypeStruct((1, SSD_CONV_DIM), F32)],
        input_output_aliases={6: 0},
        compiler_params=_params("parallel"), name=name)(zx, conv_w, conv_b, d_xs, d_bm, d_cm, dzx)


def _softplus_fwd(dt_raw, bias_row, alog_row, *, name):
    t = dt_raw.shape[0]
    q = SSD_CHUNK
    tr = _row_tile(t, 1024)

    def body(x_ref, b_ref, al_ref, dt_ref, cum_ref):
        v = x_ref[...] + b_ref[...]
        e = jnp.exp(-jnp.abs(v))
        u = 1.0 + e
        log1p = jnp.where(u == 1.0, e, jnp.log(u) * (e / (u - 1.0)))
        dt = jnp.maximum(v, 0.0) + log1p
        a = dt * -jnp.exp(al_ref[...])
        lower = (lax.broadcasted_iota(jnp.int32, (q, q), 1) <= lax.broadcasted_iota(jnp.int32, (q, q), 0)).astype(F32)
        cums = [lax.dot_general(lower, a[c * q:(c + 1) * q, :], ((((1,), (0,))), ((), ())), precision=lax.Precision.HIGHEST,
                                preferred_element_type=F32) for c in range(tr // q)]
        dt_t, cum_t = dt.T, jnp.concatenate(cums, axis=0).T
        for g in range(SSD_N_GROUPS):
            rows = slice(g * SSD_HPG, (g + 1) * SSD_HPG)
            dt_ref[g] = dt_t[rows, :]
            cum_ref[g] = cum_t[rows, :]

    vec = pl.BlockSpec((1, LANES), lambda i: (0, 0))
    by_group = pl.BlockSpec((SSD_N_GROUPS, SSD_HPG, tr), lambda i: (0, 0, i))
    return pl.pallas_call(
        body, grid=(t // tr,),
        in_specs=[pl.BlockSpec((tr, LANES), lambda i: (i, 0)), vec, vec],
        out_specs=[by_group, by_group],
        out_shape=[jax.ShapeDtypeStruct((SSD_N_GROUPS, SSD_HPG, t), F32)] * 2,
        compiler_params=_params("parallel"), name=name)(dt_raw, bias_row, alog_row)


def _softplus_bwd(dt_raw, bias_row, ddt_rows, dzx, *, name):
    t = dt_raw.shape[0]
    tr = _row_tile(t, 1024)
    tail = SSD_IN_PAD - SSD_DT_COL

    def body(x_ref, b_ref, g_ref, _, o_ref, db_ref):
        v = x_ref[...] + b_ref[...]
        lane = lax.broadcasted_iota(jnp.int32, v.shape, 1)
        by_head = jnp.concatenate([g_ref[g] for g in range(SSD_N_GROUPS)]
                                  + [jnp.zeros((LANES - SSD_N_HEADS, tr), F32)], axis=0)
        d = jnp.where(lane < SSD_N_HEADS, by_head.T * _sigmoid(v), 0.0)
        o_ref[:, pl.ds(0, LANES)] = d.astype(o_ref.dtype)
        o_ref[:, pl.ds(LANES, tail - LANES)] = jnp.zeros((tr, tail - LANES), o_ref.dtype)

        @pl.when(pl.program_id(0) == 0)
        def _():
            db_ref[...] = jnp.zeros_like(db_ref)

        db_ref[...] += jnp.sum(d, axis=0, keepdims=True)

    return pl.pallas_call(
        body, grid=(t // tr,),
        in_specs=[pl.BlockSpec((tr, LANES), lambda i: (i, 0)), pl.BlockSpec((1, LANES), lambda i: (0, 0)),
                  pl.BlockSpec((SSD_N_GROUPS, SSD_HPG, tr), lambda i: (0, 0, i)), ANY],
        out_specs=[pl.BlockSpec((tr, tail), lambda i: (i, SSD_DT_COL // tail)), pl.BlockSpec((1, LANES), lambda i: (0, 0))],
        out_shape=[jax.ShapeDtypeStruct(dzx.shape, dzx.dtype), jax.ShapeDtypeStruct((1, LANES), F32)],
        input_output_aliases={3: 0},
        compiler_params=_params("arbitrary"), name=name)(dt_raw, bias_row, ddt_rows, dzx)


def _ssd_masks():
    q = SSD_CHUNK
    tt = lax.broadcasted_iota(jnp.int32, (q, q), 0)
    ss = lax.broadcasted_iota(jnp.int32, (q, q), 1)
    lane = lax.broadcasted_iota(jnp.int32, (1, SSD_GW), 1)
    srow = lax.broadcasted_iota(jnp.int32, (SSD_GW, 1), 0)
    hm = [(lane >= SSD_HEAD_DIM * j) & (lane < SSD_HEAD_DIM * (j + 1)) for j in range(SSD_HPG)]
    rm = [(srow >= SSD_HEAD_DIM * j) & (srow < SSD_HEAD_DIM * (j + 1)) for j in range(SSD_HPG)]
    return tt, ss, hm, rm


def _ssd_head_terms(dt_rows, cum_rows, a_rows, j, tt, ss):
    q = SSD_CHUNK
    dt_row = dt_rows[j:j + 1, :]
    dt_col = jnp.sum(jnp.where(tt == ss, dt_row, 0.0), axis=1, keepdims=True)
    a_row1 = a_rows[j:j + 1, :]
    a_11 = a_rows[j:j + 1, 0:1]
    cum_col = jnp.sum(jnp.where(ss <= tt, dt_row * a_row1, 0.0), axis=1, keepdims=True)
    cum_row = cum_rows[j:j + 1, :]
    decay = jnp.exp(jnp.where(ss <= tt, cum_col - cum_row, -jnp.inf))
    cum_last = cum_col[q - 1:q, :]
    e_col = jnp.exp(cum_col)
    dte_col = jnp.exp(cum_last - cum_col)
    e_last = jnp.exp(cum_last)
    return dt_col, dt_row, a_row1, a_11, decay, e_col, dte_col, e_last


SSD_CHUNKS_PER_STEP = 8
SSD_BC_COL0 = SSD_D_INNER // SSD_D_STATE


def _ssd_head_selects(terms, hm, rm):
    e_all = jnp.zeros((SSD_CHUNK, SSD_GW), F32)
    w_all = jnp.zeros((SSD_CHUNK, SSD_GW), F32)
    e_s = jnp.zeros((SSD_GW, 1), F32)
    for j in range(SSD_HPG):
        dt_col, _, _, _, _, e_col, dte_col, e_last = terms[j]
        e_all = jnp.where(hm[j], e_col, e_all)
        w_all = jnp.where(hm[j], dt_col * dte_col, w_all)
        e_s = jnp.where(rm[j], e_last, e_s)
    return e_all, w_all, e_s


def _ssd_fwd(xc, dtr, cumr, alog_b, d_b, *, name, hook=None):
    t = xc.shape[0]
    q = SSD_CHUNK
    nc = t // q
    kc = min(SSD_CHUNKS_PER_STEP, nc)
    rows = kc * q
    hk = _HookSlots(hook, n_in=7, n_out=2, n_scratch=1)

    def body(*refs):
        (x_ref, b_ref, c_ref, dtr_ref, cumr_ref, alog_ref, d_ref), (y_ref, st_ref), (s_scr,) = hk.own(refs)
        if hook is not None:
            hk.run(refs, pl.program_id(0) * (nc // kc) + pl.program_id(1), SSD_N_GROUPS * (nc // kc))

        @pl.when(pl.program_id(1) == 0)
        def _():
            s_scr[...] = jnp.zeros_like(s_scr)

        tt, ss, hm, rm = _ssd_masks()
        a_rows = -jnp.exp(alog_ref[...])
        d_rows = d_ref[...]
        d_all = jnp.zeros((1, SSD_GW), F32)
        for j in range(SSD_HPG):
            d_all = jnp.where(hm[j], d_rows[j:j + 1, 0:1], d_all)
        ks, hs = range(kc), range(SSD_HPG)
        sl = [pl.ds(k * q, q) for k in ks]
        x = [x_ref[sl[k], :] for k in ks]
        bm = [b_ref[sl[k], :].astype(BF16) for k in ks]
        cm = [c_ref[sl[k], :].astype(BF16) for k in ks]
        xb = [x[k].astype(BF16) for k in ks]
        terms = [[_ssd_head_terms(dtr_ref[:, sl[k]], cumr_ref[:, sl[k]], a_rows, j, tt, ss) for j in hs] for k in ks]
        g = [_dot_nt(cm[k], bm[k]) for k in ks]
        m = [[(g[k] * terms[k][j][4] * terms[k][j][1]).astype(BF16) for j in hs] for k in ks]
        yj = [[_dot_nn(m[k][j], xb[k]) for j in hs] for k in ks]
        sel = [_ssd_head_selects(terms[k], hm, rm) for k in ks]
        upd = [_dot_tn((x[k] * sel[k][1]).astype(BF16), bm[k]) for k in ks]
        states = [s_scr[...]]
        for k in ks:
            states.append(states[k] * sel[k][2] + upd[k])
        inter = [_dot_nt(cm[k], states[k].astype(BF16)) for k in ks]
        ys = []
        for k in ks:
            y = jnp.zeros((q, SSD_GW), F32)
            for j in hs:
                y = jnp.where(hm[j], yj[k][j], y)
            ys.append(y + inter[k] * sel[k][0] + x[k] * d_all)
        for k in ks:
            st_ref[k] = states[k]
        y_ref[...] = jnp.concatenate(ys, axis=0)
        s_scr[...] = states[kc]

    blk = lambda width, off: pl.BlockSpec((rows, width), lambda g, c: (c, off + g))
    par_s = pl.BlockSpec((None, SSD_HPG, LANES), lambda g, c: (g, 0, 0))
    row_s = pl.BlockSpec((None, SSD_HPG, rows), lambda g, c: (g, 0, c))
    outs = pl.pallas_call(
        body, grid=(SSD_N_GROUPS, nc // kc),
        in_specs=[blk(SSD_GW, 0), blk(SSD_D_STATE, SSD_BC_COL0), blk(SSD_D_STATE, SSD_BC_COL0 + SSD_N_GROUPS),
                  row_s, row_s, par_s, par_s] + hk.in_specs,
        out_specs=[blk(SSD_GW, 0), pl.BlockSpec((None, kc, SSD_GW, SSD_D_STATE), lambda g, c: (g, c, 0, 0))] + hk.out_specs,
        out_shape=[jax.ShapeDtypeStruct((t, SSD_D_INNER), F32),
                   jax.ShapeDtypeStruct((SSD_N_GROUPS, nc, SSD_GW, SSD_D_STATE), F32)] + hk.out_shape,
        scratch_shapes=[pltpu.VMEM((SSD_GW, SSD_D_STATE), F32)] + hk.scratch,
        compiler_params=_params(*hk.semantics("parallel", "arbitrary")), name=name)(
            xc, xc, xc, dtr, cumr, alog_b, d_b, *hk.inputs)
    return outs if hook is None else (outs[:2], outs[2:])


def _ssd_bwd(xc, dtr, cumr, alog_b, d_b, states, dy, *, name, hook=None):
    t = xc.shape[0]
    q = SSD_CHUNK
    nc = t // q
    kc = min(SSD_CHUNKS_PER_STEP, nc)
    nst = nc // kc
    rows = kc * q
    rev = lambda c: nst - 1 - c
    hk = _HookSlots(hook, n_in=9, n_out=5, n_scratch=1)

    def body(*refs):
        ((x_ref, b_ref, c_ref, dtr_ref, cumr_ref, alog_ref, d_ref, st_ref, dy_ref),
         (dx_ref, db_ref, dc_ref, ddt_ref, dpar_ref), (ds_scr,)) = hk.own(refs)
        if hook is not None:
            hk.run(refs, pl.program_id(0) * nst + pl.program_id(1), SSD_N_GROUPS * nst)

        @pl.when(pl.program_id(1) == 0)
        def _():
            ds_scr[...] = jnp.zeros_like(ds_scr)
            dpar_ref[...] = jnp.zeros_like(dpar_ref)

        tt, ss, hm, rm = _ssd_masks()
        tcol = lax.broadcasted_iota(jnp.int32, (q, 1), 0)
        lane = lax.broadcasted_iota(jnp.int32, (1, LANES), 1)
        a_rows = -jnp.exp(alog_ref[...])
        d_rows = d_ref[...]
        d_all = jnp.zeros((1, SSD_GW), F32)
        for j in range(SSD_HPG):
            d_all = jnp.where(hm[j], d_rows[j:j + 1, 0:1], d_all)
        ks, hs = range(kc), range(SSD_HPG)
        sl = [pl.ds(k * q, q) for k in ks]
        x = [x_ref[sl[k], :] for k in ks]
        dyv = [dy_ref[sl[k], :] for k in ks]
        bm = [b_ref[sl[k], :].astype(BF16) for k in ks]
        cm = [c_ref[sl[k], :].astype(BF16) for k in ks]
        s_in = [st_ref[k] for k in ks]
        xb = [x[k].astype(BF16) for k in ks]
        dyb = [dyv[k].astype(BF16) for k in ks]
        s_b = [s_in[k].astype(BF16) for k in ks]
        terms = [[_ssd_head_terms(dtr_ref[:, sl[k]], cumr_ref[:, sl[k]], a_rows, j, tt, ss) for j in hs] for k in ks]
        sel = [_ssd_head_selects(terms[k], hm, rm) for k in ks]
        e_all, w_all, e_s = [s_[0] for s_ in sel], [s_[1] for s_ in sel], [s_[2] for s_ in sel]
        dye = [(dyv[k] * e_all[k]).astype(BF16) for k in ks]
        ds_loc = [_dot_tn(dye[k], cm[k]) for k in ks]
        ds = [None] * kc
        running = ds_scr[...]
        for k in reversed(ks):
            ds[k] = running
            running = running * e_s[k] + ds_loc[k]
        ds_scr[...] = running
        ds_b = [ds[k].astype(BF16) for k in ks]
        g = [_dot_nt(cm[k], bm[k]) for k in ks]
        cs = [_dot_nt(cm[k], s_b[k]) for k in ks]
        bds = [_dot_nt(bm[k], ds_b[k]) for k in ks]
        dm = [[_dot_nt(jnp.where(hm[j], dyv[k], 0.0).astype(BF16), xb[k]) for j in hs] for k in ks]
        gl = [[g[k] * terms[k][j][4] for j in hs] for k in ks]
        wp = [[dm[k][j] * gl[k][j] for j in hs] for k in ks]
        mt = [[(gl[k][j] * terms[k][j][1]).astype(BF16) for j in hs] for k in ks]
        dxj = [[_dot_tn(mt[k][j], dyb[k]) for j in hs] for k in ks]
        dg = []
        for k in ks:
            acc = jnp.zeros((q, q), F32)
            for j in hs:
                acc = acc + dm[k][j] * terms[k][j][4] * terms[k][j][1]
            dg.append(acc.astype(BF16))
        dy_cs = [dyv[k] * cs[k] for k in ks]
        x_bds = [x[k] * bds[k] for k in ks]
        dy_x = [dyv[k] * x[k] for k in ks]
        ds_s = [ds[k] * s_in[k] for k in ks]
        w = [[wp[k][j] * terms[k][j][1] for j in hs] for k in ks]
        rw_col = [[jnp.sum(w[k][j], axis=1, keepdims=True) for j in hs] for k in ks]
        cw_row = [[jnp.sum(w[k][j], axis=0, keepdims=True) for j in hs] for k in ks]
        cwp_row = [[jnp.sum(wp[k][j], axis=0, keepdims=True) for j in hs] for k in ks]
        r1_col = [[jnp.sum(jnp.where(hm[j], dy_cs[k], 0.0), axis=1, keepdims=True) * terms[k][j][5] for j in hs] for k in ks]
        dw_col = [[jnp.sum(jnp.where(hm[j], x_bds[k], 0.0), axis=1, keepdims=True) for j in hs] for k in ks]
        head_rows = [slice(j * SSD_HEAD_DIM, (j + 1) * SSD_HEAD_DIM) for j in hs]
        lane_sum = lambda v: jnp.sum(v, axis=1, keepdims=True)
        s_sum = [[lane_sum(jnp.sum(ds_s[k][head_rows[j], :], axis=0, keepdims=True)) for j in hs] for k in ks]
        dy_x_cols = [jnp.sum(dy_x[k], axis=0, keepdims=True) for k in ks]
        d_d = [[lane_sum(jnp.where(hm[j], dy_x_cols[k], 0.0)) for j in hs] for k in ks]
        ddt_rows = [[None] * SSD_HPG for _ in ks]
        dpar = [jnp.zeros((1, LANES), F32) for _ in hs]
        for k in ks:
            for j in hs:
                dt_col, dt_row, a_row1, a_11, _, _, dte_col, e_last = terms[k][j]
                dww = dw_col[k][j] * (dt_col * dte_col)
                last_add = jnp.sum(dww, axis=0, keepdims=True) + e_last * s_sum[k][j]
                dcum_col = rw_col[k][j] + r1_col[k][j] - dww + jnp.where(tcol == q - 1, last_add, 0.0)
                da_row = jnp.sum(jnp.where(tt >= ss, dcum_col, 0.0), axis=0, keepdims=True)
                da_col = jnp.sum(jnp.where(ss >= tt, -cw_row[k][j], 0.0), axis=1, keepdims=True)
                ddt_col = a_11 * da_col + dw_col[k][j] * dte_col
                ddt_rows[k][j] = (a_row1 * da_row + cwp_row[k][j]
                                  + jnp.sum(jnp.where(tt == ss, ddt_col, 0.0), axis=0, keepdims=True))
                d_a = jnp.sum(dt_row * da_row, axis=1, keepdims=True) + jnp.sum(dt_col * da_col, axis=0, keepdims=True)
                dpar[j] = dpar[j] + jnp.where(lane == 0, d_a * a_11, 0.0) + jnp.where(lane == 1, d_d[k][j], 0.0)
        dxs = []
        for k in ks:
            acc = jnp.zeros((q, SSD_GW), F32)
            for j in hs:
                acc = jnp.where(hm[j], dxj[k][j], acc)
            dxs.append(acc + w_all[k] * bds[k] + d_all * dyv[k])
        xw = [(x[k] * w_all[k]).astype(BF16) for k in ks]
        dc = [_dot_nn(dg[k], bm[k]) + _dot_nn(dye[k], s_b[k]) for k in ks]
        db = [_dot_tn(dg[k], cm[k]) + _dot_nn(xw[k], ds_b[k]) for k in ks]
        dx_ref[...] = jnp.concatenate(dxs, axis=0)
        dc_ref[...] = jnp.concatenate(dc, axis=0)
        db_ref[...] = jnp.concatenate(db, axis=0)
        ddt_ref[...] = jnp.concatenate([jnp.concatenate([ddt_rows[k][j] for k in ks], axis=1) for j in hs], axis=0)
        dpar_ref[...] += jnp.concatenate(dpar, axis=0)

    blk = lambda width, off: pl.BlockSpec((rows, width), lambda g, c: (rev(c), off + g))
    par_s = pl.BlockSpec((None, SSD_HPG, LANES), lambda g, c: (g, 0, 0))
    outs = pl.pallas_call(
        body, grid=(SSD_N_GROUPS, nst),
        in_specs=[blk(SSD_GW, 0), blk(SSD_D_STATE, SSD_BC_COL0), blk(SSD_D_STATE, SSD_BC_COL0 + SSD_N_GROUPS),
                  pl.BlockSpec((None, SSD_HPG, rows), lambda g, c: (g, 0, rev(c))),
                  pl.BlockSpec((None, SSD_HPG, rows), lambda g, c: (g, 0, rev(c))), par_s, par_s,
                  pl.BlockSpec((None, kc, SSD_GW, SSD_D_STATE), lambda g, c: (g, rev(c), 0, 0)), blk(SSD_GW, 0)] + hk.in_specs,
        out_specs=[blk(SSD_GW, 0), blk(SSD_D_STATE, 0), blk(SSD_D_STATE, 0),
                   pl.BlockSpec((None, SSD_HPG, rows), lambda g, c: (g, 0, rev(c))), par_s] + hk.out_specs,
        out_shape=[jax.ShapeDtypeStruct((t, SSD_D_INNER), F32),
                   jax.ShapeDtypeStruct((t, SSD_N_GROUPS * SSD_D_STATE), F32),
                   jax.ShapeDtypeStruct((t, SSD_N_GROUPS * SSD_D_STATE), F32),
                   jax.ShapeDtypeStruct((SSD_N_GROUPS, SSD_HPG, t), F32),
                   jax.ShapeDtypeStruct((SSD_N_GROUPS, SSD_HPG, LANES), F32)] + hk.out_shape,
        scratch_shapes=[pltpu.VMEM((SSD_GW, SSD_D_STATE), F32)] + hk.scratch,
        compiler_params=_params(*hk.semantics("parallel", "arbitrary")), name=name)(
            xc, xc, xc, dtr, cumr, alog_b, d_b, states, dy, *hk.inputs)
    return outs if hook is None else (outs[:5], outs[5:])


def _gate_norm_fwd(y, zx, norm_w, *, name):
    t = y.shape[0]
    tr = _row_tile(t, 256)
    row = pl.BlockSpec((tr, SSD_D_INNER), lambda i: (i, 0))

    def body(y_ref, z_ref, w_ref, o_ref):
        for gi in range(SSD_N_GROUPS):
            sl = pl.ds(gi * SSD_GW, SSD_GW)
            z = z_ref[:, sl].astype(F32)
            gv = y_ref[:, sl] * (z * _sigmoid(z))
            r = lax.rsqrt(jnp.mean(gv * gv, axis=-1, keepdims=True) + NORM_EPS)
            o_ref[:, sl] = (gv * r * w_ref[:, sl]).astype(BF16)

    return pl.pallas_call(
        body, grid=(t // tr,), in_specs=[row, row, pl.BlockSpec((1, SSD_D_INNER), lambda i: (0, 0))],
        out_specs=row, out_shape=jax.ShapeDtypeStruct((t, SSD_D_INNER), BF16),
        compiler_params=_params("parallel"), name=name)(y, zx, norm_w)


def _gate_norm_bwd(y, zx, norm_w, dyn, *, name):
    t = y.shape[0]
    tr = _row_tile(t, 256)
    row = pl.BlockSpec((tr, SSD_D_INNER), lambda i: (i, 0))
    vec = pl.BlockSpec((1, SSD_D_INNER), lambda i: (0, 0))

    def body(y_ref, z_ref, w_ref, dyn_ref, dy_ref, dz_ref, dw_ref):
        @pl.when(pl.program_id(0) == 0)
        def _():
            dw_ref[...] = jnp.zeros_like(dw_ref)

        for gi in range(SSD_N_GROUPS):
            sl = pl.ds(gi * SSD_GW, SSD_GW)
            z = z_ref[:, sl].astype(F32)
            yv = y_ref[:, sl]
            sg = _sigmoid(z)
            sz = z * sg
            gv = yv * sz
            r = lax.rsqrt(jnp.mean(gv * gv, axis=-1, keepdims=True) + NORM_EPS)
            ghat = gv * r
            dout = dyn_ref[:, sl].astype(F32)
            dgh = dout * w_ref[:, sl]
            dgv = r * (dgh - ghat * jnp.mean(dgh * ghat, axis=-1, keepdims=True))
            dy_ref[:, sl] = dgv * sz
            dz_ref[:, sl] = (dgv * yv * (sg * (1.0 + z * (1.0 - sg)))).astype(dz_ref.dtype)
            dw_ref[:, sl] += jnp.sum(dout * ghat, axis=0, keepdims=True)

    return pl.pallas_call(
        body, grid=(t // tr,), in_specs=[row, row, vec, row], out_specs=[row, row, vec],
        out_shape=[jax.ShapeDtypeStruct((t, SSD_D_INNER), F32), jax.ShapeDtypeStruct((t, SSD_IN_PAD), BF16),
                   jax.ShapeDtypeStruct((1, SSD_D_INNER), F32)],
        compiler_params=_params("arbitrary"), name=name)(y, zx, norm_w, dyn)


ATTN_KV_W = ATTN_N_KV * ATTN_HEAD_DIM
ATTN_Q_HALF = 512
ATTN_K_BLK = ATTN_N_Q * ATTN_HEAD_DIM // ATTN_KV_W
ATTN_V_BLK = ATTN_K_BLK + 1


def _attn_valid(first_block):
    w = ATTN_WINDOW
    qpos = lax.broadcasted_iota(jnp.int32, (w, 2 * w), 0) + w
    kpos = lax.broadcasted_iota(jnp.int32, (w, 2 * w), 1)
    rel = qpos - kpos
    return (rel >= 0) & (rel < w) & jnp.logical_not(first_block & (kpos < w))


def _attn_head_views(lo_ref, hi_ref):
    hd = ATTN_HEAD_DIM
    per_half = ATTN_Q_HALF // hd
    return [(lo_ref if h < per_half else hi_ref)[:, pl.ds((h % per_half) * hd, hd)] for h in range(ATTN_N_Q)]


def _attn_block_views(lo_ref, hi_ref, kc_ref, kp_ref, vc_ref, vp_ref):
    hd = ATTN_HEAD_DIM
    kv_cols = [pl.ds(kh * hd, hd) for kh in range(ATTN_N_KV)]
    kb = [jnp.concatenate([kp_ref[:, c], kc_ref[:, c]], axis=0) for c in kv_cols]
    vb = [jnp.concatenate([vp_ref[:, c], vc_ref[:, c]], axis=0) for c in kv_cols]
    return _attn_head_views(lo_ref, hi_ref), kb, vb


def _attn_scores(q, kb, valid):
    scale = ATTN_HEAD_DIM ** -0.5
    return [jnp.where(valid, _dot_nt(q[h], kb[h // ATTN_REP]) * scale, -jnp.inf) for h in range(ATTN_N_Q)]


def _attn_softmax(s, sink):
    heads = range(ATTN_N_Q)
    m = [jnp.maximum(jnp.max(s[h], axis=1, keepdims=True), sink[h]) for h in heads]
    e = [jnp.exp(s[h] - m[h]) for h in heads]
    es = [jnp.exp(sink[h] - m[h]) for h in heads]
    inv = [1.0 / (jnp.sum(e[h], axis=1, keepdims=True) + es[h]) for h in heads]
    return e, es, inv


def _attn_fwd(qkv, sinks_b, *, name, hook=None):
    t = qkv.shape[0]
    w = ATTN_WINDOW
    nb = t // w
    prev = lambda n: jnp.maximum(n - 1, 0)
    hk = _HookSlots(hook, n_in=7, n_out=1, n_scratch=0)

    def body(*refs):
        (qlo_ref, qhi_ref, kc_ref, kp_ref, vc_ref, vp_ref, sink_ref), (o_ref,), _ = hk.own(refs)
        if hook is not None:
            hk.run(refs, pl.program_id(0), nb)
        heads = range(ATTN_N_Q)
        q, kb, vb = _attn_block_views(qlo_ref, qhi_ref, kc_ref, kp_ref, vc_ref, vp_ref)
        sink = [sink_ref[h:h + 1, 0:1] for h in heads]
        e, _, inv = _attn_softmax(_attn_scores(q, kb, _attn_valid(pl.program_id(0) == 0)), sink)
        out = [_dot_nn((e[h] * inv[h]).astype(BF16), vb[h // ATTN_REP]).astype(o_ref.dtype) for h in heads]
        o_ref[...] = jnp.concatenate(out, axis=1)

    qh = lambda half: pl.BlockSpec((w, ATTN_Q_HALF), lambda n: (n, half))
    kv = lambda blk, idx: pl.BlockSpec((w, ATTN_KV_W), lambda n: (idx(n), blk))
    cur = lambda n: n
    outs = pl.pallas_call(
        body, grid=(nb,),
        in_specs=[qh(0), qh(1), kv(ATTN_K_BLK, cur), kv(ATTN_K_BLK, prev), kv(ATTN_V_BLK, cur), kv(ATTN_V_BLK, prev),
                  pl.BlockSpec((ATTN_N_Q, LANES), lambda n: (0, 0))] + hk.in_specs,
        out_specs=[pl.BlockSpec((w, D_MODEL), lambda n: (n, 0))] + hk.out_specs,
        out_shape=[jax.ShapeDtypeStruct((t, D_MODEL), BF16)] + hk.out_shape,
        scratch_shapes=hk.scratch,
        compiler_params=_params(*hk.semantics("parallel")), name=name)(qkv, qkv, qkv, qkv, qkv, qkv, sinks_b, *hk.inputs)
    return outs[0] if hook is None else (outs[0], outs[1:])


def _attn_bwd(qkv, sinks_b, dout, *, name):
    t = qkv.shape[0]
    w = ATTN_WINDOW
    nb = t // w
    hd = ATTN_HEAD_DIM
    clamp = lambda n: jnp.minimum(n, nb - 1)
    prev = lambda n: jnp.maximum(clamp(n) - 1, 0)

    def body(qlo_ref, qhi_ref, kc_ref, kp_ref, vc_ref, vp_ref, sink_ref, dolo_ref, dohi_ref,
             dq_ref, dkv_ref, dsink_ref, carry):
        n = pl.program_id(0)

        @pl.when(n == 0)
        def _():
            carry[...] = jnp.zeros_like(carry)
            dsink_ref[...] = jnp.zeros_like(dsink_ref)

        @pl.when(n < nb)
        def _():
            heads, kvs = range(ATTN_N_Q), range(ATTN_N_KV)
            q, kb, vb = _attn_block_views(qlo_ref, qhi_ref, kc_ref, kp_ref, vc_ref, vp_ref)
            do = _attn_head_views(dolo_ref, dohi_ref)
            sink = [sink_ref[h:h + 1, 0:1] for h in heads]
            s = _attn_scores(q, kb, _attn_valid(n == 0))
            dp = [_dot_nt(do[h], vb[h // ATTN_REP]) for h in heads]
            e, es, inv = _attn_softmax(s, sink)
            p = [e[h] * inv[h] for h in heads]
            delta = [jnp.sum(p[h] * dp[h], axis=1, keepdims=True) for h in heads]
            dsc = [(p[h] * (dp[h] - delta[h]) * (hd ** -0.5)).astype(BF16) for h in heads]
            pb = [p[h].astype(BF16) for h in heads]
            dq = [_dot_nn(dsc[h], kb[h // ATTN_REP]).astype(dq_ref.dtype) for h in heads]
            stack = lambda per_head, kh: jnp.concatenate(per_head[kh * ATTN_REP:(kh + 1) * ATTN_REP], axis=0)
            dkb = [_dot_tn(stack(dsc, kh), stack(q, kh)) for kh in kvs]
            dvb = [_dot_tn(stack(pb, kh), stack(do, kh)) for kh in kvs]
            dsink = [jnp.broadcast_to(jnp.sum(-es[h] * inv[h] * delta[h], axis=0, keepdims=True), (1, LANES)) for h in heads]
            dq_ref[...] = jnp.concatenate(dq, axis=1)
            dsink_ref[...] += jnp.concatenate(dsink, axis=0)
            dkv_ref[...] = (carry[...] + jnp.concatenate([d[0:w, :] for d in dkb + dvb], axis=1)).astype(dkv_ref.dtype)
            carry[...] = jnp.concatenate([d[w:2 * w, :] for d in dkb + dvb], axis=1)

        @pl.when(n == nb)
        def _():
            dkv_ref[...] = carry[...].astype(dkv_ref.dtype)

    qh = lambda half: pl.BlockSpec((w, ATTN_Q_HALF), lambda n: (clamp(n), half))
    kv = lambda blk, idx: pl.BlockSpec((w, ATTN_KV_W), lambda n: (idx(n), blk))
    return pl.pallas_call(
        body, grid=(nb + 1,),
        in_specs=[qh(0), qh(1), kv(ATTN_K_BLK, clamp), kv(ATTN_K_BLK, prev), kv(ATTN_V_BLK, clamp), kv(ATTN_V_BLK, prev),
                  pl.BlockSpec((ATTN_N_Q, LANES), lambda n: (0, 0)), qh(0), qh(1)],
        out_specs=[pl.BlockSpec((w, D_MODEL), lambda n: (clamp(n), 0)),
                   pl.BlockSpec((w, 2 * ATTN_KV_W), lambda n: (jnp.maximum(n - 1, 0), 0)),
                   pl.BlockSpec((ATTN_N_Q, LANES), lambda n: (0, 0))],
        out_shape=[jax.ShapeDtypeStruct((t, D_MODEL), BF16), jax.ShapeDtypeStruct((t, 2 * ATTN_KV_W), BF16),
                   jax.ShapeDtypeStruct((ATTN_N_Q, LANES), F32)],
        scratch_shapes=[pltpu.VMEM((w, 2 * ATTN_KV_W), F32)],
        compiler_params=_params("arbitrary"), name=name)(qkv, qkv, qkv, qkv, qkv, qkv, sinks_b, dout, dout)


def _sq_relu_epilogue(acc):
    r = jnp.maximum(acc, 0.0)
    return (r * r,)


def _sq_relu_bwd_epilogue(acc, act):
    return (acc * (2.0 * jnp.sqrt(act.astype(F32))),)


def _bias_epilogue(acc, bias):
    return (acc + bias,)


def _plain_run(stage, fn, *args, **kwargs):
    return fn(*args, **kwargs)


def _mlp_fwd(u, w_up, w_down, tag, run=_plain_run):
    act = run(f"mlp_up_{tag}", _matmul, u, w_up, mode="nn", out_dtypes=(BF16,), epilogue=_sq_relu_epilogue, b_shards=True,
              tm=BIG_TILE, name=f"mlp_up_{tag}")
    f = run(f"mlp_down_{tag}", _matmul, act, w_down, mode="nn", out_dtypes=(F32,), tk=BIG_TILE, name=f"mlp_down_{tag}")
    return act, f


def _mlp_bwd(u, act, w_up, w_down, df, tag):
    dpre = _matmul(df, w_down, mode="nt", out_dtypes=(BF16,), epilogue=_sq_relu_bwd_epilogue,
                   extras=((act, "tile"),), name=f"mlp_dact_{tag}")
    dw_down = _matmul(act, df, mode="tn", out_dtypes=(BF16,), tk=BIG_TILE, name=f"mlp_dwdown_{tag}")
    du = _matmul(dpre, w_up, mode="nt", out_dtypes=(F32,), b_shards=True, tm=BIG_TILE, name=f"mlp_du_{tag}")
    dw_up = _matmul(u, dpre, mode="tn", out_dtypes=(BF16,), out_shards=True, tk=BIG_TILE, name=f"mlp_dwup_{tag}")
    return du, dw_up, dw_down


def _head_param_rows(p):
    return jnp.broadcast_to(p.reshape(SSD_N_GROUPS, SSD_HPG, 1), (SSD_N_GROUPS, SSD_HPG, LANES))


def _local_step(x, target, wts, comm=None):
    t = x.shape[0]
    wts = dict(wts)
    row = lambda v: v.reshape(1, -1)
    mix_pre, mix_post, ffn_pre, ffn_post = wts["mix_pre_norm"], wts["mix_post_norm"], wts["ffn_pre_norm"], wts["ffn_post_norm"]

    def gathering(stage, fn, *args, **kwargs):
        hook = comm.gather_hook(stage) if comm is not None else None
        if hook is None:
            return fn(*args, **kwargs)
        out, got = fn(*args, hook=hook, **kwargs)
        wts.update(comm.weights_from(stage, got))
        return out

    u0 = _rms_fwd(x, row(mix_pre[0]), name="rms_pre_mix0")
    zx, dt_raw = gathering("in_proj", _matmul, u0, wts["ssd_w_in"], mode="nn", out_dtypes=(BF16,), tn=SSD_IN_TILE,
                           f32_block=SSD_DT_COL - (SSD_IN_PAD - SSD_IN_TILE),
                           name="ssd_in_proj")
    xc = gathering("conv", _conv_fwd, zx, wts["ssd_conv_w"], row(wts["ssd_conv_b"]), name="ssd_conv_fwd")
    bias_row = jnp.pad(wts["ssd_dt_bias"], (0, LANES - SSD_N_HEADS)).reshape(1, LANES)
    alog_row = jnp.pad(wts["ssd_a_log"], (0, LANES - SSD_N_HEADS)).reshape(1, LANES)
    dtr, cumr = _softplus_fwd(dt_raw, bias_row, alog_row, name="ssd_dt_fwd")
    alog_b, d_b = _head_param_rows(wts["ssd_a_log"]), _head_param_rows(wts["ssd_d"])
    y_ssd, states = gathering("scan", _ssd_fwd, xc, dtr, cumr, alog_b, d_b, name="ssd_scan_fwd")
    norm_w = row(wts["ssd_norm_w"])
    yn = _gate_norm_fwd(y_ssd, zx, norm_w, name="ssd_gate_norm_fwd")
    mix0 = _matmul(yn, wts["ssd_w_out"], mode="nn", out_dtypes=(F32,), tk=BIG_TILE, name="ssd_out_proj")
    h1, v0 = _rms_fwd(mix0, row(mix_post[0]), resid=x, want_u=row(ffn_pre[0]), name="rms_post_mix0")
    act0, f0 = _mlp_fwd(v0, wts["mlp_w_up0"], wts["mlp_w_down0"], "l0", run=gathering)
    h2, u1 = _rms_fwd(f0, row(ffn_post[0]), resid=h1, want_u=row(mix_pre[1]), name="rms_post_ffn0")

    qkv = _matmul(u1, wts["attn_w_qkv"], mode="nn", out_dtypes=(BF16,), epilogue=_bias_epilogue,
                  extras=((row(wts["attn_b_qkv"]), "row"),), b_shards=True, name="attn_qkv_proj")
    sinks_b = jnp.broadcast_to(wts["attn_sinks"].reshape(ATTN_N_Q, 1), (ATTN_N_Q, LANES))
    ao = gathering("attn_fwd", _attn_fwd, qkv, sinks_b, name="attn_fwd")
    mix1 = _matmul(ao, wts["attn_w_o"], mode="nn", out_dtypes=(F32,), epilogue=_bias_epilogue,
                   extras=((row(wts["attn_b_o"]), "row"),), name="attn_out_proj")
    h3, v1 = _rms_fwd(mix1, row(mix_post[1]), resid=h2, want_u=row(ffn_pre[1]), name="rms_post_mix1")
    act1, f1 = _mlp_fwd(v1, wts["mlp_w_up1"], wts["mlp_w_down1"], "l1")
    dh4, loss_tile = _rms_fwd(f1, row(ffn_post[1]), resid=h3, target=target, name="rms_post_ffn1_loss")

    df1, g_ffn_post1 = _rms_bwd(f1, row(ffn_post[1]), dh4, out_dtype=BF16, name="rms_post_ffn1_bwd")
    dv1, g_up1, g_down1 = _mlp_bwd(v1, act1, wts["mlp_w_up1"], wts["mlp_w_down1"], df1, "l1")
    dh3, g_ffn_pre1 = _rms_bwd(h3, row(ffn_pre[1]), dv1, resid=dh4, name="rms_pre_ffn1_bwd")
    dmix1, g_mix_post1, g_b_o = _rms_bwd(mix1, row(mix_post[1]), dh3, out_dtype=BF16, dx_col_sum=True, name="rms_post_mix1_bwd")
    g_w_o = _matmul(ao, dmix1, mode="tn", out_dtypes=(BF16,), tk=BIG_TILE, name="attn_dwo")
    dao = _matmul(dmix1, wts["attn_w_o"], mode="nt", out_dtypes=(BF16,), name="attn_dao")
    dq, dkv, g_sinks = _attn_bwd(qkv, sinks_b, dao, name="attn_bwd")
    dqkv = jnp.concatenate([dq, dkv], axis=1)
    g_b_qkv = _col_sum(dqkv, name="attn_bqkv_grad")
    g_w_qkv = _matmul(u1, dqkv, mode="tn", out_dtypes=(BF16,), tn=ATTN_QKV // N_CHIPS, out_shards=True, tk=BIG_TILE, name="attn_dwqkv")
    du1 = _matmul(dqkv, wts["attn_w_qkv"], mode="nt", out_dtypes=(F32,), b_shards=True, name="attn_du")
    dh2, g_mix_pre1 = _rms_bwd(h2, row(mix_pre[1]), du1, resid=dh3, name="rms_pre_mix1_bwd")

    df0, g_ffn_post0 = _rms_bwd(f0, row(ffn_post[0]), dh2, out_dtype=BF16, name="rms_post_ffn0_bwd")
    dv0, g_up0, g_down0 = _mlp_bwd(v0, act0, wts["mlp_w_up0"], wts["mlp_w_down0"], df0, "l0")
    dh1, g_ffn_pre0 = _rms_bwd(h1, row(ffn_pre[0]), dv0, resid=dh2, name="rms_pre_ffn0_bwd")
    dmix0, g_mix_post0 = _rms_bwd(mix0, row(mix_post[0]), dh1, out_dtype=BF16, name="rms_post_mix0_bwd")
    g_w_out = _matmul(yn, dmix0, mode="tn", out_dtypes=(BF16,), tk=BIG_TILE, name="ssd_dwout")
    dyn = _matmul(dmix0, wts["ssd_w_out"], mode="nt", out_dtypes=(BF16,), name="ssd_dyn")
    dy_ssd, dzx, g_norm_w = _gate_norm_bwd(y_ssd, zx, norm_w, dyn, name="ssd_gate_norm_bwd")
    mats = {"ssd_w_out": g_w_out, "attn_w_qkv": g_w_qkv, "attn_w_o": g_w_o,
            "mlp_w_up0": g_up0, "mlp_w_up1": g_up1, "mlp_w_down0": g_down0, "mlp_w_down1": g_down1}
    if comm is None:
        dxc, dbm, dcm, ddt_r, dpar = _ssd_bwd(xc, dtr, cumr, alog_b, d_b, states, dy_ssd, name="ssd_scan_bwd")
    else:
        (dxc, dbm, dcm, ddt_r, dpar), received = _ssd_bwd(xc, dtr, cumr, alog_b, d_b, states, dy_ssd,
                                                          name="ssd_scan_bwd", hook=comm.exchange_hook(mats, "early"))
        comm.received(received)
    dzx, g_conv_w, g_conv_b = _conv_bwd(zx, wts["ssd_conv_w"], row(wts["ssd_conv_b"]), dxc, dbm, dcm, dzx, name="ssd_conv_bwd")
    dzx, g_dt_bias = _softplus_bwd(dt_raw, bias_row, ddt_r, dzx, name="ssd_dt_bwd")
    g_w_in = _w_in_to_shards(_matmul(u0, dzx, mode="tn", out_dtypes=(BF16,), tn=SSD_IN_TILE, tk=BIG_TILE, name="ssd_dwin"), name="ssd_dwin_shards")
    mats["ssd_w_in"] = g_w_in
    if comm is None:
        du0 = _matmul(dzx, wts["ssd_w_in"], mode="nt", out_dtypes=(F32,), tk=SSD_IN_TILE, name="ssd_du")
    else:
        du0, received = _matmul(dzx, wts["ssd_w_in"], mode="nt", out_dtypes=(F32,), tk=SSD_IN_TILE, name="ssd_du",
                                hook=comm.exchange_hook(mats, "late"))
        comm.received(received)
    grad_x, g_mix_pre0 = _rms_bwd(x, row(mix_pre[0]), du0, resid=dh1, name="rms_pre_mix0_bwd")

    dpar = dpar.reshape(SSD_N_HEADS, LANES)
    vecs = {
        "ssd_conv_w": g_conv_w, "ssd_conv_b": g_conv_b.reshape(-1),
        "ssd_dt_bias": g_dt_bias[0, :SSD_N_HEADS], "ssd_a_log": dpar[:, 0], "ssd_d": dpar[:, 1],
        "ssd_norm_w": g_norm_w.reshape(-1), "attn_b_qkv": g_b_qkv.reshape(-1), "attn_sinks": g_sinks[:, 0],
        "attn_b_o": g_b_o.reshape(-1),
        "mix_pre_norm": jnp.concatenate([g_mix_pre0, g_mix_pre1]), "mix_post_norm": jnp.concatenate([g_mix_post0, g_mix_post1]),
        "ffn_pre_norm": jnp.concatenate([g_ffn_pre0, g_ffn_pre1]), "ffn_post_norm": jnp.concatenate([g_ffn_post0, g_ffn_post1]),
    }
    return loss_tile, grad_x, mats, vecs


def _mesh_position():
    return lax.axis_index("x"), lax.axis_index("y"), lax.axis_index("c")


def _flip(v, bit):
    return 1 - v if bit else v


OTHER_CHIPS = ((1, 0), (0, 1), (1, 1))


def _comm_params():
    return pltpu.CompilerParams(vmem_limit_bytes=VMEM_LIMIT)


def _staged_copies(srcs, dsts, bufs, sems_in, sems_out):
    loads = [pltpu.make_async_copy(s, b, sems_in.at[i]) for i, (s, b) in enumerate(zip(srcs, bufs))]
    stores = [pltpu.make_async_copy(b, d, sems_out.at[i]) for i, (b, d) in enumerate(zip(bufs, dsts))]
    return loads, stores


class _GatherHook:
    def __init__(self, mats, vecs=()):
        self.arrs = list(mats) + list(vecs)
        self.nm, self.n = len(mats), len(self.arrs)
        n_ici, n_fwd = (N_CHIPS - 1) * self.n, max((N_CHIPS - 1) * self.nm, 1)
        dma = pltpu.SemaphoreType.DMA
        self.out_shape = [jax.ShapeDtypeStruct((N_CHIPS,) + a.shape, a.dtype) for a in self.arrs]
        self.scratch = [pltpu.VMEM(a.shape, a.dtype) for a in self.arrs] + [
            dma((n_ici,)), dma((n_ici,)), dma((n_fwd,)), dma((n_fwd,)), dma((self.n,)), dma((self.n,))]

    def plan(self, ins, outs, scratch):
        n, nm = self.n, self.nm
        bufs = scratch[:n]
        ici_send, ici_recv, fwd_send, fwd_recv, load_sems, store_sems = scratch[n:]
        xi, yi, ci = _mesh_position()
        me = 2 * xi + yi
        loads, stores = _staged_copies(ins, [outs[i].at[me] for i in range(n)], bufs, load_sems, store_sems)
        sends, landed, forwards, from_sibling = [], [], [], []
        for j, (bx, by) in enumerate(OTHER_CHIPS):
            px, py = _flip(xi, bx), _flip(yi, by)
            peer = 2 * px + py
            for i in range(n):
                k = j * n + i
                mk = functools.partial(pltpu.make_async_remote_copy, send_sem=ici_send.at[k], recv_sem=ici_recv.at[k],
                                       device_id=(px, py, ci), device_id_type=MESH)
                if i < nm:
                    sends.append(mk(src_ref=ins[i].at[ci], dst_ref=outs[i].at[me, ci]))
                    landed.append(mk(src_ref=ins[i].at[ci], dst_ref=outs[i].at[peer, ci]))
                    kf = j * nm + i
                    fw = functools.partial(pltpu.make_async_remote_copy, send_sem=fwd_send.at[kf], recv_sem=fwd_recv.at[kf],
                                           device_id=(xi, yi, 1 - ci), device_id_type=MESH)
                    forwards.append(fw(src_ref=outs[i].at[peer, ci], dst_ref=outs[i].at[peer, ci]))
                    from_sibling.append(fw(src_ref=outs[i].at[peer, ci], dst_ref=outs[i].at[peer, 1 - ci]))
                else:
                    sends.append(mk(src_ref=ins[i], dst_ref=outs[i].at[me]))
                    landed.append(mk(src_ref=ins[i], dst_ref=outs[i].at[peer]))
                    forwards.append(None)
        return loads, stores, sends, landed, forwards, from_sibling

    @staticmethod
    def start(p):
        loads, _, sends, _, _, _ = p
        for cp in loads + sends:
            cp.start()

    @staticmethod
    def relay(p):
        loads, stores, _, landed, forwards, _ = p
        for ld, st in zip(loads, stores):
            ld.wait()
            st.start()
        for cp, fw in zip(landed, forwards):
            cp.wait_recv()
            if fw is not None:
                fw.start()

    @staticmethod
    def finish(p):
        _, stores, sends, _, forwards, from_sibling = p
        for cp in from_sibling:
            cp.wait_recv()
        for cp in sends + [fw for fw in forwards if fw is not None]:
            cp.wait_send()
        for st in stores:
            st.wait()


def _run_hook(hook, ins, outs, scratch, step, n_steps):
    p = hook.plan(ins, outs, scratch)
    relay_step = min(max(1, (3 * n_steps) // 4), n_steps - 1)

    @pl.when(step == 0)
    def _():
        hook.start(p)

    if relay_step < n_steps - 1:
        @pl.when(step == relay_step)
        def _():
            hook.relay(p)

    @pl.when(step == n_steps - 1)
    def _():
        if relay_step == n_steps - 1:
            hook.relay(p)
        hook.finish(p)


def _hook_call(hook, *, name):
    n = len(hook.arrs)

    def body(*refs):
        p = hook.plan(refs[:n], refs[n:n + len(hook.out_shape)], refs[n + len(hook.out_shape):])
        hook.start(p)
        hook.relay(p)
        hook.finish(p)

    return pl.pallas_call(
        body, in_specs=[ANY] * n, out_specs=[ANY] * len(hook.out_shape), out_shape=hook.out_shape,
        scratch_shapes=hook.scratch, compiler_params=_comm_params(), name=name)(*hook.arrs)


def _send_other_half(parts, *, name):
    n = len(parts)

    def body(*refs):
        ins, outs = refs[:n], refs[n:2 * n]
        send_sems, recv_sems = refs[2 * n:]
        xi, yi, ci = _mesh_position()
        sibling = (xi, yi, 1 - ci)
        for i in range(n):
            for s in range(N_CHIPS):
                pltpu.make_async_remote_copy(src_ref=ins[i].at[s, 1 - ci], dst_ref=outs[i].at[s], send_sem=send_sems.at[i],
                                             recv_sem=recv_sems.at[i], device_id=sibling, device_id_type=MESH).start()
        for i in range(n):
            pltpu.make_async_remote_copy(src_ref=outs[i], dst_ref=outs[i], send_sem=send_sems.at[i], recv_sem=recv_sems.at[i],
                                         device_id=sibling, device_id_type=MESH).wait()

    return pl.pallas_call(
        body, in_specs=[ANY] * n, out_specs=[ANY] * n,
        out_shape=[jax.ShapeDtypeStruct((p.shape[0],) + p.shape[2:], p.dtype) for p in parts],
        scratch_shapes=[pltpu.SemaphoreType.DMA((n,)), pltpu.SemaphoreType.DMA((n,))],
        name=name)(*parts)


ROW_BLOCKS = 8


def _add_sibling_half(parts, theirs, core, *, name):
    n = len(parts)

    def body(core_ref, *refs):
        for a_ref, b_ref, o_ref in zip(refs[:n], refs[n:2 * n], refs[2 * n:]):
            o_ref[...] = (a_ref[...].astype(F32) + b_ref[...].astype(F32)).astype(o_ref.dtype)

    mine = lambda p: pl.BlockSpec((None, None, p.shape[2] // ROW_BLOCKS, p.shape[3]), lambda s, rb, core_ref: (s, core_ref[0], rb, 0))
    other = lambda p: pl.BlockSpec((None, p.shape[1] // ROW_BLOCKS, p.shape[2]), lambda s, rb, core_ref: (s, rb, 0))
    return pl.pallas_call(
        body,
        grid_spec=pltpu.PrefetchScalarGridSpec(
            num_scalar_prefetch=1, grid=(N_CHIPS, ROW_BLOCKS),
            in_specs=[mine(p) for p in parts] + [other(q) for q in theirs], out_specs=[other(q) for q in theirs]),
        out_shape=[jax.ShapeDtypeStruct(q.shape, BF16) for q in theirs],
        compiler_params=_params("parallel", "parallel"), name=name)(core, *parts, *theirs)


class _ExchangeHook:
    def __init__(self, parts, to_all=()):
        self.arrs = list(parts) + list(to_all)
        self.n_parts, self.n = len(parts), len(self.arrs)
        n_ici, n_peer = max((N_CHIPS - 1) * self.n_parts, 1), (N_DEV - 1) * max(len(to_all), 1)
        dma = pltpu.SemaphoreType.DMA
        self.out_shape = [jax.ShapeDtypeStruct(p.shape, p.dtype) for p in parts] + [
            jax.ShapeDtypeStruct((N_DEV,) + a.shape, a.dtype) for a in to_all]
        self.scratch = [pltpu.VMEM(p.shape[1:], p.dtype) for p in parts] + [pltpu.VMEM(a.shape, a.dtype) for a in to_all] + [
            dma((n_ici,)), dma((n_ici,)), dma((n_peer,)), dma((n_peer,)), dma((self.n,)), dma((self.n,))]

    def plan(self, ins, outs, scratch):
        n, npt = self.n, self.n_parts
        bufs = scratch[:n]
        send_sems, recv_sems, all_send, all_recv, load_sems, store_sems = scratch[n:]
        xi, yi, ci = _mesh_position()
        me_chip = 2 * xi + yi
        me = 4 * xi + 2 * yi + ci
        loads, stores = _staged_copies([ins[i].at[me_chip] for i in range(npt)] + list(ins[npt:]),
                                       [outs[i].at[me_chip] for i in range(npt)] + [outs[i].at[me] for i in range(npt, n)],
                                       bufs, load_sems, store_sems)
        sends, recvs = [], []
        for j, (bx, by) in enumerate(OTHER_CHIPS):
            px, py = _flip(xi, bx), _flip(yi, by)
            peer = 2 * px + py
            for i in range(npt):
                k = j * npt + i
                mk = functools.partial(pltpu.make_async_remote_copy, src_ref=ins[i].at[peer], send_sem=send_sems.at[k],
                                       recv_sem=recv_sems.at[k], device_id=(px, py, ci), device_id_type=MESH)
                sends.append(mk(dst_ref=outs[i].at[me_chip]))
                recvs.append(mk(dst_ref=outs[i].at[peer]))
        for i in range(npt, n):
            for k in range(1, N_DEV):
                px, py, pc = _flip(xi, (k >> 2) & 1), _flip(yi, (k >> 1) & 1), _flip(ci, k & 1)
                slot = (i - npt) * (N_DEV - 1) + k - 1
                mk = functools.partial(pltpu.make_async_remote_copy, src_ref=ins[i], send_sem=all_send.at[slot],
                                       recv_sem=all_recv.at[slot], device_id=(px, py, pc), device_id_type=MESH)
                sends.append(mk(dst_ref=outs[i].at[me]))
                recvs.append(mk(dst_ref=outs[i].at[4 * px + 2 * py + pc]))
        return loads, stores, sends, recvs

    @staticmethod
    def start(p):
        loads, _, sends, _ = p
        for cp in loads + sends:
            cp.start()

    @staticmethod
    def relay(p):
        loads, stores, _, _ = p
        for ld, st in zip(loads, stores):
            ld.wait()
            st.start()

    @staticmethod
    def finish(p):
        _, stores, sends, recvs = p
        for cp in recvs:
            cp.wait_recv()
        for cp in sends:
            cp.wait_send()
        for st in stores:
            st.wait()


def _sum_chips(parts, *, name):
    n = len(parts)
    p = parts[0].shape[0]

    def body(*refs):
        s = pl.program_id(1)
        for x_ref, o_ref in zip(refs[:n], refs[n:]):
            @pl.when(s == 0)
            def _():
                o_ref[...] = x_ref[...].astype(F32)

            @pl.when(s > 0)
            def _():
                o_ref[...] += x_ref[...].astype(F32)

    blocks = lambda q: ROW_BLOCKS if q.shape[1] % (8 * ROW_BLOCKS) == 0 else 1
    assert len({blocks(q) for q in parts}) == 1
    nb = blocks(parts[0])
    return pl.pallas_call(
        body, grid=(nb, p),
        in_specs=[pl.BlockSpec((None, q.shape[1] // nb, q.shape[2]), lambda rb, s: (s, rb, 0)) for q in parts],
        out_specs=[pl.BlockSpec((q.shape[1] // nb, q.shape[2]), lambda rb, s: (rb, 0)) for q in parts],
        out_shape=[jax.ShapeDtypeStruct(q.shape[1:], F32) for q in parts],
        compiler_params=_params("parallel", "arbitrary"), name=name)(*parts)


def _swap_halves(halves, layers, *, name):
    n = len(halves)
    out_shapes, slots = [], []
    for i, h in enumerate(halves):
        pair = [p for p in layers if i in p]
        if pair and pair[0][1] == i:
            slots.append((slots[pair[0][0]][0], 1))
        elif pair:
            out_shapes.append(jax.ShapeDtypeStruct((2, 2) + h.shape, h.dtype))
            slots.append((len(out_shapes) - 1, 0))
        else:
            out_shapes.append(jax.ShapeDtypeStruct((2,) + h.shape, h.dtype))
            slots.append((len(out_shapes) - 1, None))
    n_out = len(out_shapes)

    def body(*refs):
        ins, outs, bufs = refs[:n], refs[n:n + n_out], refs[n + n_out:2 * n + n_out]
        send_sems, recv_sems, load_sems, store_sems = refs[2 * n + n_out:]
        xi, yi, ci = _mesh_position()
        own, sends, recvs = [], [], []
        for i in range(n):
            o, layer = slots[i]
            dst = (lambda core: outs[o].at[core]) if layer is None else (lambda core: outs[o].at[layer, core])
            own.append(dst(ci))
            mk = functools.partial(pltpu.make_async_remote_copy, src_ref=ins[i], send_sem=send_sems.at[i],
                                   recv_sem=recv_sems.at[i], device_id=(xi, yi, 1 - ci), device_id_type=MESH)
            sends.append(mk(dst_ref=dst(ci)))
            recvs.append(mk(dst_ref=dst(1 - ci)))
        loads, stores = _staged_copies(ins, own, bufs, load_sems, store_sems)
        for cp in loads + sends:
            cp.start()
        for ld, st in zip(loads, stores):
            ld.wait()
            st.start()
        for cp in recvs:
            cp.wait_recv()
        for cp in sends:
            cp.wait_send()
        for st in stores:
            st.wait()

    return pl.pallas_call(
        body, in_specs=[ANY] * n, out_specs=[ANY] * n_out, out_shape=out_shapes,
        scratch_shapes=[pltpu.VMEM(h.shape, h.dtype) for h in halves]
        + [pltpu.SemaphoreType.DMA((n,)), pltpu.SemaphoreType.DMA((n,)), pltpu.SemaphoreType.DMA((n,)), pltpu.SemaphoreType.DMA((n,))],
        compiler_params=_comm_params(), name=name)(*halves)


def _cast_bf16(layers, *, name, hook=None):
    n = len(layers)
    hk = _HookSlots(hook, n_in=n, n_out=n, n_scratch=0)

    def body(*refs):
        ins, outs, _ = hk.own(refs)
        if hook is not None:
            hk.run(refs, pl.program_id(0), ROW_BLOCKS)
        for i_ref, o_ref in zip(ins, outs):
            o_ref[...] = i_ref[...].astype(o_ref.dtype)

    in_blk = lambda a, l: pl.BlockSpec((None, a.shape[1] // ROW_BLOCKS, a.shape[2]), lambda i: (l, i, 0))
    out_blk = lambda a: pl.BlockSpec((a.shape[1] // ROW_BLOCKS, a.shape[2]), lambda i: (i, 0))
    outs = pl.pallas_call(
        body, grid=(ROW_BLOCKS,),
        in_specs=[in_blk(a, l) for a, l in layers] + hk.in_specs,
        out_specs=[out_blk(a) for a, _ in layers] + hk.out_specs,
        out_shape=[jax.ShapeDtypeStruct(a.shape[1:], BF16) for a, _ in layers] + hk.out_shape,
        scratch_shapes=hk.scratch,
        compiler_params=_params(*hk.semantics("parallel")), name=name)(*[a for a, _ in layers], *hk.inputs)
    return outs[:n] if hook is None else (outs[:n], outs[n:])


def _full_weight(name, gathered):
    s, _, r, c = gathered.shape
    if name == "ssd_w_in":
        return _w_in_from_shards(gathered.reshape(s, 2 * r, c), name="ssd_w_in_unshard")
    if name in ("attn_w_qkv", "mlp_w_up0", "mlp_w_up1"):
        return gathered.reshape(s, 2 * r, c)
    return gathered.reshape(s * 2 * r, c)


class _StepComm:
    GATHER = {"in_proj": ("mlp_w_up0", "attn_w_o"), "conv": ("mlp_w_down0",), "scan": ("ssd_w_out", "mlp_w_up1"),
              "mlp_up_l0": ("attn_w_qkv",), "attn_fwd": ("mlp_w_down1",)}
    EXCHANGE = {"early": ("ssd_w_out", "attn_w_qkv", "attn_w_o", "mlp_w_up0", "mlp_w_up1", "mlp_w_down0", "mlp_w_down1"),
                "late": ("ssd_w_in",)}

    def __init__(self, shards, core):
        self.shards, self.core = shards, core
        self.chip_parts = {}
        self._pending = None

    def gather_hook(self, stage):
        names = self.GATHER.get(stage)
        return _GatherHook([self.shards[n] for n in names]) if names else None

    def weights_from(self, stage, gathered):
        return {n: _full_weight(n, g) for n, g in zip(self.GATHER[stage], gathered)}

    def chip_sums(self, mats, tag):
        parts = [_shard_halves(a) for a in mats.values()]
        theirs = _send_other_half(parts, name=f"grad_sibling_send_{tag}")
        return _add_sibling_half(parts, theirs, self.core, name=f"grad_chip_sum_{tag}")

    def exchange_hook(self, mats, which):
        self._pending = self.EXCHANGE[which]
        return _ExchangeHook(self.chip_sums({n: mats[n] for n in self._pending}, which))

    def received(self, arrays):
        self.chip_parts.update(zip(self._pending, arrays))


ADAMW_ROW_BLOCKS = 16


def _adamw(ws, gs, ms, vs, *, name):
    n = len(ws)
    nb = ADAMW_ROW_BLOCKS if all(a.shape[0] % (8 * ADAMW_ROW_BLOCKS) == 0 for a in ws) else 1

    def body(*refs):
        ins, outs = refs[:4 * n], refs[4 * n:]
        for i in range(n):
            w_ref, g_ref, m_ref, v_ref = ins[i], ins[n + i], ins[2 * n + i], ins[3 * n + i]
            go_ref, d_ref, nm_ref, nv_ref = outs[i], outs[n + i], outs[2 * n + i], outs[3 * n + i]
            gv = g_ref[...]
            nm = ADAM_B1 * m_ref[...] + (1.0 - ADAM_B1) * gv
            nv = ADAM_B2 * v_ref[...] + (1.0 - ADAM_B2) * (gv * gv)
            m_hat = nm / (1.0 - ADAM_B1 ** ADAM_STEP)
            v_hat = nv / (1.0 - ADAM_B2 ** ADAM_STEP)
            go_ref[...] = gv
            d_ref[...] = -ADAM_LR * (m_hat / (jnp.sqrt(v_hat) + ADAM_EPS) + ADAM_WD * w_ref[...])
            nm_ref[...] = nm
            nv_ref[...] = nv

    blks = [pl.BlockSpec((a.shape[0] // nb, a.shape[1]), lambda i: (i, 0)) for a in ws]
    shapes = [jax.ShapeDtypeStruct(a.shape, F32) for a in ws]
    outs = pl.pallas_call(body, grid=(nb,), in_specs=blks * 4, out_specs=blks * 4, out_shape=shapes * 4,
                          compiler_params=_params("parallel"), name=name)(*ws, *gs, *ms, *vs)
    return [tuple(outs[k * n + i] for k in range(4)) for i in range(n)]


SM_CONV_B, SM_NORM_W, SM_MIX_PRE, SM_MIX_POST, SM_FFN_PRE, SM_FFN_POST, SM_MISC, SM_CONV_W, SM_B_QKV, SM_B_O = 0, 4, 6, 8, 10, 12, 14, 16, 32, 34
SM_ROWS = 40
MISC_DT_BIAS, MISC_A_LOG, MISC_D, MISC_SINKS, MISC_LOSS = 0, 32, 64, 96, 112


def _shard_halves(a):
    c = a.shape[-1]
    return a.reshape(N_CHIPS, 2, -1, c)


def _rows(v):
    return v.reshape(-1, D_MODEL)


def _misc_row(dt_bias, a_log, d, sinks, loss):
    pad = jnp.zeros((D_MODEL - MISC_LOSS - 1,), F32)
    return jnp.concatenate([dt_bias.reshape(-1), a_log.reshape(-1), d.reshape(-1), sinks.reshape(-1), loss.reshape(1), pad]).reshape(1, D_MODEL)


def _replicated_rows(p, loss):
    return jnp.concatenate([
        _rows(p["ssd_conv_b"]), _rows(p["ssd_norm_w"]), _rows(p["mix_pre_norm"]), _rows(p["mix_post_norm"]),
        _rows(p["ffn_pre_norm"]), _rows(p["ffn_post_norm"]),
        _misc_row(p["ssd_dt_bias"], p["ssd_a_log"], p["ssd_d"], p["attn_sinks"], loss), jnp.zeros((1, D_MODEL), F32)], axis=0)


def _sharded_rows(conv_w, b_qkv, b_o):
    last = jnp.concatenate([b_qkv.reshape(-1), b_o.reshape(-1), jnp.zeros((D_MODEL - 640,), F32)]).reshape(1, D_MODEL)
    return jnp.concatenate([conv_w.reshape(SSD_CONV_WIDTH, D_MODEL), last, jnp.zeros((3, D_MODEL), F32)], axis=0)


REPLICATED = ("ssd_conv_b", "ssd_dt_bias", "ssd_a_log", "ssd_d", "ssd_norm_w", "attn_sinks",
              "mix_pre_norm", "mix_post_norm", "ffn_pre_norm", "ffn_post_norm")
MATRICES = ("ssd_w_in", "ssd_w_out", "attn_w_qkv", "attn_w_o", "mlp_w_up", "mlp_w_down")
WEIGHT_NAMES = ("ssd_w_in", "ssd_conv_w", "ssd_conv_b", "ssd_dt_bias", "ssd_a_log", "ssd_d", "ssd_norm_w", "ssd_w_out",
                "attn_w_qkv", "attn_b_qkv", "attn_sinks", "attn_w_o", "attn_b_o", "mlp_w_up", "mlp_w_down",
                "mix_pre_norm", "mix_post_norm", "ffn_pre_norm", "ffn_post_norm")


def _unpack_small(rows16, rows8, like):
    misc = rows16[SM_MISC]
    out = {
        "ssd_conv_b": rows16[SM_CONV_B:SM_CONV_B + 4], "ssd_norm_w": rows16[SM_NORM_W:SM_NORM_W + 2],
        "mix_pre_norm": rows16[SM_MIX_PRE:SM_MIX_PRE + 2], "mix_post_norm": rows16[SM_MIX_POST:SM_MIX_POST + 2],
        "ffn_pre_norm": rows16[SM_FFN_PRE:SM_FFN_PRE + 2], "ffn_post_norm": rows16[SM_FFN_POST:SM_FFN_POST + 2],
        "ssd_dt_bias": misc[MISC_DT_BIAS:MISC_DT_BIAS + 32], "ssd_a_log": misc[MISC_A_LOG:MISC_A_LOG + 32],
        "ssd_d": misc[MISC_D:MISC_D + 32], "attn_sinks": misc[MISC_SINKS:MISC_SINKS + 16],
        "ssd_conv_w": rows8[0:SSD_CONV_WIDTH], "attn_b_qkv": rows8[SSD_CONV_WIDTH, 0:384], "attn_b_o": rows8[SSD_CONV_WIDTH, 384:640],
    }
    return {k: v.reshape(like[k].shape) for k, v in out.items()}


def kernel(x, ssd_w_in, ssd_conv_w, ssd_conv_b, ssd_dt_bias, ssd_a_log, ssd_d, ssd_norm_w, ssd_w_out, attn_w_qkv, attn_b_qkv, attn_sinks, attn_w_o, attn_b_o, mlp_w_up, mlp_w_down, mix_pre_norm, mix_post_norm, ffn_pre_norm, ffn_post_norm, loss_target, m_ssd_w_in, m_ssd_conv_w, m_ssd_conv_b, m_ssd_dt_bias, m_ssd_a_log, m_ssd_d, m_ssd_norm_w, m_ssd_w_out, m_attn_w_qkv, m_attn_b_qkv, m_attn_sinks, m_attn_w_o, m_attn_b_o, m_mlp_w_up, m_mlp_w_down, m_mix_pre_norm, m_mix_post_norm, m_ffn_pre_norm, m_ffn_post_norm, v_ssd_w_in, v_ssd_conv_w, v_ssd_conv_b, v_ssd_dt_bias, v_ssd_a_log, v_ssd_d, v_ssd_norm_w, v_ssd_w_out, v_attn_w_qkv, v_attn_b_qkv, v_attn_sinks, v_attn_w_o, v_attn_b_o, v_mlp_w_up, v_mlp_w_down, v_mix_pre_norm, v_mix_post_norm, v_ffn_pre_norm, v_ffn_post_norm):
    w = dict(zip(WEIGHT_NAMES, (ssd_w_in, ssd_conv_w, ssd_conv_b, ssd_dt_bias, ssd_a_log, ssd_d, ssd_norm_w, ssd_w_out, attn_w_qkv, attn_b_qkv, attn_sinks, attn_w_o, attn_b_o, mlp_w_up, mlp_w_down, mix_pre_norm, mix_post_norm, ffn_pre_norm, ffn_post_norm)))
    m = dict(zip(WEIGHT_NAMES, (m_ssd_w_in, m_ssd_conv_w, m_ssd_conv_b, m_ssd_dt_bias, m_ssd_a_log, m_ssd_d, m_ssd_norm_w, m_ssd_w_out, m_attn_w_qkv, m_attn_b_qkv, m_attn_sinks, m_attn_w_o, m_attn_b_o, m_mlp_w_up, m_mlp_w_down, m_mix_pre_norm, m_mix_post_norm, m_ffn_pre_norm, m_ffn_post_norm)))
    v = dict(zip(WEIGHT_NAMES, (v_ssd_w_in, v_ssd_conv_w, v_ssd_conv_b, v_ssd_dt_bias, v_ssd_a_log, v_ssd_d, v_ssd_norm_w, v_ssd_w_out, v_attn_w_qkv, v_attn_b_qkv, v_attn_sinks, v_attn_w_o, v_attn_b_o, v_mlp_w_up, v_mlp_w_down, v_mix_pre_norm, v_mix_post_norm, v_ffn_pre_norm, v_ffn_post_norm)))
    chip = 2 * lax.axis_index("x") + lax.axis_index("y")

    two_halves = lambda a: a.reshape(2, a.shape[0] // 2, a.shape[1])
    later = {"ssd_w_out": (w["ssd_w_out"], 0), "attn_w_qkv": (w["attn_w_qkv"], 0), "attn_w_o": (w["attn_w_o"], 0),
             "mlp_w_up0": (w["mlp_w_up"], 0), "mlp_w_up1": (w["mlp_w_up"], 1),
             "mlp_w_down0": (w["mlp_w_down"], 0), "mlp_w_down1": (w["mlp_w_down"], 1)}
    first = _GatherHook([two_halves(w["ssd_w_in"][0].astype(BF16))], [w["ssd_conv_w"][0], w["attn_b_qkv"], w["attn_b_o"]])
    cast, (g_in, g_conv, g_bqkv, g_bo) = _cast_bf16(list(later.values()), name="weights_to_bf16", hook=first)
    core = lax.axis_index("c").astype(jnp.int32).reshape(1)
    comm = _StepComm({k: two_halves(a) for k, a in zip(later, cast)}, core)
    full = {
        "ssd_w_in": _full_weight("ssd_w_in", g_in),
        "ssd_conv_w": g_conv.transpose(1, 0, 2).reshape(SSD_CONV_WIDTH, SSD_CONV_DIM),
        "attn_b_qkv": g_bqkv.reshape(ATTN_QKV), "attn_b_o": g_bo.reshape(D_MODEL),
    }
    for name in REPLICATED:
        full[name] = w[name][0] if name.startswith(("ssd_", "attn_")) else w[name]

    loss_tile, grad_x, gm, g = _local_step(x[0], loss_target[0], full, comm)

    conv_w_rows = g["ssd_conv_w"].reshape(SSD_CONV_WIDTH * N_CHIPS, D_MODEL)
    b_qkv_rows = jnp.pad(g["attn_b_qkv"], (0, 2 * D_MODEL - ATTN_QKV)).reshape(2, D_MODEL)
    small = jnp.concatenate([_replicated_rows(g, loss_tile[0, 0]), conv_w_rows, b_qkv_rows, _rows(g["attn_b_o"]),
                             jnp.zeros((SM_ROWS - SM_B_O - 1, D_MODEL), F32)], axis=0)
    small_all, = _hook_call(_ExchangeHook([], [small]), name="vector_grad_all_gather")
    order = ("ssd_w_in", "ssd_w_out", "attn_w_qkv", "attn_w_o", "mlp_w_up0", "mlp_w_up1", "mlp_w_down0", "mlp_w_down1")
    halves = _sum_chips([comm.chip_parts[k] for k in order], name="grad_sum")
    r_in, r_out, r_qkv, r_o, r_up, r_down = _swap_halves(halves, layers=((4, 5), (6, 7)), name="grad_halves_swap")
    small_sum, = _sum_chips([small_all], name="small_grad_sum")

    grads = {"ssd_w_in": r_in, "ssd_w_out": r_out, "attn_w_qkv": r_qkv, "attn_w_o": r_o, "mlp_w_up": r_up, "mlp_w_down": r_down}
    grads = {k: a.reshape(w[k].shape) for k, a in grads.items()}
    conv_w_g = lax.dynamic_index_in_dim(small_sum[SM_CONV_W:SM_CONV_W + 16].reshape(SSD_CONV_WIDTH, N_CHIPS, D_MODEL), chip, axis=1, keepdims=False)
    b_qkv_g = lax.dynamic_slice_in_dim(small_sum[SM_B_QKV:SM_B_QKV + 2].reshape(-1), chip * 384, 384)
    b_o_g = lax.dynamic_slice_in_dim(small_sum[SM_B_O], chip * 256, 256)
    small_g = jnp.concatenate([small_sum[0:16], _sharded_rows(conv_w_g, b_qkv_g, b_o_g)], axis=0)
    grads.update(_unpack_small(small_g[0:16], small_g[16:24], w))
    loss = small_sum[SM_MISC, MISC_LOSS]

    delta, new_m, new_v = {}, {}, {}
    as2d = lambda p: [p[name].reshape(-1, p[name].shape[-1]) for name in MATRICES]
    for name, (g2, d2, m2, v2) in zip(MATRICES, _adamw(as2d(w), as2d(grads), as2d(m), as2d(v), name="adamw_matrices")):
        shape = w[name].shape
        grads[name], delta[name], new_m[name], new_v[name] = g2.reshape(shape), d2.reshape(shape), m2.reshape(shape), v2.reshape(shape)
    zero = jnp.zeros((), F32)
    small_pack = lambda p: jnp.concatenate([_replicated_rows({k: p[k] for k in REPLICATED}, zero),
                                            _sharded_rows(p["ssd_conv_w"], p["attn_b_qkv"], p["attn_b_o"])], axis=0)
    (_, d_s, m_s, v_s), = _adamw([small_pack(w)], [small_g], [small_pack(m)], [small_pack(v)], name="adamw_vectors")
    delta.update(_unpack_small(d_s[0:16], d_s[16:24], w))
    new_m.update(_unpack_small(m_s[0:16], m_s[16:24], w))
    new_v.update(_unpack_small(v_s[0:16], v_s[16:24], w))

    return (loss, grad_x[None], *[grads[n] for n in WEIGHT_NAMES], *[delta[n] for n in WEIGHT_NAMES],
            *[new_m[n] for n in WEIGHT_NAMES], *[new_v[n] for n in WEIGHT_NAMES])
```

```python
import functools

import jax
import jax.numpy as jnp
from jax import lax
from jax.experimental import pallas as pl
from jax.experimental.pallas import tpu as pltpu

F32 = jnp.float32
BF16 = jnp.bfloat16

D_MODEL = 1024
SSD_D_INNER = 2048
SSD_HEAD_DIM = 64
SSD_N_HEADS = 32
SSD_N_GROUPS = 8
SSD_HPG = 4
SSD_D_STATE = 128
SSD_CONV_WIDTH = 4
SSD_CHUNK = 128
SSD_CONV_DIM = 4096
SSD_IN_DIM = 6176
SSD_IN_PAD = 6400
SSD_IN_TILE = 1280
SSD_DT_COL = 6144
SSD_GW = SSD_HPG * SSD_HEAD_DIM
ATTN_HEAD_DIM = 64
ATTN_N_Q = 16
ATTN_N_KV = 4
ATTN_REP = 4
ATTN_WINDOW = 128
ATTN_QKV = 1536
D_FF = 4096
NORM_EPS = 1e-6

ADAM_LR = 0.001
ADAM_B1 = 0.9
ADAM_B2 = 0.999
ADAM_EPS = 1e-08
ADAM_WD = 0.01
ADAM_STEP = 10

N_CHIPS = 4
N_DEV = 8
LANES = 128
VMEM_LIMIT = 48 * 1024 * 1024
BIG_TILE = 2048

MESH = pl.DeviceIdType.MESH


def _params(*sem):
    return pltpu.CompilerParams(dimension_semantics=sem, vmem_limit_bytes=VMEM_LIMIT)


def _dot(a, b, dims):
    return lax.dot_general(a, b, (dims, ((), ())), preferred_element_type=F32)


def _dot_nn(a, b):
    return _dot(a, b, ((1,), (0,)))


def _dot_nt(a, b):
    return _dot(a, b, ((1,), (1,)))


def _dot_tn(a, b):
    return _dot(a, b, ((0,), (0,)))


def _sigmoid(x):
    return 0.5 * jnp.tanh(0.5 * x) + 0.5


ANY = pl.BlockSpec(memory_space=pl.ANY)


class _HookSlots:
    def __init__(self, hook, n_in, n_out, n_scratch):
        self.hook = hook
        self.n_in, self.n_out, self.n_scratch = n_in, n_out, n_scratch
        self.inputs = list(hook.arrs) if hook else []
        self.out_shape = list(hook.out_shape) if hook else []
        self.scratch = list(hook.scratch) if hook else []
        self.in_specs = [ANY] * len(self.inputs)
        self.out_specs = [ANY] * len(self.out_shape)

    def _split(self, refs):
        a = self.n_in
        b = a + len(self.inputs)
        c = b + self.n_out
        d = c + len(self.out_shape)
        e = d + self.n_scratch
        return refs[:a], refs[a:b], refs[b:c], refs[c:d], refs[d:e], refs[e:]

    def own(self, refs):
        ins, _, outs, _, scratch, _ = self._split(refs)
        return ins, outs, scratch

    def run(self, refs, step, n_steps):
        _, h_in, _, h_out, _, h_scratch = self._split(refs)
        _run_hook(self.hook, h_in, h_out, h_scratch, step, n_steps)

    def semantics(self, *sem):
        return sem if self.hook is None else ("arbitrary",) * len(sem)


def _matmul(a, b, *, mode, out_dtypes, name, epilogue=None, extras=(), tm=1024, tn=1024, tk=1024,
            b_shards=False, out_shards=False, hook=None, f32_block=None):
    f32_tail = f32_block is not None
    if b_shards:
        s, b_rows, b_cols = b.shape
        b2 = (b_rows, s * b_cols)
        if mode == "nn":
            tn = b_cols
        else:
            assert mode == "nt"
            tk = b_cols
    else:
        b2 = b.shape
    if mode == "nn":
        (m, k), (k2, n) = a.shape, b2
    elif mode == "nt":
        (m, k), (n, k2) = a.shape, b2
    else:
        (k, m), (k2, n) = a.shape, b2
    assert k == k2, (a.shape, b.shape, mode)
    tm, tn, tk = min(tm, m), min(tn, n), min(tk, k)
    assert m % tm == 0 and n % tn == 0 and k % tk == 0, (m, n, k, tm, tn, tk)
    nk = k // tk
    if mode == "tn":
        a_spec = pl.BlockSpec((tk, tm), lambda i, j, kk: (kk, i))
    else:
        a_spec = pl.BlockSpec((tm, tk), lambda i, j, kk: (i, kk))
    if b_shards and mode == "nn":
        b_spec = pl.BlockSpec((None, tk, tn), lambda i, j, kk: (j, kk, 0))
    elif b_shards:
        b_spec = pl.BlockSpec((None, tn, tk), lambda i, j, kk: (kk, j, 0))
    elif mode == "nt":
        b_spec = pl.BlockSpec((tn, tk), lambda i, j, kk: (j, kk))
    else:
        b_spec = pl.BlockSpec((tk, tn), lambda i, j, kk: (kk, j))
    dims = {"nn": ((1,), (0,)), "nt": ((1,), (1,)), "tn": ((0,), (0,))}[mode]
    ex_specs = []
    for arr, kind in extras:
        if kind == "tile":
            ex_specs.append(pl.BlockSpec((tm, tn), lambda i, j, kk: (i, j)))
        else:
            ex_specs.append(pl.BlockSpec((1, tn), lambda i, j, kk: (0, j)))
    n_ex, n_out = len(extras), len(out_dtypes)
    if epilogue is None:
        epilogue = lambda acc: (acc,)
    hk = _HookSlots(hook, n_in=2 + n_ex, n_out=n_out + f32_tail, n_scratch=0 if nk == 1 else 1)
    grid = (m // tm, n // tn, nk)

    def body(*refs):
        (a_ref, b_ref, *ex), outs, scratch = hk.own(refs)
        if hook is not None:
            step = (pl.program_id(0) * grid[1] + pl.program_id(1)) * grid[2] + pl.program_id(2)
            hk.run(refs, step, grid[0] * grid[1] * grid[2])

        def finish(acc):
            res = epilogue(acc, *[e[...] for e in ex])
            for o, r in zip(outs, res):
                o[...] = r.astype(o.dtype)
            if f32_tail:
                outs[n_out][...] = acc[:, f32_block:f32_block + LANES]

        if nk == 1:
            finish(_dot(a_ref[...], b_ref[...], dims))
        else:
            acc_ref = scratch[0]
            kk = pl.program_id(2)

            @pl.when(kk == 0)
            def _():
                acc_ref[...] = jnp.zeros_like(acc_ref)

            acc_ref[...] += _dot(a_ref[...], b_ref[...], dims)

            @pl.when(kk == nk - 1)
            def _():
                finish(acc_ref[...])

    if out_shards:
        out_spec = pl.BlockSpec((None, tm, tn), lambda i, j, kk: (j, i, 0))
        out_dims = (n // tn, m, tn)
    else:
        out_spec = pl.BlockSpec((tm, tn), lambda i, j, kk: (i, j))
        out_dims = (m, n)
    tail_specs = [pl.BlockSpec((tm, LANES), lambda i, j, kk: (i, 0))] if f32_tail else []
    tail_shapes = [jax.ShapeDtypeStruct((m, LANES), F32)] if f32_tail else []
    outs = pl.pallas_call(
        body,
        grid=grid,
        in_specs=[a_spec, b_spec] + ex_specs + hk.in_specs,
        out_specs=[out_spec for _ in out_dtypes] + tail_specs + hk.out_specs,
        out_shape=[jax.ShapeDtypeStruct(out_dims, dt) for dt in out_dtypes] + tail_shapes + hk.out_shape,
        scratch_shapes=([] if nk == 1 else [pltpu.VMEM((tm, tn), F32)]) + hk.scratch,
        compiler_params=_params(*hk.semantics("parallel", "arbitrary" if f32_tail else "parallel", "arbitrary")),
        name=name,
    )(a, b, *[arr for arr, _ in extras], *hk.inputs)
    n_own = n_out + f32_tail
    own = outs[0] if n_own == 1 else outs[:n_own]
    return own if hook is None else (own, outs[n_own:])


def _row_tile(t, want):
    return min(t, want)


def _rms_fwd(x, w, *, name, resid=None, want_u=None, target=None):
    t, d = x.shape
    tr = _row_tile(t, 512)

    def norm(v, wv):
        return v * lax.rsqrt(jnp.mean(v * v, axis=-1, keepdims=True) + NORM_EPS) * wv

    row = pl.BlockSpec((tr, d), lambda i: (i, 0))
    vec = pl.BlockSpec((1, d), lambda i: (0, 0))
    if target is not None:
        def body(x_ref, w_ref, r_ref, t_ref, dh_ref, loss_ref):
            err = r_ref[...] + norm(x_ref[...], w_ref[...]) - t_ref[...]
            dh_ref[...] = err * (1.0 / d)

            @pl.when(pl.program_id(0) == 0)
            def _():
                loss_ref[...] = jnp.zeros_like(loss_ref)

            part = jnp.sum(jnp.sum(err * err, axis=1, keepdims=True), axis=0, keepdims=True) * (0.5 / d)
            loss_ref[...] += jnp.broadcast_to(part, loss_ref.shape)

        return pl.pallas_call(
            body, grid=(t // tr,), in_specs=[row, vec, row, row],
            out_specs=[row, pl.BlockSpec((8, LANES), lambda i: (0, 0))],
            out_shape=[jax.ShapeDtypeStruct((t, d), F32), jax.ShapeDtypeStruct((8, LANES), F32)],
            compiler_params=_params("arbitrary"), name=name)(x, w, resid, target)
    if resid is None:
        def body(x_ref, w_ref, o_ref):
            o_ref[...] = norm(x_ref[...], w_ref[...]).astype(BF16)
        ins, in_specs = (x, w), [row, vec]
        out_shape, out_specs = jax.ShapeDtypeStruct((t, d), BF16), row
    elif want_u is None:
        def body(x_ref, w_ref, r_ref, o_ref):
            o_ref[...] = r_ref[...] + norm(x_ref[...], w_ref[...])
        ins, in_specs = (x, w, resid), [row, vec, row]
        out_shape, out_specs = jax.ShapeDtypeStruct((t, d), F32), row
    else:
        def body(x_ref, w_ref, r_ref, w2_ref, o_ref, u_ref):
            h = r_ref[...] + norm(x_ref[...], w_ref[...])
            o_ref[...] = h
            u_ref[...] = norm(h, w2_ref[...]).astype(BF16)
        ins, in_specs = (x, w, resid, want_u), [row, vec, row, vec]
        out_shape = [jax.ShapeDtypeStruct((t, d), F32), jax.ShapeDtypeStruct((t, d), BF16)]
        out_specs = [row, row]
    return pl.pallas_call(body, grid=(t // tr,), in_specs=in_specs, out_specs=out_specs, out_shape=out_shape,
                          compiler_params=_params("parallel"), name=name)(*ins)


def _rms_bwd(x, w, dy, *, name, resid=None, out_dtype=F32, dx_col_sum=False):
    t, d = x.shape
    tr = _row_tile(t, 512)
    row = pl.BlockSpec((tr, d), lambda i: (i, 0))
    vec = pl.BlockSpec((1, d), lambda i: (0, 0))
    has_res = resid is not None

    def body(x_ref, w_ref, dy_ref, *rest):
        r_ref = rest[0] if has_res else None
        dx_ref, dw_ref = rest[has_res:has_res + 2]
        xv = x_ref[...]
        dyv = dy_ref[...].astype(F32)
        r = lax.rsqrt(jnp.mean(xv * xv, axis=-1, keepdims=True) + NORM_EPS)
        xhat = xv * r
        dyw = dyv * w_ref[...]
        dx = r * (dyw - xhat * jnp.mean(dyw * xhat, axis=-1, keepdims=True))
        if has_res:
            dx = dx + r_ref[...]
        dx_ref[...] = dx.astype(dx_ref.dtype)

        sums = [(dw_ref, dyv * xhat)] + ([(rest[-1], dx)] if dx_col_sum else [])

        @pl.when(pl.program_id(0) == 0)
        def _():
            for acc_ref, _ in sums:
                acc_ref[...] = jnp.zeros_like(acc_ref)

        for acc_ref, rows in sums:
            acc_ref[...] += jnp.sum(rows, axis=0, keepdims=True)

    ins = (x, w, dy) + ((resid,) if has_res else ())
    in_specs = [row, vec, row] + ([row] if has_res else [])
    n_vec = 2 if dx_col_sum else 1
    return pl.pallas_call(
        body, grid=(t // tr,), in_specs=in_specs, out_specs=[row] + [vec] * n_vec,
        out_shape=[jax.ShapeDtypeStruct((t, d), out_dtype)] + [jax.ShapeDtypeStruct((1, d), F32)] * n_vec,
        compiler_params=_params("arbitrary"), name=name)(*ins)


def _col_sum(x, *, name):
    t, n = x.shape
    tr = _row_tile(t, 512)

    def body(x_ref, o_ref):
        @pl.when(pl.program_id(0) == 0)
        def _():
            o_ref[...] = jnp.zeros_like(o_ref)

        o_ref[...] += jnp.sum(x_ref[...].astype(F32), axis=0, keepdims=True)

    return pl.pallas_call(
        body, grid=(t // tr,), in_specs=[pl.BlockSpec((tr, n), lambda i: (i, 0))],
        out_specs=pl.BlockSpec((1, n), lambda i: (0, 0)), out_shape=jax.ShapeDtypeStruct((1, n), F32),
        compiler_params=_params("arbitrary"), name=name)(x)


SSD_IN_SHARD = SSD_IN_DIM // N_CHIPS


def _w_in_from_shards(shards, *, name):
    d = shards.shape[1]
    tr = 256

    def body(s_ref, o_ref):
        o_ref[:, pl.ds(SSD_DT_COL, SSD_IN_PAD - SSD_DT_COL)] = jnp.zeros((tr, SSD_IN_PAD - SSD_DT_COL), o_ref.dtype)
        for s in range(N_CHIPS):
            o_ref[:, pl.ds(SSD_IN_SHARD * s, SSD_IN_SHARD)] = s_ref[s]

    return pl.pallas_call(
        body, grid=(d // tr,), in_specs=[pl.BlockSpec((N_CHIPS, tr, SSD_IN_SHARD), lambda i: (0, i, 0))],
        out_specs=pl.BlockSpec((tr, SSD_IN_PAD), lambda i: (i, 0)),
        out_shape=jax.ShapeDtypeStruct((d, SSD_IN_PAD), shards.dtype),
        compiler_params=_params("parallel"), name=name)(shards)


def _w_in_to_shards(g, *, name):
    d = g.shape[0]
    tr = 256

    def body(g_ref, o_ref):
        for s in range(N_CHIPS):
            o_ref[s] = g_ref[:, pl.ds(SSD_IN_SHARD * s, SSD_IN_SHARD)].astype(o_ref.dtype)

    return pl.pallas_call(
        body, grid=(d // tr,), in_specs=[pl.BlockSpec((tr, SSD_IN_PAD), lambda i: (i, 0))],
        out_specs=pl.BlockSpec((N_CHIPS, tr, SSD_IN_SHARD), lambda i: (0, i, 0)),
        out_shape=jax.ShapeDtypeStruct((N_CHIPS, d, SSD_IN_SHARD), BF16),
        compiler_params=_params("parallel"), name=name)(g)


XBC_COL0 = SSD_D_INNER // LANES


def _shift_down(v, k, row_ids):
    return jnp.where(row_ids >= k, pltpu.roll(v, k, axis=0), 0.0)


def _shift_up(v, k, row_ids):
    n = v.shape[0]
    return jnp.where(row_ids < n - k, pltpu.roll(v, n - k, axis=0), 0.0)


def _conv_pre(x, w, b, row_ids):
    pre = b + w[3:4, :] * x
    for k in (1, 2, 3):
        pre = pre + w[3 - k:4 - k, :] * _shift_down(x, k, row_ids)
    return pre


def _conv_fwd(zx, conv_w, conv_b, *, name, hook=None):
    t = zx.shape[0]
    nct = SSD_CONV_DIM // LANES
    hk = _HookSlots(hook, n_in=3, n_out=1, n_scratch=0)

    def body(*refs):
        (x_ref, w_ref, b_ref), (o_ref,), _ = hk.own(refs)
        if hook is not None:
            hk.run(refs, pl.program_id(0), nct)
        x = x_ref[...].astype(F32)
        row_ids = lax.broadcasted_iota(jnp.int32, x.shape, 0)
        pre = _conv_pre(x, w_ref[...], b_ref[...], row_ids)
        o_ref[...] = pre * _sigmoid(pre)

    outs = pl.pallas_call(
        body, grid=(nct,),
        in_specs=[pl.BlockSpec((t, LANES), lambda j: (0, XBC_COL0 + j)),
                  pl.BlockSpec((SSD_CONV_WIDTH, LANES), lambda j: (0, j)),
                  pl.BlockSpec((1, LANES), lambda j: (0, j))] + hk.in_specs,
        out_specs=[pl.BlockSpec((t, LANES), lambda j: (0, j))] + hk.out_specs,
        out_shape=[jax.ShapeDtypeStruct((t, SSD_CONV_DIM), F32)] + hk.out_shape,
        scratch_shapes=hk.scratch,
        compiler_params=_params(*hk.semantics("parallel")), name=name)(zx, conv_w, conv_b, *hk.inputs)
    return outs[0] if hook is None else (outs[0], outs[1:])


def _conv_bwd(zx, conv_w, conv_b, d_xs, d_bm, d_cm, dzx, *, name):
    t = zx.shape[0]
    nct = SSD_CONV_DIM // LANES
    n_xs = SSD_D_INNER // LANES
    n_bm = SSD_N_GROUPS * SSD_D_STATE // LANES

    def body(x_ref, w_ref, b_ref, dxs_ref, dbm_ref, dcm_ref, _, dx_ref, dw_ref, db_ref):
        x = x_ref[...].astype(F32)
        w = w_ref[...]
        j = pl.program_id(0)
        dy = jnp.where(j < n_xs, dxs_ref[...], jnp.where(j < n_xs + n_bm, dbm_ref[...], dcm_ref[...]))
        row_ids = lax.broadcasted_iota(jnp.int32, x.shape, 0)
        pre = _conv_pre(x, w, b_ref[...], row_ids)
        sg = _sigmoid(pre)
        dpre = dy * (sg * (1.0 + pre * (1.0 - sg)))
        dx = w[3:4, :] * dpre
        for k in (1, 2, 3):
            dx = dx + w[3 - k:4 - k, :] * _shift_up(dpre, k, row_ids)
        dx_ref[...] = dx.astype(dx_ref.dtype)
        db_ref[...] = jnp.sum(dpre, axis=0, keepdims=True)
        dw_ref[3:4, :] = jnp.sum(dpre * x, axis=0, keepdims=True)
        for k in (1, 2, 3):
            dw_ref[3 - k:4 - k, :] = jnp.sum(dpre * _shift_down(x, k, row_ids), axis=0, keepdims=True)

    clip = lambda j, lo, n: jnp.clip(j - lo, 0, n - 1)
    return pl.pallas_call(
        body, grid=(nct,),
        in_specs=[pl.BlockSpec((t, LANES), lambda j: (0, XBC_COL0 + j)),
                  pl.BlockSpec((SSD_CONV_WIDTH, LANES), lambda j: (0, j)),
                  pl.BlockSpec((1, LANES), lambda j: (0, j)),
                  pl.BlockSpec((t, LANES), lambda j: (0, clip(j, 0, n_xs))),
                  pl.BlockSpec((t, LANES), lambda j: (0, clip(j, n_xs, n_bm))),
                  pl.BlockSpec((t, LANES), lambda j: (0, clip(j, n_xs + n_bm, n_bm))), ANY],
        out_specs=[pl.BlockSpec((t, LANES), lambda j: (0, XBC_COL0 + j)),
                   pl.BlockSpec((SSD_CONV_WIDTH, LANES), lambda j: (0, j)), pl.BlockSpec((1, LANES), lambda j: (0, j))],
        out_shape=[jax.ShapeDtypeStruct(dzx.shape, dzx.dtype),
                   jax.ShapeDtypeStruct((SSD_CONV_WIDTH, SSD_CONV_DIM), F32),
                   jax.ShapeDtypeStruct((1, SSD_CONV_DIM), F32)],
        input_output_aliases={6: 0},
        compiler_params=_params("parallel"), name=name)(zx, conv_w, conv_b, d_xs, d_bm, d_cm, dzx)


def _softplus_fwd(dt_raw, bias_row, alog_row, *, name):
    t = dt_raw.shape[0]
    q = SSD_CHUNK
    tr = _row_tile(t, 1024)

    def body(x_ref, b_ref, al_ref, dt_ref, cum_ref):
        v = x_ref[...] + b_ref[...]
        e = jnp.exp(-jnp.abs(v))
        u = 1.0 + e
        log1p = jnp.where(u == 1.0, e, jnp.log(u) * (e / (u - 1.0)))
        dt = jnp.maximum(v, 0.0) + log1p
        a = dt * -jnp.exp(al_ref[...])
        lower = (lax.broadcasted_iota(jnp.int32, (q, q), 1) <= lax.broadcasted_iota(jnp.int32, (q, q), 0)).astype(F32)
        cums = [lax.dot_general(lower, a[c * q:(c + 1) * q, :], ((((1,), (0,))), ((), ())), precision=lax.Precision.HIGHEST,
                                preferred_element_type=F32) for c in range(tr // q)]
        dt_t, cum_t = dt.T, jnp.concatenate(cums, axis=0).T
        for g in range(SSD_N_GROUPS):
            rows = slice(g * SSD_HPG, (g + 1) * SSD_HPG)
            dt_ref[g] = dt_t[rows, :]
            cum_ref[g] = cum_t[rows, :]

    vec = pl.BlockSpec((1, LANES), lambda i: (0, 0))
    by_group = pl.BlockSpec((SSD_N_GROUPS, SSD_HPG, tr), lambda i: (0, 0, i))
    return pl.pallas_call(
        body, grid=(t // tr,),
        in_specs=[pl.BlockSpec((tr, LANES), lambda i: (i, 0)), vec, vec],
        out_specs=[by_group, by_group],
        out_shape=[jax.ShapeDtypeStruct((SSD_N_GROUPS, SSD_HPG, t), F32)] * 2,
        compiler_params=_params("parallel"), name=name)(dt_raw, bias_row, alog_row)


def _softplus_bwd(dt_raw, bias_row, ddt_rows, dzx, *, name):
    t = dt_raw.shape[0]
    tr = _row_tile(t, 1024)
    tail = SSD_IN_PAD - SSD_DT_COL

    def body(x_ref, b_ref, g_ref, _, o_ref, db_ref):
        v = x_ref[...] + b_ref[...]
        lane = lax.broadcasted_iota(jnp.int32, v.shape, 1)
        by_head = jnp.concatenate([g_ref[g] for g in range(SSD_N_GROUPS)]
                                  + [jnp.zeros((LANES - SSD_N_HEADS, tr), F32)], axis=0)
        d = jnp.where(lane < SSD_N_HEADS, by_head.T * _sigmoid(v), 0.0)
        o_ref[:, pl.ds(0, LANES)] = d.astype(o_ref.dtype)
        o_ref[:, pl.ds(LANES, tail - LANES)] = jnp.zeros((tr, tail - LANES), o_ref.dtype)

        @pl.when(pl.program_id(0) == 0)
        def _():
            db_ref[...] = jnp.zeros_like(db_ref)

        db_ref[...] += jnp.sum(d, axis=0, keepdims=True)

    return pl.pallas_call(
        body, grid=(t // tr,),
        in_specs=[pl.BlockSpec((tr, LANES), lambda i: (i, 0)), pl.BlockSpec((1, LANES), lambda i: (0, 0)),
                  pl.BlockSpec((SSD_N_GROUPS, SSD_HPG, tr), lambda i: (0, 0, i)), ANY],
        out_specs=[pl.BlockSpec((tr, tail), lambda i: (i, SSD_DT_COL // tail)), pl.BlockSpec((1, LANES), lambda i: (0, 0))],
        out_shape=[jax.ShapeDtypeStruct(dzx.shape, dzx.dtype), jax.ShapeDtypeStruct((1, LANES), F32)],
        input_output_aliases={3: 0},
        compiler_params=_params("arbitrary"), name=name)(dt_raw, bias_row, ddt_rows, dzx)


def _ssd_masks():
    q = SSD_CHUNK
    tt = lax.broadcasted_iota(jnp.int32, (q, q), 0)
    ss = lax.broadcasted_iota(jnp.int32, (q, q), 1)
    lane = lax.broadcasted_iota(jnp.int32, (1, SSD_GW), 1)
    srow = lax.broadcasted_iota(jnp.int32, (SSD_GW, 1), 0)
    hm = [(lane >= SSD_HEAD_DIM * j) & (lane < SSD_HEAD_DIM * (j + 1)) for j in range(SSD_HPG)]
    rm = [(srow >= SSD_HEAD_DIM * j) & (srow < SSD_HEAD_DIM * (j + 1)) for j in range(SSD_HPG)]
    return tt, ss, hm, rm


def _ssd_head_terms(dt_rows, cum_rows, a_rows, j, tt, ss):
    q = SSD_CHUNK
    dt_row = dt_rows[j:j + 1, :]
    dt_col = jnp.sum(jnp.where(tt == ss, dt_row, 0.0), axis=1, keepdims=True)
    a_row1 = a_rows[j:j + 1, :]
    a_11 = a_rows[j:j + 1, 0:1]
    cum_col = jnp.sum(jnp.where(ss <= tt, dt_row * a_row1, 0.0), axis=1, keepdims=True)
    cum_row = cum_rows[j:j + 1, :]
    decay = jnp.exp(jnp.where(ss <= tt, cum_col - cum_row, -jnp.inf))
    cum_last = cum_col[q - 1:q, :]
    e_col = jnp.exp(cum_col)
    dte_col = jnp.exp(cum_last - cum_col)
    e_last = jnp.exp(cum_last)
    return dt_col, dt_row, a_row1, a_11, decay, e_col, dte_col, e_last


SSD_CHUNKS_PER_STEP = 8
SSD_BC_COL0 = SSD_D_INNER // SSD_D_STATE


def _ssd_head_selects(terms, hm, rm):
    e_all = jnp.zeros((SSD_CHUNK, SSD_GW), F32)
    w_all = jnp.zeros((SSD_CHUNK, SSD_GW), F32)
    e_s = jnp.zeros((SSD_GW, 1), F32)
    for j in range(SSD_HPG):
        dt_col, _, _, _, _, e_col, dte_col, e_last = terms[j]
        e_all = jnp.where(hm[j], e_col, e_all)
        w_all = jnp.where(hm[j], dt_col * dte_col, w_all)
        e_s = jnp.where(rm[j], e_last, e_s)
    return e_all, w_all, e_s


def _ssd_fwd(xc, dtr, cumr, alog_b, d_b, *, name, hook=None):
    t = xc.shape[0]
    q = SSD_CHUNK
    nc = t // q
    kc = min(SSD_CHUNKS_PER_STEP, nc)
    rows = kc * q
    hk = _HookSlots(hook, n_in=7, n_out=2, n_scratch=1)

    def body(*refs):
        (x_ref, b_ref, c_ref, dtr_ref, cumr_ref, alog_ref, d_ref), (y_ref, st_ref), (s_scr,) = hk.own(refs)
        if hook is not None:
            hk.run(refs, pl.program_id(0) * (nc // kc) + pl.program_id(1), SSD_N_GROUPS * (nc // kc))

        @pl.when(pl.program_id(1) == 0)
        def _():
            s_scr[...] = jnp.zeros_like(s_scr)

        tt, ss, hm, rm = _ssd_masks()
        a_rows = -jnp.exp(alog_ref[...])
        d_rows = d_ref[...]
        d_all = jnp.zeros((1, SSD_GW), F32)
        for j in range(SSD_HPG):
            d_all = jnp.where(hm[j], d_rows[j:j + 1, 0:1], d_all)
        ks, hs = range(kc), range(SSD_HPG)
        sl = [pl.ds(k * q, q) for k in ks]
        x = [x_ref[sl[k], :] for k in ks]
        bm = [b_ref[sl[k], :].astype(BF16) for k in ks]
        cm = [c_ref[sl[k], :].astype(BF16) for k in ks]
        xb = [x[k].astype(BF16) for k in ks]
        terms = [[_ssd_head_terms(dtr_ref[:, sl[k]], cumr_ref[:, sl[k]], a_rows, j, tt, ss) for j in hs] for k in ks]
        g = [_dot_nt(cm[k], bm[k]) for k in ks]
        m = [[(g[k] * terms[k][j][4] * terms[k][j][1]).astype(BF16) for j in hs] for k in ks]
        yj = [[_dot_nn(m[k][j], xb[k]) for j in hs] for k in ks]
        sel = [_ssd_head_selects(terms[k], hm, rm) for k in ks]
        upd = [_dot_tn((x[k] * sel[k][1]).astype(BF16), bm[k]) for k in ks]
        states = [s_scr[...]]
        for k in ks:
            states.append(states[k] * sel[k][2] + upd[k])
        inter = [_dot_nt(cm[k], states[k].astype(BF16)) for k in ks]
        ys = []
        for k in ks:
            y = jnp.zeros((q, SSD_GW), F32)
            for j in hs:
                y = jnp.where(hm[j], yj[k][j], y)
            ys.append(y + inter[k] * sel[k][0] + x[k] * d_all)
        for k in ks:
            st_ref[k] = states[k]
        y_ref[...] = jnp.concatenate(ys, axis=0)
        s_scr[...] = states[kc]

    blk = lambda width, off: pl.BlockSpec((rows, width), lambda g, c: (c, off + g))
    par_s = pl.BlockSpec((None, SSD_HPG, LANES), lambda g, c: (g, 0, 0))
    row_s = pl.BlockSpec((None, SSD_HPG, rows), lambda g, c: (g, 0, c))
    outs = pl.pallas_call(
        body, grid=(SSD_N_GROUPS, nc // kc),
        in_specs=[blk(SSD_GW, 0), blk(SSD_D_STATE, SSD_BC_COL0), blk(SSD_D_STATE, SSD_BC_COL0 + SSD_N_GROUPS),
                  row_s, row_s, par_s, par_s] + hk.in_specs,
        out_specs=[blk(SSD_GW, 0), pl.BlockSpec((None, kc, SSD_GW, SSD_D_STATE), lambda g, c: (g, c, 0, 0))] + hk.out_specs,
        out_shape=[jax.ShapeDtypeStruct((t, SSD_D_INNER), F32),
                   jax.ShapeDtypeStruct((SSD_N_GROUPS, nc, SSD_GW, SSD_D_STATE), F32)] + hk.out_shape,
        scratch_shapes=[pltpu.VMEM((SSD_GW, SSD_D_STATE), F32)] + hk.scratch,
        compiler_params=_params(*hk.semantics("parallel", "arbitrary")), name=name)(
            xc, xc, xc, dtr, cumr, alog_b, d_b, *hk.inputs)
    return outs if hook is None else (outs[:2], outs[2:])


def _ssd_bwd(xc, dtr, cumr, alog_b, d_b, states, dy, *, name, hook=None):
    t = xc.shape[0]
    q = SSD_CHUNK
    nc = t // q
    kc = min(SSD_CHUNKS_PER_STEP, nc)
    nst = nc // kc
    rows = kc * q
    rev = lambda c: nst - 1 - c
    hk = _HookSlots(hook, n_in=9, n_out=5, n_scratch=1)

    def body(*refs):
        ((x_ref, b_ref, c_ref, dtr_ref, cumr_ref, alog_ref, d_ref, st_ref, dy_ref),
         (dx_ref, db_ref, dc_ref, ddt_ref, dpar_ref), (ds_scr,)) = hk.own(refs)
        if hook is not None:
            hk.run(refs, pl.program_id(0) * nst + pl.program_id(1), SSD_N_GROUPS * nst)

        @pl.when(pl.program_id(1) == 0)
        def _():
            ds_scr[...] = jnp.zeros_like(ds_scr)
            dpar_ref[...] = jnp.zeros_like(dpar_ref)

        tt, ss, hm, rm = _ssd_masks()
        tcol = lax.broadcasted_iota(jnp.int32, (q, 1), 0)
        lane = lax.broadcasted_iota(jnp.int32, (1, LANES), 1)
        a_rows = -jnp.exp(alog_ref[...])
        d_rows = d_ref[...]
        d_all = jnp.zeros((1, SSD_GW), F32)
        for j in range(SSD_HPG):
            d_all = jnp.where(hm[j], d_rows[j:j + 1, 0:1], d_all)
        ks, hs = range(kc), range(SSD_HPG)
        sl = [pl.ds(k * q, q) for k in ks]
        x = [x_ref[sl[k], :] for k in ks]
        dyv = [dy_ref[sl[k], :] for k in ks]
        bm = [b_ref[sl[k], :].astype(BF16) for k in ks]
        cm = [c_ref[sl[k], :].astype(BF16) for k in ks]
        s_in = [st_ref[k] for k in ks]
        xb = [x[k].astype(BF16) for k in ks]
        dyb = [dyv[k].astype(BF16) for k in ks]
        s_b = [s_in[k].astype(BF16) for k in ks]
        terms = [[_ssd_head_terms(dtr_ref[:, sl[k]], cumr_ref[:, sl[k]], a_rows, j, tt, ss) for j in hs] for k in ks]
        sel = [_ssd_head_selects(terms[k], hm, rm) for k in ks]
        e_all, w_all, e_s = [s_[0] for s_ in sel], [s_[1] for s_ in sel], [s_[2] for s_ in sel]
        dye = [(dyv[k] * e_all[k]).astype(BF16) for k in ks]
        ds_loc = [_dot_tn(dye[k], cm[k]) for k in ks]
        ds = [None] * kc
        running = ds_scr[...]
        for k in reversed(ks):
            ds[k] = running
            running = running * e_s[k] + ds_loc[k]
        ds_scr[...] = running
        ds_b = [ds[k].astype(BF16) for k in ks]
        g = [_dot_nt(cm[k], bm[k]) for k in ks]
        cs = [_dot_nt(cm[k], s_b[k]) for k in ks]
        bds = [_dot_nt(bm[k], ds_b[k]) for k in ks]
        dm = [[_dot_nt(jnp.where(hm[j], dyv[k], 0.0).astype(BF16), xb[k]) for j in hs] for k in ks]
        gl = [[g[k] * terms[k][j][4] for j in hs] for k in ks]
        wp = [[dm[k][j] * gl[k][j] for j in hs] for k in ks]
        mt = [[(gl[k][j] * terms[k][j][1]).astype(BF16) for j in hs] for k in ks]
        dxj = [[_dot_tn(mt[k][j], dyb[k]) for j in hs] for k in ks]
        dg = []
        for k in ks:
            acc = jnp.zeros((q, q), F32)
            for j in hs:
                acc = acc + dm[k][j] * terms[k][j][4] * terms[k][j][1]
            dg.append(acc.astype(BF16))
        dy_cs = [dyv[k] * cs[k] for k in ks]
        x_bds = [x[k] * bds[k] for k in ks]
        dy_x = [dyv[k] * x[k] for k in ks]
        ds_s = [ds[k] * s_in[k] for k in ks]
        w = [[wp[k][j] * terms[k][j][1] for j in hs] for k in ks]
        rw_col = [[jnp.sum(w[k][j], axis=1, keepdims=True) for j in hs] for k in ks]
        cw_row = [[jnp.sum(w[k][j], axis=0, keepdims=True) for j in hs] for k in ks]
        cwp_row = [[jnp.sum(wp[k][j], axis=0, keepdims=True) for j in hs] for k in ks]
        r1_col = [[jnp.sum(jnp.where(hm[j], dy_cs[k], 0.0), axis=1, keepdims=True) * terms[k][j][5] for j in hs] for k in ks]
        dw_col = [[jnp.sum(jnp.where(hm[j], x_bds[k], 0.0), axis=1, keepdims=True) for j in hs] for k in ks]
        head_rows = [slice(j * SSD_HEAD_DIM, (j + 1) * SSD_HEAD_DIM) for j in hs]
        lane_sum = lambda v: jnp.sum(v, axis=1, keepdims=True)
        s_sum = [[lane_sum(jnp.sum(ds_s[k][head_rows[j], :], axis=0, keepdims=True)) for j in hs] for k in ks]
        dy_x_cols = [jnp.sum(dy_x[k], axis=0, keepdims=True) for k in ks]
        d_d = [[lane_sum(jnp.where(hm[j], dy_x_cols[k], 0.0)) for j in hs] for k in ks]
        ddt_rows = [[None] * SSD_HPG for _ in ks]
        dpar = [jnp.zeros((1, LANES), F32) for _ in hs]
        for k in ks:
            for j in hs:
                dt_col, dt_row, a_row1, a_11, _, _, dte_col, e_last = terms[k][j]
                dww = dw_col[k][j] * (dt_col * dte_col)
                last_add = jnp.sum(dww, axis=0, keepdims=True) + e_last * s_sum[k][j]
                dcum_col = rw_col[k][j] + r1_col[k][j] - dww + jnp.where(tcol == q - 1, last_add, 0.0)
                da_row = jnp.sum(jnp.where(tt >= ss, dcum_col, 0.0), axis=0, keepdims=True)
                da_col = jnp.sum(jnp.where(ss >= tt, -cw_row[k][j], 0.0), axis=1, keepdims=True)
                ddt_col = a_11 * da_col + dw_col[k][j] * dte_col
                ddt_rows[k][j] = (a_row1 * da_row + cwp_row[k][j]
                                  + jnp.sum(jnp.where(tt == ss, ddt_col, 0.0), axis=0, keepdims=True))
                d_a = jnp.sum(dt_row * da_row, axis=1, keepdims=True) + jnp.sum(dt_col * da_col, axis=0, keepdims=True)
                dpar[j] = dpar[j] + jnp.where(lane == 0, d_a * a_11, 0.0) + jnp.where(lane == 1, d_d[k][j], 0.0)
        dxs = []
        for k in ks:
            acc = jnp.zeros((q, SSD_GW), F32)
            for j in hs:
                acc = jnp.where(hm[j], dxj[k][j], acc)
            dxs.append(acc + w_all[k] * bds[k] + d_all * dyv[k])
        xw = [(x[k] * w_all[k]).astype(BF16) for k in ks]
        dc = [_dot_nn(dg[k], bm[k]) + _dot_nn(dye[k], s_b[k]) for k in ks]
        db = [_dot_tn(dg[k], cm[k]) + _dot_nn(xw[k], ds_b[k]) for k in ks]
        dx_ref[...] = jnp.concatenate(dxs, axis=0)
        dc_ref[...] = jnp.concatenate(dc, axis=0)
        db_ref[...] = jnp.concatenate(db, axis=0)
        ddt_ref[...] = jnp.concatenate([jnp.concatenate([ddt_rows[k][j] for k in ks], axis=1) for j in hs], axis=0)
        dpar_ref[...] += jnp.concatenate(dpar, axis=0)

    blk = lambda width, off: pl.BlockSpec((rows, width), lambda g, c: (rev(c), off + g))
    par_s = pl.BlockSpec((None, SSD_HPG, LANES), lambda g, c: (g, 0, 0))
    outs = pl.pallas_call(
        body, grid=(SSD_N_GROUPS, nst),
        in_specs=[blk(SSD_GW, 0), blk(SSD_D_STATE, SSD_BC_COL0), blk(SSD_D_STATE, SSD_BC_COL0 + SSD_N_GROUPS),
                  pl.BlockSpec((None, SSD_HPG, rows), lambda g, c: (g, 0, rev(c))),
                  pl.BlockSpec((None, SSD_HPG, rows), lambda g, c: (g, 0, rev(c))), par_s, par_s,
                  pl.BlockSpec((None, kc, SSD_GW, SSD_D_STATE), lambda g, c: (g, rev(c), 0, 0)), blk(SSD_GW, 0)] + hk.in_specs,
        out_specs=[blk(SSD_GW, 0), blk(SSD_D_STATE, 0), blk(SSD_D_STATE, 0),
                   pl.BlockSpec((None, SSD_HPG, rows), lambda g, c: (g, 0, rev(c))), par_s] + hk.out_specs,
        out_shape=[jax.ShapeDtypeStruct((t, SSD_D_INNER), F32),
                   jax.ShapeDtypeStruct((t, SSD_N_GROUPS * SSD_D_STATE), F32),
                   jax.ShapeDtypeStruct((t, SSD_N_GROUPS * SSD_D_STATE), F32),
                   jax.ShapeDtypeStruct((SSD_N_GROUPS, SSD_HPG, t), F32),
                   jax.ShapeDtypeStruct((SSD_N_GROUPS, SSD_HPG, LANES), F32)] + hk.out_shape,
        scratch_shapes=[pltpu.VMEM((SSD_GW, SSD_D_STATE), F32)] + hk.scratch,
        compiler_params=_params(*hk.semantics("parallel", "arbitrary")), name=name)(
            xc, xc, xc, dtr, cumr, alog_b, d_b, states, dy, *hk.inputs)
    return outs if hook is None else (outs[:5], outs[5:])


def _gate_norm_fwd(y, zx, norm_w, *, name):
    t = y.shape[0]
    tr = _row_tile(t, 256)
    row = pl.BlockSpec((tr, SSD_D_INNER), lambda i: (i, 0))

    def body(y_ref, z_ref, w_ref, o_ref):
        for gi in range(SSD_N_GROUPS):
            sl = pl.ds(gi * SSD_GW, SSD_GW)
            z = z_ref[:, sl].astype(F32)
            gv = y_ref[:, sl] * (z * _sigmoid(z))
            r = lax.rsqrt(jnp.mean(gv * gv, axis=-1, keepdims=True) + NORM_EPS)
            o_ref[:, sl] = (gv * r * w_ref[:, sl]).astype(BF16)

    return pl.pallas_call(
        body, grid=(t // tr,), in_specs=[row, row, pl.BlockSpec((1, SSD_D_INNER), lambda i: (0, 0))],
        out_specs=row, out_shape=jax.ShapeDtypeStruct((t, SSD_D_INNER), BF16),
        compiler_params=_params("parallel"), name=name)(y, zx, norm_w)


def _gate_norm_bwd(y, zx, norm_w, dyn, *, name):
    t = y.shape[0]
    tr = _row_tile(t, 256)
    row = pl.BlockSpec((tr, SSD_D_INNER), lambda i: (i, 0))
    vec = pl.BlockSpec((1, SSD_D_INNER), lambda i: (0, 0))

    def body(y_ref, z_ref, w_ref, dyn_ref, dy_ref, dz_ref, dw_ref):
        @pl.when(pl.program_id(0) == 0)
        def _():
            dw_ref[...] = jnp.zeros_like(dw_ref)

        for gi in range(SSD_N_GROUPS):
            sl = pl.ds(gi * SSD_GW, SSD_GW)
            z = z_ref[:, sl].astype(F32)
            yv = y_ref[:, sl]
            sg = _sigmoid(z)
            sz = z * sg
            gv = yv * sz
            r = lax.rsqrt(jnp.mean(gv * gv, axis=-1, keepdims=True) + NORM_EPS)
            ghat = gv * r
            dout = dyn_ref[:, sl].astype(F32)
            dgh = dout * w_ref[:, sl]
            dgv = r * (dgh - ghat * jnp.mean(dgh * ghat, axis=-1, keepdims=True))
            dy_ref[:, sl] = dgv * sz
            dz_ref[:, sl] = (dgv * yv * (sg * (1.0 + z * (1.0 - sg)))).astype(dz_ref.dtype)
            dw_ref[:, sl] += jnp.sum(dout * ghat, axis=0, keepdims=True)

    return pl.pallas_call(
        body, grid=(t // tr,), in_specs=[row, row, vec, row], out_specs=[row, row, vec],
        out_shape=[jax.ShapeDtypeStruct((t, SSD_D_INNER), F32), jax.ShapeDtypeStruct((t, SSD_IN_PAD), BF16),
                   jax.ShapeDtypeStruct((1, SSD_D_INNER), F32)],
        compiler_params=_params("arbitrary"), name=name)(y, zx, norm_w, dyn)


ATTN_KV_W = ATTN_N_KV * ATTN_HEAD_DIM
ATTN_Q_HALF = 512
ATTN_K_BLK = ATTN_N_Q * ATTN_HEAD_DIM // ATTN_KV_W
ATTN_V_BLK = ATTN_K_BLK + 1


def _attn_valid(first_block):
    w = ATTN_WINDOW
    qpos = lax.broadcasted_iota(jnp.int32, (w, 2 * w), 0) + w
    kpos = lax.broadcasted_iota(jnp.int32, (w, 2 * w), 1)
    rel = qpos - kpos
    return (rel >= 0) & (rel < w) & jnp.logical_not(first_block & (kpos < w))


def _attn_head_views(lo_ref, hi_ref):
    hd = ATTN_HEAD_DIM
    per_half = ATTN_Q_HALF // hd
    return [(lo_ref if h < per_half else hi_ref)[:, pl.ds((h % per_half) * hd, hd)] for h in range(ATTN_N_Q)]


def _attn_block_views(lo_ref, hi_ref, kc_ref, kp_ref, vc_ref, vp_ref):
    hd = ATTN_HEAD_DIM
    kv_cols = [pl.ds(kh * hd, hd) for kh in range(ATTN_N_KV)]
    kb = [jnp.concatenate([kp_ref[:, c], kc_ref[:, c]], axis=0) for c in kv_cols]
    vb = [jnp.concatenate([vp_ref[:, c], vc_ref[:, c]], axis=0) for c in kv_cols]
    return _attn_head_views(lo_ref, hi_ref), kb, vb


def _attn_scores(q, kb, valid):
    scale = ATTN_HEAD_DIM ** -0.5
    return [jnp.where(valid, _dot_nt(q[h], kb[h // ATTN_REP]) * scale, -jnp.inf) for h in range(ATTN_N_Q)]


def _attn_softmax(s, sink):
    heads = range(ATTN_N_Q)
    m = [jnp.maximum(jnp.max(s[h], axis=1, keepdims=True), sink[h]) for h in heads]
    e = [jnp.exp(s[h] - m[h]) for h in heads]
    es = [jnp.exp(sink[h] - m[h]) for h in heads]
    inv = [1.0 / (jnp.sum(e[h], axis=1, keepdims=True) + es[h]) for h in heads]
    return e, es, inv


def _attn_fwd(qkv, sinks_b, *, name, hook=None):
    t = qkv.shape[0]
    w = ATTN_WINDOW
    nb = t // w
    prev = lambda n: jnp.maximum(n - 1, 0)
    hk = _HookSlots(hook, n_in=7, n_out=1, n_scratch=0)

    def body(*refs):
        (qlo_ref, qhi_ref, kc_ref, kp_ref, vc_ref, vp_ref, sink_ref), (o_ref,), _ = hk.own(refs)
        if hook is not None:
            hk.run(refs, pl.program_id(0), nb)
        heads = range(ATTN_N_Q)
        q, kb, vb = _attn_block_views(qlo_ref, qhi_ref, kc_ref, kp_ref, vc_ref, vp_ref)
        sink = [sink_ref[h:h + 1, 0:1] for h in heads]
        e, _, inv = _attn_softmax(_attn_scores(q, kb, _attn_valid(pl.program_id(0) == 0)), sink)
        out = [_dot_nn((e[h] * inv[h]).astype(BF16), vb[h // ATTN_REP]).astype(o_ref.dtype) for h in heads]
        o_ref[...] = jnp.concatenate(out, axis=1)

    qh = lambda half: pl.BlockSpec((w, ATTN_Q_HALF), lambda n: (n, half))
    kv = lambda blk, idx: pl.BlockSpec((w, ATTN_KV_W), lambda n: (idx(n), blk))
    cur = lambda n: n
    outs = pl.pallas_call(
        body, grid=(nb,),
        in_specs=[qh(0), qh(1), kv(ATTN_K_BLK, cur), kv(ATTN_K_BLK, prev), kv(ATTN_V_BLK, cur), kv(ATTN_V_BLK, prev),
                  pl.BlockSpec((ATTN_N_Q, LANES), lambda n: (0, 0))] + hk.in_specs,
        out_specs=[pl.BlockSpec((w, D_MODEL), lambda n: (n, 0))] + hk.out_specs,
        out_shape=[jax.ShapeDtypeStruct((t, D_MODEL), BF16)] + hk.out_shape,
        scratch_shapes=hk.scratch,
        compiler_params=_params(*hk.semantics("parallel")), name=name)(qkv, qkv, qkv, qkv, qkv, qkv, sinks_b, *hk.inputs)
    return outs[0] if hook is None else (outs[0], outs[1:])


def _attn_bwd(qkv, sinks_b, dout, *, name):
    t = qkv.shape[0]
    w = ATTN_WINDOW
    nb = t // w
    hd = ATTN_HEAD_DIM
    clamp = lambda n: jnp.minimum(n, nb - 1)
    prev = lambda n: jnp.maximum(clamp(n) - 1, 0)

    def body(qlo_ref, qhi_ref, kc_ref, kp_ref, vc_ref, vp_ref, sink_ref, dolo_ref, dohi_ref,
             dq_ref, dkv_ref, dsink_ref, carry):
        n = pl.program_id(0)

        @pl.when(n == 0)
        def _():
            carry[...] = jnp.zeros_like(carry)
            dsink_ref[...] = jnp.zeros_like(dsink_ref)

        @pl.when(n < nb)
        def _():
            heads, kvs = range(ATTN_N_Q), range(ATTN_N_KV)
            q, kb, vb = _attn_block_views(qlo_ref, qhi_ref, kc_ref, kp_ref, vc_ref, vp_ref)
            do = _attn_head_views(dolo_ref, dohi_ref)
            sink = [sink_ref[h:h + 1, 0:1] for h in heads]
            s = _attn_scores(q, kb, _attn_valid(n == 0))
            dp = [_dot_nt(do[h], vb[h // ATTN_REP]) for h in heads]
            e, es, inv = _attn_softmax(s, sink)
            p = [e[h] * inv[h] for h in heads]
            delta = [jnp.sum(p[h] * dp[h], axis=1, keepdims=True) for h in heads]
            dsc = [(p[h] * (dp[h] - delta[h]) * (hd ** -0.5)).astype(BF16) for h in heads]
            pb = [p[h].astype(BF16) for h in heads]
            dq = [_dot_nn(dsc[h], kb[h // ATTN_REP]).astype(dq_ref.dtype) for h in heads]
            stack = lambda per_head, kh: jnp.concatenate(per_head[kh * ATTN_REP:(kh + 1) * ATTN_REP], axis=0)
            dkb = [_dot_tn(stack(dsc, kh), stack(q, kh)) for kh in kvs]
            dvb = [_dot_tn(stack(pb, kh), stack(do, kh)) for kh in kvs]
            dsink = [jnp.broadcast_to(jnp.sum(-es[h] * inv[h] * delta[h], axis=0, keepdims=True), (1, LANES)) for h in heads]
            dq_ref[...] = jnp.concatenate(dq, axis=1)
            dsink_ref[...] += jnp.concatenate(dsink, axis=0)
            dkv_ref[...] = (carry[...] + jnp.concatenate([d[0:w, :] for d in dkb + dvb], axis=1)).astype(dkv_ref.dtype)
            carry[...] = jnp.concatenate([d[w:2 * w, :] for d in dkb + dvb], axis=1)

        @pl.when(n == nb)
        def _():
            dkv_ref[...] = carry[...].astype(dkv_ref.dtype)

    qh = lambda half: pl.BlockSpec((w, ATTN_Q_HALF), lambda n: (clamp(n), half))
    kv = lambda blk, idx: pl.BlockSpec((w, ATTN_KV_W), lambda n: (idx(n), blk))
    return pl.pallas_call(
        body, grid=(nb + 1,),
        in_specs=[qh(0), qh(1), kv(ATTN_K_BLK, clamp), kv(ATTN_K_BLK, prev), kv(ATTN_V_BLK, clamp), kv(ATTN_V_BLK, prev),
                  pl.BlockSpec((ATTN_N_Q, LANES), lambda n: (0, 0)), qh(0), qh(1)],
        out_specs=[pl.BlockSpec((w, D_MODEL), lambda n: (clamp(n), 0)),
                   pl.BlockSpec((w, 2 * ATTN_KV_W), lambda n: (jnp.maximum(n - 1, 0), 0)),
                   pl.BlockSpec((ATTN_N_Q, LANES), lambda n: (0, 0))],
        out_shape=[jax.ShapeDtypeStruct((t, D_MODEL), BF16), jax.ShapeDtypeStruct((t, 2 * ATTN_KV_W), BF16),
                   jax.ShapeDtypeStruct((ATTN_N_Q, LANES), F32)],
        scratch_shapes=[pltpu.VMEM((w, 2 * ATTN_KV_W), F32)],
        compiler_params=_params("arbitrary"), name=name)(qkv, qkv, qkv, qkv, qkv, qkv, sinks_b, dout, dout)


def _sq_relu_epilogue(acc):
    r = jnp.maximum(acc, 0.0)
    return (r * r,)


def _sq_relu_bwd_epilogue(acc, act):
    return (acc * (2.0 * jnp.sqrt(act.astype(F32))),)


def _bias_epilogue(acc, bias):
    return (acc + bias,)


def _plain_run(stage, fn, *args, **kwargs):
    return fn(*args, **kwargs)


def _mlp_fwd(u, w_up, w_down, tag, run=_plain_run):
    act = run(f"mlp_up_{tag}", _matmul, u, w_up, mode="nn", out_dtypes=(BF16,), epilogue=_sq_relu_epilogue, b_shards=True,
              tm=BIG_TILE, name=f"mlp_up_{tag}")
    f = run(f"mlp_down_{tag}", _matmul, act, w_down, mode="nn", out_dtypes=(F32,), tk=BIG_TILE, name=f"mlp_down_{tag}")
    return act, f


def _mlp_bwd(u, act, w_up, w_down, df, tag):
    dpre = _matmul(df, w_down, mode="nt", out_dtypes=(BF16,), epilogue=_sq_relu_bwd_epilogue,
                   extras=((act, "tile"),), name=f"mlp_dact_{tag}")
    dw_down = _matmul(act, df, mode="tn", out_dtypes=(BF16,), tk=BIG_TILE, name=f"mlp_dwdown_{tag}")
    du = _matmul(dpre, w_up, mode="nt", out_dtypes=(F32,), b_shards=True, tm=BIG_TILE, name=f"mlp_du_{tag}")
    dw_up = _matmul(u, dpre, mode="tn", out_dtypes=(BF16,), out_shards=True, tk=BIG_TILE, name=f"mlp_dwup_{tag}")
    return du, dw_up, dw_down


def _head_param_rows(p):
    return jnp.broadcast_to(p.reshape(SSD_N_GROUPS, SSD_HPG, 1), (SSD_N_GROUPS, SSD_HPG, LANES))


def _local_step(x, target, wts, comm=None):
    t = x.shape[0]
    wts = dict(wts)
    row = lambda v: v.reshape(1, -1)
    mix_pre, mix_post, ffn_pre, ffn_post = wts["mix_pre_norm"], wts["mix_post_norm"], wts["ffn_pre_norm"], wts["ffn_post_norm"]

    def gathering(stage, fn, *args, **kwargs):
        hook = comm.gather_hook(stage) if comm is not None else None
        if hook is None:
            return fn(*args, **kwargs)
        out, got = fn(*args, hook=hook, **kwargs)
        wts.update(comm.weights_from(stage, got))
        return out

    u0 = _rms_fwd(x, row(mix_pre[0]), name="rms_pre_mix0")
    zx, dt_raw = gathering("in_proj", _matmul, u0, wts["ssd_w_in"], mode="nn", out_dtypes=(BF16,), tn=SSD_IN_TILE,
                           f32_block=SSD_DT_COL - (SSD_IN_PAD - SSD_IN_TILE),
                           name="ssd_in_proj")
    xc = gathering("conv", _conv_fwd, zx, wts["ssd_conv_w"], row(wts["ssd_conv_b"]), name="ssd_conv_fwd")
    bias_row = jnp.pad(wts["ssd_dt_bias"], (0, LANES - SSD_N_HEADS)).reshape(1, LANES)
    alog_row = jnp.pad(wts["ssd_a_log"], (0, LANES - SSD_N_HEADS)).reshape(1, LANES)
    dtr, cumr = _softplus_fwd(dt_raw, bias_row, alog_row, name="ssd_dt_fwd")
    alog_b, d_b = _head_param_rows(wts["ssd_a_log"]), _head_param_rows(wts["ssd_d"])
    y_ssd, states = gathering("scan", _ssd_fwd, xc, dtr, cumr, alog_b, d_b, name="ssd_scan_fwd")
    norm_w = row(wts["ssd_norm_w"])
    yn = _gate_norm_fwd(y_ssd, zx, norm_w, name="ssd_gate_norm_fwd")
    mix0 = _matmul(yn, wts["ssd_w_out"], mode="nn", out_dtypes=(F32,), tk=BIG_TILE, name="ssd_out_proj")
    h1, v0 = _rms_fwd(mix0, row(mix_post[0]), resid=x, want_u=row(ffn_pre[0]), name="rms_post_mix0")
    act0, f0 = _mlp_fwd(v0, wts["mlp_w_up0"], wts["mlp_w_down0"], "l0", run=gathering)
    h2, u1 = _rms_fwd(f0, row(ffn_post[0]), resid=h1, want_u=row(mix_pre[1]), name="rms_post_ffn0")

    qkv = _matmul(u1, wts["attn_w_qkv"], mode="nn", out_dtypes=(BF16,), epilogue=_bias_epilogue,
                  extras=((row(wts["attn_b_qkv"]), "row"),), b_shards=True, name="attn_qkv_proj")
    sinks_b = jnp.broadcast_to(wts["attn_sinks"].reshape(ATTN_N_Q, 1), (ATTN_N_Q, LANES))
    ao = gathering("attn_fwd", _attn_fwd, qkv, sinks_b, name="attn_fwd")
    mix1 = _matmul(ao, wts["attn_w_o"], mode="nn", out_dtypes=(F32,), epilogue=_bias_epilogue,
                   extras=((row(wts["attn_b_o"]), "row"),), name="attn_out_proj")
    h3, v1 = _rms_fwd(mix1, row(mix_post[1]), resid=h2, want_u=row(ffn_pre[1]), name="rms_post_mix1")
    act1, f1 = _mlp_fwd(v1, wts["mlp_w_up1"], wts["mlp_w_down1"], "l1")
    dh4, loss_tile = _rms_fwd(f1, row(ffn_post[1]), resid=h3, target=target, name="rms_post_ffn1_loss")

    df1, g_ffn_post1 = _rms_bwd(f1, row(ffn_post[1]), dh4, out_dtype=BF16, name="rms_post_ffn1_bwd")
    dv1, g_up1, g_down1 = _mlp_bwd(v1, act1, wts["mlp_w_up1"], wts["mlp_w_down1"], df1, "l1")
    dh3, g_ffn_pre1 = _rms_bwd(h3, row(ffn_pre[1]), dv1, resid=dh4, name="rms_pre_ffn1_bwd")
    dmix1, g_mix_post1, g_b_o = _rms_bwd(mix1, row(mix_post[1]), dh3, out_dtype=BF16, dx_col_sum=True, name="rms_post_mix1_bwd")
    g_w_o = _matmul(ao, dmix1, mode="tn", out_dtypes=(BF16,), tk=BIG_TILE, name="attn_dwo")
    dao = _matmul(dmix1, wts["attn_w_o"], mode="nt", out_dtypes=(BF16,), name="attn_dao")
    dq, dkv, g_sinks = _attn_bwd(qkv, sinks_b, dao, name="attn_bwd")
    dqkv = jnp.concatenate([dq, dkv], axis=1)
    g_b_qkv = _col_sum(dqkv, name="attn_bqkv_grad")
    g_w_qkv = _matmul(u1, dqkv, mode="tn", out_dtypes=(BF16,), tn=ATTN_QKV // N_CHIPS, out_shards=True, tk=BIG_TILE, name="attn_dwqkv")
    du1 = _matmul(dqkv, wts["attn_w_qkv"], mode="nt", out_dtypes=(F32,), b_shards=True, name="attn_du")
    dh2, g_mix_pre1 = _rms_bwd(h2, row(mix_pre[1]), du1, resid=dh3, name="rms_pre_mix1_bwd")

    df0, g_ffn_post0 = _rms_bwd(f0, row(ffn_post[0]), dh2, out_dtype=BF16, name="rms_post_ffn0_bwd")
    dv0, g_up0, g_down0 = _mlp_bwd(v0, act0, wts["mlp_w_up0"], wts["mlp_w_down0"], df0, "l0")
    dh1, g_ffn_pre0 = _rms_bwd(h1, row(ffn_pre[0]), dv0, resid=dh2, name="rms_pre_ffn0_bwd")
    dmix0, g_mix_post0 = _rms_bwd(mix0, row(mix_post[0]), dh1, out_dtype=BF16, name="rms_post_mix0_bwd")
    g_w_out = _matmul(yn, dmix0, mode="tn", out_dtypes=(BF16,), tk=BIG_TILE, name="ssd_dwout")
    dyn = _matmul(dmix0, wts["ssd_w_out"], mode="nt", out_dtypes=(BF16,), name="ssd_dyn")
    dy_ssd, dzx, g_norm_w = _gate_norm_bwd(y_ssd, zx, norm_w, dyn, name="ssd_gate_norm_bwd")
    mats = {"ssd_w_out": g_w_out, "attn_w_qkv": g_w_qkv, "attn_w_o": g_w_o,
            "mlp_w_up0": g_up0, "mlp_w_up1": g_up1, "mlp_w_down0": g_down0, "mlp_w_down1": g_down1}
    if comm is None:
        dxc, dbm, dcm, ddt_r, dpar = _ssd_bwd(xc, dtr, cumr, alog_b, d_b, states, dy_ssd, name="ssd_scan_bwd")
    else:
        (dxc, dbm, dcm, ddt_r, dpar), received = _ssd_bwd(xc, dtr, cumr, alog_b, d_b, states, dy_ssd,
                                                          name="ssd_scan_bwd", hook=comm.exchange_hook(mats, "early"))
        comm.received(received)
    dzx, g_conv_w, g_conv_b = _conv_bwd(zx, wts["ssd_conv_w"], row(wts["ssd_conv_b"]), dxc, dbm, dcm, dzx, name="ssd_conv_bwd")
    dzx, g_dt_bias = _softplus_bwd(dt_raw, bias_row, ddt_r, dzx, name="ssd_dt_bwd")
    g_w_in = _w_in_to_shards(_matmul(u0, dzx, mode="tn", out_dtypes=(BF16,), tn=SSD_IN_TILE, tk=BIG_TILE, name="ssd_dwin"), name="ssd_dwin_shards")
    mats["ssd_w_in"] = g_w_in
    if comm is None:
        du0 = _matmul(dzx, wts["ssd_w_in"], mode="nt", out_dtypes=(F32,), tk=SSD_IN_TILE, name="ssd_du")
    else:
        du0, received = _matmul(dzx, wts["ssd_w_in"], mode="nt", out_dtypes=(F32,), tk=SSD_IN_TILE, name="ssd_du",
                                hook=comm.exchange_hook(mats, "late"))
        comm.received(received)
    grad_x, g_mix_pre0 = _rms_bwd(x, row(mix_pre[0]), du0, resid=dh1, name="rms_pre_mix0_bwd")

    dpar = dpar.reshape(SSD_N_HEADS, LANES)
    vecs = {
        "ssd_conv_w": g_conv_w, "ssd_conv_b": g_conv_b.reshape(-1),
        "ssd_dt_bias": g_dt_bias[0, :SSD_N_HEADS], "ssd_a_log": dpar[:, 0], "ssd_d": dpar[:, 1],
        "ssd_norm_w": g_norm_w.reshape(-1), "attn_b_qkv": g_b_qkv.reshape(-1), "attn_sinks": g_sinks[:, 0],
        "attn_b_o": g_b_o.reshape(-1),
        "mix_pre_norm": jnp.concatenate([g_mix_pre0, g_mix_pre1]), "mix_post_norm": jnp.concatenate([g_mix_post0, g_mix_post1]),
        "ffn_pre_norm": jnp.concatenate([g_ffn_pre0, g_ffn_pre1]), "ffn_post_norm": jnp.concatenate([g_ffn_post0, g_ffn_post1]),
    }
    return loss_tile, grad_x, mats, vecs


def _mesh_position():
    return lax.axis_index("x"), lax.axis_index("y"), lax.axis_index("c")


def _flip(v, bit):
    return 1 - v if bit else v


OTHER_CHIPS = ((1, 0), (0, 1), (1, 1))


def _comm_params():
    return pltpu.CompilerParams(vmem_limit_bytes=VMEM_LIMIT)


def _staged_copies(srcs, dsts, bufs, sems_in, sems_out):
    loads = [pltpu.make_async_copy(s, b, sems_in.at[i]) for i, (s, b) in enumerate(zip(srcs, bufs))]
    stores = [pltpu.make_async_copy(b, d, sems_out.at[i]) for i, (b, d) in enumerate(zip(bufs, dsts))]
    return loads, stores


class _GatherHook:
    def __init__(self, mats, vecs=()):
        self.arrs = list(mats) + list(vecs)
        self.nm, self.n = len(mats), len(self.arrs)
        n_ici, n_fwd = (N_CHIPS - 1) * self.n, max((N_CHIPS - 1) * self.nm, 1)
        dma = pltpu.SemaphoreType.DMA
        self.out_shape = [jax.ShapeDtypeStruct((N_CHIPS,) + a.shape, a.dtype) for a in self.arrs]
        self.scratch = [pltpu.VMEM(a.shape, a.dtype) for a in self.arrs] + [
            dma((n_ici,)), dma((n_ici,)), dma((n_fwd,)), dma((n_fwd,)), dma((self.n,)), dma((self.n,))]

    def plan(self, ins, outs, scratch):
        n, nm = self.n, self.nm
        bufs = scratch[:n]
        ici_send, ici_recv, fwd_send, fwd_recv, load_sems, store_sems = scratch[n:]
        xi, yi, ci = _mesh_position()
        me = 2 * xi + yi
        loads, stores = _staged_copies(ins, [outs[i].at[me] for i in range(n)], bufs, load_sems, store_sems)
        sends, landed, forwards, from_sibling = [], [], [], []
        for j, (bx, by) in enumerate(OTHER_CHIPS):
            px, py = _flip(xi, bx), _flip(yi, by)
            peer = 2 * px + py
            for i in range(n):
                k = j * n + i
                mk = functools.partial(pltpu.make_async_remote_copy, send_sem=ici_send.at[k], recv_sem=ici_recv.at[k],
                                       device_id=(px, py, ci), device_id_type=MESH)
                if i < nm:
                    sends.append(mk(src_ref=ins[i].at[ci], dst_ref=outs[i].at[me, ci]))
                    landed.append(mk(src_ref=ins[i].at[ci], dst_ref=outs[i].at[peer, ci]))
                    kf = j * nm + i
                    fw = functools.partial(pltpu.make_async_remote_copy, send_sem=fwd_send.at[kf], recv_sem=fwd_recv.at[kf],
                                           device_id=(xi, yi, 1 - ci), device_id_type=MESH)
                    forwards.append(fw(src_ref=outs[i].at[peer, ci], dst_ref=outs[i].at[peer, ci]))
                    from_sibling.append(fw(src_ref=outs[i].at[peer, ci], dst_ref=outs[i].at[peer, 1 - ci]))
                else:
                    sends.append(mk(src_ref=ins[i], dst_ref=outs[i].at[me]))
                    landed.append(mk(src_ref=ins[i], dst_ref=outs[i].at[peer]))
                    forwards.append(None)
        return loads, stores, sends, landed, forwards, from_sibling

    @staticmethod
    def start(p):
        loads, _, sends, _, _, _ = p
        for cp in loads + sends:
            cp.start()

    @staticmethod
    def relay(p):
        loads, stores, _, landed, forwards, _ = p
        for ld, st in zip(loads, stores):
            ld.wait()
            st.start()
        for cp, fw in zip(landed, forwards):
            cp.wait_recv()
            if fw is not None:
                fw.start()

    @staticmethod
    def finish(p):
        _, stores, sends, _, forwards, from_sibling = p
        for cp in from_sibling:
            cp.wait_recv()
        for cp in sends + [fw for fw in forwards if fw is not None]:
            cp.wait_send()
        for st in stores:
            st.wait()


def _run_hook(hook, ins, outs, scratch, step, n_steps):
    p = hook.plan(ins, outs, scratch)
    relay_step = min(max(1, (3 * n_steps) // 4), n_steps - 1)

    @pl.when(step == 0)
    def _():
        hook.start(p)

    if relay_step < n_steps - 1:
        @pl.when(step == relay_step)
        def _():
            hook.relay(p)

    @pl.when(step == n_steps - 1)
    def _():
        if relay_step == n_steps - 1:
            hook.relay(p)
        hook.finish(p)


def _hook_call(hook, *, name):
    n = len(hook.arrs)

    def body(*refs):
        p = hook.plan(refs[:n], refs[n:n + len(hook.out_shape)], refs[n + len(hook.out_shape):])
        hook.start(p)
        hook.relay(p)
        hook.finish(p)

    return pl.pallas_call(
        body, in_specs=[ANY] * n, out_specs=[ANY] * len(hook.out_shape), out_shape=hook.out_shape,
        scratch_shapes=hook.scratch, compiler_params=_comm_params(), name=name)(*hook.arrs)


def _send_other_half(parts, *, name):
    n = len(parts)

    def body(*refs):
        ins, outs = refs[:n], refs[n:2 * n]
        send_sems, recv_sems = refs[2 * n:]
        xi, yi, ci = _mesh_position()
        sibling = (xi, yi, 1 - ci)
        for i in range(n):
            for s in range(N_CHIPS):
                pltpu.make_async_remote_copy(src_ref=ins[i].at[s, 1 - ci], dst_ref=outs[i].at[s], send_sem=send_sems.at[i],
                                             recv_sem=recv_sems.at[i], device_id=sibling, device_id_type=MESH).start()
        for i in range(n):
            pltpu.make_async_remote_copy(src_ref=outs[i], dst_ref=outs[i], send_sem=send_sems.at[i], recv_sem=recv_sems.at[i],
                                         device_id=sibling, device_id_type=MESH).wait()

    return pl.pallas_call(
        body, in_specs=[ANY] * n, out_specs=[ANY] * n,
        out_shape=[jax.ShapeDtypeStruct((p.shape[0],) + p.shape[2:], p.dtype) for p in parts],
        scratch_shapes=[pltpu.SemaphoreType.DMA((n,)), pltpu.SemaphoreType.DMA((n,))],
        name=name)(*parts)


ROW_BLOCKS = 8


def _add_sibling_half(parts, theirs, core, *, name):
    n = len(parts)

    def body(core_ref, *refs):
        for a_ref, b_ref, o_ref in zip(refs[:n], refs[n:2 * n], refs[2 * n:]):
            o_ref[...] = (a_ref[...].astype(F32) + b_ref[...].astype(F32)).astype(o_ref.dtype)

    mine = lambda p: pl.BlockSpec((None, None, p.shape[2] // ROW_BLOCKS, p.shape[3]), lambda s, rb, core_ref: (s, core_ref[0], rb, 0))
    other = lambda p: pl.BlockSpec((None, p.shape[1] // ROW_BLOCKS, p.shape[2]), lambda s, rb, core_ref: (s, rb, 0))
    return pl.pallas_call(
        body,
        grid_spec=pltpu.PrefetchScalarGridSpec(
            num_scalar_prefetch=1, grid=(N_CHIPS, ROW_BLOCKS),
            in_specs=[mine(p) for p in parts] + [other(q) for q in theirs], out_specs=[other(q) for q in theirs]),
        out_shape=[jax.ShapeDtypeStruct(q.shape, BF16) for q in theirs],
        compiler_params=_params("parallel", "parallel"), name=name)(core, *parts, *theirs)


class _ExchangeHook:
    def __init__(self, parts, to_all=()):
        self.arrs = list(parts) + list(to_all)
        self.n_parts, self.n = len(parts), len(self.arrs)
        n_ici, n_peer = max((N_CHIPS - 1) * self.n_parts, 1), (N_DEV - 1) * max(len(to_all), 1)
        dma = pltpu.SemaphoreType.DMA
        self.out_shape = [jax.ShapeDtypeStruct(p.shape, p.dtype) for p in parts] + [
            jax.ShapeDtypeStruct((N_DEV,) + a.shape, a.dtype) for a in to_all]
        self.scratch = [pltpu.VMEM(p.shape[1:], p.dtype) for p in parts] + [pltpu.VMEM(a.shape, a.dtype) for a in to_all] + [
            dma((n_ici,)), dma((n_ici,)), dma((n_peer,)), dma((n_peer,)), dma((self.n,)), dma((self.n,))]

    def plan(self, ins, outs, scratch):
        n, npt = self.n, self.n_parts
        bufs = scratch[:n]
        send_sems, recv_sems, all_send, all_recv, load_sems, store_sems = scratch[n:]
        xi, yi, ci = _mesh_position()
        me_chip = 2 * xi + yi
        me = 4 * xi + 2 * yi + ci
        loads, stores = _staged_copies([ins[i].at[me_chip] for i in range(npt)] + list(ins[npt:]),
                                       [outs[i].at[me_chip] for i in range(npt)] + [outs[i].at[me] for i in range(npt, n)],
                                       bufs, load_sems, store_sems)
        sends, recvs = [], []
        for j, (bx, by) in enumerate(OTHER_CHIPS):
            px, py = _flip(xi, bx), _flip(yi, by)
            peer = 2 * px + py
            for i in range(npt):
                k = j * npt + i
                mk = functools.partial(pltpu.make_async_remote_copy, src_ref=ins[i].at[peer], send_sem=send_sems.at[k],
                                       recv_sem=recv_sems.at[k], device_id=(px, py, ci), device_id_type=MESH)
                sends.append(mk(dst_ref=outs[i].at[me_chip]))
                recvs.append(mk(dst_ref=outs[i].at[peer]))
        for i in range(npt, n):
            for k in range(1, N_DEV):
                px, py, pc = _flip(xi, (k >> 2) & 1), _flip(yi, (k >> 1) & 1), _flip(ci, k & 1)
                slot = (i - npt) * (N_DEV - 1) + k - 1
                mk = functools.partial(pltpu.make_async_remote_copy, src_ref=ins[i], send_sem=all_send.at[slot],
                                       recv_sem=all_recv.at[slot], device_id=(px, py, pc), device_id_type=MESH)
                sends.append(mk(dst_ref=outs[i].at[me]))
                recvs.append(mk(dst_ref=outs[i].at[4 * px + 2 * py + pc]))
        return loads, stores, sends, recvs

    @staticmethod
    def start(p):
        loads, _, sends, _ = p
        for cp in loads + sends:
            cp.start()

    @staticmethod
    def relay(p):
        loads, stores, _, _ = p
        for ld, st in zip(loads, stores):
            ld.wait()
            st.start()

    @staticmethod
    def finish(p):
        _, stores, sends, recvs = p
        for cp in recvs:
            cp.wait_recv()
        for cp in sends:
            cp.wait_send()
        for st in stores:
            st.wait()


def _sum_chips(parts, *, name):
    n = len(parts)
    p = parts[0].shape[0]

    def body(*refs):
        s = pl.program_id(1)
        for x_ref, o_ref in zip(refs[:n], refs[n:]):
            @pl.when(s == 0)
            def _():
                o_ref[...] = x_ref[...].astype(F32)

            @pl.when(s > 0)
            def _():
                o_ref[...] += x_ref[...].astype(F32)

    blocks = lambda q: ROW_BLOCKS if q.shape[1] % (8 * ROW_BLOCKS) == 0 else 1
    assert len({blocks(q) for q in parts}) == 1
    nb = blocks(parts[0])
    return pl.pallas_call(
        body, grid=(nb, p),
        in_specs=[pl.BlockSpec((None, q.shape[1] // nb, q.shape[2]), lambda rb, s: (s, rb, 0)) for q in parts],
        out_specs=[pl.BlockSpec((q.shape[1] // nb, q.shape[2]), lambda rb, s: (rb, 0)) for q in parts],
        out_shape=[jax.ShapeDtypeStruct(q.shape[1:], F32) for q in parts],
        compiler_params=_params("parallel", "arbitrary"), name=name)(*parts)


def _swap_halves(halves, layers, *, name):
    n = len(halves)
    out_shapes, slots = [], []
    for i, h in enumerate(halves):
        pair = [p for p in layers if i in p]
        if pair and pair[0][1] == i:
            slots.append((slots[pair[0][0]][0], 1))
        elif pair:
            out_shapes.append(jax.ShapeDtypeStruct((2, 2) + h.shape, h.dtype))
            slots.append((len(out_shapes) - 1, 0))
        else:
            out_shapes.append(jax.ShapeDtypeStruct((2,) + h.shape, h.dtype))
            slots.append((len(out_shapes) - 1, None))
    n_out = len(out_shapes)

    def body(*refs):
        ins, outs, bufs = refs[:n], refs[n:n + n_out], refs[n + n_out:2 * n + n_out]
        send_sems, recv_sems, load_sems, store_sems = refs[2 * n + n_out:]
        xi, yi, ci = _mesh_position()
        own, sends, recvs = [], [], []
        for i in range(n):
            o, layer = slots[i]
            dst = (lambda core: outs[o].at[core]) if layer is None else (lambda core: outs[o].at[layer, core])
            own.append(dst(ci))
            mk = functools.partial(pltpu.make_async_remote_copy, src_ref=ins[i], send_sem=send_sems.at[i],
                                   recv_sem=recv_sems.at[i], device_id=(xi, yi, 1 - ci), device_id_type=MESH)
            sends.append(mk(dst_ref=dst(ci)))
            recvs.append(mk(dst_ref=dst(1 - ci)))
        loads, stores = _staged_copies(ins, own, bufs, load_sems, store_sems)
        for cp in loads + sends:
            cp.start()
        for ld, st in zip(loads, stores):
            ld.wait()
            st.start()
        for cp in recvs:
            cp.wait_recv()
        for cp in sends:
            cp.wait_send()
        for st in stores:
            st.wait()

    return pl.pallas_call(
        body, in_specs=[ANY] * n, out_specs=[ANY] * n_out, out_shape=out_shapes,
        scratch_shapes=[pltpu.VMEM(h.shape, h.dtype) for h in halves]
        + [pltpu.SemaphoreType.DMA((n,)), pltpu.SemaphoreType.DMA((n,)), pltpu.SemaphoreType.DMA((n,)), pltpu.SemaphoreType.DMA((n,))],
        compiler_params=_comm_params(), name=name)(*halves)


def _cast_bf16(layers, *, name, hook=None):
    n = len(layers)
    hk = _HookSlots(hook, n_in=n, n_out=n, n_scratch=0)

    def body(*refs):
        ins, outs, _ = hk.own(refs)
        if hook is not None:
            hk.run(refs, pl.program_id(0), ROW_BLOCKS)
        for i_ref, o_ref in zip(ins, outs):
            o_ref[...] = i_ref[...].astype(o_ref.dtype)

    in_blk = lambda a, l: pl.BlockSpec((None, a.shape[1] // ROW_BLOCKS, a.shape[2]), lambda i: (l, i, 0))
    out_blk = lambda a: pl.BlockSpec((a.shape[1] // ROW_BLOCKS, a.shape[2]), lambda i: (i, 0))
    outs = pl.pallas_call(
        body, grid=(ROW_BLOCKS,),
        in_specs=[in_blk(a, l) for a, l in layers] + hk.in_specs,
        out_specs=[out_blk(a) for a, _ in layers] + hk.out_specs,
        out_shape=[jax.ShapeDtypeStruct(a.shape[1:], BF16) for a, _ in layers] + hk.out_shape,
        scratch_shapes=hk.scratch,
        compiler_params=_params(*hk.semantics("parallel")), name=name)(*[a for a, _ in layers], *hk.inputs)
    return outs[:n] if hook is None else (outs[:n], outs[n:])


def _full_weight(name, gathered):
    s, _, r, c = gathered.shape
    if name == "ssd_w_in":
        return _w_in_from_shards(gathered.reshape(s, 2 * r, c), name="ssd_w_in_unshard")
    if name in ("attn_w_qkv", "mlp_w_up0", "mlp_w_up1"):
        return gathered.reshape(s, 2 * r, c)
    return gathered.reshape(s * 2 * r, c)


class _StepComm:
    GATHER = {"in_proj": ("mlp_w_up0", "attn_w_o"), "conv": ("mlp_w_down0",), "scan": ("ssd_w_out", "mlp_w_up1"),
              "mlp_up_l0": ("attn_w_qkv",), "attn_fwd": ("mlp_w_down1",)}
    EXCHANGE = {"early": ("ssd_w_out", "attn_w_qkv", "attn_w_o", "mlp_w_up0", "mlp_w_up1", "mlp_w_down0", "mlp_w_down1"),
                "late": ("ssd_w_in",)}

    def __init__(self, shards, core):
        self.shards, self.core = shards, core
        self.chip_parts = {}
        self._pending = None

    def gather_hook(self, stage):
        names = self.GATHER.get(stage)
        return _GatherHook([self.shards[n] for n in names]) if names else None

    def weights_from(self, stage, gathered):
        return {n: _full_weight(n, g) for n, g in zip(self.GATHER[stage], gathered)}

    def chip_sums(self, mats, tag):
        parts = [_shard_halves(a) for a in mats.values()]
        theirs = _send_other_half(parts, name=f"grad_sibling_send_{tag}")
        return _add_sibling_half(parts, theirs, self.core, name=f"grad_chip_sum_{tag}")

    def exchange_hook(self, mats, which):
        self._pending = self.EXCHANGE[which]
        return _ExchangeHook(self.chip_sums({n: mats[n] for n in self._pending}, which))

    def received(self, arrays):
        self.chip_parts.update(zip(self._pending, arrays))


ADAMW_ROW_BLOCKS = 16


def _adamw(ws, gs, ms, vs, *, name, by_lanes=False):
    n = len(ws)
    if by_lanes:
        nb = min(a.shape[2] for a in ws) // LANES
    else:
        nb = ADAMW_ROW_BLOCKS if all(a.shape[1] % (8 * ADAMW_ROW_BLOCKS) == 0 for a in ws) else 1

    def body(*refs):
        ins, outs = refs[:4 * n], refs[4 * n:]
        for i in range(n):
            w_ref, g_ref, m_ref, v_ref = ins[i], ins[n + i], ins[2 * n + i], ins[3 * n + i]
            go_ref, d_ref, nm_ref, nv_ref = outs[i], outs[n + i], outs[2 * n + i], outs[3 * n + i]
            gv = g_ref[...]
            nm = ADAM_B1 * m_ref[...] + (1.0 - ADAM_B1) * gv
            nv = ADAM_B2 * v_ref[...] + (1.0 - ADAM_B2) * (gv * gv)
            m_hat = nm / (1.0 - ADAM_B1 ** ADAM_STEP)
            v_hat = nv / (1.0 - ADAM_B2 ** ADAM_STEP)
            go_ref[...] = gv
            d_ref[...] = -ADAM_LR * (m_hat / (jnp.sqrt(v_hat) + ADAM_EPS) + ADAM_WD * w_ref[...])
            nm_ref[...] = nm
            nv_ref[...] = nv

    if by_lanes:
        blks = [pl.BlockSpec((a.shape[0], a.shape[1], a.shape[2] // nb), lambda i: (0, 0, i)) for a in ws]
    else:
        blks = [pl.BlockSpec((a.shape[0], a.shape[1] // nb, a.shape[2]), lambda i: (0, i, 0)) for a in ws]
    shapes = [jax.ShapeDtypeStruct(a.shape, F32) for a in ws]
    outs = pl.pallas_call(body, grid=(nb,), in_specs=blks * 4, out_specs=blks * 4, out_shape=shapes * 4,
                          compiler_params=_params("parallel"), name=name)(*ws, *gs, *ms, *vs)
    return [tuple(outs[k * n + i] for k in range(4)) for i in range(n)]


SM_CONV_B, SM_NORM_W, SM_MIX_PRE, SM_MIX_POST, SM_FFN_PRE, SM_FFN_POST, SM_MISC, SM_CONV_W, SM_B_QKV, SM_B_O = 0, 4, 6, 8, 10, 12, 14, 16, 32, 34
SM_ROWS = 40
MISC_DT_BIAS, MISC_A_LOG, MISC_D, MISC_SINKS, MISC_LOSS = 0, 32, 64, 96, 112


def _shard_halves(a):
    c = a.shape[-1]
    return a.reshape(N_CHIPS, 2, -1, c)


def _rows(v):
    return v.reshape(-1, D_MODEL)


def _misc_row(dt_bias, a_log, d, sinks, loss):
    pad = jnp.zeros((D_MODEL - MISC_LOSS - 1,), F32)
    return jnp.concatenate([dt_bias.reshape(-1), a_log.reshape(-1), d.reshape(-1), sinks.reshape(-1), loss.reshape(1), pad]).reshape(1, D_MODEL)


def _replicated_rows(p, loss):
    return jnp.concatenate([
        _rows(p["ssd_conv_b"]), _rows(p["ssd_norm_w"]), _rows(p["mix_pre_norm"]), _rows(p["mix_post_norm"]),
        _rows(p["ffn_pre_norm"]), _rows(p["ffn_post_norm"]),
        _misc_row(p["ssd_dt_bias"], p["ssd_a_log"], p["ssd_d"], p["attn_sinks"], loss), jnp.zeros((1, D_MODEL), F32)], axis=0)


def _sharded_rows(conv_w, b_qkv, b_o):
    last = jnp.concatenate([b_qkv.reshape(-1), b_o.reshape(-1), jnp.zeros((D_MODEL - 640,), F32)]).reshape(1, D_MODEL)
    return jnp.concatenate([conv_w.reshape(SSD_CONV_WIDTH, D_MODEL), last, jnp.zeros((3, D_MODEL), F32)], axis=0)


REPLICATED = ("ssd_conv_b", "ssd_dt_bias", "ssd_a_log", "ssd_d", "ssd_norm_w", "attn_sinks",
              "mix_pre_norm", "mix_post_norm", "ffn_pre_norm", "ffn_post_norm")
MATRICES = ("ssd_w_in", "ssd_w_out", "attn_w_qkv", "attn_w_o", "mlp_w_up", "mlp_w_down")
WEIGHT_NAMES = ("ssd_w_in", "ssd_conv_w", "ssd_conv_b", "ssd_dt_bias", "ssd_a_log", "ssd_d", "ssd_norm_w", "ssd_w_out",
                "attn_w_qkv", "attn_b_qkv", "attn_sinks", "attn_w_o", "attn_b_o", "mlp_w_up", "mlp_w_down",
                "mix_pre_norm", "mix_post_norm", "ffn_pre_norm", "ffn_post_norm")


def _unpack_small(rows16, rows8, like):
    misc = rows16[SM_MISC]
    out = {
        "ssd_conv_b": rows16[SM_CONV_B:SM_CONV_B + 4], "ssd_norm_w": rows16[SM_NORM_W:SM_NORM_W + 2],
        "mix_pre_norm": rows16[SM_MIX_PRE:SM_MIX_PRE + 2], "mix_post_norm": rows16[SM_MIX_POST:SM_MIX_POST + 2],
        "ffn_pre_norm": rows16[SM_FFN_PRE:SM_FFN_PRE + 2], "ffn_post_norm": rows16[SM_FFN_POST:SM_FFN_POST + 2],
        "ssd_dt_bias": misc[MISC_DT_BIAS:MISC_DT_BIAS + 32], "ssd_a_log": misc[MISC_A_LOG:MISC_A_LOG + 32],
        "ssd_d": misc[MISC_D:MISC_D + 32], "attn_sinks": misc[MISC_SINKS:MISC_SINKS + 16],
        "ssd_conv_w": rows8[0:SSD_CONV_WIDTH], "attn_b_qkv": rows8[SSD_CONV_WIDTH, 0:384], "attn_b_o": rows8[SSD_CONV_WIDTH, 384:640],
    }
    return {k: v.reshape(like[k].shape) for k, v in out.items()}


def kernel(x, ssd_w_in, ssd_conv_w, ssd_conv_b, ssd_dt_bias, ssd_a_log, ssd_d, ssd_norm_w, ssd_w_out, attn_w_qkv, attn_b_qkv, attn_sinks, attn_w_o, attn_b_o, mlp_w_up, mlp_w_down, mix_pre_norm, mix_post_norm, ffn_pre_norm, ffn_post_norm, loss_target, m_ssd_w_in, m_ssd_conv_w, m_ssd_conv_b, m_ssd_dt_bias, m_ssd_a_log, m_ssd_d, m_ssd_norm_w, m_ssd_w_out, m_attn_w_qkv, m_attn_b_qkv, m_attn_sinks, m_attn_w_o, m_attn_b_o, m_mlp_w_up, m_mlp_w_down, m_mix_pre_norm, m_mix_post_norm, m_ffn_pre_norm, m_ffn_post_norm, v_ssd_w_in, v_ssd_conv_w, v_ssd_conv_b, v_ssd_dt_bias, v_ssd_a_log, v_ssd_d, v_ssd_norm_w, v_ssd_w_out, v_attn_w_qkv, v_attn_b_qkv, v_attn_sinks, v_attn_w_o, v_attn_b_o, v_mlp_w_up, v_mlp_w_down, v_mix_pre_norm, v_mix_post_norm, v_ffn_pre_norm, v_ffn_post_norm):
    w = dict(zip(WEIGHT_NAMES, (ssd_w_in, ssd_conv_w, ssd_conv_b, ssd_dt_bias, ssd_a_log, ssd_d, ssd_norm_w, ssd_w_out, attn_w_qkv, attn_b_qkv, attn_sinks, attn_w_o, attn_b_o, mlp_w_up, mlp_w_down, mix_pre_norm, mix_post_norm, ffn_pre_norm, ffn_post_norm)))
    m = dict(zip(WEIGHT_NAMES, (m_ssd_w_in, m_ssd_conv_w, m_ssd_conv_b, m_ssd_dt_bias, m_ssd_a_log, m_ssd_d, m_ssd_norm_w, m_ssd_w_out, m_attn_w_qkv, m_attn_b_qkv, m_attn_sinks, m_attn_w_o, m_attn_b_o, m_mlp_w_up, m_mlp_w_down, m_mix_pre_norm, m_mix_post_norm, m_ffn_pre_norm, m_ffn_post_norm)))
    v = dict(zip(WEIGHT_NAMES, (v_ssd_w_in, v_ssd_conv_w, v_ssd_conv_b, v_ssd_dt_bias, v_ssd_a_log, v_ssd_d, v_ssd_norm_w, v_ssd_w_out, v_attn_w_qkv, v_attn_b_qkv, v_attn_sinks, v_attn_w_o, v_attn_b_o, v_mlp_w_up, v_mlp_w_down, v_mix_pre_norm, v_mix_post_norm, v_ffn_pre_norm, v_ffn_post_norm)))
    chip = 2 * lax.axis_index("x") + lax.axis_index("y")

    two_halves = lambda a: a.reshape(2, a.shape[-2] // 2, a.shape[-1])
    later = {"ssd_w_out": (w["ssd_w_out"], 0), "attn_w_qkv": (w["attn_w_qkv"], 0), "attn_w_o": (w["attn_w_o"], 0),
             "mlp_w_up0": (w["mlp_w_up"], 0), "mlp_w_up1": (w["mlp_w_up"], 1),
             "mlp_w_down0": (w["mlp_w_down"], 0), "mlp_w_down1": (w["mlp_w_down"], 1)}
    first = _GatherHook([two_halves(w["ssd_w_in"].astype(BF16))], [w["ssd_conv_w"][0], w["attn_b_qkv"], w["attn_b_o"]])
    cast, (g_in, g_conv, g_bqkv, g_bo) = _cast_bf16(list(later.values()), name="weights_to_bf16", hook=first)
    core = lax.axis_index("c").astype(jnp.int32).reshape(1)
    comm = _StepComm({k: two_halves(a) for k, a in zip(later, cast)}, core)
    full = {
        "ssd_w_in": _full_weight("ssd_w_in", g_in),
        "ssd_conv_w": g_conv.transpose(1, 0, 2).reshape(SSD_CONV_WIDTH, SSD_CONV_DIM),
        "attn_b_qkv": g_bqkv.reshape(ATTN_QKV), "attn_b_o": g_bo.reshape(D_MODEL),
    }
    for name in REPLICATED:
        full[name] = w[name][0] if name.startswith(("ssd_", "attn_")) else w[name]

    loss_tile, grad_x, gm, g = _local_step(x[0], loss_target[0], full, comm)

    conv_w_rows = g["ssd_conv_w"].reshape(SSD_CONV_WIDTH * N_CHIPS, D_MODEL)
    b_qkv_rows = jnp.pad(g["attn_b_qkv"], (0, 2 * D_MODEL - ATTN_QKV)).reshape(2, D_MODEL)
    small = jnp.concatenate([_replicated_rows(g, loss_tile[0, 0]), conv_w_rows, b_qkv_rows, _rows(g["attn_b_o"]),
                             jnp.zeros((SM_ROWS - SM_B_O - 1, D_MODEL), F32)], axis=0)
    small_all, = _hook_call(_ExchangeHook([], [small]), name="vector_grad_all_gather")
    order = ("ssd_w_in", "ssd_w_out", "attn_w_qkv", "attn_w_o", "mlp_w_up0", "mlp_w_up1", "mlp_w_down0", "mlp_w_down1")
    halves = _sum_chips([comm.chip_parts[k] for k in order], name="grad_sum")
    r_in, r_out, r_qkv, r_o, r_up, r_down = _swap_halves(halves, layers=((4, 5), (6, 7)), name="grad_halves_swap")
    small_sum, = _sum_chips([small_all], name="small_grad_sum")

    grads = {"ssd_w_in": r_in, "ssd_w_out": r_out, "attn_w_qkv": r_qkv, "attn_w_o": r_o, "mlp_w_up": r_up, "mlp_w_down": r_down}
    grads = {k: a.reshape(w[k].shape) for k, a in grads.items()}
    conv_w_g = lax.dynamic_index_in_dim(small_sum[SM_CONV_W:SM_CONV_W + 16].reshape(SSD_CONV_WIDTH, N_CHIPS, D_MODEL), chip, axis=1, keepdims=False)
    b_qkv_g = lax.dynamic_slice_in_dim(small_sum[SM_B_QKV:SM_B_QKV + 2].reshape(-1), chip * 384, 384)
    b_o_g = lax.dynamic_slice_in_dim(small_sum[SM_B_O], chip * 256, 256)
    small_g = jnp.concatenate([small_sum[0:16], _sharded_rows(conv_w_g, b_qkv_g, b_o_g)], axis=0)
    grads.update(_unpack_small(small_g[0:16], small_g[16:24], w))
    loss = small_sum[SM_MISC, MISC_LOSS]

    delta, new_m, new_v = {}, {}, {}
    stored = lambda a: jnp.swapaxes(a, 1, 2)
    rest = [name for name in MATRICES if name != "ssd_w_in"]
    mats = lambda p: [p[name] for name in rest]
    results = dict(zip(rest, _adamw(mats(w), mats(grads), mats(m), mats(v), name="adamw_matrices")))
    (w_in_result,) = _adamw([stored(w["ssd_w_in"])], [stored(grads["ssd_w_in"])], [stored(m["ssd_w_in"])],
                            [stored(v["ssd_w_in"])], by_lanes=True, name="adamw_ssd_w_in")
    results["ssd_w_in"] = tuple(stored(a) for a in w_in_result)
    for name in MATRICES:
        grads[name], delta[name], new_m[name], new_v[name] = results[name]
    zero = jnp.zeros((), F32)
    small_pack = lambda p: jnp.concatenate([_replicated_rows({k: p[k] for k in REPLICATED}, zero),
                                            _sharded_rows(p["ssd_conv_w"], p["attn_b_qkv"], p["attn_b_o"])], axis=0)[None]
    (_, d_s, m_s, v_s), = _adamw([small_pack(w)], [small_g[None]], [small_pack(m)], [small_pack(v)], name="adamw_vectors")
    d_s, m_s, v_s = d_s[0], m_s[0], v_s[0]
    delta.update(_unpack_small(d_s[0:16], d_s[16:24], w))
    new_m.update(_unpack_small(m_s[0:16], m_s[16:24], w))
    new_v.update(_unpack_small(v_s[0:16], v_s[16:24], w))

    return (loss, grad_x[None], *[grads[n] for n in WEIGHT_NAMES], *[delta[n] for n in WEIGHT_NAMES],
            *[new_m[n] for n in WEIGHT_NAMES], *[new_v[n] for n in WEIGHT_NAMES])
```

```python
import functools

import jax
import jax.numpy as jnp
from jax import lax
from jax.experimental import pallas as pl
from jax.experimental.pallas import tpu as pltpu

F32 = jnp.float32
BF16 = jnp.bfloat16

D_MODEL = 1024
SSD_D_INNER = 2048
SSD_HEAD_DIM = 64
SSD_N_HEADS = 32
SSD_N_GROUPS = 8
SSD_HPG = 4
SSD_D_STATE = 128
SSD_CONV_WIDTH = 4
SSD_CHUNK = 128
SSD_CONV_DIM = 4096
SSD_IN_DIM = 6176
SSD_IN_PAD = 6400
SSD_IN_TILE = 1280
SSD_DT_COL = 6144
SSD_GW = SSD_HPG * SSD_HEAD_DIM
ATTN_HEAD_DIM = 64
ATTN_N_Q = 16
ATTN_N_KV = 4
ATTN_REP = 4
ATTN_WINDOW = 128
ATTN_QKV = 1536
D_FF = 4096
NORM_EPS = 1e-6

ADAM_LR = 0.001
ADAM_B1 = 0.9
ADAM_B2 = 0.999
ADAM_EPS = 1e-08
ADAM_WD = 0.01
ADAM_STEP = 10

N_CHIPS = 4
N_DEV = 8
LANES = 128
VMEM_LIMIT = 48 * 1024 * 1024
BIG_TILE = 2048

MESH = pl.DeviceIdType.MESH


def _params(*sem):
    return pltpu.CompilerParams(dimension_semantics=sem, vmem_limit_bytes=VMEM_LIMIT)


def _dot(a, b, dims):
    return lax.dot_general(a, b, (dims, ((), ())), preferred_element_type=F32)


def _dot_nn(a, b):
    return _dot(a, b, ((1,), (0,)))


def _dot_nt(a, b):
    return _dot(a, b, ((1,), (1,)))


def _dot_tn(a, b):
    return _dot(a, b, ((0,), (0,)))


def _sigmoid(x):
    return 0.5 * jnp.tanh(0.5 * x) + 0.5


ANY = pl.BlockSpec(memory_space=pl.ANY)


class _HookSlots:
    def __init__(self, hook, n_in, n_out, n_scratch):
        self.hook = hook
        self.n_in, self.n_out, self.n_scratch = n_in, n_out, n_scratch
        self.inputs = list(hook.arrs) if hook else []
        self.out_shape = list(hook.out_shape) if hook else []
        self.scratch = list(hook.scratch) if hook else []
        self.in_specs = [ANY] * len(self.inputs)
        self.out_specs = [ANY] * len(self.out_shape)

    def _split(self, refs):
        a = self.n_in
        b = a + len(self.inputs)
        c = b + self.n_out
        d = c + len(self.out_shape)
        e = d + self.n_scratch
        return refs[:a], refs[a:b], refs[b:c], refs[c:d], refs[d:e], refs[e:]

    def own(self, refs):
        ins, _, outs, _, scratch, _ = self._split(refs)
        return ins, outs, scratch

    def run(self, refs, step, n_steps):
        _, h_in, _, h_out, _, h_scratch = self._split(refs)
        _run_hook(self.hook, h_in, h_out, h_scratch, step, n_steps)

    def semantics(self, *sem):
        return sem if self.hook is None else ("arbitrary",) * len(sem)


def _matmul(a, b, *, mode, out_dtypes, name, epilogue=None, extras=(), tm=1024, tn=1024, tk=1024,
            b_shards=False, out_shards=False, hook=None, f32_block=None):
    f32_tail = f32_block is not None
    if b_shards:
        s, b_rows, b_cols = b.shape
        b2 = (b_rows, s * b_cols)
        if mode == "nn":
            tn = b_cols
        else:
            assert mode == "nt"
            tk = b_cols
    else:
        b2 = b.shape
    if mode == "nn":
        (m, k), (k2, n) = a.shape, b2
    elif mode == "nt":
        (m, k), (n, k2) = a.shape, b2
    else:
        (k, m), (k2, n) = a.shape, b2
    assert k == k2, (a.shape, b.shape, mode)
    tm, tn, tk = min(tm, m), min(tn, n), min(tk, k)
    assert m % tm == 0 and n % tn == 0 and k % tk == 0, (m, n, k, tm, tn, tk)
    nk = k // tk
    if mode == "tn":
        a_spec = pl.BlockSpec((tk, tm), lambda i, j, kk: (kk, i))
    else:
        a_spec = pl.BlockSpec((tm, tk), lambda i, j, kk: (i, kk))
    if b_shards and mode == "nn":
        b_spec = pl.BlockSpec((None, tk, tn), lambda i, j, kk: (j, kk, 0))
    elif b_shards:
        b_spec = pl.BlockSpec((None, tn, tk), lambda i, j, kk: (kk, j, 0))
    elif mode == "nt":
        b_spec = pl.BlockSpec((tn, tk), lambda i, j, kk: (j, kk))
    else:
        b_spec = pl.BlockSpec((tk, tn), lambda i, j, kk: (kk, j))
    dims = {"nn": ((1,), (0,)), "nt": ((1,), (1,)), "tn": ((0,), (0,))}[mode]
    ex_specs = []
    for arr, kind in extras:
        if kind == "tile":
            ex_specs.append(pl.BlockSpec((tm, tn), lambda i, j, kk: (i, j)))
        else:
            ex_specs.append(pl.BlockSpec((1, tn), lambda i, j, kk: (0, j)))
    n_ex, n_out = len(extras), len(out_dtypes)
    if epilogue is None:
        epilogue = lambda acc: (acc,)
    hk = _HookSlots(hook, n_in=2 + n_ex, n_out=n_out + f32_tail, n_scratch=0 if nk == 1 else 1)
    grid = (m // tm, n // tn, nk)

    def body(*refs):
        (a_ref, b_ref, *ex), outs, scratch = hk.own(refs)
        if hook is not None:
            step = (pl.program_id(0) * grid[1] + pl.program_id(1)) * grid[2] + pl.program_id(2)
            hk.run(refs, step, grid[0] * grid[1] * grid[2])

        def finish(acc):
            res = epilogue(acc, *[e[...] for e in ex])
            for o, r in zip(outs, res):
                o[...] = r.astype(o.dtype)
            if f32_tail:
                outs[n_out][...] = acc[:, f32_block:f32_block + LANES]

        if nk == 1:
            finish(_dot(a_ref[...], b_ref[...], dims))
        else:
            acc_ref = scratch[0]
            kk = pl.program_id(2)

            @pl.when(kk == 0)
            def _():
                acc_ref[...] = jnp.zeros_like(acc_ref)

            acc_ref[...] += _dot(a_ref[...], b_ref[...], dims)

            @pl.when(kk == nk - 1)
            def _():
                finish(acc_ref[...])

    if out_shards:
        out_spec = pl.BlockSpec((None, tm, tn), lambda i, j, kk: (j, i, 0))
        out_dims = (n // tn, m, tn)
    else:
        out_spec = pl.BlockSpec((tm, tn), lambda i, j, kk: (i, j))
        out_dims = (m, n)
    tail_specs = [pl.BlockSpec((tm, LANES), lambda i, j, kk: (i, 0))] if f32_tail else []
    tail_shapes = [jax.ShapeDtypeStruct((m, LANES), F32)] if f32_tail else []
    outs = pl.pallas_call(
        body,
        grid=grid,
        in_specs=[a_spec, b_spec] + ex_specs + hk.in_specs,
        out_specs=[out_spec for _ in out_dtypes] + tail_specs + hk.out_specs,
        out_shape=[jax.ShapeDtypeStruct(out_dims, dt) for dt in out_dtypes] + tail_shapes + hk.out_shape,
        scratch_shapes=([] if nk == 1 else [pltpu.VMEM((tm, tn), F32)]) + hk.scratch,
        compiler_params=_params(*hk.semantics("parallel", "arbitrary" if f32_tail else "parallel", "arbitrary")),
        name=name,
    )(a, b, *[arr for arr, _ in extras], *hk.inputs)
    n_own = n_out + f32_tail
    own = outs[0] if n_own == 1 else outs[:n_own]
    return own if hook is None else (own, outs[n_own:])


def _row_tile(t, want):
    return min(t, want)


def _rms_fwd(x, w, *, name, resid=None, want_u=None, target=None):
    t, d = x.shape
    tr = _row_tile(t, 512)

    def norm(v, wv):
        return v * lax.rsqrt(jnp.mean(v * v, axis=-1, keepdims=True) + NORM_EPS) * wv

    row = pl.BlockSpec((tr, d), lambda i: (i, 0))
    vec = pl.BlockSpec((1, d), lambda i: (0, 0))
    if target is not None:
        def body(x_ref, w_ref, r_ref, t_ref, dh_ref, loss_ref):
            err = r_ref[...] + norm(x_ref[...], w_ref[...]) - t_ref[...]
            dh_ref[...] = err * (1.0 / d)

            @pl.when(pl.program_id(0) == 0)
            def _():
                loss_ref[...] = jnp.zeros_like(loss_ref)

            part = jnp.sum(jnp.sum(err * err, axis=1, keepdims=True), axis=0, keepdims=True) * (0.5 / d)
            loss_ref[...] += jnp.broadcast_to(part, loss_ref.shape)

        return pl.pallas_call(
            body, grid=(t // tr,), in_specs=[row, vec, row, row],
            out_specs=[row, pl.BlockSpec((8, LANES), lambda i: (0, 0))],
            out_shape=[jax.ShapeDtypeStruct((t, d), F32), jax.ShapeDtypeStruct((8, LANES), F32)],
            compiler_params=_params("arbitrary"), name=name)(x, w, resid, target)
    if resid is None:
        def body(x_ref, w_ref, o_ref):
            o_ref[...] = norm(x_ref[...], w_ref[...]).astype(BF16)
        ins, in_specs = (x, w), [row, vec]
        out_shape, out_specs = jax.ShapeDtypeStruct((t, d), BF16), row
    elif want_u is None:
        def body(x_ref, w_ref, r_ref, o_ref):
            o_ref[...] = r_ref[...] + norm(x_ref[...], w_ref[...])
        ins, in_specs = (x, w, resid), [row, vec, row]
        out_shape, out_specs = jax.ShapeDtypeStruct((t, d), F32), row
    else:
        def body(x_ref, w_ref, r_ref, w2_ref, o_ref, u_ref):
            h = r_ref[...] + norm(x_ref[...], w_ref[...])
            o_ref[...] = h
            u_ref[...] = norm(h, w2_ref[...]).astype(BF16)
        ins, in_specs = (x, w, resid, want_u), [row, vec, row, vec]
        out_shape = [jax.ShapeDtypeStruct((t, d), F32), jax.ShapeDtypeStruct((t, d), BF16)]
        out_specs = [row, row]
    return pl.pallas_call(body, grid=(t // tr,), in_specs=in_specs, out_specs=out_specs, out_shape=out_shape,
                          compiler_params=_params("parallel"), name=name)(*ins)


def _rms_bwd(x, w, dy, *, name, resid=None, out_dtype=F32, dx_col_sum=False):
    t, d = x.shape
    tr = _row_tile(t, 512)
    row = pl.BlockSpec((tr, d), lambda i: (i, 0))
    vec = pl.BlockSpec((1, d), lambda i: (0, 0))
    has_res = resid is not None

    def body(x_ref, w_ref, dy_ref, *rest):
        r_ref = rest[0] if has_res else None
        dx_ref, dw_ref = rest[has_res:has_res + 2]
        xv = x_ref[...]
        dyv = dy_ref[...].astype(F32)
        r = lax.rsqrt(jnp.mean(xv * xv, axis=-1, keepdims=True) + NORM_EPS)
        xhat = xv * r
        dyw = dyv * w_ref[...]
        dx = r * (dyw - xhat * jnp.mean(dyw * xhat, axis=-1, keepdims=True))
        if has_res:
            dx = dx + r_ref[...]
        dx_ref[...] = dx.astype(dx_ref.dtype)

        sums = [(dw_ref, dyv * xhat)] + ([(rest[-1], dx)] if dx_col_sum else [])

        @pl.when(pl.program_id(0) == 0)
        def _():
            for acc_ref, _ in sums:
                acc_ref[...] = jnp.zeros_like(acc_ref)

        for acc_ref, rows in sums:
            acc_ref[...] += jnp.sum(rows, axis=0, keepdims=True)

    ins = (x, w, dy) + ((resid,) if has_res else ())
    in_specs = [row, vec, row] + ([row] if has_res else [])
    n_vec = 2 if dx_col_sum else 1
    return pl.pallas_call(
        body, grid=(t // tr,), in_specs=in_specs, out_specs=[row] + [vec] * n_vec,
        out_shape=[jax.ShapeDtypeStruct((t, d), out_dtype)] + [jax.ShapeDtypeStruct((1, d), F32)] * n_vec,
        compiler_params=_params("arbitrary"), name=name)(*ins)


def _col_sum(x, *, name):
    t, n = x.shape
    tr = _row_tile(t, 512)

    def body(x_ref, o_ref):
        @pl.when(pl.program_id(0) == 0)
        def _():
            o_ref[...] = jnp.zeros_like(o_ref)

        o_ref[...] += jnp.sum(x_ref[...].astype(F32), axis=0, keepdims=True)

    return pl.pallas_call(
        body, grid=(t // tr,), in_specs=[pl.BlockSpec((tr, n), lambda i: (i, 0))],
        out_specs=pl.BlockSpec((1, n), lambda i: (0, 0)), out_shape=jax.ShapeDtypeStruct((1, n), F32),
        compiler_params=_params("arbitrary"), name=name)(x)


SSD_IN_SHARD = SSD_IN_DIM // N_CHIPS


def _w_in_from_shards(shards, *, name):
    d = shards.shape[1]
    tr = 256

    def body(s_ref, o_ref):
        o_ref[:, pl.ds(SSD_DT_COL, SSD_IN_PAD - SSD_DT_COL)] = jnp.zeros((tr, SSD_IN_PAD - SSD_DT_COL), o_ref.dtype)
        for s in range(N_CHIPS):
            o_ref[:, pl.ds(SSD_IN_SHARD * s, SSD_IN_SHARD)] = s_ref[s]

    return pl.pallas_call(
        body, grid=(d // tr,), in_specs=[pl.BlockSpec((N_CHIPS, tr, SSD_IN_SHARD), lambda i: (0, i, 0))],
        out_specs=pl.BlockSpec((tr, SSD_IN_PAD), lambda i: (i, 0)),
        out_shape=jax.ShapeDtypeStruct((d, SSD_IN_PAD), shards.dtype),
        compiler_params=_params("parallel"), name=name)(shards)


def _w_in_to_shards(g, *, name):
    d = g.shape[0]
    tr = 256

    def body(g_ref, o_ref):
        for s in range(N_CHIPS):
            o_ref[s] = g_ref[:, pl.ds(SSD_IN_SHARD * s, SSD_IN_SHARD)].astype(o_ref.dtype)

    return pl.pallas_call(
        body, grid=(d // tr,), in_specs=[pl.BlockSpec((tr, SSD_IN_PAD), lambda i: (i, 0))],
        out_specs=pl.BlockSpec((N_CHIPS, tr, SSD_IN_SHARD), lambda i: (0, i, 0)),
        out_shape=jax.ShapeDtypeStruct((N_CHIPS, d, SSD_IN_SHARD), BF16),
        compiler_params=_params("parallel"), name=name)(g)


XBC_COL0 = SSD_D_INNER // LANES


def _shift_down(v, k, row_ids):
    return jnp.where(row_ids >= k, pltpu.roll(v, k, axis=0), 0.0)


def _shift_up(v, k, row_ids):
    n = v.shape[0]
    return jnp.where(row_ids < n - k, pltpu.roll(v, n - k, axis=0), 0.0)


def _conv_pre(x, w, b, row_ids):
    pre = b + w[3:4, :] * x
    for k in (1, 2, 3):
        pre = pre + w[3 - k:4 - k, :] * _shift_down(x, k, row_ids)
    return pre


def _conv_fwd(zx, conv_w, conv_b, *, name, hook=None):
    t = zx.shape[0]
    nct = SSD_CONV_DIM // LANES
    hk = _HookSlots(hook, n_in=3, n_out=1, n_scratch=0)

    def body(*refs):
        (x_ref, w_ref, b_ref), (o_ref,), _ = hk.own(refs)
        if hook is not None:
            hk.run(refs, pl.program_id(0), nct)
        x = x_ref[...].astype(F32)
        row_ids = lax.broadcasted_iota(jnp.int32, x.shape, 0)
        pre = _conv_pre(x, w_ref[...], b_ref[...], row_ids)
        o_ref[...] = pre * _sigmoid(pre)

    outs = pl.pallas_call(
        body, grid=(nct,),
        in_specs=[pl.BlockSpec((t, LANES), lambda j: (0, XBC_COL0 + j)),
                  pl.BlockSpec((SSD_CONV_WIDTH, LANES), lambda j: (0, j)),
                  pl.BlockSpec((1, LANES), lambda j: (0, j))] + hk.in_specs,
        out_specs=[pl.BlockSpec((t, LANES), lambda j: (0, j))] + hk.out_specs,
        out_shape=[jax.ShapeDtypeStruct((t, SSD_CONV_DIM), F32)] + hk.out_shape,
        scratch_shapes=hk.scratch,
        compiler_params=_params(*hk.semantics("parallel")), name=name)(zx, conv_w, conv_b, *hk.inputs)
    return outs[0] if hook is None else (outs[0], outs[1:])


def _conv_bwd(zx, conv_w, conv_b, d_xs, d_bm, d_cm, dzx, *, name):
    t = zx.shape[0]
    nct = SSD_CONV_DIM // LANES
    n_xs = SSD_D_INNER // LANES
    n_bm = SSD_N_GROUPS * SSD_D_STATE // LANES

    def body(x_ref, w_ref, b_ref, dxs_ref, dbm_ref, dcm_ref, _, dx_ref, dw_ref, db_ref):
        x = x_ref[...].astype(F32)
        w = w_ref[...]
        j = pl.program_id(0)
        dy = jnp.where(j < n_xs, dxs_ref[...], jnp.where(j < n_xs + n_bm, dbm_ref[...], dcm_ref[...]))
        row_ids = lax.broadcasted_iota(jnp.int32, x.shape, 0)
        pre = _conv_pre(x, w, b_ref[...], row_ids)
        sg = _sigmoid(pre)
        dpre = dy * (sg * (1.0 + pre * (1.0 - sg)))
        dx = w[3:4, :] * dpre
        for k in (1, 2, 3):
            dx = dx + w[3 - k:4 - k, :] * _shift_up(dpre, k, row_ids)
        dx_ref[...] = dx.astype(dx_ref.dtype)
        db_ref[...] = jnp.sum(dpre, axis=0, keepdims=True)
        dw_ref[3:4, :] = jnp.sum(dpre * x, axis=0, keepdims=True)
        for k in (1, 2, 3):
            dw_ref[3 - k:4 - k, :] = jnp.sum(dpre * _shift_down(x, k, row_ids), axis=0, keepdims=True)

    clip = lambda j, lo, n: jnp.clip(j - lo, 0, n - 1)
    return pl.pallas_call(
        body, grid=(nct,),
        in_specs=[pl.BlockSpec((t, LANES), lambda j: (0, XBC_COL0 + j)),
                  pl.BlockSpec((SSD_CONV_WIDTH, LANES), lambda j: (0, j)),
                  pl.BlockSpec((1, LANES), lambda j: (0, j)),
                  pl.BlockSpec((t, LANES), lambda j: (0, clip(j, 0, n_xs))),
                  pl.BlockSpec((t, LANES), lambda j: (0, clip(j, n_xs, n_bm))),
                  pl.BlockSpec((t, LANES), lambda j: (0, clip(j, n_xs + n_bm, n_bm))), ANY],
        out_specs=[pl.BlockSpec((t, LANES), lambda j: (0, XBC_COL0 + j)),
                   pl.BlockSpec((SSD_CONV_WIDTH, LANES), lambda j: (0, j)), pl.BlockSpec((1, LANES), lambda j: (0, j))],
        out_shape=[jax.ShapeDtypeStruct(dzx.shape, dzx.dtype),
                   jax.ShapeDtypeStruct((SSD_CONV_WIDTH, SSD_CONV_DIM), F32),
                   jax.ShapeDtypeStruct((1, SSD_CONV_DIM), F32)],
        input_output_aliases={6: 0},
        compiler_params=_params("parallel"), name=name)(zx, conv_w, conv_b, d_xs, d_bm, d_cm, dzx)


def _softplus_fwd(dt_raw, bias_row, alog_row, *, name):
    t = dt_raw.shape[0]
    q = SSD_CHUNK
    tr = _row_tile(t, 1024)

    def body(x_ref, b_ref, al_ref, dt_ref, cum_ref):
        v = x_ref[...] + b_ref[...]
        e = jnp.exp(-jnp.abs(v))
        u = 1.0 + e
        log1p = jnp.where(u == 1.0, e, jnp.log(u) * (e / (u - 1.0)))
        dt = jnp.maximum(v, 0.0) + log1p
        a = dt * -jnp.exp(al_ref[...])
        lower = (lax.broadcasted_iota(jnp.int32, (q, q), 1) <= lax.broadcasted_iota(jnp.int32, (q, q), 0)).astype(F32)
        cums = [lax.dot_general(lower, a[c * q:(c + 1) * q, :], ((((1,), (0,))), ((), ())), precision=lax.Precision.HIGHEST,
                                preferred_element_type=F32) for c in range(tr // q)]
        dt_t, cum_t = dt.T, jnp.concatenate(cums, axis=0).T
        for g in range(SSD_N_GROUPS):
            rows = slice(g * SSD_HPG, (g + 1) * SSD_HPG)
            dt_ref[g] = dt_t[rows, :]
            cum_ref[g] = cum_t[rows, :]

    vec = pl.BlockSpec((1, LANES), lambda i: (0, 0))
    by_group = pl.BlockSpec((SSD_N_GROUPS, SSD_HPG, tr), lambda i: (0, 0, i))
    return pl.pallas_call(
        body, grid=(t // tr,),
        in_specs=[pl.BlockSpec((tr, LANES), lambda i: (i, 0)), vec, vec],
        out_specs=[by_group, by_group],
        out_shape=[jax.ShapeDtypeStruct((SSD_N_GROUPS, SSD_HPG, t), F32)] * 2,
        compiler_params=_params("parallel"), name=name)(dt_raw, bias_row, alog_row)


def _softplus_bwd(dt_raw, bias_row, ddt_rows, dzx, *, name):
    t = dt_raw.shape[0]
    tr = _row_tile(t, 1024)
    tail = SSD_IN_PAD - SSD_DT_COL

    def body(x_ref, b_ref, g_ref, _, o_ref, db_ref):
        v = x_ref[...] + b_ref[...]
        lane = lax.broadcasted_iota(jnp.int32, v.shape, 1)
        by_head = jnp.concatenate([g_ref[g] for g in range(SSD_N_GROUPS)]
                                  + [jnp.zeros((LANES - SSD_N_HEADS, tr), F32)], axis=0)
        d = jnp.where(lane < SSD_N_HEADS, by_head.T * _sigmoid(v), 0.0)
        o_ref[:, pl.ds(0, LANES)] = d.astype(o_ref.dtype)
        o_ref[:, pl.ds(LANES, tail - LANES)] = jnp.zeros((tr, tail - LANES), o_ref.dtype)

        @pl.when(pl.program_id(0) == 0)
        def _():
            db_ref[...] = jnp.zeros_like(db_ref)

        db_ref[...] += jnp.sum(d, axis=0, keepdims=True)

    return pl.pallas_call(
        body, grid=(t // tr,),
        in_specs=[pl.BlockSpec((tr, LANES), lambda i: (i, 0)), pl.BlockSpec((1, LANES), lambda i: (0, 0)),
                  pl.BlockSpec((SSD_N_GROUPS, SSD_HPG, tr), lambda i: (0, 0, i)), ANY],
        out_specs=[pl.BlockSpec((tr, tail), lambda i: (i, SSD_DT_COL // tail)), pl.BlockSpec((1, LANES), lambda i: (0, 0))],
        out_shape=[jax.ShapeDtypeStruct(dzx.shape, dzx.dtype), jax.ShapeDtypeStruct((1, LANES), F32)],
        input_output_aliases={3: 0},
        compiler_params=_params("arbitrary"), name=name)(dt_raw, bias_row, ddt_rows, dzx)


def _ssd_masks():
    q = SSD_CHUNK
    tt = lax.broadcasted_iota(jnp.int32, (q, q), 0)
    ss = lax.broadcasted_iota(jnp.int32, (q, q), 1)
    lane = lax.broadcasted_iota(jnp.int32, (1, SSD_GW), 1)
    srow = lax.broadcasted_iota(jnp.int32, (SSD_GW, 1), 0)
    hm = [(lane >= SSD_HEAD_DIM * j) & (lane < SSD_HEAD_DIM * (j + 1)) for j in range(SSD_HPG)]
    rm = [(srow >= SSD_HEAD_DIM * j) & (srow < SSD_HEAD_DIM * (j + 1)) for j in range(SSD_HPG)]
    return tt, ss, hm, rm


def _ssd_head_terms(dt_rows, cum_rows, a_rows, j, tt, ss):
    q = SSD_CHUNK
    dt_row = dt_rows[j:j + 1, :]
    dt_col = jnp.sum(jnp.where(tt == ss, dt_row, 0.0), axis=1, keepdims=True)
    a_row1 = a_rows[j:j + 1, :]
    a_11 = a_rows[j:j + 1, 0:1]
    cum_col = jnp.sum(jnp.where(ss <= tt, dt_row * a_row1, 0.0), axis=1, keepdims=True)
    cum_row = cum_rows[j:j + 1, :]
    decay = jnp.exp(jnp.where(ss <= tt, cum_col - cum_row, -jnp.inf))
    cum_last = cum_col[q - 1:q, :]
    e_col = jnp.exp(cum_col)
    dte_col = jnp.exp(cum_last - cum_col)
    e_last = jnp.exp(cum_last)
    return dt_col, dt_row, a_row1, a_11, decay, e_col, dte_col, e_last


SSD_CHUNKS_PER_STEP = 8
SSD_BC_COL0 = SSD_D_INNER // SSD_D_STATE


def _ssd_head_selects(terms, hm, rm):
    e_all = jnp.zeros((SSD_CHUNK, SSD_GW), F32)
    w_all = jnp.zeros((SSD_CHUNK, SSD_GW), F32)
    e_s = jnp.zeros((SSD_GW, 1), F32)
    for j in range(SSD_HPG):
        dt_col, _, _, _, _, e_col, dte_col, e_last = terms[j]
        e_all = jnp.where(hm[j], e_col, e_all)
        w_all = jnp.where(hm[j], dt_col * dte_col, w_all)
        e_s = jnp.where(rm[j], e_last, e_s)
    return e_all, w_all, e_s


def _ssd_fwd(xc, dtr, cumr, alog_b, d_b, *, name, hook=None):
    t = xc.shape[0]
    q = SSD_CHUNK
    nc = t // q
    kc = min(SSD_CHUNKS_PER_STEP, nc)
    rows = kc * q
    hk = _HookSlots(hook, n_in=7, n_out=2, n_scratch=1)

    def body(*refs):
        (x_ref, b_ref, c_ref, dtr_ref, cumr_ref, alog_ref, d_ref), (y_ref, st_ref), (s_scr,) = hk.own(refs)
        if hook is not None:
            hk.run(refs, pl.program_id(0) * (nc // kc) + pl.program_id(1), SSD_N_GROUPS * (nc // kc))

        @pl.when(pl.program_id(1) == 0)
        def _():
            s_scr[...] = jnp.zeros_like(s_scr)

        tt, ss, hm, rm = _ssd_masks()
        a_rows = -jnp.exp(alog_ref[...])
        d_rows = d_ref[...]
        d_all = jnp.zeros((1, SSD_GW), F32)
        for j in range(SSD_HPG):
            d_all = jnp.where(hm[j], d_rows[j:j + 1, 0:1], d_all)
        ks, hs = range(kc), range(SSD_HPG)
        sl = [pl.ds(k * q, q) for k in ks]
        x = [x_ref[sl[k], :] for k in ks]
        bm = [b_ref[sl[k], :].astype(BF16) for k in ks]
        cm = [c_ref[sl[k], :].astype(BF16) for k in ks]
        xb = [x[k].astype(BF16) for k in ks]
        terms = [[_ssd_head_terms(dtr_ref[:, sl[k]], cumr_ref[:, sl[k]], a_rows, j, tt, ss) for j in hs] for k in ks]
        g = [_dot_nt(cm[k], bm[k]) for k in ks]
        m = [[(g[k] * terms[k][j][4] * terms[k][j][1]).astype(BF16) for j in hs] for k in ks]
        yj = [[_dot_nn(m[k][j], xb[k]) for j in hs] for k in ks]
        sel = [_ssd_head_selects(terms[k], hm, rm) for k in ks]
        upd = [_dot_tn((x[k] * sel[k][1]).astype(BF16), bm[k]) for k in ks]
        states = [s_scr[...]]
        for k in ks:
            states.append(states[k] * sel[k][2] + upd[k])
        inter = [_dot_nt(cm[k], states[k].astype(BF16)) for k in ks]
        ys = []
        for k in ks:
            y = jnp.zeros((q, SSD_GW), F32)
            for j in hs:
                y = jnp.where(hm[j], yj[k][j], y)
            ys.append(y + inter[k] * sel[k][0] + x[k] * d_all)
        for k in ks:
            st_ref[k] = states[k]
        y_ref[...] = jnp.concatenate(ys, axis=0)
        s_scr[...] = states[kc]

    blk = lambda width, off: pl.BlockSpec((rows, width), lambda g, c: (c, off + g))
    par_s = pl.BlockSpec((None, SSD_HPG, LANES), lambda g, c: (g, 0, 0))
    row_s = pl.BlockSpec((None, SSD_HPG, rows), lambda g, c: (g, 0, c))
    outs = pl.pallas_call(
        body, grid=(SSD_N_GROUPS, nc // kc),
        in_specs=[blk(SSD_GW, 0), blk(SSD_D_STATE, SSD_BC_COL0), blk(SSD_D_STATE, SSD_BC_COL0 + SSD_N_GROUPS),
                  row_s, row_s, par_s, par_s] + hk.in_specs,
        out_specs=[blk(SSD_GW, 0), pl.BlockSpec((None, kc, SSD_GW, SSD_D_STATE), lambda g, c: (g, c, 0, 0))] + hk.out_specs,
        out_shape=[jax.ShapeDtypeStruct((t, SSD_D_INNER), F32),
                   jax.ShapeDtypeStruct((SSD_N_GROUPS, nc, SSD_GW, SSD_D_STATE), F32)] + hk.out_shape,
        scratch_shapes=[pltpu.VMEM((SSD_GW, SSD_D_STATE), F32)] + hk.scratch,
        compiler_params=_params(*hk.semantics("parallel", "arbitrary")), name=name)(
            xc, xc, xc, dtr, cumr, alog_b, d_b, *hk.inputs)
    return outs if hook is None else (outs[:2], outs[2:])


def _ssd_bwd(xc, dtr, cumr, alog_b, d_b, states, dy, *, name, hook=None):
    t = xc.shape[0]
    q = SSD_CHUNK
    nc = t // q
    kc = min(SSD_CHUNKS_PER_STEP, nc)
    nst = nc // kc
    rows = kc * q
    rev = lambda c: nst - 1 - c
    hk = _HookSlots(hook, n_in=9, n_out=5, n_scratch=1)

    def body(*refs):
        ((x_ref, b_ref, c_ref, dtr_ref, cumr_ref, alog_ref, d_ref, st_ref, dy_ref),
         (dx_ref, db_ref, dc_ref, ddt_ref, dpar_ref), (ds_scr,)) = hk.own(refs)
        if hook is not None:
            hk.run(refs, pl.program_id(0) * nst + pl.program_id(1), SSD_N_GROUPS * nst)

        @pl.when(pl.program_id(1) == 0)
        def _():
            ds_scr[...] = jnp.zeros_like(ds_scr)
            dpar_ref[...] = jnp.zeros_like(dpar_ref)

        tt, ss, hm, rm = _ssd_masks()
        tcol = lax.broadcasted_iota(jnp.int32, (q, 1), 0)
        lane = lax.broadcasted_iota(jnp.int32, (1, LANES), 1)
        a_rows = -jnp.exp(alog_ref[...])
        d_rows = d_ref[...]
        d_all = jnp.zeros((1, SSD_GW), F32)
        for j in range(SSD_HPG):
            d_all = jnp.where(hm[j], d_rows[j:j + 1, 0:1], d_all)
        ks, hs = range(kc), range(SSD_HPG)
        sl = [pl.ds(k * q, q) for k in ks]
        x = [x_ref[sl[k], :] for k in ks]
        dyv = [dy_ref[sl[k], :] for k in ks]
        bm = [b_ref[sl[k], :].astype(BF16) for k in ks]
        cm = [c_ref[sl[k], :].astype(BF16) for k in ks]
        s_in = [st_ref[k] for k in ks]
        xb = [x[k].astype(BF16) for k in ks]
        dyb = [dyv[k].astype(BF16) for k in ks]
        s_b = [s_in[k].astype(BF16) for k in ks]
        terms = [[_ssd_head_terms(dtr_ref[:, sl[k]], cumr_ref[:, sl[k]], a_rows, j, tt, ss) for j in hs] for k in ks]
        sel = [_ssd_head_selects(terms[k], hm, rm) for k in ks]
        e_all, w_all, e_s = [s_[0] for s_ in sel], [s_[1] for s_ in sel], [s_[2] for s_ in sel]
        dye = [(dyv[k] * e_all[k]).astype(BF16) for k in ks]
        ds_loc = [_dot_tn(dye[k], cm[k]) for k in ks]
        ds = [None] * kc
        running = ds_scr[...]
        for k in reversed(ks):
            ds[k] = running
            running = running * e_s[k] + ds_loc[k]
        ds_scr[...] = running
        ds_b = [ds[k].astype(BF16) for k in ks]
        g = [_dot_nt(cm[k], bm[k]) for k in ks]
        cs = [_dot_nt(cm[k], s_b[k]) for k in ks]
        bds = [_dot_nt(bm[k], ds_b[k]) for k in ks]
        dm = [[_dot_nt(jnp.where(hm[j], dyv[k], 0.0).astype(BF16), xb[k]) for j in hs] for k in ks]
        gl = [[g[k] * terms[k][j][4] for j in hs] for k in ks]
        wp = [[dm[k][j] * gl[k][j] for j in hs] for k in ks]
        mt = [[(gl[k][j] * terms[k][j][1]).astype(BF16) for j in hs] for k in ks]
        dxj = [[_dot_tn(mt[k][j], dyb[k]) for j in hs] for k in ks]
        dg = []
        for k in ks:
            acc = jnp.zeros((q, q), F32)
            for j in hs:
                acc = acc + dm[k][j] * terms[k][j][4] * terms[k][j][1]
            dg.append(acc.astype(BF16))
        dy_cs = [dyv[k] * cs[k] for k in ks]
        x_bds = [x[k] * bds[k] for k in ks]
        dy_x = [dyv[k] * x[k] for k in ks]
        ds_s = [ds[k] * s_in[k] for k in ks]
        w = [[wp[k][j] * terms[k][j][1] for j in hs] for k in ks]
        rw_col = [[jnp.sum(w[k][j], axis=1, keepdims=True) for j in hs] for k in ks]
        cw_row = [[jnp.sum(w[k][j], axis=0, keepdims=True) for j in hs] for k in ks]
        cwp_row = [[jnp.sum(wp[k][j], axis=0, keepdims=True) for j in hs] for k in ks]
        r1_col = [[jnp.sum(jnp.where(hm[j], dy_cs[k], 0.0), axis=1, keepdims=True) * terms[k][j][5] for j in hs] for k in ks]
        dw_col = [[jnp.sum(jnp.where(hm[j], x_bds[k], 0.0), axis=1, keepdims=True) for j in hs] for k in ks]
        head_rows = [slice(j * SSD_HEAD_DIM, (j + 1) * SSD_HEAD_DIM) for j in hs]
        lane_sum = lambda v: jnp.sum(v, axis=1, keepdims=True)
        s_sum = [[lane_sum(jnp.sum(ds_s[k][head_rows[j], :], axis=0, keepdims=True)) for j in hs] for k in ks]
        dy_x_cols = [jnp.sum(dy_x[k], axis=0, keepdims=True) for k in ks]
        d_d = [[lane_sum(jnp.where(hm[j], dy_x_cols[k], 0.0)) for j in hs] for k in ks]
        ddt_rows = [[None] * SSD_HPG for _ in ks]
        dpar = [jnp.zeros((1, LANES), F32) for _ in hs]
        for k in ks:
            for j in hs:
                dt_col, dt_row, a_row1, a_11, _, _, dte_col, e_last = terms[k][j]
                dww = dw_col[k][j] * (dt_col * dte_col)
                last_add = jnp.sum(dww, axis=0, keepdims=True) + e_last * s_sum[k][j]
                dcum_col = rw_col[k][j] + r1_col[k][j] - dww + jnp.where(tcol == q - 1, last_add, 0.0)
                da_row = jnp.sum(jnp.where(tt >= ss, dcum_col, 0.0), axis=0, keepdims=True)
                da_col = jnp.sum(jnp.where(ss >= tt, -cw_row[k][j], 0.0), axis=1, keepdims=True)
                ddt_col = a_11 * da_col + dw_col[k][j] * dte_col
                ddt_rows[k][j] = (a_row1 * da_row + cwp_row[k][j]
                                  + jnp.sum(jnp.where(tt == ss, ddt_col, 0.0), axis=0, keepdims=True))
                d_a = jnp.sum(dt_row * da_row, axis=1, keepdims=True) + jnp.sum(dt_col * da_col, axis=0, keepdims=True)
                dpar[j] = dpar[j] + jnp.where(lane == 0, d_a * a_11, 0.0) + jnp.where(lane == 1, d_d[k][j], 0.0)
        dxs = []
        for k in ks:
            acc = jnp.zeros((q, SSD_GW), F32)
            for j in hs:
                acc = jnp.where(hm[j], dxj[k][j], acc)
            dxs.append(acc + w_all[k] * bds[k] + d_all * dyv[k])
        xw = [(x[k] * w_all[k]).astype(BF16) for k in ks]
        dc = [_dot_nn(dg[k], bm[k]) + _dot_nn(dye[k], s_b[k]) for k in ks]
        db = [_dot_tn(dg[k], cm[k]) + _dot_nn(xw[k], ds_b[k]) for k in ks]
        dx_ref[...] = jnp.concatenate(dxs, axis=0)
        dc_ref[...] = jnp.concatenate(dc, axis=0)
        db_ref[...] = jnp.concatenate(db, axis=0)
        ddt_ref[...] = jnp.concatenate([jnp.concatenate([ddt_rows[k][j] for k in ks], axis=1) for j in hs], axis=0)
        dpar_ref[...] += jnp.concatenate(dpar, axis=0)

    blk = lambda width, off: pl.BlockSpec((rows, width), lambda g, c: (rev(c), off + g))
    par_s = pl.BlockSpec((None, SSD_HPG, LANES), lambda g, c: (g, 0, 0))
    outs = pl.pallas_call(
        body, grid=(SSD_N_GROUPS, nst),
        in_specs=[blk(SSD_GW, 0), blk(SSD_D_STATE, SSD_BC_COL0), blk(SSD_D_STATE, SSD_BC_COL0 + SSD_N_GROUPS),
                  pl.BlockSpec((None, SSD_HPG, rows), lambda g, c: (g, 0, rev(c))),
                  pl.BlockSpec((None, SSD_HPG, rows), lambda g, c: (g, 0, rev(c))), par_s, par_s,
                  pl.BlockSpec((None, kc, SSD_GW, SSD_D_STATE), lambda g, c: (g, rev(c), 0, 0)), blk(SSD_GW, 0)] + hk.in_specs,
        out_specs=[blk(SSD_GW, 0), blk(SSD_D_STATE, 0), blk(SSD_D_STATE, 0),
                   pl.BlockSpec((None, SSD_HPG, rows), lambda g, c: (g, 0, rev(c))), par_s] + hk.out_specs,
        out_shape=[jax.ShapeDtypeStruct((t, SSD_D_INNER), F32),
                   jax.ShapeDtypeStruct((t, SSD_N_GROUPS * SSD_D_STATE), F32),
                   jax.ShapeDtypeStruct((t, SSD_N_GROUPS * SSD_D_STATE), F32),
                   jax.ShapeDtypeStruct((SSD_N_GROUPS, SSD_HPG, t), F32),
                   jax.ShapeDtypeStruct((SSD_N_GROUPS, SSD_HPG, LANES), F32)] + hk.out_shape,
        scratch_shapes=[pltpu.VMEM((SSD_GW, SSD_D_STATE), F32)] + hk.scratch,
        compiler_params=_params(*hk.semantics("parallel", "arbitrary")), name=name)(
            xc, xc, xc, dtr, cumr, alog_b, d_b, states, dy, *hk.inputs)
    return outs if hook is None else (outs[:5], outs[5:])


def _gate_norm_fwd(y, zx, norm_w, *, name):
    t = y.shape[0]
    tr = _row_tile(t, 256)
    row = pl.BlockSpec((tr, SSD_D_INNER), lambda i: (i, 0))

    def body(y_ref, z_ref, w_ref, o_ref):
        for gi in range(SSD_N_GROUPS):
            sl = pl.ds(gi * SSD_GW, SSD_GW)
            z = z_ref[:, sl].astype(F32)
            gv = y_ref[:, sl] * (z * _sigmoid(z))
            r = lax.rsqrt(jnp.mean(gv * gv, axis=-1, keepdims=True) + NORM_EPS)
            o_ref[:, sl] = (gv * r * w_ref[:, sl]).astype(BF16)

    return pl.pallas_call(
        body, grid=(t // tr,), in_specs=[row, row, pl.BlockSpec((1, SSD_D_INNER), lambda i: (0, 0))],
        out_specs=row, out_shape=jax.ShapeDtypeStruct((t, SSD_D_INNER), BF16),
        compiler_params=_params("parallel"), name=name)(y, zx, norm_w)


def _gate_norm_bwd(y, zx, norm_w, dyn, *, name):
    t = y.shape[0]
    tr = _row_tile(t, 256)
    row = pl.BlockSpec((tr, SSD_D_INNER), lambda i: (i, 0))
    vec = pl.BlockSpec((1, SSD_D_INNER), lambda i: (0, 0))

    def body(y_ref, z_ref, w_ref, dyn_ref, dy_ref, dz_ref, dw_ref):
        @pl.when(pl.program_id(0) == 0)
        def _():
            dw_ref[...] = jnp.zeros_like(dw_ref)

        for gi in range(SSD_N_GROUPS):
            sl = pl.ds(gi * SSD_GW, SSD_GW)
            z = z_ref[:, sl].astype(F32)
            yv = y_ref[:, sl]
            sg = _sigmoid(z)
            sz = z * sg
            gv = yv * sz
            r = lax.rsqrt(jnp.mean(gv * gv, axis=-1, keepdims=True) + NORM_EPS)
            ghat = gv * r
            dout = dyn_ref[:, sl].astype(F32)
            dgh = dout * w_ref[:, sl]
            dgv = r * (dgh - ghat * jnp.mean(dgh * ghat, axis=-1, keepdims=True))
            dy_ref[:, sl] = dgv * sz
            dz_ref[:, sl] = (dgv * yv * (sg * (1.0 + z * (1.0 - sg)))).astype(dz_ref.dtype)
            dw_ref[:, sl] += jnp.sum(dout * ghat, axis=0, keepdims=True)

    return pl.pallas_call(
        body, grid=(t // tr,), in_specs=[row, row, vec, row], out_specs=[row, row, vec],
        out_shape=[jax.ShapeDtypeStruct((t, SSD_D_INNER), F32), jax.ShapeDtypeStruct((t, SSD_IN_PAD), BF16),
                   jax.ShapeDtypeStruct((1, SSD_D_INNER), F32)],
        compiler_params=_params("arbitrary"), name=name)(y, zx, norm_w, dyn)


ATTN_KV_W = ATTN_N_KV * ATTN_HEAD_DIM
ATTN_Q_HALF = 512
ATTN_K_BLK = ATTN_N_Q * ATTN_HEAD_DIM // ATTN_KV_W
ATTN_V_BLK = ATTN_K_BLK + 1


def _attn_valid(first_block):
    w = ATTN_WINDOW
    qpos = lax.broadcasted_iota(jnp.int32, (w, 2 * w), 0) + w
    kpos = lax.broadcasted_iota(jnp.int32, (w, 2 * w), 1)
    rel = qpos - kpos
    return (rel >= 0) & (rel < w) & jnp.logical_not(first_block & (kpos < w))


def _attn_head_views(lo_ref, hi_ref):
    hd = ATTN_HEAD_DIM
    per_half = ATTN_Q_HALF // hd
    return [(lo_ref if h < per_half else hi_ref)[:, pl.ds((h % per_half) * hd, hd)] for h in range(ATTN_N_Q)]


def _attn_block_views(lo_ref, hi_ref, kc_ref, kp_ref, vc_ref, vp_ref):
    hd = ATTN_HEAD_DIM
    kv_cols = [pl.ds(kh * hd, hd) for kh in range(ATTN_N_KV)]
    kb = [jnp.concatenate([kp_ref[:, c], kc_ref[:, c]], axis=0) for c in kv_cols]
    vb = [jnp.concatenate([vp_ref[:, c], vc_ref[:, c]], axis=0) for c in kv_cols]
    return _attn_head_views(lo_ref, hi_ref), kb, vb


def _attn_scores(q, kb, valid):
    scale = ATTN_HEAD_DIM ** -0.5
    return [jnp.where(valid, _dot_nt(q[h], kb[h // ATTN_REP]) * scale, -jnp.inf) for h in range(ATTN_N_Q)]


def _attn_softmax(s, sink):
    heads = range(ATTN_N_Q)
    m = [jnp.maximum(jnp.max(s[h], axis=1, keepdims=True), sink[h]) for h in heads]
    e = [jnp.exp(s[h] - m[h]) for h in heads]
    es = [jnp.exp(sink[h] - m[h]) for h in heads]
    inv = [1.0 / (jnp.sum(e[h], axis=1, keepdims=True) + es[h]) for h in heads]
    return e, es, inv


def _attn_fwd(qkv, sinks_b, *, name, hook=None):
    t = qkv.shape[0]
    w = ATTN_WINDOW
    nb = t // w
    prev = lambda n: jnp.maximum(n - 1, 0)
    hk = _HookSlots(hook, n_in=7, n_out=1, n_scratch=0)

    def body(*refs):
        (qlo_ref, qhi_ref, kc_ref, kp_ref, vc_ref, vp_ref, sink_ref), (o_ref,), _ = hk.own(refs)
        if hook is not None:
            hk.run(refs, pl.program_id(0), nb)
        heads = range(ATTN_N_Q)
        q, kb, vb = _attn_block_views(qlo_ref, qhi_ref, kc_ref, kp_ref, vc_ref, vp_ref)
        sink = [sink_ref[h:h + 1, 0:1] for h in heads]
        e, _, inv = _attn_softmax(_attn_scores(q, kb, _attn_valid(pl.program_id(0) == 0)), sink)
        out = [_dot_nn((e[h] * inv[h]).astype(BF16), vb[h // ATTN_REP]).astype(o_ref.dtype) for h in heads]
        o_ref[...] = jnp.concatenate(out, axis=1)

    qh = lambda half: pl.BlockSpec((w, ATTN_Q_HALF), lambda n: (n, half))
    kv = lambda blk, idx: pl.BlockSpec((w, ATTN_KV_W), lambda n: (idx(n), blk))
    cur = lambda n: n
    outs = pl.pallas_call(
        body, grid=(nb,),
        in_specs=[qh(0), qh(1), kv(ATTN_K_BLK, cur), kv(ATTN_K_BLK, prev), kv(ATTN_V_BLK, cur), kv(ATTN_V_BLK, prev),
                  pl.BlockSpec((ATTN_N_Q, LANES), lambda n: (0, 0))] + hk.in_specs,
        out_specs=[pl.BlockSpec((w, D_MODEL), lambda n: (n, 0))] + hk.out_specs,
        out_shape=[jax.ShapeDtypeStruct((t, D_MODEL), BF16)] + hk.out_shape,
        scratch_shapes=hk.scratch,
        compiler_params=_params(*hk.semantics("parallel")), name=name)(qkv, qkv, qkv, qkv, qkv, qkv, sinks_b, *hk.inputs)
    return outs[0] if hook is None else (outs[0], outs[1:])


def _attn_bwd(qkv, sinks_b, dout, *, name):
    t = qkv.shape[0]
    w = ATTN_WINDOW
    nb = t // w
    hd = ATTN_HEAD_DIM
    clamp = lambda n: jnp.minimum(n, nb - 1)
    prev = lambda n: jnp.maximum(clamp(n) - 1, 0)

    def body(qlo_ref, qhi_ref, kc_ref, kp_ref, vc_ref, vp_ref, sink_ref, dolo_ref, dohi_ref,
             dq_ref, dkv_ref, dsink_ref, carry):
        n = pl.program_id(0)

        @pl.when(n == 0)
        def _():
            carry[...] = jnp.zeros_like(carry)
            dsink_ref[...] = jnp.zeros_like(dsink_ref)

        @pl.when(n < nb)
        def _():
            heads, kvs = range(ATTN_N_Q), range(ATTN_N_KV)
            q, kb, vb = _attn_block_views(qlo_ref, qhi_ref, kc_ref, kp_ref, vc_ref, vp_ref)
            do = _attn_head_views(dolo_ref, dohi_ref)
            sink = [sink_ref[h:h + 1, 0:1] for h in heads]
            s = _attn_scores(q, kb, _attn_valid(n == 0))
            dp = [_dot_nt(do[h], vb[h // ATTN_REP]) for h in heads]
            e, es, inv = _attn_softmax(s, sink)
            p = [e[h] * inv[h] for h in heads]
            delta = [jnp.sum(p[h] * dp[h], axis=1, keepdims=True) for h in heads]
            dsc = [(p[h] * (dp[h] - delta[h]) * (hd ** -0.5)).astype(BF16) for h in heads]
            pb = [p[h].astype(BF16) for h in heads]
            dq = [_dot_nn(dsc[h], kb[h // ATTN_REP]).astype(dq_ref.dtype) for h in heads]
            stack = lambda per_head, kh: jnp.concatenate(per_head[kh * ATTN_REP:(kh + 1) * ATTN_REP], axis=0)
            dkb = [_dot_tn(stack(dsc, kh), stack(q, kh)) for kh in kvs]
            dvb = [_dot_tn(stack(pb, kh), stack(do, kh)) for kh in kvs]
            dsink = [jnp.broadcast_to(jnp.sum(-es[h] * inv[h] * delta[h], axis=0, keepdims=True), (1, LANES)) for h in heads]
            dq_ref[...] = jnp.concatenate(dq, axis=1)
            dsink_ref[...] += jnp.concatenate(dsink, axis=0)
            dkv_ref[...] = (carry[...] + jnp.concatenate([d[0:w, :] for d in dkb + dvb], axis=1)).astype(dkv_ref.dtype)
            carry[...] = jnp.concatenate([d[w:2 * w, :] for d in dkb + dvb], axis=1)

        @pl.when(n == nb)
        def _():
            dkv_ref[...] = carry[...].astype(dkv_ref.dtype)

    qh = lambda half: pl.BlockSpec((w, ATTN_Q_HALF), lambda n: (clamp(n), half))
    kv = lambda blk, idx: pl.BlockSpec((w, ATTN_KV_W), lambda n: (idx(n), blk))
    return pl.pallas_call(
        body, grid=(nb + 1,),
        in_specs=[qh(0), qh(1), kv(ATTN_K_BLK, clamp), kv(ATTN_K_BLK, prev), kv(ATTN_V_BLK, clamp), kv(ATTN_V_BLK, prev),
                  pl.BlockSpec((ATTN_N_Q, LANES), lambda n: (0, 0)), qh(0), qh(1)],
        out_specs=[pl.BlockSpec((w, D_MODEL), lambda n: (clamp(n), 0)),
                   pl.BlockSpec((w, 2 * ATTN_KV_W), lambda n: (jnp.maximum(n - 1, 0), 0)),
                   pl.BlockSpec((ATTN_N_Q, LANES), lambda n: (0, 0))],
        out_shape=[jax.ShapeDtypeStruct((t, D_MODEL), BF16), jax.ShapeDtypeStruct((t, 2 * ATTN_KV_W), BF16),
                   jax.ShapeDtypeStruct((ATTN_N_Q, LANES), F32)],
        scratch_shapes=[pltpu.VMEM((w, 2 * ATTN_KV_W), F32)],
        compiler_params=_params("arbitrary"), name=name)(qkv, qkv, qkv, qkv, qkv, qkv, sinks_b, dout, dout)


def _sq_relu_epilogue(acc):
    r = jnp.maximum(acc, 0.0)
    return (r * r,)


def _sq_relu_bwd_epilogue(acc, act):
    return (acc * (2.0 * jnp.sqrt(act.astype(F32))),)


def _bias_epilogue(acc, bias):
    return (acc + bias,)


def _plain_run(stage, fn, *args, **kwargs):
    return fn(*args, **kwargs)


def _mlp_fwd(u, w_up, w_down, tag, run=_plain_run):
    act = run(f"mlp_up_{tag}", _matmul, u, w_up, mode="nn", out_dtypes=(BF16,), epilogue=_sq_relu_epilogue, b_shards=True,
              tm=BIG_TILE, name=f"mlp_up_{tag}")
    f = run(f"mlp_down_{tag}", _matmul, act, w_down, mode="nn", out_dtypes=(F32,), tk=BIG_TILE, name=f"mlp_down_{tag}")
    return act, f


def _mlp_bwd(u, act, w_up, w_down, df, tag):
    dpre = _matmul(df, w_down, mode="nt", out_dtypes=(BF16,), epilogue=_sq_relu_bwd_epilogue,
                   extras=((act, "tile"),), name=f"mlp_dact_{tag}")
    dw_down = _matmul(act, df, mode="tn", out_dtypes=(BF16,), tk=BIG_TILE, name=f"mlp_dwdown_{tag}")
    du = _matmul(dpre, w_up, mode="nt", out_dtypes=(F32,), b_shards=True, tm=BIG_TILE, name=f"mlp_du_{tag}")
    dw_up = _matmul(u, dpre, mode="tn", out_dtypes=(BF16,), out_shards=True, tk=BIG_TILE, name=f"mlp_dwup_{tag}")
    return du, dw_up, dw_down


def _head_param_rows(p):
    return jnp.broadcast_to(p.reshape(SSD_N_GROUPS, SSD_HPG, 1), (SSD_N_GROUPS, SSD_HPG, LANES))


def _local_step(x, target, wts, comm=None, u0=None):
    t = x.shape[0]
    wts = dict(wts)
    row = lambda v: v.reshape(1, -1)
    mix_pre, mix_post, ffn_pre, ffn_post = wts["mix_pre_norm"], wts["mix_post_norm"], wts["ffn_pre_norm"], wts["ffn_post_norm"]

    def gathering(stage, fn, *args, **kwargs):
        hook = comm.gather_hook(stage) if comm is not None else None
        if hook is None:
            return fn(*args, **kwargs)
        out, got = fn(*args, hook=hook, **kwargs)
        wts.update(comm.weights_from(stage, got))
        return out

    if u0 is None:
        u0 = _rms_fwd(x, row(mix_pre[0]), name="rms_pre_mix0")
    zx, dt_raw = gathering("in_proj", _matmul, u0, wts["ssd_w_in"], mode="nn", out_dtypes=(BF16,), tn=SSD_IN_TILE,
                           f32_block=SSD_DT_COL - (SSD_IN_PAD - SSD_IN_TILE),
                           name="ssd_in_proj")
    xc = gathering("conv", _conv_fwd, zx, wts["ssd_conv_w"], row(wts["ssd_conv_b"]), name="ssd_conv_fwd")
    bias_row = jnp.pad(wts["ssd_dt_bias"], (0, LANES - SSD_N_HEADS)).reshape(1, LANES)
    alog_row = jnp.pad(wts["ssd_a_log"], (0, LANES - SSD_N_HEADS)).reshape(1, LANES)
    dtr, cumr = _softplus_fwd(dt_raw, bias_row, alog_row, name="ssd_dt_fwd")
    alog_b, d_b = _head_param_rows(wts["ssd_a_log"]), _head_param_rows(wts["ssd_d"])
    y_ssd, states = gathering("scan", _ssd_fwd, xc, dtr, cumr, alog_b, d_b, name="ssd_scan_fwd")
    norm_w = row(wts["ssd_norm_w"])
    yn = _gate_norm_fwd(y_ssd, zx, norm_w, name="ssd_gate_norm_fwd")
    mix0 = _matmul(yn, wts["ssd_w_out"], mode="nn", out_dtypes=(F32,), tk=BIG_TILE, name="ssd_out_proj")
    h1, v0 = _rms_fwd(mix0, row(mix_post[0]), resid=x, want_u=row(ffn_pre[0]), name="rms_post_mix0")
    act0, f0 = _mlp_fwd(v0, wts["mlp_w_up0"], wts["mlp_w_down0"], "l0", run=gathering)
    h2, u1 = _rms_fwd(f0, row(ffn_post[0]), resid=h1, want_u=row(mix_pre[1]), name="rms_post_ffn0")

    qkv = _matmul(u1, wts["attn_w_qkv"], mode="nn", out_dtypes=(BF16,), epilogue=_bias_epilogue,
                  extras=((row(wts["attn_b_qkv"]), "row"),), b_shards=True, name="attn_qkv_proj")
    sinks_b = jnp.broadcast_to(wts["attn_sinks"].reshape(ATTN_N_Q, 1), (ATTN_N_Q, LANES))
    ao = gathering("attn_fwd", _attn_fwd, qkv, sinks_b, name="attn_fwd")
    mix1 = _matmul(ao, wts["attn_w_o"], mode="nn", out_dtypes=(F32,), epilogue=_bias_epilogue,
                   extras=((row(wts["attn_b_o"]), "row"),), name="attn_out_proj")
    h3, v1 = _rms_fwd(mix1, row(mix_post[1]), resid=h2, want_u=row(ffn_pre[1]), name="rms_post_mix1")
    act1, f1 = _mlp_fwd(v1, wts["mlp_w_up1"], wts["mlp_w_down1"], "l1")
    dh4, loss_tile = _rms_fwd(f1, row(ffn_post[1]), resid=h3, target=target, name="rms_post_ffn1_loss")

    df1, g_ffn_post1 = _rms_bwd(f1, row(ffn_post[1]), dh4, out_dtype=BF16, name="rms_post_ffn1_bwd")
    dv1, g_up1, g_down1 = _mlp_bwd(v1, act1, wts["mlp_w_up1"], wts["mlp_w_down1"], df1, "l1")
    dh3, g_ffn_pre1 = _rms_bwd(h3, row(ffn_pre[1]), dv1, resid=dh4, name="rms_pre_ffn1_bwd")
    dmix1, g_mix_post1, g_b_o = _rms_bwd(mix1, row(mix_post[1]), dh3, out_dtype=BF16, dx_col_sum=True, name="rms_post_mix1_bwd")
    g_w_o = _matmul(ao, dmix1, mode="tn", out_dtypes=(BF16,), tk=BIG_TILE, name="attn_dwo")
    dao = _matmul(dmix1, wts["attn_w_o"], mode="nt", out_dtypes=(BF16,), name="attn_dao")
    dq, dkv, g_sinks = _attn_bwd(qkv, sinks_b, dao, name="attn_bwd")
    dqkv = jnp.concatenate([dq, dkv], axis=1)
    g_b_qkv = _col_sum(dqkv, name="attn_bqkv_grad")
    g_w_qkv = _matmul(u1, dqkv, mode="tn", out_dtypes=(BF16,), tn=ATTN_QKV // N_CHIPS, out_shards=True, tk=BIG_TILE, name="attn_dwqkv")
    du1 = _matmul(dqkv, wts["attn_w_qkv"], mode="nt", out_dtypes=(F32,), b_shards=True, name="attn_du")
    dh2, g_mix_pre1 = _rms_bwd(h2, row(mix_pre[1]), du1, resid=dh3, name="rms_pre_mix1_bwd")

    df0, g_ffn_post0 = _rms_bwd(f0, row(ffn_post[0]), dh2, out_dtype=BF16, name="rms_post_ffn0_bwd")
    dv0, g_up0, g_down0 = _mlp_bwd(v0, act0, wts["mlp_w_up0"], wts["mlp_w_down0"], df0, "l0")
    dh1, g_ffn_pre0 = _rms_bwd(h1, row(ffn_pre[0]), dv0, resid=dh2, name="rms_pre_ffn0_bwd")
    dmix0, g_mix_post0 = _rms_bwd(mix0, row(mix_post[0]), dh1, out_dtype=BF16, name="rms_post_mix0_bwd")
    g_w_out = _matmul(yn, dmix0, mode="tn", out_dtypes=(BF16,), tk=BIG_TILE, name="ssd_dwout")
    dyn = _matmul(dmix0, wts["ssd_w_out"], mode="nt", out_dtypes=(BF16,), name="ssd_dyn")
    dy_ssd, dzx, g_norm_w = _gate_norm_bwd(y_ssd, zx, norm_w, dyn, name="ssd_gate_norm_bwd")
    mats = {"ssd_w_out": g_w_out, "attn_w_qkv": g_w_qkv, "attn_w_o": g_w_o,
            "mlp_w_up0": g_up0, "mlp_w_up1": g_up1, "mlp_w_down0": g_down0, "mlp_w_down1": g_down1}
    if comm is None:
        dxc, dbm, dcm, ddt_r, dpar = _ssd_bwd(xc, dtr, cumr, alog_b, d_b, states, dy_ssd, name="ssd_scan_bwd")
    else:
        (dxc, dbm, dcm, ddt_r, dpar), received = _ssd_bwd(xc, dtr, cumr, alog_b, d_b, states, dy_ssd,
                                                          name="ssd_scan_bwd", hook=comm.exchange_hook(mats, "early"))
        comm.received(received)
    dzx, g_conv_w, g_conv_b = _conv_bwd(zx, wts["ssd_conv_w"], row(wts["ssd_conv_b"]), dxc, dbm, dcm, dzx, name="ssd_conv_bwd")
    dzx, g_dt_bias = _softplus_bwd(dt_raw, bias_row, ddt_r, dzx, name="ssd_dt_bwd")
    g_w_in = _w_in_to_shards(_matmul(u0, dzx, mode="tn", out_dtypes=(BF16,), tn=SSD_IN_TILE, tk=BIG_TILE, name="ssd_dwin"), name="ssd_dwin_shards")
    mats["ssd_w_in"] = g_w_in
    if comm is None:
        du0 = _matmul(dzx, wts["ssd_w_in"], mode="nt", out_dtypes=(F32,), tk=SSD_IN_TILE, name="ssd_du")
    else:
        du0, received = _matmul(dzx, wts["ssd_w_in"], mode="nt", out_dtypes=(F32,), tk=SSD_IN_TILE, name="ssd_du",
                                hook=comm.exchange_hook(mats, "late"))
        comm.received(received)
    grad_x, g_mix_pre0 = _rms_bwd(x, row(mix_pre[0]), du0, resid=dh1, name="rms_pre_mix0_bwd")

    dpar = dpar.reshape(SSD_N_HEADS, LANES)
    vecs = {
        "ssd_conv_w": g_conv_w, "ssd_conv_b": g_conv_b.reshape(-1),
        "ssd_dt_bias": g_dt_bias[0, :SSD_N_HEADS], "ssd_a_log": dpar[:, 0], "ssd_d": dpar[:, 1],
        "ssd_norm_w": g_norm_w.reshape(-1), "attn_b_qkv": g_b_qkv.reshape(-1), "attn_sinks": g_sinks[:, 0],
        "attn_b_o": g_b_o.reshape(-1),
        "mix_pre_norm": jnp.concatenate([g_mix_pre0, g_mix_pre1]), "mix_post_norm": jnp.concatenate([g_mix_post0, g_mix_post1]),
        "ffn_pre_norm": jnp.concatenate([g_ffn_pre0, g_ffn_pre1]), "ffn_post_norm": jnp.concatenate([g_ffn_post0, g_ffn_post1]),
    }
    return loss_tile, grad_x, mats, vecs


def _mesh_position():
    return lax.axis_index("x"), lax.axis_index("y"), lax.axis_index("c")


def _flip(v, bit):
    return 1 - v if bit else v


OTHER_CHIPS = ((1, 0), (0, 1), (1, 1))


def _comm_params():
    return pltpu.CompilerParams(vmem_limit_bytes=VMEM_LIMIT)


def _staged_copies(srcs, dsts, bufs, sems_in, sems_out):
    loads = [pltpu.make_async_copy(s, b, sems_in.at[i]) for i, (s, b) in enumerate(zip(srcs, bufs))]
    stores = [pltpu.make_async_copy(b, d, sems_out.at[i]) for i, (b, d) in enumerate(zip(bufs, dsts))]
    return loads, stores


class _GatherHook:
    def __init__(self, mats, vecs=()):
        self.arrs = list(mats) + list(vecs)
        self.nm, self.n = len(mats), len(self.arrs)
        n_ici, n_fwd = (N_CHIPS - 1) * self.n, max((N_CHIPS - 1) * self.nm, 1)
        dma = pltpu.SemaphoreType.DMA
        self.out_shape = [jax.ShapeDtypeStruct((N_CHIPS,) + a.shape, a.dtype) for a in self.arrs]
        self.scratch = [pltpu.VMEM(a.shape, a.dtype) for a in self.arrs] + [
            dma((n_ici,)), dma((n_ici,)), dma((n_fwd,)), dma((n_fwd,)), dma((self.n,)), dma((self.n,))]

    def plan(self, ins, outs, scratch):
        n, nm = self.n, self.nm
        bufs = scratch[:n]
        ici_send, ici_recv, fwd_send, fwd_recv, load_sems, store_sems = scratch[n:]
        xi, yi, ci = _mesh_position()
        me = 2 * xi + yi
        loads, stores = _staged_copies(ins, [outs[i].at[me] for i in range(n)], bufs, load_sems, store_sems)
        sends, landed, forwards, from_sibling = [], [], [], []
        for j, (bx, by) in enumerate(OTHER_CHIPS):
            px, py = _flip(xi, bx), _flip(yi, by)
            peer = 2 * px + py
            for i in range(n):
                k = j * n + i
                mk = functools.partial(pltpu.make_async_remote_copy, send_sem=ici_send.at[k], recv_sem=ici_recv.at[k],
                                       device_id=(px, py, ci), device_id_type=MESH)
                if i < nm:
                    sends.append(mk(src_ref=ins[i].at[ci], dst_ref=outs[i].at[me, ci]))
                    landed.append(mk(src_ref=ins[i].at[ci], dst_ref=outs[i].at[peer, ci]))
                    kf = j * nm + i
                    fw = functools.partial(pltpu.make_async_remote_copy, send_sem=fwd_send.at[kf], recv_sem=fwd_recv.at[kf],
                                           device_id=(xi, yi, 1 - ci), device_id_type=MESH)
                    forwards.append(fw(src_ref=outs[i].at[peer, ci], dst_ref=outs[i].at[peer, ci]))
                    from_sibling.append(fw(src_ref=outs[i].at[peer, ci], dst_ref=outs[i].at[peer, 1 - ci]))
                else:
                    sends.append(mk(src_ref=ins[i], dst_ref=outs[i].at[me]))
                    landed.append(mk(src_ref=ins[i], dst_ref=outs[i].at[peer]))
                    forwards.append(None)
        return loads, stores, sends, landed, forwards, from_sibling

    @staticmethod
    def start(p):
        loads, _, sends, _, _, _ = p
        for cp in loads + sends:
            cp.start()

    @staticmethod
    def relay(p):
        loads, stores, _, landed, forwards, _ = p
        for ld, st in zip(loads, stores):
            ld.wait()
            st.start()
        for cp, fw in zip(landed, forwards):
            cp.wait_recv()
            if fw is not None:
                fw.start()

    @staticmethod
    def finish(p):
        _, stores, sends, _, forwards, from_sibling = p
        for cp in from_sibling:
            cp.wait_recv()
        for cp in sends + [fw for fw in forwards if fw is not None]:
            cp.wait_send()
        for st in stores:
            st.wait()


def _run_hook(hook, ins, outs, scratch, step, n_steps):
    p = hook.plan(ins, outs, scratch)
    relay_step = min(max(1, (3 * n_steps) // 4), n_steps - 1)

    @pl.when(step == 0)
    def _():
        hook.start(p)

    if relay_step < n_steps - 1:
        @pl.when(step == relay_step)
        def _():
            hook.relay(p)

    @pl.when(step == n_steps - 1)
    def _():
        if relay_step == n_steps - 1:
            hook.relay(p)
        hook.finish(p)


def _hook_call(hook, *, name):
    n = len(hook.arrs)

    def body(*refs):
        p = hook.plan(refs[:n], refs[n:n + len(hook.out_shape)], refs[n + len(hook.out_shape):])
        hook.start(p)
        hook.relay(p)
        hook.finish(p)

    return pl.pallas_call(
        body, in_specs=[ANY] * n, out_specs=[ANY] * len(hook.out_shape), out_shape=hook.out_shape,
        scratch_shapes=hook.scratch, compiler_params=_comm_params(), name=name)(*hook.arrs)


def _send_other_half(parts, *, name):
    n = len(parts)

    def body(*refs):
        ins, outs = refs[:n], refs[n:2 * n]
        send_sems, recv_sems = refs[2 * n:]
        xi, yi, ci = _mesh_position()
        sibling = (xi, yi, 1 - ci)
        for i in range(n):
            for s in range(N_CHIPS):
                pltpu.make_async_remote_copy(src_ref=ins[i].at[s, 1 - ci], dst_ref=outs[i].at[s], send_sem=send_sems.at[i],
                                             recv_sem=recv_sems.at[i], device_id=sibling, device_id_type=MESH).start()
        for i in range(n):
            pltpu.make_async_remote_copy(src_ref=outs[i], dst_ref=outs[i], send_sem=send_sems.at[i], recv_sem=recv_sems.at[i],
                                         device_id=sibling, device_id_type=MESH).wait()

    return pl.pallas_call(
        body, in_specs=[ANY] * n, out_specs=[ANY] * n,
        out_shape=[jax.ShapeDtypeStruct((p.shape[0],) + p.shape[2:], p.dtype) for p in parts],
        scratch_shapes=[pltpu.SemaphoreType.DMA((n,)), pltpu.SemaphoreType.DMA((n,))],
        name=name)(*parts)


ROW_BLOCKS = 8


def _add_sibling_half(parts, theirs, core, *, name):
    n = len(parts)

    def body(core_ref, *refs):
        for a_ref, b_ref, o_ref in zip(refs[:n], refs[n:2 * n], refs[2 * n:]):
            o_ref[...] = (a_ref[...].astype(F32) + b_ref[...].astype(F32)).astype(o_ref.dtype)

    mine = lambda p: pl.BlockSpec((None, None, p.shape[2] // ROW_BLOCKS, p.shape[3]), lambda s, rb, core_ref: (s, core_ref[0], rb, 0))
    other = lambda p: pl.BlockSpec((None, p.shape[1] // ROW_BLOCKS, p.shape[2]), lambda s, rb, core_ref: (s, rb, 0))
    return pl.pallas_call(
        body,
        grid_spec=pltpu.PrefetchScalarGridSpec(
            num_scalar_prefetch=1, grid=(N_CHIPS, ROW_BLOCKS),
            in_specs=[mine(p) for p in parts] + [other(q) for q in theirs], out_specs=[other(q) for q in theirs]),
        out_shape=[jax.ShapeDtypeStruct(q.shape, BF16) for q in theirs],
        compiler_params=_params("parallel", "parallel"), name=name)(core, *parts, *theirs)


class _ExchangeHook:
    def __init__(self, parts, to_all=()):
        self.arrs = list(parts) + list(to_all)
        self.n_parts, self.n = len(parts), len(self.arrs)
        n_ici, n_peer = max((N_CHIPS - 1) * self.n_parts, 1), (N_DEV - 1) * max(len(to_all), 1)
        dma = pltpu.SemaphoreType.DMA
        self.out_shape = [jax.ShapeDtypeStruct(p.shape, p.dtype) for p in parts] + [
            jax.ShapeDtypeStruct((N_DEV,) + a.shape, a.dtype) for a in to_all]
        self.scratch = [pltpu.VMEM(p.shape[1:], p.dtype) for p in parts] + [pltpu.VMEM(a.shape, a.dtype) for a in to_all] + [
            dma((n_ici,)), dma((n_ici,)), dma((n_peer,)), dma((n_peer,)), dma((self.n,)), dma((self.n,))]

    def plan(self, ins, outs, scratch):
        n, npt = self.n, self.n_parts
        bufs = scratch[:n]
        send_sems, recv_sems, all_send, all_recv, load_sems, store_sems = scratch[n:]
        xi, yi, ci = _mesh_position()
        me_chip = 2 * xi + yi
        me = 4 * xi + 2 * yi + ci
        loads, stores = _staged_copies([ins[i].at[me_chip] for i in range(npt)] + list(ins[npt:]),
                                       [outs[i].at[me_chip] for i in range(npt)] + [outs[i].at[me] for i in range(npt, n)],
                                       bufs, load_sems, store_sems)
        sends, recvs = [], []
        for j, (bx, by) in enumerate(OTHER_CHIPS):
            px, py = _flip(xi, bx), _flip(yi, by)
            peer = 2 * px + py
            for i in range(npt):
                k = j * npt + i
                mk = functools.partial(pltpu.make_async_remote_copy, src_ref=ins[i].at[peer], send_sem=send_sems.at[k],
                                       recv_sem=recv_sems.at[k], device_id=(px, py, ci), device_id_type=MESH)
                sends.append(mk(dst_ref=outs[i].at[me_chip]))
                recvs.append(mk(dst_ref=outs[i].at[peer]))
        for i in range(npt, n):
            for k in range(1, N_DEV):
                px, py, pc = _flip(xi, (k >> 2) & 1), _flip(yi, (k >> 1) & 1), _flip(ci, k & 1)
                slot = (i - npt) * (N_DEV - 1) + k - 1
                mk = functools.partial(pltpu.make_async_remote_copy, src_ref=ins[i], send_sem=all_send.at[slot],
                                       recv_sem=all_recv.at[slot], device_id=(px, py, pc), device_id_type=MESH)
                sends.append(mk(dst_ref=outs[i].at[me]))
                recvs.append(mk(dst_ref=outs[i].at[4 * px + 2 * py + pc]))
        return loads, stores, sends, recvs

    @staticmethod
    def start(p):
        loads, _, sends, _ = p
        for cp in loads + sends:
            cp.start()

    @staticmethod
    def relay(p):
        loads, stores, _, _ = p
        for ld, st in zip(loads, stores):
            ld.wait()
            st.start()

    @staticmethod
    def finish(p):
        _, stores, sends, recvs = p
        for cp in recvs:
            cp.wait_recv()
        for cp in sends:
            cp.wait_send()
        for st in stores:
            st.wait()


def _sum_chips(parts, *, name):
    n = len(parts)
    p = parts[0].shape[0]

    def body(*refs):
        s = pl.program_id(1)
        for x_ref, o_ref in zip(refs[:n], refs[n:]):
            @pl.when(s == 0)
            def _():
                o_ref[...] = x_ref[...].astype(F32)

            @pl.when(s > 0)
            def _():
                o_ref[...] += x_ref[...].astype(F32)

    blocks = lambda q: ROW_BLOCKS if q.shape[1] % (8 * ROW_BLOCKS) == 0 else 1
    assert len({blocks(q) for q in parts}) == 1
    nb = blocks(parts[0])
    return pl.pallas_call(
        body, grid=(nb, p),
        in_specs=[pl.BlockSpec((None, q.shape[1] // nb, q.shape[2]), lambda rb, s: (s, rb, 0)) for q in parts],
        out_specs=[pl.BlockSpec((q.shape[1] // nb, q.shape[2]), lambda rb, s: (rb, 0)) for q in parts],
        out_shape=[jax.ShapeDtypeStruct(q.shape[1:], F32) for q in parts],
        compiler_params=_params("parallel", "arbitrary"), name=name)(*parts)


def _swap_halves(halves, layers, *, name, hook=None):
    n = len(halves)
    out_shapes, slots = [], []
    for i, h in enumerate(halves):
        pair = [p for p in layers if i in p]
        if pair and pair[0][1] == i:
            slots.append((slots[pair[0][0]][0], 1))
        elif pair:
            out_shapes.append(jax.ShapeDtypeStruct((2, 2) + h.shape, h.dtype))
            slots.append((len(out_shapes) - 1, 0))
        else:
            out_shapes.append(jax.ShapeDtypeStruct((2,) + h.shape, h.dtype))
            slots.append((len(out_shapes) - 1, None))
    n_out = len(out_shapes)
    hk = _HookSlots(hook, n_in=n, n_out=n_out, n_scratch=n + 4)

    def body(*refs):
        ins, outs, scratch = hk.own(refs)
        bufs = scratch[:n]
        send_sems, recv_sems, load_sems, store_sems = scratch[n:]
        if hook is not None:
            _, h_in, _, h_out, _, h_scratch = hk._split(refs)
            extra = hook.plan(h_in, h_out, h_scratch)
            hook.start(extra)
        xi, yi, ci = _mesh_position()
        own, sends, recvs = [], [], []
        for i in range(n):
            o, layer = slots[i]
            dst = (lambda core: outs[o].at[core]) if layer is None else (lambda core: outs[o].at[layer, core])
            own.append(dst(ci))
            mk = functools.partial(pltpu.make_async_remote_copy, src_ref=ins[i], send_sem=send_sems.at[i],
                                   recv_sem=recv_sems.at[i], device_id=(xi, yi, 1 - ci), device_id_type=MESH)
            sends.append(mk(dst_ref=dst(ci)))
            recvs.append(mk(dst_ref=dst(1 - ci)))
        loads, stores = _staged_copies(ins, own, bufs, load_sems, store_sems)
        for cp in loads + sends:
            cp.start()
        for ld, st in zip(loads, stores):
            ld.wait()
            st.start()
        for cp in recvs:
            cp.wait_recv()
        for cp in sends:
            cp.wait_send()
        for st in stores:
            st.wait()
        if hook is not None:
            hook.relay(extra)
            hook.finish(extra)

    outs = pl.pallas_call(
        body, in_specs=[ANY] * n + hk.in_specs, out_specs=[ANY] * n_out + hk.out_specs, out_shape=out_shapes + hk.out_shape,
        scratch_shapes=[pltpu.VMEM(h.shape, h.dtype) for h in halves]
        + [pltpu.SemaphoreType.DMA((n,)), pltpu.SemaphoreType.DMA((n,)), pltpu.SemaphoreType.DMA((n,)), pltpu.SemaphoreType.DMA((n,))]
        + hk.scratch,
        compiler_params=_comm_params(), name=name)(*halves, *hk.inputs)
    return outs if hook is None else (outs[:n_out], outs[n_out:])


def _cast_bf16(layers, x, norm_w, *, name, hook=None):
    n = len(layers)
    hk = _HookSlots(hook, n_in=n + 2, n_out=n + 1, n_scratch=0)

    def body(*refs):
        ins, outs, _ = hk.own(refs)
        if hook is not None:
            hk.run(refs, pl.program_id(0), ROW_BLOCKS)
        for i_ref, o_ref in zip(ins[:n], outs[:n]):
            o_ref[...] = i_ref[...].astype(o_ref.dtype)
        xv = ins[n][...]
        outs[n][...] = (xv * lax.rsqrt(jnp.mean(xv * xv, axis=-1, keepdims=True) + NORM_EPS) * ins[n + 1][...]).astype(BF16)

    in_blk = lambda a, l: pl.BlockSpec((None, a.shape[1] // ROW_BLOCKS, a.shape[2]), lambda i: (l, i, 0))
    out_blk = lambda a: pl.BlockSpec((a.shape[1] // ROW_BLOCKS, a.shape[2]), lambda i: (i, 0))
    x_blk = pl.BlockSpec((x.shape[0] // ROW_BLOCKS, x.shape[1]), lambda i: (i, 0))
    outs = pl.pallas_call(
        body, grid=(ROW_BLOCKS,),
        in_specs=[in_blk(a, l) for a, l in layers] + [x_blk, pl.BlockSpec((1, x.shape[1]), lambda i: (0, 0))] + hk.in_specs,
        out_specs=[out_blk(a) for a, _ in layers] + [x_blk] + hk.out_specs,
        out_shape=[jax.ShapeDtypeStruct(a.shape[1:], BF16) for a, _ in layers] + [jax.ShapeDtypeStruct(x.shape, BF16)] + hk.out_shape,
        scratch_shapes=hk.scratch,
        compiler_params=_params(*hk.semantics("parallel")), name=name)(*[a for a, _ in layers], x, norm_w, *hk.inputs)
    own = (outs[:n], outs[n])
    return own if hook is None else (own, outs[n + 1:])


def _full_weight(name, gathered):
    s, _, r, c = gathered.shape
    if name == "ssd_w_in":
        return _w_in_from_shards(gathered.reshape(s, 2 * r, c), name="ssd_w_in_unshard")
    if name in ("attn_w_qkv", "mlp_w_up0", "mlp_w_up1"):
        return gathered.reshape(s, 2 * r, c)
    return gathered.reshape(s * 2 * r, c)


class _StepComm:
    GATHER = {"in_proj": ("mlp_w_up0", "attn_w_o"), "conv": ("mlp_w_down0",), "scan": ("ssd_w_out", "mlp_w_up1"),
              "mlp_up_l0": ("attn_w_qkv",), "attn_fwd": ("mlp_w_down1",)}
    EXCHANGE = {"early": ("ssd_w_out", "attn_w_qkv", "attn_w_o", "mlp_w_up0", "mlp_w_up1", "mlp_w_down0", "mlp_w_down1"),
                "late": ("ssd_w_in",)}

    def __init__(self, shards, core):
        self.shards, self.core = shards, core
        self.chip_parts = {}
        self._pending = None

    def gather_hook(self, stage):
        names = self.GATHER.get(stage)
        return _GatherHook([self.shards[n] for n in names]) if names else None

    def weights_from(self, stage, gathered):
        return {n: _full_weight(n, g) for n, g in zip(self.GATHER[stage], gathered)}

    def chip_sums(self, mats, tag):
        parts = [_shard_halves(a) for a in mats.values()]
        theirs = _send_other_half(parts, name=f"grad_sibling_send_{tag}")
        return _add_sibling_half(parts, theirs, self.core, name=f"grad_chip_sum_{tag}")

    def exchange_hook(self, mats, which):
        self._pending = self.EXCHANGE[which]
        return _ExchangeHook(self.chip_sums({n: mats[n] for n in self._pending}, which))

    def received(self, arrays):
        self.chip_parts.update(zip(self._pending, arrays))


ADAMW_ROW_BLOCKS = 16


def _adamw(ws, gs, ms, vs, *, name, by_lanes=False):
    n = len(ws)
    if by_lanes:
        nb = min(a.shape[2] for a in ws) // LANES
    else:
        nb = ADAMW_ROW_BLOCKS if all(a.shape[1] % (8 * ADAMW_ROW_BLOCKS) == 0 for a in ws) else 1

    def body(*refs):
        ins, outs = refs[:4 * n], refs[4 * n:]
        for i in range(n):
            w_ref, g_ref, m_ref, v_ref = ins[i], ins[n + i], ins[2 * n + i], ins[3 * n + i]
            go_ref, d_ref, nm_ref, nv_ref = outs[i], outs[n + i], outs[2 * n + i], outs[3 * n + i]
            gv = g_ref[...]
            nm = ADAM_B1 * m_ref[...] + (1.0 - ADAM_B1) * gv
            nv = ADAM_B2 * v_ref[...] + (1.0 - ADAM_B2) * (gv * gv)
            m_hat = nm / (1.0 - ADAM_B1 ** ADAM_STEP)
            v_hat = nv / (1.0 - ADAM_B2 ** ADAM_STEP)
            go_ref[...] = gv
            d_ref[...] = -ADAM_LR * (m_hat / (jnp.sqrt(v_hat) + ADAM_EPS) + ADAM_WD * w_ref[...])
            nm_ref[...] = nm
            nv_ref[...] = nv

    if by_lanes:
        blks = [pl.BlockSpec((a.shape[0], a.shape[1], a.shape[2] // nb), lambda i: (0, 0, i)) for a in ws]
    else:
        blks = [pl.BlockSpec((a.shape[0], a.shape[1] // nb, a.shape[2]), lambda i: (0, i, 0)) for a in ws]
    shapes = [jax.ShapeDtypeStruct(a.shape, F32) for a in ws]
    outs = pl.pallas_call(body, grid=(nb,), in_specs=blks * 4, out_specs=blks * 4, out_shape=shapes * 4,
                          compiler_params=_params("parallel"), name=name)(*ws, *gs, *ms, *vs)
    return [tuple(outs[k * n + i] for k in range(4)) for i in range(n)]


SM_CONV_B, SM_NORM_W, SM_MIX_PRE, SM_MIX_POST, SM_FFN_PRE, SM_FFN_POST, SM_MISC, SM_CONV_W, SM_B_QKV, SM_B_O = 0, 4, 6, 8, 10, 12, 14, 16, 32, 34
SM_ROWS = 40
MISC_DT_BIAS, MISC_A_LOG, MISC_D, MISC_SINKS, MISC_LOSS = 0, 32, 64, 96, 112


def _shard_halves(a):
    c = a.shape[-1]
    return a.reshape(N_CHIPS, 2, -1, c)


def _rows(v):
    return v.reshape(-1, D_MODEL)


def _misc_row(dt_bias, a_log, d, sinks, loss):
    pad = jnp.zeros((D_MODEL - MISC_LOSS - 1,), F32)
    return jnp.concatenate([dt_bias.reshape(-1), a_log.reshape(-1), d.reshape(-1), sinks.reshape(-1), loss.reshape(1), pad]).reshape(1, D_MODEL)


def _replicated_rows(p, loss):
    return jnp.concatenate([
        _rows(p["ssd_conv_b"]), _rows(p["ssd_norm_w"]), _rows(p["mix_pre_norm"]), _rows(p["mix_post_norm"]),
        _rows(p["ffn_pre_norm"]), _rows(p["ffn_post_norm"]),
        _misc_row(p["ssd_dt_bias"], p["ssd_a_log"], p["ssd_d"], p["attn_sinks"], loss), jnp.zeros((1, D_MODEL), F32)], axis=0)


def _sharded_rows(conv_w, b_qkv, b_o):
    last = jnp.concatenate([b_qkv.reshape(-1), b_o.reshape(-1), jnp.zeros((D_MODEL - 640,), F32)]).reshape(1, D_MODEL)
    return jnp.concatenate([conv_w.reshape(SSD_CONV_WIDTH, D_MODEL), last, jnp.zeros((3, D_MODEL), F32)], axis=0)


REPLICATED = ("ssd_conv_b", "ssd_dt_bias", "ssd_a_log", "ssd_d", "ssd_norm_w", "attn_sinks",
              "mix_pre_norm", "mix_post_norm", "ffn_pre_norm", "ffn_post_norm")
MATRICES = ("ssd_w_in", "ssd_w_out", "attn_w_qkv", "attn_w_o", "mlp_w_up", "mlp_w_down")
WEIGHT_NAMES = ("ssd_w_in", "ssd_conv_w", "ssd_conv_b", "ssd_dt_bias", "ssd_a_log", "ssd_d", "ssd_norm_w", "ssd_w_out",
                "attn_w_qkv", "attn_b_qkv", "attn_sinks", "attn_w_o", "attn_b_o", "mlp_w_up", "mlp_w_down",
                "mix_pre_norm", "mix_post_norm", "ffn_pre_norm", "ffn_post_norm")


def _unpack_small(rows16, rows8, like):
    misc = rows16[SM_MISC]
    out = {
        "ssd_conv_b": rows16[SM_CONV_B:SM_CONV_B + 4], "ssd_norm_w": rows16[SM_NORM_W:SM_NORM_W + 2],
        "mix_pre_norm": rows16[SM_MIX_PRE:SM_MIX_PRE + 2], "mix_post_norm": rows16[SM_MIX_POST:SM_MIX_POST + 2],
        "ffn_pre_norm": rows16[SM_FFN_PRE:SM_FFN_PRE + 2], "ffn_post_norm": rows16[SM_FFN_POST:SM_FFN_POST + 2],
        "ssd_dt_bias": misc[MISC_DT_BIAS:MISC_DT_BIAS + 32], "ssd_a_log": misc[MISC_A_LOG:MISC_A_LOG + 32],
        "ssd_d": misc[MISC_D:MISC_D + 32], "attn_sinks": misc[MISC_SINKS:MISC_SINKS + 16],
        "ssd_conv_w": rows8[0:SSD_CONV_WIDTH], "attn_b_qkv": rows8[SSD_CONV_WIDTH, 0:384], "attn_b_o": rows8[SSD_CONV_WIDTH, 384:640],
    }
    return {k: v.reshape(like[k].shape) for k, v in out.items()}


def kernel(x, ssd_w_in, ssd_conv_w, ssd_conv_b, ssd_dt_bias, ssd_a_log, ssd_d, ssd_norm_w, ssd_w_out, attn_w_qkv, attn_b_qkv, attn_sinks, attn_w_o, attn_b_o, mlp_w_up, mlp_w_down, mix_pre_norm, mix_post_norm, ffn_pre_norm, ffn_post_norm, loss_target, m_ssd_w_in, m_ssd_conv_w, m_ssd_conv_b, m_ssd_dt_bias, m_ssd_a_log, m_ssd_d, m_ssd_norm_w, m_ssd_w_out, m_attn_w_qkv, m_attn_b_qkv, m_attn_sinks, m_attn_w_o, m_attn_b_o, m_mlp_w_up, m_mlp_w_down, m_mix_pre_norm, m_mix_post_norm, m_ffn_pre_norm, m_ffn_post_norm, v_ssd_w_in, v_ssd_conv_w, v_ssd_conv_b, v_ssd_dt_bias, v_ssd_a_log, v_ssd_d, v_ssd_norm_w, v_ssd_w_out, v_attn_w_qkv, v_attn_b_qkv, v_attn_sinks, v_attn_w_o, v_attn_b_o, v_mlp_w_up, v_mlp_w_down, v_mix_pre_norm, v_mix_post_norm, v_ffn_pre_norm, v_ffn_post_norm):
    w = dict(zip(WEIGHT_NAMES, (ssd_w_in, ssd_conv_w, ssd_conv_b, ssd_dt_bias, ssd_a_log, ssd_d, ssd_norm_w, ssd_w_out, attn_w_qkv, attn_b_qkv, attn_sinks, attn_w_o, attn_b_o, mlp_w_up, mlp_w_down, mix_pre_norm, mix_post_norm, ffn_pre_norm, ffn_post_norm)))
    m = dict(zip(WEIGHT_NAMES, (m_ssd_w_in, m_ssd_conv_w, m_ssd_conv_b, m_ssd_dt_bias, m_ssd_a_log, m_ssd_d, m_ssd_norm_w, m_ssd_w_out, m_attn_w_qkv, m_attn_b_qkv, m_attn_sinks, m_attn_w_o, m_attn_b_o, m_mlp_w_up, m_mlp_w_down, m_mix_pre_norm, m_mix_post_norm, m_ffn_pre_norm, m_ffn_post_norm)))
    v = dict(zip(WEIGHT_NAMES, (v_ssd_w_in, v_ssd_conv_w, v_ssd_conv_b, v_ssd_dt_bias, v_ssd_a_log, v_ssd_d, v_ssd_norm_w, v_ssd_w_out, v_attn_w_qkv, v_attn_b_qkv, v_attn_sinks, v_attn_w_o, v_attn_b_o, v_mlp_w_up, v_mlp_w_down, v_mix_pre_norm, v_mix_post_norm, v_ffn_pre_norm, v_ffn_post_norm)))
    chip = 2 * lax.axis_index("x") + lax.axis_index("y")

    two_halves = lambda a: a.reshape(2, a.shape[-2] // 2, a.shape[-1])
    later = {"ssd_w_out": (w["ssd_w_out"], 0), "attn_w_qkv": (w["attn_w_qkv"], 0), "attn_w_o": (w["attn_w_o"], 0),
             "mlp_w_up0": (w["mlp_w_up"], 0), "mlp_w_up1": (w["mlp_w_up"], 1),
             "mlp_w_down0": (w["mlp_w_down"], 0), "mlp_w_down1": (w["mlp_w_down"], 1)}
    first = _GatherHook([two_halves(w["ssd_w_in"].astype(BF16))], [w["ssd_conv_w"][0], w["attn_b_qkv"], w["attn_b_o"]])
    (cast, u0), (g_in, g_conv, g_bqkv, g_bo) = _cast_bf16(list(later.values()), x[0], w["mix_pre_norm"][0:1],
                                                          name="weights_to_bf16", hook=first)
    core = lax.axis_index("c").astype(jnp.int32).reshape(1)
    comm = _StepComm({k: two_halves(a) for k, a in zip(later, cast)}, core)
    full = {
        "ssd_w_in": _full_weight("ssd_w_in", g_in),
        "ssd_conv_w": g_conv.transpose(1, 0, 2).reshape(SSD_CONV_WIDTH, SSD_CONV_DIM),
        "attn_b_qkv": g_bqkv.reshape(ATTN_QKV), "attn_b_o": g_bo.reshape(D_MODEL),
    }
    for name in REPLICATED:
        full[name] = w[name][0] if name.startswith(("ssd_", "attn_")) else w[name]

    loss_tile, grad_x, gm, g = _local_step(x[0], loss_target[0], full, comm, u0)

    conv_w_rows = g["ssd_conv_w"].reshape(SSD_CONV_WIDTH * N_CHIPS, D_MODEL)
    b_qkv_rows = jnp.pad(g["attn_b_qkv"], (0, 2 * D_MODEL - ATTN_QKV)).reshape(2, D_MODEL)
    small = jnp.concatenate([_replicated_rows(g, loss_tile[0, 0]), conv_w_rows, b_qkv_rows, _rows(g["attn_b_o"]),
                             jnp.zeros((SM_ROWS - SM_B_O - 1, D_MODEL), F32)], axis=0)
    order = ("ssd_w_in", "ssd_w_out", "attn_w_qkv", "attn_w_o", "mlp_w_up0", "mlp_w_up1", "mlp_w_down0", "mlp_w_down1")
    halves = _sum_chips([comm.chip_parts[k] for k in order], name="grad_sum")
    (r_in, r_out, r_qkv, r_o, r_up, r_down), (small_all,) = _swap_halves(
        halves, layers=((4, 5), (6, 7)), hook=_ExchangeHook([], [small]), name="grad_halves_swap")
    small_sum, = _sum_chips([small_all], name="small_grad_sum")

    grads = {"ssd_w_in": r_in, "ssd_w_out": r_out, "attn_w_qkv": r_qkv, "attn_w_o": r_o, "mlp_w_up": r_up, "mlp_w_down": r_down}
    grads = {k: a.reshape(w[k].shape) for k, a in grads.items()}
    conv_w_g = lax.dynamic_index_in_dim(small_sum[SM_CONV_W:SM_CONV_W + 16].reshape(SSD_CONV_WIDTH, N_CHIPS, D_MODEL), chip, axis=1, keepdims=False)
    b_qkv_g = lax.dynamic_slice_in_dim(small_sum[SM_B_QKV:SM_B_QKV + 2].reshape(-1), chip * 384, 384)
    b_o_g = lax.dynamic_slice_in_dim(small_sum[SM_B_O], chip * 256, 256)
    small_g = jnp.concatenate([small_sum[0:16], _sharded_rows(conv_w_g, b_qkv_g, b_o_g)], axis=0)
    grads.update(_unpack_small(small_g[0:16], small_g[16:24], w))
    loss = small_sum[SM_MISC, MISC_LOSS]

    delta, new_m, new_v = {}, {}, {}
    stored = lambda a: jnp.swapaxes(a, 1, 2)
    rest = [name for name in MATRICES if name != "ssd_w_in"]
    mats = lambda p: [p[name] for name in rest]
    results = dict(zip(rest, _adamw(mats(w), mats(grads), mats(m), mats(v), name="adamw_matrices")))
    (w_in_result,) = _adamw([stored(w["ssd_w_in"])], [stored(grads["ssd_w_in"])], [stored(m["ssd_w_in"])],
                            [stored(v["ssd_w_in"])], by_lanes=True, name="adamw_ssd_w_in")
    results["ssd_w_in"] = tuple(stored(a) for a in w_in_result)
    for name in MATRICES:
        grads[name], delta[name], new_m[name], new_v[name] = results[name]
    zero = jnp.zeros((), F32)
    small_pack = lambda p: jnp.concatenate([_replicated_rows({k: p[k] for k in REPLICATED}, zero),
                                            _sharded_rows(p["ssd_conv_w"], p["attn_b_qkv"], p["attn_b_o"])], axis=0)[None]
    (_, d_s, m_s, v_s), = _adamw([small_pack(w)], [small_g[None]], [small_pack(m)], [small_pack(v)], name="adamw_vectors")
    d_s, m_s, v_s = d_s[0], m_s[0], v_s[0]
    delta.update(_unpack_small(d_s[0:16], d_s[16:24], w))
    new_m.update(_unpack_small(m_s[0:16], m_s[16:24], w))
    new_v.update(_unpack_small(v_s[0:16], v_s[16:24], w))

    return (loss, grad_x[None], *[grads[n] for n in WEIGHT_NAMES], *[delta[n] for n in WEIGHT_NAMES],
            *[new_m[n] for n in WEIGHT_NAMES], *[new_v[n] for n in WEIGHT_NAMES])
```

```python
import functools

import jax
import jax.numpy as jnp
from jax import lax
from jax.experimental import pallas as pl
from jax.experimental.pallas import tpu as pltpu

F32 = jnp.float32
BF16 = jnp.bfloat16

D_MODEL = 1024
SSD_D_INNER = 2048
SSD_HEAD_DIM = 64
SSD_N_HEADS = 32
SSD_N_GROUPS = 8
SSD_HPG = 4
SSD_D_STATE = 128
SSD_CONV_WIDTH = 4
SSD_CHUNK = 128
SSD_CONV_DIM = 4096
SSD_IN_DIM = 6176
SSD_IN_PAD = 6400
SSD_IN_TILE = 1280
SSD_DT_COL = 6144
SSD_GW = SSD_HPG * SSD_HEAD_DIM
ATTN_HEAD_DIM = 64
ATTN_N_Q = 16
ATTN_N_KV = 4
ATTN_REP = 4
ATTN_WINDOW = 128
ATTN_QKV = 1536
D_FF = 4096
NORM_EPS = 1e-6

ADAM_LR = 0.001
ADAM_B1 = 0.9
ADAM_B2 = 0.999
ADAM_EPS = 1e-08
ADAM_WD = 0.01
ADAM_STEP = 10

N_CHIPS = 4
N_DEV = 8
LANES = 128
VMEM_LIMIT = 48 * 1024 * 1024
BIG_TILE = 2048
MESH = pl.DeviceIdType.MESH


def _params(*sem):
    return pltpu.CompilerParams(dimension_semantics=sem, vmem_limit_bytes=VMEM_LIMIT)


def _dot(a, b, dims):
    return lax.dot_general(a, b, (dims, ((), ())), preferred_element_type=F32)


def _dot_nn(a, b):
    return _dot(a, b, ((1,), (0,)))


def _dot_nt(a, b):
    return _dot(a, b, ((1,), (1,)))


def _dot_tn(a, b):
    return _dot(a, b, ((0,), (0,)))


def _sigmoid(x):
    return 0.5 * jnp.tanh(0.5 * x) + 0.5


ANY = pl.BlockSpec(memory_space=pl.ANY)


class _HookSlots:
    def __init__(self, hook, n_in, n_out, n_scratch):
        self.hook = hook
        self.n_in, self.n_out, self.n_scratch = n_in, n_out, n_scratch
        self.inputs = list(hook.arrs) if hook else []
        self.out_shape = list(hook.out_shape) if hook else []
        self.scratch = list(hook.scratch) if hook else []
        self.in_specs = [ANY] * len(self.inputs)
        self.out_specs = [ANY] * len(self.out_shape)

    def _split(self, refs):
        a = self.n_in
        b = a + len(self.inputs)
        c = b + self.n_out
        d = c + len(self.out_shape)
        e = d + self.n_scratch
        return refs[:a], refs[a:b], refs[b:c], refs[c:d], refs[d:e], refs[e:]

    def own(self, refs):
        ins, _, outs, _, scratch, _ = self._split(refs)
        return ins, outs, scratch

    def run(self, refs, step, n_steps):
        _, h_in, _, h_out, _, h_scratch = self._split(refs)
        _run_hook(self.hook, h_in, h_out, h_scratch, step, n_steps)

    def semantics(self, *sem):
        return sem if self.hook is None else ("arbitrary",) * len(sem)


def _matmul(a, b, *, mode, out_dtypes, name, epilogue=None, extras=(), tm=1024, tn=1024, tk=1024,
            b_shards=False, out_shards=False, hook=None, f32_block=None):
    f32_tail = f32_block is not None
    if b_shards:
        s, b_rows, b_cols = b.shape
        b2 = (b_rows, s * b_cols)
        if mode == "nn":
            tn = b_cols
        else:
            assert mode == "nt"
            tk = b_cols
    else:
        b2 = b.shape
    if mode == "nn":
        (m, k), (k2, n) = a.shape, b2
    elif mode == "nt":
        (m, k), (n, k2) = a.shape, b2
    else:
        (k, m), (k2, n) = a.shape, b2
    assert k == k2, (a.shape, b.shape, mode)
    tm, tn, tk = min(tm, m), min(tn, n), min(tk, k)
    assert m % tm == 0 and n % tn == 0 and k % tk == 0, (m, n, k, tm, tn, tk)
    nk = k // tk
    if mode == "tn":
        a_spec = pl.BlockSpec((tk, tm), lambda i, j, kk: (kk, i))
    else:
        a_spec = pl.BlockSpec((tm, tk), lambda i, j, kk: (i, kk))
    if b_shards and mode == "nn":
        b_spec = pl.BlockSpec((None, tk, tn), lambda i, j, kk: (j, kk, 0))
    elif b_shards:
        b_spec = pl.BlockSpec((None, tn, tk), lambda i, j, kk: (kk, j, 0))
    elif mode == "nt":
        b_spec = pl.BlockSpec((tn, tk), lambda i, j, kk: (j, kk))
    else:
        b_spec = pl.BlockSpec((tk, tn), lambda i, j, kk: (kk, j))
    dims = {"nn": ((1,), (0,)), "nt": ((1,), (1,)), "tn": ((0,), (0,))}[mode]
    ex_specs = []
    for arr, kind in extras:
        if kind == "tile":
            ex_specs.append(pl.BlockSpec((tm, tn), lambda i, j, kk: (i, j)))
        else:
            ex_specs.append(pl.BlockSpec((1, tn), lambda i, j, kk: (0, j)))
    n_ex, n_out = len(extras), len(out_dtypes)
    if epilogue is None:
        epilogue = lambda acc: (acc,)
    hk = _HookSlots(hook, n_in=2 + n_ex, n_out=n_out + f32_tail, n_scratch=0 if nk == 1 else 1)
    grid = (m // tm, n // tn, nk)

    def body(*refs):
        (a_ref, b_ref, *ex), outs, scratch = hk.own(refs)
        if hook is not None:
            step = (pl.program_id(0) * grid[1] + pl.program_id(1)) * grid[2] + pl.program_id(2)
            hk.run(refs, step, grid[0] * grid[1] * grid[2])

        def finish(acc):
            res = epilogue(acc, *[e[...] for e in ex])
            for o, r in zip(outs, res):
                o[...] = r.astype(o.dtype)
            if f32_tail:
                outs[n_out][...] = acc[:, f32_block:f32_block + LANES]

        if nk == 1:
            finish(_dot(a_ref[...], b_ref[...], dims))
        else:
            acc_ref = scratch[0]
            kk = pl.program_id(2)

            @pl.when(kk == 0)
            def _():
                acc_ref[...] = jnp.zeros_like(acc_ref)

            acc_ref[...] += _dot(a_ref[...], b_ref[...], dims)

            @pl.when(kk == nk - 1)
            def _():
                finish(acc_ref[...])

    if out_shards:
        out_spec = pl.BlockSpec((None, tm, tn), lambda i, j, kk: (j, i, 0))
        out_dims = (n // tn, m, tn)
    else:
        out_spec = pl.BlockSpec((tm, tn), lambda i, j, kk: (i, j))
        out_dims = (m, n)
    tail_specs = [pl.BlockSpec((tm, LANES), lambda i, j, kk: (i, 0))] if f32_tail else []
    tail_shapes = [jax.ShapeDtypeStruct((m, LANES), F32)] if f32_tail else []
    outs = pl.pallas_call(
        body,
        grid=grid,
        in_specs=[a_spec, b_spec] + ex_specs + hk.in_specs,
        out_specs=[out_spec for _ in out_dtypes] + tail_specs + hk.out_specs,
        out_shape=[jax.ShapeDtypeStruct(out_dims, dt) for dt in out_dtypes] + tail_shapes + hk.out_shape,
        scratch_shapes=([] if nk == 1 else [pltpu.VMEM((tm, tn), F32)]) + hk.scratch,
        compiler_params=_params(*hk.semantics("parallel", "arbitrary" if f32_tail else "parallel", "arbitrary")),
        name=name,
    )(a, b, *[arr for arr, _ in extras], *hk.inputs)
    n_own = n_out + f32_tail
    own = outs[0] if n_own == 1 else outs[:n_own]
    return own if hook is None else (own, outs[n_own:])


def _row_tile(t, want):
    return min(t, want)


def _rms_fwd(x, w, *, name, resid=None, want_u=None, target=None):
    t, d = x.shape
    tr = _row_tile(t, 512)

    def norm(v, wv):
        return v * lax.rsqrt(jnp.mean(v * v, axis=-1, keepdims=True) + NORM_EPS) * wv

    row = pl.BlockSpec((tr, d), lambda i: (i, 0))
    vec = pl.BlockSpec((1, d), lambda i: (0, 0))
    if target is not None:
        def body(x_ref, w_ref, r_ref, t_ref, dh_ref, loss_ref):
            err = r_ref[...] + norm(x_ref[...].astype(F32), w_ref[...]) - t_ref[...]
            dh_ref[...] = err * (1.0 / d)

            @pl.when(pl.program_id(0) == 0)
            def _():
                loss_ref[...] = jnp.zeros_like(loss_ref)

            part = jnp.sum(jnp.sum(err * err, axis=1, keepdims=True), axis=0, keepdims=True) * (0.5 / d)
            loss_ref[...] += jnp.broadcast_to(part, loss_ref.shape)

        return pl.pallas_call(
            body, grid=(t // tr,), in_specs=[row, vec, row, row],
            out_specs=[row, pl.BlockSpec((8, LANES), lambda i: (0, 0))],
            out_shape=[jax.ShapeDtypeStruct((t, d), F32), jax.ShapeDtypeStruct((8, LANES), F32)],
            compiler_params=_params("arbitrary"), name=name)(x, w, resid, target)
    if resid is None:
        def body(x_ref, w_ref, o_ref):
            o_ref[...] = norm(x_ref[...].astype(F32), w_ref[...]).astype(BF16)
        ins, in_specs = (x, w), [row, vec]
        out_shape, out_specs = jax.ShapeDtypeStruct((t, d), BF16), row
    elif want_u is None:
        def body(x_ref, w_ref, r_ref, o_ref):
            o_ref[...] = r_ref[...] + norm(x_ref[...].astype(F32), w_ref[...])
        ins, in_specs = (x, w, resid), [row, vec, row]
        out_shape, out_specs = jax.ShapeDtypeStruct((t, d), F32), row
    else:
        def body(x_ref, w_ref, r_ref, w2_ref, o_ref, u_ref):
            h = r_ref[...] + norm(x_ref[...].astype(F32), w_ref[...])
            o_ref[...] = h
            u_ref[...] = norm(h, w2_ref[...]).astype(BF16)
        ins, in_specs = (x, w, resid, want_u), [row, vec, row, vec]
        out_shape = [jax.ShapeDtypeStruct((t, d), F32), jax.ShapeDtypeStruct((t, d), BF16)]
        out_specs = [row, row]
    return pl.pallas_call(body, grid=(t // tr,), in_specs=in_specs, out_specs=out_specs, out_shape=out_shape,
                          compiler_params=_params("parallel"), name=name)(*ins)


def _rms_bwd(x, w, dy, *, name, resid=None, out_dtype=F32, dx_col_sum=False):
    t, d = x.shape
    tr = _row_tile(t, 512)
    row = pl.BlockSpec((tr, d), lambda i: (i, 0))
    vec = pl.BlockSpec((1, d), lambda i: (0, 0))
    has_res = resid is not None

    def body(x_ref, w_ref, dy_ref, *rest):
        r_ref = rest[0] if has_res else None
        dx_ref, dw_ref = rest[has_res:has_res + 2]
        xv = x_ref[...].astype(F32)
        dyv = dy_ref[...].astype(F32)
        r = lax.rsqrt(jnp.mean(xv * xv, axis=-1, keepdims=True) + NORM_EPS)
        xhat = xv * r
        dyw = dyv * w_ref[...]
        dx = r * (dyw - xhat * jnp.mean(dyw * xhat, axis=-1, keepdims=True))
        if has_res:
            dx = dx + r_ref[...]
        dx_ref[...] = dx.astype(dx_ref.dtype)

        sums = [(dw_ref, dyv * xhat)] + ([(rest[-1], dx)] if dx_col_sum else [])

        @pl.when(pl.program_id(0) == 0)
        def _():
            for acc_ref, _ in sums:
                acc_ref[...] = jnp.zeros_like(acc_ref)

        for acc_ref, rows in sums:
            acc_ref[...] += jnp.sum(rows, axis=0, keepdims=True)

    ins = (x, w, dy) + ((resid,) if has_res else ())
    in_specs = [row, vec, row] + ([row] if has_res else [])
    n_vec = 2 if dx_col_sum else 1
    return pl.pallas_call(
        body, grid=(t // tr,), in_specs=in_specs, out_specs=[row] + [vec] * n_vec,
        out_shape=[jax.ShapeDtypeStruct((t, d), out_dtype)] + [jax.ShapeDtypeStruct((1, d), F32)] * n_vec,
        compiler_params=_params("arbitrary"), name=name)(*ins)


def _col_sum(x, *, name):
    t, n = x.shape
    tr = _row_tile(t, 512)

    def body(x_ref, o_ref):
        @pl.when(pl.program_id(0) == 0)
        def _():
            o_ref[...] = jnp.zeros_like(o_ref)

        o_ref[...] += jnp.sum(x_ref[...].astype(F32), axis=0, keepdims=True)

    return pl.pallas_call(
        body, grid=(t // tr,), in_specs=[pl.BlockSpec((tr, n), lambda i: (i, 0))],
        out_specs=pl.BlockSpec((1, n), lambda i: (0, 0)), out_shape=jax.ShapeDtypeStruct((1, n), F32),
        compiler_params=_params("arbitrary"), name=name)(x)


SSD_IN_SHARD = SSD_IN_DIM // N_CHIPS


def _w_in_from_shards(shards, *, name):
    d = shards.shape[1]
    tr = 256

    def body(s_ref, o_ref):
        o_ref[:, pl.ds(SSD_DT_COL, SSD_IN_PAD - SSD_DT_COL)] = jnp.zeros((tr, SSD_IN_PAD - SSD_DT_COL), o_ref.dtype)
        for s in range(N_CHIPS):
            o_ref[:, pl.ds(SSD_IN_SHARD * s, SSD_IN_SHARD)] = s_ref[s]

    return pl.pallas_call(
        body, grid=(d // tr,), in_specs=[pl.BlockSpec((N_CHIPS, tr, SSD_IN_SHARD), lambda i: (0, i, 0))],
        out_specs=pl.BlockSpec((tr, SSD_IN_PAD), lambda i: (i, 0)),
        out_shape=jax.ShapeDtypeStruct((d, SSD_IN_PAD), shards.dtype),
        compiler_params=_params("parallel"), name=name)(shards)


def _w_in_to_shards(g, *, name):
    d = g.shape[0]
    tr = 256

    def body(g_ref, o_ref):
        for s in range(N_CHIPS):
            o_ref[s] = g_ref[:, pl.ds(SSD_IN_SHARD * s, SSD_IN_SHARD)].astype(o_ref.dtype)

    return pl.pallas_call(
        body, grid=(d // tr,), in_specs=[pl.BlockSpec((tr, SSD_IN_PAD), lambda i: (i, 0))],
        out_specs=pl.BlockSpec((N_CHIPS, tr, SSD_IN_SHARD), lambda i: (0, i, 0)),
        out_shape=jax.ShapeDtypeStruct((N_CHIPS, d, SSD_IN_SHARD), BF16),
        compiler_params=_params("parallel"), name=name)(g)


XBC_COL0 = SSD_D_INNER // LANES


def _shift_down(v, k, row_ids):
    return jnp.where(row_ids >= k, pltpu.roll(v, k, axis=0), 0.0)


def _shift_up(v, k, row_ids):
    n = v.shape[0]
    return jnp.where(row_ids < n - k, pltpu.roll(v, n - k, axis=0), 0.0)


def _conv_pre(x, w, b, row_ids):
    pre = b + w[3:4, :] * x
    for k in (1, 2, 3):
        pre = pre + w[3 - k:4 - k, :] * _shift_down(x, k, row_ids)
    return pre


def _conv_fwd(zx, conv_w, conv_b, *, name, hook=None):
    t = zx.shape[0]
    nct = SSD_CONV_DIM // LANES
    hk = _HookSlots(hook, n_in=3, n_out=1, n_scratch=0)

    def body(*refs):
        (x_ref, w_ref, b_ref), (o_ref,), _ = hk.own(refs)
        if hook is not None:
            hk.run(refs, pl.program_id(0), nct)
        x = x_ref[...].astype(F32)
        row_ids = lax.broadcasted_iota(jnp.int32, x.shape, 0)
        pre = _conv_pre(x, w_ref[...], b_ref[...], row_ids)
        o_ref[...] = pre * _sigmoid(pre)

    outs = pl.pallas_call(
        body, grid=(nct,),
        in_specs=[pl.BlockSpec((t, LANES), lambda j: (0, XBC_COL0 + j)),
                  pl.BlockSpec((SSD_CONV_WIDTH, LANES), lambda j: (0, j)),
                  pl.BlockSpec((1, LANES), lambda j: (0, j))] + hk.in_specs,
        out_specs=[pl.BlockSpec((t, LANES), lambda j: (0, j))] + hk.out_specs,
        out_shape=[jax.ShapeDtypeStruct((t, SSD_CONV_DIM), F32)] + hk.out_shape,
        scratch_shapes=hk.scratch,
        compiler_params=_params(*hk.semantics("parallel")), name=name)(zx, conv_w, conv_b, *hk.inputs)
    return outs[0] if hook is None else (outs[0], outs[1:])


def _conv_bwd(zx, conv_w, conv_b, d_xs, d_bm, d_cm, dzx, *, name):
    t = zx.shape[0]
    nct = SSD_CONV_DIM // LANES
    n_xs = SSD_D_INNER // LANES
    n_bm = SSD_N_GROUPS * SSD_D_STATE // LANES

    def body(x_ref, w_ref, b_ref, dxs_ref, dbm_ref, dcm_ref, _, dx_ref, dw_ref, db_ref):
        x = x_ref[...].astype(F32)
        w = w_ref[...]
        j = pl.program_id(0)
        dy = jnp.where(j < n_xs, dxs_ref[...], jnp.where(j < n_xs + n_bm, dbm_ref[...], dcm_ref[...]))
        row_ids = lax.broadcasted_iota(jnp.int32, x.shape, 0)
        pre = _conv_pre(x, w, b_ref[...], row_ids)
        sg = _sigmoid(pre)
        dpre = dy * (sg * (1.0 + pre * (1.0 - sg)))
        dx = w[3:4, :] * dpre
        for k in (1, 2, 3):
            dx = dx + w[3 - k:4 - k, :] * _shift_up(dpre, k, row_ids)
        dx_ref[...] = dx.astype(dx_ref.dtype)
        db_ref[...] = jnp.sum(dpre, axis=0, keepdims=True)
        dw_ref[3:4, :] = jnp.sum(dpre * x, axis=0, keepdims=True)
        for k in (1, 2, 3):
            dw_ref[3 - k:4 - k, :] = jnp.sum(dpre * _shift_down(x, k, row_ids), axis=0, keepdims=True)

    clip = lambda j, lo, n: jnp.clip(j - lo, 0, n - 1)
    return pl.pallas_call(
        body, grid=(nct,),
        in_specs=[pl.BlockSpec((t, LANES), lambda j: (0, XBC_COL0 + j)),
                  pl.BlockSpec((SSD_CONV_WIDTH, LANES), lambda j: (0, j)),
                  pl.BlockSpec((1, LANES), lambda j: (0, j)),
                  pl.BlockSpec((t, LANES), lambda j: (0, clip(j, 0, n_xs))),
                  pl.BlockSpec((t, LANES), lambda j: (0, clip(j, n_xs, n_bm))),
                  pl.BlockSpec((t, LANES), lambda j: (0, clip(j, n_xs + n_bm, n_bm))), ANY],
        out_specs=[pl.BlockSpec((t, LANES), lambda j: (0, XBC_COL0 + j)),
                   pl.BlockSpec((SSD_CONV_WIDTH, LANES), lambda j: (0, j)), pl.BlockSpec((1, LANES), lambda j: (0, j))],
        out_shape=[jax.ShapeDtypeStruct(dzx.shape, dzx.dtype),
                   jax.ShapeDtypeStruct((SSD_CONV_WIDTH, SSD_CONV_DIM), F32),
                   jax.ShapeDtypeStruct((1, SSD_CONV_DIM), F32)],
        input_output_aliases={6: 0},
        compiler_params=_params("parallel"), name=name)(zx, conv_w, conv_b, d_xs, d_bm, d_cm, dzx)


def _softplus_fwd(dt_raw, bias_row, alog_row, *, name):
    t = dt_raw.shape[0]
    q = SSD_CHUNK
    tr = _row_tile(t, 1024)

    def body(x_ref, b_ref, al_ref, dt_ref, cum_ref):
        v = x_ref[...] + b_ref[...]
        e = jnp.exp(-jnp.abs(v))
        u = 1.0 + e
        log1p = jnp.where(u == 1.0, e, jnp.log(u) * (e / (u - 1.0)))
        dt = jnp.maximum(v, 0.0) + log1p
        a = dt * -jnp.exp(al_ref[...])
        lower = (lax.broadcasted_iota(jnp.int32, (q, q), 1) <= lax.broadcasted_iota(jnp.int32, (q, q), 0)).astype(F32)
        cums = [lax.dot_general(lower, a[c * q:(c + 1) * q, :], ((((1,), (0,))), ((), ())), precision=lax.Precision.HIGHEST,
                                preferred_element_type=F32) for c in range(tr // q)]
        dt_t, cum_t = dt.T, jnp.concatenate(cums, axis=0).T
        for g in range(SSD_N_GROUPS):
            rows = slice(g * SSD_HPG, (g + 1) * SSD_HPG)
            dt_ref[g] = dt_t[rows, :]
            cum_ref[g] = cum_t[rows, :]

    vec = pl.BlockSpec((1, LANES), lambda i: (0, 0))
    by_group = pl.BlockSpec((SSD_N_GROUPS, SSD_HPG, tr), lambda i: (0, 0, i))
    return pl.pallas_call(
        body, grid=(t // tr,),
        in_specs=[pl.BlockSpec((tr, LANES), lambda i: (i, 0)), vec, vec],
        out_specs=[by_group, by_group],
        out_shape=[jax.ShapeDtypeStruct((SSD_N_GROUPS, SSD_HPG, t), F32)] * 2,
        compiler_params=_params("parallel"), name=name)(dt_raw, bias_row, alog_row)


def _softplus_bwd(dt_raw, bias_row, ddt_rows, dzx, *, name):
    t = dt_raw.shape[0]
    tr = _row_tile(t, 1024)
    tail = SSD_IN_PAD - SSD_DT_COL

    def body(x_ref, b_ref, g_ref, _, o_ref, db_ref):
        v = x_ref[...] + b_ref[...]
        lane = lax.broadcasted_iota(jnp.int32, v.shape, 1)
        by_head = jnp.concatenate([g_ref[g] for g in range(SSD_N_GROUPS)]
                                  + [jnp.zeros((LANES - SSD_N_HEADS, tr), F32)], axis=0)
        d = jnp.where(lane < SSD_N_HEADS, by_head.T * _sigmoid(v), 0.0)
        o_ref[:, pl.ds(0, LANES)] = d.astype(o_ref.dtype)
        o_ref[:, pl.ds(LANES, tail - LANES)] = jnp.zeros((tr, tail - LANES), o_ref.dtype)

        @pl.when(pl.program_id(0) == 0)
        def _():
            db_ref[...] = jnp.zeros_like(db_ref)

        db_ref[...] += jnp.sum(d, axis=0, keepdims=True)

    return pl.pallas_call(
        body, grid=(t // tr,),
        in_specs=[pl.BlockSpec((tr, LANES), lambda i: (i, 0)), pl.BlockSpec((1, LANES), lambda i: (0, 0)),
                  pl.BlockSpec((SSD_N_GROUPS, SSD_HPG, tr), lambda i: (0, 0, i)), ANY],
        out_specs=[pl.BlockSpec((tr, tail), lambda i: (i, SSD_DT_COL // tail)), pl.BlockSpec((1, LANES), lambda i: (0, 0))],
        out_shape=[jax.ShapeDtypeStruct(dzx.shape, dzx.dtype), jax.ShapeDtypeStruct((1, LANES), F32)],
        input_output_aliases={3: 0},
        compiler_params=_params("arbitrary"), name=name)(dt_raw, bias_row, ddt_rows, dzx)


def _ssd_masks():
    q = SSD_CHUNK
    tt = lax.broadcasted_iota(jnp.int32, (q, q), 0)
    ss = lax.broadcasted_iota(jnp.int32, (q, q), 1)
    lane = lax.broadcasted_iota(jnp.int32, (1, SSD_GW), 1)
    srow = lax.broadcasted_iota(jnp.int32, (SSD_GW, 1), 0)
    hm = [(lane >= SSD_HEAD_DIM * j) & (lane < SSD_HEAD_DIM * (j + 1)) for j in range(SSD_HPG)]
    rm = [(srow >= SSD_HEAD_DIM * j) & (srow < SSD_HEAD_DIM * (j + 1)) for j in range(SSD_HPG)]
    return tt, ss, hm, rm


def _ssd_head_terms(dt_rows, cum_rows, a_rows, j, tt, ss):
    q = SSD_CHUNK
    dt_row = dt_rows[j:j + 1, :]
    dt_col = jnp.sum(jnp.where(tt == ss, dt_row, 0.0), axis=1, keepdims=True)
    a_row1 = a_rows[j:j + 1, :]
    a_11 = a_rows[j:j + 1, 0:1]
    cum_col = jnp.sum(jnp.where(ss <= tt, dt_row * a_row1, 0.0), axis=1, keepdims=True)
    cum_row = cum_rows[j:j + 1, :]
    decay = jnp.exp(jnp.where(ss <= tt, cum_col - cum_row, -jnp.inf))
    cum_last = cum_col[q - 1:q, :]
    e_col = jnp.exp(cum_col)
    dte_col = jnp.exp(cum_last - cum_col)
    e_last = jnp.exp(cum_last)
    return dt_col, dt_row, a_row1, a_11, decay, e_col, dte_col, e_last


SSD_CHUNKS_PER_STEP = 8
SSD_BC_COL0 = SSD_D_INNER // SSD_D_STATE


def _ssd_head_selects(terms, hm, rm):
    e_all = jnp.zeros((SSD_CHUNK, SSD_GW), F32)
    w_all = jnp.zeros((SSD_CHUNK, SSD_GW), F32)
    e_s = jnp.zeros((SSD_GW, 1), F32)
    for j in range(SSD_HPG):
        dt_col, _, _, _, _, e_col, dte_col, e_last = terms[j]
        e_all = jnp.where(hm[j], e_col, e_all)
        w_all = jnp.where(hm[j], dt_col * dte_col, w_all)
        e_s = jnp.where(rm[j], e_last, e_s)
    return e_all, w_all, e_s


def _ssd_fwd(xc, dtr, cumr, alog_b, d_b, *, name, hook=None):
    t = xc.shape[0]
    q = SSD_CHUNK
    nc = t // q
    kc = min(SSD_CHUNKS_PER_STEP, nc)
    rows = kc * q
    hk = _HookSlots(hook, n_in=7, n_out=2, n_scratch=1)

    def body(*refs):
        (x_ref, b_ref, c_ref, dtr_ref, cumr_ref, alog_ref, d_ref), (y_ref, st_ref), (s_scr,) = hk.own(refs)
        if hook is not None:
            hk.run(refs, pl.program_id(0) * (nc // kc) + pl.program_id(1), SSD_N_GROUPS * (nc // kc))

        @pl.when(pl.program_id(1) == 0)
        def _():
            s_scr[...] = jnp.zeros_like(s_scr)

        tt, ss, hm, rm = _ssd_masks()
        a_rows = -jnp.exp(alog_ref[...])
        d_rows = d_ref[...]
        d_all = jnp.zeros((1, SSD_GW), F32)
        for j in range(SSD_HPG):
            d_all = jnp.where(hm[j], d_rows[j:j + 1, 0:1], d_all)
        ks, hs = range(kc), range(SSD_HPG)
        sl = [pl.ds(k * q, q) for k in ks]
        x = [x_ref[sl[k], :] for k in ks]
        bm = [b_ref[sl[k], :].astype(BF16) for k in ks]
        cm = [c_ref[sl[k], :].astype(BF16) for k in ks]
        xb = [x[k].astype(BF16) for k in ks]
        terms = [[_ssd_head_terms(dtr_ref[:, sl[k]], cumr_ref[:, sl[k]], a_rows, j, tt, ss) for j in hs] for k in ks]
        g = [_dot_nt(cm[k], bm[k]) for k in ks]
        m = [[(g[k] * terms[k][j][4] * terms[k][j][1]).astype(BF16) for j in hs] for k in ks]
        yj = [[_dot_nn(m[k][j], xb[k]) for j in hs] for k in ks]
        sel = [_ssd_head_selects(terms[k], hm, rm) for k in ks]
        upd = [_dot_tn((x[k] * sel[k][1]).astype(BF16), bm[k]) for k in ks]
        states = [s_scr[...]]
        for k in ks:
            states.append(states[k] * sel[k][2] + upd[k])
        inter = [_dot_nt(cm[k], states[k].astype(BF16)) for k in ks]
        ys = []
        for k in ks:
            y = jnp.zeros((q, SSD_GW), F32)
            for j in hs:
                y = jnp.where(hm[j], yj[k][j], y)
            ys.append(y + inter[k] * sel[k][0] + x[k] * d_all)
        for k in ks:
            st_ref[k] = states[k]
        y_ref[...] = jnp.concatenate(ys, axis=0)
        s_scr[...] = states[kc]

    blk = lambda width, off: pl.BlockSpec((rows, width), lambda g, c: (c, off + g))
    par_s = pl.BlockSpec((None, SSD_HPG, LANES), lambda g, c: (g, 0, 0))
    row_s = pl.BlockSpec((None, SSD_HPG, rows), lambda g, c: (g, 0, c))
    outs = pl.pallas_call(
        body, grid=(SSD_N_GROUPS, nc // kc),
        in_specs=[blk(SSD_GW, 0), blk(SSD_D_STATE, SSD_BC_COL0), blk(SSD_D_STATE, SSD_BC_COL0 + SSD_N_GROUPS),
                  row_s, row_s, par_s, par_s] + hk.in_specs,
        out_specs=[blk(SSD_GW, 0), pl.BlockSpec((None, kc, SSD_GW, SSD_D_STATE), lambda g, c: (g, c, 0, 0))] + hk.out_specs,
        out_shape=[jax.ShapeDtypeStruct((t, SSD_D_INNER), F32),
                   jax.ShapeDtypeStruct((SSD_N_GROUPS, nc, SSD_GW, SSD_D_STATE), F32)] + hk.out_shape,
        scratch_shapes=[pltpu.VMEM((SSD_GW, SSD_D_STATE), F32)] + hk.scratch,
        compiler_params=_params(*hk.semantics("parallel", "arbitrary")), name=name)(
            xc, xc, xc, dtr, cumr, alog_b, d_b, *hk.inputs)
    return outs if hook is None else (outs[:2], outs[2:])


def _ssd_bwd(xc, dtr, cumr, alog_b, d_b, states, dy, *, name, hook=None):
    t = xc.shape[0]
    q = SSD_CHUNK
    nc = t // q
    kc = min(SSD_CHUNKS_PER_STEP, nc)
    nst = nc // kc
    rows = kc * q
    rev = lambda c: nst - 1 - c
    hk = _HookSlots(hook, n_in=9, n_out=5, n_scratch=1)

    def body(*refs):
        ((x_ref, b_ref, c_ref, dtr_ref, cumr_ref, alog_ref, d_ref, st_ref, dy_ref),
         (dx_ref, db_ref, dc_ref, ddt_ref, dpar_ref), (ds_scr,)) = hk.own(refs)
        if hook is not None:
            hk.run(refs, pl.program_id(0) * nst + pl.program_id(1), SSD_N_GROUPS * nst)

        @pl.when(pl.program_id(1) == 0)
        def _():
            ds_scr[...] = jnp.zeros_like(ds_scr)
            dpar_ref[...] = jnp.zeros_like(dpar_ref)

        tt, ss, hm, rm = _ssd_masks()
        tcol = lax.broadcasted_iota(jnp.int32, (q, 1), 0)
        lane = lax.broadcasted_iota(jnp.int32, (1, LANES), 1)
        a_rows = -jnp.exp(alog_ref[...])
        d_rows = d_ref[...]
        d_all = jnp.zeros((1, SSD_GW), F32)
        for j in range(SSD_HPG):
            d_all = jnp.where(hm[j], d_rows[j:j + 1, 0:1], d_all)
        ks, hs = range(kc), range(SSD_HPG)
        sl = [pl.ds(k * q, q) for k in ks]
        x = [x_ref[sl[k], :] for k in ks]
        dyv = [dy_ref[sl[k], :] for k in ks]
        bm = [b_ref[sl[k], :].astype(BF16) for k in ks]
        cm = [c_ref[sl[k], :].astype(BF16) for k in ks]
        s_in = [st_ref[k] for k in ks]
        xb = [x[k].astype(BF16) for k in ks]
        dyb = [dyv[k].astype(BF16) for k in ks]
        s_b = [s_in[k].astype(BF16) for k in ks]
        terms = [[_ssd_head_terms(dtr_ref[:, sl[k]], cumr_ref[:, sl[k]], a_rows, j, tt, ss) for j in hs] for k in ks]
        sel = [_ssd_head_selects(terms[k], hm, rm) for k in ks]
        e_all, w_all, e_s = [s_[0] for s_ in sel], [s_[1] for s_ in sel], [s_[2] for s_ in sel]
        dye = [(dyv[k] * e_all[k]).astype(BF16) for k in ks]
        ds_loc = [_dot_tn(dye[k], cm[k]) for k in ks]
        ds = [None] * kc
        running = ds_scr[...]
        for k in reversed(ks):
            ds[k] = running
            running = running * e_s[k] + ds_loc[k]
        ds_scr[...] = running
        ds_b = [ds[k].astype(BF16) for k in ks]
        g = [_dot_nt(cm[k], bm[k]) for k in ks]
        cs = [_dot_nt(cm[k], s_b[k]) for k in ks]
        bds = [_dot_nt(bm[k], ds_b[k]) for k in ks]
        dm = [[_dot_nt(jnp.where(hm[j], dyv[k], 0.0).astype(BF16), xb[k]) for j in hs] for k in ks]
        gl = [[g[k] * terms[k][j][4] for j in hs] for k in ks]
        wp = [[dm[k][j] * gl[k][j] for j in hs] for k in ks]
        mt = [[(gl[k][j] * terms[k][j][1]).astype(BF16) for j in hs] for k in ks]
        dxj = [[_dot_tn(mt[k][j], dyb[k]) for j in hs] for k in ks]
        dg = []
        for k in ks:
            acc = jnp.zeros((q, q), F32)
            for j in hs:
                acc = acc + dm[k][j] * terms[k][j][4] * terms[k][j][1]
            dg.append(acc.astype(BF16))
        dy_cs = [dyv[k] * cs[k] for k in ks]
        x_bds = [x[k] * bds[k] for k in ks]
        dy_x = [dyv[k] * x[k] for k in ks]
        ds_s = [ds[k] * s_in[k] for k in ks]
        w = [[wp[k][j] * terms[k][j][1] for j in hs] for k in ks]
        rw_col = [[jnp.sum(w[k][j], axis=1, keepdims=True) for j in hs] for k in ks]
        cw_row = [[jnp.sum(w[k][j], axis=0, keepdims=True) for j in hs] for k in ks]
        cwp_row = [[jnp.sum(wp[k][j], axis=0, keepdims=True) for j in hs] for k in ks]
        r1_col = [[jnp.sum(jnp.where(hm[j], dy_cs[k], 0.0), axis=1, keepdims=True) * terms[k][j][5] for j in hs] for k in ks]
        dw_col = [[jnp.sum(jnp.where(hm[j], x_bds[k], 0.0), axis=1, keepdims=True) for j in hs] for k in ks]
        head_rows = [slice(j * SSD_HEAD_DIM, (j + 1) * SSD_HEAD_DIM) for j in hs]
        lane_sum = lambda v: jnp.sum(v, axis=1, keepdims=True)
        s_sum = [[lane_sum(jnp.sum(ds_s[k][head_rows[j], :], axis=0, keepdims=True)) for j in hs] for k in ks]
        dy_x_cols = [jnp.sum(dy_x[k], axis=0, keepdims=True) for k in ks]
        d_d = [[lane_sum(jnp.where(hm[j], dy_x_cols[k], 0.0)) for j in hs] for k in ks]
        ddt_rows = [[None] * SSD_HPG for _ in ks]
        dpar = [jnp.zeros((1, LANES), F32) for _ in hs]
        for k in ks:
            for j in hs:
                dt_col, dt_row, a_row1, a_11, _, _, dte_col, e_last = terms[k][j]
                dww = dw_col[k][j] * (dt_col * dte_col)
                last_add = jnp.sum(dww, axis=0, keepdims=True) + e_last * s_sum[k][j]
                dcum_col = rw_col[k][j] + r1_col[k][j] - dww + jnp.where(tcol == q - 1, last_add, 0.0)
                da_row = jnp.sum(jnp.where(tt >= ss, dcum_col, 0.0), axis=0, keepdims=True)
                da_col = jnp.sum(jnp.where(ss >= tt, -cw_row[k][j], 0.0), axis=1, keepdims=True)
                ddt_col = a_11 * da_col + dw_col[k][j] * dte_col
                ddt_rows[k][j] = (a_row1 * da_row + cwp_row[k][j]
                                  + jnp.sum(jnp.where(tt == ss, ddt_col, 0.0), axis=0, keepdims=True))
                d_a = jnp.sum(dt_row * da_row, axis=1, keepdims=True) + jnp.sum(dt_col * da_col, axis=0, keepdims=True)
                dpar[j] = dpar[j] + jnp.where(lane == 0, d_a * a_11, 0.0) + jnp.where(lane == 1, d_d[k][j], 0.0)
        dxs = []
        for k in ks:
            acc = jnp.zeros((q, SSD_GW), F32)
            for j in hs:
                acc = jnp.where(hm[j], dxj[k][j], acc)
            dxs.append(acc + w_all[k] * bds[k] + d_all * dyv[k])
        xw = [(x[k] * w_all[k]).astype(BF16) for k in ks]
        dc = [_dot_nn(dg[k], bm[k]) + _dot_nn(dye[k], s_b[k]) for k in ks]
        db = [_dot_tn(dg[k], cm[k]) + _dot_nn(xw[k], ds_b[k]) for k in ks]
        dx_ref[...] = jnp.concatenate(dxs, axis=0)
        dc_ref[...] = jnp.concatenate(dc, axis=0)
        db_ref[...] = jnp.concatenate(db, axis=0)
        ddt_ref[...] = jnp.concatenate([jnp.concatenate([ddt_rows[k][j] for k in ks], axis=1) for j in hs], axis=0)
        dpar_ref[...] += jnp.concatenate(dpar, axis=0)

    blk = lambda width, off: pl.BlockSpec((rows, width), lambda g, c: (rev(c), off + g))
    par_s = pl.BlockSpec((None, SSD_HPG, LANES), lambda g, c: (g, 0, 0))
    outs = pl.pallas_call(
        body, grid=(SSD_N_GROUPS, nst),
        in_specs=[blk(SSD_GW, 0), blk(SSD_D_STATE, SSD_BC_COL0), blk(SSD_D_STATE, SSD_BC_COL0 + SSD_N_GROUPS),
                  pl.BlockSpec((None, SSD_HPG, rows), lambda g, c: (g, 0, rev(c))),
                  pl.BlockSpec((None, SSD_HPG, rows), lambda g, c: (g, 0, rev(c))), par_s, par_s,
                  pl.BlockSpec((None, kc, SSD_GW, SSD_D_STATE), lambda g, c: (g, rev(c), 0, 0)), blk(SSD_GW, 0)] + hk.in_specs,
        out_specs=[blk(SSD_GW, 0), blk(SSD_D_STATE, 0), blk(SSD_D_STATE, 0),
                   pl.BlockSpec((None, SSD_HPG, rows), lambda g, c: (g, 0, rev(c))), par_s] + hk.out_specs,
        out_shape=[jax.ShapeDtypeStruct((t, SSD_D_INNER), F32),
                   jax.ShapeDtypeStruct((t, SSD_N_GROUPS * SSD_D_STATE), F32),
                   jax.ShapeDtypeStruct((t, SSD_N_GROUPS * SSD_D_STATE), F32),
                   jax.ShapeDtypeStruct((SSD_N_GROUPS, SSD_HPG, t), F32),
                   jax.ShapeDtypeStruct((SSD_N_GROUPS, SSD_HPG, LANES), F32)] + hk.out_shape,
        scratch_shapes=[pltpu.VMEM((SSD_GW, SSD_D_STATE), F32)] + hk.scratch,
        compiler_params=_params(*hk.semantics("parallel", "arbitrary")), name=name)(
            xc, xc, xc, dtr, cumr, alog_b, d_b, states, dy, *hk.inputs)
    return outs if hook is None else (outs[:5], outs[5:])


def _gate_norm_fwd(y, zx, norm_w, *, name):
    t = y.shape[0]
    tr = _row_tile(t, 256)
    row = pl.BlockSpec((tr, SSD_D_INNER), lambda i: (i, 0))

    def body(y_ref, z_ref, w_ref, o_ref):
        for gi in range(SSD_N_GROUPS):
            sl = pl.ds(gi * SSD_GW, SSD_GW)
            z = z_ref[:, sl].astype(F32)
            gv = y_ref[:, sl] * (z * _sigmoid(z))
            r = lax.rsqrt(jnp.mean(gv * gv, axis=-1, keepdims=True) + NORM_EPS)
            o_ref[:, sl] = (gv * r * w_ref[:, sl]).astype(BF16)

    return pl.pallas_call(
        body, grid=(t // tr,), in_specs=[row, row, pl.BlockSpec((1, SSD_D_INNER), lambda i: (0, 0))],
        out_specs=row, out_shape=jax.ShapeDtypeStruct((t, SSD_D_INNER), BF16),
        compiler_params=_params("parallel"), name=name)(y, zx, norm_w)


def _gate_norm_bwd(y, zx, norm_w, dyn, *, name):
    t = y.shape[0]
    tr = _row_tile(t, 256)
    row = pl.BlockSpec((tr, SSD_D_INNER), lambda i: (i, 0))
    vec = pl.BlockSpec((1, SSD_D_INNER), lambda i: (0, 0))

    def body(y_ref, z_ref, w_ref, dyn_ref, dy_ref, dz_ref, dw_ref):
        @pl.when(pl.program_id(0) == 0)
        def _():
            dw_ref[...] = jnp.zeros_like(dw_ref)

        for gi in range(SSD_N_GROUPS):
            sl = pl.ds(gi * SSD_GW, SSD_GW)
            z = z_ref[:, sl].astype(F32)
            yv = y_ref[:, sl]
            sg = _sigmoid(z)
            sz = z * sg
            gv = yv * sz
            r = lax.rsqrt(jnp.mean(gv * gv, axis=-1, keepdims=True) + NORM_EPS)
            ghat = gv * r
            dout = dyn_ref[:, sl].astype(F32)
            dgh = dout * w_ref[:, sl]
            dgv = r * (dgh - ghat * jnp.mean(dgh * ghat, axis=-1, keepdims=True))
            dy_ref[:, sl] = dgv * sz
            dz_ref[:, sl] = (dgv * yv * (sg * (1.0 + z * (1.0 - sg)))).astype(dz_ref.dtype)
            dw_ref[:, sl] += jnp.sum(dout * ghat, axis=0, keepdims=True)

    return pl.pallas_call(
        body, grid=(t // tr,), in_specs=[row, row, vec, row], out_specs=[row, row, vec],
        out_shape=[jax.ShapeDtypeStruct((t, SSD_D_INNER), F32), jax.ShapeDtypeStruct((t, SSD_IN_PAD), BF16),
                   jax.ShapeDtypeStruct((1, SSD_D_INNER), F32)],
        compiler_params=_params("arbitrary"), name=name)(y, zx, norm_w, dyn)


ATTN_KV_W = ATTN_N_KV * ATTN_HEAD_DIM
ATTN_Q_HALF = 512
ATTN_K_BLK = ATTN_N_Q * ATTN_HEAD_DIM // ATTN_KV_W
ATTN_V_BLK = ATTN_K_BLK + 1


def _attn_valid(first_block):
    w = ATTN_WINDOW
    qpos = lax.broadcasted_iota(jnp.int32, (w, 2 * w), 0) + w
    kpos = lax.broadcasted_iota(jnp.int32, (w, 2 * w), 1)
    rel = qpos - kpos
    return (rel >= 0) & (rel < w) & jnp.logical_not(first_block & (kpos < w))


def _attn_head_views(lo_ref, hi_ref):
    hd = ATTN_HEAD_DIM
    per_half = ATTN_Q_HALF // hd
    return [(lo_ref if h < per_half else hi_ref)[:, pl.ds((h % per_half) * hd, hd)] for h in range(ATTN_N_Q)]


def _attn_block_views(lo_ref, hi_ref, kc_ref, kp_ref, vc_ref, vp_ref):
    hd = ATTN_HEAD_DIM
    kv_cols = [pl.ds(kh * hd, hd) for kh in range(ATTN_N_KV)]
    kb = [jnp.concatenate([kp_ref[:, c], kc_ref[:, c]], axis=0) for c in kv_cols]
    vb = [jnp.concatenate([vp_ref[:, c], vc_ref[:, c]], axis=0) for c in kv_cols]
    return _attn_head_views(lo_ref, hi_ref), kb, vb


def _attn_scores(q, kb, valid):
    scale = ATTN_HEAD_DIM ** -0.5
    return [jnp.where(valid, _dot_nt(q[h], kb[h // ATTN_REP]) * scale, -jnp.inf) for h in range(ATTN_N_Q)]


def _attn_softmax(s, sink):
    heads = range(ATTN_N_Q)
    m = [jnp.maximum(jnp.max(s[h], axis=1, keepdims=True), sink[h]) for h in heads]
    e = [jnp.exp(s[h] - m[h]) for h in heads]
    es = [jnp.exp(sink[h] - m[h]) for h in heads]
    inv = [1.0 / (jnp.sum(e[h], axis=1, keepdims=True) + es[h]) for h in heads]
    return e, es, inv


def _attn_fwd(qkv, sinks_b, *, name, hook=None):
    t = qkv.shape[0]
    w = ATTN_WINDOW
    nb = t // w
    prev = lambda n: jnp.maximum(n - 1, 0)
    hk = _HookSlots(hook, n_in=7, n_out=1, n_scratch=0)

    def body(*refs):
        (qlo_ref, qhi_ref, kc_ref, kp_ref, vc_ref, vp_ref, sink_ref), (o_ref,), _ = hk.own(refs)
        if hook is not None:
            hk.run(refs, pl.program_id(0), nb)
        heads = range(ATTN_N_Q)
        q, kb, vb = _attn_block_views(qlo_ref, qhi_ref, kc_ref, kp_ref, vc_ref, vp_ref)
        sink = [sink_ref[h:h + 1, 0:1] for h in heads]
        e, _, inv = _attn_softmax(_attn_scores(q, kb, _attn_valid(pl.program_id(0) == 0)), sink)
        out = [_dot_nn((e[h] * inv[h]).astype(BF16), vb[h // ATTN_REP]).astype(o_ref.dtype) for h in heads]
        o_ref[...] = jnp.concatenate(out, axis=1)

    qh = lambda half: pl.BlockSpec((w, ATTN_Q_HALF), lambda n: (n, half))
    kv = lambda blk, idx: pl.BlockSpec((w, ATTN_KV_W), lambda n: (idx(n), blk))
    cur = lambda n: n
    outs = pl.pallas_call(
        body, grid=(nb,),
        in_specs=[qh(0), qh(1), kv(ATTN_K_BLK, cur), kv(ATTN_K_BLK, prev), kv(ATTN_V_BLK, cur), kv(ATTN_V_BLK, prev),
                  pl.BlockSpec((ATTN_N_Q, LANES), lambda n: (0, 0))] + hk.in_specs,
        out_specs=[pl.BlockSpec((w, D_MODEL), lambda n: (n, 0))] + hk.out_specs,
        out_shape=[jax.ShapeDtypeStruct((t, D_MODEL), BF16)] + hk.out_shape,
        scratch_shapes=hk.scratch,
        compiler_params=_params(*hk.semantics("parallel")), name=name)(qkv, qkv, qkv, qkv, qkv, qkv, sinks_b, *hk.inputs)
    return outs[0] if hook is None else (outs[0], outs[1:])


def _attn_bwd(qkv, sinks_b, dout, *, name):
    t = qkv.shape[0]
    w = ATTN_WINDOW
    nb = t // w
    hd = ATTN_HEAD_DIM
    clamp = lambda n: jnp.minimum(n, nb - 1)
    prev = lambda n: jnp.maximum(clamp(n) - 1, 0)

    def body(qlo_ref, qhi_ref, kc_ref, kp_ref, vc_ref, vp_ref, sink_ref, dolo_ref, dohi_ref,
             dq_ref, dkv_ref, dsink_ref, carry):
        n = pl.program_id(0)

        @pl.when(n == 0)
        def _():
            carry[...] = jnp.zeros_like(carry)
            dsink_ref[...] = jnp.zeros_like(dsink_ref)

        @pl.when(n < nb)
        def _():
            heads, kvs = range(ATTN_N_Q), range(ATTN_N_KV)
            q, kb, vb = _attn_block_views(qlo_ref, qhi_ref, kc_ref, kp_ref, vc_ref, vp_ref)
            do = _attn_head_views(dolo_ref, dohi_ref)
            sink = [sink_ref[h:h + 1, 0:1] for h in heads]
            s = _attn_scores(q, kb, _attn_valid(n == 0))
            dp = [_dot_nt(do[h], vb[h // ATTN_REP]) for h in heads]
            e, es, inv = _attn_softmax(s, sink)
            p = [e[h] * inv[h] for h in heads]
            delta = [jnp.sum(p[h] * dp[h], axis=1, keepdims=True) for h in heads]
            dsc = [(p[h] * (dp[h] - delta[h]) * (hd ** -0.5)).astype(BF16) for h in heads]
            pb = [p[h].astype(BF16) for h in heads]
            dq = [_dot_nn(dsc[h], kb[h // ATTN_REP]).astype(dq_ref.dtype) for h in heads]
            stack = lambda per_head, kh: jnp.concatenate(per_head[kh * ATTN_REP:(kh + 1) * ATTN_REP], axis=0)
            dkb = [_dot_tn(stack(dsc, kh), stack(q, kh)) for kh in kvs]
            dvb = [_dot_tn(stack(pb, kh), stack(do, kh)) for kh in kvs]
            dsink = [jnp.broadcast_to(jnp.sum(-es[h] * inv[h] * delta[h], axis=0, keepdims=True), (1, LANES)) for h in heads]
            dq_ref[...] = jnp.concatenate(dq, axis=1)
            dsink_ref[...] += jnp.concatenate(dsink, axis=0)
            dkv_ref[...] = (carry[...] + jnp.concatenate([d[0:w, :] for d in dkb + dvb], axis=1)).astype(dkv_ref.dtype)
            carry[...] = jnp.concatenate([d[w:2 * w, :] for d in dkb + dvb], axis=1)

        @pl.when(n == nb)
        def _():
            dkv_ref[...] = carry[...].astype(dkv_ref.dtype)

    qh = lambda half: pl.BlockSpec((w, ATTN_Q_HALF), lambda n: (clamp(n), half))
    kv = lambda blk, idx: pl.BlockSpec((w, ATTN_KV_W), lambda n: (idx(n), blk))
    return pl.pallas_call(
        body, grid=(nb + 1,),
        in_specs=[qh(0), qh(1), kv(ATTN_K_BLK, clamp), kv(ATTN_K_BLK, prev), kv(ATTN_V_BLK, clamp), kv(ATTN_V_BLK, prev),
                  pl.BlockSpec((ATTN_N_Q, LANES), lambda n: (0, 0)), qh(0), qh(1)],
        out_specs=[pl.BlockSpec((w, D_MODEL), lambda n: (clamp(n), 0)),
                   pl.BlockSpec((w, 2 * ATTN_KV_W), lambda n: (jnp.maximum(n - 1, 0), 0)),
                   pl.BlockSpec((ATTN_N_Q, LANES), lambda n: (0, 0))],
        out_shape=[jax.ShapeDtypeStruct((t, D_MODEL), BF16), jax.ShapeDtypeStruct((t, 2 * ATTN_KV_W), BF16),
                   jax.ShapeDtypeStruct((ATTN_N_Q, LANES), F32)],
        scratch_shapes=[pltpu.VMEM((w, 2 * ATTN_KV_W), F32)],
        compiler_params=_params("arbitrary"), name=name)(qkv, qkv, qkv, qkv, qkv, qkv, sinks_b, dout, dout)


def _sq_relu_epilogue(acc):
    r = jnp.maximum(acc, 0.0)
    return (r * r,)


def _sq_relu_bwd_epilogue(acc, act):
    return (acc * (2.0 * jnp.sqrt(act.astype(F32))),)


def _bias_epilogue(acc, bias):
    return (acc + bias,)


def _plain_run(stage, fn, *args, **kwargs):
    return fn(*args, **kwargs)


def _mlp_fwd(u, w_up, w_down, tag, run=_plain_run):
    act = run(f"mlp_up_{tag}", _matmul, u, w_up, mode="nn", out_dtypes=(BF16,), epilogue=_sq_relu_epilogue, b_shards=True,
              tm=BIG_TILE, name=f"mlp_up_{tag}")
    f = run(f"mlp_down_{tag}", _matmul, act, w_down, mode="nn", out_dtypes=(BF16,), tk=BIG_TILE, name=f"mlp_down_{tag}")
    return act, f


def _mlp_bwd(u, act, w_up, w_down, df, tag):
    dpre = _matmul(df, w_down, mode="nt", out_dtypes=(BF16,), epilogue=_sq_relu_bwd_epilogue,
                   extras=((act, "tile"),), name=f"mlp_dact_{tag}")
    dw_down = _matmul(act, df, mode="tn", out_dtypes=(BF16,), tk=BIG_TILE, name=f"mlp_dwdown_{tag}")
    du = _matmul(dpre, w_up, mode="nt", out_dtypes=(F32,), b_shards=True, tm=BIG_TILE, name=f"mlp_du_{tag}")
    dw_up = _matmul(u, dpre, mode="tn", out_dtypes=(BF16,), out_shards=True, tk=BIG_TILE, name=f"mlp_dwup_{tag}")
    return du, dw_up, dw_down


def _head_param_rows(p):
    return jnp.broadcast_to(p.reshape(SSD_N_GROUPS, SSD_HPG, 1), (SSD_N_GROUPS, SSD_HPG, LANES))


def _local_step(x, target, wts, comm=None, u0=None):
    t = x.shape[0]
    wts = dict(wts)
    row = lambda v: v.reshape(1, -1)
    mix_pre, mix_post, ffn_pre, ffn_post = wts["mix_pre_norm"], wts["mix_post_norm"], wts["ffn_pre_norm"], wts["ffn_post_norm"]

    def gathering(stage, fn, *args, **kwargs):
        hook = comm.gather_hook(stage) if comm is not None else None
        if hook is None:
            return fn(*args, **kwargs)
        out, got = fn(*args, hook=hook, **kwargs)
        wts.update(comm.weights_from(stage, got))
        return out

    if u0 is None:
        u0 = _rms_fwd(x, row(mix_pre[0]), name="rms_pre_mix0")
    zx, dt_raw = gathering("in_proj", _matmul, u0, wts["ssd_w_in"], mode="nn", out_dtypes=(BF16,), tn=SSD_IN_TILE,
                           f32_block=SSD_DT_COL - (SSD_IN_PAD - SSD_IN_TILE),
                           name="ssd_in_proj")
    xc = gathering("conv", _conv_fwd, zx, wts["ssd_conv_w"], row(wts["ssd_conv_b"]), name="ssd_conv_fwd")
    bias_row = jnp.pad(wts["ssd_dt_bias"], (0, LANES - SSD_N_HEADS)).reshape(1, LANES)
    alog_row = jnp.pad(wts["ssd_a_log"], (0, LANES - SSD_N_HEADS)).reshape(1, LANES)
    dtr, cumr = _softplus_fwd(dt_raw, bias_row, alog_row, name="ssd_dt_fwd")
    alog_b, d_b = _head_param_rows(wts["ssd_a_log"]), _head_param_rows(wts["ssd_d"])
    y_ssd, states = gathering("scan", _ssd_fwd, xc, dtr, cumr, alog_b, d_b, name="ssd_scan_fwd")
    norm_w = row(wts["ssd_norm_w"])
    yn = _gate_norm_fwd(y_ssd, zx, norm_w, name="ssd_gate_norm_fwd")
    mix0 = _matmul(yn, wts["ssd_w_out"], mode="nn", out_dtypes=(BF16,), tk=BIG_TILE, name="ssd_out_proj")
    h1, v0 = _rms_fwd(mix0, row(mix_post[0]), resid=x, want_u=row(ffn_pre[0]), name="rms_post_mix0")
    act0, f0 = _mlp_fwd(v0, wts["mlp_w_up0"], wts["mlp_w_down0"], "l0", run=gathering)
    h2, u1 = _rms_fwd(f0, row(ffn_post[0]), resid=h1, want_u=row(mix_pre[1]), name="rms_post_ffn0")

    qkv = _matmul(u1, wts["attn_w_qkv"], mode="nn", out_dtypes=(BF16,), epilogue=_bias_epilogue,
                  extras=((row(wts["attn_b_qkv"]), "row"),), b_shards=True, name="attn_qkv_proj")
    sinks_b = jnp.broadcast_to(wts["attn_sinks"].reshape(ATTN_N_Q, 1), (ATTN_N_Q, LANES))
    ao = gathering("attn_fwd", _attn_fwd, qkv, sinks_b, name="attn_fwd")
    mix1 = _matmul(ao, wts["attn_w_o"], mode="nn", out_dtypes=(BF16,), epilogue=_bias_epilogue,
                   extras=((row(wts["attn_b_o"]), "row"),), name="attn_out_proj")
    h3, v1 = _rms_fwd(mix1, row(mix_post[1]), resid=h2, want_u=row(ffn_pre[1]), name="rms_post_mix1")
    act1, f1 = _mlp_fwd(v1, wts["mlp_w_up1"], wts["mlp_w_down1"], "l1")
    dh4, loss_tile = _rms_fwd(f1, row(ffn_post[1]), resid=h3, target=target, name="rms_post_ffn1_loss")

    df1, g_ffn_post1 = _rms_bwd(f1, row(ffn_post[1]), dh4, out_dtype=BF16, name="rms_post_ffn1_bwd")
    dv1, g_up1, g_down1 = _mlp_bwd(v1, act1, wts["mlp_w_up1"], wts["mlp_w_down1"], df1, "l1")
    dh3, g_ffn_pre1 = _rms_bwd(h3, row(ffn_pre[1]), dv1, resid=dh4, name="rms_pre_ffn1_bwd")
    dmix1, g_mix_post1, g_b_o = _rms_bwd(mix1, row(mix_post[1]), dh3, out_dtype=BF16, dx_col_sum=True, name="rms_post_mix1_bwd")
    g_w_o = _matmul(ao, dmix1, mode="tn", out_dtypes=(BF16,), tk=BIG_TILE, name="attn_dwo")
    dao = _matmul(dmix1, wts["attn_w_o"], mode="nt", out_dtypes=(BF16,), name="attn_dao")
    dq, dkv, g_sinks = _attn_bwd(qkv, sinks_b, dao, name="attn_bwd")
    dqkv = jnp.concatenate([dq, dkv], axis=1)
    g_b_qkv = _col_sum(dqkv, name="attn_bqkv_grad")
    g_w_qkv = _matmul(u1, dqkv, mode="tn", out_dtypes=(BF16,), tn=ATTN_QKV // N_CHIPS, out_shards=True, tk=BIG_TILE, name="attn_dwqkv")
    du1 = _matmul(dqkv, wts["attn_w_qkv"], mode="nt", out_dtypes=(F32,), b_shards=True, name="attn_du")
    dh2, g_mix_pre1 = _rms_bwd(h2, row(mix_pre[1]), du1, resid=dh3, name="rms_pre_mix1_bwd")

    df0, g_ffn_post0 = _rms_bwd(f0, row(ffn_post[0]), dh2, out_dtype=BF16, name="rms_post_ffn0_bwd")
    dv0, g_up0, g_down0 = _mlp_bwd(v0, act0, wts["mlp_w_up0"], wts["mlp_w_down0"], df0, "l0")
    dh1, g_ffn_pre0 = _rms_bwd(h1, row(ffn_pre[0]), dv0, resid=dh2, name="rms_pre_ffn0_bwd")
    dmix0, g_mix_post0 = _rms_bwd(mix0, row(mix_post[0]), dh1, out_dtype=BF16, name="rms_post_mix0_bwd")
    g_w_out = _matmul(yn, dmix0, mode="tn", out_dtypes=(BF16,), tk=BIG_TILE, name="ssd_dwout")
    dyn = _matmul(dmix0, wts["ssd_w_out"], mode="nt", out_dtypes=(BF16,), name="ssd_dyn")
    dy_ssd, dzx, g_norm_w = _gate_norm_bwd(y_ssd, zx, norm_w, dyn, name="ssd_gate_norm_bwd")
    mats = {"ssd_w_out": g_w_out, "attn_w_qkv": g_w_qkv, "attn_w_o": g_w_o,
            "mlp_w_up0": g_up0, "mlp_w_up1": g_up1, "mlp_w_down0": g_down0, "mlp_w_down1": g_down1}
    if comm is None:
        dxc, dbm, dcm, ddt_r, dpar = _ssd_bwd(xc, dtr, cumr, alog_b, d_b, states, dy_ssd, name="ssd_scan_bwd")
    else:
        (dxc, dbm, dcm, ddt_r, dpar), received = _ssd_bwd(xc, dtr, cumr, alog_b, d_b, states, dy_ssd,
                                                          name="ssd_scan_bwd", hook=comm.exchange_hook(mats, "early"))
        comm.received(received)
    dzx, g_conv_w, g_conv_b = _conv_bwd(zx, wts["ssd_conv_w"], row(wts["ssd_conv_b"]), dxc, dbm, dcm, dzx, name="ssd_conv_bwd")
    dzx, g_dt_bias = _softplus_bwd(dt_raw, bias_row, ddt_r, dzx, name="ssd_dt_bwd")
    g_w_in = _w_in_to_shards(_matmul(u0, dzx, mode="tn", out_dtypes=(BF16,), tn=SSD_IN_TILE, tk=BIG_TILE, name="ssd_dwin"), name="ssd_dwin_shards")
    mats["ssd_w_in"] = g_w_in
    if comm is None:
        du0 = _matmul(dzx, wts["ssd_w_in"], mode="nt", out_dtypes=(F32,), tk=SSD_IN_TILE, name="ssd_du")
    else:
        du0, received = _matmul(dzx, wts["ssd_w_in"], mode="nt", out_dtypes=(F32,), tk=SSD_IN_TILE, name="ssd_du",
                                hook=comm.exchange_hook(mats, "late"))
        comm.received(received)
    grad_x, g_mix_pre0 = _rms_bwd(x, row(mix_pre[0]), du0, resid=dh1, name="rms_pre_mix0_bwd")

    dpar = dpar.reshape(SSD_N_HEADS, LANES)
    vecs = {
        "ssd_conv_w": g_conv_w, "ssd_conv_b": g_conv_b.reshape(-1),
        "ssd_dt_bias": g_dt_bias[0, :SSD_N_HEADS], "ssd_a_log": dpar[:, 0], "ssd_d": dpar[:, 1],
        "ssd_norm_w": g_norm_w.reshape(-1), "attn_b_qkv": g_b_qkv.reshape(-1), "attn_sinks": g_sinks[:, 0],
        "attn_b_o": g_b_o.reshape(-1),
        "mix_pre_norm": jnp.concatenate([g_mix_pre0, g_mix_pre1]), "mix_post_norm": jnp.concatenate([g_mix_post0, g_mix_post1]),
        "ffn_pre_norm": jnp.concatenate([g_ffn_pre0, g_ffn_pre1]), "ffn_post_norm": jnp.concatenate([g_ffn_post0, g_ffn_post1]),
    }
    return loss_tile, grad_x, mats, vecs


def _mesh_position():
    return lax.axis_index("x"), lax.axis_index("y"), lax.axis_index("c")


def _flip(v, bit):
    return 1 - v if bit else v


OTHER_CHIPS = ((1, 0), (0, 1), (1, 1))


def _comm_params():
    return pltpu.CompilerParams(vmem_limit_bytes=VMEM_LIMIT)


def _staged_copies(srcs, dsts, bufs, sems_in, sems_out):
    loads = [pltpu.make_async_copy(s, b, sems_in.at[i]) for i, (s, b) in enumerate(zip(srcs, bufs))]
    stores = [pltpu.make_async_copy(b, d, sems_out.at[i]) for i, (b, d) in enumerate(zip(bufs, dsts))]
    return loads, stores


class _GatherHook:
    def __init__(self, mats, vecs=()):
        self.arrs = list(mats) + list(vecs)
        self.nm, self.n = len(mats), len(self.arrs)
        n_ici, n_fwd = (N_CHIPS - 1) * self.n, max((N_CHIPS - 1) * self.nm, 1)
        dma = pltpu.SemaphoreType.DMA
        self.out_shape = [jax.ShapeDtypeStruct((N_CHIPS,) + a.shape, a.dtype) for a in self.arrs]
        self.scratch = [pltpu.VMEM(a.shape, a.dtype) for a in self.arrs] + [
            dma((n_ici,)), dma((n_ici,)), dma((n_fwd,)), dma((n_fwd,)), dma((self.n,)), dma((self.n,))]

    def plan(self, ins, outs, scratch):
        n, nm = self.n, self.nm
        bufs = scratch[:n]
        ici_send, ici_recv, fwd_send, fwd_recv, load_sems, store_sems = scratch[n:]
        xi, yi, ci = _mesh_position()
        me = 2 * xi + yi
        loads, stores = _staged_copies(ins, [outs[i].at[me] for i in range(n)], bufs, load_sems, store_sems)
        sends, landed, forwards, from_sibling = [], [], [], []
        for j, (bx, by) in enumerate(OTHER_CHIPS):
            px, py = _flip(xi, bx), _flip(yi, by)
            peer = 2 * px + py
            for i in range(n):
                k = j * n + i
                mk = functools.partial(pltpu.make_async_remote_copy, send_sem=ici_send.at[k], recv_sem=ici_recv.at[k],
                                       device_id=(px, py, ci), device_id_type=MESH)
                if i < nm:
                    sends.append(mk(src_ref=ins[i].at[ci], dst_ref=outs[i].at[me, ci]))
                    landed.append(mk(src_ref=ins[i].at[ci], dst_ref=outs[i].at[peer, ci]))
                    kf = j * nm + i
                    fw = functools.partial(pltpu.make_async_remote_copy, send_sem=fwd_send.at[kf], recv_sem=fwd_recv.at[kf],
                                           device_id=(xi, yi, 1 - ci), device_id_type=MESH)
                    forwards.append(fw(src_ref=outs[i].at[peer, ci], dst_ref=outs[i].at[peer, ci]))
                    from_sibling.append(fw(src_ref=outs[i].at[peer, ci], dst_ref=outs[i].at[peer, 1 - ci]))
                else:
                    sends.append(mk(src_ref=ins[i], dst_ref=outs[i].at[me]))
                    landed.append(mk(src_ref=ins[i], dst_ref=outs[i].at[peer]))
                    forwards.append(None)
        return loads, stores, sends, landed, forwards, from_sibling

    @staticmethod
    def start(p):
        loads, _, sends, _, _, _ = p
        for cp in loads + sends:
            cp.start()

    @staticmethod
    def relay(p):
        loads, stores, _, landed, forwards, _ = p
        for ld, st in zip(loads, stores):
            ld.wait()
            st.start()
        for cp, fw in zip(landed, forwards):
            cp.wait_recv()
            if fw is not None:
                fw.start()

    @staticmethod
    def finish(p):
        _, stores, sends, _, forwards, from_sibling = p
        for cp in from_sibling:
            cp.wait_recv()
        for cp in sends + [fw for fw in forwards if fw is not None]:
            cp.wait_send()
        for st in stores:
            st.wait()


def _run_hook(hook, ins, outs, scratch, step, n_steps):
    p = hook.plan(ins, outs, scratch)
    relay_step = min(max(1, (3 * n_steps) // 4), n_steps - 1)

    @pl.when(step == 0)
    def _():
        hook.start(p)

    if relay_step < n_steps - 1:
        @pl.when(step == relay_step)
        def _():
            hook.relay(p)

    @pl.when(step == n_steps - 1)
    def _():
        if relay_step == n_steps - 1:
            hook.relay(p)
        hook.finish(p)


def _hook_call(hook, *, name):
    n = len(hook.arrs)

    def body(*refs):
        p = hook.plan(refs[:n], refs[n:n + len(hook.out_shape)], refs[n + len(hook.out_shape):])
        hook.start(p)
        hook.relay(p)
        hook.finish(p)

    return pl.pallas_call(
        body, in_specs=[ANY] * n, out_specs=[ANY] * len(hook.out_shape), out_shape=hook.out_shape,
        scratch_shapes=hook.scratch, compiler_params=_comm_params(), name=name)(*hook.arrs)


def _send_other_half(parts, *, name):
    n = len(parts)

    def body(*refs):
        ins, outs = refs[:n], refs[n:2 * n]
        send_sems, recv_sems = refs[2 * n:]
        xi, yi, ci = _mesh_position()
        sibling = (xi, yi, 1 - ci)
        for i in range(n):
            for s in range(N_CHIPS):
                pltpu.make_async_remote_copy(src_ref=ins[i].at[s, 1 - ci], dst_ref=outs[i].at[s], send_sem=send_sems.at[i],
                                             recv_sem=recv_sems.at[i], device_id=sibling, device_id_type=MESH).start()
        for i in range(n):
            pltpu.make_async_remote_copy(src_ref=outs[i], dst_ref=outs[i], send_sem=send_sems.at[i], recv_sem=recv_sems.at[i],
                                         device_id=sibling, device_id_type=MESH).wait()

    return pl.pallas_call(
        body, in_specs=[ANY] * n, out_specs=[ANY] * n,
        out_shape=[jax.ShapeDtypeStruct((p.shape[0],) + p.shape[2:], p.dtype) for p in parts],
        scratch_shapes=[pltpu.SemaphoreType.DMA((n,)), pltpu.SemaphoreType.DMA((n,))],
        name=name)(*parts)


ROW_BLOCKS = 8


def _add_sibling_half(parts, theirs, core, *, name):
    n = len(parts)

    def body(core_ref, *refs):
        for a_ref, b_ref, o_ref in zip(refs[:n], refs[n:2 * n], refs[2 * n:]):
            o_ref[...] = (a_ref[...].astype(F32) + b_ref[...].astype(F32)).astype(o_ref.dtype)

    mine = lambda p: pl.BlockSpec((None, None, p.shape[2] // ROW_BLOCKS, p.shape[3]), lambda s, rb, core_ref: (s, core_ref[0], rb, 0))
    other = lambda p: pl.BlockSpec((None, p.shape[1] // ROW_BLOCKS, p.shape[2]), lambda s, rb, core_ref: (s, rb, 0))
    return pl.pallas_call(
        body,
        grid_spec=pltpu.PrefetchScalarGridSpec(
            num_scalar_prefetch=1, grid=(N_CHIPS, ROW_BLOCKS),
            in_specs=[mine(p) for p in parts] + [other(q) for q in theirs], out_specs=[other(q) for q in theirs]),
        out_shape=[jax.ShapeDtypeStruct(q.shape, BF16) for q in theirs],
        compiler_params=_params("parallel", "parallel"), name=name)(core, *parts, *theirs)


class _ExchangeHook:
    def __init__(self, parts, to_all=()):
        self.arrs = list(parts) + list(to_all)
        self.n_parts, self.n = len(parts), len(self.arrs)
        n_ici, n_peer = max((N_CHIPS - 1) * self.n_parts, 1), (N_DEV - 1) * max(len(to_all), 1)
        dma = pltpu.SemaphoreType.DMA
        self.out_shape = [jax.ShapeDtypeStruct(p.shape, p.dtype) for p in parts] + [
            jax.ShapeDtypeStruct((N_DEV,) + a.shape, a.dtype) for a in to_all]
        self.scratch = [pltpu.VMEM(p.shape[1:], p.dtype) for p in parts] + [pltpu.VMEM(a.shape, a.dtype) for a in to_all] + [
            dma((n_ici,)), dma((n_ici,)), dma((n_peer,)), dma((n_peer,)), dma((self.n,)), dma((self.n,))]

    def plan(self, ins, outs, scratch):
        n, npt = self.n, self.n_parts
        bufs = scratch[:n]
        send_sems, recv_sems, all_send, all_recv, load_sems, store_sems = scratch[n:]
        xi, yi, ci = _mesh_position()
        me_chip = 2 * xi + yi
        me = 4 * xi + 2 * yi + ci
        loads, stores = _staged_copies([ins[i].at[me_chip] for i in range(npt)] + list(ins[npt:]),
                                       [outs[i].at[me_chip] for i in range(npt)] + [outs[i].at[me] for i in range(npt, n)],
                                       bufs, load_sems, store_sems)
        sends, recvs = [], []
        for j, (bx, by) in enumerate(OTHER_CHIPS):
            px, py = _flip(xi, bx), _flip(yi, by)
            peer = 2 * px + py
            for i in range(npt):
                k = j * npt + i
                mk = functools.partial(pltpu.make_async_remote_copy, src_ref=ins[i].at[peer], send_sem=send_sems.at[k],
                                       recv_sem=recv_sems.at[k], device_id=(px, py, ci), device_id_type=MESH)
                sends.append(mk(dst_ref=outs[i].at[me_chip]))
                recvs.append(mk(dst_ref=outs[i].at[peer]))
        for i in range(npt, n):
            for k in range(1, N_DEV):
                px, py, pc = _flip(xi, (k >> 2) & 1), _flip(yi, (k >> 1) & 1), _flip(ci, k & 1)
                slot = (i - npt) * (N_DEV - 1) + k - 1
                mk = functools.partial(pltpu.make_async_remote_copy, src_ref=ins[i], send_sem=all_send.at[slot],
                                       recv_sem=all_recv.at[slot], device_id=(px, py, pc), device_id_type=MESH)
                sends.append(mk(dst_ref=outs[i].at[me]))
                recvs.append(mk(dst_ref=outs[i].at[4 * px + 2 * py + pc]))
        return loads, stores, sends, recvs

    @staticmethod
    def start(p):
        loads, _, sends, _ = p
        for cp in loads + sends:
            cp.start()

    @staticmethod
    def relay(p):
        loads, stores, _, _ = p
        for ld, st in zip(loads, stores):
            ld.wait()
            st.start()

    @staticmethod
    def finish(p):
        _, stores, sends, recvs = p
        for cp in recvs:
            cp.wait_recv()
        for cp in sends:
            cp.wait_send()
        for st in stores:
            st.wait()


def _sum_chips(parts, *, name):
    n = len(parts)
    p = parts[0].shape[0]

    def body(*refs):
        s = pl.program_id(1)
        for x_ref, o_ref in zip(refs[:n], refs[n:]):
            @pl.when(s == 0)
            def _():
                o_ref[...] = x_ref[...].astype(F32)

            @pl.when(s > 0)
            def _():
                o_ref[...] += x_ref[...].astype(F32)

    blocks = lambda q: ROW_BLOCKS if q.shape[1] % (8 * ROW_BLOCKS) == 0 else 1
    assert len({blocks(q) for q in parts}) == 1
    nb = blocks(parts[0])
    return pl.pallas_call(
        body, grid=(nb, p),
        in_specs=[pl.BlockSpec((None, q.shape[1] // nb, q.shape[2]), lambda rb, s: (s, rb, 0)) for q in parts],
        out_specs=[pl.BlockSpec((q.shape[1] // nb, q.shape[2]), lambda rb, s: (rb, 0)) for q in parts],
        out_shape=[jax.ShapeDtypeStruct(q.shape[1:], F32) for q in parts],
        compiler_params=_params("parallel", "arbitrary"), name=name)(*parts)


def _swap_halves(halves, layers, *, name, hook=None):
    n = len(halves)
    out_shapes, slots = [], []
    for i, h in enumerate(halves):
        pair = [p for p in layers if i in p]
        if pair and pair[0][1] == i:
            slots.append((slots[pair[0][0]][0], 1))
        elif pair:
            out_shapes.append(jax.ShapeDtypeStruct((2, 2) + h.shape, h.dtype))
            slots.append((len(out_shapes) - 1, 0))
        else:
            out_shapes.append(jax.ShapeDtypeStruct((2,) + h.shape, h.dtype))
            slots.append((len(out_shapes) - 1, None))
    n_out = len(out_shapes)
    hk = _HookSlots(hook, n_in=n, n_out=n_out, n_scratch=n + 4)

    def body(*refs):
        ins, outs, scratch = hk.own(refs)
        bufs = scratch[:n]
        send_sems, recv_sems, load_sems, store_sems = scratch[n:]
        if hook is not None:
            _, h_in, _, h_out, _, h_scratch = hk._split(refs)
            extra = hook.plan(h_in, h_out, h_scratch)
            hook.start(extra)
        xi, yi, ci = _mesh_position()
        own, sends, recvs = [], [], []
        for i in range(n):
            o, layer = slots[i]
            dst = (lambda core: outs[o].at[core]) if layer is None else (lambda core: outs[o].at[layer, core])
            own.append(dst(ci))
            mk = functools.partial(pltpu.make_async_remote_copy, src_ref=ins[i], send_sem=send_sems.at[i],
                                   recv_sem=recv_sems.at[i], device_id=(xi, yi, 1 - ci), device_id_type=MESH)
            sends.append(mk(dst_ref=dst(ci)))
            recvs.append(mk(dst_ref=dst(1 - ci)))
        loads, stores = _staged_copies(ins, own, bufs, load_sems, store_sems)
        for cp in loads + sends:
            cp.start()
        for ld, st in zip(loads, stores):
            ld.wait()
            st.start()
        for cp in recvs:
            cp.wait_recv()
        for cp in sends:
            cp.wait_send()
        for st in stores:
            st.wait()
        if hook is not None:
            hook.relay(extra)
            hook.finish(extra)

    outs = pl.pallas_call(
        body, in_specs=[ANY] * n + hk.in_specs, out_specs=[ANY] * n_out + hk.out_specs, out_shape=out_shapes + hk.out_shape,
        scratch_shapes=[pltpu.VMEM(h.shape, h.dtype) for h in halves]
        + [pltpu.SemaphoreType.DMA((n,)), pltpu.SemaphoreType.DMA((n,)), pltpu.SemaphoreType.DMA((n,)), pltpu.SemaphoreType.DMA((n,))]
        + hk.scratch,
        compiler_params=_comm_params(), name=name)(*halves, *hk.inputs)
    return outs if hook is None else (outs[:n_out], outs[n_out:])


def _cast_bf16(layers, x, norm_w, *, name, hook=None):
    n = len(layers)
    hk = _HookSlots(hook, n_in=n + 2, n_out=n + 1, n_scratch=0)

    def body(*refs):
        ins, outs, _ = hk.own(refs)
        if hook is not None:
            hk.run(refs, pl.program_id(0), ROW_BLOCKS)
        for i_ref, o_ref in zip(ins[:n], outs[:n]):
            o_ref[...] = i_ref[...].astype(o_ref.dtype)
        xv = ins[n][...]
        outs[n][...] = (xv * lax.rsqrt(jnp.mean(xv * xv, axis=-1, keepdims=True) + NORM_EPS) * ins[n + 1][...]).astype(BF16)

    in_blk = lambda a, l: pl.BlockSpec((None, a.shape[1] // ROW_BLOCKS, a.shape[2]), lambda i: (l, i, 0))
    out_blk = lambda a: pl.BlockSpec((a.shape[1] // ROW_BLOCKS, a.shape[2]), lambda i: (i, 0))
    x_blk = pl.BlockSpec((x.shape[0] // ROW_BLOCKS, x.shape[1]), lambda i: (i, 0))
    outs = pl.pallas_call(
        body, grid=(ROW_BLOCKS,),
        in_specs=[in_blk(a, l) for a, l in layers] + [x_blk, pl.BlockSpec((1, x.shape[1]), lambda i: (0, 0))] + hk.in_specs,
        out_specs=[out_blk(a) for a, _ in layers] + [x_blk] + hk.out_specs,
        out_shape=[jax.ShapeDtypeStruct(a.shape[1:], BF16) for a, _ in layers] + [jax.ShapeDtypeStruct(x.shape, BF16)] + hk.out_shape,
        scratch_shapes=hk.scratch,
        compiler_params=_params(*hk.semantics("parallel")), name=name)(*[a for a, _ in layers], x, norm_w, *hk.inputs)
    own = (outs[:n], outs[n])
    return own if hook is None else (own, outs[n + 1:])


def _full_weight(name, gathered):
    s, _, r, c = gathered.shape
    if name == "ssd_w_in":
        return _w_in_from_shards(gathered.reshape(s, 2 * r, c), name="ssd_w_in_unshard")
    if name in ("attn_w_qkv", "mlp_w_up0", "mlp_w_up1"):
        return gathered.reshape(s, 2 * r, c)
    return gathered.reshape(s * 2 * r, c)


class _StepComm:
    GATHER = {"in_proj": ("mlp_w_up0", "attn_w_o"), "conv": ("mlp_w_down0",), "scan": ("ssd_w_out", "mlp_w_up1"),
              "mlp_up_l0": ("attn_w_qkv",), "attn_fwd": ("mlp_w_down1",)}
    EXCHANGE = {"early": ("ssd_w_out", "attn_w_qkv", "attn_w_o", "mlp_w_up0", "mlp_w_up1", "mlp_w_down0", "mlp_w_down1"),
                "late": ("ssd_w_in",)}

    def __init__(self, shards, core):
        self.shards, self.core = shards, core
        self.chip_parts = {}
        self._pending = None

    def gather_hook(self, stage):
        names = self.GATHER.get(stage)
        return _GatherHook([self.shards[n] for n in names]) if names else None

    def weights_from(self, stage, gathered):
        return {n: _full_weight(n, g) for n, g in zip(self.GATHER[stage], gathered)}

    def chip_sums(self, mats, tag):
        parts = [_shard_halves(a) for a in mats.values()]
        theirs = _send_other_half(parts, name=f"grad_sibling_send_{tag}")
        return _add_sibling_half(parts, theirs, self.core, name=f"grad_chip_sum_{tag}")

    def exchange_hook(self, mats, which):
        self._pending = self.EXCHANGE[which]
        return _ExchangeHook(self.chip_sums({n: mats[n] for n in self._pending}, which))

    def received(self, arrays):
        self.chip_parts.update(zip(self._pending, arrays))


ADAMW_ROW_BLOCKS = 16


def _adamw(ws, gs, ms, vs, *, name, by_lanes=False):
    n = len(ws)
    if by_lanes:
        nb = min(a.shape[2] for a in ws) // LANES
    else:
        nb = ADAMW_ROW_BLOCKS if all(a.shape[1] % (8 * ADAMW_ROW_BLOCKS) == 0 for a in ws) else 1

    def body(*refs):
        ins, outs = refs[:4 * n], refs[4 * n:]
        for i in range(n):
            w_ref, g_ref, m_ref, v_ref = ins[i], ins[n + i], ins[2 * n + i], ins[3 * n + i]
            go_ref, d_ref, nm_ref, nv_ref = outs[i], outs[n + i], outs[2 * n + i], outs[3 * n + i]
            gv = g_ref[...]
            nm = ADAM_B1 * m_ref[...] + (1.0 - ADAM_B1) * gv
            nv = ADAM_B2 * v_ref[...] + (1.0 - ADAM_B2) * (gv * gv)
            m_hat = nm / (1.0 - ADAM_B1 ** ADAM_STEP)
            v_hat = nv / (1.0 - ADAM_B2 ** ADAM_STEP)
            go_ref[...] = gv
            d_ref[...] = -ADAM_LR * (m_hat / (jnp.sqrt(v_hat) + ADAM_EPS) + ADAM_WD * w_ref[...])
            nm_ref[...] = nm
            nv_ref[...] = nv

    if by_lanes:
        blks = [pl.BlockSpec((a.shape[0], a.shape[1], a.shape[2] // nb), lambda i: (0, 0, i)) for a in ws]
    else:
        blks = [pl.BlockSpec((a.shape[0], a.shape[1] // nb, a.shape[2]), lambda i: (0, i, 0)) for a in ws]
    shapes = [jax.ShapeDtypeStruct(a.shape, F32) for a in ws]
    outs = pl.pallas_call(body, grid=(nb,), in_specs=blks * 4, out_specs=blks * 4, out_shape=shapes * 4,
                          compiler_params=_params("parallel"), name=name)(*ws, *gs, *ms, *vs)
    return [tuple(outs[k * n + i] for k in range(4)) for i in range(n)]


SM_CONV_B, SM_NORM_W, SM_MIX_PRE, SM_MIX_POST, SM_FFN_PRE, SM_FFN_POST, SM_MISC, SM_CONV_W, SM_B_QKV, SM_B_O = 0, 4, 6, 8, 10, 12, 14, 16, 32, 34
SM_ROWS = 40
MISC_DT_BIAS, MISC_A_LOG, MISC_D, MISC_SINKS, MISC_LOSS = 0, 32, 64, 96, 112


def _shard_halves(a):
    c = a.shape[-1]
    return a.reshape(N_CHIPS, 2, -1, c)


def _rows(v):
    return v.reshape(-1, D_MODEL)


def _misc_row(dt_bias, a_log, d, sinks, loss):
    pad = jnp.zeros((D_MODEL - MISC_LOSS - 1,), F32)
    return jnp.concatenate([dt_bias.reshape(-1), a_log.reshape(-1), d.reshape(-1), sinks.reshape(-1), loss.reshape(1), pad]).reshape(1, D_MODEL)


def _replicated_rows(p, loss):
    return jnp.concatenate([
        _rows(p["ssd_conv_b"]), _rows(p["ssd_norm_w"]), _rows(p["mix_pre_norm"]), _rows(p["mix_post_norm"]),
        _rows(p["ffn_pre_norm"]), _rows(p["ffn_post_norm"]),
        _misc_row(p["ssd_dt_bias"], p["ssd_a_log"], p["ssd_d"], p["attn_sinks"], loss), jnp.zeros((1, D_MODEL), F32)], axis=0)


def _sharded_rows(conv_w, b_qkv, b_o):
    last = jnp.concatenate([b_qkv.reshape(-1), b_o.reshape(-1), jnp.zeros((D_MODEL - 640,), F32)]).reshape(1, D_MODEL)
    return jnp.concatenate([conv_w.reshape(SSD_CONV_WIDTH, D_MODEL), last, jnp.zeros((3, D_MODEL), F32)], axis=0)


REPLICATED = ("ssd_conv_b", "ssd_dt_bias", "ssd_a_log", "ssd_d", "ssd_norm_w", "attn_sinks",
              "mix_pre_norm", "mix_post_norm", "ffn_pre_norm", "ffn_post_norm")
MATRICES = ("ssd_w_in", "ssd_w_out", "attn_w_qkv", "attn_w_o", "mlp_w_up", "mlp_w_down")
WEIGHT_NAMES = ("ssd_w_in", "ssd_conv_w", "ssd_conv_b", "ssd_dt_bias", "ssd_a_log", "ssd_d", "ssd_norm_w", "ssd_w_out",
                "attn_w_qkv", "attn_b_qkv", "attn_sinks", "attn_w_o", "attn_b_o", "mlp_w_up", "mlp_w_down",
                "mix_pre_norm", "mix_post_norm", "ffn_pre_norm", "ffn_post_norm")


def _unpack_small(rows16, rows8, like):
    misc = rows16[SM_MISC]
    out = {
        "ssd_conv_b": rows16[SM_CONV_B:SM_CONV_B + 4], "ssd_norm_w": rows16[SM_NORM_W:SM_NORM_W + 2],
        "mix_pre_norm": rows16[SM_MIX_PRE:SM_MIX_PRE + 2], "mix_post_norm": rows16[SM_MIX_POST:SM_MIX_POST + 2],
        "ffn_pre_norm": rows16[SM_FFN_PRE:SM_FFN_PRE + 2], "ffn_post_norm": rows16[SM_FFN_POST:SM_FFN_POST + 2],
        "ssd_dt_bias": misc[MISC_DT_BIAS:MISC_DT_BIAS + 32], "ssd_a_log": misc[MISC_A_LOG:MISC_A_LOG + 32],
        "ssd_d": misc[MISC_D:MISC_D + 32], "attn_sinks": misc[MISC_SINKS:MISC_SINKS + 16],
        "ssd_conv_w": rows8[0:SSD_CONV_WIDTH], "attn_b_qkv": rows8[SSD_CONV_WIDTH, 0:384], "attn_b_o": rows8[SSD_CONV_WIDTH, 384:640],
    }
    return {k: v.reshape(like[k].shape) for k, v in out.items()}


def kernel(x, ssd_w_in, ssd_conv_w, ssd_conv_b, ssd_dt_bias, ssd_a_log, ssd_d, ssd_norm_w, ssd_w_out, attn_w_qkv, attn_b_qkv, attn_sinks, attn_w_o, attn_b_o, mlp_w_up, mlp_w_down, mix_pre_norm, mix_post_norm, ffn_pre_norm, ffn_post_norm, loss_target, m_ssd_w_in, m_ssd_conv_w, m_ssd_conv_b, m_ssd_dt_bias, m_ssd_a_log, m_ssd_d, m_ssd_norm_w, m_ssd_w_out, m_attn_w_qkv, m_attn_b_qkv, m_attn_sinks, m_attn_w_o, m_attn_b_o, m_mlp_w_up, m_mlp_w_down, m_mix_pre_norm, m_mix_post_norm, m_ffn_pre_norm, m_ffn_post_norm, v_ssd_w_in, v_ssd_conv_w, v_ssd_conv_b, v_ssd_dt_bias, v_ssd_a_log, v_ssd_d, v_ssd_norm_w, v_ssd_w_out, v_attn_w_qkv, v_attn_b_qkv, v_attn_sinks, v_attn_w_o, v_attn_b_o, v_mlp_w_up, v_mlp_w_down, v_mix_pre_norm, v_mix_post_norm, v_ffn_pre_norm, v_ffn_post_norm):
    w = dict(zip(WEIGHT_NAMES, (ssd_w_in, ssd_conv_w, ssd_conv_b, ssd_dt_bias, ssd_a_log, ssd_d, ssd_norm_w, ssd_w_out, attn_w_qkv, attn_b_qkv, attn_sinks, attn_w_o, attn_b_o, mlp_w_up, mlp_w_down, mix_pre_norm, mix_post_norm, ffn_pre_norm, ffn_post_norm)))
    m = dict(zip(WEIGHT_NAMES, (m_ssd_w_in, m_ssd_conv_w, m_ssd_conv_b, m_ssd_dt_bias, m_ssd_a_log, m_ssd_d, m_ssd_norm_w, m_ssd_w_out, m_attn_w_qkv, m_attn_b_qkv, m_attn_sinks, m_attn_w_o, m_attn_b_o, m_mlp_w_up, m_mlp_w_down, m_mix_pre_norm, m_mix_post_norm, m_ffn_pre_norm, m_ffn_post_norm)))
    v = dict(zip(WEIGHT_NAMES, (v_ssd_w_in, v_ssd_conv_w, v_ssd_conv_b, v_ssd_dt_bias, v_ssd_a_log, v_ssd_d, v_ssd_norm_w, v_ssd_w_out, v_attn_w_qkv, v_attn_b_qkv, v_attn_sinks, v_attn_w_o, v_attn_b_o, v_mlp_w_up, v_mlp_w_down, v_mix_pre_norm, v_mix_post_norm, v_ffn_pre_norm, v_ffn_post_norm)))
    chip = 2 * lax.axis_index("x") + lax.axis_index("y")

    two_halves = lambda a: a.reshape(2, a.shape[-2] // 2, a.shape[-1])
    later = {"ssd_w_out": (w["ssd_w_out"], 0), "attn_w_qkv": (w["attn_w_qkv"], 0), "attn_w_o": (w["attn_w_o"], 0),
             "mlp_w_up0": (w["mlp_w_up"], 0), "mlp_w_up1": (w["mlp_w_up"], 1),
             "mlp_w_down0": (w["mlp_w_down"], 0), "mlp_w_down1": (w["mlp_w_down"], 1)}
    first = _GatherHook([two_halves(w["ssd_w_in"].astype(BF16))], [w["ssd_conv_w"][0], w["attn_b_qkv"], w["attn_b_o"]])
    (cast, u0), (g_in, g_conv, g_bqkv, g_bo) = _cast_bf16(list(later.values()), x[0], w["mix_pre_norm"][0:1],
                                                          name="weights_to_bf16", hook=first)
    core = lax.axis_index("c").astype(jnp.int32).reshape(1)
    comm = _StepComm({k: two_halves(a) for k, a in zip(later, cast)}, core)
    full = {
        "ssd_w_in": _full_weight("ssd_w_in", g_in),
        "ssd_conv_w": g_conv.transpose(1, 0, 2).reshape(SSD_CONV_WIDTH, SSD_CONV_DIM),
        "attn_b_qkv": g_bqkv.reshape(ATTN_QKV), "attn_b_o": g_bo.reshape(D_MODEL),
    }
    for name in REPLICATED:
        full[name] = w[name][0] if name.startswith(("ssd_", "attn_")) else w[name]

    loss_tile, grad_x, gm, g = _local_step(x[0], loss_target[0], full, comm, u0)

    conv_w_rows = g["ssd_conv_w"].reshape(SSD_CONV_WIDTH * N_CHIPS, D_MODEL)
    b_qkv_rows = jnp.pad(g["attn_b_qkv"], (0, 2 * D_MODEL - ATTN_QKV)).reshape(2, D_MODEL)
    small = jnp.concatenate([_replicated_rows(g, loss_tile[0, 0]), conv_w_rows, b_qkv_rows, _rows(g["attn_b_o"]),
                             jnp.zeros((SM_ROWS - SM_B_O - 1, D_MODEL), F32)], axis=0)
    order = ("ssd_w_in", "ssd_w_out", "attn_w_qkv", "attn_w_o", "mlp_w_up0", "mlp_w_up1", "mlp_w_down0", "mlp_w_down1")
    halves = _sum_chips([comm.chip_parts[k] for k in order], name="grad_sum")
    (r_in, r_out, r_qkv, r_o, r_up, r_down), (small_all,) = _swap_halves(
        halves, layers=((4, 5), (6, 7)), hook=_ExchangeHook([], [small]), name="grad_halves_swap")
    small_sum, = _sum_chips([small_all], name="small_grad_sum")

    grads = {"ssd_w_in": r_in, "ssd_w_out": r_out, "attn_w_qkv": r_qkv, "attn_w_o": r_o, "mlp_w_up": r_up, "mlp_w_down": r_down}
    grads = {k: a.reshape(w[k].shape) for k, a in grads.items()}
    conv_w_g = lax.dynamic_index_in_dim(small_sum[SM_CONV_W:SM_CONV_W + 16].reshape(SSD_CONV_WIDTH, N_CHIPS, D_MODEL), chip, axis=1, keepdims=False)
    b_qkv_g = lax.dynamic_slice_in_dim(small_sum[SM_B_QKV:SM_B_QKV + 2].reshape(-1), chip * 384, 384)
    b_o_g = lax.dynamic_slice_in_dim(small_sum[SM_B_O], chip * 256, 256)
    small_g = jnp.concatenate([small_sum[0:16], _sharded_rows(conv_w_g, b_qkv_g, b_o_g)], axis=0)
    grads.update(_unpack_small(small_g[0:16], small_g[16:24], w))
    loss = small_sum[SM_MISC, MISC_LOSS]

    delta, new_m, new_v = {}, {}, {}
    stored = lambda a: jnp.swapaxes(a, 1, 2)
    rest = [name for name in MATRICES if name != "ssd_w_in"]
    mats = lambda p: [p[name] for name in rest]
    results = dict(zip(rest, _adamw(mats(w), mats(grads), mats(m), mats(v), name="adamw_matrices")))
    (w_in_result,) = _adamw([stored(w["ssd_w_in"])], [stored(grads["ssd_w_in"])], [stored(m["ssd_w_in"])],
                            [stored(v["ssd_w_in"])], by_lanes=True, name="adamw_ssd_w_in")
    results["ssd_w_in"] = tuple(stored(a) for a in w_in_result)
    for name in MATRICES:
        grads[name], delta[name], new_m[name], new_v[name] = results[name]
    zero = jnp.zeros((), F32)
    small_pack = lambda p: jnp.concatenate([_replicated_rows({k: p[k] for k in REPLICATED}, zero),
                                            _sharded_rows(p["ssd_conv_w"], p["attn_b_qkv"], p["attn_b_o"])], axis=0)[None]
    (_, d_s, m_s, v_s), = _adamw([small_pack(w)], [small_g[None]], [small_pack(m)], [small_pack(v)], name="adamw_vectors")
    d_s, m_s, v_s = d_s[0], m_s[0], v_s[0]
    delta.update(_unpack_small(d_s[0:16], d_s[16:24], w))
    new_m.update(_unpack_small(m_s[0:16], m_s[16:24], w))
    new_v.update(_unpack_small(v_s[0:16], v_s[16:24], w))

    return (loss, grad_x[None], *[grads[n] for n in WEIGHT_NAMES], *[delta[n] for n in WEIGHT_NAMES],
            *[new_m[n] for n in WEIGHT_NAMES], *[new_v[n] for n in WEIGHT_NAMES])
```

```python
import functools

import jax
import jax.numpy as jnp
from jax import lax
from jax.experimental import pallas as pl
from jax.experimental.pallas import tpu as pltpu

F32 = jnp.float32
BF16 = jnp.bfloat16

D_MODEL = 1024
SSD_D_INNER = 2048
SSD_HEAD_DIM = 64
SSD_N_HEADS = 32
SSD_N_GROUPS = 8
SSD_HPG = 4
SSD_D_STATE = 128
SSD_CONV_WIDTH = 4
SSD_CHUNK = 128
SSD_CONV_DIM = 4096
SSD_IN_DIM = 6176
SSD_IN_PAD = 6400
SSD_IN_TILE = 1280
SSD_DT_COL = 6144
SSD_GW = SSD_HPG * SSD_HEAD_DIM
ATTN_HEAD_DIM = 64
ATTN_N_Q = 16
ATTN_N_KV = 4
ATTN_REP = 4
ATTN_WINDOW = 128
ATTN_QKV = 1536
D_FF = 4096
NORM_EPS = 1e-6

ADAM_LR = 0.001
ADAM_B1 = 0.9
ADAM_B2 = 0.999
ADAM_EPS = 1e-08
ADAM_WD = 0.01
ADAM_STEP = 10

N_CHIPS = 4
N_DEV = 8
LANES = 128
VMEM_LIMIT = 48 * 1024 * 1024
BIG_TILE = 2048
MESH = pl.DeviceIdType.MESH


def _params(*sem):
    return pltpu.CompilerParams(dimension_semantics=sem, vmem_limit_bytes=VMEM_LIMIT)


def _dot(a, b, dims):
    return lax.dot_general(a, b, (dims, ((), ())), preferred_element_type=F32)


def _dot_nn(a, b):
    return _dot(a, b, ((1,), (0,)))


def _dot_nt(a, b):
    return _dot(a, b, ((1,), (1,)))


def _dot_tn(a, b):
    return _dot(a, b, ((0,), (0,)))


def _sigmoid(x):
    return 0.5 * jnp.tanh(0.5 * x) + 0.5


ANY = pl.BlockSpec(memory_space=pl.ANY)


class _HookSlots:
    def __init__(self, hook, n_in, n_out, n_scratch):
        self.hook = hook
        self.n_in, self.n_out, self.n_scratch = n_in, n_out, n_scratch
        self.inputs = list(hook.arrs) if hook else []
        self.out_shape = list(hook.out_shape) if hook else []
        self.scratch = list(hook.scratch) if hook else []
        self.in_specs = [ANY] * len(self.inputs)
        self.out_specs = [ANY] * len(self.out_shape)

    def _split(self, refs):
        a = self.n_in
        b = a + len(self.inputs)
        c = b + self.n_out
        d = c + len(self.out_shape)
        e = d + self.n_scratch
        return refs[:a], refs[a:b], refs[b:c], refs[c:d], refs[d:e], refs[e:]

    def own(self, refs):
        ins, _, outs, _, scratch, _ = self._split(refs)
        return ins, outs, scratch

    def run(self, refs, step, n_steps):
        _, h_in, _, h_out, _, h_scratch = self._split(refs)
        _run_hook(self.hook, h_in, h_out, h_scratch, step, n_steps)

    def semantics(self, *sem):
        return sem if self.hook is None else ("arbitrary",) * len(sem)


def _matmul(a, b, *, mode, out_dtypes, name, epilogue=None, extras=(), tm=1024, tn=1024, tk=1024,
            b_shards=False, out_shards=False, hook=None, f32_block=None):
    f32_tail = f32_block is not None
    if b_shards:
        s, b_rows, b_cols = b.shape
        b2 = (b_rows, s * b_cols)
        if mode == "nn":
            tn = b_cols
        else:
            assert mode == "nt"
            tk = b_cols
    else:
        b2 = b.shape
    if mode == "nn":
        (m, k), (k2, n) = a.shape, b2
    elif mode == "nt":
        (m, k), (n, k2) = a.shape, b2
    else:
        (k, m), (k2, n) = a.shape, b2
    assert k == k2, (a.shape, b.shape, mode)
    tm, tn, tk = min(tm, m), min(tn, n), min(tk, k)
    assert m % tm == 0 and n % tn == 0 and k % tk == 0, (m, n, k, tm, tn, tk)
    nk = k // tk
    if mode == "tn":
        a_spec = pl.BlockSpec((tk, tm), lambda i, j, kk: (kk, i))
    else:
        a_spec = pl.BlockSpec((tm, tk), lambda i, j, kk: (i, kk))
    if b_shards and mode == "nn":
        b_spec = pl.BlockSpec((None, tk, tn), lambda i, j, kk: (j, kk, 0))
    elif b_shards:
        b_spec = pl.BlockSpec((None, tn, tk), lambda i, j, kk: (kk, j, 0))
    elif mode == "nt":
        b_spec = pl.BlockSpec((tn, tk), lambda i, j, kk: (j, kk))
    else:
        b_spec = pl.BlockSpec((tk, tn), lambda i, j, kk: (kk, j))
    dims = {"nn": ((1,), (0,)), "nt": ((1,), (1,)), "tn": ((0,), (0,))}[mode]
    ex_specs = []
    for arr, kind in extras:
        if kind == "tile":
            ex_specs.append(pl.BlockSpec((tm, tn), lambda i, j, kk: (i, j)))
        else:
            ex_specs.append(pl.BlockSpec((1, tn), lambda i, j, kk: (0, j)))
    n_ex, n_out = len(extras), len(out_dtypes)
    if epilogue is None:
        epilogue = lambda acc: (acc,)
    hk = _HookSlots(hook, n_in=2 + n_ex, n_out=n_out + f32_tail, n_scratch=0 if nk == 1 else 1)
    grid = (m // tm, n // tn, nk)

    def body(*refs):
        (a_ref, b_ref, *ex), outs, scratch = hk.own(refs)
        if hook is not None:
            step = (pl.program_id(0) * grid[1] + pl.program_id(1)) * grid[2] + pl.program_id(2)
            hk.run(refs, step, grid[0] * grid[1] * grid[2])

        def finish(acc):
            res = epilogue(acc, *[e[...] for e in ex])
            for o, r in zip(outs, res):
                o[...] = r.astype(o.dtype)
            if f32_tail:
                outs[n_out][...] = acc[:, f32_block:f32_block + LANES]

        if nk == 1:
            finish(_dot(a_ref[...], b_ref[...], dims))
        else:
            acc_ref = scratch[0]
            kk = pl.program_id(2)

            @pl.when(kk == 0)
            def _():
                acc_ref[...] = jnp.zeros_like(acc_ref)

            acc_ref[...] += _dot(a_ref[...], b_ref[...], dims)

            @pl.when(kk == nk - 1)
            def _():
                finish(acc_ref[...])

    if out_shards:
        out_spec = pl.BlockSpec((None, tm, tn), lambda i, j, kk: (j, i, 0))
        out_dims = (n // tn, m, tn)
    else:
        out_spec = pl.BlockSpec((tm, tn), lambda i, j, kk: (i, j))
        out_dims = (m, n)
    tail_specs = [pl.BlockSpec((tm, LANES), lambda i, j, kk: (i, 0))] if f32_tail else []
    tail_shapes = [jax.ShapeDtypeStruct((m, LANES), F32)] if f32_tail else []
    outs = pl.pallas_call(
        body,
        grid=grid,
        in_specs=[a_spec, b_spec] + ex_specs + hk.in_specs,
        out_specs=[out_spec for _ in out_dtypes] + tail_specs + hk.out_specs,
        out_shape=[jax.ShapeDtypeStruct(out_dims, dt) for dt in out_dtypes] + tail_shapes + hk.out_shape,
        scratch_shapes=([] if nk == 1 else [pltpu.VMEM((tm, tn), F32)]) + hk.scratch,
        compiler_params=_params(*hk.semantics("parallel", "arbitrary" if f32_tail else "parallel", "arbitrary")),
        name=name,
    )(a, b, *[arr for arr, _ in extras], *hk.inputs)
    n_own = n_out + f32_tail
    own = outs[0] if n_own == 1 else outs[:n_own]
    return own if hook is None else (own, outs[n_own:])


def _row_tile(t, want):
    return min(t, want)


def _rms_fwd(x, w, *, name, resid=None, want_u=None, target=None):
    t, d = x.shape
    tr = _row_tile(t, 512)

    def norm(v, wv):
        return v * lax.rsqrt(jnp.mean(v * v, axis=-1, keepdims=True) + NORM_EPS) * wv

    row = pl.BlockSpec((tr, d), lambda i: (i, 0))
    vec = pl.BlockSpec((1, d), lambda i: (0, 0))
    if target is not None:
        def body(x_ref, w_ref, r_ref, t_ref, dh_ref, loss_ref):
            err = r_ref[...] + norm(x_ref[...].astype(F32), w_ref[...]) - t_ref[...]
            dh_ref[...] = err * (1.0 / d)

            @pl.when(pl.program_id(0) == 0)
            def _():
                loss_ref[...] = jnp.zeros_like(loss_ref)

            part = jnp.sum(jnp.sum(err * err, axis=1, keepdims=True), axis=0, keepdims=True) * (0.5 / d)
            loss_ref[...] += jnp.broadcast_to(part, loss_ref.shape)

        return pl.pallas_call(
            body, grid=(t // tr,), in_specs=[row, vec, row, row],
            out_specs=[row, pl.BlockSpec((8, LANES), lambda i: (0, 0))],
            out_shape=[jax.ShapeDtypeStruct((t, d), F32), jax.ShapeDtypeStruct((8, LANES), F32)],
            compiler_params=_params("arbitrary"), name=name)(x, w, resid, target)
    if resid is None:
        def body(x_ref, w_ref, o_ref):
            o_ref[...] = norm(x_ref[...].astype(F32), w_ref[...]).astype(BF16)
        ins, in_specs = (x, w), [row, vec]
        out_shape, out_specs = jax.ShapeDtypeStruct((t, d), BF16), row
    elif want_u is None:
        def body(x_ref, w_ref, r_ref, o_ref):
            o_ref[...] = r_ref[...] + norm(x_ref[...].astype(F32), w_ref[...])
        ins, in_specs = (x, w, resid), [row, vec, row]
        out_shape, out_specs = jax.ShapeDtypeStruct((t, d), F32), row
    else:
        def body(x_ref, w_ref, r_ref, w2_ref, o_ref, u_ref):
            h = r_ref[...] + norm(x_ref[...].astype(F32), w_ref[...])
            o_ref[...] = h
            u_ref[...] = norm(h, w2_ref[...]).astype(BF16)
        ins, in_specs = (x, w, resid, want_u), [row, vec, row, vec]
        out_shape = [jax.ShapeDtypeStruct((t, d), F32), jax.ShapeDtypeStruct((t, d), BF16)]
        out_specs = [row, row]
    return pl.pallas_call(body, grid=(t // tr,), in_specs=in_specs, out_specs=out_specs, out_shape=out_shape,
                          compiler_params=_params("parallel"), name=name)(*ins)


def _rms_bwd(x, w, dy, *, name, resid=None, out_dtype=F32, dx_col_sum=False):
    t, d = x.shape
    tr = _row_tile(t, 512)
    row = pl.BlockSpec((tr, d), lambda i: (i, 0))
    vec = pl.BlockSpec((1, d), lambda i: (0, 0))
    has_res = resid is not None

    def body(x_ref, w_ref, dy_ref, *rest):
        r_ref = rest[0] if has_res else None
        dx_ref, dw_ref = rest[has_res:has_res + 2]
        xv = x_ref[...].astype(F32)
        dyv = dy_ref[...].astype(F32)
        r = lax.rsqrt(jnp.mean(xv * xv, axis=-1, keepdims=True) + NORM_EPS)
        xhat = xv * r
        dyw = dyv * w_ref[...]
        dx = r * (dyw - xhat * jnp.mean(dyw * xhat, axis=-1, keepdims=True))
        if has_res:
            dx = dx + r_ref[...]
        dx_ref[...] = dx.astype(dx_ref.dtype)

        sums = [(dw_ref, dyv * xhat)] + ([(rest[-1], dx)] if dx_col_sum else [])

        @pl.when(pl.program_id(0) == 0)
        def _():
            for acc_ref, _ in sums:
                acc_ref[...] = jnp.zeros_like(acc_ref)

        for acc_ref, rows in sums:
            acc_ref[...] += jnp.sum(rows, axis=0, keepdims=True)

    ins = (x, w, dy) + ((resid,) if has_res else ())
    in_specs = [row, vec, row] + ([row] if has_res else [])
    n_vec = 2 if dx_col_sum else 1
    return pl.pallas_call(
        body, grid=(t // tr,), in_specs=in_specs, out_specs=[row] + [vec] * n_vec,
        out_shape=[jax.ShapeDtypeStruct((t, d), out_dtype)] + [jax.ShapeDtypeStruct((1, d), F32)] * n_vec,
        compiler_params=_params("arbitrary"), name=name)(*ins)


def _col_sum(x, *, name):
    t, n = x.shape
    tr = _row_tile(t, 512)

    def body(x_ref, o_ref):
        @pl.when(pl.program_id(0) == 0)
        def _():
            o_ref[...] = jnp.zeros_like(o_ref)

        o_ref[...] += jnp.sum(x_ref[...].astype(F32), axis=0, keepdims=True)

    return pl.pallas_call(
        body, grid=(t // tr,), in_specs=[pl.BlockSpec((tr, n), lambda i: (i, 0))],
        out_specs=pl.BlockSpec((1, n), lambda i: (0, 0)), out_shape=jax.ShapeDtypeStruct((1, n), F32),
        compiler_params=_params("arbitrary"), name=name)(x)


SSD_IN_SHARD = SSD_IN_DIM // N_CHIPS


def _w_in_from_shards(shards, *, name):
    d = shards.shape[1]
    tr = 256

    def body(s_ref, o_ref):
        o_ref[:, pl.ds(SSD_DT_COL, SSD_IN_PAD - SSD_DT_COL)] = jnp.zeros((tr, SSD_IN_PAD - SSD_DT_COL), o_ref.dtype)
        for s in range(N_CHIPS):
            o_ref[:, pl.ds(SSD_IN_SHARD * s, SSD_IN_SHARD)] = s_ref[s]

    return pl.pallas_call(
        body, grid=(d // tr,), in_specs=[pl.BlockSpec((N_CHIPS, tr, SSD_IN_SHARD), lambda i: (0, i, 0))],
        out_specs=pl.BlockSpec((tr, SSD_IN_PAD), lambda i: (i, 0)),
        out_shape=jax.ShapeDtypeStruct((d, SSD_IN_PAD), shards.dtype),
        compiler_params=_params("parallel"), name=name)(shards)


def _w_in_to_shards(g, *, name):
    d = g.shape[0]
    tr = 256

    def body(g_ref, o_ref):
        for s in range(N_CHIPS):
            o_ref[s] = g_ref[:, pl.ds(SSD_IN_SHARD * s, SSD_IN_SHARD)].astype(o_ref.dtype)

    return pl.pallas_call(
        body, grid=(d // tr,), in_specs=[pl.BlockSpec((tr, SSD_IN_PAD), lambda i: (i, 0))],
        out_specs=pl.BlockSpec((N_CHIPS, tr, SSD_IN_SHARD), lambda i: (0, i, 0)),
        out_shape=jax.ShapeDtypeStruct((N_CHIPS, d, SSD_IN_SHARD), BF16),
        compiler_params=_params("parallel"), name=name)(g)


XBC_COL0 = SSD_D_INNER // LANES


def _shift_down(v, k, row_ids):
    return jnp.where(row_ids >= k, pltpu.roll(v, k, axis=0), 0.0)


def _shift_up(v, k, row_ids):
    n = v.shape[0]
    return jnp.where(row_ids < n - k, pltpu.roll(v, n - k, axis=0), 0.0)


def _conv_pre(x, w, b, row_ids):
    pre = b + w[3:4, :] * x
    for k in (1, 2, 3):
        pre = pre + w[3 - k:4 - k, :] * _shift_down(x, k, row_ids)
    return pre


def _conv_fwd(zx, conv_w, conv_b, *, name, hook=None):
    t = zx.shape[0]
    nct = SSD_CONV_DIM // LANES
    hk = _HookSlots(hook, n_in=3, n_out=1, n_scratch=0)

    def body(*refs):
        (x_ref, w_ref, b_ref), (o_ref,), _ = hk.own(refs)
        if hook is not None:
            hk.run(refs, pl.program_id(0), nct)
        x = x_ref[...].astype(F32)
        row_ids = lax.broadcasted_iota(jnp.int32, x.shape, 0)
        pre = _conv_pre(x, w_ref[...], b_ref[...], row_ids)
        o_ref[...] = pre * _sigmoid(pre)

    outs = pl.pallas_call(
        body, grid=(nct,),
        in_specs=[pl.BlockSpec((t, LANES), lambda j: (0, XBC_COL0 + j)),
                  pl.BlockSpec((SSD_CONV_WIDTH, LANES), lambda j: (0, j)),
                  pl.BlockSpec((1, LANES), lambda j: (0, j))] + hk.in_specs,
        out_specs=[pl.BlockSpec((t, LANES), lambda j: (0, j))] + hk.out_specs,
        out_shape=[jax.ShapeDtypeStruct((t, SSD_CONV_DIM), F32)] + hk.out_shape,
        scratch_shapes=hk.scratch,
        compiler_params=_params(*hk.semantics("parallel")), name=name)(zx, conv_w, conv_b, *hk.inputs)
    return outs[0] if hook is None else (outs[0], outs[1:])


def _conv_bwd(zx, conv_w, conv_b, d_xs, d_bm, d_cm, dzx, *, name):
    t = zx.shape[0]
    nct = SSD_CONV_DIM // LANES
    n_xs = SSD_D_INNER // LANES
    n_bm = SSD_N_GROUPS * SSD_D_STATE // LANES

    def body(x_ref, w_ref, b_ref, dxs_ref, dbm_ref, dcm_ref, _, dx_ref, dw_ref, db_ref):
        x = x_ref[...].astype(F32)
        w = w_ref[...]
        j = pl.program_id(0)
        dy = jnp.where(j < n_xs, dxs_ref[...], jnp.where(j < n_xs + n_bm, dbm_ref[...], dcm_ref[...]))
        row_ids = lax.broadcasted_iota(jnp.int32, x.shape, 0)
        pre = _conv_pre(x, w, b_ref[...], row_ids)
        sg = _sigmoid(pre)
        dpre = dy * (sg * (1.0 + pre * (1.0 - sg)))
        dx = w[3:4, :] * dpre
        for k in (1, 2, 3):
            dx = dx + w[3 - k:4 - k, :] * _shift_up(dpre, k, row_ids)
        dx_ref[...] = dx.astype(dx_ref.dtype)
        db_ref[...] = jnp.sum(dpre, axis=0, keepdims=True)
        dw_ref[3:4, :] = jnp.sum(dpre * x, axis=0, keepdims=True)
        for k in (1, 2, 3):
            dw_ref[3 - k:4 - k, :] = jnp.sum(dpre * _shift_down(x, k, row_ids), axis=0, keepdims=True)

    clip = lambda j, lo, n: jnp.clip(j - lo, 0, n - 1)
    return pl.pallas_call(
        body, grid=(nct,),
        in_specs=[pl.BlockSpec((t, LANES), lambda j: (0, XBC_COL0 + j)),
                  pl.BlockSpec((SSD_CONV_WIDTH, LANES), lambda j: (0, j)),
                  pl.BlockSpec((1, LANES), lambda j: (0, j)),
                  pl.BlockSpec((t, LANES), lambda j: (0, clip(j, 0, n_xs))),
                  pl.BlockSpec((t, LANES), lambda j: (0, clip(j, n_xs, n_bm))),
                  pl.BlockSpec((t, LANES), lambda j: (0, clip(j, n_xs + n_bm, n_bm))), ANY],
        out_specs=[pl.BlockSpec((t, LANES), lambda j: (0, XBC_COL0 + j)),
                   pl.BlockSpec((SSD_CONV_WIDTH, LANES), lambda j: (0, j)), pl.BlockSpec((1, LANES), lambda j: (0, j))],
        out_shape=[jax.ShapeDtypeStruct(dzx.shape, dzx.dtype),
                   jax.ShapeDtypeStruct((SSD_CONV_WIDTH, SSD_CONV_DIM), F32),
                   jax.ShapeDtypeStruct((1, SSD_CONV_DIM), F32)],
        input_output_aliases={6: 0},
        compiler_params=_params("parallel"), name=name)(zx, conv_w, conv_b, d_xs, d_bm, d_cm, dzx)


def _softplus_fwd(dt_raw, bias_row, alog_row, *, name):
    t = dt_raw.shape[0]
    q = SSD_CHUNK
    tr = _row_tile(t, 1024)

    def body(x_ref, b_ref, al_ref, dt_ref, cum_ref):
        v = x_ref[...] + b_ref[...]
        e = jnp.exp(-jnp.abs(v))
        u = 1.0 + e
        log1p = jnp.where(u == 1.0, e, jnp.log(u) * (e / (u - 1.0)))
        dt = jnp.maximum(v, 0.0) + log1p
        a = dt * -jnp.exp(al_ref[...])
        lower = (lax.broadcasted_iota(jnp.int32, (q, q), 1) <= lax.broadcasted_iota(jnp.int32, (q, q), 0)).astype(F32)
        cums = [lax.dot_general(lower, a[c * q:(c + 1) * q, :], ((((1,), (0,))), ((), ())), precision=lax.Precision.HIGHEST,
                                preferred_element_type=F32) for c in range(tr // q)]
        dt_t, cum_t = dt.T, jnp.concatenate(cums, axis=0).T
        for g in range(SSD_N_GROUPS):
            rows = slice(g * SSD_HPG, (g + 1) * SSD_HPG)
            dt_ref[g] = dt_t[rows, :]
            cum_ref[g] = cum_t[rows, :]

    vec = pl.BlockSpec((1, LANES), lambda i: (0, 0))
    by_group = pl.BlockSpec((SSD_N_GROUPS, SSD_HPG, tr), lambda i: (0, 0, i))
    return pl.pallas_call(
        body, grid=(t // tr,),
        in_specs=[pl.BlockSpec((tr, LANES), lambda i: (i, 0)), vec, vec],
        out_specs=[by_group, by_group],
        out_shape=[jax.ShapeDtypeStruct((SSD_N_GROUPS, SSD_HPG, t), F32)] * 2,
        compiler_params=_params("parallel"), name=name)(dt_raw, bias_row, alog_row)


def _softplus_bwd(dt_raw, bias_row, ddt_rows, dzx, *, name):
    t = dt_raw.shape[0]
    tr = _row_tile(t, 1024)
    tail = SSD_IN_PAD - SSD_DT_COL

    def body(x_ref, b_ref, g_ref, _, o_ref, db_ref):
        v = x_ref[...] + b_ref[...]
        lane = lax.broadcasted_iota(jnp.int32, v.shape, 1)
        by_head = jnp.concatenate([g_ref[g] for g in range(SSD_N_GROUPS)]
                                  + [jnp.zeros((LANES - SSD_N_HEADS, tr), F32)], axis=0)
        d = jnp.where(lane < SSD_N_HEADS, by_head.T * _sigmoid(v), 0.0)
        o_ref[:, pl.ds(0, LANES)] = d.astype(o_ref.dtype)
        o_ref[:, pl.ds(LANES, tail - LANES)] = jnp.zeros((tr, tail - LANES), o_ref.dtype)

        @pl.when(pl.program_id(0) == 0)
        def _():
            db_ref[...] = jnp.zeros_like(db_ref)

        db_ref[...] += jnp.sum(d, axis=0, keepdims=True)

    return pl.pallas_call(
        body, grid=(t // tr,),
        in_specs=[pl.BlockSpec((tr, LANES), lambda i: (i, 0)), pl.BlockSpec((1, LANES), lambda i: (0, 0)),
                  pl.BlockSpec((SSD_N_GROUPS, SSD_HPG, tr), lambda i: (0, 0, i)), ANY],
        out_specs=[pl.BlockSpec((tr, tail), lambda i: (i, SSD_DT_COL // tail)), pl.BlockSpec((1, LANES), lambda i: (0, 0))],
        out_shape=[jax.ShapeDtypeStruct(dzx.shape, dzx.dtype), jax.ShapeDtypeStruct((1, LANES), F32)],
        input_output_aliases={3: 0},
        compiler_params=_params("arbitrary"), name=name)(dt_raw, bias_row, ddt_rows, dzx)


def _ssd_masks():
    q = SSD_CHUNK
    tt = lax.broadcasted_iota(jnp.int32, (q, q), 0)
    ss = lax.broadcasted_iota(jnp.int32, (q, q), 1)
    lane = lax.broadcasted_iota(jnp.int32, (1, SSD_GW), 1)
    srow = lax.broadcasted_iota(jnp.int32, (SSD_GW, 1), 0)
    hm = [(lane >= SSD_HEAD_DIM * j) & (lane < SSD_HEAD_DIM * (j + 1)) for j in range(SSD_HPG)]
    rm = [(srow >= SSD_HEAD_DIM * j) & (srow < SSD_HEAD_DIM * (j + 1)) for j in range(SSD_HPG)]
    return tt, ss, hm, rm


def _ssd_head_terms(dt_rows, cum_rows, a_rows, j, tt, ss):
    q = SSD_CHUNK
    dt_row = dt_rows[j:j + 1, :]
    dt_col = jnp.sum(jnp.where(tt == ss, dt_row, 0.0), axis=1, keepdims=True)
    a_row1 = a_rows[j:j + 1, :]
    a_11 = a_rows[j:j + 1, 0:1]
    cum_col = jnp.sum(jnp.where(ss <= tt, dt_row * a_row1, 0.0), axis=1, keepdims=True)
    cum_row = cum_rows[j:j + 1, :]
    decay = jnp.exp(jnp.where(ss <= tt, cum_col - cum_row, -jnp.inf))
    cum_last = cum_col[q - 1:q, :]
    e_col = jnp.exp(cum_col)
    dte_col = jnp.exp(cum_last - cum_col)
    e_last = jnp.exp(cum_last)
    return dt_col, dt_row, a_row1, a_11, decay, e_col, dte_col, e_last


SSD_CHUNKS_PER_STEP = 8
SSD_BC_COL0 = SSD_D_INNER // SSD_D_STATE


def _ssd_head_selects(terms, hm, rm):
    e_all = jnp.zeros((SSD_CHUNK, SSD_GW), F32)
    w_all = jnp.zeros((SSD_CHUNK, SSD_GW), F32)
    e_s = jnp.zeros((SSD_GW, 1), F32)
    for j in range(SSD_HPG):
        dt_col, _, _, _, _, e_col, dte_col, e_last = terms[j]
        e_all = jnp.where(hm[j], e_col, e_all)
        w_all = jnp.where(hm[j], dt_col * dte_col, w_all)
        e_s = jnp.where(rm[j], e_last, e_s)
    return e_all, w_all, e_s


def _ssd_fwd(xc, dtr, cumr, alog_b, d_b, *, name, hook=None):
    t = xc.shape[0]
    q = SSD_CHUNK
    nc = t // q
    kc = min(SSD_CHUNKS_PER_STEP, nc)
    rows = kc * q
    hk = _HookSlots(hook, n_in=7, n_out=2, n_scratch=1)

    def body(*refs):
        (x_ref, b_ref, c_ref, dtr_ref, cumr_ref, alog_ref, d_ref), (y_ref, st_ref), (s_scr,) = hk.own(refs)
        if hook is not None:
            hk.run(refs, pl.program_id(0) * (nc // kc) + pl.program_id(1), SSD_N_GROUPS * (nc // kc))

        @pl.when(pl.program_id(1) == 0)
        def _():
            s_scr[...] = jnp.zeros_like(s_scr)

        tt, ss, hm, rm = _ssd_masks()
        a_rows = -jnp.exp(alog_ref[...])
        d_rows = d_ref[...]
        d_all = jnp.zeros((1, SSD_GW), F32)
        for j in range(SSD_HPG):
            d_all = jnp.where(hm[j], d_rows[j:j + 1, 0:1], d_all)
        ks, hs = range(kc), range(SSD_HPG)
        sl = [pl.ds(k * q, q) for k in ks]
        x = [x_ref[sl[k], :] for k in ks]
        bm = [b_ref[sl[k], :].astype(BF16) for k in ks]
        cm = [c_ref[sl[k], :].astype(BF16) for k in ks]
        xb = [x[k].astype(BF16) for k in ks]
        terms = [[_ssd_head_terms(dtr_ref[:, sl[k]], cumr_ref[:, sl[k]], a_rows, j, tt, ss) for j in hs] for k in ks]
        g = [_dot_nt(cm[k], bm[k]) for k in ks]
        m = [[(g[k] * terms[k][j][4] * terms[k][j][1]).astype(BF16) for j in hs] for k in ks]
        yj = [[_dot_nn(m[k][j], xb[k]) for j in hs] for k in ks]
        sel = [_ssd_head_selects(terms[k], hm, rm) for k in ks]
        upd = [_dot_tn((x[k] * sel[k][1]).astype(BF16), bm[k]) for k in ks]
        states = [s_scr[...]]
        for k in ks:
            states.append(states[k] * sel[k][2] + upd[k])
        inter = [_dot_nt(cm[k], states[k].astype(BF16)) for k in ks]
        ys = []
        for k in ks:
            y = jnp.zeros((q, SSD_GW), F32)
            for j in hs:
                y = jnp.where(hm[j], yj[k][j], y)
            ys.append(y + inter[k] * sel[k][0] + x[k] * d_all)
        for k in ks:
            st_ref[k] = states[k]
        y_ref[...] = jnp.concatenate(ys, axis=0)
        s_scr[...] = states[kc]

    blk = lambda width, off: pl.BlockSpec((rows, width), lambda g, c: (c, off + g))
    par_s = pl.BlockSpec((None, SSD_HPG, LANES), lambda g, c: (g, 0, 0))
    row_s = pl.BlockSpec((None, SSD_HPG, rows), lambda g, c: (g, 0, c))
    outs = pl.pallas_call(
        body, grid=(SSD_N_GROUPS, nc // kc),
        in_specs=[blk(SSD_GW, 0), blk(SSD_D_STATE, SSD_BC_COL0), blk(SSD_D_STATE, SSD_BC_COL0 + SSD_N_GROUPS),
                  row_s, row_s, par_s, par_s] + hk.in_specs,
        out_specs=[blk(SSD_GW, 0), pl.BlockSpec((None, kc, SSD_GW, SSD_D_STATE), lambda g, c: (g, c, 0, 0))] + hk.out_specs,
        out_shape=[jax.ShapeDtypeStruct((t, SSD_D_INNER), F32),
                   jax.ShapeDtypeStruct((SSD_N_GROUPS, nc, SSD_GW, SSD_D_STATE), F32)] + hk.out_shape,
        scratch_shapes=[pltpu.VMEM((SSD_GW, SSD_D_STATE), F32)] + hk.scratch,
        compiler_params=_params(*hk.semantics("parallel", "arbitrary")), name=name)(
            xc, xc, xc, dtr, cumr, alog_b, d_b, *hk.inputs)
    return outs if hook is None else (outs[:2], outs[2:])


def _ssd_bwd(xc, dtr, cumr, alog_b, d_b, states, dy, *, name, hook=None):
    t = xc.shape[0]
    q = SSD_CHUNK
    nc = t // q
    kc = min(SSD_CHUNKS_PER_STEP, nc)
    nst = nc // kc
    rows = kc * q
    rev = lambda c: nst - 1 - c
    hk = _HookSlots(hook, n_in=9, n_out=5, n_scratch=1)

    def body(*refs):
        ((x_ref, b_ref, c_ref, dtr_ref, cumr_ref, alog_ref, d_ref, st_ref, dy_ref),
         (dx_ref, db_ref, dc_ref, ddt_ref, dpar_ref), (ds_scr,)) = hk.own(refs)
        if hook is not None:
            hk.run(refs, pl.program_id(0) * nst + pl.program_id(1), SSD_N_GROUPS * nst)

        @pl.when(pl.program_id(1) == 0)
        def _():
            ds_scr[...] = jnp.zeros_like(ds_scr)
            dpar_ref[...] = jnp.zeros_like(dpar_ref)

        tt, ss, hm, rm = _ssd_masks()
        tcol = lax.broadcasted_iota(jnp.int32, (q, 1), 0)
        lane = lax.broadcasted_iota(jnp.int32, (1, LANES), 1)
        a_rows = -jnp.exp(alog_ref[...])
        d_rows = d_ref[...]
        d_all = jnp.zeros((1, SSD_GW), F32)
        for j in range(SSD_HPG):
            d_all = jnp.where(hm[j], d_rows[j:j + 1, 0:1], d_all)
        ks, hs = range(kc), range(SSD_HPG)
        sl = [pl.ds(k * q, q) for k in ks]
        x = [x_ref[sl[k], :] for k in ks]
        dyv = [dy_ref[sl[k], :] for k in ks]
        bm = [b_ref[sl[k], :].astype(BF16) for k in ks]
        cm = [c_ref[sl[k], :].astype(BF16) for k in ks]
        s_in = [st_ref[k] for k in ks]
        xb = [x[k].astype(BF16) for k in ks]
        dyb = [dyv[k].astype(BF16) for k in ks]
        s_b = [s_in[k].astype(BF16) for k in ks]
        terms = [[_ssd_head_terms(dtr_ref[:, sl[k]], cumr_ref[:, sl[k]], a_rows, j, tt, ss) for j in hs] for k in ks]
        sel = [_ssd_head_selects(terms[k], hm, rm) for k in ks]
        e_all, w_all, e_s = [s_[0] for s_ in sel], [s_[1] for s_ in sel], [s_[2] for s_ in sel]
        dye = [(dyv[k] * e_all[k]).astype(BF16) for k in ks]
        ds_loc = [_dot_tn(dye[k], cm[k]) for k in ks]
        ds = [None] * kc
        running = ds_scr[...]
        for k in reversed(ks):
            ds[k] = running
            running = running * e_s[k] + ds_loc[k]
        ds_scr[...] = running
        ds_b = [ds[k].astype(BF16) for k in ks]
        g = [_dot_nt(cm[k], bm[k]) for k in ks]
        cs = [_dot_nt(cm[k], s_b[k]) for k in ks]
        bds = [_dot_nt(bm[k], ds_b[k]) for k in ks]
        dm = [[_dot_nt(jnp.where(hm[j], dyv[k], 0.0).astype(BF16), xb[k]) for j in hs] for k in ks]
        gl = [[g[k] * terms[k][j][4] for j in hs] for k in ks]
        wp = [[dm[k][j] * gl[k][j] for j in hs] for k in ks]
        mt = [[(gl[k][j] * terms[k][j][1]).astype(BF16) for j in hs] for k in ks]
        dxj = [[_dot_tn(mt[k][j], dyb[k]) for j in hs] for k in ks]
        dg = []
        for k in ks:
            acc = jnp.zeros((q, q), F32)
            for j in hs:
                acc = acc + dm[k][j] * terms[k][j][4] * terms[k][j][1]
            dg.append(acc.astype(BF16))
        dy_cs = [dyv[k] * cs[k] for k in ks]
        x_bds = [x[k] * bds[k] for k in ks]
        dy_x = [dyv[k] * x[k] for k in ks]
        ds_s = [ds[k] * s_in[k] for k in ks]
        w = [[wp[k][j] * terms[k][j][1] for j in hs] for k in ks]
        rw_col = [[jnp.sum(w[k][j], axis=1, keepdims=True) for j in hs] for k in ks]
        cw_row = [[jnp.sum(w[k][j], axis=0, keepdims=True) for j in hs] for k in ks]
        cwp_row = [[jnp.sum(wp[k][j], axis=0, keepdims=True) for j in hs] for k in ks]
        r1_col = [[jnp.sum(jnp.where(hm[j], dy_cs[k], 0.0), axis=1, keepdims=True) * terms[k][j][5] for j in hs] for k in ks]
        dw_col = [[jnp.sum(jnp.where(hm[j], x_bds[k], 0.0), axis=1, keepdims=True) for j in hs] for k in ks]
        head_rows = [slice(j * SSD_HEAD_DIM, (j + 1) * SSD_HEAD_DIM) for j in hs]
        lane_sum = lambda v: jnp.sum(v, axis=1, keepdims=True)
        s_sum = [[lane_sum(jnp.sum(ds_s[k][head_rows[j], :], axis=0, keepdims=True)) for j in hs] for k in ks]
        dy_x_cols = [jnp.sum(dy_x[k], axis=0, keepdims=True) for k in ks]
        d_d = [[lane_sum(jnp.where(hm[j], dy_x_cols[k], 0.0)) for j in hs] for k in ks]
        ddt_rows = [[None] * SSD_HPG for _ in ks]
        dpar = [jnp.zeros((1, LANES), F32) for _ in hs]
        for k in ks:
            for j in hs:
                dt_col, dt_row, a_row1, a_11, _, _, dte_col, e_last = terms[k][j]
                dww = dw_col[k][j] * (dt_col * dte_col)
                last_add = jnp.sum(dww, axis=0, keepdims=True) + e_last * s_sum[k][j]
                dcum_col = rw_col[k][j] + r1_col[k][j] - dww + jnp.where(tcol == q - 1, last_add, 0.0)
                da_row = jnp.sum(jnp.where(tt >= ss, dcum_col, 0.0), axis=0, keepdims=True)
                da_col = jnp.sum(jnp.where(ss >= tt, -cw_row[k][j], 0.0), axis=1, keepdims=True)
                ddt_col = a_11 * da_col + dw_col[k][j] * dte_col
                ddt_rows[k][j] = (a_row1 * da_row + cwp_row[k][j]
                                  + jnp.sum(jnp.where(tt == ss, ddt_col, 0.0), axis=0, keepdims=True))
                d_a = jnp.sum(dt_row * da_row, axis=1, keepdims=True) + jnp.sum(dt_col * da_col, axis=0, keepdims=True)
                dpar[j] = dpar[j] + jnp.where(lane == 0, d_a * a_11, 0.0) + jnp.where(lane == 1, d_d[k][j], 0.0)
        dxs = []
        for k in ks:
            acc = jnp.zeros((q, SSD_GW), F32)
            for j in hs:
                acc = jnp.where(hm[j], dxj[k][j], acc)
            dxs.append(acc + w_all[k] * bds[k] + d_all * dyv[k])
        xw = [(x[k] * w_all[k]).astype(BF16) for k in ks]
        dc = [_dot_nn(dg[k], bm[k]) + _dot_nn(dye[k], s_b[k]) for k in ks]
        db = [_dot_tn(dg[k], cm[k]) + _dot_nn(xw[k], ds_b[k]) for k in ks]
        dx_ref[...] = jnp.concatenate(dxs, axis=0)
        dc_ref[...] = jnp.concatenate(dc, axis=0)
        db_ref[...] = jnp.concatenate(db, axis=0)
        ddt_ref[...] = jnp.concatenate([jnp.concatenate([ddt_rows[k][j] for k in ks], axis=1) for j in hs], axis=0)
        dpar_ref[...] += jnp.concatenate(dpar, axis=0)

    blk = lambda width, off: pl.BlockSpec((rows, width), lambda g, c: (rev(c), off + g))
    par_s = pl.BlockSpec((None, SSD_HPG, LANES), lambda g, c: (g, 0, 0))
    outs = pl.pallas_call(
        body, grid=(SSD_N_GROUPS, nst),
        in_specs=[blk(SSD_GW, 0), blk(SSD_D_STATE, SSD_BC_COL0), blk(SSD_D_STATE, SSD_BC_COL0 + SSD_N_GROUPS),
                  pl.BlockSpec((None, SSD_HPG, rows), lambda g, c: (g, 0, rev(c))),
                  pl.BlockSpec((None, SSD_HPG, rows), lambda g, c: (g, 0, rev(c))), par_s, par_s,
                  pl.BlockSpec((None, kc, SSD_GW, SSD_D_STATE), lambda g, c: (g, rev(c), 0, 0)), blk(SSD_GW, 0)] + hk.in_specs,
        out_specs=[blk(SSD_GW, 0), blk(SSD_D_STATE, 0), blk(SSD_D_STATE, 0),
                   pl.BlockSpec((None, SSD_HPG, rows), lambda g, c: (g, 0, rev(c))), par_s] + hk.out_specs,
        out_shape=[jax.ShapeDtypeStruct((t, SSD_D_INNER), F32),
                   jax.ShapeDtypeStruct((t, SSD_N_GROUPS * SSD_D_STATE), F32),
                   jax.ShapeDtypeStruct((t, SSD_N_GROUPS * SSD_D_STATE), F32),
                   jax.ShapeDtypeStruct((SSD_N_GROUPS, SSD_HPG, t), F32),
                   jax.ShapeDtypeStruct((SSD_N_GROUPS, SSD_HPG, LANES), F32)] + hk.out_shape,
        scratch_shapes=[pltpu.VMEM((SSD_GW, SSD_D_STATE), F32)] + hk.scratch,
        compiler_params=_params(*hk.semantics("parallel", "arbitrary")), name=name)(
            xc, xc, xc, dtr, cumr, alog_b, d_b, states, dy, *hk.inputs)
    return outs if hook is None else (outs[:5], outs[5:])


def _gate_norm_fwd(y, zx, norm_w, *, name):
    t = y.shape[0]
    tr = _row_tile(t, 256)
    row = pl.BlockSpec((tr, SSD_D_INNER), lambda i: (i, 0))

    def body(y_ref, z_ref, w_ref, o_ref):
        for gi in range(SSD_N_GROUPS):
            sl = pl.ds(gi * SSD_GW, SSD_GW)
            z = z_ref[:, sl].astype(F32)
            gv = y_ref[:, sl] * (z * _sigmoid(z))
            r = lax.rsqrt(jnp.mean(gv * gv, axis=-1, keepdims=True) + NORM_EPS)
            o_ref[:, sl] = (gv * r * w_ref[:, sl]).astype(BF16)

    return pl.pallas_call(
        body, grid=(t // tr,), in_specs=[row, row, pl.BlockSpec((1, SSD_D_INNER), lambda i: (0, 0))],
        out_specs=row, out_shape=jax.ShapeDtypeStruct((t, SSD_D_INNER), BF16),
        compiler_params=_params("parallel"), name=name)(y, zx, norm_w)


def _gate_norm_bwd(y, zx, norm_w, dyn, *, name):
    t = y.shape[0]
    tr = _row_tile(t, 256)
    row = pl.BlockSpec((tr, SSD_D_INNER), lambda i: (i, 0))
    vec = pl.BlockSpec((1, SSD_D_INNER), lambda i: (0, 0))

    def body(y_ref, z_ref, w_ref, dyn_ref, dy_ref, dz_ref, dw_ref):
        @pl.when(pl.program_id(0) == 0)
        def _():
            dw_ref[...] = jnp.zeros_like(dw_ref)

        for gi in range(SSD_N_GROUPS):
            sl = pl.ds(gi * SSD_GW, SSD_GW)
            z = z_ref[:, sl].astype(F32)
            yv = y_ref[:, sl]
            sg = _sigmoid(z)
            sz = z * sg
            gv = yv * sz
            r = lax.rsqrt(jnp.mean(gv * gv, axis=-1, keepdims=True) + NORM_EPS)
            ghat = gv * r
            dout = dyn_ref[:, sl].astype(F32)
            dgh = dout * w_ref[:, sl]
            dgv = r * (dgh - ghat * jnp.mean(dgh * ghat, axis=-1, keepdims=True))
            dy_ref[:, sl] = dgv * sz
            dz_ref[:, sl] = (dgv * yv * (sg * (1.0 + z * (1.0 - sg)))).astype(dz_ref.dtype)
            dw_ref[:, sl] += jnp.sum(dout * ghat, axis=0, keepdims=True)

    return pl.pallas_call(
        body, grid=(t // tr,), in_specs=[row, row, vec, row], out_specs=[row, row, vec],
        out_shape=[jax.ShapeDtypeStruct((t, SSD_D_INNER), F32), jax.ShapeDtypeStruct((t, SSD_IN_PAD), BF16),
                   jax.ShapeDtypeStruct((1, SSD_D_INNER), F32)],
        compiler_params=_params("arbitrary"), name=name)(y, zx, norm_w, dyn)


ATTN_KV_W = ATTN_N_KV * ATTN_HEAD_DIM
ATTN_Q_HALF = 512
ATTN_K_BLK = ATTN_N_Q * ATTN_HEAD_DIM // ATTN_KV_W
ATTN_V_BLK = ATTN_K_BLK + 1


def _attn_valid(first_block):
    w = ATTN_WINDOW
    qpos = lax.broadcasted_iota(jnp.int32, (w, 2 * w), 0) + w
    kpos = lax.broadcasted_iota(jnp.int32, (w, 2 * w), 1)
    rel = qpos - kpos
    return (rel >= 0) & (rel < w) & jnp.logical_not(first_block & (kpos < w))


def _attn_head_views(lo_ref, hi_ref):
    hd = ATTN_HEAD_DIM
    per_half = ATTN_Q_HALF // hd
    return [(lo_ref if h < per_half else hi_ref)[:, pl.ds((h % per_half) * hd, hd)] for h in range(ATTN_N_Q)]


def _attn_block_views(lo_ref, hi_ref, kc_ref, kp_ref, vc_ref, vp_ref):
    hd = ATTN_HEAD_DIM
    kv_cols = [pl.ds(kh * hd, hd) for kh in range(ATTN_N_KV)]
    kb = [jnp.concatenate([kp_ref[:, c], kc_ref[:, c]], axis=0) for c in kv_cols]
    vb = [jnp.concatenate([vp_ref[:, c], vc_ref[:, c]], axis=0) for c in kv_cols]
    return _attn_head_views(lo_ref, hi_ref), kb, vb


def _attn_scores(q, kb, valid):
    scale = ATTN_HEAD_DIM ** -0.5
    return [jnp.where(valid, _dot_nt(q[h], kb[h // ATTN_REP]) * scale, -jnp.inf) for h in range(ATTN_N_Q)]


def _attn_softmax(s, sink):
    heads = range(ATTN_N_Q)
    m = [jnp.maximum(jnp.max(s[h], axis=1, keepdims=True), sink[h]) for h in heads]
    e = [jnp.exp(s[h] - m[h]) for h in heads]
    es = [jnp.exp(sink[h] - m[h]) for h in heads]
    inv = [1.0 / (jnp.sum(e[h], axis=1, keepdims=True) + es[h]) for h in heads]
    return e, es, inv


def _attn_fwd(qkv, sinks_b, *, name, hook=None):
    t = qkv.shape[0]
    w = ATTN_WINDOW
    nb = t // w
    prev = lambda n: jnp.maximum(n - 1, 0)
    hk = _HookSlots(hook, n_in=7, n_out=1, n_scratch=0)

    def body(*refs):
        (qlo_ref, qhi_ref, kc_ref, kp_ref, vc_ref, vp_ref, sink_ref), (o_ref,), _ = hk.own(refs)
        if hook is not None:
            hk.run(refs, pl.program_id(0), nb)
        heads = range(ATTN_N_Q)
        q, kb, vb = _attn_block_views(qlo_ref, qhi_ref, kc_ref, kp_ref, vc_ref, vp_ref)
        sink = [sink_ref[h:h + 1, 0:1] for h in heads]
        e, _, inv = _attn_softmax(_attn_scores(q, kb, _attn_valid(pl.program_id(0) == 0)), sink)
        out = [_dot_nn((e[h] * inv[h]).astype(BF16), vb[h // ATTN_REP]).astype(o_ref.dtype) for h in heads]
        o_ref[...] = jnp.concatenate(out, axis=1)

    qh = lambda half: pl.BlockSpec((w, ATTN_Q_HALF), lambda n: (n, half))
    kv = lambda blk, idx: pl.BlockSpec((w, ATTN_KV_W), lambda n: (idx(n), blk))
    cur = lambda n: n
    outs = pl.pallas_call(
        body, grid=(nb,),
        in_specs=[qh(0), qh(1), kv(ATTN_K_BLK, cur), kv(ATTN_K_BLK, prev), kv(ATTN_V_BLK, cur), kv(ATTN_V_BLK, prev),
                  pl.BlockSpec((ATTN_N_Q, LANES), lambda n: (0, 0))] + hk.in_specs,
        out_specs=[pl.BlockSpec((w, D_MODEL), lambda n: (n, 0))] + hk.out_specs,
        out_shape=[jax.ShapeDtypeStruct((t, D_MODEL), BF16)] + hk.out_shape,
        scratch_shapes=hk.scratch,
        compiler_params=_params(*hk.semantics("parallel")), name=name)(qkv, qkv, qkv, qkv, qkv, qkv, sinks_b, *hk.inputs)
    return outs[0] if hook is None else (outs[0], outs[1:])


def _attn_bwd(qkv, sinks_b, dout, *, name):
    t = qkv.shape[0]
    w = ATTN_WINDOW
    nb = t // w
    hd = ATTN_HEAD_DIM
    clamp = lambda n: jnp.minimum(n, nb - 1)
    prev = lambda n: jnp.maximum(clamp(n) - 1, 0)

    def body(qlo_ref, qhi_ref, kc_ref, kp_ref, vc_ref, vp_ref, sink_ref, dolo_ref, dohi_ref,
             dq_ref, dkv_ref, dsink_ref, carry):
        n = pl.program_id(0)

        @pl.when(n == 0)
        def _():
            carry[...] = jnp.zeros_like(carry)
            dsink_ref[...] = jnp.zeros_like(dsink_ref)

        @pl.when(n < nb)
        def _():
            heads, kvs = range(ATTN_N_Q), range(ATTN_N_KV)
            q, kb, vb = _attn_block_views(qlo_ref, qhi_ref, kc_ref, kp_ref, vc_ref, vp_ref)
            do = _attn_head_views(dolo_ref, dohi_ref)
            sink = [sink_ref[h:h + 1, 0:1] for h in heads]
            s = _attn_scores(q, kb, _attn_valid(n == 0))
            dp = [_dot_nt(do[h], vb[h // ATTN_REP]) for h in heads]
            e, es, inv = _attn_softmax(s, sink)
            p = [e[h] * inv[h] for h in heads]
            delta = [jnp.sum(p[h] * dp[h], axis=1, keepdims=True) for h in heads]
            dsc = [(p[h] * (dp[h] - delta[h]) * (hd ** -0.5)).astype(BF16) for h in heads]
            pb = [p[h].astype(BF16) for h in heads]
            dq = [_dot_nn(dsc[h], kb[h // ATTN_REP]).astype(dq_ref.dtype) for h in heads]
            stack = lambda per_head, kh: jnp.concatenate(per_head[kh * ATTN_REP:(kh + 1) * ATTN_REP], axis=0)
            dkb = [_dot_tn(stack(dsc, kh), stack(q, kh)) for kh in kvs]
            dvb = [_dot_tn(stack(pb, kh), stack(do, kh)) for kh in kvs]
            dsink = [jnp.broadcast_to(jnp.sum(-es[h] * inv[h] * delta[h], axis=0, keepdims=True), (1, LANES)) for h in heads]
            dq_ref[...] = jnp.concatenate(dq, axis=1)
            dsink_ref[...] += jnp.concatenate(dsink, axis=0)
            dkv_ref[...] = (carry[...] + jnp.concatenate([d[0:w, :] for d in dkb + dvb], axis=1)).astype(dkv_ref.dtype)
            carry[...] = jnp.concatenate([d[w:2 * w, :] for d in dkb + dvb], axis=1)

        @pl.when(n == nb)
        def _():
            dkv_ref[...] = carry[...].astype(dkv_ref.dtype)

    qh = lambda half: pl.BlockSpec((w, ATTN_Q_HALF), lambda n: (clamp(n), half))
    kv = lambda blk, idx: pl.BlockSpec((w, ATTN_KV_W), lambda n: (idx(n), blk))
    return pl.pallas_call(
        body, grid=(nb + 1,),
        in_specs=[qh(0), qh(1), kv(ATTN_K_BLK, clamp), kv(ATTN_K_BLK, prev), kv(ATTN_V_BLK, clamp), kv(ATTN_V_BLK, prev),
                  pl.BlockSpec((ATTN_N_Q, LANES), lambda n: (0, 0)), qh(0), qh(1)],
        out_specs=[pl.BlockSpec((w, D_MODEL), lambda n: (clamp(n), 0)),
                   pl.BlockSpec((w, 2 * ATTN_KV_W), lambda n: (jnp.maximum(n - 1, 0), 0)),
                   pl.BlockSpec((ATTN_N_Q, LANES), lambda n: (0, 0))],
        out_shape=[jax.ShapeDtypeStruct((t, D_MODEL), BF16), jax.ShapeDtypeStruct((t, 2 * ATTN_KV_W), BF16),
                   jax.ShapeDtypeStruct((ATTN_N_Q, LANES), F32)],
        scratch_shapes=[pltpu.VMEM((w, 2 * ATTN_KV_W), F32)],
        compiler_params=_params("arbitrary"), name=name)(qkv, qkv, qkv, qkv, qkv, qkv, sinks_b, dout, dout)


def _sq_relu_epilogue(acc):
    r = jnp.maximum(acc, 0.0)
    return (r * r,)


def _sq_relu_bwd_epilogue(acc, act):
    return (acc * (2.0 * jnp.sqrt(act.astype(F32))),)


def _bias_epilogue(acc, bias):
    return (acc + bias,)


def _plain_run(stage, fn, *args, **kwargs):
    return fn(*args, **kwargs)


def _mlp_fwd(u, w_up, w_down, tag, run=_plain_run):
    act = run(f"mlp_up_{tag}", _matmul, u, w_up, mode="nn", out_dtypes=(BF16,), epilogue=_sq_relu_epilogue, b_shards=True,
              tm=BIG_TILE, name=f"mlp_up_{tag}")
    f = run(f"mlp_down_{tag}", _matmul, act, w_down, mode="nn", out_dtypes=(BF16,), tk=BIG_TILE, name=f"mlp_down_{tag}")
    return act, f


def _mlp_bwd(u, act, w_up, w_down, df, tag):
    dpre = _matmul(df, w_down, mode="nt", out_dtypes=(BF16,), epilogue=_sq_relu_bwd_epilogue,
                   extras=((act, "tile"),), name=f"mlp_dact_{tag}")
    dw_down = _matmul(act, df, mode="tn", out_dtypes=(BF16,), tk=BIG_TILE, name=f"mlp_dwdown_{tag}")
    du = _matmul(dpre, w_up, mode="nt", out_dtypes=(BF16,), b_shards=True, tm=BIG_TILE, name=f"mlp_du_{tag}")
    dw_up = _matmul(u, dpre, mode="tn", out_dtypes=(BF16,), out_shards=True, tk=BIG_TILE, name=f"mlp_dwup_{tag}")
    return du, dw_up, dw_down


def _head_param_rows(p):
    return jnp.broadcast_to(p.reshape(SSD_N_GROUPS, SSD_HPG, 1), (SSD_N_GROUPS, SSD_HPG, LANES))


def _local_step(x, target, wts, comm=None, u0=None):
    t = x.shape[0]
    wts = dict(wts)
    row = lambda v: v.reshape(1, -1)
    mix_pre, mix_post, ffn_pre, ffn_post = wts["mix_pre_norm"], wts["mix_post_norm"], wts["ffn_pre_norm"], wts["ffn_post_norm"]

    def gathering(stage, fn, *args, **kwargs):
        hook = comm.gather_hook(stage) if comm is not None else None
        if hook is None:
            return fn(*args, **kwargs)
        out, got = fn(*args, hook=hook, **kwargs)
        wts.update(comm.weights_from(stage, got))
        return out

    if u0 is None:
        u0 = _rms_fwd(x, row(mix_pre[0]), name="rms_pre_mix0")
    zx, dt_raw = gathering("in_proj", _matmul, u0, wts["ssd_w_in"], mode="nn", out_dtypes=(BF16,), tn=SSD_IN_TILE,
                           f32_block=SSD_DT_COL - (SSD_IN_PAD - SSD_IN_TILE),
                           name="ssd_in_proj")
    xc = gathering("conv", _conv_fwd, zx, wts["ssd_conv_w"], row(wts["ssd_conv_b"]), name="ssd_conv_fwd")
    bias_row = jnp.pad(wts["ssd_dt_bias"], (0, LANES - SSD_N_HEADS)).reshape(1, LANES)
    alog_row = jnp.pad(wts["ssd_a_log"], (0, LANES - SSD_N_HEADS)).reshape(1, LANES)
    dtr, cumr = _softplus_fwd(dt_raw, bias_row, alog_row, name="ssd_dt_fwd")
    alog_b, d_b = _head_param_rows(wts["ssd_a_log"]), _head_param_rows(wts["ssd_d"])
    y_ssd, states = gathering("scan", _ssd_fwd, xc, dtr, cumr, alog_b, d_b, name="ssd_scan_fwd")
    norm_w = row(wts["ssd_norm_w"])
    yn = _gate_norm_fwd(y_ssd, zx, norm_w, name="ssd_gate_norm_fwd")
    mix0 = _matmul(yn, wts["ssd_w_out"], mode="nn", out_dtypes=(BF16,), tk=BIG_TILE, name="ssd_out_proj")
    h1, v0 = _rms_fwd(mix0, row(mix_post[0]), resid=x, want_u=row(ffn_pre[0]), name="rms_post_mix0")
    act0, f0 = _mlp_fwd(v0, wts["mlp_w_up0"], wts["mlp_w_down0"], "l0", run=gathering)
    h2, u1 = _rms_fwd(f0, row(ffn_post[0]), resid=h1, want_u=row(mix_pre[1]), name="rms_post_ffn0")

    qkv = _matmul(u1, wts["attn_w_qkv"], mode="nn", out_dtypes=(BF16,), epilogue=_bias_epilogue,
                  extras=((row(wts["attn_b_qkv"]), "row"),), b_shards=True, name="attn_qkv_proj")
    sinks_b = jnp.broadcast_to(wts["attn_sinks"].reshape(ATTN_N_Q, 1), (ATTN_N_Q, LANES))
    ao = gathering("attn_fwd", _attn_fwd, qkv, sinks_b, name="attn_fwd")
    mix1 = _matmul(ao, wts["attn_w_o"], mode="nn", out_dtypes=(BF16,), epilogue=_bias_epilogue,
                   extras=((row(wts["attn_b_o"]), "row"),), name="attn_out_proj")
    h3, v1 = _rms_fwd(mix1, row(mix_post[1]), resid=h2, want_u=row(ffn_pre[1]), name="rms_post_mix1")
    act1, f1 = _mlp_fwd(v1, wts["mlp_w_up1"], wts["mlp_w_down1"], "l1")
    dh4, loss_tile = _rms_fwd(f1, row(ffn_post[1]), resid=h3, target=target, name="rms_post_ffn1_loss")

    df1, g_ffn_post1 = _rms_bwd(f1, row(ffn_post[1]), dh4, out_dtype=BF16, name="rms_post_ffn1_bwd")
    dv1, g_up1, g_down1 = _mlp_bwd(v1, act1, wts["mlp_w_up1"], wts["mlp_w_down1"], df1, "l1")
    dh3, g_ffn_pre1 = _rms_bwd(h3, row(ffn_pre[1]), dv1, resid=dh4, name="rms_pre_ffn1_bwd")
    dmix1, g_mix_post1, g_b_o = _rms_bwd(mix1, row(mix_post[1]), dh3, out_dtype=BF16, dx_col_sum=True, name="rms_post_mix1_bwd")
    g_w_o = _matmul(ao, dmix1, mode="tn", out_dtypes=(BF16,), tk=BIG_TILE, name="attn_dwo")
    dao = _matmul(dmix1, wts["attn_w_o"], mode="nt", out_dtypes=(BF16,), name="attn_dao")
    dq, dkv, g_sinks = _attn_bwd(qkv, sinks_b, dao, name="attn_bwd")
    dqkv = jnp.concatenate([dq, dkv], axis=1)
    g_b_qkv = _col_sum(dqkv, name="attn_bqkv_grad")
    g_w_qkv = _matmul(u1, dqkv, mode="tn", out_dtypes=(BF16,), tn=ATTN_QKV // N_CHIPS, out_shards=True, tk=BIG_TILE, name="attn_dwqkv")
    du1 = _matmul(dqkv, wts["attn_w_qkv"], mode="nt", out_dtypes=(BF16,), b_shards=True, name="attn_du")
    dh2, g_mix_pre1 = _rms_bwd(h2, row(mix_pre[1]), du1, resid=dh3, name="rms_pre_mix1_bwd")

    df0, g_ffn_post0 = _rms_bwd(f0, row(ffn_post[0]), dh2, out_dtype=BF16, name="rms_post_ffn0_bwd")
    dv0, g_up0, g_down0 = _mlp_bwd(v0, act0, wts["mlp_w_up0"], wts["mlp_w_down0"], df0, "l0")
    dh1, g_ffn_pre0 = _rms_bwd(h1, row(ffn_pre[0]), dv0, resid=dh2, name="rms_pre_ffn0_bwd")
    dmix0, g_mix_post0 = _rms_bwd(mix0, row(mix_post[0]), dh1, out_dtype=BF16, name="rms_post_mix0_bwd")
    g_w_out = _matmul(yn, dmix0, mode="tn", out_dtypes=(BF16,), tk=BIG_TILE, name="ssd_dwout")
    dyn = _matmul(dmix0, wts["ssd_w_out"], mode="nt", out_dtypes=(BF16,), name="ssd_dyn")
    dy_ssd, dzx, g_norm_w = _gate_norm_bwd(y_ssd, zx, norm_w, dyn, name="ssd_gate_norm_bwd")
    mats = {"ssd_w_out": g_w_out, "attn_w_qkv": g_w_qkv, "attn_w_o": g_w_o,
            "mlp_w_up0": g_up0, "mlp_w_up1": g_up1, "mlp_w_down0": g_down0, "mlp_w_down1": g_down1}
    if comm is None:
        dxc, dbm, dcm, ddt_r, dpar = _ssd_bwd(xc, dtr, cumr, alog_b, d_b, states, dy_ssd, name="ssd_scan_bwd")
    else:
        (dxc, dbm, dcm, ddt_r, dpar), received = _ssd_bwd(xc, dtr, cumr, alog_b, d_b, states, dy_ssd,
                                                          name="ssd_scan_bwd", hook=comm.exchange_hook(mats, "early"))
        comm.received(received)
    dzx, g_conv_w, g_conv_b = _conv_bwd(zx, wts["ssd_conv_w"], row(wts["ssd_conv_b"]), dxc, dbm, dcm, dzx, name="ssd_conv_bwd")
    dzx, g_dt_bias = _softplus_bwd(dt_raw, bias_row, ddt_r, dzx, name="ssd_dt_bwd")
    g_w_in = _w_in_to_shards(_matmul(u0, dzx, mode="tn", out_dtypes=(BF16,), tn=SSD_IN_TILE, tk=BIG_TILE, name="ssd_dwin"), name="ssd_dwin_shards")
    mats["ssd_w_in"] = g_w_in
    if comm is None:
        du0 = _matmul(dzx, wts["ssd_w_in"], mode="nt", out_dtypes=(BF16,), tk=SSD_IN_TILE, name="ssd_du")
    else:
        du0, received = _matmul(dzx, wts["ssd_w_in"], mode="nt", out_dtypes=(BF16,), tk=SSD_IN_TILE, name="ssd_du",
                                hook=comm.exchange_hook(mats, "late"))
        comm.received(received)
    grad_x, g_mix_pre0 = _rms_bwd(x, row(mix_pre[0]), du0, resid=dh1, name="rms_pre_mix0_bwd")

    dpar = dpar.reshape(SSD_N_HEADS, LANES)
    vecs = {
        "ssd_conv_w": g_conv_w, "ssd_conv_b": g_conv_b.reshape(-1),
        "ssd_dt_bias": g_dt_bias[0, :SSD_N_HEADS], "ssd_a_log": dpar[:, 0], "ssd_d": dpar[:, 1],
        "ssd_norm_w": g_norm_w.reshape(-1), "attn_b_qkv": g_b_qkv.reshape(-1), "attn_sinks": g_sinks[:, 0],
        "attn_b_o": g_b_o.reshape(-1),
        "mix_pre_norm": jnp.concatenate([g_mix_pre0, g_mix_pre1]), "mix_post_norm": jnp.concatenate([g_mix_post0, g_mix_post1]),
        "ffn_pre_norm": jnp.concatenate([g_ffn_pre0, g_ffn_pre1]), "ffn_post_norm": jnp.concatenate([g_ffn_post0, g_ffn_post1]),
    }
    return loss_tile, grad_x, mats, vecs


def _mesh_position():
    return lax.axis_index("x"), lax.axis_index("y"), lax.axis_index("c")


def _flip(v, bit):
    return 1 - v if bit else v


OTHER_CHIPS = ((1, 0), (0, 1), (1, 1))


def _comm_params():
    return pltpu.CompilerParams(vmem_limit_bytes=VMEM_LIMIT)


def _staged_copies(srcs, dsts, bufs, sems_in, sems_out):
    loads = [pltpu.make_async_copy(s, b, sems_in.at[i]) for i, (s, b) in enumerate(zip(srcs, bufs))]
    stores = [pltpu.make_async_copy(b, d, sems_out.at[i]) for i, (b, d) in enumerate(zip(bufs, dsts))]
    return loads, stores


class _GatherHook:
    def __init__(self, mats, vecs=()):
        self.arrs = list(mats) + list(vecs)
        self.nm, self.n = len(mats), len(self.arrs)
        n_ici, n_fwd = (N_CHIPS - 1) * self.n, max((N_CHIPS - 1) * self.nm, 1)
        dma = pltpu.SemaphoreType.DMA
        self.out_shape = [jax.ShapeDtypeStruct((N_CHIPS,) + a.shape, a.dtype) for a in self.arrs]
        self.scratch = [pltpu.VMEM(a.shape, a.dtype) for a in self.arrs] + [
            dma((n_ici,)), dma((n_ici,)), dma((n_fwd,)), dma((n_fwd,)), dma((self.n,)), dma((self.n,))]

    def plan(self, ins, outs, scratch):
        n, nm = self.n, self.nm
        bufs = scratch[:n]
        ici_send, ici_recv, fwd_send, fwd_recv, load_sems, store_sems = scratch[n:]
        xi, yi, ci = _mesh_position()
        me = 2 * xi + yi
        loads, stores = _staged_copies(ins, [outs[i].at[me] for i in range(n)], bufs, load_sems, store_sems)
        sends, landed, forwards, from_sibling = [], [], [], []
        for j, (bx, by) in enumerate(OTHER_CHIPS):
            px, py = _flip(xi, bx), _flip(yi, by)
            peer = 2 * px + py
            for i in range(n):
                k = j * n + i
                mk = functools.partial(pltpu.make_async_remote_copy, send_sem=ici_send.at[k], recv_sem=ici_recv.at[k],
                                       device_id=(px, py, ci), device_id_type=MESH)
                if i < nm:
                    sends.append(mk(src_ref=ins[i].at[ci], dst_ref=outs[i].at[me, ci]))
                    landed.append(mk(src_ref=ins[i].at[ci], dst_ref=outs[i].at[peer, ci]))
                    kf = j * nm + i
                    fw = functools.partial(pltpu.make_async_remote_copy, send_sem=fwd_send.at[kf], recv_sem=fwd_recv.at[kf],
                                           device_id=(xi, yi, 1 - ci), device_id_type=MESH)
                    forwards.append(fw(src_ref=outs[i].at[peer, ci], dst_ref=outs[i].at[peer, ci]))
                    from_sibling.append(fw(src_ref=outs[i].at[peer, ci], dst_ref=outs[i].at[peer, 1 - ci]))
                else:
                    sends.append(mk(src_ref=ins[i], dst_ref=outs[i].at[me]))
                    landed.append(mk(src_ref=ins[i], dst_ref=outs[i].at[peer]))
                    forwards.append(None)
        return loads, stores, sends, landed, forwards, from_sibling

    @staticmethod
    def start(p):
        loads, _, sends, _, _, _ = p
        for cp in loads + sends:
            cp.start()

    @staticmethod
    def relay(p):
        loads, stores, _, landed, forwards, _ = p
        for ld, st in zip(loads, stores):
            ld.wait()
            st.start()
        for cp, fw in zip(landed, forwards):
            cp.wait_recv()
            if fw is not None:
                fw.start()

    @staticmethod
    def finish(p):
        _, stores, sends, _, forwards, from_sibling = p
        for cp in from_sibling:
            cp.wait_recv()
        for cp in sends + [fw for fw in forwards if fw is not None]:
            cp.wait_send()
        for st in stores:
            st.wait()


def _run_hook(hook, ins, outs, scratch, step, n_steps):
    p = hook.plan(ins, outs, scratch)
    relay_step = min(max(1, (3 * n_steps) // 4), n_steps - 1)

    @pl.when(step == 0)
    def _():
        hook.start(p)

    if relay_step < n_steps - 1:
        @pl.when(step == relay_step)
        def _():
            hook.relay(p)

    @pl.when(step == n_steps - 1)
    def _():
        if relay_step == n_steps - 1:
            hook.relay(p)
        hook.finish(p)


def _hook_call(hook, *, name):
    n = len(hook.arrs)

    def body(*refs):
        p = hook.plan(refs[:n], refs[n:n + len(hook.out_shape)], refs[n + len(hook.out_shape):])
        hook.start(p)
        hook.relay(p)
        hook.finish(p)

    return pl.pallas_call(
        body, in_specs=[ANY] * n, out_specs=[ANY] * len(hook.out_shape), out_shape=hook.out_shape,
        scratch_shapes=hook.scratch, compiler_params=_comm_params(), name=name)(*hook.arrs)


def _send_other_half(parts, *, name):
    n = len(parts)

    def body(*refs):
        ins, outs = refs[:n], refs[n:2 * n]
        send_sems, recv_sems = refs[2 * n:]
        xi, yi, ci = _mesh_position()
        sibling = (xi, yi, 1 - ci)
        for i in range(n):
            for s in range(N_CHIPS):
                pltpu.make_async_remote_copy(src_ref=ins[i].at[s, 1 - ci], dst_ref=outs[i].at[s], send_sem=send_sems.at[i],
                                             recv_sem=recv_sems.at[i], device_id=sibling, device_id_type=MESH).start()
        for i in range(n):
            pltpu.make_async_remote_copy(src_ref=outs[i], dst_ref=outs[i], send_sem=send_sems.at[i], recv_sem=recv_sems.at[i],
                                         device_id=sibling, device_id_type=MESH).wait()

    return pl.pallas_call(
        body, in_specs=[ANY] * n, out_specs=[ANY] * n,
        out_shape=[jax.ShapeDtypeStruct((p.shape[0],) + p.shape[2:], p.dtype) for p in parts],
        scratch_shapes=[pltpu.SemaphoreType.DMA((n,)), pltpu.SemaphoreType.DMA((n,))],
        name=name)(*parts)


ROW_BLOCKS = 8


def _add_sibling_half(parts, theirs, core, *, name):
    n = len(parts)

    def body(core_ref, *refs):
        for a_ref, b_ref, o_ref in zip(refs[:n], refs[n:2 * n], refs[2 * n:]):
            o_ref[...] = (a_ref[...].astype(F32) + b_ref[...].astype(F32)).astype(o_ref.dtype)

    mine = lambda p: pl.BlockSpec((None, None, p.shape[2] // ROW_BLOCKS, p.shape[3]), lambda s, rb, core_ref: (s, core_ref[0], rb, 0))
    other = lambda p: pl.BlockSpec((None, p.shape[1] // ROW_BLOCKS, p.shape[2]), lambda s, rb, core_ref: (s, rb, 0))
    return pl.pallas_call(
        body,
        grid_spec=pltpu.PrefetchScalarGridSpec(
            num_scalar_prefetch=1, grid=(N_CHIPS, ROW_BLOCKS),
            in_specs=[mine(p) for p in parts] + [other(q) for q in theirs], out_specs=[other(q) for q in theirs]),
        out_shape=[jax.ShapeDtypeStruct(q.shape, BF16) for q in theirs],
        compiler_params=_params("parallel", "parallel"), name=name)(core, *parts, *theirs)


class _ExchangeHook:
    def __init__(self, parts, to_all=()):
        self.arrs = list(parts) + list(to_all)
        self.n_parts, self.n = len(parts), len(self.arrs)
        n_ici, n_peer = max((N_CHIPS - 1) * self.n_parts, 1), (N_DEV - 1) * max(len(to_all), 1)
        dma = pltpu.SemaphoreType.DMA
        self.out_shape = [jax.ShapeDtypeStruct(p.shape, p.dtype) for p in parts] + [
            jax.ShapeDtypeStruct((N_DEV,) + a.shape, a.dtype) for a in to_all]
        self.scratch = [pltpu.VMEM(p.shape[1:], p.dtype) for p in parts] + [pltpu.VMEM(a.shape, a.dtype) for a in to_all] + [
            dma((n_ici,)), dma((n_ici,)), dma((n_peer,)), dma((n_peer,)), dma((self.n,)), dma((self.n,))]

    def plan(self, ins, outs, scratch):
        n, npt = self.n, self.n_parts
        bufs = scratch[:n]
        send_sems, recv_sems, all_send, all_recv, load_sems, store_sems = scratch[n:]
        xi, yi, ci = _mesh_position()
        me_chip = 2 * xi + yi
        me = 4 * xi + 2 * yi + ci
        loads, stores = _staged_copies([ins[i].at[me_chip] for i in range(npt)] + list(ins[npt:]),
                                       [outs[i].at[me_chip] for i in range(npt)] + [outs[i].at[me] for i in range(npt, n)],
                                       bufs, load_sems, store_sems)
        sends, recvs = [], []
        for j, (bx, by) in enumerate(OTHER_CHIPS):
            px, py = _flip(xi, bx), _flip(yi, by)
            peer = 2 * px + py
            for i in range(npt):
                k = j * npt + i
                mk = functools.partial(pltpu.make_async_remote_copy, src_ref=ins[i].at[peer], send_sem=send_sems.at[k],
                                       recv_sem=recv_sems.at[k], device_id=(px, py, ci), device_id_type=MESH)
                sends.append(mk(dst_ref=outs[i].at[me_chip]))
                recvs.append(mk(dst_ref=outs[i].at[peer]))
        for i in range(npt, n):
            for k in range(1, N_DEV):
                px, py, pc = _flip(xi, (k >> 2) & 1), _flip(yi, (k >> 1) & 1), _flip(ci, k & 1)
                slot = (i - npt) * (N_DEV - 1) + k - 1
                mk = functools.partial(pltpu.make_async_remote_copy, src_ref=ins[i], send_sem=all_send.at[slot],
                                       recv_sem=all_recv.at[slot], device_id=(px, py, pc), device_id_type=MESH)
                sends.append(mk(dst_ref=outs[i].at[me]))
                recvs.append(mk(dst_ref=outs[i].at[4 * px + 2 * py + pc]))
        return loads, stores, sends, recvs

    @staticmethod
    def start(p):
        loads, _, sends, _ = p
        for cp in loads + sends:
            cp.start()

    @staticmethod
    def relay(p):
        loads, stores, _, _ = p
        for ld, st in zip(loads, stores):
            ld.wait()
            st.start()

    @staticmethod
    def finish(p):
        _, stores, sends, recvs = p
        for cp in recvs:
            cp.wait_recv()
        for cp in sends:
            cp.wait_send()
        for st in stores:
            st.wait()


def _sum_chips(parts, *, name):
    n = len(parts)
    p = parts[0].shape[0]

    def body(*refs):
        s = pl.program_id(1)
        for x_ref, o_ref in zip(refs[:n], refs[n:]):
            @pl.when(s == 0)
            def _():
                o_ref[...] = x_ref[...].astype(F32)

            @pl.when(s > 0)
            def _():
                o_ref[...] += x_ref[...].astype(F32)

    blocks = lambda q: ROW_BLOCKS if q.shape[1] % (8 * ROW_BLOCKS) == 0 else 1
    assert len({blocks(q) for q in parts}) == 1
    nb = blocks(parts[0])
    return pl.pallas_call(
        body, grid=(nb, p),
        in_specs=[pl.BlockSpec((None, q.shape[1] // nb, q.shape[2]), lambda rb, s: (s, rb, 0)) for q in parts],
        out_specs=[pl.BlockSpec((q.shape[1] // nb, q.shape[2]), lambda rb, s: (rb, 0)) for q in parts],
        out_shape=[jax.ShapeDtypeStruct(q.shape[1:], F32) for q in parts],
        compiler_params=_params("parallel", "arbitrary"), name=name)(*parts)


def _swap_halves(halves, layers, *, name, hook=None):
    n = len(halves)
    out_shapes, slots = [], []
    for i, h in enumerate(halves):
        pair = [p for p in layers if i in p]
        if pair and pair[0][1] == i:
            slots.append((slots[pair[0][0]][0], 1))
        elif pair:
            out_shapes.append(jax.ShapeDtypeStruct((2, 2) + h.shape, h.dtype))
            slots.append((len(out_shapes) - 1, 0))
        else:
            out_shapes.append(jax.ShapeDtypeStruct((2,) + h.shape, h.dtype))
            slots.append((len(out_shapes) - 1, None))
    n_out = len(out_shapes)
    hk = _HookSlots(hook, n_in=n, n_out=n_out, n_scratch=n + 4)

    def body(*refs):
        ins, outs, scratch = hk.own(refs)
        bufs = scratch[:n]
        send_sems, recv_sems, load_sems, store_sems = scratch[n:]
        if hook is not None:
            _, h_in, _, h_out, _, h_scratch = hk._split(refs)
            extra = hook.plan(h_in, h_out, h_scratch)
            hook.start(extra)
        xi, yi, ci = _mesh_position()
        own, sends, recvs = [], [], []
        for i in range(n):
            o, layer = slots[i]
            dst = (lambda core: outs[o].at[core]) if layer is None else (lambda core: outs[o].at[layer, core])
            own.append(dst(ci))
            mk = functools.partial(pltpu.make_async_remote_copy, src_ref=ins[i], send_sem=send_sems.at[i],
                                   recv_sem=recv_sems.at[i], device_id=(xi, yi, 1 - ci), device_id_type=MESH)
            sends.append(mk(dst_ref=dst(ci)))
            recvs.append(mk(dst_ref=dst(1 - ci)))
        loads, stores = _staged_copies(ins, own, bufs, load_sems, store_sems)
        for cp in loads + sends:
            cp.start()
        for ld, st in zip(loads, stores):
            ld.wait()
            st.start()
        for cp in recvs:
            cp.wait_recv()
        for cp in sends:
            cp.wait_send()
        for st in stores:
            st.wait()
        if hook is not None:
            hook.relay(extra)
            hook.finish(extra)

    outs = pl.pallas_call(
        body, in_specs=[ANY] * n + hk.in_specs, out_specs=[ANY] * n_out + hk.out_specs, out_shape=out_shapes + hk.out_shape,
        scratch_shapes=[pltpu.VMEM(h.shape, h.dtype) for h in halves]
        + [pltpu.SemaphoreType.DMA((n,)), pltpu.SemaphoreType.DMA((n,)), pltpu.SemaphoreType.DMA((n,)), pltpu.SemaphoreType.DMA((n,))]
        + hk.scratch,
        compiler_params=_comm_params(), name=name)(*halves, *hk.inputs)
    return outs if hook is None else (outs[:n_out], outs[n_out:])


def _cast_bf16(layers, x, norm_w, *, name, hook=None):
    n = len(layers)
    hk = _HookSlots(hook, n_in=n + 2, n_out=n + 1, n_scratch=0)

    def body(*refs):
        ins, outs, _ = hk.own(refs)
        if hook is not None:
            hk.run(refs, pl.program_id(0), ROW_BLOCKS)
        for i_ref, o_ref in zip(ins[:n], outs[:n]):
            o_ref[...] = i_ref[...].astype(o_ref.dtype)
        xv = ins[n][...]
        outs[n][...] = (xv * lax.rsqrt(jnp.mean(xv * xv, axis=-1, keepdims=True) + NORM_EPS) * ins[n + 1][...]).astype(BF16)

    in_blk = lambda a, l: pl.BlockSpec((None, a.shape[1] // ROW_BLOCKS, a.shape[2]), lambda i: (l, i, 0))
    out_blk = lambda a: pl.BlockSpec((a.shape[1] // ROW_BLOCKS, a.shape[2]), lambda i: (i, 0))
    x_blk = pl.BlockSpec((x.shape[0] // ROW_BLOCKS, x.shape[1]), lambda i: (i, 0))
    outs = pl.pallas_call(
        body, grid=(ROW_BLOCKS,),
        in_specs=[in_blk(a, l) for a, l in layers] + [x_blk, pl.BlockSpec((1, x.shape[1]), lambda i: (0, 0))] + hk.in_specs,
        out_specs=[out_blk(a) for a, _ in layers] + [x_blk] + hk.out_specs,
        out_shape=[jax.ShapeDtypeStruct(a.shape[1:], BF16) for a, _ in layers] + [jax.ShapeDtypeStruct(x.shape, BF16)] + hk.out_shape,
        scratch_shapes=hk.scratch,
        compiler_params=_params(*hk.semantics("parallel")), name=name)(*[a for a, _ in layers], x, norm_w, *hk.inputs)
    own = (outs[:n], outs[n])
    return own if hook is None else (own, outs[n + 1:])


def _full_weight(name, gathered):
    s, _, r, c = gathered.shape
    if name == "ssd_w_in":
        return _w_in_from_shards(gathered.reshape(s, 2 * r, c), name="ssd_w_in_unshard")
    if name in ("attn_w_qkv", "mlp_w_up0", "mlp_w_up1"):
        return gathered.reshape(s, 2 * r, c)
    return gathered.reshape(s * 2 * r, c)


class _StepComm:
    GATHER = {"in_proj": ("mlp_w_up0", "attn_w_o"), "conv": ("mlp_w_down0",), "scan": ("ssd_w_out", "mlp_w_up1"),
              "mlp_up_l0": ("attn_w_qkv",), "attn_fwd": ("mlp_w_down1",)}
    EXCHANGE = {"early": ("ssd_w_out", "attn_w_qkv", "attn_w_o", "mlp_w_up0", "mlp_w_up1", "mlp_w_down0", "mlp_w_down1"),
                "late": ("ssd_w_in",)}

    def __init__(self, shards, core):
        self.shards, self.core = shards, core
        self.chip_parts = {}
        self._pending = None

    def gather_hook(self, stage):
        names = self.GATHER.get(stage)
        return _GatherHook([self.shards[n] for n in names]) if names else None

    def weights_from(self, stage, gathered):
        return {n: _full_weight(n, g) for n, g in zip(self.GATHER[stage], gathered)}

    def chip_sums(self, mats, tag):
        parts = [_shard_halves(a) for a in mats.values()]
        theirs = _send_other_half(parts, name=f"grad_sibling_send_{tag}")
        return _add_sibling_half(parts, theirs, self.core, name=f"grad_chip_sum_{tag}")

    def exchange_hook(self, mats, which):
        self._pending = self.EXCHANGE[which]
        return _ExchangeHook(self.chip_sums({n: mats[n] for n in self._pending}, which))

    def received(self, arrays):
        self.chip_parts.update(zip(self._pending, arrays))


ADAMW_ROW_BLOCKS = 16


def _adamw(ws, gs, ms, vs, *, name, by_lanes=False):
    n = len(ws)
    if by_lanes:
        nb = min(a.shape[2] for a in ws) // LANES
    else:
        nb = ADAMW_ROW_BLOCKS if all(a.shape[1] % (8 * ADAMW_ROW_BLOCKS) == 0 for a in ws) else 1

    def body(*refs):
        ins, outs = refs[:4 * n], refs[4 * n:]
        for i in range(n):
            w_ref, g_ref, m_ref, v_ref = ins[i], ins[n + i], ins[2 * n + i], ins[3 * n + i]
            go_ref, d_ref, nm_ref, nv_ref = outs[i], outs[n + i], outs[2 * n + i], outs[3 * n + i]
            gv = g_ref[...]
            nm = ADAM_B1 * m_ref[...] + (1.0 - ADAM_B1) * gv
            nv = ADAM_B2 * v_ref[...] + (1.0 - ADAM_B2) * (gv * gv)
            m_hat = nm / (1.0 - ADAM_B1 ** ADAM_STEP)
            v_hat = nv / (1.0 - ADAM_B2 ** ADAM_STEP)
            go_ref[...] = gv
            d_ref[...] = -ADAM_LR * (m_hat / (jnp.sqrt(v_hat) + ADAM_EPS) + ADAM_WD * w_ref[...])
            nm_ref[...] = nm
            nv_ref[...] = nv

    if by_lanes:
        blks = [pl.BlockSpec((a.shape[0], a.shape[1], a.shape[2] // nb), lambda i: (0, 0, i)) for a in ws]
    else:
        blks = [pl.BlockSpec((a.shape[0], a.shape[1] // nb, a.shape[2]), lambda i: (0, i, 0)) for a in ws]
    shapes = [jax.ShapeDtypeStruct(a.shape, F32) for a in ws]
    outs = pl.pallas_call(body, grid=(nb,), in_specs=blks * 4, out_specs=blks * 4, out_shape=shapes * 4,
                          compiler_params=_params("parallel"), name=name)(*ws, *gs, *ms, *vs)
    return [tuple(outs[k * n + i] for k in range(4)) for i in range(n)]


SM_CONV_B, SM_NORM_W, SM_MIX_PRE, SM_MIX_POST, SM_FFN_PRE, SM_FFN_POST, SM_MISC, SM_CONV_W, SM_B_QKV, SM_B_O = 0, 4, 6, 8, 10, 12, 14, 16, 32, 34
SM_ROWS = 40
MISC_DT_BIAS, MISC_A_LOG, MISC_D, MISC_SINKS, MISC_LOSS = 0, 32, 64, 96, 112


def _shard_halves(a):
    c = a.shape[-1]
    return a.reshape(N_CHIPS, 2, -1, c)


def _rows(v):
    return v.reshape(-1, D_MODEL)


def _misc_row(dt_bias, a_log, d, sinks, loss):
    pad = jnp.zeros((D_MODEL - MISC_LOSS - 1,), F32)
    return jnp.concatenate([dt_bias.reshape(-1), a_log.reshape(-1), d.reshape(-1), sinks.reshape(-1), loss.reshape(1), pad]).reshape(1, D_MODEL)


def _replicated_rows(p, loss):
    return jnp.concatenate([
        _rows(p["ssd_conv_b"]), _rows(p["ssd_norm_w"]), _rows(p["mix_pre_norm"]), _rows(p["mix_post_norm"]),
        _rows(p["ffn_pre_norm"]), _rows(p["ffn_post_norm"]),
        _misc_row(p["ssd_dt_bias"], p["ssd_a_log"], p["ssd_d"], p["attn_sinks"], loss), jnp.zeros((1, D_MODEL), F32)], axis=0)


def _sharded_rows(conv_w, b_qkv, b_o):
    last = jnp.concatenate([b_qkv.reshape(-1), b_o.reshape(-1), jnp.zeros((D_MODEL - 640,), F32)]).reshape(1, D_MODEL)
    return jnp.concatenate([conv_w.reshape(SSD_CONV_WIDTH, D_MODEL), last, jnp.zeros((3, D_MODEL), F32)], axis=0)


REPLICATED = ("ssd_conv_b", "ssd_dt_bias", "ssd_a_log", "ssd_d", "ssd_norm_w", "attn_sinks",
              "mix_pre_norm", "mix_post_norm", "ffn_pre_norm", "ffn_post_norm")
MATRICES = ("ssd_w_in", "ssd_w_out", "attn_w_qkv", "attn_w_o", "mlp_w_up", "mlp_w_down")
WEIGHT_NAMES = ("ssd_w_in", "ssd_conv_w", "ssd_conv_b", "ssd_dt_bias", "ssd_a_log", "ssd_d", "ssd_norm_w", "ssd_w_out",
                "attn_w_qkv", "attn_b_qkv", "attn_sinks", "attn_w_o", "attn_b_o", "mlp_w_up", "mlp_w_down",
                "mix_pre_norm", "mix_post_norm", "ffn_pre_norm", "ffn_post_norm")


def _unpack_small(rows16, rows8, like):
    misc = rows16[SM_MISC]
    out = {
        "ssd_conv_b": rows16[SM_CONV_B:SM_CONV_B + 4], "ssd_norm_w": rows16[SM_NORM_W:SM_NORM_W + 2],
        "mix_pre_norm": rows16[SM_MIX_PRE:SM_MIX_PRE + 2], "mix_post_norm": rows16[SM_MIX_POST:SM_MIX_POST + 2],
        "ffn_pre_norm": rows16[SM_FFN_PRE:SM_FFN_PRE + 2], "ffn_post_norm": rows16[SM_FFN_POST:SM_FFN_POST + 2],
        "ssd_dt_bias": misc[MISC_DT_BIAS:MISC_DT_BIAS + 32], "ssd_a_log": misc[MISC_A_LOG:MISC_A_LOG + 32],
        "ssd_d": misc[MISC_D:MISC_D + 32], "attn_sinks": misc[MISC_SINKS:MISC_SINKS + 16],
        "ssd_conv_w": rows8[0:SSD_CONV_WIDTH], "attn_b_qkv": rows8[SSD_CONV_WIDTH, 0:384], "attn_b_o": rows8[SSD_CONV_WIDTH, 384:640],
    }
    return {k: v.reshape(like[k].shape) for k, v in out.items()}


def kernel(x, ssd_w_in, ssd_conv_w, ssd_conv_b, ssd_dt_bias, ssd_a_log, ssd_d, ssd_norm_w, ssd_w_out, attn_w_qkv, attn_b_qkv, attn_sinks, attn_w_o, attn_b_o, mlp_w_up, mlp_w_down, mix_pre_norm, mix_post_norm, ffn_pre_norm, ffn_post_norm, loss_target, m_ssd_w_in, m_ssd_conv_w, m_ssd_conv_b, m_ssd_dt_bias, m_ssd_a_log, m_ssd_d, m_ssd_norm_w, m_ssd_w_out, m_attn_w_qkv, m_attn_b_qkv, m_attn_sinks, m_attn_w_o, m_attn_b_o, m_mlp_w_up, m_mlp_w_down, m_mix_pre_norm, m_mix_post_norm, m_ffn_pre_norm, m_ffn_post_norm, v_ssd_w_in, v_ssd_conv_w, v_ssd_conv_b, v_ssd_dt_bias, v_ssd_a_log, v_ssd_d, v_ssd_norm_w, v_ssd_w_out, v_attn_w_qkv, v_attn_b_qkv, v_attn_sinks, v_attn_w_o, v_attn_b_o, v_mlp_w_up, v_mlp_w_down, v_mix_pre_norm, v_mix_post_norm, v_ffn_pre_norm, v_ffn_post_norm):
    w = dict(zip(WEIGHT_NAMES, (ssd_w_in, ssd_conv_w, ssd_conv_b, ssd_dt_bias, ssd_a_log, ssd_d, ssd_norm_w, ssd_w_out, attn_w_qkv, attn_b_qkv, attn_sinks, attn_w_o, attn_b_o, mlp_w_up, mlp_w_down, mix_pre_norm, mix_post_norm, ffn_pre_norm, ffn_post_norm)))
    m = dict(zip(WEIGHT_NAMES, (m_ssd_w_in, m_ssd_conv_w, m_ssd_conv_b, m_ssd_dt_bias, m_ssd_a_log, m_ssd_d, m_ssd_norm_w, m_ssd_w_out, m_attn_w_qkv, m_attn_b_qkv, m_attn_sinks, m_attn_w_o, m_attn_b_o, m_mlp_w_up, m_mlp_w_down, m_mix_pre_norm, m_mix_post_norm, m_ffn_pre_norm, m_ffn_post_norm)))
    v = dict(zip(WEIGHT_NAMES, (v_ssd_w_in, v_ssd_conv_w, v_ssd_conv_b, v_ssd_dt_bias, v_ssd_a_log, v_ssd_d, v_ssd_norm_w, v_ssd_w_out, v_attn_w_qkv, v_attn_b_qkv, v_attn_sinks, v_attn_w_o, v_attn_b_o, v_mlp_w_up, v_mlp_w_down, v_mix_pre_norm, v_mix_post_norm, v_ffn_pre_norm, v_ffn_post_norm)))
    chip = 2 * lax.axis_index("x") + lax.axis_index("y")

    two_halves = lambda a: a.reshape(2, a.shape[-2] // 2, a.shape[-1])
    later = {"ssd_w_out": (w["ssd_w_out"], 0), "attn_w_qkv": (w["attn_w_qkv"], 0), "attn_w_o": (w["attn_w_o"], 0),
             "mlp_w_up0": (w["mlp_w_up"], 0), "mlp_w_up1": (w["mlp_w_up"], 1),
             "mlp_w_down0": (w["mlp_w_down"], 0), "mlp_w_down1": (w["mlp_w_down"], 1)}
    first = _GatherHook([two_halves(w["ssd_w_in"].astype(BF16))], [w["ssd_conv_w"][0], w["attn_b_qkv"], w["attn_b_o"]])
    (cast, u0), (g_in, g_conv, g_bqkv, g_bo) = _cast_bf16(list(later.values()), x[0], w["mix_pre_norm"][0:1],
                                                          name="weights_to_bf16", hook=first)
    core = lax.axis_index("c").astype(jnp.int32).reshape(1)
    comm = _StepComm({k: two_halves(a) for k, a in zip(later, cast)}, core)
    full = {
        "ssd_w_in": _full_weight("ssd_w_in", g_in),
        "ssd_conv_w": g_conv.transpose(1, 0, 2).reshape(SSD_CONV_WIDTH, SSD_CONV_DIM),
        "attn_b_qkv": g_bqkv.reshape(ATTN_QKV), "attn_b_o": g_bo.reshape(D_MODEL),
    }
    for name in REPLICATED:
        full[name] = w[name][0] if name.startswith(("ssd_", "attn_")) else w[name]

    loss_tile, grad_x, gm, g = _local_step(x[0], loss_target[0], full, comm, u0)

    conv_w_rows = g["ssd_conv_w"].reshape(SSD_CONV_WIDTH * N_CHIPS, D_MODEL)
    b_qkv_rows = jnp.pad(g["attn_b_qkv"], (0, 2 * D_MODEL - ATTN_QKV)).reshape(2, D_MODEL)
    small = jnp.concatenate([_replicated_rows(g, loss_tile[0, 0]), conv_w_rows, b_qkv_rows, _rows(g["attn_b_o"]),
                             jnp.zeros((SM_ROWS - SM_B_O - 1, D_MODEL), F32)], axis=0)
    order = ("ssd_w_in", "ssd_w_out", "attn_w_qkv", "attn_w_o", "mlp_w_up0", "mlp_w_up1", "mlp_w_down0", "mlp_w_down1")
    halves = _sum_chips([comm.chip_parts[k] for k in order], name="grad_sum")
    (r_in, r_out, r_qkv, r_o, r_up, r_down), (small_all,) = _swap_halves(
        halves, layers=((4, 5), (6, 7)), hook=_ExchangeHook([], [small]), name="grad_halves_swap")
    small_sum, = _sum_chips([small_all], name="small_grad_sum")

    grads = {"ssd_w_in": r_in, "ssd_w_out": r_out, "attn_w_qkv": r_qkv, "attn_w_o": r_o, "mlp_w_up": r_up, "mlp_w_down": r_down}
    grads = {k: a.reshape(w[k].shape) for k, a in grads.items()}
    conv_w_g = lax.dynamic_index_in_dim(small_sum[SM_CONV_W:SM_CONV_W + 16].reshape(SSD_CONV_WIDTH, N_CHIPS, D_MODEL), chip, axis=1, keepdims=False)
    b_qkv_g = lax.dynamic_slice_in_dim(small_sum[SM_B_QKV:SM_B_QKV + 2].reshape(-1), chip * 384, 384)
    b_o_g = lax.dynamic_slice_in_dim(small_sum[SM_B_O], chip * 256, 256)
    small_g = jnp.concatenate([small_sum[0:16], _sharded_rows(conv_w_g, b_qkv_g, b_o_g)], axis=0)
    grads.update(_unpack_small(small_g[0:16], small_g[16:24], w))
    loss = small_sum[SM_MISC, MISC_LOSS]

    delta, new_m, new_v = {}, {}, {}
    stored = lambda a: jnp.swapaxes(a, 1, 2)
    rest = [name for name in MATRICES if name != "ssd_w_in"]
    mats = lambda p: [p[name] for name in rest]
    results = dict(zip(rest, _adamw(mats(w), mats(grads), mats(m), mats(v), name="adamw_matrices")))
    (w_in_result,) = _adamw([stored(w["ssd_w_in"])], [stored(grads["ssd_w_in"])], [stored(m["ssd_w_in"])],
                            [stored(v["ssd_w_in"])], by_lanes=True, name="adamw_ssd_w_in")
    results["ssd_w_in"] = tuple(stored(a) for a in w_in_result)
    for name in MATRICES:
        grads[name], delta[name], new_m[name], new_v[name] = results[name]
    zero = jnp.zeros((), F32)
    small_pack = lambda p: jnp.concatenate([_replicated_rows({k: p[k] for k in REPLICATED}, zero),
                                            _sharded_rows(p["ssd_conv_w"], p["attn_b_qkv"], p["attn_b_o"])], axis=0)[None]
    (_, d_s, m_s, v_s), = _adamw([small_pack(w)], [small_g[None]], [small_pack(m)], [small_pack(v)], name="adamw_vectors")
    d_s, m_s, v_s = d_s[0], m_s[0], v_s[0]
    delta.update(_unpack_small(d_s[0:16], d_s[16:24], w))
    new_m.update(_unpack_small(m_s[0:16], m_s[16:24], w))
    new_v.update(_unpack_small(v_s[0:16], v_s[16:24], w))

    return (loss, grad_x[None], *[grads[n] for n in WEIGHT_NAMES], *[delta[n] for n in WEIGHT_NAMES],
            *[new_m[n] for n in WEIGHT_NAMES], *[new_v[n] for n in WEIGHT_NAMES])
```

```python
import functools

import jax
import jax.numpy as jnp
from jax import lax
from jax.experimental import pallas as pl
from jax.experimental.pallas import tpu as pltpu

F32 = jnp.float32
BF16 = jnp.bfloat16

D_MODEL = 1024
SSD_D_INNER = 2048
SSD_HEAD_DIM = 64
SSD_N_HEADS = 32
SSD_N_GROUPS = 8
SSD_HPG = 4
SSD_D_STATE = 128
SSD_CONV_WIDTH = 4
SSD_CHUNK = 128
SSD_CONV_DIM = 4096
SSD_IN_DIM = 6176
SSD_IN_PAD = 6400
SSD_IN_TILE = 1280
SSD_DT_COL = 6144
SSD_GW = SSD_HPG * SSD_HEAD_DIM
ATTN_HEAD_DIM = 64
ATTN_N_Q = 16
ATTN_N_KV = 4
ATTN_REP = 4
ATTN_WINDOW = 128
ATTN_QKV = 1536
D_FF = 4096
NORM_EPS = 1e-6

ADAM_LR = 0.001
ADAM_B1 = 0.9
ADAM_B2 = 0.999
ADAM_EPS = 1e-08
ADAM_WD = 0.01
ADAM_STEP = 10

N_CHIPS = 4
N_DEV = 8
LANES = 128
VMEM_LIMIT = 48 * 1024 * 1024
BIG_TILE = 2048
MESH = pl.DeviceIdType.MESH


def _params(*sem):
    return pltpu.CompilerParams(dimension_semantics=sem, vmem_limit_bytes=VMEM_LIMIT)


def _dot(a, b, dims):
    return lax.dot_general(a, b, (dims, ((), ())), preferred_element_type=F32)


def _dot_nn(a, b):
    return _dot(a, b, ((1,), (0,)))


def _dot_nt(a, b):
    return _dot(a, b, ((1,), (1,)))


def _dot_tn(a, b):
    return _dot(a, b, ((0,), (0,)))


def _sigmoid(x):
    return 0.5 * jnp.tanh(0.5 * x) + 0.5


ANY = pl.BlockSpec(memory_space=pl.ANY)


class _HookSlots:
    def __init__(self, hook, n_in, n_out, n_scratch):
        self.hook = hook
        self.n_in, self.n_out, self.n_scratch = n_in, n_out, n_scratch
        self.inputs = list(hook.arrs) if hook else []
        self.out_shape = list(hook.out_shape) if hook else []
        self.scratch = list(hook.scratch) if hook else []
        self.in_specs = [ANY] * len(self.inputs)
        self.out_specs = [ANY] * len(self.out_shape)

    def _split(self, refs):
        a = self.n_in
        b = a + len(self.inputs)
        c = b + self.n_out
        d = c + len(self.out_shape)
        e = d + self.n_scratch
        return refs[:a], refs[a:b], refs[b:c], refs[c:d], refs[d:e], refs[e:]

    def own(self, refs):
        ins, _, outs, _, scratch, _ = self._split(refs)
        return ins, outs, scratch

    def run(self, refs, step, n_steps):
        _, h_in, _, h_out, _, h_scratch = self._split(refs)
        _run_hook(self.hook, h_in, h_out, h_scratch, step, n_steps)

    def semantics(self, *sem):
        return sem if self.hook is None else ("arbitrary",) * len(sem)


def _matmul(a, b, *, mode, out_dtypes, name, epilogue=None, extras=(), tm=1024, tn=1024, tk=1024,
            b_shards=False, out_shards=False, hook=None, f32_block=None):
    f32_tail = f32_block is not None
    if b_shards:
        s, b_rows, b_cols = b.shape
        b2 = (b_rows, s * b_cols)
        if mode == "nn":
            tn = b_cols
        else:
            assert mode == "nt"
            tk = b_cols
    else:
        b2 = b.shape
    if mode == "nn":
        (m, k), (k2, n) = a.shape, b2
    elif mode == "nt":
        (m, k), (n, k2) = a.shape, b2
    else:
        (k, m), (k2, n) = a.shape, b2
    assert k == k2, (a.shape, b.shape, mode)
    tm, tn, tk = min(tm, m), min(tn, n), min(tk, k)
    assert m % tm == 0 and n % tn == 0 and k % tk == 0, (m, n, k, tm, tn, tk)
    nk = k // tk
    if mode == "tn":
        a_spec = pl.BlockSpec((tk, tm), lambda i, j, kk: (kk, i))
    else:
        a_spec = pl.BlockSpec((tm, tk), lambda i, j, kk: (i, kk))
    if b_shards and mode == "nn":
        b_spec = pl.BlockSpec((None, tk, tn), lambda i, j, kk: (j, kk, 0))
    elif b_shards:
        b_spec = pl.BlockSpec((None, tn, tk), lambda i, j, kk: (kk, j, 0))
    elif mode == "nt":
        b_spec = pl.BlockSpec((tn, tk), lambda i, j, kk: (j, kk))
    else:
        b_spec = pl.BlockSpec((tk, tn), lambda i, j, kk: (kk, j))
    dims = {"nn": ((1,), (0,)), "nt": ((1,), (1,)), "tn": ((0,), (0,))}[mode]
    ex_specs = []
    for arr, kind in extras:
        if kind == "tile":
            ex_specs.append(pl.BlockSpec((tm, tn), lambda i, j, kk: (i, j)))
        else:
            ex_specs.append(pl.BlockSpec((1, tn), lambda i, j, kk: (0, j)))
    n_ex, n_out = len(extras), len(out_dtypes)
    if epilogue is None:
        epilogue = lambda acc: (acc,)
    hk = _HookSlots(hook, n_in=2 + n_ex, n_out=n_out + f32_tail, n_scratch=0 if nk == 1 else 1)
    grid = (m // tm, n // tn, nk)

    def body(*refs):
        (a_ref, b_ref, *ex), outs, scratch = hk.own(refs)
        if hook is not None:
            step = (pl.program_id(0) * grid[1] + pl.program_id(1)) * grid[2] + pl.program_id(2)
            hk.run(refs, step, grid[0] * grid[1] * grid[2])

        def finish(acc):
            res = epilogue(acc, *[e[...] for e in ex])
            for o, r in zip(outs, res):
                o[...] = r.astype(o.dtype)
            if f32_tail:
                outs[n_out][...] = acc[:, f32_block:f32_block + LANES]

        if nk == 1:
            finish(_dot(a_ref[...], b_ref[...], dims))
        else:
            acc_ref = scratch[0]
            kk = pl.program_id(2)

            @pl.when(kk == 0)
            def _():
                acc_ref[...] = jnp.zeros_like(acc_ref)

            acc_ref[...] += _dot(a_ref[...], b_ref[...], dims)

            @pl.when(kk == nk - 1)
            def _():
                finish(acc_ref[...])

    if out_shards:
        out_spec = pl.BlockSpec((None, tm, tn), lambda i, j, kk: (j, i, 0))
        out_dims = (n // tn, m, tn)
    else:
        out_spec = pl.BlockSpec((tm, tn), lambda i, j, kk: (i, j))
        out_dims = (m, n)
    tail_specs = [pl.BlockSpec((tm, LANES), lambda i, j, kk: (i, 0))] if f32_tail else []
    tail_shapes = [jax.ShapeDtypeStruct((m, LANES), F32)] if f32_tail else []
    outs = pl.pallas_call(
        body,
        grid=grid,
        in_specs=[a_spec, b_spec] + ex_specs + hk.in_specs,
        out_specs=[out_spec for _ in out_dtypes] + tail_specs + hk.out_specs,
        out_shape=[jax.ShapeDtypeStruct(out_dims, dt) for dt in out_dtypes] + tail_shapes + hk.out_shape,
        scratch_shapes=([] if nk == 1 else [pltpu.VMEM((tm, tn), F32)]) + hk.scratch,
        compiler_params=_params(*hk.semantics("parallel", "arbitrary" if f32_tail else "parallel", "arbitrary")),
        name=name,
    )(a, b, *[arr for arr, _ in extras], *hk.inputs)
    n_own = n_out + f32_tail
    own = outs[0] if n_own == 1 else outs[:n_own]
    return own if hook is None else (own, outs[n_own:])


def _row_tile(t, want):
    return min(t, want)


def _rms_fwd(x, w, *, name, resid=None, want_u=None, target=None):
    t, d = x.shape
    tr = _row_tile(t, 512)

    def norm(v, wv):
        return v * lax.rsqrt(jnp.mean(v * v, axis=-1, keepdims=True) + NORM_EPS) * wv

    row = pl.BlockSpec((tr, d), lambda i: (i, 0))
    vec = pl.BlockSpec((1, d), lambda i: (0, 0))
    if target is not None:
        def body(x_ref, w_ref, r_ref, t_ref, dh_ref, loss_ref):
            err = r_ref[...] + norm(x_ref[...].astype(F32), w_ref[...]) - t_ref[...]
            dh_ref[...] = err * (1.0 / d)

            @pl.when(pl.program_id(0) == 0)
            def _():
                loss_ref[...] = jnp.zeros_like(loss_ref)

            part = jnp.sum(jnp.sum(err * err, axis=1, keepdims=True), axis=0, keepdims=True) * (0.5 / d)
            loss_ref[...] += jnp.broadcast_to(part, loss_ref.shape)

        return pl.pallas_call(
            body, grid=(t // tr,), in_specs=[row, vec, row, row],
            out_specs=[row, pl.BlockSpec((8, LANES), lambda i: (0, 0))],
            out_shape=[jax.ShapeDtypeStruct((t, d), F32), jax.ShapeDtypeStruct((8, LANES), F32)],
            compiler_params=_params("arbitrary"), name=name)(x, w, resid, target)
    if resid is None:
        def body(x_ref, w_ref, o_ref):
            o_ref[...] = norm(x_ref[...].astype(F32), w_ref[...]).astype(BF16)
        ins, in_specs = (x, w), [row, vec]
        out_shape, out_specs = jax.ShapeDtypeStruct((t, d), BF16), row
    elif want_u is None:
        def body(x_ref, w_ref, r_ref, o_ref):
            o_ref[...] = r_ref[...] + norm(x_ref[...].astype(F32), w_ref[...])
        ins, in_specs = (x, w, resid), [row, vec, row]
        out_shape, out_specs = jax.ShapeDtypeStruct((t, d), F32), row
    else:
        def body(x_ref, w_ref, r_ref, w2_ref, o_ref, u_ref):
            h = r_ref[...] + norm(x_ref[...].astype(F32), w_ref[...])
            o_ref[...] = h
            u_ref[...] = norm(h, w2_ref[...]).astype(BF16)
        ins, in_specs = (x, w, resid, want_u), [row, vec, row, vec]
        out_shape = [jax.ShapeDtypeStruct((t, d), F32), jax.ShapeDtypeStruct((t, d), BF16)]
        out_specs = [row, row]
    return pl.pallas_call(body, grid=(t // tr,), in_specs=in_specs, out_specs=out_specs, out_shape=out_shape,
                          compiler_params=_params("parallel"), name=name)(*ins)


def _rms_bwd(x, w, dy, *, name, resid=None, out_dtype=F32, dx_col_sum=False):
    t, d = x.shape
    tr = _row_tile(t, 512)
    row = pl.BlockSpec((tr, d), lambda i: (i, 0))
    vec = pl.BlockSpec((1, d), lambda i: (0, 0))
    has_res = resid is not None

    def body(x_ref, w_ref, dy_ref, *rest):
        r_ref = rest[0] if has_res else None
        dx_ref, dw_ref = rest[has_res:has_res + 2]
        xv = x_ref[...].astype(F32)
        dyv = dy_ref[...].astype(F32)
        r = lax.rsqrt(jnp.mean(xv * xv, axis=-1, keepdims=True) + NORM_EPS)
        xhat = xv * r
        dyw = dyv * w_ref[...]
        dx = r * (dyw - xhat * jnp.mean(dyw * xhat, axis=-1, keepdims=True))
        if has_res:
            dx = dx + r_ref[...]
        dx_ref[...] = dx.astype(dx_ref.dtype)

        sums = [(dw_ref, dyv * xhat)] + ([(rest[-1], dx)] if dx_col_sum else [])

        @pl.when(pl.program_id(0) == 0)
        def _():
            for acc_ref, _ in sums:
                acc_ref[...] = jnp.zeros_like(acc_ref)

        for acc_ref, rows in sums:
            acc_ref[...] += jnp.sum(rows, axis=0, keepdims=True)

    ins = (x, w, dy) + ((resid,) if has_res else ())
    in_specs = [row, vec, row] + ([row] if has_res else [])
    n_vec = 2 if dx_col_sum else 1
    return pl.pallas_call(
        body, grid=(t // tr,), in_specs=in_specs, out_specs=[row] + [vec] * n_vec,
        out_shape=[jax.ShapeDtypeStruct((t, d), out_dtype)] + [jax.ShapeDtypeStruct((1, d), F32)] * n_vec,
        compiler_params=_params("arbitrary"), name=name)(*ins)


def _col_sum(x, *, name):
    t, n = x.shape
    tr = _row_tile(t, 512)

    def body(x_ref, o_ref):
        @pl.when(pl.program_id(0) == 0)
        def _():
            o_ref[...] = jnp.zeros_like(o_ref)

        o_ref[...] += jnp.sum(x_ref[...].astype(F32), axis=0, keepdims=True)

    return pl.pallas_call(
        body, grid=(t // tr,), in_specs=[pl.BlockSpec((tr, n), lambda i: (i, 0))],
        out_specs=pl.BlockSpec((1, n), lambda i: (0, 0)), out_shape=jax.ShapeDtypeStruct((1, n), F32),
        compiler_params=_params("arbitrary"), name=name)(x)


SSD_IN_SHARD = SSD_IN_DIM // N_CHIPS


def _w_in_from_shards(shards, *, name):
    d = shards.shape[1]
    tr = 256

    def body(s_ref, o_ref):
        o_ref[:, pl.ds(SSD_DT_COL, SSD_IN_PAD - SSD_DT_COL)] = jnp.zeros((tr, SSD_IN_PAD - SSD_DT_COL), o_ref.dtype)
        for s in range(N_CHIPS):
            o_ref[:, pl.ds(SSD_IN_SHARD * s, SSD_IN_SHARD)] = s_ref[s]

    return pl.pallas_call(
        body, grid=(d // tr,), in_specs=[pl.BlockSpec((N_CHIPS, tr, SSD_IN_SHARD), lambda i: (0, i, 0))],
        out_specs=pl.BlockSpec((tr, SSD_IN_PAD), lambda i: (i, 0)),
        out_shape=jax.ShapeDtypeStruct((d, SSD_IN_PAD), shards.dtype),
        compiler_params=_params("parallel"), name=name)(shards)


def _w_in_to_shards(g, *, name):
    d = g.shape[0]
    tr = 256

    def body(g_ref, o_ref):
        for s in range(N_CHIPS):
            o_ref[s] = g_ref[:, pl.ds(SSD_IN_SHARD * s, SSD_IN_SHARD)].astype(o_ref.dtype)

    return pl.pallas_call(
        body, grid=(d // tr,), in_specs=[pl.BlockSpec((tr, SSD_IN_PAD), lambda i: (i, 0))],
        out_specs=pl.BlockSpec((N_CHIPS, tr, SSD_IN_SHARD), lambda i: (0, i, 0)),
        out_shape=jax.ShapeDtypeStruct((N_CHIPS, d, SSD_IN_SHARD), BF16),
        compiler_params=_params("parallel"), name=name)(g)


XBC_COL0 = SSD_D_INNER // LANES


def _shift_down(v, k, row_ids):
    return jnp.where(row_ids >= k, pltpu.roll(v, k, axis=0), 0.0)


def _shift_up(v, k, row_ids):
    n = v.shape[0]
    return jnp.where(row_ids < n - k, pltpu.roll(v, n - k, axis=0), 0.0)


def _conv_pre(x, w, b, row_ids):
    pre = b + w[3:4, :] * x
    for k in (1, 2, 3):
        pre = pre + w[3 - k:4 - k, :] * _shift_down(x, k, row_ids)
    return pre


def _conv_fwd(zx, conv_w, conv_b, *, name, hook=None):
    t = zx.shape[0]
    nct = SSD_CONV_DIM // LANES
    hk = _HookSlots(hook, n_in=3, n_out=1, n_scratch=0)

    def body(*refs):
        (x_ref, w_ref, b_ref), (o_ref,), _ = hk.own(refs)
        if hook is not None:
            hk.run(refs, pl.program_id(0), nct)
        x = x_ref[...].astype(F32)
        row_ids = lax.broadcasted_iota(jnp.int32, x.shape, 0)
        pre = _conv_pre(x, w_ref[...], b_ref[...], row_ids)
        o_ref[...] = pre * _sigmoid(pre)

    outs = pl.pallas_call(
        body, grid=(nct,),
        in_specs=[pl.BlockSpec((t, LANES), lambda j: (0, XBC_COL0 + j)),
                  pl.BlockSpec((SSD_CONV_WIDTH, LANES), lambda j: (0, j)),
                  pl.BlockSpec((1, LANES), lambda j: (0, j))] + hk.in_specs,
        out_specs=[pl.BlockSpec((t, LANES), lambda j: (0, j))] + hk.out_specs,
        out_shape=[jax.ShapeDtypeStruct((t, SSD_CONV_DIM), F32)] + hk.out_shape,
        scratch_shapes=hk.scratch,
        compiler_params=_params(*hk.semantics("parallel")), name=name)(zx, conv_w, conv_b, *hk.inputs)
    return outs[0] if hook is None else (outs[0], outs[1:])


def _conv_bwd(zx, conv_w, conv_b, d_xs, d_bm, d_cm, dzx, *, name):
    t = zx.shape[0]
    nct = SSD_CONV_DIM // LANES
    n_xs = SSD_D_INNER // LANES
    n_bm = SSD_N_GROUPS * SSD_D_STATE // LANES

    def body(x_ref, w_ref, b_ref, dxs_ref, dbm_ref, dcm_ref, _, dx_ref, dw_ref, db_ref):
        x = x_ref[...].astype(F32)
        w = w_ref[...]
        j = pl.program_id(0)
        dy = jnp.where(j < n_xs, dxs_ref[...], jnp.where(j < n_xs + n_bm, dbm_ref[...], dcm_ref[...]))
        row_ids = lax.broadcasted_iota(jnp.int32, x.shape, 0)
        pre = _conv_pre(x, w, b_ref[...], row_ids)
        sg = _sigmoid(pre)
        dpre = dy * (sg * (1.0 + pre * (1.0 - sg)))
        dx = w[3:4, :] * dpre
        for k in (1, 2, 3):
            dx = dx + w[3 - k:4 - k, :] * _shift_up(dpre, k, row_ids)
        dx_ref[...] = dx.astype(dx_ref.dtype)
        db_ref[...] = jnp.sum(dpre, axis=0, keepdims=True)
        dw_ref[3:4, :] = jnp.sum(dpre * x, axis=0, keepdims=True)
        for k in (1, 2, 3):
            dw_ref[3 - k:4 - k, :] = jnp.sum(dpre * _shift_down(x, k, row_ids), axis=0, keepdims=True)

    clip = lambda j, lo, n: jnp.clip(j - lo, 0, n - 1)
    return pl.pallas_call(
        body, grid=(nct,),
        in_specs=[pl.BlockSpec((t, LANES), lambda j: (0, XBC_COL0 + j)),
                  pl.BlockSpec((SSD_CONV_WIDTH, LANES), lambda j: (0, j)),
                  pl.BlockSpec((1, LANES), lambda j: (0, j)),
                  pl.BlockSpec((t, LANES), lambda j: (0, clip(j, 0, n_xs))),
                  pl.BlockSpec((t, LANES), lambda j: (0, clip(j, n_xs, n_bm))),
                  pl.BlockSpec((t, LANES), lambda j: (0, clip(j, n_xs + n_bm, n_bm))), ANY],
        out_specs=[pl.BlockSpec((t, LANES), lambda j: (0, XBC_COL0 + j)),
                   pl.BlockSpec((SSD_CONV_WIDTH, LANES), lambda j: (0, j)), pl.BlockSpec((1, LANES), lambda j: (0, j))],
        out_shape=[jax.ShapeDtypeStruct(dzx.shape, dzx.dtype),
                   jax.ShapeDtypeStruct((SSD_CONV_WIDTH, SSD_CONV_DIM), F32),
                   jax.ShapeDtypeStruct((1, SSD_CONV_DIM), F32)],
        input_output_aliases={6: 0},
        compiler_params=_params("parallel"), name=name)(zx, conv_w, conv_b, d_xs, d_bm, d_cm, dzx)


def _softplus_fwd(dt_raw, bias_row, alog_row, *, name):
    t = dt_raw.shape[0]
    q = SSD_CHUNK
    tr = _row_tile(t, 1024)

    def body(x_ref, b_ref, al_ref, dt_ref, cum_ref):
        v = x_ref[...] + b_ref[...]
        e = jnp.exp(-jnp.abs(v))
        u = 1.0 + e
        log1p = jnp.where(u == 1.0, e, jnp.log(u) * (e / (u - 1.0)))
        dt = jnp.maximum(v, 0.0) + log1p
        a = dt * -jnp.exp(al_ref[...])
        lower = (lax.broadcasted_iota(jnp.int32, (q, q), 1) <= lax.broadcasted_iota(jnp.int32, (q, q), 0)).astype(F32)
        cums = [lax.dot_general(lower, a[c * q:(c + 1) * q, :], ((((1,), (0,))), ((), ())), precision=lax.Precision.HIGHEST,
                                preferred_element_type=F32) for c in range(tr // q)]
        dt_t, cum_t = dt.T, jnp.concatenate(cums, axis=0).T
        for g in range(SSD_N_GROUPS):
            rows = slice(g * SSD_HPG, (g + 1) * SSD_HPG)
            dt_ref[g] = dt_t[rows, :]
            cum_ref[g] = cum_t[rows, :]

    vec = pl.BlockSpec((1, LANES), lambda i: (0, 0))
    by_group = pl.BlockSpec((SSD_N_GROUPS, SSD_HPG, tr), lambda i: (0, 0, i))
    return pl.pallas_call(
        body, grid=(t // tr,),
        in_specs=[pl.BlockSpec((tr, LANES), lambda i: (i, 0)), vec, vec],
        out_specs=[by_group, by_group],
        out_shape=[jax.ShapeDtypeStruct((SSD_N_GROUPS, SSD_HPG, t), F32)] * 2,
        compiler_params=_params("parallel"), name=name)(dt_raw, bias_row, alog_row)


def _softplus_bwd(dt_raw, bias_row, ddt_rows, dzx, *, name):
    t = dt_raw.shape[0]
    tr = _row_tile(t, 1024)
    tail = SSD_IN_PAD - SSD_DT_COL

    def body(x_ref, b_ref, g_ref, _, o_ref, db_ref):
        v = x_ref[...] + b_ref[...]
        lane = lax.broadcasted_iota(jnp.int32, v.shape, 1)
        by_head = jnp.concatenate([g_ref[g] for g in range(SSD_N_GROUPS)]
                                  + [jnp.zeros((LANES - SSD_N_HEADS, tr), F32)], axis=0)
        d = jnp.where(lane < SSD_N_HEADS, by_head.T * _sigmoid(v), 0.0)
        o_ref[:, pl.ds(0, LANES)] = d.astype(o_ref.dtype)
        o_ref[:, pl.ds(LANES, tail - LANES)] = jnp.zeros((tr, tail - LANES), o_ref.dtype)

        @pl.when(pl.program_id(0) == 0)
        def _():
            db_ref[...] = jnp.zeros_like(db_ref)

        db_ref[...] += jnp.sum(d, axis=0, keepdims=True)

    return pl.pallas_call(
        body, grid=(t // tr,),
        in_specs=[pl.BlockSpec((tr, LANES), lambda i: (i, 0)), pl.BlockSpec((1, LANES), lambda i: (0, 0)),
                  pl.BlockSpec((SSD_N_GROUPS, SSD_HPG, tr), lambda i: (0, 0, i)), ANY],
        out_specs=[pl.BlockSpec((tr, tail), lambda i: (i, SSD_DT_COL // tail)), pl.BlockSpec((1, LANES), lambda i: (0, 0))],
        out_shape=[jax.ShapeDtypeStruct(dzx.shape, dzx.dtype), jax.ShapeDtypeStruct((1, LANES), F32)],
        input_output_aliases={3: 0},
        compiler_params=_params("arbitrary"), name=name)(dt_raw, bias_row, ddt_rows, dzx)


def _ssd_masks():
    q = SSD_CHUNK
    tt = lax.broadcasted_iota(jnp.int32, (q, q), 0)
    ss = lax.broadcasted_iota(jnp.int32, (q, q), 1)
    lane = lax.broadcasted_iota(jnp.int32, (1, SSD_GW), 1)
    srow = lax.broadcasted_iota(jnp.int32, (SSD_GW, 1), 0)
    hm = [(lane >= SSD_HEAD_DIM * j) & (lane < SSD_HEAD_DIM * (j + 1)) for j in range(SSD_HPG)]
    rm = [(srow >= SSD_HEAD_DIM * j) & (srow < SSD_HEAD_DIM * (j + 1)) for j in range(SSD_HPG)]
    return tt, ss, hm, rm


def _ssd_head_terms(dt_rows, cum_rows, a_rows, j, tt, ss):
    q = SSD_CHUNK
    dt_row = dt_rows[j:j + 1, :]
    dt_col = jnp.sum(jnp.where(tt == ss, dt_row, 0.0), axis=1, keepdims=True)
    a_row1 = a_rows[j:j + 1, :]
    a_11 = a_rows[j:j + 1, 0:1]
    cum_col = jnp.sum(jnp.where(ss <= tt, dt_row * a_row1, 0.0), axis=1, keepdims=True)
    cum_row = cum_rows[j:j + 1, :]
    decay = jnp.exp(jnp.where(ss <= tt, cum_col - cum_row, -jnp.inf))
    cum_last = cum_col[q - 1:q, :]
    e_col = jnp.exp(cum_col)
    dte_col = jnp.exp(cum_last - cum_col)
    e_last = jnp.exp(cum_last)
    return dt_col, dt_row, a_row1, a_11, decay, e_col, dte_col, e_last


SSD_CHUNKS_PER_STEP = 8
SSD_BC_COL0 = SSD_D_INNER // SSD_D_STATE


def _ssd_head_selects(terms, hm, rm):
    e_all = jnp.zeros((SSD_CHUNK, SSD_GW), F32)
    w_all = jnp.zeros((SSD_CHUNK, SSD_GW), F32)
    e_s = jnp.zeros((SSD_GW, 1), F32)
    for j in range(SSD_HPG):
        dt_col, _, _, _, _, e_col, dte_col, e_last = terms[j]
        e_all = jnp.where(hm[j], e_col, e_all)
        w_all = jnp.where(hm[j], dt_col * dte_col, w_all)
        e_s = jnp.where(rm[j], e_last, e_s)
    return e_all, w_all, e_s


def _ssd_fwd(xc, dtr, cumr, alog_b, d_b, *, name, hook=None):
    t = xc.shape[0]
    q = SSD_CHUNK
    nc = t // q
    kc = min(SSD_CHUNKS_PER_STEP, nc)
    rows = kc * q
    hk = _HookSlots(hook, n_in=7, n_out=2, n_scratch=1)

    def body(*refs):
        (x_ref, b_ref, c_ref, dtr_ref, cumr_ref, alog_ref, d_ref), (y_ref, st_ref), (s_scr,) = hk.own(refs)
        if hook is not None:
            hk.run(refs, pl.program_id(0) * (nc // kc) + pl.program_id(1), SSD_N_GROUPS * (nc // kc))

        @pl.when(pl.program_id(1) == 0)
        def _():
            s_scr[...] = jnp.zeros_like(s_scr)

        tt, ss, hm, rm = _ssd_masks()
        a_rows = -jnp.exp(alog_ref[...])
        d_rows = d_ref[...]
        d_all = jnp.zeros((1, SSD_GW), F32)
        for j in range(SSD_HPG):
            d_all = jnp.where(hm[j], d_rows[j:j + 1, 0:1], d_all)
        ks, hs = range(kc), range(SSD_HPG)
        sl = [pl.ds(k * q, q) for k in ks]
        x = [x_ref[sl[k], :] for k in ks]
        bm = [b_ref[sl[k], :].astype(BF16) for k in ks]
        cm = [c_ref[sl[k], :].astype(BF16) for k in ks]
        xb = [x[k].astype(BF16) for k in ks]
        terms = [[_ssd_head_terms(dtr_ref[:, sl[k]], cumr_ref[:, sl[k]], a_rows, j, tt, ss) for j in hs] for k in ks]
        g = [_dot_nt(cm[k], bm[k]) for k in ks]
        m = [[(g[k] * terms[k][j][4] * terms[k][j][1]).astype(BF16) for j in hs] for k in ks]
        yj = [[_dot_nn(m[k][j], xb[k]) for j in hs] for k in ks]
        sel = [_ssd_head_selects(terms[k], hm, rm) for k in ks]
        upd = [_dot_tn((x[k] * sel[k][1]).astype(BF16), bm[k]) for k in ks]
        states = [s_scr[...]]
        for k in ks:
            states.append(states[k] * sel[k][2] + upd[k])
        inter = [_dot_nt(cm[k], states[k].astype(BF16)) for k in ks]
        ys = []
        for k in ks:
            y = jnp.zeros((q, SSD_GW), F32)
            for j in hs:
                y = jnp.where(hm[j], yj[k][j], y)
            ys.append(y + inter[k] * sel[k][0] + x[k] * d_all)
        for k in ks:
            st_ref[k] = states[k]
        y_ref[...] = jnp.concatenate(ys, axis=0).astype(y_ref.dtype)
        s_scr[...] = states[kc]

    blk = lambda width, off: pl.BlockSpec((rows, width), lambda g, c: (c, off + g))
    par_s = pl.BlockSpec((None, SSD_HPG, LANES), lambda g, c: (g, 0, 0))
    row_s = pl.BlockSpec((None, SSD_HPG, rows), lambda g, c: (g, 0, c))
    outs = pl.pallas_call(
        body, grid=(SSD_N_GROUPS, nc // kc),
        in_specs=[blk(SSD_GW, 0), blk(SSD_D_STATE, SSD_BC_COL0), blk(SSD_D_STATE, SSD_BC_COL0 + SSD_N_GROUPS),
                  row_s, row_s, par_s, par_s] + hk.in_specs,
        out_specs=[blk(SSD_GW, 0), pl.BlockSpec((None, kc, SSD_GW, SSD_D_STATE), lambda g, c: (g, c, 0, 0))] + hk.out_specs,
        out_shape=[jax.ShapeDtypeStruct((t, SSD_D_INNER), BF16),
                   jax.ShapeDtypeStruct((SSD_N_GROUPS, nc, SSD_GW, SSD_D_STATE), F32)] + hk.out_shape,
        scratch_shapes=[pltpu.VMEM((SSD_GW, SSD_D_STATE), F32)] + hk.scratch,
        compiler_params=_params(*hk.semantics("parallel", "arbitrary")), name=name)(
            xc, xc, xc, dtr, cumr, alog_b, d_b, *hk.inputs)
    return outs if hook is None else (outs[:2], outs[2:])


def _ssd_bwd(xc, dtr, cumr, alog_b, d_b, states, dy, *, name, hook=None):
    t = xc.shape[0]
    q = SSD_CHUNK
    nc = t // q
    kc = min(SSD_CHUNKS_PER_STEP, nc)
    nst = nc // kc
    rows = kc * q
    rev = lambda c: nst - 1 - c
    hk = _HookSlots(hook, n_in=9, n_out=5, n_scratch=1)

    def body(*refs):
        ((x_ref, b_ref, c_ref, dtr_ref, cumr_ref, alog_ref, d_ref, st_ref, dy_ref),
         (dx_ref, db_ref, dc_ref, ddt_ref, dpar_ref), (ds_scr,)) = hk.own(refs)
        if hook is not None:
            hk.run(refs, pl.program_id(0) * nst + pl.program_id(1), SSD_N_GROUPS * nst)

        @pl.when(pl.program_id(1) == 0)
        def _():
            ds_scr[...] = jnp.zeros_like(ds_scr)
            dpar_ref[...] = jnp.zeros_like(dpar_ref)

        tt, ss, hm, rm = _ssd_masks()
        tcol = lax.broadcasted_iota(jnp.int32, (q, 1), 0)
        lane = lax.broadcasted_iota(jnp.int32, (1, LANES), 1)
        a_rows = -jnp.exp(alog_ref[...])
        d_rows = d_ref[...]
        d_all = jnp.zeros((1, SSD_GW), F32)
        for j in range(SSD_HPG):
            d_all = jnp.where(hm[j], d_rows[j:j + 1, 0:1], d_all)
        ks, hs = range(kc), range(SSD_HPG)
        sl = [pl.ds(k * q, q) for k in ks]
        x = [x_ref[sl[k], :] for k in ks]
        dyv = [dy_ref[sl[k], :].astype(F32) for k in ks]
        bm = [b_ref[sl[k], :].astype(BF16) for k in ks]
        cm = [c_ref[sl[k], :].astype(BF16) for k in ks]
        s_in = [st_ref[k] for k in ks]
        xb = [x[k].astype(BF16) for k in ks]
        dyb = [dyv[k].astype(BF16) for k in ks]
        s_b = [s_in[k].astype(BF16) for k in ks]
        terms = [[_ssd_head_terms(dtr_ref[:, sl[k]], cumr_ref[:, sl[k]], a_rows, j, tt, ss) for j in hs] for k in ks]
        sel = [_ssd_head_selects(terms[k], hm, rm) for k in ks]
        e_all, w_all, e_s = [s_[0] for s_ in sel], [s_[1] for s_ in sel], [s_[2] for s_ in sel]
        dye = [(dyv[k] * e_all[k]).astype(BF16) for k in ks]
        ds_loc = [_dot_tn(dye[k], cm[k]) for k in ks]
        ds = [None] * kc
        running = ds_scr[...]
        for k in reversed(ks):
            ds[k] = running
            running = running * e_s[k] + ds_loc[k]
        ds_scr[...] = running
        ds_b = [ds[k].astype(BF16) for k in ks]
        g = [_dot_nt(cm[k], bm[k]) for k in ks]
        cs = [_dot_nt(cm[k], s_b[k]) for k in ks]
        bds = [_dot_nt(bm[k], ds_b[k]) for k in ks]
        dm = [[_dot_nt(jnp.where(hm[j], dyv[k], 0.0).astype(BF16), xb[k]) for j in hs] for k in ks]
        gl = [[g[k] * terms[k][j][4] for j in hs] for k in ks]
        wp = [[dm[k][j] * gl[k][j] for j in hs] for k in ks]
        mt = [[(gl[k][j] * terms[k][j][1]).astype(BF16) for j in hs] for k in ks]
        dxj = [[_dot_tn(mt[k][j], dyb[k]) for j in hs] for k in ks]
        dg = []
        for k in ks:
            acc = jnp.zeros((q, q), F32)
            for j in hs:
                acc = acc + dm[k][j] * terms[k][j][4] * terms[k][j][1]
            dg.append(acc.astype(BF16))
        dy_cs = [dyv[k] * cs[k] for k in ks]
        x_bds = [x[k] * bds[k] for k in ks]
        dy_x = [dyv[k] * x[k] for k in ks]
        ds_s = [ds[k] * s_in[k] for k in ks]
        w = [[wp[k][j] * terms[k][j][1] for j in hs] for k in ks]
        rw_col = [[jnp.sum(w[k][j], axis=1, keepdims=True) for j in hs] for k in ks]
        cw_row = [[jnp.sum(w[k][j], axis=0, keepdims=True) for j in hs] for k in ks]
        cwp_row = [[jnp.sum(wp[k][j], axis=0, keepdims=True) for j in hs] for k in ks]
        r1_col = [[jnp.sum(jnp.where(hm[j], dy_cs[k], 0.0), axis=1, keepdims=True) * terms[k][j][5] for j in hs] for k in ks]
        dw_col = [[jnp.sum(jnp.where(hm[j], x_bds[k], 0.0), axis=1, keepdims=True) for j in hs] for k in ks]
        head_rows = [slice(j * SSD_HEAD_DIM, (j + 1) * SSD_HEAD_DIM) for j in hs]
        lane_sum = lambda v: jnp.sum(v, axis=1, keepdims=True)
        s_sum = [[lane_sum(jnp.sum(ds_s[k][head_rows[j], :], axis=0, keepdims=True)) for j in hs] for k in ks]
        dy_x_cols = [jnp.sum(dy_x[k], axis=0, keepdims=True) for k in ks]
        d_d = [[lane_sum(jnp.where(hm[j], dy_x_cols[k], 0.0)) for j in hs] for k in ks]
        ddt_rows = [[None] * SSD_HPG for _ in ks]
        dpar = [jnp.zeros((1, LANES), F32) for _ in hs]
        for k in ks:
            for j in hs:
                dt_col, dt_row, a_row1, a_11, _, _, dte_col, e_last = terms[k][j]
                dww = dw_col[k][j] * (dt_col * dte_col)
                last_add = jnp.sum(dww, axis=0, keepdims=True) + e_last * s_sum[k][j]
                dcum_col = rw_col[k][j] + r1_col[k][j] - dww + jnp.where(tcol == q - 1, last_add, 0.0)
                da_row = jnp.sum(jnp.where(tt >= ss, dcum_col, 0.0), axis=0, keepdims=True)
                da_col = jnp.sum(jnp.where(ss >= tt, -cw_row[k][j], 0.0), axis=1, keepdims=True)
                ddt_col = a_11 * da_col + dw_col[k][j] * dte_col
                ddt_rows[k][j] = (a_row1 * da_row + cwp_row[k][j]
                                  + jnp.sum(jnp.where(tt == ss, ddt_col, 0.0), axis=0, keepdims=True))
                d_a = jnp.sum(dt_row * da_row, axis=1, keepdims=True) + jnp.sum(dt_col * da_col, axis=0, keepdims=True)
                dpar[j] = dpar[j] + jnp.where(lane == 0, d_a * a_11, 0.0) + jnp.where(lane == 1, d_d[k][j], 0.0)
        dxs = []
        for k in ks:
            acc = jnp.zeros((q, SSD_GW), F32)
            for j in hs:
                acc = jnp.where(hm[j], dxj[k][j], acc)
            dxs.append(acc + w_all[k] * bds[k] + d_all * dyv[k])
        xw = [(x[k] * w_all[k]).astype(BF16) for k in ks]
        dc = [_dot_nn(dg[k], bm[k]) + _dot_nn(dye[k], s_b[k]) for k in ks]
        db = [_dot_tn(dg[k], cm[k]) + _dot_nn(xw[k], ds_b[k]) for k in ks]
        dx_ref[...] = jnp.concatenate(dxs, axis=0)
        dc_ref[...] = jnp.concatenate(dc, axis=0)
        db_ref[...] = jnp.concatenate(db, axis=0)
        ddt_ref[...] = jnp.concatenate([jnp.concatenate([ddt_rows[k][j] for k in ks], axis=1) for j in hs], axis=0)
        dpar_ref[...] += jnp.concatenate(dpar, axis=0)

    blk = lambda width, off: pl.BlockSpec((rows, width), lambda g, c: (rev(c), off + g))
    par_s = pl.BlockSpec((None, SSD_HPG, LANES), lambda g, c: (g, 0, 0))
    outs = pl.pallas_call(
        body, grid=(SSD_N_GROUPS, nst),
        in_specs=[blk(SSD_GW, 0), blk(SSD_D_STATE, SSD_BC_COL0), blk(SSD_D_STATE, SSD_BC_COL0 + SSD_N_GROUPS),
                  pl.BlockSpec((None, SSD_HPG, rows), lambda g, c: (g, 0, rev(c))),
                  pl.BlockSpec((None, SSD_HPG, rows), lambda g, c: (g, 0, rev(c))), par_s, par_s,
                  pl.BlockSpec((None, kc, SSD_GW, SSD_D_STATE), lambda g, c: (g, rev(c), 0, 0)), blk(SSD_GW, 0)] + hk.in_specs,
        out_specs=[blk(SSD_GW, 0), blk(SSD_D_STATE, 0), blk(SSD_D_STATE, 0),
                   pl.BlockSpec((None, SSD_HPG, rows), lambda g, c: (g, 0, rev(c))), par_s] + hk.out_specs,
        out_shape=[jax.ShapeDtypeStruct((t, SSD_D_INNER), F32),
                   jax.ShapeDtypeStruct((t, SSD_N_GROUPS * SSD_D_STATE), F32),
                   jax.ShapeDtypeStruct((t, SSD_N_GROUPS * SSD_D_STATE), F32),
                   jax.ShapeDtypeStruct((SSD_N_GROUPS, SSD_HPG, t), F32),
                   jax.ShapeDtypeStruct((SSD_N_GROUPS, SSD_HPG, LANES), F32)] + hk.out_shape,
        scratch_shapes=[pltpu.VMEM((SSD_GW, SSD_D_STATE), F32)] + hk.scratch,
        compiler_params=_params(*hk.semantics("parallel", "arbitrary")), name=name)(
            xc, xc, xc, dtr, cumr, alog_b, d_b, states, dy, *hk.inputs)
    return outs if hook is None else (outs[:5], outs[5:])


def _gate_norm_fwd(y, zx, norm_w, *, name):
    t = y.shape[0]
    tr = _row_tile(t, 256)
    row = pl.BlockSpec((tr, SSD_D_INNER), lambda i: (i, 0))

    def body(y_ref, z_ref, w_ref, o_ref):
        for gi in range(SSD_N_GROUPS):
            sl = pl.ds(gi * SSD_GW, SSD_GW)
            z = z_ref[:, sl].astype(F32)
            gv = y_ref[:, sl].astype(F32) * (z * _sigmoid(z))
            r = lax.rsqrt(jnp.mean(gv * gv, axis=-1, keepdims=True) + NORM_EPS)
            o_ref[:, sl] = (gv * r * w_ref[:, sl]).astype(BF16)

    return pl.pallas_call(
        body, grid=(t // tr,), in_specs=[row, row, pl.BlockSpec((1, SSD_D_INNER), lambda i: (0, 0))],
        out_specs=row, out_shape=jax.ShapeDtypeStruct((t, SSD_D_INNER), BF16),
        compiler_params=_params("parallel"), name=name)(y, zx, norm_w)


def _gate_norm_bwd(y, zx, norm_w, dyn, *, name):
    t = y.shape[0]
    tr = _row_tile(t, 256)
    row = pl.BlockSpec((tr, SSD_D_INNER), lambda i: (i, 0))
    vec = pl.BlockSpec((1, SSD_D_INNER), lambda i: (0, 0))

    def body(y_ref, z_ref, w_ref, dyn_ref, dy_ref, dz_ref, dw_ref):
        @pl.when(pl.program_id(0) == 0)
        def _():
            dw_ref[...] = jnp.zeros_like(dw_ref)

        for gi in range(SSD_N_GROUPS):
            sl = pl.ds(gi * SSD_GW, SSD_GW)
            z = z_ref[:, sl].astype(F32)
            yv = y_ref[:, sl].astype(F32)
            sg = _sigmoid(z)
            sz = z * sg
            gv = yv * sz
            r = lax.rsqrt(jnp.mean(gv * gv, axis=-1, keepdims=True) + NORM_EPS)
            ghat = gv * r
            dout = dyn_ref[:, sl].astype(F32)
            dgh = dout * w_ref[:, sl]
            dgv = r * (dgh - ghat * jnp.mean(dgh * ghat, axis=-1, keepdims=True))
            dy_ref[:, sl] = (dgv * sz).astype(dy_ref.dtype)
            dz_ref[:, sl] = (dgv * yv * (sg * (1.0 + z * (1.0 - sg)))).astype(dz_ref.dtype)
            dw_ref[:, sl] += jnp.sum(dout * ghat, axis=0, keepdims=True)

    return pl.pallas_call(
        body, grid=(t // tr,), in_specs=[row, row, vec, row], out_specs=[row, row, vec],
        out_shape=[jax.ShapeDtypeStruct((t, SSD_D_INNER), BF16), jax.ShapeDtypeStruct((t, SSD_IN_PAD), BF16),
                   jax.ShapeDtypeStruct((1, SSD_D_INNER), F32)],
        compiler_params=_params("arbitrary"), name=name)(y, zx, norm_w, dyn)


ATTN_KV_W = ATTN_N_KV * ATTN_HEAD_DIM
ATTN_Q_HALF = 512
ATTN_K_BLK = ATTN_N_Q * ATTN_HEAD_DIM // ATTN_KV_W
ATTN_V_BLK = ATTN_K_BLK + 1


def _attn_valid(first_block):
    w = ATTN_WINDOW
    qpos = lax.broadcasted_iota(jnp.int32, (w, 2 * w), 0) + w
    kpos = lax.broadcasted_iota(jnp.int32, (w, 2 * w), 1)
    rel = qpos - kpos
    return (rel >= 0) & (rel < w) & jnp.logical_not(first_block & (kpos < w))


def _attn_head_views(lo_ref, hi_ref):
    hd = ATTN_HEAD_DIM
    per_half = ATTN_Q_HALF // hd
    return [(lo_ref if h < per_half else hi_ref)[:, pl.ds((h % per_half) * hd, hd)] for h in range(ATTN_N_Q)]


def _attn_block_views(lo_ref, hi_ref, kc_ref, kp_ref, vc_ref, vp_ref):
    hd = ATTN_HEAD_DIM
    kv_cols = [pl.ds(kh * hd, hd) for kh in range(ATTN_N_KV)]
    kb = [jnp.concatenate([kp_ref[:, c], kc_ref[:, c]], axis=0) for c in kv_cols]
    vb = [jnp.concatenate([vp_ref[:, c], vc_ref[:, c]], axis=0) for c in kv_cols]
    return _attn_head_views(lo_ref, hi_ref), kb, vb


def _attn_scores(q, kb, valid):
    scale = ATTN_HEAD_DIM ** -0.5
    return [jnp.where(valid, _dot_nt(q[h], kb[h // ATTN_REP]) * scale, -jnp.inf) for h in range(ATTN_N_Q)]


def _attn_softmax(s, sink):
    heads = range(ATTN_N_Q)
    m = [jnp.maximum(jnp.max(s[h], axis=1, keepdims=True), sink[h]) for h in heads]
    e = [jnp.exp(s[h] - m[h]) for h in heads]
    es = [jnp.exp(sink[h] - m[h]) for h in heads]
    inv = [1.0 / (jnp.sum(e[h], axis=1, keepdims=True) + es[h]) for h in heads]
    return e, es, inv


def _attn_fwd(qkv, sinks_b, *, name, hook=None):
    t = qkv.shape[0]
    w = ATTN_WINDOW
    nb = t // w
    prev = lambda n: jnp.maximum(n - 1, 0)
    hk = _HookSlots(hook, n_in=7, n_out=1, n_scratch=0)

    def body(*refs):
        (qlo_ref, qhi_ref, kc_ref, kp_ref, vc_ref, vp_ref, sink_ref), (o_ref,), _ = hk.own(refs)
        if hook is not None:
            hk.run(refs, pl.program_id(0), nb)
        heads = range(ATTN_N_Q)
        q, kb, vb = _attn_block_views(qlo_ref, qhi_ref, kc_ref, kp_ref, vc_ref, vp_ref)
        sink = [sink_ref[h:h + 1, 0:1] for h in heads]
        e, _, inv = _attn_softmax(_attn_scores(q, kb, _attn_valid(pl.program_id(0) == 0)), sink)
        out = [_dot_nn((e[h] * inv[h]).astype(BF16), vb[h // ATTN_REP]).astype(o_ref.dtype) for h in heads]
        o_ref[...] = jnp.concatenate(out, axis=1)

    qh = lambda half: pl.BlockSpec((w, ATTN_Q_HALF), lambda n: (n, half))
    kv = lambda blk, idx: pl.BlockSpec((w, ATTN_KV_W), lambda n: (idx(n), blk))
    cur = lambda n: n
    outs = pl.pallas_call(
        body, grid=(nb,),
        in_specs=[qh(0), qh(1), kv(ATTN_K_BLK, cur), kv(ATTN_K_BLK, prev), kv(ATTN_V_BLK, cur), kv(ATTN_V_BLK, prev),
                  pl.BlockSpec((ATTN_N_Q, LANES), lambda n: (0, 0))] + hk.in_specs,
        out_specs=[pl.BlockSpec((w, D_MODEL), lambda n: (n, 0))] + hk.out_specs,
        out_shape=[jax.ShapeDtypeStruct((t, D_MODEL), BF16)] + hk.out_shape,
        scratch_shapes=hk.scratch,
        compiler_params=_params(*hk.semantics("parallel")), name=name)(qkv, qkv, qkv, qkv, qkv, qkv, sinks_b, *hk.inputs)
    return outs[0] if hook is None else (outs[0], outs[1:])


def _attn_bwd(qkv, sinks_b, dout, *, name):
    t = qkv.shape[0]
    w = ATTN_WINDOW
    nb = t // w
    hd = ATTN_HEAD_DIM
    clamp = lambda n: jnp.minimum(n, nb - 1)
    prev = lambda n: jnp.maximum(clamp(n) - 1, 0)

    def body(qlo_ref, qhi_ref, kc_ref, kp_ref, vc_ref, vp_ref, sink_ref, dolo_ref, dohi_ref,
             dq_ref, dkv_ref, dsink_ref, carry):
        n = pl.program_id(0)

        @pl.when(n == 0)
        def _():
            carry[...] = jnp.zeros_like(carry)
            dsink_ref[...] = jnp.zeros_like(dsink_ref)

        @pl.when(n < nb)
        def _():
            heads, kvs = range(ATTN_N_Q), range(ATTN_N_KV)
            q, kb, vb = _attn_block_views(qlo_ref, qhi_ref, kc_ref, kp_ref, vc_ref, vp_ref)
            do = _attn_head_views(dolo_ref, dohi_ref)
            sink = [sink_ref[h:h + 1, 0:1] for h in heads]
            s = _attn_scores(q, kb, _attn_valid(n == 0))
            dp = [_dot_nt(do[h], vb[h // ATTN_REP]) for h in heads]
            e, es, inv = _attn_softmax(s, sink)
            p = [e[h] * inv[h] for h in heads]
            delta = [jnp.sum(p[h] * dp[h], axis=1, keepdims=True) for h in heads]
            dsc = [(p[h] * (dp[h] - delta[h]) * (hd ** -0.5)).astype(BF16) for h in heads]
            pb = [p[h].astype(BF16) for h in heads]
            dq = [_dot_nn(dsc[h], kb[h // ATTN_REP]).astype(dq_ref.dtype) for h in heads]
            stack = lambda per_head, kh: jnp.concatenate(per_head[kh * ATTN_REP:(kh + 1) * ATTN_REP], axis=0)
            dkb = [_dot_tn(stack(dsc, kh), stack(q, kh)) for kh in kvs]
            dvb = [_dot_tn(stack(pb, kh), stack(do, kh)) for kh in kvs]
            dsink = [jnp.broadcast_to(jnp.sum(-es[h] * inv[h] * delta[h], axis=0, keepdims=True), (1, LANES)) for h in heads]
            dq_ref[...] = jnp.concatenate(dq, axis=1)
            dsink_ref[...] += jnp.concatenate(dsink, axis=0)
            dkv_ref[...] = (carry[...] + jnp.concatenate([d[0:w, :] for d in dkb + dvb], axis=1)).astype(dkv_ref.dtype)
            carry[...] = jnp.concatenate([d[w:2 * w, :] for d in dkb + dvb], axis=1)

        @pl.when(n == nb)
        def _():
            dkv_ref[...] = carry[...].astype(dkv_ref.dtype)

    qh = lambda half: pl.BlockSpec((w, ATTN_Q_HALF), lambda n: (clamp(n), half))
    kv = lambda blk, idx: pl.BlockSpec((w, ATTN_KV_W), lambda n: (idx(n), blk))
    return pl.pallas_call(
        body, grid=(nb + 1,),
        in_specs=[qh(0), qh(1), kv(ATTN_K_BLK, clamp), kv(ATTN_K_BLK, prev), kv(ATTN_V_BLK, clamp), kv(ATTN_V_BLK, prev),
                  pl.BlockSpec((ATTN_N_Q, LANES), lambda n: (0, 0)), qh(0), qh(1)],
        out_specs=[pl.BlockSpec((w, D_MODEL), lambda n: (clamp(n), 0)),
                   pl.BlockSpec((w, 2 * ATTN_KV_W), lambda n: (jnp.maximum(n - 1, 0), 0)),
                   pl.BlockSpec((ATTN_N_Q, LANES), lambda n: (0, 0))],
        out_shape=[jax.ShapeDtypeStruct((t, D_MODEL), BF16), jax.ShapeDtypeStruct((t, 2 * ATTN_KV_W), BF16),
                   jax.ShapeDtypeStruct((ATTN_N_Q, LANES), F32)],
        scratch_shapes=[pltpu.VMEM((w, 2 * ATTN_KV_W), F32)],
        compiler_params=_params("arbitrary"), name=name)(qkv, qkv, qkv, qkv, qkv, qkv, sinks_b, dout, dout)


def _sq_relu_epilogue(acc):
    r = jnp.maximum(acc, 0.0)
    return (r * r,)


def _sq_relu_bwd_epilogue(acc, act):
    return (acc * (2.0 * jnp.sqrt(act.astype(F32))),)


def _bias_epilogue(acc, bias):
    return (acc + bias,)


def _plain_run(stage, fn, *args, **kwargs):
    return fn(*args, **kwargs)


def _mlp_fwd(u, w_up, w_down, tag, run=_plain_run):
    act = run(f"mlp_up_{tag}", _matmul, u, w_up, mode="nn", out_dtypes=(BF16,), epilogue=_sq_relu_epilogue, b_shards=True,
              tm=BIG_TILE, name=f"mlp_up_{tag}")
    f = run(f"mlp_down_{tag}", _matmul, act, w_down, mode="nn", out_dtypes=(BF16,), tk=BIG_TILE, name=f"mlp_down_{tag}")
    return act, f


def _mlp_bwd(u, act, w_up, w_down, df, tag):
    dpre = _matmul(df, w_down, mode="nt", out_dtypes=(BF16,), epilogue=_sq_relu_bwd_epilogue,
                   extras=((act, "tile"),), name=f"mlp_dact_{tag}")
    dw_down = _matmul(act, df, mode="tn", out_dtypes=(BF16,), tk=BIG_TILE, name=f"mlp_dwdown_{tag}")
    du = _matmul(dpre, w_up, mode="nt", out_dtypes=(BF16,), b_shards=True, tm=BIG_TILE, name=f"mlp_du_{tag}")
    dw_up = _matmul(u, dpre, mode="tn", out_dtypes=(BF16,), out_shards=True, tk=BIG_TILE, name=f"mlp_dwup_{tag}")
    return du, dw_up, dw_down


def _head_param_rows(p):
    return jnp.broadcast_to(p.reshape(SSD_N_GROUPS, SSD_HPG, 1), (SSD_N_GROUPS, SSD_HPG, LANES))


def _local_step(x, target, wts, comm=None, u0=None):
    t = x.shape[0]
    wts = dict(wts)
    row = lambda v: v.reshape(1, -1)
    mix_pre, mix_post, ffn_pre, ffn_post = wts["mix_pre_norm"], wts["mix_post_norm"], wts["ffn_pre_norm"], wts["ffn_post_norm"]

    def gathering(stage, fn, *args, **kwargs):
        hook = comm.gather_hook(stage) if comm is not None else None
        if hook is None:
            return fn(*args, **kwargs)
        out, got = fn(*args, hook=hook, **kwargs)
        wts.update(comm.weights_from(stage, got))
        return out

    if u0 is None:
        u0 = _rms_fwd(x, row(mix_pre[0]), name="rms_pre_mix0")
    zx, dt_raw = gathering("in_proj", _matmul, u0, wts["ssd_w_in"], mode="nn", out_dtypes=(BF16,), tn=SSD_IN_TILE,
                           f32_block=SSD_DT_COL - (SSD_IN_PAD - SSD_IN_TILE),
                           name="ssd_in_proj")
    xc = gathering("conv", _conv_fwd, zx, wts["ssd_conv_w"], row(wts["ssd_conv_b"]), name="ssd_conv_fwd")
    bias_row = jnp.pad(wts["ssd_dt_bias"], (0, LANES - SSD_N_HEADS)).reshape(1, LANES)
    alog_row = jnp.pad(wts["ssd_a_log"], (0, LANES - SSD_N_HEADS)).reshape(1, LANES)
    dtr, cumr = _softplus_fwd(dt_raw, bias_row, alog_row, name="ssd_dt_fwd")
    alog_b, d_b = _head_param_rows(wts["ssd_a_log"]), _head_param_rows(wts["ssd_d"])
    y_ssd, states = gathering("scan", _ssd_fwd, xc, dtr, cumr, alog_b, d_b, name="ssd_scan_fwd")
    norm_w = row(wts["ssd_norm_w"])
    yn = _gate_norm_fwd(y_ssd, zx, norm_w, name="ssd_gate_norm_fwd")
    mix0 = _matmul(yn, wts["ssd_w_out"], mode="nn", out_dtypes=(BF16,), tk=BIG_TILE, name="ssd_out_proj")
    h1, v0 = _rms_fwd(mix0, row(mix_post[0]), resid=x, want_u=row(ffn_pre[0]), name="rms_post_mix0")
    act0, f0 = _mlp_fwd(v0, wts["mlp_w_up0"], wts["mlp_w_down0"], "l0", run=gathering)
    h2, u1 = _rms_fwd(f0, row(ffn_post[0]), resid=h1, want_u=row(mix_pre[1]), name="rms_post_ffn0")

    qkv = _matmul(u1, wts["attn_w_qkv"], mode="nn", out_dtypes=(BF16,), epilogue=_bias_epilogue,
                  extras=((row(wts["attn_b_qkv"]), "row"),), b_shards=True, name="attn_qkv_proj")
    sinks_b = jnp.broadcast_to(wts["attn_sinks"].reshape(ATTN_N_Q, 1), (ATTN_N_Q, LANES))
    ao = gathering("attn_fwd", _attn_fwd, qkv, sinks_b, name="attn_fwd")
    mix1 = _matmul(ao, wts["attn_w_o"], mode="nn", out_dtypes=(BF16,), epilogue=_bias_epilogue,
                   extras=((row(wts["attn_b_o"]), "row"),), name="attn_out_proj")
    h3, v1 = _rms_fwd(mix1, row(mix_post[1]), resid=h2, want_u=row(ffn_pre[1]), name="rms_post_mix1")
    act1, f1 = _mlp_fwd(v1, wts["mlp_w_up1"], wts["mlp_w_down1"], "l1")
    dh4, loss_tile = _rms_fwd(f1, row(ffn_post[1]), resid=h3, target=target, name="rms_post_ffn1_loss")

    df1, g_ffn_post1 = _rms_bwd(f1, row(ffn_post[1]), dh4, out_dtype=BF16, name="rms_post_ffn1_bwd")
    dv1, g_up1, g_down1 = _mlp_bwd(v1, act1, wts["mlp_w_up1"], wts["mlp_w_down1"], df1, "l1")
    dh3, g_ffn_pre1 = _rms_bwd(h3, row(ffn_pre[1]), dv1, resid=dh4, name="rms_pre_ffn1_bwd")
    dmix1, g_mix_post1, g_b_o = _rms_bwd(mix1, row(mix_post[1]), dh3, out_dtype=BF16, dx_col_sum=True, name="rms_post_mix1_bwd")
    g_w_o = _matmul(ao, dmix1, mode="tn", out_dtypes=(BF16,), tk=BIG_TILE, name="attn_dwo")
    dao = _matmul(dmix1, wts["attn_w_o"], mode="nt", out_dtypes=(BF16,), name="attn_dao")
    dq, dkv, g_sinks = _attn_bwd(qkv, sinks_b, dao, name="attn_bwd")
    dqkv = jnp.concatenate([dq, dkv], axis=1)
    g_b_qkv = _col_sum(dqkv, name="attn_bqkv_grad")
    g_w_qkv = _matmul(u1, dqkv, mode="tn", out_dtypes=(BF16,), tn=ATTN_QKV // N_CHIPS, out_shards=True, tk=BIG_TILE, name="attn_dwqkv")
    du1 = _matmul(dqkv, wts["attn_w_qkv"], mode="nt", out_dtypes=(BF16,), b_shards=True, name="attn_du")
    dh2, g_mix_pre1 = _rms_bwd(h2, row(mix_pre[1]), du1, resid=dh3, name="rms_pre_mix1_bwd")

    df0, g_ffn_post0 = _rms_bwd(f0, row(ffn_post[0]), dh2, out_dtype=BF16, name="rms_post_ffn0_bwd")
    dv0, g_up0, g_down0 = _mlp_bwd(v0, act0, wts["mlp_w_up0"], wts["mlp_w_down0"], df0, "l0")
    dh1, g_ffn_pre0 = _rms_bwd(h1, row(ffn_pre[0]), dv0, resid=dh2, name="rms_pre_ffn0_bwd")
    dmix0, g_mix_post0 = _rms_bwd(mix0, row(mix_post[0]), dh1, out_dtype=BF16, name="rms_post_mix0_bwd")
    g_w_out = _matmul(yn, dmix0, mode="tn", out_dtypes=(BF16,), tk=BIG_TILE, name="ssd_dwout")
    dyn = _matmul(dmix0, wts["ssd_w_out"], mode="nt", out_dtypes=(BF16,), name="ssd_dyn")
    dy_ssd, dzx, g_norm_w = _gate_norm_bwd(y_ssd, zx, norm_w, dyn, name="ssd_gate_norm_bwd")
    mats = {"ssd_w_out": g_w_out, "attn_w_qkv": g_w_qkv, "attn_w_o": g_w_o,
            "mlp_w_up0": g_up0, "mlp_w_up1": g_up1, "mlp_w_down0": g_down0, "mlp_w_down1": g_down1}
    if comm is None:
        dxc, dbm, dcm, ddt_r, dpar = _ssd_bwd(xc, dtr, cumr, alog_b, d_b, states, dy_ssd, name="ssd_scan_bwd")
    else:
        (dxc, dbm, dcm, ddt_r, dpar), received = _ssd_bwd(xc, dtr, cumr, alog_b, d_b, states, dy_ssd,
                                                          name="ssd_scan_bwd", hook=comm.exchange_hook(mats, "early"))
        comm.received(received)
    dzx, g_conv_w, g_conv_b = _conv_bwd(zx, wts["ssd_conv_w"], row(wts["ssd_conv_b"]), dxc, dbm, dcm, dzx, name="ssd_conv_bwd")
    dzx, g_dt_bias = _softplus_bwd(dt_raw, bias_row, ddt_r, dzx, name="ssd_dt_bwd")
    g_w_in = _w_in_to_shards(_matmul(u0, dzx, mode="tn", out_dtypes=(BF16,), tn=SSD_IN_TILE, tk=BIG_TILE, name="ssd_dwin"), name="ssd_dwin_shards")
    mats["ssd_w_in"] = g_w_in
    if comm is None:
        du0 = _matmul(dzx, wts["ssd_w_in"], mode="nt", out_dtypes=(BF16,), tk=SSD_IN_TILE, name="ssd_du")
    else:
        du0, received = _matmul(dzx, wts["ssd_w_in"], mode="nt", out_dtypes=(BF16,), tk=SSD_IN_TILE, name="ssd_du",
                                hook=comm.exchange_hook(mats, "late"))
        comm.received(received)
    grad_x, g_mix_pre0 = _rms_bwd(x, row(mix_pre[0]), du0, resid=dh1, name="rms_pre_mix0_bwd")

    dpar = dpar.reshape(SSD_N_HEADS, LANES)
    vecs = {
        "ssd_conv_w": g_conv_w, "ssd_conv_b": g_conv_b.reshape(-1),
        "ssd_dt_bias": g_dt_bias[0, :SSD_N_HEADS], "ssd_a_log": dpar[:, 0], "ssd_d": dpar[:, 1],
        "ssd_norm_w": g_norm_w.reshape(-1), "attn_b_qkv": g_b_qkv.reshape(-1), "attn_sinks": g_sinks[:, 0],
        "attn_b_o": g_b_o.reshape(-1),
        "mix_pre_norm": jnp.concatenate([g_mix_pre0, g_mix_pre1]), "mix_post_norm": jnp.concatenate([g_mix_post0, g_mix_post1]),
        "ffn_pre_norm": jnp.concatenate([g_ffn_pre0, g_ffn_pre1]), "ffn_post_norm": jnp.concatenate([g_ffn_post0, g_ffn_post1]),
    }
    return loss_tile, grad_x, mats, vecs


def _mesh_position():
    return lax.axis_index("x"), lax.axis_index("y"), lax.axis_index("c")


def _flip(v, bit):
    return 1 - v if bit else v


OTHER_CHIPS = ((1, 0), (0, 1), (1, 1))


def _comm_params():
    return pltpu.CompilerParams(vmem_limit_bytes=VMEM_LIMIT)


def _staged_copies(srcs, dsts, bufs, sems_in, sems_out):
    loads = [pltpu.make_async_copy(s, b, sems_in.at[i]) for i, (s, b) in enumerate(zip(srcs, bufs))]
    stores = [pltpu.make_async_copy(b, d, sems_out.at[i]) for i, (b, d) in enumerate(zip(bufs, dsts))]
    return loads, stores


class _GatherHook:
    def __init__(self, mats, vecs=()):
        self.arrs = list(mats) + list(vecs)
        self.nm, self.n = len(mats), len(self.arrs)
        n_ici, n_fwd = (N_CHIPS - 1) * self.n, max((N_CHIPS - 1) * self.nm, 1)
        dma = pltpu.SemaphoreType.DMA
        self.out_shape = [jax.ShapeDtypeStruct((N_CHIPS,) + a.shape, a.dtype) for a in self.arrs]
        self.scratch = [pltpu.VMEM(a.shape, a.dtype) for a in self.arrs] + [
            dma((n_ici,)), dma((n_ici,)), dma((n_fwd,)), dma((n_fwd,)), dma((self.n,)), dma((self.n,))]

    def plan(self, ins, outs, scratch):
        n, nm = self.n, self.nm
        bufs = scratch[:n]
        ici_send, ici_recv, fwd_send, fwd_recv, load_sems, store_sems = scratch[n:]
        xi, yi, ci = _mesh_position()
        me = 2 * xi + yi
        loads, stores = _staged_copies(ins, [outs[i].at[me] for i in range(n)], bufs, load_sems, store_sems)
        sends, landed, forwards, from_sibling = [], [], [], []
        for j, (bx, by) in enumerate(OTHER_CHIPS):
            px, py = _flip(xi, bx), _flip(yi, by)
            peer = 2 * px + py
            for i in range(n):
                k = j * n + i
                mk = functools.partial(pltpu.make_async_remote_copy, send_sem=ici_send.at[k], recv_sem=ici_recv.at[k],
                                       device_id=(px, py, ci), device_id_type=MESH)
                if i < nm:
                    sends.append(mk(src_ref=ins[i].at[ci], dst_ref=outs[i].at[me, ci]))
                    landed.append(mk(src_ref=ins[i].at[ci], dst_ref=outs[i].at[peer, ci]))
                    kf = j * nm + i
                    fw = functools.partial(pltpu.make_async_remote_copy, send_sem=fwd_send.at[kf], recv_sem=fwd_recv.at[kf],
                                           device_id=(xi, yi, 1 - ci), device_id_type=MESH)
                    forwards.append(fw(src_ref=outs[i].at[peer, ci], dst_ref=outs[i].at[peer, ci]))
                    from_sibling.append(fw(src_ref=outs[i].at[peer, ci], dst_ref=outs[i].at[peer, 1 - ci]))
                else:
                    sends.append(mk(src_ref=ins[i], dst_ref=outs[i].at[me]))
                    landed.append(mk(src_ref=ins[i], dst_ref=outs[i].at[peer]))
                    forwards.append(None)
        return loads, stores, sends, landed, forwards, from_sibling

    @staticmethod
    def start(p):
        loads, _, sends, _, _, _ = p
        for cp in loads + sends:
            cp.start()

    @staticmethod
    def relay(p):
        loads, stores, _, landed, forwards, _ = p
        for ld, st in zip(loads, stores):
            ld.wait()
            st.start()
        for cp, fw in zip(landed, forwards):
            cp.wait_recv()
            if fw is not None:
                fw.start()

    @staticmethod
    def finish(p):
        _, stores, sends, _, forwards, from_sibling = p
        for cp in from_sibling:
            cp.wait_recv()
        for cp in sends + [fw for fw in forwards if fw is not None]:
            cp.wait_send()
        for st in stores:
            st.wait()


def _run_hook(hook, ins, outs, scratch, step, n_steps):
    p = hook.plan(ins, outs, scratch)
    relay_step = min(max(1, (3 * n_steps) // 4), n_steps - 1)

    @pl.when(step == 0)
    def _():
        hook.start(p)

    if relay_step < n_steps - 1:
        @pl.when(step == relay_step)
        def _():
            hook.relay(p)

    @pl.when(step == n_steps - 1)
    def _():
        if relay_step == n_steps - 1:
            hook.relay(p)
        hook.finish(p)


def _hook_call(hook, *, name):
    n = len(hook.arrs)

    def body(*refs):
        p = hook.plan(refs[:n], refs[n:n + len(hook.out_shape)], refs[n + len(hook.out_shape):])
        hook.start(p)
        hook.relay(p)
        hook.finish(p)

    return pl.pallas_call(
        body, in_specs=[ANY] * n, out_specs=[ANY] * len(hook.out_shape), out_shape=hook.out_shape,
        scratch_shapes=hook.scratch, compiler_params=_comm_params(), name=name)(*hook.arrs)


def _send_other_half(parts, *, name):
    n = len(parts)

    def body(*refs):
        ins, outs = refs[:n], refs[n:2 * n]
        send_sems, recv_sems = refs[2 * n:]
        xi, yi, ci = _mesh_position()
        sibling = (xi, yi, 1 - ci)
        for i in range(n):
            for s in range(N_CHIPS):
                pltpu.make_async_remote_copy(src_ref=ins[i].at[s, 1 - ci], dst_ref=outs[i].at[s], send_sem=send_sems.at[i],
                                             recv_sem=recv_sems.at[i], device_id=sibling, device_id_type=MESH).start()
        for i in range(n):
            pltpu.make_async_remote_copy(src_ref=outs[i], dst_ref=outs[i], send_sem=send_sems.at[i], recv_sem=recv_sems.at[i],
                                         device_id=sibling, device_id_type=MESH).wait()

    return pl.pallas_call(
        body, in_specs=[ANY] * n, out_specs=[ANY] * n,
        out_shape=[jax.ShapeDtypeStruct((p.shape[0],) + p.shape[2:], p.dtype) for p in parts],
        scratch_shapes=[pltpu.SemaphoreType.DMA((n,)), pltpu.SemaphoreType.DMA((n,))],
        name=name)(*parts)


ROW_BLOCKS = 8


def _add_sibling_half(parts, theirs, core, *, name):
    n = len(parts)

    def body(core_ref, *refs):
        for a_ref, b_ref, o_ref in zip(refs[:n], refs[n:2 * n], refs[2 * n:]):
            o_ref[...] = (a_ref[...].astype(F32) + b_ref[...].astype(F32)).astype(o_ref.dtype)

    mine = lambda p: pl.BlockSpec((None, None, p.shape[2] // ROW_BLOCKS, p.shape[3]), lambda s, rb, core_ref: (s, core_ref[0], rb, 0))
    other = lambda p: pl.BlockSpec((None, p.shape[1] // ROW_BLOCKS, p.shape[2]), lambda s, rb, core_ref: (s, rb, 0))
    return pl.pallas_call(
        body,
        grid_spec=pltpu.PrefetchScalarGridSpec(
            num_scalar_prefetch=1, grid=(N_CHIPS, ROW_BLOCKS),
            in_specs=[mine(p) for p in parts] + [other(q) for q in theirs], out_specs=[other(q) for q in theirs]),
        out_shape=[jax.ShapeDtypeStruct(q.shape, BF16) for q in theirs],
        compiler_params=_params("parallel", "parallel"), name=name)(core, *parts, *theirs)


class _ExchangeHook:
    def __init__(self, parts, to_all=()):
        self.arrs = list(parts) + list(to_all)
        self.n_parts, self.n = len(parts), len(self.arrs)
        n_ici, n_peer = max((N_CHIPS - 1) * self.n_parts, 1), (N_DEV - 1) * max(len(to_all), 1)
        dma = pltpu.SemaphoreType.DMA
        self.out_shape = [jax.ShapeDtypeStruct(p.shape, p.dtype) for p in parts] + [
            jax.ShapeDtypeStruct((N_DEV,) + a.shape, a.dtype) for a in to_all]
        self.scratch = [pltpu.VMEM(p.shape[1:], p.dtype) for p in parts] + [pltpu.VMEM(a.shape, a.dtype) for a in to_all] + [
            dma((n_ici,)), dma((n_ici,)), dma((n_peer,)), dma((n_peer,)), dma((self.n,)), dma((self.n,))]

    def plan(self, ins, outs, scratch):
        n, npt = self.n, self.n_parts
        bufs = scratch[:n]
        send_sems, recv_sems, all_send, all_recv, load_sems, store_sems = scratch[n:]
        xi, yi, ci = _mesh_position()
        me_chip = 2 * xi + yi
        me = 4 * xi + 2 * yi + ci
        loads, stores = _staged_copies([ins[i].at[me_chip] for i in range(npt)] + list(ins[npt:]),
                                       [outs[i].at[me_chip] for i in range(npt)] + [outs[i].at[me] for i in range(npt, n)],
                                       bufs, load_sems, store_sems)
        sends, recvs = [], []
        for j, (bx, by) in enumerate(OTHER_CHIPS):
            px, py = _flip(xi, bx), _flip(yi, by)
            peer = 2 * px + py
            for i in range(npt):
                k = j * npt + i
                mk = functools.partial(pltpu.make_async_remote_copy, src_ref=ins[i].at[peer], send_sem=send_sems.at[k],
                                       recv_sem=recv_sems.at[k], device_id=(px, py, ci), device_id_type=MESH)
                sends.append(mk(dst_ref=outs[i].at[me_chip]))
                recvs.append(mk(dst_ref=outs[i].at[peer]))
        for i in range(npt, n):
            for k in range(1, N_DEV):
                px, py, pc = _flip(xi, (k >> 2) & 1), _flip(yi, (k >> 1) & 1), _flip(ci, k & 1)
                slot = (i - npt) * (N_DEV - 1) + k - 1
                mk = functools.partial(pltpu.make_async_remote_copy, src_ref=ins[i], send_sem=all_send.at[slot],
                                       recv_sem=all_recv.at[slot], device_id=(px, py, pc), device_id_type=MESH)
                sends.append(mk(dst_ref=outs[i].at[me]))
                recvs.append(mk(dst_ref=outs[i].at[4 * px + 2 * py + pc]))
        return loads, stores, sends, recvs

    @staticmethod
    def start(p):
        loads, _, sends, _ = p
        for cp in loads + sends:
            cp.start()

    @staticmethod
    def relay(p):
        loads, stores, _, _ = p
        for ld, st in zip(loads, stores):
            ld.wait()
            st.start()

    @staticmethod
    def finish(p):
        _, stores, sends, recvs = p
        for cp in recvs:
            cp.wait_recv()
        for cp in sends:
            cp.wait_send()
        for st in stores:
            st.wait()


def _sum_chips(parts, *, name):
    n = len(parts)
    p = parts[0].shape[0]

    def body(*refs):
        s = pl.program_id(1)
        for x_ref, o_ref in zip(refs[:n], refs[n:]):
            @pl.when(s == 0)
            def _():
                o_ref[...] = x_ref[...].astype(F32)

            @pl.when(s > 0)
            def _():
                o_ref[...] += x_ref[...].astype(F32)

    blocks = lambda q: ROW_BLOCKS if q.shape[1] % (8 * ROW_BLOCKS) == 0 else 1
    assert len({blocks(q) for q in parts}) == 1
    nb = blocks(parts[0])
    return pl.pallas_call(
        body, grid=(nb, p),
        in_specs=[pl.BlockSpec((None, q.shape[1] // nb, q.shape[2]), lambda rb, s: (s, rb, 0)) for q in parts],
        out_specs=[pl.BlockSpec((q.shape[1] // nb, q.shape[2]), lambda rb, s: (rb, 0)) for q in parts],
        out_shape=[jax.ShapeDtypeStruct(q.shape[1:], F32) for q in parts],
        compiler_params=_params("parallel", "arbitrary"), name=name)(*parts)


def _swap_halves(halves, layers, *, name, hook=None):
    n = len(halves)
    out_shapes, slots = [], []
    for i, h in enumerate(halves):
        pair = [p for p in layers if i in p]
        if pair and pair[0][1] == i:
            slots.append((slots[pair[0][0]][0], 1))
        elif pair:
            out_shapes.append(jax.ShapeDtypeStruct((2, 2) + h.shape, h.dtype))
            slots.append((len(out_shapes) - 1, 0))
        else:
            out_shapes.append(jax.ShapeDtypeStruct((2,) + h.shape, h.dtype))
            slots.append((len(out_shapes) - 1, None))
    n_out = len(out_shapes)
    hk = _HookSlots(hook, n_in=n, n_out=n_out, n_scratch=n + 4)

    def body(*refs):
        ins, outs, scratch = hk.own(refs)
        bufs = scratch[:n]
        send_sems, recv_sems, load_sems, store_sems = scratch[n:]
        if hook is not None:
            _, h_in, _, h_out, _, h_scratch = hk._split(refs)
            extra = hook.plan(h_in, h_out, h_scratch)
            hook.start(extra)
        xi, yi, ci = _mesh_position()
        own, sends, recvs = [], [], []
        for i in range(n):
            o, layer = slots[i]
            dst = (lambda core: outs[o].at[core]) if layer is None else (lambda core: outs[o].at[layer, core])
            own.append(dst(ci))
            mk = functools.partial(pltpu.make_async_remote_copy, src_ref=ins[i], send_sem=send_sems.at[i],
                                   recv_sem=recv_sems.at[i], device_id=(xi, yi, 1 - ci), device_id_type=MESH)
            sends.append(mk(dst_ref=dst(ci)))
            recvs.append(mk(dst_ref=dst(1 - ci)))
        loads, stores = _staged_copies(ins, own, bufs, load_sems, store_sems)
        for cp in loads + sends:
            cp.start()
        for ld, st in zip(loads, stores):
            ld.wait()
            st.start()
        for cp in recvs:
            cp.wait_recv()
        for cp in sends:
            cp.wait_send()
        for st in stores:
            st.wait()
        if hook is not None:
            hook.relay(extra)
            hook.finish(extra)

    outs = pl.pallas_call(
        body, in_specs=[ANY] * n + hk.in_specs, out_specs=[ANY] * n_out + hk.out_specs, out_shape=out_shapes + hk.out_shape,
        scratch_shapes=[pltpu.VMEM(h.shape, h.dtype) for h in halves]
        + [pltpu.SemaphoreType.DMA((n,)), pltpu.SemaphoreType.DMA((n,)), pltpu.SemaphoreType.DMA((n,)), pltpu.SemaphoreType.DMA((n,))]
        + hk.scratch,
        compiler_params=_comm_params(), name=name)(*halves, *hk.inputs)
    return outs if hook is None else (outs[:n_out], outs[n_out:])


def _cast_bf16(layers, x, norm_w, *, name, hook=None):
    n = len(layers)
    hk = _HookSlots(hook, n_in=n + 2, n_out=n + 1, n_scratch=0)

    def body(*refs):
        ins, outs, _ = hk.own(refs)
        if hook is not None:
            hk.run(refs, pl.program_id(0), ROW_BLOCKS)
        for i_ref, o_ref in zip(ins[:n], outs[:n]):
            o_ref[...] = i_ref[...].astype(o_ref.dtype)
        xv = ins[n][...]
        outs[n][...] = (xv * lax.rsqrt(jnp.mean(xv * xv, axis=-1, keepdims=True) + NORM_EPS) * ins[n + 1][...]).astype(BF16)

    in_blk = lambda a, l: pl.BlockSpec((None, a.shape[1] // ROW_BLOCKS, a.shape[2]), lambda i: (l, i, 0))
    out_blk = lambda a: pl.BlockSpec((a.shape[1] // ROW_BLOCKS, a.shape[2]), lambda i: (i, 0))
    x_blk = pl.BlockSpec((x.shape[0] // ROW_BLOCKS, x.shape[1]), lambda i: (i, 0))
    outs = pl.pallas_call(
        body, grid=(ROW_BLOCKS,),
        in_specs=[in_blk(a, l) for a, l in layers] + [x_blk, pl.BlockSpec((1, x.shape[1]), lambda i: (0, 0))] + hk.in_specs,
        out_specs=[out_blk(a) for a, _ in layers] + [x_blk] + hk.out_specs,
        out_shape=[jax.ShapeDtypeStruct(a.shape[1:], BF16) for a, _ in layers] + [jax.ShapeDtypeStruct(x.shape, BF16)] + hk.out_shape,
        scratch_shapes=hk.scratch,
        compiler_params=_params(*hk.semantics("parallel")), name=name)(*[a for a, _ in layers], x, norm_w, *hk.inputs)
    own = (outs[:n], outs[n])
    return own if hook is None else (own, outs[n + 1:])


def _full_weight(name, gathered):
    s, _, r, c = gathered.shape
    if name == "ssd_w_in":
        return _w_in_from_shards(gathered.reshape(s, 2 * r, c), name="ssd_w_in_unshard")
    if name in ("attn_w_qkv", "mlp_w_up0", "mlp_w_up1"):
        return gathered.reshape(s, 2 * r, c)
    return gathered.reshape(s * 2 * r, c)


class _StepComm:
    GATHER = {"in_proj": ("mlp_w_up0", "attn_w_o"), "conv": ("mlp_w_down0",), "scan": ("ssd_w_out", "mlp_w_up1"),
              "mlp_up_l0": ("attn_w_qkv",), "attn_fwd": ("mlp_w_down1",)}
    EXCHANGE = {"early": ("ssd_w_out", "attn_w_qkv", "attn_w_o", "mlp_w_up0", "mlp_w_up1", "mlp_w_down0", "mlp_w_down1"),
                "late": ("ssd_w_in",)}

    def __init__(self, shards, core):
        self.shards, self.core = shards, core
        self.chip_parts = {}
        self._pending = None

    def gather_hook(self, stage):
        names = self.GATHER.get(stage)
        return _GatherHook([self.shards[n] for n in names]) if names else None

    def weights_from(self, stage, gathered):
        return {n: _full_weight(n, g) for n, g in zip(self.GATHER[stage], gathered)}

    def chip_sums(self, mats, tag):
        parts = [_shard_halves(a) for a in mats.values()]
        theirs = _send_other_half(parts, name=f"grad_sibling_send_{tag}")
        return _add_sibling_half(parts, theirs, self.core, name=f"grad_chip_sum_{tag}")

    def exchange_hook(self, mats, which):
        self._pending = self.EXCHANGE[which]
        return _ExchangeHook(self.chip_sums({n: mats[n] for n in self._pending}, which))

    def received(self, arrays):
        self.chip_parts.update(zip(self._pending, arrays))


ADAMW_ROW_BLOCKS = 16


def _adamw(ws, gs, ms, vs, *, name, by_lanes=False):
    n = len(ws)
    if by_lanes:
        nb = min(a.shape[2] for a in ws) // LANES
    else:
        nb = ADAMW_ROW_BLOCKS if all(a.shape[1] % (8 * ADAMW_ROW_BLOCKS) == 0 for a in ws) else 1

    def body(*refs):
        ins, outs = refs[:4 * n], refs[4 * n:]
        for i in range(n):
            w_ref, g_ref, m_ref, v_ref = ins[i], ins[n + i], ins[2 * n + i], ins[3 * n + i]
            go_ref, d_ref, nm_ref, nv_ref = outs[i], outs[n + i], outs[2 * n + i], outs[3 * n + i]
            gv = g_ref[...]
            nm = ADAM_B1 * m_ref[...] + (1.0 - ADAM_B1) * gv
            nv = ADAM_B2 * v_ref[...] + (1.0 - ADAM_B2) * (gv * gv)
            m_hat = nm / (1.0 - ADAM_B1 ** ADAM_STEP)
            v_hat = nv / (1.0 - ADAM_B2 ** ADAM_STEP)
            go_ref[...] = gv
            d_ref[...] = -ADAM_LR * (m_hat / (jnp.sqrt(v_hat) + ADAM_EPS) + ADAM_WD * w_ref[...])
            nm_ref[...] = nm
            nv_ref[...] = nv

    if by_lanes:
        blks = [pl.BlockSpec((a.shape[0], a.shape[1], a.shape[2] // nb), lambda i: (0, 0, i)) for a in ws]
    else:
        blks = [pl.BlockSpec((a.shape[0], a.shape[1] // nb, a.shape[2]), lambda i: (0, i, 0)) for a in ws]
    shapes = [jax.ShapeDtypeStruct(a.shape, F32) for a in ws]
    outs = pl.pallas_call(body, grid=(nb,), in_specs=blks * 4, out_specs=blks * 4, out_shape=shapes * 4,
                          compiler_params=_params("parallel"), name=name)(*ws, *gs, *ms, *vs)
    return [tuple(outs[k * n + i] for k in range(4)) for i in range(n)]


SM_CONV_B, SM_NORM_W, SM_MIX_PRE, SM_MIX_POST, SM_FFN_PRE, SM_FFN_POST, SM_MISC, SM_CONV_W, SM_B_QKV, SM_B_O = 0, 4, 6, 8, 10, 12, 14, 16, 32, 34
SM_ROWS = 40
MISC_DT_BIAS, MISC_A_LOG, MISC_D, MISC_SINKS, MISC_LOSS = 0, 32, 64, 96, 112


def _shard_halves(a):
    c = a.shape[-1]
    return a.reshape(N_CHIPS, 2, -1, c)


def _rows(v):
    return v.reshape(-1, D_MODEL)


def _misc_row(dt_bias, a_log, d, sinks, loss):
    pad = jnp.zeros((D_MODEL - MISC_LOSS - 1,), F32)
    return jnp.concatenate([dt_bias.reshape(-1), a_log.reshape(-1), d.reshape(-1), sinks.reshape(-1), loss.reshape(1), pad]).reshape(1, D_MODEL)


def _replicated_rows(p, loss):
    return jnp.concatenate([
        _rows(p["ssd_conv_b"]), _rows(p["ssd_norm_w"]), _rows(p["mix_pre_norm"]), _rows(p["mix_post_norm"]),
        _rows(p["ffn_pre_norm"]), _rows(p["ffn_post_norm"]),
        _misc_row(p["ssd_dt_bias"], p["ssd_a_log"], p["ssd_d"], p["attn_sinks"], loss), jnp.zeros((1, D_MODEL), F32)], axis=0)


def _sharded_rows(conv_w, b_qkv, b_o):
    last = jnp.concatenate([b_qkv.reshape(-1), b_o.reshape(-1), jnp.zeros((D_MODEL - 640,), F32)]).reshape(1, D_MODEL)
    return jnp.concatenate([conv_w.reshape(SSD_CONV_WIDTH, D_MODEL), last, jnp.zeros((3, D_MODEL), F32)], axis=0)


REPLICATED = ("ssd_conv_b", "ssd_dt_bias", "ssd_a_log", "ssd_d", "ssd_norm_w", "attn_sinks",
              "mix_pre_norm", "mix_post_norm", "ffn_pre_norm", "ffn_post_norm")
MATRICES = ("ssd_w_in", "ssd_w_out", "attn_w_qkv", "attn_w_o", "mlp_w_up", "mlp_w_down")
WEIGHT_NAMES = ("ssd_w_in", "ssd_conv_w", "ssd_conv_b", "ssd_dt_bias", "ssd_a_log", "ssd_d", "ssd_norm_w", "ssd_w_out",
                "attn_w_qkv", "attn_b_qkv", "attn_sinks", "attn_w_o", "attn_b_o", "mlp_w_up", "mlp_w_down",
                "mix_pre_norm", "mix_post_norm", "ffn_pre_norm", "ffn_post_norm")


def _unpack_small(rows16, rows8, like):
    misc = rows16[SM_MISC]
    out = {
        "ssd_conv_b": rows16[SM_CONV_B:SM_CONV_B + 4], "ssd_norm_w": rows16[SM_NORM_W:SM_NORM_W + 2],
        "mix_pre_norm": rows16[SM_MIX_PRE:SM_MIX_PRE + 2], "mix_post_norm": rows16[SM_MIX_POST:SM_MIX_POST + 2],
        "ffn_pre_norm": rows16[SM_FFN_PRE:SM_FFN_PRE + 2], "ffn_post_norm": rows16[SM_FFN_POST:SM_FFN_POST + 2],
        "ssd_dt_bias": misc[MISC_DT_BIAS:MISC_DT_BIAS + 32], "ssd_a_log": misc[MISC_A_LOG:MISC_A_LOG + 32],
        "ssd_d": misc[MISC_D:MISC_D + 32], "attn_sinks": misc[MISC_SINKS:MISC_SINKS + 16],
        "ssd_conv_w": rows8[0:SSD_CONV_WIDTH], "attn_b_qkv": rows8[SSD_CONV_WIDTH, 0:384], "attn_b_o": rows8[SSD_CONV_WIDTH, 384:640],
    }
    return {k: v.reshape(like[k].shape) for k, v in out.items()}


def kernel(x, ssd_w_in, ssd_conv_w, ssd_conv_b, ssd_dt_bias, ssd_a_log, ssd_d, ssd_norm_w, ssd_w_out, attn_w_qkv, attn_b_qkv, attn_sinks, attn_w_o, attn_b_o, mlp_w_up, mlp_w_down, mix_pre_norm, mix_post_norm, ffn_pre_norm, ffn_post_norm, loss_target, m_ssd_w_in, m_ssd_conv_w, m_ssd_conv_b, m_ssd_dt_bias, m_ssd_a_log, m_ssd_d, m_ssd_norm_w, m_ssd_w_out, m_attn_w_qkv, m_attn_b_qkv, m_attn_sinks, m_attn_w_o, m_attn_b_o, m_mlp_w_up, m_mlp_w_down, m_mix_pre_norm, m_mix_post_norm, m_ffn_pre_norm, m_ffn_post_norm, v_ssd_w_in, v_ssd_conv_w, v_ssd_conv_b, v_ssd_dt_bias, v_ssd_a_log, v_ssd_d, v_ssd_norm_w, v_ssd_w_out, v_attn_w_qkv, v_attn_b_qkv, v_attn_sinks, v_attn_w_o, v_attn_b_o, v_mlp_w_up, v_mlp_w_down, v_mix_pre_norm, v_mix_post_norm, v_ffn_pre_norm, v_ffn_post_norm):
    w = dict(zip(WEIGHT_NAMES, (ssd_w_in, ssd_conv_w, ssd_conv_b, ssd_dt_bias, ssd_a_log, ssd_d, ssd_norm_w, ssd_w_out, attn_w_qkv, attn_b_qkv, attn_sinks, attn_w_o, attn_b_o, mlp_w_up, mlp_w_down, mix_pre_norm, mix_post_norm, ffn_pre_norm, ffn_post_norm)))
    m = dict(zip(WEIGHT_NAMES, (m_ssd_w_in, m_ssd_conv_w, m_ssd_conv_b, m_ssd_dt_bias, m_ssd_a_log, m_ssd_d, m_ssd_norm_w, m_ssd_w_out, m_attn_w_qkv, m_attn_b_qkv, m_attn_sinks, m_attn_w_o, m_attn_b_o, m_mlp_w_up, m_mlp_w_down, m_mix_pre_norm, m_mix_post_norm, m_ffn_pre_norm, m_ffn_post_norm)))
    v = dict(zip(WEIGHT_NAMES, (v_ssd_w_in, v_ssd_conv_w, v_ssd_conv_b, v_ssd_dt_bias, v_ssd_a_log, v_ssd_d, v_ssd_norm_w, v_ssd_w_out, v_attn_w_qkv, v_attn_b_qkv, v_attn_sinks, v_attn_w_o, v_attn_b_o, v_mlp_w_up, v_mlp_w_down, v_mix_pre_norm, v_mix_post_norm, v_ffn_pre_norm, v_ffn_post_norm)))
    chip = 2 * lax.axis_index("x") + lax.axis_index("y")

    two_halves = lambda a: a.reshape(2, a.shape[-2] // 2, a.shape[-1])
    later = {"ssd_w_out": (w["ssd_w_out"], 0), "attn_w_qkv": (w["attn_w_qkv"], 0), "attn_w_o": (w["attn_w_o"], 0),
             "mlp_w_up0": (w["mlp_w_up"], 0), "mlp_w_up1": (w["mlp_w_up"], 1),
             "mlp_w_down0": (w["mlp_w_down"], 0), "mlp_w_down1": (w["mlp_w_down"], 1)}
    first = _GatherHook([two_halves(w["ssd_w_in"].astype(BF16))], [w["ssd_conv_w"][0], w["attn_b_qkv"], w["attn_b_o"]])
    (cast, u0), (g_in, g_conv, g_bqkv, g_bo) = _cast_bf16(list(later.values()), x[0], w["mix_pre_norm"][0:1],
                                                          name="weights_to_bf16", hook=first)
    core = lax.axis_index("c").astype(jnp.int32).reshape(1)
    comm = _StepComm({k: two_halves(a) for k, a in zip(later, cast)}, core)
    full = {
        "ssd_w_in": _full_weight("ssd_w_in", g_in),
        "ssd_conv_w": g_conv.transpose(1, 0, 2).reshape(SSD_CONV_WIDTH, SSD_CONV_DIM),
        "attn_b_qkv": g_bqkv.reshape(ATTN_QKV), "attn_b_o": g_bo.reshape(D_MODEL),
    }
    for name in REPLICATED:
        full[name] = w[name][0] if name.startswith(("ssd_", "attn_")) else w[name]

    loss_tile, grad_x, gm, g = _local_step(x[0], loss_target[0], full, comm, u0)

    conv_w_rows = g["ssd_conv_w"].reshape(SSD_CONV_WIDTH * N_CHIPS, D_MODEL)
    b_qkv_rows = jnp.pad(g["attn_b_qkv"], (0, 2 * D_MODEL - ATTN_QKV)).reshape(2, D_MODEL)
    small = jnp.concatenate([_replicated_rows(g, loss_tile[0, 0]), conv_w_rows, b_qkv_rows, _rows(g["attn_b_o"]),
                             jnp.zeros((SM_ROWS - SM_B_O - 1, D_MODEL), F32)], axis=0)
    order = ("ssd_w_in", "ssd_w_out", "attn_w_qkv", "attn_w_o", "mlp_w_up0", "mlp_w_up1", "mlp_w_down0", "mlp_w_down1")
    halves = _sum_chips([comm.chip_parts[k] for k in order], name="grad_sum")
    (r_in, r_out, r_qkv, r_o, r_up, r_down), (small_all,) = _swap_halves(
        halves, layers=((4, 5), (6, 7)), hook=_ExchangeHook([], [small]), name="grad_halves_swap")
    small_sum, = _sum_chips([small_all], name="small_grad_sum")

    grads = {"ssd_w_in": r_in, "ssd_w_out": r_out, "attn_w_qkv": r_qkv, "attn_w_o": r_o, "mlp_w_up": r_up, "mlp_w_down": r_down}
    grads = {k: a.reshape(w[k].shape) for k, a in grads.items()}
    conv_w_g = lax.dynamic_index_in_dim(small_sum[SM_CONV_W:SM_CONV_W + 16].reshape(SSD_CONV_WIDTH, N_CHIPS, D_MODEL), chip, axis=1, keepdims=False)
    b_qkv_g = lax.dynamic_slice_in_dim(small_sum[SM_B_QKV:SM_B_QKV + 2].reshape(-1), chip * 384, 384)
    b_o_g = lax.dynamic_slice_in_dim(small_sum[SM_B_O], chip * 256, 256)
    small_g = jnp.concatenate([small_sum[0:16], _sharded_rows(conv_w_g, b_qkv_g, b_o_g)], axis=0)
    grads.update(_unpack_small(small_g[0:16], small_g[16:24], w))
    loss = small_sum[SM_MISC, MISC_LOSS]

    delta, new_m, new_v = {}, {}, {}
    stored = lambda a: jnp.swapaxes(a, 1, 2)
    rest = [name for name in MATRICES if name != "ssd_w_in"]
    mats = lambda p: [p[name] for name in rest]
    results = dict(zip(rest, _adamw(mats(w), mats(grads), mats(m), mats(v), name="adamw_matrices")))
    (w_in_result,) = _adamw([stored(w["ssd_w_in"])], [stored(grads["ssd_w_in"])], [stored(m["ssd_w_in"])],
                            [stored(v["ssd_w_in"])], by_lanes=True, name="adamw_ssd_w_in")
    results["ssd_w_in"] = tuple(stored(a) for a in w_in_result)
    for name in MATRICES:
        grads[name], delta[name], new_m[name], new_v[name] = results[name]
    zero = jnp.zeros((), F32)
    small_pack = lambda p: jnp.concatenate([_replicated_rows({k: p[k] for k in REPLICATED}, zero),
                                            _sharded_rows(p["ssd_conv_w"], p["attn_b_qkv"], p["attn_b_o"])], axis=0)[None]
    (_, d_s, m_s, v_s), = _adamw([small_pack(w)], [small_g[None]], [small_pack(m)], [small_pack(v)], name="adamw_vectors")
    d_s, m_s, v_s = d_s[0], m_s[0], v_s[0]
    delta.update(_unpack_small(d_s[0:16], d_s[16:24], w))
    new_m.update(_unpack_small(m_s[0:16], m_s[16:24], w))
    new_v.update(_unpack_small(v_s[0:16], v_s[16:24], w))

    return (loss, grad_x[None], *[grads[n] for n in WEIGHT_NAMES], *[delta[n] for n in WEIGHT_NAMES],
            *[new_m[n] for n in WEIGHT_NAMES], *[new_v[n] for n in WEIGHT_NAMES])
```

```python
import functools

import jax
import jax.numpy as jnp
from jax import lax
from jax.experimental import pallas as pl
from jax.experimental.pallas import tpu as pltpu

F32 = jnp.float32
BF16 = jnp.bfloat16

D_MODEL = 1024
SSD_D_INNER = 2048
SSD_HEAD_DIM = 64
SSD_N_HEADS = 32
SSD_N_GROUPS = 8
SSD_HPG = 4
SSD_D_STATE = 128
SSD_CONV_WIDTH = 4
SSD_CHUNK = 128
SSD_CONV_DIM = 4096
SSD_IN_DIM = 6176
SSD_IN_PAD = 6400
SSD_IN_TILE = 1280
SSD_DT_COL = 6144
SSD_GW = SSD_HPG * SSD_HEAD_DIM
ATTN_HEAD_DIM = 64
ATTN_N_Q = 16
ATTN_N_KV = 4
ATTN_REP = 4
ATTN_WINDOW = 128
ATTN_QKV = 1536
D_FF = 4096
NORM_EPS = 1e-6

ADAM_LR = 0.001
ADAM_B1 = 0.9
ADAM_B2 = 0.999
ADAM_EPS = 1e-08
ADAM_WD = 0.01
ADAM_STEP = 10

N_CHIPS = 4
N_DEV = 8
LANES = 128
VMEM_LIMIT = 48 * 1024 * 1024
BIG_TILE = 2048
MESH = pl.DeviceIdType.MESH


def _params(*sem):
    return pltpu.CompilerParams(dimension_semantics=sem, vmem_limit_bytes=VMEM_LIMIT)


def _dot(a, b, dims):
    return lax.dot_general(a, b, (dims, ((), ())), preferred_element_type=F32)


def _dot_nn(a, b):
    return _dot(a, b, ((1,), (0,)))


def _dot_nt(a, b):
    return _dot(a, b, ((1,), (1,)))


def _dot_tn(a, b):
    return _dot(a, b, ((0,), (0,)))


def _sigmoid(x):
    return 0.5 * jnp.tanh(0.5 * x) + 0.5


ANY = pl.BlockSpec(memory_space=pl.ANY)


class _HookSlots:
    def __init__(self, hook, n_in, n_out, n_scratch):
        self.hook = hook
        self.n_in, self.n_out, self.n_scratch = n_in, n_out, n_scratch
        self.inputs = list(hook.arrs) if hook else []
        self.out_shape = list(hook.out_shape) if hook else []
        self.scratch = list(hook.scratch) if hook else []
        self.in_specs = [ANY] * len(self.inputs)
        self.out_specs = [ANY] * len(self.out_shape)

    def _split(self, refs):
        a = self.n_in
        b = a + len(self.inputs)
        c = b + self.n_out
        d = c + len(self.out_shape)
        e = d + self.n_scratch
        return refs[:a], refs[a:b], refs[b:c], refs[c:d], refs[d:e], refs[e:]

    def own(self, refs):
        ins, _, outs, _, scratch, _ = self._split(refs)
        return ins, outs, scratch

    def plan(self, refs):
        _, h_in, _, h_out, _, h_scratch = self._split(refs)
        return self.hook.plan(h_in, h_out, h_scratch)

    def run(self, refs, step, n_steps):
        _, h_in, _, h_out, _, h_scratch = self._split(refs)
        _run_hook(self.hook, h_in, h_out, h_scratch, step, n_steps)

    def semantics(self, *sem):
        return sem if self.hook is None else ("arbitrary",) * len(sem)


def _matmul(a, b, *, mode, out_dtypes, name, epilogue=None, extras=(), tm=1024, tn=1024, tk=1024,
            b_shards=False, out_shards=False, hook=None, f32_block=None):
    f32_tail = f32_block is not None
    if b_shards:
        s, b_rows, b_cols = b.shape
        b2 = (b_rows, s * b_cols)
        if mode == "nn":
            tn = b_cols
        else:
            assert mode == "nt"
            tk = b_cols
    else:
        b2 = b.shape
    if mode == "nn":
        (m, k), (k2, n) = a.shape, b2
    elif mode == "nt":
        (m, k), (n, k2) = a.shape, b2
    else:
        (k, m), (k2, n) = a.shape, b2
    assert k == k2, (a.shape, b.shape, mode)
    tm, tn, tk = min(tm, m), min(tn, n), min(tk, k)
    assert m % tm == 0 and n % tn == 0 and k % tk == 0, (m, n, k, tm, tn, tk)
    nk = k // tk
    if mode == "tn":
        a_spec = pl.BlockSpec((tk, tm), lambda i, j, kk: (kk, i))
    else:
        a_spec = pl.BlockSpec((tm, tk), lambda i, j, kk: (i, kk))
    if b_shards and mode == "nn":
        b_spec = pl.BlockSpec((None, tk, tn), lambda i, j, kk: (j, kk, 0))
    elif b_shards:
        b_spec = pl.BlockSpec((None, tn, tk), lambda i, j, kk: (kk, j, 0))
    elif mode == "nt":
        b_spec = pl.BlockSpec((tn, tk), lambda i, j, kk: (j, kk))
    else:
        b_spec = pl.BlockSpec((tk, tn), lambda i, j, kk: (kk, j))
    dims = {"nn": ((1,), (0,)), "nt": ((1,), (1,)), "tn": ((0,), (0,))}[mode]
    ex_specs = []
    for arr, kind in extras:
        if kind == "tile":
            ex_specs.append(pl.BlockSpec((tm, tn), lambda i, j, kk: (i, j)))
        else:
            ex_specs.append(pl.BlockSpec((1, tn), lambda i, j, kk: (0, j)))
    n_ex, n_out = len(extras), len(out_dtypes)
    if epilogue is None:
        epilogue = lambda acc: (acc,)
    hk = _HookSlots(hook, n_in=2 + n_ex, n_out=n_out + f32_tail, n_scratch=0 if nk == 1 else 1)
    grid = (m // tm, n // tn, nk)

    def body(*refs):
        (a_ref, b_ref, *ex), outs, scratch = hk.own(refs)
        if hook is not None:
            step = (pl.program_id(0) * grid[1] + pl.program_id(1)) * grid[2] + pl.program_id(2)
            hk.run(refs, step, grid[0] * grid[1] * grid[2])

        def finish(acc):
            res = epilogue(acc, *[e[...] for e in ex])
            for o, r in zip(outs, res):
                o[...] = r.astype(o.dtype)
            if f32_tail:
                outs[n_out][...] = acc[:, f32_block:f32_block + LANES]

        if nk == 1:
            finish(_dot(a_ref[...], b_ref[...], dims))
        else:
            acc_ref = scratch[0]
            kk = pl.program_id(2)

            @pl.when(kk == 0)
            def _():
                acc_ref[...] = jnp.zeros_like(acc_ref)

            acc_ref[...] += _dot(a_ref[...], b_ref[...], dims)

            @pl.when(kk == nk - 1)
            def _():
                finish(acc_ref[...])

    if out_shards:
        out_spec = pl.BlockSpec((None, tm, tn), lambda i, j, kk: (j, i, 0))
        out_dims = (n // tn, m, tn)
    else:
        out_spec = pl.BlockSpec((tm, tn), lambda i, j, kk: (i, j))
        out_dims = (m, n)
    tail_specs = [pl.BlockSpec((tm, LANES), lambda i, j, kk: (i, 0))] if f32_tail else []
    tail_shapes = [jax.ShapeDtypeStruct((m, LANES), F32)] if f32_tail else []
    outs = pl.pallas_call(
        body,
        grid=grid,
        in_specs=[a_spec, b_spec] + ex_specs + hk.in_specs,
        out_specs=[out_spec for _ in out_dtypes] + tail_specs + hk.out_specs,
        out_shape=[jax.ShapeDtypeStruct(out_dims, dt) for dt in out_dtypes] + tail_shapes + hk.out_shape,
        scratch_shapes=([] if nk == 1 else [pltpu.VMEM((tm, tn), F32)]) + hk.scratch,
        compiler_params=_params(*hk.semantics("parallel", "arbitrary" if f32_tail else "parallel", "arbitrary")),
        name=name,
    )(a, b, *[arr for arr, _ in extras], *hk.inputs)
    n_own = n_out + f32_tail
    own = outs[0] if n_own == 1 else outs[:n_own]
    return own if hook is None else (own, outs[n_own:])


def _row_tile(t, want):
    return min(t, want)


def _rms_fwd(x, w, *, name, resid=None, want_u=None, target=None):
    t, d = x.shape
    tr = _row_tile(t, 512)

    def norm(v, wv):
        return v * lax.rsqrt(jnp.mean(v * v, axis=-1, keepdims=True) + NORM_EPS) * wv

    row = pl.BlockSpec((tr, d), lambda i: (i, 0))
    vec = pl.BlockSpec((1, d), lambda i: (0, 0))
    if target is not None:
        def body(x_ref, w_ref, r_ref, t_ref, dh_ref, loss_ref):
            err = r_ref[...] + norm(x_ref[...].astype(F32), w_ref[...]) - t_ref[...]
            dh_ref[...] = err * (1.0 / d)

            @pl.when(pl.program_id(0) == 0)
            def _():
                loss_ref[...] = jnp.zeros_like(loss_ref)

            part = jnp.sum(jnp.sum(err * err, axis=1, keepdims=True), axis=0, keepdims=True) * (0.5 / d)
            loss_ref[...] += jnp.broadcast_to(part, loss_ref.shape)

        return pl.pallas_call(
            body, grid=(t // tr,), in_specs=[row, vec, row, row],
            out_specs=[row, pl.BlockSpec((8, LANES), lambda i: (0, 0))],
            out_shape=[jax.ShapeDtypeStruct((t, d), F32), jax.ShapeDtypeStruct((8, LANES), F32)],
            compiler_params=_params("arbitrary"), name=name)(x, w, resid, target)
    if resid is None:
        def body(x_ref, w_ref, o_ref):
            o_ref[...] = norm(x_ref[...].astype(F32), w_ref[...]).astype(BF16)
        ins, in_specs = (x, w), [row, vec]
        out_shape, out_specs = jax.ShapeDtypeStruct((t, d), BF16), row
    elif want_u is None:
        def body(x_ref, w_ref, r_ref, o_ref):
            o_ref[...] = r_ref[...] + norm(x_ref[...].astype(F32), w_ref[...])
        ins, in_specs = (x, w, resid), [row, vec, row]
        out_shape, out_specs = jax.ShapeDtypeStruct((t, d), F32), row
    else:
        def body(x_ref, w_ref, r_ref, w2_ref, o_ref, u_ref):
            h = r_ref[...] + norm(x_ref[...].astype(F32), w_ref[...])
            o_ref[...] = h
            u_ref[...] = norm(h, w2_ref[...]).astype(BF16)
        ins, in_specs = (x, w, resid, want_u), [row, vec, row, vec]
        out_shape = [jax.ShapeDtypeStruct((t, d), F32), jax.ShapeDtypeStruct((t, d), BF16)]
        out_specs = [row, row]
    return pl.pallas_call(body, grid=(t // tr,), in_specs=in_specs, out_specs=out_specs, out_shape=out_shape,
                          compiler_params=_params("parallel"), name=name)(*ins)


def _rms_bwd(x, w, dy, *, name, resid=None, out_dtype=F32, dx_col_sum=False):
    t, d = x.shape
    tr = _row_tile(t, 512)
    row = pl.BlockSpec((tr, d), lambda i: (i, 0))
    vec = pl.BlockSpec((1, d), lambda i: (0, 0))
    has_res = resid is not None

    def body(x_ref, w_ref, dy_ref, *rest):
        r_ref = rest[0] if has_res else None
        dx_ref, dw_ref = rest[has_res:has_res + 2]
        xv = x_ref[...].astype(F32)
        dyv = dy_ref[...].astype(F32)
        r = lax.rsqrt(jnp.mean(xv * xv, axis=-1, keepdims=True) + NORM_EPS)
        xhat = xv * r
        dyw = dyv * w_ref[...]
        dx = r * (dyw - xhat * jnp.mean(dyw * xhat, axis=-1, keepdims=True))
        if has_res:
            dx = dx + r_ref[...]
        dx_ref[...] = dx.astype(dx_ref.dtype)

        sums = [(dw_ref, dyv * xhat)] + ([(rest[-1], dx)] if dx_col_sum else [])

        @pl.when(pl.program_id(0) == 0)
        def _():
            for acc_ref, _ in sums:
                acc_ref[...] = jnp.zeros_like(acc_ref)

        for acc_ref, rows in sums:
            acc_ref[...] += jnp.sum(rows, axis=0, keepdims=True)

    ins = (x, w, dy) + ((resid,) if has_res else ())
    in_specs = [row, vec, row] + ([row] if has_res else [])
    n_vec = 2 if dx_col_sum else 1
    return pl.pallas_call(
        body, grid=(t // tr,), in_specs=in_specs, out_specs=[row] + [vec] * n_vec,
        out_shape=[jax.ShapeDtypeStruct((t, d), out_dtype)] + [jax.ShapeDtypeStruct((1, d), F32)] * n_vec,
        compiler_params=_params("arbitrary"), name=name)(*ins)


def _col_sum(x, *, name):
    t, n = x.shape
    tr = _row_tile(t, 512)

    def body(x_ref, o_ref):
        @pl.when(pl.program_id(0) == 0)
        def _():
            o_ref[...] = jnp.zeros_like(o_ref)

        o_ref[...] += jnp.sum(x_ref[...].astype(F32), axis=0, keepdims=True)

    return pl.pallas_call(
        body, grid=(t // tr,), in_specs=[pl.BlockSpec((tr, n), lambda i: (i, 0))],
        out_specs=pl.BlockSpec((1, n), lambda i: (0, 0)), out_shape=jax.ShapeDtypeStruct((1, n), F32),
        compiler_params=_params("arbitrary"), name=name)(x)


SSD_IN_SHARD = SSD_IN_DIM // N_CHIPS


def _w_in_from_shards(shards, *, name):
    d = shards.shape[1]
    tr = 256

    def body(s_ref, o_ref):
        o_ref[:, pl.ds(SSD_DT_COL, SSD_IN_PAD - SSD_DT_COL)] = jnp.zeros((tr, SSD_IN_PAD - SSD_DT_COL), o_ref.dtype)
        for s in range(N_CHIPS):
            o_ref[:, pl.ds(SSD_IN_SHARD * s, SSD_IN_SHARD)] = s_ref[s]

    return pl.pallas_call(
        body, grid=(d // tr,), in_specs=[pl.BlockSpec((N_CHIPS, tr, SSD_IN_SHARD), lambda i: (0, i, 0))],
        out_specs=pl.BlockSpec((tr, SSD_IN_PAD), lambda i: (i, 0)),
        out_shape=jax.ShapeDtypeStruct((d, SSD_IN_PAD), shards.dtype),
        compiler_params=_params("parallel"), name=name)(shards)


def _w_in_to_shards(g, *, name):
    d = g.shape[0]
    tr = 256

    def body(g_ref, o_ref):
        for s in range(N_CHIPS):
            o_ref[s] = g_ref[:, pl.ds(SSD_IN_SHARD * s, SSD_IN_SHARD)].astype(o_ref.dtype)

    return pl.pallas_call(
        body, grid=(d // tr,), in_specs=[pl.BlockSpec((tr, SSD_IN_PAD), lambda i: (i, 0))],
        out_specs=pl.BlockSpec((N_CHIPS, tr, SSD_IN_SHARD), lambda i: (0, i, 0)),
        out_shape=jax.ShapeDtypeStruct((N_CHIPS, d, SSD_IN_SHARD), BF16),
        compiler_params=_params("parallel"), name=name)(g)


XBC_COL0 = SSD_D_INNER // LANES


def _shift_down(v, k, row_ids):
    return jnp.where(row_ids >= k, pltpu.roll(v, k, axis=0), 0.0)


def _shift_up(v, k, row_ids):
    n = v.shape[0]
    return jnp.where(row_ids < n - k, pltpu.roll(v, n - k, axis=0), 0.0)


def _conv_pre(x, w, b, row_ids):
    pre = b + w[3:4, :] * x
    for k in (1, 2, 3):
        pre = pre + w[3 - k:4 - k, :] * _shift_down(x, k, row_ids)
    return pre


def _conv_fwd(zx, conv_w, conv_b, *, name, hook=None):
    t = zx.shape[0]
    nct = SSD_CONV_DIM // LANES
    hk = _HookSlots(hook, n_in=3, n_out=1, n_scratch=0)

    def body(*refs):
        (x_ref, w_ref, b_ref), (o_ref,), _ = hk.own(refs)
        if hook is not None:
            hk.run(refs, pl.program_id(0), nct)
        x = x_ref[...].astype(F32)
        row_ids = lax.broadcasted_iota(jnp.int32, x.shape, 0)
        pre = _conv_pre(x, w_ref[...], b_ref[...], row_ids)
        o_ref[...] = pre * _sigmoid(pre)

    outs = pl.pallas_call(
        body, grid=(nct,),
        in_specs=[pl.BlockSpec((t, LANES), lambda j: (0, XBC_COL0 + j)),
                  pl.BlockSpec((SSD_CONV_WIDTH, LANES), lambda j: (0, j)),
                  pl.BlockSpec((1, LANES), lambda j: (0, j))] + hk.in_specs,
        out_specs=[pl.BlockSpec((t, LANES), lambda j: (0, j))] + hk.out_specs,
        out_shape=[jax.ShapeDtypeStruct((t, SSD_CONV_DIM), F32)] + hk.out_shape,
        scratch_shapes=hk.scratch,
        compiler_params=_params(*hk.semantics("parallel")), name=name)(zx, conv_w, conv_b, *hk.inputs)
    return outs[0] if hook is None else (outs[0], outs[1:])


def _conv_bwd(zx, conv_w, conv_b, d_xs, d_bm, d_cm, dzx, *, name):
    t = zx.shape[0]
    nct = SSD_CONV_DIM // LANES
    n_xs = SSD_D_INNER // LANES
    n_bm = SSD_N_GROUPS * SSD_D_STATE // LANES

    def body(x_ref, w_ref, b_ref, dxs_ref, dbm_ref, dcm_ref, _, dx_ref, dw_ref, db_ref):
        x = x_ref[...].astype(F32)
        w = w_ref[...]
        j = pl.program_id(0)
        dy = jnp.where(j < n_xs, dxs_ref[...], jnp.where(j < n_xs + n_bm, dbm_ref[...], dcm_ref[...]))
        row_ids = lax.broadcasted_iota(jnp.int32, x.shape, 0)
        pre = _conv_pre(x, w, b_ref[...], row_ids)
        sg = _sigmoid(pre)
        dpre = dy * (sg * (1.0 + pre * (1.0 - sg)))
        dx = w[3:4, :] * dpre
        for k in (1, 2, 3):
            dx = dx + w[3 - k:4 - k, :] * _shift_up(dpre, k, row_ids)
        dx_ref[...] = dx.astype(dx_ref.dtype)
        db_ref[...] = jnp.sum(dpre, axis=0, keepdims=True)
        dw_ref[3:4, :] = jnp.sum(dpre * x, axis=0, keepdims=True)
        for k in (1, 2, 3):
            dw_ref[3 - k:4 - k, :] = jnp.sum(dpre * _shift_down(x, k, row_ids), axis=0, keepdims=True)

    clip = lambda j, lo, n: jnp.clip(j - lo, 0, n - 1)
    return pl.pallas_call(
        body, grid=(nct,),
        in_specs=[pl.BlockSpec((t, LANES), lambda j: (0, XBC_COL0 + j)),
                  pl.BlockSpec((SSD_CONV_WIDTH, LANES), lambda j: (0, j)),
                  pl.BlockSpec((1, LANES), lambda j: (0, j)),
                  pl.BlockSpec((t, LANES), lambda j: (0, clip(j, 0, n_xs))),
                  pl.BlockSpec((t, LANES), lambda j: (0, clip(j, n_xs, n_bm))),
                  pl.BlockSpec((t, LANES), lambda j: (0, clip(j, n_xs + n_bm, n_bm))), ANY],
        out_specs=[pl.BlockSpec((t, LANES), lambda j: (0, XBC_COL0 + j)),
                   pl.BlockSpec((SSD_CONV_WIDTH, LANES), lambda j: (0, j)), pl.BlockSpec((1, LANES), lambda j: (0, j))],
        out_shape=[jax.ShapeDtypeStruct(dzx.shape, dzx.dtype),
                   jax.ShapeDtypeStruct((SSD_CONV_WIDTH, SSD_CONV_DIM), F32),
                   jax.ShapeDtypeStruct((1, SSD_CONV_DIM), F32)],
        input_output_aliases={6: 0},
        compiler_params=_params("parallel"), name=name)(zx, conv_w, conv_b, d_xs, d_bm, d_cm, dzx)


def _softplus_fwd(dt_raw, bias_row, alog_row, *, name):
    t = dt_raw.shape[0]
    q = SSD_CHUNK
    tr = _row_tile(t, 1024)

    def body(x_ref, b_ref, al_ref, dt_ref, cum_ref):
        v = x_ref[...] + b_ref[...]
        e = jnp.exp(-jnp.abs(v))
        u = 1.0 + e
        log1p = jnp.where(u == 1.0, e, jnp.log(u) * (e / (u - 1.0)))
        dt = jnp.maximum(v, 0.0) + log1p
        a = dt * -jnp.exp(al_ref[...])
        lower = (lax.broadcasted_iota(jnp.int32, (q, q), 1) <= lax.broadcasted_iota(jnp.int32, (q, q), 0)).astype(F32)
        cums = [lax.dot_general(lower, a[c * q:(c + 1) * q, :], ((((1,), (0,))), ((), ())), precision=lax.Precision.HIGHEST,
                                preferred_element_type=F32) for c in range(tr // q)]
        dt_t, cum_t = dt.T, jnp.concatenate(cums, axis=0).T
        for g in range(SSD_N_GROUPS):
            rows = slice(g * SSD_HPG, (g + 1) * SSD_HPG)
            dt_ref[g] = dt_t[rows, :]
            cum_ref[g] = cum_t[rows, :]

    vec = pl.BlockSpec((1, LANES), lambda i: (0, 0))
    by_group = pl.BlockSpec((SSD_N_GROUPS, SSD_HPG, tr), lambda i: (0, 0, i))
    return pl.pallas_call(
        body, grid=(t // tr,),
        in_specs=[pl.BlockSpec((tr, LANES), lambda i: (i, 0)), vec, vec],
        out_specs=[by_group, by_group],
        out_shape=[jax.ShapeDtypeStruct((SSD_N_GROUPS, SSD_HPG, t), F32)] * 2,
        compiler_params=_params("parallel"), name=name)(dt_raw, bias_row, alog_row)


def _softplus_bwd(dt_raw, bias_row, ddt_rows, dzx, *, name):
    t = dt_raw.shape[0]
    tr = _row_tile(t, 1024)
    tail = SSD_IN_PAD - SSD_DT_COL

    def body(x_ref, b_ref, g_ref, _, o_ref, db_ref):
        v = x_ref[...] + b_ref[...]
        lane = lax.broadcasted_iota(jnp.int32, v.shape, 1)
        by_head = jnp.concatenate([g_ref[g] for g in range(SSD_N_GROUPS)]
                                  + [jnp.zeros((LANES - SSD_N_HEADS, tr), F32)], axis=0)
        d = jnp.where(lane < SSD_N_HEADS, by_head.T * _sigmoid(v), 0.0)
        o_ref[:, pl.ds(0, LANES)] = d.astype(o_ref.dtype)
        o_ref[:, pl.ds(LANES, tail - LANES)] = jnp.zeros((tr, tail - LANES), o_ref.dtype)

        @pl.when(pl.program_id(0) == 0)
        def _():
            db_ref[...] = jnp.zeros_like(db_ref)

        db_ref[...] += jnp.sum(d, axis=0, keepdims=True)

    return pl.pallas_call(
        body, grid=(t // tr,),
        in_specs=[pl.BlockSpec((tr, LANES), lambda i: (i, 0)), pl.BlockSpec((1, LANES), lambda i: (0, 0)),
                  pl.BlockSpec((SSD_N_GROUPS, SSD_HPG, tr), lambda i: (0, 0, i)), ANY],
        out_specs=[pl.BlockSpec((tr, tail), lambda i: (i, SSD_DT_COL // tail)), pl.BlockSpec((1, LANES), lambda i: (0, 0))],
        out_shape=[jax.ShapeDtypeStruct(dzx.shape, dzx.dtype), jax.ShapeDtypeStruct((1, LANES), F32)],
        input_output_aliases={3: 0},
        compiler_params=_params("arbitrary"), name=name)(dt_raw, bias_row, ddt_rows, dzx)


def _ssd_masks():
    q = SSD_CHUNK
    tt = lax.broadcasted_iota(jnp.int32, (q, q), 0)
    ss = lax.broadcasted_iota(jnp.int32, (q, q), 1)
    lane = lax.broadcasted_iota(jnp.int32, (1, SSD_GW), 1)
    srow = lax.broadcasted_iota(jnp.int32, (SSD_GW, 1), 0)
    hm = [(lane >= SSD_HEAD_DIM * j) & (lane < SSD_HEAD_DIM * (j + 1)) for j in range(SSD_HPG)]
    rm = [(srow >= SSD_HEAD_DIM * j) & (srow < SSD_HEAD_DIM * (j + 1)) for j in range(SSD_HPG)]
    return tt, ss, hm, rm


def _ssd_head_terms(dt_rows, cum_rows, a_rows, j, tt, ss):
    q = SSD_CHUNK
    dt_row = dt_rows[j:j + 1, :]
    dt_col = jnp.sum(jnp.where(tt == ss, dt_row, 0.0), axis=1, keepdims=True)
    a_row1 = a_rows[j:j + 1, :]
    a_11 = a_rows[j:j + 1, 0:1]
    cum_col = jnp.sum(jnp.where(ss <= tt, dt_row * a_row1, 0.0), axis=1, keepdims=True)
    cum_row = cum_rows[j:j + 1, :]
    decay = jnp.exp(jnp.where(ss <= tt, cum_col - cum_row, -jnp.inf))
    cum_last = cum_col[q - 1:q, :]
    e_col = jnp.exp(cum_col)
    dte_col = jnp.exp(cum_last - cum_col)
    e_last = jnp.exp(cum_last)
    return dt_col, dt_row, a_row1, a_11, decay, e_col, dte_col, e_last


SSD_CHUNKS_PER_STEP = 8
SSD_BC_COL0 = SSD_D_INNER // SSD_D_STATE


def _ssd_head_selects(terms, hm, rm):
    e_all = jnp.zeros((SSD_CHUNK, SSD_GW), F32)
    w_all = jnp.zeros((SSD_CHUNK, SSD_GW), F32)
    e_s = jnp.zeros((SSD_GW, 1), F32)
    for j in range(SSD_HPG):
        dt_col, _, _, _, _, e_col, dte_col, e_last = terms[j]
        e_all = jnp.where(hm[j], e_col, e_all)
        w_all = jnp.where(hm[j], dt_col * dte_col, w_all)
        e_s = jnp.where(rm[j], e_last, e_s)
    return e_all, w_all, e_s


def _ssd_fwd(xc, dtr, cumr, alog_b, d_b, *, name, hook=None):
    t = xc.shape[0]
    q = SSD_CHUNK
    nc = t // q
    kc = min(SSD_CHUNKS_PER_STEP, nc)
    rows = kc * q
    hk = _HookSlots(hook, n_in=7, n_out=2, n_scratch=1)

    def body(*refs):
        (x_ref, b_ref, c_ref, dtr_ref, cumr_ref, alog_ref, d_ref), (y_ref, st_ref), (s_scr,) = hk.own(refs)
        if hook is not None:
            hk.run(refs, pl.program_id(0) * (nc // kc) + pl.program_id(1), SSD_N_GROUPS * (nc // kc))

        @pl.when(pl.program_id(1) == 0)
        def _():
            s_scr[...] = jnp.zeros_like(s_scr)

        tt, ss, hm, rm = _ssd_masks()
        a_rows = -jnp.exp(alog_ref[...])
        d_rows = d_ref[...]
        d_all = jnp.zeros((1, SSD_GW), F32)
        for j in range(SSD_HPG):
            d_all = jnp.where(hm[j], d_rows[j:j + 1, 0:1], d_all)
        ks, hs = range(kc), range(SSD_HPG)
        sl = [pl.ds(k * q, q) for k in ks]
        x = [x_ref[sl[k], :] for k in ks]
        bm = [b_ref[sl[k], :].astype(BF16) for k in ks]
        cm = [c_ref[sl[k], :].astype(BF16) for k in ks]
        xb = [x[k].astype(BF16) for k in ks]
        terms = [[_ssd_head_terms(dtr_ref[:, sl[k]], cumr_ref[:, sl[k]], a_rows, j, tt, ss) for j in hs] for k in ks]
        g = [_dot_nt(cm[k], bm[k]) for k in ks]
        m = [[(g[k] * terms[k][j][4] * terms[k][j][1]).astype(BF16) for j in hs] for k in ks]
        yj = [[_dot_nn(m[k][j], xb[k]) for j in hs] for k in ks]
        sel = [_ssd_head_selects(terms[k], hm, rm) for k in ks]
        upd = [_dot_tn((x[k] * sel[k][1]).astype(BF16), bm[k]) for k in ks]
        states = [s_scr[...]]
        for k in ks:
            states.append(states[k] * sel[k][2] + upd[k])
        inter = [_dot_nt(cm[k], states[k].astype(BF16)) for k in ks]
        ys = []
        for k in ks:
            y = jnp.zeros((q, SSD_GW), F32)
            for j in hs:
                y = jnp.where(hm[j], yj[k][j], y)
            ys.append(y + inter[k] * sel[k][0] + x[k] * d_all)
        for k in ks:
            st_ref[k] = states[k]
        y_ref[...] = jnp.concatenate(ys, axis=0).astype(y_ref.dtype)
        s_scr[...] = states[kc]

    blk = lambda width, off: pl.BlockSpec((rows, width), lambda g, c: (c, off + g))
    par_s = pl.BlockSpec((None, SSD_HPG, LANES), lambda g, c: (g, 0, 0))
    row_s = pl.BlockSpec((None, SSD_HPG, rows), lambda g, c: (g, 0, c))
    outs = pl.pallas_call(
        body, grid=(SSD_N_GROUPS, nc // kc),
        in_specs=[blk(SSD_GW, 0), blk(SSD_D_STATE, SSD_BC_COL0), blk(SSD_D_STATE, SSD_BC_COL0 + SSD_N_GROUPS),
                  row_s, row_s, par_s, par_s] + hk.in_specs,
        out_specs=[blk(SSD_GW, 0), pl.BlockSpec((None, kc, SSD_GW, SSD_D_STATE), lambda g, c: (g, c, 0, 0))] + hk.out_specs,
        out_shape=[jax.ShapeDtypeStruct((t, SSD_D_INNER), BF16),
                   jax.ShapeDtypeStruct((SSD_N_GROUPS, nc, SSD_GW, SSD_D_STATE), F32)] + hk.out_shape,
        scratch_shapes=[pltpu.VMEM((SSD_GW, SSD_D_STATE), F32)] + hk.scratch,
        compiler_params=_params(*hk.semantics("parallel", "arbitrary")), name=name)(
            xc, xc, xc, dtr, cumr, alog_b, d_b, *hk.inputs)
    return outs if hook is None else (outs[:2], outs[2:])


def _ssd_bwd(xc, dtr, cumr, alog_b, d_b, states, dy, *, name, hook=None):
    t = xc.shape[0]
    q = SSD_CHUNK
    nc = t // q
    kc = min(SSD_CHUNKS_PER_STEP, nc)
    nst = nc // kc
    rows = kc * q
    rev = lambda c: nst - 1 - c
    hk = _HookSlots(hook, n_in=9, n_out=5, n_scratch=1)

    def body(*refs):
        ((x_ref, b_ref, c_ref, dtr_ref, cumr_ref, alog_ref, d_ref, st_ref, dy_ref),
         (dx_ref, db_ref, dc_ref, ddt_ref, dpar_ref), (ds_scr,)) = hk.own(refs)
        if hook is not None:
            hk.run(refs, pl.program_id(0) * nst + pl.program_id(1), SSD_N_GROUPS * nst)

        @pl.when(pl.program_id(1) == 0)
        def _():
            ds_scr[...] = jnp.zeros_like(ds_scr)
            dpar_ref[...] = jnp.zeros_like(dpar_ref)

        tt, ss, hm, rm = _ssd_masks()
        tcol = lax.broadcasted_iota(jnp.int32, (q, 1), 0)
        lane = lax.broadcasted_iota(jnp.int32, (1, LANES), 1)
        a_rows = -jnp.exp(alog_ref[...])
        d_rows = d_ref[...]
        d_all = jnp.zeros((1, SSD_GW), F32)
        for j in range(SSD_HPG):
            d_all = jnp.where(hm[j], d_rows[j:j + 1, 0:1], d_all)
        ks, hs = range(kc), range(SSD_HPG)
        sl = [pl.ds(k * q, q) for k in ks]
        x = [x_ref[sl[k], :] for k in ks]
        dyv = [dy_ref[sl[k], :].astype(F32) for k in ks]
        bm = [b_ref[sl[k], :].astype(BF16) for k in ks]
        cm = [c_ref[sl[k], :].astype(BF16) for k in ks]
        s_in = [st_ref[k] for k in ks]
        xb = [x[k].astype(BF16) for k in ks]
        dyb = [dyv[k].astype(BF16) for k in ks]
        s_b = [s_in[k].astype(BF16) for k in ks]
        terms = [[_ssd_head_terms(dtr_ref[:, sl[k]], cumr_ref[:, sl[k]], a_rows, j, tt, ss) for j in hs] for k in ks]
        sel = [_ssd_head_selects(terms[k], hm, rm) for k in ks]
        e_all, w_all, e_s = [s_[0] for s_ in sel], [s_[1] for s_ in sel], [s_[2] for s_ in sel]
        dye = [(dyv[k] * e_all[k]).astype(BF16) for k in ks]
        ds_loc = [_dot_tn(dye[k], cm[k]) for k in ks]
        ds = [None] * kc
        running = ds_scr[...]
        for k in reversed(ks):
            ds[k] = running
            running = running * e_s[k] + ds_loc[k]
        ds_scr[...] = running
        ds_b = [ds[k].astype(BF16) for k in ks]
        g = [_dot_nt(cm[k], bm[k]) for k in ks]
        cs = [_dot_nt(cm[k], s_b[k]) for k in ks]
        bds = [_dot_nt(bm[k], ds_b[k]) for k in ks]
        dm = [[_dot_nt(jnp.where(hm[j], dyv[k], 0.0).astype(BF16), xb[k]) for j in hs] for k in ks]
        gl = [[g[k] * terms[k][j][4] for j in hs] for k in ks]
        wp = [[dm[k][j] * gl[k][j] for j in hs] for k in ks]
        mt = [[(gl[k][j] * terms[k][j][1]).astype(BF16) for j in hs] for k in ks]
        dxj = [[_dot_tn(mt[k][j], dyb[k]) for j in hs] for k in ks]
        dg = []
        for k in ks:
            acc = jnp.zeros((q, q), F32)
            for j in hs:
                acc = acc + dm[k][j] * terms[k][j][4] * terms[k][j][1]
            dg.append(acc.astype(BF16))
        dy_cs = [dyv[k] * cs[k] for k in ks]
        x_bds = [x[k] * bds[k] for k in ks]
        dy_x = [dyv[k] * x[k] for k in ks]
        ds_s = [ds[k] * s_in[k] for k in ks]
        w = [[wp[k][j] * terms[k][j][1] for j in hs] for k in ks]
        rw_col = [[jnp.sum(w[k][j], axis=1, keepdims=True) for j in hs] for k in ks]
        cw_row = [[jnp.sum(w[k][j], axis=0, keepdims=True) for j in hs] for k in ks]
        cwp_row = [[jnp.sum(wp[k][j], axis=0, keepdims=True) for j in hs] for k in ks]
        r1_col = [[jnp.sum(jnp.where(hm[j], dy_cs[k], 0.0), axis=1, keepdims=True) * terms[k][j][5] for j in hs] for k in ks]
        dw_col = [[jnp.sum(jnp.where(hm[j], x_bds[k], 0.0), axis=1, keepdims=True) for j in hs] for k in ks]
        head_rows = [slice(j * SSD_HEAD_DIM, (j + 1) * SSD_HEAD_DIM) for j in hs]
        lane_sum = lambda v: jnp.sum(v, axis=1, keepdims=True)
        s_sum = [[lane_sum(jnp.sum(ds_s[k][head_rows[j], :], axis=0, keepdims=True)) for j in hs] for k in ks]
        dy_x_cols = [jnp.sum(dy_x[k], axis=0, keepdims=True) for k in ks]
        d_d = [[lane_sum(jnp.where(hm[j], dy_x_cols[k], 0.0)) for j in hs] for k in ks]
        ddt_rows = [[None] * SSD_HPG for _ in ks]
        dpar = [jnp.zeros((1, LANES), F32) for _ in hs]
        for k in ks:
            for j in hs:
                dt_col, dt_row, a_row1, a_11, _, _, dte_col, e_last = terms[k][j]
                dww = dw_col[k][j] * (dt_col * dte_col)
                last_add = jnp.sum(dww, axis=0, keepdims=True) + e_last * s_sum[k][j]
                dcum_col = rw_col[k][j] + r1_col[k][j] - dww + jnp.where(tcol == q - 1, last_add, 0.0)
                da_row = jnp.sum(jnp.where(tt >= ss, dcum_col, 0.0), axis=0, keepdims=True)
                da_col = jnp.sum(jnp.where(ss >= tt, -cw_row[k][j], 0.0), axis=1, keepdims=True)
                ddt_col = a_11 * da_col + dw_col[k][j] * dte_col
                ddt_rows[k][j] = (a_row1 * da_row + cwp_row[k][j]
                                  + jnp.sum(jnp.where(tt == ss, ddt_col, 0.0), axis=0, keepdims=True))
                d_a = jnp.sum(dt_row * da_row, axis=1, keepdims=True) + jnp.sum(dt_col * da_col, axis=0, keepdims=True)
                dpar[j] = dpar[j] + jnp.where(lane == 0, d_a * a_11, 0.0) + jnp.where(lane == 1, d_d[k][j], 0.0)
        dxs = []
        for k in ks:
            acc = jnp.zeros((q, SSD_GW), F32)
            for j in hs:
                acc = jnp.where(hm[j], dxj[k][j], acc)
            dxs.append(acc + w_all[k] * bds[k] + d_all * dyv[k])
        xw = [(x[k] * w_all[k]).astype(BF16) for k in ks]
        dc = [_dot_nn(dg[k], bm[k]) + _dot_nn(dye[k], s_b[k]) for k in ks]
        db = [_dot_tn(dg[k], cm[k]) + _dot_nn(xw[k], ds_b[k]) for k in ks]
        dx_ref[...] = jnp.concatenate(dxs, axis=0)
        dc_ref[...] = jnp.concatenate(dc, axis=0)
        db_ref[...] = jnp.concatenate(db, axis=0)
        ddt_ref[...] = jnp.concatenate([jnp.concatenate([ddt_rows[k][j] for k in ks], axis=1) for j in hs], axis=0)
        dpar_ref[...] += jnp.concatenate(dpar, axis=0)

    blk = lambda width, off: pl.BlockSpec((rows, width), lambda g, c: (rev(c), off + g))
    par_s = pl.BlockSpec((None, SSD_HPG, LANES), lambda g, c: (g, 0, 0))
    outs = pl.pallas_call(
        body, grid=(SSD_N_GROUPS, nst),
        in_specs=[blk(SSD_GW, 0), blk(SSD_D_STATE, SSD_BC_COL0), blk(SSD_D_STATE, SSD_BC_COL0 + SSD_N_GROUPS),
                  pl.BlockSpec((None, SSD_HPG, rows), lambda g, c: (g, 0, rev(c))),
                  pl.BlockSpec((None, SSD_HPG, rows), lambda g, c: (g, 0, rev(c))), par_s, par_s,
                  pl.BlockSpec((None, kc, SSD_GW, SSD_D_STATE), lambda g, c: (g, rev(c), 0, 0)), blk(SSD_GW, 0)] + hk.in_specs,
        out_specs=[blk(SSD_GW, 0), blk(SSD_D_STATE, 0), blk(SSD_D_STATE, 0),
                   pl.BlockSpec((None, SSD_HPG, rows), lambda g, c: (g, 0, rev(c))), par_s] + hk.out_specs,
        out_shape=[jax.ShapeDtypeStruct((t, SSD_D_INNER), F32),
                   jax.ShapeDtypeStruct((t, SSD_N_GROUPS * SSD_D_STATE), F32),
                   jax.ShapeDtypeStruct((t, SSD_N_GROUPS * SSD_D_STATE), F32),
                   jax.ShapeDtypeStruct((SSD_N_GROUPS, SSD_HPG, t), F32),
                   jax.ShapeDtypeStruct((SSD_N_GROUPS, SSD_HPG, LANES), F32)] + hk.out_shape,
        scratch_shapes=[pltpu.VMEM((SSD_GW, SSD_D_STATE), F32)] + hk.scratch,
        compiler_params=_params(*hk.semantics("parallel", "arbitrary")), name=name)(
            xc, xc, xc, dtr, cumr, alog_b, d_b, states, dy, *hk.inputs)
    return outs if hook is None else (outs[:5], outs[5:])


def _gate_norm_fwd(y, zx, norm_w, *, name):
    t = y.shape[0]
    tr = _row_tile(t, 256)
    row = pl.BlockSpec((tr, SSD_D_INNER), lambda i: (i, 0))

    def body(y_ref, z_ref, w_ref, o_ref):
        for gi in range(SSD_N_GROUPS):
            sl = pl.ds(gi * SSD_GW, SSD_GW)
            z = z_ref[:, sl].astype(F32)
            gv = y_ref[:, sl].astype(F32) * (z * _sigmoid(z))
            r = lax.rsqrt(jnp.mean(gv * gv, axis=-1, keepdims=True) + NORM_EPS)
            o_ref[:, sl] = (gv * r * w_ref[:, sl]).astype(BF16)

    return pl.pallas_call(
        body, grid=(t // tr,), in_specs=[row, row, pl.BlockSpec((1, SSD_D_INNER), lambda i: (0, 0))],
        out_specs=row, out_shape=jax.ShapeDtypeStruct((t, SSD_D_INNER), BF16),
        compiler_params=_params("parallel"), name=name)(y, zx, norm_w)


def _gate_norm_bwd(y, zx, norm_w, dyn, *, name):
    t = y.shape[0]
    tr = _row_tile(t, 256)
    row = pl.BlockSpec((tr, SSD_D_INNER), lambda i: (i, 0))
    vec = pl.BlockSpec((1, SSD_D_INNER), lambda i: (0, 0))

    def body(y_ref, z_ref, w_ref, dyn_ref, dy_ref, dz_ref, dw_ref):
        @pl.when(pl.program_id(0) == 0)
        def _():
            dw_ref[...] = jnp.zeros_like(dw_ref)

        for gi in range(SSD_N_GROUPS):
            sl = pl.ds(gi * SSD_GW, SSD_GW)
            z = z_ref[:, sl].astype(F32)
            yv = y_ref[:, sl].astype(F32)
            sg = _sigmoid(z)
            sz = z * sg
            gv = yv * sz
            r = lax.rsqrt(jnp.mean(gv * gv, axis=-1, keepdims=True) + NORM_EPS)
            ghat = gv * r
            dout = dyn_ref[:, sl].astype(F32)
            dgh = dout * w_ref[:, sl]
            dgv = r * (dgh - ghat * jnp.mean(dgh * ghat, axis=-1, keepdims=True))
            dy_ref[:, sl] = (dgv * sz).astype(dy_ref.dtype)
            dz_ref[:, sl] = (dgv * yv * (sg * (1.0 + z * (1.0 - sg)))).astype(dz_ref.dtype)
            dw_ref[:, sl] += jnp.sum(dout * ghat, axis=0, keepdims=True)

    return pl.pallas_call(
        body, grid=(t // tr,), in_specs=[row, row, vec, row], out_specs=[row, row, vec],
        out_shape=[jax.ShapeDtypeStruct((t, SSD_D_INNER), BF16), jax.ShapeDtypeStruct((t, SSD_IN_PAD), BF16),
                   jax.ShapeDtypeStruct((1, SSD_D_INNER), F32)],
        compiler_params=_params("arbitrary"), name=name)(y, zx, norm_w, dyn)


ATTN_KV_W = ATTN_N_KV * ATTN_HEAD_DIM
ATTN_Q_HALF = 512
ATTN_K_BLK = ATTN_N_Q * ATTN_HEAD_DIM // ATTN_KV_W
ATTN_V_BLK = ATTN_K_BLK + 1


def _attn_valid(first_block):
    w = ATTN_WINDOW
    qpos = lax.broadcasted_iota(jnp.int32, (w, 2 * w), 0) + w
    kpos = lax.broadcasted_iota(jnp.int32, (w, 2 * w), 1)
    rel = qpos - kpos
    return (rel >= 0) & (rel < w) & jnp.logical_not(first_block & (kpos < w))


def _attn_head_views(lo_ref, hi_ref):
    hd = ATTN_HEAD_DIM
    per_half = ATTN_Q_HALF // hd
    return [(lo_ref if h < per_half else hi_ref)[:, pl.ds((h % per_half) * hd, hd)] for h in range(ATTN_N_Q)]


def _attn_block_views(lo_ref, hi_ref, kc_ref, kp_ref, vc_ref, vp_ref):
    hd = ATTN_HEAD_DIM
    kv_cols = [pl.ds(kh * hd, hd) for kh in range(ATTN_N_KV)]
    kb = [jnp.concatenate([kp_ref[:, c], kc_ref[:, c]], axis=0) for c in kv_cols]
    vb = [jnp.concatenate([vp_ref[:, c], vc_ref[:, c]], axis=0) for c in kv_cols]
    return _attn_head_views(lo_ref, hi_ref), kb, vb


def _attn_scores(q, kb, valid):
    scale = ATTN_HEAD_DIM ** -0.5
    return [jnp.where(valid, _dot_nt(q[h], kb[h // ATTN_REP]) * scale, -jnp.inf) for h in range(ATTN_N_Q)]


def _attn_softmax(s, sink):
    heads = range(ATTN_N_Q)
    m = [jnp.maximum(jnp.max(s[h], axis=1, keepdims=True), sink[h]) for h in heads]
    e = [jnp.exp(s[h] - m[h]) for h in heads]
    es = [jnp.exp(sink[h] - m[h]) for h in heads]
    inv = [1.0 / (jnp.sum(e[h], axis=1, keepdims=True) + es[h]) for h in heads]
    return e, es, inv


def _attn_fwd(qkv, sinks_b, *, name, hook=None):
    t = qkv.shape[0]
    w = ATTN_WINDOW
    nb = t // w
    prev = lambda n: jnp.maximum(n - 1, 0)
    hk = _HookSlots(hook, n_in=7, n_out=1, n_scratch=0)

    def body(*refs):
        (qlo_ref, qhi_ref, kc_ref, kp_ref, vc_ref, vp_ref, sink_ref), (o_ref,), _ = hk.own(refs)
        if hook is not None:
            hk.run(refs, pl.program_id(0), nb)
        heads = range(ATTN_N_Q)
        q, kb, vb = _attn_block_views(qlo_ref, qhi_ref, kc_ref, kp_ref, vc_ref, vp_ref)
        sink = [sink_ref[h:h + 1, 0:1] for h in heads]
        e, _, inv = _attn_softmax(_attn_scores(q, kb, _attn_valid(pl.program_id(0) == 0)), sink)
        out = [_dot_nn((e[h] * inv[h]).astype(BF16), vb[h // ATTN_REP]).astype(o_ref.dtype) for h in heads]
        o_ref[...] = jnp.concatenate(out, axis=1)

    qh = lambda half: pl.BlockSpec((w, ATTN_Q_HALF), lambda n: (n, half))
    kv = lambda blk, idx: pl.BlockSpec((w, ATTN_KV_W), lambda n: (idx(n), blk))
    cur = lambda n: n
    outs = pl.pallas_call(
        body, grid=(nb,),
        in_specs=[qh(0), qh(1), kv(ATTN_K_BLK, cur), kv(ATTN_K_BLK, prev), kv(ATTN_V_BLK, cur), kv(ATTN_V_BLK, prev),
                  pl.BlockSpec((ATTN_N_Q, LANES), lambda n: (0, 0))] + hk.in_specs,
        out_specs=[pl.BlockSpec((w, D_MODEL), lambda n: (n, 0))] + hk.out_specs,
        out_shape=[jax.ShapeDtypeStruct((t, D_MODEL), BF16)] + hk.out_shape,
        scratch_shapes=hk.scratch,
        compiler_params=_params(*hk.semantics("parallel")), name=name)(qkv, qkv, qkv, qkv, qkv, qkv, sinks_b, *hk.inputs)
    return outs[0] if hook is None else (outs[0], outs[1:])


def _attn_bwd(qkv, sinks_b, dout, *, name):
    t = qkv.shape[0]
    w = ATTN_WINDOW
    nb = t // w
    hd = ATTN_HEAD_DIM
    clamp = lambda n: jnp.minimum(n, nb - 1)
    prev = lambda n: jnp.maximum(clamp(n) - 1, 0)

    def body(qlo_ref, qhi_ref, kc_ref, kp_ref, vc_ref, vp_ref, sink_ref, dolo_ref, dohi_ref,
             dq_ref, dkv_ref, dsink_ref, carry):
        n = pl.program_id(0)

        @pl.when(n == 0)
        def _():
            carry[...] = jnp.zeros_like(carry)
            dsink_ref[...] = jnp.zeros_like(dsink_ref)

        @pl.when(n < nb)
        def _():
            heads, kvs = range(ATTN_N_Q), range(ATTN_N_KV)
            q, kb, vb = _attn_block_views(qlo_ref, qhi_ref, kc_ref, kp_ref, vc_ref, vp_ref)
            do = _attn_head_views(dolo_ref, dohi_ref)
            sink = [sink_ref[h:h + 1, 0:1] for h in heads]
            s = _attn_scores(q, kb, _attn_valid(n == 0))
            dp = [_dot_nt(do[h], vb[h // ATTN_REP]) for h in heads]
            e, es, inv = _attn_softmax(s, sink)
            p = [e[h] * inv[h] for h in heads]
            delta = [jnp.sum(p[h] * dp[h], axis=1, keepdims=True) for h in heads]
            dsc = [(p[h] * (dp[h] - delta[h]) * (hd ** -0.5)).astype(BF16) for h in heads]
            pb = [p[h].astype(BF16) for h in heads]
            dq = [_dot_nn(dsc[h], kb[h // ATTN_REP]).astype(dq_ref.dtype) for h in heads]
            stack = lambda per_head, kh: jnp.concatenate(per_head[kh * ATTN_REP:(kh + 1) * ATTN_REP], axis=0)
            dkb = [_dot_tn(stack(dsc, kh), stack(q, kh)) for kh in kvs]
            dvb = [_dot_tn(stack(pb, kh), stack(do, kh)) for kh in kvs]
            dsink = [jnp.broadcast_to(jnp.sum(-es[h] * inv[h] * delta[h], axis=0, keepdims=True), (1, LANES)) for h in heads]
            dq_ref[...] = jnp.concatenate(dq, axis=1)
            dsink_ref[...] += jnp.concatenate(dsink, axis=0)
            dkv_ref[...] = (carry[...] + jnp.concatenate([d[0:w, :] for d in dkb + dvb], axis=1)).astype(dkv_ref.dtype)
            carry[...] = jnp.concatenate([d[w:2 * w, :] for d in dkb + dvb], axis=1)

        @pl.when(n == nb)
        def _():
            dkv_ref[...] = carry[...].astype(dkv_ref.dtype)

    qh = lambda half: pl.BlockSpec((w, ATTN_Q_HALF), lambda n: (clamp(n), half))
    kv = lambda blk, idx: pl.BlockSpec((w, ATTN_KV_W), lambda n: (idx(n), blk))
    return pl.pallas_call(
        body, grid=(nb + 1,),
        in_specs=[qh(0), qh(1), kv(ATTN_K_BLK, clamp), kv(ATTN_K_BLK, prev), kv(ATTN_V_BLK, clamp), kv(ATTN_V_BLK, prev),
                  pl.BlockSpec((ATTN_N_Q, LANES), lambda n: (0, 0)), qh(0), qh(1)],
        out_specs=[pl.BlockSpec((w, D_MODEL), lambda n: (clamp(n), 0)),
                   pl.BlockSpec((w, 2 * ATTN_KV_W), lambda n: (jnp.maximum(n - 1, 0), 0)),
                   pl.BlockSpec((ATTN_N_Q, LANES), lambda n: (0, 0))],
        out_shape=[jax.ShapeDtypeStruct((t, D_MODEL), BF16), jax.ShapeDtypeStruct((t, 2 * ATTN_KV_W), BF16),
                   jax.ShapeDtypeStruct((ATTN_N_Q, LANES), F32)],
        scratch_shapes=[pltpu.VMEM((w, 2 * ATTN_KV_W), F32)],
        compiler_params=_params("arbitrary"), name=name)(qkv, qkv, qkv, qkv, qkv, qkv, sinks_b, dout, dout)


def _sq_relu_epilogue(acc):
    r = jnp.maximum(acc, 0.0)
    return (r * r,)


def _sq_relu_bwd_epilogue(acc, act):
    return (acc * (2.0 * jnp.sqrt(act.astype(F32))),)


def _bias_epilogue(acc, bias):
    return (acc + bias,)


def _plain_run(stage, fn, *args, **kwargs):
    return fn(*args, **kwargs)


def _mlp_fwd(u, w_up, w_down, tag, run=_plain_run):
    act = run(f"mlp_up_{tag}", _matmul, u, w_up, mode="nn", out_dtypes=(BF16,), epilogue=_sq_relu_epilogue, b_shards=True,
              tm=BIG_TILE, name=f"mlp_up_{tag}")
    f = run(f"mlp_down_{tag}", _matmul, act, w_down, mode="nn", out_dtypes=(BF16,), tk=BIG_TILE, name=f"mlp_down_{tag}")
    return act, f


def _mlp_bwd(u, act, w_up, w_down, df, tag):
    dpre = _matmul(df, w_down, mode="nt", out_dtypes=(BF16,), epilogue=_sq_relu_bwd_epilogue,
                   extras=((act, "tile"),), name=f"mlp_dact_{tag}")
    dw_down = _matmul(act, df, mode="tn", out_dtypes=(BF16,), tk=BIG_TILE, name=f"mlp_dwdown_{tag}")
    du = _matmul(dpre, w_up, mode="nt", out_dtypes=(BF16,), b_shards=True, tm=BIG_TILE, name=f"mlp_du_{tag}")
    dw_up = _matmul(u, dpre, mode="tn", out_dtypes=(BF16,), out_shards=True, tk=BIG_TILE, name=f"mlp_dwup_{tag}")
    return du, dw_up, dw_down


def _head_param_rows(p):
    return jnp.broadcast_to(p.reshape(SSD_N_GROUPS, SSD_HPG, 1), (SSD_N_GROUPS, SSD_HPG, LANES))


def _local_step(x, target, wts, comm=None, u0=None):
    wts = dict(wts)
    row = lambda v: v.reshape(1, -1)
    mix_pre, mix_post, ffn_pre, ffn_post = wts["mix_pre_norm"], wts["mix_post_norm"], wts["ffn_pre_norm"], wts["ffn_post_norm"]

    def gathering(stage, fn, *args, **kwargs):
        hook = comm.gather_hook(stage) if comm is not None else None
        if hook is None:
            return fn(*args, **kwargs)
        out, got = fn(*args, hook=hook, **kwargs)
        wts.update(comm.weights_from(stage, got))
        return out

    if u0 is None:
        u0 = _rms_fwd(x, row(mix_pre[0]), name="rms_pre_mix0")
    zx, dt_raw = gathering("in_proj", _matmul, u0, wts["ssd_w_in"], mode="nn", out_dtypes=(BF16,), tn=SSD_IN_TILE,
                           f32_block=SSD_DT_COL - (SSD_IN_PAD - SSD_IN_TILE),
                           name="ssd_in_proj")
    xc = gathering("conv", _conv_fwd, zx, wts["ssd_conv_w"], row(wts["ssd_conv_b"]), name="ssd_conv_fwd")
    bias_row = jnp.pad(wts["ssd_dt_bias"], (0, LANES - SSD_N_HEADS)).reshape(1, LANES)
    alog_row = jnp.pad(wts["ssd_a_log"], (0, LANES - SSD_N_HEADS)).reshape(1, LANES)
    dtr, cumr = _softplus_fwd(dt_raw, bias_row, alog_row, name="ssd_dt_fwd")
    alog_b, d_b = _head_param_rows(wts["ssd_a_log"]), _head_param_rows(wts["ssd_d"])
    y_ssd, states = gathering("scan", _ssd_fwd, xc, dtr, cumr, alog_b, d_b, name="ssd_scan_fwd")
    norm_w = row(wts["ssd_norm_w"])
    yn = _gate_norm_fwd(y_ssd, zx, norm_w, name="ssd_gate_norm_fwd")
    mix0 = _matmul(yn, wts["ssd_w_out"], mode="nn", out_dtypes=(BF16,), tk=BIG_TILE, name="ssd_out_proj")
    h1, v0 = _rms_fwd(mix0, row(mix_post[0]), resid=x, want_u=row(ffn_pre[0]), name="rms_post_mix0")
    act0, f0 = _mlp_fwd(v0, wts["mlp_w_up0"], wts["mlp_w_down0"], "l0", run=gathering)
    h2, u1 = _rms_fwd(f0, row(ffn_post[0]), resid=h1, want_u=row(mix_pre[1]), name="rms_post_ffn0")

    qkv = _matmul(u1, wts["attn_w_qkv"], mode="nn", out_dtypes=(BF16,), epilogue=_bias_epilogue,
                  extras=((row(wts["attn_b_qkv"]), "row"),), b_shards=True, name="attn_qkv_proj")
    sinks_b = jnp.broadcast_to(wts["attn_sinks"].reshape(ATTN_N_Q, 1), (ATTN_N_Q, LANES))
    ao = gathering("attn_fwd", _attn_fwd, qkv, sinks_b, name="attn_fwd")
    mix1 = _matmul(ao, wts["attn_w_o"], mode="nn", out_dtypes=(BF16,), epilogue=_bias_epilogue,
                   extras=((row(wts["attn_b_o"]), "row"),), name="attn_out_proj")
    h3, v1 = _rms_fwd(mix1, row(mix_post[1]), resid=h2, want_u=row(ffn_pre[1]), name="rms_post_mix1")
    act1, f1 = _mlp_fwd(v1, wts["mlp_w_up1"], wts["mlp_w_down1"], "l1")
    dh4, loss_tile = _rms_fwd(f1, row(ffn_post[1]), resid=h3, target=target, name="rms_post_ffn1_loss")

    df1, g_ffn_post1 = _rms_bwd(f1, row(ffn_post[1]), dh4, out_dtype=BF16, name="rms_post_ffn1_bwd")
    dv1, g_up1, g_down1 = _mlp_bwd(v1, act1, wts["mlp_w_up1"], wts["mlp_w_down1"], df1, "l1")
    dh3, g_ffn_pre1 = _rms_bwd(h3, row(ffn_pre[1]), dv1, resid=dh4, name="rms_pre_ffn1_bwd")
    dmix1, g_mix_post1, g_b_o = _rms_bwd(mix1, row(mix_post[1]), dh3, out_dtype=BF16, dx_col_sum=True, name="rms_post_mix1_bwd")
    g_w_o = _matmul(ao, dmix1, mode="tn", out_dtypes=(BF16,), tk=BIG_TILE, name="attn_dwo")
    dao = _matmul(dmix1, wts["attn_w_o"], mode="nt", out_dtypes=(BF16,), name="attn_dao")
    dq, dkv, g_sinks = _attn_bwd(qkv, sinks_b, dao, name="attn_bwd")
    dqkv = jnp.concatenate([dq, dkv], axis=1)
    g_b_qkv = _col_sum(dqkv, name="attn_bqkv_grad")
    g_w_qkv = _matmul(u1, dqkv, mode="tn", out_dtypes=(BF16,), tn=ATTN_QKV // N_CHIPS, out_shards=True, tk=BIG_TILE, name="attn_dwqkv")
    du1 = _matmul(dqkv, wts["attn_w_qkv"], mode="nt", out_dtypes=(BF16,), b_shards=True, name="attn_du")
    dh2, g_mix_pre1 = _rms_bwd(h2, row(mix_pre[1]), du1, resid=dh3, name="rms_pre_mix1_bwd")

    df0, g_ffn_post0 = _rms_bwd(f0, row(ffn_post[0]), dh2, out_dtype=BF16, name="rms_post_ffn0_bwd")
    dv0, g_up0, g_down0 = _mlp_bwd(v0, act0, wts["mlp_w_up0"], wts["mlp_w_down0"], df0, "l0")
    dh1, g_ffn_pre0 = _rms_bwd(h1, row(ffn_pre[0]), dv0, resid=dh2, name="rms_pre_ffn0_bwd")
    dmix0, g_mix_post0 = _rms_bwd(mix0, row(mix_post[0]), dh1, out_dtype=BF16, name="rms_post_mix0_bwd")
    g_w_out = _matmul(yn, dmix0, mode="tn", out_dtypes=(BF16,), tk=BIG_TILE, name="ssd_dwout")
    dyn = _matmul(dmix0, wts["ssd_w_out"], mode="nt", out_dtypes=(BF16,), name="ssd_dyn")
    dy_ssd, dzx, g_norm_w = _gate_norm_bwd(y_ssd, zx, norm_w, dyn, name="ssd_gate_norm_bwd")
    mats = {"ssd_w_out": g_w_out, "attn_w_qkv": g_w_qkv, "attn_w_o": g_w_o,
            "mlp_w_up0": g_up0, "mlp_w_up1": g_up1, "mlp_w_down0": g_down0, "mlp_w_down1": g_down1}
    if comm is None:
        dxc, dbm, dcm, ddt_r, dpar = _ssd_bwd(xc, dtr, cumr, alog_b, d_b, states, dy_ssd, name="ssd_scan_bwd")
    else:
        (dxc, dbm, dcm, ddt_r, dpar), received = _ssd_bwd(xc, dtr, cumr, alog_b, d_b, states, dy_ssd,
                                                          name="ssd_scan_bwd", hook=comm.exchange_hook(mats, "early"))
        comm.received(received)
    dzx, g_conv_w, g_conv_b = _conv_bwd(zx, wts["ssd_conv_w"], row(wts["ssd_conv_b"]), dxc, dbm, dcm, dzx, name="ssd_conv_bwd")
    dzx, g_dt_bias = _softplus_bwd(dt_raw, bias_row, ddt_r, dzx, name="ssd_dt_bwd")
    g_w_in = _w_in_to_shards(_matmul(u0, dzx, mode="tn", out_dtypes=(BF16,), tn=SSD_IN_TILE, tk=BIG_TILE, name="ssd_dwin"), name="ssd_dwin_shards")
    mats["ssd_w_in"] = g_w_in
    if comm is None:
        du0 = _matmul(dzx, wts["ssd_w_in"], mode="nt", out_dtypes=(BF16,), tk=SSD_IN_TILE, name="ssd_du")
    else:
        du0, received = _matmul(dzx, wts["ssd_w_in"], mode="nt", out_dtypes=(BF16,), tk=SSD_IN_TILE, name="ssd_du",
                                hook=comm.exchange_hook(mats, "late"))
        comm.received(received)
    grad_x, g_mix_pre0 = _rms_bwd(x, row(mix_pre[0]), du0, resid=dh1, name="rms_pre_mix0_bwd")

    dpar = dpar.reshape(SSD_N_HEADS, LANES)
    vecs = {
        "ssd_conv_w": g_conv_w, "ssd_conv_b": g_conv_b.reshape(-1),
        "ssd_dt_bias": g_dt_bias[0, :SSD_N_HEADS], "ssd_a_log": dpar[:, 0], "ssd_d": dpar[:, 1],
        "ssd_norm_w": g_norm_w.reshape(-1), "attn_b_qkv": g_b_qkv.reshape(-1), "attn_sinks": g_sinks[:, 0],
        "attn_b_o": g_b_o.reshape(-1),
        "mix_pre_norm": jnp.concatenate([g_mix_pre0, g_mix_pre1]), "mix_post_norm": jnp.concatenate([g_mix_post0, g_mix_post1]),
        "ffn_pre_norm": jnp.concatenate([g_ffn_pre0, g_ffn_pre1]), "ffn_post_norm": jnp.concatenate([g_ffn_post0, g_ffn_post1]),
    }
    return loss_tile, grad_x, mats, vecs


def _mesh_position():
    return lax.axis_index("x"), lax.axis_index("y"), lax.axis_index("c")


def _flip(v, bit):
    return 1 - v if bit else v


OTHER_CHIPS = ((1, 0), (0, 1), (1, 1))


def _comm_params():
    return pltpu.CompilerParams(vmem_limit_bytes=VMEM_LIMIT)


def _staged_copies(srcs, dsts, bufs, sems_in, sems_out):
    loads = [pltpu.make_async_copy(s, b, sems_in.at[i]) for i, (s, b) in enumerate(zip(srcs, bufs))]
    stores = [pltpu.make_async_copy(b, d, sems_out.at[i]) for i, (b, d) in enumerate(zip(bufs, dsts))]
    return loads, stores


class _GatherHook:
    def __init__(self, mats, vecs=()):
        self.arrs = list(mats) + list(vecs)
        self.nm, self.n = len(mats), len(self.arrs)
        n_ici, n_fwd = (N_CHIPS - 1) * self.n, max((N_CHIPS - 1) * self.nm, 1)
        dma = pltpu.SemaphoreType.DMA
        self.out_shape = [jax.ShapeDtypeStruct((N_CHIPS,) + a.shape, a.dtype) for a in self.arrs]
        self.scratch = [pltpu.VMEM(a.shape, a.dtype) for a in self.arrs] + [
            dma((n_ici,)), dma((n_ici,)), dma((n_fwd,)), dma((n_fwd,)), dma((self.n,)), dma((self.n,))]

    def plan(self, ins, outs, scratch):
        n, nm = self.n, self.nm
        bufs = scratch[:n]
        ici_send, ici_recv, fwd_send, fwd_recv, load_sems, store_sems = scratch[n:]
        xi, yi, ci = _mesh_position()
        me = 2 * xi + yi
        loads, stores = _staged_copies(ins, [outs[i].at[me] for i in range(n)], bufs, load_sems, store_sems)
        sends, landed, forwards, from_sibling = [], [], [], []
        for j, (bx, by) in enumerate(OTHER_CHIPS):
            px, py = _flip(xi, bx), _flip(yi, by)
            peer = 2 * px + py
            for i in range(n):
                k = j * n + i
                mk = functools.partial(pltpu.make_async_remote_copy, send_sem=ici_send.at[k], recv_sem=ici_recv.at[k],
                                       device_id=(px, py, ci), device_id_type=MESH)
                if i < nm:
                    sends.append(mk(src_ref=ins[i].at[ci], dst_ref=outs[i].at[me, ci]))
                    landed.append(mk(src_ref=ins[i].at[ci], dst_ref=outs[i].at[peer, ci]))
                    kf = j * nm + i
                    fw = functools.partial(pltpu.make_async_remote_copy, send_sem=fwd_send.at[kf], recv_sem=fwd_recv.at[kf],
                                           device_id=(xi, yi, 1 - ci), device_id_type=MESH)
                    forwards.append(fw(src_ref=outs[i].at[peer, ci], dst_ref=outs[i].at[peer, ci]))
                    from_sibling.append(fw(src_ref=outs[i].at[peer, ci], dst_ref=outs[i].at[peer, 1 - ci]))
                else:
                    sends.append(mk(src_ref=ins[i], dst_ref=outs[i].at[me]))
                    landed.append(mk(src_ref=ins[i], dst_ref=outs[i].at[peer]))
                    forwards.append(None)
        return loads, stores, sends, landed, forwards, from_sibling

    @staticmethod
    def start(p):
        loads, _, sends, _, _, _ = p
        for cp in loads + sends:
            cp.start()

    @staticmethod
    def relay(p):
        loads, stores, _, landed, forwards, _ = p
        for ld, st in zip(loads, stores):
            ld.wait()
            st.start()
        for cp, fw in zip(landed, forwards):
            cp.wait_recv()
            if fw is not None:
                fw.start()

    @staticmethod
    def finish(p):
        _, stores, sends, _, forwards, from_sibling = p
        for cp in from_sibling:
            cp.wait_recv()
        for cp in sends + [fw for fw in forwards if fw is not None]:
            cp.wait_send()
        for st in stores:
            st.wait()


def _run_hook(hook, ins, outs, scratch, step, n_steps):
    p = hook.plan(ins, outs, scratch)
    relay_step = min(max(1, (3 * n_steps) // 4), n_steps - 1)

    @pl.when(step == 0)
    def _():
        hook.start(p)

    if relay_step < n_steps - 1:
        @pl.when(step == relay_step)
        def _():
            hook.relay(p)

    @pl.when(step == n_steps - 1)
    def _():
        if relay_step == n_steps - 1:
            hook.relay(p)
        hook.finish(p)


def _hook_call(hook, *, name):
    n = len(hook.arrs)

    def body(*refs):
        p = hook.plan(refs[:n], refs[n:n + len(hook.out_shape)], refs[n + len(hook.out_shape):])
        hook.start(p)
        hook.relay(p)
        hook.finish(p)

    return pl.pallas_call(
        body, in_specs=[ANY] * n, out_specs=[ANY] * len(hook.out_shape), out_shape=hook.out_shape,
        scratch_shapes=hook.scratch, compiler_params=_comm_params(), name=name)(*hook.arrs)


def _send_other_half(parts, *, name):
    n = len(parts)

    def body(*refs):
        ins, outs = refs[:n], refs[n:2 * n]
        send_sems, recv_sems = refs[2 * n:]
        xi, yi, ci = _mesh_position()
        sibling = (xi, yi, 1 - ci)
        for i in range(n):
            for s in range(N_CHIPS):
                pltpu.make_async_remote_copy(src_ref=ins[i].at[s, 1 - ci], dst_ref=outs[i].at[s], send_sem=send_sems.at[i],
                                             recv_sem=recv_sems.at[i], device_id=sibling, device_id_type=MESH).start()
        for i in range(n):
            pltpu.make_async_remote_copy(src_ref=outs[i], dst_ref=outs[i], send_sem=send_sems.at[i], recv_sem=recv_sems.at[i],
                                         device_id=sibling, device_id_type=MESH).wait()

    return pl.pallas_call(
        body, in_specs=[ANY] * n, out_specs=[ANY] * n,
        out_shape=[jax.ShapeDtypeStruct((p.shape[0],) + p.shape[2:], p.dtype) for p in parts],
        scratch_shapes=[pltpu.SemaphoreType.DMA((n,)), pltpu.SemaphoreType.DMA((n,))],
        name=name)(*parts)


ROW_BLOCKS = 8
SUM_ROW_BLOCKS = 2


def _add_sibling_half(parts, theirs, core, *, name):
    n = len(parts)

    def body(core_ref, *refs):
        for a_ref, b_ref, o_ref in zip(refs[:n], refs[n:2 * n], refs[2 * n:]):
            o_ref[...] = (a_ref[...].astype(F32) + b_ref[...].astype(F32)).astype(o_ref.dtype)

    nb = SUM_ROW_BLOCKS
    mine = lambda p: pl.BlockSpec((None, None, p.shape[2] // nb, p.shape[3]), lambda s, rb, core_ref: (s, core_ref[0], rb, 0))
    other = lambda p: pl.BlockSpec((None, p.shape[1] // nb, p.shape[2]), lambda s, rb, core_ref: (s, rb, 0))
    return pl.pallas_call(
        body,
        grid_spec=pltpu.PrefetchScalarGridSpec(
            num_scalar_prefetch=1, grid=(N_CHIPS, nb),
            in_specs=[mine(p) for p in parts] + [other(q) for q in theirs], out_specs=[other(q) for q in theirs]),
        out_shape=[jax.ShapeDtypeStruct(q.shape, BF16) for q in theirs],
        compiler_params=_params("parallel", "parallel"), name=name)(core, *parts, *theirs)


class _ExchangeHook:
    def __init__(self, parts, to_all=()):
        self.arrs = list(parts) + list(to_all)
        self.n_parts, self.n = len(parts), len(self.arrs)
        n_ici, n_peer = max((N_CHIPS - 1) * self.n_parts, 1), (N_DEV - 1) * max(len(to_all), 1)
        dma = pltpu.SemaphoreType.DMA
        self.out_shape = [jax.ShapeDtypeStruct(p.shape, p.dtype) for p in parts] + [
            jax.ShapeDtypeStruct((N_DEV,) + a.shape, a.dtype) for a in to_all]
        self.scratch = [pltpu.VMEM(p.shape[1:], p.dtype) for p in parts] + [pltpu.VMEM(a.shape, a.dtype) for a in to_all] + [
            dma((n_ici,)), dma((n_ici,)), dma((n_peer,)), dma((n_peer,)), dma((self.n,)), dma((self.n,))]

    def plan(self, ins, outs, scratch):
        n, npt = self.n, self.n_parts
        bufs = scratch[:n]
        send_sems, recv_sems, all_send, all_recv, load_sems, store_sems = scratch[n:]
        xi, yi, ci = _mesh_position()
        me_chip = 2 * xi + yi
        me = 4 * xi + 2 * yi + ci
        loads, stores = _staged_copies([ins[i].at[me_chip] for i in range(npt)] + list(ins[npt:]),
                                       [outs[i].at[me_chip] for i in range(npt)] + [outs[i].at[me] for i in range(npt, n)],
                                       bufs, load_sems, store_sems)
        sends, recvs = [], []
        for j, (bx, by) in enumerate(OTHER_CHIPS):
            px, py = _flip(xi, bx), _flip(yi, by)
            peer = 2 * px + py
            for i in range(npt):
                k = j * npt + i
                mk = functools.partial(pltpu.make_async_remote_copy, src_ref=ins[i].at[peer], send_sem=send_sems.at[k],
                                       recv_sem=recv_sems.at[k], device_id=(px, py, ci), device_id_type=MESH)
                sends.append(mk(dst_ref=outs[i].at[me_chip]))
                recvs.append(mk(dst_ref=outs[i].at[peer]))
        for i in range(npt, n):
            for k in range(1, N_DEV):
                px, py, pc = _flip(xi, (k >> 2) & 1), _flip(yi, (k >> 1) & 1), _flip(ci, k & 1)
                slot = (i - npt) * (N_DEV - 1) + k - 1
                mk = functools.partial(pltpu.make_async_remote_copy, src_ref=ins[i], send_sem=all_send.at[slot],
                                       recv_sem=all_recv.at[slot], device_id=(px, py, pc), device_id_type=MESH)
                sends.append(mk(dst_ref=outs[i].at[me]))
                recvs.append(mk(dst_ref=outs[i].at[4 * px + 2 * py + pc]))
        return loads, stores, sends, recvs

    @staticmethod
    def start(p):
        loads, _, sends, _ = p
        for cp in loads + sends:
            cp.start()

    @staticmethod
    def relay(p):
        loads, stores, _, _ = p
        for ld, st in zip(loads, stores):
            ld.wait()
            st.start()

    @staticmethod
    def finish(p):
        _, stores, sends, recvs = p
        for cp in recvs:
            cp.wait_recv()
        for cp in sends:
            cp.wait_send()
        for st in stores:
            st.wait()


def _sum_chips(parts, *, name):
    n = len(parts)
    p = parts[0].shape[0]

    def body(*refs):
        s = pl.program_id(1)
        for x_ref, o_ref in zip(refs[:n], refs[n:]):
            @pl.when(s == 0)
            def _():
                o_ref[...] = x_ref[...].astype(F32)

            @pl.when(s > 0)
            def _():
                o_ref[...] += x_ref[...].astype(F32)

    blocks = lambda q: SUM_ROW_BLOCKS if q.shape[1] % (16 * SUM_ROW_BLOCKS) == 0 else 1
    assert len({blocks(q) for q in parts}) == 1
    nb = blocks(parts[0])
    return pl.pallas_call(
        body, grid=(nb, p),
        in_specs=[pl.BlockSpec((None, q.shape[1] // nb, q.shape[2]), lambda rb, s: (s, rb, 0)) for q in parts],
        out_specs=[pl.BlockSpec((q.shape[1] // nb, q.shape[2]), lambda rb, s: (rb, 0)) for q in parts],
        out_shape=[jax.ShapeDtypeStruct(q.shape[1:], F32) for q in parts],
        compiler_params=_params("parallel", "arbitrary"), name=name)(*parts)


def _swap_halves(halves, layers, *, name, hook=None):
    n = len(halves)
    out_shapes, slots = [], []
    for i, h in enumerate(halves):
        pair = [p for p in layers if i in p]
        if pair and pair[0][1] == i:
            slots.append((slots[pair[0][0]][0], 1))
        elif pair:
            out_shapes.append(jax.ShapeDtypeStruct((2, 2) + h.shape, h.dtype))
            slots.append((len(out_shapes) - 1, 0))
        else:
            out_shapes.append(jax.ShapeDtypeStruct((2,) + h.shape, h.dtype))
            slots.append((len(out_shapes) - 1, None))
    n_out = len(out_shapes)
    hk = _HookSlots(hook, n_in=n, n_out=n_out, n_scratch=n + 4)

    def body(*refs):
        ins, outs, scratch = hk.own(refs)
        bufs = scratch[:n]
        send_sems, recv_sems, load_sems, store_sems = scratch[n:]
        if hook is not None:
            extra = hk.plan(refs)
            hook.start(extra)
        xi, yi, ci = _mesh_position()
        own, sends, recvs = [], [], []
        for i in range(n):
            o, layer = slots[i]
            dst = (lambda core: outs[o].at[core]) if layer is None else (lambda core: outs[o].at[layer, core])
            own.append(dst(ci))
            mk = functools.partial(pltpu.make_async_remote_copy, src_ref=ins[i], send_sem=send_sems.at[i],
                                   recv_sem=recv_sems.at[i], device_id=(xi, yi, 1 - ci), device_id_type=MESH)
            sends.append(mk(dst_ref=dst(ci)))
            recvs.append(mk(dst_ref=dst(1 - ci)))
        loads, stores = _staged_copies(ins, own, bufs, load_sems, store_sems)
        for cp in loads + sends:
            cp.start()
        for ld, st in zip(loads, stores):
            ld.wait()
            st.start()
        for cp in recvs:
            cp.wait_recv()
        for cp in sends:
            cp.wait_send()
        for st in stores:
            st.wait()
        if hook is not None:
            hook.relay(extra)
            hook.finish(extra)

    outs = pl.pallas_call(
        body, in_specs=[ANY] * n + hk.in_specs, out_specs=[ANY] * n_out + hk.out_specs, out_shape=out_shapes + hk.out_shape,
        scratch_shapes=[pltpu.VMEM(h.shape, h.dtype) for h in halves]
        + [pltpu.SemaphoreType.DMA((n,)), pltpu.SemaphoreType.DMA((n,)), pltpu.SemaphoreType.DMA((n,)), pltpu.SemaphoreType.DMA((n,))]
        + hk.scratch,
        compiler_params=_comm_params(), name=name)(*halves, *hk.inputs)
    return outs if hook is None else (outs[:n_out], outs[n_out:])


def _cast_bf16(layers, x, norm_w, *, name, hook=None):
    n = len(layers)
    hk = _HookSlots(hook, n_in=n + 2, n_out=n + 1, n_scratch=0)

    def body(*refs):
        ins, outs, _ = hk.own(refs)
        if hook is not None:
            hk.run(refs, pl.program_id(0), ROW_BLOCKS)
        for i_ref, o_ref in zip(ins[:n], outs[:n]):
            o_ref[...] = i_ref[...].astype(o_ref.dtype)
        xv = ins[n][...]
        outs[n][...] = (xv * lax.rsqrt(jnp.mean(xv * xv, axis=-1, keepdims=True) + NORM_EPS) * ins[n + 1][...]).astype(BF16)

    in_blk = lambda a, l: pl.BlockSpec((None, a.shape[1] // ROW_BLOCKS, a.shape[2]), lambda i: (l, i, 0))
    out_blk = lambda a: pl.BlockSpec((a.shape[1] // ROW_BLOCKS, a.shape[2]), lambda i: (i, 0))
    x_blk = pl.BlockSpec((x.shape[0] // ROW_BLOCKS, x.shape[1]), lambda i: (i, 0))
    outs = pl.pallas_call(
        body, grid=(ROW_BLOCKS,),
        in_specs=[in_blk(a, l) for a, l in layers] + [x_blk, pl.BlockSpec((1, x.shape[1]), lambda i: (0, 0))] + hk.in_specs,
        out_specs=[out_blk(a) for a, _ in layers] + [x_blk] + hk.out_specs,
        out_shape=[jax.ShapeDtypeStruct(a.shape[1:], BF16) for a, _ in layers] + [jax.ShapeDtypeStruct(x.shape, BF16)] + hk.out_shape,
        scratch_shapes=hk.scratch,
        compiler_params=_params(*hk.semantics("parallel")), name=name)(*[a for a, _ in layers], x, norm_w, *hk.inputs)
    own = (outs[:n], outs[n])
    return own if hook is None else (own, outs[n + 1:])


def _full_weight(name, gathered):
    s, _, r, c = gathered.shape
    if name == "ssd_w_in":
        return _w_in_from_shards(gathered.reshape(s, 2 * r, c), name="ssd_w_in_unshard")
    if name in ("attn_w_qkv", "mlp_w_up0", "mlp_w_up1"):
        return gathered.reshape(s, 2 * r, c)
    return gathered.reshape(s * 2 * r, c)


class _StepComm:
    GATHER = {"in_proj": ("mlp_w_up0", "attn_w_o"), "conv": ("mlp_w_down0",), "scan": ("ssd_w_out", "mlp_w_up1"),
              "mlp_up_l0": ("attn_w_qkv",), "attn_fwd": ("mlp_w_down1",)}
    EXCHANGE = {"early": ("ssd_w_out", "attn_w_qkv", "attn_w_o", "mlp_w_up0", "mlp_w_up1", "mlp_w_down0", "mlp_w_down1"),
                "late": ("ssd_w_in",)}

    def __init__(self, shards, core):
        self.shards, self.core = shards, core
        self.chip_parts = {}
        self._pending = None

    def gather_hook(self, stage):
        names = self.GATHER.get(stage)
        return _GatherHook([self.shards[n] for n in names]) if names else None

    def weights_from(self, stage, gathered):
        return {n: _full_weight(n, g) for n, g in zip(self.GATHER[stage], gathered)}

    def chip_sums(self, mats, tag):
        parts = [_shard_halves(a) for a in mats.values()]
        theirs = _send_other_half(parts, name=f"grad_sibling_send_{tag}")
        return _add_sibling_half(parts, theirs, self.core, name=f"grad_chip_sum_{tag}")

    def exchange_hook(self, mats, which):
        self._pending = self.EXCHANGE[which]
        return _ExchangeHook(self.chip_sums({n: mats[n] for n in self._pending}, which))

    def received(self, arrays):
        self.chip_parts.update(zip(self._pending, arrays))


ADAMW_ROW_BLOCKS = 16


def _adamw(ws, gs, ms, vs, *, name, by_lanes=False):
    n = len(ws)
    if by_lanes:
        nb = min(a.shape[2] for a in ws) // LANES
    else:
        nb = ADAMW_ROW_BLOCKS if all(a.shape[1] % (8 * ADAMW_ROW_BLOCKS) == 0 for a in ws) else 1

    def body(*refs):
        ins, outs = refs[:4 * n], refs[4 * n:]
        for i in range(n):
            w_ref, g_ref, m_ref, v_ref = ins[i], ins[n + i], ins[2 * n + i], ins[3 * n + i]
            go_ref, d_ref, nm_ref, nv_ref = outs[i], outs[n + i], outs[2 * n + i], outs[3 * n + i]
            gv = g_ref[...]
            nm = ADAM_B1 * m_ref[...] + (1.0 - ADAM_B1) * gv
            nv = ADAM_B2 * v_ref[...] + (1.0 - ADAM_B2) * (gv * gv)
            m_hat = nm / (1.0 - ADAM_B1 ** ADAM_STEP)
            v_hat = nv / (1.0 - ADAM_B2 ** ADAM_STEP)
            go_ref[...] = gv
            d_ref[...] = -ADAM_LR * (m_hat / (jnp.sqrt(v_hat) + ADAM_EPS) + ADAM_WD * w_ref[...])
            nm_ref[...] = nm
            nv_ref[...] = nv

    if by_lanes:
        blks = [pl.BlockSpec((a.shape[0], a.shape[1], a.shape[2] // nb), lambda i: (0, 0, i)) for a in ws]
    else:
        blks = [pl.BlockSpec((a.shape[0], a.shape[1] // nb, a.shape[2]), lambda i: (0, i, 0)) for a in ws]
    shapes = [jax.ShapeDtypeStruct(a.shape, F32) for a in ws]
    outs = pl.pallas_call(body, grid=(nb,), in_specs=blks * 4, out_specs=blks * 4, out_shape=shapes * 4,
                          compiler_params=_params("parallel"), name=name)(*ws, *gs, *ms, *vs)
    return [tuple(outs[k * n + i] for k in range(4)) for i in range(n)]


SM_CONV_B, SM_NORM_W, SM_MIX_PRE, SM_MIX_POST, SM_FFN_PRE, SM_FFN_POST, SM_MISC, SM_CONV_W, SM_B_QKV, SM_B_O = 0, 4, 6, 8, 10, 12, 14, 16, 32, 34
SM_ROWS = 40
MISC_DT_BIAS, MISC_A_LOG, MISC_D, MISC_SINKS, MISC_LOSS = 0, 32, 64, 96, 112


def _shard_halves(a):
    c = a.shape[-1]
    return a.reshape(N_CHIPS, 2, -1, c)


def _rows(v):
    return v.reshape(-1, D_MODEL)


def _misc_row(dt_bias, a_log, d, sinks, loss):
    pad = jnp.zeros((D_MODEL - MISC_LOSS - 1,), F32)
    return jnp.concatenate([dt_bias.reshape(-1), a_log.reshape(-1), d.reshape(-1), sinks.reshape(-1), loss.reshape(1), pad]).reshape(1, D_MODEL)


def _replicated_rows(p, loss):
    return jnp.concatenate([
        _rows(p["ssd_conv_b"]), _rows(p["ssd_norm_w"]), _rows(p["mix_pre_norm"]), _rows(p["mix_post_norm"]),
        _rows(p["ffn_pre_norm"]), _rows(p["ffn_post_norm"]),
        _misc_row(p["ssd_dt_bias"], p["ssd_a_log"], p["ssd_d"], p["attn_sinks"], loss), jnp.zeros((1, D_MODEL), F32)], axis=0)


def _sharded_rows(conv_w, b_qkv, b_o):
    last = jnp.concatenate([b_qkv.reshape(-1), b_o.reshape(-1), jnp.zeros((D_MODEL - 640,), F32)]).reshape(1, D_MODEL)
    return jnp.concatenate([conv_w.reshape(SSD_CONV_WIDTH, D_MODEL), last, jnp.zeros((3, D_MODEL), F32)], axis=0)


REPLICATED = ("ssd_conv_b", "ssd_dt_bias", "ssd_a_log", "ssd_d", "ssd_norm_w", "attn_sinks",
              "mix_pre_norm", "mix_post_norm", "ffn_pre_norm", "ffn_post_norm")
MATRICES = ("ssd_w_in", "ssd_w_out", "attn_w_qkv", "attn_w_o", "mlp_w_up", "mlp_w_down")
WEIGHT_NAMES = ("ssd_w_in", "ssd_conv_w", "ssd_conv_b", "ssd_dt_bias", "ssd_a_log", "ssd_d", "ssd_norm_w", "ssd_w_out",
                "attn_w_qkv", "attn_b_qkv", "attn_sinks", "attn_w_o", "attn_b_o", "mlp_w_up", "mlp_w_down",
                "mix_pre_norm", "mix_post_norm", "ffn_pre_norm", "ffn_post_norm")


def _unpack_small(rows16, rows8, like):
    misc = rows16[SM_MISC]
    out = {
        "ssd_conv_b": rows16[SM_CONV_B:SM_CONV_B + 4], "ssd_norm_w": rows16[SM_NORM_W:SM_NORM_W + 2],
        "mix_pre_norm": rows16[SM_MIX_PRE:SM_MIX_PRE + 2], "mix_post_norm": rows16[SM_MIX_POST:SM_MIX_POST + 2],
        "ffn_pre_norm": rows16[SM_FFN_PRE:SM_FFN_PRE + 2], "ffn_post_norm": rows16[SM_FFN_POST:SM_FFN_POST + 2],
        "ssd_dt_bias": misc[MISC_DT_BIAS:MISC_DT_BIAS + 32], "ssd_a_log": misc[MISC_A_LOG:MISC_A_LOG + 32],
        "ssd_d": misc[MISC_D:MISC_D + 32], "attn_sinks": misc[MISC_SINKS:MISC_SINKS + 16],
        "ssd_conv_w": rows8[0:SSD_CONV_WIDTH], "attn_b_qkv": rows8[SSD_CONV_WIDTH, 0:384], "attn_b_o": rows8[SSD_CONV_WIDTH, 384:640],
    }
    return {k: v.reshape(like[k].shape) for k, v in out.items()}


def kernel(x, ssd_w_in, ssd_conv_w, ssd_conv_b, ssd_dt_bias, ssd_a_log, ssd_d, ssd_norm_w, ssd_w_out, attn_w_qkv, attn_b_qkv, attn_sinks, attn_w_o, attn_b_o, mlp_w_up, mlp_w_down, mix_pre_norm, mix_post_norm, ffn_pre_norm, ffn_post_norm, loss_target, m_ssd_w_in, m_ssd_conv_w, m_ssd_conv_b, m_ssd_dt_bias, m_ssd_a_log, m_ssd_d, m_ssd_norm_w, m_ssd_w_out, m_attn_w_qkv, m_attn_b_qkv, m_attn_sinks, m_attn_w_o, m_attn_b_o, m_mlp_w_up, m_mlp_w_down, m_mix_pre_norm, m_mix_post_norm, m_ffn_pre_norm, m_ffn_post_norm, v_ssd_w_in, v_ssd_conv_w, v_ssd_conv_b, v_ssd_dt_bias, v_ssd_a_log, v_ssd_d, v_ssd_norm_w, v_ssd_w_out, v_attn_w_qkv, v_attn_b_qkv, v_attn_sinks, v_attn_w_o, v_attn_b_o, v_mlp_w_up, v_mlp_w_down, v_mix_pre_norm, v_mix_post_norm, v_ffn_pre_norm, v_ffn_post_norm):
    w = dict(zip(WEIGHT_NAMES, (ssd_w_in, ssd_conv_w, ssd_conv_b, ssd_dt_bias, ssd_a_log, ssd_d, ssd_norm_w, ssd_w_out, attn_w_qkv, attn_b_qkv, attn_sinks, attn_w_o, attn_b_o, mlp_w_up, mlp_w_down, mix_pre_norm, mix_post_norm, ffn_pre_norm, ffn_post_norm)))
    m = dict(zip(WEIGHT_NAMES, (m_ssd_w_in, m_ssd_conv_w, m_ssd_conv_b, m_ssd_dt_bias, m_ssd_a_log, m_ssd_d, m_ssd_norm_w, m_ssd_w_out, m_attn_w_qkv, m_attn_b_qkv, m_attn_sinks, m_attn_w_o, m_attn_b_o, m_mlp_w_up, m_mlp_w_down, m_mix_pre_norm, m_mix_post_norm, m_ffn_pre_norm, m_ffn_post_norm)))
    v = dict(zip(WEIGHT_NAMES, (v_ssd_w_in, v_ssd_conv_w, v_ssd_conv_b, v_ssd_dt_bias, v_ssd_a_log, v_ssd_d, v_ssd_norm_w, v_ssd_w_out, v_attn_w_qkv, v_attn_b_qkv, v_attn_sinks, v_attn_w_o, v_attn_b_o, v_mlp_w_up, v_mlp_w_down, v_mix_pre_norm, v_mix_post_norm, v_ffn_pre_norm, v_ffn_post_norm)))
    chip = 2 * lax.axis_index("x") + lax.axis_index("y")

    two_halves = lambda a: a.reshape(2, a.shape[-2] // 2, a.shape[-1])
    later = {"ssd_w_out": (w["ssd_w_out"], 0), "attn_w_qkv": (w["attn_w_qkv"], 0), "attn_w_o": (w["attn_w_o"], 0),
             "mlp_w_up0": (w["mlp_w_up"], 0), "mlp_w_up1": (w["mlp_w_up"], 1),
             "mlp_w_down0": (w["mlp_w_down"], 0), "mlp_w_down1": (w["mlp_w_down"], 1)}
    first = _GatherHook([two_halves(w["ssd_w_in"].astype(BF16))], [w["ssd_conv_w"][0], w["attn_b_qkv"], w["attn_b_o"]])
    (cast, u0), (g_in, g_conv, g_bqkv, g_bo) = _cast_bf16(list(later.values()), x[0], w["mix_pre_norm"][0:1],
                                                          name="weights_to_bf16", hook=first)
    core = lax.axis_index("c").astype(jnp.int32).reshape(1)
    comm = _StepComm({k: two_halves(a) for k, a in zip(later, cast)}, core)
    full = {
        "ssd_w_in": _full_weight("ssd_w_in", g_in),
        "ssd_conv_w": g_conv.transpose(1, 0, 2).reshape(SSD_CONV_WIDTH, SSD_CONV_DIM),
        "attn_b_qkv": g_bqkv.reshape(ATTN_QKV), "attn_b_o": g_bo.reshape(D_MODEL),
    }
    for name in REPLICATED:
        full[name] = w[name][0] if name.startswith(("ssd_", "attn_")) else w[name]

    loss_tile, grad_x, gm, g = _local_step(x[0], loss_target[0], full, comm, u0)

    conv_w_rows = g["ssd_conv_w"].reshape(SSD_CONV_WIDTH * N_CHIPS, D_MODEL)
    b_qkv_rows = jnp.pad(g["attn_b_qkv"], (0, 2 * D_MODEL - ATTN_QKV)).reshape(2, D_MODEL)
    small = jnp.concatenate([_replicated_rows(g, loss_tile[0, 0]), conv_w_rows, b_qkv_rows, _rows(g["attn_b_o"]),
                             jnp.zeros((SM_ROWS - SM_B_O - 1, D_MODEL), F32)], axis=0)
    order = ("ssd_w_in", "ssd_w_out", "attn_w_qkv", "attn_w_o", "mlp_w_up0", "mlp_w_up1", "mlp_w_down0", "mlp_w_down1")
    halves = _sum_chips([comm.chip_parts[k] for k in order], name="grad_sum")
    (r_in, r_out, r_qkv, r_o, r_up, r_down), (small_all,) = _swap_halves(
        halves, layers=((4, 5), (6, 7)), hook=_ExchangeHook([], [small]), name="grad_halves_swap")
    small_sum, = _sum_chips([small_all], name="small_grad_sum")

    grads = {"ssd_w_in": r_in, "ssd_w_out": r_out, "attn_w_qkv": r_qkv, "attn_w_o": r_o, "mlp_w_up": r_up, "mlp_w_down": r_down}
    grads = {k: a.reshape(w[k].shape) for k, a in grads.items()}
    conv_w_g = lax.dynamic_index_in_dim(small_sum[SM_CONV_W:SM_CONV_W + 16].reshape(SSD_CONV_WIDTH, N_CHIPS, D_MODEL), chip, axis=1, keepdims=False)
    b_qkv_g = lax.dynamic_slice_in_dim(small_sum[SM_B_QKV:SM_B_QKV + 2].reshape(-1), chip * 384, 384)
    b_o_g = lax.dynamic_slice_in_dim(small_sum[SM_B_O], chip * 256, 256)
    small_g = jnp.concatenate([small_sum[0:16], _sharded_rows(conv_w_g, b_qkv_g, b_o_g)], axis=0)
    grads.update(_unpack_small(small_g[0:16], small_g[16:24], w))
    loss = small_sum[SM_MISC, MISC_LOSS]

    delta, new_m, new_v = {}, {}, {}
    stored = lambda a: jnp.swapaxes(a, 1, 2)
    rest = [name for name in MATRICES if name != "ssd_w_in"]
    mats = lambda p: [p[name] for name in rest]
    results = dict(zip(rest, _adamw(mats(w), mats(grads), mats(m), mats(v), name="adamw_matrices")))
    (w_in_result,) = _adamw([stored(w["ssd_w_in"])], [stored(grads["ssd_w_in"])], [stored(m["ssd_w_in"])],
                            [stored(v["ssd_w_in"])], by_lanes=True, name="adamw_ssd_w_in")
    results["ssd_w_in"] = tuple(stored(a) for a in w_in_result)
    for name in MATRICES:
        grads[name], delta[name], new_m[name], new_v[name] = results[name]
    zero = jnp.zeros((), F32)
    small_pack = lambda p: jnp.concatenate([_replicated_rows({k: p[k] for k in REPLICATED}, zero),
                                            _sharded_rows(p["ssd_conv_w"], p["attn_b_qkv"], p["attn_b_o"])], axis=0)[None]
    (_, d_s, m_s, v_s), = _adamw([small_pack(w)], [small_g[None]], [small_pack(m)], [small_pack(v)], name="adamw_vectors")
    d_s, m_s, v_s = d_s[0], m_s[0], v_s[0]
    delta.update(_unpack_small(d_s[0:16], d_s[16:24], w))
    new_m.update(_unpack_small(m_s[0:16], m_s[16:24], w))
    new_v.update(_unpack_small(v_s[0:16], v_s[16:24], w))

    return (loss, grad_x[None], *[grads[n] for n in WEIGHT_NAMES], *[delta[n] for n in WEIGHT_NAMES],
            *[new_m[n] for n in WEIGHT_NAMES], *[new_v[n] for n in WEIGHT_NAMES])
```

```python
import functools

import jax
import jax.numpy as jnp
from jax import lax
from jax.experimental import pallas as pl
from jax.experimental.pallas import tpu as pltpu

F32 = jnp.float32
BF16 = jnp.bfloat16

D_MODEL = 1024
SSD_D_INNER = 2048
SSD_HEAD_DIM = 64
SSD_N_HEADS = 32
SSD_N_GROUPS = 8
SSD_HPG = 4
SSD_D_STATE = 128
SSD_CONV_WIDTH = 4
SSD_CHUNK = 128
SSD_CONV_DIM = 4096
SSD_IN_DIM = 6176
SSD_IN_PAD = 6400
SSD_IN_TILE = 1280
SSD_DT_COL = 6144
SSD_GW = SSD_HPG * SSD_HEAD_DIM
ATTN_HEAD_DIM = 64
ATTN_N_Q = 16
ATTN_N_KV = 4
ATTN_REP = 4
ATTN_WINDOW = 128
ATTN_QKV = 1536
D_FF = 4096
NORM_EPS = 1e-6

ADAM_LR = 0.001
ADAM_B1 = 0.9
ADAM_B2 = 0.999
ADAM_EPS = 1e-08
ADAM_WD = 0.01
ADAM_STEP = 10

N_CHIPS = 4
N_DEV = 8
LANES = 128
VMEM_LIMIT = 48 * 1024 * 1024
BIG_TILE = 2048
MESH = pl.DeviceIdType.MESH


def _params(*sem):
    return pltpu.CompilerParams(dimension_semantics=sem, vmem_limit_bytes=VMEM_LIMIT)


def _dot(a, b, dims):
    return lax.dot_general(a, b, (dims, ((), ())), preferred_element_type=F32)


def _dot_nn(a, b):
    return _dot(a, b, ((1,), (0,)))


def _dot_nt(a, b):
    return _dot(a, b, ((1,), (1,)))


def _dot_tn(a, b):
    return _dot(a, b, ((0,), (0,)))


def _sigmoid(x):
    return 0.5 * jnp.tanh(0.5 * x) + 0.5


ANY = pl.BlockSpec(memory_space=pl.ANY)


class _HookSlots:
    def __init__(self, hook, n_in, n_out, n_scratch):
        self.hook = hook
        self.n_in, self.n_out, self.n_scratch = n_in, n_out, n_scratch
        self.inputs = list(hook.arrs) if hook else []
        self.out_shape = list(hook.out_shape) if hook else []
        self.scratch = list(hook.scratch) if hook else []
        self.in_specs = [ANY] * len(self.inputs)
        self.out_specs = [ANY] * len(self.out_shape)

    def _split(self, refs):
        a = self.n_in
        b = a + len(self.inputs)
        c = b + self.n_out
        d = c + len(self.out_shape)
        e = d + self.n_scratch
        return refs[:a], refs[a:b], refs[b:c], refs[c:d], refs[d:e], refs[e:]

    def own(self, refs):
        ins, _, outs, _, scratch, _ = self._split(refs)
        return ins, outs, scratch

    def plan(self, refs):
        _, h_in, _, h_out, _, h_scratch = self._split(refs)
        return self.hook.plan(h_in, h_out, h_scratch)

    def run(self, refs, step, n_steps):
        _, h_in, _, h_out, _, h_scratch = self._split(refs)
        _run_hook(self.hook, h_in, h_out, h_scratch, step, n_steps)

    def semantics(self, *sem):
        return sem if self.hook is None else ("arbitrary",) * len(sem)


def _matmul(a, b, *, mode, out_dtypes, name, epilogue=None, extras=(), tm=1024, tn=1024, tk=1024,
            b_shards=False, out_shards=False, hook=None, f32_block=None):
    f32_tail = f32_block is not None
    if b_shards:
        s, b_rows, b_cols = b.shape
        b2 = (b_rows, s * b_cols)
        if mode == "nn":
            tn = b_cols
        else:
            assert mode == "nt"
            tk = b_cols
    else:
        b2 = b.shape
    if mode == "nn":
        (m, k), (k2, n) = a.shape, b2
    elif mode == "nt":
        (m, k), (n, k2) = a.shape, b2
    else:
        (k, m), (k2, n) = a.shape, b2
    assert k == k2, (a.shape, b.shape, mode)
    tm, tn, tk = min(tm, m), min(tn, n), min(tk, k)
    assert m % tm == 0 and n % tn == 0 and k % tk == 0, (m, n, k, tm, tn, tk)
    nk = k // tk
    if mode == "tn":
        a_spec = pl.BlockSpec((tk, tm), lambda i, j, kk: (kk, i))
    else:
        a_spec = pl.BlockSpec((tm, tk), lambda i, j, kk: (i, kk))
    if b_shards and mode == "nn":
        b_spec = pl.BlockSpec((None, tk, tn), lambda i, j, kk: (j, kk, 0))
    elif b_shards:
        b_spec = pl.BlockSpec((None, tn, tk), lambda i, j, kk: (kk, j, 0))
    elif mode == "nt":
        b_spec = pl.BlockSpec((tn, tk), lambda i, j, kk: (j, kk))
    else:
        b_spec = pl.BlockSpec((tk, tn), lambda i, j, kk: (kk, j))
    dims = {"nn": ((1,), (0,)), "nt": ((1,), (1,)), "tn": ((0,), (0,))}[mode]
    ex_specs = []
    for arr, kind in extras:
        if kind == "tile":
            ex_specs.append(pl.BlockSpec((tm, tn), lambda i, j, kk: (i, j)))
        else:
            ex_specs.append(pl.BlockSpec((1, tn), lambda i, j, kk: (0, j)))
    n_ex, n_out = len(extras), len(out_dtypes)
    if epilogue is None:
        epilogue = lambda acc: (acc,)
    hk = _HookSlots(hook, n_in=2 + n_ex, n_out=n_out + f32_tail, n_scratch=0 if nk == 1 else 1)
    grid = (m // tm, n // tn, nk)

    def body(*refs):
        (a_ref, b_ref, *ex), outs, scratch = hk.own(refs)
        if hook is not None:
            step = (pl.program_id(0) * grid[1] + pl.program_id(1)) * grid[2] + pl.program_id(2)
            hk.run(refs, step, grid[0] * grid[1] * grid[2])

        def finish(acc):
            res = epilogue(acc, *[e[...] for e in ex])
            for o, r in zip(outs, res):
                o[...] = r.astype(o.dtype)
            if f32_tail:
                outs[n_out][...] = acc[:, f32_block:f32_block + LANES]

        if nk == 1:
            finish(_dot(a_ref[...], b_ref[...], dims))
        else:
            acc_ref = scratch[0]
            kk = pl.program_id(2)

            @pl.when(kk == 0)
            def _():
                acc_ref[...] = jnp.zeros_like(acc_ref)

            acc_ref[...] += _dot(a_ref[...], b_ref[...], dims)

            @pl.when(kk == nk - 1)
            def _():
                finish(acc_ref[...])

    if out_shards:
        out_spec = pl.BlockSpec((None, tm, tn), lambda i, j, kk: (j, i, 0))
        out_dims = (n // tn, m, tn)
    else:
        out_spec = pl.BlockSpec((tm, tn), lambda i, j, kk: (i, j))
        out_dims = (m, n)
    tail_specs = [pl.BlockSpec((tm, LANES), lambda i, j, kk: (i, 0))] if f32_tail else []
    tail_shapes = [jax.ShapeDtypeStruct((m, LANES), F32)] if f32_tail else []
    outs = pl.pallas_call(
        body,
        grid=grid,
        in_specs=[a_spec, b_spec] + ex_specs + hk.in_specs,
        out_specs=[out_spec for _ in out_dtypes] + tail_specs + hk.out_specs,
        out_shape=[jax.ShapeDtypeStruct(out_dims, dt) for dt in out_dtypes] + tail_shapes + hk.out_shape,
        scratch_shapes=([] if nk == 1 else [pltpu.VMEM((tm, tn), F32)]) + hk.scratch,
        compiler_params=_params(*hk.semantics("parallel", "arbitrary" if f32_tail else "parallel", "arbitrary")),
        name=name,
    )(a, b, *[arr for arr, _ in extras], *hk.inputs)
    n_own = n_out + f32_tail
    own = outs[0] if n_own == 1 else outs[:n_own]
    return own if hook is None else (own, outs[n_own:])


def _row_tile(t, want):
    return min(t, want)


def _rms_fwd(x, w, *, name, resid=None, want_u=None, target=None):
    t, d = x.shape
    tr = _row_tile(t, 1024)

    def norm(v, wv):
        return v * lax.rsqrt(jnp.mean(v * v, axis=-1, keepdims=True) + NORM_EPS) * wv

    row = pl.BlockSpec((tr, d), lambda i: (i, 0))
    vec = pl.BlockSpec((1, d), lambda i: (0, 0))
    if target is not None:
        def body(x_ref, w_ref, r_ref, t_ref, dh_ref, loss_ref):
            err = r_ref[...] + norm(x_ref[...].astype(F32), w_ref[...]) - t_ref[...]
            dh_ref[...] = err * (1.0 / d)

            @pl.when(pl.program_id(0) == 0)
            def _():
                loss_ref[...] = jnp.zeros_like(loss_ref)

            part = jnp.sum(jnp.sum(err * err, axis=1, keepdims=True), axis=0, keepdims=True) * (0.5 / d)
            loss_ref[...] += jnp.broadcast_to(part, loss_ref.shape)

        return pl.pallas_call(
            body, grid=(t // tr,), in_specs=[row, vec, row, row],
            out_specs=[row, pl.BlockSpec((8, LANES), lambda i: (0, 0))],
            out_shape=[jax.ShapeDtypeStruct((t, d), F32), jax.ShapeDtypeStruct((8, LANES), F32)],
            compiler_params=_params("arbitrary"), name=name)(x, w, resid, target)
    if resid is None:
        def body(x_ref, w_ref, o_ref):
            o_ref[...] = norm(x_ref[...].astype(F32), w_ref[...]).astype(BF16)
        ins, in_specs = (x, w), [row, vec]
        out_shape, out_specs = jax.ShapeDtypeStruct((t, d), BF16), row
    elif want_u is None:
        def body(x_ref, w_ref, r_ref, o_ref):
            o_ref[...] = r_ref[...] + norm(x_ref[...].astype(F32), w_ref[...])
        ins, in_specs = (x, w, resid), [row, vec, row]
        out_shape, out_specs = jax.ShapeDtypeStruct((t, d), F32), row
    else:
        def body(x_ref, w_ref, r_ref, w2_ref, o_ref, u_ref):
            h = r_ref[...] + norm(x_ref[...].astype(F32), w_ref[...])
            o_ref[...] = h
            u_ref[...] = norm(h, w2_ref[...]).astype(BF16)
        ins, in_specs = (x, w, resid, want_u), [row, vec, row, vec]
        out_shape = [jax.ShapeDtypeStruct((t, d), F32), jax.ShapeDtypeStruct((t, d), BF16)]
        out_specs = [row, row]
    return pl.pallas_call(body, grid=(t // tr,), in_specs=in_specs, out_specs=out_specs, out_shape=out_shape,
                          compiler_params=_params("parallel"), name=name)(*ins)


def _rms_bwd(x, w, dy, *, name, resid=None, out_dtype=F32, dx_col_sum=False):
    t, d = x.shape
    tr = _row_tile(t, 1024)
    row = pl.BlockSpec((tr, d), lambda i: (i, 0))
    vec = pl.BlockSpec((1, d), lambda i: (0, 0))
    has_res = resid is not None

    def body(x_ref, w_ref, dy_ref, *rest):
        r_ref = rest[0] if has_res else None
        dx_ref, dw_ref = rest[has_res:has_res + 2]
        xv = x_ref[...].astype(F32)
        dyv = dy_ref[...].astype(F32)
        r = lax.rsqrt(jnp.mean(xv * xv, axis=-1, keepdims=True) + NORM_EPS)
        xhat = xv * r
        dyw = dyv * w_ref[...]
        dx = r * (dyw - xhat * jnp.mean(dyw * xhat, axis=-1, keepdims=True))
        if has_res:
            dx = dx + r_ref[...]
        dx_ref[...] = dx.astype(dx_ref.dtype)

        sums = [(dw_ref, dyv * xhat)] + ([(rest[-1], dx)] if dx_col_sum else [])

        @pl.when(pl.program_id(0) == 0)
        def _():
            for acc_ref, _ in sums:
                acc_ref[...] = jnp.zeros_like(acc_ref)

        for acc_ref, rows in sums:
            acc_ref[...] += jnp.sum(rows, axis=0, keepdims=True)

    ins = (x, w, dy) + ((resid,) if has_res else ())
    in_specs = [row, vec, row] + ([row] if has_res else [])
    n_vec = 2 if dx_col_sum else 1
    return pl.pallas_call(
        body, grid=(t // tr,), in_specs=in_specs, out_specs=[row] + [vec] * n_vec,
        out_shape=[jax.ShapeDtypeStruct((t, d), out_dtype)] + [jax.ShapeDtypeStruct((1, d), F32)] * n_vec,
        compiler_params=_params("arbitrary"), name=name)(*ins)


def _col_sum(x, *, name):
    t, n = x.shape
    tr = _row_tile(t, 1024)

    def body(x_ref, o_ref):
        @pl.when(pl.program_id(0) == 0)
        def _():
            o_ref[...] = jnp.zeros_like(o_ref)

        o_ref[...] += jnp.sum(x_ref[...].astype(F32), axis=0, keepdims=True)

    return pl.pallas_call(
        body, grid=(t // tr,), in_specs=[pl.BlockSpec((tr, n), lambda i: (i, 0))],
        out_specs=pl.BlockSpec((1, n), lambda i: (0, 0)), out_shape=jax.ShapeDtypeStruct((1, n), F32),
        compiler_params=_params("arbitrary"), name=name)(x)


SSD_IN_SHARD = SSD_IN_DIM // N_CHIPS


def _w_in_from_shards(shards, *, name):
    d = shards.shape[1]
    tr = 256

    def body(s_ref, o_ref):
        o_ref[:, pl.ds(SSD_DT_COL, SSD_IN_PAD - SSD_DT_COL)] = jnp.zeros((tr, SSD_IN_PAD - SSD_DT_COL), o_ref.dtype)
        for s in range(N_CHIPS):
            o_ref[:, pl.ds(SSD_IN_SHARD * s, SSD_IN_SHARD)] = s_ref[s]

    return pl.pallas_call(
        body, grid=(d // tr,), in_specs=[pl.BlockSpec((N_CHIPS, tr, SSD_IN_SHARD), lambda i: (0, i, 0))],
        out_specs=pl.BlockSpec((tr, SSD_IN_PAD), lambda i: (i, 0)),
        out_shape=jax.ShapeDtypeStruct((d, SSD_IN_PAD), shards.dtype),
        compiler_params=_params("parallel"), name=name)(shards)


def _w_in_to_shards(g, *, name):
    d = g.shape[0]
    tr = 256

    def body(g_ref, o_ref):
        for s in range(N_CHIPS):
            o_ref[s] = g_ref[:, pl.ds(SSD_IN_SHARD * s, SSD_IN_SHARD)].astype(o_ref.dtype)

    return pl.pallas_call(
        body, grid=(d // tr,), in_specs=[pl.BlockSpec((tr, SSD_IN_PAD), lambda i: (i, 0))],
        out_specs=pl.BlockSpec((N_CHIPS, tr, SSD_IN_SHARD), lambda i: (0, i, 0)),
        out_shape=jax.ShapeDtypeStruct((N_CHIPS, d, SSD_IN_SHARD), BF16),
        compiler_params=_params("parallel"), name=name)(g)


XBC_COL0 = SSD_D_INNER // LANES


def _shift_down(v, k, row_ids):
    return jnp.where(row_ids >= k, pltpu.roll(v, k, axis=0), 0.0)


def _shift_up(v, k, row_ids):
    n = v.shape[0]
    return jnp.where(row_ids < n - k, pltpu.roll(v, n - k, axis=0), 0.0)


def _conv_pre(x, w, b, row_ids):
    pre = b + w[3:4, :] * x
    for k in (1, 2, 3):
        pre = pre + w[3 - k:4 - k, :] * _shift_down(x, k, row_ids)
    return pre


def _conv_fwd(zx, conv_w, conv_b, *, name, hook=None):
    t = zx.shape[0]
    cw = 2 * LANES
    nct = SSD_CONV_DIM // cw
    col0 = SSD_D_INNER // cw
    hk = _HookSlots(hook, n_in=3, n_out=1, n_scratch=0)

    def body(*refs):
        (x_ref, w_ref, b_ref), (o_ref,), _ = hk.own(refs)
        if hook is not None:
            hk.run(refs, pl.program_id(0), nct)
        x = x_ref[...].astype(F32)
        row_ids = lax.broadcasted_iota(jnp.int32, x.shape, 0)
        pre = _conv_pre(x, w_ref[...], b_ref[...], row_ids)
        o_ref[...] = pre * _sigmoid(pre)

    outs = pl.pallas_call(
        body, grid=(nct,),
        in_specs=[pl.BlockSpec((t, cw), lambda j: (0, col0 + j)),
                  pl.BlockSpec((SSD_CONV_WIDTH, cw), lambda j: (0, j)),
                  pl.BlockSpec((1, cw), lambda j: (0, j))] + hk.in_specs,
        out_specs=[pl.BlockSpec((t, cw), lambda j: (0, j))] + hk.out_specs,
        out_shape=[jax.ShapeDtypeStruct((t, SSD_CONV_DIM), F32)] + hk.out_shape,
        scratch_shapes=hk.scratch,
        compiler_params=_params(*hk.semantics("parallel")), name=name)(zx, conv_w, conv_b, *hk.inputs)
    return outs[0] if hook is None else (outs[0], outs[1:])


def _conv_bwd(zx, conv_w, conv_b, d_xs, d_bm, d_cm, dzx, *, name):
    t = zx.shape[0]
    nct = SSD_CONV_DIM // LANES
    n_xs = SSD_D_INNER // LANES
    n_bm = SSD_N_GROUPS * SSD_D_STATE // LANES

    def body(x_ref, w_ref, b_ref, dxs_ref, dbm_ref, dcm_ref, _, dx_ref, dw_ref, db_ref):
        x = x_ref[...].astype(F32)
        w = w_ref[...]
        j = pl.program_id(0)
        dy = jnp.where(j < n_xs, dxs_ref[...], jnp.where(j < n_xs + n_bm, dbm_ref[...], dcm_ref[...]))
        row_ids = lax.broadcasted_iota(jnp.int32, x.shape, 0)
        pre = _conv_pre(x, w, b_ref[...], row_ids)
        sg = _sigmoid(pre)
        dpre = dy * (sg * (1.0 + pre * (1.0 - sg)))
        dx = w[3:4, :] * dpre
        for k in (1, 2, 3):
            dx = dx + w[3 - k:4 - k, :] * _shift_up(dpre, k, row_ids)
        dx_ref[...] = dx.astype(dx_ref.dtype)
        db_ref[...] = jnp.sum(dpre, axis=0, keepdims=True)
        dw_ref[3:4, :] = jnp.sum(dpre * x, axis=0, keepdims=True)
        for k in (1, 2, 3):
            dw_ref[3 - k:4 - k, :] = jnp.sum(dpre * _shift_down(x, k, row_ids), axis=0, keepdims=True)

    clip = lambda j, lo, n: jnp.clip(j - lo, 0, n - 1)
    return pl.pallas_call(
        body, grid=(nct,),
        in_specs=[pl.BlockSpec((t, LANES), lambda j: (0, XBC_COL0 + j)),
                  pl.BlockSpec((SSD_CONV_WIDTH, LANES), lambda j: (0, j)),
                  pl.BlockSpec((1, LANES), lambda j: (0, j)),
                  pl.BlockSpec((t, LANES), lambda j: (0, clip(j, 0, n_xs))),
                  pl.BlockSpec((t, LANES), lambda j: (0, clip(j, n_xs, n_bm))),
                  pl.BlockSpec((t, LANES), lambda j: (0, clip(j, n_xs + n_bm, n_bm))), ANY],
        out_specs=[pl.BlockSpec((t, LANES), lambda j: (0, XBC_COL0 + j)),
                   pl.BlockSpec((SSD_CONV_WIDTH, LANES), lambda j: (0, j)), pl.BlockSpec((1, LANES), lambda j: (0, j))],
        out_shape=[jax.ShapeDtypeStruct(dzx.shape, dzx.dtype),
                   jax.ShapeDtypeStruct((SSD_CONV_WIDTH, SSD_CONV_DIM), F32),
                   jax.ShapeDtypeStruct((1, SSD_CONV_DIM), F32)],
        input_output_aliases={6: 0},
        compiler_params=_params("parallel"), name=name)(zx, conv_w, conv_b, d_xs, d_bm, d_cm, dzx)


def _softplus_fwd(dt_raw, bias_row, alog_row, *, name):
    t = dt_raw.shape[0]
    q = SSD_CHUNK
    tr = _row_tile(t, 1024)

    def body(x_ref, b_ref, al_ref, dt_ref, cum_ref):
        v = x_ref[...] + b_ref[...]
        e = jnp.exp(-jnp.abs(v))
        u = 1.0 + e
        log1p = jnp.where(u == 1.0, e, jnp.log(u) * (e / (u - 1.0)))
        dt = jnp.maximum(v, 0.0) + log1p
        a = dt * -jnp.exp(al_ref[...])
        lower = (lax.broadcasted_iota(jnp.int32, (q, q), 1) <= lax.broadcasted_iota(jnp.int32, (q, q), 0)).astype(F32)
        cums = [lax.dot_general(lower, a[c * q:(c + 1) * q, :], ((((1,), (0,))), ((), ())), precision=lax.Precision.HIGHEST,
                                preferred_element_type=F32) for c in range(tr // q)]
        dt_t, cum_t = dt.T, jnp.concatenate(cums, axis=0).T
        for g in range(SSD_N_GROUPS):
            rows = slice(g * SSD_HPG, (g + 1) * SSD_HPG)
            dt_ref[g] = dt_t[rows, :]
            cum_ref[g] = cum_t[rows, :]

    vec = pl.BlockSpec((1, LANES), lambda i: (0, 0))
    by_group = pl.BlockSpec((SSD_N_GROUPS, SSD_HPG, tr), lambda i: (0, 0, i))
    return pl.pallas_call(
        body, grid=(t // tr,),
        in_specs=[pl.BlockSpec((tr, LANES), lambda i: (i, 0)), vec, vec],
        out_specs=[by_group, by_group],
        out_shape=[jax.ShapeDtypeStruct((SSD_N_GROUPS, SSD_HPG, t), F32)] * 2,
        compiler_params=_params("parallel"), name=name)(dt_raw, bias_row, alog_row)


def _softplus_bwd(dt_raw, bias_row, ddt_rows, dzx, *, name):
    t = dt_raw.shape[0]
    tr = _row_tile(t, 1024)
    tail = SSD_IN_PAD - SSD_DT_COL

    def body(x_ref, b_ref, g_ref, _, o_ref, db_ref):
        v = x_ref[...] + b_ref[...]
        lane = lax.broadcasted_iota(jnp.int32, v.shape, 1)
        by_head = jnp.concatenate([g_ref[g] for g in range(SSD_N_GROUPS)]
                                  + [jnp.zeros((LANES - SSD_N_HEADS, tr), F32)], axis=0)
        d = jnp.where(lane < SSD_N_HEADS, by_head.T * _sigmoid(v), 0.0)
        o_ref[:, pl.ds(0, LANES)] = d.astype(o_ref.dtype)
        o_ref[:, pl.ds(LANES, tail - LANES)] = jnp.zeros((tr, tail - LANES), o_ref.dtype)

        @pl.when(pl.program_id(0) == 0)
        def _():
            db_ref[...] = jnp.zeros_like(db_ref)

        db_ref[...] += jnp.sum(d, axis=0, keepdims=True)

    return pl.pallas_call(
        body, grid=(t // tr,),
        in_specs=[pl.BlockSpec((tr, LANES), lambda i: (i, 0)), pl.BlockSpec((1, LANES), lambda i: (0, 0)),
                  pl.BlockSpec((SSD_N_GROUPS, SSD_HPG, tr), lambda i: (0, 0, i)), ANY],
        out_specs=[pl.BlockSpec((tr, tail), lambda i: (i, SSD_DT_COL // tail)), pl.BlockSpec((1, LANES), lambda i: (0, 0))],
        out_shape=[jax.ShapeDtypeStruct(dzx.shape, dzx.dtype), jax.ShapeDtypeStruct((1, LANES), F32)],
        input_output_aliases={3: 0},
        compiler_params=_params("arbitrary"), name=name)(dt_raw, bias_row, ddt_rows, dzx)


def _ssd_masks():
    q = SSD_CHUNK
    tt = lax.broadcasted_iota(jnp.int32, (q, q), 0)
    ss = lax.broadcasted_iota(jnp.int32, (q, q), 1)
    lane = lax.broadcasted_iota(jnp.int32, (1, SSD_GW), 1)
    srow = lax.broadcasted_iota(jnp.int32, (SSD_GW, 1), 0)
    hm = [(lane >= SSD_HEAD_DIM * j) & (lane < SSD_HEAD_DIM * (j + 1)) for j in range(SSD_HPG)]
    rm = [(srow >= SSD_HEAD_DIM * j) & (srow < SSD_HEAD_DIM * (j + 1)) for j in range(SSD_HPG)]
    return tt, ss, hm, rm


def _ssd_head_terms(dt_rows, cum_rows, a_rows, j, tt, ss):
    q = SSD_CHUNK
    dt_row = dt_rows[j:j + 1, :]
    dt_col = jnp.sum(jnp.where(tt == ss, dt_row, 0.0), axis=1, keepdims=True)
    a_row1 = a_rows[j:j + 1, :]
    a_11 = a_rows[j:j + 1, 0:1]
    cum_col = jnp.sum(jnp.where(ss <= tt, dt_row * a_row1, 0.0), axis=1, keepdims=True)
    cum_row = cum_rows[j:j + 1, :]
    decay = jnp.exp(jnp.where(ss <= tt, cum_col - cum_row, -jnp.inf))
    cum_last = cum_col[q - 1:q, :]
    e_col = jnp.exp(cum_col)
    dte_col = jnp.exp(cum_last - cum_col)
    e_last = jnp.exp(cum_last)
    return dt_col, dt_row, a_row1, a_11, decay, e_col, dte_col, e_last


SSD_CHUNKS_PER_STEP = 8
SSD_BC_COL0 = SSD_D_INNER // SSD_D_STATE


def _ssd_head_selects(terms, hm, rm):
    e_all = jnp.zeros((SSD_CHUNK, SSD_GW), F32)
    w_all = jnp.zeros((SSD_CHUNK, SSD_GW), F32)
    e_s = jnp.zeros((SSD_GW, 1), F32)
    for j in range(SSD_HPG):
        dt_col, _, _, _, _, e_col, dte_col, e_last = terms[j]
        e_all = jnp.where(hm[j], e_col, e_all)
        w_all = jnp.where(hm[j], dt_col * dte_col, w_all)
        e_s = jnp.where(rm[j], e_last, e_s)
    return e_all, w_all, e_s


def _ssd_fwd(xc, dtr, cumr, alog_b, d_b, *, name, hook=None):
    t = xc.shape[0]
    q = SSD_CHUNK
    nc = t // q
    kc = min(SSD_CHUNKS_PER_STEP, nc)
    rows = kc * q
    hk = _HookSlots(hook, n_in=7, n_out=2, n_scratch=1)

    def body(*refs):
        (x_ref, b_ref, c_ref, dtr_ref, cumr_ref, alog_ref, d_ref), (y_ref, st_ref), (s_scr,) = hk.own(refs)
        if hook is not None:
            hk.run(refs, pl.program_id(0) * (nc // kc) + pl.program_id(1), SSD_N_GROUPS * (nc // kc))

        @pl.when(pl.program_id(1) == 0)
        def _():
            s_scr[...] = jnp.zeros_like(s_scr)

        tt, ss, hm, rm = _ssd_masks()
        a_rows = -jnp.exp(alog_ref[...])
        d_rows = d_ref[...]
        d_all = jnp.zeros((1, SSD_GW), F32)
        for j in range(SSD_HPG):
            d_all = jnp.where(hm[j], d_rows[j:j + 1, 0:1], d_all)
        ks, hs = range(kc), range(SSD_HPG)
        sl = [pl.ds(k * q, q) for k in ks]
        x = [x_ref[sl[k], :] for k in ks]
        bm = [b_ref[sl[k], :].astype(BF16) for k in ks]
        cm = [c_ref[sl[k], :].astype(BF16) for k in ks]
        xb = [x[k].astype(BF16) for k in ks]
        terms = [[_ssd_head_terms(dtr_ref[:, sl[k]], cumr_ref[:, sl[k]], a_rows, j, tt, ss) for j in hs] for k in ks]
        g = [_dot_nt(cm[k], bm[k]) for k in ks]
        m = [[(g[k] * terms[k][j][4] * terms[k][j][1]).astype(BF16) for j in hs] for k in ks]
        yj = [[_dot_nn(m[k][j], xb[k]) for j in hs] for k in ks]
        sel = [_ssd_head_selects(terms[k], hm, rm) for k in ks]
        upd = [_dot_tn((x[k] * sel[k][1]).astype(BF16), bm[k]) for k in ks]
        states = [s_scr[...]]
        for k in ks:
            states.append(states[k] * sel[k][2] + upd[k])
        inter = [_dot_nt(cm[k], states[k].astype(BF16)) for k in ks]
        ys = []
        for k in ks:
            y = jnp.zeros((q, SSD_GW), F32)
            for j in hs:
                y = jnp.where(hm[j], yj[k][j], y)
            ys.append(y + inter[k] * sel[k][0] + x[k] * d_all)
        for k in ks:
            st_ref[k] = states[k]
        y_ref[...] = jnp.concatenate(ys, axis=0).astype(y_ref.dtype)
        s_scr[...] = states[kc]

    blk = lambda width, off: pl.BlockSpec((rows, width), lambda g, c: (c, off + g))
    par_s = pl.BlockSpec((None, SSD_HPG, LANES), lambda g, c: (g, 0, 0))
    row_s = pl.BlockSpec((None, SSD_HPG, rows), lambda g, c: (g, 0, c))
    outs = pl.pallas_call(
        body, grid=(SSD_N_GROUPS, nc // kc),
        in_specs=[blk(SSD_GW, 0), blk(SSD_D_STATE, SSD_BC_COL0), blk(SSD_D_STATE, SSD_BC_COL0 + SSD_N_GROUPS),
                  row_s, row_s, par_s, par_s] + hk.in_specs,
        out_specs=[blk(SSD_GW, 0), pl.BlockSpec((None, kc, SSD_GW, SSD_D_STATE), lambda g, c: (g, c, 0, 0))] + hk.out_specs,
        out_shape=[jax.ShapeDtypeStruct((t, SSD_D_INNER), BF16),
                   jax.ShapeDtypeStruct((SSD_N_GROUPS, nc, SSD_GW, SSD_D_STATE), F32)] + hk.out_shape,
        scratch_shapes=[pltpu.VMEM((SSD_GW, SSD_D_STATE), F32)] + hk.scratch,
        compiler_params=_params(*hk.semantics("parallel", "arbitrary")), name=name)(
            xc, xc, xc, dtr, cumr, alog_b, d_b, *hk.inputs)
    return outs if hook is None else (outs[:2], outs[2:])


def _ssd_bwd(xc, dtr, cumr, alog_b, d_b, states, dy, *, name, hook=None):
    t = xc.shape[0]
    q = SSD_CHUNK
    nc = t // q
    kc = min(SSD_CHUNKS_PER_STEP, nc)
    nst = nc // kc
    rows = kc * q
    rev = lambda c: nst - 1 - c
    hk = _HookSlots(hook, n_in=9, n_out=5, n_scratch=1)

    def body(*refs):
        ((x_ref, b_ref, c_ref, dtr_ref, cumr_ref, alog_ref, d_ref, st_ref, dy_ref),
         (dx_ref, db_ref, dc_ref, ddt_ref, dpar_ref), (ds_scr,)) = hk.own(refs)
        if hook is not None:
            hk.run(refs, pl.program_id(0) * nst + pl.program_id(1), SSD_N_GROUPS * nst)

        @pl.when(pl.program_id(1) == 0)
        def _():
            ds_scr[...] = jnp.zeros_like(ds_scr)
            dpar_ref[...] = jnp.zeros_like(dpar_ref)

        tt, ss, hm, rm = _ssd_masks()
        tcol = lax.broadcasted_iota(jnp.int32, (q, 1), 0)
        lane = lax.broadcasted_iota(jnp.int32, (1, LANES), 1)
        a_rows = -jnp.exp(alog_ref[...])
        d_rows = d_ref[...]
        d_all = jnp.zeros((1, SSD_GW), F32)
        for j in range(SSD_HPG):
            d_all = jnp.where(hm[j], d_rows[j:j + 1, 0:1], d_all)
        ks, hs = range(kc), range(SSD_HPG)
        sl = [pl.ds(k * q, q) for k in ks]
        x = [x_ref[sl[k], :] for k in ks]
        dyv = [dy_ref[sl[k], :].astype(F32) for k in ks]
        bm = [b_ref[sl[k], :].astype(BF16) for k in ks]
        cm = [c_ref[sl[k], :].astype(BF16) for k in ks]
        s_in = [st_ref[k] for k in ks]
        xb = [x[k].astype(BF16) for k in ks]
        dyb = [dyv[k].astype(BF16) for k in ks]
        s_b = [s_in[k].astype(BF16) for k in ks]
        terms = [[_ssd_head_terms(dtr_ref[:, sl[k]], cumr_ref[:, sl[k]], a_rows, j, tt, ss) for j in hs] for k in ks]
        sel = [_ssd_head_selects(terms[k], hm, rm) for k in ks]
        e_all, w_all, e_s = [s_[0] for s_ in sel], [s_[1] for s_ in sel], [s_[2] for s_ in sel]
        dye = [(dyv[k] * e_all[k]).astype(BF16) for k in ks]
        ds_loc = [_dot_tn(dye[k], cm[k]) for k in ks]
        ds = [None] * kc
        running = ds_scr[...]
        for k in reversed(ks):
            ds[k] = running
            running = running * e_s[k] + ds_loc[k]
        ds_scr[...] = running
        ds_b = [ds[k].astype(BF16) for k in ks]
        g = [_dot_nt(cm[k], bm[k]) for k in ks]
        cs = [_dot_nt(cm[k], s_b[k]) for k in ks]
        bds = [_dot_nt(bm[k], ds_b[k]) for k in ks]
        dm = [[_dot_nt(jnp.where(hm[j], dyv[k], 0.0).astype(BF16), xb[k]) for j in hs] for k in ks]
        gl = [[g[k] * terms[k][j][4] for j in hs] for k in ks]
        wp = [[dm[k][j] * gl[k][j] for j in hs] for k in ks]
        mt = [[(gl[k][j] * terms[k][j][1]).astype(BF16) for j in hs] for k in ks]
        dxj = [[_dot_tn(mt[k][j], dyb[k]) for j in hs] for k in ks]
        dg = []
        for k in ks:
            acc = jnp.zeros((q, q), F32)
            for j in hs:
                acc = acc + dm[k][j] * terms[k][j][4] * terms[k][j][1]
            dg.append(acc.astype(BF16))
        dy_cs = [dyv[k] * cs[k] for k in ks]
        x_bds = [x[k] * bds[k] for k in ks]
        dy_x = [dyv[k] * x[k] for k in ks]
        ds_s = [ds[k] * s_in[k] for k in ks]
        w = [[wp[k][j] * terms[k][j][1] for j in hs] for k in ks]
        rw_col = [[jnp.sum(w[k][j], axis=1, keepdims=True) for j in hs] for k in ks]
        cw_row = [[jnp.sum(w[k][j], axis=0, keepdims=True) for j in hs] for k in ks]
        cwp_row = [[jnp.sum(wp[k][j], axis=0, keepdims=True) for j in hs] for k in ks]
        r1_col = [[jnp.sum(jnp.where(hm[j], dy_cs[k], 0.0), axis=1, keepdims=True) * terms[k][j][5] for j in hs] for k in ks]
        dw_col = [[jnp.sum(jnp.where(hm[j], x_bds[k], 0.0), axis=1, keepdims=True) for j in hs] for k in ks]
        head_rows = [slice(j * SSD_HEAD_DIM, (j + 1) * SSD_HEAD_DIM) for j in hs]
        lane_sum = lambda v: jnp.sum(v, axis=1, keepdims=True)
        s_sum = [[lane_sum(jnp.sum(ds_s[k][head_rows[j], :], axis=0, keepdims=True)) for j in hs] for k in ks]
        dy_x_cols = [jnp.sum(dy_x[k], axis=0, keepdims=True) for k in ks]
        d_d = [[lane_sum(jnp.where(hm[j], dy_x_cols[k], 0.0)) for j in hs] for k in ks]
        ddt_rows = [[None] * SSD_HPG for _ in ks]
        dpar = [jnp.zeros((1, LANES), F32) for _ in hs]
        for k in ks:
            for j in hs:
                dt_col, dt_row, a_row1, a_11, _, _, dte_col, e_last = terms[k][j]
                dww = dw_col[k][j] * (dt_col * dte_col)
                last_add = jnp.sum(dww, axis=0, keepdims=True) + e_last * s_sum[k][j]
                dcum_col = rw_col[k][j] + r1_col[k][j] - dww + jnp.where(tcol == q - 1, last_add, 0.0)
                da_row = jnp.sum(jnp.where(tt >= ss, dcum_col, 0.0), axis=0, keepdims=True)
                da_col = jnp.sum(jnp.where(ss >= tt, -cw_row[k][j], 0.0), axis=1, keepdims=True)
                ddt_col = a_11 * da_col + dw_col[k][j] * dte_col
                ddt_rows[k][j] = (a_row1 * da_row + cwp_row[k][j]
                                  + jnp.sum(jnp.where(tt == ss, ddt_col, 0.0), axis=0, keepdims=True))
                d_a = jnp.sum(dt_row * da_row, axis=1, keepdims=True) + jnp.sum(dt_col * da_col, axis=0, keepdims=True)
                dpar[j] = dpar[j] + jnp.where(lane == 0, d_a * a_11, 0.0) + jnp.where(lane == 1, d_d[k][j], 0.0)
        dxs = []
        for k in ks:
            acc = jnp.zeros((q, SSD_GW), F32)
            for j in hs:
                acc = jnp.where(hm[j], dxj[k][j], acc)
            dxs.append(acc + w_all[k] * bds[k] + d_all * dyv[k])
        xw = [(x[k] * w_all[k]).astype(BF16) for k in ks]
        dc = [_dot_nn(dg[k], bm[k]) + _dot_nn(dye[k], s_b[k]) for k in ks]
        db = [_dot_tn(dg[k], cm[k]) + _dot_nn(xw[k], ds_b[k]) for k in ks]
        dx_ref[...] = jnp.concatenate(dxs, axis=0)
        dc_ref[...] = jnp.concatenate(dc, axis=0)
        db_ref[...] = jnp.concatenate(db, axis=0)
        ddt_ref[...] = jnp.concatenate([jnp.concatenate([ddt_rows[k][j] for k in ks], axis=1) for j in hs], axis=0)
        dpar_ref[...] += jnp.concatenate(dpar, axis=0)

    blk = lambda width, off: pl.BlockSpec((rows, width), lambda g, c: (rev(c), off + g))
    par_s = pl.BlockSpec((None, SSD_HPG, LANES), lambda g, c: (g, 0, 0))
    outs = pl.pallas_call(
        body, grid=(SSD_N_GROUPS, nst),
        in_specs=[blk(SSD_GW, 0), blk(SSD_D_STATE, SSD_BC_COL0), blk(SSD_D_STATE, SSD_BC_COL0 + SSD_N_GROUPS),
                  pl.BlockSpec((None, SSD_HPG, rows), lambda g, c: (g, 0, rev(c))),
                  pl.BlockSpec((None, SSD_HPG, rows), lambda g, c: (g, 0, rev(c))), par_s, par_s,
                  pl.BlockSpec((None, kc, SSD_GW, SSD_D_STATE), lambda g, c: (g, rev(c), 0, 0)), blk(SSD_GW, 0)] + hk.in_specs,
        out_specs=[blk(SSD_GW, 0), blk(SSD_D_STATE, 0), blk(SSD_D_STATE, 0),
                   pl.BlockSpec((None, SSD_HPG, rows), lambda g, c: (g, 0, rev(c))), par_s] + hk.out_specs,
        out_shape=[jax.ShapeDtypeStruct((t, SSD_D_INNER), F32),
                   jax.ShapeDtypeStruct((t, SSD_N_GROUPS * SSD_D_STATE), F32),
                   jax.ShapeDtypeStruct((t, SSD_N_GROUPS * SSD_D_STATE), F32),
                   jax.ShapeDtypeStruct((SSD_N_GROUPS, SSD_HPG, t), F32),
                   jax.ShapeDtypeStruct((SSD_N_GROUPS, SSD_HPG, LANES), F32)] + hk.out_shape,
        scratch_shapes=[pltpu.VMEM((SSD_GW, SSD_D_STATE), F32)] + hk.scratch,
        compiler_params=_params(*hk.semantics("parallel", "arbitrary")), name=name)(
            xc, xc, xc, dtr, cumr, alog_b, d_b, states, dy, *hk.inputs)
    return outs if hook is None else (outs[:5], outs[5:])


def _gate_norm_fwd(y, zx, norm_w, *, name):
    t = y.shape[0]
    tr = _row_tile(t, 512)
    row = pl.BlockSpec((tr, SSD_D_INNER), lambda i: (i, 0))

    def body(y_ref, z_ref, w_ref, o_ref):
        for gi in range(SSD_N_GROUPS):
            sl = pl.ds(gi * SSD_GW, SSD_GW)
            z = z_ref[:, sl].astype(F32)
            gv = y_ref[:, sl].astype(F32) * (z * _sigmoid(z))
            r = lax.rsqrt(jnp.mean(gv * gv, axis=-1, keepdims=True) + NORM_EPS)
            o_ref[:, sl] = (gv * r * w_ref[:, sl]).astype(BF16)

    return pl.pallas_call(
        body, grid=(t // tr,), in_specs=[row, row, pl.BlockSpec((1, SSD_D_INNER), lambda i: (0, 0))],
        out_specs=row, out_shape=jax.ShapeDtypeStruct((t, SSD_D_INNER), BF16),
        compiler_params=_params("parallel"), name=name)(y, zx, norm_w)


def _gate_norm_bwd(y, zx, norm_w, dyn, *, name):
    t = y.shape[0]
    tr = _row_tile(t, 512)
    row = pl.BlockSpec((tr, SSD_D_INNER), lambda i: (i, 0))
    vec = pl.BlockSpec((1, SSD_D_INNER), lambda i: (0, 0))

    def body(y_ref, z_ref, w_ref, dyn_ref, dy_ref, dz_ref, dw_ref):
        @pl.when(pl.program_id(0) == 0)
        def _():
            dw_ref[...] = jnp.zeros_like(dw_ref)

        for gi in range(SSD_N_GROUPS):
            sl = pl.ds(gi * SSD_GW, SSD_GW)
            z = z_ref[:, sl].astype(F32)
            yv = y_ref[:, sl].astype(F32)
            sg = _sigmoid(z)
            sz = z * sg
            gv = yv * sz
            r = lax.rsqrt(jnp.mean(gv * gv, axis=-1, keepdims=True) + NORM_EPS)
            ghat = gv * r
            dout = dyn_ref[:, sl].astype(F32)
            dgh = dout * w_ref[:, sl]
            dgv = r * (dgh - ghat * jnp.mean(dgh * ghat, axis=-1, keepdims=True))
            dy_ref[:, sl] = (dgv * sz).astype(dy_ref.dtype)
            dz_ref[:, sl] = (dgv * yv * (sg * (1.0 + z * (1.0 - sg)))).astype(dz_ref.dtype)
            dw_ref[:, sl] += jnp.sum(dout * ghat, axis=0, keepdims=True)

    return pl.pallas_call(
        body, grid=(t // tr,), in_specs=[row, row, vec, row], out_specs=[row, row, vec],
        out_shape=[jax.ShapeDtypeStruct((t, SSD_D_INNER), BF16), jax.ShapeDtypeStruct((t, SSD_IN_PAD), BF16),
                   jax.ShapeDtypeStruct((1, SSD_D_INNER), F32)],
        compiler_params=_params("arbitrary"), name=name)(y, zx, norm_w, dyn)


ATTN_KV_W = ATTN_N_KV * ATTN_HEAD_DIM
ATTN_Q_HALF = 512
ATTN_K_BLK = ATTN_N_Q * ATTN_HEAD_DIM // ATTN_KV_W
ATTN_V_BLK = ATTN_K_BLK + 1


def _attn_valid(first_block):
    w = ATTN_WINDOW
    qpos = lax.broadcasted_iota(jnp.int32, (w, 2 * w), 0) + w
    kpos = lax.broadcasted_iota(jnp.int32, (w, 2 * w), 1)
    rel = qpos - kpos
    return (rel >= 0) & (rel < w) & jnp.logical_not(first_block & (kpos < w))


def _attn_head_views(lo_ref, hi_ref):
    hd = ATTN_HEAD_DIM
    per_half = ATTN_Q_HALF // hd
    return [(lo_ref if h < per_half else hi_ref)[:, pl.ds((h % per_half) * hd, hd)] for h in range(ATTN_N_Q)]


def _attn_block_views(lo_ref, hi_ref, kc_ref, kp_ref, vc_ref, vp_ref):
    hd = ATTN_HEAD_DIM
    kv_cols = [pl.ds(kh * hd, hd) for kh in range(ATTN_N_KV)]
    kb = [jnp.concatenate([kp_ref[:, c], kc_ref[:, c]], axis=0) for c in kv_cols]
    vb = [jnp.concatenate([vp_ref[:, c], vc_ref[:, c]], axis=0) for c in kv_cols]
    return _attn_head_views(lo_ref, hi_ref), kb, vb


def _attn_scores(q, kb, valid):
    scale = ATTN_HEAD_DIM ** -0.5
    return [jnp.where(valid, _dot_nt(q[h], kb[h // ATTN_REP]) * scale, -jnp.inf) for h in range(ATTN_N_Q)]


def _attn_softmax(s, sink):
    heads = range(ATTN_N_Q)
    m = [jnp.maximum(jnp.max(s[h], axis=1, keepdims=True), sink[h]) for h in heads]
    e = [jnp.exp(s[h] - m[h]) for h in heads]
    es = [jnp.exp(sink[h] - m[h]) for h in heads]
    inv = [1.0 / (jnp.sum(e[h], axis=1, keepdims=True) + es[h]) for h in heads]
    return e, es, inv


def _attn_fwd(qkv, sinks_b, *, name, hook=None):
    t = qkv.shape[0]
    w = ATTN_WINDOW
    nb = t // w
    prev = lambda n: jnp.maximum(n - 1, 0)
    hk = _HookSlots(hook, n_in=7, n_out=1, n_scratch=0)

    def body(*refs):
        (qlo_ref, qhi_ref, kc_ref, kp_ref, vc_ref, vp_ref, sink_ref), (o_ref,), _ = hk.own(refs)
        if hook is not None:
            hk.run(refs, pl.program_id(0), nb)
        heads = range(ATTN_N_Q)
        q, kb, vb = _attn_block_views(qlo_ref, qhi_ref, kc_ref, kp_ref, vc_ref, vp_ref)
        sink = [sink_ref[h:h + 1, 0:1] for h in heads]
        e, _, inv = _attn_softmax(_attn_scores(q, kb, _attn_valid(pl.program_id(0) == 0)), sink)
        out = [_dot_nn((e[h] * inv[h]).astype(BF16), vb[h // ATTN_REP]).astype(o_ref.dtype) for h in heads]
        o_ref[...] = jnp.concatenate(out, axis=1)

    qh = lambda half: pl.BlockSpec((w, ATTN_Q_HALF), lambda n: (n, half))
    kv = lambda blk, idx: pl.BlockSpec((w, ATTN_KV_W), lambda n: (idx(n), blk))
    cur = lambda n: n
    outs = pl.pallas_call(
        body, grid=(nb,),
        in_specs=[qh(0), qh(1), kv(ATTN_K_BLK, cur), kv(ATTN_K_BLK, prev), kv(ATTN_V_BLK, cur), kv(ATTN_V_BLK, prev),
                  pl.BlockSpec((ATTN_N_Q, LANES), lambda n: (0, 0))] + hk.in_specs,
        out_specs=[pl.BlockSpec((w, D_MODEL), lambda n: (n, 0))] + hk.out_specs,
        out_shape=[jax.ShapeDtypeStruct((t, D_MODEL), BF16)] + hk.out_shape,
        scratch_shapes=hk.scratch,
        compiler_params=_params(*hk.semantics("parallel")), name=name)(qkv, qkv, qkv, qkv, qkv, qkv, sinks_b, *hk.inputs)
    return outs[0] if hook is None else (outs[0], outs[1:])


def _attn_bwd(qkv, sinks_b, dout, *, name):
    t = qkv.shape[0]
    w = ATTN_WINDOW
    nb = t // w
    hd = ATTN_HEAD_DIM
    clamp = lambda n: jnp.minimum(n, nb - 1)
    prev = lambda n: jnp.maximum(clamp(n) - 1, 0)

    def body(qlo_ref, qhi_ref, kc_ref, kp_ref, vc_ref, vp_ref, sink_ref, dolo_ref, dohi_ref,
             dq_ref, dkv_ref, dsink_ref, carry):
        n = pl.program_id(0)

        @pl.when(n == 0)
        def _():
            carry[...] = jnp.zeros_like(carry)
            dsink_ref[...] = jnp.zeros_like(dsink_ref)

        @pl.when(n < nb)
        def _():
            heads, kvs = range(ATTN_N_Q), range(ATTN_N_KV)
            q, kb, vb = _attn_block_views(qlo_ref, qhi_ref, kc_ref, kp_ref, vc_ref, vp_ref)
            do = _attn_head_views(dolo_ref, dohi_ref)
            sink = [sink_ref[h:h + 1, 0:1] for h in heads]
            s = _attn_scores(q, kb, _attn_valid(n == 0))
            dp = [_dot_nt(do[h], vb[h // ATTN_REP]) for h in heads]
            e, es, inv = _attn_softmax(s, sink)
            p = [e[h] * inv[h] for h in heads]
            delta = [jnp.sum(p[h] * dp[h], axis=1, keepdims=True) for h in heads]
            dsc = [(p[h] * (dp[h] - delta[h]) * (hd ** -0.5)).astype(BF16) for h in heads]
            pb = [p[h].astype(BF16) for h in heads]
            dq = [_dot_nn(dsc[h], kb[h // ATTN_REP]).astype(dq_ref.dtype) for h in heads]
            stack = lambda per_head, kh: jnp.concatenate(per_head[kh * ATTN_REP:(kh + 1) * ATTN_REP], axis=0)
            dkb = [_dot_tn(stack(dsc, kh), stack(q, kh)) for kh in kvs]
            dvb = [_dot_tn(stack(pb, kh), stack(do, kh)) for kh in kvs]
            dsink = [jnp.broadcast_to(jnp.sum(-es[h] * inv[h] * delta[h], axis=0, keepdims=True), (1, LANES)) for h in heads]
            dq_ref[...] = jnp.concatenate(dq, axis=1)
            dsink_ref[...] += jnp.concatenate(dsink, axis=0)
            dkv_ref[...] = (carry[...] + jnp.concatenate([d[0:w, :] for d in dkb + dvb], axis=1)).astype(dkv_ref.dtype)
            carry[...] = jnp.concatenate([d[w:2 * w, :] for d in dkb + dvb], axis=1)

        @pl.when(n == nb)
        def _():
            dkv_ref[...] = carry[...].astype(dkv_ref.dtype)

    qh = lambda half: pl.BlockSpec((w, ATTN_Q_HALF), lambda n: (clamp(n), half))
    kv = lambda blk, idx: pl.BlockSpec((w, ATTN_KV_W), lambda n: (idx(n), blk))
    return pl.pallas_call(
        body, grid=(nb + 1,),
        in_specs=[qh(0), qh(1), kv(ATTN_K_BLK, clamp), kv(ATTN_K_BLK, prev), kv(ATTN_V_BLK, clamp), kv(ATTN_V_BLK, prev),
                  pl.BlockSpec((ATTN_N_Q, LANES), lambda n: (0, 0)), qh(0), qh(1)],
        out_specs=[pl.BlockSpec((w, D_MODEL), lambda n: (clamp(n), 0)),
                   pl.BlockSpec((w, 2 * ATTN_KV_W), lambda n: (jnp.maximum(n - 1, 0), 0)),
                   pl.BlockSpec((ATTN_N_Q, LANES), lambda n: (0, 0))],
        out_shape=[jax.ShapeDtypeStruct((t, D_MODEL), BF16), jax.ShapeDtypeStruct((t, 2 * ATTN_KV_W), BF16),
                   jax.ShapeDtypeStruct((ATTN_N_Q, LANES), F32)],
        scratch_shapes=[pltpu.VMEM((w, 2 * ATTN_KV_W), F32)],
        compiler_params=_params("arbitrary"), name=name)(qkv, qkv, qkv, qkv, qkv, qkv, sinks_b, dout, dout)


def _sq_relu_epilogue(acc):
    r = jnp.maximum(acc, 0.0)
    return (r * r,)


def _sq_relu_bwd_epilogue(acc, act):
    return (acc * (2.0 * jnp.sqrt(act.astype(F32))),)


def _bias_epilogue(acc, bias):
    return (acc + bias,)


def _plain_run(stage, fn, *args, **kwargs):
    return fn(*args, **kwargs)


def _mlp_fwd(u, w_up, w_down, tag, run=_plain_run):
    act = run(f"mlp_up_{tag}", _matmul, u, w_up, mode="nn", out_dtypes=(BF16,), epilogue=_sq_relu_epilogue, b_shards=True,
              tm=BIG_TILE, name=f"mlp_up_{tag}")
    f = run(f"mlp_down_{tag}", _matmul, act, w_down, mode="nn", out_dtypes=(BF16,), tk=BIG_TILE, name=f"mlp_down_{tag}")
    return act, f


def _mlp_bwd(u, act, w_up, w_down, df, tag):
    dpre = _matmul(df, w_down, mode="nt", out_dtypes=(BF16,), epilogue=_sq_relu_bwd_epilogue,
                   extras=((act, "tile"),), name=f"mlp_dact_{tag}")
    dw_down = _matmul(act, df, mode="tn", out_dtypes=(BF16,), tk=BIG_TILE, name=f"mlp_dwdown_{tag}")
    du = _matmul(dpre, w_up, mode="nt", out_dtypes=(BF16,), b_shards=True, tm=BIG_TILE, name=f"mlp_du_{tag}")
    dw_up = _matmul(u, dpre, mode="tn", out_dtypes=(BF16,), out_shards=True, tk=BIG_TILE, name=f"mlp_dwup_{tag}")
    return du, dw_up, dw_down


def _head_param_rows(p):
    return jnp.broadcast_to(p.reshape(SSD_N_GROUPS, SSD_HPG, 1), (SSD_N_GROUPS, SSD_HPG, LANES))


def _local_step(x, target, wts, comm=None, u0=None):
    wts = dict(wts)
    row = lambda v: v.reshape(1, -1)
    mix_pre, mix_post, ffn_pre, ffn_post = wts["mix_pre_norm"], wts["mix_post_norm"], wts["ffn_pre_norm"], wts["ffn_post_norm"]

    def gathering(stage, fn, *args, **kwargs):
        hook = comm.gather_hook(stage) if comm is not None else None
        if hook is None:
            return fn(*args, **kwargs)
        out, got = fn(*args, hook=hook, **kwargs)
        wts.update(comm.weights_from(stage, got))
        return out

    if u0 is None:
        u0 = _rms_fwd(x, row(mix_pre[0]), name="rms_pre_mix0")
    zx, dt_raw = gathering("in_proj", _matmul, u0, wts["ssd_w_in"], mode="nn", out_dtypes=(BF16,), tn=SSD_IN_TILE,
                           f32_block=SSD_DT_COL - (SSD_IN_PAD - SSD_IN_TILE),
                           name="ssd_in_proj")
    xc = gathering("conv", _conv_fwd, zx, wts["ssd_conv_w"], row(wts["ssd_conv_b"]), name="ssd_conv_fwd")
    bias_row = jnp.pad(wts["ssd_dt_bias"], (0, LANES - SSD_N_HEADS)).reshape(1, LANES)
    alog_row = jnp.pad(wts["ssd_a_log"], (0, LANES - SSD_N_HEADS)).reshape(1, LANES)
    dtr, cumr = _softplus_fwd(dt_raw, bias_row, alog_row, name="ssd_dt_fwd")
    alog_b, d_b = _head_param_rows(wts["ssd_a_log"]), _head_param_rows(wts["ssd_d"])
    y_ssd, states = gathering("scan", _ssd_fwd, xc, dtr, cumr, alog_b, d_b, name="ssd_scan_fwd")
    norm_w = row(wts["ssd_norm_w"])
    yn = _gate_norm_fwd(y_ssd, zx, norm_w, name="ssd_gate_norm_fwd")
    mix0 = _matmul(yn, wts["ssd_w_out"], mode="nn", out_dtypes=(BF16,), tk=BIG_TILE, name="ssd_out_proj")
    h1, v0 = _rms_fwd(mix0, row(mix_post[0]), resid=x, want_u=row(ffn_pre[0]), name="rms_post_mix0")
    act0, f0 = _mlp_fwd(v0, wts["mlp_w_up0"], wts["mlp_w_down0"], "l0", run=gathering)
    h2, u1 = _rms_fwd(f0, row(ffn_post[0]), resid=h1, want_u=row(mix_pre[1]), name="rms_post_ffn0")

    qkv = _matmul(u1, wts["attn_w_qkv"], mode="nn", out_dtypes=(BF16,), epilogue=_bias_epilogue,
                  extras=((row(wts["attn_b_qkv"]), "row"),), b_shards=True, name="attn_qkv_proj")
    sinks_b = jnp.broadcast_to(wts["attn_sinks"].reshape(ATTN_N_Q, 1), (ATTN_N_Q, LANES))
    ao = gathering("attn_fwd", _attn_fwd, qkv, sinks_b, name="attn_fwd")
    mix1 = _matmul(ao, wts["attn_w_o"], mode="nn", out_dtypes=(BF16,), epilogue=_bias_epilogue,
                   extras=((row(wts["attn_b_o"]), "row"),), name="attn_out_proj")
    h3, v1 = _rms_fwd(mix1, row(mix_post[1]), resid=h2, want_u=row(ffn_pre[1]), name="rms_post_mix1")
    act1, f1 = _mlp_fwd(v1, wts["mlp_w_up1"], wts["mlp_w_down1"], "l1")
    dh4, loss_tile = _rms_fwd(f1, row(ffn_post[1]), resid=h3, target=target, name="rms_post_ffn1_loss")

    df1, g_ffn_post1 = _rms_bwd(f1, row(ffn_post[1]), dh4, out_dtype=BF16, name="rms_post_ffn1_bwd")
    dv1, g_up1, g_down1 = _mlp_bwd(v1, act1, wts["mlp_w_up1"], wts["mlp_w_down1"], df1, "l1")
    dh3, g_ffn_pre1 = _rms_bwd(h3, row(ffn_pre[1]), dv1, resid=dh4, name="rms_pre_ffn1_bwd")
    dmix1, g_mix_post1, g_b_o = _rms_bwd(mix1, row(mix_post[1]), dh3, out_dtype=BF16, dx_col_sum=True, name="rms_post_mix1_bwd")
    g_w_o = _matmul(ao, dmix1, mode="tn", out_dtypes=(BF16,), tk=BIG_TILE, name="attn_dwo")
    dao = _matmul(dmix1, wts["attn_w_o"], mode="nt", out_dtypes=(BF16,), name="attn_dao")
    dq, dkv, g_sinks = _attn_bwd(qkv, sinks_b, dao, name="attn_bwd")
    dqkv = jnp.concatenate([dq, dkv], axis=1)
    g_b_qkv = _col_sum(dqkv, name="attn_bqkv_grad")
    g_w_qkv = _matmul(u1, dqkv, mode="tn", out_dtypes=(BF16,), tn=ATTN_QKV // N_CHIPS, out_shards=True, tk=BIG_TILE, name="attn_dwqkv")
    du1 = _matmul(dqkv, wts["attn_w_qkv"], mode="nt", out_dtypes=(BF16,), b_shards=True, name="attn_du")
    dh2, g_mix_pre1 = _rms_bwd(h2, row(mix_pre[1]), du1, resid=dh3, name="rms_pre_mix1_bwd")

    df0, g_ffn_post0 = _rms_bwd(f0, row(ffn_post[0]), dh2, out_dtype=BF16, name="rms_post_ffn0_bwd")
    dv0, g_up0, g_down0 = _mlp_bwd(v0, act0, wts["mlp_w_up0"], wts["mlp_w_down0"], df0, "l0")
    dh1, g_ffn_pre0 = _rms_bwd(h1, row(ffn_pre[0]), dv0, resid=dh2, name="rms_pre_ffn0_bwd")
    dmix0, g_mix_post0 = _rms_bwd(mix0, row(mix_post[0]), dh1, out_dtype=BF16, name="rms_post_mix0_bwd")
    g_w_out = _matmul(yn, dmix0, mode="tn", out_dtypes=(BF16,), tk=BIG_TILE, name="ssd_dwout")
    dyn = _matmul(dmix0, wts["ssd_w_out"], mode="nt", out_dtypes=(BF16,), name="ssd_dyn")
    dy_ssd, dzx, g_norm_w = _gate_norm_bwd(y_ssd, zx, norm_w, dyn, name="ssd_gate_norm_bwd")
    mats = {"ssd_w_out": g_w_out, "attn_w_qkv": g_w_qkv, "attn_w_o": g_w_o,
            "mlp_w_up0": g_up0, "mlp_w_up1": g_up1, "mlp_w_down0": g_down0, "mlp_w_down1": g_down1}
    if comm is None:
        dxc, dbm, dcm, ddt_r, dpar = _ssd_bwd(xc, dtr, cumr, alog_b, d_b, states, dy_ssd, name="ssd_scan_bwd")
    else:
        (dxc, dbm, dcm, ddt_r, dpar), received = _ssd_bwd(xc, dtr, cumr, alog_b, d_b, states, dy_ssd,
                                                          name="ssd_scan_bwd", hook=comm.exchange_hook(mats, "early"))
        comm.received(received)
    dzx, g_conv_w, g_conv_b = _conv_bwd(zx, wts["ssd_conv_w"], row(wts["ssd_conv_b"]), dxc, dbm, dcm, dzx, name="ssd_conv_bwd")
    dzx, g_dt_bias = _softplus_bwd(dt_raw, bias_row, ddt_r, dzx, name="ssd_dt_bwd")
    g_w_in = _w_in_to_shards(_matmul(u0, dzx, mode="tn", out_dtypes=(BF16,), tn=SSD_IN_TILE, tk=BIG_TILE, name="ssd_dwin"), name="ssd_dwin_shards")
    mats["ssd_w_in"] = g_w_in
    if comm is None:
        du0 = _matmul(dzx, wts["ssd_w_in"], mode="nt", out_dtypes=(BF16,), tk=SSD_IN_TILE, name="ssd_du")
    else:
        du0, received = _matmul(dzx, wts["ssd_w_in"], mode="nt", out_dtypes=(BF16,), tk=SSD_IN_TILE, name="ssd_du",
                                hook=comm.exchange_hook(mats, "late"))
        comm.received(received)
    grad_x, g_mix_pre0 = _rms_bwd(x, row(mix_pre[0]), du0, resid=dh1, name="rms_pre_mix0_bwd")

    dpar = dpar.reshape(SSD_N_HEADS, LANES)
    vecs = {
        "ssd_conv_w": g_conv_w, "ssd_conv_b": g_conv_b.reshape(-1),
        "ssd_dt_bias": g_dt_bias[0, :SSD_N_HEADS], "ssd_a_log": dpar[:, 0], "ssd_d": dpar[:, 1],
        "ssd_norm_w": g_norm_w.reshape(-1), "attn_b_qkv": g_b_qkv.reshape(-1), "attn_sinks": g_sinks[:, 0],
        "attn_b_o": g_b_o.reshape(-1),
        "mix_pre_norm": jnp.concatenate([g_mix_pre0, g_mix_pre1]), "mix_post_norm": jnp.concatenate([g_mix_post0, g_mix_post1]),
        "ffn_pre_norm": jnp.concatenate([g_ffn_pre0, g_ffn_pre1]), "ffn_post_norm": jnp.concatenate([g_ffn_post0, g_ffn_post1]),
    }
    return loss_tile, grad_x, mats, vecs


def _mesh_position():
    return lax.axis_index("x"), lax.axis_index("y"), lax.axis_index("c")


def _flip(v, bit):
    return 1 - v if bit else v


OTHER_CHIPS = ((1, 0), (0, 1), (1, 1))


def _comm_params():
    return pltpu.CompilerParams(vmem_limit_bytes=VMEM_LIMIT)


def _staged_copies(srcs, dsts, bufs, sems_in, sems_out):
    loads = [pltpu.make_async_copy(s, b, sems_in.at[i]) for i, (s, b) in enumerate(zip(srcs, bufs))]
    stores = [pltpu.make_async_copy(b, d, sems_out.at[i]) for i, (b, d) in enumerate(zip(bufs, dsts))]
    return loads, stores


class _GatherHook:
    def __init__(self, mats, vecs=()):
        self.arrs = list(mats) + list(vecs)
        self.nm, self.n = len(mats), len(self.arrs)
        n_ici, n_fwd = (N_CHIPS - 1) * self.n, max((N_CHIPS - 1) * self.nm, 1)
        dma = pltpu.SemaphoreType.DMA
        self.out_shape = [jax.ShapeDtypeStruct((N_CHIPS,) + a.shape, a.dtype) for a in self.arrs]
        self.scratch = [pltpu.VMEM(a.shape, a.dtype) for a in self.arrs] + [
            dma((n_ici,)), dma((n_ici,)), dma((n_fwd,)), dma((n_fwd,)), dma((self.n,)), dma((self.n,))]

    def plan(self, ins, outs, scratch):
        n, nm = self.n, self.nm
        bufs = scratch[:n]
        ici_send, ici_recv, fwd_send, fwd_recv, load_sems, store_sems = scratch[n:]
        xi, yi, ci = _mesh_position()
        me = 2 * xi + yi
        loads, stores = _staged_copies(ins, [outs[i].at[me] for i in range(n)], bufs, load_sems, store_sems)
        sends, landed, forwards, from_sibling = [], [], [], []
        for j, (bx, by) in enumerate(OTHER_CHIPS):
            px, py = _flip(xi, bx), _flip(yi, by)
            peer = 2 * px + py
            for i in range(n):
                k = j * n + i
                mk = functools.partial(pltpu.make_async_remote_copy, send_sem=ici_send.at[k], recv_sem=ici_recv.at[k],
                                       device_id=(px, py, ci), device_id_type=MESH)
                if i < nm:
                    sends.append(mk(src_ref=ins[i].at[ci], dst_ref=outs[i].at[me, ci]))
                    landed.append(mk(src_ref=ins[i].at[ci], dst_ref=outs[i].at[peer, ci]))
                    kf = j * nm + i
                    fw = functools.partial(pltpu.make_async_remote_copy, send_sem=fwd_send.at[kf], recv_sem=fwd_recv.at[kf],
                                           device_id=(xi, yi, 1 - ci), device_id_type=MESH)
                    forwards.append(fw(src_ref=outs[i].at[peer, ci], dst_ref=outs[i].at[peer, ci]))
                    from_sibling.append(fw(src_ref=outs[i].at[peer, ci], dst_ref=outs[i].at[peer, 1 - ci]))
                else:
                    sends.append(mk(src_ref=ins[i], dst_ref=outs[i].at[me]))
                    landed.append(mk(src_ref=ins[i], dst_ref=outs[i].at[peer]))
                    forwards.append(None)
        return loads, stores, sends, landed, forwards, from_sibling

    @staticmethod
    def start(p):
        loads, _, sends, _, _, _ = p
        for cp in loads + sends:
            cp.start()

    @staticmethod
    def relay(p):
        loads, stores, _, landed, forwards, _ = p
        for ld, st in zip(loads, stores):
            ld.wait()
            st.start()
        for cp, fw in zip(landed, forwards):
            cp.wait_recv()
            if fw is not None:
                fw.start()

    @staticmethod
    def finish(p):
        _, stores, sends, _, forwards, from_sibling = p
        for cp in from_sibling:
            cp.wait_recv()
        for cp in sends + [fw for fw in forwards if fw is not None]:
            cp.wait_send()
        for st in stores:
            st.wait()


def _run_hook(hook, ins, outs, scratch, step, n_steps):
    p = hook.plan(ins, outs, scratch)
    relay_step = min(max(1, (3 * n_steps) // 4), n_steps - 1)

    @pl.when(step == 0)
    def _():
        hook.start(p)

    if relay_step < n_steps - 1:
        @pl.when(step == relay_step)
        def _():
            hook.relay(p)

    @pl.when(step == n_steps - 1)
    def _():
        if relay_step == n_steps - 1:
            hook.relay(p)
        hook.finish(p)


def _hook_call(hook, *, name):
    n = len(hook.arrs)

    def body(*refs):
        p = hook.plan(refs[:n], refs[n:n + len(hook.out_shape)], refs[n + len(hook.out_shape):])
        hook.start(p)
        hook.relay(p)
        hook.finish(p)

    return pl.pallas_call(
        body, in_specs=[ANY] * n, out_specs=[ANY] * len(hook.out_shape), out_shape=hook.out_shape,
        scratch_shapes=hook.scratch, compiler_params=_comm_params(), name=name)(*hook.arrs)


def _send_other_half(parts, *, name):
    n = len(parts)

    def body(*refs):
        ins, outs = refs[:n], refs[n:2 * n]
        send_sems, recv_sems = refs[2 * n:]
        xi, yi, ci = _mesh_position()
        sibling = (xi, yi, 1 - ci)
        for i in range(n):
            for s in range(N_CHIPS):
                pltpu.make_async_remote_copy(src_ref=ins[i].at[s, 1 - ci], dst_ref=outs[i].at[s], send_sem=send_sems.at[i],
                                             recv_sem=recv_sems.at[i], device_id=sibling, device_id_type=MESH).start()
        for i in range(n):
            pltpu.make_async_remote_copy(src_ref=outs[i], dst_ref=outs[i], send_sem=send_sems.at[i], recv_sem=recv_sems.at[i],
                                         device_id=sibling, device_id_type=MESH).wait()

    return pl.pallas_call(
        body, in_specs=[ANY] * n, out_specs=[ANY] * n,
        out_shape=[jax.ShapeDtypeStruct((p.shape[0],) + p.shape[2:], p.dtype) for p in parts],
        scratch_shapes=[pltpu.SemaphoreType.DMA((n,)), pltpu.SemaphoreType.DMA((n,))],
        name=name)(*parts)


ROW_BLOCKS = 8
SUM_ROW_BLOCKS = 2


def _add_sibling_half(parts, theirs, core, *, name):
    n = len(parts)

    def body(core_ref, *refs):
        for a_ref, b_ref, o_ref in zip(refs[:n], refs[n:2 * n], refs[2 * n:]):
            o_ref[...] = (a_ref[...].astype(F32) + b_ref[...].astype(F32)).astype(o_ref.dtype)

    nb = SUM_ROW_BLOCKS
    mine = lambda p: pl.BlockSpec((None, None, p.shape[2] // nb, p.shape[3]), lambda s, rb, core_ref: (s, core_ref[0], rb, 0))
    other = lambda p: pl.BlockSpec((None, p.shape[1] // nb, p.shape[2]), lambda s, rb, core_ref: (s, rb, 0))
    return pl.pallas_call(
        body,
        grid_spec=pltpu.PrefetchScalarGridSpec(
            num_scalar_prefetch=1, grid=(N_CHIPS, nb),
            in_specs=[mine(p) for p in parts] + [other(q) for q in theirs], out_specs=[other(q) for q in theirs]),
        out_shape=[jax.ShapeDtypeStruct(q.shape, BF16) for q in theirs],
        compiler_params=_params("parallel", "parallel"), name=name)(core, *parts, *theirs)


class _ExchangeHook:
    def __init__(self, parts, to_all=()):
        self.arrs = list(parts) + list(to_all)
        self.n_parts, self.n = len(parts), len(self.arrs)
        n_ici, n_peer = max((N_CHIPS - 1) * self.n_parts, 1), (N_DEV - 1) * max(len(to_all), 1)
        dma = pltpu.SemaphoreType.DMA
        self.out_shape = [jax.ShapeDtypeStruct(p.shape, p.dtype) for p in parts] + [
            jax.ShapeDtypeStruct((N_DEV,) + a.shape, a.dtype) for a in to_all]
        self.scratch = [pltpu.VMEM(p.shape[1:], p.dtype) for p in parts] + [pltpu.VMEM(a.shape, a.dtype) for a in to_all] + [
            dma((n_ici,)), dma((n_ici,)), dma((n_peer,)), dma((n_peer,)), dma((self.n,)), dma((self.n,))]

    def plan(self, ins, outs, scratch):
        n, npt = self.n, self.n_parts
        bufs = scratch[:n]
        send_sems, recv_sems, all_send, all_recv, load_sems, store_sems = scratch[n:]
        xi, yi, ci = _mesh_position()
        me_chip = 2 * xi + yi
        me = 4 * xi + 2 * yi + ci
        loads, stores = _staged_copies([ins[i].at[me_chip] for i in range(npt)] + list(ins[npt:]),
                                       [outs[i].at[me_chip] for i in range(npt)] + [outs[i].at[me] for i in range(npt, n)],
                                       bufs, load_sems, store_sems)
        sends, recvs = [], []
        for j, (bx, by) in enumerate(OTHER_CHIPS):
            px, py = _flip(xi, bx), _flip(yi, by)
            peer = 2 * px + py
            for i in range(npt):
                k = j * npt + i
                mk = functools.partial(pltpu.make_async_remote_copy, src_ref=ins[i].at[peer], send_sem=send_sems.at[k],
                                       recv_sem=recv_sems.at[k], device_id=(px, py, ci), device_id_type=MESH)
                sends.append(mk(dst_ref=outs[i].at[me_chip]))
                recvs.append(mk(dst_ref=outs[i].at[peer]))
        for i in range(npt, n):
            for k in range(1, N_DEV):
                px, py, pc = _flip(xi, (k >> 2) & 1), _flip(yi, (k >> 1) & 1), _flip(ci, k & 1)
                slot = (i - npt) * (N_DEV - 1) + k - 1
                mk = functools.partial(pltpu.make_async_remote_copy, src_ref=ins[i], send_sem=all_send.at[slot],
                                       recv_sem=all_recv.at[slot], device_id=(px, py, pc), device_id_type=MESH)
                sends.append(mk(dst_ref=outs[i].at[me]))
                recvs.append(mk(dst_ref=outs[i].at[4 * px + 2 * py + pc]))
        return loads, stores, sends, recvs

    @staticmethod
    def start(p):
        loads, _, sends, _ = p
        for cp in loads + sends:
            cp.start()

    @staticmethod
    def relay(p):
        loads, stores, _, _ = p
        for ld, st in zip(loads, stores):
            ld.wait()
            st.start()

    @staticmethod
    def finish(p):
        _, stores, sends, recvs = p
        for cp in recvs:
            cp.wait_recv()
        for cp in sends:
            cp.wait_send()
        for st in stores:
            st.wait()


def _sum_chips(parts, *, name):
    n = len(parts)
    p = parts[0].shape[0]

    def body(*refs):
        s = pl.program_id(1)
        for x_ref, o_ref in zip(refs[:n], refs[n:]):
            @pl.when(s == 0)
            def _():
                o_ref[...] = x_ref[...].astype(F32)

            @pl.when(s > 0)
            def _():
                o_ref[...] += x_ref[...].astype(F32)

    blocks = lambda q: SUM_ROW_BLOCKS if q.shape[1] % (16 * SUM_ROW_BLOCKS) == 0 else 1
    assert len({blocks(q) for q in parts}) == 1
    nb = blocks(parts[0])
    return pl.pallas_call(
        body, grid=(nb, p),
        in_specs=[pl.BlockSpec((None, q.shape[1] // nb, q.shape[2]), lambda rb, s: (s, rb, 0)) for q in parts],
        out_specs=[pl.BlockSpec((q.shape[1] // nb, q.shape[2]), lambda rb, s: (rb, 0)) for q in parts],
        out_shape=[jax.ShapeDtypeStruct(q.shape[1:], F32) for q in parts],
        compiler_params=_params("parallel", "arbitrary"), name=name)(*parts)


def _swap_halves(halves, layers, *, name, hook=None):
    n = len(halves)
    out_shapes, slots = [], []
    for i, h in enumerate(halves):
        pair = [p for p in layers if i in p]
        if pair and pair[0][1] == i:
            slots.append((slots[pair[0][0]][0], 1))
        elif pair:
            out_shapes.append(jax.ShapeDtypeStruct((2, 2) + h.shape, h.dtype))
            slots.append((len(out_shapes) - 1, 0))
        else:
            out_shapes.append(jax.ShapeDtypeStruct((2,) + h.shape, h.dtype))
            slots.append((len(out_shapes) - 1, None))
    n_out = len(out_shapes)
    hk = _HookSlots(hook, n_in=n, n_out=n_out, n_scratch=n + 4)

    def body(*refs):
        ins, outs, scratch = hk.own(refs)
        bufs = scratch[:n]
        send_sems, recv_sems, load_sems, store_sems = scratch[n:]
        if hook is not None:
            extra = hk.plan(refs)
            hook.start(extra)
        xi, yi, ci = _mesh_position()
        own, sends, recvs = [], [], []
        for i in range(n):
            o, layer = slots[i]
            dst = (lambda core: outs[o].at[core]) if layer is None else (lambda core: outs[o].at[layer, core])
            own.append(dst(ci))
            mk = functools.partial(pltpu.make_async_remote_copy, src_ref=ins[i], send_sem=send_sems.at[i],
                                   recv_sem=recv_sems.at[i], device_id=(xi, yi, 1 - ci), device_id_type=MESH)
            sends.append(mk(dst_ref=dst(ci)))
            recvs.append(mk(dst_ref=dst(1 - ci)))
        loads, stores = _staged_copies(ins, own, bufs, load_sems, store_sems)
        for cp in loads + sends:
            cp.start()
        for ld, st in zip(loads, stores):
            ld.wait()
            st.start()
        for cp in recvs:
            cp.wait_recv()
        for cp in sends:
            cp.wait_send()
        for st in stores:
            st.wait()
        if hook is not None:
            hook.relay(extra)
            hook.finish(extra)

    outs = pl.pallas_call(
        body, in_specs=[ANY] * n + hk.in_specs, out_specs=[ANY] * n_out + hk.out_specs, out_shape=out_shapes + hk.out_shape,
        scratch_shapes=[pltpu.VMEM(h.shape, h.dtype) for h in halves]
        + [pltpu.SemaphoreType.DMA((n,)), pltpu.SemaphoreType.DMA((n,)), pltpu.SemaphoreType.DMA((n,)), pltpu.SemaphoreType.DMA((n,))]
        + hk.scratch,
        compiler_params=_comm_params(), name=name)(*halves, *hk.inputs)
    return outs if hook is None else (outs[:n_out], outs[n_out:])


def _cast_bf16(layers, x, norm_w, *, name, hook=None):
    n = len(layers)
    hk = _HookSlots(hook, n_in=n + 2, n_out=n + 1, n_scratch=0)

    def body(*refs):
        ins, outs, _ = hk.own(refs)
        if hook is not None:
            hk.run(refs, pl.program_id(0), ROW_BLOCKS)
        for i_ref, o_ref in zip(ins[:n], outs[:n]):
            o_ref[...] = i_ref[...].astype(o_ref.dtype)
        xv = ins[n][...]
        outs[n][...] = (xv * lax.rsqrt(jnp.mean(xv * xv, axis=-1, keepdims=True) + NORM_EPS) * ins[n + 1][...]).astype(BF16)

    in_blk = lambda a, l: pl.BlockSpec((None, a.shape[1] // ROW_BLOCKS, a.shape[2]), lambda i: (l, i, 0))
    out_blk = lambda a: pl.BlockSpec((a.shape[1] // ROW_BLOCKS, a.shape[2]), lambda i: (i, 0))
    x_blk = pl.BlockSpec((x.shape[0] // ROW_BLOCKS, x.shape[1]), lambda i: (i, 0))
    outs = pl.pallas_call(
        body, grid=(ROW_BLOCKS,),
        in_specs=[in_blk(a, l) for a, l in layers] + [x_blk, pl.BlockSpec((1, x.shape[1]), lambda i: (0, 0))] + hk.in_specs,
        out_specs=[out_blk(a) for a, _ in layers] + [x_blk] + hk.out_specs,
        out_shape=[jax.ShapeDtypeStruct(a.shape[1:], BF16) for a, _ in layers] + [jax.ShapeDtypeStruct(x.shape, BF16)] + hk.out_shape,
        scratch_shapes=hk.scratch,
        compiler_params=_params(*hk.semantics("parallel")), name=name)(*[a for a, _ in layers], x, norm_w, *hk.inputs)
    own = (outs[:n], outs[n])
    return own if hook is None else (own, outs[n + 1:])


def _full_weight(name, gathered):
    s, _, r, c = gathered.shape
    if name == "ssd_w_in":
        return _w_in_from_shards(gathered.reshape(s, 2 * r, c), name="ssd_w_in_unshard")
    if name in ("attn_w_qkv", "mlp_w_up0", "mlp_w_up1"):
        return gathered.reshape(s, 2 * r, c)
    return gathered.reshape(s * 2 * r, c)


class _StepComm:
    GATHER = {"in_proj": ("mlp_w_up0", "attn_w_o"), "conv": ("mlp_w_down0",), "scan": ("ssd_w_out", "mlp_w_up1"),
              "mlp_up_l0": ("attn_w_qkv",), "attn_fwd": ("mlp_w_down1",)}
    EXCHANGE = {"early": ("ssd_w_out", "attn_w_qkv", "attn_w_o", "mlp_w_up0", "mlp_w_up1", "mlp_w_down0", "mlp_w_down1"),
                "late": ("ssd_w_in",)}

    def __init__(self, shards, core):
        self.shards, self.core = shards, core
        self.chip_parts = {}
        self._pending = None

    def gather_hook(self, stage):
        names = self.GATHER.get(stage)
        return _GatherHook([self.shards[n] for n in names]) if names else None

    def weights_from(self, stage, gathered):
        return {n: _full_weight(n, g) for n, g in zip(self.GATHER[stage], gathered)}

    def chip_sums(self, mats, tag):
        parts = [_shard_halves(a) for a in mats.values()]
        theirs = _send_other_half(parts, name=f"grad_sibling_send_{tag}")
        return _add_sibling_half(parts, theirs, self.core, name=f"grad_chip_sum_{tag}")

    def exchange_hook(self, mats, which):
        self._pending = self.EXCHANGE[which]
        return _ExchangeHook(self.chip_sums({n: mats[n] for n in self._pending}, which))

    def received(self, arrays):
        self.chip_parts.update(zip(self._pending, arrays))


ADAMW_ROW_BLOCKS = 16


def _adamw(ws, gs, ms, vs, *, name, by_lanes=False):
    n = len(ws)
    if by_lanes:
        nb = min(a.shape[2] for a in ws) // LANES
    else:
        nb = ADAMW_ROW_BLOCKS if all(a.shape[1] % (8 * ADAMW_ROW_BLOCKS) == 0 for a in ws) else 1

    def body(*refs):
        ins, outs = refs[:4 * n], refs[4 * n:]
        for i in range(n):
            w_ref, g_ref, m_ref, v_ref = ins[i], ins[n + i], ins[2 * n + i], ins[3 * n + i]
            go_ref, d_ref, nm_ref, nv_ref = outs[i], outs[n + i], outs[2 * n + i], outs[3 * n + i]
            gv = g_ref[...]
            nm = ADAM_B1 * m_ref[...] + (1.0 - ADAM_B1) * gv
            nv = ADAM_B2 * v_ref[...] + (1.0 - ADAM_B2) * (gv * gv)
            m_hat = nm / (1.0 - ADAM_B1 ** ADAM_STEP)
            v_hat = nv / (1.0 - ADAM_B2 ** ADAM_STEP)
            go_ref[...] = gv
            d_ref[...] = -ADAM_LR * (m_hat / (jnp.sqrt(v_hat) + ADAM_EPS) + ADAM_WD * w_ref[...])
            nm_ref[...] = nm
            nv_ref[...] = nv

    if by_lanes:
        blks = [pl.BlockSpec((a.shape[0], a.shape[1], a.shape[2] // nb), lambda i: (0, 0, i)) for a in ws]
    else:
        blks = [pl.BlockSpec((a.shape[0], a.shape[1] // nb, a.shape[2]), lambda i: (0, i, 0)) for a in ws]
    shapes = [jax.ShapeDtypeStruct(a.shape, F32) for a in ws]
    outs = pl.pallas_call(body, grid=(nb,), in_specs=blks * 4, out_specs=blks * 4, out_shape=shapes * 4,
                          compiler_params=_params("parallel"), name=name)(*ws, *gs, *ms, *vs)
    return [tuple(outs[k * n + i] for k in range(4)) for i in range(n)]


SM_CONV_B, SM_NORM_W, SM_MIX_PRE, SM_MIX_POST, SM_FFN_PRE, SM_FFN_POST, SM_MISC, SM_CONV_W, SM_B_QKV, SM_B_O = 0, 4, 6, 8, 10, 12, 14, 16, 32, 34
SM_ROWS = 40
MISC_DT_BIAS, MISC_A_LOG, MISC_D, MISC_SINKS, MISC_LOSS = 0, 32, 64, 96, 112


def _shard_halves(a):
    c = a.shape[-1]
    return a.reshape(N_CHIPS, 2, -1, c)


def _rows(v):
    return v.reshape(-1, D_MODEL)


def _misc_row(dt_bias, a_log, d, sinks, loss):
    pad = jnp.zeros((D_MODEL - MISC_LOSS - 1,), F32)
    return jnp.concatenate([dt_bias.reshape(-1), a_log.reshape(-1), d.reshape(-1), sinks.reshape(-1), loss.reshape(1), pad]).reshape(1, D_MODEL)


def _replicated_rows(p, loss):
    return jnp.concatenate([
        _rows(p["ssd_conv_b"]), _rows(p["ssd_norm_w"]), _rows(p["mix_pre_norm"]), _rows(p["mix_post_norm"]),
        _rows(p["ffn_pre_norm"]), _rows(p["ffn_post_norm"]),
        _misc_row(p["ssd_dt_bias"], p["ssd_a_log"], p["ssd_d"], p["attn_sinks"], loss), jnp.zeros((1, D_MODEL), F32)], axis=0)


def _sharded_rows(conv_w, b_qkv, b_o):
    last = jnp.concatenate([b_qkv.reshape(-1), b_o.reshape(-1), jnp.zeros((D_MODEL - 640,), F32)]).reshape(1, D_MODEL)
    return jnp.concatenate([conv_w.reshape(SSD_CONV_WIDTH, D_MODEL), last, jnp.zeros((3, D_MODEL), F32)], axis=0)


REPLICATED = ("ssd_conv_b", "ssd_dt_bias", "ssd_a_log", "ssd_d", "ssd_norm_w", "attn_sinks",
              "mix_pre_norm", "mix_post_norm", "ffn_pre_norm", "ffn_post_norm")
MATRICES = ("ssd_w_in", "ssd_w_out", "attn_w_qkv", "attn_w_o", "mlp_w_up", "mlp_w_down")
WEIGHT_NAMES = ("ssd_w_in", "ssd_conv_w", "ssd_conv_b", "ssd_dt_bias", "ssd_a_log", "ssd_d", "ssd_norm_w", "ssd_w_out",
                "attn_w_qkv", "attn_b_qkv", "attn_sinks", "attn_w_o", "attn_b_o", "mlp_w_up", "mlp_w_down",
                "mix_pre_norm", "mix_post_norm", "ffn_pre_norm", "ffn_post_norm")


def _unpack_small(rows16, rows8, like):
    misc = rows16[SM_MISC]
    out = {
        "ssd_conv_b": rows16[SM_CONV_B:SM_CONV_B + 4], "ssd_norm_w": rows16[SM_NORM_W:SM_NORM_W + 2],
        "mix_pre_norm": rows16[SM_MIX_PRE:SM_MIX_PRE + 2], "mix_post_norm": rows16[SM_MIX_POST:SM_MIX_POST + 2],
        "ffn_pre_norm": rows16[SM_FFN_PRE:SM_FFN_PRE + 2], "ffn_post_norm": rows16[SM_FFN_POST:SM_FFN_POST + 2],
        "ssd_dt_bias": misc[MISC_DT_BIAS:MISC_DT_BIAS + 32], "ssd_a_log": misc[MISC_A_LOG:MISC_A_LOG + 32],
        "ssd_d": misc[MISC_D:MISC_D + 32], "attn_sinks": misc[MISC_SINKS:MISC_SINKS + 16],
        "ssd_conv_w": rows8[0:SSD_CONV_WIDTH], "attn_b_qkv": rows8[SSD_CONV_WIDTH, 0:384], "attn_b_o": rows8[SSD_CONV_WIDTH, 384:640],
    }
    return {k: v.reshape(like[k].shape) for k, v in out.items()}


def kernel(x, ssd_w_in, ssd_conv_w, ssd_conv_b, ssd_dt_bias, ssd_a_log, ssd_d, ssd_norm_w, ssd_w_out, attn_w_qkv, attn_b_qkv, attn_sinks, attn_w_o, attn_b_o, mlp_w_up, mlp_w_down, mix_pre_norm, mix_post_norm, ffn_pre_norm, ffn_post_norm, loss_target, m_ssd_w_in, m_ssd_conv_w, m_ssd_conv_b, m_ssd_dt_bias, m_ssd_a_log, m_ssd_d, m_ssd_norm_w, m_ssd_w_out, m_attn_w_qkv, m_attn_b_qkv, m_attn_sinks, m_attn_w_o, m_attn_b_o, m_mlp_w_up, m_mlp_w_down, m_mix_pre_norm, m_mix_post_norm, m_ffn_pre_norm, m_ffn_post_norm, v_ssd_w_in, v_ssd_conv_w, v_ssd_conv_b, v_ssd_dt_bias, v_ssd_a_log, v_ssd_d, v_ssd_norm_w, v_ssd_w_out, v_attn_w_qkv, v_attn_b_qkv, v_attn_sinks, v_attn_w_o, v_attn_b_o, v_mlp_w_up, v_mlp_w_down, v_mix_pre_norm, v_mix_post_norm, v_ffn_pre_norm, v_ffn_post_norm):
    w = dict(zip(WEIGHT_NAMES, (ssd_w_in, ssd_conv_w, ssd_conv_b, ssd_dt_bias, ssd_a_log, ssd_d, ssd_norm_w, ssd_w_out, attn_w_qkv, attn_b_qkv, attn_sinks, attn_w_o, attn_b_o, mlp_w_up, mlp_w_down, mix_pre_norm, mix_post_norm, ffn_pre_norm, ffn_post_norm)))
    m = dict(zip(WEIGHT_NAMES, (m_ssd_w_in, m_ssd_conv_w, m_ssd_conv_b, m_ssd_dt_bias, m_ssd_a_log, m_ssd_d, m_ssd_norm_w, m_ssd_w_out, m_attn_w_qkv, m_attn_b_qkv, m_attn_sinks, m_attn_w_o, m_attn_b_o, m_mlp_w_up, m_mlp_w_down, m_mix_pre_norm, m_mix_post_norm, m_ffn_pre_norm, m_ffn_post_norm)))
    v = dict(zip(WEIGHT_NAMES, (v_ssd_w_in, v_ssd_conv_w, v_ssd_conv_b, v_ssd_dt_bias, v_ssd_a_log, v_ssd_d, v_ssd_norm_w, v_ssd_w_out, v_attn_w_qkv, v_attn_b_qkv, v_attn_sinks, v_attn_w_o, v_attn_b_o, v_mlp_w_up, v_mlp_w_down, v_mix_pre_norm, v_mix_post_norm, v_ffn_pre_norm, v_ffn_post_norm)))
    chip = 2 * lax.axis_index("x") + lax.axis_index("y")

    two_halves = lambda a: a.reshape(2, a.shape[-2] // 2, a.shape[-1])
    later = {"ssd_w_out": (w["ssd_w_out"], 0), "attn_w_qkv": (w["attn_w_qkv"], 0), "attn_w_o": (w["attn_w_o"], 0),
             "mlp_w_up0": (w["mlp_w_up"], 0), "mlp_w_up1": (w["mlp_w_up"], 1),
             "mlp_w_down0": (w["mlp_w_down"], 0), "mlp_w_down1": (w["mlp_w_down"], 1)}
    first = _GatherHook([two_halves(w["ssd_w_in"].astype(BF16))], [w["ssd_conv_w"][0], w["attn_b_qkv"], w["attn_b_o"]])
    (cast, u0), (g_in, g_conv, g_bqkv, g_bo) = _cast_bf16(list(later.values()), x[0], w["mix_pre_norm"][0:1],
                                                          name="weights_to_bf16", hook=first)
    core = lax.axis_index("c").astype(jnp.int32).reshape(1)
    comm = _StepComm({k: two_halves(a) for k, a in zip(later, cast)}, core)
    full = {
        "ssd_w_in": _full_weight("ssd_w_in", g_in),
        "ssd_conv_w": g_conv.transpose(1, 0, 2).reshape(SSD_CONV_WIDTH, SSD_CONV_DIM),
        "attn_b_qkv": g_bqkv.reshape(ATTN_QKV), "attn_b_o": g_bo.reshape(D_MODEL),
    }
    for name in REPLICATED:
        full[name] = w[name][0] if name.startswith(("ssd_", "attn_")) else w[name]

    loss_tile, grad_x, gm, g = _local_step(x[0], loss_target[0], full, comm, u0)

    conv_w_rows = g["ssd_conv_w"].reshape(SSD_CONV_WIDTH * N_CHIPS, D_MODEL)
    b_qkv_rows = jnp.pad(g["attn_b_qkv"], (0, 2 * D_MODEL - ATTN_QKV)).reshape(2, D_MODEL)
    small = jnp.concatenate([_replicated_rows(g, loss_tile[0, 0]), conv_w_rows, b_qkv_rows, _rows(g["attn_b_o"]),
                             jnp.zeros((SM_ROWS - SM_B_O - 1, D_MODEL), F32)], axis=0)
    order = ("ssd_w_in", "ssd_w_out", "attn_w_qkv", "attn_w_o", "mlp_w_up0", "mlp_w_up1", "mlp_w_down0", "mlp_w_down1")
    halves = _sum_chips([comm.chip_parts[k] for k in order], name="grad_sum")
    (r_in, r_out, r_qkv, r_o, r_up, r_down), (small_all,) = _swap_halves(
        halves, layers=((4, 5), (6, 7)), hook=_ExchangeHook([], [small]), name="grad_halves_swap")
    small_sum, = _sum_chips([small_all], name="small_grad_sum")

    grads = {"ssd_w_in": r_in, "ssd_w_out": r_out, "attn_w_qkv": r_qkv, "attn_w_o": r_o, "mlp_w_up": r_up, "mlp_w_down": r_down}
    grads = {k: a.reshape(w[k].shape) for k, a in grads.items()}
    conv_w_g = lax.dynamic_index_in_dim(small_sum[SM_CONV_W:SM_CONV_W + 16].reshape(SSD_CONV_WIDTH, N_CHIPS, D_MODEL), chip, axis=1, keepdims=False)
    b_qkv_g = lax.dynamic_slice_in_dim(small_sum[SM_B_QKV:SM_B_QKV + 2].reshape(-1), chip * 384, 384)
    b_o_g = lax.dynamic_slice_in_dim(small_sum[SM_B_O], chip * 256, 256)
    small_g = jnp.concatenate([small_sum[0:16], _sharded_rows(conv_w_g, b_qkv_g, b_o_g)], axis=0)
    grads.update(_unpack_small(small_g[0:16], small_g[16:24], w))
    loss = small_sum[SM_MISC, MISC_LOSS]

    delta, new_m, new_v = {}, {}, {}
    stored = lambda a: jnp.swapaxes(a, 1, 2)
    rest = [name for name in MATRICES if name != "ssd_w_in"]
    mats = lambda p: [p[name] for name in rest]
    results = dict(zip(rest, _adamw(mats(w), mats(grads), mats(m), mats(v), name="adamw_matrices")))
    (w_in_result,) = _adamw([stored(w["ssd_w_in"])], [stored(grads["ssd_w_in"])], [stored(m["ssd_w_in"])],
                            [stored(v["ssd_w_in"])], by_lanes=True, name="adamw_ssd_w_in")
    results["ssd_w_in"] = tuple(stored(a) for a in w_in_result)
    for name in MATRICES:
        grads[name], delta[name], new_m[name], new_v[name] = results[name]
    zero = jnp.zeros((), F32)
    small_pack = lambda p: jnp.concatenate([_replicated_rows({k: p[k] for k in REPLICATED}, zero),
                                            _sharded_rows(p["ssd_conv_w"], p["attn_b_qkv"], p["attn_b_o"])], axis=0)[None]
    (_, d_s, m_s, v_s), = _adamw([small_pack(w)], [small_g[None]], [small_pack(m)], [small_pack(v)], name="adamw_vectors")
    d_s, m_s, v_s = d_s[0], m_s[0], v_s[0]
    delta.update(_unpack_small(d_s[0:16], d_s[16:24], w))
    new_m.update(_unpack_small(m_s[0:16], m_s[16:24], w))
    new_v.update(_unpack_small(v_s[0:16], v_s[16:24], w))

    return (loss, grad_x[None], *[grads[n] for n in WEIGHT_NAMES], *[delta[n] for n in WEIGHT_NAMES],
            *[new_m[n] for n in WEIGHT_NAMES], *[new_v[n] for n in WEIGHT_NAMES])
```

```python
import functools

import jax
import jax.numpy as jnp
from jax import lax
from jax.experimental import pallas as pl
from jax.experimental.pallas import tpu as pltpu

F32 = jnp.float32
BF16 = jnp.bfloat16

D_MODEL = 1024
SSD_D_INNER = 2048
SSD_HEAD_DIM = 64
SSD_N_HEADS = 32
SSD_N_GROUPS = 8
SSD_HPG = 4
SSD_D_STATE = 128
SSD_CONV_WIDTH = 4
SSD_CHUNK = 128
SSD_CONV_DIM = 4096
SSD_IN_DIM = 6176
SSD_IN_PAD = 6400
SSD_IN_TILE = 1280
SSD_DT_COL = 6144
SSD_GW = SSD_HPG * SSD_HEAD_DIM
ATTN_HEAD_DIM = 64
ATTN_N_Q = 16
ATTN_N_KV = 4
ATTN_REP = 4
ATTN_WINDOW = 128
ATTN_QKV = 1536
D_FF = 4096
NORM_EPS = 1e-6

ADAM_LR = 0.001
ADAM_B1 = 0.9
ADAM_B2 = 0.999
ADAM_EPS = 1e-08
ADAM_WD = 0.01
ADAM_STEP = 10

N_CHIPS = 4
N_DEV = 8
LANES = 128
VMEM_LIMIT = 48 * 1024 * 1024
BIG_TILE = 2048
MESH = pl.DeviceIdType.MESH


def _params(*sem):
    return pltpu.CompilerParams(dimension_semantics=sem, vmem_limit_bytes=VMEM_LIMIT)


def _dot(a, b, dims):
    return lax.dot_general(a, b, (dims, ((), ())), preferred_element_type=F32)


def _dot_nn(a, b):
    return _dot(a, b, ((1,), (0,)))


def _dot_nt(a, b):
    return _dot(a, b, ((1,), (1,)))


def _dot_tn(a, b):
    return _dot(a, b, ((0,), (0,)))


def _sigmoid(x):
    return 0.5 * jnp.tanh(0.5 * x) + 0.5


ANY = pl.BlockSpec(memory_space=pl.ANY)


class _HookSlots:
    def __init__(self, hook, n_in, n_out, n_scratch):
        self.hook = hook
        self.n_in, self.n_out, self.n_scratch = n_in, n_out, n_scratch
        self.inputs = list(hook.arrs) if hook else []
        self.out_shape = list(hook.out_shape) if hook else []
        self.scratch = list(hook.scratch) if hook else []
        self.in_specs = [ANY] * len(self.inputs)
        self.out_specs = [ANY] * len(self.out_shape)

    def _split(self, refs):
        a = self.n_in
        b = a + len(self.inputs)
        c = b + self.n_out
        d = c + len(self.out_shape)
        e = d + self.n_scratch
        return refs[:a], refs[a:b], refs[b:c], refs[c:d], refs[d:e], refs[e:]

    def own(self, refs):
        ins, _, outs, _, scratch, _ = self._split(refs)
        return ins, outs, scratch

    def plan(self, refs):
        _, h_in, _, h_out, _, h_scratch = self._split(refs)
        return self.hook.plan(h_in, h_out, h_scratch)

    def run(self, refs, step, n_steps):
        _, h_in, _, h_out, _, h_scratch = self._split(refs)
        _run_hook(self.hook, h_in, h_out, h_scratch, step, n_steps)

    def semantics(self, *sem):
        return sem if self.hook is None else ("arbitrary",) * len(sem)


def _matmul(a, b, *, mode, out_dtypes, name, epilogue=None, extras=(), tm=1024, tn=1024, tk=1024,
            b_shards=False, out_shards=False, hook=None, f32_block=None):
    f32_tail = f32_block is not None
    if b_shards:
        s, b_rows, b_cols = b.shape
        b2 = (b_rows, s * b_cols)
        if mode == "nn":
            tn = b_cols
        else:
            assert mode == "nt"
            tk = b_cols
    else:
        b2 = b.shape
    if mode == "nn":
        (m, k), (k2, n) = a.shape, b2
    elif mode == "nt":
        (m, k), (n, k2) = a.shape, b2
    else:
        (k, m), (k2, n) = a.shape, b2
    assert k == k2, (a.shape, b.shape, mode)
    tm, tn, tk = min(tm, m), min(tn, n), min(tk, k)
    assert m % tm == 0 and n % tn == 0 and k % tk == 0, (m, n, k, tm, tn, tk)
    nk = k // tk
    if mode == "tn":
        a_spec = pl.BlockSpec((tk, tm), lambda i, j, kk: (kk, i))
    else:
        a_spec = pl.BlockSpec((tm, tk), lambda i, j, kk: (i, kk))
    if b_shards and mode == "nn":
        b_spec = pl.BlockSpec((None, tk, tn), lambda i, j, kk: (j, kk, 0))
    elif b_shards:
        b_spec = pl.BlockSpec((None, tn, tk), lambda i, j, kk: (kk, j, 0))
    elif mode == "nt":
        b_spec = pl.BlockSpec((tn, tk), lambda i, j, kk: (j, kk))
    else:
        b_spec = pl.BlockSpec((tk, tn), lambda i, j, kk: (kk, j))
    dims = {"nn": ((1,), (0,)), "nt": ((1,), (1,)), "tn": ((0,), (0,))}[mode]
    ex_specs = []
    for arr, kind in extras:
        if kind == "tile":
            ex_specs.append(pl.BlockSpec((tm, tn), lambda i, j, kk: (i, j)))
        else:
            ex_specs.append(pl.BlockSpec((1, tn), lambda i, j, kk: (0, j)))
    n_ex, n_out = len(extras), len(out_dtypes)
    if epilogue is None:
        epilogue = lambda acc: (acc,)
    hk = _HookSlots(hook, n_in=2 + n_ex, n_out=n_out + f32_tail, n_scratch=0 if nk == 1 else 1)
    grid = (m // tm, n // tn, nk)

    def body(*refs):
        (a_ref, b_ref, *ex), outs, scratch = hk.own(refs)
        if hook is not None:
            step = (pl.program_id(0) * grid[1] + pl.program_id(1)) * grid[2] + pl.program_id(2)
            hk.run(refs, step, grid[0] * grid[1] * grid[2])

        def finish(acc):
            res = epilogue(acc, *[e[...] for e in ex])
            for o, r in zip(outs, res):
                o[...] = r.astype(o.dtype)
            if f32_tail:
                outs[n_out][...] = acc[:, f32_block:f32_block + LANES]

        if nk == 1:
            finish(_dot(a_ref[...], b_ref[...], dims))
        else:
            acc_ref = scratch[0]
            kk = pl.program_id(2)

            @pl.when(kk == 0)
            def _():
                acc_ref[...] = jnp.zeros_like(acc_ref)

            acc_ref[...] += _dot(a_ref[...], b_ref[...], dims)

            @pl.when(kk == nk - 1)
            def _():
                finish(acc_ref[...])

    if out_shards:
        out_spec = pl.BlockSpec((None, tm, tn), lambda i, j, kk: (j, i, 0))
        out_dims = (n // tn, m, tn)
    else:
        out_spec = pl.BlockSpec((tm, tn), lambda i, j, kk: (i, j))
        out_dims = (m, n)
    tail_specs = [pl.BlockSpec((tm, LANES), lambda i, j, kk: (i, 0))] if f32_tail else []
    tail_shapes = [jax.ShapeDtypeStruct((m, LANES), F32)] if f32_tail else []
    outs = pl.pallas_call(
        body,
        grid=grid,
        in_specs=[a_spec, b_spec] + ex_specs + hk.in_specs,
        out_specs=[out_spec for _ in out_dtypes] + tail_specs + hk.out_specs,
        out_shape=[jax.ShapeDtypeStruct(out_dims, dt) for dt in out_dtypes] + tail_shapes + hk.out_shape,
        scratch_shapes=([] if nk == 1 else [pltpu.VMEM((tm, tn), F32)]) + hk.scratch,
        compiler_params=_params(*hk.semantics("parallel", "arbitrary" if f32_tail else "parallel", "arbitrary")),
        name=name,
    )(a, b, *[arr for arr, _ in extras], *hk.inputs)
    n_own = n_out + f32_tail
    own = outs[0] if n_own == 1 else outs[:n_own]
    return own if hook is None else (own, outs[n_own:])


def _row_tile(t, want):
    return min(t, want)


def _rms_fwd(x, w, *, name, resid=None, want_u=None, target=None):
    t, d = x.shape
    tr = _row_tile(t, 1024)

    def norm(v, wv):
        return v * lax.rsqrt(jnp.mean(v * v, axis=-1, keepdims=True) + NORM_EPS) * wv

    row = pl.BlockSpec((tr, d), lambda i: (i, 0))
    vec = pl.BlockSpec((1, d), lambda i: (0, 0))
    if target is not None:
        def body(x_ref, w_ref, r_ref, t_ref, dh_ref, loss_ref):
            err = r_ref[...] + norm(x_ref[...].astype(F32), w_ref[...]) - t_ref[...]
            dh_ref[...] = err * (1.0 / d)

            @pl.when(pl.program_id(0) == 0)
            def _():
                loss_ref[...] = jnp.zeros_like(loss_ref)

            part = jnp.sum(jnp.sum(err * err, axis=1, keepdims=True), axis=0, keepdims=True) * (0.5 / d)
            loss_ref[...] += jnp.broadcast_to(part, loss_ref.shape)

        return pl.pallas_call(
            body, grid=(t // tr,), in_specs=[row, vec, row, row],
            out_specs=[row, pl.BlockSpec((8, LANES), lambda i: (0, 0))],
            out_shape=[jax.ShapeDtypeStruct((t, d), F32), jax.ShapeDtypeStruct((8, LANES), F32)],
            compiler_params=_params("arbitrary"), name=name)(x, w, resid, target)
    if resid is None:
        def body(x_ref, w_ref, o_ref):
            o_ref[...] = norm(x_ref[...].astype(F32), w_ref[...]).astype(BF16)
        ins, in_specs = (x, w), [row, vec]
        out_shape, out_specs = jax.ShapeDtypeStruct((t, d), BF16), row
    elif want_u is None:
        def body(x_ref, w_ref, r_ref, o_ref):
            o_ref[...] = r_ref[...] + norm(x_ref[...].astype(F32), w_ref[...])
        ins, in_specs = (x, w, resid), [row, vec, row]
        out_shape, out_specs = jax.ShapeDtypeStruct((t, d), F32), row
    else:
        def body(x_ref, w_ref, r_ref, w2_ref, o_ref, u_ref):
            h = r_ref[...] + norm(x_ref[...].astype(F32), w_ref[...])
            o_ref[...] = h
            u_ref[...] = norm(h, w2_ref[...]).astype(BF16)
        ins, in_specs = (x, w, resid, want_u), [row, vec, row, vec]
        out_shape = [jax.ShapeDtypeStruct((t, d), F32), jax.ShapeDtypeStruct((t, d), BF16)]
        out_specs = [row, row]
    return pl.pallas_call(body, grid=(t // tr,), in_specs=in_specs, out_specs=out_specs, out_shape=out_shape,
                          compiler_params=_params("parallel"), name=name)(*ins)


def _rms_bwd(x, w, dy, *, name, resid=None, out_dtype=F32, dx_col_sum=False):
    t, d = x.shape
    tr = _row_tile(t, 1024)
    row = pl.BlockSpec((tr, d), lambda i: (i, 0))
    vec = pl.BlockSpec((1, d), lambda i: (0, 0))
    has_res = resid is not None

    def body(x_ref, w_ref, dy_ref, *rest):
        r_ref = rest[0] if has_res else None
        dx_ref, dw_ref = rest[has_res:has_res + 2]
        xv = x_ref[...].astype(F32)
        dyv = dy_ref[...].astype(F32)
        r = lax.rsqrt(jnp.mean(xv * xv, axis=-1, keepdims=True) + NORM_EPS)
        xhat = xv * r
        dyw = dyv * w_ref[...]
        dx = r * (dyw - xhat * jnp.mean(dyw * xhat, axis=-1, keepdims=True))
        if has_res:
            dx = dx + r_ref[...]
        dx_ref[...] = dx.astype(dx_ref.dtype)

        sums = [(dw_ref, dyv * xhat)] + ([(rest[-1], dx)] if dx_col_sum else [])

        @pl.when(pl.program_id(0) == 0)
        def _():
            for acc_ref, _ in sums:
                acc_ref[...] = jnp.zeros_like(acc_ref)

        for acc_ref, rows in sums:
            acc_ref[...] += jnp.sum(rows, axis=0, keepdims=True)

    ins = (x, w, dy) + ((resid,) if has_res else ())
    in_specs = [row, vec, row] + ([row] if has_res else [])
    n_vec = 2 if dx_col_sum else 1
    return pl.pallas_call(
        body, grid=(t // tr,), in_specs=in_specs, out_specs=[row] + [vec] * n_vec,
        out_shape=[jax.ShapeDtypeStruct((t, d), out_dtype)] + [jax.ShapeDtypeStruct((1, d), F32)] * n_vec,
        compiler_params=_params("arbitrary"), name=name)(*ins)


def _col_sum(x, *, name):
    t, n = x.shape
    tr = _row_tile(t, 1024)

    def body(x_ref, o_ref):
        @pl.when(pl.program_id(0) == 0)
        def _():
            o_ref[...] = jnp.zeros_like(o_ref)

        o_ref[...] += jnp.sum(x_ref[...].astype(F32), axis=0, keepdims=True)

    return pl.pallas_call(
        body, grid=(t // tr,), in_specs=[pl.BlockSpec((tr, n), lambda i: (i, 0))],
        out_specs=pl.BlockSpec((1, n), lambda i: (0, 0)), out_shape=jax.ShapeDtypeStruct((1, n), F32),
        compiler_params=_params("arbitrary"), name=name)(x)


SSD_IN_SHARD = SSD_IN_DIM // N_CHIPS


def _w_in_from_shards(shards, *, name):
    d = shards.shape[1]
    tr = 256

    def body(s_ref, o_ref):
        o_ref[:, pl.ds(SSD_DT_COL, SSD_IN_PAD - SSD_DT_COL)] = jnp.zeros((tr, SSD_IN_PAD - SSD_DT_COL), o_ref.dtype)
        for s in range(N_CHIPS):
            o_ref[:, pl.ds(SSD_IN_SHARD * s, SSD_IN_SHARD)] = s_ref[s]

    return pl.pallas_call(
        body, grid=(d // tr,), in_specs=[pl.BlockSpec((N_CHIPS, tr, SSD_IN_SHARD), lambda i: (0, i, 0))],
        out_specs=pl.BlockSpec((tr, SSD_IN_PAD), lambda i: (i, 0)),
        out_shape=jax.ShapeDtypeStruct((d, SSD_IN_PAD), shards.dtype),
        compiler_params=_params("parallel"), name=name)(shards)


def _w_in_to_shards(g, *, name):
    d = g.shape[0]
    tr = 256

    def body(g_ref, o_ref):
        for s in range(N_CHIPS):
            o_ref[s] = g_ref[:, pl.ds(SSD_IN_SHARD * s, SSD_IN_SHARD)].astype(o_ref.dtype)

    return pl.pallas_call(
        body, grid=(d // tr,), in_specs=[pl.BlockSpec((tr, SSD_IN_PAD), lambda i: (i, 0))],
        out_specs=pl.BlockSpec((N_CHIPS, tr, SSD_IN_SHARD), lambda i: (0, i, 0)),
        out_shape=jax.ShapeDtypeStruct((N_CHIPS, d, SSD_IN_SHARD), BF16),
        compiler_params=_params("parallel"), name=name)(g)


XBC_COL0 = SSD_D_INNER // LANES


def _shift_down(v, k, row_ids):
    return jnp.where(row_ids >= k, pltpu.roll(v, k, axis=0), 0.0)


def _shift_up(v, k, row_ids):
    n = v.shape[0]
    return jnp.where(row_ids < n - k, pltpu.roll(v, n - k, axis=0), 0.0)


def _conv_pre(x, w, b, row_ids):
    pre = b + w[3:4, :] * x
    for k in (1, 2, 3):
        pre = pre + w[3 - k:4 - k, :] * _shift_down(x, k, row_ids)
    return pre


def _conv_fwd(zx, conv_w, conv_b, *, name, hook=None):
    t = zx.shape[0]
    cw = 2 * LANES
    nct = SSD_CONV_DIM // cw
    col0 = SSD_D_INNER // cw
    hk = _HookSlots(hook, n_in=3, n_out=1, n_scratch=0)

    def body(*refs):
        (x_ref, w_ref, b_ref), (o_ref,), _ = hk.own(refs)
        if hook is not None:
            hk.run(refs, pl.program_id(0), nct)
        x = x_ref[...].astype(F32)
        row_ids = lax.broadcasted_iota(jnp.int32, x.shape, 0)
        pre = _conv_pre(x, w_ref[...], b_ref[...], row_ids)
        o_ref[...] = pre * _sigmoid(pre)

    outs = pl.pallas_call(
        body, grid=(nct,),
        in_specs=[pl.BlockSpec((t, cw), lambda j: (0, col0 + j)),
                  pl.BlockSpec((SSD_CONV_WIDTH, cw), lambda j: (0, j)),
                  pl.BlockSpec((1, cw), lambda j: (0, j))] + hk.in_specs,
        out_specs=[pl.BlockSpec((t, cw), lambda j: (0, j))] + hk.out_specs,
        out_shape=[jax.ShapeDtypeStruct((t, SSD_CONV_DIM), F32)] + hk.out_shape,
        scratch_shapes=hk.scratch,
        compiler_params=_params(*hk.semantics("parallel")), name=name)(zx, conv_w, conv_b, *hk.inputs)
    return outs[0] if hook is None else (outs[0], outs[1:])


def _conv_bwd(zx, conv_w, conv_b, d_xs, d_bm, d_cm, dzx, *, name):
    t = zx.shape[0]
    nct = SSD_CONV_DIM // LANES
    n_xs = SSD_D_INNER // LANES
    n_bm = SSD_N_GROUPS * SSD_D_STATE // LANES

    def body(x_ref, w_ref, b_ref, dxs_ref, dbm_ref, dcm_ref, _, dx_ref, dw_ref, db_ref):
        x = x_ref[...].astype(F32)
        w = w_ref[...]
        j = pl.program_id(0)
        dy = jnp.where(j < n_xs, dxs_ref[...], jnp.where(j < n_xs + n_bm, dbm_ref[...], dcm_ref[...]))
        row_ids = lax.broadcasted_iota(jnp.int32, x.shape, 0)
        pre = _conv_pre(x, w, b_ref[...], row_ids)
        sg = _sigmoid(pre)
        dpre = dy * (sg * (1.0 + pre * (1.0 - sg)))
        dx = w[3:4, :] * dpre
        for k in (1, 2, 3):
            dx = dx + w[3 - k:4 - k, :] * _shift_up(dpre, k, row_ids)
        dx_ref[...] = dx.astype(dx_ref.dtype)
        db_ref[...] = jnp.sum(dpre, axis=0, keepdims=True)
        dw_ref[3:4, :] = jnp.sum(dpre * x, axis=0, keepdims=True)
        for k in (1, 2, 3):
            dw_ref[3 - k:4 - k, :] = jnp.sum(dpre * _shift_down(x, k, row_ids), axis=0, keepdims=True)

    clip = lambda j, lo, n: jnp.clip(j - lo, 0, n - 1)
    return pl.pallas_call(
        body, grid=(nct,),
        in_specs=[pl.BlockSpec((t, LANES), lambda j: (0, XBC_COL0 + j)),
                  pl.BlockSpec((SSD_CONV_WIDTH, LANES), lambda j: (0, j)),
                  pl.BlockSpec((1, LANES), lambda j: (0, j)),
                  pl.BlockSpec((t, LANES), lambda j: (0, clip(j, 0, n_xs))),
                  pl.BlockSpec((t, LANES), lambda j: (0, clip(j, n_xs, n_bm))),
                  pl.BlockSpec((t, LANES), lambda j: (0, clip(j, n_xs + n_bm, n_bm))), ANY],
        out_specs=[pl.BlockSpec((t, LANES), lambda j: (0, XBC_COL0 + j)),
                   pl.BlockSpec((SSD_CONV_WIDTH, LANES), lambda j: (0, j)), pl.BlockSpec((1, LANES), lambda j: (0, j))],
        out_shape=[jax.ShapeDtypeStruct(dzx.shape, dzx.dtype),
                   jax.ShapeDtypeStruct((SSD_CONV_WIDTH, SSD_CONV_DIM), F32),
                   jax.ShapeDtypeStruct((1, SSD_CONV_DIM), F32)],
        input_output_aliases={6: 0},
        compiler_params=_params("parallel"), name=name)(zx, conv_w, conv_b, d_xs, d_bm, d_cm, dzx)


def _softplus_fwd(dt_raw, bias_row, alog_row, *, name):
    t = dt_raw.shape[0]
    q = SSD_CHUNK
    tr = _row_tile(t, 1024)

    def body(x_ref, b_ref, al_ref, dt_ref, cum_ref):
        v = x_ref[...] + b_ref[...]
        e = jnp.exp(-jnp.abs(v))
        u = 1.0 + e
        log1p = jnp.where(u == 1.0, e, jnp.log(u) * (e / (u - 1.0)))
        dt = jnp.maximum(v, 0.0) + log1p
        a = dt * -jnp.exp(al_ref[...])
        lower = (lax.broadcasted_iota(jnp.int32, (q, q), 1) <= lax.broadcasted_iota(jnp.int32, (q, q), 0)).astype(F32)
        cums = [lax.dot_general(lower, a[c * q:(c + 1) * q, :], ((((1,), (0,))), ((), ())), precision=lax.Precision.HIGHEST,
                                preferred_element_type=F32) for c in range(tr // q)]
        dt_t, cum_t = dt.T, jnp.concatenate(cums, axis=0).T
        for g in range(SSD_N_GROUPS):
            rows = slice(g * SSD_HPG, (g + 1) * SSD_HPG)
            dt_ref[g] = dt_t[rows, :]
            cum_ref[g] = cum_t[rows, :]

    vec = pl.BlockSpec((1, LANES), lambda i: (0, 0))
    by_group = pl.BlockSpec((SSD_N_GROUPS, SSD_HPG, tr), lambda i: (0, 0, i))
    return pl.pallas_call(
        body, grid=(t // tr,),
        in_specs=[pl.BlockSpec((tr, LANES), lambda i: (i, 0)), vec, vec],
        out_specs=[by_group, by_group],
        out_shape=[jax.ShapeDtypeStruct((SSD_N_GROUPS, SSD_HPG, t), F32)] * 2,
        compiler_params=_params("parallel"), name=name)(dt_raw, bias_row, alog_row)


def _softplus_bwd(dt_raw, bias_row, ddt_rows, dzx, *, name):
    t = dt_raw.shape[0]
    tr = _row_tile(t, 1024)
    tail = SSD_IN_PAD - SSD_DT_COL

    def body(x_ref, b_ref, g_ref, _, o_ref, db_ref):
        v = x_ref[...] + b_ref[...]
        lane = lax.broadcasted_iota(jnp.int32, v.shape, 1)
        by_head = jnp.concatenate([g_ref[g] for g in range(SSD_N_GROUPS)]
                                  + [jnp.zeros((LANES - SSD_N_HEADS, tr), F32)], axis=0)
        d = jnp.where(lane < SSD_N_HEADS, by_head.T * _sigmoid(v), 0.0)
        o_ref[:, pl.ds(0, LANES)] = d.astype(o_ref.dtype)
        o_ref[:, pl.ds(LANES, tail - LANES)] = jnp.zeros((tr, tail - LANES), o_ref.dtype)

        @pl.when(pl.program_id(0) == 0)
        def _():
            db_ref[...] = jnp.zeros_like(db_ref)

        db_ref[...] += jnp.sum(d, axis=0, keepdims=True)

    return pl.pallas_call(
        body, grid=(t // tr,),
        in_specs=[pl.BlockSpec((tr, LANES), lambda i: (i, 0)), pl.BlockSpec((1, LANES), lambda i: (0, 0)),
                  pl.BlockSpec((SSD_N_GROUPS, SSD_HPG, tr), lambda i: (0, 0, i)), ANY],
        out_specs=[pl.BlockSpec((tr, tail), lambda i: (i, SSD_DT_COL // tail)), pl.BlockSpec((1, LANES), lambda i: (0, 0))],
        out_shape=[jax.ShapeDtypeStruct(dzx.shape, dzx.dtype), jax.ShapeDtypeStruct((1, LANES), F32)],
        input_output_aliases={3: 0},
        compiler_params=_params("arbitrary"), name=name)(dt_raw, bias_row, ddt_rows, dzx)


def _ssd_masks():
    q = SSD_CHUNK
    tt = lax.broadcasted_iota(jnp.int32, (q, q), 0)
    ss = lax.broadcasted_iota(jnp.int32, (q, q), 1)
    lane = lax.broadcasted_iota(jnp.int32, (1, SSD_GW), 1)
    srow = lax.broadcasted_iota(jnp.int32, (SSD_GW, 1), 0)
    hm = [(lane >= SSD_HEAD_DIM * j) & (lane < SSD_HEAD_DIM * (j + 1)) for j in range(SSD_HPG)]
    rm = [(srow >= SSD_HEAD_DIM * j) & (srow < SSD_HEAD_DIM * (j + 1)) for j in range(SSD_HPG)]
    return tt, ss, hm, rm


def _ssd_head_terms(dt_rows, cum_rows, a_rows, j, tt, ss):
    q = SSD_CHUNK
    dt_row = dt_rows[j:j + 1, :]
    dt_col = jnp.sum(jnp.where(tt == ss, dt_row, 0.0), axis=1, keepdims=True)
    a_row1 = a_rows[j:j + 1, :]
    a_11 = a_rows[j:j + 1, 0:1]
    cum_col = jnp.sum(jnp.where(ss <= tt, dt_row * a_row1, 0.0), axis=1, keepdims=True)
    cum_row = cum_rows[j:j + 1, :]
    decay = jnp.exp(jnp.where(ss <= tt, cum_col - cum_row, -jnp.inf))
    cum_last = cum_col[q - 1:q, :]
    e_col = jnp.exp(cum_col)
    dte_col = jnp.exp(cum_last - cum_col)
    e_last = jnp.exp(cum_last)
    return dt_col, dt_row, a_row1, a_11, decay, e_col, dte_col, e_last


SSD_FWD_CHUNKS_PER_STEP = 16
SSD_BWD_CHUNKS_PER_STEP = 8
SSD_BC_COL0 = SSD_D_INNER // SSD_D_STATE


def _ssd_head_selects(terms, hm, rm):
    e_all = jnp.zeros((SSD_CHUNK, SSD_GW), F32)
    w_all = jnp.zeros((SSD_CHUNK, SSD_GW), F32)
    e_s = jnp.zeros((SSD_GW, 1), F32)
    for j in range(SSD_HPG):
        dt_col, _, _, _, _, e_col, dte_col, e_last = terms[j]
        e_all = jnp.where(hm[j], e_col, e_all)
        w_all = jnp.where(hm[j], dt_col * dte_col, w_all)
        e_s = jnp.where(rm[j], e_last, e_s)
    return e_all, w_all, e_s


def _ssd_fwd(xc, dtr, cumr, alog_b, d_b, *, name, hook=None):
    t = xc.shape[0]
    q = SSD_CHUNK
    nc = t // q
    kc = min(SSD_FWD_CHUNKS_PER_STEP, nc)
    rows = kc * q
    hk = _HookSlots(hook, n_in=7, n_out=2, n_scratch=1)

    def body(*refs):
        (x_ref, b_ref, c_ref, dtr_ref, cumr_ref, alog_ref, d_ref), (y_ref, st_ref), (s_scr,) = hk.own(refs)
        if hook is not None:
            hk.run(refs, pl.program_id(0) * (nc // kc) + pl.program_id(1), SSD_N_GROUPS * (nc // kc))

        @pl.when(pl.program_id(1) == 0)
        def _():
            s_scr[...] = jnp.zeros_like(s_scr)

        tt, ss, hm, rm = _ssd_masks()
        a_rows = -jnp.exp(alog_ref[...])
        d_rows = d_ref[...]
        d_all = jnp.zeros((1, SSD_GW), F32)
        for j in range(SSD_HPG):
            d_all = jnp.where(hm[j], d_rows[j:j + 1, 0:1], d_all)
        ks, hs = range(kc), range(SSD_HPG)
        sl = [pl.ds(k * q, q) for k in ks]
        x = [x_ref[sl[k], :] for k in ks]
        bm = [b_ref[sl[k], :].astype(BF16) for k in ks]
        cm = [c_ref[sl[k], :].astype(BF16) for k in ks]
        xb = [x[k].astype(BF16) for k in ks]
        terms = [[_ssd_head_terms(dtr_ref[:, sl[k]], cumr_ref[:, sl[k]], a_rows, j, tt, ss) for j in hs] for k in ks]
        g = [_dot_nt(cm[k], bm[k]) for k in ks]
        m = [[(g[k] * terms[k][j][4] * terms[k][j][1]).astype(BF16) for j in hs] for k in ks]
        yj = [[_dot_nn(m[k][j], xb[k]) for j in hs] for k in ks]
        sel = [_ssd_head_selects(terms[k], hm, rm) for k in ks]
        upd = [_dot_tn((x[k] * sel[k][1]).astype(BF16), bm[k]) for k in ks]
        states = [s_scr[...]]
        for k in ks:
            states.append(states[k] * sel[k][2] + upd[k])
        inter = [_dot_nt(cm[k], states[k].astype(BF16)) for k in ks]
        ys = []
        for k in ks:
            y = jnp.zeros((q, SSD_GW), F32)
            for j in hs:
                y = jnp.where(hm[j], yj[k][j], y)
            ys.append(y + inter[k] * sel[k][0] + x[k] * d_all)
        for k in ks:
            st_ref[k] = states[k]
        y_ref[...] = jnp.concatenate(ys, axis=0).astype(y_ref.dtype)
        s_scr[...] = states[kc]

    blk = lambda width, off: pl.BlockSpec((rows, width), lambda g, c: (c, off + g))
    par_s = pl.BlockSpec((None, SSD_HPG, LANES), lambda g, c: (g, 0, 0))
    row_s = pl.BlockSpec((None, SSD_HPG, rows), lambda g, c: (g, 0, c))
    outs = pl.pallas_call(
        body, grid=(SSD_N_GROUPS, nc // kc),
        in_specs=[blk(SSD_GW, 0), blk(SSD_D_STATE, SSD_BC_COL0), blk(SSD_D_STATE, SSD_BC_COL0 + SSD_N_GROUPS),
                  row_s, row_s, par_s, par_s] + hk.in_specs,
        out_specs=[blk(SSD_GW, 0), pl.BlockSpec((None, kc, SSD_GW, SSD_D_STATE), lambda g, c: (g, c, 0, 0))] + hk.out_specs,
        out_shape=[jax.ShapeDtypeStruct((t, SSD_D_INNER), BF16),
                   jax.ShapeDtypeStruct((SSD_N_GROUPS, nc, SSD_GW, SSD_D_STATE), F32)] + hk.out_shape,
        scratch_shapes=[pltpu.VMEM((SSD_GW, SSD_D_STATE), F32)] + hk.scratch,
        compiler_params=_params(*hk.semantics("parallel", "arbitrary")), name=name)(
            xc, xc, xc, dtr, cumr, alog_b, d_b, *hk.inputs)
    return outs if hook is None else (outs[:2], outs[2:])


def _ssd_bwd(xc, dtr, cumr, alog_b, d_b, states, dy, *, name, hook=None):
    t = xc.shape[0]
    q = SSD_CHUNK
    nc = t // q
    kc = min(SSD_BWD_CHUNKS_PER_STEP, nc)
    nst = nc // kc
    rows = kc * q
    rev = lambda c: nst - 1 - c
    hk = _HookSlots(hook, n_in=9, n_out=5, n_scratch=1)

    def body(*refs):
        ((x_ref, b_ref, c_ref, dtr_ref, cumr_ref, alog_ref, d_ref, st_ref, dy_ref),
         (dx_ref, db_ref, dc_ref, ddt_ref, dpar_ref), (ds_scr,)) = hk.own(refs)
        if hook is not None:
            hk.run(refs, pl.program_id(0) * nst + pl.program_id(1), SSD_N_GROUPS * nst)

        @pl.when(pl.program_id(1) == 0)
        def _():
            ds_scr[...] = jnp.zeros_like(ds_scr)
            dpar_ref[...] = jnp.zeros_like(dpar_ref)

        tt, ss, hm, rm = _ssd_masks()
        tcol = lax.broadcasted_iota(jnp.int32, (q, 1), 0)
        lane = lax.broadcasted_iota(jnp.int32, (1, LANES), 1)
        a_rows = -jnp.exp(alog_ref[...])
        d_rows = d_ref[...]
        d_all = jnp.zeros((1, SSD_GW), F32)
        for j in range(SSD_HPG):
            d_all = jnp.where(hm[j], d_rows[j:j + 1, 0:1], d_all)
        ks, hs = range(kc), range(SSD_HPG)
        sl = [pl.ds(k * q, q) for k in ks]
        x = [x_ref[sl[k], :] for k in ks]
        dyv = [dy_ref[sl[k], :].astype(F32) for k in ks]
        bm = [b_ref[sl[k], :].astype(BF16) for k in ks]
        cm = [c_ref[sl[k], :].astype(BF16) for k in ks]
        s_in = [st_ref[k] for k in ks]
        xb = [x[k].astype(BF16) for k in ks]
        dyb = [dyv[k].astype(BF16) for k in ks]
        s_b = [s_in[k].astype(BF16) for k in ks]
        terms = [[_ssd_head_terms(dtr_ref[:, sl[k]], cumr_ref[:, sl[k]], a_rows, j, tt, ss) for j in hs] for k in ks]
        sel = [_ssd_head_selects(terms[k], hm, rm) for k in ks]
        e_all, w_all, e_s = [s_[0] for s_ in sel], [s_[1] for s_ in sel], [s_[2] for s_ in sel]
        dye = [(dyv[k] * e_all[k]).astype(BF16) for k in ks]
        ds_loc = [_dot_tn(dye[k], cm[k]) for k in ks]
        ds = [None] * kc
        running = ds_scr[...]
        for k in reversed(ks):
            ds[k] = running
            running = running * e_s[k] + ds_loc[k]
        ds_scr[...] = running
        ds_b = [ds[k].astype(BF16) for k in ks]
        g = [_dot_nt(cm[k], bm[k]) for k in ks]
        cs = [_dot_nt(cm[k], s_b[k]) for k in ks]
        bds = [_dot_nt(bm[k], ds_b[k]) for k in ks]
        dm = [[_dot_nt(jnp.where(hm[j], dyv[k], 0.0).astype(BF16), xb[k]) for j in hs] for k in ks]
        gl = [[g[k] * terms[k][j][4] for j in hs] for k in ks]
        wp = [[dm[k][j] * gl[k][j] for j in hs] for k in ks]
        mt = [[(gl[k][j] * terms[k][j][1]).astype(BF16) for j in hs] for k in ks]
        dxj = [[_dot_tn(mt[k][j], dyb[k]) for j in hs] for k in ks]
        dg = []
        for k in ks:
            acc = jnp.zeros((q, q), F32)
            for j in hs:
                acc = acc + dm[k][j] * terms[k][j][4] * terms[k][j][1]
            dg.append(acc.astype(BF16))
        dy_cs = [dyv[k] * cs[k] for k in ks]
        x_bds = [x[k] * bds[k] for k in ks]
        dy_x = [dyv[k] * x[k] for k in ks]
        ds_s = [ds[k] * s_in[k] for k in ks]
        w = [[wp[k][j] * terms[k][j][1] for j in hs] for k in ks]
        rw_col = [[jnp.sum(w[k][j], axis=1, keepdims=True) for j in hs] for k in ks]
        cw_row = [[jnp.sum(w[k][j], axis=0, keepdims=True) for j in hs] for k in ks]
        cwp_row = [[jnp.sum(wp[k][j], axis=0, keepdims=True) for j in hs] for k in ks]
        r1_col = [[jnp.sum(jnp.where(hm[j], dy_cs[k], 0.0), axis=1, keepdims=True) * terms[k][j][5] for j in hs] for k in ks]
        dw_col = [[jnp.sum(jnp.where(hm[j], x_bds[k], 0.0), axis=1, keepdims=True) for j in hs] for k in ks]
        head_rows = [slice(j * SSD_HEAD_DIM, (j + 1) * SSD_HEAD_DIM) for j in hs]
        lane_sum = lambda v: jnp.sum(v, axis=1, keepdims=True)
        s_sum = [[lane_sum(jnp.sum(ds_s[k][head_rows[j], :], axis=0, keepdims=True)) for j in hs] for k in ks]
        dy_x_cols = [jnp.sum(dy_x[k], axis=0, keepdims=True) for k in ks]
        d_d = [[lane_sum(jnp.where(hm[j], dy_x_cols[k], 0.0)) for j in hs] for k in ks]
        ddt_rows = [[None] * SSD_HPG for _ in ks]
        dpar = [jnp.zeros((1, LANES), F32) for _ in hs]
        for k in ks:
            for j in hs:
                dt_col, dt_row, a_row1, a_11, _, _, dte_col, e_last = terms[k][j]
                dww = dw_col[k][j] * (dt_col * dte_col)
                last_add = jnp.sum(dww, axis=0, keepdims=True) + e_last * s_sum[k][j]
                dcum_col = rw_col[k][j] + r1_col[k][j] - dww + jnp.where(tcol == q - 1, last_add, 0.0)
                da_row = jnp.sum(jnp.where(tt >= ss, dcum_col, 0.0), axis=0, keepdims=True)
                da_col = jnp.sum(jnp.where(ss >= tt, -cw_row[k][j], 0.0), axis=1, keepdims=True)
                ddt_col = a_11 * da_col + dw_col[k][j] * dte_col
                ddt_rows[k][j] = (a_row1 * da_row + cwp_row[k][j]
                                  + jnp.sum(jnp.where(tt == ss, ddt_col, 0.0), axis=0, keepdims=True))
                d_a = jnp.sum(dt_row * da_row, axis=1, keepdims=True) + jnp.sum(dt_col * da_col, axis=0, keepdims=True)
                dpar[j] = dpar[j] + jnp.where(lane == 0, d_a * a_11, 0.0) + jnp.where(lane == 1, d_d[k][j], 0.0)
        dxs = []
        for k in ks:
            acc = jnp.zeros((q, SSD_GW), F32)
            for j in hs:
                acc = jnp.where(hm[j], dxj[k][j], acc)
            dxs.append(acc + w_all[k] * bds[k] + d_all * dyv[k])
        xw = [(x[k] * w_all[k]).astype(BF16) for k in ks]
        dc = [_dot_nn(dg[k], bm[k]) + _dot_nn(dye[k], s_b[k]) for k in ks]
        db = [_dot_tn(dg[k], cm[k]) + _dot_nn(xw[k], ds_b[k]) for k in ks]
        dx_ref[...] = jnp.concatenate(dxs, axis=0)
        dc_ref[...] = jnp.concatenate(dc, axis=0)
        db_ref[...] = jnp.concatenate(db, axis=0)
        ddt_ref[...] = jnp.concatenate([jnp.concatenate([ddt_rows[k][j] for k in ks], axis=1) for j in hs], axis=0)
        dpar_ref[...] += jnp.concatenate(dpar, axis=0)

    blk = lambda width, off: pl.BlockSpec((rows, width), lambda g, c: (rev(c), off + g))
    par_s = pl.BlockSpec((None, SSD_HPG, LANES), lambda g, c: (g, 0, 0))
    outs = pl.pallas_call(
        body, grid=(SSD_N_GROUPS, nst),
        in_specs=[blk(SSD_GW, 0), blk(SSD_D_STATE, SSD_BC_COL0), blk(SSD_D_STATE, SSD_BC_COL0 + SSD_N_GROUPS),
                  pl.BlockSpec((None, SSD_HPG, rows), lambda g, c: (g, 0, rev(c))),
                  pl.BlockSpec((None, SSD_HPG, rows), lambda g, c: (g, 0, rev(c))), par_s, par_s,
                  pl.BlockSpec((None, kc, SSD_GW, SSD_D_STATE), lambda g, c: (g, rev(c), 0, 0)), blk(SSD_GW, 0)] + hk.in_specs,
        out_specs=[blk(SSD_GW, 0), blk(SSD_D_STATE, 0), blk(SSD_D_STATE, 0),
                   pl.BlockSpec((None, SSD_HPG, rows), lambda g, c: (g, 0, rev(c))), par_s] + hk.out_specs,
        out_shape=[jax.ShapeDtypeStruct((t, SSD_D_INNER), F32),
                   jax.ShapeDtypeStruct((t, SSD_N_GROUPS * SSD_D_STATE), F32),
                   jax.ShapeDtypeStruct((t, SSD_N_GROUPS * SSD_D_STATE), F32),
                   jax.ShapeDtypeStruct((SSD_N_GROUPS, SSD_HPG, t), F32),
                   jax.ShapeDtypeStruct((SSD_N_GROUPS, SSD_HPG, LANES), F32)] + hk.out_shape,
        scratch_shapes=[pltpu.VMEM((SSD_GW, SSD_D_STATE), F32)] + hk.scratch,
        compiler_params=_params(*hk.semantics("parallel", "arbitrary")), name=name)(
            xc, xc, xc, dtr, cumr, alog_b, d_b, states, dy, *hk.inputs)
    return outs if hook is None else (outs[:5], outs[5:])


def _gate_norm_fwd(y, zx, norm_w, *, name):
    t = y.shape[0]
    tr = _row_tile(t, 512)
    row = pl.BlockSpec((tr, SSD_D_INNER), lambda i: (i, 0))

    def body(y_ref, z_ref, w_ref, o_ref):
        for gi in range(SSD_N_GROUPS):
            sl = pl.ds(gi * SSD_GW, SSD_GW)
            z = z_ref[:, sl].astype(F32)
            gv = y_ref[:, sl].astype(F32) * (z * _sigmoid(z))
            r = lax.rsqrt(jnp.mean(gv * gv, axis=-1, keepdims=True) + NORM_EPS)
            o_ref[:, sl] = (gv * r * w_ref[:, sl]).astype(BF16)

    return pl.pallas_call(
        body, grid=(t // tr,), in_specs=[row, row, pl.BlockSpec((1, SSD_D_INNER), lambda i: (0, 0))],
        out_specs=row, out_shape=jax.ShapeDtypeStruct((t, SSD_D_INNER), BF16),
        compiler_params=_params("parallel"), name=name)(y, zx, norm_w)


def _gate_norm_bwd(y, zx, norm_w, dyn, *, name):
    t = y.shape[0]
    tr = _row_tile(t, 512)
    row = pl.BlockSpec((tr, SSD_D_INNER), lambda i: (i, 0))
    vec = pl.BlockSpec((1, SSD_D_INNER), lambda i: (0, 0))

    def body(y_ref, z_ref, w_ref, dyn_ref, dy_ref, dz_ref, dw_ref):
        @pl.when(pl.program_id(0) == 0)
        def _():
            dw_ref[...] = jnp.zeros_like(dw_ref)

        for gi in range(SSD_N_GROUPS):
            sl = pl.ds(gi * SSD_GW, SSD_GW)
            z = z_ref[:, sl].astype(F32)
            yv = y_ref[:, sl].astype(F32)
            sg = _sigmoid(z)
            sz = z * sg
            gv = yv * sz
            r = lax.rsqrt(jnp.mean(gv * gv, axis=-1, keepdims=True) + NORM_EPS)
            ghat = gv * r
            dout = dyn_ref[:, sl].astype(F32)
            dgh = dout * w_ref[:, sl]
            dgv = r * (dgh - ghat * jnp.mean(dgh * ghat, axis=-1, keepdims=True))
            dy_ref[:, sl] = (dgv * sz).astype(dy_ref.dtype)
            dz_ref[:, sl] = (dgv * yv * (sg * (1.0 + z * (1.0 - sg)))).astype(dz_ref.dtype)
            dw_ref[:, sl] += jnp.sum(dout * ghat, axis=0, keepdims=True)

    return pl.pallas_call(
        body, grid=(t // tr,), in_specs=[row, row, vec, row], out_specs=[row, row, vec],
        out_shape=[jax.ShapeDtypeStruct((t, SSD_D_INNER), BF16), jax.ShapeDtypeStruct((t, SSD_IN_PAD), BF16),
                   jax.ShapeDtypeStruct((1, SSD_D_INNER), F32)],
        compiler_params=_params("arbitrary"), name=name)(y, zx, norm_w, dyn)


ATTN_KV_W = ATTN_N_KV * ATTN_HEAD_DIM
ATTN_Q_HALF = 512
ATTN_K_BLK = ATTN_N_Q * ATTN_HEAD_DIM // ATTN_KV_W
ATTN_V_BLK = ATTN_K_BLK + 1


def _attn_valid(first_block):
    w = ATTN_WINDOW
    qpos = lax.broadcasted_iota(jnp.int32, (w, 2 * w), 0) + w
    kpos = lax.broadcasted_iota(jnp.int32, (w, 2 * w), 1)
    rel = qpos - kpos
    return (rel >= 0) & (rel < w) & jnp.logical_not(first_block & (kpos < w))


def _attn_head_views(lo_ref, hi_ref):
    hd = ATTN_HEAD_DIM
    per_half = ATTN_Q_HALF // hd
    return [(lo_ref if h < per_half else hi_ref)[:, pl.ds((h % per_half) * hd, hd)] for h in range(ATTN_N_Q)]


def _attn_block_views(lo_ref, hi_ref, kc_ref, kp_ref, vc_ref, vp_ref):
    hd = ATTN_HEAD_DIM
    kv_cols = [pl.ds(kh * hd, hd) for kh in range(ATTN_N_KV)]
    kb = [jnp.concatenate([kp_ref[:, c], kc_ref[:, c]], axis=0) for c in kv_cols]
    vb = [jnp.concatenate([vp_ref[:, c], vc_ref[:, c]], axis=0) for c in kv_cols]
    return _attn_head_views(lo_ref, hi_ref), kb, vb


def _attn_scores(q, kb, valid):
    scale = ATTN_HEAD_DIM ** -0.5
    return [jnp.where(valid, _dot_nt(q[h], kb[h // ATTN_REP]) * scale, -jnp.inf) for h in range(ATTN_N_Q)]


def _attn_softmax(s, sink):
    heads = range(ATTN_N_Q)
    m = [jnp.maximum(jnp.max(s[h], axis=1, keepdims=True), sink[h]) for h in heads]
    e = [jnp.exp(s[h] - m[h]) for h in heads]
    es = [jnp.exp(sink[h] - m[h]) for h in heads]
    inv = [1.0 / (jnp.sum(e[h], axis=1, keepdims=True) + es[h]) for h in heads]
    return e, es, inv


def _attn_fwd(qkv, sinks_b, *, name, hook=None):
    t = qkv.shape[0]
    w = ATTN_WINDOW
    nb = t // w
    prev = lambda n: jnp.maximum(n - 1, 0)
    hk = _HookSlots(hook, n_in=7, n_out=1, n_scratch=0)

    def body(*refs):
        (qlo_ref, qhi_ref, kc_ref, kp_ref, vc_ref, vp_ref, sink_ref), (o_ref,), _ = hk.own(refs)
        if hook is not None:
            hk.run(refs, pl.program_id(0), nb)
        heads = range(ATTN_N_Q)
        q, kb, vb = _attn_block_views(qlo_ref, qhi_ref, kc_ref, kp_ref, vc_ref, vp_ref)
        sink = [sink_ref[h:h + 1, 0:1] for h in heads]
        e, _, inv = _attn_softmax(_attn_scores(q, kb, _attn_valid(pl.program_id(0) == 0)), sink)
        out = [_dot_nn((e[h] * inv[h]).astype(BF16), vb[h // ATTN_REP]).astype(o_ref.dtype) for h in heads]
        o_ref[...] = jnp.concatenate(out, axis=1)

    qh = lambda half: pl.BlockSpec((w, ATTN_Q_HALF), lambda n: (n, half))
    kv = lambda blk, idx: pl.BlockSpec((w, ATTN_KV_W), lambda n: (idx(n), blk))
    cur = lambda n: n
    outs = pl.pallas_call(
        body, grid=(nb,),
        in_specs=[qh(0), qh(1), kv(ATTN_K_BLK, cur), kv(ATTN_K_BLK, prev), kv(ATTN_V_BLK, cur), kv(ATTN_V_BLK, prev),
                  pl.BlockSpec((ATTN_N_Q, LANES), lambda n: (0, 0))] + hk.in_specs,
        out_specs=[pl.BlockSpec((w, D_MODEL), lambda n: (n, 0))] + hk.out_specs,
        out_shape=[jax.ShapeDtypeStruct((t, D_MODEL), BF16)] + hk.out_shape,
        scratch_shapes=hk.scratch,
        compiler_params=_params(*hk.semantics("parallel")), name=name)(qkv, qkv, qkv, qkv, qkv, qkv, sinks_b, *hk.inputs)
    return outs[0] if hook is None else (outs[0], outs[1:])


def _attn_bwd(qkv, sinks_b, dout, *, name):
    t = qkv.shape[0]
    w = ATTN_WINDOW
    nb = t // w
    hd = ATTN_HEAD_DIM
    clamp = lambda n: jnp.minimum(n, nb - 1)
    prev = lambda n: jnp.maximum(clamp(n) - 1, 0)

    def body(qlo_ref, qhi_ref, kc_ref, kp_ref, vc_ref, vp_ref, sink_ref, dolo_ref, dohi_ref,
             dq_ref, dkv_ref, dsink_ref, carry):
        n = pl.program_id(0)

        @pl.when(n == 0)
        def _():
            carry[...] = jnp.zeros_like(carry)
            dsink_ref[...] = jnp.zeros_like(dsink_ref)

        @pl.when(n < nb)
        def _():
            heads, kvs = range(ATTN_N_Q), range(ATTN_N_KV)
            q, kb, vb = _attn_block_views(qlo_ref, qhi_ref, kc_ref, kp_ref, vc_ref, vp_ref)
            do = _attn_head_views(dolo_ref, dohi_ref)
            sink = [sink_ref[h:h + 1, 0:1] for h in heads]
            s = _attn_scores(q, kb, _attn_valid(n == 0))
            dp = [_dot_nt(do[h], vb[h // ATTN_REP]) for h in heads]
            e, es, inv = _attn_softmax(s, sink)
            p = [e[h] * inv[h] for h in heads]
            delta = [jnp.sum(p[h] * dp[h], axis=1, keepdims=True) for h in heads]
            dsc = [(p[h] * (dp[h] - delta[h]) * (hd ** -0.5)).astype(BF16) for h in heads]
            pb = [p[h].astype(BF16) for h in heads]
            dq = [_dot_nn(dsc[h], kb[h // ATTN_REP]).astype(dq_ref.dtype) for h in heads]
            stack = lambda per_head, kh: jnp.concatenate(per_head[kh * ATTN_REP:(kh + 1) * ATTN_REP], axis=0)
            dkb = [_dot_tn(stack(dsc, kh), stack(q, kh)) for kh in kvs]
            dvb = [_dot_tn(stack(pb, kh), stack(do, kh)) for kh in kvs]
            dsink = [jnp.broadcast_to(jnp.sum(-es[h] * inv[h] * delta[h], axis=0, keepdims=True), (1, LANES)) for h in heads]
            dq_ref[...] = jnp.concatenate(dq, axis=1)
            dsink_ref[...] += jnp.concatenate(dsink, axis=0)
            dkv_ref[...] = (carry[...] + jnp.concatenate([d[0:w, :] for d in dkb + dvb], axis=1)).astype(dkv_ref.dtype)
            carry[...] = jnp.concatenate([d[w:2 * w, :] for d in dkb + dvb], axis=1)

        @pl.when(n == nb)
        def _():
            dkv_ref[...] = carry[...].astype(dkv_ref.dtype)

    qh = lambda half: pl.BlockSpec((w, ATTN_Q_HALF), lambda n: (clamp(n), half))
    kv = lambda blk, idx: pl.BlockSpec((w, ATTN_KV_W), lambda n: (idx(n), blk))
    return pl.pallas_call(
        body, grid=(nb + 1,),
        in_specs=[qh(0), qh(1), kv(ATTN_K_BLK, clamp), kv(ATTN_K_BLK, prev), kv(ATTN_V_BLK, clamp), kv(ATTN_V_BLK, prev),
                  pl.BlockSpec((ATTN_N_Q, LANES), lambda n: (0, 0)), qh(0), qh(1)],
        out_specs=[pl.BlockSpec((w, D_MODEL), lambda n: (clamp(n), 0)),
                   pl.BlockSpec((w, 2 * ATTN_KV_W), lambda n: (jnp.maximum(n - 1, 0), 0)),
                   pl.BlockSpec((ATTN_N_Q, LANES), lambda n: (0, 0))],
        out_shape=[jax.ShapeDtypeStruct((t, D_MODEL), BF16), jax.ShapeDtypeStruct((t, 2 * ATTN_KV_W), BF16),
                   jax.ShapeDtypeStruct((ATTN_N_Q, LANES), F32)],
        scratch_shapes=[pltpu.VMEM((w, 2 * ATTN_KV_W), F32)],
        compiler_params=_params("arbitrary"), name=name)(qkv, qkv, qkv, qkv, qkv, qkv, sinks_b, dout, dout)


def _sq_relu_epilogue(acc):
    r = jnp.maximum(acc, 0.0)
    return (r * r,)


def _sq_relu_bwd_epilogue(acc, act):
    return (acc * (2.0 * jnp.sqrt(act.astype(F32))),)


def _bias_epilogue(acc, bias):
    return (acc + bias,)


def _plain_run(stage, fn, *args, **kwargs):
    return fn(*args, **kwargs)


def _mlp_fwd(u, w_up, w_down, tag, run=_plain_run):
    act = run(f"mlp_up_{tag}", _matmul, u, w_up, mode="nn", out_dtypes=(BF16,), epilogue=_sq_relu_epilogue, b_shards=True,
              tm=BIG_TILE, name=f"mlp_up_{tag}")
    f = run(f"mlp_down_{tag}", _matmul, act, w_down, mode="nn", out_dtypes=(BF16,), tk=BIG_TILE, name=f"mlp_down_{tag}")
    return act, f


def _mlp_bwd(u, act, w_up, w_down, df, tag):
    dpre = _matmul(df, w_down, mode="nt", out_dtypes=(BF16,), epilogue=_sq_relu_bwd_epilogue,
                   extras=((act, "tile"),), name=f"mlp_dact_{tag}")
    dw_down = _matmul(act, df, mode="tn", out_dtypes=(BF16,), tk=BIG_TILE, name=f"mlp_dwdown_{tag}")
    du = _matmul(dpre, w_up, mode="nt", out_dtypes=(BF16,), b_shards=True, tm=BIG_TILE, name=f"mlp_du_{tag}")
    dw_up = _matmul(u, dpre, mode="tn", out_dtypes=(BF16,), out_shards=True, tk=BIG_TILE, name=f"mlp_dwup_{tag}")
    return du, dw_up, dw_down


def _head_param_rows(p):
    return jnp.broadcast_to(p.reshape(SSD_N_GROUPS, SSD_HPG, 1), (SSD_N_GROUPS, SSD_HPG, LANES))


def _local_step(x, target, wts, comm=None, u0=None):
    wts = dict(wts)
    row = lambda v: v.reshape(1, -1)
    mix_pre, mix_post, ffn_pre, ffn_post = wts["mix_pre_norm"], wts["mix_post_norm"], wts["ffn_pre_norm"], wts["ffn_post_norm"]

    def gathering(stage, fn, *args, **kwargs):
        hook = comm.gather_hook(stage) if comm is not None else None
        if hook is None:
            return fn(*args, **kwargs)
        out, got = fn(*args, hook=hook, **kwargs)
        wts.update(comm.weights_from(stage, got))
        return out

    if u0 is None:
        u0 = _rms_fwd(x, row(mix_pre[0]), name="rms_pre_mix0")
    zx, dt_raw = gathering("in_proj", _matmul, u0, wts["ssd_w_in"], mode="nn", out_dtypes=(BF16,), tn=SSD_IN_TILE,
                           f32_block=SSD_DT_COL - (SSD_IN_PAD - SSD_IN_TILE),
                           name="ssd_in_proj")
    xc = gathering("conv", _conv_fwd, zx, wts["ssd_conv_w"], row(wts["ssd_conv_b"]), name="ssd_conv_fwd")
    bias_row = jnp.pad(wts["ssd_dt_bias"], (0, LANES - SSD_N_HEADS)).reshape(1, LANES)
    alog_row = jnp.pad(wts["ssd_a_log"], (0, LANES - SSD_N_HEADS)).reshape(1, LANES)
    dtr, cumr = _softplus_fwd(dt_raw, bias_row, alog_row, name="ssd_dt_fwd")
    alog_b, d_b = _head_param_rows(wts["ssd_a_log"]), _head_param_rows(wts["ssd_d"])
    y_ssd, states = gathering("scan", _ssd_fwd, xc, dtr, cumr, alog_b, d_b, name="ssd_scan_fwd")
    norm_w = row(wts["ssd_norm_w"])
    yn = _gate_norm_fwd(y_ssd, zx, norm_w, name="ssd_gate_norm_fwd")
    mix0 = _matmul(yn, wts["ssd_w_out"], mode="nn", out_dtypes=(BF16,), tk=BIG_TILE, name="ssd_out_proj")
    h1, v0 = _rms_fwd(mix0, row(mix_post[0]), resid=x, want_u=row(ffn_pre[0]), name="rms_post_mix0")
    act0, f0 = _mlp_fwd(v0, wts["mlp_w_up0"], wts["mlp_w_down0"], "l0", run=gathering)
    h2, u1 = _rms_fwd(f0, row(ffn_post[0]), resid=h1, want_u=row(mix_pre[1]), name="rms_post_ffn0")

    qkv = _matmul(u1, wts["attn_w_qkv"], mode="nn", out_dtypes=(BF16,), epilogue=_bias_epilogue,
                  extras=((row(wts["attn_b_qkv"]), "row"),), b_shards=True, name="attn_qkv_proj")
    sinks_b = jnp.broadcast_to(wts["attn_sinks"].reshape(ATTN_N_Q, 1), (ATTN_N_Q, LANES))
    ao = gathering("attn_fwd", _attn_fwd, qkv, sinks_b, name="attn_fwd")
    mix1 = _matmul(ao, wts["attn_w_o"], mode="nn", out_dtypes=(BF16,), epilogue=_bias_epilogue,
                   extras=((row(wts["attn_b_o"]), "row"),), name="attn_out_proj")
    h3, v1 = _rms_fwd(mix1, row(mix_post[1]), resid=h2, want_u=row(ffn_pre[1]), name="rms_post_mix1")
    act1, f1 = _mlp_fwd(v1, wts["mlp_w_up1"], wts["mlp_w_down1"], "l1")
    dh4, loss_tile = _rms_fwd(f1, row(ffn_post[1]), resid=h3, target=target, name="rms_post_ffn1_loss")

    df1, g_ffn_post1 = _rms_bwd(f1, row(ffn_post[1]), dh4, out_dtype=BF16, name="rms_post_ffn1_bwd")
    dv1, g_up1, g_down1 = _mlp_bwd(v1, act1, wts["mlp_w_up1"], wts["mlp_w_down1"], df1, "l1")
    dh3, g_ffn_pre1 = _rms_bwd(h3, row(ffn_pre[1]), dv1, resid=dh4, name="rms_pre_ffn1_bwd")
    dmix1, g_mix_post1, g_b_o = _rms_bwd(mix1, row(mix_post[1]), dh3, out_dtype=BF16, dx_col_sum=True, name="rms_post_mix1_bwd")
    g_w_o = _matmul(ao, dmix1, mode="tn", out_dtypes=(BF16,), tk=BIG_TILE, name="attn_dwo")
    dao = _matmul(dmix1, wts["attn_w_o"], mode="nt", out_dtypes=(BF16,), name="attn_dao")
    dq, dkv, g_sinks = _attn_bwd(qkv, sinks_b, dao, name="attn_bwd")
    dqkv = jnp.concatenate([dq, dkv], axis=1)
    g_b_qkv = _col_sum(dqkv, name="attn_bqkv_grad")
    g_w_qkv = _matmul(u1, dqkv, mode="tn", out_dtypes=(BF16,), tn=ATTN_QKV // N_CHIPS, out_shards=True, tk=BIG_TILE, name="attn_dwqkv")
    du1 = _matmul(dqkv, wts["attn_w_qkv"], mode="nt", out_dtypes=(BF16,), b_shards=True, name="attn_du")
    dh2, g_mix_pre1 = _rms_bwd(h2, row(mix_pre[1]), du1, resid=dh3, name="rms_pre_mix1_bwd")

    df0, g_ffn_post0 = _rms_bwd(f0, row(ffn_post[0]), dh2, out_dtype=BF16, name="rms_post_ffn0_bwd")
    dv0, g_up0, g_down0 = _mlp_bwd(v0, act0, wts["mlp_w_up0"], wts["mlp_w_down0"], df0, "l0")
    dh1, g_ffn_pre0 = _rms_bwd(h1, row(ffn_pre[0]), dv0, resid=dh2, name="rms_pre_ffn0_bwd")
    dmix0, g_mix_post0 = _rms_bwd(mix0, row(mix_post[0]), dh1, out_dtype=BF16, name="rms_post_mix0_bwd")
    g_w_out = _matmul(yn, dmix0, mode="tn", out_dtypes=(BF16,), tk=BIG_TILE, name="ssd_dwout")
    dyn = _matmul(dmix0, wts["ssd_w_out"], mode="nt", out_dtypes=(BF16,), name="ssd_dyn")
    dy_ssd, dzx, g_norm_w = _gate_norm_bwd(y_ssd, zx, norm_w, dyn, name="ssd_gate_norm_bwd")
    mats = {"ssd_w_out": g_w_out, "attn_w_qkv": g_w_qkv, "attn_w_o": g_w_o,
            "mlp_w_up0": g_up0, "mlp_w_up1": g_up1, "mlp_w_down0": g_down0, "mlp_w_down1": g_down1}
    if comm is None:
        dxc, dbm, dcm, ddt_r, dpar = _ssd_bwd(xc, dtr, cumr, alog_b, d_b, states, dy_ssd, name="ssd_scan_bwd")
    else:
        (dxc, dbm, dcm, ddt_r, dpar), received = _ssd_bwd(xc, dtr, cumr, alog_b, d_b, states, dy_ssd,
                                                          name="ssd_scan_bwd", hook=comm.exchange_hook(mats, "early"))
        comm.received(received)
    dzx, g_conv_w, g_conv_b = _conv_bwd(zx, wts["ssd_conv_w"], row(wts["ssd_conv_b"]), dxc, dbm, dcm, dzx, name="ssd_conv_bwd")
    dzx, g_dt_bias = _softplus_bwd(dt_raw, bias_row, ddt_r, dzx, name="ssd_dt_bwd")
    g_w_in = _w_in_to_shards(_matmul(u0, dzx, mode="tn", out_dtypes=(BF16,), tn=SSD_IN_TILE, tk=BIG_TILE, name="ssd_dwin"), name="ssd_dwin_shards")
    mats["ssd_w_in"] = g_w_in
    if comm is None:
        du0 = _matmul(dzx, wts["ssd_w_in"], mode="nt", out_dtypes=(BF16,), tk=SSD_IN_TILE, name="ssd_du")
    else:
        du0, received = _matmul(dzx, wts["ssd_w_in"], mode="nt", out_dtypes=(BF16,), tk=SSD_IN_TILE, name="ssd_du",
                                hook=comm.exchange_hook(mats, "late"))
        comm.received(received)
    grad_x, g_mix_pre0 = _rms_bwd(x, row(mix_pre[0]), du0, resid=dh1, name="rms_pre_mix0_bwd")

    dpar = dpar.reshape(SSD_N_HEADS, LANES)
    vecs = {
        "ssd_conv_w": g_conv_w, "ssd_conv_b": g_conv_b.reshape(-1),
        "ssd_dt_bias": g_dt_bias[0, :SSD_N_HEADS], "ssd_a_log": dpar[:, 0], "ssd_d": dpar[:, 1],
        "ssd_norm_w": g_norm_w.reshape(-1), "attn_b_qkv": g_b_qkv.reshape(-1), "attn_sinks": g_sinks[:, 0],
        "attn_b_o": g_b_o.reshape(-1),
        "mix_pre_norm": jnp.concatenate([g_mix_pre0, g_mix_pre1]), "mix_post_norm": jnp.concatenate([g_mix_post0, g_mix_post1]),
        "ffn_pre_norm": jnp.concatenate([g_ffn_pre0, g_ffn_pre1]), "ffn_post_norm": jnp.concatenate([g_ffn_post0, g_ffn_post1]),
    }
    return loss_tile, grad_x, mats, vecs


def _mesh_position():
    return lax.axis_index("x"), lax.axis_index("y"), lax.axis_index("c")


def _flip(v, bit):
    return 1 - v if bit else v


OTHER_CHIPS = ((1, 0), (0, 1), (1, 1))


def _comm_params():
    return pltpu.CompilerParams(vmem_limit_bytes=VMEM_LIMIT)


def _staged_copies(srcs, dsts, bufs, sems_in, sems_out):
    loads = [pltpu.make_async_copy(s, b, sems_in.at[i]) for i, (s, b) in enumerate(zip(srcs, bufs))]
    stores = [pltpu.make_async_copy(b, d, sems_out.at[i]) for i, (b, d) in enumerate(zip(bufs, dsts))]
    return loads, stores


class _GatherHook:
    def __init__(self, mats, vecs=()):
        self.arrs = list(mats) + list(vecs)
        self.nm, self.n = len(mats), len(self.arrs)
        n_ici, n_fwd = (N_CHIPS - 1) * self.n, max((N_CHIPS - 1) * self.nm, 1)
        dma = pltpu.SemaphoreType.DMA
        self.out_shape = [jax.ShapeDtypeStruct((N_CHIPS,) + a.shape, a.dtype) for a in self.arrs]
        self.scratch = [pltpu.VMEM(a.shape, a.dtype) for a in self.arrs] + [
            dma((n_ici,)), dma((n_ici,)), dma((n_fwd,)), dma((n_fwd,)), dma((self.n,)), dma((self.n,))]

    def plan(self, ins, outs, scratch):
        n, nm = self.n, self.nm
        bufs = scratch[:n]
        ici_send, ici_recv, fwd_send, fwd_recv, load_sems, store_sems = scratch[n:]
        xi, yi, ci = _mesh_position()
        me = 2 * xi + yi
        loads, stores = _staged_copies(ins, [outs[i].at[me] for i in range(n)], bufs, load_sems, store_sems)
        sends, landed, forwards, from_sibling = [], [], [], []
        for j, (bx, by) in enumerate(OTHER_CHIPS):
            px, py = _flip(xi, bx), _flip(yi, by)
            peer = 2 * px + py
            for i in range(n):
                k = j * n + i
                mk = functools.partial(pltpu.make_async_remote_copy, send_sem=ici_send.at[k], recv_sem=ici_recv.at[k],
                                       device_id=(px, py, ci), device_id_type=MESH)
                if i < nm:
                    sends.append(mk(src_ref=ins[i].at[ci], dst_ref=outs[i].at[me, ci]))
                    landed.append(mk(src_ref=ins[i].at[ci], dst_ref=outs[i].at[peer, ci]))
                    kf = j * nm + i
                    fw = functools.partial(pltpu.make_async_remote_copy, send_sem=fwd_send.at[kf], recv_sem=fwd_recv.at[kf],
                                           device_id=(xi, yi, 1 - ci), device_id_type=MESH)
                    forwards.append(fw(src_ref=outs[i].at[peer, ci], dst_ref=outs[i].at[peer, ci]))
                    from_sibling.append(fw(src_ref=outs[i].at[peer, ci], dst_ref=outs[i].at[peer, 1 - ci]))
                else:
                    sends.append(mk(src_ref=ins[i], dst_ref=outs[i].at[me]))
                    landed.append(mk(src_ref=ins[i], dst_ref=outs[i].at[peer]))
                    forwards.append(None)
        return loads, stores, sends, landed, forwards, from_sibling

    @staticmethod
    def start(p):
        loads, _, sends, _, _, _ = p
        for cp in loads + sends:
            cp.start()

    @staticmethod
    def relay(p):
        loads, stores, _, landed, forwards, _ = p
        for ld, st in zip(loads, stores):
            ld.wait()
            st.start()
        for cp, fw in zip(landed, forwards):
            cp.wait_recv()
            if fw is not None:
                fw.start()

    @staticmethod
    def finish(p):
        _, stores, sends, _, forwards, from_sibling = p
        for cp in from_sibling:
            cp.wait_recv()
        for cp in sends + [fw for fw in forwards if fw is not None]:
            cp.wait_send()
        for st in stores:
            st.wait()


def _run_hook(hook, ins, outs, scratch, step, n_steps):
    p = hook.plan(ins, outs, scratch)
    relay_step = min(max(1, (3 * n_steps) // 4), n_steps - 1)

    @pl.when(step == 0)
    def _():
        hook.start(p)

    if relay_step < n_steps - 1:
        @pl.when(step == relay_step)
        def _():
            hook.relay(p)

    @pl.when(step == n_steps - 1)
    def _():
        if relay_step == n_steps - 1:
            hook.relay(p)
        hook.finish(p)


def _hook_call(hook, *, name):
    n = len(hook.arrs)

    def body(*refs):
        p = hook.plan(refs[:n], refs[n:n + len(hook.out_shape)], refs[n + len(hook.out_shape):])
        hook.start(p)
        hook.relay(p)
        hook.finish(p)

    return pl.pallas_call(
        body, in_specs=[ANY] * n, out_specs=[ANY] * len(hook.out_shape), out_shape=hook.out_shape,
        scratch_shapes=hook.scratch, compiler_params=_comm_params(), name=name)(*hook.arrs)


def _send_other_half(parts, *, name):
    n = len(parts)

    def body(*refs):
        ins, outs = refs[:n], refs[n:2 * n]
        send_sems, recv_sems = refs[2 * n:]
        xi, yi, ci = _mesh_position()
        sibling = (xi, yi, 1 - ci)
        for i in range(n):
            for s in range(N_CHIPS):
                pltpu.make_async_remote_copy(src_ref=ins[i].at[s, 1 - ci], dst_ref=outs[i].at[s], send_sem=send_sems.at[i],
                                             recv_sem=recv_sems.at[i], device_id=sibling, device_id_type=MESH).start()
        for i in range(n):
            pltpu.make_async_remote_copy(src_ref=outs[i], dst_ref=outs[i], send_sem=send_sems.at[i], recv_sem=recv_sems.at[i],
                                         device_id=sibling, device_id_type=MESH).wait()

    return pl.pallas_call(
        body, in_specs=[ANY] * n, out_specs=[ANY] * n,
        out_shape=[jax.ShapeDtypeStruct((p.shape[0],) + p.shape[2:], p.dtype) for p in parts],
        scratch_shapes=[pltpu.SemaphoreType.DMA((n,)), pltpu.SemaphoreType.DMA((n,))],
        name=name)(*parts)


ROW_BLOCKS = 8
SUM_ROW_BLOCKS = 2


def _add_sibling_half(parts, theirs, core, *, name):
    n = len(parts)

    def body(core_ref, *refs):
        for a_ref, b_ref, o_ref in zip(refs[:n], refs[n:2 * n], refs[2 * n:]):
            o_ref[...] = (a_ref[...].astype(F32) + b_ref[...].astype(F32)).astype(o_ref.dtype)

    nb = SUM_ROW_BLOCKS
    mine = lambda p: pl.BlockSpec((None, None, p.shape[2] // nb, p.shape[3]), lambda s, rb, core_ref: (s, core_ref[0], rb, 0))
    other = lambda p: pl.BlockSpec((None, p.shape[1] // nb, p.shape[2]), lambda s, rb, core_ref: (s, rb, 0))
    return pl.pallas_call(
        body,
        grid_spec=pltpu.PrefetchScalarGridSpec(
            num_scalar_prefetch=1, grid=(N_CHIPS, nb),
            in_specs=[mine(p) for p in parts] + [other(q) for q in theirs], out_specs=[other(q) for q in theirs]),
        out_shape=[jax.ShapeDtypeStruct(q.shape, BF16) for q in theirs],
        compiler_params=_params("parallel", "parallel"), name=name)(core, *parts, *theirs)


class _ExchangeHook:
    def __init__(self, parts, to_all=()):
        self.arrs = list(parts) + list(to_all)
        self.n_parts, self.n = len(parts), len(self.arrs)
        n_ici, n_peer = max((N_CHIPS - 1) * self.n_parts, 1), (N_DEV - 1) * max(len(to_all), 1)
        dma = pltpu.SemaphoreType.DMA
        self.out_shape = [jax.ShapeDtypeStruct(p.shape, p.dtype) for p in parts] + [
            jax.ShapeDtypeStruct((N_DEV,) + a.shape, a.dtype) for a in to_all]
        self.scratch = [pltpu.VMEM(p.shape[1:], p.dtype) for p in parts] + [pltpu.VMEM(a.shape, a.dtype) for a in to_all] + [
            dma((n_ici,)), dma((n_ici,)), dma((n_peer,)), dma((n_peer,)), dma((self.n,)), dma((self.n,))]

    def plan(self, ins, outs, scratch):
        n, npt = self.n, self.n_parts
        bufs = scratch[:n]
        send_sems, recv_sems, all_send, all_recv, load_sems, store_sems = scratch[n:]
        xi, yi, ci = _mesh_position()
        me_chip = 2 * xi + yi
        me = 4 * xi + 2 * yi + ci
        loads, stores = _staged_copies([ins[i].at[me_chip] for i in range(npt)] + list(ins[npt:]),
                                       [outs[i].at[me_chip] for i in range(npt)] + [outs[i].at[me] for i in range(npt, n)],
                                       bufs, load_sems, store_sems)
        sends, recvs = [], []
        for j, (bx, by) in enumerate(OTHER_CHIPS):
            px, py = _flip(xi, bx), _flip(yi, by)
            peer = 2 * px + py
            for i in range(npt):
                k = j * npt + i
                mk = functools.partial(pltpu.make_async_remote_copy, src_ref=ins[i].at[peer], send_sem=send_sems.at[k],
                                       recv_sem=recv_sems.at[k], device_id=(px, py, ci), device_id_type=MESH)
                sends.append(mk(dst_ref=outs[i].at[me_chip]))
                recvs.append(mk(dst_ref=outs[i].at[peer]))
        for i in range(npt, n):
            for k in range(1, N_DEV):
                px, py, pc = _flip(xi, (k >> 2) & 1), _flip(yi, (k >> 1) & 1), _flip(ci, k & 1)
                slot = (i - npt) * (N_DEV - 1) + k - 1
                mk = functools.partial(pltpu.make_async_remote_copy, src_ref=ins[i], send_sem=all_send.at[slot],
                                       recv_sem=all_recv.at[slot], device_id=(px, py, pc), device_id_type=MESH)
                sends.append(mk(dst_ref=outs[i].at[me]))
                recvs.append(mk(dst_ref=outs[i].at[4 * px + 2 * py + pc]))
        return loads, stores, sends, recvs

    @staticmethod
    def start(p):
        loads, _, sends, _ = p
        for cp in loads + sends:
            cp.start()

    @staticmethod
    def relay(p):
        loads, stores, _, _ = p
        for ld, st in zip(loads, stores):
            ld.wait()
            st.start()

    @staticmethod
    def finish(p):
        _, stores, sends, recvs = p
        for cp in recvs:
            cp.wait_recv()
        for cp in sends:
            cp.wait_send()
        for st in stores:
            st.wait()


def _sum_chips(parts, *, name):
    n = len(parts)
    p = parts[0].shape[0]

    def body(*refs):
        s = pl.program_id(1)
        for x_ref, o_ref in zip(refs[:n], refs[n:]):
            @pl.when(s == 0)
            def _():
                o_ref[...] = x_ref[...].astype(F32)

            @pl.when(s > 0)
            def _():
                o_ref[...] += x_ref[...].astype(F32)

    blocks = lambda q: SUM_ROW_BLOCKS if q.shape[1] % (16 * SUM_ROW_BLOCKS) == 0 else 1
    assert len({blocks(q) for q in parts}) == 1
    nb = blocks(parts[0])
    return pl.pallas_call(
        body, grid=(nb, p),
        in_specs=[pl.BlockSpec((None, q.shape[1] // nb, q.shape[2]), lambda rb, s: (s, rb, 0)) for q in parts],
        out_specs=[pl.BlockSpec((q.shape[1] // nb, q.shape[2]), lambda rb, s: (rb, 0)) for q in parts],
        out_shape=[jax.ShapeDtypeStruct(q.shape[1:], F32) for q in parts],
        compiler_params=_params("parallel", "arbitrary"), name=name)(*parts)


def _swap_halves(halves, layers, *, name, hook=None):
    n = len(halves)
    out_shapes, slots = [], []
    for i, h in enumerate(halves):
        pair = [p for p in layers if i in p]
        if pair and pair[0][1] == i:
            slots.append((slots[pair[0][0]][0], 1))
        elif pair:
            out_shapes.append(jax.ShapeDtypeStruct((2, 2) + h.shape, h.dtype))
            slots.append((len(out_shapes) - 1, 0))
        else:
            out_shapes.append(jax.ShapeDtypeStruct((2,) + h.shape, h.dtype))
            slots.append((len(out_shapes) - 1, None))
    n_out = len(out_shapes)
    hk = _HookSlots(hook, n_in=n, n_out=n_out, n_scratch=n + 4)

    def body(*refs):
        ins, outs, scratch = hk.own(refs)
        bufs = scratch[:n]
        send_sems, recv_sems, load_sems, store_sems = scratch[n:]
        if hook is not None:
            extra = hk.plan(refs)
            hook.start(extra)
        xi, yi, ci = _mesh_position()
        own, sends, recvs = [], [], []
        for i in range(n):
            o, layer = slots[i]
            dst = (lambda core: outs[o].at[core]) if layer is None else (lambda core: outs[o].at[layer, core])
            own.append(dst(ci))
            mk = functools.partial(pltpu.make_async_remote_copy, src_ref=ins[i], send_sem=send_sems.at[i],
                                   recv_sem=recv_sems.at[i], device_id=(xi, yi, 1 - ci), device_id_type=MESH)
            sends.append(mk(dst_ref=dst(ci)))
            recvs.append(mk(dst_ref=dst(1 - ci)))
        loads, stores = _staged_copies(ins, own, bufs, load_sems, store_sems)
        for cp in loads + sends:
            cp.start()
        for ld, st in zip(loads, stores):
            ld.wait()
            st.start()
        for cp in recvs:
            cp.wait_recv()
        for cp in sends:
            cp.wait_send()
        for st in stores:
            st.wait()
        if hook is not None:
            hook.relay(extra)
            hook.finish(extra)

    outs = pl.pallas_call(
        body, in_specs=[ANY] * n + hk.in_specs, out_specs=[ANY] * n_out + hk.out_specs, out_shape=out_shapes + hk.out_shape,
        scratch_shapes=[pltpu.VMEM(h.shape, h.dtype) for h in halves]
        + [pltpu.SemaphoreType.DMA((n,)), pltpu.SemaphoreType.DMA((n,)), pltpu.SemaphoreType.DMA((n,)), pltpu.SemaphoreType.DMA((n,))]
        + hk.scratch,
        compiler_params=_comm_params(), name=name)(*halves, *hk.inputs)
    return outs if hook is None else (outs[:n_out], outs[n_out:])


def _cast_bf16(layers, x, norm_w, *, name, hook=None):
    n = len(layers)
    hk = _HookSlots(hook, n_in=n + 2, n_out=n + 1, n_scratch=0)

    def body(*refs):
        ins, outs, _ = hk.own(refs)
        if hook is not None:
            hk.run(refs, pl.program_id(0), ROW_BLOCKS)
        for i_ref, o_ref in zip(ins[:n], outs[:n]):
            o_ref[...] = i_ref[...].astype(o_ref.dtype)
        xv = ins[n][...]
        outs[n][...] = (xv * lax.rsqrt(jnp.mean(xv * xv, axis=-1, keepdims=True) + NORM_EPS) * ins[n + 1][...]).astype(BF16)

    in_blk = lambda a, l: pl.BlockSpec((None, a.shape[1] // ROW_BLOCKS, a.shape[2]), lambda i: (l, i, 0))
    out_blk = lambda a: pl.BlockSpec((a.shape[1] // ROW_BLOCKS, a.shape[2]), lambda i: (i, 0))
    x_blk = pl.BlockSpec((x.shape[0] // ROW_BLOCKS, x.shape[1]), lambda i: (i, 0))
    outs = pl.pallas_call(
        body, grid=(ROW_BLOCKS,),
        in_specs=[in_blk(a, l) for a, l in layers] + [x_blk, pl.BlockSpec((1, x.shape[1]), lambda i: (0, 0))] + hk.in_specs,
        out_specs=[out_blk(a) for a, _ in layers] + [x_blk] + hk.out_specs,
        out_shape=[jax.ShapeDtypeStruct(a.shape[1:], BF16) for a, _ in layers] + [jax.ShapeDtypeStruct(x.shape, BF16)] + hk.out_shape,
        scratch_shapes=hk.scratch,
        compiler_params=_params(*hk.semantics("parallel")), name=name)(*[a for a, _ in layers], x, norm_w, *hk.inputs)
    own = (outs[:n], outs[n])
    return own if hook is None else (own, outs[n + 1:])


def _full_weight(name, gathered):
    s, _, r, c = gathered.shape
    if name == "ssd_w_in":
        return _w_in_from_shards(gathered.reshape(s, 2 * r, c), name="ssd_w_in_unshard")
    if name in ("attn_w_qkv", "mlp_w_up0", "mlp_w_up1"):
        return gathered.reshape(s, 2 * r, c)
    return gathered.reshape(s * 2 * r, c)


class _StepComm:
    GATHER = {"in_proj": ("mlp_w_up0", "attn_w_o"), "conv": ("mlp_w_down0",), "scan": ("ssd_w_out", "mlp_w_up1"),
              "mlp_up_l0": ("attn_w_qkv",), "attn_fwd": ("mlp_w_down1",)}
    EXCHANGE = {"early": ("ssd_w_out", "attn_w_qkv", "attn_w_o", "mlp_w_up0", "mlp_w_up1", "mlp_w_down0", "mlp_w_down1"),
                "late": ("ssd_w_in",)}

    def __init__(self, shards, core):
        self.shards, self.core = shards, core
        self.chip_parts = {}
        self._pending = None

    def gather_hook(self, stage):
        names = self.GATHER.get(stage)
        return _GatherHook([self.shards[n] for n in names]) if names else None

    def weights_from(self, stage, gathered):
        return {n: _full_weight(n, g) for n, g in zip(self.GATHER[stage], gathered)}

    def chip_sums(self, mats, tag):
        parts = [_shard_halves(a) for a in mats.values()]
        theirs = _send_other_half(parts, name=f"grad_sibling_send_{tag}")
        return _add_sibling_half(parts, theirs, self.core, name=f"grad_chip_sum_{tag}")

    def exchange_hook(self, mats, which):
        self._pending = self.EXCHANGE[which]
        return _ExchangeHook(self.chip_sums({n: mats[n] for n in self._pending}, which))

    def received(self, arrays):
        self.chip_parts.update(zip(self._pending, arrays))


ADAMW_ROW_BLOCKS = 16


def _adamw(ws, gs, ms, vs, *, name, by_lanes=False):
    n = len(ws)
    if by_lanes:
        nb = min(a.shape[2] for a in ws) // LANES
    else:
        nb = ADAMW_ROW_BLOCKS if all(a.shape[1] % (8 * ADAMW_ROW_BLOCKS) == 0 for a in ws) else 1

    def body(*refs):
        ins, outs = refs[:4 * n], refs[4 * n:]
        for i in range(n):
            w_ref, g_ref, m_ref, v_ref = ins[i], ins[n + i], ins[2 * n + i], ins[3 * n + i]
            go_ref, d_ref, nm_ref, nv_ref = outs[i], outs[n + i], outs[2 * n + i], outs[3 * n + i]
            gv = g_ref[...]
            nm = ADAM_B1 * m_ref[...] + (1.0 - ADAM_B1) * gv
            nv = ADAM_B2 * v_ref[...] + (1.0 - ADAM_B2) * (gv * gv)
            m_hat = nm / (1.0 - ADAM_B1 ** ADAM_STEP)
            v_hat = nv / (1.0 - ADAM_B2 ** ADAM_STEP)
            go_ref[...] = gv
            d_ref[...] = -ADAM_LR * (m_hat / (jnp.sqrt(v_hat) + ADAM_EPS) + ADAM_WD * w_ref[...])
            nm_ref[...] = nm
            nv_ref[...] = nv

    if by_lanes:
        blks = [pl.BlockSpec((a.shape[0], a.shape[1], a.shape[2] // nb), lambda i: (0, 0, i)) for a in ws]
    else:
        blks = [pl.BlockSpec((a.shape[0], a.shape[1] // nb, a.shape[2]), lambda i: (0, i, 0)) for a in ws]
    shapes = [jax.ShapeDtypeStruct(a.shape, F32) for a in ws]
    outs = pl.pallas_call(body, grid=(nb,), in_specs=blks * 4, out_specs=blks * 4, out_shape=shapes * 4,
                          compiler_params=_params("parallel"), name=name)(*ws, *gs, *ms, *vs)
    return [tuple(outs[k * n + i] for k in range(4)) for i in range(n)]


SM_CONV_B, SM_NORM_W, SM_MIX_PRE, SM_MIX_POST, SM_FFN_PRE, SM_FFN_POST, SM_MISC, SM_CONV_W, SM_B_QKV, SM_B_O = 0, 4, 6, 8, 10, 12, 14, 16, 32, 34
SM_ROWS = 40
MISC_DT_BIAS, MISC_A_LOG, MISC_D, MISC_SINKS, MISC_LOSS = 0, 32, 64, 96, 112


def _shard_halves(a):
    c = a.shape[-1]
    return a.reshape(N_CHIPS, 2, -1, c)


def _rows(v):
    return v.reshape(-1, D_MODEL)


def _misc_row(dt_bias, a_log, d, sinks, loss):
    pad = jnp.zeros((D_MODEL - MISC_LOSS - 1,), F32)
    return jnp.concatenate([dt_bias.reshape(-1), a_log.reshape(-1), d.reshape(-1), sinks.reshape(-1), loss.reshape(1), pad]).reshape(1, D_MODEL)


def _replicated_rows(p, loss):
    return jnp.concatenate([
        _rows(p["ssd_conv_b"]), _rows(p["ssd_norm_w"]), _rows(p["mix_pre_norm"]), _rows(p["mix_post_norm"]),
        _rows(p["ffn_pre_norm"]), _rows(p["ffn_post_norm"]),
        _misc_row(p["ssd_dt_bias"], p["ssd_a_log"], p["ssd_d"], p["attn_sinks"], loss), jnp.zeros((1, D_MODEL), F32)], axis=0)


def _sharded_rows(conv_w, b_qkv, b_o):
    last = jnp.concatenate([b_qkv.reshape(-1), b_o.reshape(-1), jnp.zeros((D_MODEL - 640,), F32)]).reshape(1, D_MODEL)
    return jnp.concatenate([conv_w.reshape(SSD_CONV_WIDTH, D_MODEL), last, jnp.zeros((3, D_MODEL), F32)], axis=0)


REPLICATED = ("ssd_conv_b", "ssd_dt_bias", "ssd_a_log", "ssd_d", "ssd_norm_w", "attn_sinks",
              "mix_pre_norm", "mix_post_norm", "ffn_pre_norm", "ffn_post_norm")
MATRICES = ("ssd_w_in", "ssd_w_out", "attn_w_qkv", "attn_w_o", "mlp_w_up", "mlp_w_down")
WEIGHT_NAMES = ("ssd_w_in", "ssd_conv_w", "ssd_conv_b", "ssd_dt_bias", "ssd_a_log", "ssd_d", "ssd_norm_w", "ssd_w_out",
                "attn_w_qkv", "attn_b_qkv", "attn_sinks", "attn_w_o", "attn_b_o", "mlp_w_up", "mlp_w_down",
                "mix_pre_norm", "mix_post_norm", "ffn_pre_norm", "ffn_post_norm")


def _unpack_small(rows16, rows8, like):
    misc = rows16[SM_MISC]
    out = {
        "ssd_conv_b": rows16[SM_CONV_B:SM_CONV_B + 4], "ssd_norm_w": rows16[SM_NORM_W:SM_NORM_W + 2],
        "mix_pre_norm": rows16[SM_MIX_PRE:SM_MIX_PRE + 2], "mix_post_norm": rows16[SM_MIX_POST:SM_MIX_POST + 2],
        "ffn_pre_norm": rows16[SM_FFN_PRE:SM_FFN_PRE + 2], "ffn_post_norm": rows16[SM_FFN_POST:SM_FFN_POST + 2],
        "ssd_dt_bias": misc[MISC_DT_BIAS:MISC_DT_BIAS + 32], "ssd_a_log": misc[MISC_A_LOG:MISC_A_LOG + 32],
        "ssd_d": misc[MISC_D:MISC_D + 32], "attn_sinks": misc[MISC_SINKS:MISC_SINKS + 16],
        "ssd_conv_w": rows8[0:SSD_CONV_WIDTH], "attn_b_qkv": rows8[SSD_CONV_WIDTH, 0:384], "attn_b_o": rows8[SSD_CONV_WIDTH, 384:640],
    }
    return {k: v.reshape(like[k].shape) for k, v in out.items()}


def kernel(x, ssd_w_in, ssd_conv_w, ssd_conv_b, ssd_dt_bias, ssd_a_log, ssd_d, ssd_norm_w, ssd_w_out, attn_w_qkv, attn_b_qkv, attn_sinks, attn_w_o, attn_b_o, mlp_w_up, mlp_w_down, mix_pre_norm, mix_post_norm, ffn_pre_norm, ffn_post_norm, loss_target, m_ssd_w_in, m_ssd_conv_w, m_ssd_conv_b, m_ssd_dt_bias, m_ssd_a_log, m_ssd_d, m_ssd_norm_w, m_ssd_w_out, m_attn_w_qkv, m_attn_b_qkv, m_attn_sinks, m_attn_w_o, m_attn_b_o, m_mlp_w_up, m_mlp_w_down, m_mix_pre_norm, m_mix_post_norm, m_ffn_pre_norm, m_ffn_post_norm, v_ssd_w_in, v_ssd_conv_w, v_ssd_conv_b, v_ssd_dt_bias, v_ssd_a_log, v_ssd_d, v_ssd_norm_w, v_ssd_w_out, v_attn_w_qkv, v_attn_b_qkv, v_attn_sinks, v_attn_w_o, v_attn_b_o, v_mlp_w_up, v_mlp_w_down, v_mix_pre_norm, v_mix_post_norm, v_ffn_pre_norm, v_ffn_post_norm):
    w = dict(zip(WEIGHT_NAMES, (ssd_w_in, ssd_conv_w, ssd_conv_b, ssd_dt_bias, ssd_a_log, ssd_d, ssd_norm_w, ssd_w_out, attn_w_qkv, attn_b_qkv, attn_sinks, attn_w_o, attn_b_o, mlp_w_up, mlp_w_down, mix_pre_norm, mix_post_norm, ffn_pre_norm, ffn_post_norm)))
    m = dict(zip(WEIGHT_NAMES, (m_ssd_w_in, m_ssd_conv_w, m_ssd_conv_b, m_ssd_dt_bias, m_ssd_a_log, m_ssd_d, m_ssd_norm_w, m_ssd_w_out, m_attn_w_qkv, m_attn_b_qkv, m_attn_sinks, m_attn_w_o, m_attn_b_o, m_mlp_w_up, m_mlp_w_down, m_mix_pre_norm, m_mix_post_norm, m_ffn_pre_norm, m_ffn_post_norm)))
    v = dict(zip(WEIGHT_NAMES, (v_ssd_w_in, v_ssd_conv_w, v_ssd_conv_b, v_ssd_dt_bias, v_ssd_a_log, v_ssd_d, v_ssd_norm_w, v_ssd_w_out, v_attn_w_qkv, v_attn_b_qkv, v_attn_sinks, v_attn_w_o, v_attn_b_o, v_mlp_w_up, v_mlp_w_down, v_mix_pre_norm, v_mix_post_norm, v_ffn_pre_norm, v_ffn_post_norm)))
    chip = 2 * lax.axis_index("x") + lax.axis_index("y")

    two_halves = lambda a: a.reshape(2, a.shape[-2] // 2, a.shape[-1])
    later = {"ssd_w_out": (w["ssd_w_out"], 0), "attn_w_qkv": (w["attn_w_qkv"], 0), "attn_w_o": (w["attn_w_o"], 0),
             "mlp_w_up0": (w["mlp_w_up"], 0), "mlp_w_up1": (w["mlp_w_up"], 1),
             "mlp_w_down0": (w["mlp_w_down"], 0), "mlp_w_down1": (w["mlp_w_down"], 1)}
    first = _GatherHook([two_halves(w["ssd_w_in"].astype(BF16))], [w["ssd_conv_w"][0], w["attn_b_qkv"], w["attn_b_o"]])
    (cast, u0), (g_in, g_conv, g_bqkv, g_bo) = _cast_bf16(list(later.values()), x[0], w["mix_pre_norm"][0:1],
                                                          name="weights_to_bf16", hook=first)
    core = lax.axis_index("c").astype(jnp.int32).reshape(1)
    comm = _StepComm({k: two_halves(a) for k, a in zip(later, cast)}, core)
    full = {
        "ssd_w_in": _full_weight("ssd_w_in", g_in),
        "ssd_conv_w": g_conv.transpose(1, 0, 2).reshape(SSD_CONV_WIDTH, SSD_CONV_DIM),
        "attn_b_qkv": g_bqkv.reshape(ATTN_QKV), "attn_b_o": g_bo.reshape(D_MODEL),
    }
    for name in REPLICATED:
        full[name] = w[name][0] if name.startswith(("ssd_", "attn_")) else w[name]

    loss_tile, grad_x, gm, g = _local_step(x[0], loss_target[0], full, comm, u0)

    conv_w_rows = g["ssd_conv_w"].reshape(SSD_CONV_WIDTH * N_CHIPS, D_MODEL)
    b_qkv_rows = jnp.pad(g["attn_b_qkv"], (0, 2 * D_MODEL - ATTN_QKV)).reshape(2, D_MODEL)
    small = jnp.concatenate([_replicated_rows(g, loss_tile[0, 0]), conv_w_rows, b_qkv_rows, _rows(g["attn_b_o"]),
                             jnp.zeros((SM_ROWS - SM_B_O - 1, D_MODEL), F32)], axis=0)
    order = ("ssd_w_in", "ssd_w_out", "attn_w_qkv", "attn_w_o", "mlp_w_up0", "mlp_w_up1", "mlp_w_down0", "mlp_w_down1")
    halves = _sum_chips([comm.chip_parts[k] for k in order], name="grad_sum")
    (r_in, r_out, r_qkv, r_o, r_up, r_down), (small_all,) = _swap_halves(
        halves, layers=((4, 5), (6, 7)), hook=_ExchangeHook([], [small]), name="grad_halves_swap")
    small_sum, = _sum_chips([small_all], name="small_grad_sum")

    grads = {"ssd_w_in": r_in, "ssd_w_out": r_out, "attn_w_qkv": r_qkv, "attn_w_o": r_o, "mlp_w_up": r_up, "mlp_w_down": r_down}
    grads = {k: a.reshape(w[k].shape) for k, a in grads.items()}
    conv_w_g = lax.dynamic_index_in_dim(small_sum[SM_CONV_W:SM_CONV_W + 16].reshape(SSD_CONV_WIDTH, N_CHIPS, D_MODEL), chip, axis=1, keepdims=False)
    b_qkv_g = lax.dynamic_slice_in_dim(small_sum[SM_B_QKV:SM_B_QKV + 2].reshape(-1), chip * 384, 384)
    b_o_g = lax.dynamic_slice_in_dim(small_sum[SM_B_O], chip * 256, 256)
    small_g = jnp.concatenate([small_sum[0:16], _sharded_rows(conv_w_g, b_qkv_g, b_o_g)], axis=0)
    grads.update(_unpack_small(small_g[0:16], small_g[16:24], w))
    loss = small_sum[SM_MISC, MISC_LOSS]

    delta, new_m, new_v = {}, {}, {}
    stored = lambda a: jnp.swapaxes(a, 1, 2)
    rest = [name for name in MATRICES if name != "ssd_w_in"]
    mats = lambda p: [p[name] for name in rest]
    results = dict(zip(rest, _adamw(mats(w), mats(grads), mats(m), mats(v), name="adamw_matrices")))
    (w_in_result,) = _adamw([stored(w["ssd_w_in"])], [stored(grads["ssd_w_in"])], [stored(m["ssd_w_in"])],
                            [stored(v["ssd_w_in"])], by_lanes=True, name="adamw_ssd_w_in")
    results["ssd_w_in"] = tuple(stored(a) for a in w_in_result)
    for name in MATRICES:
        grads[name], delta[name], new_m[name], new_v[name] = results[name]
    zero = jnp.zeros((), F32)
    small_pack = lambda p: jnp.concatenate([_replicated_rows({k: p[k] for k in REPLICATED}, zero),
                                            _sharded_rows(p["ssd_conv_w"], p["attn_b_qkv"], p["attn_b_o"])], axis=0)[None]
    (_, d_s, m_s, v_s), = _adamw([small_pack(w)], [small_g[None]], [small_pack(m)], [small_pack(v)], name="adamw_vectors")
    d_s, m_s, v_s = d_s[0], m_s[0], v_s[0]
    delta.update(_unpack_small(d_s[0:16], d_s[16:24], w))
    new_m.update(_unpack_small(m_s[0:16], m_s[16:24], w))
    new_v.update(_unpack_small(v_s[0:16], v_s[16:24], w))

    return (loss, grad_x[None], *[grads[n] for n in WEIGHT_NAMES], *[delta[n] for n in WEIGHT_NAMES],
            *[new_m[n] for n in WEIGHT_NAMES], *[new_v[n] for n in WEIGHT_NAMES])
```

```python
import functools

import jax
import jax.numpy as jnp
from jax import lax
from jax.experimental import pallas as pl
from jax.experimental.pallas import tpu as pltpu

F32 = jnp.float32
BF16 = jnp.bfloat16

D_MODEL = 1024
SSD_D_INNER = 2048
SSD_HEAD_DIM = 64
SSD_N_HEADS = 32
SSD_N_GROUPS = 8
SSD_HPG = 4
SSD_D_STATE = 128
SSD_CONV_WIDTH = 4
SSD_CHUNK = 128
SSD_CONV_DIM = 4096
SSD_IN_DIM = 6176
SSD_IN_PAD = 6400
SSD_IN_TILE = 1280
SSD_DT_COL = 6144
SSD_GW = SSD_HPG * SSD_HEAD_DIM
ATTN_HEAD_DIM = 64
ATTN_N_Q = 16
ATTN_N_KV = 4
ATTN_REP = 4
ATTN_WINDOW = 128
ATTN_QKV = 1536
D_FF = 4096
NORM_EPS = 1e-6

ADAM_LR = 0.001
ADAM_B1 = 0.9
ADAM_B2 = 0.999
ADAM_EPS = 1e-08
ADAM_WD = 0.01
ADAM_STEP = 10

N_CHIPS = 4
N_DEV = 8
LANES = 128
VMEM_LIMIT = 48 * 1024 * 1024
BIG_TILE = 2048
MESH = pl.DeviceIdType.MESH


def _params(*sem):
    return pltpu.CompilerParams(dimension_semantics=sem, vmem_limit_bytes=VMEM_LIMIT)


def _dot(a, b, dims):
    return lax.dot_general(a, b, (dims, ((), ())), preferred_element_type=F32)


def _dot_nn(a, b):
    return _dot(a, b, ((1,), (0,)))


def _dot_nt(a, b):
    return _dot(a, b, ((1,), (1,)))


def _dot_tn(a, b):
    return _dot(a, b, ((0,), (0,)))


def _sigmoid(x):
    return 0.5 * jnp.tanh(0.5 * x) + 0.5


ANY = pl.BlockSpec(memory_space=pl.ANY)


class _HookSlots:
    def __init__(self, hook, n_in, n_out, n_scratch):
        self.hook = hook
        self.n_in, self.n_out, self.n_scratch = n_in, n_out, n_scratch
        self.inputs = list(hook.arrs) if hook else []
        self.out_shape = list(hook.out_shape) if hook else []
        self.scratch = list(hook.scratch) if hook else []
        self.in_specs = [ANY] * len(self.inputs)
        self.out_specs = [ANY] * len(self.out_shape)

    def _split(self, refs):
        a = self.n_in
        b = a + len(self.inputs)
        c = b + self.n_out
        d = c + len(self.out_shape)
        e = d + self.n_scratch
        return refs[:a], refs[a:b], refs[b:c], refs[c:d], refs[d:e], refs[e:]

    def own(self, refs):
        ins, _, outs, _, scratch, _ = self._split(refs)
        return ins, outs, scratch

    def plan(self, refs):
        _, h_in, _, h_out, _, h_scratch = self._split(refs)
        return self.hook.plan(h_in, h_out, h_scratch)

    def run(self, refs, step, n_steps):
        _, h_in, _, h_out, _, h_scratch = self._split(refs)
        _run_hook(self.hook, h_in, h_out, h_scratch, step, n_steps)

    def semantics(self, *sem):
        return sem if self.hook is None else ("arbitrary",) * len(sem)


def _matmul(a, b, *, mode, out_dtypes, name, epilogue=None, extras=(), tm=1024, tn=1024, tk=1024,
            b_shards=False, out_shards=False, hook=None, f32_block=None):
    f32_tail = f32_block is not None
    if b_shards:
        s, b_rows, b_cols = b.shape
        b2 = (b_rows, s * b_cols)
        if mode == "nn":
            tn = b_cols
        else:
            assert mode == "nt"
            tk = b_cols
    else:
        b2 = b.shape
    if mode == "nn":
        (m, k), (k2, n) = a.shape, b2
    elif mode == "nt":
        (m, k), (n, k2) = a.shape, b2
    else:
        (k, m), (k2, n) = a.shape, b2
    assert k == k2, (a.shape, b.shape, mode)
    tm, tn, tk = min(tm, m), min(tn, n), min(tk, k)
    assert m % tm == 0 and n % tn == 0 and k % tk == 0, (m, n, k, tm, tn, tk)
    nk = k // tk
    if mode == "tn":
        a_spec = pl.BlockSpec((tk, tm), lambda i, j, kk: (kk, i))
    else:
        a_spec = pl.BlockSpec((tm, tk), lambda i, j, kk: (i, kk))
    if b_shards and mode == "nn":
        b_spec = pl.BlockSpec((None, tk, tn), lambda i, j, kk: (j, kk, 0))
    elif b_shards:
        b_spec = pl.BlockSpec((None, tn, tk), lambda i, j, kk: (kk, j, 0))
    elif mode == "nt":
        b_spec = pl.BlockSpec((tn, tk), lambda i, j, kk: (j, kk))
    else:
        b_spec = pl.BlockSpec((tk, tn), lambda i, j, kk: (kk, j))
    dims = {"nn": ((1,), (0,)), "nt": ((1,), (1,)), "tn": ((0,), (0,))}[mode]
    ex_specs = []
    for arr, kind in extras:
        if kind == "tile":
            ex_specs.append(pl.BlockSpec((tm, tn), lambda i, j, kk: (i, j)))
        else:
            ex_specs.append(pl.BlockSpec((1, tn), lambda i, j, kk: (0, j)))
    n_ex, n_out = len(extras), len(out_dtypes)
    if epilogue is None:
        epilogue = lambda acc: (acc,)
    hk = _HookSlots(hook, n_in=2 + n_ex, n_out=n_out + f32_tail, n_scratch=0 if nk == 1 else 1)
    grid = (m // tm, n // tn, nk)

    def body(*refs):
        (a_ref, b_ref, *ex), outs, scratch = hk.own(refs)
        if hook is not None:
            step = (pl.program_id(0) * grid[1] + pl.program_id(1)) * grid[2] + pl.program_id(2)
            hk.run(refs, step, grid[0] * grid[1] * grid[2])

        def finish(acc):
            res = epilogue(acc, *[e[...] for e in ex])
            for o, r in zip(outs, res):
                o[...] = r.astype(o.dtype)
            if f32_tail:
                outs[n_out][...] = acc[:, f32_block:f32_block + LANES]

        if nk == 1:
            finish(_dot(a_ref[...], b_ref[...], dims))
        else:
            acc_ref = scratch[0]
            kk = pl.program_id(2)

            @pl.when(kk == 0)
            def _():
                acc_ref[...] = jnp.zeros_like(acc_ref)

            acc_ref[...] += _dot(a_ref[...], b_ref[...], dims)

            @pl.when(kk == nk - 1)
            def _():
                finish(acc_ref[...])

    if out_shards:
        out_spec = pl.BlockSpec((None, tm, tn), lambda i, j, kk: (j, i, 0))
        out_dims = (n // tn, m, tn)
    else:
        out_spec = pl.BlockSpec((tm, tn), lambda i, j, kk: (i, j))
        out_dims = (m, n)
    tail_specs = [pl.BlockSpec((tm, LANES), lambda i, j, kk: (i, 0))] if f32_tail else []
    tail_shapes = [jax.ShapeDtypeStruct((m, LANES), F32)] if f32_tail else []
    outs = pl.pallas_call(
        body,
        grid=grid,
        in_specs=[a_spec, b_spec] + ex_specs + hk.in_specs,
        out_specs=[out_spec for _ in out_dtypes] + tail_specs + hk.out_specs,
        out_shape=[jax.ShapeDtypeStruct(out_dims, dt) for dt in out_dtypes] + tail_shapes + hk.out_shape,
        scratch_shapes=([] if nk == 1 else [pltpu.VMEM((tm, tn), F32)]) + hk.scratch,
        compiler_params=_params(*hk.semantics("parallel", "arbitrary" if f32_tail else "parallel", "arbitrary")),
        name=name,
    )(a, b, *[arr for arr, _ in extras], *hk.inputs)
    n_own = n_out + f32_tail
    own = outs[0] if n_own == 1 else outs[:n_own]
    return own if hook is None else (own, outs[n_own:])


def _row_tile(t, want):
    return min(t, want)


def _rms_fwd(x, w, *, name, resid=None, want_u=None, target=None):
    t, d = x.shape
    tr = _row_tile(t, 1024)

    def norm(v, wv):
        return v * lax.rsqrt(jnp.mean(v * v, axis=-1, keepdims=True) + NORM_EPS) * wv

    row = pl.BlockSpec((tr, d), lambda i: (i, 0))
    vec = pl.BlockSpec((1, d), lambda i: (0, 0))
    if target is not None:
        def body(x_ref, w_ref, r_ref, t_ref, dh_ref, loss_ref):
            err = r_ref[...] + norm(x_ref[...].astype(F32), w_ref[...]) - t_ref[...]
            dh_ref[...] = err * (1.0 / d)

            @pl.when(pl.program_id(0) == 0)
            def _():
                loss_ref[...] = jnp.zeros_like(loss_ref)

            part = jnp.sum(jnp.sum(err * err, axis=1, keepdims=True), axis=0, keepdims=True) * (0.5 / d)
            loss_ref[...] += jnp.broadcast_to(part, loss_ref.shape)

        return pl.pallas_call(
            body, grid=(t // tr,), in_specs=[row, vec, row, row],
            out_specs=[row, pl.BlockSpec((8, LANES), lambda i: (0, 0))],
            out_shape=[jax.ShapeDtypeStruct((t, d), F32), jax.ShapeDtypeStruct((8, LANES), F32)],
            compiler_params=_params("arbitrary"), name=name)(x, w, resid, target)
    if resid is None:
        def body(x_ref, w_ref, o_ref):
            o_ref[...] = norm(x_ref[...].astype(F32), w_ref[...]).astype(BF16)
        ins, in_specs = (x, w), [row, vec]
        out_shape, out_specs = jax.ShapeDtypeStruct((t, d), BF16), row
    elif want_u is None:
        def body(x_ref, w_ref, r_ref, o_ref):
            o_ref[...] = r_ref[...] + norm(x_ref[...].astype(F32), w_ref[...])
        ins, in_specs = (x, w, resid), [row, vec, row]
        out_shape, out_specs = jax.ShapeDtypeStruct((t, d), F32), row
    else:
        def body(x_ref, w_ref, r_ref, w2_ref, o_ref, u_ref):
            h = r_ref[...] + norm(x_ref[...].astype(F32), w_ref[...])
            o_ref[...] = h
            u_ref[...] = norm(h, w2_ref[...]).astype(BF16)
        ins, in_specs = (x, w, resid, want_u), [row, vec, row, vec]
        out_shape = [jax.ShapeDtypeStruct((t, d), F32), jax.ShapeDtypeStruct((t, d), BF16)]
        out_specs = [row, row]
    return pl.pallas_call(body, grid=(t // tr,), in_specs=in_specs, out_specs=out_specs, out_shape=out_shape,
                          compiler_params=_params("parallel"), name=name)(*ins)


def _rms_bwd(x, w, dy, *, name, resid=None, out_dtype=F32, dx_col_sum=False):
    t, d = x.shape
    tr = _row_tile(t, 1024)
    row = pl.BlockSpec((tr, d), lambda i: (i, 0))
    vec = pl.BlockSpec((1, d), lambda i: (0, 0))
    has_res = resid is not None

    def body(x_ref, w_ref, dy_ref, *rest):
        r_ref = rest[0] if has_res else None
        dx_ref, dw_ref = rest[has_res:has_res + 2]
        xv = x_ref[...].astype(F32)
        dyv = dy_ref[...].astype(F32)
        r = lax.rsqrt(jnp.mean(xv * xv, axis=-1, keepdims=True) + NORM_EPS)
        xhat = xv * r
        dyw = dyv * w_ref[...]
        dx = r * (dyw - xhat * jnp.mean(dyw * xhat, axis=-1, keepdims=True))
        if has_res:
            dx = dx + r_ref[...]
        dx_ref[...] = dx.astype(dx_ref.dtype)

        sums = [(dw_ref, dyv * xhat)] + ([(rest[-1], dx)] if dx_col_sum else [])

        @pl.when(pl.program_id(0) == 0)
        def _():
            for acc_ref, _ in sums:
                acc_ref[...] = jnp.zeros_like(acc_ref)

        for acc_ref, rows in sums:
            acc_ref[...] += jnp.sum(rows, axis=0, keepdims=True)

    ins = (x, w, dy) + ((resid,) if has_res else ())
    in_specs = [row, vec, row] + ([row] if has_res else [])
    n_vec = 2 if dx_col_sum else 1
    return pl.pallas_call(
        body, grid=(t // tr,), in_specs=in_specs, out_specs=[row] + [vec] * n_vec,
        out_shape=[jax.ShapeDtypeStruct((t, d), out_dtype)] + [jax.ShapeDtypeStruct((1, d), F32)] * n_vec,
        compiler_params=_params("arbitrary"), name=name)(*ins)


def _col_sum(x, *, name):
    t, n = x.shape
    tr = _row_tile(t, 1024)

    def body(x_ref, o_ref):
        @pl.when(pl.program_id(0) == 0)
        def _():
            o_ref[...] = jnp.zeros_like(o_ref)

        o_ref[...] += jnp.sum(x_ref[...].astype(F32), axis=0, keepdims=True)

    return pl.pallas_call(
        body, grid=(t // tr,), in_specs=[pl.BlockSpec((tr, n), lambda i: (i, 0))],
        out_specs=pl.BlockSpec((1, n), lambda i: (0, 0)), out_shape=jax.ShapeDtypeStruct((1, n), F32),
        compiler_params=_params("arbitrary"), name=name)(x)


SSD_IN_SHARD = SSD_IN_DIM // N_CHIPS


def _w_in_from_shards(shards, *, name):
    d = shards.shape[1]
    tr = 256

    def body(s_ref, o_ref):
        o_ref[:, pl.ds(SSD_DT_COL, SSD_IN_PAD - SSD_DT_COL)] = jnp.zeros((tr, SSD_IN_PAD - SSD_DT_COL), o_ref.dtype)
        for s in range(N_CHIPS):
            o_ref[:, pl.ds(SSD_IN_SHARD * s, SSD_IN_SHARD)] = s_ref[s]

    return pl.pallas_call(
        body, grid=(d // tr,), in_specs=[pl.BlockSpec((N_CHIPS, tr, SSD_IN_SHARD), lambda i: (0, i, 0))],
        out_specs=pl.BlockSpec((tr, SSD_IN_PAD), lambda i: (i, 0)),
        out_shape=jax.ShapeDtypeStruct((d, SSD_IN_PAD), shards.dtype),
        compiler_params=_params("parallel"), name=name)(shards)


def _w_in_to_shards(g, *, name):
    d = g.shape[0]
    tr = 256

    def body(g_ref, o_ref):
        for s in range(N_CHIPS):
            o_ref[s] = g_ref[:, pl.ds(SSD_IN_SHARD * s, SSD_IN_SHARD)].astype(o_ref.dtype)

    return pl.pallas_call(
        body, grid=(d // tr,), in_specs=[pl.BlockSpec((tr, SSD_IN_PAD), lambda i: (i, 0))],
        out_specs=pl.BlockSpec((N_CHIPS, tr, SSD_IN_SHARD), lambda i: (0, i, 0)),
        out_shape=jax.ShapeDtypeStruct((N_CHIPS, d, SSD_IN_SHARD), BF16),
        compiler_params=_params("parallel"), name=name)(g)


XBC_COL0 = SSD_D_INNER // LANES


def _shift_down(v, k, row_ids):
    return jnp.where(row_ids >= k, pltpu.roll(v, k, axis=0), 0.0)


def _shift_up(v, k, row_ids):
    n = v.shape[0]
    return jnp.where(row_ids < n - k, pltpu.roll(v, n - k, axis=0), 0.0)


def _conv_pre(x, w, b, row_ids):
    pre = b + w[3:4, :] * x
    for k in (1, 2, 3):
        pre = pre + w[3 - k:4 - k, :] * _shift_down(x, k, row_ids)
    return pre


def _conv_fwd(zx, conv_w, conv_b, *, name, hook=None):
    t = zx.shape[0]
    cw = 2 * LANES
    nct = SSD_CONV_DIM // cw
    col0 = SSD_D_INNER // cw
    hk = _HookSlots(hook, n_in=3, n_out=1, n_scratch=0)

    def body(*refs):
        (x_ref, w_ref, b_ref), (o_ref,), _ = hk.own(refs)
        if hook is not None:
            hk.run(refs, pl.program_id(0), nct)
        x = x_ref[...].astype(F32)
        row_ids = lax.broadcasted_iota(jnp.int32, x.shape, 0)
        pre = _conv_pre(x, w_ref[...], b_ref[...], row_ids)
        o_ref[...] = pre * _sigmoid(pre)

    outs = pl.pallas_call(
        body, grid=(nct,),
        in_specs=[pl.BlockSpec((t, cw), lambda j: (0, col0 + j)),
                  pl.BlockSpec((SSD_CONV_WIDTH, cw), lambda j: (0, j)),
                  pl.BlockSpec((1, cw), lambda j: (0, j))] + hk.in_specs,
        out_specs=[pl.BlockSpec((t, cw), lambda j: (0, j))] + hk.out_specs,
        out_shape=[jax.ShapeDtypeStruct((t, SSD_CONV_DIM), F32)] + hk.out_shape,
        scratch_shapes=hk.scratch,
        compiler_params=_params(*hk.semantics("parallel")), name=name)(zx, conv_w, conv_b, *hk.inputs)
    return outs[0] if hook is None else (outs[0], outs[1:])


def _conv_bwd(zx, conv_w, conv_b, d_xs, d_bm, d_cm, dzx, *, name):
    t = zx.shape[0]
    nct = SSD_CONV_DIM // LANES
    n_xs = SSD_D_INNER // LANES
    n_bm = SSD_N_GROUPS * SSD_D_STATE // LANES

    def body(x_ref, w_ref, b_ref, dxs_ref, dbm_ref, dcm_ref, _, dx_ref, dw_ref, db_ref):
        x = x_ref[...].astype(F32)
        w = w_ref[...]
        j = pl.program_id(0)
        dy = jnp.where(j < n_xs, dxs_ref[...], jnp.where(j < n_xs + n_bm, dbm_ref[...], dcm_ref[...]))
        row_ids = lax.broadcasted_iota(jnp.int32, x.shape, 0)
        pre = _conv_pre(x, w, b_ref[...], row_ids)
        sg = _sigmoid(pre)
        dpre = dy * (sg * (1.0 + pre * (1.0 - sg)))
        dx = w[3:4, :] * dpre
        for k in (1, 2, 3):
            dx = dx + w[3 - k:4 - k, :] * _shift_up(dpre, k, row_ids)
        dx_ref[...] = dx.astype(dx_ref.dtype)
        db_ref[...] = jnp.sum(dpre, axis=0, keepdims=True)
        dw_ref[3:4, :] = jnp.sum(dpre * x, axis=0, keepdims=True)
        for k in (1, 2, 3):
            dw_ref[3 - k:4 - k, :] = jnp.sum(dpre * _shift_down(x, k, row_ids), axis=0, keepdims=True)

    clip = lambda j, lo, n: jnp.clip(j - lo, 0, n - 1)
    return pl.pallas_call(
        body, grid=(nct,),
        in_specs=[pl.BlockSpec((t, LANES), lambda j: (0, XBC_COL0 + j)),
                  pl.BlockSpec((SSD_CONV_WIDTH, LANES), lambda j: (0, j)),
                  pl.BlockSpec((1, LANES), lambda j: (0, j)),
                  pl.BlockSpec((t, LANES), lambda j: (0, clip(j, 0, n_xs))),
                  pl.BlockSpec((t, LANES), lambda j: (0, clip(j, n_xs, n_bm))),
                  pl.BlockSpec((t, LANES), lambda j: (0, clip(j, n_xs + n_bm, n_bm))), ANY],
        out_specs=[pl.BlockSpec((t, LANES), lambda j: (0, XBC_COL0 + j)),
                   pl.BlockSpec((SSD_CONV_WIDTH, LANES), lambda j: (0, j)), pl.BlockSpec((1, LANES), lambda j: (0, j))],
        out_shape=[jax.ShapeDtypeStruct(dzx.shape, dzx.dtype),
                   jax.ShapeDtypeStruct((SSD_CONV_WIDTH, SSD_CONV_DIM), F32),
                   jax.ShapeDtypeStruct((1, SSD_CONV_DIM), F32)],
        input_output_aliases={6: 0},
        compiler_params=_params("parallel"), name=name)(zx, conv_w, conv_b, d_xs, d_bm, d_cm, dzx)


def _softplus_fwd(dt_raw, bias_row, alog_row, *, name):
    t = dt_raw.shape[0]
    q = SSD_CHUNK
    tr = _row_tile(t, 1024)

    def body(x_ref, b_ref, al_ref, dt_ref, cum_ref):
        v = x_ref[...] + b_ref[...]
        e = jnp.exp(-jnp.abs(v))
        u = 1.0 + e
        log1p = jnp.where(u == 1.0, e, jnp.log(u) * (e / (u - 1.0)))
        dt = jnp.maximum(v, 0.0) + log1p
        a = dt * -jnp.exp(al_ref[...])
        lower = (lax.broadcasted_iota(jnp.int32, (q, q), 1) <= lax.broadcasted_iota(jnp.int32, (q, q), 0)).astype(F32)
        cums = [lax.dot_general(lower, a[c * q:(c + 1) * q, :], ((((1,), (0,))), ((), ())), precision=lax.Precision.HIGHEST,
                                preferred_element_type=F32) for c in range(tr // q)]
        dt_t, cum_t = dt.T, jnp.concatenate(cums, axis=0).T
        for g in range(SSD_N_GROUPS):
            rows = slice(g * SSD_HPG, (g + 1) * SSD_HPG)
            dt_ref[g] = dt_t[rows, :]
            cum_ref[g] = cum_t[rows, :]

    vec = pl.BlockSpec((1, LANES), lambda i: (0, 0))
    by_group = pl.BlockSpec((SSD_N_GROUPS, SSD_HPG, tr), lambda i: (0, 0, i))
    return pl.pallas_call(
        body, grid=(t // tr,),
        in_specs=[pl.BlockSpec((tr, LANES), lambda i: (i, 0)), vec, vec],
        out_specs=[by_group, by_group],
        out_shape=[jax.ShapeDtypeStruct((SSD_N_GROUPS, SSD_HPG, t), F32)] * 2,
        compiler_params=_params("parallel"), name=name)(dt_raw, bias_row, alog_row)


def _softplus_bwd(dt_raw, bias_row, ddt_rows, dzx, *, name):
    t = dt_raw.shape[0]
    tr = _row_tile(t, 1024)
    tail = SSD_IN_PAD - SSD_DT_COL

    def body(x_ref, b_ref, g_ref, _, o_ref, db_ref):
        v = x_ref[...] + b_ref[...]
        lane = lax.broadcasted_iota(jnp.int32, v.shape, 1)
        by_head = jnp.concatenate([g_ref[g] for g in range(SSD_N_GROUPS)]
                                  + [jnp.zeros((LANES - SSD_N_HEADS, tr), F32)], axis=0)
        d = jnp.where(lane < SSD_N_HEADS, by_head.T * _sigmoid(v), 0.0)
        o_ref[:, pl.ds(0, LANES)] = d.astype(o_ref.dtype)
        o_ref[:, pl.ds(LANES, tail - LANES)] = jnp.zeros((tr, tail - LANES), o_ref.dtype)

        @pl.when(pl.program_id(0) == 0)
        def _():
            db_ref[...] = jnp.zeros_like(db_ref)

        db_ref[...] += jnp.sum(d, axis=0, keepdims=True)

    return pl.pallas_call(
        body, grid=(t // tr,),
        in_specs=[pl.BlockSpec((tr, LANES), lambda i: (i, 0)), pl.BlockSpec((1, LANES), lambda i: (0, 0)),
                  pl.BlockSpec((SSD_N_GROUPS, SSD_HPG, tr), lambda i: (0, 0, i)), ANY],
        out_specs=[pl.BlockSpec((tr, tail), lambda i: (i, SSD_DT_COL // tail)), pl.BlockSpec((1, LANES), lambda i: (0, 0))],
        out_shape=[jax.ShapeDtypeStruct(dzx.shape, dzx.dtype), jax.ShapeDtypeStruct((1, LANES), F32)],
        input_output_aliases={3: 0},
        compiler_params=_params("arbitrary"), name=name)(dt_raw, bias_row, ddt_rows, dzx)


def _ssd_masks():
    q = SSD_CHUNK
    tt = lax.broadcasted_iota(jnp.int32, (q, q), 0)
    ss = lax.broadcasted_iota(jnp.int32, (q, q), 1)
    lane = lax.broadcasted_iota(jnp.int32, (1, SSD_GW), 1)
    srow = lax.broadcasted_iota(jnp.int32, (SSD_GW, 1), 0)
    hm = [(lane >= SSD_HEAD_DIM * j) & (lane < SSD_HEAD_DIM * (j + 1)) for j in range(SSD_HPG)]
    rm = [(srow >= SSD_HEAD_DIM * j) & (srow < SSD_HEAD_DIM * (j + 1)) for j in range(SSD_HPG)]
    return tt, ss, hm, rm


def _ssd_head_terms(dt_rows, cum_rows, a_rows, j, tt, ss):
    q = SSD_CHUNK
    dt_row = dt_rows[j:j + 1, :]
    dt_col = jnp.sum(jnp.where(tt == ss, dt_row, 0.0), axis=1, keepdims=True)
    a_row1 = a_rows[j:j + 1, :]
    a_11 = a_rows[j:j + 1, 0:1]
    cum_col = jnp.sum(jnp.where(ss <= tt, dt_row * a_row1, 0.0), axis=1, keepdims=True)
    cum_row = cum_rows[j:j + 1, :]
    decay = jnp.exp(jnp.where(ss <= tt, cum_col - cum_row, -jnp.inf))
    cum_last = cum_col[q - 1:q, :]
    e_col = jnp.exp(cum_col)
    dte_col = jnp.exp(cum_last - cum_col)
    e_last = jnp.exp(cum_last)
    return dt_col, dt_row, a_row1, a_11, decay, e_col, dte_col, e_last


SSD_CHUNKS_PER_STEP = 8
SSD_BC_COL0 = SSD_D_INNER // SSD_D_STATE


def _ssd_head_selects(terms, hm, rm):
    e_all = jnp.zeros((SSD_CHUNK, SSD_GW), F32)
    w_all = jnp.zeros((SSD_CHUNK, SSD_GW), F32)
    e_s = jnp.zeros((SSD_GW, 1), F32)
    for j in range(SSD_HPG):
        dt_col, _, _, _, _, e_col, dte_col, e_last = terms[j]
        e_all = jnp.where(hm[j], e_col, e_all)
        w_all = jnp.where(hm[j], dt_col * dte_col, w_all)
        e_s = jnp.where(rm[j], e_last, e_s)
    return e_all, w_all, e_s


def _ssd_fwd(xc, dtr, cumr, alog_b, d_b, *, name, hook=None):
    t = xc.shape[0]
    q = SSD_CHUNK
    nc = t // q
    kc = min(SSD_CHUNKS_PER_STEP, nc)
    rows = kc * q
    hk = _HookSlots(hook, n_in=7, n_out=2, n_scratch=1)

    def body(*refs):
        (x_ref, b_ref, c_ref, dtr_ref, cumr_ref, alog_ref, d_ref), (y_ref, st_ref), (s_scr,) = hk.own(refs)
        if hook is not None:
            hk.run(refs, pl.program_id(0) * (nc // kc) + pl.program_id(1), SSD_N_GROUPS * (nc // kc))

        @pl.when(pl.program_id(1) == 0)
        def _():
            s_scr[...] = jnp.zeros_like(s_scr)

        tt, ss, hm, rm = _ssd_masks()
        a_rows = -jnp.exp(alog_ref[...])
        d_rows = d_ref[...]
        d_all = jnp.zeros((1, SSD_GW), F32)
        for j in range(SSD_HPG):
            d_all = jnp.where(hm[j], d_rows[j:j + 1, 0:1], d_all)
        ks, hs = range(kc), range(SSD_HPG)
        sl = [pl.ds(k * q, q) for k in ks]
        x = [x_ref[sl[k], :] for k in ks]
        bm = [b_ref[sl[k], :].astype(BF16) for k in ks]
        cm = [c_ref[sl[k], :].astype(BF16) for k in ks]
        xb = [x[k].astype(BF16) for k in ks]
        terms = [[_ssd_head_terms(dtr_ref[:, sl[k]], cumr_ref[:, sl[k]], a_rows, j, tt, ss) for j in hs] for k in ks]
        g = [_dot_nt(cm[k], bm[k]) for k in ks]
        m = [[(g[k] * terms[k][j][4] * terms[k][j][1]).astype(BF16) for j in hs] for k in ks]
        yj = [[_dot_nn(m[k][j], xb[k]) for j in hs] for k in ks]
        sel = [_ssd_head_selects(terms[k], hm, rm) for k in ks]
        upd = [_dot_tn((x[k] * sel[k][1]).astype(BF16), bm[k]) for k in ks]
        states = [s_scr[...]]
        for k in ks:
            states.append(states[k] * sel[k][2] + upd[k])
        inter = [_dot_nt(cm[k], states[k].astype(BF16)) for k in ks]
        ys = []
        for k in ks:
            y = jnp.zeros((q, SSD_GW), F32)
            for j in hs:
                y = jnp.where(hm[j], yj[k][j], y)
            ys.append(y + inter[k] * sel[k][0] + x[k] * d_all)
        for k in ks:
            st_ref[k] = states[k]
        y_ref[...] = jnp.concatenate(ys, axis=0).astype(y_ref.dtype)
        s_scr[...] = states[kc]

    blk = lambda width, off: pl.BlockSpec((rows, width), lambda g, c: (c, off + g))
    par_s = pl.BlockSpec((None, SSD_HPG, LANES), lambda g, c: (g, 0, 0))
    row_s = pl.BlockSpec((None, SSD_HPG, rows), lambda g, c: (g, 0, c))
    outs = pl.pallas_call(
        body, grid=(SSD_N_GROUPS, nc // kc),
        in_specs=[blk(SSD_GW, 0), blk(SSD_D_STATE, SSD_BC_COL0), blk(SSD_D_STATE, SSD_BC_COL0 + SSD_N_GROUPS),
                  row_s, row_s, par_s, par_s] + hk.in_specs,
        out_specs=[blk(SSD_GW, 0), pl.BlockSpec((None, kc, SSD_GW, SSD_D_STATE), lambda g, c: (g, c, 0, 0))] + hk.out_specs,
        out_shape=[jax.ShapeDtypeStruct((t, SSD_D_INNER), BF16),
                   jax.ShapeDtypeStruct((SSD_N_GROUPS, nc, SSD_GW, SSD_D_STATE), F32)] + hk.out_shape,
        scratch_shapes=[pltpu.VMEM((SSD_GW, SSD_D_STATE), F32)] + hk.scratch,
        compiler_params=_params(*hk.semantics("parallel", "arbitrary")), name=name)(
            xc, xc, xc, dtr, cumr, alog_b, d_b, *hk.inputs)
    return outs if hook is None else (outs[:2], outs[2:])


def _ssd_bwd(xc, dtr, cumr, alog_b, d_b, states, dy, *, name, hook=None):
    t = xc.shape[0]
    q = SSD_CHUNK
    nc = t // q
    kc = min(SSD_CHUNKS_PER_STEP, nc)
    nst = nc // kc
    rows = kc * q
    rev = lambda c: nst - 1 - c
    hk = _HookSlots(hook, n_in=9, n_out=5, n_scratch=1)

    def body(*refs):
        ((x_ref, b_ref, c_ref, dtr_ref, cumr_ref, alog_ref, d_ref, st_ref, dy_ref),
         (dx_ref, db_ref, dc_ref, ddt_ref, dpar_ref), (ds_scr,)) = hk.own(refs)
        if hook is not None:
            hk.run(refs, pl.program_id(0) * nst + pl.program_id(1), SSD_N_GROUPS * nst)

        @pl.when(pl.program_id(1) == 0)
        def _():
            ds_scr[...] = jnp.zeros_like(ds_scr)
            dpar_ref[...] = jnp.zeros_like(dpar_ref)

        tt, ss, hm, rm = _ssd_masks()
        tcol = lax.broadcasted_iota(jnp.int32, (q, 1), 0)
        lane = lax.broadcasted_iota(jnp.int32, (1, LANES), 1)
        a_rows = -jnp.exp(alog_ref[...])
        d_rows = d_ref[...]
        d_all = jnp.zeros((1, SSD_GW), F32)
        for j in range(SSD_HPG):
            d_all = jnp.where(hm[j], d_rows[j:j + 1, 0:1], d_all)
        ks, hs = range(kc), range(SSD_HPG)
        sl = [pl.ds(k * q, q) for k in ks]
        x = [x_ref[sl[k], :] for k in ks]
        dyv = [dy_ref[sl[k], :].astype(F32) for k in ks]
        bm = [b_ref[sl[k], :].astype(BF16) for k in ks]
        cm = [c_ref[sl[k], :].astype(BF16) for k in ks]
        s_in = [st_ref[k] for k in ks]
        xb = [x[k].astype(BF16) for k in ks]
        dyb = [dyv[k].astype(BF16) for k in ks]
        s_b = [s_in[k].astype(BF16) for k in ks]
        terms = [[_ssd_head_terms(dtr_ref[:, sl[k]], cumr_ref[:, sl[k]], a_rows, j, tt, ss) for j in hs] for k in ks]
        sel = [_ssd_head_selects(terms[k], hm, rm) for k in ks]
        e_all, w_all, e_s = [s_[0] for s_ in sel], [s_[1] for s_ in sel], [s_[2] for s_ in sel]
        dye = [(dyv[k] * e_all[k]).astype(BF16) for k in ks]
        ds_loc = [_dot_tn(dye[k], cm[k]) for k in ks]
        ds = [None] * kc
        running = ds_scr[...]
        for k in reversed(ks):
            ds[k] = running
            running = running * e_s[k] + ds_loc[k]
        ds_scr[...] = running
        ds_b = [ds[k].astype(BF16) for k in ks]
        g = [_dot_nt(cm[k], bm[k]) for k in ks]
        cs = [_dot_nt(cm[k], s_b[k]) for k in ks]
        bds = [_dot_nt(bm[k], ds_b[k]) for k in ks]
        dm = [[_dot_nt(jnp.where(hm[j], dyv[k], 0.0).astype(BF16), xb[k]) for j in hs] for k in ks]
        gl = [[g[k] * terms[k][j][4] for j in hs] for k in ks]
        wp = [[dm[k][j] * gl[k][j] for j in hs] for k in ks]
        mt = [[(gl[k][j] * terms[k][j][1]).astype(BF16) for j in hs] for k in ks]
        dxj = [[_dot_tn(mt[k][j], dyb[k]) for j in hs] for k in ks]
        dg = []
        for k in ks:
            acc = jnp.zeros((q, q), F32)
            for j in hs:
                acc = acc + dm[k][j] * terms[k][j][4] * terms[k][j][1]
            dg.append(acc.astype(BF16))
        dy_cs = [dyv[k] * cs[k] for k in ks]
        x_bds = [x[k] * bds[k] for k in ks]
        dy_x = [dyv[k] * x[k] for k in ks]
        ds_s = [ds[k] * s_in[k] for k in ks]
        w = [[wp[k][j] * terms[k][j][1] for j in hs] for k in ks]
        rw_col = [[jnp.sum(w[k][j], axis=1, keepdims=True) for j in hs] for k in ks]
        cw_row = [[jnp.sum(w[k][j], axis=0, keepdims=True) for j in hs] for k in ks]
        cwp_row = [[jnp.sum(wp[k][j], axis=0, keepdims=True) for j in hs] for k in ks]
        r1_col = [[jnp.sum(jnp.where(hm[j], dy_cs[k], 0.0), axis=1, keepdims=True) * terms[k][j][5] for j in hs] for k in ks]
        dw_col = [[jnp.sum(jnp.where(hm[j], x_bds[k], 0.0), axis=1, keepdims=True) for j in hs] for k in ks]
        head_rows = [slice(j * SSD_HEAD_DIM, (j + 1) * SSD_HEAD_DIM) for j in hs]
        lane_sum = lambda v: jnp.sum(v, axis=1, keepdims=True)
        s_sum = [[lane_sum(jnp.sum(ds_s[k][head_rows[j], :], axis=0, keepdims=True)) for j in hs] for k in ks]
        dy_x_cols = [jnp.sum(dy_x[k], axis=0, keepdims=True) for k in ks]
        d_d = [[lane_sum(jnp.where(hm[j], dy_x_cols[k], 0.0)) for j in hs] for k in ks]
        ddt_rows = [[None] * SSD_HPG for _ in ks]
        dpar = [jnp.zeros((1, LANES), F32) for _ in hs]
        for k in ks:
            for j in hs:
                dt_col, dt_row, a_row1, a_11, _, _, dte_col, e_last = terms[k][j]
                dww = dw_col[k][j] * (dt_col * dte_col)
                last_add = jnp.sum(dww, axis=0, keepdims=True) + e_last * s_sum[k][j]
                dcum_col = rw_col[k][j] + r1_col[k][j] - dww + jnp.where(tcol == q - 1, last_add, 0.0)
                da_row = jnp.sum(jnp.where(tt >= ss, dcum_col, 0.0), axis=0, keepdims=True)
                da_col = jnp.sum(jnp.where(ss >= tt, -cw_row[k][j], 0.0), axis=1, keepdims=True)
                ddt_col = a_11 * da_col + dw_col[k][j] * dte_col
                ddt_rows[k][j] = (a_row1 * da_row + cwp_row[k][j]
                                  + jnp.sum(jnp.where(tt == ss, ddt_col, 0.0), axis=0, keepdims=True))
                d_a = jnp.sum(dt_row * da_row, axis=1, keepdims=True) + jnp.sum(dt_col * da_col, axis=0, keepdims=True)
                dpar[j] = dpar[j] + jnp.where(lane == 0, d_a * a_11, 0.0) + jnp.where(lane == 1, d_d[k][j], 0.0)
        dxs = []
        for k in ks:
            acc = jnp.zeros((q, SSD_GW), F32)
            for j in hs:
                acc = jnp.where(hm[j], dxj[k][j], acc)
            dxs.append(acc + w_all[k] * bds[k] + d_all * dyv[k])
        xw = [(x[k] * w_all[k]).astype(BF16) for k in ks]
        dc = [_dot_nn(dg[k], bm[k]) + _dot_nn(dye[k], s_b[k]) for k in ks]
        db = [_dot_tn(dg[k], cm[k]) + _dot_nn(xw[k], ds_b[k]) for k in ks]
        dx_ref[...] = jnp.concatenate(dxs, axis=0)
        dc_ref[...] = jnp.concatenate(dc, axis=0)
        db_ref[...] = jnp.concatenate(db, axis=0)
        ddt_ref[...] = jnp.concatenate([jnp.concatenate([ddt_rows[k][j] for k in ks], axis=1) for j in hs], axis=0)
        dpar_ref[...] += jnp.concatenate(dpar, axis=0)

    blk = lambda width, off: pl.BlockSpec((rows, width), lambda g, c: (rev(c), off + g))
    par_s = pl.BlockSpec((None, SSD_HPG, LANES), lambda g, c: (g, 0, 0))
    outs = pl.pallas_call(
        body, grid=(SSD_N_GROUPS, nst),
        in_specs=[blk(SSD_GW, 0), blk(SSD_D_STATE, SSD_BC_COL0), blk(SSD_D_STATE, SSD_BC_COL0 + SSD_N_GROUPS),
                  pl.BlockSpec((None, SSD_HPG, rows), lambda g, c: (g, 0, rev(c))),
                  pl.BlockSpec((None, SSD_HPG, rows), lambda g, c: (g, 0, rev(c))), par_s, par_s,
                  pl.BlockSpec((None, kc, SSD_GW, SSD_D_STATE), lambda g, c: (g, rev(c), 0, 0)), blk(SSD_GW, 0)] + hk.in_specs,
        out_specs=[blk(SSD_GW, 0), blk(SSD_D_STATE, 0), blk(SSD_D_STATE, 0),
                   pl.BlockSpec((None, SSD_HPG, rows), lambda g, c: (g, 0, rev(c))), par_s] + hk.out_specs,
        out_shape=[jax.ShapeDtypeStruct((t, SSD_D_INNER), F32),
                   jax.ShapeDtypeStruct((t, SSD_N_GROUPS * SSD_D_STATE), F32),
                   jax.ShapeDtypeStruct((t, SSD_N_GROUPS * SSD_D_STATE), F32),
                   jax.ShapeDtypeStruct((SSD_N_GROUPS, SSD_HPG, t), F32),
                   jax.ShapeDtypeStruct((SSD_N_GROUPS, SSD_HPG, LANES), F32)] + hk.out_shape,
        scratch_shapes=[pltpu.VMEM((SSD_GW, SSD_D_STATE), F32)] + hk.scratch,
        compiler_params=_params(*hk.semantics("parallel", "arbitrary")), name=name)(
            xc, xc, xc, dtr, cumr, alog_b, d_b, states, dy, *hk.inputs)
    return outs if hook is None else (outs[:5], outs[5:])


def _gate_norm_fwd(y, zx, norm_w, *, name):
    t = y.shape[0]
    tr = _row_tile(t, 512)
    row = pl.BlockSpec((tr, SSD_D_INNER), lambda i: (i, 0))

    def body(y_ref, z_ref, w_ref, o_ref):
        for gi in range(SSD_N_GROUPS):
            sl = pl.ds(gi * SSD_GW, SSD_GW)
            z = z_ref[:, sl].astype(F32)
            gv = y_ref[:, sl].astype(F32) * (z * _sigmoid(z))
            r = lax.rsqrt(jnp.mean(gv * gv, axis=-1, keepdims=True) + NORM_EPS)
            o_ref[:, sl] = (gv * r * w_ref[:, sl]).astype(BF16)

    return pl.pallas_call(
        body, grid=(t // tr,), in_specs=[row, row, pl.BlockSpec((1, SSD_D_INNER), lambda i: (0, 0))],
        out_specs=row, out_shape=jax.ShapeDtypeStruct((t, SSD_D_INNER), BF16),
        compiler_params=_params("parallel"), name=name)(y, zx, norm_w)


def _gate_norm_bwd(y, zx, norm_w, dyn, *, name):
    t = y.shape[0]
    tr = _row_tile(t, 512)
    row = pl.BlockSpec((tr, SSD_D_INNER), lambda i: (i, 0))
    vec = pl.BlockSpec((1, SSD_D_INNER), lambda i: (0, 0))

    def body(y_ref, z_ref, w_ref, dyn_ref, dy_ref, dz_ref, dw_ref):
        @pl.when(pl.program_id(0) == 0)
        def _():
            dw_ref[...] = jnp.zeros_like(dw_ref)

        for gi in range(SSD_N_GROUPS):
            sl = pl.ds(gi * SSD_GW, SSD_GW)
            z = z_ref[:, sl].astype(F32)
            yv = y_ref[:, sl].astype(F32)
            sg = _sigmoid(z)
            sz = z * sg
            gv = yv * sz
            r = lax.rsqrt(jnp.mean(gv * gv, axis=-1, keepdims=True) + NORM_EPS)
            ghat = gv * r
            dout = dyn_ref[:, sl].astype(F32)
            dgh = dout * w_ref[:, sl]
            dgv = r * (dgh - ghat * jnp.mean(dgh * ghat, axis=-1, keepdims=True))
            dy_ref[:, sl] = (dgv * sz).astype(dy_ref.dtype)
            dz_ref[:, sl] = (dgv * yv * (sg * (1.0 + z * (1.0 - sg)))).astype(dz_ref.dtype)
            dw_ref[:, sl] += jnp.sum(dout * ghat, axis=0, keepdims=True)

    return pl.pallas_call(
        body, grid=(t // tr,), in_specs=[row, row, vec, row], out_specs=[row, row, vec],
        out_shape=[jax.ShapeDtypeStruct((t, SSD_D_INNER), BF16), jax.ShapeDtypeStruct((t, SSD_IN_PAD), BF16),
                   jax.ShapeDtypeStruct((1, SSD_D_INNER), F32)],
        compiler_params=_params("arbitrary"), name=name)(y, zx, norm_w, dyn)


ATTN_KV_W = ATTN_N_KV * ATTN_HEAD_DIM
ATTN_Q_HALF = 512
ATTN_K_BLK = ATTN_N_Q * ATTN_HEAD_DIM // ATTN_KV_W
ATTN_V_BLK = ATTN_K_BLK + 1


def _attn_valid(first_block):
    w = ATTN_WINDOW
    qpos = lax.broadcasted_iota(jnp.int32, (w, 2 * w), 0) + w
    kpos = lax.broadcasted_iota(jnp.int32, (w, 2 * w), 1)
    rel = qpos - kpos
    return (rel >= 0) & (rel < w) & jnp.logical_not(first_block & (kpos < w))


def _attn_head_views(lo_ref, hi_ref):
    hd = ATTN_HEAD_DIM
    per_half = ATTN_Q_HALF // hd
    return [(lo_ref if h < per_half else hi_ref)[:, pl.ds((h % per_half) * hd, hd)] for h in range(ATTN_N_Q)]


def _attn_block_views(lo_ref, hi_ref, kc_ref, kp_ref, vc_ref, vp_ref):
    hd = ATTN_HEAD_DIM
    kv_cols = [pl.ds(kh * hd, hd) for kh in range(ATTN_N_KV)]
    kb = [jnp.concatenate([kp_ref[:, c], kc_ref[:, c]], axis=0) for c in kv_cols]
    vb = [jnp.concatenate([vp_ref[:, c], vc_ref[:, c]], axis=0) for c in kv_cols]
    return _attn_head_views(lo_ref, hi_ref), kb, vb


def _attn_scores(q, kb, valid):
    scale = ATTN_HEAD_DIM ** -0.5
    return [jnp.where(valid, _dot_nt(q[h], kb[h // ATTN_REP]) * scale, -jnp.inf) for h in range(ATTN_N_Q)]


def _attn_softmax(s, sink):
    heads = range(ATTN_N_Q)
    m = [jnp.maximum(jnp.max(s[h], axis=1, keepdims=True), sink[h]) for h in heads]
    e = [jnp.exp(s[h] - m[h]) for h in heads]
    es = [jnp.exp(sink[h] - m[h]) for h in heads]
    inv = [1.0 / (jnp.sum(e[h], axis=1, keepdims=True) + es[h]) for h in heads]
    return e, es, inv


def _attn_fwd(qkv, sinks_b, *, name, hook=None):
    t = qkv.shape[0]
    w = ATTN_WINDOW
    nb = t // w
    prev = lambda n: jnp.maximum(n - 1, 0)
    hk = _HookSlots(hook, n_in=7, n_out=1, n_scratch=0)

    def body(*refs):
        (qlo_ref, qhi_ref, kc_ref, kp_ref, vc_ref, vp_ref, sink_ref), (o_ref,), _ = hk.own(refs)
        if hook is not None:
            hk.run(refs, pl.program_id(0), nb)
        heads = range(ATTN_N_Q)
        q, kb, vb = _attn_block_views(qlo_ref, qhi_ref, kc_ref, kp_ref, vc_ref, vp_ref)
        sink = [sink_ref[h:h + 1, 0:1] for h in heads]
        e, _, inv = _attn_softmax(_attn_scores(q, kb, _attn_valid(pl.program_id(0) == 0)), sink)
        out = [_dot_nn((e[h] * inv[h]).astype(BF16), vb[h // ATTN_REP]).astype(o_ref.dtype) for h in heads]
        o_ref[...] = jnp.concatenate(out, axis=1)

    qh = lambda half: pl.BlockSpec((w, ATTN_Q_HALF), lambda n: (n, half))
    kv = lambda blk, idx: pl.BlockSpec((w, ATTN_KV_W), lambda n: (idx(n), blk))
    cur = lambda n: n
    outs = pl.pallas_call(
        body, grid=(nb,),
        in_specs=[qh(0), qh(1), kv(ATTN_K_BLK, cur), kv(ATTN_K_BLK, prev), kv(ATTN_V_BLK, cur), kv(ATTN_V_BLK, prev),
                  pl.BlockSpec((ATTN_N_Q, LANES), lambda n: (0, 0))] + hk.in_specs,
        out_specs=[pl.BlockSpec((w, D_MODEL), lambda n: (n, 0))] + hk.out_specs,
        out_shape=[jax.ShapeDtypeStruct((t, D_MODEL), BF16)] + hk.out_shape,
        scratch_shapes=hk.scratch,
        compiler_params=_params(*hk.semantics("parallel")), name=name)(qkv, qkv, qkv, qkv, qkv, qkv, sinks_b, *hk.inputs)
    return outs[0] if hook is None else (outs[0], outs[1:])


def _attn_bwd(qkv, sinks_b, dout, *, name):
    t = qkv.shape[0]
    w = ATTN_WINDOW
    nb = t // w
    hd = ATTN_HEAD_DIM
    clamp = lambda n: jnp.minimum(n, nb - 1)
    prev = lambda n: jnp.maximum(clamp(n) - 1, 0)

    def body(qlo_ref, qhi_ref, kc_ref, kp_ref, vc_ref, vp_ref, sink_ref, dolo_ref, dohi_ref,
             dq_ref, dkv_ref, dsink_ref, carry):
        n = pl.program_id(0)

        @pl.when(n == 0)
        def _():
            carry[...] = jnp.zeros_like(carry)
            dsink_ref[...] = jnp.zeros_like(dsink_ref)

        @pl.when(n < nb)
        def _():
            heads, kvs = range(ATTN_N_Q), range(ATTN_N_KV)
            q, kb, vb = _attn_block_views(qlo_ref, qhi_ref, kc_ref, kp_ref, vc_ref, vp_ref)
            do = _attn_head_views(dolo_ref, dohi_ref)
            sink = [sink_ref[h:h + 1, 0:1] for h in heads]
            s = _attn_scores(q, kb, _attn_valid(n == 0))
            dp = [_dot_nt(do[h], vb[h // ATTN_REP]) for h in heads]
            e, es, inv = _attn_softmax(s, sink)
            p = [e[h] * inv[h] for h in heads]
            delta = [jnp.sum(p[h] * dp[h], axis=1, keepdims=True) for h in heads]
            dsc = [(p[h] * (dp[h] - delta[h]) * (hd ** -0.5)).astype(BF16) for h in heads]
            pb = [p[h].astype(BF16) for h in heads]
            dq = [_dot_nn(dsc[h], kb[h // ATTN_REP]).astype(dq_ref.dtype) for h in heads]
            stack = lambda per_head, kh: jnp.concatenate(per_head[kh * ATTN_REP:(kh + 1) * ATTN_REP], axis=0)
            dkb = [_dot_tn(stack(dsc, kh), stack(q, kh)) for kh in kvs]
            dvb = [_dot_tn(stack(pb, kh), stack(do, kh)) for kh in kvs]
            dsink = [jnp.broadcast_to(jnp.sum(-es[h] * inv[h] * delta[h], axis=0, keepdims=True), (1, LANES)) for h in heads]
            dq_ref[...] = jnp.concatenate(dq, axis=1)
            dsink_ref[...] += jnp.concatenate(dsink, axis=0)
            dkv_ref[...] = (carry[...] + jnp.concatenate([d[0:w, :] for d in dkb + dvb], axis=1)).astype(dkv_ref.dtype)
            carry[...] = jnp.concatenate([d[w:2 * w, :] for d in dkb + dvb], axis=1)

        @pl.when(n == nb)
        def _():
            dkv_ref[...] = carry[...].astype(dkv_ref.dtype)

    qh = lambda half: pl.BlockSpec((w, ATTN_Q_HALF), lambda n: (clamp(n), half))
    kv = lambda blk, idx: pl.BlockSpec((w, ATTN_KV_W), lambda n: (idx(n), blk))
    return pl.pallas_call(
        body, grid=(nb + 1,),
        in_specs=[qh(0), qh(1), kv(ATTN_K_BLK, clamp), kv(ATTN_K_BLK, prev), kv(ATTN_V_BLK, clamp), kv(ATTN_V_BLK, prev),
                  pl.BlockSpec((ATTN_N_Q, LANES), lambda n: (0, 0)), qh(0), qh(1)],
        out_specs=[pl.BlockSpec((w, D_MODEL), lambda n: (clamp(n), 0)),
                   pl.BlockSpec((w, 2 * ATTN_KV_W), lambda n: (jnp.maximum(n - 1, 0), 0)),
                   pl.BlockSpec((ATTN_N_Q, LANES), lambda n: (0, 0))],
        out_shape=[jax.ShapeDtypeStruct((t, D_MODEL), BF16), jax.ShapeDtypeStruct((t, 2 * ATTN_KV_W), BF16),
                   jax.ShapeDtypeStruct((ATTN_N_Q, LANES), F32)],
        scratch_shapes=[pltpu.VMEM((w, 2 * ATTN_KV_W), F32)],
        compiler_params=_params("arbitrary"), name=name)(qkv, qkv, qkv, qkv, qkv, qkv, sinks_b, dout, dout)


def _sq_relu_epilogue(acc):
    r = jnp.maximum(acc, 0.0)
    return (r * r,)


def _sq_relu_bwd_epilogue(acc, act):
    return (acc * (2.0 * jnp.sqrt(act.astype(F32))),)


def _bias_epilogue(acc, bias):
    return (acc + bias,)


def _plain_run(stage, fn, *args, **kwargs):
    return fn(*args, **kwargs)


def _mlp_fwd(u, w_up, w_down, tag, run=_plain_run):
    act = run(f"mlp_up_{tag}", _matmul, u, w_up, mode="nn", out_dtypes=(BF16,), epilogue=_sq_relu_epilogue, b_shards=True,
              tm=BIG_TILE, name=f"mlp_up_{tag}")
    f = run(f"mlp_down_{tag}", _matmul, act, w_down, mode="nn", out_dtypes=(BF16,), tk=BIG_TILE, name=f"mlp_down_{tag}")
    return act, f


def _mlp_bwd(u, act, w_up, w_down, df, tag):
    dpre = _matmul(df, w_down, mode="nt", out_dtypes=(BF16,), epilogue=_sq_relu_bwd_epilogue,
                   extras=((act, "tile"),), name=f"mlp_dact_{tag}")
    dw_down = _matmul(act, df, mode="tn", out_dtypes=(BF16,), tk=BIG_TILE, name=f"mlp_dwdown_{tag}")
    du = _matmul(dpre, w_up, mode="nt", out_dtypes=(BF16,), b_shards=True, tm=BIG_TILE, name=f"mlp_du_{tag}")
    dw_up = _matmul(u, dpre, mode="tn", out_dtypes=(BF16,), out_shards=True, tk=BIG_TILE, name=f"mlp_dwup_{tag}")
    return du, dw_up, dw_down


def _head_param_rows(p):
    return jnp.broadcast_to(p.reshape(SSD_N_GROUPS, SSD_HPG, 1), (SSD_N_GROUPS, SSD_HPG, LANES))


def _local_step(x, target, wts, comm=None, u0=None):
    wts = dict(wts)
    row = lambda v: v.reshape(1, -1)
    mix_pre, mix_post, ffn_pre, ffn_post = wts["mix_pre_norm"], wts["mix_post_norm"], wts["ffn_pre_norm"], wts["ffn_post_norm"]

    def gathering(stage, fn, *args, **kwargs):
        hook = comm.gather_hook(stage) if comm is not None else None
        if hook is None:
            return fn(*args, **kwargs)
        out, got = fn(*args, hook=hook, **kwargs)
        wts.update(comm.weights_from(stage, got))
        return out

    if u0 is None:
        u0 = _rms_fwd(x, row(mix_pre[0]), name="rms_pre_mix0")
    zx, dt_raw = gathering("in_proj", _matmul, u0, wts["ssd_w_in"], mode="nn", out_dtypes=(BF16,), tn=SSD_IN_TILE,
                           f32_block=SSD_DT_COL - (SSD_IN_PAD - SSD_IN_TILE),
                           name="ssd_in_proj")
    xc = gathering("conv", _conv_fwd, zx, wts["ssd_conv_w"], row(wts["ssd_conv_b"]), name="ssd_conv_fwd")
    bias_row = jnp.pad(wts["ssd_dt_bias"], (0, LANES - SSD_N_HEADS)).reshape(1, LANES)
    alog_row = jnp.pad(wts["ssd_a_log"], (0, LANES - SSD_N_HEADS)).reshape(1, LANES)
    dtr, cumr = _softplus_fwd(dt_raw, bias_row, alog_row, name="ssd_dt_fwd")
    alog_b, d_b = _head_param_rows(wts["ssd_a_log"]), _head_param_rows(wts["ssd_d"])
    y_ssd, states = gathering("scan", _ssd_fwd, xc, dtr, cumr, alog_b, d_b, name="ssd_scan_fwd")
    norm_w = row(wts["ssd_norm_w"])
    yn = _gate_norm_fwd(y_ssd, zx, norm_w, name="ssd_gate_norm_fwd")
    mix0 = _matmul(yn, wts["ssd_w_out"], mode="nn", out_dtypes=(BF16,), tk=BIG_TILE, name="ssd_out_proj")
    h1, v0 = _rms_fwd(mix0, row(mix_post[0]), resid=x, want_u=row(ffn_pre[0]), name="rms_post_mix0")
    act0, f0 = _mlp_fwd(v0, wts["mlp_w_up0"], wts["mlp_w_down0"], "l0", run=gathering)
    h2, u1 = _rms_fwd(f0, row(ffn_post[0]), resid=h1, want_u=row(mix_pre[1]), name="rms_post_ffn0")

    qkv = _matmul(u1, wts["attn_w_qkv"], mode="nn", out_dtypes=(BF16,), epilogue=_bias_epilogue,
                  extras=((row(wts["attn_b_qkv"]), "row"),), b_shards=True, name="attn_qkv_proj")
    sinks_b = jnp.broadcast_to(wts["attn_sinks"].reshape(ATTN_N_Q, 1), (ATTN_N_Q, LANES))
    ao = gathering("attn_fwd", _attn_fwd, qkv, sinks_b, name="attn_fwd")
    mix1 = _matmul(ao, wts["attn_w_o"], mode="nn", out_dtypes=(BF16,), epilogue=_bias_epilogue,
                   extras=((row(wts["attn_b_o"]), "row"),), name="attn_out_proj")
    h3, v1 = _rms_fwd(mix1, row(mix_post[1]), resid=h2, want_u=row(ffn_pre[1]), name="rms_post_mix1")
    act1, f1 = _mlp_fwd(v1, wts["mlp_w_up1"], wts["mlp_w_down1"], "l1")
    dh4, loss_tile = _rms_fwd(f1, row(ffn_post[1]), resid=h3, target=target, name="rms_post_ffn1_loss")

    df1, g_ffn_post1 = _rms_bwd(f1, row(ffn_post[1]), dh4, out_dtype=BF16, name="rms_post_ffn1_bwd")
    dv1, g_up1, g_down1 = _mlp_bwd(v1, act1, wts["mlp_w_up1"], wts["mlp_w_down1"], df1, "l1")
    dh3, g_ffn_pre1 = _rms_bwd(h3, row(ffn_pre[1]), dv1, resid=dh4, name="rms_pre_ffn1_bwd")
    dmix1, g_mix_post1, g_b_o = _rms_bwd(mix1, row(mix_post[1]), dh3, out_dtype=BF16, dx_col_sum=True, name="rms_post_mix1_bwd")
    g_w_o = _matmul(ao, dmix1, mode="tn", out_dtypes=(BF16,), tk=BIG_TILE, name="attn_dwo")
    dao = _matmul(dmix1, wts["attn_w_o"], mode="nt", out_dtypes=(BF16,), name="attn_dao")
    dq, dkv, g_sinks = _attn_bwd(qkv, sinks_b, dao, name="attn_bwd")
    dqkv = jnp.concatenate([dq, dkv], axis=1)
    g_b_qkv = _col_sum(dqkv, name="attn_bqkv_grad")
    g_w_qkv = _matmul(u1, dqkv, mode="tn", out_dtypes=(BF16,), tn=ATTN_QKV // N_CHIPS, out_shards=True, tk=BIG_TILE, name="attn_dwqkv")
    du1 = _matmul(dqkv, wts["attn_w_qkv"], mode="nt", out_dtypes=(BF16,), b_shards=True, name="attn_du")
    dh2, g_mix_pre1 = _rms_bwd(h2, row(mix_pre[1]), du1, resid=dh3, name="rms_pre_mix1_bwd")

    df0, g_ffn_post0 = _rms_bwd(f0, row(ffn_post[0]), dh2, out_dtype=BF16, name="rms_post_ffn0_bwd")
    dv0, g_up0, g_down0 = _mlp_bwd(v0, act0, wts["mlp_w_up0"], wts["mlp_w_down0"], df0, "l0")
    dh1, g_ffn_pre0 = _rms_bwd(h1, row(ffn_pre[0]), dv0, resid=dh2, name="rms_pre_ffn0_bwd")
    dmix0, g_mix_post0 = _rms_bwd(mix0, row(mix_post[0]), dh1, out_dtype=BF16, name="rms_post_mix0_bwd")
    g_w_out = _matmul(yn, dmix0, mode="tn", out_dtypes=(BF16,), tk=BIG_TILE, name="ssd_dwout")
    dyn = _matmul(dmix0, wts["ssd_w_out"], mode="nt", out_dtypes=(BF16,), name="ssd_dyn")
    dy_ssd, dzx, g_norm_w = _gate_norm_bwd(y_ssd, zx, norm_w, dyn, name="ssd_gate_norm_bwd")
    mats = {"ssd_w_out": g_w_out, "attn_w_qkv": g_w_qkv, "attn_w_o": g_w_o,
            "mlp_w_up0": g_up0, "mlp_w_up1": g_up1, "mlp_w_down0": g_down0, "mlp_w_down1": g_down1}
    if comm is None:
        dxc, dbm, dcm, ddt_r, dpar = _ssd_bwd(xc, dtr, cumr, alog_b, d_b, states, dy_ssd, name="ssd_scan_bwd")
    else:
        (dxc, dbm, dcm, ddt_r, dpar), received = _ssd_bwd(xc, dtr, cumr, alog_b, d_b, states, dy_ssd,
                                                          name="ssd_scan_bwd", hook=comm.exchange_hook(mats, "early"))
        comm.received(received)
    dzx, g_conv_w, g_conv_b = _conv_bwd(zx, wts["ssd_conv_w"], row(wts["ssd_conv_b"]), dxc, dbm, dcm, dzx, name="ssd_conv_bwd")
    dzx, g_dt_bias = _softplus_bwd(dt_raw, bias_row, ddt_r, dzx, name="ssd_dt_bwd")
    g_w_in = _w_in_to_shards(_matmul(u0, dzx, mode="tn", out_dtypes=(BF16,), tn=SSD_IN_TILE, tk=BIG_TILE, name="ssd_dwin"), name="ssd_dwin_shards")
    mats["ssd_w_in"] = g_w_in
    if comm is None:
        du0 = _matmul(dzx, wts["ssd_w_in"], mode="nt", out_dtypes=(BF16,), tk=SSD_IN_TILE, name="ssd_du")
    else:
        du0, received = _matmul(dzx, wts["ssd_w_in"], mode="nt", out_dtypes=(BF16,), tk=SSD_IN_TILE, name="ssd_du",
                                hook=comm.exchange_hook(mats, "late"))
        comm.received(received)
    grad_x, g_mix_pre0 = _rms_bwd(x, row(mix_pre[0]), du0, resid=dh1, name="rms_pre_mix0_bwd")

    dpar = dpar.reshape(SSD_N_HEADS, LANES)
    vecs = {
        "ssd_conv_w": g_conv_w, "ssd_conv_b": g_conv_b.reshape(-1),
        "ssd_dt_bias": g_dt_bias[0, :SSD_N_HEADS], "ssd_a_log": dpar[:, 0], "ssd_d": dpar[:, 1],
        "ssd_norm_w": g_norm_w.reshape(-1), "attn_b_qkv": g_b_qkv.reshape(-1), "attn_sinks": g_sinks[:, 0],
        "attn_b_o": g_b_o.reshape(-1),
        "mix_pre_norm": jnp.concatenate([g_mix_pre0, g_mix_pre1]), "mix_post_norm": jnp.concatenate([g_mix_post0, g_mix_post1]),
        "ffn_pre_norm": jnp.concatenate([g_ffn_pre0, g_ffn_pre1]), "ffn_post_norm": jnp.concatenate([g_ffn_post0, g_ffn_post1]),
    }
    return loss_tile, grad_x, mats, vecs


def _mesh_position():
    return lax.axis_index("x"), lax.axis_index("y"), lax.axis_index("c")


def _flip(v, bit):
    return 1 - v if bit else v


OTHER_CHIPS = ((1, 0), (0, 1), (1, 1))


def _comm_params():
    return pltpu.CompilerParams(vmem_limit_bytes=VMEM_LIMIT)


def _staged_copies(srcs, dsts, bufs, sems_in, sems_out):
    loads = [pltpu.make_async_copy(s, b, sems_in.at[i]) for i, (s, b) in enumerate(zip(srcs, bufs))]
    stores = [pltpu.make_async_copy(b, d, sems_out.at[i]) for i, (b, d) in enumerate(zip(bufs, dsts))]
    return loads, stores


class _GatherHook:
    def __init__(self, mats, vecs=()):
        self.arrs = list(mats) + list(vecs)
        self.nm, self.n = len(mats), len(self.arrs)
        n_ici, n_fwd = (N_CHIPS - 1) * self.n, max((N_CHIPS - 1) * self.nm, 1)
        dma = pltpu.SemaphoreType.DMA
        self.out_shape = [jax.ShapeDtypeStruct((N_CHIPS,) + a.shape, a.dtype) for a in self.arrs]
        self.scratch = [pltpu.VMEM(a.shape, a.dtype) for a in self.arrs] + [
            dma((n_ici,)), dma((n_ici,)), dma((n_fwd,)), dma((n_fwd,)), dma((self.n,)), dma((self.n,))]

    def plan(self, ins, outs, scratch):
        n, nm = self.n, self.nm
        bufs = scratch[:n]
        ici_send, ici_recv, fwd_send, fwd_recv, load_sems, store_sems = scratch[n:]
        xi, yi, ci = _mesh_position()
        me = 2 * xi + yi
        loads, stores = _staged_copies(ins, [outs[i].at[me] for i in range(n)], bufs, load_sems, store_sems)
        sends, landed, forwards, from_sibling = [], [], [], []
        for j, (bx, by) in enumerate(OTHER_CHIPS):
            px, py = _flip(xi, bx), _flip(yi, by)
            peer = 2 * px + py
            for i in range(n):
                k = j * n + i
                mk = functools.partial(pltpu.make_async_remote_copy, send_sem=ici_send.at[k], recv_sem=ici_recv.at[k],
                                       device_id=(px, py, ci), device_id_type=MESH)
                if i < nm:
                    sends.append(mk(src_ref=ins[i].at[ci], dst_ref=outs[i].at[me, ci]))
                    landed.append(mk(src_ref=ins[i].at[ci], dst_ref=outs[i].at[peer, ci]))
                    kf = j * nm + i
                    fw = functools.partial(pltpu.make_async_remote_copy, send_sem=fwd_send.at[kf], recv_sem=fwd_recv.at[kf],
                                           device_id=(xi, yi, 1 - ci), device_id_type=MESH)
                    forwards.append(fw(src_ref=outs[i].at[peer, ci], dst_ref=outs[i].at[peer, ci]))
                    from_sibling.append(fw(src_ref=outs[i].at[peer, ci], dst_ref=outs[i].at[peer, 1 - ci]))
                else:
                    sends.append(mk(src_ref=ins[i], dst_ref=outs[i].at[me]))
                    landed.append(mk(src_ref=ins[i], dst_ref=outs[i].at[peer]))
                    forwards.append(None)
        return loads, stores, sends, landed, forwards, from_sibling

    @staticmethod
    def start(p):
        loads, _, sends, _, _, _ = p
        for cp in loads + sends:
            cp.start()

    @staticmethod
    def relay(p):
        loads, stores, _, landed, forwards, _ = p
        for ld, st in zip(loads, stores):
            ld.wait()
            st.start()
        for cp, fw in zip(landed, forwards):
            cp.wait_recv()
            if fw is not None:
                fw.start()

    @staticmethod
    def finish(p):
        _, stores, sends, _, forwards, from_sibling = p
        for cp in from_sibling:
            cp.wait_recv()
        for cp in sends + [fw for fw in forwards if fw is not None]:
            cp.wait_send()
        for st in stores:
            st.wait()


def _run_hook(hook, ins, outs, scratch, step, n_steps):
    p = hook.plan(ins, outs, scratch)
    relay_step = min(max(1, (3 * n_steps) // 4), n_steps - 1)

    @pl.when(step == 0)
    def _():
        hook.start(p)

    if relay_step < n_steps - 1:
        @pl.when(step == relay_step)
        def _():
            hook.relay(p)

    @pl.when(step == n_steps - 1)
    def _():
        if relay_step == n_steps - 1:
            hook.relay(p)
        hook.finish(p)


def _hook_call(hook, *, name):
    n = len(hook.arrs)

    def body(*refs):
        p = hook.plan(refs[:n], refs[n:n + len(hook.out_shape)], refs[n + len(hook.out_shape):])
        hook.start(p)
        hook.relay(p)
        hook.finish(p)

    return pl.pallas_call(
        body, in_specs=[ANY] * n, out_specs=[ANY] * len(hook.out_shape), out_shape=hook.out_shape,
        scratch_shapes=hook.scratch, compiler_params=_comm_params(), name=name)(*hook.arrs)


def _send_other_half(parts, *, name):
    n = len(parts)

    def body(*refs):
        ins, outs = refs[:n], refs[n:2 * n]
        send_sems, recv_sems = refs[2 * n:]
        xi, yi, ci = _mesh_position()
        sibling = (xi, yi, 1 - ci)
        for i in range(n):
            for s in range(N_CHIPS):
                pltpu.make_async_remote_copy(src_ref=ins[i].at[s, 1 - ci], dst_ref=outs[i].at[s], send_sem=send_sems.at[i],
                                             recv_sem=recv_sems.at[i], device_id=sibling, device_id_type=MESH).start()
        for i in range(n):
            pltpu.make_async_remote_copy(src_ref=outs[i], dst_ref=outs[i], send_sem=send_sems.at[i], recv_sem=recv_sems.at[i],
                                         device_id=sibling, device_id_type=MESH).wait()

    return pl.pallas_call(
        body, in_specs=[ANY] * n, out_specs=[ANY] * n,
        out_shape=[jax.ShapeDtypeStruct((p.shape[0],) + p.shape[2:], p.dtype) for p in parts],
        scratch_shapes=[pltpu.SemaphoreType.DMA((n,)), pltpu.SemaphoreType.DMA((n,))],
        name=name)(*parts)


ROW_BLOCKS = 8
SUM_ROW_BLOCKS = 2


def _add_sibling_half(parts, theirs, core, *, name):
    n = len(parts)

    def body(core_ref, *refs):
        for a_ref, b_ref, o_ref in zip(refs[:n], refs[n:2 * n], refs[2 * n:]):
            o_ref[...] = (a_ref[...].astype(F32) + b_ref[...].astype(F32)).astype(o_ref.dtype)

    nb = SUM_ROW_BLOCKS
    mine = lambda p: pl.BlockSpec((None, None, p.shape[2] // nb, p.shape[3]), lambda s, rb, core_ref: (s, core_ref[0], rb, 0))
    other = lambda p: pl.BlockSpec((None, p.shape[1] // nb, p.shape[2]), lambda s, rb, core_ref: (s, rb, 0))
    return pl.pallas_call(
        body,
        grid_spec=pltpu.PrefetchScalarGridSpec(
            num_scalar_prefetch=1, grid=(N_CHIPS, nb),
            in_specs=[mine(p) for p in parts] + [other(q) for q in theirs], out_specs=[other(q) for q in theirs]),
        out_shape=[jax.ShapeDtypeStruct(q.shape, BF16) for q in theirs],
        compiler_params=_params("parallel", "parallel"), name=name)(core, *parts, *theirs)


class _ExchangeHook:
    def __init__(self, parts, to_all=()):
        self.arrs = list(parts) + list(to_all)
        self.n_parts, self.n = len(parts), len(self.arrs)
        n_ici, n_peer = max((N_CHIPS - 1) * self.n_parts, 1), (N_DEV - 1) * max(len(to_all), 1)
        dma = pltpu.SemaphoreType.DMA
        self.out_shape = [jax.ShapeDtypeStruct(p.shape, p.dtype) for p in parts] + [
            jax.ShapeDtypeStruct((N_DEV,) + a.shape, a.dtype) for a in to_all]
        self.scratch = [pltpu.VMEM(p.shape[1:], p.dtype) for p in parts] + [pltpu.VMEM(a.shape, a.dtype) for a in to_all] + [
            dma((n_ici,)), dma((n_ici,)), dma((n_peer,)), dma((n_peer,)), dma((self.n,)), dma((self.n,))]

    def plan(self, ins, outs, scratch):
        n, npt = self.n, self.n_parts
        bufs = scratch[:n]
        send_sems, recv_sems, all_send, all_recv, load_sems, store_sems = scratch[n:]
        xi, yi, ci = _mesh_position()
        me_chip = 2 * xi + yi
        me = 4 * xi + 2 * yi + ci
        loads, stores = _staged_copies([ins[i].at[me_chip] for i in range(npt)] + list(ins[npt:]),
                                       [outs[i].at[me_chip] for i in range(npt)] + [outs[i].at[me] for i in range(npt, n)],
                                       bufs, load_sems, store_sems)
        sends, recvs = [], []
        for j, (bx, by) in enumerate(OTHER_CHIPS):
            px, py = _flip(xi, bx), _flip(yi, by)
            peer = 2 * px + py
            for i in range(npt):
                k = j * npt + i
                mk = functools.partial(pltpu.make_async_remote_copy, src_ref=ins[i].at[peer], send_sem=send_sems.at[k],
                                       recv_sem=recv_sems.at[k], device_id=(px, py, ci), device_id_type=MESH)
                sends.append(mk(dst_ref=outs[i].at[me_chip]))
                recvs.append(mk(dst_ref=outs[i].at[peer]))
        for i in range(npt, n):
            for k in range(1, N_DEV):
                px, py, pc = _flip(xi, (k >> 2) & 1), _flip(yi, (k >> 1) & 1), _flip(ci, k & 1)
                slot = (i - npt) * (N_DEV - 1) + k - 1
                mk = functools.partial(pltpu.make_async_remote_copy, src_ref=ins[i], send_sem=all_send.at[slot],
                                       recv_sem=all_recv.at[slot], device_id=(px, py, pc), device_id_type=MESH)
                sends.append(mk(dst_ref=outs[i].at[me]))
                recvs.append(mk(dst_ref=outs[i].at[4 * px + 2 * py + pc]))
        return loads, stores, sends, recvs

    @staticmethod
    def start(p):
        loads, _, sends, _ = p
        for cp in loads + sends:
            cp.start()

    @staticmethod
    def relay(p):
        loads, stores, _, _ = p
        for ld, st in zip(loads, stores):
            ld.wait()
            st.start()

    @staticmethod
    def finish(p):
        _, stores, sends, recvs = p
        for cp in recvs:
            cp.wait_recv()
        for cp in sends:
            cp.wait_send()
        for st in stores:
            st.wait()


def _sum_chips(parts, *, name):
    n = len(parts)
    p = parts[0].shape[0]

    def body(*refs):
        s = pl.program_id(1)
        for x_ref, o_ref in zip(refs[:n], refs[n:]):
            @pl.when(s == 0)
            def _():
                o_ref[...] = x_ref[...].astype(F32)

            @pl.when(s > 0)
            def _():
                o_ref[...] += x_ref[...].astype(F32)

    blocks = lambda q: SUM_ROW_BLOCKS if q.shape[1] % (16 * SUM_ROW_BLOCKS) == 0 else 1
    assert len({blocks(q) for q in parts}) == 1
    nb = blocks(parts[0])
    return pl.pallas_call(
        body, grid=(nb, p),
        in_specs=[pl.BlockSpec((None, q.shape[1] // nb, q.shape[2]), lambda rb, s: (s, rb, 0)) for q in parts],
        out_specs=[pl.BlockSpec((q.shape[1] // nb, q.shape[2]), lambda rb, s: (rb, 0)) for q in parts],
        out_shape=[jax.ShapeDtypeStruct(q.shape[1:], F32) for q in parts],
        compiler_params=_params("parallel", "arbitrary"), name=name)(*parts)


def _swap_halves(halves, layers, *, name, hook=None):
    n = len(halves)
    out_shapes, slots = [], []
    for i, h in enumerate(halves):
        pair = [p for p in layers if i in p]
        if pair and pair[0][1] == i:
            slots.append((slots[pair[0][0]][0], 1))
        elif pair:
            out_shapes.append(jax.ShapeDtypeStruct((2, 2) + h.shape, h.dtype))
            slots.append((len(out_shapes) - 1, 0))
        else:
            out_shapes.append(jax.ShapeDtypeStruct((2,) + h.shape, h.dtype))
            slots.append((len(out_shapes) - 1, None))
    n_out = len(out_shapes)
    hk = _HookSlots(hook, n_in=n, n_out=n_out, n_scratch=n + 4)

    def body(*refs):
        ins, outs, scratch = hk.own(refs)
        bufs = scratch[:n]
        send_sems, recv_sems, load_sems, store_sems = scratch[n:]
        if hook is not None:
            extra = hk.plan(refs)
            hook.start(extra)
        xi, yi, ci = _mesh_position()
        own, sends, recvs = [], [], []
        for i in range(n):
            o, layer = slots[i]
            dst = (lambda core: outs[o].at[core]) if layer is None else (lambda core: outs[o].at[layer, core])
            own.append(dst(ci))
            mk = functools.partial(pltpu.make_async_remote_copy, src_ref=ins[i], send_sem=send_sems.at[i],
                                   recv_sem=recv_sems.at[i], device_id=(xi, yi, 1 - ci), device_id_type=MESH)
            sends.append(mk(dst_ref=dst(ci)))
            recvs.append(mk(dst_ref=dst(1 - ci)))
        loads, stores = _staged_copies(ins, own, bufs, load_sems, store_sems)
        for cp in loads + sends:
            cp.start()
        for ld, st in zip(loads, stores):
            ld.wait()
            st.start()
        for cp in recvs:
            cp.wait_recv()
        for cp in sends:
            cp.wait_send()
        for st in stores:
            st.wait()
        if hook is not None:
            hook.relay(extra)
            hook.finish(extra)

    outs = pl.pallas_call(
        body, in_specs=[ANY] * n + hk.in_specs, out_specs=[ANY] * n_out + hk.out_specs, out_shape=out_shapes + hk.out_shape,
        scratch_shapes=[pltpu.VMEM(h.shape, h.dtype) for h in halves]
        + [pltpu.SemaphoreType.DMA((n,)), pltpu.SemaphoreType.DMA((n,)), pltpu.SemaphoreType.DMA((n,)), pltpu.SemaphoreType.DMA((n,))]
        + hk.scratch,
        compiler_params=_comm_params(), name=name)(*halves, *hk.inputs)
    return outs if hook is None else (outs[:n_out], outs[n_out:])


def _cast_bf16(layers, x, norm_w, *, name, hook=None):
    n = len(layers)
    hk = _HookSlots(hook, n_in=n + 2, n_out=n + 1, n_scratch=0)

    def body(*refs):
        ins, outs, _ = hk.own(refs)
        if hook is not None:
            hk.run(refs, pl.program_id(0), ROW_BLOCKS)
        for i_ref, o_ref in zip(ins[:n], outs[:n]):
            o_ref[...] = i_ref[...].astype(o_ref.dtype)
        xv = ins[n][...]
        outs[n][...] = (xv * lax.rsqrt(jnp.mean(xv * xv, axis=-1, keepdims=True) + NORM_EPS) * ins[n + 1][...]).astype(BF16)

    in_blk = lambda a, l: pl.BlockSpec((None, a.shape[1] // ROW_BLOCKS, a.shape[2]), lambda i: (l, i, 0))
    out_blk = lambda a: pl.BlockSpec((a.shape[1] // ROW_BLOCKS, a.shape[2]), lambda i: (i, 0))
    x_blk = pl.BlockSpec((x.shape[0] // ROW_BLOCKS, x.shape[1]), lambda i: (i, 0))
    outs = pl.pallas_call(
        body, grid=(ROW_BLOCKS,),
        in_specs=[in_blk(a, l) for a, l in layers] + [x_blk, pl.BlockSpec((1, x.shape[1]), lambda i: (0, 0))] + hk.in_specs,
        out_specs=[out_blk(a) for a, _ in layers] + [x_blk] + hk.out_specs,
        out_shape=[jax.ShapeDtypeStruct(a.shape[1:], BF16) for a, _ in layers] + [jax.ShapeDtypeStruct(x.shape, BF16)] + hk.out_shape,
        scratch_shapes=hk.scratch,
        compiler_params=_params(*hk.semantics("parallel")), name=name)(*[a for a, _ in layers], x, norm_w, *hk.inputs)
    own = (outs[:n], outs[n])
    return own if hook is None else (own, outs[n + 1:])


def _full_weight(name, gathered):
    s, _, r, c = gathered.shape
    if name == "ssd_w_in":
        return _w_in_from_shards(gathered.reshape(s, 2 * r, c), name="ssd_w_in_unshard")
    if name in ("attn_w_qkv", "mlp_w_up0", "mlp_w_up1"):
        return gathered.reshape(s, 2 * r, c)
    return gathered.reshape(s * 2 * r, c)


class _StepComm:
    GATHER = {"in_proj": ("mlp_w_up0", "ssd_w_out"), "conv": ("mlp_w_down0", "attn_w_o"), "scan": ("mlp_w_up1",),
              "mlp_up_l0": ("attn_w_qkv",), "attn_fwd": ("mlp_w_down1",)}
    EXCHANGE = {"early": ("ssd_w_out", "attn_w_qkv", "attn_w_o", "mlp_w_up0", "mlp_w_up1", "mlp_w_down0", "mlp_w_down1"),
                "late": ("ssd_w_in",)}

    def __init__(self, shards, core):
        self.shards, self.core = shards, core
        self.chip_parts = {}
        self._pending = None

    def gather_hook(self, stage):
        names = self.GATHER.get(stage)
        return _GatherHook([self.shards[n] for n in names]) if names else None

    def weights_from(self, stage, gathered):
        return {n: _full_weight(n, g) for n, g in zip(self.GATHER[stage], gathered)}

    def chip_sums(self, mats, tag):
        parts = [_shard_halves(a) for a in mats.values()]
        theirs = _send_other_half(parts, name=f"grad_sibling_send_{tag}")
        return _add_sibling_half(parts, theirs, self.core, name=f"grad_chip_sum_{tag}")

    def exchange_hook(self, mats, which):
        self._pending = self.EXCHANGE[which]
        return _ExchangeHook(self.chip_sums({n: mats[n] for n in self._pending}, which))

    def received(self, arrays):
        self.chip_parts.update(zip(self._pending, arrays))


ADAMW_ROW_BLOCKS = 16


def _adamw(ws, gs, ms, vs, *, name, by_lanes=False):
    n = len(ws)
    if by_lanes:
        nb = min(a.shape[2] for a in ws) // LANES
    else:
        nb = ADAMW_ROW_BLOCKS if all(a.shape[1] % (8 * ADAMW_ROW_BLOCKS) == 0 for a in ws) else 1

    def body(*refs):
        ins, outs = refs[:4 * n], refs[4 * n:]
        for i in range(n):
            w_ref, g_ref, m_ref, v_ref = ins[i], ins[n + i], ins[2 * n + i], ins[3 * n + i]
            go_ref, d_ref, nm_ref, nv_ref = outs[i], outs[n + i], outs[2 * n + i], outs[3 * n + i]
            gv = g_ref[...]
            nm = ADAM_B1 * m_ref[...] + (1.0 - ADAM_B1) * gv
            nv = ADAM_B2 * v_ref[...] + (1.0 - ADAM_B2) * (gv * gv)
            m_hat = nm / (1.0 - ADAM_B1 ** ADAM_STEP)
            v_hat = nv / (1.0 - ADAM_B2 ** ADAM_STEP)
            go_ref[...] = gv
            d_ref[...] = -ADAM_LR * (m_hat / (jnp.sqrt(v_hat) + ADAM_EPS) + ADAM_WD * w_ref[...])
            nm_ref[...] = nm
            nv_ref[...] = nv

    if by_lanes:
        blks = [pl.BlockSpec((a.shape[0], a.shape[1], a.shape[2] // nb), lambda i: (0, 0, i)) for a in ws]
    else:
        blks = [pl.BlockSpec((a.shape[0], a.shape[1] // nb, a.shape[2]), lambda i: (0, i, 0)) for a in ws]
    shapes = [jax.ShapeDtypeStruct(a.shape, F32) for a in ws]
    outs = pl.pallas_call(body, grid=(nb,), in_specs=blks * 4, out_specs=blks * 4, out_shape=shapes * 4,
                          compiler_params=_params("parallel"), name=name)(*ws, *gs, *ms, *vs)
    return [tuple(outs[k * n + i] for k in range(4)) for i in range(n)]


SM_CONV_B, SM_NORM_W, SM_MIX_PRE, SM_MIX_POST, SM_FFN_PRE, SM_FFN_POST, SM_MISC, SM_CONV_W, SM_B_QKV, SM_B_O = 0, 4, 6, 8, 10, 12, 14, 16, 32, 34
SM_ROWS = 40
MISC_DT_BIAS, MISC_A_LOG, MISC_D, MISC_SINKS, MISC_LOSS = 0, 32, 64, 96, 112


def _shard_halves(a):
    c = a.shape[-1]
    return a.reshape(N_CHIPS, 2, -1, c)


def _rows(v):
    return v.reshape(-1, D_MODEL)


def _misc_row(dt_bias, a_log, d, sinks, loss):
    pad = jnp.zeros((D_MODEL - MISC_LOSS - 1,), F32)
    return jnp.concatenate([dt_bias.reshape(-1), a_log.reshape(-1), d.reshape(-1), sinks.reshape(-1), loss.reshape(1), pad]).reshape(1, D_MODEL)


def _replicated_rows(p, loss):
    return jnp.concatenate([
        _rows(p["ssd_conv_b"]), _rows(p["ssd_norm_w"]), _rows(p["mix_pre_norm"]), _rows(p["mix_post_norm"]),
        _rows(p["ffn_pre_norm"]), _rows(p["ffn_post_norm"]),
        _misc_row(p["ssd_dt_bias"], p["ssd_a_log"], p["ssd_d"], p["attn_sinks"], loss), jnp.zeros((1, D_MODEL), F32)], axis=0)


def _sharded_rows(conv_w, b_qkv, b_o):
    last = jnp.concatenate([b_qkv.reshape(-1), b_o.reshape(-1), jnp.zeros((D_MODEL - 640,), F32)]).reshape(1, D_MODEL)
    return jnp.concatenate([conv_w.reshape(SSD_CONV_WIDTH, D_MODEL), last, jnp.zeros((3, D_MODEL), F32)], axis=0)


REPLICATED = ("ssd_conv_b", "ssd_dt_bias", "ssd_a_log", "ssd_d", "ssd_norm_w", "attn_sinks",
              "mix_pre_norm", "mix_post_norm", "ffn_pre_norm", "ffn_post_norm")
MATRICES = ("ssd_w_in", "ssd_w_out", "attn_w_qkv", "attn_w_o", "mlp_w_up", "mlp_w_down")
WEIGHT_NAMES = ("ssd_w_in", "ssd_conv_w", "ssd_conv_b", "ssd_dt_bias", "ssd_a_log", "ssd_d", "ssd_norm_w", "ssd_w_out",
                "attn_w_qkv", "attn_b_qkv", "attn_sinks", "attn_w_o", "attn_b_o", "mlp_w_up", "mlp_w_down",
                "mix_pre_norm", "mix_post_norm", "ffn_pre_norm", "ffn_post_norm")


def _unpack_small(rows16, rows8, like):
    misc = rows16[SM_MISC]
    out = {
        "ssd_conv_b": rows16[SM_CONV_B:SM_CONV_B + 4], "ssd_norm_w": rows16[SM_NORM_W:SM_NORM_W + 2],
        "mix_pre_norm": rows16[SM_MIX_PRE:SM_MIX_PRE + 2], "mix_post_norm": rows16[SM_MIX_POST:SM_MIX_POST + 2],
        "ffn_pre_norm": rows16[SM_FFN_PRE:SM_FFN_PRE + 2], "ffn_post_norm": rows16[SM_FFN_POST:SM_FFN_POST + 2],
        "ssd_dt_bias": misc[MISC_DT_BIAS:MISC_DT_BIAS + 32], "ssd_a_log": misc[MISC_A_LOG:MISC_A_LOG + 32],
        "ssd_d": misc[MISC_D:MISC_D + 32], "attn_sinks": misc[MISC_SINKS:MISC_SINKS + 16],
        "ssd_conv_w": rows8[0:SSD_CONV_WIDTH], "attn_b_qkv": rows8[SSD_CONV_WIDTH, 0:384], "attn_b_o": rows8[SSD_CONV_WIDTH, 384:640],
    }
    return {k: v.reshape(like[k].shape) for k, v in out.items()}


def kernel(x, ssd_w_in, ssd_conv_w, ssd_conv_b, ssd_dt_bias, ssd_a_log, ssd_d, ssd_norm_w, ssd_w_out, attn_w_qkv, attn_b_qkv, attn_sinks, attn_w_o, attn_b_o, mlp_w_up, mlp_w_down, mix_pre_norm, mix_post_norm, ffn_pre_norm, ffn_post_norm, loss_target, m_ssd_w_in, m_ssd_conv_w, m_ssd_conv_b, m_ssd_dt_bias, m_ssd_a_log, m_ssd_d, m_ssd_norm_w, m_ssd_w_out, m_attn_w_qkv, m_attn_b_qkv, m_attn_sinks, m_attn_w_o, m_attn_b_o, m_mlp_w_up, m_mlp_w_down, m_mix_pre_norm, m_mix_post_norm, m_ffn_pre_norm, m_ffn_post_norm, v_ssd_w_in, v_ssd_conv_w, v_ssd_conv_b, v_ssd_dt_bias, v_ssd_a_log, v_ssd_d, v_ssd_norm_w, v_ssd_w_out, v_attn_w_qkv, v_attn_b_qkv, v_attn_sinks, v_attn_w_o, v_attn_b_o, v_mlp_w_up, v_mlp_w_down, v_mix_pre_norm, v_mix_post_norm, v_ffn_pre_norm, v_ffn_post_norm):
    w = dict(zip(WEIGHT_NAMES, (ssd_w_in, ssd_conv_w, ssd_conv_b, ssd_dt_bias, ssd_a_log, ssd_d, ssd_norm_w, ssd_w_out, attn_w_qkv, attn_b_qkv, attn_sinks, attn_w_o, attn_b_o, mlp_w_up, mlp_w_down, mix_pre_norm, mix_post_norm, ffn_pre_norm, ffn_post_norm)))
    m = dict(zip(WEIGHT_NAMES, (m_ssd_w_in, m_ssd_conv_w, m_ssd_conv_b, m_ssd_dt_bias, m_ssd_a_log, m_ssd_d, m_ssd_norm_w, m_ssd_w_out, m_attn_w_qkv, m_attn_b_qkv, m_attn_sinks, m_attn_w_o, m_attn_b_o, m_mlp_w_up, m_mlp_w_down, m_mix_pre_norm, m_mix_post_norm, m_ffn_pre_norm, m_ffn_post_norm)))
    v = dict(zip(WEIGHT_NAMES, (v_ssd_w_in, v_ssd_conv_w, v_ssd_conv_b, v_ssd_dt_bias, v_ssd_a_log, v_ssd_d, v_ssd_norm_w, v_ssd_w_out, v_attn_w_qkv, v_attn_b_qkv, v_attn_sinks, v_attn_w_o, v_attn_b_o, v_mlp_w_up, v_mlp_w_down, v_mix_pre_norm, v_mix_post_norm, v_ffn_pre_norm, v_ffn_post_norm)))
    chip = 2 * lax.axis_index("x") + lax.axis_index("y")

    two_halves = lambda a: a.reshape(2, a.shape[-2] // 2, a.shape[-1])
    later = {"ssd_w_out": (w["ssd_w_out"], 0), "attn_w_qkv": (w["attn_w_qkv"], 0), "attn_w_o": (w["attn_w_o"], 0),
             "mlp_w_up0": (w["mlp_w_up"], 0), "mlp_w_up1": (w["mlp_w_up"], 1),
             "mlp_w_down0": (w["mlp_w_down"], 0), "mlp_w_down1": (w["mlp_w_down"], 1)}
    first = _GatherHook([two_halves(w["ssd_w_in"].astype(BF16))], [w["ssd_conv_w"][0], w["attn_b_qkv"], w["attn_b_o"]])
    (cast, u0), (g_in, g_conv, g_bqkv, g_bo) = _cast_bf16(list(later.values()), x[0], w["mix_pre_norm"][0:1],
                                                          name="weights_to_bf16", hook=first)
    core = lax.axis_index("c").astype(jnp.int32).reshape(1)
    comm = _StepComm({k: two_halves(a) for k, a in zip(later, cast)}, core)
    full = {
        "ssd_w_in": _full_weight("ssd_w_in", g_in),
        "ssd_conv_w": g_conv.transpose(1, 0, 2).reshape(SSD_CONV_WIDTH, SSD_CONV_DIM),
        "attn_b_qkv": g_bqkv.reshape(ATTN_QKV), "attn_b_o": g_bo.reshape(D_MODEL),
    }
    for name in REPLICATED:
        full[name] = w[name][0] if name.startswith(("ssd_", "attn_")) else w[name]

    loss_tile, grad_x, gm, g = _local_step(x[0], loss_target[0], full, comm, u0)

    conv_w_rows = g["ssd_conv_w"].reshape(SSD_CONV_WIDTH * N_CHIPS, D_MODEL)
    b_qkv_rows = jnp.pad(g["attn_b_qkv"], (0, 2 * D_MODEL - ATTN_QKV)).reshape(2, D_MODEL)
    small = jnp.concatenate([_replicated_rows(g, loss_tile[0, 0]), conv_w_rows, b_qkv_rows, _rows(g["attn_b_o"]),
                             jnp.zeros((SM_ROWS - SM_B_O - 1, D_MODEL), F32)], axis=0)
    order = ("ssd_w_in", "ssd_w_out", "attn_w_qkv", "attn_w_o", "mlp_w_up0", "mlp_w_up1", "mlp_w_down0", "mlp_w_down1")
    halves = _sum_chips([comm.chip_parts[k] for k in order], name="grad_sum")
    (r_in, r_out, r_qkv, r_o, r_up, r_down), (small_all,) = _swap_halves(
        halves, layers=((4, 5), (6, 7)), hook=_ExchangeHook([], [small]), name="grad_halves_swap")
    small_sum, = _sum_chips([small_all], name="small_grad_sum")

    grads = {"ssd_w_in": r_in, "ssd_w_out": r_out, "attn_w_qkv": r_qkv, "attn_w_o": r_o, "mlp_w_up": r_up, "mlp_w_down": r_down}
    grads = {k: a.reshape(w[k].shape) for k, a in grads.items()}
    conv_w_g = lax.dynamic_index_in_dim(small_sum[SM_CONV_W:SM_CONV_W + 16].reshape(SSD_CONV_WIDTH, N_CHIPS, D_MODEL), chip, axis=1, keepdims=False)
    b_qkv_g = lax.dynamic_slice_in_dim(small_sum[SM_B_QKV:SM_B_QKV + 2].reshape(-1), chip * 384, 384)
    b_o_g = lax.dynamic_slice_in_dim(small_sum[SM_B_O], chip * 256, 256)
    small_g = jnp.concatenate([small_sum[0:16], _sharded_rows(conv_w_g, b_qkv_g, b_o_g)], axis=0)
    grads.update(_unpack_small(small_g[0:16], small_g[16:24], w))
    loss = small_sum[SM_MISC, MISC_LOSS]

    delta, new_m, new_v = {}, {}, {}
    stored = lambda a: jnp.swapaxes(a, 1, 2)
    rest = [name for name in MATRICES if name != "ssd_w_in"]
    mats = lambda p: [p[name] for name in rest]
    results = dict(zip(rest, _adamw(mats(w), mats(grads), mats(m), mats(v), name="adamw_matrices")))
    (w_in_result,) = _adamw([stored(w["ssd_w_in"])], [stored(grads["ssd_w_in"])], [stored(m["ssd_w_in"])],
                            [stored(v["ssd_w_in"])], by_lanes=True, name="adamw_ssd_w_in")
    results["ssd_w_in"] = tuple(stored(a) for a in w_in_result)
    for name in MATRICES:
        grads[name], delta[name], new_m[name], new_v[name] = results[name]
    zero = jnp.zeros((), F32)
    small_pack = lambda p: jnp.concatenate([_replicated_rows({k: p[k] for k in REPLICATED}, zero),
                                            _sharded_rows(p["ssd_conv_w"], p["attn_b_qkv"], p["attn_b_o"])], axis=0)[None]
    (_, d_s, m_s, v_s), = _adamw([small_pack(w)], [small_g[None]], [small_pack(m)], [small_pack(v)], name="adamw_vectors")
    d_s, m_s, v_s = d_s[0], m_s[0], v_s[0]
    delta.update(_unpack_small(d_s[0:16], d_s[16:24], w))
    new_m.update(_unpack_small(m_s[0:16], m_s[16:24], w))
    new_v.update(_unpack_small(v_s[0:16], v_s[16:24], w))

    return (loss, grad_x[None], *[grads[n] for n in WEIGHT_NAMES], *[delta[n] for n in WEIGHT_NAMES],
            *[new_m[n] for n in WEIGHT_NAMES], *[new_v[n] for n in WEIGHT_NAMES])
```

```python
import functools

import jax
import jax.numpy as jnp
from jax import lax
from jax.experimental import pallas as pl
from jax.experimental.pallas import tpu as pltpu

F32 = jnp.float32
BF16 = jnp.bfloat16

D_MODEL = 1024
SSD_D_INNER = 2048
SSD_HEAD_DIM = 64
SSD_N_HEADS = 32
SSD_N_GROUPS = 8
SSD_HPG = 4
SSD_D_STATE = 128
SSD_CONV_WIDTH = 4
SSD_CHUNK = 128
SSD_CONV_DIM = 4096
SSD_IN_DIM = 6176
SSD_IN_PAD = 6400
SSD_IN_TILE = 1280
SSD_DT_COL = 6144
SSD_GW = SSD_HPG * SSD_HEAD_DIM
ATTN_HEAD_DIM = 64
ATTN_N_Q = 16
ATTN_N_KV = 4
ATTN_REP = 4
ATTN_WINDOW = 128
ATTN_QKV = 1536
D_FF = 4096
NORM_EPS = 1e-6

ADAM_LR = 0.001
ADAM_B1 = 0.9
ADAM_B2 = 0.999
ADAM_EPS = 1e-08
ADAM_WD = 0.01
ADAM_STEP = 10

N_CHIPS = 4
N_DEV = 8
LANES = 128
VMEM_LIMIT = 48 * 1024 * 1024
BIG_TILE = 2048
MESH = pl.DeviceIdType.MESH


def _params(*sem):
    return pltpu.CompilerParams(dimension_semantics=sem, vmem_limit_bytes=VMEM_LIMIT)


def _dot(a, b, dims):
    return lax.dot_general(a, b, (dims, ((), ())), preferred_element_type=F32)


def _dot_nn(a, b):
    return _dot(a, b, ((1,), (0,)))


def _dot_nt(a, b):
    return _dot(a, b, ((1,), (1,)))


def _dot_tn(a, b):
    return _dot(a, b, ((0,), (0,)))


def _sigmoid(x):
    return 0.5 * jnp.tanh(0.5 * x) + 0.5


ANY = pl.BlockSpec(memory_space=pl.ANY)


class _HookSlots:
    def __init__(self, hook, n_in, n_out, n_scratch):
        self.hook = hook
        self.n_in, self.n_out, self.n_scratch = n_in, n_out, n_scratch
        self.inputs = list(hook.arrs) if hook else []
        self.out_shape = list(hook.out_shape) if hook else []
        self.scratch = list(hook.scratch) if hook else []
        self.in_specs = [ANY] * len(self.inputs)
        self.out_specs = [ANY] * len(self.out_shape)

    def _split(self, refs):
        a = self.n_in
        b = a + len(self.inputs)
        c = b + self.n_out
        d = c + len(self.out_shape)
        e = d + self.n_scratch
        return refs[:a], refs[a:b], refs[b:c], refs[c:d], refs[d:e], refs[e:]

    def own(self, refs):
        ins, _, outs, _, scratch, _ = self._split(refs)
        return ins, outs, scratch

    def plan(self, refs):
        _, h_in, _, h_out, _, h_scratch = self._split(refs)
        return self.hook.plan(h_in, h_out, h_scratch)

    def run(self, refs, step, n_steps):
        _, h_in, _, h_out, _, h_scratch = self._split(refs)
        _run_hook(self.hook, h_in, h_out, h_scratch, step, n_steps)

    def semantics(self, *sem):
        return sem if self.hook is None else ("arbitrary",) * len(sem)


def _matmul(a, b, *, mode, out_dtypes, name, epilogue=None, extras=(), tm=1024, tn=1024, tk=1024,
            b_shards=False, out_shards=False, hook=None, f32_block=None):
    f32_tail = f32_block is not None
    if b_shards:
        s, b_rows, b_cols = b.shape
        b2 = (b_rows, s * b_cols)
        if mode == "nn":
            tn = b_cols
        else:
            assert mode == "nt"
            tk = b_cols
    else:
        b2 = b.shape
    if mode == "nn":
        (m, k), (k2, n) = a.shape, b2
    elif mode == "nt":
        (m, k), (n, k2) = a.shape, b2
    else:
        (k, m), (k2, n) = a.shape, b2
    assert k == k2, (a.shape, b.shape, mode)
    tm, tn, tk = min(tm, m), min(tn, n), min(tk, k)
    assert m % tm == 0 and n % tn == 0 and k % tk == 0, (m, n, k, tm, tn, tk)
    nk = k // tk
    if mode == "tn":
        a_spec = pl.BlockSpec((tk, tm), lambda i, j, kk: (kk, i))
    else:
        a_spec = pl.BlockSpec((tm, tk), lambda i, j, kk: (i, kk))
    if b_shards and mode == "nn":
        b_spec = pl.BlockSpec((None, tk, tn), lambda i, j, kk: (j, kk, 0))
    elif b_shards:
        b_spec = pl.BlockSpec((None, tn, tk), lambda i, j, kk: (kk, j, 0))
    elif mode == "nt":
        b_spec = pl.BlockSpec((tn, tk), lambda i, j, kk: (j, kk))
    else:
        b_spec = pl.BlockSpec((tk, tn), lambda i, j, kk: (kk, j))
    dims = {"nn": ((1,), (0,)), "nt": ((1,), (1,)), "tn": ((0,), (0,))}[mode]
    ex_specs = []
    for arr, kind in extras:
        if kind == "tile":
            ex_specs.append(pl.BlockSpec((tm, tn), lambda i, j, kk: (i, j)))
        else:
            ex_specs.append(pl.BlockSpec((1, tn), lambda i, j, kk: (0, j)))
    n_ex, n_out = len(extras), len(out_dtypes)
    if epilogue is None:
        epilogue = lambda acc: (acc,)
    hk = _HookSlots(hook, n_in=2 + n_ex, n_out=n_out + f32_tail, n_scratch=0 if nk == 1 else 1)
    grid = (m // tm, n // tn, nk)

    def body(*refs):
        (a_ref, b_ref, *ex), outs, scratch = hk.own(refs)
        if hook is not None:
            step = (pl.program_id(0) * grid[1] + pl.program_id(1)) * grid[2] + pl.program_id(2)
            hk.run(refs, step, grid[0] * grid[1] * grid[2])

        def finish(acc):
            res = epilogue(acc, *[e[...] for e in ex])
            for o, r in zip(outs, res):
                o[...] = r.astype(o.dtype)
            if f32_tail:
                outs[n_out][...] = acc[:, f32_block:f32_block + LANES]

        if nk == 1:
            finish(_dot(a_ref[...], b_ref[...], dims))
        else:
            acc_ref = scratch[0]
            kk = pl.program_id(2)

            @pl.when(kk == 0)
            def _():
                acc_ref[...] = jnp.zeros_like(acc_ref)

            acc_ref[...] += _dot(a_ref[...], b_ref[...], dims)

            @pl.when(kk == nk - 1)
            def _():
                finish(acc_ref[...])

    if out_shards:
        out_spec = pl.BlockSpec((None, tm, tn), lambda i, j, kk: (j, i, 0))
        out_dims = (n // tn, m, tn)
    else:
        out_spec = pl.BlockSpec((tm, tn), lambda i, j, kk: (i, j))
        out_dims = (m, n)
    tail_specs = [pl.BlockSpec((tm, LANES), lambda i, j, kk: (i, 0))] if f32_tail else []
    tail_shapes = [jax.ShapeDtypeStruct((m, LANES), F32)] if f32_tail else []
    outs = pl.pallas_call(
        body,
        grid=grid,
        in_specs=[a_spec, b_spec] + ex_specs + hk.in_specs,
        out_specs=[out_spec for _ in out_dtypes] + tail_specs + hk.out_specs,
        out_shape=[jax.ShapeDtypeStruct(out_dims, dt) for dt in out_dtypes] + tail_shapes + hk.out_shape,
        scratch_shapes=([] if nk == 1 else [pltpu.VMEM((tm, tn), F32)]) + hk.scratch,
        compiler_params=_params(*hk.semantics("parallel", "arbitrary" if f32_tail else "parallel", "arbitrary")),
        name=name,
    )(a, b, *[arr for arr, _ in extras], *hk.inputs)
    n_own = n_out + f32_tail
    own = outs[0] if n_own == 1 else outs[:n_own]
    return own if hook is None else (own, outs[n_own:])


def _row_tile(t, want):
    return min(t, want)


def _rms_fwd(x, w, *, name, resid=None, want_u=None, target=None):
    t, d = x.shape
    tr = _row_tile(t, 1024)

    def norm(v, wv):
        return v * lax.rsqrt(jnp.mean(v * v, axis=-1, keepdims=True) + NORM_EPS) * wv

    row = pl.BlockSpec((tr, d), lambda i: (i, 0))
    vec = pl.BlockSpec((1, d), lambda i: (0, 0))
    if target is not None:
        def body(x_ref, w_ref, r_ref, t_ref, dh_ref, loss_ref):
            err = r_ref[...] + norm(x_ref[...].astype(F32), w_ref[...]) - t_ref[...]
            dh_ref[...] = err * (1.0 / d)

            @pl.when(pl.program_id(0) == 0)
            def _():
                loss_ref[...] = jnp.zeros_like(loss_ref)

            part = jnp.sum(jnp.sum(err * err, axis=1, keepdims=True), axis=0, keepdims=True) * (0.5 / d)
            loss_ref[...] += jnp.broadcast_to(part, loss_ref.shape)

        return pl.pallas_call(
            body, grid=(t // tr,), in_specs=[row, vec, row, row],
            out_specs=[row, pl.BlockSpec((8, LANES), lambda i: (0, 0))],
            out_shape=[jax.ShapeDtypeStruct((t, d), F32), jax.ShapeDtypeStruct((8, LANES), F32)],
            compiler_params=_params("arbitrary"), name=name)(x, w, resid, target)
    if resid is None:
        def body(x_ref, w_ref, o_ref):
            o_ref[...] = norm(x_ref[...].astype(F32), w_ref[...]).astype(BF16)
        ins, in_specs = (x, w), [row, vec]
        out_shape, out_specs = jax.ShapeDtypeStruct((t, d), BF16), row
    elif want_u is None:
        def body(x_ref, w_ref, r_ref, o_ref):
            o_ref[...] = r_ref[...] + norm(x_ref[...].astype(F32), w_ref[...])
        ins, in_specs = (x, w, resid), [row, vec, row]
        out_shape, out_specs = jax.ShapeDtypeStruct((t, d), F32), row
    else:
        def body(x_ref, w_ref, r_ref, w2_ref, o_ref, u_ref):
            h = r_ref[...] + norm(x_ref[...].astype(F32), w_ref[...])
            o_ref[...] = h
            u_ref[...] = norm(h, w2_ref[...]).astype(BF16)
        ins, in_specs = (x, w, resid, want_u), [row, vec, row, vec]
        out_shape = [jax.ShapeDtypeStruct((t, d), F32), jax.ShapeDtypeStruct((t, d), BF16)]
        out_specs = [row, row]
    return pl.pallas_call(body, grid=(t // tr,), in_specs=in_specs, out_specs=out_specs, out_shape=out_shape,
                          compiler_params=_params("parallel"), name=name)(*ins)


def _rms_bwd(x, w, dy, *, name, resid=None, out_dtype=F32, dx_col_sum=False):
    t, d = x.shape
    tr = _row_tile(t, 1024)
    row = pl.BlockSpec((tr, d), lambda i: (i, 0))
    vec = pl.BlockSpec((1, d), lambda i: (0, 0))
    has_res = resid is not None

    def body(x_ref, w_ref, dy_ref, *rest):
        r_ref = rest[0] if has_res else None
        dx_ref, dw_ref = rest[has_res:has_res + 2]
        xv = x_ref[...].astype(F32)
        dyv = dy_ref[...].astype(F32)
        r = lax.rsqrt(jnp.mean(xv * xv, axis=-1, keepdims=True) + NORM_EPS)
        xhat = xv * r
        dyw = dyv * w_ref[...]
        dx = r * (dyw - xhat * jnp.mean(dyw * xhat, axis=-1, keepdims=True))
        if has_res:
            dx = dx + r_ref[...]
        dx_ref[...] = dx.astype(dx_ref.dtype)

        sums = [(dw_ref, dyv * xhat)] + ([(rest[-1], dx)] if dx_col_sum else [])

        @pl.when(pl.program_id(0) == 0)
        def _():
            for acc_ref, _ in sums:
                acc_ref[...] = jnp.zeros_like(acc_ref)

        for acc_ref, rows in sums:
            acc_ref[...] += jnp.sum(rows, axis=0, keepdims=True)

    ins = (x, w, dy) + ((resid,) if has_res else ())
    in_specs = [row, vec, row] + ([row] if has_res else [])
    n_vec = 2 if dx_col_sum else 1
    return pl.pallas_call(
        body, grid=(t // tr,), in_specs=in_specs, out_specs=[row] + [vec] * n_vec,
        out_shape=[jax.ShapeDtypeStruct((t, d), out_dtype)] + [jax.ShapeDtypeStruct((1, d), F32)] * n_vec,
        compiler_params=_params("arbitrary"), name=name)(*ins)


SSD_IN_SHARD = SSD_IN_DIM // N_CHIPS


def _w_in_from_shards(shards, *, name):
    d = shards.shape[1]
    tr = 256

    def body(s_ref, o_ref):
        o_ref[:, pl.ds(SSD_DT_COL, SSD_IN_PAD - SSD_DT_COL)] = jnp.zeros((tr, SSD_IN_PAD - SSD_DT_COL), o_ref.dtype)
        for s in range(N_CHIPS):
            o_ref[:, pl.ds(SSD_IN_SHARD * s, SSD_IN_SHARD)] = s_ref[s]

    return pl.pallas_call(
        body, grid=(d // tr,), in_specs=[pl.BlockSpec((N_CHIPS, tr, SSD_IN_SHARD), lambda i: (0, i, 0))],
        out_specs=pl.BlockSpec((tr, SSD_IN_PAD), lambda i: (i, 0)),
        out_shape=jax.ShapeDtypeStruct((d, SSD_IN_PAD), shards.dtype),
        compiler_params=_params("parallel"), name=name)(shards)


def _w_in_to_shards(g, *, name):
    d = g.shape[0]
    tr = 256

    def body(g_ref, o_ref):
        for s in range(N_CHIPS):
            o_ref[s] = g_ref[:, pl.ds(SSD_IN_SHARD * s, SSD_IN_SHARD)].astype(o_ref.dtype)

    return pl.pallas_call(
        body, grid=(d // tr,), in_specs=[pl.BlockSpec((tr, SSD_IN_PAD), lambda i: (i, 0))],
        out_specs=pl.BlockSpec((N_CHIPS, tr, SSD_IN_SHARD), lambda i: (0, i, 0)),
        out_shape=jax.ShapeDtypeStruct((N_CHIPS, d, SSD_IN_SHARD), BF16),
        compiler_params=_params("parallel"), name=name)(g)


XBC_COL0 = SSD_D_INNER // LANES


def _shift_down(v, k, row_ids):
    return jnp.where(row_ids >= k, pltpu.roll(v, k, axis=0), 0.0)


def _shift_up(v, k, row_ids):
    n = v.shape[0]
    return jnp.where(row_ids < n - k, pltpu.roll(v, n - k, axis=0), 0.0)


def _conv_pre(x, w, b, row_ids):
    pre = b + w[3:4, :] * x
    for k in (1, 2, 3):
        pre = pre + w[3 - k:4 - k, :] * _shift_down(x, k, row_ids)
    return pre


def _conv_fwd(zx, conv_w, conv_b, *, name, hook=None):
    t = zx.shape[0]
    cw = 2 * LANES
    nct = SSD_CONV_DIM // cw
    col0 = SSD_D_INNER // cw
    hk = _HookSlots(hook, n_in=3, n_out=1, n_scratch=0)

    def body(*refs):
        (x_ref, w_ref, b_ref), (o_ref,), _ = hk.own(refs)
        if hook is not None:
            hk.run(refs, pl.program_id(0), nct)
        x = x_ref[...].astype(F32)
        row_ids = lax.broadcasted_iota(jnp.int32, x.shape, 0)
        pre = _conv_pre(x, w_ref[...], b_ref[...], row_ids)
        o_ref[...] = pre * _sigmoid(pre)

    outs = pl.pallas_call(
        body, grid=(nct,),
        in_specs=[pl.BlockSpec((t, cw), lambda j: (0, col0 + j)),
                  pl.BlockSpec((SSD_CONV_WIDTH, cw), lambda j: (0, j)),
                  pl.BlockSpec((1, cw), lambda j: (0, j))] + hk.in_specs,
        out_specs=[pl.BlockSpec((t, cw), lambda j: (0, j))] + hk.out_specs,
        out_shape=[jax.ShapeDtypeStruct((t, SSD_CONV_DIM), F32)] + hk.out_shape,
        scratch_shapes=hk.scratch,
        compiler_params=_params(*hk.semantics("parallel")), name=name)(zx, conv_w, conv_b, *hk.inputs)
    return outs[0] if hook is None else (outs[0], outs[1:])


def _conv_bwd(zx, conv_w, conv_b, d_xs, d_bm, d_cm, dzx, *, name):
    t = zx.shape[0]
    nct = SSD_CONV_DIM // LANES
    n_xs = SSD_D_INNER // LANES
    n_bm = SSD_N_GROUPS * SSD_D_STATE // LANES

    def body(x_ref, w_ref, b_ref, dxs_ref, dbm_ref, dcm_ref, _, dx_ref, dw_ref, db_ref):
        x = x_ref[...].astype(F32)
        w = w_ref[...]
        j = pl.program_id(0)
        dy = jnp.where(j < n_xs, dxs_ref[...], jnp.where(j < n_xs + n_bm, dbm_ref[...], dcm_ref[...]))
        row_ids = lax.broadcasted_iota(jnp.int32, x.shape, 0)
        pre = _conv_pre(x, w, b_ref[...], row_ids)
        sg = _sigmoid(pre)
        dpre = dy * (sg * (1.0 + pre * (1.0 - sg)))
        dx = w[3:4, :] * dpre
        for k in (1, 2, 3):
            dx = dx + w[3 - k:4 - k, :] * _shift_up(dpre, k, row_ids)
        dx_ref[...] = dx.astype(dx_ref.dtype)
        db_ref[...] = jnp.sum(dpre, axis=0, keepdims=True)
        dw_ref[3:4, :] = jnp.sum(dpre * x, axis=0, keepdims=True)
        for k in (1, 2, 3):
            dw_ref[3 - k:4 - k, :] = jnp.sum(dpre * _shift_down(x, k, row_ids), axis=0, keepdims=True)

    clip = lambda j, lo, n: jnp.clip(j - lo, 0, n - 1)
    return pl.pallas_call(
        body, grid=(nct,),
        in_specs=[pl.BlockSpec((t, LANES), lambda j: (0, XBC_COL0 + j)),
                  pl.BlockSpec((SSD_CONV_WIDTH, LANES), lambda j: (0, j)),
                  pl.BlockSpec((1, LANES), lambda j: (0, j)),
                  pl.BlockSpec((t, LANES), lambda j: (0, clip(j, 0, n_xs))),
                  pl.BlockSpec((t, LANES), lambda j: (0, clip(j, n_xs, n_bm))),
                  pl.BlockSpec((t, LANES), lambda j: (0, clip(j, n_xs + n_bm, n_bm))), ANY],
        out_specs=[pl.BlockSpec((t, LANES), lambda j: (0, XBC_COL0 + j)),
                   pl.BlockSpec((SSD_CONV_WIDTH, LANES), lambda j: (0, j)), pl.BlockSpec((1, LANES), lambda j: (0, j))],
        out_shape=[jax.ShapeDtypeStruct(dzx.shape, dzx.dtype),
                   jax.ShapeDtypeStruct((SSD_CONV_WIDTH, SSD_CONV_DIM), F32),
                   jax.ShapeDtypeStruct((1, SSD_CONV_DIM), F32)],
        input_output_aliases={6: 0},
        compiler_params=_params("parallel"), name=name)(zx, conv_w, conv_b, d_xs, d_bm, d_cm, dzx)


def _softplus_fwd(dt_raw, bias_row, alog_row, *, name):
    t = dt_raw.shape[0]
    q = SSD_CHUNK
    tr = _row_tile(t, 1024)

    def body(x_ref, b_ref, al_ref, dt_ref, cum_ref):
        v = x_ref[...] + b_ref[...]
        e = jnp.exp(-jnp.abs(v))
        u = 1.0 + e
        log1p = jnp.where(u == 1.0, e, jnp.log(u) * (e / (u - 1.0)))
        dt = jnp.maximum(v, 0.0) + log1p
        a = dt * -jnp.exp(al_ref[...])
        lower = (lax.broadcasted_iota(jnp.int32, (q, q), 1) <= lax.broadcasted_iota(jnp.int32, (q, q), 0)).astype(F32)
        cums = [lax.dot_general(lower, a[c * q:(c + 1) * q, :], ((((1,), (0,))), ((), ())), precision=lax.Precision.HIGHEST,
                                preferred_element_type=F32) for c in range(tr // q)]
        dt_t, cum_t = dt.T, jnp.concatenate(cums, axis=0).T
        for g in range(SSD_N_GROUPS):
            rows = slice(g * SSD_HPG, (g + 1) * SSD_HPG)
            dt_ref[g] = dt_t[rows, :]
            cum_ref[g] = cum_t[rows, :]

    vec = pl.BlockSpec((1, LANES), lambda i: (0, 0))
    by_group = pl.BlockSpec((SSD_N_GROUPS, SSD_HPG, tr), lambda i: (0, 0, i))
    return pl.pallas_call(
        body, grid=(t // tr,),
        in_specs=[pl.BlockSpec((tr, LANES), lambda i: (i, 0)), vec, vec],
        out_specs=[by_group, by_group],
        out_shape=[jax.ShapeDtypeStruct((SSD_N_GROUPS, SSD_HPG, t), F32)] * 2,
        compiler_params=_params("parallel"), name=name)(dt_raw, bias_row, alog_row)


def _softplus_bwd(dt_raw, bias_row, ddt_rows, dzx, *, name):
    t = dt_raw.shape[0]
    tr = _row_tile(t, 1024)
    tail = SSD_IN_PAD - SSD_DT_COL

    def body(x_ref, b_ref, g_ref, _, o_ref, db_ref):
        v = x_ref[...] + b_ref[...]
        lane = lax.broadcasted_iota(jnp.int32, v.shape, 1)
        by_head = jnp.concatenate([g_ref[g] for g in range(SSD_N_GROUPS)]
                                  + [jnp.zeros((LANES - SSD_N_HEADS, tr), F32)], axis=0)
        d = jnp.where(lane < SSD_N_HEADS, by_head.T * _sigmoid(v), 0.0)
        o_ref[:, pl.ds(0, LANES)] = d.astype(o_ref.dtype)
        o_ref[:, pl.ds(LANES, tail - LANES)] = jnp.zeros((tr, tail - LANES), o_ref.dtype)

        @pl.when(pl.program_id(0) == 0)
        def _():
            db_ref[...] = jnp.zeros_like(db_ref)

        db_ref[...] += jnp.sum(d, axis=0, keepdims=True)

    return pl.pallas_call(
        body, grid=(t // tr,),
        in_specs=[pl.BlockSpec((tr, LANES), lambda i: (i, 0)), pl.BlockSpec((1, LANES), lambda i: (0, 0)),
                  pl.BlockSpec((SSD_N_GROUPS, SSD_HPG, tr), lambda i: (0, 0, i)), ANY],
        out_specs=[pl.BlockSpec((tr, tail), lambda i: (i, SSD_DT_COL // tail)), pl.BlockSpec((1, LANES), lambda i: (0, 0))],
        out_shape=[jax.ShapeDtypeStruct(dzx.shape, dzx.dtype), jax.ShapeDtypeStruct((1, LANES), F32)],
        input_output_aliases={3: 0},
        compiler_params=_params("arbitrary"), name=name)(dt_raw, bias_row, ddt_rows, dzx)


def _ssd_masks():
    q = SSD_CHUNK
    tt = lax.broadcasted_iota(jnp.int32, (q, q), 0)
    ss = lax.broadcasted_iota(jnp.int32, (q, q), 1)
    lane = lax.broadcasted_iota(jnp.int32, (1, SSD_GW), 1)
    srow = lax.broadcasted_iota(jnp.int32, (SSD_GW, 1), 0)
    hm = [(lane >= SSD_HEAD_DIM * j) & (lane < SSD_HEAD_DIM * (j + 1)) for j in range(SSD_HPG)]
    rm = [(srow >= SSD_HEAD_DIM * j) & (srow < SSD_HEAD_DIM * (j + 1)) for j in range(SSD_HPG)]
    return tt, ss, hm, rm


def _ssd_head_terms(dt_rows, cum_rows, a_rows, j, tt, ss):
    q = SSD_CHUNK
    dt_row = dt_rows[j:j + 1, :]
    dt_col = jnp.sum(jnp.where(tt == ss, dt_row, 0.0), axis=1, keepdims=True)
    a_row1 = a_rows[j:j + 1, :]
    a_11 = a_rows[j:j + 1, 0:1]
    cum_col = jnp.sum(jnp.where(ss <= tt, dt_row * a_row1, 0.0), axis=1, keepdims=True)
    cum_row = cum_rows[j:j + 1, :]
    decay = jnp.exp(jnp.where(ss <= tt, cum_col - cum_row, -jnp.inf))
    cum_last = cum_col[q - 1:q, :]
    e_col = jnp.exp(cum_col)
    dte_col = jnp.exp(cum_last - cum_col)
    e_last = jnp.exp(cum_last)
    return dt_col, dt_row, a_row1, a_11, decay, e_col, dte_col, e_last


SSD_CHUNKS_PER_STEP = 8
SSD_BC_COL0 = SSD_D_INNER // SSD_D_STATE


def _ssd_head_selects(terms, hm, rm):
    e_all = jnp.zeros((SSD_CHUNK, SSD_GW), F32)
    w_all = jnp.zeros((SSD_CHUNK, SSD_GW), F32)
    e_s = jnp.zeros((SSD_GW, 1), F32)
    for j in range(SSD_HPG):
        dt_col, _, _, _, _, e_col, dte_col, e_last = terms[j]
        e_all = jnp.where(hm[j], e_col, e_all)
        w_all = jnp.where(hm[j], dt_col * dte_col, w_all)
        e_s = jnp.where(rm[j], e_last, e_s)
    return e_all, w_all, e_s


def _ssd_fwd(xc, dtr, cumr, alog_b, d_b, *, name, hook=None):
    t = xc.shape[0]
    q = SSD_CHUNK
    nc = t // q
    kc = min(SSD_CHUNKS_PER_STEP, nc)
    rows = kc * q
    hk = _HookSlots(hook, n_in=7, n_out=2, n_scratch=1)

    def body(*refs):
        (x_ref, b_ref, c_ref, dtr_ref, cumr_ref, alog_ref, d_ref), (y_ref, st_ref), (s_scr,) = hk.own(refs)
        if hook is not None:
            hk.run(refs, pl.program_id(0) * (nc // kc) + pl.program_id(1), SSD_N_GROUPS * (nc // kc))

        @pl.when(pl.program_id(1) == 0)
        def _():
            s_scr[...] = jnp.zeros_like(s_scr)

        tt, ss, hm, rm = _ssd_masks()
        a_rows = -jnp.exp(alog_ref[...])
        d_rows = d_ref[...]
        d_all = jnp.zeros((1, SSD_GW), F32)
        for j in range(SSD_HPG):
            d_all = jnp.where(hm[j], d_rows[j:j + 1, 0:1], d_all)
        ks, hs = range(kc), range(SSD_HPG)
        sl = [pl.ds(k * q, q) for k in ks]
        x = [x_ref[sl[k], :] for k in ks]
        bm = [b_ref[sl[k], :].astype(BF16) for k in ks]
        cm = [c_ref[sl[k], :].astype(BF16) for k in ks]
        xb = [x[k].astype(BF16) for k in ks]
        terms = [[_ssd_head_terms(dtr_ref[:, sl[k]], cumr_ref[:, sl[k]], a_rows, j, tt, ss) for j in hs] for k in ks]
        g = [_dot_nt(cm[k], bm[k]) for k in ks]
        m = [[(g[k] * terms[k][j][4] * terms[k][j][1]).astype(BF16) for j in hs] for k in ks]
        yj = [[_dot_nn(m[k][j], xb[k]) for j in hs] for k in ks]
        sel = [_ssd_head_selects(terms[k], hm, rm) for k in ks]
        upd = [_dot_tn((x[k] * sel[k][1]).astype(BF16), bm[k]) for k in ks]
        states = [s_scr[...]]
        for k in ks:
            states.append(states[k] * sel[k][2] + upd[k])
        inter = [_dot_nt(cm[k], states[k].astype(BF16)) for k in ks]
        ys = []
        for k in ks:
            y = jnp.zeros((q, SSD_GW), F32)
            for j in hs:
                y = jnp.where(hm[j], yj[k][j], y)
            ys.append(y + inter[k] * sel[k][0] + x[k] * d_all)
        for k in ks:
            st_ref[k] = states[k]
        y_ref[...] = jnp.concatenate(ys, axis=0).astype(y_ref.dtype)
        s_scr[...] = states[kc]

    blk = lambda width, off: pl.BlockSpec((rows, width), lambda g, c: (c, off + g))
    par_s = pl.BlockSpec((None, SSD_HPG, LANES), lambda g, c: (g, 0, 0))
    row_s = pl.BlockSpec((None, SSD_HPG, rows), lambda g, c: (g, 0, c))
    outs = pl.pallas_call(
        body, grid=(SSD_N_GROUPS, nc // kc),
        in_specs=[blk(SSD_GW, 0), blk(SSD_D_STATE, SSD_BC_COL0), blk(SSD_D_STATE, SSD_BC_COL0 + SSD_N_GROUPS),
                  row_s, row_s, par_s, par_s] + hk.in_specs,
        out_specs=[blk(SSD_GW, 0), pl.BlockSpec((None, kc, SSD_GW, SSD_D_STATE), lambda g, c: (g, c, 0, 0))] + hk.out_specs,
        out_shape=[jax.ShapeDtypeStruct((t, SSD_D_INNER), BF16),
                   jax.ShapeDtypeStruct((SSD_N_GROUPS, nc, SSD_GW, SSD_D_STATE), F32)] + hk.out_shape,
        scratch_shapes=[pltpu.VMEM((SSD_GW, SSD_D_STATE), F32)] + hk.scratch,
        compiler_params=_params(*hk.semantics("parallel", "arbitrary")), name=name)(
            xc, xc, xc, dtr, cumr, alog_b, d_b, *hk.inputs)
    return outs if hook is None else (outs[:2], outs[2:])


def _ssd_bwd(xc, dtr, cumr, alog_b, d_b, states, dy, *, name, hook=None):
    t = xc.shape[0]
    q = SSD_CHUNK
    nc = t // q
    kc = min(SSD_CHUNKS_PER_STEP, nc)
    nst = nc // kc
    rows = kc * q
    rev = lambda c: nst - 1 - c
    hk = _HookSlots(hook, n_in=9, n_out=5, n_scratch=1)

    def body(*refs):
        ((x_ref, b_ref, c_ref, dtr_ref, cumr_ref, alog_ref, d_ref, st_ref, dy_ref),
         (dx_ref, db_ref, dc_ref, ddt_ref, dpar_ref), (ds_scr,)) = hk.own(refs)
        if hook is not None:
            hk.run(refs, pl.program_id(0) * nst + pl.program_id(1), SSD_N_GROUPS * nst)

        @pl.when(pl.program_id(1) == 0)
        def _():
            ds_scr[...] = jnp.zeros_like(ds_scr)
            dpar_ref[...] = jnp.zeros_like(dpar_ref)

        tt, ss, hm, rm = _ssd_masks()
        tcol = lax.broadcasted_iota(jnp.int32, (q, 1), 0)
        lane = lax.broadcasted_iota(jnp.int32, (1, LANES), 1)
        a_rows = -jnp.exp(alog_ref[...])
        d_rows = d_ref[...]
        d_all = jnp.zeros((1, SSD_GW), F32)
        for j in range(SSD_HPG):
            d_all = jnp.where(hm[j], d_rows[j:j + 1, 0:1], d_all)
        ks, hs = range(kc), range(SSD_HPG)
        sl = [pl.ds(k * q, q) for k in ks]
        x = [x_ref[sl[k], :] for k in ks]
        dyv = [dy_ref[sl[k], :].astype(F32) for k in ks]
        bm = [b_ref[sl[k], :].astype(BF16) for k in ks]
        cm = [c_ref[sl[k], :].astype(BF16) for k in ks]
        s_in = [st_ref[k] for k in ks]
        xb = [x[k].astype(BF16) for k in ks]
        dyb = [dyv[k].astype(BF16) for k in ks]
        s_b = [s_in[k].astype(BF16) for k in ks]
        terms = [[_ssd_head_terms(dtr_ref[:, sl[k]], cumr_ref[:, sl[k]], a_rows, j, tt, ss) for j in hs] for k in ks]
        sel = [_ssd_head_selects(terms[k], hm, rm) for k in ks]
        e_all, w_all, e_s = [s_[0] for s_ in sel], [s_[1] for s_ in sel], [s_[2] for s_ in sel]
        dye = [(dyv[k] * e_all[k]).astype(BF16) for k in ks]
        ds_loc = [_dot_tn(dye[k], cm[k]) for k in ks]
        ds = [None] * kc
        running = ds_scr[...]
        for k in reversed(ks):
            ds[k] = running
            running = running * e_s[k] + ds_loc[k]
        ds_scr[...] = running
        ds_b = [ds[k].astype(BF16) for k in ks]
        g = [_dot_nt(cm[k], bm[k]) for k in ks]
        cs = [_dot_nt(cm[k], s_b[k]) for k in ks]
        bds = [_dot_nt(bm[k], ds_b[k]) for k in ks]
        dm = [[_dot_nt(jnp.where(hm[j], dyv[k], 0.0).astype(BF16), xb[k]) for j in hs] for k in ks]
        gl = [[g[k] * terms[k][j][4] for j in hs] for k in ks]
        wp = [[dm[k][j] * gl[k][j] for j in hs] for k in ks]
        mt = [[(gl[k][j] * terms[k][j][1]).astype(BF16) for j in hs] for k in ks]
        dxj = [[_dot_tn(mt[k][j], dyb[k]) for j in hs] for k in ks]
        dg = []
        for k in ks:
            acc = jnp.zeros((q, q), F32)
            for j in hs:
                acc = acc + dm[k][j] * terms[k][j][4] * terms[k][j][1]
            dg.append(acc.astype(BF16))
        dy_cs = [dyv[k] * cs[k] for k in ks]
        x_bds = [x[k] * bds[k] for k in ks]
        dy_x = [dyv[k] * x[k] for k in ks]
        ds_s = [ds[k] * s_in[k] for k in ks]
        w = [[wp[k][j] * terms[k][j][1] for j in hs] for k in ks]
        rw_col = [[jnp.sum(w[k][j], axis=1, keepdims=True) for j in hs] for k in ks]
        cw_row = [[jnp.sum(w[k][j], axis=0, keepdims=True) for j in hs] for k in ks]
        cwp_row = [[jnp.sum(wp[k][j], axis=0, keepdims=True) for j in hs] for k in ks]
        r1_col = [[jnp.sum(jnp.where(hm[j], dy_cs[k], 0.0), axis=1, keepdims=True) * terms[k][j][5] for j in hs] for k in ks]
        dw_col = [[jnp.sum(jnp.where(hm[j], x_bds[k], 0.0), axis=1, keepdims=True) for j in hs] for k in ks]
        head_rows = [slice(j * SSD_HEAD_DIM, (j + 1) * SSD_HEAD_DIM) for j in hs]
        lane_sum = lambda v: jnp.sum(v, axis=1, keepdims=True)
        s_sum = [[lane_sum(jnp.sum(ds_s[k][head_rows[j], :], axis=0, keepdims=True)) for j in hs] for k in ks]
        dy_x_cols = [jnp.sum(dy_x[k], axis=0, keepdims=True) for k in ks]
        d_d = [[lane_sum(jnp.where(hm[j], dy_x_cols[k], 0.0)) for j in hs] for k in ks]
        ddt_rows = [[None] * SSD_HPG for _ in ks]
        dpar = [jnp.zeros((1, LANES), F32) for _ in hs]
        for k in ks:
            for j in hs:
                dt_col, dt_row, a_row1, a_11, _, _, dte_col, e_last = terms[k][j]
                dww = dw_col[k][j] * (dt_col * dte_col)
                last_add = jnp.sum(dww, axis=0, keepdims=True) + e_last * s_sum[k][j]
                dcum_col = rw_col[k][j] + r1_col[k][j] - dww + jnp.where(tcol == q - 1, last_add, 0.0)
                da_row = jnp.sum(jnp.where(tt >= ss, dcum_col, 0.0), axis=0, keepdims=True)
                da_col = jnp.sum(jnp.where(ss >= tt, -cw_row[k][j], 0.0), axis=1, keepdims=True)
                ddt_col = a_11 * da_col + dw_col[k][j] * dte_col
                ddt_rows[k][j] = (a_row1 * da_row + cwp_row[k][j]
                                  + jnp.sum(jnp.where(tt == ss, ddt_col, 0.0), axis=0, keepdims=True))
                d_a = jnp.sum(dt_row * da_row, axis=1, keepdims=True) + jnp.sum(dt_col * da_col, axis=0, keepdims=True)
                dpar[j] = dpar[j] + jnp.where(lane == 0, d_a * a_11, 0.0) + jnp.where(lane == 1, d_d[k][j], 0.0)
        dxs = []
        for k in ks:
            acc = jnp.zeros((q, SSD_GW), F32)
            for j in hs:
                acc = jnp.where(hm[j], dxj[k][j], acc)
            dxs.append(acc + w_all[k] * bds[k] + d_all * dyv[k])
        xw = [(x[k] * w_all[k]).astype(BF16) for k in ks]
        dc = [_dot_nn(dg[k], bm[k]) + _dot_nn(dye[k], s_b[k]) for k in ks]
        db = [_dot_tn(dg[k], cm[k]) + _dot_nn(xw[k], ds_b[k]) for k in ks]
        dx_ref[...] = jnp.concatenate(dxs, axis=0)
        dc_ref[...] = jnp.concatenate(dc, axis=0)
        db_ref[...] = jnp.concatenate(db, axis=0)
        ddt_ref[...] = jnp.concatenate([jnp.concatenate([ddt_rows[k][j] for k in ks], axis=1) for j in hs], axis=0)
        dpar_ref[...] += jnp.concatenate(dpar, axis=0)

    blk = lambda width, off: pl.BlockSpec((rows, width), lambda g, c: (rev(c), off + g))
    par_s = pl.BlockSpec((None, SSD_HPG, LANES), lambda g, c: (g, 0, 0))
    outs = pl.pallas_call(
        body, grid=(SSD_N_GROUPS, nst),
        in_specs=[blk(SSD_GW, 0), blk(SSD_D_STATE, SSD_BC_COL0), blk(SSD_D_STATE, SSD_BC_COL0 + SSD_N_GROUPS),
                  pl.BlockSpec((None, SSD_HPG, rows), lambda g, c: (g, 0, rev(c))),
                  pl.BlockSpec((None, SSD_HPG, rows), lambda g, c: (g, 0, rev(c))), par_s, par_s,
                  pl.BlockSpec((None, kc, SSD_GW, SSD_D_STATE), lambda g, c: (g, rev(c), 0, 0)), blk(SSD_GW, 0)] + hk.in_specs,
        out_specs=[blk(SSD_GW, 0), blk(SSD_D_STATE, 0), blk(SSD_D_STATE, 0),
                   pl.BlockSpec((None, SSD_HPG, rows), lambda g, c: (g, 0, rev(c))), par_s] + hk.out_specs,
        out_shape=[jax.ShapeDtypeStruct((t, SSD_D_INNER), F32),
                   jax.ShapeDtypeStruct((t, SSD_N_GROUPS * SSD_D_STATE), F32),
                   jax.ShapeDtypeStruct((t, SSD_N_GROUPS * SSD_D_STATE), F32),
                   jax.ShapeDtypeStruct((SSD_N_GROUPS, SSD_HPG, t), F32),
                   jax.ShapeDtypeStruct((SSD_N_GROUPS, SSD_HPG, LANES), F32)] + hk.out_shape,
        scratch_shapes=[pltpu.VMEM((SSD_GW, SSD_D_STATE), F32)] + hk.scratch,
        compiler_params=_params(*hk.semantics("parallel", "arbitrary")), name=name)(
            xc, xc, xc, dtr, cumr, alog_b, d_b, states, dy, *hk.inputs)
    return outs if hook is None else (outs[:5], outs[5:])


def _gate_norm_fwd(y, zx, norm_w, *, name):
    t = y.shape[0]
    tr = _row_tile(t, 512)
    row = pl.BlockSpec((tr, SSD_D_INNER), lambda i: (i, 0))

    def body(y_ref, z_ref, w_ref, o_ref):
        for gi in range(SSD_N_GROUPS):
            sl = pl.ds(gi * SSD_GW, SSD_GW)
            z = z_ref[:, sl].astype(F32)
            gv = y_ref[:, sl].astype(F32) * (z * _sigmoid(z))
            r = lax.rsqrt(jnp.mean(gv * gv, axis=-1, keepdims=True) + NORM_EPS)
            o_ref[:, sl] = (gv * r * w_ref[:, sl]).astype(BF16)

    return pl.pallas_call(
        body, grid=(t // tr,), in_specs=[row, row, pl.BlockSpec((1, SSD_D_INNER), lambda i: (0, 0))],
        out_specs=row, out_shape=jax.ShapeDtypeStruct((t, SSD_D_INNER), BF16),
        compiler_params=_params("parallel"), name=name)(y, zx, norm_w)


def _gate_norm_bwd(y, zx, norm_w, dyn, *, name):
    t = y.shape[0]
    tr = _row_tile(t, 512)
    row = pl.BlockSpec((tr, SSD_D_INNER), lambda i: (i, 0))
    vec = pl.BlockSpec((1, SSD_D_INNER), lambda i: (0, 0))

    def body(y_ref, z_ref, w_ref, dyn_ref, dy_ref, dz_ref, dw_ref):
        @pl.when(pl.program_id(0) == 0)
        def _():
            dw_ref[...] = jnp.zeros_like(dw_ref)

        for gi in range(SSD_N_GROUPS):
            sl = pl.ds(gi * SSD_GW, SSD_GW)
            z = z_ref[:, sl].astype(F32)
            yv = y_ref[:, sl].astype(F32)
            sg = _sigmoid(z)
            sz = z * sg
            gv = yv * sz
            r = lax.rsqrt(jnp.mean(gv * gv, axis=-1, keepdims=True) + NORM_EPS)
            ghat = gv * r
            dout = dyn_ref[:, sl].astype(F32)
            dgh = dout * w_ref[:, sl]
            dgv = r * (dgh - ghat * jnp.mean(dgh * ghat, axis=-1, keepdims=True))
            dy_ref[:, sl] = (dgv * sz).astype(dy_ref.dtype)
            dz_ref[:, sl] = (dgv * yv * (sg * (1.0 + z * (1.0 - sg)))).astype(dz_ref.dtype)
            dw_ref[:, sl] += jnp.sum(dout * ghat, axis=0, keepdims=True)

    return pl.pallas_call(
        body, grid=(t // tr,), in_specs=[row, row, vec, row], out_specs=[row, row, vec],
        out_shape=[jax.ShapeDtypeStruct((t, SSD_D_INNER), BF16), jax.ShapeDtypeStruct((t, SSD_IN_PAD), BF16),
                   jax.ShapeDtypeStruct((1, SSD_D_INNER), F32)],
        compiler_params=_params("arbitrary"), name=name)(y, zx, norm_w, dyn)


ATTN_KV_W = ATTN_N_KV * ATTN_HEAD_DIM
ATTN_Q_HALF = 512
ATTN_K_BLK = ATTN_N_Q * ATTN_HEAD_DIM // ATTN_KV_W
ATTN_V_BLK = ATTN_K_BLK + 1


def _attn_valid(first_block):
    w = ATTN_WINDOW
    qpos = lax.broadcasted_iota(jnp.int32, (w, 2 * w), 0) + w
    kpos = lax.broadcasted_iota(jnp.int32, (w, 2 * w), 1)
    rel = qpos - kpos
    return (rel >= 0) & (rel < w) & jnp.logical_not(first_block & (kpos < w))


def _attn_head_views(lo_ref, hi_ref):
    hd = ATTN_HEAD_DIM
    per_half = ATTN_Q_HALF // hd
    return [(lo_ref if h < per_half else hi_ref)[:, pl.ds((h % per_half) * hd, hd)] for h in range(ATTN_N_Q)]


def _attn_block_views(lo_ref, hi_ref, kc_ref, kp_ref, vc_ref, vp_ref):
    hd = ATTN_HEAD_DIM
    kv_cols = [pl.ds(kh * hd, hd) for kh in range(ATTN_N_KV)]
    kb = [jnp.concatenate([kp_ref[:, c], kc_ref[:, c]], axis=0) for c in kv_cols]
    vb = [jnp.concatenate([vp_ref[:, c], vc_ref[:, c]], axis=0) for c in kv_cols]
    return _attn_head_views(lo_ref, hi_ref), kb, vb


def _attn_scores(q, kb, valid):
    scale = ATTN_HEAD_DIM ** -0.5
    return [jnp.where(valid, _dot_nt(q[h], kb[h // ATTN_REP]) * scale, -jnp.inf) for h in range(ATTN_N_Q)]


def _attn_softmax(s, sink):
    heads = range(ATTN_N_Q)
    m = [jnp.maximum(jnp.max(s[h], axis=1, keepdims=True), sink[h]) for h in heads]
    e = [jnp.exp(s[h] - m[h]) for h in heads]
    es = [jnp.exp(sink[h] - m[h]) for h in heads]
    inv = [1.0 / (jnp.sum(e[h], axis=1, keepdims=True) + es[h]) for h in heads]
    return e, es, inv


def _attn_fwd(qkv, sinks_b, *, name, hook=None):
    t = qkv.shape[0]
    w = ATTN_WINDOW
    nb = t // w
    prev = lambda n: jnp.maximum(n - 1, 0)
    hk = _HookSlots(hook, n_in=7, n_out=1, n_scratch=0)

    def body(*refs):
        (qlo_ref, qhi_ref, kc_ref, kp_ref, vc_ref, vp_ref, sink_ref), (o_ref,), _ = hk.own(refs)
        if hook is not None:
            hk.run(refs, pl.program_id(0), nb)
        heads = range(ATTN_N_Q)
        q, kb, vb = _attn_block_views(qlo_ref, qhi_ref, kc_ref, kp_ref, vc_ref, vp_ref)
        sink = [sink_ref[h:h + 1, 0:1] for h in heads]
        e, _, inv = _attn_softmax(_attn_scores(q, kb, _attn_valid(pl.program_id(0) == 0)), sink)
        out = [_dot_nn((e[h] * inv[h]).astype(BF16), vb[h // ATTN_REP]).astype(o_ref.dtype) for h in heads]
        o_ref[...] = jnp.concatenate(out, axis=1)

    qh = lambda half: pl.BlockSpec((w, ATTN_Q_HALF), lambda n: (n, half))
    kv = lambda blk, idx: pl.BlockSpec((w, ATTN_KV_W), lambda n: (idx(n), blk))
    cur = lambda n: n
    outs = pl.pallas_call(
        body, grid=(nb,),
        in_specs=[qh(0), qh(1), kv(ATTN_K_BLK, cur), kv(ATTN_K_BLK, prev), kv(ATTN_V_BLK, cur), kv(ATTN_V_BLK, prev),
                  pl.BlockSpec((ATTN_N_Q, LANES), lambda n: (0, 0))] + hk.in_specs,
        out_specs=[pl.BlockSpec((w, D_MODEL), lambda n: (n, 0))] + hk.out_specs,
        out_shape=[jax.ShapeDtypeStruct((t, D_MODEL), BF16)] + hk.out_shape,
        scratch_shapes=hk.scratch,
        compiler_params=_params(*hk.semantics("parallel")), name=name)(qkv, qkv, qkv, qkv, qkv, qkv, sinks_b, *hk.inputs)
    return outs[0] if hook is None else (outs[0], outs[1:])


def _attn_bwd(qkv, sinks_b, dout, *, name):
    t = qkv.shape[0]
    w = ATTN_WINDOW
    nb = t // w
    hd = ATTN_HEAD_DIM
    clamp = lambda n: jnp.minimum(n, nb - 1)
    prev = lambda n: jnp.maximum(clamp(n) - 1, 0)

    def body(qlo_ref, qhi_ref, kc_ref, kp_ref, vc_ref, vp_ref, sink_ref, dolo_ref, dohi_ref,
             dq_ref, dkv_ref, dsink_ref, dbias_ref, carry):
        n = pl.program_id(0)

        @pl.when(n == 0)
        def _():
            carry[...] = jnp.zeros_like(carry)
            dsink_ref[...] = jnp.zeros_like(dsink_ref)
            dbias_ref[...] = jnp.zeros_like(dbias_ref)

        @pl.when(n < nb)
        def _():
            heads, kvs = range(ATTN_N_Q), range(ATTN_N_KV)
            q, kb, vb = _attn_block_views(qlo_ref, qhi_ref, kc_ref, kp_ref, vc_ref, vp_ref)
            do = _attn_head_views(dolo_ref, dohi_ref)
            sink = [sink_ref[h:h + 1, 0:1] for h in heads]
            s = _attn_scores(q, kb, _attn_valid(n == 0))
            dp = [_dot_nt(do[h], vb[h // ATTN_REP]) for h in heads]
            e, es, inv = _attn_softmax(s, sink)
            p = [e[h] * inv[h] for h in heads]
            delta = [jnp.sum(p[h] * dp[h], axis=1, keepdims=True) for h in heads]
            dsc = [(p[h] * (dp[h] - delta[h]) * (hd ** -0.5)).astype(BF16) for h in heads]
            pb = [p[h].astype(BF16) for h in heads]
            dq = [_dot_nn(dsc[h], kb[h // ATTN_REP]) for h in heads]
            stack = lambda per_head, kh: jnp.concatenate(per_head[kh * ATTN_REP:(kh + 1) * ATTN_REP], axis=0)
            dkb = [_dot_tn(stack(dsc, kh), stack(q, kh)) for kh in kvs]
            dvb = [_dot_tn(stack(pb, kh), stack(do, kh)) for kh in kvs]
            dsink = [jnp.broadcast_to(jnp.sum(-es[h] * inv[h] * delta[h], axis=0, keepdims=True), (1, LANES)) for h in heads]
            dq_ref[...] = jnp.concatenate([d.astype(dq_ref.dtype) for d in dq], axis=1)
            dsink_ref[...] += jnp.concatenate(dsink, axis=0)
            dbias_ref[...] += jnp.concatenate([jnp.sum(d, axis=0, keepdims=True) for d in dq + dkb + dvb], axis=1)
            dkv_ref[...] = (carry[...] + jnp.concatenate([d[0:w, :] for d in dkb + dvb], axis=1)).astype(dkv_ref.dtype)
            carry[...] = jnp.concatenate([d[w:2 * w, :] for d in dkb + dvb], axis=1)

        @pl.when(n == nb)
        def _():
            dkv_ref[...] = carry[...].astype(dkv_ref.dtype)

    qh = lambda half: pl.BlockSpec((w, ATTN_Q_HALF), lambda n: (clamp(n), half))
    kv = lambda blk, idx: pl.BlockSpec((w, ATTN_KV_W), lambda n: (idx(n), blk))
    return pl.pallas_call(
        body, grid=(nb + 1,),
        in_specs=[qh(0), qh(1), kv(ATTN_K_BLK, clamp), kv(ATTN_K_BLK, prev), kv(ATTN_V_BLK, clamp), kv(ATTN_V_BLK, prev),
                  pl.BlockSpec((ATTN_N_Q, LANES), lambda n: (0, 0)), qh(0), qh(1)],
        out_specs=[pl.BlockSpec((w, D_MODEL), lambda n: (clamp(n), 0)),
                   pl.BlockSpec((w, 2 * ATTN_KV_W), lambda n: (jnp.maximum(n - 1, 0), 0)),
                   pl.BlockSpec((ATTN_N_Q, LANES), lambda n: (0, 0)), pl.BlockSpec((1, ATTN_QKV), lambda n: (0, 0))],
        out_shape=[jax.ShapeDtypeStruct((t, D_MODEL), BF16), jax.ShapeDtypeStruct((t, 2 * ATTN_KV_W), BF16),
                   jax.ShapeDtypeStruct((ATTN_N_Q, LANES), F32), jax.ShapeDtypeStruct((1, ATTN_QKV), F32)],
        scratch_shapes=[pltpu.VMEM((w, 2 * ATTN_KV_W), F32)],
        compiler_params=_params("arbitrary"), name=name)(qkv, qkv, qkv, qkv, qkv, qkv, sinks_b, dout, dout)


def _sq_relu_epilogue(acc):
    r = jnp.maximum(acc, 0.0)
    return (r * r,)


def _sq_relu_bwd_epilogue(acc, act):
    return (acc * (2.0 * jnp.sqrt(act.astype(F32))),)


def _bias_epilogue(acc, bias):
    return (acc + bias,)


def _plain_run(stage, fn, *args, **kwargs):
    return fn(*args, **kwargs)


def _mlp_fwd(u, w_up, w_down, tag, run=_plain_run):
    act = run(f"mlp_up_{tag}", _matmul, u, w_up, mode="nn", out_dtypes=(BF16,), epilogue=_sq_relu_epilogue, b_shards=True,
              tm=BIG_TILE, name=f"mlp_up_{tag}")
    f = run(f"mlp_down_{tag}", _matmul, act, w_down, mode="nn", out_dtypes=(BF16,), tk=BIG_TILE, name=f"mlp_down_{tag}")
    return act, f


def _mlp_bwd(u, act, w_up, w_down, df, tag):
    dpre = _matmul(df, w_down, mode="nt", out_dtypes=(BF16,), epilogue=_sq_relu_bwd_epilogue,
                   extras=((act, "tile"),), name=f"mlp_dact_{tag}")
    dw_down = _matmul(act, df, mode="tn", out_dtypes=(BF16,), tk=BIG_TILE, name=f"mlp_dwdown_{tag}")
    du = _matmul(dpre, w_up, mode="nt", out_dtypes=(BF16,), b_shards=True, tm=BIG_TILE, name=f"mlp_du_{tag}")
    dw_up = _matmul(u, dpre, mode="tn", out_dtypes=(BF16,), out_shards=True, tk=BIG_TILE, name=f"mlp_dwup_{tag}")
    return du, dw_up, dw_down


def _head_param_rows(p):
    return jnp.broadcast_to(p.reshape(SSD_N_GROUPS, SSD_HPG, 1), (SSD_N_GROUPS, SSD_HPG, LANES))


def _local_step(x, target, wts, comm=None, u0=None):
    wts = dict(wts)
    row = lambda v: v.reshape(1, -1)
    mix_pre, mix_post, ffn_pre, ffn_post = wts["mix_pre_norm"], wts["mix_post_norm"], wts["ffn_pre_norm"], wts["ffn_post_norm"]

    def gathering(stage, fn, *args, **kwargs):
        hook = comm.gather_hook(stage) if comm is not None else None
        if hook is None:
            return fn(*args, **kwargs)
        out, got = fn(*args, hook=hook, **kwargs)
        wts.update(comm.weights_from(stage, got))
        return out

    if u0 is None:
        u0 = _rms_fwd(x, row(mix_pre[0]), name="rms_pre_mix0")
    zx, dt_raw = gathering("in_proj", _matmul, u0, wts["ssd_w_in"], mode="nn", out_dtypes=(BF16,), tn=SSD_IN_TILE,
                           f32_block=SSD_DT_COL - (SSD_IN_PAD - SSD_IN_TILE),
                           name="ssd_in_proj")
    xc = gathering("conv", _conv_fwd, zx, wts["ssd_conv_w"], row(wts["ssd_conv_b"]), name="ssd_conv_fwd")
    bias_row = jnp.pad(wts["ssd_dt_bias"], (0, LANES - SSD_N_HEADS)).reshape(1, LANES)
    alog_row = jnp.pad(wts["ssd_a_log"], (0, LANES - SSD_N_HEADS)).reshape(1, LANES)
    dtr, cumr = _softplus_fwd(dt_raw, bias_row, alog_row, name="ssd_dt_fwd")
    alog_b, d_b = _head_param_rows(wts["ssd_a_log"]), _head_param_rows(wts["ssd_d"])
    y_ssd, states = gathering("scan", _ssd_fwd, xc, dtr, cumr, alog_b, d_b, name="ssd_scan_fwd")
    norm_w = row(wts["ssd_norm_w"])
    yn = _gate_norm_fwd(y_ssd, zx, norm_w, name="ssd_gate_norm_fwd")
    mix0 = _matmul(yn, wts["ssd_w_out"], mode="nn", out_dtypes=(BF16,), tk=BIG_TILE, name="ssd_out_proj")
    h1, v0 = _rms_fwd(mix0, row(mix_post[0]), resid=x, want_u=row(ffn_pre[0]), name="rms_post_mix0")
    act0, f0 = _mlp_fwd(v0, wts["mlp_w_up0"], wts["mlp_w_down0"], "l0", run=gathering)
    h2, u1 = _rms_fwd(f0, row(ffn_post[0]), resid=h1, want_u=row(mix_pre[1]), name="rms_post_ffn0")

    qkv = _matmul(u1, wts["attn_w_qkv"], mode="nn", out_dtypes=(BF16,), epilogue=_bias_epilogue,
                  extras=((row(wts["attn_b_qkv"]), "row"),), b_shards=True, name="attn_qkv_proj")
    sinks_b = jnp.broadcast_to(wts["attn_sinks"].reshape(ATTN_N_Q, 1), (ATTN_N_Q, LANES))
    ao = gathering("attn_fwd", _attn_fwd, qkv, sinks_b, name="attn_fwd")
    mix1 = _matmul(ao, wts["attn_w_o"], mode="nn", out_dtypes=(BF16,), epilogue=_bias_epilogue,
                   extras=((row(wts["attn_b_o"]), "row"),), name="attn_out_proj")
    h3, v1 = _rms_fwd(mix1, row(mix_post[1]), resid=h2, want_u=row(ffn_pre[1]), name="rms_post_mix1")
    act1, f1 = _mlp_fwd(v1, wts["mlp_w_up1"], wts["mlp_w_down1"], "l1")
    dh4, loss_tile = _rms_fwd(f1, row(ffn_post[1]), resid=h3, target=target, name="rms_post_ffn1_loss")

    df1, g_ffn_post1 = _rms_bwd(f1, row(ffn_post[1]), dh4, out_dtype=BF16, name="rms_post_ffn1_bwd")
    dv1, g_up1, g_down1 = _mlp_bwd(v1, act1, wts["mlp_w_up1"], wts["mlp_w_down1"], df1, "l1")
    dh3, g_ffn_pre1 = _rms_bwd(h3, row(ffn_pre[1]), dv1, resid=dh4, name="rms_pre_ffn1_bwd")
    dmix1, g_mix_post1, g_b_o = _rms_bwd(mix1, row(mix_post[1]), dh3, out_dtype=BF16, dx_col_sum=True, name="rms_post_mix1_bwd")
    g_w_o = _matmul(ao, dmix1, mode="tn", out_dtypes=(BF16,), tk=BIG_TILE, name="attn_dwo")
    dao = _matmul(dmix1, wts["attn_w_o"], mode="nt", out_dtypes=(BF16,), name="attn_dao")
    dq, dkv, g_sinks, g_b_qkv = _attn_bwd(qkv, sinks_b, dao, name="attn_bwd")
    dqkv = jnp.concatenate([dq, dkv], axis=1)
    g_w_qkv = _matmul(u1, dqkv, mode="tn", out_dtypes=(BF16,), tn=ATTN_QKV // N_CHIPS, out_shards=True, tk=BIG_TILE, name="attn_dwqkv")
    du1 = _matmul(dqkv, wts["attn_w_qkv"], mode="nt", out_dtypes=(BF16,), b_shards=True, name="attn_du")
    dh2, g_mix_pre1 = _rms_bwd(h2, row(mix_pre[1]), du1, resid=dh3, name="rms_pre_mix1_bwd")

    df0, g_ffn_post0 = _rms_bwd(f0, row(ffn_post[0]), dh2, out_dtype=BF16, name="rms_post_ffn0_bwd")
    dv0, g_up0, g_down0 = _mlp_bwd(v0, act0, wts["mlp_w_up0"], wts["mlp_w_down0"], df0, "l0")
    dh1, g_ffn_pre0 = _rms_bwd(h1, row(ffn_pre[0]), dv0, resid=dh2, name="rms_pre_ffn0_bwd")
    dmix0, g_mix_post0 = _rms_bwd(mix0, row(mix_post[0]), dh1, out_dtype=BF16, name="rms_post_mix0_bwd")
    g_w_out = _matmul(yn, dmix0, mode="tn", out_dtypes=(BF16,), tk=BIG_TILE, name="ssd_dwout")
    dyn = _matmul(dmix0, wts["ssd_w_out"], mode="nt", out_dtypes=(BF16,), name="ssd_dyn")
    dy_ssd, dzx, g_norm_w = _gate_norm_bwd(y_ssd, zx, norm_w, dyn, name="ssd_gate_norm_bwd")
    mats = {"ssd_w_out": g_w_out, "attn_w_qkv": g_w_qkv, "attn_w_o": g_w_o,
            "mlp_w_up0": g_up0, "mlp_w_up1": g_up1, "mlp_w_down0": g_down0, "mlp_w_down1": g_down1}
    if comm is None:
        dxc, dbm, dcm, ddt_r, dpar = _ssd_bwd(xc, dtr, cumr, alog_b, d_b, states, dy_ssd, name="ssd_scan_bwd")
    else:
        (dxc, dbm, dcm, ddt_r, dpar), received = _ssd_bwd(xc, dtr, cumr, alog_b, d_b, states, dy_ssd,
                                                          name="ssd_scan_bwd", hook=comm.exchange_hook(mats, "early"))
        comm.received(received)
    dzx, g_conv_w, g_conv_b = _conv_bwd(zx, wts["ssd_conv_w"], row(wts["ssd_conv_b"]), dxc, dbm, dcm, dzx, name="ssd_conv_bwd")
    dzx, g_dt_bias = _softplus_bwd(dt_raw, bias_row, ddt_r, dzx, name="ssd_dt_bwd")
    g_w_in = _w_in_to_shards(_matmul(u0, dzx, mode="tn", out_dtypes=(BF16,), tn=SSD_IN_TILE, tk=BIG_TILE, name="ssd_dwin"), name="ssd_dwin_shards")
    mats["ssd_w_in"] = g_w_in
    if comm is None:
        du0 = _matmul(dzx, wts["ssd_w_in"], mode="nt", out_dtypes=(BF16,), tk=SSD_IN_TILE, name="ssd_du")
    else:
        du0, received = _matmul(dzx, wts["ssd_w_in"], mode="nt", out_dtypes=(BF16,), tk=SSD_IN_TILE, name="ssd_du",
                                hook=comm.exchange_hook(mats, "late"))
        comm.received(received)
    grad_x, g_mix_pre0 = _rms_bwd(x, row(mix_pre[0]), du0, resid=dh1, name="rms_pre_mix0_bwd")

    dpar = dpar.reshape(SSD_N_HEADS, LANES)
    vecs = {
        "ssd_conv_w": g_conv_w, "ssd_conv_b": g_conv_b.reshape(-1),
        "ssd_dt_bias": g_dt_bias[0, :SSD_N_HEADS], "ssd_a_log": dpar[:, 0], "ssd_d": dpar[:, 1],
        "ssd_norm_w": g_norm_w.reshape(-1), "attn_b_qkv": g_b_qkv.reshape(-1), "attn_sinks": g_sinks[:, 0],
        "attn_b_o": g_b_o.reshape(-1),
        "mix_pre_norm": jnp.concatenate([g_mix_pre0, g_mix_pre1]), "mix_post_norm": jnp.concatenate([g_mix_post0, g_mix_post1]),
        "ffn_pre_norm": jnp.concatenate([g_ffn_pre0, g_ffn_pre1]), "ffn_post_norm": jnp.concatenate([g_ffn_post0, g_ffn_post1]),
    }
    return loss_tile, grad_x, mats, vecs


def _mesh_position():
    return lax.axis_index("x"), lax.axis_index("y"), lax.axis_index("c")


def _flip(v, bit):
    return 1 - v if bit else v


OTHER_CHIPS = ((1, 0), (0, 1), (1, 1))


def _comm_params():
    return pltpu.CompilerParams(vmem_limit_bytes=VMEM_LIMIT)


def _staged_copies(srcs, dsts, bufs, sems_in, sems_out):
    loads = [pltpu.make_async_copy(s, b, sems_in.at[i]) for i, (s, b) in enumerate(zip(srcs, bufs))]
    stores = [pltpu.make_async_copy(b, d, sems_out.at[i]) for i, (b, d) in enumerate(zip(bufs, dsts))]
    return loads, stores


class _GatherHook:
    def __init__(self, mats, vecs=()):
        self.arrs = list(mats) + list(vecs)
        self.nm, self.n = len(mats), len(self.arrs)
        n_ici, n_fwd = (N_CHIPS - 1) * self.n, max((N_CHIPS - 1) * self.nm, 1)
        dma = pltpu.SemaphoreType.DMA
        self.out_shape = [jax.ShapeDtypeStruct((N_CHIPS,) + a.shape, a.dtype) for a in self.arrs]
        self.scratch = [pltpu.VMEM(a.shape, a.dtype) for a in self.arrs] + [
            dma((n_ici,)), dma((n_ici,)), dma((n_fwd,)), dma((n_fwd,)), dma((self.n,)), dma((self.n,))]

    def plan(self, ins, outs, scratch):
        n, nm = self.n, self.nm
        bufs = scratch[:n]
        ici_send, ici_recv, fwd_send, fwd_recv, load_sems, store_sems = scratch[n:]
        xi, yi, ci = _mesh_position()
        me = 2 * xi + yi
        loads, stores = _staged_copies(ins, [outs[i].at[me] for i in range(n)], bufs, load_sems, store_sems)
        sends, landed, forwards, from_sibling = [], [], [], []
        for j, (bx, by) in enumerate(OTHER_CHIPS):
            px, py = _flip(xi, bx), _flip(yi, by)
            peer = 2 * px + py
            for i in range(n):
                k = j * n + i
                mk = functools.partial(pltpu.make_async_remote_copy, send_sem=ici_send.at[k], recv_sem=ici_recv.at[k],
                                       device_id=(px, py, ci), device_id_type=MESH)
                if i < nm:
                    sends.append(mk(src_ref=ins[i].at[ci], dst_ref=outs[i].at[me, ci]))
                    landed.append(mk(src_ref=ins[i].at[ci], dst_ref=outs[i].at[peer, ci]))
                    kf = j * nm + i
                    fw = functools.partial(pltpu.make_async_remote_copy, send_sem=fwd_send.at[kf], recv_sem=fwd_recv.at[kf],
                                           device_id=(xi, yi, 1 - ci), device_id_type=MESH)
                    forwards.append(fw(src_ref=outs[i].at[peer, ci], dst_ref=outs[i].at[peer, ci]))
                    from_sibling.append(fw(src_ref=outs[i].at[peer, ci], dst_ref=outs[i].at[peer, 1 - ci]))
                else:
                    sends.append(mk(src_ref=ins[i], dst_ref=outs[i].at[me]))
                    landed.append(mk(src_ref=ins[i], dst_ref=outs[i].at[peer]))
                    forwards.append(None)
        return loads, stores, sends, landed, forwards, from_sibling

    @staticmethod
    def start(p):
        loads, _, sends, _, _, _ = p
        for cp in loads + sends:
            cp.start()

    @staticmethod
    def relay(p):
        loads, stores, _, landed, forwards, _ = p
        for ld, st in zip(loads, stores):
            ld.wait()
            st.start()
        for cp, fw in zip(landed, forwards):
            cp.wait_recv()
            if fw is not None:
                fw.start()

    @staticmethod
    def finish(p):
        _, stores, sends, _, forwards, from_sibling = p
        for cp in from_sibling:
            cp.wait_recv()
        for cp in sends + [fw for fw in forwards if fw is not None]:
            cp.wait_send()
        for st in stores:
            st.wait()


def _run_hook(hook, ins, outs, scratch, step, n_steps):
    p = hook.plan(ins, outs, scratch)
    relay_step = min(max(1, (3 * n_steps) // 4), n_steps - 1)

    @pl.when(step == 0)
    def _():
        hook.start(p)

    if relay_step < n_steps - 1:
        @pl.when(step == relay_step)
        def _():
            hook.relay(p)

    @pl.when(step == n_steps - 1)
    def _():
        if relay_step == n_steps - 1:
            hook.relay(p)
        hook.finish(p)


def _hook_call(hook, *, name):
    n = len(hook.arrs)

    def body(*refs):
        p = hook.plan(refs[:n], refs[n:n + len(hook.out_shape)], refs[n + len(hook.out_shape):])
        hook.start(p)
        hook.relay(p)
        hook.finish(p)

    return pl.pallas_call(
        body, in_specs=[ANY] * n, out_specs=[ANY] * len(hook.out_shape), out_shape=hook.out_shape,
        scratch_shapes=hook.scratch, compiler_params=_comm_params(), name=name)(*hook.arrs)


def _send_other_half(parts, *, name):
    n = len(parts)

    def body(*refs):
        ins, outs = refs[:n], refs[n:2 * n]
        send_sems, recv_sems = refs[2 * n:]
        xi, yi, ci = _mesh_position()
        sibling = (xi, yi, 1 - ci)
        for i in range(n):
            for s in range(N_CHIPS):
                pltpu.make_async_remote_copy(src_ref=ins[i].at[s, 1 - ci], dst_ref=outs[i].at[s], send_sem=send_sems.at[i],
                                             recv_sem=recv_sems.at[i], device_id=sibling, device_id_type=MESH).start()
        for i in range(n):
            pltpu.make_async_remote_copy(src_ref=outs[i], dst_ref=outs[i], send_sem=send_sems.at[i], recv_sem=recv_sems.at[i],
                                         device_id=sibling, device_id_type=MESH).wait()

    return pl.pallas_call(
        body, in_specs=[ANY] * n, out_specs=[ANY] * n,
        out_shape=[jax.ShapeDtypeStruct((p.shape[0],) + p.shape[2:], p.dtype) for p in parts],
        scratch_shapes=[pltpu.SemaphoreType.DMA((n,)), pltpu.SemaphoreType.DMA((n,))],
        name=name)(*parts)


ROW_BLOCKS = 8
SUM_ROW_BLOCKS = 2


def _add_sibling_half(parts, theirs, core, *, name):
    n = len(parts)

    def body(core_ref, *refs):
        for a_ref, b_ref, o_ref in zip(refs[:n], refs[n:2 * n], refs[2 * n:]):
            o_ref[...] = (a_ref[...].astype(F32) + b_ref[...].astype(F32)).astype(o_ref.dtype)

    nb = SUM_ROW_BLOCKS
    mine = lambda p: pl.BlockSpec((None, None, p.shape[2] // nb, p.shape[3]), lambda s, rb, core_ref: (s, core_ref[0], rb, 0))
    other = lambda p: pl.BlockSpec((None, p.shape[1] // nb, p.shape[2]), lambda s, rb, core_ref: (s, rb, 0))
    return pl.pallas_call(
        body,
        grid_spec=pltpu.PrefetchScalarGridSpec(
            num_scalar_prefetch=1, grid=(N_CHIPS, nb),
            in_specs=[mine(p) for p in parts] + [other(q) for q in theirs], out_specs=[other(q) for q in theirs]),
        out_shape=[jax.ShapeDtypeStruct(q.shape, BF16) for q in theirs],
        compiler_params=_params("parallel", "parallel"), name=name)(core, *parts, *theirs)


class _ExchangeHook:
    def __init__(self, parts, to_all=()):
        self.arrs = list(parts) + list(to_all)
        self.n_parts, self.n = len(parts), len(self.arrs)
        n_ici, n_peer = max((N_CHIPS - 1) * self.n_parts, 1), (N_DEV - 1) * max(len(to_all), 1)
        dma = pltpu.SemaphoreType.DMA
        self.out_shape = [jax.ShapeDtypeStruct(p.shape, p.dtype) for p in parts] + [
            jax.ShapeDtypeStruct((N_DEV,) + a.shape, a.dtype) for a in to_all]
        self.scratch = [pltpu.VMEM(p.shape[1:], p.dtype) for p in parts] + [pltpu.VMEM(a.shape, a.dtype) for a in to_all] + [
            dma((n_ici,)), dma((n_ici,)), dma((n_peer,)), dma((n_peer,)), dma((self.n,)), dma((self.n,))]

    def plan(self, ins, outs, scratch):
        n, npt = self.n, self.n_parts
        bufs = scratch[:n]
        send_sems, recv_sems, all_send, all_recv, load_sems, store_sems = scratch[n:]
        xi, yi, ci = _mesh_position()
        me_chip = 2 * xi + yi
        me = 4 * xi + 2 * yi + ci
        loads, stores = _staged_copies([ins[i].at[me_chip] for i in range(npt)] + list(ins[npt:]),
                                       [outs[i].at[me_chip] for i in range(npt)] + [outs[i].at[me] for i in range(npt, n)],
                                       bufs, load_sems, store_sems)
        sends, recvs = [], []
        for j, (bx, by) in enumerate(OTHER_CHIPS):
            px, py = _flip(xi, bx), _flip(yi, by)
            peer = 2 * px + py
            for i in range(npt):
                k = j * npt + i
                mk = functools.partial(pltpu.make_async_remote_copy, src_ref=ins[i].at[peer], send_sem=send_sems.at[k],
                                       recv_sem=recv_sems.at[k], device_id=(px, py, ci), device_id_type=MESH)
                sends.append(mk(dst_ref=outs[i].at[me_chip]))
                recvs.append(mk(dst_ref=outs[i].at[peer]))
        for i in range(npt, n):
            for k in range(1, N_DEV):
                px, py, pc = _flip(xi, (k >> 2) & 1), _flip(yi, (k >> 1) & 1), _flip(ci, k & 1)
                slot = (i - npt) * (N_DEV - 1) + k - 1
                mk = functools.partial(pltpu.make_async_remote_copy, src_ref=ins[i], send_sem=all_send.at[slot],
                                       recv_sem=all_recv.at[slot], device_id=(px, py, pc), device_id_type=MESH)
                sends.append(mk(dst_ref=outs[i].at[me]))
                recvs.append(mk(dst_ref=outs[i].at[4 * px + 2 * py + pc]))
        return loads, stores, sends, recvs

    @staticmethod
    def start(p):
        loads, _, sends, _ = p
        for cp in loads + sends:
            cp.start()

    @staticmethod
    def relay(p):
        loads, stores, _, _ = p
        for ld, st in zip(loads, stores):
            ld.wait()
            st.start()

    @staticmethod
    def finish(p):
        _, stores, sends, recvs = p
        for cp in recvs:
            cp.wait_recv()
        for cp in sends:
            cp.wait_send()
        for st in stores:
            st.wait()


def _sum_chips(parts, *, name):
    n = len(parts)
    p = parts[0].shape[0]

    def body(*refs):
        s = pl.program_id(1)
        for x_ref, o_ref in zip(refs[:n], refs[n:]):
            @pl.when(s == 0)
            def _():
                o_ref[...] = x_ref[...].astype(F32)

            @pl.when(s > 0)
            def _():
                o_ref[...] += x_ref[...].astype(F32)

    blocks = lambda q: SUM_ROW_BLOCKS if q.shape[1] % (16 * SUM_ROW_BLOCKS) == 0 else 1
    assert len({blocks(q) for q in parts}) == 1
    nb = blocks(parts[0])
    return pl.pallas_call(
        body, grid=(nb, p),
        in_specs=[pl.BlockSpec((None, q.shape[1] // nb, q.shape[2]), lambda rb, s: (s, rb, 0)) for q in parts],
        out_specs=[pl.BlockSpec((q.shape[1] // nb, q.shape[2]), lambda rb, s: (rb, 0)) for q in parts],
        out_shape=[jax.ShapeDtypeStruct(q.shape[1:], F32) for q in parts],
        compiler_params=_params("parallel", "arbitrary"), name=name)(*parts)


def _swap_halves(halves, layers, *, name, hook=None):
    n = len(halves)
    out_shapes, slots = [], []
    for i, h in enumerate(halves):
        pair = [p for p in layers if i in p]
        if pair and pair[0][1] == i:
            slots.append((slots[pair[0][0]][0], 1))
        elif pair:
            out_shapes.append(jax.ShapeDtypeStruct((2, 2) + h.shape, h.dtype))
            slots.append((len(out_shapes) - 1, 0))
        else:
            out_shapes.append(jax.ShapeDtypeStruct((2,) + h.shape, h.dtype))
            slots.append((len(out_shapes) - 1, None))
    n_out = len(out_shapes)
    hk = _HookSlots(hook, n_in=n, n_out=n_out, n_scratch=n + 4)

    def body(*refs):
        ins, outs, scratch = hk.own(refs)
        bufs = scratch[:n]
        send_sems, recv_sems, load_sems, store_sems = scratch[n:]
        if hook is not None:
            extra = hk.plan(refs)
            hook.start(extra)
        xi, yi, ci = _mesh_position()
        own, sends, recvs = [], [], []
        for i in range(n):
            o, layer = slots[i]
            dst = (lambda core: outs[o].at[core]) if layer is None else (lambda core: outs[o].at[layer, core])
            own.append(dst(ci))
            mk = functools.partial(pltpu.make_async_remote_copy, src_ref=ins[i], send_sem=send_sems.at[i],
                                   recv_sem=recv_sems.at[i], device_id=(xi, yi, 1 - ci), device_id_type=MESH)
            sends.append(mk(dst_ref=dst(ci)))
            recvs.append(mk(dst_ref=dst(1 - ci)))
        loads, stores = _staged_copies(ins, own, bufs, load_sems, store_sems)
        for cp in loads + sends:
            cp.start()
        for ld, st in zip(loads, stores):
            ld.wait()
            st.start()
        for cp in recvs:
            cp.wait_recv()
        for cp in sends:
            cp.wait_send()
        for st in stores:
            st.wait()
        if hook is not None:
            hook.relay(extra)
            hook.finish(extra)

    outs = pl.pallas_call(
        body, in_specs=[ANY] * n + hk.in_specs, out_specs=[ANY] * n_out + hk.out_specs, out_shape=out_shapes + hk.out_shape,
        scratch_shapes=[pltpu.VMEM(h.shape, h.dtype) for h in halves]
        + [pltpu.SemaphoreType.DMA((n,)), pltpu.SemaphoreType.DMA((n,)), pltpu.SemaphoreType.DMA((n,)), pltpu.SemaphoreType.DMA((n,))]
        + hk.scratch,
        compiler_params=_comm_params(), name=name)(*halves, *hk.inputs)
    return outs if hook is None else (outs[:n_out], outs[n_out:])


def _cast_bf16(layers, x, norm_w, *, name, hook=None):
    n = len(layers)
    hk = _HookSlots(hook, n_in=n + 2, n_out=n + 1, n_scratch=0)

    def body(*refs):
        ins, outs, _ = hk.own(refs)
        if hook is not None:
            hk.run(refs, pl.program_id(0), ROW_BLOCKS)
        for i_ref, o_ref in zip(ins[:n], outs[:n]):
            o_ref[...] = i_ref[...].astype(o_ref.dtype)
        xv = ins[n][...]
        outs[n][...] = (xv * lax.rsqrt(jnp.mean(xv * xv, axis=-1, keepdims=True) + NORM_EPS) * ins[n + 1][...]).astype(BF16)

    in_blk = lambda a, l: pl.BlockSpec((None, a.shape[1] // ROW_BLOCKS, a.shape[2]), lambda i: (l, i, 0))
    out_blk = lambda a: pl.BlockSpec((a.shape[1] // ROW_BLOCKS, a.shape[2]), lambda i: (i, 0))
    x_blk = pl.BlockSpec((x.shape[0] // ROW_BLOCKS, x.shape[1]), lambda i: (i, 0))
    outs = pl.pallas_call(
        body, grid=(ROW_BLOCKS,),
        in_specs=[in_blk(a, l) for a, l in layers] + [x_blk, pl.BlockSpec((1, x.shape[1]), lambda i: (0, 0))] + hk.in_specs,
        out_specs=[out_blk(a) for a, _ in layers] + [x_blk] + hk.out_specs,
        out_shape=[jax.ShapeDtypeStruct(a.shape[1:], BF16) for a, _ in layers] + [jax.ShapeDtypeStruct(x.shape, BF16)] + hk.out_shape,
        scratch_shapes=hk.scratch,
        compiler_params=_params(*hk.semantics("parallel")), name=name)(*[a for a, _ in layers], x, norm_w, *hk.inputs)
    own = (outs[:n], outs[n])
    return own if hook is None else (own, outs[n + 1:])


def _full_weight(name, gathered):
    s, _, r, c = gathered.shape
    if name == "ssd_w_in":
        return _w_in_from_shards(gathered.reshape(s, 2 * r, c), name="ssd_w_in_unshard")
    if name in ("attn_w_qkv", "mlp_w_up0", "mlp_w_up1"):
        return gathered.reshape(s, 2 * r, c)
    return gathered.reshape(s * 2 * r, c)


class _StepComm:
    GATHER = {"in_proj": ("mlp_w_up0", "attn_w_o"), "conv": ("mlp_w_down0",), "scan": ("ssd_w_out", "mlp_w_up1"),
              "mlp_up_l0": ("attn_w_qkv",), "attn_fwd": ("mlp_w_down1",)}
    EXCHANGE = {"early": ("ssd_w_out", "attn_w_qkv", "attn_w_o", "mlp_w_up0", "mlp_w_up1", "mlp_w_down0", "mlp_w_down1"),
                "late": ("ssd_w_in",)}

    def __init__(self, shards, core):
        self.shards, self.core = shards, core
        self.chip_parts = {}
        self._pending = None

    def gather_hook(self, stage):
        names = self.GATHER.get(stage)
        return _GatherHook([self.shards[n] for n in names]) if names else None

    def weights_from(self, stage, gathered):
        return {n: _full_weight(n, g) for n, g in zip(self.GATHER[stage], gathered)}

    def chip_sums(self, mats, tag):
        parts = [_shard_halves(a) for a in mats.values()]
        theirs = _send_other_half(parts, name=f"grad_sibling_send_{tag}")
        return _add_sibling_half(parts, theirs, self.core, name=f"grad_chip_sum_{tag}")

    def exchange_hook(self, mats, which):
        self._pending = self.EXCHANGE[which]
        return _ExchangeHook(self.chip_sums({n: mats[n] for n in self._pending}, which))

    def received(self, arrays):
        self.chip_parts.update(zip(self._pending, arrays))


ADAMW_ROW_BLOCKS = 16


def _adamw(ws, gs, ms, vs, *, name, by_lanes=False):
    n = len(ws)
    if by_lanes:
        nb = min(a.shape[2] for a in ws) // LANES
    else:
        nb = ADAMW_ROW_BLOCKS if all(a.shape[1] % (8 * ADAMW_ROW_BLOCKS) == 0 for a in ws) else 1

    def body(*refs):
        ins, outs = refs[:4 * n], refs[4 * n:]
        for i in range(n):
            w_ref, g_ref, m_ref, v_ref = ins[i], ins[n + i], ins[2 * n + i], ins[3 * n + i]
            go_ref, d_ref, nm_ref, nv_ref = outs[i], outs[n + i], outs[2 * n + i], outs[3 * n + i]
            gv = g_ref[...]
            nm = ADAM_B1 * m_ref[...] + (1.0 - ADAM_B1) * gv
            nv = ADAM_B2 * v_ref[...] + (1.0 - ADAM_B2) * (gv * gv)
            m_hat = nm / (1.0 - ADAM_B1 ** ADAM_STEP)
            v_hat = nv / (1.0 - ADAM_B2 ** ADAM_STEP)
            go_ref[...] = gv
            d_ref[...] = -ADAM_LR * (m_hat / (jnp.sqrt(v_hat) + ADAM_EPS) + ADAM_WD * w_ref[...])
            nm_ref[...] = nm
            nv_ref[...] = nv

    if by_lanes:
        blks = [pl.BlockSpec((a.shape[0], a.shape[1], a.shape[2] // nb), lambda i: (0, 0, i)) for a in ws]
    else:
        blks = [pl.BlockSpec((a.shape[0], a.shape[1] // nb, a.shape[2]), lambda i: (0, i, 0)) for a in ws]
    shapes = [jax.ShapeDtypeStruct(a.shape, F32) for a in ws]
    outs = pl.pallas_call(body, grid=(nb,), in_specs=blks * 4, out_specs=blks * 4, out_shape=shapes * 4,
                          compiler_params=_params("parallel"), name=name)(*ws, *gs, *ms, *vs)
    return [tuple(outs[k * n + i] for k in range(4)) for i in range(n)]


SM_CONV_B, SM_NORM_W, SM_MIX_PRE, SM_MIX_POST, SM_FFN_PRE, SM_FFN_POST, SM_MISC, SM_CONV_W, SM_B_QKV, SM_B_O = 0, 4, 6, 8, 10, 12, 14, 16, 32, 34
SM_ROWS = 40
MISC_DT_BIAS, MISC_A_LOG, MISC_D, MISC_SINKS, MISC_LOSS = 0, 32, 64, 96, 112


def _shard_halves(a):
    c = a.shape[-1]
    return a.reshape(N_CHIPS, 2, -1, c)


def _rows(v):
    return v.reshape(-1, D_MODEL)


def _misc_row(dt_bias, a_log, d, sinks, loss):
    pad = jnp.zeros((D_MODEL - MISC_LOSS - 1,), F32)
    return jnp.concatenate([dt_bias.reshape(-1), a_log.reshape(-1), d.reshape(-1), sinks.reshape(-1), loss.reshape(1), pad]).reshape(1, D_MODEL)


def _replicated_rows(p, loss):
    return jnp.concatenate([
        _rows(p["ssd_conv_b"]), _rows(p["ssd_norm_w"]), _rows(p["mix_pre_norm"]), _rows(p["mix_post_norm"]),
        _rows(p["ffn_pre_norm"]), _rows(p["ffn_post_norm"]),
        _misc_row(p["ssd_dt_bias"], p["ssd_a_log"], p["ssd_d"], p["attn_sinks"], loss), jnp.zeros((1, D_MODEL), F32)], axis=0)


def _sharded_rows(conv_w, b_qkv, b_o):
    last = jnp.concatenate([b_qkv.reshape(-1), b_o.reshape(-1), jnp.zeros((D_MODEL - 640,), F32)]).reshape(1, D_MODEL)
    return jnp.concatenate([conv_w.reshape(SSD_CONV_WIDTH, D_MODEL), last, jnp.zeros((3, D_MODEL), F32)], axis=0)


REPLICATED = ("ssd_conv_b", "ssd_dt_bias", "ssd_a_log", "ssd_d", "ssd_norm_w", "attn_sinks",
              "mix_pre_norm", "mix_post_norm", "ffn_pre_norm", "ffn_post_norm")
MATRICES = ("ssd_w_in", "ssd_w_out", "attn_w_qkv", "attn_w_o", "mlp_w_up", "mlp_w_down")
WEIGHT_NAMES = ("ssd_w_in", "ssd_conv_w", "ssd_conv_b", "ssd_dt_bias", "ssd_a_log", "ssd_d", "ssd_norm_w", "ssd_w_out",
                "attn_w_qkv", "attn_b_qkv", "attn_sinks", "attn_w_o", "attn_b_o", "mlp_w_up", "mlp_w_down",
                "mix_pre_norm", "mix_post_norm", "ffn_pre_norm", "ffn_post_norm")


def _unpack_small(rows16, rows8, like):
    misc = rows16[SM_MISC]
    out = {
        "ssd_conv_b": rows16[SM_CONV_B:SM_CONV_B + 4], "ssd_norm_w": rows16[SM_NORM_W:SM_NORM_W + 2],
        "mix_pre_norm": rows16[SM_MIX_PRE:SM_MIX_PRE + 2], "mix_post_norm": rows16[SM_MIX_POST:SM_MIX_POST + 2],
        "ffn_pre_norm": rows16[SM_FFN_PRE:SM_FFN_PRE + 2], "ffn_post_norm": rows16[SM_FFN_POST:SM_FFN_POST + 2],
        "ssd_dt_bias": misc[MISC_DT_BIAS:MISC_DT_BIAS + 32], "ssd_a_log": misc[MISC_A_LOG:MISC_A_LOG + 32],
        "ssd_d": misc[MISC_D:MISC_D + 32], "attn_sinks": misc[MISC_SINKS:MISC_SINKS + 16],
        "ssd_conv_w": rows8[0:SSD_CONV_WIDTH], "attn_b_qkv": rows8[SSD_CONV_WIDTH, 0:384], "attn_b_o": rows8[SSD_CONV_WIDTH, 384:640],
    }
    return {k: v.reshape(like[k].shape) for k, v in out.items()}


def kernel(x, ssd_w_in, ssd_conv_w, ssd_conv_b, ssd_dt_bias, ssd_a_log, ssd_d, ssd_norm_w, ssd_w_out, attn_w_qkv, attn_b_qkv, attn_sinks, attn_w_o, attn_b_o, mlp_w_up, mlp_w_down, mix_pre_norm, mix_post_norm, ffn_pre_norm, ffn_post_norm, loss_target, m_ssd_w_in, m_ssd_conv_w, m_ssd_conv_b, m_ssd_dt_bias, m_ssd_a_log, m_ssd_d, m_ssd_norm_w, m_ssd_w_out, m_attn_w_qkv, m_attn_b_qkv, m_attn_sinks, m_attn_w_o, m_attn_b_o, m_mlp_w_up, m_mlp_w_down, m_mix_pre_norm, m_mix_post_norm, m_ffn_pre_norm, m_ffn_post_norm, v_ssd_w_in, v_ssd_conv_w, v_ssd_conv_b, v_ssd_dt_bias, v_ssd_a_log, v_ssd_d, v_ssd_norm_w, v_ssd_w_out, v_attn_w_qkv, v_attn_b_qkv, v_attn_sinks, v_attn_w_o, v_attn_b_o, v_mlp_w_up, v_mlp_w_down, v_mix_pre_norm, v_mix_post_norm, v_ffn_pre_norm, v_ffn_post_norm):
    w = dict(zip(WEIGHT_NAMES, (ssd_w_in, ssd_conv_w, ssd_conv_b, ssd_dt_bias, ssd_a_log, ssd_d, ssd_norm_w, ssd_w_out, attn_w_qkv, attn_b_qkv, attn_sinks, attn_w_o, attn_b_o, mlp_w_up, mlp_w_down, mix_pre_norm, mix_post_norm, ffn_pre_norm, ffn_post_norm)))
    m = dict(zip(WEIGHT_NAMES, (m_ssd_w_in, m_ssd_conv_w, m_ssd_conv_b, m_ssd_dt_bias, m_ssd_a_log, m_ssd_d, m_ssd_norm_w, m_ssd_w_out, m_attn_w_qkv, m_attn_b_qkv, m_attn_sinks, m_attn_w_o, m_attn_b_o, m_mlp_w_up, m_mlp_w_down, m_mix_pre_norm, m_mix_post_norm, m_ffn_pre_norm, m_ffn_post_norm)))
    v = dict(zip(WEIGHT_NAMES, (v_ssd_w_in, v_ssd_conv_w, v_ssd_conv_b, v_ssd_dt_bias, v_ssd_a_log, v_ssd_d, v_ssd_norm_w, v_ssd_w_out, v_attn_w_qkv, v_attn_b_qkv, v_attn_sinks, v_attn_w_o, v_attn_b_o, v_mlp_w_up, v_mlp_w_down, v_mix_pre_norm, v_mix_post_norm, v_ffn_pre_norm, v_ffn_post_norm)))
    chip = 2 * lax.axis_index("x") + lax.axis_index("y")

    two_halves = lambda a: a.reshape(2, a.shape[-2] // 2, a.shape[-1])
    later = {"ssd_w_out": (w["ssd_w_out"], 0), "attn_w_qkv": (w["attn_w_qkv"], 0), "attn_w_o": (w["attn_w_o"], 0),
             "mlp_w_up0": (w["mlp_w_up"], 0), "mlp_w_up1": (w["mlp_w_up"], 1),
             "mlp_w_down0": (w["mlp_w_down"], 0), "mlp_w_down1": (w["mlp_w_down"], 1)}
    first = _GatherHook([two_halves(w["ssd_w_in"].astype(BF16))], [w["ssd_conv_w"][0], w["attn_b_qkv"], w["attn_b_o"]])
    (cast, u0), (g_in, g_conv, g_bqkv, g_bo) = _cast_bf16(list(later.values()), x[0], w["mix_pre_norm"][0:1],
                                                          name="weights_to_bf16", hook=first)
    core = lax.axis_index("c").astype(jnp.int32).reshape(1)
    comm = _StepComm({k: two_halves(a) for k, a in zip(later, cast)}, core)
    full = {
        "ssd_w_in": _full_weight("ssd_w_in", g_in),
        "ssd_conv_w": g_conv.transpose(1, 0, 2).reshape(SSD_CONV_WIDTH, SSD_CONV_DIM),
        "attn_b_qkv": g_bqkv.reshape(ATTN_QKV), "attn_b_o": g_bo.reshape(D_MODEL),
    }
    for name in REPLICATED:
        full[name] = w[name][0] if name.startswith(("ssd_", "attn_")) else w[name]

    loss_tile, grad_x, gm, g = _local_step(x[0], loss_target[0], full, comm, u0)

    conv_w_rows = g["ssd_conv_w"].reshape(SSD_CONV_WIDTH * N_CHIPS, D_MODEL)
    b_qkv_rows = jnp.pad(g["attn_b_qkv"], (0, 2 * D_MODEL - ATTN_QKV)).reshape(2, D_MODEL)
    small = jnp.concatenate([_replicated_rows(g, loss_tile[0, 0]), conv_w_rows, b_qkv_rows, _rows(g["attn_b_o"]),
                             jnp.zeros((SM_ROWS - SM_B_O - 1, D_MODEL), F32)], axis=0)
    order = ("ssd_w_in", "ssd_w_out", "attn_w_qkv", "attn_w_o", "mlp_w_up0", "mlp_w_up1", "mlp_w_down0", "mlp_w_down1")
    halves = _sum_chips([comm.chip_parts[k] for k in order], name="grad_sum")
    (r_in, r_out, r_qkv, r_o, r_up, r_down), (small_all,) = _swap_halves(
        halves, layers=((4, 5), (6, 7)), hook=_ExchangeHook([], [small]), name="grad_halves_swap")
    small_sum, = _sum_chips([small_all], name="small_grad_sum")

    grads = {"ssd_w_in": r_in, "ssd_w_out": r_out, "attn_w_qkv": r_qkv, "attn_w_o": r_o, "mlp_w_up": r_up, "mlp_w_down": r_down}
    grads = {k: a.reshape(w[k].shape) for k, a in grads.items()}
    conv_w_g = lax.dynamic_index_in_dim(small_sum[SM_CONV_W:SM_CONV_W + 16].reshape(SSD_CONV_WIDTH, N_CHIPS, D_MODEL), chip, axis=1, keepdims=False)
    b_qkv_g = lax.dynamic_slice_in_dim(small_sum[SM_B_QKV:SM_B_QKV + 2].reshape(-1), chip * 384, 384)
    b_o_g = lax.dynamic_slice_in_dim(small_sum[SM_B_O], chip * 256, 256)
    small_g = jnp.concatenate([small_sum[0:16], _sharded_rows(conv_w_g, b_qkv_g, b_o_g)], axis=0)
    grads.update(_unpack_small(small_g[0:16], small_g[16:24], w))
    loss = small_sum[SM_MISC, MISC_LOSS]

    delta, new_m, new_v = {}, {}, {}
    stored = lambda a: jnp.swapaxes(a, 1, 2)
    rest = [name for name in MATRICES if name != "ssd_w_in"]
    mats = lambda p: [p[name] for name in rest]
    results = dict(zip(rest, _adamw(mats(w), mats(grads), mats(m), mats(v), name="adamw_matrices")))
    (w_in_result,) = _adamw([stored(w["ssd_w_in"])], [stored(grads["ssd_w_in"])], [stored(m["ssd_w_in"])],
                            [stored(v["ssd_w_in"])], by_lanes=True, name="adamw_ssd_w_in")
    results["ssd_w_in"] = tuple(stored(a) for a in w_in_result)
    for name in MATRICES:
        grads[name], delta[name], new_m[name], new_v[name] = results[name]
    zero = jnp.zeros((), F32)
    small_pack = lambda p: jnp.concatenate([_replicated_rows({k: p[k] for k in REPLICATED}, zero),
                                            _sharded_rows(p["ssd_conv_w"], p["attn_b_qkv"], p["attn_b_o"])], axis=0)[None]
    (_, d_s, m_s, v_s), = _adamw([small_pack(w)], [small_g[None]], [small_pack(m)], [small_pack(v)], name="adamw_vectors")
    d_s, m_s, v_s = d_s[0], m_s[0], v_s[0]
    delta.update(_unpack_small(d_s[0:16], d_s[16:24], w))
    new_m.update(_unpack_small(m_s[0:16], m_s[16:24], w))
    new_v.update(_unpack_small(v_s[0:16], v_s[16:24], w))

    return (loss, grad_x[None], *[grads[n] for n in WEIGHT_NAMES], *[delta[n] for n in WEIGHT_NAMES],
            *[new_m[n] for n in WEIGHT_NAMES], *[new_v[n] for n in WEIGHT_NAMES])
```

```python
import functools

import jax
import jax.numpy as jnp
from jax import lax
from jax.experimental import pallas as pl
from jax.experimental.pallas import tpu as pltpu

F32 = jnp.float32
BF16 = jnp.bfloat16

D_MODEL = 1024
SSD_D_INNER = 2048
SSD_HEAD_DIM = 64
SSD_N_HEADS = 32
SSD_N_GROUPS = 8
SSD_HPG = 4
SSD_D_STATE = 128
SSD_CONV_WIDTH = 4
SSD_CHUNK = 128
SSD_CONV_DIM = 4096
SSD_IN_DIM = 6176
SSD_IN_PAD = 6400
SSD_IN_TILE = 1280
SSD_DT_COL = 6144
SSD_GW = SSD_HPG * SSD_HEAD_DIM
ATTN_HEAD_DIM = 64
ATTN_N_Q = 16
ATTN_N_KV = 4
ATTN_REP = 4
ATTN_WINDOW = 128
ATTN_QKV = 1536
D_FF = 4096
NORM_EPS = 1e-6

ADAM_LR = 0.001
ADAM_B1 = 0.9
ADAM_B2 = 0.999
ADAM_EPS = 1e-08
ADAM_WD = 0.01
ADAM_STEP = 10

N_CHIPS = 4
N_DEV = 8
LANES = 128
VMEM_LIMIT = 48 * 1024 * 1024
BIG_TILE = 2048
MESH = pl.DeviceIdType.MESH


def _params(*sem):
    return pltpu.CompilerParams(dimension_semantics=sem, vmem_limit_bytes=VMEM_LIMIT)


def _dot(a, b, dims):
    return lax.dot_general(a, b, (dims, ((), ())), preferred_element_type=F32)


def _dot_nn(a, b):
    return _dot(a, b, ((1,), (0,)))


def _dot_nt(a, b):
    return _dot(a, b, ((1,), (1,)))


def _dot_tn(a, b):
    return _dot(a, b, ((0,), (0,)))


def _sigmoid(x):
    return 0.5 * jnp.tanh(0.5 * x) + 0.5


ANY = pl.BlockSpec(memory_space=pl.ANY)


class _HookSlots:
    def __init__(self, hook, n_in, n_out, n_scratch):
        self.hook = hook
        self.n_in, self.n_out, self.n_scratch = n_in, n_out, n_scratch
        self.inputs = list(hook.arrs) if hook else []
        self.out_shape = list(hook.out_shape) if hook else []
        self.scratch = list(hook.scratch) if hook else []
        self.in_specs = [ANY] * len(self.inputs)
        self.out_specs = [ANY] * len(self.out_shape)

    def _split(self, refs):
        a = self.n_in
        b = a + len(self.inputs)
        c = b + self.n_out
        d = c + len(self.out_shape)
        e = d + self.n_scratch
        return refs[:a], refs[a:b], refs[b:c], refs[c:d], refs[d:e], refs[e:]

    def own(self, refs):
        ins, _, outs, _, scratch, _ = self._split(refs)
        return ins, outs, scratch

    def plan(self, refs):
        _, h_in, _, h_out, _, h_scratch = self._split(refs)
        return self.hook.plan(h_in, h_out, h_scratch)

    def run(self, refs, step, n_steps):
        _, h_in, _, h_out, _, h_scratch = self._split(refs)
        _run_hook(self.hook, h_in, h_out, h_scratch, step, n_steps)

    def semantics(self, *sem):
        return sem if self.hook is None else ("arbitrary",) * len(sem)


def _matmul(a, b, *, mode, out_dtypes, name, epilogue=None, extras=(), tm=1024, tn=1024, tk=1024,
            b_shards=False, out_shards=False, hook=None, f32_block=None):
    f32_tail = f32_block is not None
    if b_shards:
        s, b_rows, b_cols = b.shape
        b2 = (b_rows, s * b_cols)
        if mode == "nn":
            tn = b_cols
        else:
            assert mode == "nt"
            tk = b_cols
    else:
        b2 = b.shape
    if mode == "nn":
        (m, k), (k2, n) = a.shape, b2
    elif mode == "nt":
        (m, k), (n, k2) = a.shape, b2
    else:
        (k, m), (k2, n) = a.shape, b2
    assert k == k2, (a.shape, b.shape, mode)
    tm, tn, tk = min(tm, m), min(tn, n), min(tk, k)
    assert m % tm == 0 and n % tn == 0 and k % tk == 0, (m, n, k, tm, tn, tk)
    nk = k // tk
    if mode == "tn":
        a_spec = pl.BlockSpec((tk, tm), lambda i, j, kk: (kk, i))
    else:
        a_spec = pl.BlockSpec((tm, tk), lambda i, j, kk: (i, kk))
    if b_shards and mode == "nn":
        b_spec = pl.BlockSpec((None, tk, tn), lambda i, j, kk: (j, kk, 0))
    elif b_shards:
        b_spec = pl.BlockSpec((None, tn, tk), lambda i, j, kk: (kk, j, 0))
    elif mode == "nt":
        b_spec = pl.BlockSpec((tn, tk), lambda i, j, kk: (j, kk))
    else:
        b_spec = pl.BlockSpec((tk, tn), lambda i, j, kk: (kk, j))
    dims = {"nn": ((1,), (0,)), "nt": ((1,), (1,)), "tn": ((0,), (0,))}[mode]
    ex_specs = []
    for arr, kind in extras:
        if kind == "tile":
            ex_specs.append(pl.BlockSpec((tm, tn), lambda i, j, kk: (i, j)))
        else:
            ex_specs.append(pl.BlockSpec((1, tn), lambda i, j, kk: (0, j)))
    n_ex, n_out = len(extras), len(out_dtypes)
    if epilogue is None:
        epilogue = lambda acc: (acc,)
    hk = _HookSlots(hook, n_in=2 + n_ex, n_out=n_out + f32_tail, n_scratch=0 if nk == 1 else 1)
    grid = (m // tm, n // tn, nk)

    def body(*refs):
        (a_ref, b_ref, *ex), outs, scratch = hk.own(refs)
        if hook is not None:
            step = (pl.program_id(0) * grid[1] + pl.program_id(1)) * grid[2] + pl.program_id(2)
            hk.run(refs, step, grid[0] * grid[1] * grid[2])

        def finish(acc):
            res = epilogue(acc, *[e[...] for e in ex])
            for o, r in zip(outs, res):
                o[...] = r.astype(o.dtype)
            if f32_tail:
                outs[n_out][...] = acc[:, f32_block:f32_block + LANES]

        if nk == 1:
            finish(_dot(a_ref[...], b_ref[...], dims))
        else:
            acc_ref = scratch[0]
            kk = pl.program_id(2)

            @pl.when(kk == 0)
            def _():
                acc_ref[...] = jnp.zeros_like(acc_ref)

            acc_ref[...] += _dot(a_ref[...], b_ref[...], dims)

            @pl.when(kk == nk - 1)
            def _():
                finish(acc_ref[...])

    if out_shards:
        out_spec = pl.BlockSpec((None, tm, tn), lambda i, j, kk: (j, i, 0))
        out_dims = (n // tn, m, tn)
    else:
        out_spec = pl.BlockSpec((tm, tn), lambda i, j, kk: (i, j))
        out_dims = (m, n)
    tail_specs = [pl.BlockSpec((tm, LANES), lambda i, j, kk: (i, 0))] if f32_tail else []
    tail_shapes = [jax.ShapeDtypeStruct((m, LANES), F32)] if f32_tail else []
    outs = pl.pallas_call(
        body,
        grid=grid,
        in_specs=[a_spec, b_spec] + ex_specs + hk.in_specs,
        out_specs=[out_spec for _ in out_dtypes] + tail_specs + hk.out_specs,
        out_shape=[jax.ShapeDtypeStruct(out_dims, dt) for dt in out_dtypes] + tail_shapes + hk.out_shape,
        scratch_shapes=([] if nk == 1 else [pltpu.VMEM((tm, tn), F32)]) + hk.scratch,
        compiler_params=_params(*hk.semantics("parallel", "arbitrary" if f32_tail else "parallel", "arbitrary")),
        name=name,
    )(a, b, *[arr for arr, _ in extras], *hk.inputs)
    n_own = n_out + f32_tail
    own = outs[0] if n_own == 1 else outs[:n_own]
    return own if hook is None else (own, outs[n_own:])


def _row_tile(t, want):
    return min(t, want)


def _rms_fwd(x, w, *, name, resid=None, want_u=None, target=None):
    t, d = x.shape
    tr = _row_tile(t, 1024)

    def norm(v, wv):
        return v * lax.rsqrt(jnp.mean(v * v, axis=-1, keepdims=True) + NORM_EPS) * wv

    row = pl.BlockSpec((tr, d), lambda i: (i, 0))
    vec = pl.BlockSpec((1, d), lambda i: (0, 0))
    if target is not None:
        def body(x_ref, w_ref, r_ref, t_ref, dh_ref, loss_ref):
            err = r_ref[...] + norm(x_ref[...].astype(F32), w_ref[...]) - t_ref[...]
            dh_ref[...] = err * (1.0 / d)

            @pl.when(pl.program_id(0) == 0)
            def _():
                loss_ref[...] = jnp.zeros_like(loss_ref)

            part = jnp.sum(jnp.sum(err * err, axis=1, keepdims=True), axis=0, keepdims=True) * (0.5 / d)
            loss_ref[...] += jnp.broadcast_to(part, loss_ref.shape)

        return pl.pallas_call(
            body, grid=(t // tr,), in_specs=[row, vec, row, row],
            out_specs=[row, pl.BlockSpec((8, LANES), lambda i: (0, 0))],
            out_shape=[jax.ShapeDtypeStruct((t, d), F32), jax.ShapeDtypeStruct((8, LANES), F32)],
            compiler_params=_params("arbitrary"), name=name)(x, w, resid, target)
    if resid is None:
        def body(x_ref, w_ref, o_ref):
            o_ref[...] = norm(x_ref[...].astype(F32), w_ref[...]).astype(BF16)
        ins, in_specs = (x, w), [row, vec]
        out_shape, out_specs = jax.ShapeDtypeStruct((t, d), BF16), row
    elif want_u is None:
        def body(x_ref, w_ref, r_ref, o_ref):
            o_ref[...] = r_ref[...] + norm(x_ref[...].astype(F32), w_ref[...])
        ins, in_specs = (x, w, resid), [row, vec, row]
        out_shape, out_specs = jax.ShapeDtypeStruct((t, d), F32), row
    else:
        def body(x_ref, w_ref, r_ref, w2_ref, o_ref, u_ref):
            h = r_ref[...] + norm(x_ref[...].astype(F32), w_ref[...])
            o_ref[...] = h
            u_ref[...] = norm(h, w2_ref[...]).astype(BF16)
        ins, in_specs = (x, w, resid, want_u), [row, vec, row, vec]
        out_shape = [jax.ShapeDtypeStruct((t, d), F32), jax.ShapeDtypeStruct((t, d), BF16)]
        out_specs = [row, row]
    return pl.pallas_call(body, grid=(t // tr,), in_specs=in_specs, out_specs=out_specs, out_shape=out_shape,
                          compiler_params=_params("parallel"), name=name)(*ins)


def _rms_bwd(x, w, dy, *, name, resid=None, out_dtype=F32, dx_col_sum=False):
    t, d = x.shape
    tr = _row_tile(t, 1024)
    row = pl.BlockSpec((tr, d), lambda i: (i, 0))
    vec = pl.BlockSpec((1, d), lambda i: (0, 0))
    has_res = resid is not None

    def body(x_ref, w_ref, dy_ref, *rest):
        r_ref = rest[0] if has_res else None
        dx_ref, dw_ref = rest[has_res:has_res + 2]
        xv = x_ref[...].astype(F32)
        dyv = dy_ref[...].astype(F32)
        r = lax.rsqrt(jnp.mean(xv * xv, axis=-1, keepdims=True) + NORM_EPS)
        xhat = xv * r
        dyw = dyv * w_ref[...]
        dx = r * (dyw - xhat * jnp.mean(dyw * xhat, axis=-1, keepdims=True))
        if has_res:
            dx = dx + r_ref[...]
        dx_ref[...] = dx.astype(dx_ref.dtype)

        sums = [(dw_ref, dyv * xhat)] + ([(rest[-1], dx)] if dx_col_sum else [])

        @pl.when(pl.program_id(0) == 0)
        def _():
            for acc_ref, _ in sums:
                acc_ref[...] = jnp.zeros_like(acc_ref)

        for acc_ref, rows in sums:
            acc_ref[...] += jnp.sum(rows, axis=0, keepdims=True)

    ins = (x, w, dy) + ((resid,) if has_res else ())
    in_specs = [row, vec, row] + ([row] if has_res else [])
    n_vec = 2 if dx_col_sum else 1
    return pl.pallas_call(
        body, grid=(t // tr,), in_specs=in_specs, out_specs=[row] + [vec] * n_vec,
        out_shape=[jax.ShapeDtypeStruct((t, d), out_dtype)] + [jax.ShapeDtypeStruct((1, d), F32)] * n_vec,
        compiler_params=_params("arbitrary"), name=name)(*ins)


SSD_IN_SHARD = SSD_IN_DIM // N_CHIPS


def _w_in_from_shards(shards, *, name):
    d = shards.shape[1]
    tr = 256

    def body(s_ref, o_ref):
        o_ref[:, pl.ds(SSD_DT_COL, SSD_IN_PAD - SSD_DT_COL)] = jnp.zeros((tr, SSD_IN_PAD - SSD_DT_COL), o_ref.dtype)
        for s in range(N_CHIPS):
            o_ref[:, pl.ds(SSD_IN_SHARD * s, SSD_IN_SHARD)] = s_ref[s]

    return pl.pallas_call(
        body, grid=(d // tr,), in_specs=[pl.BlockSpec((N_CHIPS, tr, SSD_IN_SHARD), lambda i: (0, i, 0))],
        out_specs=pl.BlockSpec((tr, SSD_IN_PAD), lambda i: (i, 0)),
        out_shape=jax.ShapeDtypeStruct((d, SSD_IN_PAD), shards.dtype),
        compiler_params=_params("parallel"), name=name)(shards)


def _w_in_to_shards(g, *, name):
    d = g.shape[0]
    tr = 256

    def body(g_ref, o_ref):
        for s in range(N_CHIPS):
            o_ref[s] = g_ref[:, pl.ds(SSD_IN_SHARD * s, SSD_IN_SHARD)].astype(o_ref.dtype)

    return pl.pallas_call(
        body, grid=(d // tr,), in_specs=[pl.BlockSpec((tr, SSD_IN_PAD), lambda i: (i, 0))],
        out_specs=pl.BlockSpec((N_CHIPS, tr, SSD_IN_SHARD), lambda i: (0, i, 0)),
        out_shape=jax.ShapeDtypeStruct((N_CHIPS, d, SSD_IN_SHARD), BF16),
        compiler_params=_params("parallel"), name=name)(g)


XBC_COL0 = SSD_D_INNER // LANES


def _shift_down(v, k, row_ids):
    return jnp.where(row_ids >= k, pltpu.roll(v, k, axis=0), 0.0)


def _shift_up(v, k, row_ids):
    n = v.shape[0]
    return jnp.where(row_ids < n - k, pltpu.roll(v, n - k, axis=0), 0.0)


def _conv_pre(x, w, b, row_ids):
    pre = b + w[3:4, :] * x
    for k in (1, 2, 3):
        pre = pre + w[3 - k:4 - k, :] * _shift_down(x, k, row_ids)
    return pre


def _conv_fwd(zx, conv_w, conv_b, *, name, hook=None):
    t = zx.shape[0]
    cw = 2 * LANES
    nct = SSD_CONV_DIM // cw
    col0 = SSD_D_INNER // cw
    hk = _HookSlots(hook, n_in=3, n_out=1, n_scratch=0)

    def body(*refs):
        (x_ref, w_ref, b_ref), (o_ref,), _ = hk.own(refs)
        if hook is not None:
            hk.run(refs, pl.program_id(0), nct)
        x = x_ref[...].astype(F32)
        row_ids = lax.broadcasted_iota(jnp.int32, x.shape, 0)
        pre = _conv_pre(x, w_ref[...], b_ref[...], row_ids)
        o_ref[...] = pre * _sigmoid(pre)

    outs = pl.pallas_call(
        body, grid=(nct,),
        in_specs=[pl.BlockSpec((t, cw), lambda j: (0, col0 + j)),
                  pl.BlockSpec((SSD_CONV_WIDTH, cw), lambda j: (0, j)),
                  pl.BlockSpec((1, cw), lambda j: (0, j))] + hk.in_specs,
        out_specs=[pl.BlockSpec((t, cw), lambda j: (0, j))] + hk.out_specs,
        out_shape=[jax.ShapeDtypeStruct((t, SSD_CONV_DIM), F32)] + hk.out_shape,
        scratch_shapes=hk.scratch,
        compiler_params=_params(*hk.semantics("parallel")), name=name)(zx, conv_w, conv_b, *hk.inputs)
    return outs[0] if hook is None else (outs[0], outs[1:])


def _conv_bwd(zx, conv_w, conv_b, d_xs, d_bm, d_cm, dzx, *, name):
    t = zx.shape[0]
    nct = SSD_CONV_DIM // LANES
    n_xs = SSD_D_INNER // LANES
    n_bm = SSD_N_GROUPS * SSD_D_STATE // LANES

    def body(x_ref, w_ref, b_ref, dxs_ref, dbm_ref, dcm_ref, _, dx_ref, dw_ref, db_ref):
        x = x_ref[...].astype(F32)
        w = w_ref[...]
        j = pl.program_id(0)
        dy = jnp.where(j < n_xs, dxs_ref[...], jnp.where(j < n_xs + n_bm, dbm_ref[...], dcm_ref[...]))
        row_ids = lax.broadcasted_iota(jnp.int32, x.shape, 0)
        pre = _conv_pre(x, w, b_ref[...], row_ids)
        sg = _sigmoid(pre)
        dpre = dy * (sg * (1.0 + pre * (1.0 - sg)))
        dx = w[3:4, :] * dpre
        for k in (1, 2, 3):
            dx = dx + w[3 - k:4 - k, :] * _shift_up(dpre, k, row_ids)
        dx_ref[...] = dx.astype(dx_ref.dtype)
        db_ref[...] = jnp.sum(dpre, axis=0, keepdims=True)
        dw_ref[3:4, :] = jnp.sum(dpre * x, axis=0, keepdims=True)
        for k in (1, 2, 3):
            dw_ref[3 - k:4 - k, :] = jnp.sum(dpre * _shift_down(x, k, row_ids), axis=0, keepdims=True)

    clip = lambda j, lo, n: jnp.clip(j - lo, 0, n - 1)
    return pl.pallas_call(
        body, grid=(nct,),
        in_specs=[pl.BlockSpec((t, LANES), lambda j: (0, XBC_COL0 + j)),
                  pl.BlockSpec((SSD_CONV_WIDTH, LANES), lambda j: (0, j)),
                  pl.BlockSpec((1, LANES), lambda j: (0, j)),
                  pl.BlockSpec((t, LANES), lambda j: (0, clip(j, 0, n_xs))),
                  pl.BlockSpec((t, LANES), lambda j: (0, clip(j, n_xs, n_bm))),
                  pl.BlockSpec((t, LANES), lambda j: (0, clip(j, n_xs + n_bm, n_bm))), ANY],
        out_specs=[pl.BlockSpec((t, LANES), lambda j: (0, XBC_COL0 + j)),
                   pl.BlockSpec((SSD_CONV_WIDTH, LANES), lambda j: (0, j)), pl.BlockSpec((1, LANES), lambda j: (0, j))],
        out_shape=[jax.ShapeDtypeStruct(dzx.shape, dzx.dtype),
                   jax.ShapeDtypeStruct((SSD_CONV_WIDTH, SSD_CONV_DIM), F32),
                   jax.ShapeDtypeStruct((1, SSD_CONV_DIM), F32)],
        input_output_aliases={6: 0},
        compiler_params=_params("parallel"), name=name)(zx, conv_w, conv_b, d_xs, d_bm, d_cm, dzx)


def _softplus_fwd(dt_raw, bias_row, alog_row, *, name):
    t = dt_raw.shape[0]
    q = SSD_CHUNK
    tr = _row_tile(t, 1024)

    def body(x_ref, b_ref, al_ref, dt_ref, cum_ref):
        v = x_ref[...] + b_ref[...]
        e = jnp.exp(-jnp.abs(v))
        u = 1.0 + e
        log1p = jnp.where(u == 1.0, e, jnp.log(u) * (e / (u - 1.0)))
        dt = jnp.maximum(v, 0.0) + log1p
        a = dt * -jnp.exp(al_ref[...])
        lower = (lax.broadcasted_iota(jnp.int32, (q, q), 1) <= lax.broadcasted_iota(jnp.int32, (q, q), 0)).astype(F32)
        cums = [lax.dot_general(lower, a[c * q:(c + 1) * q, :], ((((1,), (0,))), ((), ())), precision=lax.Precision.HIGHEST,
                                preferred_element_type=F32) for c in range(tr // q)]
        dt_t, cum_t = dt.T, jnp.concatenate(cums, axis=0).T
        for g in range(SSD_N_GROUPS):
            rows = slice(g * SSD_HPG, (g + 1) * SSD_HPG)
            dt_ref[g] = dt_t[rows, :]
            cum_ref[g] = cum_t[rows, :]

    vec = pl.BlockSpec((1, LANES), lambda i: (0, 0))
    by_group = pl.BlockSpec((SSD_N_GROUPS, SSD_HPG, tr), lambda i: (0, 0, i))
    return pl.pallas_call(
        body, grid=(t // tr,),
        in_specs=[pl.BlockSpec((tr, LANES), lambda i: (i, 0)), vec, vec],
        out_specs=[by_group, by_group],
        out_shape=[jax.ShapeDtypeStruct((SSD_N_GROUPS, SSD_HPG, t), F32)] * 2,
        compiler_params=_params("parallel"), name=name)(dt_raw, bias_row, alog_row)


def _softplus_bwd(dt_raw, bias_row, ddt_rows, dzx, *, name):
    t = dt_raw.shape[0]
    tr = _row_tile(t, 1024)
    tail = SSD_IN_PAD - SSD_DT_COL

    def body(x_ref, b_ref, g_ref, _, o_ref, db_ref):
        v = x_ref[...] + b_ref[...]
        lane = lax.broadcasted_iota(jnp.int32, v.shape, 1)
        by_head = jnp.concatenate([g_ref[g] for g in range(SSD_N_GROUPS)]
                                  + [jnp.zeros((LANES - SSD_N_HEADS, tr), F32)], axis=0)
        d = jnp.where(lane < SSD_N_HEADS, by_head.T * _sigmoid(v), 0.0)
        o_ref[:, pl.ds(0, LANES)] = d.astype(o_ref.dtype)
        o_ref[:, pl.ds(LANES, tail - LANES)] = jnp.zeros((tr, tail - LANES), o_ref.dtype)

        @pl.when(pl.program_id(0) == 0)
        def _():
            db_ref[...] = jnp.zeros_like(db_ref)

        db_ref[...] += jnp.sum(d, axis=0, keepdims=True)

    return pl.pallas_call(
        body, grid=(t // tr,),
        in_specs=[pl.BlockSpec((tr, LANES), lambda i: (i, 0)), pl.BlockSpec((1, LANES), lambda i: (0, 0)),
                  pl.BlockSpec((SSD_N_GROUPS, SSD_HPG, tr), lambda i: (0, 0, i)), ANY],
        out_specs=[pl.BlockSpec((tr, tail), lambda i: (i, SSD_DT_COL // tail)), pl.BlockSpec((1, LANES), lambda i: (0, 0))],
        out_shape=[jax.ShapeDtypeStruct(dzx.shape, dzx.dtype), jax.ShapeDtypeStruct((1, LANES), F32)],
        input_output_aliases={3: 0},
        compiler_params=_params("arbitrary"), name=name)(dt_raw, bias_row, ddt_rows, dzx)


def _ssd_masks():
    q = SSD_CHUNK
    tt = lax.broadcasted_iota(jnp.int32, (q, q), 0)
    ss = lax.broadcasted_iota(jnp.int32, (q, q), 1)
    lane = lax.broadcasted_iota(jnp.int32, (1, SSD_GW), 1)
    srow = lax.broadcasted_iota(jnp.int32, (SSD_GW, 1), 0)
    hm = [(lane >= SSD_HEAD_DIM * j) & (lane < SSD_HEAD_DIM * (j + 1)) for j in range(SSD_HPG)]
    rm = [(srow >= SSD_HEAD_DIM * j) & (srow < SSD_HEAD_DIM * (j + 1)) for j in range(SSD_HPG)]
    return tt, ss, hm, rm


def _ssd_head_terms(dt_rows, cum_rows, a_rows, j, tt, ss):
    q = SSD_CHUNK
    dt_row = dt_rows[j:j + 1, :]
    dt_col = jnp.sum(jnp.where(tt == ss, dt_row, 0.0), axis=1, keepdims=True)
    a_row1 = a_rows[j:j + 1, :]
    a_11 = a_rows[j:j + 1, 0:1]
    cum_col = jnp.sum(jnp.where(ss <= tt, dt_row * a_row1, 0.0), axis=1, keepdims=True)
    cum_row = cum_rows[j:j + 1, :]
    decay = jnp.exp(jnp.where(ss <= tt, cum_col - cum_row, -jnp.inf))
    cum_last = cum_col[q - 1:q, :]
    e_col = jnp.exp(cum_col)
    dte_col = jnp.exp(cum_last - cum_col)
    e_last = jnp.exp(cum_last)
    return dt_col, dt_row, a_row1, a_11, decay, e_col, dte_col, e_last


SSD_CHUNKS_PER_STEP = 8
SSD_BC_COL0 = SSD_D_INNER // SSD_D_STATE


def _ssd_head_selects(terms, hm, rm):
    e_all = jnp.zeros((SSD_CHUNK, SSD_GW), F32)
    w_all = jnp.zeros((SSD_CHUNK, SSD_GW), F32)
    e_s = jnp.zeros((SSD_GW, 1), F32)
    for j in range(SSD_HPG):
        dt_col, _, _, _, _, e_col, dte_col, e_last = terms[j]
        e_all = jnp.where(hm[j], e_col, e_all)
        w_all = jnp.where(hm[j], dt_col * dte_col, w_all)
        e_s = jnp.where(rm[j], e_last, e_s)
    return e_all, w_all, e_s


def _ssd_fwd(xc, dtr, cumr, alog_b, d_b, *, name, hook=None):
    t = xc.shape[0]
    q = SSD_CHUNK
    nc = t // q
    kc = min(SSD_CHUNKS_PER_STEP, nc)
    rows = kc * q
    hk = _HookSlots(hook, n_in=7, n_out=2, n_scratch=1)

    def body(*refs):
        (x_ref, b_ref, c_ref, dtr_ref, cumr_ref, alog_ref, d_ref), (y_ref, st_ref), (s_scr,) = hk.own(refs)
        if hook is not None:
            hk.run(refs, pl.program_id(0) * (nc // kc) + pl.program_id(1), SSD_N_GROUPS * (nc // kc))

        @pl.when(pl.program_id(1) == 0)
        def _():
            s_scr[...] = jnp.zeros_like(s_scr)

        tt, ss, hm, rm = _ssd_masks()
        a_rows = -jnp.exp(alog_ref[...])
        d_rows = d_ref[...]
        d_all = jnp.zeros((1, SSD_GW), F32)
        for j in range(SSD_HPG):
            d_all = jnp.where(hm[j], d_rows[j:j + 1, 0:1], d_all)
        ks, hs = range(kc), range(SSD_HPG)
        sl = [pl.ds(k * q, q) for k in ks]
        x = [x_ref[sl[k], :] for k in ks]
        bm = [b_ref[sl[k], :].astype(BF16) for k in ks]
        cm = [c_ref[sl[k], :].astype(BF16) for k in ks]
        xb = [x[k].astype(BF16) for k in ks]
        terms = [[_ssd_head_terms(dtr_ref[:, sl[k]], cumr_ref[:, sl[k]], a_rows, j, tt, ss) for j in hs] for k in ks]
        g = [_dot_nt(cm[k], bm[k]) for k in ks]
        m = [[(g[k] * terms[k][j][4] * terms[k][j][1]).astype(BF16) for j in hs] for k in ks]
        yj = [[_dot_nn(m[k][j], xb[k]) for j in hs] for k in ks]
        sel = [_ssd_head_selects(terms[k], hm, rm) for k in ks]
        upd = [_dot_tn((x[k] * sel[k][1]).astype(BF16), bm[k]) for k in ks]
        states = [s_scr[...]]
        for k in ks:
            states.append(states[k] * sel[k][2] + upd[k])
        inter = [_dot_nt(cm[k], states[k].astype(BF16)) for k in ks]
        ys = []
        for k in ks:
            y = jnp.zeros((q, SSD_GW), F32)
            for j in hs:
                y = jnp.where(hm[j], yj[k][j], y)
            ys.append(y + inter[k] * sel[k][0] + x[k] * d_all)
        for k in ks:
            st_ref[k] = states[k]
        y_ref[...] = jnp.concatenate(ys, axis=0).astype(y_ref.dtype)
        s_scr[...] = states[kc]

    blk = lambda width, off: pl.BlockSpec((rows, width), lambda g, c: (c, off + g))
    par_s = pl.BlockSpec((None, SSD_HPG, LANES), lambda g, c: (g, 0, 0))
    row_s = pl.BlockSpec((None, SSD_HPG, rows), lambda g, c: (g, 0, c))
    outs = pl.pallas_call(
        body, grid=(SSD_N_GROUPS, nc // kc),
        in_specs=[blk(SSD_GW, 0), blk(SSD_D_STATE, SSD_BC_COL0), blk(SSD_D_STATE, SSD_BC_COL0 + SSD_N_GROUPS),
                  row_s, row_s, par_s, par_s] + hk.in_specs,
        out_specs=[blk(SSD_GW, 0), pl.BlockSpec((None, kc, SSD_GW, SSD_D_STATE), lambda g, c: (g, c, 0, 0))] + hk.out_specs,
        out_shape=[jax.ShapeDtypeStruct((t, SSD_D_INNER), BF16),
                   jax.ShapeDtypeStruct((SSD_N_GROUPS, nc, SSD_GW, SSD_D_STATE), F32)] + hk.out_shape,
        scratch_shapes=[pltpu.VMEM((SSD_GW, SSD_D_STATE), F32)] + hk.scratch,
        compiler_params=_params(*hk.semantics("parallel", "arbitrary")), name=name)(
            xc, xc, xc, dtr, cumr, alog_b, d_b, *hk.inputs)
    return outs if hook is None else (outs[:2], outs[2:])


def _ssd_bwd(xc, dtr, cumr, alog_b, d_b, states, dy, *, name, hook=None):
    t = xc.shape[0]
    q = SSD_CHUNK
    nc = t // q
    kc = min(SSD_CHUNKS_PER_STEP, nc)
    nst = nc // kc
    rows = kc * q
    rev = lambda c: nst - 1 - c
    hk = _HookSlots(hook, n_in=9, n_out=5, n_scratch=1)

    def body(*refs):
        ((x_ref, b_ref, c_ref, dtr_ref, cumr_ref, alog_ref, d_ref, st_ref, dy_ref),
         (dx_ref, db_ref, dc_ref, ddt_ref, dpar_ref), (ds_scr,)) = hk.own(refs)
        if hook is not None:
            hk.run(refs, pl.program_id(0) * nst + pl.program_id(1), SSD_N_GROUPS * nst)

        @pl.when(pl.program_id(1) == 0)
        def _():
            ds_scr[...] = jnp.zeros_like(ds_scr)
            dpar_ref[...] = jnp.zeros_like(dpar_ref)

        tt, ss, hm, rm = _ssd_masks()
        tcol = lax.broadcasted_iota(jnp.int32, (q, 1), 0)
        lane = lax.broadcasted_iota(jnp.int32, (1, LANES), 1)
        a_rows = -jnp.exp(alog_ref[...])
        d_rows = d_ref[...]
        d_all = jnp.zeros((1, SSD_GW), F32)
        for j in range(SSD_HPG):
            d_all = jnp.where(hm[j], d_rows[j:j + 1, 0:1], d_all)
        ks, hs = range(kc), range(SSD_HPG)
        sl = [pl.ds(k * q, q) for k in ks]
        x = [x_ref[sl[k], :] for k in ks]
        dyv = [dy_ref[sl[k], :].astype(F32) for k in ks]
        bm = [b_ref[sl[k], :].astype(BF16) for k in ks]
        cm = [c_ref[sl[k], :].astype(BF16) for k in ks]
        s_in = [st_ref[k] for k in ks]
        xb = [x[k].astype(BF16) for k in ks]
        dyb = [dyv[k].astype(BF16) for k in ks]
        s_b = [s_in[k].astype(BF16) for k in ks]
        terms = [[_ssd_head_terms(dtr_ref[:, sl[k]], cumr_ref[:, sl[k]], a_rows, j, tt, ss) for j in hs] for k in ks]
        sel = [_ssd_head_selects(terms[k], hm, rm) for k in ks]
        e_all, w_all, e_s = [s_[0] for s_ in sel], [s_[1] for s_ in sel], [s_[2] for s_ in sel]
        dye = [(dyv[k] * e_all[k]).astype(BF16) for k in ks]
        ds_loc = [_dot_tn(dye[k], cm[k]) for k in ks]
        ds = [None] * kc
        running = ds_scr[...]
        for k in reversed(ks):
            ds[k] = running
            running = running * e_s[k] + ds_loc[k]
        ds_scr[...] = running
        ds_b = [ds[k].astype(BF16) for k in ks]
        g = [_dot_nt(cm[k], bm[k]) for k in ks]
        cs = [_dot_nt(cm[k], s_b[k]) for k in ks]
        bds = [_dot_nt(bm[k], ds_b[k]) for k in ks]
        dm = [[_dot_nt(jnp.where(hm[j], dyv[k], 0.0).astype(BF16), xb[k]) for j in hs] for k in ks]
        gl = [[g[k] * terms[k][j][4] for j in hs] for k in ks]
        wp = [[dm[k][j] * gl[k][j] for j in hs] for k in ks]
        mt = [[(gl[k][j] * terms[k][j][1]).astype(BF16) for j in hs] for k in ks]
        dxj = [[_dot_tn(mt[k][j], dyb[k]) for j in hs] for k in ks]
        dg = []
        for k in ks:
            acc = jnp.zeros((q, q), F32)
            for j in hs:
                acc = acc + dm[k][j] * terms[k][j][4] * terms[k][j][1]
            dg.append(acc.astype(BF16))
        dy_cs = [dyv[k] * cs[k] for k in ks]
        x_bds = [x[k] * bds[k] for k in ks]
        dy_x = [dyv[k] * x[k] for k in ks]
        ds_s = [ds[k] * s_in[k] for k in ks]
        w = [[wp[k][j] * terms[k][j][1] for j in hs] for k in ks]
        rw_col = [[jnp.sum(w[k][j], axis=1, keepdims=True) for j in hs] for k in ks]
        cw_row = [[jnp.sum(w[k][j], axis=0, keepdims=True) for j in hs] for k in ks]
        cwp_row = [[jnp.sum(wp[k][j], axis=0, keepdims=True) for j in hs] for k in ks]
        r1_col = [[jnp.sum(jnp.where(hm[j], dy_cs[k], 0.0), axis=1, keepdims=True) * terms[k][j][5] for j in hs] for k in ks]
        dw_col = [[jnp.sum(jnp.where(hm[j], x_bds[k], 0.0), axis=1, keepdims=True) for j in hs] for k in ks]
        head_rows = [slice(j * SSD_HEAD_DIM, (j + 1) * SSD_HEAD_DIM) for j in hs]
        lane_sum = lambda v: jnp.sum(v, axis=1, keepdims=True)
        s_sum = [[lane_sum(jnp.sum(ds_s[k][head_rows[j], :], axis=0, keepdims=True)) for j in hs] for k in ks]
        dy_x_cols = [jnp.sum(dy_x[k], axis=0, keepdims=True) for k in ks]
        d_d = [[lane_sum(jnp.where(hm[j], dy_x_cols[k], 0.0)) for j in hs] for k in ks]
        ddt_rows = [[None] * SSD_HPG for _ in ks]
        dpar = [jnp.zeros((1, LANES), F32) for _ in hs]
        for k in ks:
            for j in hs:
                dt_col, dt_row, a_row1, a_11, _, _, dte_col, e_last = terms[k][j]
                dww = dw_col[k][j] * (dt_col * dte_col)
                last_add = jnp.sum(dww, axis=0, keepdims=True) + e_last * s_sum[k][j]
                dcum_col = rw_col[k][j] + r1_col[k][j] - dww + jnp.where(tcol == q - 1, last_add, 0.0)
                da_row = jnp.sum(jnp.where(tt >= ss, dcum_col, 0.0), axis=0, keepdims=True)
                da_col = jnp.sum(jnp.where(ss >= tt, -cw_row[k][j], 0.0), axis=1, keepdims=True)
                ddt_col = a_11 * da_col + dw_col[k][j] * dte_col
                ddt_rows[k][j] = (a_row1 * da_row + cwp_row[k][j]
                                  + jnp.sum(jnp.where(tt == ss, ddt_col, 0.0), axis=0, keepdims=True))
                d_a = jnp.sum(dt_row * da_row, axis=1, keepdims=True) + jnp.sum(dt_col * da_col, axis=0, keepdims=True)
                dpar[j] = dpar[j] + jnp.where(lane == 0, d_a * a_11, 0.0) + jnp.where(lane == 1, d_d[k][j], 0.0)
        dxs = []
        for k in ks:
            acc = jnp.zeros((q, SSD_GW), F32)
            for j in hs:
                acc = jnp.where(hm[j], dxj[k][j], acc)
            dxs.append(acc + w_all[k] * bds[k] + d_all * dyv[k])
        xw = [(x[k] * w_all[k]).astype(BF16) for k in ks]
        dc = [_dot_nn(dg[k], bm[k]) + _dot_nn(dye[k], s_b[k]) for k in ks]
        db = [_dot_tn(dg[k], cm[k]) + _dot_nn(xw[k], ds_b[k]) for k in ks]
        dx_ref[...] = jnp.concatenate(dxs, axis=0)
        dc_ref[...] = jnp.concatenate(dc, axis=0)
        db_ref[...] = jnp.concatenate(db, axis=0)
        ddt_ref[...] = jnp.concatenate([jnp.concatenate([ddt_rows[k][j] for k in ks], axis=1) for j in hs], axis=0)
        dpar_ref[...] += jnp.concatenate(dpar, axis=0)

    blk = lambda width, off: pl.BlockSpec((rows, width), lambda g, c: (rev(c), off + g))
    par_s = pl.BlockSpec((None, SSD_HPG, LANES), lambda g, c: (g, 0, 0))
    outs = pl.pallas_call(
        body, grid=(SSD_N_GROUPS, nst),
        in_specs=[blk(SSD_GW, 0), blk(SSD_D_STATE, SSD_BC_COL0), blk(SSD_D_STATE, SSD_BC_COL0 + SSD_N_GROUPS),
                  pl.BlockSpec((None, SSD_HPG, rows), lambda g, c: (g, 0, rev(c))),
                  pl.BlockSpec((None, SSD_HPG, rows), lambda g, c: (g, 0, rev(c))), par_s, par_s,
                  pl.BlockSpec((None, kc, SSD_GW, SSD_D_STATE), lambda g, c: (g, rev(c), 0, 0)), blk(SSD_GW, 0)] + hk.in_specs,
        out_specs=[blk(SSD_GW, 0), blk(SSD_D_STATE, 0), blk(SSD_D_STATE, 0),
                   pl.BlockSpec((None, SSD_HPG, rows), lambda g, c: (g, 0, rev(c))), par_s] + hk.out_specs,
        out_shape=[jax.ShapeDtypeStruct((t, SSD_D_INNER), F32),
                   jax.ShapeDtypeStruct((t, SSD_N_GROUPS * SSD_D_STATE), F32),
                   jax.ShapeDtypeStruct((t, SSD_N_GROUPS * SSD_D_STATE), F32),
                   jax.ShapeDtypeStruct((SSD_N_GROUPS, SSD_HPG, t), F32),
                   jax.ShapeDtypeStruct((SSD_N_GROUPS, SSD_HPG, LANES), F32)] + hk.out_shape,
        scratch_shapes=[pltpu.VMEM((SSD_GW, SSD_D_STATE), F32)] + hk.scratch,
        compiler_params=_params(*hk.semantics("parallel", "arbitrary")), name=name)(
            xc, xc, xc, dtr, cumr, alog_b, d_b, states, dy, *hk.inputs)
    return outs if hook is None else (outs[:5], outs[5:])


def _gate_norm_fwd(y, zx, norm_w, *, name):
    t = y.shape[0]
    tr = _row_tile(t, 512)
    row = pl.BlockSpec((tr, SSD_D_INNER), lambda i: (i, 0))

    def body(y_ref, z_ref, w_ref, o_ref):
        for gi in range(SSD_N_GROUPS):
            sl = pl.ds(gi * SSD_GW, SSD_GW)
            z = z_ref[:, sl].astype(F32)
            gv = y_ref[:, sl].astype(F32) * (z * _sigmoid(z))
            r = lax.rsqrt(jnp.mean(gv * gv, axis=-1, keepdims=True) + NORM_EPS)
            o_ref[:, sl] = (gv * r * w_ref[:, sl]).astype(BF16)

    return pl.pallas_call(
        body, grid=(t // tr,), in_specs=[row, row, pl.BlockSpec((1, SSD_D_INNER), lambda i: (0, 0))],
        out_specs=row, out_shape=jax.ShapeDtypeStruct((t, SSD_D_INNER), BF16),
        compiler_params=_params("parallel"), name=name)(y, zx, norm_w)


def _gate_norm_bwd(y, zx, norm_w, dyn, *, name):
    t = y.shape[0]
    tr = _row_tile(t, 512)
    row = pl.BlockSpec((tr, SSD_D_INNER), lambda i: (i, 0))
    vec = pl.BlockSpec((1, SSD_D_INNER), lambda i: (0, 0))

    def body(y_ref, z_ref, w_ref, dyn_ref, dy_ref, dz_ref, dw_ref):
        @pl.when(pl.program_id(0) == 0)
        def _():
            dw_ref[...] = jnp.zeros_like(dw_ref)

        for gi in range(SSD_N_GROUPS):
            sl = pl.ds(gi * SSD_GW, SSD_GW)
            z = z_ref[:, sl].astype(F32)
            yv = y_ref[:, sl].astype(F32)
            sg = _sigmoid(z)
            sz = z * sg
            gv = yv * sz
            r = lax.rsqrt(jnp.mean(gv * gv, axis=-1, keepdims=True) + NORM_EPS)
            ghat = gv * r
            dout = dyn_ref[:, sl].astype(F32)
            dgh = dout * w_ref[:, sl]
            dgv = r * (dgh - ghat * jnp.mean(dgh * ghat, axis=-1, keepdims=True))
            dy_ref[:, sl] = (dgv * sz).astype(dy_ref.dtype)
            dz_ref[:, sl] = (dgv * yv * (sg * (1.0 + z * (1.0 - sg)))).astype(dz_ref.dtype)
            dw_ref[:, sl] += jnp.sum(dout * ghat, axis=0, keepdims=True)

    return pl.pallas_call(
        body, grid=(t // tr,), in_specs=[row, row, vec, row], out_specs=[row, row, vec],
        out_shape=[jax.ShapeDtypeStruct((t, SSD_D_INNER), BF16), jax.ShapeDtypeStruct((t, SSD_IN_PAD), BF16),
                   jax.ShapeDtypeStruct((1, SSD_D_INNER), F32)],
        compiler_params=_params("arbitrary"), name=name)(y, zx, norm_w, dyn)


ATTN_KV_W = ATTN_N_KV * ATTN_HEAD_DIM
ATTN_Q_HALF = 512
ATTN_K_BLK = ATTN_N_Q * ATTN_HEAD_DIM // ATTN_KV_W
ATTN_V_BLK = ATTN_K_BLK + 1


def _attn_valid(first_block):
    w = ATTN_WINDOW
    qpos = lax.broadcasted_iota(jnp.int32, (w, 2 * w), 0) + w
    kpos = lax.broadcasted_iota(jnp.int32, (w, 2 * w), 1)
    rel = qpos - kpos
    return (rel >= 0) & (rel < w) & jnp.logical_not(first_block & (kpos < w))


def _attn_head_views(lo_ref, hi_ref):
    hd = ATTN_HEAD_DIM
    per_half = ATTN_Q_HALF // hd
    return [(lo_ref if h < per_half else hi_ref)[:, pl.ds((h % per_half) * hd, hd)] for h in range(ATTN_N_Q)]


def _attn_block_views(lo_ref, hi_ref, kc_ref, kp_ref, vc_ref, vp_ref):
    hd = ATTN_HEAD_DIM
    kv_cols = [pl.ds(kh * hd, hd) for kh in range(ATTN_N_KV)]
    kb = [jnp.concatenate([kp_ref[:, c], kc_ref[:, c]], axis=0) for c in kv_cols]
    vb = [jnp.concatenate([vp_ref[:, c], vc_ref[:, c]], axis=0) for c in kv_cols]
    return _attn_head_views(lo_ref, hi_ref), kb, vb


def _attn_scores(q, kb, valid):
    scale = ATTN_HEAD_DIM ** -0.5
    return [jnp.where(valid, _dot_nt(q[h], kb[h // ATTN_REP]) * scale, -jnp.inf) for h in range(ATTN_N_Q)]


def _attn_softmax(s, sink):
    heads = range(ATTN_N_Q)
    m = [jnp.maximum(jnp.max(s[h], axis=1, keepdims=True), sink[h]) for h in heads]
    e = [jnp.exp(s[h] - m[h]) for h in heads]
    es = [jnp.exp(sink[h] - m[h]) for h in heads]
    inv = [1.0 / (jnp.sum(e[h], axis=1, keepdims=True) + es[h]) for h in heads]
    return e, es, inv


def _attn_fwd(qkv, sinks_b, *, name, hook=None):
    t = qkv.shape[0]
    w = ATTN_WINDOW
    nb = t // w
    prev = lambda n: jnp.maximum(n - 1, 0)
    hk = _HookSlots(hook, n_in=7, n_out=1, n_scratch=0)

    def body(*refs):
        (qlo_ref, qhi_ref, kc_ref, kp_ref, vc_ref, vp_ref, sink_ref), (o_ref,), _ = hk.own(refs)
        if hook is not None:
            hk.run(refs, pl.program_id(0), nb)
        heads = range(ATTN_N_Q)
        q, kb, vb = _attn_block_views(qlo_ref, qhi_ref, kc_ref, kp_ref, vc_ref, vp_ref)
        sink = [sink_ref[h:h + 1, 0:1] for h in heads]
        e, _, inv = _attn_softmax(_attn_scores(q, kb, _attn_valid(pl.program_id(0) == 0)), sink)
        out = [_dot_nn((e[h] * inv[h]).astype(BF16), vb[h // ATTN_REP]).astype(o_ref.dtype) for h in heads]
        o_ref[...] = jnp.concatenate(out, axis=1)

    qh = lambda half: pl.BlockSpec((w, ATTN_Q_HALF), lambda n: (n, half))
    kv = lambda blk, idx: pl.BlockSpec((w, ATTN_KV_W), lambda n: (idx(n), blk))
    cur = lambda n: n
    outs = pl.pallas_call(
        body, grid=(nb,),
        in_specs=[qh(0), qh(1), kv(ATTN_K_BLK, cur), kv(ATTN_K_BLK, prev), kv(ATTN_V_BLK, cur), kv(ATTN_V_BLK, prev),
                  pl.BlockSpec((ATTN_N_Q, LANES), lambda n: (0, 0))] + hk.in_specs,
        out_specs=[pl.BlockSpec((w, D_MODEL), lambda n: (n, 0))] + hk.out_specs,
        out_shape=[jax.ShapeDtypeStruct((t, D_MODEL), BF16)] + hk.out_shape,
        scratch_shapes=hk.scratch,
        compiler_params=_params(*hk.semantics("parallel")), name=name)(qkv, qkv, qkv, qkv, qkv, qkv, sinks_b, *hk.inputs)
    return outs[0] if hook is None else (outs[0], outs[1:])


def _attn_bwd(qkv, sinks_b, dout, *, name):
    t = qkv.shape[0]
    w = ATTN_WINDOW
    nb = t // w
    hd = ATTN_HEAD_DIM
    clamp = lambda n: jnp.minimum(n, nb - 1)
    prev = lambda n: jnp.maximum(clamp(n) - 1, 0)

    def body(qlo_ref, qhi_ref, kc_ref, kp_ref, vc_ref, vp_ref, sink_ref, dolo_ref, dohi_ref,
             dq_ref, dkv_ref, dsink_ref, dbias_ref, carry):
        n = pl.program_id(0)

        @pl.when(n == 0)
        def _():
            carry[...] = jnp.zeros_like(carry)
            dsink_ref[...] = jnp.zeros_like(dsink_ref)
            dbias_ref[...] = jnp.zeros_like(dbias_ref)

        @pl.when(n < nb)
        def _():
            heads, kvs = range(ATTN_N_Q), range(ATTN_N_KV)
            q, kb, vb = _attn_block_views(qlo_ref, qhi_ref, kc_ref, kp_ref, vc_ref, vp_ref)
            do = _attn_head_views(dolo_ref, dohi_ref)
            sink = [sink_ref[h:h + 1, 0:1] for h in heads]
            s = _attn_scores(q, kb, _attn_valid(n == 0))
            dp = [_dot_nt(do[h], vb[h // ATTN_REP]) for h in heads]
            e, es, inv = _attn_softmax(s, sink)
            p = [e[h] * inv[h] for h in heads]
            delta = [jnp.sum(p[h] * dp[h], axis=1, keepdims=True) for h in heads]
            dsc = [(p[h] * (dp[h] - delta[h]) * (hd ** -0.5)).astype(BF16) for h in heads]
            pb = [p[h].astype(BF16) for h in heads]
            dq = [_dot_nn(dsc[h], kb[h // ATTN_REP]) for h in heads]
            stack = lambda per_head, kh: jnp.concatenate(per_head[kh * ATTN_REP:(kh + 1) * ATTN_REP], axis=0)
            dkb = [_dot_tn(stack(dsc, kh), stack(q, kh)) for kh in kvs]
            dvb = [_dot_tn(stack(pb, kh), stack(do, kh)) for kh in kvs]
            dsink = [jnp.broadcast_to(jnp.sum(-es[h] * inv[h] * delta[h], axis=0, keepdims=True), (1, LANES)) for h in heads]
            dq_ref[...] = jnp.concatenate([d.astype(dq_ref.dtype) for d in dq], axis=1)
            dsink_ref[...] += jnp.concatenate(dsink, axis=0)
            dbias_ref[...] += jnp.concatenate([jnp.sum(d, axis=0, keepdims=True) for d in dq + dkb + dvb], axis=1)
            dkv_ref[...] = (carry[...] + jnp.concatenate([d[0:w, :] for d in dkb + dvb], axis=1)).astype(dkv_ref.dtype)
            carry[...] = jnp.concatenate([d[w:2 * w, :] for d in dkb + dvb], axis=1)

        @pl.when(n == nb)
        def _():
            dkv_ref[...] = carry[...].astype(dkv_ref.dtype)

    qh = lambda half: pl.BlockSpec((w, ATTN_Q_HALF), lambda n: (clamp(n), half))
    kv = lambda blk, idx: pl.BlockSpec((w, ATTN_KV_W), lambda n: (idx(n), blk))
    return pl.pallas_call(
        body, grid=(nb + 1,),
        in_specs=[qh(0), qh(1), kv(ATTN_K_BLK, clamp), kv(ATTN_K_BLK, prev), kv(ATTN_V_BLK, clamp), kv(ATTN_V_BLK, prev),
                  pl.BlockSpec((ATTN_N_Q, LANES), lambda n: (0, 0)), qh(0), qh(1)],
        out_specs=[pl.BlockSpec((w, D_MODEL), lambda n: (clamp(n), 0)),
                   pl.BlockSpec((w, 2 * ATTN_KV_W), lambda n: (jnp.maximum(n - 1, 0), 0)),
                   pl.BlockSpec((ATTN_N_Q, LANES), lambda n: (0, 0)), pl.BlockSpec((1, ATTN_QKV), lambda n: (0, 0))],
        out_shape=[jax.ShapeDtypeStruct((t, D_MODEL), BF16), jax.ShapeDtypeStruct((t, 2 * ATTN_KV_W), BF16),
                   jax.ShapeDtypeStruct((ATTN_N_Q, LANES), F32), jax.ShapeDtypeStruct((1, ATTN_QKV), F32)],
        scratch_shapes=[pltpu.VMEM((w, 2 * ATTN_KV_W), F32)],
        compiler_params=_params("arbitrary"), name=name)(qkv, qkv, qkv, qkv, qkv, qkv, sinks_b, dout, dout)


def _sq_relu_epilogue(acc):
    r = jnp.maximum(acc, 0.0)
    return (r * r,)


def _sq_relu_bwd_epilogue(acc, act):
    return (acc * (2.0 * jnp.sqrt(act.astype(F32))),)


def _bias_epilogue(acc, bias):
    return (acc + bias,)


def _plain_run(stage, fn, *args, **kwargs):
    return fn(*args, **kwargs)


def _mlp_fwd(u, w_up, w_down, tag, run=_plain_run):
    act = run(f"mlp_up_{tag}", _matmul, u, w_up, mode="nn", out_dtypes=(BF16,), epilogue=_sq_relu_epilogue, b_shards=True,
              tm=BIG_TILE, name=f"mlp_up_{tag}")
    f = run(f"mlp_down_{tag}", _matmul, act, w_down, mode="nn", out_dtypes=(BF16,), tk=BIG_TILE, name=f"mlp_down_{tag}")
    return act, f


def _mlp_bwd(u, act, w_up, w_down, df, tag):
    dpre = _matmul(df, w_down, mode="nt", out_dtypes=(BF16,), epilogue=_sq_relu_bwd_epilogue,
                   extras=((act, "tile"),), name=f"mlp_dact_{tag}")
    dw_down = _matmul(act, df, mode="tn", out_dtypes=(BF16,), tk=BIG_TILE, name=f"mlp_dwdown_{tag}")
    du = _matmul(dpre, w_up, mode="nt", out_dtypes=(BF16,), b_shards=True, tm=BIG_TILE, name=f"mlp_du_{tag}")
    dw_up = _matmul(u, dpre, mode="tn", out_dtypes=(BF16,), out_shards=True, tk=BIG_TILE, name=f"mlp_dwup_{tag}")
    return du, dw_up, dw_down


def _head_param_rows(p):
    return jnp.broadcast_to(p.reshape(SSD_N_GROUPS, SSD_HPG, 1), (SSD_N_GROUPS, SSD_HPG, LANES))


def _local_step(x, target, wts, comm=None, u0=None):
    wts = dict(wts)
    row = lambda v: v.reshape(1, -1)
    mix_pre, mix_post, ffn_pre, ffn_post = wts["mix_pre_norm"], wts["mix_post_norm"], wts["ffn_pre_norm"], wts["ffn_post_norm"]

    def gathering(stage, fn, *args, **kwargs):
        hook = comm.gather_hook(stage) if comm is not None else None
        if hook is None:
            return fn(*args, **kwargs)
        out, got = fn(*args, hook=hook, **kwargs)
        wts.update(comm.weights_from(stage, got))
        return out

    if u0 is None:
        u0 = _rms_fwd(x, row(mix_pre[0]), name="rms_pre_mix0")
    zx, dt_raw = gathering("in_proj", _matmul, u0, wts["ssd_w_in"], mode="nn", out_dtypes=(BF16,), tn=SSD_IN_TILE,
                           f32_block=SSD_DT_COL - (SSD_IN_PAD - SSD_IN_TILE),
                           name="ssd_in_proj")
    xc = gathering("conv", _conv_fwd, zx, wts["ssd_conv_w"], row(wts["ssd_conv_b"]), name="ssd_conv_fwd")
    bias_row = jnp.pad(wts["ssd_dt_bias"], (0, LANES - SSD_N_HEADS)).reshape(1, LANES)
    alog_row = jnp.pad(wts["ssd_a_log"], (0, LANES - SSD_N_HEADS)).reshape(1, LANES)
    dtr, cumr = _softplus_fwd(dt_raw, bias_row, alog_row, name="ssd_dt_fwd")
    alog_b, d_b = _head_param_rows(wts["ssd_a_log"]), _head_param_rows(wts["ssd_d"])
    y_ssd, states = gathering("scan", _ssd_fwd, xc, dtr, cumr, alog_b, d_b, name="ssd_scan_fwd")
    norm_w = row(wts["ssd_norm_w"])
    yn = _gate_norm_fwd(y_ssd, zx, norm_w, name="ssd_gate_norm_fwd")
    mix0 = _matmul(yn, wts["ssd_w_out"], mode="nn", out_dtypes=(BF16,), tk=BIG_TILE, name="ssd_out_proj")
    h1, v0 = _rms_fwd(mix0, row(mix_post[0]), resid=x, want_u=row(ffn_pre[0]), name="rms_post_mix0")
    act0, f0 = _mlp_fwd(v0, wts["mlp_w_up0"], wts["mlp_w_down0"], "l0", run=gathering)
    h2, u1 = _rms_fwd(f0, row(ffn_post[0]), resid=h1, want_u=row(mix_pre[1]), name="rms_post_ffn0")

    qkv = _matmul(u1, wts["attn_w_qkv"], mode="nn", out_dtypes=(BF16,), epilogue=_bias_epilogue,
                  extras=((row(wts["attn_b_qkv"]), "row"),), b_shards=True, tm=BIG_TILE, name="attn_qkv_proj")
    sinks_b = jnp.broadcast_to(wts["attn_sinks"].reshape(ATTN_N_Q, 1), (ATTN_N_Q, LANES))
    ao = gathering("attn_fwd", _attn_fwd, qkv, sinks_b, name="attn_fwd")
    mix1 = _matmul(ao, wts["attn_w_o"], mode="nn", out_dtypes=(BF16,), epilogue=_bias_epilogue,
                   extras=((row(wts["attn_b_o"]), "row"),), name="attn_out_proj")
    h3, v1 = _rms_fwd(mix1, row(mix_post[1]), resid=h2, want_u=row(ffn_pre[1]), name="rms_post_mix1")
    act1, f1 = _mlp_fwd(v1, wts["mlp_w_up1"], wts["mlp_w_down1"], "l1")
    dh4, loss_tile = _rms_fwd(f1, row(ffn_post[1]), resid=h3, target=target, name="rms_post_ffn1_loss")

    df1, g_ffn_post1 = _rms_bwd(f1, row(ffn_post[1]), dh4, out_dtype=BF16, name="rms_post_ffn1_bwd")
    dv1, g_up1, g_down1 = _mlp_bwd(v1, act1, wts["mlp_w_up1"], wts["mlp_w_down1"], df1, "l1")
    dh3, g_ffn_pre1 = _rms_bwd(h3, row(ffn_pre[1]), dv1, resid=dh4, name="rms_pre_ffn1_bwd")
    dmix1, g_mix_post1, g_b_o = _rms_bwd(mix1, row(mix_post[1]), dh3, out_dtype=BF16, dx_col_sum=True, name="rms_post_mix1_bwd")
    g_w_o = _matmul(ao, dmix1, mode="tn", out_dtypes=(BF16,), tk=BIG_TILE, name="attn_dwo")
    dao = _matmul(dmix1, wts["attn_w_o"], mode="nt", out_dtypes=(BF16,), name="attn_dao")
    dq, dkv, g_sinks, g_b_qkv = _attn_bwd(qkv, sinks_b, dao, name="attn_bwd")
    dqkv = jnp.concatenate([dq, dkv], axis=1)
    g_w_qkv = _matmul(u1, dqkv, mode="tn", out_dtypes=(BF16,), tn=ATTN_QKV // N_CHIPS, out_shards=True, tk=BIG_TILE, name="attn_dwqkv")
    du1 = _matmul(dqkv, wts["attn_w_qkv"], mode="nt", out_dtypes=(BF16,), b_shards=True, tm=BIG_TILE, name="attn_du")
    dh2, g_mix_pre1 = _rms_bwd(h2, row(mix_pre[1]), du1, resid=dh3, name="rms_pre_mix1_bwd")

    df0, g_ffn_post0 = _rms_bwd(f0, row(ffn_post[0]), dh2, out_dtype=BF16, name="rms_post_ffn0_bwd")
    dv0, g_up0, g_down0 = _mlp_bwd(v0, act0, wts["mlp_w_up0"], wts["mlp_w_down0"], df0, "l0")
    dh1, g_ffn_pre0 = _rms_bwd(h1, row(ffn_pre[0]), dv0, resid=dh2, name="rms_pre_ffn0_bwd")
    dmix0, g_mix_post0 = _rms_bwd(mix0, row(mix_post[0]), dh1, out_dtype=BF16, name="rms_post_mix0_bwd")
    g_w_out = _matmul(yn, dmix0, mode="tn", out_dtypes=(BF16,), tk=BIG_TILE, name="ssd_dwout")
    dyn = _matmul(dmix0, wts["ssd_w_out"], mode="nt", out_dtypes=(BF16,), tm=BIG_TILE, name="ssd_dyn")
    dy_ssd, dzx, g_norm_w = _gate_norm_bwd(y_ssd, zx, norm_w, dyn, name="ssd_gate_norm_bwd")
    mats = {"ssd_w_out": g_w_out, "attn_w_qkv": g_w_qkv, "attn_w_o": g_w_o,
            "mlp_w_up0": g_up0, "mlp_w_up1": g_up1, "mlp_w_down0": g_down0, "mlp_w_down1": g_down1}
    if comm is None:
        dxc, dbm, dcm, ddt_r, dpar = _ssd_bwd(xc, dtr, cumr, alog_b, d_b, states, dy_ssd, name="ssd_scan_bwd")
    else:
        (dxc, dbm, dcm, ddt_r, dpar), received = _ssd_bwd(xc, dtr, cumr, alog_b, d_b, states, dy_ssd,
                                                          name="ssd_scan_bwd", hook=comm.exchange_hook(mats, "early"))
        comm.received(received)
    dzx, g_conv_w, g_conv_b = _conv_bwd(zx, wts["ssd_conv_w"], row(wts["ssd_conv_b"]), dxc, dbm, dcm, dzx, name="ssd_conv_bwd")
    dzx, g_dt_bias = _softplus_bwd(dt_raw, bias_row, ddt_r, dzx, name="ssd_dt_bwd")
    g_w_in = _w_in_to_shards(_matmul(u0, dzx, mode="tn", out_dtypes=(BF16,), tn=SSD_IN_TILE, tk=BIG_TILE, name="ssd_dwin"), name="ssd_dwin_shards")
    mats["ssd_w_in"] = g_w_in
    if comm is None:
        du0 = _matmul(dzx, wts["ssd_w_in"], mode="nt", out_dtypes=(BF16,), tk=SSD_IN_TILE, name="ssd_du")
    else:
        du0, received = _matmul(dzx, wts["ssd_w_in"], mode="nt", out_dtypes=(BF16,), tk=SSD_IN_TILE, name="ssd_du",
                                hook=comm.exchange_hook(mats, "late"))
        comm.received(received)
    grad_x, g_mix_pre0 = _rms_bwd(x, row(mix_pre[0]), du0, resid=dh1, name="rms_pre_mix0_bwd")

    dpar = dpar.reshape(SSD_N_HEADS, LANES)
    vecs = {
        "ssd_conv_w": g_conv_w, "ssd_conv_b": g_conv_b.reshape(-1),
        "ssd_dt_bias": g_dt_bias[0, :SSD_N_HEADS], "ssd_a_log": dpar[:, 0], "ssd_d": dpar[:, 1],
        "ssd_norm_w": g_norm_w.reshape(-1), "attn_b_qkv": g_b_qkv.reshape(-1), "attn_sinks": g_sinks[:, 0],
        "attn_b_o": g_b_o.reshape(-1),
        "mix_pre_norm": jnp.concatenate([g_mix_pre0, g_mix_pre1]), "mix_post_norm": jnp.concatenate([g_mix_post0, g_mix_post1]),
        "ffn_pre_norm": jnp.concatenate([g_ffn_pre0, g_ffn_pre1]), "ffn_post_norm": jnp.concatenate([g_ffn_post0, g_ffn_post1]),
    }
    return loss_tile, grad_x, mats, vecs


def _mesh_position():
    return lax.axis_index("x"), lax.axis_index("y"), lax.axis_index("c")


def _flip(v, bit):
    return 1 - v if bit else v


OTHER_CHIPS = ((1, 0), (0, 1), (1, 1))


def _comm_params():
    return pltpu.CompilerParams(vmem_limit_bytes=VMEM_LIMIT)


def _staged_copies(srcs, dsts, bufs, sems_in, sems_out):
    loads = [pltpu.make_async_copy(s, b, sems_in.at[i]) for i, (s, b) in enumerate(zip(srcs, bufs))]
    stores = [pltpu.make_async_copy(b, d, sems_out.at[i]) for i, (b, d) in enumerate(zip(bufs, dsts))]
    return loads, stores


class _GatherHook:
    def __init__(self, mats, vecs=()):
        self.arrs = list(mats) + list(vecs)
        self.nm, self.n = len(mats), len(self.arrs)
        n_ici, n_fwd = (N_CHIPS - 1) * self.n, max((N_CHIPS - 1) * self.nm, 1)
        dma = pltpu.SemaphoreType.DMA
        self.out_shape = [jax.ShapeDtypeStruct((N_CHIPS,) + a.shape, a.dtype) for a in self.arrs]
        self.scratch = [pltpu.VMEM(a.shape, a.dtype) for a in self.arrs] + [
            dma((n_ici,)), dma((n_ici,)), dma((n_fwd,)), dma((n_fwd,)), dma((self.n,)), dma((self.n,))]

    def plan(self, ins, outs, scratch):
        n, nm = self.n, self.nm
        bufs = scratch[:n]
        ici_send, ici_recv, fwd_send, fwd_recv, load_sems, store_sems = scratch[n:]
        xi, yi, ci = _mesh_position()
        me = 2 * xi + yi
        loads, stores = _staged_copies(ins, [outs[i].at[me] for i in range(n)], bufs, load_sems, store_sems)
        sends, landed, forwards, from_sibling = [], [], [], []
        for j, (bx, by) in enumerate(OTHER_CHIPS):
            px, py = _flip(xi, bx), _flip(yi, by)
            peer = 2 * px + py
            for i in range(n):
                k = j * n + i
                mk = functools.partial(pltpu.make_async_remote_copy, send_sem=ici_send.at[k], recv_sem=ici_recv.at[k],
                                       device_id=(px, py, ci), device_id_type=MESH)
                if i < nm:
                    sends.append(mk(src_ref=ins[i].at[ci], dst_ref=outs[i].at[me, ci]))
                    landed.append(mk(src_ref=ins[i].at[ci], dst_ref=outs[i].at[peer, ci]))
                    kf = j * nm + i
                    fw = functools.partial(pltpu.make_async_remote_copy, send_sem=fwd_send.at[kf], recv_sem=fwd_recv.at[kf],
                                           device_id=(xi, yi, 1 - ci), device_id_type=MESH)
                    forwards.append(fw(src_ref=outs[i].at[peer, ci], dst_ref=outs[i].at[peer, ci]))
                    from_sibling.append(fw(src_ref=outs[i].at[peer, ci], dst_ref=outs[i].at[peer, 1 - ci]))
                else:
                    sends.append(mk(src_ref=ins[i], dst_ref=outs[i].at[me]))
                    landed.append(mk(src_ref=ins[i], dst_ref=outs[i].at[peer]))
                    forwards.append(None)
        return loads, stores, sends, landed, forwards, from_sibling

    @staticmethod
    def start(p):
        loads, _, sends, _, _, _ = p
        for cp in loads + sends:
            cp.start()

    @staticmethod
    def relay(p):
        loads, stores, _, landed, forwards, _ = p
        for ld, st in zip(loads, stores):
            ld.wait()
            st.start()
        for cp, fw in zip(landed, forwards):
            cp.wait_recv()
            if fw is not None:
                fw.start()

    @staticmethod
    def finish(p):
        _, stores, sends, _, forwards, from_sibling = p
        for cp in from_sibling:
            cp.wait_recv()
        for cp in sends + [fw for fw in forwards if fw is not None]:
            cp.wait_send()
        for st in stores:
            st.wait()


def _run_hook(hook, ins, outs, scratch, step, n_steps):
    p = hook.plan(ins, outs, scratch)
    relay_step = min(max(1, (3 * n_steps) // 4), n_steps - 1)

    @pl.when(step == 0)
    def _():
        hook.start(p)

    if relay_step < n_steps - 1:
        @pl.when(step == relay_step)
        def _():
            hook.relay(p)

    @pl.when(step == n_steps - 1)
    def _():
        if relay_step == n_steps - 1:
            hook.relay(p)
        hook.finish(p)


def _hook_call(hook, *, name):
    n = len(hook.arrs)

    def body(*refs):
        p = hook.plan(refs[:n], refs[n:n + len(hook.out_shape)], refs[n + len(hook.out_shape):])
        hook.start(p)
        hook.relay(p)
        hook.finish(p)

    return pl.pallas_call(
        body, in_specs=[ANY] * n, out_specs=[ANY] * len(hook.out_shape), out_shape=hook.out_shape,
        scratch_shapes=hook.scratch, compiler_params=_comm_params(), name=name)(*hook.arrs)


def _send_other_half(parts, *, name):
    n = len(parts)

    def body(*refs):
        ins, outs = refs[:n], refs[n:2 * n]
        send_sems, recv_sems = refs[2 * n:]
        xi, yi, ci = _mesh_position()
        sibling = (xi, yi, 1 - ci)
        for i in range(n):
            for s in range(N_CHIPS):
                pltpu.make_async_remote_copy(src_ref=ins[i].at[s, 1 - ci], dst_ref=outs[i].at[s], send_sem=send_sems.at[i],
                                             recv_sem=recv_sems.at[i], device_id=sibling, device_id_type=MESH).start()
        for i in range(n):
            pltpu.make_async_remote_copy(src_ref=outs[i], dst_ref=outs[i], send_sem=send_sems.at[i], recv_sem=recv_sems.at[i],
                                         device_id=sibling, device_id_type=MESH).wait()

    return pl.pallas_call(
        body, in_specs=[ANY] * n, out_specs=[ANY] * n,
        out_shape=[jax.ShapeDtypeStruct((p.shape[0],) + p.shape[2:], p.dtype) for p in parts],
        scratch_shapes=[pltpu.SemaphoreType.DMA((n,)), pltpu.SemaphoreType.DMA((n,))],
        name=name)(*parts)


ROW_BLOCKS = 8
SUM_ROW_BLOCKS = 2


def _add_sibling_half(parts, theirs, core, *, name):
    n = len(parts)

    def body(core_ref, *refs):
        for a_ref, b_ref, o_ref in zip(refs[:n], refs[n:2 * n], refs[2 * n:]):
            o_ref[...] = (a_ref[...].astype(F32) + b_ref[...].astype(F32)).astype(o_ref.dtype)

    nb = SUM_ROW_BLOCKS
    mine = lambda p: pl.BlockSpec((None, None, p.shape[2] // nb, p.shape[3]), lambda s, rb, core_ref: (s, core_ref[0], rb, 0))
    other = lambda p: pl.BlockSpec((None, p.shape[1] // nb, p.shape[2]), lambda s, rb, core_ref: (s, rb, 0))
    return pl.pallas_call(
        body,
        grid_spec=pltpu.PrefetchScalarGridSpec(
            num_scalar_prefetch=1, grid=(N_CHIPS, nb),
            in_specs=[mine(p) for p in parts] + [other(q) for q in theirs], out_specs=[other(q) for q in theirs]),
        out_shape=[jax.ShapeDtypeStruct(q.shape, BF16) for q in theirs],
        compiler_params=_params("parallel", "parallel"), name=name)(core, *parts, *theirs)


class _ExchangeHook:
    def __init__(self, parts, to_all=()):
        self.arrs = list(parts) + list(to_all)
        self.n_parts, self.n = len(parts), len(self.arrs)
        n_ici, n_peer = max((N_CHIPS - 1) * self.n_parts, 1), (N_DEV - 1) * max(len(to_all), 1)
        dma = pltpu.SemaphoreType.DMA
        self.out_shape = [jax.ShapeDtypeStruct(p.shape, p.dtype) for p in parts] + [
            jax.ShapeDtypeStruct((N_DEV,) + a.shape, a.dtype) for a in to_all]
        self.scratch = [pltpu.VMEM(p.shape[1:], p.dtype) for p in parts] + [pltpu.VMEM(a.shape, a.dtype) for a in to_all] + [
            dma((n_ici,)), dma((n_ici,)), dma((n_peer,)), dma((n_peer,)), dma((self.n,)), dma((self.n,))]

    def plan(self, ins, outs, scratch):
        n, npt = self.n, self.n_parts
        bufs = scratch[:n]
        send_sems, recv_sems, all_send, all_recv, load_sems, store_sems = scratch[n:]
        xi, yi, ci = _mesh_position()
        me_chip = 2 * xi + yi
        me = 4 * xi + 2 * yi + ci
        loads, stores = _staged_copies([ins[i].at[me_chip] for i in range(npt)] + list(ins[npt:]),
                                       [outs[i].at[me_chip] for i in range(npt)] + [outs[i].at[me] for i in range(npt, n)],
                                       bufs, load_sems, store_sems)
        sends, recvs = [], []
        for j, (bx, by) in enumerate(OTHER_CHIPS):
            px, py = _flip(xi, bx), _flip(yi, by)
            peer = 2 * px + py
            for i in range(npt):
                k = j * npt + i
                mk = functools.partial(pltpu.make_async_remote_copy, src_ref=ins[i].at[peer], send_sem=send_sems.at[k],
                                       recv_sem=recv_sems.at[k], device_id=(px, py, ci), device_id_type=MESH)
                sends.append(mk(dst_ref=outs[i].at[me_chip]))
                recvs.append(mk(dst_ref=outs[i].at[peer]))
        for i in range(npt, n):
            for k in range(1, N_DEV):
                px, py, pc = _flip(xi, (k >> 2) & 1), _flip(yi, (k >> 1) & 1), _flip(ci, k & 1)
                slot = (i - npt) * (N_DEV - 1) + k - 1
                mk = functools.partial(pltpu.make_async_remote_copy, src_ref=ins[i], send_sem=all_send.at[slot],
                                       recv_sem=all_recv.at[slot], device_id=(px, py, pc), device_id_type=MESH)
                sends.append(mk(dst_ref=outs[i].at[me]))
                recvs.append(mk(dst_ref=outs[i].at[4 * px + 2 * py + pc]))
        return loads, stores, sends, recvs

    @staticmethod
    def start(p):
        loads, _, sends, _ = p
        for cp in loads + sends:
            cp.start()

    @staticmethod
    def relay(p):
        loads, stores, _, _ = p
        for ld, st in zip(loads, stores):
            ld.wait()
            st.start()

    @staticmethod
    def finish(p):
        _, stores, sends, recvs = p
        for cp in recvs:
            cp.wait_recv()
        for cp in sends:
            cp.wait_send()
        for st in stores:
            st.wait()


def _sum_chips(parts, *, name):
    n = len(parts)
    p = parts[0].shape[0]

    def body(*refs):
        s = pl.program_id(1)
        for x_ref, o_ref in zip(refs[:n], refs[n:]):
            @pl.when(s == 0)
            def _():
                o_ref[...] = x_ref[...].astype(F32)

            @pl.when(s > 0)
            def _():
                o_ref[...] += x_ref[...].astype(F32)

    blocks = lambda q: SUM_ROW_BLOCKS if q.shape[1] % (16 * SUM_ROW_BLOCKS) == 0 else 1
    assert len({blocks(q) for q in parts}) == 1
    nb = blocks(parts[0])
    return pl.pallas_call(
        body, grid=(nb, p),
        in_specs=[pl.BlockSpec((None, q.shape[1] // nb, q.shape[2]), lambda rb, s: (s, rb, 0)) for q in parts],
        out_specs=[pl.BlockSpec((q.shape[1] // nb, q.shape[2]), lambda rb, s: (rb, 0)) for q in parts],
        out_shape=[jax.ShapeDtypeStruct(q.shape[1:], F32) for q in parts],
        compiler_params=_params("parallel", "arbitrary"), name=name)(*parts)


def _swap_halves(halves, layers, *, name, hook=None):
    n = len(halves)
    out_shapes, slots = [], []
    for i, h in enumerate(halves):
        pair = [p for p in layers if i in p]
        if pair and pair[0][1] == i:
            slots.append((slots[pair[0][0]][0], 1))
        elif pair:
            out_shapes.append(jax.ShapeDtypeStruct((2, 2) + h.shape, h.dtype))
            slots.append((len(out_shapes) - 1, 0))
        else:
            out_shapes.append(jax.ShapeDtypeStruct((2,) + h.shape, h.dtype))
            slots.append((len(out_shapes) - 1, None))
    n_out = len(out_shapes)
    hk = _HookSlots(hook, n_in=n, n_out=n_out, n_scratch=n + 4)

    def body(*refs):
        ins, outs, scratch = hk.own(refs)
        bufs = scratch[:n]
        send_sems, recv_sems, load_sems, store_sems = scratch[n:]
        if hook is not None:
            extra = hk.plan(refs)
            hook.start(extra)
        xi, yi, ci = _mesh_position()
        own, sends, recvs = [], [], []
        for i in range(n):
            o, layer = slots[i]
            dst = (lambda core: outs[o].at[core]) if layer is None else (lambda core: outs[o].at[layer, core])
            own.append(dst(ci))
            mk = functools.partial(pltpu.make_async_remote_copy, src_ref=ins[i], send_sem=send_sems.at[i],
                                   recv_sem=recv_sems.at[i], device_id=(xi, yi, 1 - ci), device_id_type=MESH)
            sends.append(mk(dst_ref=dst(ci)))
            recvs.append(mk(dst_ref=dst(1 - ci)))
        loads, stores = _staged_copies(ins, own, bufs, load_sems, store_sems)
        for cp in loads + sends:
            cp.start()
        for ld, st in zip(loads, stores):
            ld.wait()
            st.start()
        for cp in recvs:
            cp.wait_recv()
        for cp in sends:
            cp.wait_send()
        for st in stores:
            st.wait()
        if hook is not None:
            hook.relay(extra)
            hook.finish(extra)

    outs = pl.pallas_call(
        body, in_specs=[ANY] * n + hk.in_specs, out_specs=[ANY] * n_out + hk.out_specs, out_shape=out_shapes + hk.out_shape,
        scratch_shapes=[pltpu.VMEM(h.shape, h.dtype) for h in halves]
        + [pltpu.SemaphoreType.DMA((n,)), pltpu.SemaphoreType.DMA((n,)), pltpu.SemaphoreType.DMA((n,)), pltpu.SemaphoreType.DMA((n,))]
        + hk.scratch,
        compiler_params=_comm_params(), name=name)(*halves, *hk.inputs)
    return outs if hook is None else (outs[:n_out], outs[n_out:])


def _cast_bf16(layers, x, norm_w, *, name, hook=None):
    n = len(layers)
    hk = _HookSlots(hook, n_in=n + 2, n_out=n + 1, n_scratch=0)

    def body(*refs):
        ins, outs, _ = hk.own(refs)
        if hook is not None:
            hk.run(refs, pl.program_id(0), ROW_BLOCKS)
        for i_ref, o_ref in zip(ins[:n], outs[:n]):
            o_ref[...] = i_ref[...].astype(o_ref.dtype)
        xv = ins[n][...]
        outs[n][...] = (xv * lax.rsqrt(jnp.mean(xv * xv, axis=-1, keepdims=True) + NORM_EPS) * ins[n + 1][...]).astype(BF16)

    in_blk = lambda a, l: pl.BlockSpec((None, a.shape[1] // ROW_BLOCKS, a.shape[2]), lambda i: (l, i, 0))
    out_blk = lambda a: pl.BlockSpec((a.shape[1] // ROW_BLOCKS, a.shape[2]), lambda i: (i, 0))
    x_blk = pl.BlockSpec((x.shape[0] // ROW_BLOCKS, x.shape[1]), lambda i: (i, 0))
    outs = pl.pallas_call(
        body, grid=(ROW_BLOCKS,),
        in_specs=[in_blk(a, l) for a, l in layers] + [x_blk, pl.BlockSpec((1, x.shape[1]), lambda i: (0, 0))] + hk.in_specs,
        out_specs=[out_blk(a) for a, _ in layers] + [x_blk] + hk.out_specs,
        out_shape=[jax.ShapeDtypeStruct(a.shape[1:], BF16) for a, _ in layers] + [jax.ShapeDtypeStruct(x.shape, BF16)] + hk.out_shape,
        scratch_shapes=hk.scratch,
        compiler_params=_params(*hk.semantics("parallel")), name=name)(*[a for a, _ in layers], x, norm_w, *hk.inputs)
    own = (outs[:n], outs[n])
    return own if hook is None else (own, outs[n + 1:])


def _full_weight(name, gathered):
    s, _, r, c = gathered.shape
    if name == "ssd_w_in":
        return _w_in_from_shards(gathered.reshape(s, 2 * r, c), name="ssd_w_in_unshard")
    if name in ("attn_w_qkv", "mlp_w_up0", "mlp_w_up1"):
        return gathered.reshape(s, 2 * r, c)
    return gathered.reshape(s * 2 * r, c)


class _StepComm:
    GATHER = {"in_proj": ("mlp_w_up0", "attn_w_o"), "conv": ("mlp_w_down0",), "scan": ("ssd_w_out", "mlp_w_up1"),
              "mlp_up_l0": ("attn_w_qkv",), "attn_fwd": ("mlp_w_down1",)}
    EXCHANGE = {"early": ("ssd_w_out", "attn_w_qkv", "attn_w_o", "mlp_w_up0", "mlp_w_up1", "mlp_w_down0", "mlp_w_down1"),
                "late": ("ssd_w_in",)}

    def __init__(self, shards, core):
        self.shards, self.core = shards, core
        self.chip_parts = {}
        self._pending = None

    def gather_hook(self, stage):
        names = self.GATHER.get(stage)
        return _GatherHook([self.shards[n] for n in names]) if names else None

    def weights_from(self, stage, gathered):
        return {n: _full_weight(n, g) for n, g in zip(self.GATHER[stage], gathered)}

    def chip_sums(self, mats, tag):
        parts = [_shard_halves(a) for a in mats.values()]
        theirs = _send_other_half(parts, name=f"grad_sibling_send_{tag}")
        return _add_sibling_half(parts, theirs, self.core, name=f"grad_chip_sum_{tag}")

    def exchange_hook(self, mats, which):
        self._pending = self.EXCHANGE[which]
        return _ExchangeHook(self.chip_sums({n: mats[n] for n in self._pending}, which))

    def received(self, arrays):
        self.chip_parts.update(zip(self._pending, arrays))


ADAMW_ROW_BLOCKS = 16


def _adamw(ws, gs, ms, vs, *, name, by_lanes=False):
    n = len(ws)
    if by_lanes:
        nb = min(a.shape[2] for a in ws) // LANES
    else:
        nb = ADAMW_ROW_BLOCKS if all(a.shape[1] % (8 * ADAMW_ROW_BLOCKS) == 0 for a in ws) else 1

    def body(*refs):
        ins, outs = refs[:4 * n], refs[4 * n:]
        for i in range(n):
            w_ref, g_ref, m_ref, v_ref = ins[i], ins[n + i], ins[2 * n + i], ins[3 * n + i]
            go_ref, d_ref, nm_ref, nv_ref = outs[i], outs[n + i], outs[2 * n + i], outs[3 * n + i]
            gv = g_ref[...]
            nm = ADAM_B1 * m_ref[...] + (1.0 - ADAM_B1) * gv
            nv = ADAM_B2 * v_ref[...] + (1.0 - ADAM_B2) * (gv * gv)
            m_hat = nm / (1.0 - ADAM_B1 ** ADAM_STEP)
            v_hat = nv / (1.0 - ADAM_B2 ** ADAM_STEP)
            go_ref[...] = gv
            d_ref[...] = -ADAM_LR * (m_hat / (jnp.sqrt(v_hat) + ADAM_EPS) + ADAM_WD * w_ref[...])
            nm_ref[...] = nm
            nv_ref[...] = nv

    if by_lanes:
        blks = [pl.BlockSpec((a.shape[0], a.shape[1], a.shape[2] // nb), lambda i: (0, 0, i)) for a in ws]
    else:
        blks = [pl.BlockSpec((a.shape[0], a.shape[1] // nb, a.shape[2]), lambda i: (0, i, 0)) for a in ws]
    shapes = [jax.ShapeDtypeStruct(a.shape, F32) for a in ws]
    outs = pl.pallas_call(body, grid=(nb,), in_specs=blks * 4, out_specs=blks * 4, out_shape=shapes * 4,
                          compiler_params=_params("parallel"), name=name)(*ws, *gs, *ms, *vs)
    return [tuple(outs[k * n + i] for k in range(4)) for i in range(n)]


SM_CONV_B, SM_NORM_W, SM_MIX_PRE, SM_MIX_POST, SM_FFN_PRE, SM_FFN_POST, SM_MISC, SM_CONV_W, SM_B_QKV, SM_B_O = 0, 4, 6, 8, 10, 12, 14, 16, 32, 34
SM_ROWS = 40
MISC_DT_BIAS, MISC_A_LOG, MISC_D, MISC_SINKS, MISC_LOSS = 0, 32, 64, 96, 112


def _shard_halves(a):
    c = a.shape[-1]
    return a.reshape(N_CHIPS, 2, -1, c)


def _rows(v):
    return v.reshape(-1, D_MODEL)


def _misc_row(dt_bias, a_log, d, sinks, loss):
    pad = jnp.zeros((D_MODEL - MISC_LOSS - 1,), F32)
    return jnp.concatenate([dt_bias.reshape(-1), a_log.reshape(-1), d.reshape(-1), sinks.reshape(-1), loss.reshape(1), pad]).reshape(1, D_MODEL)


def _replicated_rows(p, loss):
    return jnp.concatenate([
        _rows(p["ssd_conv_b"]), _rows(p["ssd_norm_w"]), _rows(p["mix_pre_norm"]), _rows(p["mix_post_norm"]),
        _rows(p["ffn_pre_norm"]), _rows(p["ffn_post_norm"]),
        _misc_row(p["ssd_dt_bias"], p["ssd_a_log"], p["ssd_d"], p["attn_sinks"], loss), jnp.zeros((1, D_MODEL), F32)], axis=0)


def _sharded_rows(conv_w, b_qkv, b_o):
    last = jnp.concatenate([b_qkv.reshape(-1), b_o.reshape(-1), jnp.zeros((D_MODEL - 640,), F32)]).reshape(1, D_MODEL)
    return jnp.concatenate([conv_w.reshape(SSD_CONV_WIDTH, D_MODEL), last, jnp.zeros((3, D_MODEL), F32)], axis=0)


REPLICATED = ("ssd_conv_b", "ssd_dt_bias", "ssd_a_log", "ssd_d", "ssd_norm_w", "attn_sinks",
              "mix_pre_norm", "mix_post_norm", "ffn_pre_norm", "ffn_post_norm")
MATRICES = ("ssd_w_in", "ssd_w_out", "attn_w_qkv", "attn_w_o", "mlp_w_up", "mlp_w_down")
WEIGHT_NAMES = ("ssd_w_in", "ssd_conv_w", "ssd_conv_b", "ssd_dt_bias", "ssd_a_log", "ssd_d", "ssd_norm_w", "ssd_w_out",
                "attn_w_qkv", "attn_b_qkv", "attn_sinks", "attn_w_o", "attn_b_o", "mlp_w_up", "mlp_w_down",
                "mix_pre_norm", "mix_post_norm", "ffn_pre_norm", "ffn_post_norm")


def _unpack_small(rows16, rows8, like):
    misc = rows16[SM_MISC]
    out = {
        "ssd_conv_b": rows16[SM_CONV_B:SM_CONV_B + 4], "ssd_norm_w": rows16[SM_NORM_W:SM_NORM_W + 2],
        "mix_pre_norm": rows16[SM_MIX_PRE:SM_MIX_PRE + 2], "mix_post_norm": rows16[SM_MIX_POST:SM_MIX_POST + 2],
        "ffn_pre_norm": rows16[SM_FFN_PRE:SM_FFN_PRE + 2], "ffn_post_norm": rows16[SM_FFN_POST:SM_FFN_POST + 2],
        "ssd_dt_bias": misc[MISC_DT_BIAS:MISC_DT_BIAS + 32], "ssd_a_log": misc[MISC_A_LOG:MISC_A_LOG + 32],
        "ssd_d": misc[MISC_D:MISC_D + 32], "attn_sinks": misc[MISC_SINKS:MISC_SINKS + 16],
        "ssd_conv_w": rows8[0:SSD_CONV_WIDTH], "attn_b_qkv": rows8[SSD_CONV_WIDTH, 0:384], "attn_b_o": rows8[SSD_CONV_WIDTH, 384:640],
    }
    return {k: v.reshape(like[k].shape) for k, v in out.items()}


def kernel(x, ssd_w_in, ssd_conv_w, ssd_conv_b, ssd_dt_bias, ssd_a_log, ssd_d, ssd_norm_w, ssd_w_out, attn_w_qkv, attn_b_qkv, attn_sinks, attn_w_o, attn_b_o, mlp_w_up, mlp_w_down, mix_pre_norm, mix_post_norm, ffn_pre_norm, ffn_post_norm, loss_target, m_ssd_w_in, m_ssd_conv_w, m_ssd_conv_b, m_ssd_dt_bias, m_ssd_a_log, m_ssd_d, m_ssd_norm_w, m_ssd_w_out, m_attn_w_qkv, m_attn_b_qkv, m_attn_sinks, m_attn_w_o, m_attn_b_o, m_mlp_w_up, m_mlp_w_down, m_mix_pre_norm, m_mix_post_norm, m_ffn_pre_norm, m_ffn_post_norm, v_ssd_w_in, v_ssd_conv_w, v_ssd_conv_b, v_ssd_dt_bias, v_ssd_a_log, v_ssd_d, v_ssd_norm_w, v_ssd_w_out, v_attn_w_qkv, v_attn_b_qkv, v_attn_sinks, v_attn_w_o, v_attn_b_o, v_mlp_w_up, v_mlp_w_down, v_mix_pre_norm, v_mix_post_norm, v_ffn_pre_norm, v_ffn_post_norm):
    w = dict(zip(WEIGHT_NAMES, (ssd_w_in, ssd_conv_w, ssd_conv_b, ssd_dt_bias, ssd_a_log, ssd_d, ssd_norm_w, ssd_w_out, attn_w_qkv, attn_b_qkv, attn_sinks, attn_w_o, attn_b_o, mlp_w_up, mlp_w_down, mix_pre_norm, mix_post_norm, ffn_pre_norm, ffn_post_norm)))
    m = dict(zip(WEIGHT_NAMES, (m_ssd_w_in, m_ssd_conv_w, m_ssd_conv_b, m_ssd_dt_bias, m_ssd_a_log, m_ssd_d, m_ssd_norm_w, m_ssd_w_out, m_attn_w_qkv, m_attn_b_qkv, m_attn_sinks, m_attn_w_o, m_attn_b_o, m_mlp_w_up, m_mlp_w_down, m_mix_pre_norm, m_mix_post_norm, m_ffn_pre_norm, m_ffn_post_norm)))
    v = dict(zip(WEIGHT_NAMES, (v_ssd_w_in, v_ssd_conv_w, v_ssd_conv_b, v_ssd_dt_bias, v_ssd_a_log, v_ssd_d, v_ssd_norm_w, v_ssd_w_out, v_attn_w_qkv, v_attn_b_qkv, v_attn_sinks, v_attn_w_o, v_attn_b_o, v_mlp_w_up, v_mlp_w_down, v_mix_pre_norm, v_mix_post_norm, v_ffn_pre_norm, v_ffn_post_norm)))
    chip = 2 * lax.axis_index("x") + lax.axis_index("y")

    two_halves = lambda a: a.reshape(2, a.shape[-2] // 2, a.shape[-1])
    later = {"ssd_w_out": (w["ssd_w_out"], 0), "attn_w_qkv": (w["attn_w_qkv"], 0), "attn_w_o": (w["attn_w_o"], 0),
             "mlp_w_up0": (w["mlp_w_up"], 0), "mlp_w_up1": (w["mlp_w_up"], 1),
             "mlp_w_down0": (w["mlp_w_down"], 0), "mlp_w_down1": (w["mlp_w_down"], 1)}
    first = _GatherHook([two_halves(w["ssd_w_in"].astype(BF16))], [w["ssd_conv_w"][0], w["attn_b_qkv"], w["attn_b_o"]])
    (cast, u0), (g_in, g_conv, g_bqkv, g_bo) = _cast_bf16(list(later.values()), x[0], w["mix_pre_norm"][0:1],
                                                          name="weights_to_bf16", hook=first)
    core = lax.axis_index("c").astype(jnp.int32).reshape(1)
    comm = _StepComm({k: two_halves(a) for k, a in zip(later, cast)}, core)
    full = {
        "ssd_w_in": _full_weight("ssd_w_in", g_in),
        "ssd_conv_w": g_conv.transpose(1, 0, 2).reshape(SSD_CONV_WIDTH, SSD_CONV_DIM),
        "attn_b_qkv": g_bqkv.reshape(ATTN_QKV), "attn_b_o": g_bo.reshape(D_MODEL),
    }
    for name in REPLICATED:
        full[name] = w[name][0] if name.startswith(("ssd_", "attn_")) else w[name]

    loss_tile, grad_x, gm, g = _local_step(x[0], loss_target[0], full, comm, u0)

    conv_w_rows = g["ssd_conv_w"].reshape(SSD_CONV_WIDTH * N_CHIPS, D_MODEL)
    b_qkv_rows = jnp.pad(g["attn_b_qkv"], (0, 2 * D_MODEL - ATTN_QKV)).reshape(2, D_MODEL)
    small = jnp.concatenate([_replicated_rows(g, loss_tile[0, 0]), conv_w_rows, b_qkv_rows, _rows(g["attn_b_o"]),
                             jnp.zeros((SM_ROWS - SM_B_O - 1, D_MODEL), F32)], axis=0)
    order = ("ssd_w_in", "ssd_w_out", "attn_w_qkv", "attn_w_o", "mlp_w_up0", "mlp_w_up1", "mlp_w_down0", "mlp_w_down1")
    halves = _sum_chips([comm.chip_parts[k] for k in order], name="grad_sum")
    (r_in, r_out, r_qkv, r_o, r_up, r_down), (small_all,) = _swap_halves(
        halves, layers=((4, 5), (6, 7)), hook=_ExchangeHook([], [small]), name="grad_halves_swap")
    small_sum, = _sum_chips([small_all], name="small_grad_sum")

    grads = {"ssd_w_in": r_in, "ssd_w_out": r_out, "attn_w_qkv": r_qkv, "attn_w_o": r_o, "mlp_w_up": r_up, "mlp_w_down": r_down}
    grads = {k: a.reshape(w[k].shape) for k, a in grads.items()}
    conv_w_g = lax.dynamic_index_in_dim(small_sum[SM_CONV_W:SM_CONV_W + 16].reshape(SSD_CONV_WIDTH, N_CHIPS, D_MODEL), chip, axis=1, keepdims=False)
    b_qkv_g = lax.dynamic_slice_in_dim(small_sum[SM_B_QKV:SM_B_QKV + 2].reshape(-1), chip * 384, 384)
    b_o_g = lax.dynamic_slice_in_dim(small_sum[SM_B_O], chip * 256, 256)
    small_g = jnp.concatenate([small_sum[0:16], _sharded_rows(conv_w_g, b_qkv_g, b_o_g)], axis=0)
    grads.update(_unpack_small(small_g[0:16], small_g[16:24], w))
    loss = small_sum[SM_MISC, MISC_LOSS]

    delta, new_m, new_v = {}, {}, {}
    stored = lambda a: jnp.swapaxes(a, 1, 2)
    rest = [name for name in MATRICES if name != "ssd_w_in"]
    mats = lambda p: [p[name] for name in rest]
    results = dict(zip(rest, _adamw(mats(w), mats(grads), mats(m), mats(v), name="adamw_matrices")))
    (w_in_result,) = _adamw([stored(w["ssd_w_in"])], [stored(grads["ssd_w_in"])], [stored(m["ssd_w_in"])],
                            [stored(v["ssd_w_in"])], by_lanes=True, name="adamw_ssd_w_in")
    results["ssd_w_in"] = tuple(stored(a) for a in w_in_result)
    for name in MATRICES:
        grads[name], delta[name], new_m[name], new_v[name] = results[name]
    zero = jnp.zeros((), F32)
    small_pack = lambda p: jnp.concatenate([_replicated_rows({k: p[k] for k in REPLICATED}, zero),
                                            _sharded_rows(p["ssd_conv_w"], p["attn_b_qkv"], p["attn_b_o"])], axis=0)[None]
    (_, d_s, m_s, v_s), = _adamw([small_pack(w)], [small_g[None]], [small_pack(m)], [small_pack(v)], name="adamw_vectors")
    d_s, m_s, v_s = d_s[0], m_s[0], v_s[0]
    delta.update(_unpack_small(d_s[0:16], d_s[16:24], w))
    new_m.update(_unpack_small(m_s[0:16], m_s[16:24], w))
    new_v.update(_unpack_small(v_s[0:16], v_s[16:24], w))

    return (loss, grad_x[None], *[grads[n] for n in WEIGHT_NAMES], *[delta[n] for n in WEIGHT_NAMES],
            *[new_m[n] for n in WEIGHT_NAMES], *[new_v[n] for n in WEIGHT_NAMES])
```

```python
import functools

import jax
import jax.numpy as jnp
from jax import lax
from jax.experimental import pallas as pl
from jax.experimental.pallas import tpu as pltpu

F32 = jnp.float32
BF16 = jnp.bfloat16

D_MODEL = 1024
SSD_D_INNER = 2048
SSD_HEAD_DIM = 64
SSD_N_HEADS = 32
SSD_N_GROUPS = 8
SSD_HPG = 4
SSD_D_STATE = 128
SSD_CONV_WIDTH = 4
SSD_CHUNK = 128
SSD_CONV_DIM = 4096
SSD_IN_DIM = 6176
SSD_IN_PAD = 6400
SSD_IN_TILE = 1280
SSD_DT_COL = 6144
SSD_GW = SSD_HPG * SSD_HEAD_DIM
ATTN_HEAD_DIM = 64
ATTN_N_Q = 16
ATTN_N_KV = 4
ATTN_REP = 4
ATTN_WINDOW = 128
ATTN_QKV = 1536
D_FF = 4096
NORM_EPS = 1e-6

ADAM_LR = 0.001
ADAM_B1 = 0.9
ADAM_B2 = 0.999
ADAM_EPS = 1e-08
ADAM_WD = 0.01
ADAM_STEP = 10

N_CHIPS = 4
N_DEV = 8
LANES = 128
VMEM_LIMIT = 48 * 1024 * 1024
BIG_TILE = 2048
MESH = pl.DeviceIdType.MESH


def _params(*sem):
    return pltpu.CompilerParams(dimension_semantics=sem, vmem_limit_bytes=VMEM_LIMIT)


def _dot(a, b, dims):
    return lax.dot_general(a, b, (dims, ((), ())), preferred_element_type=F32)


def _dot_nn(a, b):
    return _dot(a, b, ((1,), (0,)))


def _dot_nt(a, b):
    return _dot(a, b, ((1,), (1,)))


def _dot_tn(a, b):
    return _dot(a, b, ((0,), (0,)))


def _sigmoid(x):
    return 0.5 * jnp.tanh(0.5 * x) + 0.5


ANY = pl.BlockSpec(memory_space=pl.ANY)


class _HookSlots:
    def __init__(self, hook, n_in, n_out, n_scratch):
        self.hook = hook
        self.n_in, self.n_out, self.n_scratch = n_in, n_out, n_scratch
        self.inputs = list(hook.arrs) if hook else []
        self.out_shape = list(hook.out_shape) if hook else []
        self.scratch = list(hook.scratch) if hook else []
        self.in_specs = [ANY] * len(self.inputs)
        self.out_specs = [ANY] * len(self.out_shape)

    def _split(self, refs):
        a = self.n_in
        b = a + len(self.inputs)
        c = b + self.n_out
        d = c + len(self.out_shape)
        e = d + self.n_scratch
        return refs[:a], refs[a:b], refs[b:c], refs[c:d], refs[d:e], refs[e:]

    def own(self, refs):
        ins, _, outs, _, scratch, _ = self._split(refs)
        return ins, outs, scratch

    def plan(self, refs):
        _, h_in, _, h_out, _, h_scratch = self._split(refs)
        return self.hook.plan(h_in, h_out, h_scratch)

    def run(self, refs, step, n_steps):
        _, h_in, _, h_out, _, h_scratch = self._split(refs)
        _run_hook(self.hook, h_in, h_out, h_scratch, step, n_steps)

    def semantics(self, *sem):
        return sem if self.hook is None else ("arbitrary",) * len(sem)


def _matmul(a, b, *, mode, out_dtypes, name, epilogue=None, extras=(), tm=1024, tn=1024, tk=1024,
            b_shards=False, out_shards=False, hook=None, f32_block=None):
    f32_tail = f32_block is not None
    if b_shards:
        s, b_rows, b_cols = b.shape
        b2 = (b_rows, s * b_cols)
        if mode == "nn":
            tn = b_cols
        else:
            assert mode == "nt"
            tk = b_cols
    else:
        b2 = b.shape
    if mode == "nn":
        (m, k), (k2, n) = a.shape, b2
    elif mode == "nt":
        (m, k), (n, k2) = a.shape, b2
    else:
        (k, m), (k2, n) = a.shape, b2
    assert k == k2, (a.shape, b.shape, mode)
    tm, tn, tk = min(tm, m), min(tn, n), min(tk, k)
    assert m % tm == 0 and n % tn == 0 and k % tk == 0, (m, n, k, tm, tn, tk)
    nk = k // tk
    if mode == "tn":
        a_spec = pl.BlockSpec((tk, tm), lambda i, j, kk: (kk, i))
    else:
        a_spec = pl.BlockSpec((tm, tk), lambda i, j, kk: (i, kk))
    if b_shards and mode == "nn":
        b_spec = pl.BlockSpec((None, tk, tn), lambda i, j, kk: (j, kk, 0))
    elif b_shards:
        b_spec = pl.BlockSpec((None, tn, tk), lambda i, j, kk: (kk, j, 0))
    elif mode == "nt":
        b_spec = pl.BlockSpec((tn, tk), lambda i, j, kk: (j, kk))
    else:
        b_spec = pl.BlockSpec((tk, tn), lambda i, j, kk: (kk, j))
    dims = {"nn": ((1,), (0,)), "nt": ((1,), (1,)), "tn": ((0,), (0,))}[mode]
    ex_specs = []
    for arr, kind in extras:
        if kind == "tile":
            ex_specs.append(pl.BlockSpec((tm, tn), lambda i, j, kk: (i, j)))
        else:
            ex_specs.append(pl.BlockSpec((1, tn), lambda i, j, kk: (0, j)))
    n_ex, n_out = len(extras), len(out_dtypes)
    if epilogue is None:
        epilogue = lambda acc: (acc,)
    hk = _HookSlots(hook, n_in=2 + n_ex, n_out=n_out + f32_tail, n_scratch=0 if nk == 1 else 1)
    grid = (m // tm, n // tn, nk)

    def body(*refs):
        (a_ref, b_ref, *ex), outs, scratch = hk.own(refs)
        if hook is not None:
            step = (pl.program_id(0) * grid[1] + pl.program_id(1)) * grid[2] + pl.program_id(2)
            hk.run(refs, step, grid[0] * grid[1] * grid[2])

        def finish(acc):
            res = epilogue(acc, *[e[...] for e in ex])
            for o, r in zip(outs, res):
                o[...] = r.astype(o.dtype)
            if f32_tail:
                outs[n_out][...] = acc[:, f32_block:f32_block + LANES]

        if nk == 1:
            finish(_dot(a_ref[...], b_ref[...], dims))
        else:
            acc_ref = scratch[0]
            kk = pl.program_id(2)

            @pl.when(kk == 0)
            def _():
                acc_ref[...] = jnp.zeros_like(acc_ref)

            acc_ref[...] += _dot(a_ref[...], b_ref[...], dims)

            @pl.when(kk == nk - 1)
            def _():
                finish(acc_ref[...])

    if out_shards:
        out_spec = pl.BlockSpec((None, tm, tn), lambda i, j, kk: (j, i, 0))
        out_dims = (n // tn, m, tn)
    else:
        out_spec = pl.BlockSpec((tm, tn), lambda i, j, kk: (i, j))
        out_dims = (m, n)
    tail_specs = [pl.BlockSpec((tm, LANES), lambda i, j, kk: (i, 0))] if f32_tail else []
    tail_shapes = [jax.ShapeDtypeStruct((m, LANES), F32)] if f32_tail else []
    outs = pl.pallas_call(
        body,
        grid=grid,
        in_specs=[a_spec, b_spec] + ex_specs + hk.in_specs,
        out_specs=[out_spec for _ in out_dtypes] + tail_specs + hk.out_specs,
        out_shape=[jax.ShapeDtypeStruct(out_dims, dt) for dt in out_dtypes] + tail_shapes + hk.out_shape,
        scratch_shapes=([] if nk == 1 else [pltpu.VMEM((tm, tn), F32)]) + hk.scratch,
        compiler_params=_params(*hk.semantics("parallel", "arbitrary" if f32_tail else "parallel", "arbitrary")),
        name=name,
    )(a, b, *[arr for arr, _ in extras], *hk.inputs)
    n_own = n_out + f32_tail
    own = outs[0] if n_own == 1 else outs[:n_own]
    return own if hook is None else (own, outs[n_own:])


def _row_tile(t, want):
    return min(t, want)


def _rms_fwd(x, w, *, name, resid=None, want_u=None, target=None):
    t, d = x.shape
    tr = _row_tile(t, 1024)

    def norm(v, wv):
        return v * lax.rsqrt(jnp.mean(v * v, axis=-1, keepdims=True) + NORM_EPS) * wv

    row = pl.BlockSpec((tr, d), lambda i: (i, 0))
    vec = pl.BlockSpec((1, d), lambda i: (0, 0))
    if target is not None:
        def body(x_ref, w_ref, r_ref, t_ref, dh_ref, loss_ref):
            err = r_ref[...] + norm(x_ref[...].astype(F32), w_ref[...]) - t_ref[...]
            dh_ref[...] = err * (1.0 / d)

            @pl.when(pl.program_id(0) == 0)
            def _():
                loss_ref[...] = jnp.zeros_like(loss_ref)

            part = jnp.sum(jnp.sum(err * err, axis=1, keepdims=True), axis=0, keepdims=True) * (0.5 / d)
            loss_ref[...] += jnp.broadcast_to(part, loss_ref.shape)

        return pl.pallas_call(
            body, grid=(t // tr,), in_specs=[row, vec, row, row],
            out_specs=[row, pl.BlockSpec((8, LANES), lambda i: (0, 0))],
            out_shape=[jax.ShapeDtypeStruct((t, d), F32), jax.ShapeDtypeStruct((8, LANES), F32)],
            compiler_params=_params("arbitrary"), name=name)(x, w, resid, target)
    if resid is None:
        def body(x_ref, w_ref, o_ref):
            o_ref[...] = norm(x_ref[...].astype(F32), w_ref[...]).astype(BF16)
        ins, in_specs = (x, w), [row, vec]
        out_shape, out_specs = jax.ShapeDtypeStruct((t, d), BF16), row
    elif want_u is None:
        def body(x_ref, w_ref, r_ref, o_ref):
            o_ref[...] = r_ref[...] + norm(x_ref[...].astype(F32), w_ref[...])
        ins, in_specs = (x, w, resid), [row, vec, row]
        out_shape, out_specs = jax.ShapeDtypeStruct((t, d), F32), row
    else:
        def body(x_ref, w_ref, r_ref, w2_ref, o_ref, u_ref):
            h = r_ref[...] + norm(x_ref[...].astype(F32), w_ref[...])
            o_ref[...] = h
            u_ref[...] = norm(h, w2_ref[...]).astype(BF16)
        ins, in_specs = (x, w, resid, want_u), [row, vec, row, vec]
        out_shape = [jax.ShapeDtypeStruct((t, d), F32), jax.ShapeDtypeStruct((t, d), BF16)]
        out_specs = [row, row]
    return pl.pallas_call(body, grid=(t // tr,), in_specs=in_specs, out_specs=out_specs, out_shape=out_shape,
                          compiler_params=_params("parallel"), name=name)(*ins)


def _rms_bwd(x, w, dy, *, name, resid=None, out_dtype=F32, dx_col_sum=False):
    t, d = x.shape
    tr = _row_tile(t, 1024)
    row = pl.BlockSpec((tr, d), lambda i: (i, 0))
    vec = pl.BlockSpec((1, d), lambda i: (0, 0))
    has_res = resid is not None

    def body(x_ref, w_ref, dy_ref, *rest):
        r_ref = rest[0] if has_res else None
        dx_ref, dw_ref = rest[has_res:has_res + 2]
        xv = x_ref[...].astype(F32)
        dyv = dy_ref[...].astype(F32)
        r = lax.rsqrt(jnp.mean(xv * xv, axis=-1, keepdims=True) + NORM_EPS)
        xhat = xv * r
        dyw = dyv * w_ref[...]
        dx = r * (dyw - xhat * jnp.mean(dyw * xhat, axis=-1, keepdims=True))
        if has_res:
            dx = dx + r_ref[...]
        dx_ref[...] = dx.astype(dx_ref.dtype)

        sums = [(dw_ref, dyv * xhat)] + ([(rest[-1], dx)] if dx_col_sum else [])

        @pl.when(pl.program_id(0) == 0)
        def _():
            for acc_ref, _ in sums:
                acc_ref[...] = jnp.zeros_like(acc_ref)

        for acc_ref, rows in sums:
            acc_ref[...] += jnp.sum(rows, axis=0, keepdims=True)

    ins = (x, w, dy) + ((resid,) if has_res else ())
    in_specs = [row, vec, row] + ([row] if has_res else [])
    n_vec = 2 if dx_col_sum else 1
    return pl.pallas_call(
        body, grid=(t // tr,), in_specs=in_specs, out_specs=[row] + [vec] * n_vec,
        out_shape=[jax.ShapeDtypeStruct((t, d), out_dtype)] + [jax.ShapeDtypeStruct((1, d), F32)] * n_vec,
        compiler_params=_params("arbitrary"), name=name)(*ins)


SSD_IN_SHARD = SSD_IN_DIM // N_CHIPS


def _w_in_from_shards(shards, *, name):
    d = shards.shape[1]
    tr = 256

    def body(s_ref, o_ref):
        o_ref[:, pl.ds(SSD_DT_COL, SSD_IN_PAD - SSD_DT_COL)] = jnp.zeros((tr, SSD_IN_PAD - SSD_DT_COL), o_ref.dtype)
        for s in range(N_CHIPS):
            o_ref[:, pl.ds(SSD_IN_SHARD * s, SSD_IN_SHARD)] = s_ref[s]

    return pl.pallas_call(
        body, grid=(d // tr,), in_specs=[pl.BlockSpec((N_CHIPS, tr, SSD_IN_SHARD), lambda i: (0, i, 0))],
        out_specs=pl.BlockSpec((tr, SSD_IN_PAD), lambda i: (i, 0)),
        out_shape=jax.ShapeDtypeStruct((d, SSD_IN_PAD), shards.dtype),
        compiler_params=_params("parallel"), name=name)(shards)


def _w_in_to_shards(g, *, name):
    d = g.shape[0]
    tr = 256

    def body(g_ref, o_ref):
        for s in range(N_CHIPS):
            o_ref[s] = g_ref[:, pl.ds(SSD_IN_SHARD * s, SSD_IN_SHARD)].astype(o_ref.dtype)

    return pl.pallas_call(
        body, grid=(d // tr,), in_specs=[pl.BlockSpec((tr, SSD_IN_PAD), lambda i: (i, 0))],
        out_specs=pl.BlockSpec((N_CHIPS, tr, SSD_IN_SHARD), lambda i: (0, i, 0)),
        out_shape=jax.ShapeDtypeStruct((N_CHIPS, d, SSD_IN_SHARD), BF16),
        compiler_params=_params("parallel"), name=name)(g)


XBC_COL0 = SSD_D_INNER // LANES


def _shift_down(v, k, row_ids):
    return jnp.where(row_ids >= k, pltpu.roll(v, k, axis=0), 0.0)


def _shift_up(v, k, row_ids):
    n = v.shape[0]
    return jnp.where(row_ids < n - k, pltpu.roll(v, n - k, axis=0), 0.0)


def _conv_pre(x, w, b, row_ids):
    pre = b + w[3:4, :] * x
    for k in (1, 2, 3):
        pre = pre + w[3 - k:4 - k, :] * _shift_down(x, k, row_ids)
    return pre


def _conv_fwd(zx, conv_w, conv_b, *, name, hook=None):
    t = zx.shape[0]
    cw = 2 * LANES
    nct = SSD_CONV_DIM // cw
    col0 = SSD_D_INNER // cw
    hk = _HookSlots(hook, n_in=3, n_out=1, n_scratch=0)

    def body(*refs):
        (x_ref, w_ref, b_ref), (o_ref,), _ = hk.own(refs)
        if hook is not None:
            hk.run(refs, pl.program_id(0), nct)
        x = x_ref[...].astype(F32)
        row_ids = lax.broadcasted_iota(jnp.int32, x.shape, 0)
        pre = _conv_pre(x, w_ref[...], b_ref[...], row_ids)
        o_ref[...] = pre * _sigmoid(pre)

    outs = pl.pallas_call(
        body, grid=(nct,),
        in_specs=[pl.BlockSpec((t, cw), lambda j: (0, col0 + j)),
                  pl.BlockSpec((SSD_CONV_WIDTH, cw), lambda j: (0, j)),
                  pl.BlockSpec((1, cw), lambda j: (0, j))] + hk.in_specs,
        out_specs=[pl.BlockSpec((t, cw), lambda j: (0, j))] + hk.out_specs,
        out_shape=[jax.ShapeDtypeStruct((t, SSD_CONV_DIM), F32)] + hk.out_shape,
        scratch_shapes=hk.scratch,
        compiler_params=_params(*hk.semantics("parallel")), name=name)(zx, conv_w, conv_b, *hk.inputs)
    return outs[0] if hook is None else (outs[0], outs[1:])


def _conv_bwd(zx, conv_w, conv_b, d_xs, d_bm, d_cm, dzx, *, name):
    t = zx.shape[0]
    nct = SSD_CONV_DIM // LANES
    n_xs = SSD_D_INNER // LANES
    n_bm = SSD_N_GROUPS * SSD_D_STATE // LANES

    def body(x_ref, w_ref, b_ref, dxs_ref, dbm_ref, dcm_ref, _, dx_ref, dw_ref, db_ref):
        x = x_ref[...].astype(F32)
        w = w_ref[...]
        j = pl.program_id(0)
        dy = jnp.where(j < n_xs, dxs_ref[...], jnp.where(j < n_xs + n_bm, dbm_ref[...], dcm_ref[...]))
        row_ids = lax.broadcasted_iota(jnp.int32, x.shape, 0)
        pre = _conv_pre(x, w, b_ref[...], row_ids)
        sg = _sigmoid(pre)
        dpre = dy * (sg * (1.0 + pre * (1.0 - sg)))
        dx = w[3:4, :] * dpre
        for k in (1, 2, 3):
            dx = dx + w[3 - k:4 - k, :] * _shift_up(dpre, k, row_ids)
        dx_ref[...] = dx.astype(dx_ref.dtype)
        db_ref[...] = jnp.sum(dpre, axis=0, keepdims=True)
        dw_ref[3:4, :] = jnp.sum(dpre * x, axis=0, keepdims=True)
        for k in (1, 2, 3):
            dw_ref[3 - k:4 - k, :] = jnp.sum(dpre * _shift_down(x, k, row_ids), axis=0, keepdims=True)

    clip = lambda j, lo, n: jnp.clip(j - lo, 0, n - 1)
    return pl.pallas_call(
        body, grid=(nct,),
        in_specs=[pl.BlockSpec((t, LANES), lambda j: (0, XBC_COL0 + j)),
                  pl.BlockSpec((SSD_CONV_WIDTH, LANES), lambda j: (0, j)),
                  pl.BlockSpec((1, LANES), lambda j: (0, j)),
                  pl.BlockSpec((t, LANES), lambda j: (0, clip(j, 0, n_xs))),
                  pl.BlockSpec((t, LANES), lambda j: (0, clip(j, n_xs, n_bm))),
                  pl.BlockSpec((t, LANES), lambda j: (0, clip(j, n_xs + n_bm, n_bm))), ANY],
        out_specs=[pl.BlockSpec((t, LANES), lambda j: (0, XBC_COL0 + j)),
                   pl.BlockSpec((SSD_CONV_WIDTH, LANES), lambda j: (0, j)), pl.BlockSpec((1, LANES), lambda j: (0, j))],
        out_shape=[jax.ShapeDtypeStruct(dzx.shape, dzx.dtype),
                   jax.ShapeDtypeStruct((SSD_CONV_WIDTH, SSD_CONV_DIM), F32),
                   jax.ShapeDtypeStruct((1, SSD_CONV_DIM), F32)],
        input_output_aliases={6: 0},
        compiler_params=_params("parallel"), name=name)(zx, conv_w, conv_b, d_xs, d_bm, d_cm, dzx)


def _softplus_fwd(dt_raw, bias_row, alog_row, *, name):
    t = dt_raw.shape[0]
    q = SSD_CHUNK
    tr = _row_tile(t, 1024)

    def body(x_ref, b_ref, al_ref, dt_ref, cum_ref):
        v = x_ref[...] + b_ref[...]
        e = jnp.exp(-jnp.abs(v))
        u = 1.0 + e
        log1p = jnp.where(u == 1.0, e, jnp.log(u) * (e / (u - 1.0)))
        dt = jnp.maximum(v, 0.0) + log1p
        a = dt * -jnp.exp(al_ref[...])
        lower = (lax.broadcasted_iota(jnp.int32, (q, q), 1) <= lax.broadcasted_iota(jnp.int32, (q, q), 0)).astype(F32)
        cums = [lax.dot_general(lower, a[c * q:(c + 1) * q, :], ((((1,), (0,))), ((), ())), precision=lax.Precision.HIGHEST,
                                preferred_element_type=F32) for c in range(tr // q)]
        dt_t, cum_t = dt.T, jnp.concatenate(cums, axis=0).T
        for g in range(SSD_N_GROUPS):
            rows = slice(g * SSD_HPG, (g + 1) * SSD_HPG)
            dt_ref[g] = dt_t[rows, :]
            cum_ref[g] = cum_t[rows, :]

    vec = pl.BlockSpec((1, LANES), lambda i: (0, 0))
    by_group = pl.BlockSpec((SSD_N_GROUPS, SSD_HPG, tr), lambda i: (0, 0, i))
    return pl.pallas_call(
        body, grid=(t // tr,),
        in_specs=[pl.BlockSpec((tr, LANES), lambda i: (i, 0)), vec, vec],
        out_specs=[by_group, by_group],
        out_shape=[jax.ShapeDtypeStruct((SSD_N_GROUPS, SSD_HPG, t), F32)] * 2,
        compiler_params=_params("parallel"), name=name)(dt_raw, bias_row, alog_row)


def _softplus_bwd(dt_raw, bias_row, ddt_rows, dzx, *, name):
    t = dt_raw.shape[0]
    tr = _row_tile(t, 1024)
    tail = SSD_IN_PAD - SSD_DT_COL

    def body(x_ref, b_ref, g_ref, _, o_ref, db_ref):
        v = x_ref[...] + b_ref[...]
        lane = lax.broadcasted_iota(jnp.int32, v.shape, 1)
        by_head = jnp.concatenate([g_ref[g] for g in range(SSD_N_GROUPS)]
                                  + [jnp.zeros((LANES - SSD_N_HEADS, tr), F32)], axis=0)
        d = jnp.where(lane < SSD_N_HEADS, by_head.T * _sigmoid(v), 0.0)
        o_ref[:, pl.ds(0, LANES)] = d.astype(o_ref.dtype)
        o_ref[:, pl.ds(LANES, tail - LANES)] = jnp.zeros((tr, tail - LANES), o_ref.dtype)

        @pl.when(pl.program_id(0) == 0)
        def _():
            db_ref[...] = jnp.zeros_like(db_ref)

        db_ref[...] += jnp.sum(d, axis=0, keepdims=True)

    return pl.pallas_call(
        body, grid=(t // tr,),
        in_specs=[pl.BlockSpec((tr, LANES), lambda i: (i, 0)), pl.BlockSpec((1, LANES), lambda i: (0, 0)),
                  pl.BlockSpec((SSD_N_GROUPS, SSD_HPG, tr), lambda i: (0, 0, i)), ANY],
        out_specs=[pl.BlockSpec((tr, tail), lambda i: (i, SSD_DT_COL // tail)), pl.BlockSpec((1, LANES), lambda i: (0, 0))],
        out_shape=[jax.ShapeDtypeStruct(dzx.shape, dzx.dtype), jax.ShapeDtypeStruct((1, LANES), F32)],
        input_output_aliases={3: 0},
        compiler_params=_params("arbitrary"), name=name)(dt_raw, bias_row, ddt_rows, dzx)


def _ssd_masks():
    q = SSD_CHUNK
    tt = lax.broadcasted_iota(jnp.int32, (q, q), 0)
    ss = lax.broadcasted_iota(jnp.int32, (q, q), 1)
    lane = lax.broadcasted_iota(jnp.int32, (1, SSD_GW), 1)
    srow = lax.broadcasted_iota(jnp.int32, (SSD_GW, 1), 0)
    hm = [(lane >= SSD_HEAD_DIM * j) & (lane < SSD_HEAD_DIM * (j + 1)) for j in range(SSD_HPG)]
    rm = [(srow >= SSD_HEAD_DIM * j) & (srow < SSD_HEAD_DIM * (j + 1)) for j in range(SSD_HPG)]
    return tt, ss, hm, rm


def _ssd_head_terms(dt_rows, cum_rows, a_rows, j, tt, ss):
    q = SSD_CHUNK
    dt_row = dt_rows[j:j + 1, :]
    dt_col = jnp.sum(jnp.where(tt == ss, dt_row, 0.0), axis=1, keepdims=True)
    a_row1 = a_rows[j:j + 1, :]
    a_11 = a_rows[j:j + 1, 0:1]
    cum_col = jnp.sum(jnp.where(ss <= tt, dt_row * a_row1, 0.0), axis=1, keepdims=True)
    cum_row = cum_rows[j:j + 1, :]
    decay = jnp.exp(jnp.where(ss <= tt, cum_col - cum_row, -jnp.inf))
    cum_last = cum_col[q - 1:q, :]
    e_col = jnp.exp(cum_col)
    dte_col = jnp.exp(cum_last - cum_col)
    e_last = jnp.exp(cum_last)
    return dt_col, dt_row, a_row1, a_11, decay, e_col, dte_col, e_last


SSD_CHUNKS_PER_STEP = 8
SSD_BC_COL0 = SSD_D_INNER // SSD_D_STATE


def _ssd_head_selects(terms, hm, rm):
    e_all = jnp.zeros((SSD_CHUNK, SSD_GW), F32)
    w_all = jnp.zeros((SSD_CHUNK, SSD_GW), F32)
    e_s = jnp.zeros((SSD_GW, 1), F32)
    for j in range(SSD_HPG):
        dt_col, _, _, _, _, e_col, dte_col, e_last = terms[j]
        e_all = jnp.where(hm[j], e_col, e_all)
        w_all = jnp.where(hm[j], dt_col * dte_col, w_all)
        e_s = jnp.where(rm[j], e_last, e_s)
    return e_all, w_all, e_s


def _ssd_fwd(xc, dtr, cumr, alog_b, d_b, *, name, hook=None):
    t = xc.shape[0]
    q = SSD_CHUNK
    nc = t // q
    kc = min(SSD_CHUNKS_PER_STEP, nc)
    rows = kc * q
    hk = _HookSlots(hook, n_in=7, n_out=2, n_scratch=1)

    def body(*refs):
        (x_ref, b_ref, c_ref, dtr_ref, cumr_ref, alog_ref, d_ref), (y_ref, st_ref), (s_scr,) = hk.own(refs)
        if hook is not None:
            hk.run(refs, pl.program_id(0) * (nc // kc) + pl.program_id(1), SSD_N_GROUPS * (nc // kc))

        @pl.when(pl.program_id(1) == 0)
        def _():
            s_scr[...] = jnp.zeros_like(s_scr)

        tt, ss, hm, rm = _ssd_masks()
        a_rows = -jnp.exp(alog_ref[...])
        d_rows = d_ref[...]
        d_all = jnp.zeros((1, SSD_GW), F32)
        for j in range(SSD_HPG):
            d_all = jnp.where(hm[j], d_rows[j:j + 1, 0:1], d_all)
        ks, hs = range(kc), range(SSD_HPG)
        sl = [pl.ds(k * q, q) for k in ks]
        x = [x_ref[sl[k], :] for k in ks]
        bm = [b_ref[sl[k], :].astype(BF16) for k in ks]
        cm = [c_ref[sl[k], :].astype(BF16) for k in ks]
        xb = [x[k].astype(BF16) for k in ks]
        terms = [[_ssd_head_terms(dtr_ref[:, sl[k]], cumr_ref[:, sl[k]], a_rows, j, tt, ss) for j in hs] for k in ks]
        g = [_dot_nt(cm[k], bm[k]) for k in ks]
        m = [[(g[k] * terms[k][j][4] * terms[k][j][1]).astype(BF16) for j in hs] for k in ks]
        yj = [[_dot_nn(m[k][j], xb[k]) for j in hs] for k in ks]
        sel = [_ssd_head_selects(terms[k], hm, rm) for k in ks]
        upd = [_dot_tn((x[k] * sel[k][1]).astype(BF16), bm[k]) for k in ks]
        states = [s_scr[...]]
        for k in ks:
            states.append(states[k] * sel[k][2] + upd[k])
        inter = [_dot_nt(cm[k], states[k].astype(BF16)) for k in ks]
        ys = []
        for k in ks:
            y = jnp.zeros((q, SSD_GW), F32)
            for j in hs:
                y = jnp.where(hm[j], yj[k][j], y)
            ys.append(y + inter[k] * sel[k][0] + x[k] * d_all)
        for k in ks:
            st_ref[k] = states[k]
        y_ref[...] = jnp.concatenate(ys, axis=0).astype(y_ref.dtype)
        s_scr[...] = states[kc]

    blk = lambda width, off: pl.BlockSpec((rows, width), lambda g, c: (c, off + g))
    par_s = pl.BlockSpec((None, SSD_HPG, LANES), lambda g, c: (g, 0, 0))
    row_s = pl.BlockSpec((None, SSD_HPG, rows), lambda g, c: (g, 0, c))
    outs = pl.pallas_call(
        body, grid=(SSD_N_GROUPS, nc // kc),
        in_specs=[blk(SSD_GW, 0), blk(SSD_D_STATE, SSD_BC_COL0), blk(SSD_D_STATE, SSD_BC_COL0 + SSD_N_GROUPS),
                  row_s, row_s, par_s, par_s] + hk.in_specs,
        out_specs=[blk(SSD_GW, 0), pl.BlockSpec((None, kc, SSD_GW, SSD_D_STATE), lambda g, c: (g, c, 0, 0))] + hk.out_specs,
        out_shape=[jax.ShapeDtypeStruct((t, SSD_D_INNER), BF16),
                   jax.ShapeDtypeStruct((SSD_N_GROUPS, nc, SSD_GW, SSD_D_STATE), F32)] + hk.out_shape,
        scratch_shapes=[pltpu.VMEM((SSD_GW, SSD_D_STATE), F32)] + hk.scratch,
        compiler_params=_params(*hk.semantics("parallel", "arbitrary")), name=name)(
            xc, xc, xc, dtr, cumr, alog_b, d_b, *hk.inputs)
    return outs if hook is None else (outs[:2], outs[2:])


def _ssd_bwd(xc, dtr, cumr, alog_b, d_b, states, dy, *, name, hook=None):
    t = xc.shape[0]
    q = SSD_CHUNK
    nc = t // q
    kc = min(SSD_CHUNKS_PER_STEP, nc)
    nst = nc // kc
    rows = kc * q
    rev = lambda c: nst - 1 - c
    hk = _HookSlots(hook, n_in=9, n_out=5, n_scratch=1)

    def body(*refs):
        ((x_ref, b_ref, c_ref, dtr_ref, cumr_ref, alog_ref, d_ref, st_ref, dy_ref),
         (dx_ref, db_ref, dc_ref, ddt_ref, dpar_ref), (ds_scr,)) = hk.own(refs)
        if hook is not None:
            hk.run(refs, pl.program_id(0) * nst + pl.program_id(1), SSD_N_GROUPS * nst)

        @pl.when(pl.program_id(1) == 0)
        def _():
            ds_scr[...] = jnp.zeros_like(ds_scr)
            dpar_ref[...] = jnp.zeros_like(dpar_ref)

        tt, ss, hm, rm = _ssd_masks()
        tcol = lax.broadcasted_iota(jnp.int32, (q, 1), 0)
        lane = lax.broadcasted_iota(jnp.int32, (1, LANES), 1)
        a_rows = -jnp.exp(alog_ref[...])
        d_rows = d_ref[...]
        d_all = jnp.zeros((1, SSD_GW), F32)
        for j in range(SSD_HPG):
            d_all = jnp.where(hm[j], d_rows[j:j + 1, 0:1], d_all)
        ks, hs = range(kc), range(SSD_HPG)
        sl = [pl.ds(k * q, q) for k in ks]
        x = [x_ref[sl[k], :] for k in ks]
        dyv = [dy_ref[sl[k], :].astype(F32) for k in ks]
        bm = [b_ref[sl[k], :].astype(BF16) for k in ks]
        cm = [c_ref[sl[k], :].astype(BF16) for k in ks]
        s_in = [st_ref[k] for k in ks]
        xb = [x[k].astype(BF16) for k in ks]
        dyb = [dyv[k].astype(BF16) for k in ks]
        s_b = [s_in[k].astype(BF16) for k in ks]
        terms = [[_ssd_head_terms(dtr_ref[:, sl[k]], cumr_ref[:, sl[k]], a_rows, j, tt, ss) for j in hs] for k in ks]
        sel = [_ssd_head_selects(terms[k], hm, rm) for k in ks]
        e_all, w_all, e_s = [s_[0] for s_ in sel], [s_[1] for s_ in sel], [s_[2] for s_ in sel]
        dye = [(dyv[k] * e_all[k]).astype(BF16) for k in ks]
        ds_loc = [_dot_tn(dye[k], cm[k]) for k in ks]
        ds = [None] * kc
        running = ds_scr[...]
        for k in reversed(ks):
            ds[k] = running
            running = running * e_s[k] + ds_loc[k]
        ds_scr[...] = running
        ds_b = [ds[k].astype(BF16) for k in ks]
        g = [_dot_nt(cm[k], bm[k]) for k in ks]
        cs = [_dot_nt(cm[k], s_b[k]) for k in ks]
        bds = [_dot_nt(bm[k], ds_b[k]) for k in ks]
        dm = [[_dot_nt(jnp.where(hm[j], dyv[k], 0.0).astype(BF16), xb[k]) for j in hs] for k in ks]
        gl = [[g[k] * terms[k][j][4] for j in hs] for k in ks]
        wp = [[dm[k][j] * gl[k][j] for j in hs] for k in ks]
        mt = [[(gl[k][j] * terms[k][j][1]).astype(BF16) for j in hs] for k in ks]
        dxj = [[_dot_tn(mt[k][j], dyb[k]) for j in hs] for k in ks]
        dg = []
        for k in ks:
            acc = jnp.zeros((q, q), F32)
            for j in hs:
                acc = acc + dm[k][j] * terms[k][j][4] * terms[k][j][1]
            dg.append(acc.astype(BF16))
        dy_cs = [dyv[k] * cs[k] for k in ks]
        x_bds = [x[k] * bds[k] for k in ks]
        dy_x = [dyv[k] * x[k] for k in ks]
        ds_s = [ds[k] * s_in[k] for k in ks]
        w = [[wp[k][j] * terms[k][j][1] for j in hs] for k in ks]
        rw_col = [[jnp.sum(w[k][j], axis=1, keepdims=True) for j in hs] for k in ks]
        cw_row = [[jnp.sum(w[k][j], axis=0, keepdims=True) for j in hs] for k in ks]
        cwp_row = [[jnp.sum(wp[k][j], axis=0, keepdims=True) for j in hs] for k in ks]
        r1_col = [[jnp.sum(jnp.where(hm[j], dy_cs[k], 0.0), axis=1, keepdims=True) * terms[k][j][5] for j in hs] for k in ks]
        dw_col = [[jnp.sum(jnp.where(hm[j], x_bds[k], 0.0), axis=1, keepdims=True) for j in hs] for k in ks]
        head_rows = [slice(j * SSD_HEAD_DIM, (j + 1) * SSD_HEAD_DIM) for j in hs]
        lane_sum = lambda v: jnp.sum(v, axis=1, keepdims=True)
        s_sum = [[lane_sum(jnp.sum(ds_s[k][head_rows[j], :], axis=0, keepdims=True)) for j in hs] for k in ks]
        dy_x_cols = [jnp.sum(dy_x[k], axis=0, keepdims=True) for k in ks]
        d_d = [[lane_sum(jnp.where(hm[j], dy_x_cols[k], 0.0)) for j in hs] for k in ks]
        ddt_rows = [[None] * SSD_HPG for _ in ks]
        dpar = [jnp.zeros((1, LANES), F32) for _ in hs]
        for k in ks:
            for j in hs:
                dt_col, dt_row, a_row1, a_11, _, _, dte_col, e_last = terms[k][j]
                dww = dw_col[k][j] * (dt_col * dte_col)
                last_add = jnp.sum(dww, axis=0, keepdims=True) + e_last * s_sum[k][j]
                dcum_col = rw_col[k][j] + r1_col[k][j] - dww + jnp.where(tcol == q - 1, last_add, 0.0)
                da_row = jnp.sum(jnp.where(tt >= ss, dcum_col, 0.0), axis=0, keepdims=True)
                da_col = jnp.sum(jnp.where(ss >= tt, -cw_row[k][j], 0.0), axis=1, keepdims=True)
                ddt_col = a_11 * da_col + dw_col[k][j] * dte_col
                ddt_rows[k][j] = (a_row1 * da_row + cwp_row[k][j]
                                  + jnp.sum(jnp.where(tt == ss, ddt_col, 0.0), axis=0, keepdims=True))
                d_a = jnp.sum(dt_row * da_row, axis=1, keepdims=True) + jnp.sum(dt_col * da_col, axis=0, keepdims=True)
                dpar[j] = dpar[j] + jnp.where(lane == 0, d_a * a_11, 0.0) + jnp.where(lane == 1, d_d[k][j], 0.0)
        dxs = []
        for k in ks:
            acc = jnp.zeros((q, SSD_GW), F32)
            for j in hs:
                acc = jnp.where(hm[j], dxj[k][j], acc)
            dxs.append(acc + w_all[k] * bds[k] + d_all * dyv[k])
        xw = [(x[k] * w_all[k]).astype(BF16) for k in ks]
        dc = [_dot_nn(dg[k], bm[k]) + _dot_nn(dye[k], s_b[k]) for k in ks]
        db = [_dot_tn(dg[k], cm[k]) + _dot_nn(xw[k], ds_b[k]) for k in ks]
        dx_ref[...] = jnp.concatenate(dxs, axis=0)
        dc_ref[...] = jnp.concatenate(dc, axis=0)
        db_ref[...] = jnp.concatenate(db, axis=0)
        ddt_ref[...] = jnp.concatenate([jnp.concatenate([ddt_rows[k][j] for k in ks], axis=1) for j in hs], axis=0)
        dpar_ref[...] += jnp.concatenate(dpar, axis=0)

    blk = lambda width, off: pl.BlockSpec((rows, width), lambda g, c: (rev(c), off + g))
    par_s = pl.BlockSpec((None, SSD_HPG, LANES), lambda g, c: (g, 0, 0))
    outs = pl.pallas_call(
        body, grid=(SSD_N_GROUPS, nst),
        in_specs=[blk(SSD_GW, 0), blk(SSD_D_STATE, SSD_BC_COL0), blk(SSD_D_STATE, SSD_BC_COL0 + SSD_N_GROUPS),
                  pl.BlockSpec((None, SSD_HPG, rows), lambda g, c: (g, 0, rev(c))),
                  pl.BlockSpec((None, SSD_HPG, rows), lambda g, c: (g, 0, rev(c))), par_s, par_s,
                  pl.BlockSpec((None, kc, SSD_GW, SSD_D_STATE), lambda g, c: (g, rev(c), 0, 0)), blk(SSD_GW, 0)] + hk.in_specs,
        out_specs=[blk(SSD_GW, 0), blk(SSD_D_STATE, 0), blk(SSD_D_STATE, 0),
                   pl.BlockSpec((None, SSD_HPG, rows), lambda g, c: (g, 0, rev(c))), par_s] + hk.out_specs,
        out_shape=[jax.ShapeDtypeStruct((t, SSD_D_INNER), F32),
                   jax.ShapeDtypeStruct((t, SSD_N_GROUPS * SSD_D_STATE), F32),
                   jax.ShapeDtypeStruct((t, SSD_N_GROUPS * SSD_D_STATE), F32),
                   jax.ShapeDtypeStruct((SSD_N_GROUPS, SSD_HPG, t), F32),
                   jax.ShapeDtypeStruct((SSD_N_GROUPS, SSD_HPG, LANES), F32)] + hk.out_shape,
        scratch_shapes=[pltpu.VMEM((SSD_GW, SSD_D_STATE), F32)] + hk.scratch,
        compiler_params=_params(*hk.semantics("parallel", "arbitrary")), name=name)(
            xc, xc, xc, dtr, cumr, alog_b, d_b, states, dy, *hk.inputs)
    return outs if hook is None else (outs[:5], outs[5:])


def _gate_norm_fwd(y, zx, norm_w, *, name):
    t = y.shape[0]
    tr = _row_tile(t, 512)
    row = pl.BlockSpec((tr, SSD_D_INNER), lambda i: (i, 0))

    def body(y_ref, z_ref, w_ref, o_ref):
        for gi in range(SSD_N_GROUPS):
            sl = pl.ds(gi * SSD_GW, SSD_GW)
            z = z_ref[:, sl].astype(F32)
            gv = y_ref[:, sl].astype(F32) * (z * _sigmoid(z))
            r = lax.rsqrt(jnp.mean(gv * gv, axis=-1, keepdims=True) + NORM_EPS)
            o_ref[:, sl] = (gv * r * w_ref[:, sl]).astype(BF16)

    return pl.pallas_call(
        body, grid=(t // tr,), in_specs=[row, row, pl.BlockSpec((1, SSD_D_INNER), lambda i: (0, 0))],
        out_specs=row, out_shape=jax.ShapeDtypeStruct((t, SSD_D_INNER), BF16),
        compiler_params=_params("parallel"), name=name)(y, zx, norm_w)


def _gate_norm_bwd(y, zx, norm_w, dyn, *, name):
    t = y.shape[0]
    tr = _row_tile(t, 512)
    row = pl.BlockSpec((tr, SSD_D_INNER), lambda i: (i, 0))
    vec = pl.BlockSpec((1, SSD_D_INNER), lambda i: (0, 0))

    def body(y_ref, z_ref, w_ref, dyn_ref, dy_ref, dz_ref, dw_ref):
        @pl.when(pl.program_id(0) == 0)
        def _():
            dw_ref[...] = jnp.zeros_like(dw_ref)

        for gi in range(SSD_N_GROUPS):
            sl = pl.ds(gi * SSD_GW, SSD_GW)
            z = z_ref[:, sl].astype(F32)
            yv = y_ref[:, sl].astype(F32)
            sg = _sigmoid(z)
            sz = z * sg
            gv = yv * sz
            r = lax.rsqrt(jnp.mean(gv * gv, axis=-1, keepdims=True) + NORM_EPS)
            ghat = gv * r
            dout = dyn_ref[:, sl].astype(F32)
            dgh = dout * w_ref[:, sl]
            dgv = r * (dgh - ghat * jnp.mean(dgh * ghat, axis=-1, keepdims=True))
            dy_ref[:, sl] = (dgv * sz).astype(dy_ref.dtype)
            dz_ref[:, sl] = (dgv * yv * (sg * (1.0 + z * (1.0 - sg)))).astype(dz_ref.dtype)
            dw_ref[:, sl] += jnp.sum(dout * ghat, axis=0, keepdims=True)

    return pl.pallas_call(
        body, grid=(t // tr,), in_specs=[row, row, vec, row], out_specs=[row, row, vec],
        out_shape=[jax.ShapeDtypeStruct((t, SSD_D_INNER), BF16), jax.ShapeDtypeStruct((t, SSD_IN_PAD), BF16),
                   jax.ShapeDtypeStruct((1, SSD_D_INNER), F32)],
        compiler_params=_params("arbitrary"), name=name)(y, zx, norm_w, dyn)


ATTN_KV_W = ATTN_N_KV * ATTN_HEAD_DIM
ATTN_Q_HALF = 512
ATTN_K_BLK = ATTN_N_Q * ATTN_HEAD_DIM // ATTN_KV_W
ATTN_V_BLK = ATTN_K_BLK + 1


def _attn_valid(first_block):
    w = ATTN_WINDOW
    qpos = lax.broadcasted_iota(jnp.int32, (w, 2 * w), 0) + w
    kpos = lax.broadcasted_iota(jnp.int32, (w, 2 * w), 1)
    rel = qpos - kpos
    return (rel >= 0) & (rel < w) & jnp.logical_not(first_block & (kpos < w))


def _attn_head_views(lo_ref, hi_ref):
    hd = ATTN_HEAD_DIM
    per_half = ATTN_Q_HALF // hd
    return [(lo_ref if h < per_half else hi_ref)[:, pl.ds((h % per_half) * hd, hd)] for h in range(ATTN_N_Q)]


def _attn_block_views(lo_ref, hi_ref, kc_ref, kp_ref, vc_ref, vp_ref):
    hd = ATTN_HEAD_DIM
    kv_cols = [pl.ds(kh * hd, hd) for kh in range(ATTN_N_KV)]
    kb = [jnp.concatenate([kp_ref[:, c], kc_ref[:, c]], axis=0) for c in kv_cols]
    vb = [jnp.concatenate([vp_ref[:, c], vc_ref[:, c]], axis=0) for c in kv_cols]
    return _attn_head_views(lo_ref, hi_ref), kb, vb


def _attn_scores(q, kb, valid):
    scale = ATTN_HEAD_DIM ** -0.5
    return [jnp.where(valid, _dot_nt(q[h], kb[h // ATTN_REP]) * scale, -jnp.inf) for h in range(ATTN_N_Q)]


def _attn_softmax(s, sink):
    heads = range(ATTN_N_Q)
    m = [jnp.maximum(jnp.max(s[h], axis=1, keepdims=True), sink[h]) for h in heads]
    e = [jnp.exp(s[h] - m[h]) for h in heads]
    es = [jnp.exp(sink[h] - m[h]) for h in heads]
    inv = [1.0 / (jnp.sum(e[h], axis=1, keepdims=True) + es[h]) for h in heads]
    return e, es, inv


def _attn_fwd(qkv, sinks_b, *, name, hook=None):
    t = qkv.shape[0]
    w = ATTN_WINDOW
    nb = t // w
    prev = lambda n: jnp.maximum(n - 1, 0)
    hk = _HookSlots(hook, n_in=7, n_out=1, n_scratch=0)

    def body(*refs):
        (qlo_ref, qhi_ref, kc_ref, kp_ref, vc_ref, vp_ref, sink_ref), (o_ref,), _ = hk.own(refs)
        if hook is not None:
            hk.run(refs, pl.program_id(0), nb)
        heads = range(ATTN_N_Q)
        q, kb, vb = _attn_block_views(qlo_ref, qhi_ref, kc_ref, kp_ref, vc_ref, vp_ref)
        sink = [sink_ref[h:h + 1, 0:1] for h in heads]
        e, _, inv = _attn_softmax(_attn_scores(q, kb, _attn_valid(pl.program_id(0) == 0)), sink)
        out = [_dot_nn((e[h] * inv[h]).astype(BF16), vb[h // ATTN_REP]).astype(o_ref.dtype) for h in heads]
        o_ref[...] = jnp.concatenate(out, axis=1)

    qh = lambda half: pl.BlockSpec((w, ATTN_Q_HALF), lambda n: (n, half))
    kv = lambda blk, idx: pl.BlockSpec((w, ATTN_KV_W), lambda n: (idx(n), blk))
    cur = lambda n: n
    outs = pl.pallas_call(
        body, grid=(nb,),
        in_specs=[qh(0), qh(1), kv(ATTN_K_BLK, cur), kv(ATTN_K_BLK, prev), kv(ATTN_V_BLK, cur), kv(ATTN_V_BLK, prev),
                  pl.BlockSpec((ATTN_N_Q, LANES), lambda n: (0, 0))] + hk.in_specs,
        out_specs=[pl.BlockSpec((w, D_MODEL), lambda n: (n, 0))] + hk.out_specs,
        out_shape=[jax.ShapeDtypeStruct((t, D_MODEL), BF16)] + hk.out_shape,
        scratch_shapes=hk.scratch,
        compiler_params=_params(*hk.semantics("parallel")), name=name)(qkv, qkv, qkv, qkv, qkv, qkv, sinks_b, *hk.inputs)
    return outs[0] if hook is None else (outs[0], outs[1:])


def _attn_bwd(qkv, sinks_b, dout, *, name):
    t = qkv.shape[0]
    w = ATTN_WINDOW
    nb = t // w
    hd = ATTN_HEAD_DIM
    clamp = lambda n: jnp.minimum(n, nb - 1)
    prev = lambda n: jnp.maximum(clamp(n) - 1, 0)

    def body(qlo_ref, qhi_ref, kc_ref, kp_ref, vc_ref, vp_ref, sink_ref, dolo_ref, dohi_ref,
             dq_ref, dkv_ref, dsink_ref, dbias_ref, carry):
        n = pl.program_id(0)

        @pl.when(n == 0)
        def _():
            carry[...] = jnp.zeros_like(carry)
            dsink_ref[...] = jnp.zeros_like(dsink_ref)
            dbias_ref[...] = jnp.zeros_like(dbias_ref)

        @pl.when(n < nb)
        def _():
            heads, kvs = range(ATTN_N_Q), range(ATTN_N_KV)
            q, kb, vb = _attn_block_views(qlo_ref, qhi_ref, kc_ref, kp_ref, vc_ref, vp_ref)
            do = _attn_head_views(dolo_ref, dohi_ref)
            sink = [sink_ref[h:h + 1, 0:1] for h in heads]
            s = _attn_scores(q, kb, _attn_valid(n == 0))
            dp = [_dot_nt(do[h], vb[h // ATTN_REP]) for h in heads]
            e, es, inv = _attn_softmax(s, sink)
            p = [e[h] * inv[h] for h in heads]
            delta = [jnp.sum(p[h] * dp[h], axis=1, keepdims=True) for h in heads]
            dsc = [(p[h] * (dp[h] - delta[h]) * (hd ** -0.5)).astype(BF16) for h in heads]
            pb = [p[h].astype(BF16) for h in heads]
            dq = [_dot_nn(dsc[h], kb[h // ATTN_REP]) for h in heads]
            stack = lambda per_head, kh: jnp.concatenate(per_head[kh * ATTN_REP:(kh + 1) * ATTN_REP], axis=0)
            dkb = [_dot_tn(stack(dsc, kh), stack(q, kh)) for kh in kvs]
            dvb = [_dot_tn(stack(pb, kh), stack(do, kh)) for kh in kvs]
            dsink = [jnp.broadcast_to(jnp.sum(-es[h] * inv[h] * delta[h], axis=0, keepdims=True), (1, LANES)) for h in heads]
            dq_ref[...] = jnp.concatenate([d.astype(dq_ref.dtype) for d in dq], axis=1)
            dsink_ref[...] += jnp.concatenate(dsink, axis=0)
            dbias_ref[...] += jnp.concatenate([jnp.sum(d, axis=0, keepdims=True) for d in dq + dkb + dvb], axis=1)
            dkv_ref[...] = (carry[...] + jnp.concatenate([d[0:w, :] for d in dkb + dvb], axis=1)).astype(dkv_ref.dtype)
            carry[...] = jnp.concatenate([d[w:2 * w, :] for d in dkb + dvb], axis=1)

        @pl.when(n == nb)
        def _():
            dkv_ref[...] = carry[...].astype(dkv_ref.dtype)

    qh = lambda half: pl.BlockSpec((w, ATTN_Q_HALF), lambda n: (clamp(n), half))
    kv = lambda blk, idx: pl.BlockSpec((w, ATTN_KV_W), lambda n: (idx(n), blk))
    return pl.pallas_call(
        body, grid=(nb + 1,),
        in_specs=[qh(0), qh(1), kv(ATTN_K_BLK, clamp), kv(ATTN_K_BLK, prev), kv(ATTN_V_BLK, clamp), kv(ATTN_V_BLK, prev),
                  pl.BlockSpec((ATTN_N_Q, LANES), lambda n: (0, 0)), qh(0), qh(1)],
        out_specs=[pl.BlockSpec((w, D_MODEL), lambda n: (clamp(n), 0)),
                   pl.BlockSpec((w, 2 * ATTN_KV_W), lambda n: (jnp.maximum(n - 1, 0), 0)),
                   pl.BlockSpec((ATTN_N_Q, LANES), lambda n: (0, 0)), pl.BlockSpec((1, ATTN_QKV), lambda n: (0, 0))],
        out_shape=[jax.ShapeDtypeStruct((t, D_MODEL), BF16), jax.ShapeDtypeStruct((t, 2 * ATTN_KV_W), BF16),
                   jax.ShapeDtypeStruct((ATTN_N_Q, LANES), F32), jax.ShapeDtypeStruct((1, ATTN_QKV), F32)],
        scratch_shapes=[pltpu.VMEM((w, 2 * ATTN_KV_W), F32)],
        compiler_params=_params("arbitrary"), name=name)(qkv, qkv, qkv, qkv, qkv, qkv, sinks_b, dout, dout)


def _sq_relu_epilogue(acc):
    r = jnp.maximum(acc, 0.0)
    return (r * r,)


def _sq_relu_bwd_epilogue(acc, act):
    return (acc * (2.0 * jnp.sqrt(act.astype(F32))),)


def _bias_epilogue(acc, bias):
    return (acc + bias,)


def _plain_run(stage, fn, *args, **kwargs):
    return fn(*args, **kwargs)


def _mlp_fwd(u, w_up, w_down, tag, run=_plain_run):
    act = run(f"mlp_up_{tag}", _matmul, u, w_up, mode="nn", out_dtypes=(BF16,), epilogue=_sq_relu_epilogue, b_shards=True,
              tm=BIG_TILE, name=f"mlp_up_{tag}")
    f = run(f"mlp_down_{tag}", _matmul, act, w_down, mode="nn", out_dtypes=(BF16,), tk=BIG_TILE, name=f"mlp_down_{tag}")
    return act, f


def _mlp_bwd(u, act, w_up, w_down, df, tag):
    dpre = _matmul(df, w_down, mode="nt", out_dtypes=(BF16,), epilogue=_sq_relu_bwd_epilogue,
                   extras=((act, "tile"),), name=f"mlp_dact_{tag}")
    dw_down = _matmul(act, df, mode="tn", out_dtypes=(BF16,), tk=BIG_TILE, name=f"mlp_dwdown_{tag}")
    du = _matmul(dpre, w_up, mode="nt", out_dtypes=(BF16,), b_shards=True, tm=BIG_TILE, name=f"mlp_du_{tag}")
    dw_up = _matmul(u, dpre, mode="tn", out_dtypes=(BF16,), out_shards=True, tk=BIG_TILE, name=f"mlp_dwup_{tag}")
    return du, dw_up, dw_down


def _head_param_rows(p):
    return jnp.broadcast_to(p.reshape(SSD_N_GROUPS, SSD_HPG, 1), (SSD_N_GROUPS, SSD_HPG, LANES))


def _local_step(x, target, wts, comm=None, u0=None):
    wts = dict(wts)
    row = lambda v: v.reshape(1, -1)
    mix_pre, mix_post, ffn_pre, ffn_post = wts["mix_pre_norm"], wts["mix_post_norm"], wts["ffn_pre_norm"], wts["ffn_post_norm"]

    def gathering(stage, fn, *args, **kwargs):
        hook = comm.gather_hook(stage) if comm is not None else None
        if hook is None:
            return fn(*args, **kwargs)
        out, got = fn(*args, hook=hook, **kwargs)
        wts.update(comm.weights_from(stage, got))
        return out

    if u0 is None:
        u0 = _rms_fwd(x, row(mix_pre[0]), name="rms_pre_mix0")
    zx, dt_raw = gathering("in_proj", _matmul, u0, wts["ssd_w_in"], mode="nn", out_dtypes=(BF16,), tn=SSD_IN_TILE,
                           f32_block=SSD_DT_COL - (SSD_IN_PAD - SSD_IN_TILE),
                           name="ssd_in_proj")
    xc = gathering("conv", _conv_fwd, zx, wts["ssd_conv_w"], row(wts["ssd_conv_b"]), name="ssd_conv_fwd")
    bias_row = jnp.pad(wts["ssd_dt_bias"], (0, LANES - SSD_N_HEADS)).reshape(1, LANES)
    alog_row = jnp.pad(wts["ssd_a_log"], (0, LANES - SSD_N_HEADS)).reshape(1, LANES)
    dtr, cumr = _softplus_fwd(dt_raw, bias_row, alog_row, name="ssd_dt_fwd")
    alog_b, d_b = _head_param_rows(wts["ssd_a_log"]), _head_param_rows(wts["ssd_d"])
    y_ssd, states = gathering("scan", _ssd_fwd, xc, dtr, cumr, alog_b, d_b, name="ssd_scan_fwd")
    norm_w = row(wts["ssd_norm_w"])
    yn = _gate_norm_fwd(y_ssd, zx, norm_w, name="ssd_gate_norm_fwd")
    mix0 = _matmul(yn, wts["ssd_w_out"], mode="nn", out_dtypes=(BF16,), tk=BIG_TILE, name="ssd_out_proj")
    h1, v0 = _rms_fwd(mix0, row(mix_post[0]), resid=x, want_u=row(ffn_pre[0]), name="rms_post_mix0")
    act0, f0 = _mlp_fwd(v0, wts["mlp_w_up0"], wts["mlp_w_down0"], "l0", run=gathering)
    h2, u1 = _rms_fwd(f0, row(ffn_post[0]), resid=h1, want_u=row(mix_pre[1]), name="rms_post_ffn0")

    qkv = _matmul(u1, wts["attn_w_qkv"], mode="nn", out_dtypes=(BF16,), epilogue=_bias_epilogue,
                  extras=((row(wts["attn_b_qkv"]), "row"),), b_shards=True, tm=BIG_TILE, name="attn_qkv_proj")
    sinks_b = jnp.broadcast_to(wts["attn_sinks"].reshape(ATTN_N_Q, 1), (ATTN_N_Q, LANES))
    ao = gathering("attn_fwd", _attn_fwd, qkv, sinks_b, name="attn_fwd")
    mix1 = _matmul(ao, wts["attn_w_o"], mode="nn", out_dtypes=(BF16,), epilogue=_bias_epilogue,
                   extras=((row(wts["attn_b_o"]), "row"),), name="attn_out_proj")
    h3, v1 = _rms_fwd(mix1, row(mix_post[1]), resid=h2, want_u=row(ffn_pre[1]), name="rms_post_mix1")
    act1, f1 = _mlp_fwd(v1, wts["mlp_w_up1"], wts["mlp_w_down1"], "l1")
    dh4, loss_tile = _rms_fwd(f1, row(ffn_post[1]), resid=h3, target=target, name="rms_post_ffn1_loss")

    df1, g_ffn_post1 = _rms_bwd(f1, row(ffn_post[1]), dh4, out_dtype=BF16, name="rms_post_ffn1_bwd")
    dv1, g_up1, g_down1 = _mlp_bwd(v1, act1, wts["mlp_w_up1"], wts["mlp_w_down1"], df1, "l1")
    dh3, g_ffn_pre1 = _rms_bwd(h3, row(ffn_pre[1]), dv1, resid=dh4, name="rms_pre_ffn1_bwd")
    dmix1, g_mix_post1, g_b_o = _rms_bwd(mix1, row(mix_post[1]), dh3, out_dtype=BF16, dx_col_sum=True, name="rms_post_mix1_bwd")
    g_w_o = _matmul(ao, dmix1, mode="tn", out_dtypes=(BF16,), tk=BIG_TILE, name="attn_dwo")
    dao = _matmul(dmix1, wts["attn_w_o"], mode="nt", out_dtypes=(BF16,), name="attn_dao")
    dq, dkv, g_sinks, g_b_qkv = _attn_bwd(qkv, sinks_b, dao, name="attn_bwd")
    dqkv = jnp.concatenate([dq, dkv], axis=1)
    g_w_qkv = _matmul(u1, dqkv, mode="tn", out_dtypes=(BF16,), tn=ATTN_QKV // N_CHIPS, out_shards=True, tk=BIG_TILE, name="attn_dwqkv")
    du1 = _matmul(dqkv, wts["attn_w_qkv"], mode="nt", out_dtypes=(BF16,), b_shards=True, tm=BIG_TILE, name="attn_du")
    dh2, g_mix_pre1 = _rms_bwd(h2, row(mix_pre[1]), du1, resid=dh3, name="rms_pre_mix1_bwd")

    df0, g_ffn_post0 = _rms_bwd(f0, row(ffn_post[0]), dh2, out_dtype=BF16, name="rms_post_ffn0_bwd")
    dv0, g_up0, g_down0 = _mlp_bwd(v0, act0, wts["mlp_w_up0"], wts["mlp_w_down0"], df0, "l0")
    dh1, g_ffn_pre0 = _rms_bwd(h1, row(ffn_pre[0]), dv0, resid=dh2, name="rms_pre_ffn0_bwd")
    dmix0, g_mix_post0 = _rms_bwd(mix0, row(mix_post[0]), dh1, out_dtype=BF16, name="rms_post_mix0_bwd")
    g_w_out = _matmul(yn, dmix0, mode="tn", out_dtypes=(BF16,), tk=BIG_TILE, name="ssd_dwout")
    dyn = _matmul(dmix0, wts["ssd_w_out"], mode="nt", out_dtypes=(BF16,), tm=BIG_TILE, name="ssd_dyn")
    dy_ssd, dzx, g_norm_w = _gate_norm_bwd(y_ssd, zx, norm_w, dyn, name="ssd_gate_norm_bwd")
    mats = {"ssd_w_out": g_w_out, "attn_w_qkv": g_w_qkv, "attn_w_o": g_w_o,
            "mlp_w_up0": g_up0, "mlp_w_up1": g_up1, "mlp_w_down0": g_down0, "mlp_w_down1": g_down1}
    if comm is None:
        dxc, dbm, dcm, ddt_r, dpar = _ssd_bwd(xc, dtr, cumr, alog_b, d_b, states, dy_ssd, name="ssd_scan_bwd")
    else:
        (dxc, dbm, dcm, ddt_r, dpar), received = _ssd_bwd(xc, dtr, cumr, alog_b, d_b, states, dy_ssd,
                                                          name="ssd_scan_bwd", hook=comm.exchange_hook(mats, "early"))
        comm.received(received)
    dzx, g_conv_w, g_conv_b = _conv_bwd(zx, wts["ssd_conv_w"], row(wts["ssd_conv_b"]), dxc, dbm, dcm, dzx, name="ssd_conv_bwd")
    dzx, g_dt_bias = _softplus_bwd(dt_raw, bias_row, ddt_r, dzx, name="ssd_dt_bwd")
    g_w_in = _w_in_to_shards(_matmul(u0, dzx, mode="tn", out_dtypes=(BF16,), tn=SSD_IN_TILE, tk=BIG_TILE, name="ssd_dwin"), name="ssd_dwin_shards")
    mats["ssd_w_in"] = g_w_in
    if comm is None:
        du0 = _matmul(dzx, wts["ssd_w_in"], mode="nt", out_dtypes=(BF16,), tk=SSD_IN_TILE, name="ssd_du")
    else:
        du0, received = _matmul(dzx, wts["ssd_w_in"], mode="nt", out_dtypes=(BF16,), tk=SSD_IN_TILE, name="ssd_du",
                                hook=comm.exchange_hook(mats, "late"))
        comm.received(received)
    grad_x, g_mix_pre0 = _rms_bwd(x, row(mix_pre[0]), du0, resid=dh1, name="rms_pre_mix0_bwd")

    dpar = dpar.reshape(SSD_N_HEADS, LANES)
    vecs = {
        "ssd_conv_w": g_conv_w, "ssd_conv_b": g_conv_b.reshape(-1),
        "ssd_dt_bias": g_dt_bias[0, :SSD_N_HEADS], "ssd_a_log": dpar[:, 0], "ssd_d": dpar[:, 1],
        "ssd_norm_w": g_norm_w.reshape(-1), "attn_b_qkv": g_b_qkv.reshape(-1), "attn_sinks": g_sinks[:, 0],
        "attn_b_o": g_b_o.reshape(-1),
        "mix_pre_norm": jnp.concatenate([g_mix_pre0, g_mix_pre1]), "mix_post_norm": jnp.concatenate([g_mix_post0, g_mix_post1]),
        "ffn_pre_norm": jnp.concatenate([g_ffn_pre0, g_ffn_pre1]), "ffn_post_norm": jnp.concatenate([g_ffn_post0, g_ffn_post1]),
    }
    return loss_tile, grad_x, mats, vecs


def _mesh_position():
    return lax.axis_index("x"), lax.axis_index("y"), lax.axis_index("c")


def _flip(v, bit):
    return 1 - v if bit else v


OTHER_CHIPS = ((1, 0), (0, 1), (1, 1))


def _comm_params():
    return pltpu.CompilerParams(vmem_limit_bytes=VMEM_LIMIT)


def _staged_copies(srcs, dsts, bufs, sems_in, sems_out):
    loads = [pltpu.make_async_copy(s, b, sems_in.at[i]) for i, (s, b) in enumerate(zip(srcs, bufs))]
    stores = [pltpu.make_async_copy(b, d, sems_out.at[i]) for i, (b, d) in enumerate(zip(bufs, dsts))]
    return loads, stores


class _GatherHook:
    def __init__(self, mats, vecs=()):
        self.arrs = list(mats) + list(vecs)
        self.nm, self.n = len(mats), len(self.arrs)
        n_ici, n_fwd = (N_CHIPS - 1) * self.n, max((N_CHIPS - 1) * self.nm, 1)
        dma = pltpu.SemaphoreType.DMA
        self.out_shape = [jax.ShapeDtypeStruct((N_CHIPS,) + a.shape, a.dtype) for a in self.arrs]
        self.scratch = [pltpu.VMEM(a.shape, a.dtype) for a in self.arrs] + [
            dma((n_ici,)), dma((n_ici,)), dma((n_fwd,)), dma((n_fwd,)), dma((self.n,)), dma((self.n,))]

    def plan(self, ins, outs, scratch):
        n, nm = self.n, self.nm
        bufs = scratch[:n]
        ici_send, ici_recv, fwd_send, fwd_recv, load_sems, store_sems = scratch[n:]
        xi, yi, ci = _mesh_position()
        me = 2 * xi + yi
        loads, stores = _staged_copies(ins, [outs[i].at[me] for i in range(n)], bufs, load_sems, store_sems)
        sends, landed, forwards, from_sibling = [], [], [], []
        for j, (bx, by) in enumerate(OTHER_CHIPS):
            px, py = _flip(xi, bx), _flip(yi, by)
            peer = 2 * px + py
            for i in range(n):
                k = j * n + i
                mk = functools.partial(pltpu.make_async_remote_copy, send_sem=ici_send.at[k], recv_sem=ici_recv.at[k],
                                       device_id=(px, py, ci), device_id_type=MESH)
                if i < nm:
                    sends.append(mk(src_ref=ins[i].at[ci], dst_ref=outs[i].at[me, ci]))
                    landed.append(mk(src_ref=ins[i].at[ci], dst_ref=outs[i].at[peer, ci]))
                    kf = j * nm + i
                    fw = functools.partial(pltpu.make_async_remote_copy, send_sem=fwd_send.at[kf], recv_sem=fwd_recv.at[kf],
                                           device_id=(xi, yi, 1 - ci), device_id_type=MESH)
                    forwards.append(fw(src_ref=outs[i].at[peer, ci], dst_ref=outs[i].at[peer, ci]))
                    from_sibling.append(fw(src_ref=outs[i].at[peer, ci], dst_ref=outs[i].at[peer, 1 - ci]))
                else:
                    sends.append(mk(src_ref=ins[i], dst_ref=outs[i].at[me]))
                    landed.append(mk(src_ref=ins[i], dst_ref=outs[i].at[peer]))
                    forwards.append(None)
        return loads, stores, sends, landed, forwards, from_sibling

    @staticmethod
    def start(p):
        loads, _, sends, _, _, _ = p
        for cp in loads + sends:
            cp.start()

    @staticmethod
    def relay(p):
        loads, stores, _, landed, forwards, _ = p
        for ld, st in zip(loads, stores):
            ld.wait()
            st.start()
        for cp, fw in zip(landed, forwards):
            cp.wait_recv()
            if fw is not None:
                fw.start()

    @staticmethod
    def finish(p):
        _, stores, sends, _, forwards, from_sibling = p
        for cp in from_sibling:
            cp.wait_recv()
        for cp in sends + [fw for fw in forwards if fw is not None]:
            cp.wait_send()
        for st in stores:
            st.wait()


def _run_hook(hook, ins, outs, scratch, step, n_steps):
    p = hook.plan(ins, outs, scratch)
    relay_step = min(max(1, (3 * n_steps) // 4), n_steps - 1)

    @pl.when(step == 0)
    def _():
        hook.start(p)

    if relay_step < n_steps - 1:
        @pl.when(step == relay_step)
        def _():
            hook.relay(p)

    @pl.when(step == n_steps - 1)
    def _():
        if relay_step == n_steps - 1:
            hook.relay(p)
        hook.finish(p)


def _hook_call(hook, *, name):
    n = len(hook.arrs)

    def body(*refs):
        p = hook.plan(refs[:n], refs[n:n + len(hook.out_shape)], refs[n + len(hook.out_shape):])
        hook.start(p)
        hook.relay(p)
        hook.finish(p)

    return pl.pallas_call(
        body, in_specs=[ANY] * n, out_specs=[ANY] * len(hook.out_shape), out_shape=hook.out_shape,
        scratch_shapes=hook.scratch, compiler_params=_comm_params(), name=name)(*hook.arrs)


def _send_other_half(parts, *, name):
    n = len(parts)

    def body(*refs):
        ins, outs = refs[:n], refs[n:2 * n]
        send_sems, recv_sems = refs[2 * n:]
        xi, yi, ci = _mesh_position()
        sibling = (xi, yi, 1 - ci)
        for i in range(n):
            for s in range(N_CHIPS):
                pltpu.make_async_remote_copy(src_ref=ins[i].at[s, 1 - ci], dst_ref=outs[i].at[s], send_sem=send_sems.at[i],
                                             recv_sem=recv_sems.at[i], device_id=sibling, device_id_type=MESH).start()
        for i in range(n):
            pltpu.make_async_remote_copy(src_ref=outs[i], dst_ref=outs[i], send_sem=send_sems.at[i], recv_sem=recv_sems.at[i],
                                         device_id=sibling, device_id_type=MESH).wait()

    return pl.pallas_call(
        body, in_specs=[ANY] * n, out_specs=[ANY] * n,
        out_shape=[jax.ShapeDtypeStruct((p.shape[0],) + p.shape[2:], p.dtype) for p in parts],
        scratch_shapes=[pltpu.SemaphoreType.DMA((n,)), pltpu.SemaphoreType.DMA((n,))],
        name=name)(*parts)


ROW_BLOCKS = 8
SUM_ROW_BLOCKS = 2


def _add_sibling_half(parts, theirs, core, *, name):
    n = len(parts)

    def body(core_ref, *refs):
        for a_ref, b_ref, o_ref in zip(refs[:n], refs[n:2 * n], refs[2 * n:]):
            o_ref[...] = (a_ref[...].astype(F32) + b_ref[...].astype(F32)).astype(o_ref.dtype)

    nb = SUM_ROW_BLOCKS
    mine = lambda p: pl.BlockSpec((None, None, p.shape[2] // nb, p.shape[3]), lambda s, rb, core_ref: (s, core_ref[0], rb, 0))
    other = lambda p: pl.BlockSpec((None, p.shape[1] // nb, p.shape[2]), lambda s, rb, core_ref: (s, rb, 0))
    return pl.pallas_call(
        body,
        grid_spec=pltpu.PrefetchScalarGridSpec(
            num_scalar_prefetch=1, grid=(N_CHIPS, nb),
            in_specs=[mine(p) for p in parts] + [other(q) for q in theirs], out_specs=[other(q) for q in theirs]),
        out_shape=[jax.ShapeDtypeStruct(q.shape, BF16) for q in theirs],
        compiler_params=_params("parallel", "parallel"), name=name)(core, *parts, *theirs)


class _ExchangeHook:
    def __init__(self, parts, to_all=()):
        self.arrs = list(parts) + list(to_all)
        self.n_parts, self.n = len(parts), len(self.arrs)
        n_ici, n_peer = max((N_CHIPS - 1) * self.n_parts, 1), (N_DEV - 1) * max(len(to_all), 1)
        dma = pltpu.SemaphoreType.DMA
        self.out_shape = [jax.ShapeDtypeStruct(p.shape, p.dtype) for p in parts] + [
            jax.ShapeDtypeStruct((N_DEV,) + a.shape, a.dtype) for a in to_all]
        self.scratch = [pltpu.VMEM(p.shape[1:], p.dtype) for p in parts] + [pltpu.VMEM(a.shape, a.dtype) for a in to_all] + [
            dma((n_ici,)), dma((n_ici,)), dma((n_peer,)), dma((n_peer,)), dma((self.n,)), dma((self.n,))]

    def plan(self, ins, outs, scratch):
        n, npt = self.n, self.n_parts
        bufs = scratch[:n]
        send_sems, recv_sems, all_send, all_recv, load_sems, store_sems = scratch[n:]
        xi, yi, ci = _mesh_position()
        me_chip = 2 * xi + yi
        me = 4 * xi + 2 * yi + ci
        loads, stores = _staged_copies([ins[i].at[me_chip] for i in range(npt)] + list(ins[npt:]),
                                       [outs[i].at[me_chip] for i in range(npt)] + [outs[i].at[me] for i in range(npt, n)],
                                       bufs, load_sems, store_sems)
        sends, recvs = [], []
        for j, (bx, by) in enumerate(OTHER_CHIPS):
            px, py = _flip(xi, bx), _flip(yi, by)
            peer = 2 * px + py
            for i in range(npt):
                k = j * npt + i
                mk = functools.partial(pltpu.make_async_remote_copy, src_ref=ins[i].at[peer], send_sem=send_sems.at[k],
                                       recv_sem=recv_sems.at[k], device_id=(px, py, ci), device_id_type=MESH)
                sends.append(mk(dst_ref=outs[i].at[me_chip]))
                recvs.append(mk(dst_ref=outs[i].at[peer]))
        for i in range(npt, n):
            for k in range(1, N_DEV):
                px, py, pc = _flip(xi, (k >> 2) & 1), _flip(yi, (k >> 1) & 1), _flip(ci, k & 1)
                slot = (i - npt) * (N_DEV - 1) + k - 1
                mk = functools.partial(pltpu.make_async_remote_copy, src_ref=ins[i], send_sem=all_send.at[slot],
                                       recv_sem=all_recv.at[slot], device_id=(px, py, pc), device_id_type=MESH)
                sends.append(mk(dst_ref=outs[i].at[me]))
                recvs.append(mk(dst_ref=outs[i].at[4 * px + 2 * py + pc]))
        return loads, stores, sends, recvs

    @staticmethod
    def start(p):
        loads, _, sends, _ = p
        for cp in loads + sends:
            cp.start()

    @staticmethod
    def relay(p):
        loads, stores, _, _ = p
        for ld, st in zip(loads, stores):
            ld.wait()
            st.start()

    @staticmethod
    def finish(p):
        _, stores, sends, recvs = p
        for cp in recvs:
            cp.wait_recv()
        for cp in sends:
            cp.wait_send()
        for st in stores:
            st.wait()


def _sum_chips(parts, *, name):
    n = len(parts)
    p = parts[0].shape[0]

    def body(*refs):
        s = pl.program_id(1)
        for x_ref, o_ref in zip(refs[:n], refs[n:]):
            @pl.when(s == 0)
            def _():
                o_ref[...] = x_ref[...].astype(F32)

            @pl.when(s > 0)
            def _():
                o_ref[...] += x_ref[...].astype(F32)

    blocks = lambda q: SUM_ROW_BLOCKS if q.shape[1] % (16 * SUM_ROW_BLOCKS) == 0 else 1
    assert len({blocks(q) for q in parts}) == 1
    nb = blocks(parts[0])
    return pl.pallas_call(
        body, grid=(nb, p),
        in_specs=[pl.BlockSpec((None, q.shape[1] // nb, q.shape[2]), lambda rb, s: (s, rb, 0)) for q in parts],
        out_specs=[pl.BlockSpec((q.shape[1] // nb, q.shape[2]), lambda rb, s: (rb, 0)) for q in parts],
        out_shape=[jax.ShapeDtypeStruct(q.shape[1:], F32) for q in parts],
        compiler_params=_params("parallel", "arbitrary"), name=name)(*parts)


def _swap_halves(halves, layers, *, name, hook=None):
    n = len(halves)
    out_shapes, slots = [], []
    for i, h in enumerate(halves):
        pair = [p for p in layers if i in p]
        if pair and pair[0][1] == i:
            slots.append((slots[pair[0][0]][0], 1))
        elif pair:
            out_shapes.append(jax.ShapeDtypeStruct((2, 2) + h.shape, h.dtype))
            slots.append((len(out_shapes) - 1, 0))
        else:
            out_shapes.append(jax.ShapeDtypeStruct((2,) + h.shape, h.dtype))
            slots.append((len(out_shapes) - 1, None))
    n_out = len(out_shapes)
    hk = _HookSlots(hook, n_in=n, n_out=n_out, n_scratch=n + 4)

    def body(*refs):
        ins, outs, scratch = hk.own(refs)
        bufs = scratch[:n]
        send_sems, recv_sems, load_sems, store_sems = scratch[n:]
        if hook is not None:
            extra = hk.plan(refs)
            hook.start(extra)
        xi, yi, ci = _mesh_position()
        own, sends, recvs = [], [], []
        for i in range(n):
            o, layer = slots[i]
            dst = (lambda core: outs[o].at[core]) if layer is None else (lambda core: outs[o].at[layer, core])
            own.append(dst(ci))
            mk = functools.partial(pltpu.make_async_remote_copy, src_ref=ins[i], send_sem=send_sems.at[i],
                                   recv_sem=recv_sems.at[i], device_id=(xi, yi, 1 - ci), device_id_type=MESH)
            sends.append(mk(dst_ref=dst(ci)))
            recvs.append(mk(dst_ref=dst(1 - ci)))
        loads, stores = _staged_copies(ins, own, bufs, load_sems, store_sems)
        for cp in loads + sends:
            cp.start()
        for ld, st in zip(loads, stores):
            ld.wait()
            st.start()
        for cp in recvs:
            cp.wait_recv()
        for cp in sends:
            cp.wait_send()
        for st in stores:
            st.wait()
        if hook is not None:
            hook.relay(extra)
            hook.finish(extra)

    outs = pl.pallas_call(
        body, in_specs=[ANY] * n + hk.in_specs, out_specs=[ANY] * n_out + hk.out_specs, out_shape=out_shapes + hk.out_shape,
        scratch_shapes=[pltpu.VMEM(h.shape, h.dtype) for h in halves]
        + [pltpu.SemaphoreType.DMA((n,)), pltpu.SemaphoreType.DMA((n,)), pltpu.SemaphoreType.DMA((n,)), pltpu.SemaphoreType.DMA((n,))]
        + hk.scratch,
        compiler_params=_comm_params(), name=name)(*halves, *hk.inputs)
    return outs if hook is None else (outs[:n_out], outs[n_out:])


def _cast_bf16(layers, x, norm_w, *, name, hook=None):
    n = len(layers)
    hk = _HookSlots(hook, n_in=n + 2, n_out=n + 1, n_scratch=0)

    def body(*refs):
        ins, outs, _ = hk.own(refs)
        if hook is not None:
            hk.run(refs, pl.program_id(0), ROW_BLOCKS)
        for i_ref, o_ref in zip(ins[:n], outs[:n]):
            o_ref[...] = i_ref[...].astype(o_ref.dtype)
        xv = ins[n][...]
        outs[n][...] = (xv * lax.rsqrt(jnp.mean(xv * xv, axis=-1, keepdims=True) + NORM_EPS) * ins[n + 1][...]).astype(BF16)

    in_blk = lambda a, l: pl.BlockSpec((None, a.shape[1] // ROW_BLOCKS, a.shape[2]), lambda i: (l, i, 0))
    out_blk = lambda a: pl.BlockSpec((a.shape[1] // ROW_BLOCKS, a.shape[2]), lambda i: (i, 0))
    x_blk = pl.BlockSpec((x.shape[0] // ROW_BLOCKS, x.shape[1]), lambda i: (i, 0))
    outs = pl.pallas_call(
        body, grid=(ROW_BLOCKS,),
        in_specs=[in_blk(a, l) for a, l in layers] + [x_blk, pl.BlockSpec((1, x.shape[1]), lambda i: (0, 0))] + hk.in_specs,
        out_specs=[out_blk(a) for a, _ in layers] + [x_blk] + hk.out_specs,
        out_shape=[jax.ShapeDtypeStruct(a.shape[1:], BF16) for a, _ in layers] + [jax.ShapeDtypeStruct(x.shape, BF16)] + hk.out_shape,
        scratch_shapes=hk.scratch,
        compiler_params=_params(*hk.semantics("parallel")), name=name)(*[a for a, _ in layers], x, norm_w, *hk.inputs)
    own = (outs[:n], outs[n])
    return own if hook is None else (own, outs[n + 1:])


def _full_weight(name, gathered):
    s, _, r, c = gathered.shape
    if name == "ssd_w_in":
        return _w_in_from_shards(gathered.reshape(s, 2 * r, c), name="ssd_w_in_unshard")
    if name in ("attn_w_qkv", "mlp_w_up0", "mlp_w_up1"):
        return gathered.reshape(s, 2 * r, c)
    return gathered.reshape(s * 2 * r, c)


class _StepComm:
    GATHER = {"in_proj": ("mlp_w_up0",), "conv": ("mlp_w_down0",), "scan": ("ssd_w_out", "mlp_w_up1"),
              "mlp_up_l0": ("attn_w_qkv", "attn_w_o"), "attn_fwd": ("mlp_w_down1",)}
    EXCHANGE = {"early": ("ssd_w_out", "attn_w_qkv", "attn_w_o", "mlp_w_up0", "mlp_w_up1", "mlp_w_down0", "mlp_w_down1"),
                "late": ("ssd_w_in",)}

    def __init__(self, shards, core):
        self.shards, self.core = shards, core
        self.chip_parts = {}
        self._pending = None

    def gather_hook(self, stage):
        names = self.GATHER.get(stage)
        return _GatherHook([self.shards[n] for n in names]) if names else None

    def weights_from(self, stage, gathered):
        return {n: _full_weight(n, g) for n, g in zip(self.GATHER[stage], gathered)}

    def chip_sums(self, mats, tag):
        parts = [_shard_halves(a) for a in mats.values()]
        theirs = _send_other_half(parts, name=f"grad_sibling_send_{tag}")
        return _add_sibling_half(parts, theirs, self.core, name=f"grad_chip_sum_{tag}")

    def exchange_hook(self, mats, which):
        self._pending = self.EXCHANGE[which]
        return _ExchangeHook(self.chip_sums({n: mats[n] for n in self._pending}, which))

    def received(self, arrays):
        self.chip_parts.update(zip(self._pending, arrays))


ADAMW_ROW_BLOCKS = 16


def _adamw(ws, gs, ms, vs, *, name, by_lanes=False):
    n = len(ws)
    if by_lanes:
        nb = min(a.shape[2] for a in ws) // LANES
    else:
        nb = ADAMW_ROW_BLOCKS if all(a.shape[1] % (8 * ADAMW_ROW_BLOCKS) == 0 for a in ws) else 1

    def body(*refs):
        ins, outs = refs[:4 * n], refs[4 * n:]
        for i in range(n):
            w_ref, g_ref, m_ref, v_ref = ins[i], ins[n + i], ins[2 * n + i], ins[3 * n + i]
            go_ref, d_ref, nm_ref, nv_ref = outs[i], outs[n + i], outs[2 * n + i], outs[3 * n + i]
            gv = g_ref[...]
            nm = ADAM_B1 * m_ref[...] + (1.0 - ADAM_B1) * gv
            nv = ADAM_B2 * v_ref[...] + (1.0 - ADAM_B2) * (gv * gv)
            m_hat = nm / (1.0 - ADAM_B1 ** ADAM_STEP)
            v_hat = nv / (1.0 - ADAM_B2 ** ADAM_STEP)
            go_ref[...] = gv
            d_ref[...] = -ADAM_LR * (m_hat / (jnp.sqrt(v_hat) + ADAM_EPS) + ADAM_WD * w_ref[...])
            nm_ref[...] = nm
            nv_ref[...] = nv

    if by_lanes:
        blks = [pl.BlockSpec((a.shape[0], a.shape[1], a.shape[2] // nb), lambda i: (0, 0, i)) for a in ws]
    else:
        blks = [pl.BlockSpec((a.shape[0], a.shape[1] // nb, a.shape[2]), lambda i: (0, i, 0)) for a in ws]
    shapes = [jax.ShapeDtypeStruct(a.shape, F32) for a in ws]
    outs = pl.pallas_call(body, grid=(nb,), in_specs=blks * 4, out_specs=blks * 4, out_shape=shapes * 4,
                          compiler_params=_params("parallel"), name=name)(*ws, *gs, *ms, *vs)
    return [tuple(outs[k * n + i] for k in range(4)) for i in range(n)]


SM_CONV_B, SM_NORM_W, SM_MIX_PRE, SM_MIX_POST, SM_FFN_PRE, SM_FFN_POST, SM_MISC, SM_CONV_W, SM_B_QKV, SM_B_O = 0, 4, 6, 8, 10, 12, 14, 16, 32, 34
SM_ROWS = 40
MISC_DT_BIAS, MISC_A_LOG, MISC_D, MISC_SINKS, MISC_LOSS = 0, 32, 64, 96, 112


def _shard_halves(a):
    c = a.shape[-1]
    return a.reshape(N_CHIPS, 2, -1, c)


def _rows(v):
    return v.reshape(-1, D_MODEL)


def _misc_row(dt_bias, a_log, d, sinks, loss):
    pad = jnp.zeros((D_MODEL - MISC_LOSS - 1,), F32)
    return jnp.concatenate([dt_bias.reshape(-1), a_log.reshape(-1), d.reshape(-1), sinks.reshape(-1), loss.reshape(1), pad]).reshape(1, D_MODEL)


def _replicated_rows(p, loss):
    return jnp.concatenate([
        _rows(p["ssd_conv_b"]), _rows(p["ssd_norm_w"]), _rows(p["mix_pre_norm"]), _rows(p["mix_post_norm"]),
        _rows(p["ffn_pre_norm"]), _rows(p["ffn_post_norm"]),
        _misc_row(p["ssd_dt_bias"], p["ssd_a_log"], p["ssd_d"], p["attn_sinks"], loss), jnp.zeros((1, D_MODEL), F32)], axis=0)


def _sharded_rows(conv_w, b_qkv, b_o):
    last = jnp.concatenate([b_qkv.reshape(-1), b_o.reshape(-1), jnp.zeros((D_MODEL - 640,), F32)]).reshape(1, D_MODEL)
    return jnp.concatenate([conv_w.reshape(SSD_CONV_WIDTH, D_MODEL), last, jnp.zeros((3, D_MODEL), F32)], axis=0)


REPLICATED = ("ssd_conv_b", "ssd_dt_bias", "ssd_a_log", "ssd_d", "ssd_norm_w", "attn_sinks",
              "mix_pre_norm", "mix_post_norm", "ffn_pre_norm", "ffn_post_norm")
MATRICES = ("ssd_w_in", "ssd_w_out", "attn_w_qkv", "attn_w_o", "mlp_w_up", "mlp_w_down")
WEIGHT_NAMES = ("ssd_w_in", "ssd_conv_w", "ssd_conv_b", "ssd_dt_bias", "ssd_a_log", "ssd_d", "ssd_norm_w", "ssd_w_out",
                "attn_w_qkv", "attn_b_qkv", "attn_sinks", "attn_w_o", "attn_b_o", "mlp_w_up", "mlp_w_down",
                "mix_pre_norm", "mix_post_norm", "ffn_pre_norm", "ffn_post_norm")


def _unpack_small(rows16, rows8, like):
    misc = rows16[SM_MISC]
    out = {
        "ssd_conv_b": rows16[SM_CONV_B:SM_CONV_B + 4], "ssd_norm_w": rows16[SM_NORM_W:SM_NORM_W + 2],
        "mix_pre_norm": rows16[SM_MIX_PRE:SM_MIX_PRE + 2], "mix_post_norm": rows16[SM_MIX_POST:SM_MIX_POST + 2],
        "ffn_pre_norm": rows16[SM_FFN_PRE:SM_FFN_PRE + 2], "ffn_post_norm": rows16[SM_FFN_POST:SM_FFN_POST + 2],
        "ssd_dt_bias": misc[MISC_DT_BIAS:MISC_DT_BIAS + 32], "ssd_a_log": misc[MISC_A_LOG:MISC_A_LOG + 32],
        "ssd_d": misc[MISC_D:MISC_D + 32], "attn_sinks": misc[MISC_SINKS:MISC_SINKS + 16],
        "ssd_conv_w": rows8[0:SSD_CONV_WIDTH], "attn_b_qkv": rows8[SSD_CONV_WIDTH, 0:384], "attn_b_o": rows8[SSD_CONV_WIDTH, 384:640],
    }
    return {k: v.reshape(like[k].shape) for k, v in out.items()}


def kernel(x, ssd_w_in, ssd_conv_w, ssd_conv_b, ssd_dt_bias, ssd_a_log, ssd_d, ssd_norm_w, ssd_w_out, attn_w_qkv, attn_b_qkv, attn_sinks, attn_w_o, attn_b_o, mlp_w_up, mlp_w_down, mix_pre_norm, mix_post_norm, ffn_pre_norm, ffn_post_norm, loss_target, m_ssd_w_in, m_ssd_conv_w, m_ssd_conv_b, m_ssd_dt_bias, m_ssd_a_log, m_ssd_d, m_ssd_norm_w, m_ssd_w_out, m_attn_w_qkv, m_attn_b_qkv, m_attn_sinks, m_attn_w_o, m_attn_b_o, m_mlp_w_up, m_mlp_w_down, m_mix_pre_norm, m_mix_post_norm, m_ffn_pre_norm, m_ffn_post_norm, v_ssd_w_in, v_ssd_conv_w, v_ssd_conv_b, v_ssd_dt_bias, v_ssd_a_log, v_ssd_d, v_ssd_norm_w, v_ssd_w_out, v_attn_w_qkv, v_attn_b_qkv, v_attn_sinks, v_attn_w_o, v_attn_b_o, v_mlp_w_up, v_mlp_w_down, v_mix_pre_norm, v_mix_post_norm, v_ffn_pre_norm, v_ffn_post_norm):
    w = dict(zip(WEIGHT_NAMES, (ssd_w_in, ssd_conv_w, ssd_conv_b, ssd_dt_bias, ssd_a_log, ssd_d, ssd_norm_w, ssd_w_out, attn_w_qkv, attn_b_qkv, attn_sinks, attn_w_o, attn_b_o, mlp_w_up, mlp_w_down, mix_pre_norm, mix_post_norm, ffn_pre_norm, ffn_post_norm)))
    m = dict(zip(WEIGHT_NAMES, (m_ssd_w_in, m_ssd_conv_w, m_ssd_conv_b, m_ssd_dt_bias, m_ssd_a_log, m_ssd_d, m_ssd_norm_w, m_ssd_w_out, m_attn_w_qkv, m_attn_b_qkv, m_attn_sinks, m_attn_w_o, m_attn_b_o, m_mlp_w_up, m_mlp_w_down, m_mix_pre_norm, m_mix_post_norm, m_ffn_pre_norm, m_ffn_post_norm)))
    v = dict(zip(WEIGHT_NAMES, (v_ssd_w_in, v_ssd_conv_w, v_ssd_conv_b, v_ssd_dt_bias, v_ssd_a_log, v_ssd_d, v_ssd_norm_w, v_ssd_w_out, v_attn_w_qkv, v_attn_b_qkv, v_attn_sinks, v_attn_w_o, v_attn_b_o, v_mlp_w_up, v_mlp_w_down, v_mix_pre_norm, v_mix_post_norm, v_ffn_pre_norm, v_ffn_post_norm)))
    chip = 2 * lax.axis_index("x") + lax.axis_index("y")

    two_halves = lambda a: a.reshape(2, a.shape[-2] // 2, a.shape[-1])
    later = {"ssd_w_out": (w["ssd_w_out"], 0), "attn_w_qkv": (w["attn_w_qkv"], 0), "attn_w_o": (w["attn_w_o"], 0),
             "mlp_w_up0": (w["mlp_w_up"], 0), "mlp_w_up1": (w["mlp_w_up"], 1),
             "mlp_w_down0": (w["mlp_w_down"], 0), "mlp_w_down1": (w["mlp_w_down"], 1)}
    first = _GatherHook([two_halves(w["ssd_w_in"].astype(BF16))], [w["ssd_conv_w"][0], w["attn_b_qkv"], w["attn_b_o"]])
    (cast, u0), (g_in, g_conv, g_bqkv, g_bo) = _cast_bf16(list(later.values()), x[0], w["mix_pre_norm"][0:1],
                                                          name="weights_to_bf16", hook=first)
    core = lax.axis_index("c").astype(jnp.int32).reshape(1)
    comm = _StepComm({k: two_halves(a) for k, a in zip(later, cast)}, core)
    full = {
        "ssd_w_in": _full_weight("ssd_w_in", g_in),
        "ssd_conv_w": g_conv.transpose(1, 0, 2).reshape(SSD_CONV_WIDTH, SSD_CONV_DIM),
        "attn_b_qkv": g_bqkv.reshape(ATTN_QKV), "attn_b_o": g_bo.reshape(D_MODEL),
    }
    for name in REPLICATED:
        full[name] = w[name][0] if name.startswith(("ssd_", "attn_")) else w[name]

    loss_tile, grad_x, gm, g = _local_step(x[0], loss_target[0], full, comm, u0)

    conv_w_rows = g["ssd_conv_w"].reshape(SSD_CONV_WIDTH * N_CHIPS, D_MODEL)
    b_qkv_rows = jnp.pad(g["attn_b_qkv"], (0, 2 * D_MODEL - ATTN_QKV)).reshape(2, D_MODEL)
    small = jnp.concatenate([_replicated_rows(g, loss_tile[0, 0]), conv_w_rows, b_qkv_rows, _rows(g["attn_b_o"]),
                             jnp.zeros((SM_ROWS - SM_B_O - 1, D_MODEL), F32)], axis=0)
    order = ("ssd_w_in", "ssd_w_out", "attn_w_qkv", "attn_w_o", "mlp_w_up0", "mlp_w_up1", "mlp_w_down0", "mlp_w_down1")
    halves = _sum_chips([comm.chip_parts[k] for k in order], name="grad_sum")
    (r_in, r_out, r_qkv, r_o, r_up, r_down), (small_all,) = _swap_halves(
        halves, layers=((4, 5), (6, 7)), hook=_ExchangeHook([], [small]), name="grad_halves_swap")
    small_sum, = _sum_chips([small_all], name="small_grad_sum")

    grads = {"ssd_w_in": r_in, "ssd_w_out": r_out, "attn_w_qkv": r_qkv, "attn_w_o": r_o, "mlp_w_up": r_up, "mlp_w_down": r_down}
    grads = {k: a.reshape(w[k].shape) for k, a in grads.items()}
    conv_w_g = lax.dynamic_index_in_dim(small_sum[SM_CONV_W:SM_CONV_W + 16].reshape(SSD_CONV_WIDTH, N_CHIPS, D_MODEL), chip, axis=1, keepdims=False)
    b_qkv_g = lax.dynamic_slice_in_dim(small_sum[SM_B_QKV:SM_B_QKV + 2].reshape(-1), chip * 384, 384)
    b_o_g = lax.dynamic_slice_in_dim(small_sum[SM_B_O], chip * 256, 256)
    small_g = jnp.concatenate([small_sum[0:16], _sharded_rows(conv_w_g, b_qkv_g, b_o_g)], axis=0)
    grads.update(_unpack_small(small_g[0:16], small_g[16:24], w))
    loss = small_sum[SM_MISC, MISC_LOSS]

    delta, new_m, new_v = {}, {}, {}
    stored = lambda a: jnp.swapaxes(a, 1, 2)
    rest = [name for name in MATRICES if name != "ssd_w_in"]
    mats = lambda p: [p[name] for name in rest]
    results = dict(zip(rest, _adamw(mats(w), mats(grads), mats(m), mats(v), name="adamw_matrices")))
    (w_in_result,) = _adamw([stored(w["ssd_w_in"])], [stored(grads["ssd_w_in"])], [stored(m["ssd_w_in"])],
                            [stored(v["ssd_w_in"])], by_lanes=True, name="adamw_ssd_w_in")
    results["ssd_w_in"] = tuple(stored(a) for a in w_in_result)
    for name in MATRICES:
        grads[name], delta[name], new_m[name], new_v[name] = results[name]
    zero = jnp.zeros((), F32)
    small_pack = lambda p: jnp.concatenate([_replicated_rows({k: p[k] for k in REPLICATED}, zero),
                                            _sharded_rows(p["ssd_conv_w"], p["attn_b_qkv"], p["attn_b_o"])], axis=0)[None]
    (_, d_s, m_s, v_s), = _adamw([small_pack(w)], [small_g[None]], [small_pack(m)], [small_pack(v)], name="adamw_vectors")
    d_s, m_s, v_s = d_s[0], m_s[0], v_s[0]
    delta.update(_unpack_small(d_s[0:16], d_s[16:24], w))
    new_m.update(_unpack_small(m_s[0:16], m_s[16:24], w))
    new_v.update(_unpack_small(v_s[0:16], v_s[16:24], w))

    return (loss, grad_x[None], *[grads[n] for n in WEIGHT_NAMES], *[delta[n] for n in WEIGHT_NAMES],
            *[new_m[n] for n in WEIGHT_NAMES], *[new_v[n] for n in WEIGHT_NAMES])
```
